```python
import math
import jax, jax.numpy as jnp
from jax import lax
import numpy as np

D_MODEL = 1024
BATCH = 8
SEQ = 8192
DEPTH = 1

ATT_HEADS = 16
ATT_HEAD_DIM = 64
ATT_WIDTH = ATT_HEADS * ATT_HEAD_DIM
Q_BLOCK = 128
SSM_EXPAND = 2
SSM_INNER = SSM_EXPAND * D_MODEL
SSM_HEAD_DIM = 64
SSM_HEADS = SSM_INNER // SSM_HEAD_DIM
SSM_GROUPS = 4
SSM_HEADS_PER_GROUP = SSM_HEADS // SSM_GROUPS
SSM_STATE = 128
SSM_CONV = 4
SSM_CHUNK = 128
SSM_CONV_DIM = SSM_INNER + 2 * SSM_GROUPS * SSM_STATE
N_BRANCHES = 2
FFN_HIDDEN = -(-8 * D_MODEL // (3 * 256)) * 256
DEEPNORM_ALPHA = (2 * DEPTH) ** 0.25
DEEPNORM_BETA = (8 * DEPTH) ** -0.25
LN_EPS = 1e-5
RMS_EPS = 1e-5
IN_SIZES = (ATT_WIDTH, ATT_WIDTH, ATT_WIDTH, ATT_HEADS, SSM_INNER, SSM_CONV_DIM, SSM_HEADS, N_BRANCHES * D_MODEL)
IN_WIDTH = sum(IN_SIZES)

kernel_name = "fox_ssd_gated_hybrid_deepnorm"


def layer_norm(x, g, b):
    xf = x.astype(jnp.float32)
    mu = jnp.mean(xf, axis=-1, keepdims=True)
    var = jnp.mean(jnp.square(xf - mu), axis=-1, keepdims=True)
    return ((xf - mu) * lax.rsqrt(var + LN_EPS) * g.astype(jnp.float32) + b.astype(jnp.float32)).astype(x.dtype)


def forgetting_attention(q, k, v, log_f):
    bsz, seq, heads, dh = q.shape
    n_blk = seq // Q_BLOCK
    scale = 1.0 / math.sqrt(dh)
    cum = jnp.cumsum(log_f, axis=1).transpose(0, 2, 1)
    kh = k.transpose(0, 2, 1, 3)
    vh = v.transpose(0, 2, 1, 3)
    q_blocks = q.transpose(0, 2, 1, 3).reshape(bsz, heads, n_blk, Q_BLOCK, dh).transpose(2, 0, 1, 3, 4)
    dq_blocks = cum.reshape(bsz, heads, n_blk, Q_BLOCK).transpose(2, 0, 1, 3)
    key_pos = jnp.arange(seq)

    def one_block(args):
        qb, dqb, i = args
        s = jnp.einsum('bhqd,bhkd->bhqk', qb, kh, preferred_element_type=jnp.float32) * scale
        s = s + (dqb[..., :, None] - cum[..., None, :])
        q_pos = i * Q_BLOCK + jnp.arange(Q_BLOCK)
        causal = key_pos[None, :] <= q_pos[:, None]
        p = jax.nn.softmax(jnp.where(causal, s, -jnp.inf), axis=-1).astype(vh.dtype)
        return jnp.einsum('bhqk,bhkd->bhqd', p, vh)

    out = lax.map(one_block, (q_blocks, dq_blocks, jnp.arange(n_blk)))
    return out.transpose(1, 0, 3, 2, 4).reshape(bsz, seq, heads * dh)


def causal_depthwise_conv(u, w, b):
    out = lax.conv_general_dilated(u, w[:, None, :], window_strides=(1,), padding=[(SSM_CONV - 1, 0)],
                                   dimension_numbers=('NWC', 'WIO', 'NWC'), feature_group_count=u.shape[-1])
    return out + b


def ssd_chunked(x, dt, a, bmat, cmat):
    bsz, seq, heads, hd = x.shape
    nc, L = seq // SSM_CHUNK, SSM_CHUNK
    G, R, N = SSM_GROUPS, SSM_HEADS_PER_GROUP, SSM_STATE
    xc = x.reshape(bsz, nc, L, G, R, hd)
    dtc = dt.reshape(bsz, nc, L, G, R)
    bc = bmat.reshape(bsz, nc, L, G, N)
    cc = cmat.reshape(bsz, nc, L, G, N)
    da = dtc * a.reshape(G, R)
    acum = jnp.cumsum(da, axis=2).transpose(0, 1, 3, 4, 2)
    xdt = xc * dtc[..., None]
    idx = jnp.arange(L)
    causal = idx[:, None] >= idx[None, :]
    decay = jnp.exp(jnp.where(causal, acum[..., :, None] - acum[..., None, :], -jnp.inf))
    cb = jnp.einsum('bclgn,bcsgn->bcgls', cc, bc, preferred_element_type=jnp.float32)
    y_diag = jnp.einsum('bcgls,bcgrls,bcsgrp->bclgrp', cb, decay, xdt)
    decay_to_end = jnp.exp(acum[..., -1:] - acum)
    states = jnp.einsum('bcsgn,bcgrs,bcsgrp->bcgrpn', bc, decay_to_end, xdt)
    chunk_decay = jnp.exp(acum[..., -1])

    def step(h, inp):
        st, dec = inp
        return h * dec[..., None, None] + st, h

    h0 = jnp.zeros((bsz, G, R, hd, N), jnp.float32)
    _, h_prev = lax.scan(step, h0, (states.transpose(1, 0, 2, 3, 4, 5), chunk_decay.transpose(1, 0, 2, 3)))
    h_prev = h_prev.transpose(1, 0, 2, 3, 4, 5)
    y_off = jnp.einsum('bclgn,bcgrpn,bcgrl->bclgrp', cc, h_prev, jnp.exp(acum))
    return (y_diag + y_off).reshape(bsz, seq, heads, hd)


def mamba2_branch(z, xbc, dt_raw, conv_w, conv_b, dt_bias, a_log, d_skip, norm_w):
    bsz, seq, _ = z.shape
    xbc = jax.nn.silu(causal_depthwise_conv(xbc, conv_w, conv_b))
    xs, bm, cm = jnp.split(xbc, [SSM_INNER, SSM_INNER + SSM_GROUPS * SSM_STATE], axis=-1)
    xs = xs.reshape(bsz, seq, SSM_HEADS, SSM_HEAD_DIM)
    bm = bm.reshape(bsz, seq, SSM_GROUPS, SSM_STATE)
    cm = cm.reshape(bsz, seq, SSM_GROUPS, SSM_STATE)
    dt = jax.nn.softplus(dt_raw.astype(jnp.float32) + dt_bias.astype(jnp.float32))
    a = -jnp.exp(a_log.astype(jnp.float32))
    y = ssd_chunked(xs, dt, a, bm, cm) + d_skip.astype(jnp.float32)[:, None] * xs
    u = (y.reshape(bsz, seq, SSM_INNER) * jax.nn.silu(z.astype(jnp.float32))).reshape(bsz, seq, SSM_GROUPS, -1)
    u = u * lax.rsqrt(jnp.mean(jnp.square(u), axis=-1, keepdims=True) + RMS_EPS)
    return (u.reshape(bsz, seq, SSM_INNER) * norm_w.astype(jnp.float32)).astype(z.dtype)


def _fwd_setup_inputs(seed: int = 0) -> dict:
    key = jax.random.key(seed)
    ks = jax.random.split(key, 20)
    f32 = jnp.float32

    def nrm(k, shape, fan_in, mult=1.0):
        return jax.random.normal(k, shape, f32) * (fan_in ** -0.5) * mult

    dt0 = jnp.exp(jax.random.uniform(ks[5], (DEPTH, SSM_HEADS), f32, math.log(1e-3), math.log(1e-1)))
    return {
        "x": jax.random.normal(ks[0], (BATCH, SEQ, D_MODEL), f32),
        "w_in": nrm(ks[1], (DEPTH, D_MODEL, IN_WIDTH), D_MODEL),
        "b_forget": jax.random.uniform(ks[2], (DEPTH, ATT_HEADS), f32, 1.0, 6.0),
        "conv_w": jax.random.uniform(ks[3], (DEPTH, SSM_CONV, SSM_CONV_DIM), f32, -0.5, 0.5),
        "conv_b": 0.02 * jax.random.normal(ks[4], (DEPTH, SSM_CONV_DIM), f32),
        "dt_bias": dt0 + jnp.log(-jnp.expm1(-dt0)),
        "a_log": jnp.log(jax.random.uniform(ks[6], (DEPTH, SSM_HEADS), f32, 1.0, 16.0)),
        "d_skip": 1.0 + 0.1 * jax.random.normal(ks[7], (DEPTH, SSM_HEADS), f32),
        "ssm_norm_w": 1.0 + 0.1 * jax.random.normal(ks[8], (DEPTH, SSM_INNER), f32),
        "w_proj_attn": nrm(ks[9], (DEPTH, ATT_WIDTH, D_MODEL), ATT_WIDTH, DEEPNORM_BETA),
        "w_proj_ssm": nrm(ks[10], (DEPTH, SSM_INNER, D_MODEL), SSM_INNER, DEEPNORM_BETA),
        "b_gates": 0.1 * jax.random.normal(ks[11], (DEPTH, N_BRANCHES * D_MODEL), f32),
        "w_out": nrm(ks[12], (DEPTH, D_MODEL, D_MODEL), D_MODEL, DEEPNORM_BETA),
        "ln1_g": 1.0 + 0.1 * jax.random.normal(ks[13], (DEPTH, D_MODEL), f32),
        "ln1_b": 0.02 * jax.random.normal(ks[14], (DEPTH, D_MODEL), f32),
        "w_ffn_gate": nrm(ks[15], (DEPTH, D_MODEL, FFN_HIDDEN), D_MODEL),
        "w_ffn_up": nrm(ks[16], (DEPTH, D_MODEL, FFN_HIDDEN), D_MODEL),
        "w_ffn_down": nrm(ks[17], (DEPTH, FFN_HIDDEN, D_MODEL), FFN_HIDDEN, DEEPNORM_BETA),
        "ln2_g": 1.0 + 0.1 * jax.random.normal(ks[18], (DEPTH, D_MODEL), f32),
        "ln2_b": 0.02 * jax.random.normal(ks[19], (DEPTH, D_MODEL), f32),
    }


def _fwd_reference(x, w_in, b_forget, conv_w, conv_b, dt_bias, a_log, d_skip, ssm_norm_w, w_proj_attn,
              w_proj_ssm, b_gates, w_out, ln1_g, ln1_b, w_ffn_gate, w_ffn_up, w_ffn_down, ln2_g, ln2_b):
    bsz, seq, _ = x.shape
    split_idx = [int(i) for i in np.cumsum(IN_SIZES)[:-1]]
    for l in range(DEPTH):
        proj = x @ w_in[l]
        q, k, v, f_logit, z, xbc, dt_raw, gate_logit = jnp.split(proj, split_idx, axis=-1)
        log_f = jax.nn.log_sigmoid(f_logit.astype(jnp.float32) + b_forget[l].astype(jnp.float32))
        hs = (bsz, seq, ATT_HEADS, ATT_HEAD_DIM)
        attn = forgetting_attention(q.reshape(hs), k.reshape(hs), v.reshape(hs), log_f)
        attn_d = attn @ w_proj_attn[l]
        ssm = mamba2_branch(z, xbc, dt_raw, conv_w[l], conv_b[l], dt_bias[l], a_log[l], d_skip[l], ssm_norm_w[l])
        ssm_d = ssm @ w_proj_ssm[l]
        gates = jax.nn.sigmoid(gate_logit + b_gates[l]).reshape(bsz, seq, N_BRANCHES, D_MODEL)
        mixed = (gates[:, :, 0] * attn_d + gates[:, :, 1] * ssm_d) @ w_out[l]
        x = layer_norm(DEEPNORM_ALPHA * x + mixed, ln1_g[l], ln1_b[l])
        h = (jax.nn.silu(x @ w_ffn_gate[l]) * (x @ w_ffn_up[l])) @ w_ffn_down[l]
        x = layer_norm(DEEPNORM_ALPHA * x + h, ln2_g[l], ln2_b[l])
    return x


import jax as _jax
import jax.numpy as _jnp

TWIN_FORMAT = 'train_step'
FWD_PARAMS = ['x', 'w_in', 'b_forget', 'conv_w', 'conv_b', 'dt_bias', 'a_log', 'd_skip', 'ssm_norm_w', 'w_proj_attn', 'w_proj_ssm', 'b_gates', 'w_out', 'ln1_g', 'ln1_b', 'w_ffn_gate', 'w_ffn_up', 'w_ffn_down', 'ln2_g', 'ln2_b']
TWIN_WEIGHTS = ['w_in', 'b_forget', 'conv_w', 'conv_b', 'dt_bias', 'a_log', 'd_skip', 'ssm_norm_w', 'w_proj_attn', 'w_proj_ssm', 'b_gates', 'w_out', 'ln1_g', 'ln1_b', 'w_ffn_gate', 'w_ffn_up', 'w_ffn_down', 'ln2_g', 'ln2_b']
TWIN_DIFF_INPUT = 'x'
TWIN_INPUTS = ['x', 'w_in', 'b_forget', 'conv_w', 'conv_b', 'dt_bias', 'a_log', 'd_skip', 'ssm_norm_w', 'w_proj_attn', 'w_proj_ssm', 'b_gates', 'w_out', 'ln1_g', 'ln1_b', 'w_ffn_gate', 'w_ffn_up', 'w_ffn_down', 'ln2_g', 'ln2_b', 'loss_target', 'm_w_in', 'm_b_forget', 'm_conv_w', 'm_conv_b', 'm_dt_bias', 'm_a_log', 'm_d_skip', 'm_ssm_norm_w', 'm_w_proj_attn', 'm_w_proj_ssm', 'm_b_gates', 'm_w_out', 'm_ln1_g', 'm_ln1_b', 'm_w_ffn_gate', 'm_w_ffn_up', 'm_w_ffn_down', 'm_ln2_g', 'm_ln2_b', 'v_w_in', 'v_b_forget', 'v_conv_w', 'v_conv_b', 'v_dt_bias', 'v_a_log', 'v_d_skip', 'v_ssm_norm_w', 'v_w_proj_attn', 'v_w_proj_ssm', 'v_b_gates', 'v_w_out', 'v_ln1_g', 'v_ln1_b', 'v_w_ffn_gate', 'v_w_ffn_up', 'v_w_ffn_down', 'v_ln2_g', 'v_ln2_b']
TWIN_OUTPUTS = ['loss', 'grad_x', 'grad_w_in', 'grad_b_forget', 'grad_conv_w', 'grad_conv_b', 'grad_dt_bias', 'grad_a_log', 'grad_d_skip', 'grad_ssm_norm_w', 'grad_w_proj_attn', 'grad_w_proj_ssm', 'grad_b_gates', 'grad_w_out', 'grad_ln1_g', 'grad_ln1_b', 'grad_w_ffn_gate', 'grad_w_ffn_up', 'grad_w_ffn_down', 'grad_ln2_g', 'grad_ln2_b', 'delta_w_in', 'delta_b_forget', 'delta_conv_w', 'delta_conv_b', 'delta_dt_bias', 'delta_a_log', 'delta_d_skip', 'delta_ssm_norm_w', 'delta_w_proj_attn', 'delta_w_proj_ssm', 'delta_b_gates', 'delta_w_out', 'delta_ln1_g', 'delta_ln1_b', 'delta_w_ffn_gate', 'delta_w_ffn_up', 'delta_w_ffn_down', 'delta_ln2_g', 'delta_ln2_b', 'new_m_w_in', 'new_m_b_forget', 'new_m_conv_w', 'new_m_conv_b', 'new_m_dt_bias', 'new_m_a_log', 'new_m_d_skip', 'new_m_ssm_norm_w', 'new_m_w_proj_attn', 'new_m_w_proj_ssm', 'new_m_b_gates', 'new_m_w_out', 'new_m_ln1_g', 'new_m_ln1_b', 'new_m_w_ffn_gate', 'new_m_w_ffn_up', 'new_m_w_ffn_down', 'new_m_ln2_g', 'new_m_ln2_b', 'new_v_w_in', 'new_v_b_forget', 'new_v_conv_w', 'new_v_conv_b', 'new_v_dt_bias', 'new_v_a_log', 'new_v_d_skip', 'new_v_ssm_norm_w', 'new_v_w_proj_attn', 'new_v_w_proj_ssm', 'new_v_b_gates', 'new_v_w_out', 'new_v_ln1_g', 'new_v_ln1_b', 'new_v_w_ffn_gate', 'new_v_w_ffn_up', 'new_v_w_ffn_down', 'new_v_ln2_g', 'new_v_ln2_b']
TWIN_LEAF_KINDS = {'loss': 'loss', 'grad_x': 'grad_x', 'grad_w_in': 'grad_w', 'grad_b_forget': 'grad_w', 'grad_conv_w': 'grad_w', 'grad_conv_b': 'grad_w', 'grad_dt_bias': 'grad_w', 'grad_a_log': 'grad_w', 'grad_d_skip': 'grad_w', 'grad_ssm_norm_w': 'grad_w', 'grad_w_proj_attn': 'grad_w', 'grad_w_proj_ssm': 'grad_w', 'grad_b_gates': 'grad_w', 'grad_w_out': 'grad_w', 'grad_ln1_g': 'grad_w', 'grad_ln1_b': 'grad_w', 'grad_w_ffn_gate': 'grad_w', 'grad_w_ffn_up': 'grad_w', 'grad_w_ffn_down': 'grad_w', 'grad_ln2_g': 'grad_w', 'grad_ln2_b': 'grad_w', 'delta_w_in': 'delta_w', 'delta_b_forget': 'delta_w', 'delta_conv_w': 'delta_w', 'delta_conv_b': 'delta_w', 'delta_dt_bias': 'delta_w', 'delta_a_log': 'delta_w', 'delta_d_skip': 'delta_w', 'delta_ssm_norm_w': 'delta_w', 'delta_w_proj_attn': 'delta_w', 'delta_w_proj_ssm': 'delta_w', 'delta_b_gates': 'delta_w', 'delta_w_out': 'delta_w', 'delta_ln1_g': 'delta_w', 'delta_ln1_b': 'delta_w', 'delta_w_ffn_gate': 'delta_w', 'delta_w_ffn_up': 'delta_w', 'delta_w_ffn_down': 'delta_w', 'delta_ln2_g': 'delta_w', 'delta_ln2_b': 'delta_w', 'new_m_w_in': 'new_m', 'new_m_b_forget': 'new_m', 'new_m_conv_w': 'new_m', 'new_m_conv_b': 'new_m', 'new_m_dt_bias': 'new_m', 'new_m_a_log': 'new_m', 'new_m_d_skip': 'new_m', 'new_m_ssm_norm_w': 'new_m', 'new_m_w_proj_attn': 'new_m', 'new_m_w_proj_ssm': 'new_m', 'new_m_b_gates': 'new_m', 'new_m_w_out': 'new_m', 'new_m_ln1_g': 'new_m', 'new_m_ln1_b': 'new_m', 'new_m_w_ffn_gate': 'new_m', 'new_m_w_ffn_up': 'new_m', 'new_m_w_ffn_down': 'new_m', 'new_m_ln2_g': 'new_m', 'new_m_ln2_b': 'new_m', 'new_v_w_in': 'new_v', 'new_v_b_forget': 'new_v', 'new_v_conv_w': 'new_v', 'new_v_conv_b': 'new_v', 'new_v_dt_bias': 'new_v', 'new_v_a_log': 'new_v', 'new_v_d_skip': 'new_v', 'new_v_ssm_norm_w': 'new_v', 'new_v_w_proj_attn': 'new_v', 'new_v_w_proj_ssm': 'new_v', 'new_v_b_gates': 'new_v', 'new_v_w_out': 'new_v', 'new_v_ln1_g': 'new_v', 'new_v_ln1_b': 'new_v', 'new_v_w_ffn_gate': 'new_v', 'new_v_w_ffn_up': 'new_v', 'new_v_w_ffn_down': 'new_v', 'new_v_ln2_g': 'new_v', 'new_v_ln2_b': 'new_v'}


def _forward(args):
    return _fwd_reference(*[args[k] for k in FWD_PARAMS])


def _output_shape():
    def fwd():
        inp = _fwd_setup_inputs(0)
        return _fwd_reference(*[inp[k] for k in FWD_PARAMS])
    out = _jax.eval_shape(fwd)
    return out.shape, out.dtype

N_MICROBATCH = 1
ADAM_LR = 0.001
ADAM_B1 = 0.9
ADAM_B2 = 0.999
ADAM_EPS = 1e-08
ADAM_WD = 0.01
ADAM_STEP = 10
PER_EXAMPLE_BATCH_AXIS = {'x': 0, 'loss_target': 0}
SHARED_INPUTS = []
_WEIGHT_DTYPES = {'w_in': _jnp.float32, 'b_forget': _jnp.float32, 'conv_w': _jnp.float32, 'conv_b': _jnp.float32, 'dt_bias': _jnp.float32, 'a_log': _jnp.float32, 'd_skip': _jnp.float32, 'ssm_norm_w': _jnp.float32, 'w_proj_attn': _jnp.float32, 'w_proj_ssm': _jnp.float32, 'b_gates': _jnp.float32, 'w_out': _jnp.float32, 'ln1_g': _jnp.float32, 'ln1_b': _jnp.float32, 'w_ffn_gate': _jnp.float32, 'w_ffn_up': _jnp.float32, 'w_ffn_down': _jnp.float32, 'ln2_g': _jnp.float32, 'ln2_b': _jnp.float32}
MOMENT_SCALE = {'w_in': 2.179551e-02, 'b_forget': 7.681995e-02, 'conv_w': 4.648672e-02, 'conv_b': 6.814705e-02, 'dt_bias': 3.015057e-02, 'a_log': 2.162726e-02, 'd_skip': 3.190784e-01, 'ssm_norm_w': 3.077811e-02, 'w_proj_attn': 2.329068e-02, 'w_proj_ssm': 7.521008e-02, 'b_gates': 1.361148e-02, 'w_out': 7.805942e-02, 'ln1_g': 1.128108e+01, 'ln1_b': 1.041029e+00, 'w_ffn_gate': 4.614592e-02, 'w_ffn_up': 4.633615e-02, 'w_ffn_down': 1.301218e-01, 'ln2_g': 6.535407e+01, 'ln2_b': 1.636035e+00}


def _to_microbatches(a, axis):
    t = _jnp.moveaxis(a, axis, 0)
    t = t.reshape((N_MICROBATCH, t.shape[0] // N_MICROBATCH) + t.shape[1:])
    return _jnp.moveaxis(t, 1, axis + 1)


def setup_inputs(seed: int = 0) -> dict:
    inp = _fwd_setup_inputs(seed)
    key = _jax.random.fold_in(_jax.random.key(seed), 7919)
    shape, _ = _output_shape()
    out = dict(inp)
    out["loss_target"] = _jax.random.normal(_jax.random.fold_in(key, 0), shape, _jnp.float32)
    for i, name in enumerate(TWIN_WEIGHTS):
        w = inp[name].astype(_jnp.float32)
        if MOMENT_SCALE is None:
            s = _jnp.sqrt(_jnp.mean(_jnp.square(w)) + 1e-30)
        else:
            s = MOMENT_SCALE[name]
        km, kv = _jax.random.split(_jax.random.fold_in(key, i + 1))
        out[name] = w
        out["m_" + name] = s * _jax.random.normal(km, w.shape, _jnp.float32)
        out["v_" + name] = (s * s) * _jax.random.uniform(kv, w.shape, _jnp.float32, 0.5, 1.5)
    if N_MICROBATCH > 1:
        for name, axis in PER_EXAMPLE_BATCH_AXIS.items():
            out[name] = _to_microbatches(out[name], axis)
    return {'x': out['x'], 'w_in': out['w_in'], 'b_forget': out['b_forget'], 'conv_w': out['conv_w'], 'conv_b': out['conv_b'], 'dt_bias': out['dt_bias'], 'a_log': out['a_log'], 'd_skip': out['d_skip'], 'ssm_norm_w': out['ssm_norm_w'], 'w_proj_attn': out['w_proj_attn'], 'w_proj_ssm': out['w_proj_ssm'], 'b_gates': out['b_gates'], 'w_out': out['w_out'], 'ln1_g': out['ln1_g'], 'ln1_b': out['ln1_b'], 'w_ffn_gate': out['w_ffn_gate'], 'w_ffn_up': out['w_ffn_up'], 'w_ffn_down': out['w_ffn_down'], 'ln2_g': out['ln2_g'], 'ln2_b': out['ln2_b'], 'loss_target': out['loss_target'], 'm_w_in': out['m_w_in'], 'm_b_forget': out['m_b_forget'], 'm_conv_w': out['m_conv_w'], 'm_conv_b': out['m_conv_b'], 'm_dt_bias': out['m_dt_bias'], 'm_a_log': out['m_a_log'], 'm_d_skip': out['m_d_skip'], 'm_ssm_norm_w': out['m_ssm_norm_w'], 'm_w_proj_attn': out['m_w_proj_attn'], 'm_w_proj_ssm': out['m_w_proj_ssm'], 'm_b_gates': out['m_b_gates'], 'm_w_out': out['m_w_out'], 'm_ln1_g': out['m_ln1_g'], 'm_ln1_b': out['m_ln1_b'], 'm_w_ffn_gate': out['m_w_ffn_gate'], 'm_w_ffn_up': out['m_w_ffn_up'], 'm_w_ffn_down': out['m_w_ffn_down'], 'm_ln2_g': out['m_ln2_g'], 'm_ln2_b': out['m_ln2_b'], 'v_w_in': out['v_w_in'], 'v_b_forget': out['v_b_forget'], 'v_conv_w': out['v_conv_w'], 'v_conv_b': out['v_conv_b'], 'v_dt_bias': out['v_dt_bias'], 'v_a_log': out['v_a_log'], 'v_d_skip': out['v_d_skip'], 'v_ssm_norm_w': out['v_ssm_norm_w'], 'v_w_proj_attn': out['v_w_proj_attn'], 'v_w_proj_ssm': out['v_w_proj_ssm'], 'v_b_gates': out['v_b_gates'], 'v_w_out': out['v_w_out'], 'v_ln1_g': out['v_ln1_g'], 'v_ln1_b': out['v_ln1_b'], 'v_w_ffn_gate': out['v_w_ffn_gate'], 'v_w_ffn_up': out['v_w_ffn_up'], 'v_w_ffn_down': out['v_w_ffn_down'], 'v_ln2_g': out['v_ln2_g'], 'v_ln2_b': out['v_ln2_b']}


def _loss(weights, diff, rest, loss_target):
    with _jax.named_scope("forward"):
        args = {**rest, TWIN_DIFF_INPUT: diff, **{k: w.astype(_WEIGHT_DTYPES[k]) for k, w in weights.items()}}
        y = _forward(args)
    with _jax.named_scope("loss_head"):
        err = _jnp.square(y.astype(_jnp.float32) - loss_target)
        return 0.5 * _jnp.sum(_jnp.mean(err, axis=-1)) if err.ndim else 0.5 * err


def _adamw(w, g, m, v):
    m = ADAM_B1 * m + (1.0 - ADAM_B1) * g
    v = ADAM_B2 * v + (1.0 - ADAM_B2) * _jnp.square(g)
    m_hat = m / (1.0 - ADAM_B1 ** ADAM_STEP)
    v_hat = v / (1.0 - ADAM_B2 ** ADAM_STEP)
    delta = -ADAM_LR * (m_hat / (_jnp.sqrt(v_hat) + ADAM_EPS) + ADAM_WD * w)
    return delta, m, v


def reference(x, w_in, b_forget, conv_w, conv_b, dt_bias, a_log, d_skip, ssm_norm_w, w_proj_attn, w_proj_ssm, b_gates, w_out, ln1_g, ln1_b, w_ffn_gate, w_ffn_up, w_ffn_down, ln2_g, ln2_b, loss_target, m_w_in, m_b_forget, m_conv_w, m_conv_b, m_dt_bias, m_a_log, m_d_skip, m_ssm_norm_w, m_w_proj_attn, m_w_proj_ssm, m_b_gates, m_w_out, m_ln1_g, m_ln1_b, m_w_ffn_gate, m_w_ffn_up, m_w_ffn_down, m_ln2_g, m_ln2_b, v_w_in, v_b_forget, v_conv_w, v_conv_b, v_dt_bias, v_a_log, v_d_skip, v_ssm_norm_w, v_w_proj_attn, v_w_proj_ssm, v_b_gates, v_w_out, v_ln1_g, v_ln1_b, v_w_ffn_gate, v_w_ffn_up, v_w_ffn_down, v_ln2_g, v_ln2_b):
    given = dict(x=x, w_in=w_in, b_forget=b_forget, conv_w=conv_w, conv_b=conv_b, dt_bias=dt_bias, a_log=a_log, d_skip=d_skip, ssm_norm_w=ssm_norm_w, w_proj_attn=w_proj_attn, w_proj_ssm=w_proj_ssm, b_gates=b_gates, w_out=w_out, ln1_g=ln1_g, ln1_b=ln1_b, w_ffn_gate=w_ffn_gate, w_ffn_up=w_ffn_up, w_ffn_down=w_ffn_down, ln2_g=ln2_g, ln2_b=ln2_b, loss_target=loss_target, m_w_in=m_w_in, m_b_forget=m_b_forget, m_conv_w=m_conv_w, m_conv_b=m_conv_b, m_dt_bias=m_dt_bias, m_a_log=m_a_log, m_d_skip=m_d_skip, m_ssm_norm_w=m_ssm_norm_w, m_w_proj_attn=m_w_proj_attn, m_w_proj_ssm=m_w_proj_ssm, m_b_gates=m_b_gates, m_w_out=m_w_out, m_ln1_g=m_ln1_g, m_ln1_b=m_ln1_b, m_w_ffn_gate=m_w_ffn_gate, m_w_ffn_up=m_w_ffn_up, m_w_ffn_down=m_w_ffn_down, m_ln2_g=m_ln2_g, m_ln2_b=m_ln2_b, v_w_in=v_w_in, v_b_forget=v_b_forget, v_conv_w=v_conv_w, v_conv_b=v_conv_b, v_dt_bias=v_dt_bias, v_a_log=v_a_log, v_d_skip=v_d_skip, v_ssm_norm_w=v_ssm_norm_w, v_w_proj_attn=v_w_proj_attn, v_w_proj_ssm=v_w_proj_ssm, v_b_gates=v_b_gates, v_w_out=v_w_out, v_ln1_g=v_ln1_g, v_ln1_b=v_ln1_b, v_w_ffn_gate=v_w_ffn_gate, v_w_ffn_up=v_w_ffn_up, v_w_ffn_down=v_w_ffn_down, v_ln2_g=v_ln2_g, v_ln2_b=v_ln2_b)
    weights = {n: given[n] for n in TWIN_WEIGHTS}
    shared = {n: given[n] for n in SHARED_INPUTS}
    per_example = {n: given[n] for n in ['x']}
    grad_fn = _jax.value_and_grad(_loss, argnums=(0, 1))

    def one_microbatch(ex, loss_target):
        ex = dict(ex)
        diff = ex.pop(TWIN_DIFF_INPUT)
        return grad_fn(weights, diff, {**shared, **ex}, loss_target)

    if N_MICROBATCH == 1:
        loss, (grad_w, grad_x) = one_microbatch(per_example, given["loss_target"])
    else:
        def body(carry, xs):
            loss_sum, grad_sum = carry
            l_k, (gw_k, gx_k) = one_microbatch(xs[0], xs[1])
            with _jax.named_scope("update"):
                return (loss_sum + l_k, _jax.tree.map(_jnp.add, grad_sum, gw_k)), gx_k

        init = (_jnp.zeros((), _jnp.float32), _jax.tree.map(_jnp.zeros_like, weights))
        (loss, grad_w), grad_x = _jax.lax.scan(body, init, (per_example, given["loss_target"]))
    with _jax.named_scope("update"):
        delta_w, new_m, new_v = {}, {}, {}
        for n in TWIN_WEIGHTS:
            delta_w[n], new_m[n], new_v[n] = _adamw(weights[n], grad_w[n], given["m_" + n], given["v_" + n])
    return (loss, grad_x, *[grad_w[n] for n in TWIN_WEIGHTS], *[delta_w[n] for n in TWIN_WEIGHTS],
            *[new_m[n] for n in TWIN_WEIGHTS], *[new_v[n] for n in TWIN_WEIGHTS])
```

```python
import functools
import math

import jax
import jax.numpy as jnp
from jax import lax
from jax.experimental import pallas as pl
from jax.experimental.pallas import tpu as pltpu

F32 = jnp.float32
BF16 = jnp.bfloat16

N_DEV = 8
D_MODEL = 1024
ATT_HEADS = 16
ATT_HEAD_DIM = 64
SSM_INNER = 2048
SSM_HEADS = 32
SSM_HEAD_DIM = 64
SSM_GROUPS = 4
SSM_HEADS_PER_GROUP = 8
SSM_STATE = 128
SSM_CONV = 4
SSM_CHUNK = 128
SSM_CONV_DIM = 3072
FFN_HIDDEN = 2816
IN_WIDTH = 10288
DEEPNORM_ALPHA = 2.0 ** 0.25
LN_EPS = 1e-5
RMS_EPS = 1e-5
ADAM_LR, ADAM_B1, ADAM_B2, ADAM_EPS, ADAM_WD, ADAM_STEP = 0.001, 0.9, 0.999, 1e-08, 0.01, 10
ATT_SCALE = 1.0 / math.sqrt(ATT_HEAD_DIM)

LANES = 128
VMEM_LIMIT = 56 * 1024 * 1024
NEG = -1e30

DT_LANE0 = 0
F_LANE0 = 32
HI = lax.Precision.HIGHEST


def _cp(*sem):
    return pltpu.CompilerParams(dimension_semantics=sem, vmem_limit_bytes=VMEM_LIMIT)


def _tile(n, cap=1408):
    for t in (1408, 1024, 512, 384, 256, 128):
        if t <= cap and n % t == 0:
            return t
    return n


def _sigmoid(x):
    return 1.0 / (1.0 + jnp.exp(-x))


def _mm(a, b, *, ta=False, tb=False, out_dtype=F32, add=None, name):
    m, k = (a.shape[1], a.shape[0]) if ta else a.shape
    n = b.shape[0] if tb else b.shape[1]
    assert (b.shape[1] if tb else b.shape[0]) == k
    tm, tn, tk = _tile(m), _tile(n), _tile(k)
    nk = k // tk
    dims = (((0,) if ta else (1,), (1,) if tb else (0,)), ((), ()))

    def body(*refs):
        if add is None:
            a_ref, b_ref, o_ref, acc_ref = refs
        else:
            a_ref, b_ref, c_ref, o_ref, acc_ref = refs
        kk = pl.program_id(2)

        @pl.when(kk == 0)
        def _():
            acc_ref[...] = jnp.zeros_like(acc_ref)

        acc_ref[...] += lax.dot_general(a_ref[...].astype(BF16), b_ref[...].astype(BF16), dims,
                                        preferred_element_type=F32)

        @pl.when(kk == nk - 1)
        def _():
            r = acc_ref[...]
            if add is not None:
                r = r + c_ref[...]
            o_ref[...] = r.astype(o_ref.dtype)

    a_spec = pl.BlockSpec((tk, tm), lambda i, j, kk: (kk, i)) if ta else pl.BlockSpec((tm, tk), lambda i, j, kk: (i, kk))
    b_spec = pl.BlockSpec((tn, tk), lambda i, j, kk: (j, kk)) if tb else pl.BlockSpec((tk, tn), lambda i, j, kk: (kk, j))
    o_spec = pl.BlockSpec((tm, tn), lambda i, j, kk: (i, j))
    in_specs, args = [a_spec, b_spec], [a, b]
    if add is not None:
        in_specs.append(o_spec)
        args.append(add)
    return pl.pallas_call(
        body, name=name, grid=(m // tm, n // tn, nk), in_specs=in_specs, out_specs=o_spec,
        out_shape=jax.ShapeDtypeStruct((m, n), out_dtype),
        scratch_shapes=[pltpu.VMEM((tm, tn), F32)],
        compiler_params=_cp("parallel", "parallel", "arbitrary"),
    )(*args)


def _tri(n, lower=True):
    r = lax.broadcasted_iota(jnp.int32, (n, n), 0)
    c = lax.broadcasted_iota(jnp.int32, (n, n), 1)
    return jnp.where((r >= c) if lower else (c >= r), 1.0, 0.0).astype(F32)


def _lane_rep_matrix():
    r = lax.broadcasted_iota(jnp.int32, (LANES, D_MODEL), 0)
    c = lax.broadcasted_iota(jnp.int32, (LANES, D_MODEL), 1)
    return jnp.where(r == F_LANE0 + c // ATT_HEAD_DIM, 1.0, 0.0).astype(F32)


def _stats_fwd(fd, bias_row, a_row):
    s = fd.shape[0]
    blk = SSM_CHUNK

    def body(fd_ref, bias_ref, a_ref, dt_ref, ac_ref, cf_ref, dtr_ref, acr_ref, cfr_ref, cq_ref, carry_ref):
        @pl.when(pl.program_id(0) == 0)
        def _():
            carry_ref[...] = jnp.zeros_like(carry_ref)

        v = fd_ref[...] + bias_ref[...]
        dt = jnp.maximum(v, 0.0) + jnp.log(1.0 + jnp.exp(-jnp.abs(v)))
        lf = jnp.minimum(v, 0.0) - jnp.log(1.0 + jnp.exp(-jnp.abs(v)))
        tri = _tri(blk)
        ac = jnp.dot(tri, dt * a_ref[...], precision=HI, preferred_element_type=F32)
        cf = jnp.dot(tri, lf, precision=HI, preferred_element_type=F32) + carry_ref[0:1, :]
        carry_ref[...] = carry_ref[...] + jnp.sum(lf, axis=0, keepdims=True)
        dt_ref[...] = dt
        ac_ref[...] = ac
        cf_ref[...] = cf
        dtr_ref[...] = dt.T
        acr_ref[...] = ac.T
        cfr_ref[...] = cf.T
        cq_ref[...] = jnp.dot(cf, _lane_rep_matrix(), precision=HI, preferred_element_type=F32)

    col = pl.BlockSpec((blk, LANES), lambda i: (i, 0))
    row = pl.BlockSpec((LANES, blk), lambda i: (0, i))
    vec = pl.BlockSpec((1, LANES), lambda i: (0, 0))
    return pl.pallas_call(
        body, name="stats_fwd", grid=(s // blk,), in_specs=[col, vec, vec],
        out_specs=[col, col, col, row, row, row, pl.BlockSpec((blk, D_MODEL), lambda i: (i, 0))],
        out_shape=[jax.ShapeDtypeStruct((s, LANES), F32)] * 3 + [jax.ShapeDtypeStruct((LANES, s), F32)] * 3
        + [jax.ShapeDtypeStruct((s, D_MODEL), F32)],
        scratch_shapes=[pltpu.VMEM((8, LANES), F32)],
        compiler_params=_cp("arbitrary"),
    )(fd, bias_row, a_row)


def _stats_bwd(fd, bias_row, ddt, dck_rows, dcq_cols):
    s = fd.shape[0]
    blk = SSM_CHUNK
    nb = s // blk

    def body(fd_ref, bias_ref, ddt_ref, dck_ref, dcq_ref, o_ref, db_ref, carry_ref):
        @pl.when(pl.program_id(0) == 0)
        def _():
            carry_ref[...] = jnp.zeros_like(carry_ref)
            db_ref[...] = jnp.zeros_like(db_ref)

        v = fd_ref[...] + bias_ref[...]
        dcum = dck_ref[...].T + dcq_ref[...]
        dlf = jnp.dot(_tri(blk, lower=False), dcum, precision=HI, preferred_element_type=F32) + carry_ref[0:1, :]
        carry_ref[...] = carry_ref[...] + jnp.sum(dcum, axis=0, keepdims=True)
        lane = lax.broadcasted_iota(jnp.int32, v.shape, 1)
        g = jnp.where(lane < F_LANE0, ddt_ref[...] * _sigmoid(v), dlf * _sigmoid(-v))
        g = jnp.where(lane < F_LANE0 + ATT_HEADS, g, 0.0)
        o_ref[...] = g.astype(o_ref.dtype)
        db_ref[...] += jnp.sum(g, axis=0, keepdims=True)

    col = pl.BlockSpec((blk, LANES), lambda i: (nb - 1 - i, 0))
    row = pl.BlockSpec((LANES, blk), lambda i: (0, nb - 1 - i))
    vec = pl.BlockSpec((1, LANES), lambda i: (0, 0))
    return pl.pallas_call(
        body, name="stats_bwd", grid=(nb,), in_specs=[col, vec, col, row, col], out_specs=[col, vec],
        out_shape=[jax.ShapeDtypeStruct((s, LANES), BF16), jax.ShapeDtypeStruct((1, LANES), F32)],
        scratch_shapes=[pltpu.VMEM((8, LANES), F32)],
        compiler_params=_cp("arbitrary"),
    )(fd, bias_row, ddt, dck_rows, dcq_cols)


HP = LANES // ATT_HEAD_DIM
N_HP = ATT_HEADS // HP


def _att_block(s):
    return 512 if s % 512 == 0 and s >= 2048 else 128


def _att_logits(q, k, cq, ck, row0, col0):
    sc = lax.dot_general(q, k, (((1,), (1,)), ((), ())), preferred_element_type=F32)
    sc = sc + (cq - ck)
    r = row0 + lax.broadcasted_iota(jnp.int32, sc.shape, 0)
    c = col0 + lax.broadcasted_iota(jnp.int32, sc.shape, 1)
    return jnp.where(c <= r, sc, NEG)


def _attention_fwd(qkv, cq_rep, ck_rows):
    s = qkv.shape[0]
    bq = bk = _att_block(s)
    nq = s // bq

    def body(q_ref, k_ref, v_ref, cq_ref, ck_ref, o_ref, lse_ref, m_ref, l_ref, acc_ref):
        i, j = pl.program_id(1), pl.program_id(2)

        @pl.when(j == 0)
        def _():
            m_ref[...] = jnp.full_like(m_ref, NEG)
            l_ref[...] = jnp.zeros_like(l_ref)
            acc_ref[...] = jnp.zeros_like(acc_ref)

        @pl.when(j <= i)
        def _():
            for a in range(HP):
                cs = slice(a * ATT_HEAD_DIM, (a + 1) * ATT_HEAD_DIM)
                q = (q_ref[:, cs].astype(F32) * ATT_SCALE).astype(BF16)
                sc = _att_logits(q, k_ref[:, cs], cq_ref[:, a * ATT_HEAD_DIM:a * ATT_HEAD_DIM + 1],
                                 ck_ref[0, a:a + 1, :], i * bq, j * bk)
                m_old = m_ref[a]
                m_new = jnp.maximum(m_old, jnp.max(sc, axis=1, keepdims=True))
                alpha = jnp.exp(m_old - m_new)
                p = jnp.exp(sc - m_new)
                l_ref[a] = alpha * l_ref[a] + jnp.sum(p, axis=1, keepdims=True)
                acc_ref[a] = alpha * acc_ref[a] + jnp.dot(p.astype(BF16), v_ref[:, cs], preferred_element_type=F32)
                m_ref[a] = m_new

        @pl.when(j == i)
        def _():
            for a in range(HP):
                cs = slice(a * ATT_HEAD_DIM, (a + 1) * ATT_HEAD_DIM)
                o_ref[:, cs] = acc_ref[a] / l_ref[a]
                lse_ref[:, cs] = jnp.broadcast_to(m_ref[a] + jnp.log(l_ref[a]), (bq, ATT_HEAD_DIM))

    q_spec = pl.BlockSpec((bq, LANES), lambda h, i, j: (i, h))
    k_spec = pl.BlockSpec((bk, LANES), lambda h, i, j: (jnp.minimum(j, i), N_HP + h))
    v_spec = pl.BlockSpec((bk, LANES), lambda h, i, j: (jnp.minimum(j, i), 2 * N_HP + h))
    ck_spec = pl.BlockSpec((1, HP, bk), lambda h, i, j: (h, 0, jnp.minimum(j, i)))
    return pl.pallas_call(
        body, name="att_fwd", grid=(N_HP, nq, nq),
        in_specs=[q_spec, k_spec, v_spec, q_spec, ck_spec], out_specs=[q_spec, q_spec],
        out_shape=[jax.ShapeDtypeStruct((s, D_MODEL), F32)] * 2,
        scratch_shapes=[pltpu.VMEM((HP, bq, 1), F32), pltpu.VMEM((HP, bq, 1), F32),
                        pltpu.VMEM((HP, bq, ATT_HEAD_DIM), F32)],
        compiler_params=_cp("parallel", "arbitrary", "arbitrary"),
    )(qkv, qkv, qkv, cq_rep, ck_rows)


def _att_delta(do, o):
    s = do.shape[0]
    bs = _tile(s, 512)

    def body(do_ref, o_ref, d_ref):
        r = lax.broadcasted_iota(jnp.int32, (LANES, LANES), 0) // ATT_HEAD_DIM
        c = lax.broadcasted_iota(jnp.int32, (LANES, LANES), 1) // ATT_HEAD_DIM
        e = jnp.where(r == c, 1.0, 0.0).astype(F32)
        for p in range(D_MODEL // LANES):
            cs = slice(p * LANES, (p + 1) * LANES)
            d_ref[:, cs] = jnp.dot(do_ref[:, cs] * o_ref[:, cs], e, precision=HI, preferred_element_type=F32)

    spec = pl.BlockSpec((bs, D_MODEL), lambda i: (i, 0))
    return pl.pallas_call(body, name="att_delta", grid=(s // bs,), in_specs=[spec, spec], out_specs=spec,
                          out_shape=jax.ShapeDtypeStruct((s, D_MODEL), F32), compiler_params=_cp("parallel"))(do, o)


def _attention_bwd(qkv, cq_rep, ck_rows, lse_rep, delta_rep, do):
    s = qkv.shape[0]
    bq = bk = _att_block(s)
    nq = s // bq

    def body(q_ref, k_ref, v_ref, cq_ref, ck_ref, lse_ref, dl_ref, do_ref, dq_ref, dk_ref, dv_ref, dck_ref, dcq_ref,
             dk_acc, dv_acc, dck_acc):
        j, i = pl.program_id(1), pl.program_id(2)

        @pl.when((j == 0) & (i == 0))
        def _():
            dq_ref[...] = jnp.zeros_like(dq_ref)
            dcq_ref[...] = jnp.zeros_like(dcq_ref)

        @pl.when(i == j)
        def _():
            dk_acc[...] = jnp.zeros_like(dk_acc)
            dv_acc[...] = jnp.zeros_like(dv_acc)
            dck_acc[...] = jnp.zeros_like(dck_acc)

        @pl.when(i >= j)
        def _():
            rows = pl.ds(pl.multiple_of(i * bq, bq), bq)
            for a in range(HP):
                cs = slice(a * ATT_HEAD_DIM, (a + 1) * ATT_HEAD_DIM)
                c1 = slice(a * ATT_HEAD_DIM, a * ATT_HEAD_DIM + 1)
                q = (q_ref[:, cs].astype(F32) * ATT_SCALE).astype(BF16)
                k = k_ref[:, cs]
                do_a = do_ref[:, cs].astype(BF16)
                sc = _att_logits(q, k, cq_ref[:, c1], ck_ref[0, a:a + 1, :], i * bq, j * bk)
                p = jnp.exp(sc - lse_ref[:, c1])
                dp = lax.dot_general(do_a, v_ref[:, cs], (((1,), (1,)), ((), ())), preferred_element_type=F32)
                ds = p * (dp - dl_ref[:, c1])
                ds_b = ds.astype(BF16)
                dv_acc[a] += lax.dot_general(p.astype(BF16), do_a, (((0,), (0,)), ((), ())),
                                             preferred_element_type=F32)
                dk_acc[a] += lax.dot_general(ds_b, q, (((0,), (0,)), ((), ())), preferred_element_type=F32)
                dq_ref[rows, cs] += jnp.dot(ds_b, k, preferred_element_type=F32) * ATT_SCALE
                dck_acc[a] -= jnp.sum(ds, axis=0, keepdims=True)
                dcq_ref[rows, cs] += jnp.broadcast_to(jnp.sum(ds, axis=1, keepdims=True), (bq, ATT_HEAD_DIM))

        @pl.when(i == nq - 1)
        def _():
            for a in range(HP):
                cs = slice(a * ATT_HEAD_DIM, (a + 1) * ATT_HEAD_DIM)
                dk_ref[:, cs] = dk_acc[a].astype(dk_ref.dtype)
                dv_ref[:, cs] = dv_acc[a].astype(dv_ref.dtype)
                dck_ref[0, a:a + 1, :] = dck_acc[a]

    qi = lambda h, j, i: (jnp.maximum(i, j), h)
    q_spec = pl.BlockSpec((bq, LANES), qi)
    k_spec = pl.BlockSpec((bk, LANES), lambda h, j, i: (j, N_HP + h))
    v_spec = pl.BlockSpec((bk, LANES), lambda h, j, i: (j, 2 * N_HP + h))
    ck_spec = pl.BlockSpec((1, HP, bk), lambda h, j, i: (h, 0, j))
    kout = pl.BlockSpec((bk, LANES), lambda h, j, i: (j, h))
    qout = pl.BlockSpec((s, LANES), lambda h, j, i: (0, h))
    return pl.pallas_call(
        body, name="att_bwd", grid=(N_HP, nq, nq),
        in_specs=[q_spec, k_spec, v_spec, q_spec, ck_spec, q_spec, q_spec, q_spec],
        out_specs=[qout, kout, kout, ck_spec, qout],
        out_shape=[jax.ShapeDtypeStruct((s, D_MODEL), F32), jax.ShapeDtypeStruct((s, D_MODEL), BF16),
                   jax.ShapeDtypeStruct((s, D_MODEL), BF16), jax.ShapeDtypeStruct((N_HP, HP, s), F32),
                   jax.ShapeDtypeStruct((s, D_MODEL), F32)],
        scratch_shapes=[pltpu.VMEM((HP, bk, ATT_HEAD_DIM), F32), pltpu.VMEM((HP, bk, ATT_HEAD_DIM), F32),
                        pltpu.VMEM((HP, 1, bk), F32)],
        compiler_params=_cp("parallel", "arbitrary", "arbitrary"),
    )(qkv, qkv, qkv, cq_rep, ck_rows, lse_rep, delta_rep, do)


def _silu_and_grad(x):
    sg = _sigmoid(x)
    return x * sg, sg * (1.0 + x * (1.0 - sg))


def _conv_pre(cur, halo, w_ref, b_ref, first):
    halo = jnp.where(first, 0.0, halo)
    row = lax.broadcasted_iota(jnp.int32, cur.shape, 0)
    shifted = []
    for k in range(SSM_CONV):
        sh = SSM_CONV - 1 - k
        if sh == 0:
            shifted.append(cur)
            continue
        r = pltpu.roll(cur, sh, 0)
        hr = pltpu.roll(halo, sh, 0)
        top = jnp.where(row[0:8] < sh, hr, r[0:8])
        shifted.append(jnp.concatenate([top, r[8:]], axis=0))
    pre = b_ref[...] + sum(w_ref[k:k + 1, :] * shifted[k] for k in range(SSM_CONV))
    return pre, shifted


def _conv_specs(s, bs, bc):
    cur = pl.BlockSpec((bs, bc), lambda j, i: (i, j))
    halo = pl.BlockSpec((8, bc), lambda j, i: (jnp.maximum(i * (bs // 8) - 1, 0), j))
    w = pl.BlockSpec((SSM_CONV, bc), lambda j, i: (0, j))
    b = pl.BlockSpec((1, bc), lambda j, i: (0, j))
    return cur, halo, w, b


def _conv_fwd(xbc, w, b):
    s, c = xbc.shape
    bs, bc = _tile(s, 512), 1024

    def body(x_ref, h_ref, w_ref, b_ref, o_ref):
        pre, _ = _conv_pre(x_ref[...], h_ref[...], w_ref, b_ref, pl.program_id(1) == 0)
        o_ref[...] = pre * _sigmoid(pre)

    cur, halo, ws, bsp = _conv_specs(s, bs, bc)
    return pl.pallas_call(body, name="conv_fwd", grid=(c // bc, s // bs), in_specs=[cur, halo, ws, bsp],
                          out_specs=cur, out_shape=jax.ShapeDtypeStruct((s, c), F32),
                          compiler_params=_cp("parallel", "parallel"))(xbc, xbc, w, b)


def _conv_bwd_pre(xbc, w, b, dact):
    s, c = xbc.shape
    bs, bc = _tile(s, 512), 1024

    def body(x_ref, h_ref, w_ref, b_ref, g_ref, dp_ref, dw_ref, db_ref):
        @pl.when(pl.program_id(1) == 0)
        def _():
            dw_ref[...] = jnp.zeros_like(dw_ref)
            db_ref[...] = jnp.zeros_like(db_ref)

        pre, shifted = _conv_pre(x_ref[...], h_ref[...], w_ref, b_ref, pl.program_id(1) == 0)
        dpre = g_ref[...] * _silu_and_grad(pre)[1]
        dp_ref[...] = dpre
        db_ref[...] += jnp.sum(dpre, axis=0, keepdims=True)
        for k in range(SSM_CONV):
            dw_ref[k:k + 1, :] += jnp.sum(dpre * shifted[k], axis=0, keepdims=True)

    cur, halo, ws, bsp = _conv_specs(s, bs, bc)
    return pl.pallas_call(
        body, name="conv_bwd_pre", grid=(c // bc, s // bs), in_specs=[cur, halo, ws, bsp, cur],
        out_specs=[cur, ws, bsp],
        out_shape=[jax.ShapeDtypeStruct((s, c), F32), jax.ShapeDtypeStruct((SSM_CONV, c), F32),
                   jax.ShapeDtypeStruct((1, c), F32)],
        compiler_params=_cp("parallel", "arbitrary"))(xbc, xbc, w, b, dact)


def _conv_bwd_in(dpre, w):
    s, c = dpre.shape
    bs, bc = _tile(s, 512), 1024
    nb = s // bs

    def body(g_ref, n_ref, w_ref, o_ref):
        cur = g_ref[...]
        nxt = jnp.where(pl.program_id(1) == nb - 1, 0.0, n_ref[...])
        row = lax.broadcasted_iota(jnp.int32, cur.shape, 0)
        acc = w_ref[SSM_CONV - 1:SSM_CONV, :] * cur
        for sh in range(1, SSM_CONV):
            r = pltpu.roll(cur, bs - sh, 0)
            nr = pltpu.roll(nxt, 8 - sh, 0)
            bot = jnp.where(row[0:8] >= 8 - sh, nr, r[bs - 8:])
            acc = acc + w_ref[SSM_CONV - 1 - sh:SSM_CONV - sh, :] * jnp.concatenate([r[:bs - 8], bot], axis=0)
        o_ref[...] = acc.astype(o_ref.dtype)

    cur = pl.BlockSpec((bs, bc), lambda j, i: (i, j))
    nxt = pl.BlockSpec((8, bc), lambda j, i: (jnp.minimum((i + 1) * (bs // 8), s // 8 - 1), j))
    ws = pl.BlockSpec((SSM_CONV, bc), lambda j, i: (0, j))
    return pl.pallas_call(body, name="conv_bwd_in", grid=(c // bc, nb), in_specs=[cur, nxt, ws], out_specs=cur,
                          out_shape=jax.ShapeDtypeStruct((s, c), BF16),
                          compiler_params=_cp("parallel", "parallel"))(dpre, dpre, w)


def _dotT(a, b):
    return lax.dot_general(a.astype(BF16), b.astype(BF16), (((1,), (1,)), ((), ())), preferred_element_type=F32)


def _Tdot(a, b):
    return lax.dot_general(a.astype(BF16), b.astype(BF16), (((0,), (0,)), ((), ())), preferred_element_type=F32)


def _dot(a, b):
    return jnp.dot(a.astype(BF16), b.astype(BF16), preferred_element_type=F32)


def _ssd_head(xbc_ref, dt_ref, ac_ref, acr_ref, h):
    L = SSM_CHUNK
    xs = xbc_ref[:, h * SSM_HEAD_DIM:(h + 1) * SSM_HEAD_DIM]
    dt_col = dt_ref[:, h:h + 1]
    a_col = ac_ref[:, h:h + 1]
    a_row = acr_ref[h:h + 1, :]
    li = lax.broadcasted_iota(jnp.int32, (L, L), 0)
    si = lax.broadcasted_iota(jnp.int32, (L, L), 1)
    decay = jnp.exp(jnp.where(li >= si, a_col - a_row, NEG))
    a_last = ac_ref[L - 1:L, h:h + 1]
    return xs, dt_col, a_col, a_last, decay


def _ssd_fwd(xbc_act, dt_c, ac_c, ac_r, d_skip):
    s = xbc_act.shape[0]
    L, P, N, G, R = SSM_CHUNK, SSM_HEAD_DIM, SSM_STATE, SSM_GROUPS, SSM_HEADS_PER_GROUP
    nc = s // L

    def body(dsk_ref, xbc_ref, dt_ref, ac_ref, acr_ref, y_ref, hp_ref, st_ref):
        @pl.when(pl.program_id(0) == 0)
        def _():
            st_ref[...] = jnp.zeros_like(st_ref)

        for g in range(G):
            b_g = xbc_ref[:, SSM_INNER + g * N:SSM_INNER + (g + 1) * N]
            c_g = xbc_ref[:, SSM_INNER + G * N + g * N:SSM_INNER + G * N + (g + 1) * N]
            cb = _dotT(c_g, b_g)
            for r in range(R):
                h = g * R + r
                xs, dt_col, a_col, a_last, decay = _ssd_head(xbc_ref, dt_ref, ac_ref, acr_ref, h)
                xdt = xs * dt_col
                hprev = st_ref[h]
                y = _dot(cb * decay, xdt) + jnp.exp(a_col) * _dotT(c_g, hprev) + dsk_ref[h] * xs
                y_ref[:, h * P:(h + 1) * P] = y
                hp_ref[0, h] = hprev
                st_ref[h] = hprev * jnp.exp(a_last) + _Tdot(xdt * jnp.exp(a_last - a_col), b_g)

    col = pl.BlockSpec((L, LANES), lambda c: (c, 0))
    return pl.pallas_call(
        body, name="ssd_fwd", grid=(nc,),
        in_specs=[pl.BlockSpec(memory_space=pltpu.SMEM), pl.BlockSpec((L, SSM_CONV_DIM), lambda c: (c, 0)), col, col,
                  pl.BlockSpec((LANES, L), lambda c: (0, c))],
        out_specs=[pl.BlockSpec((L, SSM_INNER), lambda c: (c, 0)),
                   pl.BlockSpec((1, SSM_HEADS, P, N), lambda c: (c, 0, 0, 0))],
        out_shape=[jax.ShapeDtypeStruct((s, SSM_INNER), F32), jax.ShapeDtypeStruct((nc, SSM_HEADS, P, N), F32)],
        scratch_shapes=[pltpu.VMEM((SSM_HEADS, P, N), F32)],
        compiler_params=_cp("arbitrary"),
    )(d_skip, xbc_act, dt_c, ac_c, ac_r)


def _ssd_bwd(xbc_act, dt_c, ac_c, ac_r, hprev_all, dy, d_skip, a_row):
    s = xbc_act.shape[0]
    L, P, N, G, R = SSM_CHUNK, SSM_HEAD_DIM, SSM_STATE, SSM_GROUPS, SSM_HEADS_PER_GROUP
    nc = s // L

    def body(dsk_ref, xbc_ref, dt_ref, ac_ref, acr_ref, hp_ref, dy_ref, arow_ref,
             dx_ref, ddt_ref, da_ref, dds_ref, dh_ref):
        @pl.when(pl.program_id(0) == 0)
        def _():
            dh_ref[...] = jnp.zeros_like(dh_ref)
            da_ref[...] = jnp.zeros_like(da_ref)
            dds_ref[...] = jnp.zeros_like(dds_ref)

        lane = lax.broadcasted_iota(jnp.int32, (L, LANES), 1)
        sub = lax.broadcasted_iota(jnp.int32, (LANES, L), 0)
        rowi = lax.broadcasted_iota(jnp.int32, (L, 1), 0)
        lane1 = lax.broadcasted_iota(jnp.int32, (1, LANES), 1)
        da_c = jnp.zeros((L, LANES), F32)
        da_r = jnp.zeros((LANES, L), F32)
        ddt1 = jnp.zeros((L, LANES), F32)
        dds = jnp.zeros((1, LANES), F32)
        for g in range(G):
            b_g = xbc_ref[:, SSM_INNER + g * N:SSM_INNER + (g + 1) * N]
            c_g = xbc_ref[:, SSM_INNER + G * N + g * N:SSM_INNER + G * N + (g + 1) * N]
            cb = _dotT(c_g, b_g)
            dcb = jnp.zeros((L, L), F32)
            db_g = jnp.zeros((L, N), F32)
            dc_g = jnp.zeros((L, N), F32)
            for r in range(R):
                h = g * R + r
                xs, dt_col, a_col, a_last, decay = _ssd_head(xbc_ref, dt_ref, ac_ref, acr_ref, h)
                gy = dy_ref[:, h * P:(h + 1) * P]
                xdt = xs * dt_col
                hprev = hp_ref[0, h]
                dhn = dh_ref[h]
                e_a = jnp.exp(a_col)
                e_last = jnp.exp(a_last)
                e_col = jnp.exp(a_last - a_col)
                m = cb * decay
                yoff = e_a * _dotT(c_g, hprev)
                da_col = jnp.sum(gy * yoff, axis=1, keepdims=True)
                dc_g = dc_g + e_a * _dot(gy, hprev)
                dhp = _Tdot(gy * e_a, c_g) + dhn * e_last
                da_last = jnp.sum(jnp.sum(dhn * hprev, axis=1, keepdims=True), axis=0, keepdims=True) * e_last
                xds = _dot(xdt, dhn)
                db_g = db_g + e_col * xds
                de_e = jnp.sum(xds * b_g, axis=1, keepdims=True) * e_col
                da_col = da_col - de_e
                da_last = da_last + jnp.sum(de_e, axis=0, keepdims=True)
                dxdt = e_col * _dotT(b_g, dhn)
                dm = _dotT(gy, xdt)
                dxdt = dxdt + _Tdot(m, gy)
                dcb = dcb + dm * decay
                w = dm * m
                da_col = da_col + jnp.sum(w, axis=1, keepdims=True) + jnp.where(rowi == L - 1, da_last, 0.0)
                da_c = jnp.where(lane == h, da_col, da_c)
                da_r = jnp.where(sub == h, jnp.sum(w, axis=0, keepdims=True), da_r)
                ddt1 = jnp.where(lane == h, jnp.sum(dxdt * xs, axis=1, keepdims=True), ddt1)
                dds = jnp.where(lane1 == h, jnp.sum(jnp.sum(gy * xs, axis=1, keepdims=True), axis=0, keepdims=True),
                                dds)
                dx_ref[:, h * P:(h + 1) * P] = dxdt * dt_col + dsk_ref[h] * gy
                dh_ref[h] = dhp
            dx_ref[:, SSM_INNER + g * N:SSM_INNER + (g + 1) * N] = db_g + _Tdot(dcb, c_g)
            dx_ref[:, SSM_INNER + G * N + g * N:SSM_INNER + G * N + (g + 1) * N] = dc_g + _dot(dcb, b_g)
        dda = jnp.dot(_tri(L, lower=False), da_c - da_r.T, precision=HI, preferred_element_type=F32)
        ddt_ref[...] = dda * arow_ref[...] + ddt1
        da_ref[...] += jnp.sum(dda * dt_ref[...], axis=0, keepdims=True)
        dds_ref[...] += dds

    col = pl.BlockSpec((L, LANES), lambda c: (nc - 1 - c, 0))
    vec = pl.BlockSpec((1, LANES), lambda c: (0, 0))
    return pl.pallas_call(
        body, name="ssd_bwd", grid=(nc,),
        in_specs=[pl.BlockSpec(memory_space=pltpu.SMEM), pl.BlockSpec((L, SSM_CONV_DIM), lambda c: (nc - 1 - c, 0)),
                  col, col, pl.BlockSpec((LANES, L), lambda c: (0, nc - 1 - c)),
                  pl.BlockSpec((1, SSM_HEADS, P, N), lambda c: (nc - 1 - c, 0, 0, 0)),
                  pl.BlockSpec((L, SSM_INNER), lambda c: (nc - 1 - c, 0)), vec],
        out_specs=[pl.BlockSpec((L, SSM_CONV_DIM), lambda c: (nc - 1 - c, 0)), col, vec, vec],
        out_shape=[jax.ShapeDtypeStruct((s, SSM_CONV_DIM), F32), jax.ShapeDtypeStruct((s, LANES), F32),
                   jax.ShapeDtypeStruct((1, LANES), F32), jax.ShapeDtypeStruct((1, LANES), F32)],
        scratch_shapes=[pltpu.VMEM((SSM_HEADS, P, N), F32)],
        compiler_params=_cp("arbitrary"),
    )(d_skip, xbc_act, dt_c, ac_c, ac_r, hprev_all, dy, a_row)


ROWS = 256
GW = SSM_INNER // SSM_GROUPS


def _rows(width, dtype=F32):
    return pl.BlockSpec((ROWS, width), lambda i: (i, 0))


def _vec(width):
    return pl.BlockSpec((1, width), lambda i: (0, 0))


def _gnorm_fwd(y, z, w):
    s = y.shape[0]

    def body(y_ref, z_ref, w_ref, o_ref):
        for g in range(SSM_GROUPS):
            cs = slice(g * GW, (g + 1) * GW)
            zz = z_ref[:, cs]
            u = y_ref[:, cs] * (zz * _sigmoid(zz))
            r = lax.rsqrt(jnp.mean(u * u, axis=1, keepdims=True) + RMS_EPS)
            o_ref[:, cs] = (u * r * w_ref[:, cs]).astype(o_ref.dtype)

    return pl.pallas_call(body, name="gnorm_fwd", grid=(s // ROWS,),
                          in_specs=[_rows(SSM_INNER), _rows(SSM_INNER), _vec(SSM_INNER)], out_specs=_rows(SSM_INNER),
                          out_shape=jax.ShapeDtypeStruct((s, SSM_INNER), BF16), compiler_params=_cp("parallel"))(y, z, w)


def _gnorm_bwd(y, z, w, do):
    s = y.shape[0]

    def body(y_ref, z_ref, w_ref, do_ref, dy_ref, dz_ref, dw_ref):
        @pl.when(pl.program_id(0) == 0)
        def _():
            dw_ref[...] = jnp.zeros_like(dw_ref)

        for g in range(SSM_GROUPS):
            cs = slice(g * GW, (g + 1) * GW)
            zz, yy, dd = z_ref[:, cs], y_ref[:, cs], do_ref[:, cs]
            sz, dsz = _silu_and_grad(zz)
            u = yy * sz
            r = lax.rsqrt(jnp.mean(u * u, axis=1, keepdims=True) + RMS_EPS)
            n = u * r
            dn = dd * w_ref[:, cs]
            dw_ref[:, cs] += jnp.sum(dd * n, axis=0, keepdims=True)
            du = r * (dn - n * jnp.mean(dn * n, axis=1, keepdims=True))
            dy_ref[:, cs] = du * sz
            dz_ref[:, cs] = (du * yy * dsz).astype(dz_ref.dtype)

    return pl.pallas_call(
        body, name="gnorm_bwd", grid=(s // ROWS,),
        in_specs=[_rows(SSM_INNER), _rows(SSM_INNER), _vec(SSM_INNER), _rows(SSM_INNER)],
        out_specs=[_rows(SSM_INNER), _rows(SSM_INNER), _vec(SSM_INNER)],
        out_shape=[jax.ShapeDtypeStruct((s, SSM_INNER), F32), jax.ShapeDtypeStruct((s, SSM_INNER), BF16),
                   jax.ShapeDtypeStruct((1, SSM_INNER), F32)],
        compiler_params=_cp("arbitrary"))(y, z, w, do)


def _mix_fwd(gl, bg, attn_d, ssm_d):
    s = gl.shape[0]
    d = D_MODEL

    def body(gl_ref, bg_ref, a_ref, m_ref, o_ref):
        g0 = _sigmoid(gl_ref[:, :d] + bg_ref[:, :d])
        g1 = _sigmoid(gl_ref[:, d:] + bg_ref[:, d:])
        o_ref[...] = (g0 * a_ref[...] + g1 * m_ref[...]).astype(o_ref.dtype)

    return pl.pallas_call(body, name="mix_fwd", grid=(s // ROWS,),
                          in_specs=[_rows(2 * d), _vec(2 * d), _rows(d), _rows(d)], out_specs=_rows(d),
                          out_shape=jax.ShapeDtypeStruct((s, d), BF16), compiler_params=_cp("parallel"))(
        gl, bg, attn_d, ssm_d)


def _mix_bwd(gl, bg, attn_d, ssm_d, dmix):
    s = gl.shape[0]
    d = D_MODEL

    def body(gl_ref, bg_ref, a_ref, m_ref, dm_ref, da_ref, ds_ref, dg_ref, db_ref):
        @pl.when(pl.program_id(0) == 0)
        def _():
            db_ref[...] = jnp.zeros_like(db_ref)

        g0 = _sigmoid(gl_ref[:, :d] + bg_ref[:, :d])
        g1 = _sigmoid(gl_ref[:, d:] + bg_ref[:, d:])
        dm = dm_ref[...]
        da_ref[...] = (dm * g0).astype(da_ref.dtype)
        ds_ref[...] = (dm * g1).astype(ds_ref.dtype)
        dl0 = dm * a_ref[...] * g0 * (1.0 - g0)
        dl1 = dm * m_ref[...] * g1 * (1.0 - g1)
        dg_ref[:, :d] = dl0.astype(dg_ref.dtype)
        dg_ref[:, d:] = dl1.astype(dg_ref.dtype)
        db_ref[:, :d] += jnp.sum(dl0, axis=0, keepdims=True)
        db_ref[:, d:] += jnp.sum(dl1, axis=0, keepdims=True)

    return pl.pallas_call(
        body, name="mix_bwd", grid=(s // ROWS,),
        in_specs=[_rows(2 * d), _vec(2 * d), _rows(d), _rows(d), _rows(d)],
        out_specs=[_rows(d), _rows(d), _rows(2 * d), _vec(2 * d)],
        out_shape=[jax.ShapeDtypeStruct((s, d), BF16), jax.ShapeDtypeStruct((s, d), BF16),
                   jax.ShapeDtypeStruct((s, 2 * d), BF16), jax.ShapeDtypeStruct((1, 2 * d), F32)],
        compiler_params=_cp("arbitrary"))(gl, bg, attn_d, ssm_d, dmix)


def _ln_stats(p):
    mu = jnp.mean(p, axis=1, keepdims=True)
    c = p - mu
    rstd = lax.rsqrt(jnp.mean(c * c, axis=1, keepdims=True) + LN_EPS)
    return c * rstd, rstd


def _ln_bwd(dy, xhat, rstd, g):
    dxh = dy * g
    return rstd * (dxh - jnp.mean(dxh, axis=1, keepdims=True) - xhat * jnp.mean(dxh * xhat, axis=1, keepdims=True))


def _ln1_fwd(x, mixed, g, b):
    s, d = x.shape

    def body(x_ref, m_ref, g_ref, b_ref, o_ref):
        xhat, _ = _ln_stats(DEEPNORM_ALPHA * x_ref[...] + m_ref[...])
        o_ref[...] = xhat * g_ref[...] + b_ref[...]

    return pl.pallas_call(body, name="ln1_fwd", grid=(s // ROWS,), in_specs=[_rows(d), _rows(d), _vec(d), _vec(d)],
                          out_specs=_rows(d), out_shape=jax.ShapeDtypeStruct((s, d), F32),
                          compiler_params=_cp("parallel"))(x, mixed, g, b)


def _ln2_loss(x1, h, target, g, b):
    s, d = x1.shape

    def body(x_ref, h_ref, t_ref, g_ref, b_ref, dp_ref, loss_ref, dg_ref, db_ref):
        @pl.when(pl.program_id(0) == 0)
        def _():
            loss_ref[...] = jnp.zeros_like(loss_ref)
            dg_ref[...] = jnp.zeros_like(dg_ref)
            db_ref[...] = jnp.zeros_like(db_ref)

        xhat, rstd = _ln_stats(DEEPNORM_ALPHA * x_ref[...] + h_ref[...])
        err = xhat * g_ref[...] + b_ref[...] - t_ref[...]
        part = 0.5 * jnp.sum(jnp.mean(err * err, axis=1, keepdims=True), axis=0, keepdims=True)
        loss_ref[...] += jnp.broadcast_to(part, loss_ref.shape)
        dy = err * (1.0 / d)
        dg_ref[...] += jnp.sum(dy * xhat, axis=0, keepdims=True)
        db_ref[...] += jnp.sum(dy, axis=0, keepdims=True)
        dp_ref[...] = _ln_bwd(dy, xhat, rstd, g_ref[...])

    return pl.pallas_call(
        body, name="ln2_loss", grid=(s // ROWS,), in_specs=[_rows(d), _rows(d), _rows(d), _vec(d), _vec(d)],
        out_specs=[_rows(d), _vec(LANES), _vec(d), _vec(d)],
        out_shape=[jax.ShapeDtypeStruct((s, d), F32), jax.ShapeDtypeStruct((1, LANES), F32),
                   jax.ShapeDtypeStruct((1, d), F32), jax.ShapeDtypeStruct((1, d), F32)],
        compiler_params=_cp("arbitrary"))(x1, h, target, g, b)


def _ln1_bwd(x, mixed, g, dpre2, dffn):
    s, d = x.shape

    def body(x_ref, m_ref, g_ref, d2_ref, df_ref, dp_ref, dr_ref, dg_ref, db_ref):
        @pl.when(pl.program_id(0) == 0)
        def _():
            dg_ref[...] = jnp.zeros_like(dg_ref)
            db_ref[...] = jnp.zeros_like(db_ref)

        xhat, rstd = _ln_stats(DEEPNORM_ALPHA * x_ref[...] + m_ref[...])
        dy = DEEPNORM_ALPHA * d2_ref[...] + df_ref[...]
        dg_ref[...] += jnp.sum(dy * xhat, axis=0, keepdims=True)
        db_ref[...] += jnp.sum(dy, axis=0, keepdims=True)
        dp = _ln_bwd(dy, xhat, rstd, g_ref[...])
        dp_ref[...] = dp
        dr_ref[...] = DEEPNORM_ALPHA * dp

    return pl.pallas_call(
        body, name="ln1_bwd", grid=(s // ROWS,), in_specs=[_rows(d), _rows(d), _vec(d), _rows(d), _rows(d)],
        out_specs=[_rows(d), _rows(d), _vec(d), _vec(d)],
        out_shape=[jax.ShapeDtypeStruct((s, d), F32), jax.ShapeDtypeStruct((s, d), F32),
                   jax.ShapeDtypeStruct((1, d), F32), jax.ShapeDtypeStruct((1, d), F32)],
        compiler_params=_cp("arbitrary"))(x, mixed, g, dpre2, dffn)


def _swiglu_fwd(gu):
    s = gu.shape[0]
    f = FFN_HIDDEN

    def body(g_ref, u_ref, o_ref):
        gg = g_ref[...]
        o_ref[...] = (gg * _sigmoid(gg) * u_ref[...]).astype(o_ref.dtype)

    return pl.pallas_call(
        body, name="swiglu_fwd", grid=(s // ROWS,),
        in_specs=[pl.BlockSpec((ROWS, f), lambda i: (i, 0)), pl.BlockSpec((ROWS, f), lambda i: (i, 1))],
        out_specs=_rows(f), out_shape=jax.ShapeDtypeStruct((s, f), BF16), compiler_params=_cp("parallel"))(gu, gu)


def _swiglu_bwd(gu, dact):
    s = gu.shape[0]
    f = FFN_HIDDEN

    def body(g_ref, u_ref, d_ref, o_ref):
        sg, dsg = _silu_and_grad(g_ref[...])
        dd = d_ref[...]
        o_ref[:, :f] = (dd * u_ref[...] * dsg).astype(o_ref.dtype)
        o_ref[:, f:] = (dd * sg).astype(o_ref.dtype)

    return pl.pallas_call(
        body, name="swiglu_bwd", grid=(s // ROWS,),
        in_specs=[pl.BlockSpec((ROWS, f), lambda i: (i, 0)), pl.BlockSpec((ROWS, f), lambda i: (i, 1)), _rows(f)],
        out_specs=_rows(2 * f), out_shape=jax.ShapeDtypeStruct((s, 2 * f), BF16),
        compiler_params=_cp("parallel"))(gu, gu, dact)


def _peer(k):
    x, y, c = lax.axis_index("x"), lax.axis_index("y"), lax.axis_index("c")
    kx, ky, kc = (k >> 2) & 1, (k >> 1) & 1, k & 1
    px = (1 - x) if kx else x
    py = (1 - y) if ky else y
    pc = (1 - c) if kc else c
    return (px, py, pc), 4 * px + 2 * py + pc


def _my_index():
    return 4 * lax.axis_index("x") + 2 * lax.axis_index("y") + lax.axis_index("c")


def _all_gather(parts):
    n = len(parts)

    def body(*refs):
        ins, outs = refs[:n], refs[n:2 * n]
        send_sems, recv_sems, local_sems = refs[2 * n:]
        me = _my_index()
        local = [pltpu.make_async_copy(ins[t], outs[t].at[me], local_sems.at[t]) for t in range(n)]
        for cp in local:
            cp.start()
        remote = []
        for k in range(1, N_DEV):
            peer, _ = _peer(k)
            for t in range(n):
                remote.append(pltpu.make_async_remote_copy(
                    src_ref=ins[t], dst_ref=outs[t].at[me], send_sem=send_sems.at[t, k - 1],
                    recv_sem=recv_sems.at[t, k - 1], device_id=peer, device_id_type=pl.DeviceIdType.MESH))
        for cp in remote:
            cp.start()
        for cp in remote:
            cp.wait()
        for cp in local:
            cp.wait()

    anyspec = pl.BlockSpec(memory_space=pl.ANY)
    return pl.pallas_call(
        body, name="all_gather", in_specs=[anyspec] * n, out_specs=[anyspec] * n,
        out_shape=[jax.ShapeDtypeStruct((N_DEV,) + p.shape, p.dtype) for p in parts],
        scratch_shapes=[pltpu.SemaphoreType.DMA((n, N_DEV - 1)), pltpu.SemaphoreType.DMA((n, N_DEV - 1)),
                        pltpu.SemaphoreType.DMA((n,))],
    )(*parts)


def _exchange(gpack):
    def body(g_ref, o_ref, send_sems, recv_sems, local_sem):
        me = _my_index()
        local = pltpu.make_async_copy(g_ref.at[me], o_ref.at[me], local_sem)
        local.start()
        remote = []
        for k in range(1, N_DEV):
            peer, pidx = _peer(k)
            remote.append(pltpu.make_async_remote_copy(
                src_ref=g_ref.at[pidx], dst_ref=o_ref.at[me], send_sem=send_sems.at[k - 1],
                recv_sem=recv_sems.at[k - 1], device_id=peer, device_id_type=pl.DeviceIdType.MESH))
        for cp in remote:
            cp.start()
        for cp in remote:
            cp.wait()
        local.wait()

    anyspec = pl.BlockSpec(memory_space=pl.ANY)
    return pl.pallas_call(
        body, name="grad_exchange", in_specs=[anyspec], out_specs=anyspec,
        out_shape=jax.ShapeDtypeStruct(gpack.shape, gpack.dtype),
        scratch_shapes=[pltpu.SemaphoreType.DMA((N_DEV - 1,)), pltpu.SemaphoreType.DMA((N_DEV - 1,)),
                        pltpu.SemaphoreType.DMA],
    )(gpack)


ADAM_ROWS = 1280


def _adamw(recv, w, m, v):
    r = w.shape[0]
    c1 = 1.0 / (1.0 - ADAM_B1 ** ADAM_STEP)
    c2 = 1.0 / (1.0 - ADAM_B2 ** ADAM_STEP)

    def body(r_ref, w_ref, m_ref, v_ref, g_ref, d_ref, mo_ref, vo_ref):
        g = r_ref[0]
        for k in range(1, N_DEV):
            g = g + r_ref[k]
        mn = ADAM_B1 * m_ref[...] + (1.0 - ADAM_B1) * g
        vn = ADAM_B2 * v_ref[...] + (1.0 - ADAM_B2) * (g * g)
        g_ref[...] = g
        mo_ref[...] = mn
        vo_ref[...] = vn
        d_ref[...] = -ADAM_LR * ((mn * c1) / (jnp.sqrt(vn * c2) + ADAM_EPS) + ADAM_WD * w_ref[...])

    blk = pl.BlockSpec((ADAM_ROWS, LANES), lambda i: (i, 0))
    return pl.pallas_call(
        body, name="adamw", grid=(r // ADAM_ROWS,),
        in_specs=[pl.BlockSpec((N_DEV, ADAM_ROWS, LANES), lambda i: (0, i, 0)), blk, blk, blk],
        out_specs=[blk] * 4, out_shape=[jax.ShapeDtypeStruct((r, LANES), F32)] * 4,
        compiler_params=_cp("parallel"))(recv, w, m, v)


def _lane_row(pairs):
    row = jnp.zeros((LANES,), F32)
    for lane0, vec in pairs:
        row = lax.dynamic_update_slice(row, vec.astype(F32), (lane0,))
    return row.reshape(1, LANES)


def _local_step(x, target, wts, small):
    s = x.shape[0]
    d = D_MODEL
    a = -jnp.exp(small["a_log"])
    bias_row = _lane_row([(DT_LANE0, small["dt_bias"]), (F_LANE0, small["b_forget"])])
    a_row = _lane_row([(DT_LANE0, a)])
    conv_b = small["conv_b"].reshape(1, -1)
    norm_w = small["ssm_norm_w"].reshape(1, -1)
    bg = small["b_gates"].reshape(1, -1)
    g1, b1 = small["ln1_g"].reshape(1, -1), small["ln1_b"].reshape(1, -1)
    g2, b2 = small["ln2_g"].reshape(1, -1), small["ln2_b"].reshape(1, -1)
    d_skip = small["d_skip"]
    xb = x.astype(BF16)

    qkv = _mm(xb, wts["qkv"], out_dtype=BF16, name="f_qkv")
    z = _mm(xb, wts["z"], name="f_z")
    xbc = _mm(xb, wts["xbc"], name="f_xbc")
    gl = _mm(xb, wts["gate"], name="f_gate")
    fd = _mm(xb, wts["fd"], name="f_fd")
    dt_c, ac_c, cf_c, dt_r, ac_r, cf_r, cq_rep = _stats_fwd(fd, bias_row, a_row)
    ck_rows = cf_r[F_LANE0:F_LANE0 + ATT_HEADS].reshape(N_HP, HP, s)
    attn, lse = _attention_fwd(qkv, cq_rep, ck_rows)
    attn_d = _mm(attn, wts["pa"], name="f_pa")
    xact = _conv_fwd(xbc, wts["conv"], conv_b)
    y, hprev = _ssd_fwd(xact, dt_c, ac_c, ac_r, d_skip)
    ssm = _gnorm_fwd(y, z, norm_w)
    ssm_d = _mm(ssm, wts["ps"], name="f_ps")
    mix = _mix_fwd(gl, bg, attn_d, ssm_d)
    mixed = _mm(mix, wts["out"], name="f_out")
    x1 = _ln1_fwd(x, mixed, g1, b1)
    gu = _mm(x1, wts["gu"], name="f_gu")
    act = _swiglu_fwd(gu)
    h = _mm(act, wts["down"], name="f_down")
    dpre2, loss_row, dg2, db2 = _ln2_loss(x1, h, target, g2, b2)

    d_act = _mm(dpre2, wts["down"], tb=True, name="b_down_x")
    dw_down = _mm(act, dpre2, ta=True, name="b_down_w")
    dgu = _swiglu_bwd(gu, d_act)
    dffn = _mm(dgu, wts["gu"], tb=True, name="b_gu_x")
    dw_gu = _mm(x1, dgu, ta=True, name="b_gu_w")
    dpre1, dxr, dg1, db1 = _ln1_bwd(x, mixed, g1, dpre2, dffn)
    dmix = _mm(dpre1, wts["out"], tb=True, name="b_out_x")
    dw_out = _mm(mix, dpre1, ta=True, name="b_out_w")
    dattn_d, dssm_d, dgl, dbg = _mix_bwd(gl, bg, attn_d, ssm_d, dmix)
    dssm = _mm(dssm_d, wts["ps"], tb=True, name="b_ps_x")
    dw_ps = _mm(ssm, dssm_d, ta=True, name="b_ps_w")
    dattn = _mm(dattn_d, wts["pa"], tb=True, name="b_pa_x")
    dw_pa = _mm(attn, dattn_d, ta=True, name="b_pa_w")
    dy, dz, dnw = _gnorm_bwd(y, z, norm_w, dssm)
    dxact, ddt, da_row, dds_row = _ssd_bwd(xact, dt_c, ac_c, ac_r, hprev, dy, d_skip, a_row)
    dpre_c, dconv_w, dconv_b = _conv_bwd_pre(xbc, wts["conv"], conv_b, dxact)
    dxbc = _conv_bwd_in(dpre_c, wts["conv"])
    delta = _att_delta(dattn, attn)
    dq, dk, dv, dck, dcq = _attention_bwd(qkv, cq_rep, ck_rows, lse, delta, dattn)
    dck_rows = jnp.zeros((LANES, s), F32).at[F_LANE0:F_LANE0 + ATT_HEADS].set(dck.reshape(ATT_HEADS, s))
    dcq_cols = jnp.zeros((s, LANES), F32).at[:, F_LANE0:F_LANE0 + ATT_HEADS].set(dcq[:, ::ATT_HEAD_DIM])
    dfd, dbias = _stats_bwd(fd, bias_row, ddt, dck_rows, dcq_cols)

    wq, wk, wv = wts["qkv"][:, :d], wts["qkv"][:, d:2 * d], wts["qkv"][:, 2 * d:]
    dx = dxr
    for i, (g_, w_) in enumerate(((dq, wq), (dk, wk), (dv, wv), (dz, wts["z"]), (dxbc, wts["xbc"]),
                                  (dgl, wts["gate"]), (dfd, wts["fd"]))):
        dx = _mm(g_, w_, tb=True, add=dx, name=f"b_in_x{i}")
    dw_in = [_mm(xb, g_, ta=True, name=f"b_in_w{i}") for i, g_ in enumerate((dq, dk, dv, dz, dxbc, dgl, dfd))]

    grads = dict(q=dw_in[0], k=dw_in[1], v=dw_in[2], z=dw_in[3], xbc=dw_in[4], gate=dw_in[5], fd=dw_in[6],
                 pa=dw_pa, ps=dw_ps, out=dw_out, gu=dw_gu, down=dw_down, conv=dconv_w)
    small_g = dict(
        b_forget=dbias[0, F_LANE0:F_LANE0 + ATT_HEADS], conv_b=dconv_b[0], dt_bias=dbias[0, :SSM_HEADS],
        a_log=da_row[0, :SSM_HEADS] * a, d_skip=dds_row[0, :SSM_HEADS], ssm_norm_w=dnw[0], b_gates=dbg[0],
        ln1_g=dg1[0], ln1_b=db1[0], ln2_g=dg2[0], ln2_b=db2[0])
    return loss_row[0, 0], dx, grads, small_g


BIG = ("w_in", "w_proj_attn", "w_proj_ssm", "w_out", "w_ffn_gate", "w_ffn_up", "w_ffn_down", "conv_w")
SMALL = ("b_forget", "conv_b", "dt_bias", "a_log", "d_skip", "ssm_norm_w", "b_gates", "ln1_g", "ln1_b", "ln2_g",
         "ln2_b")
PACK_ROWS = 23040


def _pack(big, small):
    flat = jnp.concatenate([big[n].reshape(-1).astype(F32) for n in BIG] + [small[n].reshape(-1) for n in SMALL])
    return jnp.pad(flat, (0, PACK_ROWS * LANES - flat.shape[0])).reshape(PACK_ROWS, LANES)


def _unpack(pack, big_shapes, small_shapes):
    flat = pack.reshape(-1)
    out, off = {}, 0
    for n in BIG:
        sz = math.prod(big_shapes[n])
        out[n] = flat[off:off + sz].reshape(big_shapes[n])
        off += sz
    for n in SMALL:
        sz = math.prod(small_shapes[n])
        out[n] = flat[off:off + sz].reshape(small_shapes[n])
        off += sz
    return out


def _split_cols(w):
    k, n = w.shape
    return w.reshape(k, N_DEV, n // N_DEV).transpose(1, 0, 2)


def _merge_cols(w):
    return w.transpose(1, 0, 2).reshape(w.shape[1], -1)


def kernel(x, w_in, b_forget, conv_w, conv_b, dt_bias, a_log, d_skip, ssm_norm_w, w_proj_attn, w_proj_ssm, b_gates, w_out, ln1_g, ln1_b, w_ffn_gate, w_ffn_up, w_ffn_down, ln2_g, ln2_b, loss_target, m_w_in, m_b_forget, m_conv_w, m_conv_b, m_dt_bias, m_a_log, m_d_skip, m_ssm_norm_w, m_w_proj_attn, m_w_proj_ssm, m_b_gates, m_w_out, m_ln1_g, m_ln1_b, m_w_ffn_gate, m_w_ffn_up, m_w_ffn_down, m_ln2_g, m_ln2_b, v_w_in, v_b_forget, v_conv_w, v_conv_b, v_dt_bias, v_a_log, v_d_skip, v_ssm_norm_w, v_w_proj_attn, v_w_proj_ssm, v_b_gates, v_w_out, v_ln1_g, v_ln1_b, v_w_ffn_gate, v_w_ffn_up, v_w_ffn_down, v_ln2_g, v_ln2_b):
    args = dict(locals())
    d, f = D_MODEL, FFN_HIDDEN
    big_w = {n: args[n][0] for n in BIG}
    small_w = {n: args[n][0] for n in SMALL}
    big_shapes = {n: args[n].shape for n in BIG}
    small_shapes = {n: args[n].shape for n in SMALL}

    wflat = jnp.concatenate([big_w[n].reshape(-1) for n in BIG[:-1]]).astype(BF16).reshape(-1, LANES)
    cflat = big_w["conv_w"].reshape(-1, LANES)
    gw, gc = _all_gather([wflat, cflat])
    gw = gw.reshape(N_DEV, -1)
    full, off = {}, 0
    for n in BIG[:-1]:
        shp = big_shapes[n][1:]
        sz = math.prod(shp)
        full[n] = gw[:, off:off + sz].reshape((N_DEV,) + shp)
        off += sz
    win = _merge_cols(full["w_in"])
    c0 = 3 * d
    wf, wz = win[:, c0:c0 + ATT_HEADS], win[:, c0 + ATT_HEADS:c0 + ATT_HEADS + SSM_INNER]
    c1 = c0 + ATT_HEADS + SSM_INNER
    wxbc, wdt, wgate = win[:, c1:c1 + SSM_CONV_DIM], win[:, c1 + SSM_CONV_DIM:c1 + SSM_CONV_DIM + SSM_HEADS], \
        win[:, c1 + SSM_CONV_DIM + SSM_HEADS:]
    wfd = jnp.concatenate([wdt, wf, jnp.zeros((d, LANES - SSM_HEADS - ATT_HEADS), BF16)], axis=1)
    wts = dict(
        qkv=win[:, :c0], z=wz, xbc=wxbc, gate=wgate, fd=wfd,
        pa=full["w_proj_attn"].reshape(d, d), ps=full["w_proj_ssm"].reshape(SSM_INNER, d),
        out=full["w_out"].reshape(d, d),
        gu=jnp.concatenate([_merge_cols(full["w_ffn_gate"]), _merge_cols(full["w_ffn_up"])], axis=1),
        down=full["w_ffn_down"].reshape(f, d),
        conv=_merge_cols(gc.reshape(N_DEV, SSM_CONV, -1)))

    loss_part, grad_x, g, small_g = _local_step(x[0], loss_target[0], wts, small_w)
    loss = lax.psum(loss_part, ("x", "y", "c"))

    gfd = g["fd"]
    gin = jnp.concatenate([g["q"], g["k"], g["v"], gfd[:, F_LANE0:F_LANE0 + ATT_HEADS], g["z"], g["xbc"],
                           gfd[:, DT_LANE0:DT_LANE0 + SSM_HEADS], g["gate"]], axis=1)
    by_dest = dict(
        w_in=_split_cols(gin), w_proj_attn=g["pa"].reshape(N_DEV, -1), w_proj_ssm=g["ps"].reshape(N_DEV, -1),
        w_out=g["out"].reshape(N_DEV, -1), w_ffn_gate=_split_cols(g["gu"][:, :f]),
        w_ffn_up=_split_cols(g["gu"][:, f:]), w_ffn_down=g["down"].reshape(N_DEV, -1),
        conv_w=_split_cols(g["conv"]))
    sflat = jnp.concatenate([small_g[n].reshape(-1) for n in SMALL])
    gflat = jnp.concatenate([by_dest[n].reshape(N_DEV, -1) for n in BIG]
                            + [jnp.broadcast_to(sflat, (N_DEV, sflat.shape[0]))], axis=1)
    gpack = jnp.pad(gflat, ((0, 0), (0, PACK_ROWS * LANES - gflat.shape[1]))).reshape(N_DEV, PACK_ROWS, LANES)
    recv = _exchange(gpack)

    wp = _pack(big_w, small_w)
    mp = _pack({n: args["m_" + n][0] for n in BIG}, {n: args["m_" + n][0] for n in SMALL})
    vp = _pack({n: args["v_" + n][0] for n in BIG}, {n: args["v_" + n][0] for n in SMALL})
    outs = [_unpack(p, big_shapes, small_shapes) for p in _adamw(recv, wp, mp, vp)]

    order = ("w_in", "b_forget", "conv_w", "conv_b", "dt_bias", "a_log", "d_skip", "ssm_norm_w", "w_proj_attn",
             "w_proj_ssm", "b_gates", "w_out", "ln1_g", "ln1_b", "w_ffn_gate", "w_ffn_up", "w_ffn_down", "ln2_g",
             "ln2_b")
    res = [loss, grad_x[None]]
    for o in outs:
        res += [o[n] for n in order]
    return tuple(res)
```

```python
import functools
import math

import jax
import jax.numpy as jnp
from jax import lax
from jax.experimental import pallas as pl
from jax.experimental.pallas import tpu as pltpu

F32 = jnp.float32
BF16 = jnp.bfloat16

N_DEV = 8
D_MODEL = 1024
ATT_HEADS = 16
ATT_HEAD_DIM = 64
SSM_INNER = 2048
SSM_HEADS = 32
SSM_HEAD_DIM = 64
SSM_GROUPS = 4
SSM_HEADS_PER_GROUP = 8
SSM_STATE = 128
SSM_CONV = 4
SSM_CHUNK = 128
SSM_CONV_DIM = 3072
FFN_HIDDEN = 2816
IN_WIDTH = 10288
DEEPNORM_ALPHA = 2.0 ** 0.25
LN_EPS = 1e-5
RMS_EPS = 1e-5
ADAM_LR, ADAM_B1, ADAM_B2, ADAM_EPS, ADAM_WD, ADAM_STEP = 0.001, 0.9, 0.999, 1e-08, 0.01, 10
ATT_SCALE = 1.0 / math.sqrt(ATT_HEAD_DIM)

LANES = 128
VMEM_LIMIT = 56 * 1024 * 1024
NEG = -1e30

DT_LANE0 = 0
F_LANE0 = 32
HI = lax.Precision.HIGHEST


def _cp(*sem):
    return pltpu.CompilerParams(dimension_semantics=sem, vmem_limit_bytes=VMEM_LIMIT)


def _tile(n, cap=1408):
    for t in (1408, 1024, 512, 384, 256, 128):
        if t <= cap and n % t == 0:
            return t
    return n


def _sigmoid(x):
    return 1.0 / (1.0 + jnp.exp(-x))


def _mm(a, b, *, ta=False, tb=False, out_dtype=F32, add=None, name):
    m, k = (a.shape[1], a.shape[0]) if ta else a.shape
    n = b.shape[0] if tb else b.shape[1]
    assert (b.shape[1] if tb else b.shape[0]) == k
    tm, tn, tk = _tile(m), _tile(n), _tile(k)
    nk = k // tk
    dims = (((0,) if ta else (1,), (1,) if tb else (0,)), ((), ()))

    def body(*refs):
        if add is None:
            a_ref, b_ref, o_ref, acc_ref = refs
        else:
            a_ref, b_ref, c_ref, o_ref, acc_ref = refs
        kk = pl.program_id(2)

        @pl.when(kk == 0)
        def _():
            acc_ref[...] = jnp.zeros_like(acc_ref)

        acc_ref[...] += lax.dot_general(a_ref[...].astype(BF16), b_ref[...].astype(BF16), dims,
                                        preferred_element_type=F32)

        @pl.when(kk == nk - 1)
        def _():
            r = acc_ref[...]
            if add is not None:
                r = r + c_ref[...]
            o_ref[...] = r.astype(o_ref.dtype)

    a_spec = pl.BlockSpec((tk, tm), lambda i, j, kk: (kk, i)) if ta else pl.BlockSpec((tm, tk), lambda i, j, kk: (i, kk))
    b_spec = pl.BlockSpec((tn, tk), lambda i, j, kk: (j, kk)) if tb else pl.BlockSpec((tk, tn), lambda i, j, kk: (kk, j))
    o_spec = pl.BlockSpec((tm, tn), lambda i, j, kk: (i, j))
    in_specs, args = [a_spec, b_spec], [a, b]
    if add is not None:
        in_specs.append(o_spec)
        args.append(add)
    return pl.pallas_call(
        body, name=name, grid=(m // tm, n // tn, nk), in_specs=in_specs, out_specs=o_spec,
        out_shape=jax.ShapeDtypeStruct((m, n), out_dtype),
        scratch_shapes=[pltpu.VMEM((tm, tn), F32)],
        compiler_params=_cp("parallel", "parallel", "arbitrary"),
    )(*args)


def _tri(n, lower=True):
    r = lax.broadcasted_iota(jnp.int32, (n, n), 0)
    c = lax.broadcasted_iota(jnp.int32, (n, n), 1)
    return jnp.where((r >= c) if lower else (c >= r), 1.0, 0.0).astype(F32)


def _stats_fwd(fd, bias_row, a_row):
    s = fd.shape[0]
    blk = SSM_CHUNK

    def body(fd_ref, bias_ref, a_ref, dt_ref, ac_ref, cf_ref, dtr_ref, acr_ref, cfr_ref, carry_ref):
        @pl.when(pl.program_id(0) == 0)
        def _():
            carry_ref[...] = jnp.zeros_like(carry_ref)

        v = fd_ref[...] + bias_ref[...]
        dt = jnp.maximum(v, 0.0) + jnp.log(1.0 + jnp.exp(-jnp.abs(v)))
        lf = jnp.minimum(v, 0.0) - jnp.log(1.0 + jnp.exp(-jnp.abs(v)))
        tri = _tri(blk)
        ac = jnp.dot(tri, dt * a_ref[...], precision=HI, preferred_element_type=F32)
        cf = jnp.dot(tri, lf, precision=HI, preferred_element_type=F32) + carry_ref[0:1, :]
        carry_ref[...] = carry_ref[...] + jnp.sum(lf, axis=0, keepdims=True)
        dt_ref[...] = dt
        ac_ref[...] = ac
        cf_ref[...] = cf
        dtr_ref[...] = dt.T
        acr_ref[...] = ac.T
        cfr_ref[...] = cf.T

    col = pl.BlockSpec((blk, LANES), lambda i: (i, 0))
    row = pl.BlockSpec((LANES, blk), lambda i: (0, i))
    vec = pl.BlockSpec((1, LANES), lambda i: (0, 0))
    return pl.pallas_call(
        body, name="stats_fwd", grid=(s // blk,), in_specs=[col, vec, vec],
        out_specs=[col, col, col, row, row, row],
        out_shape=[jax.ShapeDtypeStruct((s, LANES), F32)] * 3 + [jax.ShapeDtypeStruct((LANES, s), F32)] * 3,
        scratch_shapes=[pltpu.VMEM((8, LANES), F32)],
        compiler_params=_cp("arbitrary"),
    )(fd, bias_row, a_row)


def _stats_bwd(fd, bias_row, ddt, dck_rows, dcq_cols):
    s = fd.shape[0]
    blk = SSM_CHUNK
    nb = s // blk

    def body(fd_ref, bias_ref, ddt_ref, dck_ref, dcq_ref, o_ref, db_ref, carry_ref):
        @pl.when(pl.program_id(0) == 0)
        def _():
            carry_ref[...] = jnp.zeros_like(carry_ref)
            db_ref[...] = jnp.zeros_like(db_ref)

        v = fd_ref[...] + bias_ref[...]
        dcum = dck_ref[...].T + dcq_ref[...]
        dlf = jnp.dot(_tri(blk, lower=False), dcum, precision=HI, preferred_element_type=F32) + carry_ref[0:1, :]
        carry_ref[...] = carry_ref[...] + jnp.sum(dcum, axis=0, keepdims=True)
        lane = lax.broadcasted_iota(jnp.int32, v.shape, 1)
        g = jnp.where(lane < F_LANE0, ddt_ref[...] * _sigmoid(v), dlf * _sigmoid(-v))
        g = jnp.where(lane < F_LANE0 + ATT_HEADS, g, 0.0)
        o_ref[...] = g.astype(o_ref.dtype)
        db_ref[...] += jnp.sum(g, axis=0, keepdims=True)

    col = pl.BlockSpec((blk, LANES), lambda i: (nb - 1 - i, 0))
    row = pl.BlockSpec((LANES, blk), lambda i: (0, nb - 1 - i))
    vec = pl.BlockSpec((1, LANES), lambda i: (0, 0))
    return pl.pallas_call(
        body, name="stats_bwd", grid=(nb,), in_specs=[col, vec, col, row, col], out_specs=[col, vec],
        out_shape=[jax.ShapeDtypeStruct((s, LANES), BF16), jax.ShapeDtypeStruct((1, LANES), F32)],
        scratch_shapes=[pltpu.VMEM((8, LANES), F32)],
        compiler_params=_cp("arbitrary"),
    )(fd, bias_row, ddt, dck_rows, dcq_cols)


HP = LANES // ATT_HEAD_DIM
N_HP = ATT_HEADS // HP


def _att_blocks(s):
    return (256, 512) if s % 512 == 0 and s >= 2048 else (64, 128)


_QK = (((1,), (1,)), ((), ()))
_HALF = ATT_HEAD_DIM // 2


def _head_cols(a):
    return slice(a * ATT_HEAD_DIM, (a + 1) * ATT_HEAD_DIM)


def _causal(shape, off):
    r = lax.broadcasted_iota(jnp.int32, shape, 0)
    c = lax.broadcasted_iota(jnp.int32, shape, 1)
    return c <= r + off


def _attention_fwd(qkv, ck4):
    s = qkv.shape[0]
    bq, bk = _att_blocks(s)
    nq, nk = s // bq, s // bk

    def body(q_ref, k_ref, v_ref, ck_ref, o_ref, lse_ref):
        i = pl.program_id(1)
        n_full = (i * bq) // bk
        qs = [(q_ref[:, _head_cols(a)].astype(F32) * ATT_SCALE).astype(BF16) for a in range(HP)]

        def step(j, carry, off=None):
            ks = pl.ds(pl.multiple_of(j * bk, bk), bk)
            out = []
            for a in range(HP):
                m, l, acc = carry[a]
                sc = lax.dot_general(qs[a], k_ref[ks, _head_cols(a)], _QK, preferred_element_type=F32)
                sc = sc - ck_ref[0, a, pl.ds(j, 1), :]
                if off is not None:
                    sc = jnp.where(_causal(sc.shape, off), sc, NEG)
                m_new = jnp.maximum(m, jnp.max(sc, axis=1, keepdims=True))
                alpha = jnp.exp(m - m_new)
                p = jnp.exp(sc - m_new)
                l = alpha * l + jnp.sum(p, axis=1, keepdims=True)
                acc = alpha * acc + jnp.dot(p.astype(BF16), v_ref[ks, _head_cols(a)], preferred_element_type=F32)
                out.append((m_new, l, acc))
            return tuple(out)

        init = tuple((jnp.full((bq, 1), NEG, F32), jnp.zeros((bq, 1), F32), jnp.zeros((bq, ATT_HEAD_DIM), F32))
                     for _ in range(HP))
        carry = lax.fori_loop(0, n_full, step, init)
        carry = step(n_full, carry, off=i * bq - n_full * bk)
        for a in range(HP):
            m, l, acc = carry[a]
            o_ref[:, _head_cols(a)] = acc / l
            lse_ref[:, _head_cols(a)] = jnp.broadcast_to(m + jnp.log(l), (bq, ATT_HEAD_DIM))

    q_spec = pl.BlockSpec((bq, LANES), lambda h, i: (i, h))
    return pl.pallas_call(
        body, name="att_fwd", grid=(N_HP, nq),
        in_specs=[q_spec, pl.BlockSpec((s, LANES), lambda h, i: (0, N_HP + h)),
                  pl.BlockSpec((s, LANES), lambda h, i: (0, 2 * N_HP + h)),
                  pl.BlockSpec((1, HP, nk, bk), lambda h, i: (h, 0, 0, 0))],
        out_specs=[q_spec, q_spec], out_shape=[jax.ShapeDtypeStruct((s, D_MODEL), F32)] * 2,
        compiler_params=_cp("parallel", "arbitrary"),
    )(qkv, qkv, qkv, ck4)


def _att_prep(do, o, lse_rep):
    s = do.shape[0]
    bs = _tile(s, 512)

    def body(do_ref, o_ref, lse_ref, st_ref, dob_ref):
        r = lax.broadcasted_iota(jnp.int32, (LANES, LANES), 0) // ATT_HEAD_DIM
        c = lax.broadcasted_iota(jnp.int32, (LANES, LANES), 1) // ATT_HEAD_DIM
        e = jnp.where(r == c, 1.0, 0.0).astype(F32)
        lane = lax.broadcasted_iota(jnp.int32, (bs, LANES), 1)
        for p in range(D_MODEL // LANES):
            cs = slice(p * LANES, (p + 1) * LANES)
            dd = do_ref[:, cs]
            delta = jnp.dot(dd * o_ref[:, cs], e, precision=HI, preferred_element_type=F32)
            st_ref[:, cs] = jnp.where(lane % ATT_HEAD_DIM < _HALF, lse_ref[:, cs], delta)
            dob_ref[:, cs] = dd.astype(BF16)

    spec = pl.BlockSpec((bs, D_MODEL), lambda i: (i, 0))
    return pl.pallas_call(body, name="att_prep", grid=(s // bs,), in_specs=[spec, spec, spec], out_specs=[spec, spec],
                          out_shape=[jax.ShapeDtypeStruct((s, D_MODEL), F32), jax.ShapeDtypeStruct((s, D_MODEL), BF16)],
                          compiler_params=_cp("parallel"))(do, o, lse_rep)


def _attention_bwd(qkv, ck4, st, do_b):
    s = qkv.shape[0]
    bq, bk = _att_blocks(s)
    nq, nk, per = s // bq, s // bk, bk // bq
    _T = (((0,), (0,)), ((), ()))

    def body(q_ref, k_ref, v_ref, ck_ref, st_ref, do_ref, dq_ref, dk_ref, dv_ref, dck_ref, dcq_ref,
             dk_acc, dv_acc, dck_acc):
        j = pl.program_id(1)

        @pl.when(j == 0)
        def _():
            dq_ref[...] = jnp.zeros_like(dq_ref)
            dcq_ref[...] = jnp.zeros_like(dcq_ref)

        dk_acc[...] = jnp.zeros_like(dk_acc)
        dv_acc[...] = jnp.zeros_like(dv_acc)
        dck_acc[...] = jnp.zeros_like(dck_acc)

        def step(i, off=None):
            rows = pl.ds(pl.multiple_of(i * bq, bq), bq)
            for a in range(HP):
                cs = _head_cols(a)
                q = (q_ref[rows, cs].astype(F32) * ATT_SCALE).astype(BF16)
                k = k_ref[:, cs]
                do_a = do_ref[rows, cs]
                sc = lax.dot_general(q, k, _QK, preferred_element_type=F32) - ck_ref[0, a, pl.ds(j, 1), :]
                if off is not None:
                    sc = jnp.where(_causal(sc.shape, off), sc, NEG)
                p = jnp.exp(sc - st_ref[rows, a * ATT_HEAD_DIM:a * ATT_HEAD_DIM + 1])
                dp = lax.dot_general(do_a, v_ref[:, cs], _QK, preferred_element_type=F32)
                ds = p * (dp - st_ref[rows, a * ATT_HEAD_DIM + _HALF:a * ATT_HEAD_DIM + _HALF + 1])
                ds_b = ds.astype(BF16)
                dv_acc[a] += lax.dot_general(p.astype(BF16), do_a, _T, preferred_element_type=F32)
                dk_acc[a] += lax.dot_general(ds_b, q, _T, preferred_element_type=F32)
                dq_ref[rows, cs] += jnp.dot(ds_b, k, preferred_element_type=F32) * ATT_SCALE
                dck_acc[a] -= jnp.sum(ds, axis=0, keepdims=True)
                dcq_ref[rows, cs] += jnp.broadcast_to(jnp.sum(ds, axis=1, keepdims=True), (bq, ATT_HEAD_DIM))

        for t in range(per):
            step(j * per + t, off=t * bq)

        def full(i, c):
            step(i)
            return c

        lax.fori_loop((j + 1) * per, nq, full, 0)
        for a in range(HP):
            dk_ref[:, _head_cols(a)] = dk_acc[a].astype(dk_ref.dtype)
            dv_ref[:, _head_cols(a)] = dv_acc[a].astype(dv_ref.dtype)
            dck_ref[0, a, pl.ds(j, 1), :] = dck_acc[a]

    res = pl.BlockSpec((s, LANES), lambda h, j: (0, h))
    ck_spec = pl.BlockSpec((1, HP, nk, bk), lambda h, j: (h, 0, 0, 0))
    kout = pl.BlockSpec((bk, LANES), lambda h, j: (j, h))
    return pl.pallas_call(
        body, name="att_bwd", grid=(N_HP, nk),
        in_specs=[res, pl.BlockSpec((bk, LANES), lambda h, j: (j, N_HP + h)),
                  pl.BlockSpec((bk, LANES), lambda h, j: (j, 2 * N_HP + h)), ck_spec, res, res],
        out_specs=[res, kout, kout, ck_spec, res],
        out_shape=[jax.ShapeDtypeStruct((s, D_MODEL), F32), jax.ShapeDtypeStruct((s, D_MODEL), BF16),
                   jax.ShapeDtypeStruct((s, D_MODEL), BF16), jax.ShapeDtypeStruct((N_HP, HP, nk, bk), F32),
                   jax.ShapeDtypeStruct((s, D_MODEL), F32)],
        scratch_shapes=[pltpu.VMEM((HP, bk, ATT_HEAD_DIM), F32), pltpu.VMEM((HP, bk, ATT_HEAD_DIM), F32),
                        pltpu.VMEM((HP, 1, bk), F32)],
        compiler_params=_cp("parallel", "arbitrary"),
    )(qkv, qkv, qkv, ck4, st, do_b)


def _silu_and_grad(x):
    sg = _sigmoid(x)
    return x * sg, sg * (1.0 + x * (1.0 - sg))


def _conv_pre(cur, halo, w_ref, b_ref, first):
    halo = jnp.where(first, 0.0, halo)
    row = lax.broadcasted_iota(jnp.int32, cur.shape, 0)
    shifted = []
    for k in range(SSM_CONV):
        sh = SSM_CONV - 1 - k
        if sh == 0:
            shifted.append(cur)
            continue
        r = pltpu.roll(cur, sh, 0)
        hr = pltpu.roll(halo, sh, 0)
        top = jnp.where(row[0:8] < sh, hr, r[0:8])
        shifted.append(jnp.concatenate([top, r[8:]], axis=0))
    pre = b_ref[...] + sum(w_ref[k:k + 1, :] * shifted[k] for k in range(SSM_CONV))
    return pre, shifted


def _conv_specs(s, bs, bc):
    cur = pl.BlockSpec((bs, bc), lambda j, i: (i, j))
    halo = pl.BlockSpec((8, bc), lambda j, i: (jnp.maximum(i * (bs // 8) - 1, 0), j))
    w = pl.BlockSpec((SSM_CONV, bc), lambda j, i: (0, j))
    b = pl.BlockSpec((1, bc), lambda j, i: (0, j))
    return cur, halo, w, b


def _conv_fwd(xbc, w, b):
    s, c = xbc.shape
    bs, bc = _tile(s, 512), 1024

    def body(x_ref, h_ref, w_ref, b_ref, o_ref):
        pre, _ = _conv_pre(x_ref[...], h_ref[...], w_ref, b_ref, pl.program_id(1) == 0)
        o_ref[...] = pre * _sigmoid(pre)

    cur, halo, ws, bsp = _conv_specs(s, bs, bc)
    return pl.pallas_call(body, name="conv_fwd", grid=(c // bc, s // bs), in_specs=[cur, halo, ws, bsp],
                          out_specs=cur, out_shape=jax.ShapeDtypeStruct((s, c), F32),
                          compiler_params=_cp("parallel", "parallel"))(xbc, xbc, w, b)


def _conv_bwd_pre(xbc, w, b, dact):
    s, c = xbc.shape
    bs, bc = _tile(s, 512), 1024

    def body(x_ref, h_ref, w_ref, b_ref, g_ref, dp_ref, dw_ref, db_ref):
        @pl.when(pl.program_id(1) == 0)
        def _():
            dw_ref[...] = jnp.zeros_like(dw_ref)
            db_ref[...] = jnp.zeros_like(db_ref)

        pre, shifted = _conv_pre(x_ref[...], h_ref[...], w_ref, b_ref, pl.program_id(1) == 0)
        dpre = g_ref[...] * _silu_and_grad(pre)[1]
        dp_ref[...] = dpre
        db_ref[...] += jnp.sum(dpre, axis=0, keepdims=True)
        for k in range(SSM_CONV):
            dw_ref[k:k + 1, :] += jnp.sum(dpre * shifted[k], axis=0, keepdims=True)

    cur, halo, ws, bsp = _conv_specs(s, bs, bc)
    return pl.pallas_call(
        body, name="conv_bwd_pre", grid=(c // bc, s // bs), in_specs=[cur, halo, ws, bsp, cur],
        out_specs=[cur, ws, bsp],
        out_shape=[jax.ShapeDtypeStruct((s, c), F32), jax.ShapeDtypeStruct((SSM_CONV, c), F32),
                   jax.ShapeDtypeStruct((1, c), F32)],
        compiler_params=_cp("parallel", "arbitrary"))(xbc, xbc, w, b, dact)


def _conv_bwd_in(dpre, w):
    s, c = dpre.shape
    bs, bc = _tile(s, 512), 1024
    nb = s // bs

    def body(g_ref, n_ref, w_ref, o_ref):
        cur = g_ref[...]
        nxt = jnp.where(pl.program_id(1) == nb - 1, 0.0, n_ref[...])
        row = lax.broadcasted_iota(jnp.int32, cur.shape, 0)
        acc = w_ref[SSM_CONV - 1:SSM_CONV, :] * cur
        for sh in range(1, SSM_CONV):
            r = pltpu.roll(cur, bs - sh, 0)
            nr = pltpu.roll(nxt, 8 - sh, 0)
            bot = jnp.where(row[0:8] >= 8 - sh, nr, r[bs - 8:])
            acc = acc + w_ref[SSM_CONV - 1 - sh:SSM_CONV - sh, :] * jnp.concatenate([r[:bs - 8], bot], axis=0)
        o_ref[...] = acc.astype(o_ref.dtype)

    cur = pl.BlockSpec((bs, bc), lambda j, i: (i, j))
    nxt = pl.BlockSpec((8, bc), lambda j, i: (jnp.minimum((i + 1) * (bs // 8), s // 8 - 1), j))
    ws = pl.BlockSpec((SSM_CONV, bc), lambda j, i: (0, j))
    return pl.pallas_call(body, name="conv_bwd_in", grid=(c // bc, nb), in_specs=[cur, nxt, ws], out_specs=cur,
                          out_shape=jax.ShapeDtypeStruct((s, c), BF16),
                          compiler_params=_cp("parallel", "parallel"))(dpre, dpre, w)


def _dotT(a, b):
    return lax.dot_general(a.astype(BF16), b.astype(BF16), (((1,), (1,)), ((), ())), preferred_element_type=F32)


def _Tdot(a, b):
    return lax.dot_general(a.astype(BF16), b.astype(BF16), (((0,), (0,)), ((), ())), preferred_element_type=F32)


def _dot(a, b):
    return jnp.dot(a.astype(BF16), b.astype(BF16), preferred_element_type=F32)


def _ssd_head(xbc_ref, dt_ref, ac_ref, acr_ref, h):
    L = SSM_CHUNK
    xs = xbc_ref[:, h * SSM_HEAD_DIM:(h + 1) * SSM_HEAD_DIM]
    dt_col = dt_ref[:, h:h + 1]
    a_col = ac_ref[:, h:h + 1]
    a_row = acr_ref[h:h + 1, :]
    li = lax.broadcasted_iota(jnp.int32, (L, L), 0)
    si = lax.broadcasted_iota(jnp.int32, (L, L), 1)
    decay = jnp.exp(jnp.where(li >= si, a_col - a_row, NEG))
    a_last = ac_ref[L - 1:L, h:h + 1]
    return xs, dt_col, a_col, a_last, decay


def _ssd_fwd(xbc_act, dt_c, ac_c, ac_r, d_skip):
    s = xbc_act.shape[0]
    L, P, N, G, R = SSM_CHUNK, SSM_HEAD_DIM, SSM_STATE, SSM_GROUPS, SSM_HEADS_PER_GROUP
    nc = s // L

    def body(dsk_ref, xbc_ref, dt_ref, ac_ref, acr_ref, y_ref, hp_ref, st_ref):
        @pl.when(pl.program_id(0) == 0)
        def _():
            st_ref[...] = jnp.zeros_like(st_ref)

        for g in range(G):
            b_g = xbc_ref[:, SSM_INNER + g * N:SSM_INNER + (g + 1) * N]
            c_g = xbc_ref[:, SSM_INNER + G * N + g * N:SSM_INNER + G * N + (g + 1) * N]
            cb = _dotT(c_g, b_g)
            for r in range(R):
                h = g * R + r
                xs, dt_col, a_col, a_last, decay = _ssd_head(xbc_ref, dt_ref, ac_ref, acr_ref, h)
                xdt = xs * dt_col
                hprev = st_ref[h]
                y = _dot(cb * decay, xdt) + jnp.exp(a_col) * _dotT(c_g, hprev) + dsk_ref[h] * xs
                y_ref[:, h * P:(h + 1) * P] = y
                hp_ref[0, h] = hprev
                st_ref[h] = hprev * jnp.exp(a_last) + _Tdot(xdt * jnp.exp(a_last - a_col), b_g)

    col = pl.BlockSpec((L, LANES), lambda c: (c, 0))
    return pl.pallas_call(
        body, name="ssd_fwd", grid=(nc,),
        in_specs=[pl.BlockSpec(memory_space=pltpu.SMEM), pl.BlockSpec((L, SSM_CONV_DIM), lambda c: (c, 0)), col, col,
                  pl.BlockSpec((LANES, L), lambda c: (0, c))],
        out_specs=[pl.BlockSpec((L, SSM_INNER), lambda c: (c, 0)),
                   pl.BlockSpec((1, SSM_HEADS, P, N), lambda c: (c, 0, 0, 0))],
        out_shape=[jax.ShapeDtypeStruct((s, SSM_INNER), F32), jax.ShapeDtypeStruct((nc, SSM_HEADS, P, N), F32)],
        scratch_shapes=[pltpu.VMEM((SSM_HEADS, P, N), F32)],
        compiler_params=_cp("arbitrary"),
    )(d_skip, xbc_act, dt_c, ac_c, ac_r)


def _ssd_bwd(xbc_act, dt_c, ac_c, ac_r, hprev_all, dy, d_skip, a_row):
    s = xbc_act.shape[0]
    L, P, N, G, R = SSM_CHUNK, SSM_HEAD_DIM, SSM_STATE, SSM_GROUPS, SSM_HEADS_PER_GROUP
    nc = s // L

    def body(dsk_ref, xbc_ref, dt_ref, ac_ref, acr_ref, hp_ref, dy_ref, arow_ref,
             dx_ref, ddt_ref, da_ref, dds_ref, dh_ref):
        @pl.when(pl.program_id(0) == 0)
        def _():
            dh_ref[...] = jnp.zeros_like(dh_ref)
            da_ref[...] = jnp.zeros_like(da_ref)
            dds_ref[...] = jnp.zeros_like(dds_ref)

        lane = lax.broadcasted_iota(jnp.int32, (L, LANES), 1)
        sub = lax.broadcasted_iota(jnp.int32, (LANES, L), 0)
        rowi = lax.broadcasted_iota(jnp.int32, (L, 1), 0)
        lane1 = lax.broadcasted_iota(jnp.int32, (1, LANES), 1)
        da_c = jnp.zeros((L, LANES), F32)
        da_r = jnp.zeros((LANES, L), F32)
        ddt1 = jnp.zeros((L, LANES), F32)
        dds = jnp.zeros((1, LANES), F32)
        for g in range(G):
            b_g = xbc_ref[:, SSM_INNER + g * N:SSM_INNER + (g + 1) * N]
            c_g = xbc_ref[:, SSM_INNER + G * N + g * N:SSM_INNER + G * N + (g + 1) * N]
            cb = _dotT(c_g, b_g)
            dcb = jnp.zeros((L, L), F32)
            db_g = jnp.zeros((L, N), F32)
            dc_g = jnp.zeros((L, N), F32)
            for r in range(R):
                h = g * R + r
                xs, dt_col, a_col, a_last, decay = _ssd_head(xbc_ref, dt_ref, ac_ref, acr_ref, h)
                gy = dy_ref[:, h * P:(h + 1) * P]
                xdt = xs * dt_col
                hprev = hp_ref[0, h]
                dhn = dh_ref[h]
                e_a = jnp.exp(a_col)
                e_last = jnp.exp(a_last)
                e_col = jnp.exp(a_last - a_col)
                m = cb * decay
                yoff = e_a * _dotT(c_g, hprev)
                da_col = jnp.sum(gy * yoff, axis=1, keepdims=True)
                dc_g = dc_g + e_a * _dot(gy, hprev)
                dhp = _Tdot(gy * e_a, c_g) + dhn * e_last
                da_last = jnp.sum(jnp.sum(dhn * hprev, axis=1, keepdims=True), axis=0, keepdims=True) * e_last
                xds = _dot(xdt, dhn)
                db_g = db_g + e_col * xds
                de_e = jnp.sum(xds * b_g, axis=1, keepdims=True) * e_col
                da_col = da_col - de_e
                da_last = da_last + jnp.sum(de_e, axis=0, keepdims=True)
                dxdt = e_col * _dotT(b_g, dhn)
                dm = _dotT(gy, xdt)
                dxdt = dxdt + _Tdot(m, gy)
                dcb = dcb + dm * decay
                w = dm * m
                da_col = da_col + jnp.sum(w, axis=1, keepdims=True) + jnp.where(rowi == L - 1, da_last, 0.0)
                da_c = jnp.where(lane == h, da_col, da_c)
                da_r = jnp.where(sub == h, jnp.sum(w, axis=0, keepdims=True), da_r)
                ddt1 = jnp.where(lane == h, jnp.sum(dxdt * xs, axis=1, keepdims=True), ddt1)
                dds = jnp.where(lane1 == h, jnp.sum(jnp.sum(gy * xs, axis=1, keepdims=True), axis=0, keepdims=True),
                                dds)
                dx_ref[:, h * P:(h + 1) * P] = dxdt * dt_col + dsk_ref[h] * gy
                dh_ref[h] = dhp
            dx_ref[:, SSM_INNER + g * N:SSM_INNER + (g + 1) * N] = db_g + _Tdot(dcb, c_g)
            dx_ref[:, SSM_INNER + G * N + g * N:SSM_INNER + G * N + (g + 1) * N] = dc_g + _dot(dcb, b_g)
        dda = jnp.dot(_tri(L, lower=False), da_c - da_r.T, precision=HI, preferred_element_type=F32)
        ddt_ref[...] = dda * arow_ref[...] + ddt1
        da_ref[...] += jnp.sum(dda * dt_ref[...], axis=0, keepdims=True)
        dds_ref[...] += dds

    col = pl.BlockSpec((L, LANES), lambda c: (nc - 1 - c, 0))
    vec = pl.BlockSpec((1, LANES), lambda c: (0, 0))
    return pl.pallas_call(
        body, name="ssd_bwd", grid=(nc,),
        in_specs=[pl.BlockSpec(memory_space=pltpu.SMEM), pl.BlockSpec((L, SSM_CONV_DIM), lambda c: (nc - 1 - c, 0)),
                  col, col, pl.BlockSpec((LANES, L), lambda c: (0, nc - 1 - c)),
                  pl.BlockSpec((1, SSM_HEADS, P, N), lambda c: (nc - 1 - c, 0, 0, 0)),
                  pl.BlockSpec((L, SSM_INNER), lambda c: (nc - 1 - c, 0)), vec],
        out_specs=[pl.BlockSpec((L, SSM_CONV_DIM), lambda c: (nc - 1 - c, 0)), col, vec, vec],
        out_shape=[jax.ShapeDtypeStruct((s, SSM_CONV_DIM), F32), jax.ShapeDtypeStruct((s, LANES), F32),
                   jax.ShapeDtypeStruct((1, LANES), F32), jax.ShapeDtypeStruct((1, LANES), F32)],
        scratch_shapes=[pltpu.VMEM((SSM_HEADS, P, N), F32)],
        compiler_params=_cp("arbitrary"),
    )(d_skip, xbc_act, dt_c, ac_c, ac_r, hprev_all, dy, a_row)


ROWS = 256
GW = SSM_INNER // SSM_GROUPS


def _rows(width, dtype=F32):
    return pl.BlockSpec((ROWS, width), lambda i: (i, 0))


def _vec(width):
    return pl.BlockSpec((1, width), lambda i: (0, 0))


def _gnorm_fwd(y, z, w):
    s = y.shape[0]

    def body(y_ref, z_ref, w_ref, o_ref):
        for g in range(SSM_GROUPS):
            cs = slice(g * GW, (g + 1) * GW)
            zz = z_ref[:, cs]
            u = y_ref[:, cs] * (zz * _sigmoid(zz))
            r = lax.rsqrt(jnp.mean(u * u, axis=1, keepdims=True) + RMS_EPS)
            o_ref[:, cs] = (u * r * w_ref[:, cs]).astype(o_ref.dtype)

    return pl.pallas_call(body, name="gnorm_fwd", grid=(s // ROWS,),
                          in_specs=[_rows(SSM_INNER), _rows(SSM_INNER), _vec(SSM_INNER)], out_specs=_rows(SSM_INNER),
                          out_shape=jax.ShapeDtypeStruct((s, SSM_INNER), BF16), compiler_params=_cp("parallel"))(y, z, w)


def _gnorm_bwd(y, z, w, do):
    s = y.shape[0]

    def body(y_ref, z_ref, w_ref, do_ref, dy_ref, dz_ref, dw_ref):
        @pl.when(pl.program_id(0) == 0)
        def _():
            dw_ref[...] = jnp.zeros_like(dw_ref)

        for g in range(SSM_GROUPS):
            cs = slice(g * GW, (g + 1) * GW)
            zz, yy, dd = z_ref[:, cs], y_ref[:, cs], do_ref[:, cs]
            sz, dsz = _silu_and_grad(zz)
            u = yy * sz
            r = lax.rsqrt(jnp.mean(u * u, axis=1, keepdims=True) + RMS_EPS)
            n = u * r
            dn = dd * w_ref[:, cs]
            dw_ref[:, cs] += jnp.sum(dd * n, axis=0, keepdims=True)
            du = r * (dn - n * jnp.mean(dn * n, axis=1, keepdims=True))
            dy_ref[:, cs] = du * sz
            dz_ref[:, cs] = (du * yy * dsz).astype(dz_ref.dtype)

    return pl.pallas_call(
        body, name="gnorm_bwd", grid=(s // ROWS,),
        in_specs=[_rows(SSM_INNER), _rows(SSM_INNER), _vec(SSM_INNER), _rows(SSM_INNER)],
        out_specs=[_rows(SSM_INNER), _rows(SSM_INNER), _vec(SSM_INNER)],
        out_shape=[jax.ShapeDtypeStruct((s, SSM_INNER), F32), jax.ShapeDtypeStruct((s, SSM_INNER), BF16),
                   jax.ShapeDtypeStruct((1, SSM_INNER), F32)],
        compiler_params=_cp("arbitrary"))(y, z, w, do)


def _mix_fwd(gl, bg, attn_d, ssm_d):
    s = gl.shape[0]
    d = D_MODEL

    def body(gl_ref, bg_ref, a_ref, m_ref, o_ref):
        g0 = _sigmoid(gl_ref[:, :d] + bg_ref[:, :d])
        g1 = _sigmoid(gl_ref[:, d:] + bg_ref[:, d:])
        o_ref[...] = (g0 * a_ref[...] + g1 * m_ref[...]).astype(o_ref.dtype)

    return pl.pallas_call(body, name="mix_fwd", grid=(s // ROWS,),
                          in_specs=[_rows(2 * d), _vec(2 * d), _rows(d), _rows(d)], out_specs=_rows(d),
                          out_shape=jax.ShapeDtypeStruct((s, d), BF16), compiler_params=_cp("parallel"))(
        gl, bg, attn_d, ssm_d)


def _mix_bwd(gl, bg, attn_d, ssm_d, dmix):
    s = gl.shape[0]
    d = D_MODEL

    def body(gl_ref, bg_ref, a_ref, m_ref, dm_ref, da_ref, ds_ref, dg_ref, db_ref):
        @pl.when(pl.program_id(0) == 0)
        def _():
            db_ref[...] = jnp.zeros_like(db_ref)

        g0 = _sigmoid(gl_ref[:, :d] + bg_ref[:, :d])
        g1 = _sigmoid(gl_ref[:, d:] + bg_ref[:, d:])
        dm = dm_ref[...]
        da_ref[...] = (dm * g0).astype(da_ref.dtype)
        ds_ref[...] = (dm * g1).astype(ds_ref.dtype)
        dl0 = dm * a_ref[...] * g0 * (1.0 - g0)
        dl1 = dm * m_ref[...] * g1 * (1.0 - g1)
        dg_ref[:, :d] = dl0.astype(dg_ref.dtype)
        dg_ref[:, d:] = dl1.astype(dg_ref.dtype)
        db_ref[:, :d] += jnp.sum(dl0, axis=0, keepdims=True)
        db_ref[:, d:] += jnp.sum(dl1, axis=0, keepdims=True)

    return pl.pallas_call(
        body, name="mix_bwd", grid=(s // ROWS,),
        in_specs=[_rows(2 * d), _vec(2 * d), _rows(d), _rows(d), _rows(d)],
        out_specs=[_rows(d), _rows(d), _rows(2 * d), _vec(2 * d)],
        out_shape=[jax.ShapeDtypeStruct((s, d), BF16), jax.ShapeDtypeStruct((s, d), BF16),
                   jax.ShapeDtypeStruct((s, 2 * d), BF16), jax.ShapeDtypeStruct((1, 2 * d), F32)],
        compiler_params=_cp("arbitrary"))(gl, bg, attn_d, ssm_d, dmix)


def _ln_stats(p):
    mu = jnp.mean(p, axis=1, keepdims=True)
    c = p - mu
    rstd = lax.rsqrt(jnp.mean(c * c, axis=1, keepdims=True) + LN_EPS)
    return c * rstd, rstd


def _ln_bwd(dy, xhat, rstd, g):
    dxh = dy * g
    return rstd * (dxh - jnp.mean(dxh, axis=1, keepdims=True) - xhat * jnp.mean(dxh * xhat, axis=1, keepdims=True))


def _ln1_fwd(x, mixed, g, b):
    s, d = x.shape

    def body(x_ref, m_ref, g_ref, b_ref, o_ref):
        xhat, _ = _ln_stats(DEEPNORM_ALPHA * x_ref[...] + m_ref[...])
        o_ref[...] = xhat * g_ref[...] + b_ref[...]

    return pl.pallas_call(body, name="ln1_fwd", grid=(s // ROWS,), in_specs=[_rows(d), _rows(d), _vec(d), _vec(d)],
                          out_specs=_rows(d), out_shape=jax.ShapeDtypeStruct((s, d), F32),
                          compiler_params=_cp("parallel"))(x, mixed, g, b)


def _ln2_loss(x1, h, target, g, b):
    s, d = x1.shape

    def body(x_ref, h_ref, t_ref, g_ref, b_ref, dp_ref, loss_ref, dg_ref, db_ref):
        @pl.when(pl.program_id(0) == 0)
        def _():
            loss_ref[...] = jnp.zeros_like(loss_ref)
            dg_ref[...] = jnp.zeros_like(dg_ref)
            db_ref[...] = jnp.zeros_like(db_ref)

        xhat, rstd = _ln_stats(DEEPNORM_ALPHA * x_ref[...] + h_ref[...])
        err = xhat * g_ref[...] + b_ref[...] - t_ref[...]
        part = 0.5 * jnp.sum(jnp.mean(err * err, axis=1, keepdims=True), axis=0, keepdims=True)
        loss_ref[...] += jnp.broadcast_to(part, loss_ref.shape)
        dy = err * (1.0 / d)
        dg_ref[...] += jnp.sum(dy * xhat, axis=0, keepdims=True)
        db_ref[...] += jnp.sum(dy, axis=0, keepdims=True)
        dp_ref[...] = _ln_bwd(dy, xhat, rstd, g_ref[...])

    return pl.pallas_call(
        body, name="ln2_loss", grid=(s // ROWS,), in_specs=[_rows(d), _rows(d), _rows(d), _vec(d), _vec(d)],
        out_specs=[_rows(d), _vec(LANES), _vec(d), _vec(d)],
        out_shape=[jax.ShapeDtypeStruct((s, d), F32), jax.ShapeDtypeStruct((1, LANES), F32),
                   jax.ShapeDtypeStruct((1, d), F32), jax.ShapeDtypeStruct((1, d), F32)],
        compiler_params=_cp("arbitrary"))(x1, h, target, g, b)


def _ln1_bwd(x, mixed, g, dpre2, dffn):
    s, d = x.shape

    def body(x_ref, m_ref, g_ref, d2_ref, df_ref, dp_ref, dr_ref, dg_ref, db_ref):
        @pl.when(pl.program_id(0) == 0)
        def _():
            dg_ref[...] = jnp.zeros_like(dg_ref)
            db_ref[...] = jnp.zeros_like(db_ref)

        xhat, rstd = _ln_stats(DEEPNORM_ALPHA * x_ref[...] + m_ref[...])
        dy = DEEPNORM_ALPHA * d2_ref[...] + df_ref[...]
        dg_ref[...] += jnp.sum(dy * xhat, axis=0, keepdims=True)
        db_ref[...] += jnp.sum(dy, axis=0, keepdims=True)
        dp = _ln_bwd(dy, xhat, rstd, g_ref[...])
        dp_ref[...] = dp
        dr_ref[...] = DEEPNORM_ALPHA * dp

    return pl.pallas_call(
        body, name="ln1_bwd", grid=(s // ROWS,), in_specs=[_rows(d), _rows(d), _vec(d), _rows(d), _rows(d)],
        out_specs=[_rows(d), _rows(d), _vec(d), _vec(d)],
        out_shape=[jax.ShapeDtypeStruct((s, d), F32), jax.ShapeDtypeStruct((s, d), F32),
                   jax.ShapeDtypeStruct((1, d), F32), jax.ShapeDtypeStruct((1, d), F32)],
        compiler_params=_cp("arbitrary"))(x, mixed, g, dpre2, dffn)


def _swiglu_fwd(gu):
    s = gu.shape[0]
    f = FFN_HIDDEN

    def body(g_ref, u_ref, o_ref):
        gg = g_ref[...]
        o_ref[...] = (gg * _sigmoid(gg) * u_ref[...]).astype(o_ref.dtype)

    return pl.pallas_call(
        body, name="swiglu_fwd", grid=(s // ROWS,),
        in_specs=[pl.BlockSpec((ROWS, f), lambda i: (i, 0)), pl.BlockSpec((ROWS, f), lambda i: (i, 1))],
        out_specs=_rows(f), out_shape=jax.ShapeDtypeStruct((s, f), BF16), compiler_params=_cp("parallel"))(gu, gu)


def _swiglu_bwd(gu, dact):
    s = gu.shape[0]
    f = FFN_HIDDEN

    def body(g_ref, u_ref, d_ref, o_ref):
        sg, dsg = _silu_and_grad(g_ref[...])
        dd = d_ref[...]
        o_ref[:, :f] = (dd * u_ref[...] * dsg).astype(o_ref.dtype)
        o_ref[:, f:] = (dd * sg).astype(o_ref.dtype)

    return pl.pallas_call(
        body, name="swiglu_bwd", grid=(s // ROWS,),
        in_specs=[pl.BlockSpec((ROWS, f), lambda i: (i, 0)), pl.BlockSpec((ROWS, f), lambda i: (i, 1)), _rows(f)],
        out_specs=_rows(2 * f), out_shape=jax.ShapeDtypeStruct((s, 2 * f), BF16),
        compiler_params=_cp("parallel"))(gu, gu, dact)


def _peer(k):
    x, y, c = lax.axis_index("x"), lax.axis_index("y"), lax.axis_index("c")
    kx, ky, kc = (k >> 2) & 1, (k >> 1) & 1, k & 1
    px = (1 - x) if kx else x
    py = (1 - y) if ky else y
    pc = (1 - c) if kc else c
    return (px, py, pc), 4 * px + 2 * py + pc


def _my_index():
    return 4 * lax.axis_index("x") + 2 * lax.axis_index("y") + lax.axis_index("c")


def _all_gather(parts):
    n = len(parts)

    def body(*refs):
        ins, outs = refs[:n], refs[n:2 * n]
        send_sems, recv_sems, local_sems = refs[2 * n:]
        me = _my_index()
        local = [pltpu.make_async_copy(ins[t], outs[t].at[me], local_sems.at[t]) for t in range(n)]
        for cp in local:
            cp.start()
        remote = []
        for k in range(1, N_DEV):
            peer, _ = _peer(k)
            for t in range(n):
                remote.append(pltpu.make_async_remote_copy(
                    src_ref=ins[t], dst_ref=outs[t].at[me], send_sem=send_sems.at[t, k - 1],
                    recv_sem=recv_sems.at[t, k - 1], device_id=peer, device_id_type=pl.DeviceIdType.MESH))
        for cp in remote:
            cp.start()
        for cp in remote:
            cp.wait()
        for cp in local:
            cp.wait()

    anyspec = pl.BlockSpec(memory_space=pl.ANY)
    return pl.pallas_call(
        body, name="all_gather", in_specs=[anyspec] * n, out_specs=[anyspec] * n,
        out_shape=[jax.ShapeDtypeStruct((N_DEV,) + p.shape, p.dtype) for p in parts],
        scratch_shapes=[pltpu.SemaphoreType.DMA((n, N_DEV - 1)), pltpu.SemaphoreType.DMA((n, N_DEV - 1)),
                        pltpu.SemaphoreType.DMA((n,))],
    )(*parts)


def _exchange(parts):
    n = len(parts)

    def body(*refs):
        ins, outs = refs[:n], refs[n:2 * n]
        send_sems, recv_sems, local_sems = refs[2 * n:]
        me = _my_index()
        local = [pltpu.make_async_copy(ins[t].at[me], outs[t].at[me], local_sems.at[t]) for t in range(n)]
        for cp in local:
            cp.start()
        remote = []
        for k in range(1, N_DEV):
            peer, pidx = _peer(k)
            for t in range(n):
                remote.append(pltpu.make_async_remote_copy(
                    src_ref=ins[t].at[pidx], dst_ref=outs[t].at[me], send_sem=send_sems.at[t, k - 1],
                    recv_sem=recv_sems.at[t, k - 1], device_id=peer, device_id_type=pl.DeviceIdType.MESH))
        for cp in remote:
            cp.start()
        for cp in remote:
            cp.wait()
        for cp in local:
            cp.wait()

    anyspec = pl.BlockSpec(memory_space=pl.ANY)
    return pl.pallas_call(
        body, name="grad_exchange", in_specs=[anyspec] * n, out_specs=[anyspec] * n,
        out_shape=[jax.ShapeDtypeStruct(p.shape, p.dtype) for p in parts],
        scratch_shapes=[pltpu.SemaphoreType.DMA((n, N_DEV - 1)), pltpu.SemaphoreType.DMA((n, N_DEV - 1)),
                        pltpu.SemaphoreType.DMA((n,))],
    )(*parts)


def _adamw(recv, w, m, v, name):
    r, c = w.shape
    br = _tile(r, 128)
    c1 = 1.0 / (1.0 - ADAM_B1 ** ADAM_STEP)
    c2 = 1.0 / (1.0 - ADAM_B2 ** ADAM_STEP)

    def body(r_ref, w_ref, m_ref, v_ref, g_ref, d_ref, mo_ref, vo_ref):
        g = r_ref[0].astype(F32)
        for k in range(1, N_DEV):
            g = g + r_ref[k].astype(F32)
        mn = ADAM_B1 * m_ref[...] + (1.0 - ADAM_B1) * g
        vn = ADAM_B2 * v_ref[...] + (1.0 - ADAM_B2) * (g * g)
        g_ref[...] = g
        mo_ref[...] = mn
        vo_ref[...] = vn
        d_ref[...] = -ADAM_LR * ((mn * c1) / (jnp.sqrt(vn * c2) + ADAM_EPS) + ADAM_WD * w_ref[...])

    blk = pl.BlockSpec((br, c), lambda i: (i, 0))
    return pl.pallas_call(
        body, name=name, grid=(r // br,),
        in_specs=[pl.BlockSpec((N_DEV, br, c), lambda i: (0, i, 0)), blk, blk, blk],
        out_specs=[blk] * 4, out_shape=[jax.ShapeDtypeStruct((r, c), F32)] * 4,
        compiler_params=_cp("parallel"))(recv, w, m, v)


def _lane_row(pairs):
    row = jnp.zeros((LANES,), F32)
    for lane0, vec in pairs:
        row = lax.dynamic_update_slice(row, vec.astype(F32), (lane0,))
    return row.reshape(1, LANES)


def _local_step(x, target, wts, small):
    s = x.shape[0]
    d = D_MODEL
    a = -jnp.exp(small["a_log"])
    bias_row = _lane_row([(DT_LANE0, small["dt_bias"]), (F_LANE0, small["b_forget"])])
    a_row = _lane_row([(DT_LANE0, a)])
    conv_b = small["conv_b"].reshape(1, -1)
    norm_w = small["ssm_norm_w"].reshape(1, -1)
    bg = small["b_gates"].reshape(1, -1)
    g1, b1 = small["ln1_g"].reshape(1, -1), small["ln1_b"].reshape(1, -1)
    g2, b2 = small["ln2_g"].reshape(1, -1), small["ln2_b"].reshape(1, -1)
    d_skip = small["d_skip"]
    xb = x.astype(BF16)

    qkv = _mm(xb, wts["qkv"], out_dtype=BF16, name="f_qkv")
    z = _mm(xb, wts["z"], name="f_z")
    xbc = _mm(xb, wts["xbc"], name="f_xbc")
    gl = _mm(xb, wts["gate"], name="f_gate")
    fd = _mm(xb, wts["fd"], name="f_fd")
    dt_c, ac_c, cf_c, dt_r, ac_r, cf_r = _stats_fwd(fd, bias_row, a_row)
    bk = _att_blocks(s)[1]
    ck4 = cf_r[F_LANE0:F_LANE0 + ATT_HEADS].reshape(N_HP, HP, s // bk, bk)
    attn, lse = _attention_fwd(qkv, ck4)
    attn_d = _mm(attn, wts["pa"], name="f_pa")
    xact = _conv_fwd(xbc, wts["conv"], conv_b)
    y, hprev = _ssd_fwd(xact, dt_c, ac_c, ac_r, d_skip)
    ssm = _gnorm_fwd(y, z, norm_w)
    ssm_d = _mm(ssm, wts["ps"], name="f_ps")
    mix = _mix_fwd(gl, bg, attn_d, ssm_d)
    mixed = _mm(mix, wts["out"], name="f_out")
    x1 = _ln1_fwd(x, mixed, g1, b1)
    gu = _mm(x1, wts["gu"], name="f_gu")
    act = _swiglu_fwd(gu)
    h = _mm(act, wts["down"], name="f_down")
    dpre2, loss_row, dg2, db2 = _ln2_loss(x1, h, target, g2, b2)

    d_act = _mm(dpre2, wts["down"], tb=True, name="b_down_x")
    dw_down = _mm(act, dpre2, ta=True, name="b_down_w")
    dgu = _swiglu_bwd(gu, d_act)
    dffn = _mm(dgu, wts["gu"], tb=True, name="b_gu_x")
    dw_gu = _mm(x1, dgu, ta=True, name="b_gu_w")
    dpre1, dxr, dg1, db1 = _ln1_bwd(x, mixed, g1, dpre2, dffn)
    dmix = _mm(dpre1, wts["out"], tb=True, name="b_out_x")
    dw_out = _mm(mix, dpre1, ta=True, name="b_out_w")
    dattn_d, dssm_d, dgl, dbg = _mix_bwd(gl, bg, attn_d, ssm_d, dmix)
    dssm = _mm(dssm_d, wts["ps"], tb=True, name="b_ps_x")
    dw_ps = _mm(ssm, dssm_d, ta=True, name="b_ps_w")
    dattn = _mm(dattn_d, wts["pa"], tb=True, name="b_pa_x")
    dw_pa = _mm(attn, dattn_d, ta=True, name="b_pa_w")
    dy, dz, dnw = _gnorm_bwd(y, z, norm_w, dssm)
    dxact, ddt, da_row, dds_row = _ssd_bwd(xact, dt_c, ac_c, ac_r, hprev, dy, d_skip, a_row)
    dpre_c, dconv_w, dconv_b = _conv_bwd_pre(xbc, wts["conv"], conv_b, dxact)
    dxbc = _conv_bwd_in(dpre_c, wts["conv"])
    st, do_b = _att_prep(dattn, attn, lse)
    dq, dk, dv, dck, dcq = _attention_bwd(qkv, ck4, st, do_b)
    dck_rows = jnp.zeros((LANES, s), F32).at[F_LANE0:F_LANE0 + ATT_HEADS].set(dck.reshape(ATT_HEADS, s))
    dcq_cols = jnp.zeros((s, LANES), F32).at[:, F_LANE0:F_LANE0 + ATT_HEADS].set(dcq[:, ::ATT_HEAD_DIM])
    dfd, dbias = _stats_bwd(fd, bias_row, ddt, dck_rows, dcq_cols)

    wq, wk, wv = wts["qkv"][:, :d], wts["qkv"][:, d:2 * d], wts["qkv"][:, 2 * d:]
    dx = dxr
    for i, (g_, w_) in enumerate(((dq, wq), (dk, wk), (dv, wv), (dz, wts["z"]), (dxbc, wts["xbc"]),
                                  (dgl, wts["gate"]), (dfd, wts["fd"]))):
        dx = _mm(g_, w_, tb=True, add=dx, name=f"b_in_x{i}")
    dw_in = [_mm(xb, g_, ta=True, name=f"b_in_w{i}") for i, g_ in enumerate((dq, dk, dv, dz, dxbc, dgl, dfd))]

    grads = dict(q=dw_in[0], k=dw_in[1], v=dw_in[2], z=dw_in[3], xbc=dw_in[4], gate=dw_in[5], fd=dw_in[6],
                 pa=dw_pa, ps=dw_ps, out=dw_out, gu=dw_gu, down=dw_down, conv=dconv_w)
    small_g = dict(
        b_forget=dbias[0, F_LANE0:F_LANE0 + ATT_HEADS], conv_b=dconv_b[0], dt_bias=dbias[0, :SSM_HEADS],
        a_log=da_row[0, :SSM_HEADS] * a, d_skip=dds_row[0, :SSM_HEADS], ssm_norm_w=dnw[0], b_gates=dbg[0],
        ln1_g=dg1[0], ln1_b=db1[0], ln2_g=dg2[0], ln2_b=db2[0])
    return loss_row[0, 0], dx, grads, small_g


BIG = ("w_in", "w_proj_attn", "w_proj_ssm", "w_out", "w_ffn_gate", "w_ffn_up", "w_ffn_down", "conv_w")
SMALL = ("b_forget", "conv_b", "dt_bias", "a_log", "d_skip", "ssm_norm_w", "b_gates", "ln1_g", "ln1_b", "ln2_g",
         "ln2_b")
SMALL_ROWS = 96
IN_SHARD = IN_WIDTH // N_DEV
IN_SEGMENTS = (("q", 0, 1024), ("k", 1024, 1024), ("v", 2048, 1024), ("f", 3072, ATT_HEADS), ("z", 3088, SSM_INNER),
               ("xbc", 5136, SSM_CONV_DIM), ("dt", 8208, SSM_HEADS), ("gate", 8240, 2 * D_MODEL))


def _cols_from_shards(shards, lo, hi):
    w = shards[0].shape[1]
    pieces = []
    for j in range(len(shards)):
        a, b = max(lo, j * w), min(hi, (j + 1) * w)
        if a < b:
            pieces.append(shards[j][:, a - j * w:b - j * w])
    return pieces[0] if len(pieces) == 1 else jnp.concatenate(pieces, axis=1)


def _shards_from_parts(parts, width):
    shards = []
    for j in range(N_DEV):
        lo, hi = j * width, (j + 1) * width
        pieces = []
        for mat, c0 in parts:
            a, b = max(lo, c0), min(hi, c0 + mat.shape[1])
            if a < b:
                pieces.append(mat[:, a - c0:b - c0])
        shards.append(pieces[0] if len(pieces) == 1 else jnp.concatenate(pieces, axis=1))
    return shards


def _pack_small(vals):
    flat = jnp.concatenate([vals[n].reshape(-1) for n in SMALL])
    return jnp.pad(flat, (0, SMALL_ROWS * LANES - flat.shape[0])).reshape(SMALL_ROWS, LANES)


def _unpack_small(pack, shapes):
    flat = pack.reshape(-1)
    out, off = {}, 0
    for n in SMALL:
        sz = math.prod(shapes[n])
        out[n] = flat[off:off + sz].reshape(shapes[n])
        off += sz
    return out


def kernel(x, w_in, b_forget, conv_w, conv_b, dt_bias, a_log, d_skip, ssm_norm_w, w_proj_attn, w_proj_ssm, b_gates, w_out, ln1_g, ln1_b, w_ffn_gate, w_ffn_up, w_ffn_down, ln2_g, ln2_b, loss_target, m_w_in, m_b_forget, m_conv_w, m_conv_b, m_dt_bias, m_a_log, m_d_skip, m_ssm_norm_w, m_w_proj_attn, m_w_proj_ssm, m_b_gates, m_w_out, m_ln1_g, m_ln1_b, m_w_ffn_gate, m_w_ffn_up, m_w_ffn_down, m_ln2_g, m_ln2_b, v_w_in, v_b_forget, v_conv_w, v_conv_b, v_dt_bias, v_a_log, v_d_skip, v_ssm_norm_w, v_w_proj_attn, v_w_proj_ssm, v_b_gates, v_w_out, v_ln1_g, v_ln1_b, v_w_ffn_gate, v_w_ffn_up, v_w_ffn_down, v_ln2_g, v_ln2_b):
    args = dict(locals())
    d, f = D_MODEL, FFN_HIDDEN
    big_w = {n: args[n][0] for n in BIG}
    small_w = {n: args[n][0] for n in SMALL}
    big_shapes = {n: args[n].shape for n in BIG}
    small_shapes = {n: args[n].shape for n in SMALL}

    sent = [big_w[n].astype(BF16) for n in BIG[:-1]] + [big_w["conv_w"]]
    full = dict(zip(BIG, _all_gather(sent)))
    in_shards = [full["w_in"][j] for j in range(N_DEV)]
    seg = {n: _cols_from_shards(in_shards, c0, c0 + w) for n, c0, w in IN_SEGMENTS}
    wfd = jnp.concatenate([seg["dt"], seg["f"], jnp.zeros((d, LANES - SSM_HEADS - ATT_HEADS), BF16)], axis=1)
    wts = dict(
        qkv=jnp.concatenate([seg["q"], seg["k"], seg["v"]], axis=1), z=seg["z"], xbc=seg["xbc"], gate=seg["gate"],
        fd=wfd, pa=full["w_proj_attn"].reshape(d, d), ps=full["w_proj_ssm"].reshape(SSM_INNER, d),
        out=full["w_out"].reshape(d, d),
        gu=jnp.concatenate([full["w_ffn_gate"][j] for j in range(N_DEV)]
                           + [full["w_ffn_up"][j] for j in range(N_DEV)], axis=1),
        down=full["w_ffn_down"].reshape(f, d),
        conv=jnp.concatenate([full["conv_w"][j] for j in range(N_DEV)], axis=1))

    loss_part, grad_x, g, small_g = _local_step(x[0], loss_target[0], wts, small_w)
    loss = lax.psum(loss_part, ("x", "y", "c"))

    gfd = g["fd"]
    in_parts = dict(q=g["q"], k=g["k"], v=g["v"], f=gfd[:, F_LANE0:F_LANE0 + ATT_HEADS], z=g["z"], xbc=g["xbc"],
                    dt=gfd[:, DT_LANE0:DT_LANE0 + SSM_HEADS], gate=g["gate"])
    col_shards = dict(
        w_in=_shards_from_parts([(in_parts[n], c0) for n, c0, _ in IN_SEGMENTS], IN_SHARD),
        w_ffn_gate=_shards_from_parts([(g["gu"][:, :f], 0)], f // N_DEV),
        w_ffn_up=_shards_from_parts([(g["gu"][:, f:], 0)], f // N_DEV),
        conv_w=_shards_from_parts([(g["conv"], 0)], SSM_CONV_DIM // N_DEV))
    by_dest = {n: jnp.stack([s_.astype(F32 if n == "conv_w" else BF16) for s_ in col_shards[n]])
               for n in col_shards}
    for n, key in (("w_proj_attn", "pa"), ("w_proj_ssm", "ps"), ("w_out", "out"), ("w_ffn_down", "down")):
        by_dest[n] = g[key].astype(BF16).reshape((N_DEV,) + big_shapes[n][1:])
    small_pack = _pack_small(small_g)
    recv = _exchange([by_dest[n] for n in BIG] + [jnp.broadcast_to(small_pack, (N_DEV,) + small_pack.shape)])

    outs = {}
    for t, n in enumerate(BIG):
        shp = big_shapes[n]
        res4 = _adamw(recv[t], big_w[n], args["m_" + n][0], args["v_" + n][0], name="adamw_" + n)
        outs[n] = [r.reshape(shp) for r in res4]
    small4 = _adamw(recv[len(BIG)], _pack_small(small_w), _pack_small({n: args["m_" + n][0] for n in SMALL}),
                    _pack_small({n: args["v_" + n][0] for n in SMALL}), name="adamw_small")
    small_out = [_unpack_small(p, small_shapes) for p in small4]
    for n in SMALL:
        outs[n] = [so[n] for so in small_out]

    order = ("w_in", "b_forget", "conv_w", "conv_b", "dt_bias", "a_log", "d_skip", "ssm_norm_w", "w_proj_attn",
             "w_proj_ssm", "b_gates", "w_out", "ln1_g", "ln1_b", "w_ffn_gate", "w_ffn_up", "w_ffn_down", "ln2_g",
             "ln2_b")
    res = [loss, grad_x[None]]
    for i in range(4):
        res += [outs[n][i] for n in order]
    return tuple(res)
```

```python
import functools
import math

import jax
import jax.numpy as jnp
from jax import lax
from jax.experimental import pallas as pl
from jax.experimental.pallas import tpu as pltpu

F32 = jnp.float32
BF16 = jnp.bfloat16

N_DEV = 8
D_MODEL = 1024
ATT_HEADS = 16
ATT_HEAD_DIM = 64
SSM_INNER = 2048
SSM_HEADS = 32
SSM_HEAD_DIM = 64
SSM_GROUPS = 4
SSM_HEADS_PER_GROUP = 8
SSM_STATE = 128
SSM_CONV = 4
SSM_CHUNK = 128
SSM_CONV_DIM = 3072
FFN_HIDDEN = 2816
IN_WIDTH = 10288
DEEPNORM_ALPHA = 2.0 ** 0.25
LN_EPS = 1e-5
RMS_EPS = 1e-5
ADAM_LR, ADAM_B1, ADAM_B2, ADAM_EPS, ADAM_WD, ADAM_STEP = 0.001, 0.9, 0.999, 1e-08, 0.01, 10
ATT_SCALE = 1.0 / math.sqrt(ATT_HEAD_DIM)

LANES = 128
VMEM_LIMIT = 56 * 1024 * 1024
NEG = -1e30

DT_LANE0 = 0
F_LANE0 = 32
HI = lax.Precision.HIGHEST


def _cp(*sem):
    return pltpu.CompilerParams(dimension_semantics=sem, vmem_limit_bytes=VMEM_LIMIT)


def _tile(n, cap=1408):
    for t in (1408, 1024, 512, 384, 256, 128):
        if t <= cap and n % t == 0:
            return t
    return n


def _sigmoid(x):
    return 1.0 / (1.0 + jnp.exp(-x))


def _mm(a, b, *, ta=False, tb=False, out_dtype=F32, add=None, name):
    m, k = (a.shape[1], a.shape[0]) if ta else a.shape
    n = b.shape[0] if tb else b.shape[1]
    assert (b.shape[1] if tb else b.shape[0]) == k
    tm, tn, tk = _tile(m), _tile(n), _tile(k)
    nk = k // tk
    dims = (((0,) if ta else (1,), (1,) if tb else (0,)), ((), ()))

    def body(*refs):
        if add is None:
            a_ref, b_ref, o_ref, acc_ref = refs
        else:
            a_ref, b_ref, c_ref, o_ref, acc_ref = refs
        kk = pl.program_id(2)

        @pl.when(kk == 0)
        def _():
            acc_ref[...] = jnp.zeros_like(acc_ref)

        acc_ref[...] += lax.dot_general(a_ref[...].astype(BF16), b_ref[...].astype(BF16), dims,
                                        preferred_element_type=F32)

        @pl.when(kk == nk - 1)
        def _():
            r = acc_ref[...]
            if add is not None:
                r = r + c_ref[...]
            o_ref[...] = r.astype(o_ref.dtype)

    a_spec = pl.BlockSpec((tk, tm), lambda i, j, kk: (kk, i)) if ta else pl.BlockSpec((tm, tk), lambda i, j, kk: (i, kk))
    b_spec = pl.BlockSpec((tn, tk), lambda i, j, kk: (j, kk)) if tb else pl.BlockSpec((tk, tn), lambda i, j, kk: (kk, j))
    o_spec = pl.BlockSpec((tm, tn), lambda i, j, kk: (i, j))
    in_specs, args = [a_spec, b_spec], [a, b]
    if add is not None:
        in_specs.append(o_spec)
        args.append(add)
    return pl.pallas_call(
        body, name=name, grid=(m // tm, n // tn, nk), in_specs=in_specs, out_specs=o_spec,
        out_shape=jax.ShapeDtypeStruct((m, n), out_dtype),
        scratch_shapes=[pltpu.VMEM((tm, tn), F32)],
        compiler_params=_cp("parallel", "parallel", "arbitrary"),
    )(*args)


def _tri(n, lower=True):
    r = lax.broadcasted_iota(jnp.int32, (n, n), 0)
    c = lax.broadcasted_iota(jnp.int32, (n, n), 1)
    return jnp.where((r >= c) if lower else (c >= r), 1.0, 0.0).astype(F32)


def _stats_fwd(fd, bias_row, a_row):
    s = fd.shape[0]
    blk = SSM_CHUNK

    def body(fd_ref, bias_ref, a_ref, dt_ref, ac_ref, cf_ref, dtr_ref, acr_ref, cfr_ref, carry_ref):
        @pl.when(pl.program_id(0) == 0)
        def _():
            carry_ref[...] = jnp.zeros_like(carry_ref)

        v = fd_ref[...] + bias_ref[...]
        dt = jnp.maximum(v, 0.0) + jnp.log(1.0 + jnp.exp(-jnp.abs(v)))
        lf = jnp.minimum(v, 0.0) - jnp.log(1.0 + jnp.exp(-jnp.abs(v)))
        tri = _tri(blk)
        ac = jnp.dot(tri, dt * a_ref[...], precision=HI, preferred_element_type=F32)
        cf = jnp.dot(tri, lf, precision=HI, preferred_element_type=F32) + carry_ref[0:1, :]
        carry_ref[...] = carry_ref[...] + jnp.sum(lf, axis=0, keepdims=True)
        dt_ref[...] = dt
        ac_ref[...] = ac
        cf_ref[...] = cf
        dtr_ref[...] = dt.T
        acr_ref[...] = ac.T
        cfr_ref[...] = cf.T

    col = pl.BlockSpec((blk, LANES), lambda i: (i, 0))
    row = pl.BlockSpec((LANES, blk), lambda i: (0, i))
    vec = pl.BlockSpec((1, LANES), lambda i: (0, 0))
    return pl.pallas_call(
        body, name="stats_fwd", grid=(s // blk,), in_specs=[col, vec, vec],
        out_specs=[col, col, col, row, row, row],
        out_shape=[jax.ShapeDtypeStruct((s, LANES), F32)] * 3 + [jax.ShapeDtypeStruct((LANES, s), F32)] * 3,
        scratch_shapes=[pltpu.VMEM((8, LANES), F32)],
        compiler_params=_cp("arbitrary"),
    )(fd, bias_row, a_row)


def _stats_bwd(fd, bias_row, ddt, dck_rows, dcq_cols):
    s = fd.shape[0]
    blk = SSM_CHUNK
    nb = s // blk

    def body(fd_ref, bias_ref, ddt_ref, dck_ref, dcq_ref, o_ref, db_ref, carry_ref):
        @pl.when(pl.program_id(0) == 0)
        def _():
            carry_ref[...] = jnp.zeros_like(carry_ref)
            db_ref[...] = jnp.zeros_like(db_ref)

        v = fd_ref[...] + bias_ref[...]
        dcum = dck_ref[...].T + dcq_ref[...]
        dlf = jnp.dot(_tri(blk, lower=False), dcum, precision=HI, preferred_element_type=F32) + carry_ref[0:1, :]
        carry_ref[...] = carry_ref[...] + jnp.sum(dcum, axis=0, keepdims=True)
        lane = lax.broadcasted_iota(jnp.int32, v.shape, 1)
        g = jnp.where(lane < F_LANE0, ddt_ref[...] * _sigmoid(v), dlf * _sigmoid(-v))
        g = jnp.where(lane < F_LANE0 + ATT_HEADS, g, 0.0)
        o_ref[...] = g.astype(o_ref.dtype)
        db_ref[...] += jnp.sum(g, axis=0, keepdims=True)

    col = pl.BlockSpec((blk, LANES), lambda i: (nb - 1 - i, 0))
    row = pl.BlockSpec((LANES, blk), lambda i: (0, nb - 1 - i))
    vec = pl.BlockSpec((1, LANES), lambda i: (0, 0))
    return pl.pallas_call(
        body, name="stats_bwd", grid=(nb,), in_specs=[col, vec, col, row, col], out_specs=[col, vec],
        out_shape=[jax.ShapeDtypeStruct((s, LANES), BF16), jax.ShapeDtypeStruct((1, LANES), F32)],
        scratch_shapes=[pltpu.VMEM((8, LANES), F32)],
        compiler_params=_cp("arbitrary"),
    )(fd, bias_row, ddt, dck_rows, dcq_cols)


HP = LANES // ATT_HEAD_DIM
N_HP = ATT_HEADS // HP


def _att_blocks(s):
    return (512, 1024) if s % 1024 == 0 and s >= 4096 else (64, 128)


_QK = (((1,), (1,)), ((), ()))
_HALF = ATT_HEAD_DIM // 2


def _head_cols(a):
    return slice(a * ATT_HEAD_DIM, (a + 1) * ATT_HEAD_DIM)


def _causal(shape, off):
    r = lax.broadcasted_iota(jnp.int32, shape, 0)
    c = lax.broadcasted_iota(jnp.int32, shape, 1)
    return c <= r + off


def _attention_fwd(qkv, ck4):
    s = qkv.shape[0]
    bq, bk = _att_blocks(s)
    nq, nk = s // bq, s // bk

    def body(q_ref, k_ref, v_ref, ck_ref, o_ref, lse_ref):
        i = pl.program_id(1)
        n_full = (i * bq) // bk
        qs = [(q_ref[:, _head_cols(a)].astype(F32) * ATT_SCALE).astype(BF16) for a in range(HP)]

        def step(j, carry, off=None):
            ks = pl.ds(pl.multiple_of(j * bk, bk), bk)
            out = []
            for a in range(HP):
                m, l, acc = carry[a]
                sc = lax.dot_general(qs[a], k_ref[ks, _head_cols(a)], _QK, preferred_element_type=F32)
                sc = sc - ck_ref[0, a, pl.ds(j, 1), :]
                if off is not None:
                    sc = jnp.where(_causal(sc.shape, off), sc, NEG)
                m_new = jnp.maximum(m, jnp.max(sc, axis=1, keepdims=True))
                alpha = jnp.exp(m - m_new)
                p = jnp.exp(sc - m_new)
                l = alpha * l + jnp.sum(p, axis=1, keepdims=True)
                acc = alpha * acc + jnp.dot(p.astype(BF16), v_ref[ks, _head_cols(a)], preferred_element_type=F32)
                out.append((m_new, l, acc))
            return tuple(out)

        init = tuple((jnp.full((bq, 1), NEG, F32), jnp.zeros((bq, 1), F32), jnp.zeros((bq, ATT_HEAD_DIM), F32))
                     for _ in range(HP))
        carry = lax.fori_loop(0, n_full, step, init)
        carry = step(n_full, carry, off=i * bq - n_full * bk)
        for a in range(HP):
            m, l, acc = carry[a]
            o_ref[:, _head_cols(a)] = acc / l
            lse_ref[:, _head_cols(a)] = jnp.broadcast_to(m + jnp.log(l), (bq, ATT_HEAD_DIM))

    q_spec = pl.BlockSpec((bq, LANES), lambda h, i: (i, h))
    return pl.pallas_call(
        body, name="att_fwd", grid=(N_HP, nq),
        in_specs=[q_spec, pl.BlockSpec((s, LANES), lambda h, i: (0, N_HP + h)),
                  pl.BlockSpec((s, LANES), lambda h, i: (0, 2 * N_HP + h)),
                  pl.BlockSpec((1, HP, nk, bk), lambda h, i: (h, 0, 0, 0))],
        out_specs=[q_spec, q_spec], out_shape=[jax.ShapeDtypeStruct((s, D_MODEL), F32)] * 2,
        compiler_params=_cp("parallel", "arbitrary"),
    )(qkv, qkv, qkv, ck4)


def _att_prep(do, o, lse_rep):
    s = do.shape[0]
    bs = _tile(s, 512)

    def body(do_ref, o_ref, lse_ref, st_ref, dob_ref):
        r = lax.broadcasted_iota(jnp.int32, (LANES, LANES), 0) // ATT_HEAD_DIM
        c = lax.broadcasted_iota(jnp.int32, (LANES, LANES), 1) // ATT_HEAD_DIM
        e = jnp.where(r == c, 1.0, 0.0).astype(F32)
        lane = lax.broadcasted_iota(jnp.int32, (bs, LANES), 1)
        for p in range(D_MODEL // LANES):
            cs = slice(p * LANES, (p + 1) * LANES)
            dd = do_ref[:, cs]
            delta = jnp.dot(dd * o_ref[:, cs], e, precision=HI, preferred_element_type=F32)
            st_ref[:, cs] = jnp.where(lane % ATT_HEAD_DIM < _HALF, lse_ref[:, cs], delta)
            dob_ref[:, cs] = dd.astype(BF16)

    spec = pl.BlockSpec((bs, D_MODEL), lambda i: (i, 0))
    return pl.pallas_call(body, name="att_prep", grid=(s // bs,), in_specs=[spec, spec, spec], out_specs=[spec, spec],
                          out_shape=[jax.ShapeDtypeStruct((s, D_MODEL), F32), jax.ShapeDtypeStruct((s, D_MODEL), BF16)],
                          compiler_params=_cp("parallel"))(do, o, lse_rep)


def _attention_bwd(qkv, ck4, st, do_b):
    s = qkv.shape[0]
    bq, bk = _att_blocks(s)
    nq, nk, per = s // bq, s // bk, bk // bq
    _T = (((0,), (0,)), ((), ()))

    def body(q_ref, k_ref, v_ref, ck_ref, st_ref, do_ref, dq_ref, dk_ref, dv_ref, dck_ref, dcq_ref,
             dk_acc, dv_acc, dck_acc):
        j = pl.program_id(1)

        @pl.when(j == 0)
        def _():
            dq_ref[...] = jnp.zeros_like(dq_ref)
            dcq_ref[...] = jnp.zeros_like(dcq_ref)

        dk_acc[...] = jnp.zeros_like(dk_acc)
        dv_acc[...] = jnp.zeros_like(dv_acc)
        dck_acc[...] = jnp.zeros_like(dck_acc)

        def step(i, off=None):
            rows = pl.ds(pl.multiple_of(i * bq, bq), bq)
            for a in range(HP):
                cs = _head_cols(a)
                q = (q_ref[rows, cs].astype(F32) * ATT_SCALE).astype(BF16)
                k = k_ref[:, cs]
                do_a = do_ref[rows, cs]
                sc = lax.dot_general(q, k, _QK, preferred_element_type=F32) - ck_ref[0, a, pl.ds(j, 1), :]
                if off is not None:
                    sc = jnp.where(_causal(sc.shape, off), sc, NEG)
                p = jnp.exp(sc - st_ref[rows, a * ATT_HEAD_DIM:a * ATT_HEAD_DIM + 1])
                dp = lax.dot_general(do_a, v_ref[:, cs], _QK, preferred_element_type=F32)
                ds = p * (dp - st_ref[rows, a * ATT_HEAD_DIM + _HALF:a * ATT_HEAD_DIM + _HALF + 1])
                ds_b = ds.astype(BF16)
                dv_acc[a] += lax.dot_general(p.astype(BF16), do_a, _T, preferred_element_type=F32)
                dk_acc[a] += lax.dot_general(ds_b, q, _T, preferred_element_type=F32)
                dq_ref[rows, cs] += jnp.dot(ds_b, k, preferred_element_type=F32) * ATT_SCALE
                dck_acc[a] -= jnp.sum(ds, axis=0, keepdims=True)
                dcq_ref[rows, cs] += jnp.broadcast_to(jnp.sum(ds, axis=1, keepdims=True), (bq, ATT_HEAD_DIM))

        for t in range(per):
            step(j * per + t, off=t * bq)

        def full(i, c):
            step(i)
            return c

        lax.fori_loop((j + 1) * per, nq, full, 0)
        for a in range(HP):
            dk_ref[:, _head_cols(a)] = dk_acc[a].astype(dk_ref.dtype)
            dv_ref[:, _head_cols(a)] = dv_acc[a].astype(dv_ref.dtype)
            dck_ref[0, a, pl.ds(j, 1), :] = dck_acc[a]

    res = pl.BlockSpec((s, LANES), lambda h, j: (0, h))
    ck_spec = pl.BlockSpec((1, HP, nk, bk), lambda h, j: (h, 0, 0, 0))
    kout = pl.BlockSpec((bk, LANES), lambda h, j: (j, h))
    return pl.pallas_call(
        body, name="att_bwd", grid=(N_HP, nk),
        in_specs=[res, pl.BlockSpec((bk, LANES), lambda h, j: (j, N_HP + h)),
                  pl.BlockSpec((bk, LANES), lambda h, j: (j, 2 * N_HP + h)), ck_spec, res, res],
        out_specs=[res, kout, kout, ck_spec, res],
        out_shape=[jax.ShapeDtypeStruct((s, D_MODEL), F32), jax.ShapeDtypeStruct((s, D_MODEL), BF16),
                   jax.ShapeDtypeStruct((s, D_MODEL), BF16), jax.ShapeDtypeStruct((N_HP, HP, nk, bk), F32),
                   jax.ShapeDtypeStruct((s, D_MODEL), F32)],
        scratch_shapes=[pltpu.VMEM((HP, bk, ATT_HEAD_DIM), F32), pltpu.VMEM((HP, bk, ATT_HEAD_DIM), F32),
                        pltpu.VMEM((HP, 1, bk), F32)],
        compiler_params=_cp("parallel", "arbitrary"),
    )(qkv, qkv, qkv, ck4, st, do_b)


def _silu_and_grad(x):
    sg = _sigmoid(x)
    return x * sg, sg * (1.0 + x * (1.0 - sg))


def _conv_pre(cur, halo, w_ref, b_ref, first):
    halo = jnp.where(first, 0.0, halo)
    row = lax.broadcasted_iota(jnp.int32, cur.shape, 0)
    shifted = []
    for k in range(SSM_CONV):
        sh = SSM_CONV - 1 - k
        if sh == 0:
            shifted.append(cur)
            continue
        r = pltpu.roll(cur, sh, 0)
        hr = pltpu.roll(halo, sh, 0)
        top = jnp.where(row[0:8] < sh, hr, r[0:8])
        shifted.append(jnp.concatenate([top, r[8:]], axis=0))
    pre = b_ref[...] + sum(w_ref[k:k + 1, :] * shifted[k] for k in range(SSM_CONV))
    return pre, shifted


def _conv_specs(s, bs, bc):
    cur = pl.BlockSpec((bs, bc), lambda j, i: (i, j))
    halo = pl.BlockSpec((8, bc), lambda j, i: (jnp.maximum(i * (bs // 8) - 1, 0), j))
    w = pl.BlockSpec((SSM_CONV, bc), lambda j, i: (0, j))
    b = pl.BlockSpec((1, bc), lambda j, i: (0, j))
    return cur, halo, w, b


def _conv_fwd(xbc, w, b):
    s, c = xbc.shape
    bs, bc = _tile(s, 512), 1024

    def body(x_ref, h_ref, w_ref, b_ref, o_ref):
        pre, _ = _conv_pre(x_ref[...], h_ref[...], w_ref, b_ref, pl.program_id(1) == 0)
        o_ref[...] = pre * _sigmoid(pre)

    cur, halo, ws, bsp = _conv_specs(s, bs, bc)
    return pl.pallas_call(body, name="conv_fwd", grid=(c // bc, s // bs), in_specs=[cur, halo, ws, bsp],
                          out_specs=cur, out_shape=jax.ShapeDtypeStruct((s, c), F32),
                          compiler_params=_cp("parallel", "parallel"))(xbc, xbc, w, b)


def _conv_bwd_pre(xbc, w, b, dact):
    s, c = xbc.shape
    bs, bc = _tile(s, 512), 1024

    def body(x_ref, h_ref, w_ref, b_ref, g_ref, dp_ref, dw_ref, db_ref):
        @pl.when(pl.program_id(1) == 0)
        def _():
            dw_ref[...] = jnp.zeros_like(dw_ref)
            db_ref[...] = jnp.zeros_like(db_ref)

        pre, shifted = _conv_pre(x_ref[...], h_ref[...], w_ref, b_ref, pl.program_id(1) == 0)
        dpre = g_ref[...] * _silu_and_grad(pre)[1]
        dp_ref[...] = dpre
        db_ref[...] += jnp.sum(dpre, axis=0, keepdims=True)
        for k in range(SSM_CONV):
            dw_ref[k:k + 1, :] += jnp.sum(dpre * shifted[k], axis=0, keepdims=True)

    cur, halo, ws, bsp = _conv_specs(s, bs, bc)
    return pl.pallas_call(
        body, name="conv_bwd_pre", grid=(c // bc, s // bs), in_specs=[cur, halo, ws, bsp, cur],
        out_specs=[cur, ws, bsp],
        out_shape=[jax.ShapeDtypeStruct((s, c), F32), jax.ShapeDtypeStruct((SSM_CONV, c), F32),
                   jax.ShapeDtypeStruct((1, c), F32)],
        compiler_params=_cp("parallel", "arbitrary"))(xbc, xbc, w, b, dact)


def _conv_bwd_in(dpre, w):
    s, c = dpre.shape
    bs, bc = _tile(s, 512), 1024
    nb = s // bs

    def body(g_ref, n_ref, w_ref, o_ref):
        cur = g_ref[...]
        nxt = jnp.where(pl.program_id(1) == nb - 1, 0.0, n_ref[...])
        row = lax.broadcasted_iota(jnp.int32, cur.shape, 0)
        acc = w_ref[SSM_CONV - 1:SSM_CONV, :] * cur
        for sh in range(1, SSM_CONV):
            r = pltpu.roll(cur, bs - sh, 0)
            nr = pltpu.roll(nxt, 8 - sh, 0)
            bot = jnp.where(row[0:8] >= 8 - sh, nr, r[bs - 8:])
            acc = acc + w_ref[SSM_CONV - 1 - sh:SSM_CONV - sh, :] * jnp.concatenate([r[:bs - 8], bot], axis=0)
        o_ref[...] = acc.astype(o_ref.dtype)

    cur = pl.BlockSpec((bs, bc), lambda j, i: (i, j))
    nxt = pl.BlockSpec((8, bc), lambda j, i: (jnp.minimum((i + 1) * (bs // 8), s // 8 - 1), j))
    ws = pl.BlockSpec((SSM_CONV, bc), lambda j, i: (0, j))
    return pl.pallas_call(body, name="conv_bwd_in", grid=(c // bc, nb), in_specs=[cur, nxt, ws], out_specs=cur,
                          out_shape=jax.ShapeDtypeStruct((s, c), BF16),
                          compiler_params=_cp("parallel", "parallel"))(dpre, dpre, w)


def _dotT(a, b):
    return lax.dot_general(a.astype(BF16), b.astype(BF16), (((1,), (1,)), ((), ())), preferred_element_type=F32)


def _Tdot(a, b):
    return lax.dot_general(a.astype(BF16), b.astype(BF16), (((0,), (0,)), ((), ())), preferred_element_type=F32)


def _dot(a, b):
    return jnp.dot(a.astype(BF16), b.astype(BF16), preferred_element_type=F32)


def _ssd_head(xbc_ref, dt_ref, ac_ref, acr_ref, h):
    L = SSM_CHUNK
    xs = xbc_ref[:, h * SSM_HEAD_DIM:(h + 1) * SSM_HEAD_DIM]
    dt_col = dt_ref[:, h:h + 1]
    a_col = ac_ref[:, h:h + 1]
    a_row = acr_ref[h:h + 1, :]
    li = lax.broadcasted_iota(jnp.int32, (L, L), 0)
    si = lax.broadcasted_iota(jnp.int32, (L, L), 1)
    decay = jnp.exp(jnp.where(li >= si, a_col - a_row, NEG))
    a_last = ac_ref[L - 1:L, h:h + 1]
    return xs, dt_col, a_col, a_last, decay


def _ssd_fwd(xbc_act, dt_c, ac_c, ac_r, d_skip):
    s = xbc_act.shape[0]
    L, P, N, G, R = SSM_CHUNK, SSM_HEAD_DIM, SSM_STATE, SSM_GROUPS, SSM_HEADS_PER_GROUP
    nc = s // L

    def body(dsk_ref, xbc_ref, dt_ref, ac_ref, acr_ref, y_ref, hp_ref, st_ref):
        @pl.when(pl.program_id(0) == 0)
        def _():
            st_ref[...] = jnp.zeros_like(st_ref)

        for g in range(G):
            b_g = xbc_ref[:, SSM_INNER + g * N:SSM_INNER + (g + 1) * N]
            c_g = xbc_ref[:, SSM_INNER + G * N + g * N:SSM_INNER + G * N + (g + 1) * N]
            cb = _dotT(c_g, b_g)
            for r in range(R):
                h = g * R + r
                xs, dt_col, a_col, a_last, decay = _ssd_head(xbc_ref, dt_ref, ac_ref, acr_ref, h)
                xdt = xs * dt_col
                hprev = st_ref[h]
                y = _dot(cb * decay, xdt) + jnp.exp(a_col) * _dotT(c_g, hprev) + dsk_ref[h] * xs
                y_ref[:, h * P:(h + 1) * P] = y
                hp_ref[0, h] = hprev
                st_ref[h] = hprev * jnp.exp(a_last) + _Tdot(xdt * jnp.exp(a_last - a_col), b_g)

    col = pl.BlockSpec((L, LANES), lambda c: (c, 0))
    return pl.pallas_call(
        body, name="ssd_fwd", grid=(nc,),
        in_specs=[pl.BlockSpec(memory_space=pltpu.SMEM), pl.BlockSpec((L, SSM_CONV_DIM), lambda c: (c, 0)), col, col,
                  pl.BlockSpec((LANES, L), lambda c: (0, c))],
        out_specs=[pl.BlockSpec((L, SSM_INNER), lambda c: (c, 0)),
                   pl.BlockSpec((1, SSM_HEADS, P, N), lambda c: (c, 0, 0, 0))],
        out_shape=[jax.ShapeDtypeStruct((s, SSM_INNER), F32), jax.ShapeDtypeStruct((nc, SSM_HEADS, P, N), F32)],
        scratch_shapes=[pltpu.VMEM((SSM_HEADS, P, N), F32)],
        compiler_params=_cp("arbitrary"),
    )(d_skip, xbc_act, dt_c, ac_c, ac_r)


def _ssd_bwd(xbc_act, dt_c, ac_c, ac_r, hprev_all, dy, d_skip, a_row):
    s = xbc_act.shape[0]
    L, P, N, G, R = SSM_CHUNK, SSM_HEAD_DIM, SSM_STATE, SSM_GROUPS, SSM_HEADS_PER_GROUP
    nc = s // L

    def body(dsk_ref, xbc_ref, dt_ref, ac_ref, acr_ref, hp_ref, dy_ref, arow_ref,
             dx_ref, ddt_ref, da_ref, dds_ref, dh_ref):
        @pl.when(pl.program_id(0) == 0)
        def _():
            dh_ref[...] = jnp.zeros_like(dh_ref)
            da_ref[...] = jnp.zeros_like(da_ref)
            dds_ref[...] = jnp.zeros_like(dds_ref)

        lane = lax.broadcasted_iota(jnp.int32, (L, LANES), 1)
        sub = lax.broadcasted_iota(jnp.int32, (LANES, L), 0)
        rowi = lax.broadcasted_iota(jnp.int32, (L, 1), 0)
        lane1 = lax.broadcasted_iota(jnp.int32, (1, LANES), 1)
        da_c = jnp.zeros((L, LANES), F32)
        da_r = jnp.zeros((LANES, L), F32)
        ddt1 = jnp.zeros((L, LANES), F32)
        dds = jnp.zeros((1, LANES), F32)
        for g in range(G):
            b_g = xbc_ref[:, SSM_INNER + g * N:SSM_INNER + (g + 1) * N]
            c_g = xbc_ref[:, SSM_INNER + G * N + g * N:SSM_INNER + G * N + (g + 1) * N]
            cb = _dotT(c_g, b_g)
            dcb = jnp.zeros((L, L), F32)
            db_g = jnp.zeros((L, N), F32)
            dc_g = jnp.zeros((L, N), F32)
            for r in range(R):
                h = g * R + r
                xs, dt_col, a_col, a_last, decay = _ssd_head(xbc_ref, dt_ref, ac_ref, acr_ref, h)
                gy = dy_ref[:, h * P:(h + 1) * P]
                xdt = xs * dt_col
                hprev = hp_ref[0, h]
                dhn = dh_ref[h]
                e_a = jnp.exp(a_col)
                e_last = jnp.exp(a_last)
                e_col = jnp.exp(a_last - a_col)
                m = cb * decay
                yoff = e_a * _dotT(c_g, hprev)
                da_col = jnp.sum(gy * yoff, axis=1, keepdims=True)
                dc_g = dc_g + e_a * _dot(gy, hprev)
                dhp = _Tdot(gy * e_a, c_g) + dhn * e_last
                da_last = jnp.sum(jnp.sum(dhn * hprev, axis=1, keepdims=True), axis=0, keepdims=True) * e_last
                xds = _dot(xdt, dhn)
                db_g = db_g + e_col * xds
                de_e = jnp.sum(xds * b_g, axis=1, keepdims=True) * e_col
                da_col = da_col - de_e
                da_last = da_last + jnp.sum(de_e, axis=0, keepdims=True)
                dxdt = e_col * _dotT(b_g, dhn)
                dm = _dotT(gy, xdt)
                dxdt = dxdt + _Tdot(m, gy)
                dcb = dcb + dm * decay
                w = dm * m
                da_col = da_col + jnp.sum(w, axis=1, keepdims=True) + jnp.where(rowi == L - 1, da_last, 0.0)
                da_c = jnp.where(lane == h, da_col, da_c)
                da_r = jnp.where(sub == h, jnp.sum(w, axis=0, keepdims=True), da_r)
                ddt1 = jnp.where(lane == h, jnp.sum(dxdt * xs, axis=1, keepdims=True), ddt1)
                dds = jnp.where(lane1 == h, jnp.sum(jnp.sum(gy * xs, axis=1, keepdims=True), axis=0, keepdims=True),
                                dds)
                dx_ref[:, h * P:(h + 1) * P] = dxdt * dt_col + dsk_ref[h] * gy
                dh_ref[h] = dhp
            dx_ref[:, SSM_INNER + g * N:SSM_INNER + (g + 1) * N] = db_g + _Tdot(dcb, c_g)
            dx_ref[:, SSM_INNER + G * N + g * N:SSM_INNER + G * N + (g + 1) * N] = dc_g + _dot(dcb, b_g)
        dda = jnp.dot(_tri(L, lower=False), da_c - da_r.T, precision=HI, preferred_element_type=F32)
        ddt_ref[...] = dda * arow_ref[...] + ddt1
        da_ref[...] += jnp.sum(dda * dt_ref[...], axis=0, keepdims=True)
        dds_ref[...] += dds

    col = pl.BlockSpec((L, LANES), lambda c: (nc - 1 - c, 0))
    vec = pl.BlockSpec((1, LANES), lambda c: (0, 0))
    return pl.pallas_call(
        body, name="ssd_bwd", grid=(nc,),
        in_specs=[pl.BlockSpec(memory_space=pltpu.SMEM), pl.BlockSpec((L, SSM_CONV_DIM), lambda c: (nc - 1 - c, 0)),
                  col, col, pl.BlockSpec((LANES, L), lambda c: (0, nc - 1 - c)),
                  pl.BlockSpec((1, SSM_HEADS, P, N), lambda c: (nc - 1 - c, 0, 0, 0)),
                  pl.BlockSpec((L, SSM_INNER), lambda c: (nc - 1 - c, 0)), vec],
        out_specs=[pl.BlockSpec((L, SSM_CONV_DIM), lambda c: (nc - 1 - c, 0)), col, vec, vec],
        out_shape=[jax.ShapeDtypeStruct((s, SSM_CONV_DIM), F32), jax.ShapeDtypeStruct((s, LANES), F32),
                   jax.ShapeDtypeStruct((1, LANES), F32), jax.ShapeDtypeStruct((1, LANES), F32)],
        scratch_shapes=[pltpu.VMEM((SSM_HEADS, P, N), F32)],
        compiler_params=_cp("arbitrary"),
    )(d_skip, xbc_act, dt_c, ac_c, ac_r, hprev_all, dy, a_row)


ROWS = 256
GW = SSM_INNER // SSM_GROUPS


def _rows(width, dtype=F32):
    return pl.BlockSpec((ROWS, width), lambda i: (i, 0))


def _vec(width):
    return pl.BlockSpec((1, width), lambda i: (0, 0))


def _gnorm_fwd(y, z, w):
    s = y.shape[0]

    def body(y_ref, z_ref, w_ref, o_ref):
        for g in range(SSM_GROUPS):
            cs = slice(g * GW, (g + 1) * GW)
            zz = z_ref[:, cs]
            u = y_ref[:, cs] * (zz * _sigmoid(zz))
            r = lax.rsqrt(jnp.mean(u * u, axis=1, keepdims=True) + RMS_EPS)
            o_ref[:, cs] = (u * r * w_ref[:, cs]).astype(o_ref.dtype)

    return pl.pallas_call(body, name="gnorm_fwd", grid=(s // ROWS,),
                          in_specs=[_rows(SSM_INNER), _rows(SSM_INNER), _vec(SSM_INNER)], out_specs=_rows(SSM_INNER),
                          out_shape=jax.ShapeDtypeStruct((s, SSM_INNER), BF16), compiler_params=_cp("parallel"))(y, z, w)


def _gnorm_bwd(y, z, w, do):
    s = y.shape[0]

    def body(y_ref, z_ref, w_ref, do_ref, dy_ref, dz_ref, dw_ref):
        @pl.when(pl.program_id(0) == 0)
        def _():
            dw_ref[...] = jnp.zeros_like(dw_ref)

        for g in range(SSM_GROUPS):
            cs = slice(g * GW, (g + 1) * GW)
            zz, yy, dd = z_ref[:, cs], y_ref[:, cs], do_ref[:, cs]
            sz, dsz = _silu_and_grad(zz)
            u = yy * sz
            r = lax.rsqrt(jnp.mean(u * u, axis=1, keepdims=True) + RMS_EPS)
            n = u * r
            dn = dd * w_ref[:, cs]
            dw_ref[:, cs] += jnp.sum(dd * n, axis=0, keepdims=True)
            du = r * (dn - n * jnp.mean(dn * n, axis=1, keepdims=True))
            dy_ref[:, cs] = du * sz
            dz_ref[:, cs] = (du * yy * dsz).astype(dz_ref.dtype)

    return pl.pallas_call(
        body, name="gnorm_bwd", grid=(s // ROWS,),
        in_specs=[_rows(SSM_INNER), _rows(SSM_INNER), _vec(SSM_INNER), _rows(SSM_INNER)],
        out_specs=[_rows(SSM_INNER), _rows(SSM_INNER), _vec(SSM_INNER)],
        out_shape=[jax.ShapeDtypeStruct((s, SSM_INNER), F32), jax.ShapeDtypeStruct((s, SSM_INNER), BF16),
                   jax.ShapeDtypeStruct((1, SSM_INNER), F32)],
        compiler_params=_cp("arbitrary"))(y, z, w, do)


def _mix_fwd(gl, bg, attn_d, ssm_d):
    s = gl.shape[0]
    d = D_MODEL

    def body(gl_ref, bg_ref, a_ref, m_ref, o_ref):
        g0 = _sigmoid(gl_ref[:, :d] + bg_ref[:, :d])
        g1 = _sigmoid(gl_ref[:, d:] + bg_ref[:, d:])
        o_ref[...] = (g0 * a_ref[...] + g1 * m_ref[...]).astype(o_ref.dtype)

    return pl.pallas_call(body, name="mix_fwd", grid=(s // ROWS,),
                          in_specs=[_rows(2 * d), _vec(2 * d), _rows(d), _rows(d)], out_specs=_rows(d),
                          out_shape=jax.ShapeDtypeStruct((s, d), BF16), compiler_params=_cp("parallel"))(
        gl, bg, attn_d, ssm_d)


def _mix_bwd(gl, bg, attn_d, ssm_d, dmix):
    s = gl.shape[0]
    d = D_MODEL

    def body(gl_ref, bg_ref, a_ref, m_ref, dm_ref, da_ref, ds_ref, dg_ref, db_ref):
        @pl.when(pl.program_id(0) == 0)
        def _():
            db_ref[...] = jnp.zeros_like(db_ref)

        g0 = _sigmoid(gl_ref[:, :d] + bg_ref[:, :d])
        g1 = _sigmoid(gl_ref[:, d:] + bg_ref[:, d:])
        dm = dm_ref[...]
        da_ref[...] = (dm * g0).astype(da_ref.dtype)
        ds_ref[...] = (dm * g1).astype(ds_ref.dtype)
        dl0 = dm * a_ref[...] * g0 * (1.0 - g0)
        dl1 = dm * m_ref[...] * g1 * (1.0 - g1)
        dg_ref[:, :d] = dl0.astype(dg_ref.dtype)
        dg_ref[:, d:] = dl1.astype(dg_ref.dtype)
        db_ref[:, :d] += jnp.sum(dl0, axis=0, keepdims=True)
        db_ref[:, d:] += jnp.sum(dl1, axis=0, keepdims=True)

    return pl.pallas_call(
        body, name="mix_bwd", grid=(s // ROWS,),
        in_specs=[_rows(2 * d), _vec(2 * d), _rows(d), _rows(d), _rows(d)],
        out_specs=[_rows(d), _rows(d), _rows(2 * d), _vec(2 * d)],
        out_shape=[jax.ShapeDtypeStruct((s, d), BF16), jax.ShapeDtypeStruct((s, d), BF16),
                   jax.ShapeDtypeStruct((s, 2 * d), BF16), jax.ShapeDtypeStruct((1, 2 * d), F32)],
        compiler_params=_cp("arbitrary"))(gl, bg, attn_d, ssm_d, dmix)


def _ln_stats(p):
    mu = jnp.mean(p, axis=1, keepdims=True)
    c = p - mu
    rstd = lax.rsqrt(jnp.mean(c * c, axis=1, keepdims=True) + LN_EPS)
    return c * rstd, rstd


def _ln_bwd(dy, xhat, rstd, g):
    dxh = dy * g
    return rstd * (dxh - jnp.mean(dxh, axis=1, keepdims=True) - xhat * jnp.mean(dxh * xhat, axis=1, keepdims=True))


def _ln1_fwd(x, mixed, g, b):
    s, d = x.shape

    def body(x_ref, m_ref, g_ref, b_ref, o_ref):
        xhat, _ = _ln_stats(DEEPNORM_ALPHA * x_ref[...] + m_ref[...])
        o_ref[...] = xhat * g_ref[...] + b_ref[...]

    return pl.pallas_call(body, name="ln1_fwd", grid=(s // ROWS,), in_specs=[_rows(d), _rows(d), _vec(d), _vec(d)],
                          out_specs=_rows(d), out_shape=jax.ShapeDtypeStruct((s, d), F32),
                          compiler_params=_cp("parallel"))(x, mixed, g, b)


def _ln2_loss(x1, h, target, g, b):
    s, d = x1.shape

    def body(x_ref, h_ref, t_ref, g_ref, b_ref, dp_ref, loss_ref, dg_ref, db_ref):
        @pl.when(pl.program_id(0) == 0)
        def _():
            loss_ref[...] = jnp.zeros_like(loss_ref)
            dg_ref[...] = jnp.zeros_like(dg_ref)
            db_ref[...] = jnp.zeros_like(db_ref)

        xhat, rstd = _ln_stats(DEEPNORM_ALPHA * x_ref[...] + h_ref[...])
        err = xhat * g_ref[...] + b_ref[...] - t_ref[...]
        part = 0.5 * jnp.sum(jnp.mean(err * err, axis=1, keepdims=True), axis=0, keepdims=True)
        loss_ref[...] += jnp.broadcast_to(part, loss_ref.shape)
        dy = err * (1.0 / d)
        dg_ref[...] += jnp.sum(dy * xhat, axis=0, keepdims=True)
        db_ref[...] += jnp.sum(dy, axis=0, keepdims=True)
        dp_ref[...] = _ln_bwd(dy, xhat, rstd, g_ref[...])

    return pl.pallas_call(
        body, name="ln2_loss", grid=(s // ROWS,), in_specs=[_rows(d), _rows(d), _rows(d), _vec(d), _vec(d)],
        out_specs=[_rows(d), _vec(LANES), _vec(d), _vec(d)],
        out_shape=[jax.ShapeDtypeStruct((s, d), F32), jax.ShapeDtypeStruct((1, LANES), F32),
                   jax.ShapeDtypeStruct((1, d), F32), jax.ShapeDtypeStruct((1, d), F32)],
        compiler_params=_cp("arbitrary"))(x1, h, target, g, b)


def _ln1_bwd(x, mixed, g, dpre2, dffn):
    s, d = x.shape

    def body(x_ref, m_ref, g_ref, d2_ref, df_ref, dp_ref, dr_ref, dg_ref, db_ref):
        @pl.when(pl.program_id(0) == 0)
        def _():
            dg_ref[...] = jnp.zeros_like(dg_ref)
            db_ref[...] = jnp.zeros_like(db_ref)

        xhat, rstd = _ln_stats(DEEPNORM_ALPHA * x_ref[...] + m_ref[...])
        dy = DEEPNORM_ALPHA * d2_ref[...] + df_ref[...]
        dg_ref[...] += jnp.sum(dy * xhat, axis=0, keepdims=True)
        db_ref[...] += jnp.sum(dy, axis=0, keepdims=True)
        dp = _ln_bwd(dy, xhat, rstd, g_ref[...])
        dp_ref[...] = dp
        dr_ref[...] = DEEPNORM_ALPHA * dp

    return pl.pallas_call(
        body, name="ln1_bwd", grid=(s // ROWS,), in_specs=[_rows(d), _rows(d), _vec(d), _rows(d), _rows(d)],
        out_specs=[_rows(d), _rows(d), _vec(d), _vec(d)],
        out_shape=[jax.ShapeDtypeStruct((s, d), F32), jax.ShapeDtypeStruct((s, d), F32),
                   jax.ShapeDtypeStruct((1, d), F32), jax.ShapeDtypeStruct((1, d), F32)],
        compiler_params=_cp("arbitrary"))(x, mixed, g, dpre2, dffn)


def _swiglu_fwd(gu):
    s = gu.shape[0]
    f = FFN_HIDDEN

    def body(g_ref, u_ref, o_ref):
        gg = g_ref[...]
        o_ref[...] = (gg * _sigmoid(gg) * u_ref[...]).astype(o_ref.dtype)

    return pl.pallas_call(
        body, name="swiglu_fwd", grid=(s // ROWS,),
        in_specs=[pl.BlockSpec((ROWS, f), lambda i: (i, 0)), pl.BlockSpec((ROWS, f), lambda i: (i, 1))],
        out_specs=_rows(f), out_shape=jax.ShapeDtypeStruct((s, f), BF16), compiler_params=_cp("parallel"))(gu, gu)


def _swiglu_bwd(gu, dact):
    s = gu.shape[0]
    f = FFN_HIDDEN

    def body(g_ref, u_ref, d_ref, o_ref):
        sg, dsg = _silu_and_grad(g_ref[...])
        dd = d_ref[...]
        o_ref[:, :f] = (dd * u_ref[...] * dsg).astype(o_ref.dtype)
        o_ref[:, f:] = (dd * sg).astype(o_ref.dtype)

    return pl.pallas_call(
        body, name="swiglu_bwd", grid=(s // ROWS,),
        in_specs=[pl.BlockSpec((ROWS, f), lambda i: (i, 0)), pl.BlockSpec((ROWS, f), lambda i: (i, 1)), _rows(f)],
        out_specs=_rows(2 * f), out_shape=jax.ShapeDtypeStruct((s, 2 * f), BF16),
        compiler_params=_cp("parallel"))(gu, gu, dact)


def _peer(k):
    x, y, c = lax.axis_index("x"), lax.axis_index("y"), lax.axis_index("c")
    kx, ky, kc = (k >> 2) & 1, (k >> 1) & 1, k & 1
    px = (1 - x) if kx else x
    py = (1 - y) if ky else y
    pc = (1 - c) if kc else c
    return (px, py, pc), 4 * px + 2 * py + pc


def _my_index():
    return 4 * lax.axis_index("x") + 2 * lax.axis_index("y") + lax.axis_index("c")


def _all_gather(parts):
    n = len(parts)

    def body(*refs):
        ins, outs = refs[:n], refs[n:2 * n]
        send_sems, recv_sems, local_sems = refs[2 * n:]
        me = _my_index()
        local = [pltpu.make_async_copy(ins[t], outs[t].at[me], local_sems.at[t]) for t in range(n)]
        for cp in local:
            cp.start()
        remote = []
        for k in range(1, N_DEV):
            peer, _ = _peer(k)
            for t in range(n):
                remote.append(pltpu.make_async_remote_copy(
                    src_ref=ins[t], dst_ref=outs[t].at[me], send_sem=send_sems.at[t, k - 1],
                    recv_sem=recv_sems.at[t, k - 1], device_id=peer, device_id_type=pl.DeviceIdType.MESH))
        for cp in remote:
            cp.start()
        for cp in remote:
            cp.wait()
        for cp in local:
            cp.wait()

    anyspec = pl.BlockSpec(memory_space=pl.ANY)
    return pl.pallas_call(
        body, name="all_gather", in_specs=[anyspec] * n, out_specs=[anyspec] * n,
        out_shape=[jax.ShapeDtypeStruct((N_DEV,) + p.shape, p.dtype) for p in parts],
        scratch_shapes=[pltpu.SemaphoreType.DMA((n, N_DEV - 1)), pltpu.SemaphoreType.DMA((n, N_DEV - 1)),
                        pltpu.SemaphoreType.DMA((n,))],
    )(*parts)


def _exchange(parts):
    n = len(parts)

    def body(*refs):
        ins, outs = refs[:n], refs[n:2 * n]
        send_sems, recv_sems, local_sems = refs[2 * n:]
        me = _my_index()
        local = [pltpu.make_async_copy(ins[t].at[me], outs[t].at[me], local_sems.at[t]) for t in range(n)]
        for cp in local:
            cp.start()
        remote = []
        for k in range(1, N_DEV):
            peer, pidx = _peer(k)
            for t in range(n):
                remote.append(pltpu.make_async_remote_copy(
                    src_ref=ins[t].at[pidx], dst_ref=outs[t].at[me], send_sem=send_sems.at[t, k - 1],
                    recv_sem=recv_sems.at[t, k - 1], device_id=peer, device_id_type=pl.DeviceIdType.MESH))
        for cp in remote:
            cp.start()
        for cp in remote:
            cp.wait()
        for cp in local:
            cp.wait()

    anyspec = pl.BlockSpec(memory_space=pl.ANY)
    return pl.pallas_call(
        body, name="grad_exchange", in_specs=[anyspec] * n, out_specs=[anyspec] * n,
        out_shape=[jax.ShapeDtypeStruct(p.shape, p.dtype) for p in parts],
        scratch_shapes=[pltpu.SemaphoreType.DMA((n, N_DEV - 1)), pltpu.SemaphoreType.DMA((n, N_DEV - 1)),
                        pltpu.SemaphoreType.DMA((n,))],
    )(*parts)


def _adamw(recv, w, m, v, name):
    r, c = w.shape
    br = _tile(r, 128)
    c1 = 1.0 / (1.0 - ADAM_B1 ** ADAM_STEP)
    c2 = 1.0 / (1.0 - ADAM_B2 ** ADAM_STEP)

    def body(r_ref, w_ref, m_ref, v_ref, g_ref, d_ref, mo_ref, vo_ref):
        g = r_ref[0].astype(F32)
        for k in range(1, N_DEV):
            g = g + r_ref[k].astype(F32)
        mn = ADAM_B1 * m_ref[...] + (1.0 - ADAM_B1) * g
        vn = ADAM_B2 * v_ref[...] + (1.0 - ADAM_B2) * (g * g)
        g_ref[...] = g
        mo_ref[...] = mn
        vo_ref[...] = vn
        d_ref[...] = -ADAM_LR * ((mn * c1) / (jnp.sqrt(vn * c2) + ADAM_EPS) + ADAM_WD * w_ref[...])

    blk = pl.BlockSpec((br, c), lambda i: (i, 0))
    return pl.pallas_call(
        body, name=name, grid=(r // br,),
        in_specs=[pl.BlockSpec((N_DEV, br, c), lambda i: (0, i, 0)), blk, blk, blk],
        out_specs=[blk] * 4, out_shape=[jax.ShapeDtypeStruct((r, c), F32)] * 4,
        compiler_params=_cp("parallel"))(recv, w, m, v)


def _lane_row(pairs):
    row = jnp.zeros((LANES,), F32)
    for lane0, vec in pairs:
        row = lax.dynamic_update_slice(row, vec.astype(F32), (lane0,))
    return row.reshape(1, LANES)


def _local_step(x, target, wts, small):
    s = x.shape[0]
    d = D_MODEL
    a = -jnp.exp(small["a_log"])
    bias_row = _lane_row([(DT_LANE0, small["dt_bias"]), (F_LANE0, small["b_forget"])])
    a_row = _lane_row([(DT_LANE0, a)])
    conv_b = small["conv_b"].reshape(1, -1)
    norm_w = small["ssm_norm_w"].reshape(1, -1)
    bg = small["b_gates"].reshape(1, -1)
    g1, b1 = small["ln1_g"].reshape(1, -1), small["ln1_b"].reshape(1, -1)
    g2, b2 = small["ln2_g"].reshape(1, -1), small["ln2_b"].reshape(1, -1)
    d_skip = small["d_skip"]
    xb = x.astype(BF16)

    qkv = _mm(xb, wts["qkv"], out_dtype=BF16, name="f_qkv")
    z = _mm(xb, wts["z"], name="f_z")
    xbc = _mm(xb, wts["xbc"], name="f_xbc")
    gl = _mm(xb, wts["gate"], name="f_gate")
    fd = _mm(xb, wts["fd"], name="f_fd")
    dt_c, ac_c, cf_c, dt_r, ac_r, cf_r = _stats_fwd(fd, bias_row, a_row)
    bk = _att_blocks(s)[1]
    ck4 = cf_r[F_LANE0:F_LANE0 + ATT_HEADS].reshape(N_HP, HP, s // bk, bk)
    attn, lse = _attention_fwd(qkv, ck4)
    attn_d = _mm(attn, wts["pa"], name="f_pa")
    xact = _conv_fwd(xbc, wts["conv"], conv_b)
    y, hprev = _ssd_fwd(xact, dt_c, ac_c, ac_r, d_skip)
    ssm = _gnorm_fwd(y, z, norm_w)
    ssm_d = _mm(ssm, wts["ps"], name="f_ps")
    mix = _mix_fwd(gl, bg, attn_d, ssm_d)
    mixed = _mm(mix, wts["out"], name="f_out")
    x1 = _ln1_fwd(x, mixed, g1, b1)
    gu = _mm(x1, wts["gu"], name="f_gu")
    act = _swiglu_fwd(gu)
    h = _mm(act, wts["down"], name="f_down")
    dpre2, loss_row, dg2, db2 = _ln2_loss(x1, h, target, g2, b2)

    d_act = _mm(dpre2, wts["down"], tb=True, name="b_down_x")
    dw_down = _mm(act, dpre2, ta=True, name="b_down_w")
    dgu = _swiglu_bwd(gu, d_act)
    dffn = _mm(dgu, wts["gu"], tb=True, name="b_gu_x")
    dw_gu = _mm(x1, dgu, ta=True, name="b_gu_w")
    dpre1, dxr, dg1, db1 = _ln1_bwd(x, mixed, g1, dpre2, dffn)
    dmix = _mm(dpre1, wts["out"], tb=True, name="b_out_x")
    dw_out = _mm(mix, dpre1, ta=True, name="b_out_w")
    dattn_d, dssm_d, dgl, dbg = _mix_bwd(gl, bg, attn_d, ssm_d, dmix)
    dssm = _mm(dssm_d, wts["ps"], tb=True, name="b_ps_x")
    dw_ps = _mm(ssm, dssm_d, ta=True, name="b_ps_w")
    dattn = _mm(dattn_d, wts["pa"], tb=True, name="b_pa_x")
    dw_pa = _mm(attn, dattn_d, ta=True, name="b_pa_w")
    dy, dz, dnw = _gnorm_bwd(y, z, norm_w, dssm)
    dxact, ddt, da_row, dds_row = _ssd_bwd(xact, dt_c, ac_c, ac_r, hprev, dy, d_skip, a_row)
    dpre_c, dconv_w, dconv_b = _conv_bwd_pre(xbc, wts["conv"], conv_b, dxact)
    dxbc = _conv_bwd_in(dpre_c, wts["conv"])
    st, do_b = _att_prep(dattn, attn, lse)
    dq, dk, dv, dck, dcq = _attention_bwd(qkv, ck4, st, do_b)
    dck_rows = jnp.zeros((LANES, s), F32).at[F_LANE0:F_LANE0 + ATT_HEADS].set(dck.reshape(ATT_HEADS, s))
    dcq_cols = jnp.zeros((s, LANES), F32).at[:, F_LANE0:F_LANE0 + ATT_HEADS].set(dcq[:, ::ATT_HEAD_DIM])
    dfd, dbias = _stats_bwd(fd, bias_row, ddt, dck_rows, dcq_cols)

    wq, wk, wv = wts["qkv"][:, :d], wts["qkv"][:, d:2 * d], wts["qkv"][:, 2 * d:]
    dx = dxr
    for i, (g_, w_) in enumerate(((dq, wq), (dk, wk), (dv, wv), (dz, wts["z"]), (dxbc, wts["xbc"]),
                                  (dgl, wts["gate"]), (dfd, wts["fd"]))):
        dx = _mm(g_, w_, tb=True, add=dx, name=f"b_in_x{i}")
    dw_in = [_mm(xb, g_, ta=True, name=f"b_in_w{i}") for i, g_ in enumerate((dq, dk, dv, dz, dxbc, dgl, dfd))]

    grads = dict(q=dw_in[0], k=dw_in[1], v=dw_in[2], z=dw_in[3], xbc=dw_in[4], gate=dw_in[5], fd=dw_in[6],
                 pa=dw_pa, ps=dw_ps, out=dw_out, gu=dw_gu, down=dw_down, conv=dconv_w)
    small_g = dict(
        b_forget=dbias[0, F_LANE0:F_LANE0 + ATT_HEADS], conv_b=dconv_b[0], dt_bias=dbias[0, :SSM_HEADS],
        a_log=da_row[0, :SSM_HEADS] * a, d_skip=dds_row[0, :SSM_HEADS], ssm_norm_w=dnw[0], b_gates=dbg[0],
        ln1_g=dg1[0], ln1_b=db1[0], ln2_g=dg2[0], ln2_b=db2[0])
    return loss_row[0, 0], dx, grads, small_g


BIG = ("w_in", "w_proj_attn", "w_proj_ssm", "w_out", "w_ffn_gate", "w_ffn_up", "w_ffn_down", "conv_w")
SMALL = ("b_forget", "conv_b", "dt_bias", "a_log", "d_skip", "ssm_norm_w", "b_gates", "ln1_g", "ln1_b", "ln2_g",
         "ln2_b")
SMALL_ROWS = 96
IN_SHARD = IN_WIDTH // N_DEV
IN_SEGMENTS = (("q", 0, 1024), ("k", 1024, 1024), ("v", 2048, 1024), ("f", 3072, ATT_HEADS), ("z", 3088, SSM_INNER),
               ("xbc", 5136, SSM_CONV_DIM), ("dt", 8208, SSM_HEADS), ("gate", 8240, 2 * D_MODEL))


def _cols_from_shards(shards, lo, hi):
    w = shards[0].shape[1]
    pieces = []
    for j in range(len(shards)):
        a, b = max(lo, j * w), min(hi, (j + 1) * w)
        if a < b:
            pieces.append(shards[j][:, a - j * w:b - j * w])
    return pieces[0] if len(pieces) == 1 else jnp.concatenate(pieces, axis=1)


def _shards_from_parts(parts, width):
    shards = []
    for j in range(N_DEV):
        lo, hi = j * width, (j + 1) * width
        pieces = []
        for mat, c0 in parts:
            a, b = max(lo, c0), min(hi, c0 + mat.shape[1])
            if a < b:
                pieces.append(mat[:, a - c0:b - c0])
        shards.append(pieces[0] if len(pieces) == 1 else jnp.concatenate(pieces, axis=1))
    return shards


def _pack_small(vals):
    flat = jnp.concatenate([vals[n].reshape(-1) for n in SMALL])
    return jnp.pad(flat, (0, SMALL_ROWS * LANES - flat.shape[0])).reshape(SMALL_ROWS, LANES)


def _unpack_small(pack, shapes):
    flat = pack.reshape(-1)
    out, off = {}, 0
    for n in SMALL:
        sz = math.prod(shapes[n])
        out[n] = flat[off:off + sz].reshape(shapes[n])
        off += sz
    return out


def kernel(x, w_in, b_forget, conv_w, conv_b, dt_bias, a_log, d_skip, ssm_norm_w, w_proj_attn, w_proj_ssm, b_gates, w_out, ln1_g, ln1_b, w_ffn_gate, w_ffn_up, w_ffn_down, ln2_g, ln2_b, loss_target, m_w_in, m_b_forget, m_conv_w, m_conv_b, m_dt_bias, m_a_log, m_d_skip, m_ssm_norm_w, m_w_proj_attn, m_w_proj_ssm, m_b_gates, m_w_out, m_ln1_g, m_ln1_b, m_w_ffn_gate, m_w_ffn_up, m_w_ffn_down, m_ln2_g, m_ln2_b, v_w_in, v_b_forget, v_conv_w, v_conv_b, v_dt_bias, v_a_log, v_d_skip, v_ssm_norm_w, v_w_proj_attn, v_w_proj_ssm, v_b_gates, v_w_out, v_ln1_g, v_ln1_b, v_w_ffn_gate, v_w_ffn_up, v_w_ffn_down, v_ln2_g, v_ln2_b):
    args = dict(locals())
    d, f = D_MODEL, FFN_HIDDEN
    big_w = {n: args[n][0] for n in BIG}
    small_w = {n: args[n][0] for n in SMALL}
    big_shapes = {n: args[n].shape for n in BIG}
    small_shapes = {n: args[n].shape for n in SMALL}

    sent = [big_w[n].astype(BF16) for n in BIG[:-1]] + [big_w["conv_w"]]
    full = dict(zip(BIG, _all_gather(sent)))
    in_shards = [full["w_in"][j] for j in range(N_DEV)]
    seg = {n: _cols_from_shards(in_shards, c0, c0 + w) for n, c0, w in IN_SEGMENTS}
    wfd = jnp.concatenate([seg["dt"], seg["f"], jnp.zeros((d, LANES - SSM_HEADS - ATT_HEADS), BF16)], axis=1)
    wts = dict(
        qkv=jnp.concatenate([seg["q"], seg["k"], seg["v"]], axis=1), z=seg["z"], xbc=seg["xbc"], gate=seg["gate"],
        fd=wfd, pa=full["w_proj_attn"].reshape(d, d), ps=full["w_proj_ssm"].reshape(SSM_INNER, d),
        out=full["w_out"].reshape(d, d),
        gu=jnp.concatenate([full["w_ffn_gate"][j] for j in range(N_DEV)]
                           + [full["w_ffn_up"][j] for j in range(N_DEV)], axis=1),
        down=full["w_ffn_down"].reshape(f, d),
        conv=jnp.concatenate([full["conv_w"][j] for j in range(N_DEV)], axis=1))

    loss_part, grad_x, g, small_g = _local_step(x[0], loss_target[0], wts, small_w)
    loss = lax.psum(loss_part, ("x", "y", "c"))

    gfd = g["fd"]
    in_parts = dict(q=g["q"], k=g["k"], v=g["v"], f=gfd[:, F_LANE0:F_LANE0 + ATT_HEADS], z=g["z"], xbc=g["xbc"],
                    dt=gfd[:, DT_LANE0:DT_LANE0 + SSM_HEADS], gate=g["gate"])
    col_shards = dict(
        w_in=_shards_from_parts([(in_parts[n], c0) for n, c0, _ in IN_SEGMENTS], IN_SHARD),
        w_ffn_gate=_shards_from_parts([(g["gu"][:, :f], 0)], f // N_DEV),
        w_ffn_up=_shards_from_parts([(g["gu"][:, f:], 0)], f // N_DEV),
        conv_w=_shards_from_parts([(g["conv"], 0)], SSM_CONV_DIM // N_DEV))
    by_dest = {n: jnp.stack([s_.astype(F32 if n == "conv_w" else BF16) for s_ in col_shards[n]])
               for n in col_shards}
    for n, key in (("w_proj_attn", "pa"), ("w_proj_ssm", "ps"), ("w_out", "out"), ("w_ffn_down", "down")):
        by_dest[n] = g[key].astype(BF16).reshape((N_DEV,) + big_shapes[n][1:])
    small_pack = _pack_small(small_g)
    recv = _exchange([by_dest[n] for n in BIG] + [jnp.broadcast_to(small_pack, (N_DEV,) + small_pack.shape)])

    outs = {}
    for t, n in enumerate(BIG):
        shp = big_shapes[n]
        res4 = _adamw(recv[t], big_w[n], args["m_" + n][0], args["v_" + n][0], name="adamw_" + n)
        outs[n] = [r.reshape(shp) for r in res4]
    small4 = _adamw(recv[len(BIG)], _pack_small(small_w), _pack_small({n: args["m_" + n][0] for n in SMALL}),
                    _pack_small({n: args["v_" + n][0] for n in SMALL}), name="adamw_small")
    small_out = [_unpack_small(p, small_shapes) for p in small4]
    for n in SMALL:
        outs[n] = [so[n] for so in small_out]

    order = ("w_in", "b_forget", "conv_w", "conv_b", "dt_bias", "a_log", "d_skip", "ssm_norm_w", "w_proj_attn",
             "w_proj_ssm", "b_gates", "w_out", "ln1_g", "ln1_b", "w_ffn_gate", "w_ffn_up", "w_ffn_down", "ln2_g",
             "ln2_b")
    res = [loss, grad_x[None]]
    for i in range(4):
        res += [outs[n][i] for n in order]
    return tuple(res)
```

```python
import functools
import math

import jax
import jax.numpy as jnp
from jax import lax
from jax.experimental import pallas as pl
from jax.experimental.pallas import tpu as pltpu

F32 = jnp.float32
BF16 = jnp.bfloat16

N_DEV = 8
D_MODEL = 1024
ATT_HEADS = 16
ATT_HEAD_DIM = 64
SSM_INNER = 2048
SSM_HEADS = 32
SSM_HEAD_DIM = 64
SSM_GROUPS = 4
SSM_HEADS_PER_GROUP = 8
SSM_STATE = 128
SSM_CONV = 4
SSM_CHUNK = 128
SSM_CONV_DIM = 3072
FFN_HIDDEN = 2816
IN_WIDTH = 10288
DEEPNORM_ALPHA = 2.0 ** 0.25
LN_EPS = 1e-5
RMS_EPS = 1e-5
ADAM_LR, ADAM_B1, ADAM_B2, ADAM_EPS, ADAM_WD, ADAM_STEP = 0.001, 0.9, 0.999, 1e-08, 0.01, 10
ATT_SCALE = 1.0 / math.sqrt(ATT_HEAD_DIM)

LANES = 128
VMEM_LIMIT = 56 * 1024 * 1024
NEG = -1e30

DT_LANE0 = 0
F_LANE0 = 32
HI = lax.Precision.HIGHEST


def _cp(*sem):
    return pltpu.CompilerParams(dimension_semantics=sem, vmem_limit_bytes=VMEM_LIMIT)


def _tile(n, cap=1408):
    for t in (1408, 1024, 512, 384, 256, 128):
        if t <= cap and n % t == 0:
            return t
    return n


def _sigmoid(x):
    return 1.0 / (1.0 + jnp.exp(-x))


def _mm(a, b, *, ta=False, tb=False, out_dtype=F32, add=None, name):
    m, k = (a.shape[1], a.shape[0]) if ta else a.shape
    n = b.shape[0] if tb else b.shape[1]
    assert (b.shape[1] if tb else b.shape[0]) == k
    tm, tn, tk = _tile(m), _tile(n), _tile(k)
    nk = k // tk
    dims = (((0,) if ta else (1,), (1,) if tb else (0,)), ((), ()))

    def body(*refs):
        if add is None:
            a_ref, b_ref, o_ref, acc_ref = refs
        else:
            a_ref, b_ref, c_ref, o_ref, acc_ref = refs
        kk = pl.program_id(2)

        @pl.when(kk == 0)
        def _():
            acc_ref[...] = jnp.zeros_like(acc_ref)

        acc_ref[...] += lax.dot_general(a_ref[...].astype(BF16), b_ref[...].astype(BF16), dims,
                                        preferred_element_type=F32)

        @pl.when(kk == nk - 1)
        def _():
            r = acc_ref[...]
            if add is not None:
                r = r + c_ref[...]
            o_ref[...] = r.astype(o_ref.dtype)

    a_spec = pl.BlockSpec((tk, tm), lambda i, j, kk: (kk, i)) if ta else pl.BlockSpec((tm, tk), lambda i, j, kk: (i, kk))
    b_spec = pl.BlockSpec((tn, tk), lambda i, j, kk: (j, kk)) if tb else pl.BlockSpec((tk, tn), lambda i, j, kk: (kk, j))
    o_spec = pl.BlockSpec((tm, tn), lambda i, j, kk: (i, j))
    in_specs, args = [a_spec, b_spec], [a, b]
    if add is not None:
        in_specs.append(o_spec)
        args.append(add)
    return pl.pallas_call(
        body, name=name, grid=(m // tm, n // tn, nk), in_specs=in_specs, out_specs=o_spec,
        out_shape=jax.ShapeDtypeStruct((m, n), out_dtype),
        scratch_shapes=[pltpu.VMEM((tm, tn), F32)],
        compiler_params=_cp("parallel", "parallel", "arbitrary"),
    )(*args)


def _tri(n, lower=True):
    r = lax.broadcasted_iota(jnp.int32, (n, n), 0)
    c = lax.broadcasted_iota(jnp.int32, (n, n), 1)
    return jnp.where((r >= c) if lower else (c >= r), 1.0, 0.0).astype(F32)


def _stats_fwd(fd, bias_row, a_row):
    s = fd.shape[0]
    blk = SSM_CHUNK

    def body(fd_ref, bias_ref, a_ref, dt_ref, ac_ref, cf_ref, dtr_ref, acr_ref, cfr_ref, carry_ref):
        @pl.when(pl.program_id(0) == 0)
        def _():
            carry_ref[...] = jnp.zeros_like(carry_ref)

        v = fd_ref[...] + bias_ref[...]
        dt = jnp.maximum(v, 0.0) + jnp.log(1.0 + jnp.exp(-jnp.abs(v)))
        lf = jnp.minimum(v, 0.0) - jnp.log(1.0 + jnp.exp(-jnp.abs(v)))
        tri = _tri(blk)
        ac = jnp.dot(tri, dt * a_ref[...], precision=HI, preferred_element_type=F32)
        cf = jnp.dot(tri, lf, precision=HI, preferred_element_type=F32) + carry_ref[0:1, :]
        carry_ref[...] = carry_ref[...] + jnp.sum(lf, axis=0, keepdims=True)
        dt_ref[...] = dt
        ac_ref[...] = ac
        cf_ref[...] = cf
        dtr_ref[...] = dt.T
        acr_ref[...] = ac.T
        cfr_ref[...] = cf.T

    col = pl.BlockSpec((blk, LANES), lambda i: (i, 0))
    row = pl.BlockSpec((LANES, blk), lambda i: (0, i))
    vec = pl.BlockSpec((1, LANES), lambda i: (0, 0))
    return pl.pallas_call(
        body, name="stats_fwd", grid=(s // blk,), in_specs=[col, vec, vec],
        out_specs=[col, col, col, row, row, row],
        out_shape=[jax.ShapeDtypeStruct((s, LANES), F32)] * 3 + [jax.ShapeDtypeStruct((LANES, s), F32)] * 3,
        scratch_shapes=[pltpu.VMEM((8, LANES), F32)],
        compiler_params=_cp("arbitrary"),
    )(fd, bias_row, a_row)


def _stats_bwd(fd, bias_row, ddt, dck_rows, dcq_cols):
    s = fd.shape[0]
    blk = SSM_CHUNK
    nb = s // blk

    def body(fd_ref, bias_ref, ddt_ref, dck_ref, dcq_ref, o_ref, db_ref, carry_ref):
        @pl.when(pl.program_id(0) == 0)
        def _():
            carry_ref[...] = jnp.zeros_like(carry_ref)
            db_ref[...] = jnp.zeros_like(db_ref)

        v = fd_ref[...] + bias_ref[...]
        dcum = dck_ref[...].T + dcq_ref[...]
        dlf = jnp.dot(_tri(blk, lower=False), dcum, precision=HI, preferred_element_type=F32) + carry_ref[0:1, :]
        carry_ref[...] = carry_ref[...] + jnp.sum(dcum, axis=0, keepdims=True)
        lane = lax.broadcasted_iota(jnp.int32, v.shape, 1)
        g = jnp.where(lane < F_LANE0, ddt_ref[...] * _sigmoid(v), dlf * _sigmoid(-v))
        g = jnp.where(lane < F_LANE0 + ATT_HEADS, g, 0.0)
        o_ref[...] = g.astype(o_ref.dtype)
        db_ref[...] += jnp.sum(g, axis=0, keepdims=True)

    col = pl.BlockSpec((blk, LANES), lambda i: (nb - 1 - i, 0))
    row = pl.BlockSpec((LANES, blk), lambda i: (0, nb - 1 - i))
    vec = pl.BlockSpec((1, LANES), lambda i: (0, 0))
    return pl.pallas_call(
        body, name="stats_bwd", grid=(nb,), in_specs=[col, vec, col, row, col], out_specs=[col, vec],
        out_shape=[jax.ShapeDtypeStruct((s, LANES), BF16), jax.ShapeDtypeStruct((1, LANES), F32)],
        scratch_shapes=[pltpu.VMEM((8, LANES), F32)],
        compiler_params=_cp("arbitrary"),
    )(fd, bias_row, ddt, dck_rows, dcq_cols)


HP = LANES // ATT_HEAD_DIM
N_HP = ATT_HEADS // HP


def _att_blocks(s):
    return (512, 1024) if s % 1024 == 0 and s >= 4096 else (64, 128)


_QK = (((1,), (1,)), ((), ()))
_HALF = ATT_HEAD_DIM // 2


def _head_cols(a):
    return slice(a * ATT_HEAD_DIM, (a + 1) * ATT_HEAD_DIM)


def _causal(shape, off):
    r = lax.broadcasted_iota(jnp.int32, shape, 0)
    c = lax.broadcasted_iota(jnp.int32, shape, 1)
    return c <= r + off


def _attention_fwd(qkv, ck4):
    s = qkv.shape[0]
    bq, bk = _att_blocks(s)
    nq, nk = s // bq, s // bk

    def body(q_ref, k_ref, v_ref, ck_ref, o_ref, lse_ref):
        i = pl.program_id(1)
        n_full = (i * bq) // bk
        qs = [(q_ref[:, _head_cols(a)].astype(F32) * ATT_SCALE).astype(BF16) for a in range(HP)]

        def step(j, carry, off=None):
            ks = pl.ds(pl.multiple_of(j * bk, bk), bk)
            out = []
            for a in range(HP):
                m, l, acc = carry[a]
                sc = lax.dot_general(qs[a], k_ref[ks, _head_cols(a)], _QK, preferred_element_type=F32)
                sc = sc - ck_ref[0, a, pl.ds(j, 1), :]
                if off is not None:
                    sc = jnp.where(_causal(sc.shape, off), sc, NEG)
                m_new = jnp.maximum(m, jnp.max(sc, axis=1, keepdims=True))
                alpha = jnp.exp(m - m_new)
                p = jnp.exp(sc - m_new)
                l = alpha * l + jnp.sum(p, axis=1, keepdims=True)
                acc = alpha * acc + jnp.dot(p.astype(BF16), v_ref[ks, _head_cols(a)], preferred_element_type=F32)
                out.append((m_new, l, acc))
            return tuple(out)

        init = tuple((jnp.full((bq, 1), NEG, F32), jnp.zeros((bq, 1), F32), jnp.zeros((bq, ATT_HEAD_DIM), F32))
                     for _ in range(HP))
        carry = lax.fori_loop(0, n_full, step, init)
        carry = step(n_full, carry, off=i * bq - n_full * bk)
        for a in range(HP):
            m, l, acc = carry[a]
            o_ref[:, _head_cols(a)] = acc / l
            lse_ref[:, _head_cols(a)] = jnp.broadcast_to(m + jnp.log(l), (bq, ATT_HEAD_DIM))

    q_spec = pl.BlockSpec((bq, LANES), lambda h, i: (i, h))
    return pl.pallas_call(
        body, name="att_fwd", grid=(N_HP, nq),
        in_specs=[q_spec, pl.BlockSpec((s, LANES), lambda h, i: (0, N_HP + h)),
                  pl.BlockSpec((s, LANES), lambda h, i: (0, 2 * N_HP + h)),
                  pl.BlockSpec((1, HP, nk, bk), lambda h, i: (h, 0, 0, 0))],
        out_specs=[q_spec, q_spec], out_shape=[jax.ShapeDtypeStruct((s, D_MODEL), F32)] * 2,
        compiler_params=_cp("parallel", "arbitrary"),
    )(qkv, qkv, qkv, ck4)


def _att_prep(do, o, lse_rep):
    s = do.shape[0]
    bs = _tile(s, 512)

    def body(do_ref, o_ref, lse_ref, st_ref, dob_ref):
        r = lax.broadcasted_iota(jnp.int32, (LANES, LANES), 0) // ATT_HEAD_DIM
        c = lax.broadcasted_iota(jnp.int32, (LANES, LANES), 1) // ATT_HEAD_DIM
        e = jnp.where(r == c, 1.0, 0.0).astype(F32)
        lane = lax.broadcasted_iota(jnp.int32, (bs, LANES), 1)
        for p in range(D_MODEL // LANES):
            cs = slice(p * LANES, (p + 1) * LANES)
            dd = do_ref[:, cs]
            delta = jnp.dot(dd * o_ref[:, cs], e, precision=HI, preferred_element_type=F32)
            st_ref[:, cs] = jnp.where(lane % ATT_HEAD_DIM < _HALF, lse_ref[:, cs], delta)
            dob_ref[:, cs] = dd.astype(BF16)

    spec = pl.BlockSpec((bs, D_MODEL), lambda i: (i, 0))
    return pl.pallas_call(body, name="att_prep", grid=(s // bs,), in_specs=[spec, spec, spec], out_specs=[spec, spec],
                          out_shape=[jax.ShapeDtypeStruct((s, D_MODEL), F32), jax.ShapeDtypeStruct((s, D_MODEL), BF16)],
                          compiler_params=_cp("parallel"))(do, o, lse_rep)


def _attention_bwd(qkv, ck4, st, do_b):
    s = qkv.shape[0]
    bq, bk = _att_blocks(s)
    nq, nk, per = s // bq, s // bk, bk // bq
    _T = (((0,), (0,)), ((), ()))

    def body(q_ref, k_ref, v_ref, ck_ref, st_ref, do_ref, dq_ref, dk_ref, dv_ref, dck_ref, dcq_ref,
             dk_acc, dv_acc, dck_acc):
        j = pl.program_id(1)

        @pl.when(j == 0)
        def _():
            dq_ref[...] = jnp.zeros_like(dq_ref)
            dcq_ref[...] = jnp.zeros_like(dcq_ref)

        dk_acc[...] = jnp.zeros_like(dk_acc)
        dv_acc[...] = jnp.zeros_like(dv_acc)
        dck_acc[...] = jnp.zeros_like(dck_acc)

        def step(i, off=None):
            rows = pl.ds(pl.multiple_of(i * bq, bq), bq)
            for a in range(HP):
                cs = _head_cols(a)
                q = (q_ref[rows, cs].astype(F32) * ATT_SCALE).astype(BF16)
                k = k_ref[:, cs]
                do_a = do_ref[rows, cs]
                sc = lax.dot_general(q, k, _QK, preferred_element_type=F32) - ck_ref[0, a, pl.ds(j, 1), :]
                if off is not None:
                    sc = jnp.where(_causal(sc.shape, off), sc, NEG)
                p = jnp.exp(sc - st_ref[rows, a * ATT_HEAD_DIM:a * ATT_HEAD_DIM + 1])
                dp = lax.dot_general(do_a, v_ref[:, cs], _QK, preferred_element_type=F32)
                ds = p * (dp - st_ref[rows, a * ATT_HEAD_DIM + _HALF:a * ATT_HEAD_DIM + _HALF + 1])
                ds_b = ds.astype(BF16)
                dv_acc[a] += lax.dot_general(p.astype(BF16), do_a, _T, preferred_element_type=F32)
                dk_acc[a] += lax.dot_general(ds_b, q, _T, preferred_element_type=F32)
                dq_ref[rows, cs] += jnp.dot(ds_b, k, preferred_element_type=F32) * ATT_SCALE
                dck_acc[a] -= jnp.sum(ds, axis=0, keepdims=True)
                dcq_ref[rows, cs] += jnp.broadcast_to(jnp.sum(ds, axis=1, keepdims=True), (bq, ATT_HEAD_DIM))

        for t in range(per):
            step(j * per + t, off=t * bq)

        def full(i, c):
            step(i)
            return c

        lax.fori_loop((j + 1) * per, nq, full, 0)
        for a in range(HP):
            dk_ref[:, _head_cols(a)] = dk_acc[a].astype(dk_ref.dtype)
            dv_ref[:, _head_cols(a)] = dv_acc[a].astype(dv_ref.dtype)
            dck_ref[0, a, pl.ds(j, 1), :] = dck_acc[a]

    res = pl.BlockSpec((s, LANES), lambda h, j: (0, h))
    ck_spec = pl.BlockSpec((1, HP, nk, bk), lambda h, j: (h, 0, 0, 0))
    kout = pl.BlockSpec((bk, LANES), lambda h, j: (j, h))
    return pl.pallas_call(
        body, name="att_bwd", grid=(N_HP, nk),
        in_specs=[res, pl.BlockSpec((bk, LANES), lambda h, j: (j, N_HP + h)),
                  pl.BlockSpec((bk, LANES), lambda h, j: (j, 2 * N_HP + h)), ck_spec, res, res],
        out_specs=[res, kout, kout, ck_spec, res],
        out_shape=[jax.ShapeDtypeStruct((s, D_MODEL), F32), jax.ShapeDtypeStruct((s, D_MODEL), BF16),
                   jax.ShapeDtypeStruct((s, D_MODEL), BF16), jax.ShapeDtypeStruct((N_HP, HP, nk, bk), F32),
                   jax.ShapeDtypeStruct((s, D_MODEL), F32)],
        scratch_shapes=[pltpu.VMEM((HP, bk, ATT_HEAD_DIM), F32), pltpu.VMEM((HP, bk, ATT_HEAD_DIM), F32),
                        pltpu.VMEM((HP, 1, bk), F32)],
        compiler_params=_cp("parallel", "arbitrary"),
    )(qkv, qkv, qkv, ck4, st, do_b)


def _silu_and_grad(x):
    sg = _sigmoid(x)
    return x * sg, sg * (1.0 + x * (1.0 - sg))


def _conv_pre(cur, halo, w_ref, b_ref, first):
    halo = jnp.where(first, 0.0, halo)
    row = lax.broadcasted_iota(jnp.int32, cur.shape, 0)
    shifted = []
    for k in range(SSM_CONV):
        sh = SSM_CONV - 1 - k
        if sh == 0:
            shifted.append(cur)
            continue
        r = pltpu.roll(cur, sh, 0)
        hr = pltpu.roll(halo, sh, 0)
        top = jnp.where(row[0:8] < sh, hr, r[0:8])
        shifted.append(jnp.concatenate([top, r[8:]], axis=0))
    pre = b_ref[...] + sum(w_ref[k:k + 1, :] * shifted[k] for k in range(SSM_CONV))
    return pre, shifted


def _conv_specs(s, bs, bc):
    cur = pl.BlockSpec((bs, bc), lambda j, i: (i, j))
    halo = pl.BlockSpec((8, bc), lambda j, i: (jnp.maximum(i * (bs // 8) - 1, 0), j))
    w = pl.BlockSpec((SSM_CONV, bc), lambda j, i: (0, j))
    b = pl.BlockSpec((1, bc), lambda j, i: (0, j))
    return cur, halo, w, b


def _conv_fwd(xbc, w, b):
    s, c = xbc.shape
    bs, bc = _tile(s, 512), 1024

    def body(x_ref, h_ref, w_ref, b_ref, o_ref):
        pre, _ = _conv_pre(x_ref[...], h_ref[...], w_ref, b_ref, pl.program_id(1) == 0)
        o_ref[...] = pre * _sigmoid(pre)

    cur, halo, ws, bsp = _conv_specs(s, bs, bc)
    return pl.pallas_call(body, name="conv_fwd", grid=(c // bc, s // bs), in_specs=[cur, halo, ws, bsp],
                          out_specs=cur, out_shape=jax.ShapeDtypeStruct((s, c), F32),
                          compiler_params=_cp("parallel", "parallel"))(xbc, xbc, w, b)


def _conv_bwd_pre(xbc, w, b, dact):
    s, c = xbc.shape
    bs, bc = _tile(s, 512), 1024

    def body(x_ref, h_ref, w_ref, b_ref, g_ref, dp_ref, dw_ref, db_ref):
        @pl.when(pl.program_id(1) == 0)
        def _():
            dw_ref[...] = jnp.zeros_like(dw_ref)
            db_ref[...] = jnp.zeros_like(db_ref)

        pre, shifted = _conv_pre(x_ref[...], h_ref[...], w_ref, b_ref, pl.program_id(1) == 0)
        dpre = g_ref[...] * _silu_and_grad(pre)[1]
        dp_ref[...] = dpre
        db_ref[...] += jnp.sum(dpre, axis=0, keepdims=True)
        for k in range(SSM_CONV):
            dw_ref[k:k + 1, :] += jnp.sum(dpre * shifted[k], axis=0, keepdims=True)

    cur, halo, ws, bsp = _conv_specs(s, bs, bc)
    return pl.pallas_call(
        body, name="conv_bwd_pre", grid=(c // bc, s // bs), in_specs=[cur, halo, ws, bsp, cur],
        out_specs=[cur, ws, bsp],
        out_shape=[jax.ShapeDtypeStruct((s, c), F32), jax.ShapeDtypeStruct((SSM_CONV, c), F32),
                   jax.ShapeDtypeStruct((1, c), F32)],
        compiler_params=_cp("parallel", "arbitrary"))(xbc, xbc, w, b, dact)


def _conv_bwd_in(dpre, w):
    s, c = dpre.shape
    bs, bc = _tile(s, 512), 1024
    nb = s // bs

    def body(g_ref, n_ref, w_ref, o_ref):
        cur = g_ref[...]
        nxt = jnp.where(pl.program_id(1) == nb - 1, 0.0, n_ref[...])
        row = lax.broadcasted_iota(jnp.int32, cur.shape, 0)
        acc = w_ref[SSM_CONV - 1:SSM_CONV, :] * cur
        for sh in range(1, SSM_CONV):
            r = pltpu.roll(cur, bs - sh, 0)
            nr = pltpu.roll(nxt, 8 - sh, 0)
            bot = jnp.where(row[0:8] >= 8 - sh, nr, r[bs - 8:])
            acc = acc + w_ref[SSM_CONV - 1 - sh:SSM_CONV - sh, :] * jnp.concatenate([r[:bs - 8], bot], axis=0)
        o_ref[...] = acc.astype(o_ref.dtype)

    cur = pl.BlockSpec((bs, bc), lambda j, i: (i, j))
    nxt = pl.BlockSpec((8, bc), lambda j, i: (jnp.minimum((i + 1) * (bs // 8), s // 8 - 1), j))
    ws = pl.BlockSpec((SSM_CONV, bc), lambda j, i: (0, j))
    return pl.pallas_call(body, name="conv_bwd_in", grid=(c // bc, nb), in_specs=[cur, nxt, ws], out_specs=cur,
                          out_shape=jax.ShapeDtypeStruct((s, c), BF16),
                          compiler_params=_cp("parallel", "parallel"))(dpre, dpre, w)


def _dotT(a, b):
    return lax.dot_general(a.astype(BF16), b.astype(BF16), (((1,), (1,)), ((), ())), preferred_element_type=F32)


def _Tdot(a, b):
    return lax.dot_general(a.astype(BF16), b.astype(BF16), (((0,), (0,)), ((), ())), preferred_element_type=F32)


def _dot(a, b):
    return jnp.dot(a.astype(BF16), b.astype(BF16), preferred_element_type=F32)


def _ssd_head(xbc_ref, dt_ref, ac_ref, acr_ref, h):
    L = SSM_CHUNK
    xs = xbc_ref[:, h * SSM_HEAD_DIM:(h + 1) * SSM_HEAD_DIM]
    dt_col = dt_ref[:, h:h + 1]
    a_col = ac_ref[:, h:h + 1]
    a_row = acr_ref[h:h + 1, :]
    li = lax.broadcasted_iota(jnp.int32, (L, L), 0)
    si = lax.broadcasted_iota(jnp.int32, (L, L), 1)
    decay = jnp.exp(jnp.where(li >= si, a_col - a_row, NEG))
    a_last = ac_ref[L - 1:L, h:h + 1]
    return xs, dt_col, a_col, a_last, decay


def _ssd_fwd(xbc_act, dt_c, ac_c, ac_r, d_skip):
    s = xbc_act.shape[0]
    L, P, N, G, R = SSM_CHUNK, SSM_HEAD_DIM, SSM_STATE, SSM_GROUPS, SSM_HEADS_PER_GROUP
    nc = s // L

    def body(dsk_ref, xbc_ref, dt_ref, ac_ref, acr_ref, y_ref, hp_ref, st_ref):
        @pl.when(pl.program_id(0) == 0)
        def _():
            st_ref[...] = jnp.zeros_like(st_ref)

        for g in range(G):
            b_g = xbc_ref[:, SSM_INNER + g * N:SSM_INNER + (g + 1) * N]
            c_g = xbc_ref[:, SSM_INNER + G * N + g * N:SSM_INNER + G * N + (g + 1) * N]
            cb = _dotT(c_g, b_g)
            for r in range(R):
                h = g * R + r
                xs, dt_col, a_col, a_last, decay = _ssd_head(xbc_ref, dt_ref, ac_ref, acr_ref, h)
                xdt = xs * dt_col
                hprev = st_ref[h]
                y = _dot(cb * decay, xdt) + jnp.exp(a_col) * _dotT(c_g, hprev) + dsk_ref[h] * xs
                y_ref[:, h * P:(h + 1) * P] = y
                hp_ref[0, h] = hprev
                st_ref[h] = hprev * jnp.exp(a_last) + _Tdot(xdt * jnp.exp(a_last - a_col), b_g)

    col = pl.BlockSpec((L, LANES), lambda c: (c, 0))
    return pl.pallas_call(
        body, name="ssd_fwd", grid=(nc,),
        in_specs=[pl.BlockSpec(memory_space=pltpu.SMEM), pl.BlockSpec((L, SSM_CONV_DIM), lambda c: (c, 0)), col, col,
                  pl.BlockSpec((LANES, L), lambda c: (0, c))],
        out_specs=[pl.BlockSpec((L, SSM_INNER), lambda c: (c, 0)),
                   pl.BlockSpec((1, SSM_HEADS, P, N), lambda c: (c, 0, 0, 0))],
        out_shape=[jax.ShapeDtypeStruct((s, SSM_INNER), F32), jax.ShapeDtypeStruct((nc, SSM_HEADS, P, N), F32)],
        scratch_shapes=[pltpu.VMEM((SSM_HEADS, P, N), F32)],
        compiler_params=_cp("arbitrary"),
    )(d_skip, xbc_act, dt_c, ac_c, ac_r)


def _ssd_bwd(xbc_act, dt_c, ac_c, ac_r, hprev_all, dy, d_skip, a_row):
    s = xbc_act.shape[0]
    L, P, N, G, R = SSM_CHUNK, SSM_HEAD_DIM, SSM_STATE, SSM_GROUPS, SSM_HEADS_PER_GROUP
    nc = s // L

    def body(dsk_ref, xbc_ref, dt_ref, ac_ref, acr_ref, hp_ref, dy_ref, arow_ref,
             dx_ref, ddt_ref, da_ref, dds_ref, dh_ref):
        @pl.when(pl.program_id(0) == 0)
        def _():
            dh_ref[...] = jnp.zeros_like(dh_ref)
            da_ref[...] = jnp.zeros_like(da_ref)
            dds_ref[...] = jnp.zeros_like(dds_ref)

        lane = lax.broadcasted_iota(jnp.int32, (L, LANES), 1)
        sub = lax.broadcasted_iota(jnp.int32, (LANES, L), 0)
        rowi = lax.broadcasted_iota(jnp.int32, (L, 1), 0)
        lane1 = lax.broadcasted_iota(jnp.int32, (1, LANES), 1)
        da_c = jnp.zeros((L, LANES), F32)
        da_r = jnp.zeros((LANES, L), F32)
        ddt1 = jnp.zeros((L, LANES), F32)
        dds = jnp.zeros((1, LANES), F32)
        for g in range(G):
            b_g = xbc_ref[:, SSM_INNER + g * N:SSM_INNER + (g + 1) * N]
            c_g = xbc_ref[:, SSM_INNER + G * N + g * N:SSM_INNER + G * N + (g + 1) * N]
            cb = _dotT(c_g, b_g)
            dcb = jnp.zeros((L, L), F32)
            db_g = jnp.zeros((L, N), F32)
            dc_g = jnp.zeros((L, N), F32)
            for r in range(R):
                h = g * R + r
                xs, dt_col, a_col, a_last, decay = _ssd_head(xbc_ref, dt_ref, ac_ref, acr_ref, h)
                gy = dy_ref[:, h * P:(h + 1) * P]
                xdt = xs * dt_col
                hprev = hp_ref[0, h]
                dhn = dh_ref[h]
                e_a = jnp.exp(a_col)
                e_last = jnp.exp(a_last)
                e_col = jnp.exp(a_last - a_col)
                m = cb * decay
                yoff = e_a * _dotT(c_g, hprev)
                da_col = jnp.sum(gy * yoff, axis=1, keepdims=True)
                dc_g = dc_g + e_a * _dot(gy, hprev)
                dhp = _Tdot(gy * e_a, c_g) + dhn * e_last
                da_last = jnp.sum(jnp.sum(dhn * hprev, axis=1, keepdims=True), axis=0, keepdims=True) * e_last
                xds = _dot(xdt, dhn)
                db_g = db_g + e_col * xds
                de_e = jnp.sum(xds * b_g, axis=1, keepdims=True) * e_col
                da_col = da_col - de_e
                da_last = da_last + jnp.sum(de_e, axis=0, keepdims=True)
                dxdt = e_col * _dotT(b_g, dhn)
                dm = _dotT(gy, xdt)
                dxdt = dxdt + _Tdot(m, gy)
                dcb = dcb + dm * decay
                w = dm * m
                da_col = da_col + jnp.sum(w, axis=1, keepdims=True) + jnp.where(rowi == L - 1, da_last, 0.0)
                da_c = jnp.where(lane == h, da_col, da_c)
                da_r = jnp.where(sub == h, jnp.sum(w, axis=0, keepdims=True), da_r)
                ddt1 = jnp.where(lane == h, jnp.sum(dxdt * xs, axis=1, keepdims=True), ddt1)
                dds = jnp.where(lane1 == h, jnp.sum(jnp.sum(gy * xs, axis=1, keepdims=True), axis=0, keepdims=True),
                                dds)
                dx_ref[:, h * P:(h + 1) * P] = dxdt * dt_col + dsk_ref[h] * gy
                dh_ref[h] = dhp
            dx_ref[:, SSM_INNER + g * N:SSM_INNER + (g + 1) * N] = db_g + _Tdot(dcb, c_g)
            dx_ref[:, SSM_INNER + G * N + g * N:SSM_INNER + G * N + (g + 1) * N] = dc_g + _dot(dcb, b_g)
        dda = jnp.dot(_tri(L, lower=False), da_c - da_r.T, precision=HI, preferred_element_type=F32)
        ddt_ref[...] = dda * arow_ref[...] + ddt1
        da_ref[...] += jnp.sum(dda * dt_ref[...], axis=0, keepdims=True)
        dds_ref[...] += dds

    col = pl.BlockSpec((L, LANES), lambda c: (nc - 1 - c, 0))
    vec = pl.BlockSpec((1, LANES), lambda c: (0, 0))
    return pl.pallas_call(
        body, name="ssd_bwd", grid=(nc,),
        in_specs=[pl.BlockSpec(memory_space=pltpu.SMEM), pl.BlockSpec((L, SSM_CONV_DIM), lambda c: (nc - 1 - c, 0)),
                  col, col, pl.BlockSpec((LANES, L), lambda c: (0, nc - 1 - c)),
                  pl.BlockSpec((1, SSM_HEADS, P, N), lambda c: (nc - 1 - c, 0, 0, 0)),
                  pl.BlockSpec((L, SSM_INNER), lambda c: (nc - 1 - c, 0)), vec],
        out_specs=[pl.BlockSpec((L, SSM_CONV_DIM), lambda c: (nc - 1 - c, 0)), col, vec, vec],
        out_shape=[jax.ShapeDtypeStruct((s, SSM_CONV_DIM), F32), jax.ShapeDtypeStruct((s, LANES), F32),
                   jax.ShapeDtypeStruct((1, LANES), F32), jax.ShapeDtypeStruct((1, LANES), F32)],
        scratch_shapes=[pltpu.VMEM((SSM_HEADS, P, N), F32)],
        compiler_params=_cp("arbitrary"),
    )(d_skip, xbc_act, dt_c, ac_c, ac_r, hprev_all, dy, a_row)


N_PAIR = SSM_HEADS // HP
PAIRS_PER_GROUP = SSM_HEADS_PER_GROUP // HP


def _pair_consts():
    L = SSM_CHUNK
    lane = lax.broadcasted_iota(jnp.int32, (L, LANES), 1)
    lane1 = lax.broadcasted_iota(jnp.int32, (1, LANES), 1)
    li = lax.broadcasted_iota(jnp.int32, (L, L), 0)
    si = lax.broadcasted_iota(jnp.int32, (L, L), 1)
    return lane >= ATT_HEAD_DIM, lane1 >= ATT_HEAD_DIM, li, si


def _ssd_pair_fwd(xbc_act, ac_c, dt_r, ac_r, dsk_pair):
    s = xbc_act.shape[0]
    L, N, G = SSM_CHUNK, SSM_STATE, SSM_GROUPS
    nc = s // L

    def body(xbc_ref, ac_ref, dtr_ref, acr_ref, dsk_ref, y_ref, hp_ref, st_ref):
        @pl.when(pl.program_id(0) == 0)
        def _():
            st_ref[...] = jnp.zeros_like(st_ref)

        upper, up1, li, si = _pair_consts()
        for g in range(G):
            b_g = xbc_ref[:, SSM_INNER + g * N:SSM_INNER + (g + 1) * N]
            c_g = xbc_ref[:, SSM_INNER + G * N + g * N:SSM_INNER + G * N + (g + 1) * N]
            cb = _dotT(c_g, b_g)
            b_t = b_g.T
            for q in range(PAIRS_PER_GROUP):
                pp = g * PAIRS_PER_GROUP + q
                cols = slice(pp * LANES, (pp + 1) * LANES)
                xs = xbc_ref[:, cols]
                ht = st_ref[pp]
                hp_ref[0, pp] = ht
                y = dsk_ref[pp:pp + 1, :] * xs
                s_new = jnp.zeros((N, LANES), F32)
                ea, el = [], []
                for a in range(HP):
                    h = HP * pp + a
                    acol = jnp.broadcast_to(ac_ref[:, h:h + 1], (L, LANES))
                    arow, dtrow = acr_ref[h:h + 1, :], dtr_ref[h:h + 1, :]
                    alast = ac_ref[L - 1:L, h:h + 1]
                    decay = jnp.exp(jnp.where(li >= si, acol - arow, NEG))
                    xs_a = jnp.where(upper == (a == 1), xs, 0.0)
                    y = y + _dot(cb * decay * dtrow, xs_a)
                    s_new = s_new + _dot(b_t * (dtrow * jnp.exp(alast - arow)), xs_a)
                    ea.append(jnp.exp(acol))
                    el.append(jnp.exp(alast))
                y_ref[:, cols] = y + jnp.where(upper, ea[1], ea[0]) * _dot(c_g, ht)
                st_ref[pp] = ht * jnp.where(up1, el[1], el[0]) + s_new

    col = pl.BlockSpec((L, LANES), lambda c: (c, 0))
    row = pl.BlockSpec((LANES, L), lambda c: (0, c))
    return pl.pallas_call(
        body, name="ssd_fwd", grid=(nc,),
        in_specs=[pl.BlockSpec((L, SSM_CONV_DIM), lambda c: (c, 0)), col, row, row,
                  pl.BlockSpec((N_PAIR, LANES), lambda c: (0, 0))],
        out_specs=[pl.BlockSpec((L, SSM_INNER), lambda c: (c, 0)),
                   pl.BlockSpec((1, N_PAIR, N, LANES), lambda c: (c, 0, 0, 0))],
        out_shape=[jax.ShapeDtypeStruct((s, SSM_INNER), F32), jax.ShapeDtypeStruct((nc, N_PAIR, N, LANES), F32)],
        scratch_shapes=[pltpu.VMEM((N_PAIR, N, LANES), F32)],
        compiler_params=_cp("arbitrary"),
    )(xbc_act, ac_c, dt_r, ac_r, dsk_pair)


def _ssd_pair_bwd(xbc_act, dt_c, ac_c, dt_r, ac_r, hprev_all, dy, dsk_pair, a_row):
    s = xbc_act.shape[0]
    L, N, G = SSM_CHUNK, SSM_STATE, SSM_GROUPS
    nc = s // L
    rev = lambda c: nc - 1 - c

    def body(xbc_ref, dt_ref, ac_ref, dtr_ref, acr_ref, hp_ref, dy_ref, dsk_ref, arow_ref,
             dx_ref, ddt_ref, da_ref, dds_ref, dh_ref):
        @pl.when(pl.program_id(0) == 0)
        def _():
            dh_ref[...] = jnp.zeros_like(dh_ref)
            da_ref[...] = jnp.zeros_like(da_ref)
            dds_ref[...] = jnp.zeros_like(dds_ref)

        upper, up1, li, si = _pair_consts()
        lane = lax.broadcasted_iota(jnp.int32, (L, LANES), 1)
        sub = lax.broadcasted_iota(jnp.int32, (LANES, L), 0)
        lastrow = lax.broadcasted_iota(jnp.int32, (L, LANES), 0) == L - 1
        da_c = jnp.zeros((L, LANES), F32)
        da_r = jnp.zeros((LANES, L), F32)
        ddt_r = jnp.zeros((LANES, L), F32)
        for g in range(G):
            b_g = xbc_ref[:, SSM_INNER + g * N:SSM_INNER + (g + 1) * N]
            c_g = xbc_ref[:, SSM_INNER + G * N + g * N:SSM_INNER + G * N + (g + 1) * N]
            cb, cb_t = _dotT(c_g, b_g), _dotT(b_g, c_g)
            b_t, c_t = b_g.T, c_g.T
            dcb = jnp.zeros((L, L), F32)
            db_t = jnp.zeros((N, L), F32)
            dc = jnp.zeros((L, N), F32)
            for q in range(PAIRS_PER_GROUP):
                pp = g * PAIRS_PER_GROUP + q
                cols = slice(pp * LANES, (pp + 1) * LANES)
                xs, gy = xbc_ref[:, cols], dy_ref[:, cols]
                ht, dhn = hp_ref[0, pp], dh_ref[pp]
                acol = [jnp.broadcast_to(ac_ref[:, HP * pp + a:HP * pp + a + 1], (L, LANES)) for a in range(HP)]
                alast = [ac_ref[L - 1:L, HP * pp + a:HP * pp + a + 1] for a in range(HP)]
                ea = jnp.where(upper, jnp.exp(acol[1]), jnp.exp(acol[0]))
                el = jnp.where(up1, jnp.exp(alast[1]), jnp.exp(alast[0]))
                ge = gy * ea
                dc = dc + _dotT(ge, ht)
                dh_ref[pp] = _dot(c_t, ge) + dhn * el
                t_off = (ge * _dot(c_g, ht)).astype(BF16)
                hsum = jnp.sum(dhn * ht, axis=0, keepdims=True)
                dxs = dsk_ref[pp:pp + 1, :] * gy
                dds_ref[pp:pp + 1, :] += jnp.sum(gy * xs, axis=0, keepdims=True)
                for a in range(HP):
                    h = HP * pp + a
                    mine, mine1 = upper == (a == 1), up1 == (a == 1)
                    arow, dtrow = acr_ref[h:h + 1, :], dtr_ref[h:h + 1, :]
                    dtcol = jnp.broadcast_to(dt_ref[:, h:h + 1], (L, LANES))
                    xs_a, gy_a = jnp.where(mine, xs, 0.0), jnp.where(mine, gy, 0.0)
                    dhn_a = jnp.where(mine1, dhn, 0.0)
                    e_row = jnp.exp(alast[a] - arow)
                    w_row = dtrow * e_row
                    xd_t = _dotT(dhn_a, xs_a)
                    db_t = db_t + xd_t * w_row
                    dw = jnp.sum(b_t * xd_t, axis=0, keepdims=True)
                    de_e = dw * w_row
                    dal = (jnp.sum(jnp.where(mine1, hsum, 0.0), axis=1, keepdims=True) * jnp.exp(alast[a])
                           + jnp.sum(de_e, axis=1, keepdims=True))
                    dxs = dxs + _dot(b_g, dhn_a) * (dtcol * jnp.exp(alast[a] - acol[a]))
                    decay = jnp.exp(jnp.where(li >= si, acol[a] - arow, NEG))
                    decay_t = jnp.exp(jnp.where(si >= li, arow - acol[a], NEG))
                    m = cb * decay
                    dmdt = _dotT(gy_a, xs_a)
                    dxs = dxs + _dot(cb_t * decay_t * dtcol, gy_a)
                    dm = dmdt * dtrow
                    dcb = dcb + dm * decay
                    wb = (dm * m).astype(BF16)
                    onehot = jnp.where(lane == h, 1.0, 0.0).astype(BF16)
                    da_c = (da_c + jnp.dot(wb, onehot, preferred_element_type=F32)
                            + jnp.dot(jnp.where(mine, t_off, 0.0).astype(BF16), onehot, preferred_element_type=F32)
                            + jnp.where(lastrow & (lane == h), dal, 0.0))
                    da_r = jnp.where(sub == h, -(jnp.sum(wb.astype(F32), axis=0, keepdims=True) + de_e), da_r)
                    ddt_r = jnp.where(sub == h, dw * e_row + jnp.sum(dmdt * m, axis=0, keepdims=True), ddt_r)
                dx_ref[:, cols] = dxs
            dx_ref[:, SSM_INNER + g * N:SSM_INNER + (g + 1) * N] = (db_t + _dot(c_t, dcb)).T
            dx_ref[:, SSM_INNER + G * N + g * N:SSM_INNER + G * N + (g + 1) * N] = dc + _dot(dcb, b_g)
        dda = jnp.dot(_tri(L, lower=False), da_c + da_r.T, precision=HI, preferred_element_type=F32)
        ddt_ref[...] = dda * arow_ref[...] + ddt_r.T
        da_ref[...] += jnp.sum(dda * dt_ref[...], axis=0, keepdims=True)

    col = pl.BlockSpec((L, LANES), lambda c: (rev(c), 0))
    row = pl.BlockSpec((LANES, L), lambda c: (0, rev(c)))
    vec = pl.BlockSpec((1, LANES), lambda c: (0, 0))
    pairs = pl.BlockSpec((N_PAIR, LANES), lambda c: (0, 0))
    return pl.pallas_call(
        body, name="ssd_bwd", grid=(nc,),
        in_specs=[pl.BlockSpec((L, SSM_CONV_DIM), lambda c: (rev(c), 0)), col, col, row, row,
                  pl.BlockSpec((1, N_PAIR, N, LANES), lambda c: (rev(c), 0, 0, 0)),
                  pl.BlockSpec((L, SSM_INNER), lambda c: (rev(c), 0)), pairs, vec],
        out_specs=[pl.BlockSpec((L, SSM_CONV_DIM), lambda c: (rev(c), 0)), col, vec, pairs],
        out_shape=[jax.ShapeDtypeStruct((s, SSM_CONV_DIM), F32), jax.ShapeDtypeStruct((s, LANES), F32),
                   jax.ShapeDtypeStruct((1, LANES), F32), jax.ShapeDtypeStruct((N_PAIR, LANES), F32)],
        scratch_shapes=[pltpu.VMEM((N_PAIR, N, LANES), F32)],
        compiler_params=_cp("arbitrary"),
    )(xbc_act, dt_c, ac_c, dt_r, ac_r, hprev_all, dy, dsk_pair, a_row)


ROWS = 256
GW = SSM_INNER // SSM_GROUPS


def _rows(width, dtype=F32):
    return pl.BlockSpec((ROWS, width), lambda i: (i, 0))


def _vec(width):
    return pl.BlockSpec((1, width), lambda i: (0, 0))


def _gnorm_fwd(y, z, w):
    s = y.shape[0]

    def body(y_ref, z_ref, w_ref, o_ref):
        for g in range(SSM_GROUPS):
            cs = slice(g * GW, (g + 1) * GW)
            zz = z_ref[:, cs]
            u = y_ref[:, cs] * (zz * _sigmoid(zz))
            r = lax.rsqrt(jnp.mean(u * u, axis=1, keepdims=True) + RMS_EPS)
            o_ref[:, cs] = (u * r * w_ref[:, cs]).astype(o_ref.dtype)

    return pl.pallas_call(body, name="gnorm_fwd", grid=(s // ROWS,),
                          in_specs=[_rows(SSM_INNER), _rows(SSM_INNER), _vec(SSM_INNER)], out_specs=_rows(SSM_INNER),
                          out_shape=jax.ShapeDtypeStruct((s, SSM_INNER), BF16), compiler_params=_cp("parallel"))(y, z, w)


def _gnorm_bwd(y, z, w, do):
    s = y.shape[0]

    def body(y_ref, z_ref, w_ref, do_ref, dy_ref, dz_ref, dw_ref):
        @pl.when(pl.program_id(0) == 0)
        def _():
            dw_ref[...] = jnp.zeros_like(dw_ref)

        for g in range(SSM_GROUPS):
            cs = slice(g * GW, (g + 1) * GW)
            zz, yy, dd = z_ref[:, cs], y_ref[:, cs], do_ref[:, cs]
            sz, dsz = _silu_and_grad(zz)
            u = yy * sz
            r = lax.rsqrt(jnp.mean(u * u, axis=1, keepdims=True) + RMS_EPS)
            n = u * r
            dn = dd * w_ref[:, cs]
            dw_ref[:, cs] += jnp.sum(dd * n, axis=0, keepdims=True)
            du = r * (dn - n * jnp.mean(dn * n, axis=1, keepdims=True))
            dy_ref[:, cs] = du * sz
            dz_ref[:, cs] = (du * yy * dsz).astype(dz_ref.dtype)

    return pl.pallas_call(
        body, name="gnorm_bwd", grid=(s // ROWS,),
        in_specs=[_rows(SSM_INNER), _rows(SSM_INNER), _vec(SSM_INNER), _rows(SSM_INNER)],
        out_specs=[_rows(SSM_INNER), _rows(SSM_INNER), _vec(SSM_INNER)],
        out_shape=[jax.ShapeDtypeStruct((s, SSM_INNER), F32), jax.ShapeDtypeStruct((s, SSM_INNER), BF16),
                   jax.ShapeDtypeStruct((1, SSM_INNER), F32)],
        compiler_params=_cp("arbitrary"))(y, z, w, do)


def _mix_fwd(gl, bg, attn_d, ssm_d):
    s = gl.shape[0]
    d = D_MODEL

    def body(gl_ref, bg_ref, a_ref, m_ref, o_ref):
        g0 = _sigmoid(gl_ref[:, :d] + bg_ref[:, :d])
        g1 = _sigmoid(gl_ref[:, d:] + bg_ref[:, d:])
        o_ref[...] = (g0 * a_ref[...] + g1 * m_ref[...]).astype(o_ref.dtype)

    return pl.pallas_call(body, name="mix_fwd", grid=(s // ROWS,),
                          in_specs=[_rows(2 * d), _vec(2 * d), _rows(d), _rows(d)], out_specs=_rows(d),
                          out_shape=jax.ShapeDtypeStruct((s, d), BF16), compiler_params=_cp("parallel"))(
        gl, bg, attn_d, ssm_d)


def _mix_bwd(gl, bg, attn_d, ssm_d, dmix):
    s = gl.shape[0]
    d = D_MODEL

    def body(gl_ref, bg_ref, a_ref, m_ref, dm_ref, da_ref, ds_ref, dg_ref, db_ref):
        @pl.when(pl.program_id(0) == 0)
        def _():
            db_ref[...] = jnp.zeros_like(db_ref)

        g0 = _sigmoid(gl_ref[:, :d] + bg_ref[:, :d])
        g1 = _sigmoid(gl_ref[:, d:] + bg_ref[:, d:])
        dm = dm_ref[...]
        da_ref[...] = (dm * g0).astype(da_ref.dtype)
        ds_ref[...] = (dm * g1).astype(ds_ref.dtype)
        dl0 = dm * a_ref[...] * g0 * (1.0 - g0)
        dl1 = dm * m_ref[...] * g1 * (1.0 - g1)
        dg_ref[:, :d] = dl0.astype(dg_ref.dtype)
        dg_ref[:, d:] = dl1.astype(dg_ref.dtype)
        db_ref[:, :d] += jnp.sum(dl0, axis=0, keepdims=True)
        db_ref[:, d:] += jnp.sum(dl1, axis=0, keepdims=True)

    return pl.pallas_call(
        body, name="mix_bwd", grid=(s // ROWS,),
        in_specs=[_rows(2 * d), _vec(2 * d), _rows(d), _rows(d), _rows(d)],
        out_specs=[_rows(d), _rows(d), _rows(2 * d), _vec(2 * d)],
        out_shape=[jax.ShapeDtypeStruct((s, d), BF16), jax.ShapeDtypeStruct((s, d), BF16),
                   jax.ShapeDtypeStruct((s, 2 * d), BF16), jax.ShapeDtypeStruct((1, 2 * d), F32)],
        compiler_params=_cp("arbitrary"))(gl, bg, attn_d, ssm_d, dmix)


def _ln_stats(p):
    mu = jnp.mean(p, axis=1, keepdims=True)
    c = p - mu
    rstd = lax.rsqrt(jnp.mean(c * c, axis=1, keepdims=True) + LN_EPS)
    return c * rstd, rstd


def _ln_bwd(dy, xhat, rstd, g):
    dxh = dy * g
    return rstd * (dxh - jnp.mean(dxh, axis=1, keepdims=True) - xhat * jnp.mean(dxh * xhat, axis=1, keepdims=True))


def _ln1_fwd(x, mixed, g, b):
    s, d = x.shape

    def body(x_ref, m_ref, g_ref, b_ref, o_ref):
        xhat, _ = _ln_stats(DEEPNORM_ALPHA * x_ref[...] + m_ref[...])
        o_ref[...] = xhat * g_ref[...] + b_ref[...]

    return pl.pallas_call(body, name="ln1_fwd", grid=(s // ROWS,), in_specs=[_rows(d), _rows(d), _vec(d), _vec(d)],
                          out_specs=_rows(d), out_shape=jax.ShapeDtypeStruct((s, d), F32),
                          compiler_params=_cp("parallel"))(x, mixed, g, b)


def _ln2_loss(x1, h, target, g, b):
    s, d = x1.shape

    def body(x_ref, h_ref, t_ref, g_ref, b_ref, dp_ref, loss_ref, dg_ref, db_ref):
        @pl.when(pl.program_id(0) == 0)
        def _():
            loss_ref[...] = jnp.zeros_like(loss_ref)
            dg_ref[...] = jnp.zeros_like(dg_ref)
            db_ref[...] = jnp.zeros_like(db_ref)

        xhat, rstd = _ln_stats(DEEPNORM_ALPHA * x_ref[...] + h_ref[...])
        err = xhat * g_ref[...] + b_ref[...] - t_ref[...]
        part = 0.5 * jnp.sum(jnp.mean(err * err, axis=1, keepdims=True), axis=0, keepdims=True)
        loss_ref[...] += jnp.broadcast_to(part, loss_ref.shape)
        dy = err * (1.0 / d)
        dg_ref[...] += jnp.sum(dy * xhat, axis=0, keepdims=True)
        db_ref[...] += jnp.sum(dy, axis=0, keepdims=True)
        dp_ref[...] = _ln_bwd(dy, xhat, rstd, g_ref[...])

    return pl.pallas_call(
        body, name="ln2_loss", grid=(s // ROWS,), in_specs=[_rows(d), _rows(d), _rows(d), _vec(d), _vec(d)],
        out_specs=[_rows(d), _vec(LANES), _vec(d), _vec(d)],
        out_shape=[jax.ShapeDtypeStruct((s, d), F32), jax.ShapeDtypeStruct((1, LANES), F32),
                   jax.ShapeDtypeStruct((1, d), F32), jax.ShapeDtypeStruct((1, d), F32)],
        compiler_params=_cp("arbitrary"))(x1, h, target, g, b)


def _ln1_bwd(x, mixed, g, dpre2, dffn):
    s, d = x.shape

    def body(x_ref, m_ref, g_ref, d2_ref, df_ref, dp_ref, dr_ref, dg_ref, db_ref):
        @pl.when(pl.program_id(0) == 0)
        def _():
            dg_ref[...] = jnp.zeros_like(dg_ref)
            db_ref[...] = jnp.zeros_like(db_ref)

        xhat, rstd = _ln_stats(DEEPNORM_ALPHA * x_ref[...] + m_ref[...])
        dy = DEEPNORM_ALPHA * d2_ref[...] + df_ref[...]
        dg_ref[...] += jnp.sum(dy * xhat, axis=0, keepdims=True)
        db_ref[...] += jnp.sum(dy, axis=0, keepdims=True)
        dp = _ln_bwd(dy, xhat, rstd, g_ref[...])
        dp_ref[...] = dp
        dr_ref[...] = DEEPNORM_ALPHA * dp

    return pl.pallas_call(
        body, name="ln1_bwd", grid=(s // ROWS,), in_specs=[_rows(d), _rows(d), _vec(d), _rows(d), _rows(d)],
        out_specs=[_rows(d), _rows(d), _vec(d), _vec(d)],
        out_shape=[jax.ShapeDtypeStruct((s, d), F32), jax.ShapeDtypeStruct((s, d), F32),
                   jax.ShapeDtypeStruct((1, d), F32), jax.ShapeDtypeStruct((1, d), F32)],
        compiler_params=_cp("arbitrary"))(x, mixed, g, dpre2, dffn)


def _swiglu_fwd(gu):
    s = gu.shape[0]
    f = FFN_HIDDEN

    def body(g_ref, u_ref, o_ref):
        gg = g_ref[...]
        o_ref[...] = (gg * _sigmoid(gg) * u_ref[...]).astype(o_ref.dtype)

    return pl.pallas_call(
        body, name="swiglu_fwd", grid=(s // ROWS,),
        in_specs=[pl.BlockSpec((ROWS, f), lambda i: (i, 0)), pl.BlockSpec((ROWS, f), lambda i: (i, 1))],
        out_specs=_rows(f), out_shape=jax.ShapeDtypeStruct((s, f), BF16), compiler_params=_cp("parallel"))(gu, gu)


def _swiglu_bwd(gu, dact):
    s = gu.shape[0]
    f = FFN_HIDDEN

    def body(g_ref, u_ref, d_ref, o_ref):
        sg, dsg = _silu_and_grad(g_ref[...])
        dd = d_ref[...]
        o_ref[:, :f] = (dd * u_ref[...] * dsg).astype(o_ref.dtype)
        o_ref[:, f:] = (dd * sg).astype(o_ref.dtype)

    return pl.pallas_call(
        body, name="swiglu_bwd", grid=(s // ROWS,),
        in_specs=[pl.BlockSpec((ROWS, f), lambda i: (i, 0)), pl.BlockSpec((ROWS, f), lambda i: (i, 1)), _rows(f)],
        out_specs=_rows(2 * f), out_shape=jax.ShapeDtypeStruct((s, 2 * f), BF16),
        compiler_params=_cp("parallel"))(gu, gu, dact)


def _peer(k):
    x, y, c = lax.axis_index("x"), lax.axis_index("y"), lax.axis_index("c")
    kx, ky, kc = (k >> 2) & 1, (k >> 1) & 1, k & 1
    px = (1 - x) if kx else x
    py = (1 - y) if ky else y
    pc = (1 - c) if kc else c
    return (px, py, pc), 4 * px + 2 * py + pc


def _my_index():
    return 4 * lax.axis_index("x") + 2 * lax.axis_index("y") + lax.axis_index("c")


def _all_gather(parts):
    n = len(parts)

    def body(*refs):
        ins, outs = refs[:n], refs[n:2 * n]
        send_sems, recv_sems, local_sems = refs[2 * n:]
        me = _my_index()
        local = [pltpu.make_async_copy(ins[t], outs[t].at[me], local_sems.at[t]) for t in range(n)]
        for cp in local:
            cp.start()
        remote = []
        for k in range(1, N_DEV):
            peer, _ = _peer(k)
            for t in range(n):
                remote.append(pltpu.make_async_remote_copy(
                    src_ref=ins[t], dst_ref=outs[t].at[me], send_sem=send_sems.at[t, k - 1],
                    recv_sem=recv_sems.at[t, k - 1], device_id=peer, device_id_type=pl.DeviceIdType.MESH))
        for cp in remote:
            cp.start()
        for cp in remote:
            cp.wait()
        for cp in local:
            cp.wait()

    anyspec = pl.BlockSpec(memory_space=pl.ANY)
    return pl.pallas_call(
        body, name="all_gather", in_specs=[anyspec] * n, out_specs=[anyspec] * n,
        out_shape=[jax.ShapeDtypeStruct((N_DEV,) + p.shape, p.dtype) for p in parts],
        scratch_shapes=[pltpu.SemaphoreType.DMA((n, N_DEV - 1)), pltpu.SemaphoreType.DMA((n, N_DEV - 1)),
                        pltpu.SemaphoreType.DMA((n,))],
    )(*parts)


def _exchange(parts):
    n = len(parts)

    def body(*refs):
        ins, outs = refs[:n], refs[n:2 * n]
        send_sems, recv_sems, local_sems = refs[2 * n:]
        me = _my_index()
        local = [pltpu.make_async_copy(ins[t].at[me], outs[t].at[me], local_sems.at[t]) for t in range(n)]
        for cp in local:
            cp.start()
        remote = []
        for k in range(1, N_DEV):
            peer, pidx = _peer(k)
            for t in range(n):
                remote.append(pltpu.make_async_remote_copy(
                    src_ref=ins[t].at[pidx], dst_ref=outs[t].at[me], send_sem=send_sems.at[t, k - 1],
                    recv_sem=recv_sems.at[t, k - 1], device_id=peer, device_id_type=pl.DeviceIdType.MESH))
        for cp in remote:
            cp.start()
        for cp in remote:
            cp.wait()
        for cp in local:
            cp.wait()

    anyspec = pl.BlockSpec(memory_space=pl.ANY)
    return pl.pallas_call(
        body, name="grad_exchange", in_specs=[anyspec] * n, out_specs=[anyspec] * n,
        out_shape=[jax.ShapeDtypeStruct(p.shape, p.dtype) for p in parts],
        scratch_shapes=[pltpu.SemaphoreType.DMA((n, N_DEV - 1)), pltpu.SemaphoreType.DMA((n, N_DEV - 1)),
                        pltpu.SemaphoreType.DMA((n,))],
    )(*parts)


def _adamw(recv, w, m, v, name):
    r, c = w.shape
    br = _tile(r, 128)
    c1 = 1.0 / (1.0 - ADAM_B1 ** ADAM_STEP)
    c2 = 1.0 / (1.0 - ADAM_B2 ** ADAM_STEP)

    def body(r_ref, w_ref, m_ref, v_ref, g_ref, d_ref, mo_ref, vo_ref):
        g = r_ref[0].astype(F32)
        for k in range(1, N_DEV):
            g = g + r_ref[k].astype(F32)
        mn = ADAM_B1 * m_ref[...] + (1.0 - ADAM_B1) * g
        vn = ADAM_B2 * v_ref[...] + (1.0 - ADAM_B2) * (g * g)
        g_ref[...] = g
        mo_ref[...] = mn
        vo_ref[...] = vn
        d_ref[...] = -ADAM_LR * ((mn * c1) / (jnp.sqrt(vn * c2) + ADAM_EPS) + ADAM_WD * w_ref[...])

    blk = pl.BlockSpec((br, c), lambda i: (i, 0))
    return pl.pallas_call(
        body, name=name, grid=(r // br,),
        in_specs=[pl.BlockSpec((N_DEV, br, c), lambda i: (0, i, 0)), blk, blk, blk],
        out_specs=[blk] * 4, out_shape=[jax.ShapeDtypeStruct((r, c), F32)] * 4,
        compiler_params=_cp("parallel"))(recv, w, m, v)


def _lane_row(pairs):
    row = jnp.zeros((LANES,), F32)
    for lane0, vec in pairs:
        row = lax.dynamic_update_slice(row, vec.astype(F32), (lane0,))
    return row.reshape(1, LANES)


def _local_step(x, target, wts, small):
    s = x.shape[0]
    d = D_MODEL
    a = -jnp.exp(small["a_log"])
    bias_row = _lane_row([(DT_LANE0, small["dt_bias"]), (F_LANE0, small["b_forget"])])
    a_row = _lane_row([(DT_LANE0, a)])
    conv_b = small["conv_b"].reshape(1, -1)
    norm_w = small["ssm_norm_w"].reshape(1, -1)
    bg = small["b_gates"].reshape(1, -1)
    g1, b1 = small["ln1_g"].reshape(1, -1), small["ln1_b"].reshape(1, -1)
    g2, b2 = small["ln2_g"].reshape(1, -1), small["ln2_b"].reshape(1, -1)
    d_skip = small["d_skip"]
    xb = x.astype(BF16)

    qkv = _mm(xb, wts["qkv"], out_dtype=BF16, name="f_qkv")
    z = _mm(xb, wts["z"], name="f_z")
    xbc = _mm(xb, wts["xbc"], name="f_xbc")
    gl = _mm(xb, wts["gate"], name="f_gate")
    fd = _mm(xb, wts["fd"], name="f_fd")
    dt_c, ac_c, cf_c, dt_r, ac_r, cf_r = _stats_fwd(fd, bias_row, a_row)
    bk = _att_blocks(s)[1]
    ck4 = cf_r[F_LANE0:F_LANE0 + ATT_HEADS].reshape(N_HP, HP, s // bk, bk)
    attn, lse = _attention_fwd(qkv, ck4)
    attn_d = _mm(attn, wts["pa"], name="f_pa")
    xact = _conv_fwd(xbc, wts["conv"], conv_b)
    dsk_pair = jnp.repeat(d_skip, SSM_HEAD_DIM).reshape(N_PAIR, LANES)
    y, hprev = _ssd_pair_fwd(xact, ac_c, dt_r, ac_r, dsk_pair)
    ssm = _gnorm_fwd(y, z, norm_w)
    ssm_d = _mm(ssm, wts["ps"], name="f_ps")
    mix = _mix_fwd(gl, bg, attn_d, ssm_d)
    mixed = _mm(mix, wts["out"], name="f_out")
    x1 = _ln1_fwd(x, mixed, g1, b1)
    gu = _mm(x1, wts["gu"], name="f_gu")
    act = _swiglu_fwd(gu)
    h = _mm(act, wts["down"], name="f_down")
    dpre2, loss_row, dg2, db2 = _ln2_loss(x1, h, target, g2, b2)

    d_act = _mm(dpre2, wts["down"], tb=True, name="b_down_x")
    dw_down = _mm(act, dpre2, ta=True, name="b_down_w")
    dgu = _swiglu_bwd(gu, d_act)
    dffn = _mm(dgu, wts["gu"], tb=True, name="b_gu_x")
    dw_gu = _mm(x1, dgu, ta=True, name="b_gu_w")
    dpre1, dxr, dg1, db1 = _ln1_bwd(x, mixed, g1, dpre2, dffn)
    dmix = _mm(dpre1, wts["out"], tb=True, name="b_out_x")
    dw_out = _mm(mix, dpre1, ta=True, name="b_out_w")
    dattn_d, dssm_d, dgl, dbg = _mix_bwd(gl, bg, attn_d, ssm_d, dmix)
    dssm = _mm(dssm_d, wts["ps"], tb=True, name="b_ps_x")
    dw_ps = _mm(ssm, dssm_d, ta=True, name="b_ps_w")
    dattn = _mm(dattn_d, wts["pa"], tb=True, name="b_pa_x")
    dw_pa = _mm(attn, dattn_d, ta=True, name="b_pa_w")
    dy, dz, dnw = _gnorm_bwd(y, z, norm_w, dssm)
    dxact, ddt, da_row, dds_pair = _ssd_pair_bwd(xact, dt_c, ac_c, dt_r, ac_r, hprev, dy, dsk_pair, a_row)
    dds = dds_pair.reshape(SSM_HEADS, SSM_HEAD_DIM).sum(axis=1)
    dpre_c, dconv_w, dconv_b = _conv_bwd_pre(xbc, wts["conv"], conv_b, dxact)
    dxbc = _conv_bwd_in(dpre_c, wts["conv"])
    st, do_b = _att_prep(dattn, attn, lse)
    dq, dk, dv, dck, dcq = _attention_bwd(qkv, ck4, st, do_b)
    dck_rows = jnp.zeros((LANES, s), F32).at[F_LANE0:F_LANE0 + ATT_HEADS].set(dck.reshape(ATT_HEADS, s))
    dcq_cols = jnp.zeros((s, LANES), F32).at[:, F_LANE0:F_LANE0 + ATT_HEADS].set(dcq[:, ::ATT_HEAD_DIM])
    dfd, dbias = _stats_bwd(fd, bias_row, ddt, dck_rows, dcq_cols)

    wq, wk, wv = wts["qkv"][:, :d], wts["qkv"][:, d:2 * d], wts["qkv"][:, 2 * d:]
    dx = dxr
    for i, (g_, w_) in enumerate(((dq, wq), (dk, wk), (dv, wv), (dz, wts["z"]), (dxbc, wts["xbc"]),
                                  (dgl, wts["gate"]), (dfd, wts["fd"]))):
        dx = _mm(g_, w_, tb=True, add=dx, name=f"b_in_x{i}")
    dw_in = [_mm(xb, g_, ta=True, name=f"b_in_w{i}") for i, g_ in enumerate((dq, dk, dv, dz, dxbc, dgl, dfd))]

    grads = dict(q=dw_in[0], k=dw_in[1], v=dw_in[2], z=dw_in[3], xbc=dw_in[4], gate=dw_in[5], fd=dw_in[6],
                 pa=dw_pa, ps=dw_ps, out=dw_out, gu=dw_gu, down=dw_down, conv=dconv_w)
    small_g = dict(
        b_forget=dbias[0, F_LANE0:F_LANE0 + ATT_HEADS], conv_b=dconv_b[0], dt_bias=dbias[0, :SSM_HEADS],
        a_log=da_row[0, :SSM_HEADS] * a, d_skip=dds, ssm_norm_w=dnw[0], b_gates=dbg[0],
        ln1_g=dg1[0], ln1_b=db1[0], ln2_g=dg2[0], ln2_b=db2[0])
    return loss_row[0, 0], dx, grads, small_g


BIG = ("w_in", "w_proj_attn", "w_proj_ssm", "w_out", "w_ffn_gate", "w_ffn_up", "w_ffn_down", "conv_w")
SMALL = ("b_forget", "conv_b", "dt_bias", "a_log", "d_skip", "ssm_norm_w", "b_gates", "ln1_g", "ln1_b", "ln2_g",
         "ln2_b")
SMALL_ROWS = 96
IN_SHARD = IN_WIDTH // N_DEV
IN_SEGMENTS = (("q", 0, 1024), ("k", 1024, 1024), ("v", 2048, 1024), ("f", 3072, ATT_HEADS), ("z", 3088, SSM_INNER),
               ("xbc", 5136, SSM_CONV_DIM), ("dt", 8208, SSM_HEADS), ("gate", 8240, 2 * D_MODEL))


def _cols_from_shards(shards, lo, hi):
    w = shards[0].shape[1]
    pieces = []
    for j in range(len(shards)):
        a, b = max(lo, j * w), min(hi, (j + 1) * w)
        if a < b:
            pieces.append(shards[j][:, a - j * w:b - j * w])
    return pieces[0] if len(pieces) == 1 else jnp.concatenate(pieces, axis=1)


def _shards_from_parts(parts, width):
    shards = []
    for j in range(N_DEV):
        lo, hi = j * width, (j + 1) * width
        pieces = []
        for mat, c0 in parts:
            a, b = max(lo, c0), min(hi, c0 + mat.shape[1])
            if a < b:
                pieces.append(mat[:, a - c0:b - c0])
        shards.append(pieces[0] if len(pieces) == 1 else jnp.concatenate(pieces, axis=1))
    return shards


def _pack_small(vals):
    flat = jnp.concatenate([vals[n].reshape(-1) for n in SMALL])
    return jnp.pad(flat, (0, SMALL_ROWS * LANES - flat.shape[0])).reshape(SMALL_ROWS, LANES)


def _unpack_small(pack, shapes):
    flat = pack.reshape(-1)
    out, off = {}, 0
    for n in SMALL:
        sz = math.prod(shapes[n])
        out[n] = flat[off:off + sz].reshape(shapes[n])
        off += sz
    return out


def kernel(x, w_in, b_forget, conv_w, conv_b, dt_bias, a_log, d_skip, ssm_norm_w, w_proj_attn, w_proj_ssm, b_gates, w_out, ln1_g, ln1_b, w_ffn_gate, w_ffn_up, w_ffn_down, ln2_g, ln2_b, loss_target, m_w_in, m_b_forget, m_conv_w, m_conv_b, m_dt_bias, m_a_log, m_d_skip, m_ssm_norm_w, m_w_proj_attn, m_w_proj_ssm, m_b_gates, m_w_out, m_ln1_g, m_ln1_b, m_w_ffn_gate, m_w_ffn_up, m_w_ffn_down, m_ln2_g, m_ln2_b, v_w_in, v_b_forget, v_conv_w, v_conv_b, v_dt_bias, v_a_log, v_d_skip, v_ssm_norm_w, v_w_proj_attn, v_w_proj_ssm, v_b_gates, v_w_out, v_ln1_g, v_ln1_b, v_w_ffn_gate, v_w_ffn_up, v_w_ffn_down, v_ln2_g, v_ln2_b):
    args = dict(locals())
    d, f = D_MODEL, FFN_HIDDEN
    big_w = {n: args[n][0] for n in BIG}
    small_w = {n: args[n][0] for n in SMALL}
    big_shapes = {n: args[n].shape for n in BIG}
    small_shapes = {n: args[n].shape for n in SMALL}

    sent = [big_w[n].astype(BF16) for n in BIG[:-1]] + [big_w["conv_w"]]
    full = dict(zip(BIG, _all_gather(sent)))
    in_shards = [full["w_in"][j] for j in range(N_DEV)]
    seg = {n: _cols_from_shards(in_shards, c0, c0 + w) for n, c0, w in IN_SEGMENTS}
    wfd = jnp.concatenate([seg["dt"], seg["f"], jnp.zeros((d, LANES - SSM_HEADS - ATT_HEADS), BF16)], axis=1)
    wts = dict(
        qkv=jnp.concatenate([seg["q"], seg["k"], seg["v"]], axis=1), z=seg["z"], xbc=seg["xbc"], gate=seg["gate"],
        fd=wfd, pa=full["w_proj_attn"].reshape(d, d), ps=full["w_proj_ssm"].reshape(SSM_INNER, d),
        out=full["w_out"].reshape(d, d),
        gu=jnp.concatenate([full["w_ffn_gate"][j] for j in range(N_DEV)]
                           + [full["w_ffn_up"][j] for j in range(N_DEV)], axis=1),
        down=full["w_ffn_down"].reshape(f, d),
        conv=jnp.concatenate([full["conv_w"][j] for j in range(N_DEV)], axis=1))

    loss_part, grad_x, g, small_g = _local_step(x[0], loss_target[0], wts, small_w)
    loss = lax.psum(loss_part, ("x", "y", "c"))

    gfd = g["fd"]
    in_parts = dict(q=g["q"], k=g["k"], v=g["v"], f=gfd[:, F_LANE0:F_LANE0 + ATT_HEADS], z=g["z"], xbc=g["xbc"],
                    dt=gfd[:, DT_LANE0:DT_LANE0 + SSM_HEADS], gate=g["gate"])
    col_shards = dict(
        w_in=_shards_from_parts([(in_parts[n], c0) for n, c0, _ in IN_SEGMENTS], IN_SHARD),
        w_ffn_gate=_shards_from_parts([(g["gu"][:, :f], 0)], f // N_DEV),
        w_ffn_up=_shards_from_parts([(g["gu"][:, f:], 0)], f // N_DEV),
        conv_w=_shards_from_parts([(g["conv"], 0)], SSM_CONV_DIM // N_DEV))
    by_dest = {n: jnp.stack([s_.astype(F32 if n == "conv_w" else BF16) for s_ in col_shards[n]])
               for n in col_shards}
    for n, key in (("w_proj_attn", "pa"), ("w_proj_ssm", "ps"), ("w_out", "out"), ("w_ffn_down", "down")):
        by_dest[n] = g[key].astype(BF16).reshape((N_DEV,) + big_shapes[n][1:])
    small_pack = _pack_small(small_g)
    recv = _exchange([by_dest[n] for n in BIG] + [jnp.broadcast_to(small_pack, (N_DEV,) + small_pack.shape)])

    outs = {}
    for t, n in enumerate(BIG):
        shp = big_shapes[n]
        res4 = _adamw(recv[t], big_w[n], args["m_" + n][0], args["v_" + n][0], name="adamw_" + n)
        outs[n] = [r.reshape(shp) for r in res4]
    small4 = _adamw(recv[len(BIG)], _pack_small(small_w), _pack_small({n: args["m_" + n][0] for n in SMALL}),
                    _pack_small({n: args["v_" + n][0] for n in SMALL}), name="adamw_small")
    small_out = [_unpack_small(p, small_shapes) for p in small4]
    for n in SMALL:
        outs[n] = [so[n] for so in small_out]

    order = ("w_in", "b_forget", "conv_w", "conv_b", "dt_bias", "a_log", "d_skip", "ssm_norm_w", "w_proj_attn",
             "w_proj_ssm", "b_gates", "w_out", "ln1_g", "ln1_b", "w_ffn_gate", "w_ffn_up", "w_ffn_down", "ln2_g",
             "ln2_b")
    res = [loss, grad_x[None]]
    for i in range(4):
        res += [outs[n][i] for n in order]
    return tuple(res)
```

```python
import functools
import math

import jax
import jax.numpy as jnp
from jax import lax
from jax.experimental import pallas as pl
from jax.experimental.pallas import tpu as pltpu

F32 = jnp.float32
BF16 = jnp.bfloat16

N_DEV = 8
D_MODEL = 1024
ATT_HEADS = 16
ATT_HEAD_DIM = 64
SSM_INNER = 2048
SSM_HEADS = 32
SSM_HEAD_DIM = 64
SSM_GROUPS = 4
SSM_HEADS_PER_GROUP = 8
SSM_STATE = 128
SSM_CONV = 4
SSM_CHUNK = 128
SSM_CONV_DIM = 3072
FFN_HIDDEN = 2816
IN_WIDTH = 10288
DEEPNORM_ALPHA = 2.0 ** 0.25
LN_EPS = 1e-5
RMS_EPS = 1e-5
ADAM_LR, ADAM_B1, ADAM_B2, ADAM_EPS, ADAM_WD, ADAM_STEP = 0.001, 0.9, 0.999, 1e-08, 0.01, 10
ATT_SCALE = 1.0 / math.sqrt(ATT_HEAD_DIM)

LANES = 128
VMEM_LIMIT = 56 * 1024 * 1024
NEG = -1e30

DT_LANE0 = 0
F_LANE0 = 32
HI = lax.Precision.HIGHEST


def _cp(*sem):
    return pltpu.CompilerParams(dimension_semantics=sem, vmem_limit_bytes=VMEM_LIMIT)


def _tile(n, cap=1408):
    for t in (1408, 1024, 512, 384, 256, 128):
        if t <= cap and n % t == 0:
            return t
    return n


def _sigmoid(x):
    return 1.0 / (1.0 + jnp.exp(-x))


def _mm(a, b, *, ta=False, tb=False, out_dtype=F32, add=None, name):
    m, k = (a.shape[1], a.shape[0]) if ta else a.shape
    n = b.shape[0] if tb else b.shape[1]
    assert (b.shape[1] if tb else b.shape[0]) == k
    tm, tn, tk = _tile(m), _tile(n), _tile(k)
    nk = k // tk
    dims = (((0,) if ta else (1,), (1,) if tb else (0,)), ((), ()))

    def body(*refs):
        if add is None:
            a_ref, b_ref, o_ref, acc_ref = refs
        else:
            a_ref, b_ref, c_ref, o_ref, acc_ref = refs
        kk = pl.program_id(2)

        @pl.when(kk == 0)
        def _():
            acc_ref[...] = jnp.zeros_like(acc_ref)

        acc_ref[...] += lax.dot_general(a_ref[...].astype(BF16), b_ref[...].astype(BF16), dims,
                                        preferred_element_type=F32)

        @pl.when(kk == nk - 1)
        def _():
            r = acc_ref[...]
            if add is not None:
                r = r + c_ref[...]
            o_ref[...] = r.astype(o_ref.dtype)

    a_spec = pl.BlockSpec((tk, tm), lambda i, j, kk: (kk, i)) if ta else pl.BlockSpec((tm, tk), lambda i, j, kk: (i, kk))
    b_spec = pl.BlockSpec((tn, tk), lambda i, j, kk: (j, kk)) if tb else pl.BlockSpec((tk, tn), lambda i, j, kk: (kk, j))
    o_spec = pl.BlockSpec((tm, tn), lambda i, j, kk: (i, j))
    in_specs, args = [a_spec, b_spec], [a, b]
    if add is not None:
        in_specs.append(o_spec)
        args.append(add)
    return pl.pallas_call(
        body, name=name, grid=(m // tm, n // tn, nk), in_specs=in_specs, out_specs=o_spec,
        out_shape=jax.ShapeDtypeStruct((m, n), out_dtype),
        scratch_shapes=[pltpu.VMEM((tm, tn), F32)],
        compiler_params=_cp("parallel", "parallel", "arbitrary"),
    )(*args)


def _tri(n, lower=True):
    r = lax.broadcasted_iota(jnp.int32, (n, n), 0)
    c = lax.broadcasted_iota(jnp.int32, (n, n), 1)
    return jnp.where((r >= c) if lower else (c >= r), 1.0, 0.0).astype(F32)


def _stats_fwd(fd, bias_row, a_row):
    s = fd.shape[0]
    blk = SSM_CHUNK

    def body(fd_ref, bias_ref, a_ref, dt_ref, ac_ref, cf_ref, dtr_ref, acr_ref, cfr_ref, carry_ref):
        @pl.when(pl.program_id(0) == 0)
        def _():
            carry_ref[...] = jnp.zeros_like(carry_ref)

        v = fd_ref[...] + bias_ref[...]
        dt = jnp.maximum(v, 0.0) + jnp.log(1.0 + jnp.exp(-jnp.abs(v)))
        lf = jnp.minimum(v, 0.0) - jnp.log(1.0 + jnp.exp(-jnp.abs(v)))
        tri = _tri(blk)
        ac = jnp.dot(tri, dt * a_ref[...], precision=HI, preferred_element_type=F32)
        cf = jnp.dot(tri, lf, precision=HI, preferred_element_type=F32) + carry_ref[0:1, :]
        carry_ref[...] = carry_ref[...] + jnp.sum(lf, axis=0, keepdims=True)
        dt_ref[...] = dt
        ac_ref[...] = ac
        cf_ref[...] = cf
        dtr_ref[...] = dt.T
        acr_ref[...] = ac.T
        cfr_ref[...] = cf.T

    col = pl.BlockSpec((blk, LANES), lambda i: (i, 0))
    row = pl.BlockSpec((LANES, blk), lambda i: (0, i))
    vec = pl.BlockSpec((1, LANES), lambda i: (0, 0))
    return pl.pallas_call(
        body, name="stats_fwd", grid=(s // blk,), in_specs=[col, vec, vec],
        out_specs=[col, col, col, row, row, row],
        out_shape=[jax.ShapeDtypeStruct((s, LANES), F32)] * 3 + [jax.ShapeDtypeStruct((LANES, s), F32)] * 3,
        scratch_shapes=[pltpu.VMEM((8, LANES), F32)],
        compiler_params=_cp("arbitrary"),
    )(fd, bias_row, a_row)


def _stats_bwd(fd, bias_row, ddt, dck_rows, dcq_cols):
    s = fd.shape[0]
    blk = SSM_CHUNK
    nb = s // blk

    def body(fd_ref, bias_ref, ddt_ref, dck_ref, dcq_ref, o_ref, db_ref, carry_ref):
        @pl.when(pl.program_id(0) == 0)
        def _():
            carry_ref[...] = jnp.zeros_like(carry_ref)
            db_ref[...] = jnp.zeros_like(db_ref)

        v = fd_ref[...] + bias_ref[...]
        dcum = dck_ref[...].T + dcq_ref[...]
        dlf = jnp.dot(_tri(blk, lower=False), dcum, precision=HI, preferred_element_type=F32) + carry_ref[0:1, :]
        carry_ref[...] = carry_ref[...] + jnp.sum(dcum, axis=0, keepdims=True)
        lane = lax.broadcasted_iota(jnp.int32, v.shape, 1)
        g = jnp.where(lane < F_LANE0, ddt_ref[...] * _sigmoid(v), dlf * _sigmoid(-v))
        g = jnp.where(lane < F_LANE0 + ATT_HEADS, g, 0.0)
        o_ref[...] = g.astype(o_ref.dtype)
        db_ref[...] += jnp.sum(g, axis=0, keepdims=True)

    col = pl.BlockSpec((blk, LANES), lambda i: (nb - 1 - i, 0))
    row = pl.BlockSpec((LANES, blk), lambda i: (0, nb - 1 - i))
    vec = pl.BlockSpec((1, LANES), lambda i: (0, 0))
    return pl.pallas_call(
        body, name="stats_bwd", grid=(nb,), in_specs=[col, vec, col, row, col], out_specs=[col, vec],
        out_shape=[jax.ShapeDtypeStruct((s, LANES), BF16), jax.ShapeDtypeStruct((1, LANES), F32)],
        scratch_shapes=[pltpu.VMEM((8, LANES), F32)],
        compiler_params=_cp("arbitrary"),
    )(fd, bias_row, ddt, dck_rows, dcq_cols)


HP = LANES // ATT_HEAD_DIM
N_HP = ATT_HEADS // HP


def _att_blocks(s):
    return (512, 1024) if s % 1024 == 0 and s >= 4096 else (64, 128)


_QK = (((1,), (1,)), ((), ()))
_HALF = ATT_HEAD_DIM // 2


def _head_cols(a):
    return slice(a * ATT_HEAD_DIM, (a + 1) * ATT_HEAD_DIM)


def _causal(shape, off):
    r = lax.broadcasted_iota(jnp.int32, shape, 0)
    c = lax.broadcasted_iota(jnp.int32, shape, 1)
    return c <= r + off


def _attention_fwd(qkv, ck4, gather_parts):
    s = qkv.shape[0]
    bq, bk = _att_blocks(s)
    nq, nk = s // bq, s // bk
    n = len(gather_parts)

    def body(q_ref, k_ref, v_ref, ck_ref, *rest):
        comm_in, (o_ref, lse_ref), comm_out, sems = rest[:n], rest[n:n + 2], rest[n + 2:2 * n + 2], rest[2 * n + 2:]
        i = pl.program_id(1)
        if n:
            @pl.when((pl.program_id(0) == 0) & (i == 0))
            def _():
                for cp in _comm_copies(comm_in, comm_out, sems, False):
                    cp.start()

        n_full = (i * bq) // bk
        qs = [(q_ref[:, _head_cols(a)].astype(F32) * ATT_SCALE).astype(BF16) for a in range(HP)]

        def step(j, carry, off=None):
            ks = pl.ds(pl.multiple_of(j * bk, bk), bk)
            out = []
            for a in range(HP):
                m, l, acc = carry[a]
                sc = lax.dot_general(qs[a], k_ref[ks, _head_cols(a)], _QK, preferred_element_type=F32)
                sc = sc - ck_ref[0, a, pl.ds(j, 1), :]
                if off is not None:
                    sc = jnp.where(_causal(sc.shape, off), sc, NEG)
                m_new = jnp.maximum(m, jnp.max(sc, axis=1, keepdims=True))
                alpha = jnp.exp(m - m_new)
                p = jnp.exp(sc - m_new)
                l = alpha * l + jnp.sum(p, axis=1, keepdims=True)
                acc = alpha * acc + jnp.dot(p.astype(BF16), v_ref[ks, _head_cols(a)], preferred_element_type=F32)
                out.append((m_new, l, acc))
            return tuple(out)

        init = tuple((jnp.full((bq, 1), NEG, F32), jnp.zeros((bq, 1), F32), jnp.zeros((bq, ATT_HEAD_DIM), F32))
                     for _ in range(HP))
        carry = lax.fori_loop(0, n_full, step, init)
        carry = step(n_full, carry, off=i * bq - n_full * bk)
        for a in range(HP):
            m, l, acc = carry[a]
            o_ref[:, _head_cols(a)] = acc / l
            lse_ref[:, _head_cols(a)] = jnp.broadcast_to(m + jnp.log(l), (bq, ATT_HEAD_DIM))
        if n:
            @pl.when((pl.program_id(0) == N_HP - 1) & (i == nq - 1))
            def _():
                for cp in _comm_copies(comm_in, comm_out, sems, False):
                    cp.wait()

    q_spec = pl.BlockSpec((bq, LANES), lambda h, i: (i, h))
    anyspec = pl.BlockSpec(memory_space=pl.ANY)
    res = pl.pallas_call(
        body, name="att_fwd", grid=(N_HP, nq),
        in_specs=[q_spec, pl.BlockSpec((s, LANES), lambda h, i: (0, N_HP + h)),
                  pl.BlockSpec((s, LANES), lambda h, i: (0, 2 * N_HP + h)),
                  pl.BlockSpec((1, HP, nk, bk), lambda h, i: (h, 0, 0, 0))] + [anyspec] * n,
        out_specs=[q_spec, q_spec] + [anyspec] * n,
        out_shape=[jax.ShapeDtypeStruct((s, D_MODEL), F32)] * 2 + _comm_out_shapes(gather_parts, False),
        scratch_shapes=_comm_sems(n) if n else [],
        compiler_params=_cp("arbitrary", "arbitrary"),
    )(qkv, qkv, qkv, ck4, *gather_parts)
    return res[0], res[1], list(res[2:])


def _att_prep(do, o, lse_rep):
    s = do.shape[0]
    bs = _tile(s, 512)

    def body(do_ref, o_ref, lse_ref, st_ref, dob_ref):
        r = lax.broadcasted_iota(jnp.int32, (LANES, LANES), 0) // ATT_HEAD_DIM
        c = lax.broadcasted_iota(jnp.int32, (LANES, LANES), 1) // ATT_HEAD_DIM
        e = jnp.where(r == c, 1.0, 0.0).astype(F32)
        lane = lax.broadcasted_iota(jnp.int32, (bs, LANES), 1)
        for p in range(D_MODEL // LANES):
            cs = slice(p * LANES, (p + 1) * LANES)
            dd = do_ref[:, cs]
            delta = jnp.dot(dd * o_ref[:, cs], e, precision=HI, preferred_element_type=F32)
            st_ref[:, cs] = jnp.where(lane % ATT_HEAD_DIM < _HALF, lse_ref[:, cs], delta)
            dob_ref[:, cs] = dd.astype(BF16)

    spec = pl.BlockSpec((bs, D_MODEL), lambda i: (i, 0))
    return pl.pallas_call(body, name="att_prep", grid=(s // bs,), in_specs=[spec, spec, spec], out_specs=[spec, spec],
                          out_shape=[jax.ShapeDtypeStruct((s, D_MODEL), F32), jax.ShapeDtypeStruct((s, D_MODEL), BF16)],
                          compiler_params=_cp("parallel"))(do, o, lse_rep)


def _attention_bwd(qkv, ck4, st, do_b, exchange_parts):
    s = qkv.shape[0]
    bq, bk = _att_blocks(s)
    nq, nk, per = s // bq, s // bk, bk // bq
    _T = (((0,), (0,)), ((), ()))
    n = len(exchange_parts)

    def body(q_ref, k_ref, v_ref, ck_ref, st_ref, do_ref, *rest):
        comm_in, (dq_ref, dk_ref, dv_ref, dck_ref, dcq_ref) = rest[:n], rest[n:n + 5]
        comm_out, sems, (dk_acc, dv_acc, dck_acc) = rest[n + 5:2 * n + 5], rest[2 * n + 5:-3], rest[-3:]
        j = pl.program_id(1)
        if n:
            @pl.when((pl.program_id(0) == 0) & (j == 0))
            def _():
                for cp in _comm_copies(comm_in, comm_out, sems, True):
                    cp.start()

        @pl.when(j == 0)
        def _():
            dq_ref[...] = jnp.zeros_like(dq_ref)
            dcq_ref[...] = jnp.zeros_like(dcq_ref)

        dk_acc[...] = jnp.zeros_like(dk_acc)
        dv_acc[...] = jnp.zeros_like(dv_acc)
        dck_acc[...] = jnp.zeros_like(dck_acc)

        def step(i, off=None):
            rows = pl.ds(pl.multiple_of(i * bq, bq), bq)
            for a in range(HP):
                cs = _head_cols(a)
                q = (q_ref[rows, cs].astype(F32) * ATT_SCALE).astype(BF16)
                k = k_ref[:, cs]
                do_a = do_ref[rows, cs]
                sc = lax.dot_general(q, k, _QK, preferred_element_type=F32) - ck_ref[0, a, pl.ds(j, 1), :]
                if off is not None:
                    sc = jnp.where(_causal(sc.shape, off), sc, NEG)
                p = jnp.exp(sc - st_ref[rows, a * ATT_HEAD_DIM:a * ATT_HEAD_DIM + 1])
                dp = lax.dot_general(do_a, v_ref[:, cs], _QK, preferred_element_type=F32)
                ds = p * (dp - st_ref[rows, a * ATT_HEAD_DIM + _HALF:a * ATT_HEAD_DIM + _HALF + 1])
                ds_b = ds.astype(BF16)
                dv_acc[a] += lax.dot_general(p.astype(BF16), do_a, _T, preferred_element_type=F32)
                dk_acc[a] += lax.dot_general(ds_b, q, _T, preferred_element_type=F32)
                dq_ref[rows, cs] += jnp.dot(ds_b, k, preferred_element_type=F32) * ATT_SCALE
                dck_acc[a] -= jnp.sum(ds, axis=0, keepdims=True)
                dcq_ref[rows, cs] += jnp.broadcast_to(jnp.sum(ds, axis=1, keepdims=True), (bq, ATT_HEAD_DIM))

        for t in range(per):
            step(j * per + t, off=t * bq)

        def full(i, c):
            step(i)
            return c

        lax.fori_loop((j + 1) * per, nq, full, 0)
        for a in range(HP):
            dk_ref[:, _head_cols(a)] = dk_acc[a].astype(dk_ref.dtype)
            dv_ref[:, _head_cols(a)] = dv_acc[a].astype(dv_ref.dtype)
            dck_ref[0, a, pl.ds(j, 1), :] = dck_acc[a]
        if n:
            @pl.when((pl.program_id(0) == N_HP - 1) & (j == nk - 1))
            def _():
                for cp in _comm_copies(comm_in, comm_out, sems, True):
                    cp.wait()

    res = pl.BlockSpec((s, LANES), lambda h, j: (0, h))
    ck_spec = pl.BlockSpec((1, HP, nk, bk), lambda h, j: (h, 0, 0, 0))
    kout = pl.BlockSpec((bk, LANES), lambda h, j: (j, h))
    anyspec = pl.BlockSpec(memory_space=pl.ANY)
    outs = pl.pallas_call(
        body, name="att_bwd", grid=(N_HP, nk),
        in_specs=[res, pl.BlockSpec((bk, LANES), lambda h, j: (j, N_HP + h)),
                  pl.BlockSpec((bk, LANES), lambda h, j: (j, 2 * N_HP + h)), ck_spec, res, res] + [anyspec] * n,
        out_specs=[res, kout, kout, ck_spec, res] + [anyspec] * n,
        out_shape=[jax.ShapeDtypeStruct((s, D_MODEL), F32), jax.ShapeDtypeStruct((s, D_MODEL), BF16),
                   jax.ShapeDtypeStruct((s, D_MODEL), BF16), jax.ShapeDtypeStruct((N_HP, HP, nk, bk), F32),
                   jax.ShapeDtypeStruct((s, D_MODEL), F32)] + _comm_out_shapes(exchange_parts, True),
        scratch_shapes=(_comm_sems(n) if n else [])
        + [pltpu.VMEM((HP, bk, ATT_HEAD_DIM), F32), pltpu.VMEM((HP, bk, ATT_HEAD_DIM), F32),
           pltpu.VMEM((HP, 1, bk), F32)],
        compiler_params=_cp("arbitrary", "arbitrary"),
    )(qkv, qkv, qkv, ck4, st, do_b, *exchange_parts)
    return outs[:5], list(outs[5:])


def _silu_and_grad(x):
    sg = _sigmoid(x)
    return x * sg, sg * (1.0 + x * (1.0 - sg))


def _conv_pre(cur, halo, w_ref, b_ref, first):
    halo = jnp.where(first, 0.0, halo)
    row = lax.broadcasted_iota(jnp.int32, cur.shape, 0)
    shifted = []
    for k in range(SSM_CONV):
        sh = SSM_CONV - 1 - k
        if sh == 0:
            shifted.append(cur)
            continue
        r = pltpu.roll(cur, sh, 0)
        hr = pltpu.roll(halo, sh, 0)
        top = jnp.where(row[0:8] < sh, hr, r[0:8])
        shifted.append(jnp.concatenate([top, r[8:]], axis=0))
    pre = b_ref[...] + sum(w_ref[k:k + 1, :] * shifted[k] for k in range(SSM_CONV))
    return pre, shifted


def _conv_specs(s, bs, bc):
    cur = pl.BlockSpec((bs, bc), lambda j, i: (i, j))
    halo = pl.BlockSpec((8, bc), lambda j, i: (jnp.maximum(i * (bs // 8) - 1, 0), j))
    w = pl.BlockSpec((SSM_CONV, bc), lambda j, i: (0, j))
    b = pl.BlockSpec((1, bc), lambda j, i: (0, j))
    return cur, halo, w, b


def _conv_fwd(xbc, w, b):
    s, c = xbc.shape
    bs, bc = _tile(s, 512), 1024

    def body(x_ref, h_ref, w_ref, b_ref, o_ref):
        pre, _ = _conv_pre(x_ref[...], h_ref[...], w_ref, b_ref, pl.program_id(1) == 0)
        o_ref[...] = pre * _sigmoid(pre)

    cur, halo, ws, bsp = _conv_specs(s, bs, bc)
    return pl.pallas_call(body, name="conv_fwd", grid=(c // bc, s // bs), in_specs=[cur, halo, ws, bsp],
                          out_specs=cur, out_shape=jax.ShapeDtypeStruct((s, c), F32),
                          compiler_params=_cp("parallel", "parallel"))(xbc, xbc, w, b)


def _conv_bwd_pre(xbc, w, b, dact):
    s, c = xbc.shape
    bs, bc = _tile(s, 512), 1024

    def body(x_ref, h_ref, w_ref, b_ref, g_ref, dp_ref, dw_ref, db_ref):
        @pl.when(pl.program_id(1) == 0)
        def _():
            dw_ref[...] = jnp.zeros_like(dw_ref)
            db_ref[...] = jnp.zeros_like(db_ref)

        pre, shifted = _conv_pre(x_ref[...], h_ref[...], w_ref, b_ref, pl.program_id(1) == 0)
        dpre = g_ref[...] * _silu_and_grad(pre)[1]
        dp_ref[...] = dpre
        db_ref[...] += jnp.sum(dpre, axis=0, keepdims=True)
        for k in range(SSM_CONV):
            dw_ref[k:k + 1, :] += jnp.sum(dpre * shifted[k], axis=0, keepdims=True)

    cur, halo, ws, bsp = _conv_specs(s, bs, bc)
    return pl.pallas_call(
        body, name="conv_bwd_pre", grid=(c // bc, s // bs), in_specs=[cur, halo, ws, bsp, cur],
        out_specs=[cur, ws, bsp],
        out_shape=[jax.ShapeDtypeStruct((s, c), F32), jax.ShapeDtypeStruct((SSM_CONV, c), F32),
                   jax.ShapeDtypeStruct((1, c), F32)],
        compiler_params=_cp("parallel", "arbitrary"))(xbc, xbc, w, b, dact)


def _conv_bwd_in(dpre, w):
    s, c = dpre.shape
    bs, bc = _tile(s, 512), 1024
    nb = s // bs

    def body(g_ref, n_ref, w_ref, o_ref):
        cur = g_ref[...]
        nxt = jnp.where(pl.program_id(1) == nb - 1, 0.0, n_ref[...])
        row = lax.broadcasted_iota(jnp.int32, cur.shape, 0)
        acc = w_ref[SSM_CONV - 1:SSM_CONV, :] * cur
        for sh in range(1, SSM_CONV):
            r = pltpu.roll(cur, bs - sh, 0)
            nr = pltpu.roll(nxt, 8 - sh, 0)
            bot = jnp.where(row[0:8] >= 8 - sh, nr, r[bs - 8:])
            acc = acc + w_ref[SSM_CONV - 1 - sh:SSM_CONV - sh, :] * jnp.concatenate([r[:bs - 8], bot], axis=0)
        o_ref[...] = acc.astype(o_ref.dtype)

    cur = pl.BlockSpec((bs, bc), lambda j, i: (i, j))
    nxt = pl.BlockSpec((8, bc), lambda j, i: (jnp.minimum((i + 1) * (bs // 8), s // 8 - 1), j))
    ws = pl.BlockSpec((SSM_CONV, bc), lambda j, i: (0, j))
    return pl.pallas_call(body, name="conv_bwd_in", grid=(c // bc, nb), in_specs=[cur, nxt, ws], out_specs=cur,
                          out_shape=jax.ShapeDtypeStruct((s, c), BF16),
                          compiler_params=_cp("parallel", "parallel"))(dpre, dpre, w)


def _dotT(a, b):
    return lax.dot_general(a.astype(BF16), b.astype(BF16), (((1,), (1,)), ((), ())), preferred_element_type=F32)


def _Tdot(a, b):
    return lax.dot_general(a.astype(BF16), b.astype(BF16), (((0,), (0,)), ((), ())), preferred_element_type=F32)


def _dot(a, b):
    return jnp.dot(a.astype(BF16), b.astype(BF16), preferred_element_type=F32)


def _ssd_head(xbc_ref, dt_ref, ac_ref, acr_ref, h):
    L = SSM_CHUNK
    xs = xbc_ref[:, h * SSM_HEAD_DIM:(h + 1) * SSM_HEAD_DIM]
    dt_col = dt_ref[:, h:h + 1]
    a_col = ac_ref[:, h:h + 1]
    a_row = acr_ref[h:h + 1, :]
    li = lax.broadcasted_iota(jnp.int32, (L, L), 0)
    si = lax.broadcasted_iota(jnp.int32, (L, L), 1)
    decay = jnp.exp(jnp.where(li >= si, a_col - a_row, NEG))
    a_last = ac_ref[L - 1:L, h:h + 1]
    return xs, dt_col, a_col, a_last, decay


def _ssd_fwd(xbc_act, dt_c, ac_c, ac_r, d_skip):
    s = xbc_act.shape[0]
    L, P, N, G, R = SSM_CHUNK, SSM_HEAD_DIM, SSM_STATE, SSM_GROUPS, SSM_HEADS_PER_GROUP
    nc = s // L

    def body(dsk_ref, xbc_ref, dt_ref, ac_ref, acr_ref, y_ref, hp_ref, st_ref):
        @pl.when(pl.program_id(0) == 0)
        def _():
            st_ref[...] = jnp.zeros_like(st_ref)

        for g in range(G):
            b_g = xbc_ref[:, SSM_INNER + g * N:SSM_INNER + (g + 1) * N]
            c_g = xbc_ref[:, SSM_INNER + G * N + g * N:SSM_INNER + G * N + (g + 1) * N]
            cb = _dotT(c_g, b_g)
            for r in range(R):
                h = g * R + r
                xs, dt_col, a_col, a_last, decay = _ssd_head(xbc_ref, dt_ref, ac_ref, acr_ref, h)
                xdt = xs * dt_col
                hprev = st_ref[h]
                y = _dot(cb * decay, xdt) + jnp.exp(a_col) * _dotT(c_g, hprev) + dsk_ref[h] * xs
                y_ref[:, h * P:(h + 1) * P] = y
                hp_ref[0, h] = hprev
                st_ref[h] = hprev * jnp.exp(a_last) + _Tdot(xdt * jnp.exp(a_last - a_col), b_g)

    col = pl.BlockSpec((L, LANES), lambda c: (c, 0))
    return pl.pallas_call(
        body, name="ssd_fwd", grid=(nc,),
        in_specs=[pl.BlockSpec(memory_space=pltpu.SMEM), pl.BlockSpec((L, SSM_CONV_DIM), lambda c: (c, 0)), col, col,
                  pl.BlockSpec((LANES, L), lambda c: (0, c))],
        out_specs=[pl.BlockSpec((L, SSM_INNER), lambda c: (c, 0)),
                   pl.BlockSpec((1, SSM_HEADS, P, N), lambda c: (c, 0, 0, 0))],
        out_shape=[jax.ShapeDtypeStruct((s, SSM_INNER), F32), jax.ShapeDtypeStruct((nc, SSM_HEADS, P, N), F32)],
        scratch_shapes=[pltpu.VMEM((SSM_HEADS, P, N), F32)],
        compiler_params=_cp("arbitrary"),
    )(d_skip, xbc_act, dt_c, ac_c, ac_r)


def _ssd_bwd(xbc_act, dt_c, ac_c, ac_r, hprev_all, dy, d_skip, a_row):
    s = xbc_act.shape[0]
    L, P, N, G, R = SSM_CHUNK, SSM_HEAD_DIM, SSM_STATE, SSM_GROUPS, SSM_HEADS_PER_GROUP
    nc = s // L

    def body(dsk_ref, xbc_ref, dt_ref, ac_ref, acr_ref, hp_ref, dy_ref, arow_ref,
             dx_ref, ddt_ref, da_ref, dds_ref, dh_ref):
        @pl.when(pl.program_id(0) == 0)
        def _():
            dh_ref[...] = jnp.zeros_like(dh_ref)
            da_ref[...] = jnp.zeros_like(da_ref)
            dds_ref[...] = jnp.zeros_like(dds_ref)

        lane = lax.broadcasted_iota(jnp.int32, (L, LANES), 1)
        sub = lax.broadcasted_iota(jnp.int32, (LANES, L), 0)
        rowi = lax.broadcasted_iota(jnp.int32, (L, 1), 0)
        lane1 = lax.broadcasted_iota(jnp.int32, (1, LANES), 1)
        da_c = jnp.zeros((L, LANES), F32)
        da_r = jnp.zeros((LANES, L), F32)
        ddt1 = jnp.zeros((L, LANES), F32)
        dds = jnp.zeros((1, LANES), F32)
        for g in range(G):
            b_g = xbc_ref[:, SSM_INNER + g * N:SSM_INNER + (g + 1) * N]
            c_g = xbc_ref[:, SSM_INNER + G * N + g * N:SSM_INNER + G * N + (g + 1) * N]
            cb = _dotT(c_g, b_g)
            dcb = jnp.zeros((L, L), F32)
            db_g = jnp.zeros((L, N), F32)
            dc_g = jnp.zeros((L, N), F32)
            for r in range(R):
                h = g * R + r
                xs, dt_col, a_col, a_last, decay = _ssd_head(xbc_ref, dt_ref, ac_ref, acr_ref, h)
                gy = dy_ref[:, h * P:(h + 1) * P]
                xdt = xs * dt_col
                hprev = hp_ref[0, h]
                dhn = dh_ref[h]
                e_a = jnp.exp(a_col)
                e_last = jnp.exp(a_last)
                e_col = jnp.exp(a_last - a_col)
                m = cb * decay
                yoff = e_a * _dotT(c_g, hprev)
                da_col = jnp.sum(gy * yoff, axis=1, keepdims=True)
                dc_g = dc_g + e_a * _dot(gy, hprev)
                dhp = _Tdot(gy * e_a, c_g) + dhn * e_last
                da_last = jnp.sum(jnp.sum(dhn * hprev, axis=1, keepdims=True), axis=0, keepdims=True) * e_last
                xds = _dot(xdt, dhn)
                db_g = db_g + e_col * xds
                de_e = jnp.sum(xds * b_g, axis=1, keepdims=True) * e_col
                da_col = da_col - de_e
                da_last = da_last + jnp.sum(de_e, axis=0, keepdims=True)
                dxdt = e_col * _dotT(b_g, dhn)
                dm = _dotT(gy, xdt)
                dxdt = dxdt + _Tdot(m, gy)
                dcb = dcb + dm * decay
                w = dm * m
                da_col = da_col + jnp.sum(w, axis=1, keepdims=True) + jnp.where(rowi == L - 1, da_last, 0.0)
                da_c = jnp.where(lane == h, da_col, da_c)
                da_r = jnp.where(sub == h, jnp.sum(w, axis=0, keepdims=True), da_r)
                ddt1 = jnp.where(lane == h, jnp.sum(dxdt * xs, axis=1, keepdims=True), ddt1)
                dds = jnp.where(lane1 == h, jnp.sum(jnp.sum(gy * xs, axis=1, keepdims=True), axis=0, keepdims=True),
                                dds)
                dx_ref[:, h * P:(h + 1) * P] = dxdt * dt_col + dsk_ref[h] * gy
                dh_ref[h] = dhp
            dx_ref[:, SSM_INNER + g * N:SSM_INNER + (g + 1) * N] = db_g + _Tdot(dcb, c_g)
            dx_ref[:, SSM_INNER + G * N + g * N:SSM_INNER + G * N + (g + 1) * N] = dc_g + _dot(dcb, b_g)
        dda = jnp.dot(_tri(L, lower=False), da_c - da_r.T, precision=HI, preferred_element_type=F32)
        ddt_ref[...] = dda * arow_ref[...] + ddt1
        da_ref[...] += jnp.sum(dda * dt_ref[...], axis=0, keepdims=True)
        dds_ref[...] += dds

    col = pl.BlockSpec((L, LANES), lambda c: (nc - 1 - c, 0))
    vec = pl.BlockSpec((1, LANES), lambda c: (0, 0))
    return pl.pallas_call(
        body, name="ssd_bwd", grid=(nc,),
        in_specs=[pl.BlockSpec(memory_space=pltpu.SMEM), pl.BlockSpec((L, SSM_CONV_DIM), lambda c: (nc - 1 - c, 0)),
                  col, col, pl.BlockSpec((LANES, L), lambda c: (0, nc - 1 - c)),
                  pl.BlockSpec((1, SSM_HEADS, P, N), lambda c: (nc - 1 - c, 0, 0, 0)),
                  pl.BlockSpec((L, SSM_INNER), lambda c: (nc - 1 - c, 0)), vec],
        out_specs=[pl.BlockSpec((L, SSM_CONV_DIM), lambda c: (nc - 1 - c, 0)), col, vec, vec],
        out_shape=[jax.ShapeDtypeStruct((s, SSM_CONV_DIM), F32), jax.ShapeDtypeStruct((s, LANES), F32),
                   jax.ShapeDtypeStruct((1, LANES), F32), jax.ShapeDtypeStruct((1, LANES), F32)],
        scratch_shapes=[pltpu.VMEM((SSM_HEADS, P, N), F32)],
        compiler_params=_cp("arbitrary"),
    )(d_skip, xbc_act, dt_c, ac_c, ac_r, hprev_all, dy, a_row)


N_PAIR = SSM_HEADS // HP
PAIRS_PER_GROUP = SSM_HEADS_PER_GROUP // HP


def _pair_consts():
    L = SSM_CHUNK
    lane = lax.broadcasted_iota(jnp.int32, (L, LANES), 1)
    lane1 = lax.broadcasted_iota(jnp.int32, (1, LANES), 1)
    li = lax.broadcasted_iota(jnp.int32, (L, L), 0)
    si = lax.broadcasted_iota(jnp.int32, (L, L), 1)
    return lane >= ATT_HEAD_DIM, lane1 >= ATT_HEAD_DIM, li, si


def _ssd_pair_fwd(xbc_act, ac_c, dt_r, ac_r, dsk_pair):
    s = xbc_act.shape[0]
    L, N, G = SSM_CHUNK, SSM_STATE, SSM_GROUPS
    nc = s // L

    def body(xbc_ref, ac_ref, dtr_ref, acr_ref, dsk_ref, y_ref, hp_ref, st_ref):
        @pl.when(pl.program_id(0) == 0)
        def _():
            st_ref[...] = jnp.zeros_like(st_ref)

        upper, up1, li, si = _pair_consts()
        for g in range(G):
            b_g = xbc_ref[:, SSM_INNER + g * N:SSM_INNER + (g + 1) * N]
            c_g = xbc_ref[:, SSM_INNER + G * N + g * N:SSM_INNER + G * N + (g + 1) * N]
            cb = _dotT(c_g, b_g)
            b_t = b_g.T
            for q in range(PAIRS_PER_GROUP):
                pp = g * PAIRS_PER_GROUP + q
                cols = slice(pp * LANES, (pp + 1) * LANES)
                xs = xbc_ref[:, cols]
                ht = st_ref[pp]
                hp_ref[0, pp] = ht
                y = dsk_ref[pp:pp + 1, :] * xs
                s_new = jnp.zeros((N, LANES), F32)
                ea, el = [], []
                for a in range(HP):
                    h = HP * pp + a
                    acol = jnp.broadcast_to(ac_ref[:, h:h + 1], (L, LANES))
                    arow, dtrow = acr_ref[h:h + 1, :], dtr_ref[h:h + 1, :]
                    alast = ac_ref[L - 1:L, h:h + 1]
                    decay = jnp.exp(jnp.where(li >= si, acol - arow, NEG))
                    xs_a = jnp.where(upper == (a == 1), xs, 0.0)
                    y = y + _dot(cb * decay * dtrow, xs_a)
                    s_new = s_new + _dot(b_t * (dtrow * jnp.exp(alast - arow)), xs_a)
                    ea.append(jnp.exp(acol))
                    el.append(jnp.exp(alast))
                y_ref[:, cols] = y + jnp.where(upper, ea[1], ea[0]) * _dot(c_g, ht)
                st_ref[pp] = ht * jnp.where(up1, el[1], el[0]) + s_new

    col = pl.BlockSpec((L, LANES), lambda c: (c, 0))
    row = pl.BlockSpec((LANES, L), lambda c: (0, c))
    return pl.pallas_call(
        body, name="ssd_fwd", grid=(nc,),
        in_specs=[pl.BlockSpec((L, SSM_CONV_DIM), lambda c: (c, 0)), col, row, row,
                  pl.BlockSpec((N_PAIR, LANES), lambda c: (0, 0))],
        out_specs=[pl.BlockSpec((L, SSM_INNER), lambda c: (c, 0)),
                   pl.BlockSpec((1, N_PAIR, N, LANES), lambda c: (c, 0, 0, 0))],
        out_shape=[jax.ShapeDtypeStruct((s, SSM_INNER), F32), jax.ShapeDtypeStruct((nc, N_PAIR, N, LANES), F32)],
        scratch_shapes=[pltpu.VMEM((N_PAIR, N, LANES), F32)],
        compiler_params=_cp("arbitrary"),
    )(xbc_act, ac_c, dt_r, ac_r, dsk_pair)


def _ssd_pair_bwd(xbc_act, dt_c, ac_c, dt_r, ac_r, hprev_all, dy, dsk_pair, a_row):
    s = xbc_act.shape[0]
    L, N, G = SSM_CHUNK, SSM_STATE, SSM_GROUPS
    nc = s // L
    rev = lambda c: nc - 1 - c

    def body(xbc_ref, dt_ref, ac_ref, dtr_ref, acr_ref, hp_ref, dy_ref, dsk_ref, arow_ref,
             dx_ref, ddt_ref, da_ref, dds_ref, dh_ref):
        @pl.when(pl.program_id(0) == 0)
        def _():
            dh_ref[...] = jnp.zeros_like(dh_ref)
            da_ref[...] = jnp.zeros_like(da_ref)
            dds_ref[...] = jnp.zeros_like(dds_ref)

        upper, up1, li, si = _pair_consts()
        lane = lax.broadcasted_iota(jnp.int32, (L, LANES), 1)
        sub = lax.broadcasted_iota(jnp.int32, (LANES, L), 0)
        lastrow = lax.broadcasted_iota(jnp.int32, (L, LANES), 0) == L - 1
        da_c = jnp.zeros((L, LANES), F32)
        da_r = jnp.zeros((LANES, L), F32)
        ddt_r = jnp.zeros((LANES, L), F32)
        for g in range(G):
            b_g = xbc_ref[:, SSM_INNER + g * N:SSM_INNER + (g + 1) * N]
            c_g = xbc_ref[:, SSM_INNER + G * N + g * N:SSM_INNER + G * N + (g + 1) * N]
            cb, cb_t = _dotT(c_g, b_g), _dotT(b_g, c_g)
            b_t, c_t = b_g.T, c_g.T
            dcb = jnp.zeros((L, L), F32)
            db_t = jnp.zeros((N, L), F32)
            dc = jnp.zeros((L, N), F32)
            for q in range(PAIRS_PER_GROUP):
                pp = g * PAIRS_PER_GROUP + q
                cols = slice(pp * LANES, (pp + 1) * LANES)
                xs, gy = xbc_ref[:, cols], dy_ref[:, cols]
                ht, dhn = hp_ref[0, pp], dh_ref[pp]
                acol = [jnp.broadcast_to(ac_ref[:, HP * pp + a:HP * pp + a + 1], (L, LANES)) for a in range(HP)]
                alast = [ac_ref[L - 1:L, HP * pp + a:HP * pp + a + 1] for a in range(HP)]
                ea = jnp.where(upper, jnp.exp(acol[1]), jnp.exp(acol[0]))
                el = jnp.where(up1, jnp.exp(alast[1]), jnp.exp(alast[0]))
                ge = gy * ea
                dc = dc + _dotT(ge, ht)
                dh_ref[pp] = _dot(c_t, ge) + dhn * el
                t_off = (ge * _dot(c_g, ht)).astype(BF16)
                hsum = jnp.sum(dhn * ht, axis=0, keepdims=True)
                dxs = dsk_ref[pp:pp + 1, :] * gy
                dds_ref[pp:pp + 1, :] += jnp.sum(gy * xs, axis=0, keepdims=True)
                for a in range(HP):
                    h = HP * pp + a
                    mine, mine1 = upper == (a == 1), up1 == (a == 1)
                    arow, dtrow = acr_ref[h:h + 1, :], dtr_ref[h:h + 1, :]
                    dtcol = jnp.broadcast_to(dt_ref[:, h:h + 1], (L, LANES))
                    xs_a, gy_a = jnp.where(mine, xs, 0.0), jnp.where(mine, gy, 0.0)
                    dhn_a = jnp.where(mine1, dhn, 0.0)
                    e_row = jnp.exp(alast[a] - arow)
                    w_row = dtrow * e_row
                    xd_t = _dotT(dhn_a, xs_a)
                    db_t = db_t + xd_t * w_row
                    dw = jnp.sum(b_t * xd_t, axis=0, keepdims=True)
                    de_e = dw * w_row
                    dal = (jnp.sum(jnp.where(mine1, hsum, 0.0), axis=1, keepdims=True) * jnp.exp(alast[a])
                           + jnp.sum(de_e, axis=1, keepdims=True))
                    dxs = dxs + _dot(b_g, dhn_a) * (dtcol * jnp.exp(alast[a] - acol[a]))
                    decay = jnp.exp(jnp.where(li >= si, acol[a] - arow, NEG))
                    decay_t = jnp.exp(jnp.where(si >= li, arow - acol[a], NEG))
                    m = cb * decay
                    dmdt = _dotT(gy_a, xs_a)
                    dxs = dxs + _dot(cb_t * decay_t * dtcol, gy_a)
                    dm = dmdt * dtrow
                    dcb = dcb + dm * decay
                    wb = (dm * m).astype(BF16)
                    onehot = jnp.where(lane == h, 1.0, 0.0).astype(BF16)
                    da_c = (da_c + jnp.dot(wb, onehot, preferred_element_type=F32)
                            + jnp.dot(jnp.where(mine, t_off, 0.0).astype(BF16), onehot, preferred_element_type=F32)
                            + jnp.where(lastrow & (lane == h), dal, 0.0))
                    da_r = jnp.where(sub == h, -(jnp.sum(wb.astype(F32), axis=0, keepdims=True) + de_e), da_r)
                    ddt_r = jnp.where(sub == h, dw * e_row + jnp.sum(dmdt * m, axis=0, keepdims=True), ddt_r)
                dx_ref[:, cols] = dxs
            dx_ref[:, SSM_INNER + g * N:SSM_INNER + (g + 1) * N] = (db_t + _dot(c_t, dcb)).T
            dx_ref[:, SSM_INNER + G * N + g * N:SSM_INNER + G * N + (g + 1) * N] = dc + _dot(dcb, b_g)
        dda = jnp.dot(_tri(L, lower=False), da_c + da_r.T, precision=HI, preferred_element_type=F32)
        ddt_ref[...] = dda * arow_ref[...] + ddt_r.T
        da_ref[...] += jnp.sum(dda * dt_ref[...], axis=0, keepdims=True)

    col = pl.BlockSpec((L, LANES), lambda c: (rev(c), 0))
    row = pl.BlockSpec((LANES, L), lambda c: (0, rev(c)))
    vec = pl.BlockSpec((1, LANES), lambda c: (0, 0))
    pairs = pl.BlockSpec((N_PAIR, LANES), lambda c: (0, 0))
    return pl.pallas_call(
        body, name="ssd_bwd", grid=(nc,),
        in_specs=[pl.BlockSpec((L, SSM_CONV_DIM), lambda c: (rev(c), 0)), col, col, row, row,
                  pl.BlockSpec((1, N_PAIR, N, LANES), lambda c: (rev(c), 0, 0, 0)),
                  pl.BlockSpec((L, SSM_INNER), lambda c: (rev(c), 0)), pairs, vec],
        out_specs=[pl.BlockSpec((L, SSM_CONV_DIM), lambda c: (rev(c), 0)), col, vec, pairs],
        out_shape=[jax.ShapeDtypeStruct((s, SSM_CONV_DIM), F32), jax.ShapeDtypeStruct((s, LANES), F32),
                   jax.ShapeDtypeStruct((1, LANES), F32), jax.ShapeDtypeStruct((N_PAIR, LANES), F32)],
        scratch_shapes=[pltpu.VMEM((N_PAIR, N, LANES), F32)],
        compiler_params=_cp("arbitrary"),
    )(xbc_act, dt_c, ac_c, dt_r, ac_r, hprev_all, dy, dsk_pair, a_row)


ROWS = 256
GW = SSM_INNER // SSM_GROUPS


def _rows(width, dtype=F32):
    return pl.BlockSpec((ROWS, width), lambda i: (i, 0))


def _vec(width):
    return pl.BlockSpec((1, width), lambda i: (0, 0))


def _gnorm_fwd(y, z, w):
    s = y.shape[0]

    def body(y_ref, z_ref, w_ref, o_ref):
        for g in range(SSM_GROUPS):
            cs = slice(g * GW, (g + 1) * GW)
            zz = z_ref[:, cs]
            u = y_ref[:, cs] * (zz * _sigmoid(zz))
            r = lax.rsqrt(jnp.mean(u * u, axis=1, keepdims=True) + RMS_EPS)
            o_ref[:, cs] = (u * r * w_ref[:, cs]).astype(o_ref.dtype)

    return pl.pallas_call(body, name="gnorm_fwd", grid=(s // ROWS,),
                          in_specs=[_rows(SSM_INNER), _rows(SSM_INNER), _vec(SSM_INNER)], out_specs=_rows(SSM_INNER),
                          out_shape=jax.ShapeDtypeStruct((s, SSM_INNER), BF16), compiler_params=_cp("parallel"))(y, z, w)


def _gnorm_bwd(y, z, w, do):
    s = y.shape[0]

    def body(y_ref, z_ref, w_ref, do_ref, dy_ref, dz_ref, dw_ref):
        @pl.when(pl.program_id(0) == 0)
        def _():
            dw_ref[...] = jnp.zeros_like(dw_ref)

        for g in range(SSM_GROUPS):
            cs = slice(g * GW, (g + 1) * GW)
            zz, yy, dd = z_ref[:, cs], y_ref[:, cs], do_ref[:, cs]
            sz, dsz = _silu_and_grad(zz)
            u = yy * sz
            r = lax.rsqrt(jnp.mean(u * u, axis=1, keepdims=True) + RMS_EPS)
            n = u * r
            dn = dd * w_ref[:, cs]
            dw_ref[:, cs] += jnp.sum(dd * n, axis=0, keepdims=True)
            du = r * (dn - n * jnp.mean(dn * n, axis=1, keepdims=True))
            dy_ref[:, cs] = du * sz
            dz_ref[:, cs] = (du * yy * dsz).astype(dz_ref.dtype)

    return pl.pallas_call(
        body, name="gnorm_bwd", grid=(s // ROWS,),
        in_specs=[_rows(SSM_INNER), _rows(SSM_INNER), _vec(SSM_INNER), _rows(SSM_INNER)],
        out_specs=[_rows(SSM_INNER), _rows(SSM_INNER), _vec(SSM_INNER)],
        out_shape=[jax.ShapeDtypeStruct((s, SSM_INNER), F32), jax.ShapeDtypeStruct((s, SSM_INNER), BF16),
                   jax.ShapeDtypeStruct((1, SSM_INNER), F32)],
        compiler_params=_cp("arbitrary"))(y, z, w, do)


def _mix_fwd(gl, bg, attn_d, ssm_d):
    s = gl.shape[0]
    d = D_MODEL

    def body(gl_ref, bg_ref, a_ref, m_ref, o_ref):
        g0 = _sigmoid(gl_ref[:, :d] + bg_ref[:, :d])
        g1 = _sigmoid(gl_ref[:, d:] + bg_ref[:, d:])
        o_ref[...] = (g0 * a_ref[...] + g1 * m_ref[...]).astype(o_ref.dtype)

    return pl.pallas_call(body, name="mix_fwd", grid=(s // ROWS,),
                          in_specs=[_rows(2 * d), _vec(2 * d), _rows(d), _rows(d)], out_specs=_rows(d),
                          out_shape=jax.ShapeDtypeStruct((s, d), BF16), compiler_params=_cp("parallel"))(
        gl, bg, attn_d, ssm_d)


def _mix_bwd(gl, bg, attn_d, ssm_d, dmix):
    s = gl.shape[0]
    d = D_MODEL

    def body(gl_ref, bg_ref, a_ref, m_ref, dm_ref, da_ref, ds_ref, dg_ref, db_ref):
        @pl.when(pl.program_id(0) == 0)
        def _():
            db_ref[...] = jnp.zeros_like(db_ref)

        g0 = _sigmoid(gl_ref[:, :d] + bg_ref[:, :d])
        g1 = _sigmoid(gl_ref[:, d:] + bg_ref[:, d:])
        dm = dm_ref[...]
        da_ref[...] = (dm * g0).astype(da_ref.dtype)
        ds_ref[...] = (dm * g1).astype(ds_ref.dtype)
        dl0 = dm * a_ref[...] * g0 * (1.0 - g0)
        dl1 = dm * m_ref[...] * g1 * (1.0 - g1)
        dg_ref[:, :d] = dl0.astype(dg_ref.dtype)
        dg_ref[:, d:] = dl1.astype(dg_ref.dtype)
        db_ref[:, :d] += jnp.sum(dl0, axis=0, keepdims=True)
        db_ref[:, d:] += jnp.sum(dl1, axis=0, keepdims=True)

    return pl.pallas_call(
        body, name="mix_bwd", grid=(s // ROWS,),
        in_specs=[_rows(2 * d), _vec(2 * d), _rows(d), _rows(d), _rows(d)],
        out_specs=[_rows(d), _rows(d), _rows(2 * d), _vec(2 * d)],
        out_shape=[jax.ShapeDtypeStruct((s, d), BF16), jax.ShapeDtypeStruct((s, d), BF16),
                   jax.ShapeDtypeStruct((s, 2 * d), BF16), jax.ShapeDtypeStruct((1, 2 * d), F32)],
        compiler_params=_cp("arbitrary"))(gl, bg, attn_d, ssm_d, dmix)


def _ln_stats(p):
    mu = jnp.mean(p, axis=1, keepdims=True)
    c = p - mu
    rstd = lax.rsqrt(jnp.mean(c * c, axis=1, keepdims=True) + LN_EPS)
    return c * rstd, rstd


def _ln_bwd(dy, xhat, rstd, g):
    dxh = dy * g
    return rstd * (dxh - jnp.mean(dxh, axis=1, keepdims=True) - xhat * jnp.mean(dxh * xhat, axis=1, keepdims=True))


def _ln1_fwd(x, mixed, g, b):
    s, d = x.shape

    def body(x_ref, m_ref, g_ref, b_ref, o_ref):
        xhat, _ = _ln_stats(DEEPNORM_ALPHA * x_ref[...] + m_ref[...])
        o_ref[...] = xhat * g_ref[...] + b_ref[...]

    return pl.pallas_call(body, name="ln1_fwd", grid=(s // ROWS,), in_specs=[_rows(d), _rows(d), _vec(d), _vec(d)],
                          out_specs=_rows(d), out_shape=jax.ShapeDtypeStruct((s, d), F32),
                          compiler_params=_cp("parallel"))(x, mixed, g, b)


def _ln2_loss(x1, h, target, g, b):
    s, d = x1.shape

    def body(x_ref, h_ref, t_ref, g_ref, b_ref, dp_ref, loss_ref, dg_ref, db_ref):
        @pl.when(pl.program_id(0) == 0)
        def _():
            loss_ref[...] = jnp.zeros_like(loss_ref)
            dg_ref[...] = jnp.zeros_like(dg_ref)
            db_ref[...] = jnp.zeros_like(db_ref)

        xhat, rstd = _ln_stats(DEEPNORM_ALPHA * x_ref[...] + h_ref[...])
        err = xhat * g_ref[...] + b_ref[...] - t_ref[...]
        part = 0.5 * jnp.sum(jnp.mean(err * err, axis=1, keepdims=True), axis=0, keepdims=True)
        loss_ref[...] += jnp.broadcast_to(part, loss_ref.shape)
        dy = err * (1.0 / d)
        dg_ref[...] += jnp.sum(dy * xhat, axis=0, keepdims=True)
        db_ref[...] += jnp.sum(dy, axis=0, keepdims=True)
        dp_ref[...] = _ln_bwd(dy, xhat, rstd, g_ref[...])

    return pl.pallas_call(
        body, name="ln2_loss", grid=(s // ROWS,), in_specs=[_rows(d), _rows(d), _rows(d), _vec(d), _vec(d)],
        out_specs=[_rows(d), _vec(LANES), _vec(d), _vec(d)],
        out_shape=[jax.ShapeDtypeStruct((s, d), F32), jax.ShapeDtypeStruct((1, LANES), F32),
                   jax.ShapeDtypeStruct((1, d), F32), jax.ShapeDtypeStruct((1, d), F32)],
        compiler_params=_cp("arbitrary"))(x1, h, target, g, b)


def _ln1_bwd(x, mixed, g, dpre2, dffn):
    s, d = x.shape

    def body(x_ref, m_ref, g_ref, d2_ref, df_ref, dp_ref, dr_ref, dg_ref, db_ref):
        @pl.when(pl.program_id(0) == 0)
        def _():
            dg_ref[...] = jnp.zeros_like(dg_ref)
            db_ref[...] = jnp.zeros_like(db_ref)

        xhat, rstd = _ln_stats(DEEPNORM_ALPHA * x_ref[...] + m_ref[...])
        dy = DEEPNORM_ALPHA * d2_ref[...] + df_ref[...]
        dg_ref[...] += jnp.sum(dy * xhat, axis=0, keepdims=True)
        db_ref[...] += jnp.sum(dy, axis=0, keepdims=True)
        dp = _ln_bwd(dy, xhat, rstd, g_ref[...])
        dp_ref[...] = dp
        dr_ref[...] = DEEPNORM_ALPHA * dp

    return pl.pallas_call(
        body, name="ln1_bwd", grid=(s // ROWS,), in_specs=[_rows(d), _rows(d), _vec(d), _rows(d), _rows(d)],
        out_specs=[_rows(d), _rows(d), _vec(d), _vec(d)],
        out_shape=[jax.ShapeDtypeStruct((s, d), F32), jax.ShapeDtypeStruct((s, d), F32),
                   jax.ShapeDtypeStruct((1, d), F32), jax.ShapeDtypeStruct((1, d), F32)],
        compiler_params=_cp("arbitrary"))(x, mixed, g, dpre2, dffn)


def _swiglu_fwd(gu):
    s = gu.shape[0]
    f = FFN_HIDDEN

    def body(g_ref, u_ref, o_ref):
        gg = g_ref[...]
        o_ref[...] = (gg * _sigmoid(gg) * u_ref[...]).astype(o_ref.dtype)

    return pl.pallas_call(
        body, name="swiglu_fwd", grid=(s // ROWS,),
        in_specs=[pl.BlockSpec((ROWS, f), lambda i: (i, 0)), pl.BlockSpec((ROWS, f), lambda i: (i, 1))],
        out_specs=_rows(f), out_shape=jax.ShapeDtypeStruct((s, f), BF16), compiler_params=_cp("parallel"))(gu, gu)


def _swiglu_bwd(gu, dact):
    s = gu.shape[0]
    f = FFN_HIDDEN

    def body(g_ref, u_ref, d_ref, o_ref):
        sg, dsg = _silu_and_grad(g_ref[...])
        dd = d_ref[...]
        o_ref[:, :f] = (dd * u_ref[...] * dsg).astype(o_ref.dtype)
        o_ref[:, f:] = (dd * sg).astype(o_ref.dtype)

    return pl.pallas_call(
        body, name="swiglu_bwd", grid=(s // ROWS,),
        in_specs=[pl.BlockSpec((ROWS, f), lambda i: (i, 0)), pl.BlockSpec((ROWS, f), lambda i: (i, 1)), _rows(f)],
        out_specs=_rows(2 * f), out_shape=jax.ShapeDtypeStruct((s, 2 * f), BF16),
        compiler_params=_cp("parallel"))(gu, gu, dact)


def _peer(k):
    x, y, c = lax.axis_index("x"), lax.axis_index("y"), lax.axis_index("c")
    kx, ky, kc = (k >> 2) & 1, (k >> 1) & 1, k & 1
    px = (1 - x) if kx else x
    py = (1 - y) if ky else y
    pc = (1 - c) if kc else c
    return (px, py, pc), 4 * px + 2 * py + pc


def _my_index():
    return 4 * lax.axis_index("x") + 2 * lax.axis_index("y") + lax.axis_index("c")


def _comm_copies(ins, outs, sems, scatter):
    send_sems, recv_sems, local_sems = sems
    me = _my_index()
    copies = [pltpu.make_async_copy(ins[t].at[me] if scatter else ins[t], outs[t].at[me], local_sems.at[t])
              for t in range(len(ins))]
    for k in range(1, N_DEV):
        peer, pidx = _peer(k)
        for t in range(len(ins)):
            copies.append(pltpu.make_async_remote_copy(
                src_ref=ins[t].at[pidx] if scatter else ins[t], dst_ref=outs[t].at[me],
                send_sem=send_sems.at[t, k - 1], recv_sem=recv_sems.at[t, k - 1], device_id=peer,
                device_id_type=pl.DeviceIdType.MESH))
    return copies


def _comm_sems(n):
    return [pltpu.SemaphoreType.DMA((n, N_DEV - 1)), pltpu.SemaphoreType.DMA((n, N_DEV - 1)),
            pltpu.SemaphoreType.DMA((n,))]


def _comm_out_shapes(parts, scatter):
    return [jax.ShapeDtypeStruct(p.shape if scatter else (N_DEV,) + p.shape, p.dtype) for p in parts]


def _comm_call(parts, scatter, name):
    n = len(parts)

    def body(*refs):
        copies = _comm_copies(refs[:n], refs[n:2 * n], refs[2 * n:], scatter)
        for cp in copies:
            cp.start()
        for cp in copies:
            cp.wait()

    anyspec = pl.BlockSpec(memory_space=pl.ANY)
    return pl.pallas_call(body, name=name, in_specs=[anyspec] * n, out_specs=[anyspec] * n,
                          out_shape=_comm_out_shapes(parts, scatter), scratch_shapes=_comm_sems(n))(*parts)


def _all_gather(parts):
    return _comm_call(parts, False, "all_gather")


def _exchange(parts):
    return _comm_call(parts, True, "grad_exchange")


def _adamw(recv, w, m, v, name):
    r, c = w.shape
    br = _tile(r, 128)
    c1 = 1.0 / (1.0 - ADAM_B1 ** ADAM_STEP)
    c2 = 1.0 / (1.0 - ADAM_B2 ** ADAM_STEP)

    def body(r_ref, w_ref, m_ref, v_ref, g_ref, d_ref, mo_ref, vo_ref):
        g = r_ref[0].astype(F32)
        for k in range(1, N_DEV):
            g = g + r_ref[k].astype(F32)
        mn = ADAM_B1 * m_ref[...] + (1.0 - ADAM_B1) * g
        vn = ADAM_B2 * v_ref[...] + (1.0 - ADAM_B2) * (g * g)
        g_ref[...] = g
        mo_ref[...] = mn
        vo_ref[...] = vn
        d_ref[...] = -ADAM_LR * ((mn * c1) / (jnp.sqrt(vn * c2) + ADAM_EPS) + ADAM_WD * w_ref[...])

    blk = pl.BlockSpec((br, c), lambda i: (i, 0))
    return pl.pallas_call(
        body, name=name, grid=(r // br,),
        in_specs=[pl.BlockSpec((N_DEV, br, c), lambda i: (0, i, 0)), blk, blk, blk],
        out_specs=[blk] * 4, out_shape=[jax.ShapeDtypeStruct((r, c), F32)] * 4,
        compiler_params=_cp("parallel"))(recv, w, m, v)


def _lane_row(pairs):
    row = jnp.zeros((LANES,), F32)
    for lane0, vec in pairs:
        row = lax.dynamic_update_slice(row, vec.astype(F32), (lane0,))
    return row.reshape(1, LANES)


def _stage_in(x, wts, small):
    s = x.shape[0]
    a = -jnp.exp(small["a_log"])
    bias_row = _lane_row([(DT_LANE0, small["dt_bias"]), (F_LANE0, small["b_forget"])])
    a_row = _lane_row([(DT_LANE0, a)])
    conv_b = small["conv_b"].reshape(1, -1)
    norm_w = small["ssm_norm_w"].reshape(1, -1)
    bg = small["b_gates"].reshape(1, -1)
    g1, b1 = small["ln1_g"].reshape(1, -1), small["ln1_b"].reshape(1, -1)
    g2, b2 = small["ln2_g"].reshape(1, -1), small["ln2_b"].reshape(1, -1)
    d_skip = small["d_skip"]
    xb = x.astype(BF16)

    qkv = _mm(xb, wts["qkv"], out_dtype=BF16, name="f_qkv")
    z = _mm(xb, wts["z"], name="f_z")
    xbc = _mm(xb, wts["xbc"], name="f_xbc")
    gl = _mm(xb, wts["gate"], name="f_gate")
    fd = _mm(xb, wts["fd"], name="f_fd")
    dt_c, ac_c, cf_c, dt_r, ac_r, cf_r = _stats_fwd(fd, bias_row, a_row)
    bk = _att_blocks(s)[1]
    ck4 = cf_r[F_LANE0:F_LANE0 + ATT_HEADS].reshape(N_HP, HP, s // bk, bk)
    return dict(locals())


def _stage_mid(c, attn, lse, wts, target):
    x, xb, qkv, z, xbc, gl, fd, ck4 = (c[k] for k in ("x", "xb", "qkv", "z", "xbc", "gl", "fd", "ck4"))
    dt_c, ac_c, dt_r, ac_r, a_row, bias_row = (c[k] for k in ("dt_c", "ac_c", "dt_r", "ac_r", "a_row", "bias_row"))
    conv_b, norm_w, bg, g1, b1, g2, b2, d_skip = (c[k] for k in ("conv_b", "norm_w", "bg", "g1", "b1", "g2", "b2",
                                                                "d_skip"))
    conv_w = c["wts"]["conv"]
    attn_d = _mm(attn, wts["pa"], name="f_pa")
    xact = _conv_fwd(xbc, conv_w, conv_b)
    dsk_pair = jnp.repeat(d_skip, SSM_HEAD_DIM).reshape(N_PAIR, LANES)
    y, hprev = _ssd_pair_fwd(xact, ac_c, dt_r, ac_r, dsk_pair)
    ssm = _gnorm_fwd(y, z, norm_w)
    ssm_d = _mm(ssm, wts["ps"], name="f_ps")
    mix = _mix_fwd(gl, bg, attn_d, ssm_d)
    mixed = _mm(mix, wts["out"], name="f_out")
    x1 = _ln1_fwd(x, mixed, g1, b1)
    gu = _mm(x1, wts["gu"], name="f_gu")
    act = _swiglu_fwd(gu)
    h = _mm(act, wts["down"], name="f_down")
    dpre2, loss_row, dg2, db2 = _ln2_loss(x1, h, target, g2, b2)

    d_act = _mm(dpre2, wts["down"], tb=True, name="b_down_x")
    dw_down = _mm(act, dpre2, ta=True, name="b_down_w")
    dgu = _swiglu_bwd(gu, d_act)
    dffn = _mm(dgu, wts["gu"], tb=True, name="b_gu_x")
    dw_gu = _mm(x1, dgu, ta=True, name="b_gu_w")
    dpre1, dxr, dg1, db1 = _ln1_bwd(x, mixed, g1, dpre2, dffn)
    dmix = _mm(dpre1, wts["out"], tb=True, name="b_out_x")
    dw_out = _mm(mix, dpre1, ta=True, name="b_out_w")
    dattn_d, dssm_d, dgl, dbg = _mix_bwd(gl, bg, attn_d, ssm_d, dmix)
    dssm = _mm(dssm_d, wts["ps"], tb=True, name="b_ps_x")
    dw_ps = _mm(ssm, dssm_d, ta=True, name="b_ps_w")
    dattn = _mm(dattn_d, wts["pa"], tb=True, name="b_pa_x")
    dw_pa = _mm(attn, dattn_d, ta=True, name="b_pa_w")
    dy, dz, dnw = _gnorm_bwd(y, z, norm_w, dssm)
    dxact, ddt, da_row, dds_pair = _ssd_pair_bwd(xact, dt_c, ac_c, dt_r, ac_r, hprev, dy, dsk_pair, a_row)
    dds = dds_pair.reshape(SSM_HEADS, SSM_HEAD_DIM).sum(axis=1)
    dpre_c, dconv_w, dconv_b = _conv_bwd_pre(xbc, conv_w, conv_b, dxact)
    dxbc = _conv_bwd_in(dpre_c, conv_w)
    st, do_b = _att_prep(dattn, attn, lse)
    late = dict(pa=dw_pa, ps=dw_ps, out=dw_out, gu=dw_gu, down=dw_down)
    keep = ("st", "do_b", "ddt", "dxr", "dz", "dxbc", "dgl", "dconv_w", "dconv_b", "da_row", "dds", "dnw", "dbg",
            "dg1", "db1", "dg2", "db2", "loss_row")
    loc = locals()
    return {**c, **{k: loc[k] for k in keep}}, late


def _stage_out(c, att_grads):
    dq, dk, dv, dck, dcq = att_grads
    wts, xb, fd, bias_row, ddt, dxr, dz, dxbc, dgl = (c[k] for k in ("wts", "xb", "fd", "bias_row", "ddt", "dxr", "dz",
                                                                  "dxbc", "dgl"))
    s, d, a = xb.shape[0], D_MODEL, c["a"]
    dck_rows = jnp.zeros((LANES, s), F32).at[F_LANE0:F_LANE0 + ATT_HEADS].set(dck.reshape(ATT_HEADS, s))
    dcq_cols = jnp.zeros((s, LANES), F32).at[:, F_LANE0:F_LANE0 + ATT_HEADS].set(dcq[:, ::ATT_HEAD_DIM])
    dfd, dbias = _stats_bwd(fd, bias_row, ddt, dck_rows, dcq_cols)

    wq, wk, wv = wts["qkv"][:, :d], wts["qkv"][:, d:2 * d], wts["qkv"][:, 2 * d:]
    dx = dxr
    for i, (g_, w_) in enumerate(((dq, wq), (dk, wk), (dv, wv), (dz, wts["z"]), (dxbc, wts["xbc"]),
                                  (dgl, wts["gate"]), (dfd, wts["fd"]))):
        dx = _mm(g_, w_, tb=True, add=dx, name=f"b_in_x{i}")
    dw_in = [_mm(xb, g_, ta=True, name=f"b_in_w{i}") for i, g_ in enumerate((dq, dk, dv, dz, dxbc, dgl, dfd))]

    grads = dict(q=dw_in[0], k=dw_in[1], v=dw_in[2], z=dw_in[3], xbc=dw_in[4], gate=dw_in[5], fd=dw_in[6],
                 conv=c["dconv_w"])
    small_g = dict(
        b_forget=dbias[0, F_LANE0:F_LANE0 + ATT_HEADS], conv_b=c["dconv_b"][0], dt_bias=dbias[0, :SSM_HEADS],
        a_log=c["da_row"][0, :SSM_HEADS] * a, d_skip=c["dds"], ssm_norm_w=c["dnw"][0], b_gates=c["dbg"][0],
        ln1_g=c["dg1"][0], ln1_b=c["db1"][0], ln2_g=c["dg2"][0], ln2_b=c["db2"][0])
    return c["loss_row"][0, 0], dx, grads, small_g


BIG = ("w_in", "w_proj_attn", "w_proj_ssm", "w_out", "w_ffn_gate", "w_ffn_up", "w_ffn_down", "conv_w")
EARLY = ("w_in", "conv_w")
LATE = ("w_proj_attn", "w_proj_ssm", "w_out", "w_ffn_gate", "w_ffn_up", "w_ffn_down")
SMALL = ("b_forget", "conv_b", "dt_bias", "a_log", "d_skip", "ssm_norm_w", "b_gates", "ln1_g", "ln1_b", "ln2_g",
         "ln2_b")
SMALL_ROWS = 96
IN_SHARD = IN_WIDTH // N_DEV
IN_SEGMENTS = (("q", 0, 1024), ("k", 1024, 1024), ("v", 2048, 1024), ("f", 3072, ATT_HEADS), ("z", 3088, SSM_INNER),
               ("xbc", 5136, SSM_CONV_DIM), ("dt", 8208, SSM_HEADS), ("gate", 8240, 2 * D_MODEL))


def _cols_from_shards(shards, lo, hi):
    w = shards[0].shape[1]
    pieces = []
    for j in range(len(shards)):
        a, b = max(lo, j * w), min(hi, (j + 1) * w)
        if a < b:
            pieces.append(shards[j][:, a - j * w:b - j * w])
    return pieces[0] if len(pieces) == 1 else jnp.concatenate(pieces, axis=1)


def _shards_from_parts(parts, width):
    shards = []
    for j in range(N_DEV):
        lo, hi = j * width, (j + 1) * width
        pieces = []
        for mat, c0 in parts:
            a, b = max(lo, c0), min(hi, c0 + mat.shape[1])
            if a < b:
                pieces.append(mat[:, a - c0:b - c0])
        shards.append(pieces[0] if len(pieces) == 1 else jnp.concatenate(pieces, axis=1))
    return shards


def _pack_small(vals):
    flat = jnp.concatenate([vals[n].reshape(-1) for n in SMALL])
    return jnp.pad(flat, (0, SMALL_ROWS * LANES - flat.shape[0])).reshape(SMALL_ROWS, LANES)


def _unpack_small(pack, shapes):
    flat = pack.reshape(-1)
    out, off = {}, 0
    for n in SMALL:
        sz = math.prod(shapes[n])
        out[n] = flat[off:off + sz].reshape(shapes[n])
        off += sz
    return out


def kernel(x, w_in, b_forget, conv_w, conv_b, dt_bias, a_log, d_skip, ssm_norm_w, w_proj_attn, w_proj_ssm, b_gates, w_out, ln1_g, ln1_b, w_ffn_gate, w_ffn_up, w_ffn_down, ln2_g, ln2_b, loss_target, m_w_in, m_b_forget, m_conv_w, m_conv_b, m_dt_bias, m_a_log, m_d_skip, m_ssm_norm_w, m_w_proj_attn, m_w_proj_ssm, m_b_gates, m_w_out, m_ln1_g, m_ln1_b, m_w_ffn_gate, m_w_ffn_up, m_w_ffn_down, m_ln2_g, m_ln2_b, v_w_in, v_b_forget, v_conv_w, v_conv_b, v_dt_bias, v_a_log, v_d_skip, v_ssm_norm_w, v_w_proj_attn, v_w_proj_ssm, v_b_gates, v_w_out, v_ln1_g, v_ln1_b, v_w_ffn_gate, v_w_ffn_up, v_w_ffn_down, v_ln2_g, v_ln2_b):
    args = dict(locals())
    d, f = D_MODEL, FFN_HIDDEN
    big_w = {n: args[n][0] for n in BIG}
    small_w = {n: args[n][0] for n in SMALL}
    big_shapes = {n: args[n].shape for n in BIG}
    small_shapes = {n: args[n].shape for n in SMALL}

    early = dict(zip(EARLY, _all_gather([big_w["w_in"].astype(BF16), big_w["conv_w"]])))
    in_shards = [early["w_in"][j] for j in range(N_DEV)]
    seg = {n: _cols_from_shards(in_shards, c0, c0 + w) for n, c0, w in IN_SEGMENTS}
    wfd = jnp.concatenate([seg["dt"], seg["f"], jnp.zeros((d, LANES - SSM_HEADS - ATT_HEADS), BF16)], axis=1)
    wts = dict(qkv=jnp.concatenate([seg["q"], seg["k"], seg["v"]], axis=1), z=seg["z"], xbc=seg["xbc"],
               gate=seg["gate"], fd=wfd, conv=jnp.concatenate([early["conv_w"][j] for j in range(N_DEV)], axis=1))

    ctx = _stage_in(x[0], wts, small_w)
    attn, lse, gathered = _attention_fwd(ctx["qkv"], ctx["ck4"], [big_w[n].astype(BF16) for n in LATE])
    full = dict(zip(LATE, gathered))
    late_w = dict(
        pa=full["w_proj_attn"].reshape(d, d), ps=full["w_proj_ssm"].reshape(SSM_INNER, d),
        out=full["w_out"].reshape(d, d),
        gu=jnp.concatenate([full["w_ffn_gate"][j] for j in range(N_DEV)]
                           + [full["w_ffn_up"][j] for j in range(N_DEV)], axis=1),
        down=full["w_ffn_down"].reshape(f, d))
    ctx, gl = _stage_mid(ctx, attn, lse, late_w, loss_target[0])
    late_dest = dict(
        w_ffn_gate=jnp.stack([s_.astype(BF16) for s_ in _shards_from_parts([(gl["gu"][:, :f], 0)], f // N_DEV)]),
        w_ffn_up=jnp.stack([s_.astype(BF16) for s_ in _shards_from_parts([(gl["gu"][:, f:], 0)], f // N_DEV)]))
    for n, key in (("w_proj_attn", "pa"), ("w_proj_ssm", "ps"), ("w_out", "out"), ("w_ffn_down", "down")):
        late_dest[n] = gl[key].astype(BF16).reshape((N_DEV,) + big_shapes[n][1:])
    att_grads, late_recv = _attention_bwd(ctx["qkv"], ctx["ck4"], ctx["st"], ctx["do_b"], [late_dest[n] for n in LATE])
    loss_part, grad_x, g, small_g = _stage_out(ctx, att_grads)
    loss = lax.psum(loss_part, ("x", "y", "c"))

    gfd = g["fd"]
    in_parts = dict(q=g["q"], k=g["k"], v=g["v"], f=gfd[:, F_LANE0:F_LANE0 + ATT_HEADS], z=g["z"], xbc=g["xbc"],
                    dt=gfd[:, DT_LANE0:DT_LANE0 + SSM_HEADS], gate=g["gate"])
    win_dest = jnp.stack([s_.astype(BF16) for s_ in
                          _shards_from_parts([(in_parts[n], c0) for n, c0, _ in IN_SEGMENTS], IN_SHARD)])
    conv_dest = jnp.stack(_shards_from_parts([(g["conv"], 0)], SSM_CONV_DIM // N_DEV))
    small_pack = _pack_small(small_g)
    early_recv = _exchange([win_dest, conv_dest, jnp.broadcast_to(small_pack, (N_DEV,) + small_pack.shape)])
    recv = dict(zip(LATE, late_recv))
    recv["w_in"], recv["conv_w"] = early_recv[0], early_recv[1]

    outs = {}
    for n in BIG:
        shp = big_shapes[n]
        res4 = _adamw(recv[n], big_w[n], args["m_" + n][0], args["v_" + n][0], name="adamw_" + n)
        outs[n] = [r.reshape(shp) for r in res4]
    small4 = _adamw(early_recv[2], _pack_small(small_w), _pack_small({n: args["m_" + n][0] for n in SMALL}),
                    _pack_small({n: args["v_" + n][0] for n in SMALL}), name="adamw_small")
    small_out = [_unpack_small(p, small_shapes) for p in small4]
    for n in SMALL:
        outs[n] = [so[n] for so in small_out]

    order = ("w_in", "b_forget", "conv_w", "conv_b", "dt_bias", "a_log", "d_skip", "ssm_norm_w", "w_proj_attn",
             "w_proj_ssm", "b_gates", "w_out", "ln1_g", "ln1_b", "w_ffn_gate", "w_ffn_up", "w_ffn_down", "ln2_g",
             "ln2_b")
    res = [loss, grad_x[None]]
    for i in range(4):
        res += [outs[n][i] for n in order]
    return tuple(res)
```

```python
import functools
import math

import jax
import jax.numpy as jnp
from jax import lax
from jax.experimental import pallas as pl
from jax.experimental.pallas import tpu as pltpu

F32 = jnp.float32
BF16 = jnp.bfloat16

N_DEV = 8
D_MODEL = 1024
ATT_HEADS = 16
ATT_HEAD_DIM = 64
SSM_INNER = 2048
SSM_HEADS = 32
SSM_HEAD_DIM = 64
SSM_GROUPS = 4
SSM_HEADS_PER_GROUP = 8
SSM_STATE = 128
SSM_CONV = 4
SSM_CHUNK = 128
SSM_CONV_DIM = 3072
FFN_HIDDEN = 2816
IN_WIDTH = 10288
DEEPNORM_ALPHA = 2.0 ** 0.25
LN_EPS = 1e-5
RMS_EPS = 1e-5
ADAM_LR, ADAM_B1, ADAM_B2, ADAM_EPS, ADAM_WD, ADAM_STEP = 0.001, 0.9, 0.999, 1e-08, 0.01, 10
ATT_SCALE = 1.0 / math.sqrt(ATT_HEAD_DIM)

LANES = 128
VMEM_LIMIT = 56 * 1024 * 1024
NEG = -1e30

DT_LANE0 = 0
F_LANE0 = 32
HI = lax.Precision.HIGHEST


def _cp(*sem):
    return pltpu.CompilerParams(dimension_semantics=sem, vmem_limit_bytes=VMEM_LIMIT)


def _tile(n, cap=1408):
    for t in (1408, 1024, 512, 384, 256, 128):
        if t <= cap and n % t == 0:
            return t
    return n


def _sigmoid(x):
    return 1.0 / (1.0 + jnp.exp(-x))


def _mm(a, b, *, ta=False, tb=False, out_dtype=F32, add=None, name):
    m, k = (a.shape[1], a.shape[0]) if ta else a.shape
    n = b.shape[0] if tb else b.shape[1]
    assert (b.shape[1] if tb else b.shape[0]) == k
    tm, tn, tk = _tile(m), _tile(n), _tile(k)
    nk = k // tk
    dims = (((0,) if ta else (1,), (1,) if tb else (0,)), ((), ()))

    def body_single(*refs):
        a_ref, b_ref = refs[:2]
        r = lax.dot_general(a_ref[...].astype(BF16), b_ref[...].astype(BF16), dims, preferred_element_type=F32)
        if add is not None:
            r = r + refs[2][...]
        refs[-1][...] = r.astype(refs[-1].dtype)

    def body(*refs):
        if add is None:
            a_ref, b_ref, o_ref, acc_ref = refs
        else:
            a_ref, b_ref, c_ref, o_ref, acc_ref = refs
        kk = pl.program_id(2)

        @pl.when(kk == 0)
        def _():
            acc_ref[...] = jnp.zeros_like(acc_ref)

        acc_ref[...] += lax.dot_general(a_ref[...].astype(BF16), b_ref[...].astype(BF16), dims,
                                        preferred_element_type=F32)

        @pl.when(kk == nk - 1)
        def _():
            r = acc_ref[...]
            if add is not None:
                r = r + c_ref[...]
            o_ref[...] = r.astype(o_ref.dtype)

    a_spec = pl.BlockSpec((tk, tm), lambda i, j, kk: (kk, i)) if ta else pl.BlockSpec((tm, tk), lambda i, j, kk: (i, kk))
    b_spec = pl.BlockSpec((tn, tk), lambda i, j, kk: (j, kk)) if tb else pl.BlockSpec((tk, tn), lambda i, j, kk: (kk, j))
    o_spec = pl.BlockSpec((tm, tn), lambda i, j, kk: (i, j))
    in_specs, args = [a_spec, b_spec], [a, b]
    if add is not None:
        in_specs.append(o_spec)
        args.append(add)
    return pl.pallas_call(
        body_single if nk == 1 else body, name=name, grid=(m // tm, n // tn, nk), in_specs=in_specs, out_specs=o_spec,
        out_shape=jax.ShapeDtypeStruct((m, n), out_dtype),
        scratch_shapes=[] if nk == 1 else [pltpu.VMEM((tm, tn), F32)],
        compiler_params=_cp("parallel", "parallel", "arbitrary"),
    )(*args)


def _tri(n, lower=True):
    r = lax.broadcasted_iota(jnp.int32, (n, n), 0)
    c = lax.broadcasted_iota(jnp.int32, (n, n), 1)
    return jnp.where((r >= c) if lower else (c >= r), 1.0, 0.0).astype(F32)


def _stats_fwd(fd, bias_row, a_row):
    s = fd.shape[0]
    blk = SSM_CHUNK

    def body(fd_ref, bias_ref, a_ref, dt_ref, ac_ref, cf_ref, dtr_ref, acr_ref, cfr_ref, carry_ref):
        @pl.when(pl.program_id(0) == 0)
        def _():
            carry_ref[...] = jnp.zeros_like(carry_ref)

        v = fd_ref[...] + bias_ref[...]
        dt = jnp.maximum(v, 0.0) + jnp.log(1.0 + jnp.exp(-jnp.abs(v)))
        lf = jnp.minimum(v, 0.0) - jnp.log(1.0 + jnp.exp(-jnp.abs(v)))
        tri = _tri(blk)
        ac = jnp.dot(tri, dt * a_ref[...], precision=HI, preferred_element_type=F32)
        cf = jnp.dot(tri, lf, precision=HI, preferred_element_type=F32) + carry_ref[0:1, :]
        carry_ref[...] = carry_ref[...] + jnp.sum(lf, axis=0, keepdims=True)
        dt_ref[...] = dt
        ac_ref[...] = ac
        cf_ref[...] = cf
        dtr_ref[...] = dt.T
        acr_ref[...] = ac.T
        cfr_ref[...] = cf.T

    col = pl.BlockSpec((blk, LANES), lambda i: (i, 0))
    row = pl.BlockSpec((LANES, blk), lambda i: (0, i))
    vec = pl.BlockSpec((1, LANES), lambda i: (0, 0))
    return pl.pallas_call(
        body, name="stats_fwd", grid=(s // blk,), in_specs=[col, vec, vec],
        out_specs=[col, col, col, row, row, row],
        out_shape=[jax.ShapeDtypeStruct((s, LANES), F32)] * 3 + [jax.ShapeDtypeStruct((LANES, s), F32)] * 3,
        scratch_shapes=[pltpu.VMEM((8, LANES), F32)],
        compiler_params=_cp("arbitrary"),
    )(fd, bias_row, a_row)


def _stats_bwd(fd, bias_row, ddt, dck_rows, dcq_cols):
    s = fd.shape[0]
    blk = SSM_CHUNK
    nb = s // blk

    def body(fd_ref, bias_ref, ddt_ref, dck_ref, dcq_ref, o_ref, db_ref, carry_ref):
        @pl.when(pl.program_id(0) == 0)
        def _():
            carry_ref[...] = jnp.zeros_like(carry_ref)
            db_ref[...] = jnp.zeros_like(db_ref)

        v = fd_ref[...] + bias_ref[...]
        dcum = dck_ref[...].T + dcq_ref[...]
        dlf = jnp.dot(_tri(blk, lower=False), dcum, precision=HI, preferred_element_type=F32) + carry_ref[0:1, :]
        carry_ref[...] = carry_ref[...] + jnp.sum(dcum, axis=0, keepdims=True)
        lane = lax.broadcasted_iota(jnp.int32, v.shape, 1)
        g = jnp.where(lane < F_LANE0, ddt_ref[...] * _sigmoid(v), dlf * _sigmoid(-v))
        g = jnp.where(lane < F_LANE0 + ATT_HEADS, g, 0.0)
        o_ref[...] = g.astype(o_ref.dtype)
        db_ref[...] += jnp.sum(g, axis=0, keepdims=True)

    col = pl.BlockSpec((blk, LANES), lambda i: (nb - 1 - i, 0))
    row = pl.BlockSpec((LANES, blk), lambda i: (0, nb - 1 - i))
    vec = pl.BlockSpec((1, LANES), lambda i: (0, 0))
    return pl.pallas_call(
        body, name="stats_bwd", grid=(nb,), in_specs=[col, vec, col, row, col], out_specs=[col, vec],
        out_shape=[jax.ShapeDtypeStruct((s, LANES), BF16), jax.ShapeDtypeStruct((1, LANES), F32)],
        scratch_shapes=[pltpu.VMEM((8, LANES), F32)],
        compiler_params=_cp("arbitrary"),
    )(fd, bias_row, ddt, dck_rows, dcq_cols)


HP = LANES // ATT_HEAD_DIM
N_HP = ATT_HEADS // HP


def _att_blocks(s):
    return (512, 1024) if s % 1024 == 0 and s >= 4096 else (64, 128)


_QK = (((1,), (1,)), ((), ()))
_HALF = ATT_HEAD_DIM // 2


def _head_cols(a):
    return slice(a * ATT_HEAD_DIM, (a + 1) * ATT_HEAD_DIM)


def _causal(shape, off):
    r = lax.broadcasted_iota(jnp.int32, shape, 0)
    c = lax.broadcasted_iota(jnp.int32, shape, 1)
    return c <= r + off


def _attention_fwd(qkv, ck4, gather_parts):
    s = qkv.shape[0]
    bq, bk = _att_blocks(s)
    nq, nk = s // bq, s // bk
    n = len(gather_parts)

    def body(q_ref, k_ref, v_ref, ck_ref, *rest):
        comm_in, (o_ref, lse_ref), comm_out, sems = rest[:n], rest[n:n + 2], rest[n + 2:2 * n + 2], rest[2 * n + 2:]
        i = pl.program_id(1)
        if n:
            @pl.when((pl.program_id(0) == 0) & (i == 0))
            def _():
                for cp in _comm_copies(comm_in, comm_out, sems, False):
                    cp.start()

        n_full = (i * bq) // bk
        qs = [(q_ref[:, _head_cols(a)].astype(F32) * ATT_SCALE).astype(BF16) for a in range(HP)]

        upper_v = lax.broadcasted_iota(jnp.int32, (bk, LANES), 1) >= ATT_HEAD_DIM
        upper_q = lax.broadcasted_iota(jnp.int32, (bq, LANES), 1) >= ATT_HEAD_DIM

        def step(j, carry, off=None):
            ks = pl.ds(pl.multiple_of(j * bk, bk), bk)
            v_both = v_ref[ks, :]
            out = []
            for a in range(HP):
                m, acc = carry[a]
                sc = lax.dot_general(qs[a], k_ref[ks, _head_cols(a)], _QK, preferred_element_type=F32)
                sc = sc - ck_ref[0, a, pl.ds(j, 1), :]
                if off is not None:
                    sc = jnp.where(_causal(sc.shape, off), sc, NEG)
                m_new = jnp.maximum(m, jnp.max(sc, axis=1, keepdims=True))
                p = jnp.exp((sc - m_new).astype(BF16))
                v_aug = jnp.where(upper_v == (a == 1), v_both, jnp.ones_like(v_both))
                acc = jnp.exp(m - m_new) * acc + jnp.dot(p, v_aug, preferred_element_type=F32)
                out.append((m_new, acc))
            return tuple(out)

        init = tuple((jnp.full((bq, 1), NEG, F32), jnp.zeros((bq, LANES), F32)) for _ in range(HP))
        carry = lax.fori_loop(0, n_full, step, init)
        carry = step(n_full, carry, off=i * bq - n_full * bk)
        outs, lses = [], []
        for a in range(HP):
            m, acc = carry[a]
            l = pltpu.roll(acc, ATT_HEAD_DIM, 1)
            outs.append(acc / l)
            lses.append(m + jnp.log(l))
        o_ref[...] = jnp.where(upper_q, outs[1], outs[0])
        lse_ref[...] = jnp.where(upper_q, lses[1], lses[0])
        if n:
            @pl.when((pl.program_id(0) == N_HP - 1) & (i == nq - 1))
            def _():
                for cp in _comm_copies(comm_in, comm_out, sems, False):
                    cp.wait()

    q_spec = pl.BlockSpec((bq, LANES), lambda h, i: (i, h))
    anyspec = pl.BlockSpec(memory_space=pl.ANY)
    res = pl.pallas_call(
        body, name="att_fwd", grid=(N_HP, nq),
        in_specs=[q_spec, pl.BlockSpec((s, LANES), lambda h, i: (0, N_HP + h)),
                  pl.BlockSpec((s, LANES), lambda h, i: (0, 2 * N_HP + h)),
                  pl.BlockSpec((1, HP, nk, bk), lambda h, i: (h, 0, 0, 0))] + [anyspec] * n,
        out_specs=[q_spec, q_spec] + [anyspec] * n,
        out_shape=[jax.ShapeDtypeStruct((s, D_MODEL), F32)] * 2 + _comm_out_shapes(gather_parts, False),
        scratch_shapes=_comm_sems(n) if n else [],
        compiler_params=_cp("arbitrary", "arbitrary"),
    )(qkv, qkv, qkv, ck4, *gather_parts)
    return res[0], res[1], list(res[2:])


def _att_prep(do, o, lse_rep):
    s = do.shape[0]
    bs = _tile(s, 512)

    def body(do_ref, o_ref, lse_ref, st_ref, dob_ref):
        r = lax.broadcasted_iota(jnp.int32, (LANES, LANES), 0) // ATT_HEAD_DIM
        c = lax.broadcasted_iota(jnp.int32, (LANES, LANES), 1) // ATT_HEAD_DIM
        e = jnp.where(r == c, 1.0, 0.0).astype(F32)
        lane = lax.broadcasted_iota(jnp.int32, (bs, LANES), 1)
        for p in range(D_MODEL // LANES):
            cs = slice(p * LANES, (p + 1) * LANES)
            dd = do_ref[:, cs]
            delta = jnp.dot(dd * o_ref[:, cs], e, precision=HI, preferred_element_type=F32)
            st_ref[:, cs] = jnp.where(lane % ATT_HEAD_DIM < _HALF, lse_ref[:, cs], delta)
            dob_ref[:, cs] = dd.astype(BF16)

    spec = pl.BlockSpec((bs, D_MODEL), lambda i: (i, 0))
    return pl.pallas_call(body, name="att_prep", grid=(s // bs,), in_specs=[spec, spec, spec], out_specs=[spec, spec],
                          out_shape=[jax.ShapeDtypeStruct((s, D_MODEL), F32), jax.ShapeDtypeStruct((s, D_MODEL), BF16)],
                          compiler_params=_cp("parallel"))(do, o, lse_rep)


def _attention_bwd(qkv, ck4, st, do_b, exchange_parts):
    s = qkv.shape[0]
    bq, bk = _att_blocks(s)
    nq, nk, per = s // bq, s // bk, bk // bq
    _T = (((0,), (0,)), ((), ()))
    n = len(exchange_parts)

    def body(q_ref, k_ref, v_ref, ck_ref, st_ref, do_ref, *rest):
        comm_in, (dq_ref, dk_ref, dv_ref, dck_ref, dcq_ref) = rest[:n], rest[n:n + 5]
        comm_out, sems, (dk_acc, dv_acc, dck_acc) = rest[n + 5:2 * n + 5], rest[2 * n + 5:-3], rest[-3:]
        j = pl.program_id(1)
        if n:
            @pl.when((pl.program_id(0) == 0) & (j == 0))
            def _():
                for cp in _comm_copies(comm_in, comm_out, sems, True):
                    cp.start()

        @pl.when(j == 0)
        def _():
            dq_ref[...] = jnp.zeros_like(dq_ref)
            dcq_ref[...] = jnp.zeros_like(dcq_ref)

        dk_acc[...] = jnp.zeros_like(dk_acc)
        dv_acc[...] = jnp.zeros_like(dv_acc)
        dck_acc[...] = jnp.zeros_like(dck_acc)

        k_t = [k_ref[:, _head_cols(a)].T for a in range(HP)]

        def step(i, off=None):
            rows = pl.ds(pl.multiple_of(i * bq, bq), bq)
            for a in range(HP):
                cs = _head_cols(a)
                q = (q_ref[rows, cs].astype(F32) * ATT_SCALE).astype(BF16)
                k = k_ref[:, cs]
                do_a = do_ref[rows, cs]
                sc = lax.dot_general(q, k, _QK, preferred_element_type=F32) - ck_ref[0, a, pl.ds(j, 1), :]
                if off is not None:
                    sc = jnp.where(_causal(sc.shape, off), sc, NEG)
                p = jnp.exp(sc - st_ref[rows, a * ATT_HEAD_DIM:a * ATT_HEAD_DIM + 1])
                dp = lax.dot_general(do_a, v_ref[:, cs], _QK, preferred_element_type=F32)
                ds = p * (dp - st_ref[rows, a * ATT_HEAD_DIM + _HALF:a * ATT_HEAD_DIM + _HALF + 1])
                ds_b = ds.astype(BF16)
                dv_acc[a] += jnp.dot(do_a.T, p.astype(BF16), preferred_element_type=F32)
                dk_acc[a] += jnp.dot(q.T, ds_b, preferred_element_type=F32)
                dq_ref[cs, rows] += lax.dot_general(k_t[a], ds_b, _QK, preferred_element_type=F32) * ATT_SCALE
                dck_acc[a] -= jnp.sum(ds, axis=0, keepdims=True)
                dcq_ref[rows, cs] += jnp.broadcast_to(jnp.sum(ds, axis=1, keepdims=True), (bq, ATT_HEAD_DIM))

        for t in range(per):
            step(j * per + t, off=t * bq)

        def full(i, c):
            step(i)
            return c

        lax.fori_loop((j + 1) * per, nq, full, 0)
        for a in range(HP):
            dk_ref[:, _head_cols(a)] = dk_acc[a].T.astype(dk_ref.dtype)
            dv_ref[:, _head_cols(a)] = dv_acc[a].T.astype(dv_ref.dtype)
            dck_ref[0, a, pl.ds(j, 1), :] = dck_acc[a]
        if n:
            @pl.when((pl.program_id(0) == N_HP - 1) & (j == nk - 1))
            def _():
                for cp in _comm_copies(comm_in, comm_out, sems, True):
                    cp.wait()

    res = pl.BlockSpec((s, LANES), lambda h, j: (0, h))
    ck_spec = pl.BlockSpec((1, HP, nk, bk), lambda h, j: (h, 0, 0, 0))
    kout = pl.BlockSpec((bk, LANES), lambda h, j: (j, h))
    anyspec = pl.BlockSpec(memory_space=pl.ANY)
    outs = pl.pallas_call(
        body, name="att_bwd", grid=(N_HP, nk),
        in_specs=[res, pl.BlockSpec((bk, LANES), lambda h, j: (j, N_HP + h)),
                  pl.BlockSpec((bk, LANES), lambda h, j: (j, 2 * N_HP + h)), ck_spec, res, res] + [anyspec] * n,
        out_specs=[pl.BlockSpec((LANES, s), lambda h, j: (h, 0)), kout, kout, ck_spec, res] + [anyspec] * n,
        out_shape=[jax.ShapeDtypeStruct((D_MODEL, s), F32), jax.ShapeDtypeStruct((s, D_MODEL), BF16),
                   jax.ShapeDtypeStruct((s, D_MODEL), BF16), jax.ShapeDtypeStruct((N_HP, HP, nk, bk), F32),
                   jax.ShapeDtypeStruct((s, D_MODEL), F32)] + _comm_out_shapes(exchange_parts, True),
        scratch_shapes=(_comm_sems(n) if n else [])
        + [pltpu.VMEM((HP, ATT_HEAD_DIM, bk), F32), pltpu.VMEM((HP, ATT_HEAD_DIM, bk), F32),
           pltpu.VMEM((HP, 1, bk), F32)],
        compiler_params=_cp("arbitrary", "arbitrary"),
    )(qkv, qkv, qkv, ck4, st, do_b, *exchange_parts)
    return outs[:5], list(outs[5:])


def _silu_and_grad(x):
    sg = _sigmoid(x)
    return x * sg, sg * (1.0 + x * (1.0 - sg))


def _conv_pre(cur, halo, w_ref, b_ref, first):
    halo = jnp.where(first, 0.0, halo)
    row = lax.broadcasted_iota(jnp.int32, cur.shape, 0)
    shifted = []
    for k in range(SSM_CONV):
        sh = SSM_CONV - 1 - k
        if sh == 0:
            shifted.append(cur)
            continue
        r = pltpu.roll(cur, sh, 0)
        hr = pltpu.roll(halo, sh, 0)
        top = jnp.where(row[0:8] < sh, hr, r[0:8])
        shifted.append(jnp.concatenate([top, r[8:]], axis=0))
    pre = b_ref[...] + sum(w_ref[k:k + 1, :] * shifted[k] for k in range(SSM_CONV))
    return pre, shifted


def _conv_specs(s, bs, bc):
    cur = pl.BlockSpec((bs, bc), lambda j, i: (i, j))
    halo = pl.BlockSpec((8, bc), lambda j, i: (jnp.maximum(i * (bs // 8) - 1, 0), j))
    w = pl.BlockSpec((SSM_CONV, bc), lambda j, i: (0, j))
    b = pl.BlockSpec((1, bc), lambda j, i: (0, j))
    return cur, halo, w, b


def _conv_fwd(xbc, w, b):
    s, c = xbc.shape
    bs, bc = _tile(s, 512), 1024

    def body(x_ref, h_ref, w_ref, b_ref, o_ref):
        pre, _ = _conv_pre(x_ref[...], h_ref[...], w_ref, b_ref, pl.program_id(1) == 0)
        o_ref[...] = pre * _sigmoid(pre)

    cur, halo, ws, bsp = _conv_specs(s, bs, bc)
    return pl.pallas_call(body, name="conv_fwd", grid=(c // bc, s // bs), in_specs=[cur, halo, ws, bsp],
                          out_specs=cur, out_shape=jax.ShapeDtypeStruct((s, c), F32),
                          compiler_params=_cp("parallel", "parallel"))(xbc, xbc, w, b)


def _conv_bwd_pre(xbc, w, b, dact):
    s, c = xbc.shape
    bs, bc = _tile(s, 512), 1024

    def body(x_ref, h_ref, w_ref, b_ref, g_ref, dp_ref, dw_ref, db_ref):
        @pl.when(pl.program_id(1) == 0)
        def _():
            dw_ref[...] = jnp.zeros_like(dw_ref)
            db_ref[...] = jnp.zeros_like(db_ref)

        pre, shifted = _conv_pre(x_ref[...], h_ref[...], w_ref, b_ref, pl.program_id(1) == 0)
        dpre = g_ref[...] * _silu_and_grad(pre)[1]
        dp_ref[...] = dpre
        db_ref[...] += jnp.sum(dpre, axis=0, keepdims=True)
        for k in range(SSM_CONV):
            dw_ref[k:k + 1, :] += jnp.sum(dpre * shifted[k], axis=0, keepdims=True)

    cur, halo, ws, bsp = _conv_specs(s, bs, bc)
    return pl.pallas_call(
        body, name="conv_bwd_pre", grid=(c // bc, s // bs), in_specs=[cur, halo, ws, bsp, cur],
        out_specs=[cur, ws, bsp],
        out_shape=[jax.ShapeDtypeStruct((s, c), F32), jax.ShapeDtypeStruct((SSM_CONV, c), F32),
                   jax.ShapeDtypeStruct((1, c), F32)],
        compiler_params=_cp("parallel", "arbitrary"))(xbc, xbc, w, b, dact)


def _conv_bwd_in(dpre, w):
    s, c = dpre.shape
    bs, bc = _tile(s, 512), 1024
    nb = s // bs

    def body(g_ref, n_ref, w_ref, o_ref):
        cur = g_ref[...]
        nxt = jnp.where(pl.program_id(1) == nb - 1, 0.0, n_ref[...])
        row = lax.broadcasted_iota(jnp.int32, cur.shape, 0)
        acc = w_ref[SSM_CONV - 1:SSM_CONV, :] * cur
        for sh in range(1, SSM_CONV):
            r = pltpu.roll(cur, bs - sh, 0)
            nr = pltpu.roll(nxt, 8 - sh, 0)
            bot = jnp.where(row[0:8] >= 8 - sh, nr, r[bs - 8:])
            acc = acc + w_ref[SSM_CONV - 1 - sh:SSM_CONV - sh, :] * jnp.concatenate([r[:bs - 8], bot], axis=0)
        o_ref[...] = acc.astype(o_ref.dtype)

    cur = pl.BlockSpec((bs, bc), lambda j, i: (i, j))
    nxt = pl.BlockSpec((8, bc), lambda j, i: (jnp.minimum((i + 1) * (bs // 8), s // 8 - 1), j))
    ws = pl.BlockSpec((SSM_CONV, bc), lambda j, i: (0, j))
    return pl.pallas_call(body, name="conv_bwd_in", grid=(c // bc, nb), in_specs=[cur, nxt, ws], out_specs=cur,
                          out_shape=jax.ShapeDtypeStruct((s, c), BF16),
                          compiler_params=_cp("parallel", "parallel"))(dpre, dpre, w)


def _dotT(a, b):
    return lax.dot_general(a.astype(BF16), b.astype(BF16), (((1,), (1,)), ((), ())), preferred_element_type=F32)


def _Tdot(a, b):
    return lax.dot_general(a.astype(BF16), b.astype(BF16), (((0,), (0,)), ((), ())), preferred_element_type=F32)


def _dot(a, b):
    return jnp.dot(a.astype(BF16), b.astype(BF16), preferred_element_type=F32)


def _ssd_head(xbc_ref, dt_ref, ac_ref, acr_ref, h):
    L = SSM_CHUNK
    xs = xbc_ref[:, h * SSM_HEAD_DIM:(h + 1) * SSM_HEAD_DIM]
    dt_col = dt_ref[:, h:h + 1]
    a_col = ac_ref[:, h:h + 1]
    a_row = acr_ref[h:h + 1, :]
    li = lax.broadcasted_iota(jnp.int32, (L, L), 0)
    si = lax.broadcasted_iota(jnp.int32, (L, L), 1)
    decay = jnp.exp(jnp.where(li >= si, a_col - a_row, NEG))
    a_last = ac_ref[L - 1:L, h:h + 1]
    return xs, dt_col, a_col, a_last, decay


def _ssd_fwd(xbc_act, dt_c, ac_c, ac_r, d_skip):
    s = xbc_act.shape[0]
    L, P, N, G, R = SSM_CHUNK, SSM_HEAD_DIM, SSM_STATE, SSM_GROUPS, SSM_HEADS_PER_GROUP
    nc = s // L

    def body(dsk_ref, xbc_ref, dt_ref, ac_ref, acr_ref, y_ref, hp_ref, st_ref):
        @pl.when(pl.program_id(0) == 0)
        def _():
            st_ref[...] = jnp.zeros_like(st_ref)

        for g in range(G):
            b_g = xbc_ref[:, SSM_INNER + g * N:SSM_INNER + (g + 1) * N]
            c_g = xbc_ref[:, SSM_INNER + G * N + g * N:SSM_INNER + G * N + (g + 1) * N]
            cb = _dotT(c_g, b_g)
            for r in range(R):
                h = g * R + r
                xs, dt_col, a_col, a_last, decay = _ssd_head(xbc_ref, dt_ref, ac_ref, acr_ref, h)
                xdt = xs * dt_col
                hprev = st_ref[h]
                y = _dot(cb * decay, xdt) + jnp.exp(a_col) * _dotT(c_g, hprev) + dsk_ref[h] * xs
                y_ref[:, h * P:(h + 1) * P] = y
                hp_ref[0, h] = hprev
                st_ref[h] = hprev * jnp.exp(a_last) + _Tdot(xdt * jnp.exp(a_last - a_col), b_g)

    col = pl.BlockSpec((L, LANES), lambda c: (c, 0))
    return pl.pallas_call(
        body, name="ssd_fwd", grid=(nc,),
        in_specs=[pl.BlockSpec(memory_space=pltpu.SMEM), pl.BlockSpec((L, SSM_CONV_DIM), lambda c: (c, 0)), col, col,
                  pl.BlockSpec((LANES, L), lambda c: (0, c))],
        out_specs=[pl.BlockSpec((L, SSM_INNER), lambda c: (c, 0)),
                   pl.BlockSpec((1, SSM_HEADS, P, N), lambda c: (c, 0, 0, 0))],
        out_shape=[jax.ShapeDtypeStruct((s, SSM_INNER), F32), jax.ShapeDtypeStruct((nc, SSM_HEADS, P, N), F32)],
        scratch_shapes=[pltpu.VMEM((SSM_HEADS, P, N), F32)],
        compiler_params=_cp("arbitrary"),
    )(d_skip, xbc_act, dt_c, ac_c, ac_r)


def _ssd_bwd(xbc_act, dt_c, ac_c, ac_r, hprev_all, dy, d_skip, a_row):
    s = xbc_act.shape[0]
    L, P, N, G, R = SSM_CHUNK, SSM_HEAD_DIM, SSM_STATE, SSM_GROUPS, SSM_HEADS_PER_GROUP
    nc = s // L

    def body(dsk_ref, xbc_ref, dt_ref, ac_ref, acr_ref, hp_ref, dy_ref, arow_ref,
             dx_ref, ddt_ref, da_ref, dds_ref, dh_ref):
        @pl.when(pl.program_id(0) == 0)
        def _():
            dh_ref[...] = jnp.zeros_like(dh_ref)
            da_ref[...] = jnp.zeros_like(da_ref)
            dds_ref[...] = jnp.zeros_like(dds_ref)

        lane = lax.broadcasted_iota(jnp.int32, (L, LANES), 1)
        sub = lax.broadcasted_iota(jnp.int32, (LANES, L), 0)
        rowi = lax.broadcasted_iota(jnp.int32, (L, 1), 0)
        lane1 = lax.broadcasted_iota(jnp.int32, (1, LANES), 1)
        da_c = jnp.zeros((L, LANES), F32)
        da_r = jnp.zeros((LANES, L), F32)
        ddt1 = jnp.zeros((L, LANES), F32)
        dds = jnp.zeros((1, LANES), F32)
        for g in range(G):
            b_g = xbc_ref[:, SSM_INNER + g * N:SSM_INNER + (g + 1) * N]
            c_g = xbc_ref[:, SSM_INNER + G * N + g * N:SSM_INNER + G * N + (g + 1) * N]
            cb = _dotT(c_g, b_g)
            dcb = jnp.zeros((L, L), F32)
            db_g = jnp.zeros((L, N), F32)
            dc_g = jnp.zeros((L, N), F32)
            for r in range(R):
                h = g * R + r
                xs, dt_col, a_col, a_last, decay = _ssd_head(xbc_ref, dt_ref, ac_ref, acr_ref, h)
                gy = dy_ref[:, h * P:(h + 1) * P]
                xdt = xs * dt_col
                hprev = hp_ref[0, h]
                dhn = dh_ref[h]
                e_a = jnp.exp(a_col)
                e_last = jnp.exp(a_last)
                e_col = jnp.exp(a_last - a_col)
                m = cb * decay
                yoff = e_a * _dotT(c_g, hprev)
                da_col = jnp.sum(gy * yoff, axis=1, keepdims=True)
                dc_g = dc_g + e_a * _dot(gy, hprev)
                dhp = _Tdot(gy * e_a, c_g) + dhn * e_last
                da_last = jnp.sum(jnp.sum(dhn * hprev, axis=1, keepdims=True), axis=0, keepdims=True) * e_last
                xds = _dot(xdt, dhn)
                db_g = db_g + e_col * xds
                de_e = jnp.sum(xds * b_g, axis=1, keepdims=True) * e_col
                da_col = da_col - de_e
                da_last = da_last + jnp.sum(de_e, axis=0, keepdims=True)
                dxdt = e_col * _dotT(b_g, dhn)
                dm = _dotT(gy, xdt)
                dxdt = dxdt + _Tdot(m, gy)
                dcb = dcb + dm * decay
                w = dm * m
                da_col = da_col + jnp.sum(w, axis=1, keepdims=True) + jnp.where(rowi == L - 1, da_last, 0.0)
                da_c = jnp.where(lane == h, da_col, da_c)
                da_r = jnp.where(sub == h, jnp.sum(w, axis=0, keepdims=True), da_r)
                ddt1 = jnp.where(lane == h, jnp.sum(dxdt * xs, axis=1, keepdims=True), ddt1)
                dds = jnp.where(lane1 == h, jnp.sum(jnp.sum(gy * xs, axis=1, keepdims=True), axis=0, keepdims=True),
                                dds)
                dx_ref[:, h * P:(h + 1) * P] = dxdt * dt_col + dsk_ref[h] * gy
                dh_ref[h] = dhp
            dx_ref[:, SSM_INNER + g * N:SSM_INNER + (g + 1) * N] = db_g + _Tdot(dcb, c_g)
            dx_ref[:, SSM_INNER + G * N + g * N:SSM_INNER + G * N + (g + 1) * N] = dc_g + _dot(dcb, b_g)
        dda = jnp.dot(_tri(L, lower=False), da_c - da_r.T, precision=HI, preferred_element_type=F32)
        ddt_ref[...] = dda * arow_ref[...] + ddt1
        da_ref[...] += jnp.sum(dda * dt_ref[...], axis=0, keepdims=True)
        dds_ref[...] += dds

    col = pl.BlockSpec((L, LANES), lambda c: (nc - 1 - c, 0))
    vec = pl.BlockSpec((1, LANES), lambda c: (0, 0))
    return pl.pallas_call(
        body, name="ssd_bwd", grid=(nc,),
        in_specs=[pl.BlockSpec(memory_space=pltpu.SMEM), pl.BlockSpec((L, SSM_CONV_DIM), lambda c: (nc - 1 - c, 0)),
                  col, col, pl.BlockSpec((LANES, L), lambda c: (0, nc - 1 - c)),
                  pl.BlockSpec((1, SSM_HEADS, P, N), lambda c: (nc - 1 - c, 0, 0, 0)),
                  pl.BlockSpec((L, SSM_INNER), lambda c: (nc - 1 - c, 0)), vec],
        out_specs=[pl.BlockSpec((L, SSM_CONV_DIM), lambda c: (nc - 1 - c, 0)), col, vec, vec],
        out_shape=[jax.ShapeDtypeStruct((s, SSM_CONV_DIM), F32), jax.ShapeDtypeStruct((s, LANES), F32),
                   jax.ShapeDtypeStruct((1, LANES), F32), jax.ShapeDtypeStruct((1, LANES), F32)],
        scratch_shapes=[pltpu.VMEM((SSM_HEADS, P, N), F32)],
        compiler_params=_cp("arbitrary"),
    )(d_skip, xbc_act, dt_c, ac_c, ac_r, hprev_all, dy, a_row)


N_PAIR = SSM_HEADS // HP
PAIRS_PER_GROUP = SSM_HEADS_PER_GROUP // HP


def _pair_consts():
    L = SSM_CHUNK
    lane = lax.broadcasted_iota(jnp.int32, (L, LANES), 1)
    lane1 = lax.broadcasted_iota(jnp.int32, (1, LANES), 1)
    li = lax.broadcasted_iota(jnp.int32, (L, L), 0)
    si = lax.broadcasted_iota(jnp.int32, (L, L), 1)
    return lane >= ATT_HEAD_DIM, lane1 >= ATT_HEAD_DIM, li, si


def _ssd_pair_fwd(xbc_act, ac_c, dt_r, ac_r, dsk_pair):
    s = xbc_act.shape[0]
    L, N, G = SSM_CHUNK, SSM_STATE, SSM_GROUPS
    nc = s // L

    def body(xbc_ref, ac_ref, dtr_ref, acr_ref, dsk_ref, y_ref, hp_ref, st_ref):
        @pl.when(pl.program_id(0) == 0)
        def _():
            st_ref[...] = jnp.zeros_like(st_ref)

        upper, up1, li, si = _pair_consts()
        for g in range(G):
            b_g = xbc_ref[:, SSM_INNER + g * N:SSM_INNER + (g + 1) * N]
            c_g = xbc_ref[:, SSM_INNER + G * N + g * N:SSM_INNER + G * N + (g + 1) * N]
            cb = _dotT(c_g, b_g)
            b_t = b_g.T
            for q in range(PAIRS_PER_GROUP):
                pp = g * PAIRS_PER_GROUP + q
                cols = slice(pp * LANES, (pp + 1) * LANES)
                xs = xbc_ref[:, cols]
                ht = st_ref[pp]
                hp_ref[0, pp] = ht
                y = dsk_ref[pp:pp + 1, :] * xs
                s_new = jnp.zeros((N, LANES), F32)
                ea, el = [], []
                for a in range(HP):
                    h = HP * pp + a
                    acol = jnp.broadcast_to(ac_ref[:, h:h + 1], (L, LANES))
                    arow, dtrow = acr_ref[h:h + 1, :], dtr_ref[h:h + 1, :]
                    alast = ac_ref[L - 1:L, h:h + 1]
                    decay = jnp.exp(jnp.where(li >= si, acol - arow, NEG))
                    xs_a = jnp.where(upper == (a == 1), xs, 0.0)
                    y = y + _dot(cb * decay * dtrow, xs_a)
                    s_new = s_new + _dot(b_t * (dtrow * jnp.exp(alast - arow)), xs_a)
                    ea.append(jnp.exp(acol))
                    el.append(jnp.exp(alast))
                y_ref[:, cols] = y + jnp.where(upper, ea[1], ea[0]) * _dot(c_g, ht)
                st_ref[pp] = ht * jnp.where(up1, el[1], el[0]) + s_new

    col = pl.BlockSpec((L, LANES), lambda c: (c, 0))
    row = pl.BlockSpec((LANES, L), lambda c: (0, c))
    return pl.pallas_call(
        body, name="ssd_fwd", grid=(nc,),
        in_specs=[pl.BlockSpec((L, SSM_CONV_DIM), lambda c: (c, 0)), col, row, row,
                  pl.BlockSpec((N_PAIR, LANES), lambda c: (0, 0))],
        out_specs=[pl.BlockSpec((L, SSM_INNER), lambda c: (c, 0)),
                   pl.BlockSpec((1, N_PAIR, N, LANES), lambda c: (c, 0, 0, 0))],
        out_shape=[jax.ShapeDtypeStruct((s, SSM_INNER), F32), jax.ShapeDtypeStruct((nc, N_PAIR, N, LANES), F32)],
        scratch_shapes=[pltpu.VMEM((N_PAIR, N, LANES), F32)],
        compiler_params=_cp("arbitrary"),
    )(xbc_act, ac_c, dt_r, ac_r, dsk_pair)


def _ssd_pair_bwd(xbc_act, dt_c, ac_c, dt_r, ac_r, hprev_all, dy, dsk_pair, a_row):
    s = xbc_act.shape[0]
    L, N, G = SSM_CHUNK, SSM_STATE, SSM_GROUPS
    nc = s // L
    rev = lambda c: nc - 1 - c

    def body(xbc_ref, dt_ref, ac_ref, dtr_ref, acr_ref, hp_ref, dy_ref, dsk_ref, arow_ref,
             dx_ref, ddt_ref, da_ref, dds_ref, dh_ref):
        @pl.when(pl.program_id(0) == 0)
        def _():
            dh_ref[...] = jnp.zeros_like(dh_ref)
            da_ref[...] = jnp.zeros_like(da_ref)
            dds_ref[...] = jnp.zeros_like(dds_ref)

        upper, up1, li, si = _pair_consts()
        lane = lax.broadcasted_iota(jnp.int32, (L, LANES), 1)
        sub = lax.broadcasted_iota(jnp.int32, (LANES, L), 0)
        lastrow = lax.broadcasted_iota(jnp.int32, (L, LANES), 0) == L - 1
        da_c = jnp.zeros((L, LANES), F32)
        da_r = jnp.zeros((LANES, L), F32)
        ddt_r = jnp.zeros((LANES, L), F32)
        for g in range(G):
            b_g = xbc_ref[:, SSM_INNER + g * N:SSM_INNER + (g + 1) * N]
            c_g = xbc_ref[:, SSM_INNER + G * N + g * N:SSM_INNER + G * N + (g + 1) * N]
            cb, cb_t = _dotT(c_g, b_g), _dotT(b_g, c_g)
            b_t, c_t = b_g.T, c_g.T
            dcb = jnp.zeros((L, L), F32)
            db_t = jnp.zeros((N, L), F32)
            dc = jnp.zeros((L, N), F32)
            for q in range(PAIRS_PER_GROUP):
                pp = g * PAIRS_PER_GROUP + q
                cols = slice(pp * LANES, (pp + 1) * LANES)
                xs, gy = xbc_ref[:, cols], dy_ref[:, cols]
                ht, dhn = hp_ref[0, pp], dh_ref[pp]
                acol = [jnp.broadcast_to(ac_ref[:, HP * pp + a:HP * pp + a + 1], (L, LANES)) for a in range(HP)]
                alast = [ac_ref[L - 1:L, HP * pp + a:HP * pp + a + 1] for a in range(HP)]
                ea = jnp.where(upper, jnp.exp(acol[1]), jnp.exp(acol[0]))
                el = jnp.where(up1, jnp.exp(alast[1]), jnp.exp(alast[0]))
                ge = gy * ea
                dc = dc + _dotT(ge, ht)
                dh_ref[pp] = _dot(c_t, ge) + dhn * el
                t_off = (ge * _dot(c_g, ht)).astype(BF16)
                hsum = jnp.sum(dhn * ht, axis=0, keepdims=True)
                dxs = dsk_ref[pp:pp + 1, :] * gy
                dds_ref[pp:pp + 1, :] += jnp.sum(gy * xs, axis=0, keepdims=True)
                for a in range(HP):
                    h = HP * pp + a
                    mine, mine1 = upper == (a == 1), up1 == (a == 1)
                    arow, dtrow = acr_ref[h:h + 1, :], dtr_ref[h:h + 1, :]
                    dtcol = jnp.broadcast_to(dt_ref[:, h:h + 1], (L, LANES))
                    xs_a, gy_a = jnp.where(mine, xs, 0.0), jnp.where(mine, gy, 0.0)
                    dhn_a = jnp.where(mine1, dhn, 0.0)
                    e_row = jnp.exp(alast[a] - arow)
                    w_row = dtrow * e_row
                    xd_t = _dotT(dhn_a, xs_a)
                    db_t = db_t + xd_t * w_row
                    dw = jnp.sum(b_t * xd_t, axis=0, keepdims=True)
                    de_e = dw * w_row
                    dal = (jnp.sum(jnp.where(mine1, hsum, 0.0), axis=1, keepdims=True) * jnp.exp(alast[a])
                           + jnp.sum(de_e, axis=1, keepdims=True))
                    dxs = dxs + _dot(b_g, dhn_a) * (dtcol * jnp.exp(alast[a] - acol[a]))
                    decay = jnp.exp(jnp.where(li >= si, acol[a] - arow, NEG))
                    decay_t = jnp.exp(jnp.where(si >= li, arow - acol[a], NEG))
                    m = cb * decay
                    dmdt = _dotT(gy_a, xs_a)
                    dxs = dxs + _dot(cb_t * decay_t * dtcol, gy_a)
                    dm = dmdt * dtrow
                    dcb = dcb + dm * decay
                    wb = (dm * m).astype(BF16)
                    onehot = jnp.where(lane == h, 1.0, 0.0).astype(BF16)
                    da_c = (da_c + jnp.dot(wb, onehot, preferred_element_type=F32)
                            + jnp.dot(jnp.where(mine, t_off, 0.0).astype(BF16), onehot, preferred_element_type=F32)
                            + jnp.where(lastrow & (lane == h), dal, 0.0))
                    da_r = jnp.where(sub == h, -(jnp.sum(wb.astype(F32), axis=0, keepdims=True) + de_e), da_r)
                    ddt_r = jnp.where(sub == h, dw * e_row + jnp.sum(dmdt * m, axis=0, keepdims=True), ddt_r)
                dx_ref[:, cols] = dxs
            dx_ref[:, SSM_INNER + g * N:SSM_INNER + (g + 1) * N] = (db_t + _dot(c_t, dcb)).T
            dx_ref[:, SSM_INNER + G * N + g * N:SSM_INNER + G * N + (g + 1) * N] = dc + _dot(dcb, b_g)
        dda = jnp.dot(_tri(L, lower=False), da_c + da_r.T, precision=HI, preferred_element_type=F32)
        ddt_ref[...] = dda * arow_ref[...] + ddt_r.T
        da_ref[...] += jnp.sum(dda * dt_ref[...], axis=0, keepdims=True)

    col = pl.BlockSpec((L, LANES), lambda c: (rev(c), 0))
    row = pl.BlockSpec((LANES, L), lambda c: (0, rev(c)))
    vec = pl.BlockSpec((1, LANES), lambda c: (0, 0))
    pairs = pl.BlockSpec((N_PAIR, LANES), lambda c: (0, 0))
    return pl.pallas_call(
        body, name="ssd_bwd", grid=(nc,),
        in_specs=[pl.BlockSpec((L, SSM_CONV_DIM), lambda c: (rev(c), 0)), col, col, row, row,
                  pl.BlockSpec((1, N_PAIR, N, LANES), lambda c: (rev(c), 0, 0, 0)),
                  pl.BlockSpec((L, SSM_INNER), lambda c: (rev(c), 0)), pairs, vec],
        out_specs=[pl.BlockSpec((L, SSM_CONV_DIM), lambda c: (rev(c), 0)), col, vec, pairs],
        out_shape=[jax.ShapeDtypeStruct((s, SSM_CONV_DIM), F32), jax.ShapeDtypeStruct((s, LANES), F32),
                   jax.ShapeDtypeStruct((1, LANES), F32), jax.ShapeDtypeStruct((N_PAIR, LANES), F32)],
        scratch_shapes=[pltpu.VMEM((N_PAIR, N, LANES), F32)],
        compiler_params=_cp("arbitrary"),
    )(xbc_act, dt_c, ac_c, dt_r, ac_r, hprev_all, dy, dsk_pair, a_row)


ROWS = 256
GW = SSM_INNER // SSM_GROUPS


def _rows(width, dtype=F32):
    return pl.BlockSpec((ROWS, width), lambda i: (i, 0))


def _vec(width):
    return pl.BlockSpec((1, width), lambda i: (0, 0))


def _gnorm_fwd(y, z, w):
    s = y.shape[0]

    def body(y_ref, z_ref, w_ref, o_ref):
        for g in range(SSM_GROUPS):
            cs = slice(g * GW, (g + 1) * GW)
            zz = z_ref[:, cs]
            u = y_ref[:, cs] * (zz * _sigmoid(zz))
            r = lax.rsqrt(jnp.mean(u * u, axis=1, keepdims=True) + RMS_EPS)
            o_ref[:, cs] = (u * r * w_ref[:, cs]).astype(o_ref.dtype)

    return pl.pallas_call(body, name="gnorm_fwd", grid=(s // ROWS,),
                          in_specs=[_rows(SSM_INNER), _rows(SSM_INNER), _vec(SSM_INNER)], out_specs=_rows(SSM_INNER),
                          out_shape=jax.ShapeDtypeStruct((s, SSM_INNER), BF16), compiler_params=_cp("parallel"))(y, z, w)


def _gnorm_bwd(y, z, w, do):
    s = y.shape[0]

    def body(y_ref, z_ref, w_ref, do_ref, dy_ref, dz_ref, dw_ref):
        @pl.when(pl.program_id(0) == 0)
        def _():
            dw_ref[...] = jnp.zeros_like(dw_ref)

        for g in range(SSM_GROUPS):
            cs = slice(g * GW, (g + 1) * GW)
            zz, yy, dd = z_ref[:, cs], y_ref[:, cs], do_ref[:, cs]
            sz, dsz = _silu_and_grad(zz)
            u = yy * sz
            r = lax.rsqrt(jnp.mean(u * u, axis=1, keepdims=True) + RMS_EPS)
            n = u * r
            dn = dd * w_ref[:, cs]
            dw_ref[:, cs] += jnp.sum(dd * n, axis=0, keepdims=True)
            du = r * (dn - n * jnp.mean(dn * n, axis=1, keepdims=True))
            dy_ref[:, cs] = du * sz
            dz_ref[:, cs] = (du * yy * dsz).astype(dz_ref.dtype)

    return pl.pallas_call(
        body, name="gnorm_bwd", grid=(s // ROWS,),
        in_specs=[_rows(SSM_INNER), _rows(SSM_INNER), _vec(SSM_INNER), _rows(SSM_INNER)],
        out_specs=[_rows(SSM_INNER), _rows(SSM_INNER), _vec(SSM_INNER)],
        out_shape=[jax.ShapeDtypeStruct((s, SSM_INNER), F32), jax.ShapeDtypeStruct((s, SSM_INNER), BF16),
                   jax.ShapeDtypeStruct((1, SSM_INNER), F32)],
        compiler_params=_cp("arbitrary"))(y, z, w, do)


def _mix_fwd(gl, bg, attn_d, ssm_d):
    s = gl.shape[0]
    d = D_MODEL

    def body(gl_ref, bg_ref, a_ref, m_ref, o_ref):
        g0 = _sigmoid(gl_ref[:, :d] + bg_ref[:, :d])
        g1 = _sigmoid(gl_ref[:, d:] + bg_ref[:, d:])
        o_ref[...] = (g0 * a_ref[...] + g1 * m_ref[...]).astype(o_ref.dtype)

    return pl.pallas_call(body, name="mix_fwd", grid=(s // ROWS,),
                          in_specs=[_rows(2 * d), _vec(2 * d), _rows(d), _rows(d)], out_specs=_rows(d),
                          out_shape=jax.ShapeDtypeStruct((s, d), BF16), compiler_params=_cp("parallel"))(
        gl, bg, attn_d, ssm_d)


def _mix_bwd(gl, bg, attn_d, ssm_d, dmix):
    s = gl.shape[0]
    d = D_MODEL

    def body(gl_ref, bg_ref, a_ref, m_ref, dm_ref, da_ref, ds_ref, dg_ref, db_ref):
        @pl.when(pl.program_id(0) == 0)
        def _():
            db_ref[...] = jnp.zeros_like(db_ref)

        g0 = _sigmoid(gl_ref[:, :d] + bg_ref[:, :d])
        g1 = _sigmoid(gl_ref[:, d:] + bg_ref[:, d:])
        dm = dm_ref[...]
        da_ref[...] = (dm * g0).astype(da_ref.dtype)
        ds_ref[...] = (dm * g1).astype(ds_ref.dtype)
        dl0 = dm * a_ref[...] * g0 * (1.0 - g0)
        dl1 = dm * m_ref[...] * g1 * (1.0 - g1)
        dg_ref[:, :d] = dl0.astype(dg_ref.dtype)
        dg_ref[:, d:] = dl1.astype(dg_ref.dtype)
        db_ref[:, :d] += jnp.sum(dl0, axis=0, keepdims=True)
        db_ref[:, d:] += jnp.sum(dl1, axis=0, keepdims=True)

    return pl.pallas_call(
        body, name="mix_bwd", grid=(s // ROWS,),
        in_specs=[_rows(2 * d), _vec(2 * d), _rows(d), _rows(d), _rows(d)],
        out_specs=[_rows(d), _rows(d), _rows(2 * d), _vec(2 * d)],
        out_shape=[jax.ShapeDtypeStruct((s, d), BF16), jax.ShapeDtypeStruct((s, d), BF16),
                   jax.ShapeDtypeStruct((s, 2 * d), BF16), jax.ShapeDtypeStruct((1, 2 * d), F32)],
        compiler_params=_cp("arbitrary"))(gl, bg, attn_d, ssm_d, dmix)


def _ln_stats(p):
    mu = jnp.mean(p, axis=1, keepdims=True)
    c = p - mu
    rstd = lax.rsqrt(jnp.mean(c * c, axis=1, keepdims=True) + LN_EPS)
    return c * rstd, rstd


def _ln_bwd(dy, xhat, rstd, g):
    dxh = dy * g
    return rstd * (dxh - jnp.mean(dxh, axis=1, keepdims=True) - xhat * jnp.mean(dxh * xhat, axis=1, keepdims=True))


def _ln1_fwd(x, mixed, g, b):
    s, d = x.shape

    def body(x_ref, m_ref, g_ref, b_ref, o_ref):
        xhat, _ = _ln_stats(DEEPNORM_ALPHA * x_ref[...] + m_ref[...])
        o_ref[...] = xhat * g_ref[...] + b_ref[...]

    return pl.pallas_call(body, name="ln1_fwd", grid=(s // ROWS,), in_specs=[_rows(d), _rows(d), _vec(d), _vec(d)],
                          out_specs=_rows(d), out_shape=jax.ShapeDtypeStruct((s, d), F32),
                          compiler_params=_cp("parallel"))(x, mixed, g, b)


def _ln2_loss(x1, h, target, g, b):
    s, d = x1.shape

    def body(x_ref, h_ref, t_ref, g_ref, b_ref, dp_ref, loss_ref, dg_ref, db_ref):
        @pl.when(pl.program_id(0) == 0)
        def _():
            loss_ref[...] = jnp.zeros_like(loss_ref)
            dg_ref[...] = jnp.zeros_like(dg_ref)
            db_ref[...] = jnp.zeros_like(db_ref)

        xhat, rstd = _ln_stats(DEEPNORM_ALPHA * x_ref[...] + h_ref[...])
        err = xhat * g_ref[...] + b_ref[...] - t_ref[...]
        part = 0.5 * jnp.sum(jnp.mean(err * err, axis=1, keepdims=True), axis=0, keepdims=True)
        loss_ref[...] += jnp.broadcast_to(part, loss_ref.shape)
        dy = err * (1.0 / d)
        dg_ref[...] += jnp.sum(dy * xhat, axis=0, keepdims=True)
        db_ref[...] += jnp.sum(dy, axis=0, keepdims=True)
        dp_ref[...] = _ln_bwd(dy, xhat, rstd, g_ref[...])

    return pl.pallas_call(
        body, name="ln2_loss", grid=(s // ROWS,), in_specs=[_rows(d), _rows(d), _rows(d), _vec(d), _vec(d)],
        out_specs=[_rows(d), _vec(LANES), _vec(d), _vec(d)],
        out_shape=[jax.ShapeDtypeStruct((s, d), F32), jax.ShapeDtypeStruct((1, LANES), F32),
                   jax.ShapeDtypeStruct((1, d), F32), jax.ShapeDtypeStruct((1, d), F32)],
        compiler_params=_cp("arbitrary"))(x1, h, target, g, b)


def _ln1_bwd(x, mixed, g, dpre2, dffn):
    s, d = x.shape

    def body(x_ref, m_ref, g_ref, d2_ref, df_ref, dp_ref, dr_ref, dg_ref, db_ref):
        @pl.when(pl.program_id(0) == 0)
        def _():
            dg_ref[...] = jnp.zeros_like(dg_ref)
            db_ref[...] = jnp.zeros_like(db_ref)

        xhat, rstd = _ln_stats(DEEPNORM_ALPHA * x_ref[...] + m_ref[...])
        dy = DEEPNORM_ALPHA * d2_ref[...] + df_ref[...]
        dg_ref[...] += jnp.sum(dy * xhat, axis=0, keepdims=True)
        db_ref[...] += jnp.sum(dy, axis=0, keepdims=True)
        dp = _ln_bwd(dy, xhat, rstd, g_ref[...])
        dp_ref[...] = dp
        dr_ref[...] = DEEPNORM_ALPHA * dp

    return pl.pallas_call(
        body, name="ln1_bwd", grid=(s // ROWS,), in_specs=[_rows(d), _rows(d), _vec(d), _rows(d), _rows(d)],
        out_specs=[_rows(d), _rows(d), _vec(d), _vec(d)],
        out_shape=[jax.ShapeDtypeStruct((s, d), F32), jax.ShapeDtypeStruct((s, d), F32),
                   jax.ShapeDtypeStruct((1, d), F32), jax.ShapeDtypeStruct((1, d), F32)],
        compiler_params=_cp("arbitrary"))(x, mixed, g, dpre2, dffn)


def _swiglu_fwd(gu):
    s = gu.shape[0]
    f = FFN_HIDDEN

    def body(g_ref, u_ref, o_ref):
        gg = g_ref[...]
        o_ref[...] = (gg * _sigmoid(gg) * u_ref[...]).astype(o_ref.dtype)

    return pl.pallas_call(
        body, name="swiglu_fwd", grid=(s // ROWS,),
        in_specs=[pl.BlockSpec((ROWS, f), lambda i: (i, 0)), pl.BlockSpec((ROWS, f), lambda i: (i, 1))],
        out_specs=_rows(f), out_shape=jax.ShapeDtypeStruct((s, f), BF16), compiler_params=_cp("parallel"))(gu, gu)


def _swiglu_bwd(gu, dact):
    s = gu.shape[0]
    f = FFN_HIDDEN

    def body(g_ref, u_ref, d_ref, o_ref):
        sg, dsg = _silu_and_grad(g_ref[...])
        dd = d_ref[...]
        o_ref[:, :f] = (dd * u_ref[...] * dsg).astype(o_ref.dtype)
        o_ref[:, f:] = (dd * sg).astype(o_ref.dtype)

    return pl.pallas_call(
        body, name="swiglu_bwd", grid=(s // ROWS,),
        in_specs=[pl.BlockSpec((ROWS, f), lambda i: (i, 0)), pl.BlockSpec((ROWS, f), lambda i: (i, 1)), _rows(f)],
        out_specs=_rows(2 * f), out_shape=jax.ShapeDtypeStruct((s, 2 * f), BF16),
        compiler_params=_cp("parallel"))(gu, gu, dact)


def _peer(k):
    x, y, c = lax.axis_index("x"), lax.axis_index("y"), lax.axis_index("c")
    kx, ky, kc = (k >> 2) & 1, (k >> 1) & 1, k & 1
    px = (1 - x) if kx else x
    py = (1 - y) if ky else y
    pc = (1 - c) if kc else c
    return (px, py, pc), 4 * px + 2 * py + pc


def _my_index():
    return 4 * lax.axis_index("x") + 2 * lax.axis_index("y") + lax.axis_index("c")


def _comm_copies(ins, outs, sems, scatter):
    send_sems, recv_sems, local_sems = sems
    me = _my_index()
    copies = [pltpu.make_async_copy(ins[t].at[me] if scatter else ins[t], outs[t].at[me], local_sems.at[t])
              for t in range(len(ins))]
    for k in range(1, N_DEV):
        peer, pidx = _peer(k)
        for t in range(len(ins)):
            copies.append(pltpu.make_async_remote_copy(
                src_ref=ins[t].at[pidx] if scatter else ins[t], dst_ref=outs[t].at[me],
                send_sem=send_sems.at[t, k - 1], recv_sem=recv_sems.at[t, k - 1], device_id=peer,
                device_id_type=pl.DeviceIdType.MESH))
    return copies


def _comm_sems(n):
    return [pltpu.SemaphoreType.DMA((n, N_DEV - 1)), pltpu.SemaphoreType.DMA((n, N_DEV - 1)),
            pltpu.SemaphoreType.DMA((n,))]


def _comm_out_shapes(parts, scatter):
    return [jax.ShapeDtypeStruct(p.shape if scatter else (N_DEV,) + p.shape, p.dtype) for p in parts]


def _comm_call(parts, scatter, name):
    n = len(parts)

    def body(*refs):
        copies = _comm_copies(refs[:n], refs[n:2 * n], refs[2 * n:], scatter)
        for cp in copies:
            cp.start()
        for cp in copies:
            cp.wait()

    anyspec = pl.BlockSpec(memory_space=pl.ANY)
    return pl.pallas_call(body, name=name, in_specs=[anyspec] * n, out_specs=[anyspec] * n,
                          out_shape=_comm_out_shapes(parts, scatter), scratch_shapes=_comm_sems(n))(*parts)


def _all_gather(parts):
    return _comm_call(parts, False, "all_gather")


def _exchange(parts):
    return _comm_call(parts, True, "grad_exchange")


def _adamw(recv, w, m, v, name):
    r, c = w.shape
    br = _tile(r, 128)
    c1 = 1.0 / (1.0 - ADAM_B1 ** ADAM_STEP)
    c2 = 1.0 / (1.0 - ADAM_B2 ** ADAM_STEP)

    def body(r_ref, w_ref, m_ref, v_ref, g_ref, d_ref, mo_ref, vo_ref):
        g = r_ref[0].astype(F32)
        for k in range(1, N_DEV):
            g = g + r_ref[k].astype(F32)
        mn = ADAM_B1 * m_ref[...] + (1.0 - ADAM_B1) * g
        vn = ADAM_B2 * v_ref[...] + (1.0 - ADAM_B2) * (g * g)
        g_ref[...] = g
        mo_ref[...] = mn
        vo_ref[...] = vn
        d_ref[...] = -ADAM_LR * ((mn * c1) / (jnp.sqrt(vn * c2) + ADAM_EPS) + ADAM_WD * w_ref[...])

    blk = pl.BlockSpec((br, c), lambda i: (i, 0))
    return pl.pallas_call(
        body, name=name, grid=(r // br,),
        in_specs=[pl.BlockSpec((N_DEV, br, c), lambda i: (0, i, 0)), blk, blk, blk],
        out_specs=[blk] * 4, out_shape=[jax.ShapeDtypeStruct((r, c), F32)] * 4,
        compiler_params=_cp("parallel"))(recv, w, m, v)


def _lane_row(pairs):
    row = jnp.zeros((LANES,), F32)
    for lane0, vec in pairs:
        row = lax.dynamic_update_slice(row, vec.astype(F32), (lane0,))
    return row.reshape(1, LANES)


def _stage_in(x, wts, small):
    s = x.shape[0]
    a = -jnp.exp(small["a_log"])
    bias_row = _lane_row([(DT_LANE0, small["dt_bias"]), (F_LANE0, small["b_forget"])])
    a_row = _lane_row([(DT_LANE0, a)])
    conv_b = small["conv_b"].reshape(1, -1)
    norm_w = small["ssm_norm_w"].reshape(1, -1)
    bg = small["b_gates"].reshape(1, -1)
    g1, b1 = small["ln1_g"].reshape(1, -1), small["ln1_b"].reshape(1, -1)
    g2, b2 = small["ln2_g"].reshape(1, -1), small["ln2_b"].reshape(1, -1)
    d_skip = small["d_skip"]
    xb = x.astype(BF16)

    qkv = _mm(xb, wts["qkv"], out_dtype=BF16, name="f_qkv")
    z = _mm(xb, wts["z"], name="f_z")
    xbc = _mm(xb, wts["xbc"], name="f_xbc")
    gl = _mm(xb, wts["gate"], name="f_gate")
    fd = _mm(xb, wts["fd"], name="f_fd")
    dt_c, ac_c, cf_c, dt_r, ac_r, cf_r = _stats_fwd(fd, bias_row, a_row)
    bk = _att_blocks(s)[1]
    ck4 = cf_r[F_LANE0:F_LANE0 + ATT_HEADS].reshape(N_HP, HP, s // bk, bk)
    return dict(locals())


def _stage_mid(c, attn, lse, wts, target):
    x, xb, qkv, z, xbc, gl, fd, ck4 = (c[k] for k in ("x", "xb", "qkv", "z", "xbc", "gl", "fd", "ck4"))
    dt_c, ac_c, dt_r, ac_r, a_row, bias_row = (c[k] for k in ("dt_c", "ac_c", "dt_r", "ac_r", "a_row", "bias_row"))
    conv_b, norm_w, bg, g1, b1, g2, b2, d_skip = (c[k] for k in ("conv_b", "norm_w", "bg", "g1", "b1", "g2", "b2",
                                                                "d_skip"))
    conv_w = c["wts"]["conv"]
    attn_d = _mm(attn, wts["pa"], name="f_pa")
    xact = _conv_fwd(xbc, conv_w, conv_b)
    dsk_pair = jnp.repeat(d_skip, SSM_HEAD_DIM).reshape(N_PAIR, LANES)
    y, hprev = _ssd_pair_fwd(xact, ac_c, dt_r, ac_r, dsk_pair)
    ssm = _gnorm_fwd(y, z, norm_w)
    ssm_d = _mm(ssm, wts["ps"], name="f_ps")
    mix = _mix_fwd(gl, bg, attn_d, ssm_d)
    mixed = _mm(mix, wts["out"], name="f_out")
    x1 = _ln1_fwd(x, mixed, g1, b1)
    gu = _mm(x1, wts["gu"], name="f_gu")
    act = _swiglu_fwd(gu)
    h = _mm(act, wts["down"], name="f_down")
    dpre2, loss_row, dg2, db2 = _ln2_loss(x1, h, target, g2, b2)

    d_act = _mm(dpre2, wts["down"], tb=True, name="b_down_x")
    dw_down = _mm(act, dpre2, ta=True, name="b_down_w")
    dgu = _swiglu_bwd(gu, d_act)
    dffn = _mm(dgu, wts["gu"], tb=True, name="b_gu_x")
    dw_gu = _mm(x1, dgu, ta=True, name="b_gu_w")
    dpre1, dxr, dg1, db1 = _ln1_bwd(x, mixed, g1, dpre2, dffn)
    dmix = _mm(dpre1, wts["out"], tb=True, name="b_out_x")
    dw_out = _mm(mix, dpre1, ta=True, name="b_out_w")
    dattn_d, dssm_d, dgl, dbg = _mix_bwd(gl, bg, attn_d, ssm_d, dmix)
    dssm = _mm(dssm_d, wts["ps"], tb=True, name="b_ps_x")
    dw_ps = _mm(ssm, dssm_d, ta=True, name="b_ps_w")
    dattn = _mm(dattn_d, wts["pa"], tb=True, name="b_pa_x")
    dw_pa = _mm(attn, dattn_d, ta=True, name="b_pa_w")
    dy, dz, dnw = _gnorm_bwd(y, z, norm_w, dssm)
    dxact, ddt, da_row, dds_pair = _ssd_pair_bwd(xact, dt_c, ac_c, dt_r, ac_r, hprev, dy, dsk_pair, a_row)
    dds = dds_pair.reshape(SSM_HEADS, SSM_HEAD_DIM).sum(axis=1)
    dpre_c, dconv_w, dconv_b = _conv_bwd_pre(xbc, conv_w, conv_b, dxact)
    dxbc = _conv_bwd_in(dpre_c, conv_w)
    st, do_b = _att_prep(dattn, attn, lse)
    late = dict(pa=dw_pa, ps=dw_ps, out=dw_out, gu=dw_gu, down=dw_down)
    keep = ("st", "do_b", "ddt", "dxr", "dz", "dxbc", "dgl", "dconv_w", "dconv_b", "da_row", "dds", "dnw", "dbg",
            "dg1", "db1", "dg2", "db2", "loss_row")
    loc = locals()
    return {**c, **{k: loc[k] for k in keep}}, late


def _stage_out(c, att_grads):
    dq, dk, dv, dck, dcq = att_grads
    wts, xb, fd, bias_row, ddt, dxr, dz, dxbc, dgl = (c[k] for k in ("wts", "xb", "fd", "bias_row", "ddt", "dxr", "dz",
                                                                  "dxbc", "dgl"))
    s, d, a = xb.shape[0], D_MODEL, c["a"]
    dck_rows = jnp.zeros((LANES, s), F32).at[F_LANE0:F_LANE0 + ATT_HEADS].set(dck.reshape(ATT_HEADS, s))
    dcq_cols = jnp.zeros((s, LANES), F32).at[:, F_LANE0:F_LANE0 + ATT_HEADS].set(dcq[:, ::ATT_HEAD_DIM])
    dfd, dbias = _stats_bwd(fd, bias_row, ddt, dck_rows, dcq_cols)

    wq, wk, wv = wts["qkv"][:, :d], wts["qkv"][:, d:2 * d], wts["qkv"][:, 2 * d:]
    dx = dxr
    for i, (g_, w_) in enumerate(((dq, wq), (dk, wk), (dv, wv), (dz, wts["z"]), (dxbc, wts["xbc"]),
                                  (dgl, wts["gate"]), (dfd, wts["fd"]))):
        dx = _mm(g_, w_, ta=(i == 0), tb=True, add=dx, name=f"b_in_x{i}")
    dw_in = [_mm(xb, g_, ta=True, tb=(i == 0), name=f"b_in_w{i}")
             for i, g_ in enumerate((dq, dk, dv, dz, dxbc, dgl, dfd))]

    grads = dict(q=dw_in[0], k=dw_in[1], v=dw_in[2], z=dw_in[3], xbc=dw_in[4], gate=dw_in[5], fd=dw_in[6],
                 conv=c["dconv_w"])
    small_g = dict(
        b_forget=dbias[0, F_LANE0:F_LANE0 + ATT_HEADS], conv_b=c["dconv_b"][0], dt_bias=dbias[0, :SSM_HEADS],
        a_log=c["da_row"][0, :SSM_HEADS] * a, d_skip=c["dds"], ssm_norm_w=c["dnw"][0], b_gates=c["dbg"][0],
        ln1_g=c["dg1"][0], ln1_b=c["db1"][0], ln2_g=c["dg2"][0], ln2_b=c["db2"][0])
    return c["loss_row"][0, 0], dx, grads, small_g


BIG = ("w_in", "w_proj_attn", "w_proj_ssm", "w_out", "w_ffn_gate", "w_ffn_up", "w_ffn_down", "conv_w")
EARLY = ("w_in", "conv_w")
LATE = ("w_proj_attn", "w_proj_ssm", "w_out", "w_ffn_gate", "w_ffn_up", "w_ffn_down")
SMALL = ("b_forget", "conv_b", "dt_bias", "a_log", "d_skip", "ssm_norm_w", "b_gates", "ln1_g", "ln1_b", "ln2_g",
         "ln2_b")
SMALL_ROWS = 96
IN_SHARD = IN_WIDTH // N_DEV
IN_SEGMENTS = (("q", 0, 1024), ("k", 1024, 1024), ("v", 2048, 1024), ("f", 3072, ATT_HEADS), ("z", 3088, SSM_INNER),
               ("xbc", 5136, SSM_CONV_DIM), ("dt", 8208, SSM_HEADS), ("gate", 8240, 2 * D_MODEL))


def _cols_from_shards(shards, lo, hi):
    w = shards[0].shape[1]
    pieces = []
    for j in range(len(shards)):
        a, b = max(lo, j * w), min(hi, (j + 1) * w)
        if a < b:
            pieces.append(shards[j][:, a - j * w:b - j * w])
    return pieces[0] if len(pieces) == 1 else jnp.concatenate(pieces, axis=1)


def _shards_from_parts(parts, width):
    shards = []
    for j in range(N_DEV):
        lo, hi = j * width, (j + 1) * width
        pieces = []
        for mat, c0 in parts:
            a, b = max(lo, c0), min(hi, c0 + mat.shape[1])
            if a < b:
                pieces.append(mat[:, a - c0:b - c0])
        shards.append(pieces[0] if len(pieces) == 1 else jnp.concatenate(pieces, axis=1))
    return shards


def _pack_small(vals):
    flat = jnp.concatenate([vals[n].reshape(-1) for n in SMALL])
    return jnp.pad(flat, (0, SMALL_ROWS * LANES - flat.shape[0])).reshape(SMALL_ROWS, LANES)


def _unpack_small(pack, shapes):
    flat = pack.reshape(-1)
    out, off = {}, 0
    for n in SMALL:
        sz = math.prod(shapes[n])
        out[n] = flat[off:off + sz].reshape(shapes[n])
        off += sz
    return out


def kernel(x, w_in, b_forget, conv_w, conv_b, dt_bias, a_log, d_skip, ssm_norm_w, w_proj_attn, w_proj_ssm, b_gates, w_out, ln1_g, ln1_b, w_ffn_gate, w_ffn_up, w_ffn_down, ln2_g, ln2_b, loss_target, m_w_in, m_b_forget, m_conv_w, m_conv_b, m_dt_bias, m_a_log, m_d_skip, m_ssm_norm_w, m_w_proj_attn, m_w_proj_ssm, m_b_gates, m_w_out, m_ln1_g, m_ln1_b, m_w_ffn_gate, m_w_ffn_up, m_w_ffn_down, m_ln2_g, m_ln2_b, v_w_in, v_b_forget, v_conv_w, v_conv_b, v_dt_bias, v_a_log, v_d_skip, v_ssm_norm_w, v_w_proj_attn, v_w_proj_ssm, v_b_gates, v_w_out, v_ln1_g, v_ln1_b, v_w_ffn_gate, v_w_ffn_up, v_w_ffn_down, v_ln2_g, v_ln2_b):
    args = dict(locals())
    d, f = D_MODEL, FFN_HIDDEN
    big_w = {n: args[n][0] for n in BIG}
    small_w = {n: args[n][0] for n in SMALL}
    big_shapes = {n: args[n].shape for n in BIG}
    small_shapes = {n: args[n].shape for n in SMALL}

    early = dict(zip(EARLY, _all_gather([big_w["w_in"].astype(BF16), big_w["conv_w"]])))
    in_shards = [early["w_in"][j] for j in range(N_DEV)]
    seg = {n: _cols_from_shards(in_shards, c0, c0 + w) for n, c0, w in IN_SEGMENTS}
    wfd = jnp.concatenate([seg["dt"], seg["f"], jnp.zeros((d, LANES - SSM_HEADS - ATT_HEADS), BF16)], axis=1)
    wts = dict(qkv=jnp.concatenate([seg["q"], seg["k"], seg["v"]], axis=1), z=seg["z"], xbc=seg["xbc"],
               gate=seg["gate"], fd=wfd, conv=jnp.concatenate([early["conv_w"][j] for j in range(N_DEV)], axis=1))

    ctx = _stage_in(x[0], wts, small_w)
    attn, lse, gathered = _attention_fwd(ctx["qkv"], ctx["ck4"], [big_w[n].astype(BF16) for n in LATE])
    full = dict(zip(LATE, gathered))
    late_w = dict(
        pa=full["w_proj_attn"].reshape(d, d), ps=full["w_proj_ssm"].reshape(SSM_INNER, d),
        out=full["w_out"].reshape(d, d),
        gu=jnp.concatenate([full["w_ffn_gate"][j] for j in range(N_DEV)]
                           + [full["w_ffn_up"][j] for j in range(N_DEV)], axis=1),
        down=full["w_ffn_down"].reshape(f, d))
    ctx, gl = _stage_mid(ctx, attn, lse, late_w, loss_target[0])
    late_dest = dict(
        w_ffn_gate=jnp.stack([s_.astype(BF16) for s_ in _shards_from_parts([(gl["gu"][:, :f], 0)], f // N_DEV)]),
        w_ffn_up=jnp.stack([s_.astype(BF16) for s_ in _shards_from_parts([(gl["gu"][:, f:], 0)], f // N_DEV)]))
    for n, key in (("w_proj_attn", "pa"), ("w_proj_ssm", "ps"), ("w_out", "out"), ("w_ffn_down", "down")):
        late_dest[n] = gl[key].astype(BF16).reshape((N_DEV,) + big_shapes[n][1:])
    att_grads, late_recv = _attention_bwd(ctx["qkv"], ctx["ck4"], ctx["st"], ctx["do_b"], [late_dest[n] for n in LATE])
    loss_part, grad_x, g, small_g = _stage_out(ctx, att_grads)
    loss = lax.psum(loss_part, ("x", "y", "c"))

    gfd = g["fd"]
    in_parts = dict(q=g["q"], k=g["k"], v=g["v"], f=gfd[:, F_LANE0:F_LANE0 + ATT_HEADS], z=g["z"], xbc=g["xbc"],
                    dt=gfd[:, DT_LANE0:DT_LANE0 + SSM_HEADS], gate=g["gate"])
    win_dest = jnp.stack([s_.astype(BF16) for s_ in
                          _shards_from_parts([(in_parts[n], c0) for n, c0, _ in IN_SEGMENTS], IN_SHARD)])
    conv_dest = jnp.stack(_shards_from_parts([(g["conv"], 0)], SSM_CONV_DIM // N_DEV))
    small_pack = _pack_small(small_g)
    early_recv = _exchange([win_dest, conv_dest, jnp.broadcast_to(small_pack, (N_DEV,) + small_pack.shape)])
    recv = dict(zip(LATE, late_recv))
    recv["w_in"], recv["conv_w"] = early_recv[0], early_recv[1]

    outs = {}
    for n in BIG:
        shp = big_shapes[n]
        res4 = _adamw(recv[n], big_w[n], args["m_" + n][0], args["v_" + n][0], name="adamw_" + n)
        outs[n] = [r.reshape(shp) for r in res4]
    small4 = _adamw(early_recv[2], _pack_small(small_w), _pack_small({n: args["m_" + n][0] for n in SMALL}),
                    _pack_small({n: args["v_" + n][0] for n in SMALL}), name="adamw_small")
    small_out = [_unpack_small(p, small_shapes) for p in small4]
    for n in SMALL:
        outs[n] = [so[n] for so in small_out]

    order = ("w_in", "b_forget", "conv_w", "conv_b", "dt_bias", "a_log", "d_skip", "ssm_norm_w", "w_proj_attn",
             "w_proj_ssm", "b_gates", "w_out", "ln1_g", "ln1_b", "w_ffn_gate", "w_ffn_up", "w_ffn_down", "ln2_g",
             "ln2_b")
    res = [loss, grad_x[None]]
    for i in range(4):
        res += [outs[n][i] for n in order]
    return tuple(res)
```

```python
import functools
import math

import jax
import jax.numpy as jnp
from jax import lax
from jax.experimental import pallas as pl
from jax.experimental.pallas import tpu as pltpu

F32 = jnp.float32
BF16 = jnp.bfloat16

N_DEV = 8
D_MODEL = 1024
ATT_HEADS = 16
ATT_HEAD_DIM = 64
SSM_INNER = 2048
SSM_HEADS = 32
SSM_HEAD_DIM = 64
SSM_GROUPS = 4
SSM_HEADS_PER_GROUP = 8
SSM_STATE = 128
SSM_CONV = 4
SSM_CHUNK = 128
SSM_CONV_DIM = 3072
FFN_HIDDEN = 2816
IN_WIDTH = 10288
DEEPNORM_ALPHA = 2.0 ** 0.25
LN_EPS = 1e-5
RMS_EPS = 1e-5
ADAM_LR, ADAM_B1, ADAM_B2, ADAM_EPS, ADAM_WD, ADAM_STEP = 0.001, 0.9, 0.999, 1e-08, 0.01, 10
ATT_SCALE = 1.0 / math.sqrt(ATT_HEAD_DIM)

LANES = 128
VMEM_LIMIT = 56 * 1024 * 1024
NEG = -1e30

DT_LANE0 = 0
F_LANE0 = 32
HI = lax.Precision.HIGHEST


def _cp(*sem):
    return pltpu.CompilerParams(dimension_semantics=sem, vmem_limit_bytes=VMEM_LIMIT)


def _tile(n, cap=1408):
    for t in (1408, 1024, 512, 384, 256, 128):
        if t <= cap and n % t == 0:
            return t
    return n


def _sigmoid(x):
    return 1.0 / (1.0 + jnp.exp(-x))


def _mm(a, b, *, ta=False, tb=False, out_dtype=F32, add=None, name):
    m, k = (a.shape[1], a.shape[0]) if ta else a.shape
    n = b.shape[0] if tb else b.shape[1]
    assert (b.shape[1] if tb else b.shape[0]) == k
    tm, tn, tk = _tile(m), _tile(n), _tile(k)
    nk = k // tk
    dims = (((0,) if ta else (1,), (1,) if tb else (0,)), ((), ()))

    def body_single(*refs):
        a_ref, b_ref = refs[:2]
        r = lax.dot_general(a_ref[...].astype(BF16), b_ref[...].astype(BF16), dims, preferred_element_type=F32)
        if add is not None:
            r = r + refs[2][...]
        refs[-1][...] = r.astype(refs[-1].dtype)

    def body(*refs):
        if add is None:
            a_ref, b_ref, o_ref, acc_ref = refs
        else:
            a_ref, b_ref, c_ref, o_ref, acc_ref = refs
        kk = pl.program_id(2)

        @pl.when(kk == 0)
        def _():
            acc_ref[...] = jnp.zeros_like(acc_ref)

        acc_ref[...] += lax.dot_general(a_ref[...].astype(BF16), b_ref[...].astype(BF16), dims,
                                        preferred_element_type=F32)

        @pl.when(kk == nk - 1)
        def _():
            r = acc_ref[...]
            if add is not None:
                r = r + c_ref[...]
            o_ref[...] = r.astype(o_ref.dtype)

    a_spec = pl.BlockSpec((tk, tm), lambda i, j, kk: (kk, i)) if ta else pl.BlockSpec((tm, tk), lambda i, j, kk: (i, kk))
    b_spec = pl.BlockSpec((tn, tk), lambda i, j, kk: (j, kk)) if tb else pl.BlockSpec((tk, tn), lambda i, j, kk: (kk, j))
    o_spec = pl.BlockSpec((tm, tn), lambda i, j, kk: (i, j))
    in_specs, args = [a_spec, b_spec], [a, b]
    if add is not None:
        in_specs.append(o_spec)
        args.append(add)
    return pl.pallas_call(
        body_single if nk == 1 else body, name=name, grid=(m // tm, n // tn, nk), in_specs=in_specs, out_specs=o_spec,
        out_shape=jax.ShapeDtypeStruct((m, n), out_dtype),
        scratch_shapes=[] if nk == 1 else [pltpu.VMEM((tm, tn), F32)],
        compiler_params=_cp("parallel", "parallel", "arbitrary"),
    )(*args)


def _tri(n, lower=True):
    r = lax.broadcasted_iota(jnp.int32, (n, n), 0)
    c = lax.broadcasted_iota(jnp.int32, (n, n), 1)
    return jnp.where((r >= c) if lower else (c >= r), 1.0, 0.0).astype(F32)


def _stats_fwd(fd, bias_row, a_row):
    s = fd.shape[0]
    blk = SSM_CHUNK

    def body(fd_ref, bias_ref, a_ref, dt_ref, ac_ref, cf_ref, dtr_ref, acr_ref, cfr_ref, carry_ref):
        @pl.when(pl.program_id(0) == 0)
        def _():
            carry_ref[...] = jnp.zeros_like(carry_ref)

        v = fd_ref[...] + bias_ref[...]
        dt = jnp.maximum(v, 0.0) + jnp.log(1.0 + jnp.exp(-jnp.abs(v)))
        lf = jnp.minimum(v, 0.0) - jnp.log(1.0 + jnp.exp(-jnp.abs(v)))
        tri = _tri(blk)
        ac = jnp.dot(tri, dt * a_ref[...], precision=HI, preferred_element_type=F32)
        cf = jnp.dot(tri, lf, precision=HI, preferred_element_type=F32) + carry_ref[0:1, :]
        carry_ref[...] = carry_ref[...] + jnp.sum(lf, axis=0, keepdims=True)
        dt_ref[...] = dt
        ac_ref[...] = ac
        cf_ref[...] = cf
        dtr_ref[...] = dt.T
        acr_ref[...] = ac.T
        cfr_ref[...] = cf.T

    col = pl.BlockSpec((blk, LANES), lambda i: (i, 0))
    row = pl.BlockSpec((LANES, blk), lambda i: (0, i))
    vec = pl.BlockSpec((1, LANES), lambda i: (0, 0))
    return pl.pallas_call(
        body, name="stats_fwd", grid=(s // blk,), in_specs=[col, vec, vec],
        out_specs=[col, col, col, row, row, row],
        out_shape=[jax.ShapeDtypeStruct((s, LANES), F32)] * 3 + [jax.ShapeDtypeStruct((LANES, s), F32)] * 3,
        scratch_shapes=[pltpu.VMEM((8, LANES), F32)],
        compiler_params=_cp("arbitrary"),
    )(fd, bias_row, a_row)


def _stats_bwd(fd, bias_row, ddt, dck_rows, dcq_cols):
    s = fd.shape[0]
    blk = SSM_CHUNK
    nb = s // blk

    def body(fd_ref, bias_ref, ddt_ref, dck_ref, dcq_ref, o_ref, db_ref, carry_ref):
        @pl.when(pl.program_id(0) == 0)
        def _():
            carry_ref[...] = jnp.zeros_like(carry_ref)
            db_ref[...] = jnp.zeros_like(db_ref)

        v = fd_ref[...] + bias_ref[...]
        dcum = dck_ref[...].T + dcq_ref[...]
        dlf = jnp.dot(_tri(blk, lower=False), dcum, precision=HI, preferred_element_type=F32) + carry_ref[0:1, :]
        carry_ref[...] = carry_ref[...] + jnp.sum(dcum, axis=0, keepdims=True)
        lane = lax.broadcasted_iota(jnp.int32, v.shape, 1)
        g = jnp.where(lane < F_LANE0, ddt_ref[...] * _sigmoid(v), dlf * _sigmoid(-v))
        g = jnp.where(lane < F_LANE0 + ATT_HEADS, g, 0.0)
        o_ref[...] = g.astype(o_ref.dtype)
        db_ref[...] += jnp.sum(g, axis=0, keepdims=True)

    col = pl.BlockSpec((blk, LANES), lambda i: (nb - 1 - i, 0))
    row = pl.BlockSpec((LANES, blk), lambda i: (0, nb - 1 - i))
    vec = pl.BlockSpec((1, LANES), lambda i: (0, 0))
    return pl.pallas_call(
        body, name="stats_bwd", grid=(nb,), in_specs=[col, vec, col, row, col], out_specs=[col, vec],
        out_shape=[jax.ShapeDtypeStruct((s, LANES), BF16), jax.ShapeDtypeStruct((1, LANES), F32)],
        scratch_shapes=[pltpu.VMEM((8, LANES), F32)],
        compiler_params=_cp("arbitrary"),
    )(fd, bias_row, ddt, dck_rows, dcq_cols)


HP = LANES // ATT_HEAD_DIM
N_HP = ATT_HEADS // HP


def _att_blocks(s):
    return (512, 1024) if s % 1024 == 0 and s >= 4096 else (64, 128)


_QK = (((1,), (1,)), ((), ()))
_HALF = ATT_HEAD_DIM // 2


def _head_cols(a):
    return slice(a * ATT_HEAD_DIM, (a + 1) * ATT_HEAD_DIM)


def _causal(shape, off):
    r = lax.broadcasted_iota(jnp.int32, shape, 0)
    c = lax.broadcasted_iota(jnp.int32, shape, 1)
    return c <= r + off


def _attention_fwd(qkv, ck4, gather_parts):
    s = qkv.shape[0]
    bq, bk = _att_blocks(s)
    nq, nk = s // bq, s // bk
    n = len(gather_parts)

    def body(q_ref, k_ref, v_ref, ck_ref, *rest):
        comm_in, (o_ref, lse_ref), comm_out, sems = rest[:n], rest[n:n + 2], rest[n + 2:2 * n + 2], rest[2 * n + 2:]
        i = pl.program_id(1)
        if n:
            @pl.when((pl.program_id(0) == 0) & (i == 0))
            def _():
                for cp in _comm_copies(comm_in, comm_out, sems, False):
                    cp.start()

        n_full = (i * bq) // bk
        qs = [(q_ref[:, _head_cols(a)].astype(F32) * ATT_SCALE).astype(BF16) for a in range(HP)]

        upper_v = lax.broadcasted_iota(jnp.int32, (bk, LANES), 1) >= ATT_HEAD_DIM
        upper_q = lax.broadcasted_iota(jnp.int32, (bq, LANES), 1) >= ATT_HEAD_DIM

        def step(j, carry, off=None):
            ks = pl.ds(pl.multiple_of(j * bk, bk), bk)
            v_both = v_ref[ks, :]
            out = []
            for a in range(HP):
                m, acc = carry[a]
                sc = lax.dot_general(qs[a], k_ref[ks, _head_cols(a)], _QK, preferred_element_type=F32)
                sc = sc - ck_ref[0, a, pl.ds(j, 1), :]
                if off is not None:
                    sc = jnp.where(_causal(sc.shape, off), sc, NEG)
                m_new = jnp.maximum(m, jnp.max(sc, axis=1, keepdims=True))
                p = jnp.exp((sc - m_new).astype(BF16))
                v_aug = jnp.where(upper_v == (a == 1), v_both, jnp.ones_like(v_both))
                acc = jnp.exp(m - m_new) * acc + jnp.dot(p, v_aug, preferred_element_type=F32)
                out.append((m_new, acc))
            return tuple(out)

        init = tuple((jnp.full((bq, 1), NEG, F32), jnp.zeros((bq, LANES), F32)) for _ in range(HP))
        carry = lax.fori_loop(0, n_full, step, init)
        carry = step(n_full, carry, off=i * bq - n_full * bk)
        outs, lses = [], []
        for a in range(HP):
            m, acc = carry[a]
            l = pltpu.roll(acc, ATT_HEAD_DIM, 1)
            outs.append(acc / l)
            lses.append(m + jnp.log(l))
        o_ref[...] = jnp.where(upper_q, outs[1], outs[0])
        lse_ref[...] = jnp.where(upper_q, lses[1], lses[0])
        if n:
            @pl.when((pl.program_id(0) == N_HP - 1) & (i == nq - 1))
            def _():
                for cp in _comm_copies(comm_in, comm_out, sems, False):
                    cp.wait()

    q_spec = pl.BlockSpec((bq, LANES), lambda h, i: (i, h))
    anyspec = pl.BlockSpec(memory_space=pl.ANY)
    res = pl.pallas_call(
        body, name="att_fwd", grid=(N_HP, nq),
        in_specs=[q_spec, pl.BlockSpec((s, LANES), lambda h, i: (0, N_HP + h)),
                  pl.BlockSpec((s, LANES), lambda h, i: (0, 2 * N_HP + h)),
                  pl.BlockSpec((1, HP, nk, bk), lambda h, i: (h, 0, 0, 0))] + [anyspec] * n,
        out_specs=[q_spec, q_spec] + [anyspec] * n,
        out_shape=[jax.ShapeDtypeStruct((s, D_MODEL), F32)] * 2 + _comm_out_shapes(gather_parts, False),
        scratch_shapes=_comm_sems(n) if n else [],
        compiler_params=_cp("arbitrary", "arbitrary"),
    )(qkv, qkv, qkv, ck4, *gather_parts)
    return res[0], res[1], list(res[2:])


def _att_prep(do, o, lse_rep):
    s = do.shape[0]
    bs = _tile(s, 512)

    def body(do_ref, o_ref, lse_ref, st_ref, dob_ref):
        r = lax.broadcasted_iota(jnp.int32, (LANES, LANES), 0) // ATT_HEAD_DIM
        c = lax.broadcasted_iota(jnp.int32, (LANES, LANES), 1) // ATT_HEAD_DIM
        e = jnp.where(r == c, 1.0, 0.0).astype(F32)
        lane = lax.broadcasted_iota(jnp.int32, (bs, LANES), 1)
        for p in range(D_MODEL // LANES):
            cs = slice(p * LANES, (p + 1) * LANES)
            dd = do_ref[:, cs]
            delta = jnp.dot(dd * o_ref[:, cs], e, precision=HI, preferred_element_type=F32)
            st_ref[:, cs] = jnp.where(lane % ATT_HEAD_DIM < _HALF, lse_ref[:, cs], delta)
            dob_ref[:, cs] = dd.astype(BF16)

    spec = pl.BlockSpec((bs, D_MODEL), lambda i: (i, 0))
    return pl.pallas_call(body, name="att_prep", grid=(s // bs,), in_specs=[spec, spec, spec], out_specs=[spec, spec],
                          out_shape=[jax.ShapeDtypeStruct((s, D_MODEL), F32), jax.ShapeDtypeStruct((s, D_MODEL), BF16)],
                          compiler_params=_cp("parallel"))(do, o, lse_rep)


def _attention_bwd(qkv, ck4, st, do_b, exchange_parts):
    s = qkv.shape[0]
    bq, bk = _att_blocks(s)
    nq, nk, per = s // bq, s // bk, bk // bq
    _T = (((0,), (0,)), ((), ()))
    n = len(exchange_parts)

    def body(q_ref, k_ref, v_ref, ck_ref, st_ref, do_ref, *rest):
        comm_in, (dq_ref, dk_ref, dv_ref, dck_ref, dcq_ref) = rest[:n], rest[n:n + 5]
        comm_out, sems, (dk_acc, dv_acc, dck_acc) = rest[n + 5:2 * n + 5], rest[2 * n + 5:-3], rest[-3:]
        j = pl.program_id(1)
        if n:
            @pl.when((pl.program_id(0) == 0) & (j == 0))
            def _():
                for cp in _comm_copies(comm_in, comm_out, sems, True):
                    cp.start()

        @pl.when(j == 0)
        def _():
            dq_ref[...] = jnp.zeros_like(dq_ref)
            dcq_ref[...] = jnp.zeros_like(dcq_ref)

        dk_acc[...] = jnp.zeros_like(dk_acc)
        dv_acc[...] = jnp.zeros_like(dv_acc)
        dck_acc[...] = jnp.zeros_like(dck_acc)

        k_t = [k_ref[:, _head_cols(a)].T for a in range(HP)]

        def step(i, off=None):
            rows = pl.ds(pl.multiple_of(i * bq, bq), bq)
            for a in range(HP):
                cs = _head_cols(a)
                q = (q_ref[rows, cs].astype(F32) * ATT_SCALE).astype(BF16)
                k = k_ref[:, cs]
                do_a = do_ref[rows, cs]
                sc = lax.dot_general(q, k, _QK, preferred_element_type=F32) - ck_ref[0, a, pl.ds(j, 1), :]
                if off is not None:
                    sc = jnp.where(_causal(sc.shape, off), sc, NEG)
                p = jnp.exp(sc - st_ref[rows, a * ATT_HEAD_DIM:a * ATT_HEAD_DIM + 1])
                dp = lax.dot_general(do_a, v_ref[:, cs], _QK, preferred_element_type=F32)
                ds = p * (dp - st_ref[rows, a * ATT_HEAD_DIM + _HALF:a * ATT_HEAD_DIM + _HALF + 1])
                ds_b = ds.astype(BF16)
                dv_acc[a] += jnp.dot(do_a.T, p.astype(BF16), preferred_element_type=F32)
                dk_acc[a] += jnp.dot(q.T, ds_b, preferred_element_type=F32)
                dq_ref[cs, rows] += lax.dot_general(k_t[a], ds_b, _QK, preferred_element_type=F32) * ATT_SCALE
                dck_acc[a] -= jnp.sum(ds, axis=0, keepdims=True)
                dcq_ref[rows, cs] += jnp.broadcast_to(jnp.sum(ds, axis=1, keepdims=True), (bq, ATT_HEAD_DIM))

        for t in range(per):
            step(j * per + t, off=t * bq)

        def full(i, c):
            step(i)
            return c

        lax.fori_loop((j + 1) * per, nq, full, 0)
        for a in range(HP):
            dk_ref[:, _head_cols(a)] = dk_acc[a].T.astype(dk_ref.dtype)
            dv_ref[:, _head_cols(a)] = dv_acc[a].T.astype(dv_ref.dtype)
            dck_ref[0, a, pl.ds(j, 1), :] = dck_acc[a]
        if n:
            @pl.when((pl.program_id(0) == N_HP - 1) & (j == nk - 1))
            def _():
                for cp in _comm_copies(comm_in, comm_out, sems, True):
                    cp.wait()

    res = pl.BlockSpec((s, LANES), lambda h, j: (0, h))
    ck_spec = pl.BlockSpec((1, HP, nk, bk), lambda h, j: (h, 0, 0, 0))
    kout = pl.BlockSpec((bk, LANES), lambda h, j: (j, h))
    anyspec = pl.BlockSpec(memory_space=pl.ANY)
    outs = pl.pallas_call(
        body, name="att_bwd", grid=(N_HP, nk),
        in_specs=[res, pl.BlockSpec((bk, LANES), lambda h, j: (j, N_HP + h)),
                  pl.BlockSpec((bk, LANES), lambda h, j: (j, 2 * N_HP + h)), ck_spec, res, res] + [anyspec] * n,
        out_specs=[pl.BlockSpec((LANES, s), lambda h, j: (h, 0)), kout, kout, ck_spec, res] + [anyspec] * n,
        out_shape=[jax.ShapeDtypeStruct((D_MODEL, s), F32), jax.ShapeDtypeStruct((s, D_MODEL), BF16),
                   jax.ShapeDtypeStruct((s, D_MODEL), BF16), jax.ShapeDtypeStruct((N_HP, HP, nk, bk), F32),
                   jax.ShapeDtypeStruct((s, D_MODEL), F32)] + _comm_out_shapes(exchange_parts, True),
        scratch_shapes=(_comm_sems(n) if n else [])
        + [pltpu.VMEM((HP, ATT_HEAD_DIM, bk), F32), pltpu.VMEM((HP, ATT_HEAD_DIM, bk), F32),
           pltpu.VMEM((HP, 1, bk), F32)],
        compiler_params=_cp("arbitrary", "arbitrary"),
    )(qkv, qkv, qkv, ck4, st, do_b, *exchange_parts)
    return outs[:5], list(outs[5:])


def _silu_and_grad(x):
    sg = _sigmoid(x)
    return x * sg, sg * (1.0 + x * (1.0 - sg))


def _conv_pre(cur, halo, w_ref, b_ref, first):
    halo = jnp.where(first, 0.0, halo)
    row = lax.broadcasted_iota(jnp.int32, cur.shape, 0)
    shifted = []
    for k in range(SSM_CONV):
        sh = SSM_CONV - 1 - k
        if sh == 0:
            shifted.append(cur)
            continue
        r = pltpu.roll(cur, sh, 0)
        hr = pltpu.roll(halo, sh, 0)
        top = jnp.where(row[0:8] < sh, hr, r[0:8])
        shifted.append(jnp.concatenate([top, r[8:]], axis=0))
    pre = b_ref[...] + sum(w_ref[k:k + 1, :] * shifted[k] for k in range(SSM_CONV))
    return pre, shifted


def _conv_specs(s, bs, bc):
    cur = pl.BlockSpec((bs, bc), lambda j, i: (i, j))
    halo = pl.BlockSpec((8, bc), lambda j, i: (jnp.maximum(i * (bs // 8) - 1, 0), j))
    w = pl.BlockSpec((SSM_CONV, bc), lambda j, i: (0, j))
    b = pl.BlockSpec((1, bc), lambda j, i: (0, j))
    return cur, halo, w, b


def _conv_fwd(xbc, w, b):
    s, c = xbc.shape
    bs, bc = _tile(s, 512), 1024

    def body(x_ref, h_ref, w_ref, b_ref, o_ref):
        pre, _ = _conv_pre(x_ref[...], h_ref[...], w_ref, b_ref, pl.program_id(1) == 0)
        o_ref[...] = pre * _sigmoid(pre)

    cur, halo, ws, bsp = _conv_specs(s, bs, bc)
    return pl.pallas_call(body, name="conv_fwd", grid=(c // bc, s // bs), in_specs=[cur, halo, ws, bsp],
                          out_specs=cur, out_shape=jax.ShapeDtypeStruct((s, c), F32),
                          compiler_params=_cp("parallel", "parallel"))(xbc, xbc, w, b)


def _conv_bwd_pre(xbc, w, b, dact):
    s, c = xbc.shape
    bs, bc = _tile(s, 512), 1024

    def body(x_ref, h_ref, w_ref, b_ref, g_ref, dp_ref, dw_ref, db_ref):
        @pl.when(pl.program_id(1) == 0)
        def _():
            dw_ref[...] = jnp.zeros_like(dw_ref)
            db_ref[...] = jnp.zeros_like(db_ref)

        pre, shifted = _conv_pre(x_ref[...], h_ref[...], w_ref, b_ref, pl.program_id(1) == 0)
        dpre = g_ref[...] * _silu_and_grad(pre)[1]
        dp_ref[...] = dpre
        db_ref[...] += jnp.sum(dpre, axis=0, keepdims=True)
        for k in range(SSM_CONV):
            dw_ref[k:k + 1, :] += jnp.sum(dpre * shifted[k], axis=0, keepdims=True)

    cur, halo, ws, bsp = _conv_specs(s, bs, bc)
    return pl.pallas_call(
        body, name="conv_bwd_pre", grid=(c // bc, s // bs), in_specs=[cur, halo, ws, bsp, cur],
        out_specs=[cur, ws, bsp],
        out_shape=[jax.ShapeDtypeStruct((s, c), F32), jax.ShapeDtypeStruct((SSM_CONV, c), F32),
                   jax.ShapeDtypeStruct((1, c), F32)],
        compiler_params=_cp("parallel", "arbitrary"))(xbc, xbc, w, b, dact)


def _conv_bwd_in(dpre, w):
    s, c = dpre.shape
    bs, bc = _tile(s, 512), 1024
    nb = s // bs

    def body(g_ref, n_ref, w_ref, o_ref):
        cur = g_ref[...]
        nxt = jnp.where(pl.program_id(1) == nb - 1, 0.0, n_ref[...])
        row = lax.broadcasted_iota(jnp.int32, cur.shape, 0)
        acc = w_ref[SSM_CONV - 1:SSM_CONV, :] * cur
        for sh in range(1, SSM_CONV):
            r = pltpu.roll(cur, bs - sh, 0)
            nr = pltpu.roll(nxt, 8 - sh, 0)
            bot = jnp.where(row[0:8] >= 8 - sh, nr, r[bs - 8:])
            acc = acc + w_ref[SSM_CONV - 1 - sh:SSM_CONV - sh, :] * jnp.concatenate([r[:bs - 8], bot], axis=0)
        o_ref[...] = acc.astype(o_ref.dtype)

    cur = pl.BlockSpec((bs, bc), lambda j, i: (i, j))
    nxt = pl.BlockSpec((8, bc), lambda j, i: (jnp.minimum((i + 1) * (bs // 8), s // 8 - 1), j))
    ws = pl.BlockSpec((SSM_CONV, bc), lambda j, i: (0, j))
    return pl.pallas_call(body, name="conv_bwd_in", grid=(c // bc, nb), in_specs=[cur, nxt, ws], out_specs=cur,
                          out_shape=jax.ShapeDtypeStruct((s, c), BF16),
                          compiler_params=_cp("parallel", "parallel"))(dpre, dpre, w)


def _dotT(a, b):
    return lax.dot_general(a.astype(BF16), b.astype(BF16), (((1,), (1,)), ((), ())), preferred_element_type=F32)


def _Tdot(a, b):
    return lax.dot_general(a.astype(BF16), b.astype(BF16), (((0,), (0,)), ((), ())), preferred_element_type=F32)


def _dot(a, b):
    return jnp.dot(a.astype(BF16), b.astype(BF16), preferred_element_type=F32)


def _ssd_head(xbc_ref, dt_ref, ac_ref, acr_ref, h):
    L = SSM_CHUNK
    xs = xbc_ref[:, h * SSM_HEAD_DIM:(h + 1) * SSM_HEAD_DIM]
    dt_col = dt_ref[:, h:h + 1]
    a_col = ac_ref[:, h:h + 1]
    a_row = acr_ref[h:h + 1, :]
    li = lax.broadcasted_iota(jnp.int32, (L, L), 0)
    si = lax.broadcasted_iota(jnp.int32, (L, L), 1)
    decay = jnp.exp(jnp.where(li >= si, a_col - a_row, NEG))
    a_last = ac_ref[L - 1:L, h:h + 1]
    return xs, dt_col, a_col, a_last, decay


def _ssd_fwd(xbc_act, dt_c, ac_c, ac_r, d_skip):
    s = xbc_act.shape[0]
    L, P, N, G, R = SSM_CHUNK, SSM_HEAD_DIM, SSM_STATE, SSM_GROUPS, SSM_HEADS_PER_GROUP
    nc = s // L

    def body(dsk_ref, xbc_ref, dt_ref, ac_ref, acr_ref, y_ref, hp_ref, st_ref):
        @pl.when(pl.program_id(0) == 0)
        def _():
            st_ref[...] = jnp.zeros_like(st_ref)

        for g in range(G):
            b_g = xbc_ref[:, SSM_INNER + g * N:SSM_INNER + (g + 1) * N]
            c_g = xbc_ref[:, SSM_INNER + G * N + g * N:SSM_INNER + G * N + (g + 1) * N]
            cb = _dotT(c_g, b_g)
            for r in range(R):
                h = g * R + r
                xs, dt_col, a_col, a_last, decay = _ssd_head(xbc_ref, dt_ref, ac_ref, acr_ref, h)
                xdt = xs * dt_col
                hprev = st_ref[h]
                y = _dot(cb * decay, xdt) + jnp.exp(a_col) * _dotT(c_g, hprev) + dsk_ref[h] * xs
                y_ref[:, h * P:(h + 1) * P] = y
                hp_ref[0, h] = hprev
                st_ref[h] = hprev * jnp.exp(a_last) + _Tdot(xdt * jnp.exp(a_last - a_col), b_g)

    col = pl.BlockSpec((L, LANES), lambda c: (c, 0))
    return pl.pallas_call(
        body, name="ssd_fwd", grid=(nc,),
        in_specs=[pl.BlockSpec(memory_space=pltpu.SMEM), pl.BlockSpec((L, SSM_CONV_DIM), lambda c: (c, 0)), col, col,
                  pl.BlockSpec((LANES, L), lambda c: (0, c))],
        out_specs=[pl.BlockSpec((L, SSM_INNER), lambda c: (c, 0)),
                   pl.BlockSpec((1, SSM_HEADS, P, N), lambda c: (c, 0, 0, 0))],
        out_shape=[jax.ShapeDtypeStruct((s, SSM_INNER), F32), jax.ShapeDtypeStruct((nc, SSM_HEADS, P, N), F32)],
        scratch_shapes=[pltpu.VMEM((SSM_HEADS, P, N), F32)],
        compiler_params=_cp("arbitrary"),
    )(d_skip, xbc_act, dt_c, ac_c, ac_r)


def _ssd_bwd(xbc_act, dt_c, ac_c, ac_r, hprev_all, dy, d_skip, a_row):
    s = xbc_act.shape[0]
    L, P, N, G, R = SSM_CHUNK, SSM_HEAD_DIM, SSM_STATE, SSM_GROUPS, SSM_HEADS_PER_GROUP
    nc = s // L

    def body(dsk_ref, xbc_ref, dt_ref, ac_ref, acr_ref, hp_ref, dy_ref, arow_ref,
             dx_ref, ddt_ref, da_ref, dds_ref, dh_ref):
        @pl.when(pl.program_id(0) == 0)
        def _():
            dh_ref[...] = jnp.zeros_like(dh_ref)
            da_ref[...] = jnp.zeros_like(da_ref)
            dds_ref[...] = jnp.zeros_like(dds_ref)

        lane = lax.broadcasted_iota(jnp.int32, (L, LANES), 1)
        sub = lax.broadcasted_iota(jnp.int32, (LANES, L), 0)
        rowi = lax.broadcasted_iota(jnp.int32, (L, 1), 0)
        lane1 = lax.broadcasted_iota(jnp.int32, (1, LANES), 1)
        da_c = jnp.zeros((L, LANES), F32)
        da_r = jnp.zeros((LANES, L), F32)
        ddt1 = jnp.zeros((L, LANES), F32)
        dds = jnp.zeros((1, LANES), F32)
        for g in range(G):
            b_g = xbc_ref[:, SSM_INNER + g * N:SSM_INNER + (g + 1) * N]
            c_g = xbc_ref[:, SSM_INNER + G * N + g * N:SSM_INNER + G * N + (g + 1) * N]
            cb = _dotT(c_g, b_g)
            dcb = jnp.zeros((L, L), F32)
            db_g = jnp.zeros((L, N), F32)
            dc_g = jnp.zeros((L, N), F32)
            for r in range(R):
                h = g * R + r
                xs, dt_col, a_col, a_last, decay = _ssd_head(xbc_ref, dt_ref, ac_ref, acr_ref, h)
                gy = dy_ref[:, h * P:(h + 1) * P]
                xdt = xs * dt_col
                hprev = hp_ref[0, h]
                dhn = dh_ref[h]
                e_a = jnp.exp(a_col)
                e_last = jnp.exp(a_last)
                e_col = jnp.exp(a_last - a_col)
                m = cb * decay
                yoff = e_a * _dotT(c_g, hprev)
                da_col = jnp.sum(gy * yoff, axis=1, keepdims=True)
                dc_g = dc_g + e_a * _dot(gy, hprev)
                dhp = _Tdot(gy * e_a, c_g) + dhn * e_last
                da_last = jnp.sum(jnp.sum(dhn * hprev, axis=1, keepdims=True), axis=0, keepdims=True) * e_last
                xds = _dot(xdt, dhn)
                db_g = db_g + e_col * xds
                de_e = jnp.sum(xds * b_g, axis=1, keepdims=True) * e_col
                da_col = da_col - de_e
                da_last = da_last + jnp.sum(de_e, axis=0, keepdims=True)
                dxdt = e_col * _dotT(b_g, dhn)
                dm = _dotT(gy, xdt)
                dxdt = dxdt + _Tdot(m, gy)
                dcb = dcb + dm * decay
                w = dm * m
                da_col = da_col + jnp.sum(w, axis=1, keepdims=True) + jnp.where(rowi == L - 1, da_last, 0.0)
                da_c = jnp.where(lane == h, da_col, da_c)
                da_r = jnp.where(sub == h, jnp.sum(w, axis=0, keepdims=True), da_r)
                ddt1 = jnp.where(lane == h, jnp.sum(dxdt * xs, axis=1, keepdims=True), ddt1)
                dds = jnp.where(lane1 == h, jnp.sum(jnp.sum(gy * xs, axis=1, keepdims=True), axis=0, keepdims=True),
                                dds)
                dx_ref[:, h * P:(h + 1) * P] = dxdt * dt_col + dsk_ref[h] * gy
                dh_ref[h] = dhp
            dx_ref[:, SSM_INNER + g * N:SSM_INNER + (g + 1) * N] = db_g + _Tdot(dcb, c_g)
            dx_ref[:, SSM_INNER + G * N + g * N:SSM_INNER + G * N + (g + 1) * N] = dc_g + _dot(dcb, b_g)
        dda = jnp.dot(_tri(L, lower=False), da_c - da_r.T, precision=HI, preferred_element_type=F32)
        ddt_ref[...] = dda * arow_ref[...] + ddt1
        da_ref[...] += jnp.sum(dda * dt_ref[...], axis=0, keepdims=True)
        dds_ref[...] += dds

    col = pl.BlockSpec((L, LANES), lambda c: (nc - 1 - c, 0))
    vec = pl.BlockSpec((1, LANES), lambda c: (0, 0))
    return pl.pallas_call(
        body, name="ssd_bwd", grid=(nc,),
        in_specs=[pl.BlockSpec(memory_space=pltpu.SMEM), pl.BlockSpec((L, SSM_CONV_DIM), lambda c: (nc - 1 - c, 0)),
                  col, col, pl.BlockSpec((LANES, L), lambda c: (0, nc - 1 - c)),
                  pl.BlockSpec((1, SSM_HEADS, P, N), lambda c: (nc - 1 - c, 0, 0, 0)),
                  pl.BlockSpec((L, SSM_INNER), lambda c: (nc - 1 - c, 0)), vec],
        out_specs=[pl.BlockSpec((L, SSM_CONV_DIM), lambda c: (nc - 1 - c, 0)), col, vec, vec],
        out_shape=[jax.ShapeDtypeStruct((s, SSM_CONV_DIM), F32), jax.ShapeDtypeStruct((s, LANES), F32),
                   jax.ShapeDtypeStruct((1, LANES), F32), jax.ShapeDtypeStruct((1, LANES), F32)],
        scratch_shapes=[pltpu.VMEM((SSM_HEADS, P, N), F32)],
        compiler_params=_cp("arbitrary"),
    )(d_skip, xbc_act, dt_c, ac_c, ac_r, hprev_all, dy, a_row)


N_PAIR = SSM_HEADS // HP
PAIRS_PER_GROUP = SSM_HEADS_PER_GROUP // HP


def _pair_consts():
    L = SSM_CHUNK
    lane = lax.broadcasted_iota(jnp.int32, (L, LANES), 1)
    lane1 = lax.broadcasted_iota(jnp.int32, (1, LANES), 1)
    li = lax.broadcasted_iota(jnp.int32, (L, L), 0)
    si = lax.broadcasted_iota(jnp.int32, (L, L), 1)
    return lane >= ATT_HEAD_DIM, lane1 >= ATT_HEAD_DIM, li, si


def _ssd_pair_fwd(xbc_act, ac_c, dt_r, ac_r, dsk_pair):
    s = xbc_act.shape[0]
    L, N, G = SSM_CHUNK, SSM_STATE, SSM_GROUPS
    nc = s // L

    def body(xbc_ref, ac_ref, dtr_ref, acr_ref, dsk_ref, y_ref, hp_ref, st_ref):
        @pl.when(pl.program_id(0) == 0)
        def _():
            st_ref[...] = jnp.zeros_like(st_ref)

        upper, up1, li, si = _pair_consts()
        for g in range(G):
            b_g = xbc_ref[:, SSM_INNER + g * N:SSM_INNER + (g + 1) * N]
            c_g = xbc_ref[:, SSM_INNER + G * N + g * N:SSM_INNER + G * N + (g + 1) * N]
            cb = _dotT(c_g, b_g)
            b_t = b_g.T
            for q in range(PAIRS_PER_GROUP):
                pp = g * PAIRS_PER_GROUP + q
                cols = slice(pp * LANES, (pp + 1) * LANES)
                xs = xbc_ref[:, cols]
                ht = st_ref[pp]
                hp_ref[0, pp] = ht
                y = dsk_ref[pp:pp + 1, :] * xs
                s_new = jnp.zeros((N, LANES), F32)
                ea, el = [], []
                for a in range(HP):
                    h = HP * pp + a
                    acol = jnp.broadcast_to(ac_ref[:, h:h + 1], (L, LANES))
                    arow, dtrow = acr_ref[h:h + 1, :], dtr_ref[h:h + 1, :]
                    alast = ac_ref[L - 1:L, h:h + 1]
                    decay = jnp.exp(jnp.where(li >= si, acol - arow, NEG))
                    xs_a = jnp.where(upper == (a == 1), xs, 0.0)
                    y = y + _dot(cb * decay * dtrow, xs_a)
                    s_new = s_new + _dot(b_t * (dtrow * jnp.exp(alast - arow)), xs_a)
                    ea.append(jnp.exp(acol))
                    el.append(jnp.exp(alast))
                y_ref[:, cols] = y + jnp.where(upper, ea[1], ea[0]) * _dot(c_g, ht)
                st_ref[pp] = ht * jnp.where(up1, el[1], el[0]) + s_new

    col = pl.BlockSpec((L, LANES), lambda c: (c, 0))
    row = pl.BlockSpec((LANES, L), lambda c: (0, c))
    return pl.pallas_call(
        body, name="ssd_fwd", grid=(nc,),
        in_specs=[pl.BlockSpec((L, SSM_CONV_DIM), lambda c: (c, 0)), col, row, row,
                  pl.BlockSpec((N_PAIR, LANES), lambda c: (0, 0))],
        out_specs=[pl.BlockSpec((L, SSM_INNER), lambda c: (c, 0)),
                   pl.BlockSpec((1, N_PAIR, N, LANES), lambda c: (c, 0, 0, 0))],
        out_shape=[jax.ShapeDtypeStruct((s, SSM_INNER), F32), jax.ShapeDtypeStruct((nc, N_PAIR, N, LANES), F32)],
        scratch_shapes=[pltpu.VMEM((N_PAIR, N, LANES), F32)],
        compiler_params=_cp("arbitrary"),
    )(xbc_act, ac_c, dt_r, ac_r, dsk_pair)


def _ssd_pair_bwd(xbc_act, dt_c, ac_c, dt_r, ac_r, hprev_all, dy, dsk_pair, a_row):
    s = xbc_act.shape[0]
    L, N, G = SSM_CHUNK, SSM_STATE, SSM_GROUPS
    nc = s // L
    rev = lambda c: nc - 1 - c

    def body(xbc_ref, dt_ref, ac_ref, dtr_ref, acr_ref, hp_ref, dy_ref, dsk_ref, arow_ref,
             dx_ref, ddt_ref, da_ref, dds_ref, dh_ref):
        @pl.when(pl.program_id(0) == 0)
        def _():
            dh_ref[...] = jnp.zeros_like(dh_ref)
            da_ref[...] = jnp.zeros_like(da_ref)
            dds_ref[...] = jnp.zeros_like(dds_ref)

        upper, up1, li, si = _pair_consts()
        lane = lax.broadcasted_iota(jnp.int32, (L, LANES), 1)
        sub = lax.broadcasted_iota(jnp.int32, (LANES, L), 0)
        lastrow = lax.broadcasted_iota(jnp.int32, (L, LANES), 0) == L - 1
        da_c = jnp.zeros((L, LANES), F32)
        da_r = jnp.zeros((LANES, L), F32)
        ddt_r = jnp.zeros((LANES, L), F32)
        for g in range(G):
            b_g = xbc_ref[:, SSM_INNER + g * N:SSM_INNER + (g + 1) * N]
            c_g = xbc_ref[:, SSM_INNER + G * N + g * N:SSM_INNER + G * N + (g + 1) * N]
            cb, cb_t = _dotT(c_g, b_g), _dotT(b_g, c_g)
            b_t, c_t = b_g.T, c_g.T
            dcb = jnp.zeros((L, L), F32)
            db_t = jnp.zeros((N, L), F32)
            dc = jnp.zeros((L, N), F32)
            for q in range(PAIRS_PER_GROUP):
                pp = g * PAIRS_PER_GROUP + q
                cols = slice(pp * LANES, (pp + 1) * LANES)
                xs, gy = xbc_ref[:, cols], dy_ref[:, cols]
                ht, dhn = hp_ref[0, pp], dh_ref[pp]
                acol = [jnp.broadcast_to(ac_ref[:, HP * pp + a:HP * pp + a + 1], (L, LANES)) for a in range(HP)]
                alast = [ac_ref[L - 1:L, HP * pp + a:HP * pp + a + 1] for a in range(HP)]
                ea = jnp.where(upper, jnp.exp(acol[1]), jnp.exp(acol[0]))
                el = jnp.where(up1, jnp.exp(alast[1]), jnp.exp(alast[0]))
                ge = gy * ea
                dc = dc + _dotT(ge, ht)
                dh_ref[pp] = _dot(c_t, ge) + dhn * el
                t_off = (ge * _dot(c_g, ht)).astype(BF16)
                hsum = jnp.sum(dhn * ht, axis=0, keepdims=True)
                dxs = dsk_ref[pp:pp + 1, :] * gy
                dds_ref[pp:pp + 1, :] += jnp.sum(gy * xs, axis=0, keepdims=True)
                for a in range(HP):
                    h = HP * pp + a
                    mine, mine1 = upper == (a == 1), up1 == (a == 1)
                    arow, dtrow = acr_ref[h:h + 1, :], dtr_ref[h:h + 1, :]
                    dtcol = jnp.broadcast_to(dt_ref[:, h:h + 1], (L, LANES))
                    xs_a, gy_a = jnp.where(mine, xs, 0.0), jnp.where(mine, gy, 0.0)
                    dhn_a = jnp.where(mine1, dhn, 0.0)
                    e_row = jnp.exp(alast[a] - arow)
                    w_row = dtrow * e_row
                    xd_t = _dotT(dhn_a, xs_a)
                    db_t = db_t + xd_t * w_row
                    dw = jnp.sum(b_t * xd_t, axis=0, keepdims=True)
                    de_e = dw * w_row
                    dal = (jnp.sum(jnp.where(mine1, hsum, 0.0), axis=1, keepdims=True) * jnp.exp(alast[a])
                           + jnp.sum(de_e, axis=1, keepdims=True))
                    dxs = dxs + _dot(b_g, dhn_a) * (dtcol * jnp.exp(alast[a] - acol[a]))
                    decay = jnp.exp(jnp.where(li >= si, acol[a] - arow, NEG))
                    decay_t = jnp.exp(jnp.where(si >= li, arow - acol[a], NEG))
                    m = cb * decay
                    dmdt = _dotT(gy_a, xs_a)
                    dxs = dxs + _dot(cb_t * decay_t * dtcol, gy_a)
                    dm = dmdt * dtrow
                    dcb = dcb + dm * decay
                    wb = (dm * m).astype(BF16)
                    onehot = jnp.where(lane == h, 1.0, 0.0).astype(BF16)
                    da_c = (da_c + jnp.dot(wb, onehot, preferred_element_type=F32)
                            + jnp.dot(jnp.where(mine, t_off, 0.0).astype(BF16), onehot, preferred_element_type=F32)
                            + jnp.where(lastrow & (lane == h), dal, 0.0))
                    da_r = jnp.where(sub == h, -(jnp.sum(wb.astype(F32), axis=0, keepdims=True) + de_e), da_r)
                    ddt_r = jnp.where(sub == h, dw * e_row + jnp.sum(dmdt * m, axis=0, keepdims=True), ddt_r)
                dx_ref[:, cols] = dxs
            dx_ref[:, SSM_INNER + g * N:SSM_INNER + (g + 1) * N] = (db_t + _dot(c_t, dcb)).T
            dx_ref[:, SSM_INNER + G * N + g * N:SSM_INNER + G * N + (g + 1) * N] = dc + _dot(dcb, b_g)
        dda = jnp.dot(_tri(L, lower=False), da_c + da_r.T, precision=HI, preferred_element_type=F32)
        ddt_ref[...] = dda * arow_ref[...] + ddt_r.T
        da_ref[...] += jnp.sum(dda * dt_ref[...], axis=0, keepdims=True)

    col = pl.BlockSpec((L, LANES), lambda c: (rev(c), 0))
    row = pl.BlockSpec((LANES, L), lambda c: (0, rev(c)))
    vec = pl.BlockSpec((1, LANES), lambda c: (0, 0))
    pairs = pl.BlockSpec((N_PAIR, LANES), lambda c: (0, 0))
    return pl.pallas_call(
        body, name="ssd_bwd", grid=(nc,),
        in_specs=[pl.BlockSpec((L, SSM_CONV_DIM), lambda c: (rev(c), 0)), col, col, row, row,
                  pl.BlockSpec((1, N_PAIR, N, LANES), lambda c: (rev(c), 0, 0, 0)),
                  pl.BlockSpec((L, SSM_INNER), lambda c: (rev(c), 0)), pairs, vec],
        out_specs=[pl.BlockSpec((L, SSM_CONV_DIM), lambda c: (rev(c), 0)), col, vec, pairs],
        out_shape=[jax.ShapeDtypeStruct((s, SSM_CONV_DIM), F32), jax.ShapeDtypeStruct((s, LANES), F32),
                   jax.ShapeDtypeStruct((1, LANES), F32), jax.ShapeDtypeStruct((N_PAIR, LANES), F32)],
        scratch_shapes=[pltpu.VMEM((N_PAIR, N, LANES), F32)],
        compiler_params=_cp("arbitrary"),
    )(xbc_act, dt_c, ac_c, dt_r, ac_r, hprev_all, dy, dsk_pair, a_row)


ROWS = 256
GW = SSM_INNER // SSM_GROUPS


def _rows(width, dtype=F32):
    return pl.BlockSpec((ROWS, width), lambda i: (i, 0))


def _vec(width):
    return pl.BlockSpec((1, width), lambda i: (0, 0))


def _gnorm_fwd(y, z, w):
    s = y.shape[0]

    def body(y_ref, z_ref, w_ref, o_ref):
        for g in range(SSM_GROUPS):
            cs = slice(g * GW, (g + 1) * GW)
            zz = z_ref[:, cs].astype(F32)
            u = y_ref[:, cs] * (zz * _sigmoid(zz))
            r = lax.rsqrt(jnp.mean(u * u, axis=1, keepdims=True) + RMS_EPS)
            o_ref[:, cs] = (u * r * w_ref[:, cs]).astype(o_ref.dtype)

    return pl.pallas_call(body, name="gnorm_fwd", grid=(s // ROWS,),
                          in_specs=[_rows(SSM_INNER), _rows(SSM_INNER), _vec(SSM_INNER)], out_specs=_rows(SSM_INNER),
                          out_shape=jax.ShapeDtypeStruct((s, SSM_INNER), BF16), compiler_params=_cp("parallel"))(y, z, w)


def _gnorm_bwd(y, z, w, do):
    s = y.shape[0]

    def body(y_ref, z_ref, w_ref, do_ref, dy_ref, dz_ref, dw_ref):
        @pl.when(pl.program_id(0) == 0)
        def _():
            dw_ref[...] = jnp.zeros_like(dw_ref)

        for g in range(SSM_GROUPS):
            cs = slice(g * GW, (g + 1) * GW)
            zz, yy, dd = z_ref[:, cs].astype(F32), y_ref[:, cs], do_ref[:, cs].astype(F32)
            sz, dsz = _silu_and_grad(zz)
            u = yy * sz
            r = lax.rsqrt(jnp.mean(u * u, axis=1, keepdims=True) + RMS_EPS)
            n = u * r
            dn = dd * w_ref[:, cs]
            dw_ref[:, cs] += jnp.sum(dd * n, axis=0, keepdims=True)
            du = r * (dn - n * jnp.mean(dn * n, axis=1, keepdims=True))
            dy_ref[:, cs] = du * sz
            dz_ref[:, cs] = (du * yy * dsz).astype(dz_ref.dtype)

    return pl.pallas_call(
        body, name="gnorm_bwd", grid=(s // ROWS,),
        in_specs=[_rows(SSM_INNER), _rows(SSM_INNER), _vec(SSM_INNER), _rows(SSM_INNER)],
        out_specs=[_rows(SSM_INNER), _rows(SSM_INNER), _vec(SSM_INNER)],
        out_shape=[jax.ShapeDtypeStruct((s, SSM_INNER), F32), jax.ShapeDtypeStruct((s, SSM_INNER), BF16),
                   jax.ShapeDtypeStruct((1, SSM_INNER), F32)],
        compiler_params=_cp("arbitrary"))(y, z, w, do)


def _mix_fwd(gl, bg, attn_d, ssm_d):
    s = gl.shape[0]
    d = D_MODEL

    def body(gl_ref, bg_ref, a_ref, m_ref, o_ref):
        g0 = _sigmoid(gl_ref[:, :d] + bg_ref[:, :d])
        g1 = _sigmoid(gl_ref[:, d:] + bg_ref[:, d:])
        o_ref[...] = (g0 * a_ref[...] + g1 * m_ref[...]).astype(o_ref.dtype)

    return pl.pallas_call(body, name="mix_fwd", grid=(s // ROWS,),
                          in_specs=[_rows(2 * d), _vec(2 * d), _rows(d), _rows(d)], out_specs=_rows(d),
                          out_shape=jax.ShapeDtypeStruct((s, d), BF16), compiler_params=_cp("parallel"))(
        gl, bg, attn_d, ssm_d)


def _mix_bwd(gl, bg, attn_d, ssm_d, dmix):
    s = gl.shape[0]
    d = D_MODEL

    def body(gl_ref, bg_ref, a_ref, m_ref, dm_ref, da_ref, ds_ref, dg_ref, db_ref):
        @pl.when(pl.program_id(0) == 0)
        def _():
            db_ref[...] = jnp.zeros_like(db_ref)

        g0 = _sigmoid(gl_ref[:, :d] + bg_ref[:, :d])
        g1 = _sigmoid(gl_ref[:, d:] + bg_ref[:, d:])
        dm = dm_ref[...].astype(F32)
        da_ref[...] = (dm * g0).astype(da_ref.dtype)
        ds_ref[...] = (dm * g1).astype(ds_ref.dtype)
        dl0 = dm * a_ref[...] * g0 * (1.0 - g0)
        dl1 = dm * m_ref[...] * g1 * (1.0 - g1)
        dg_ref[:, :d] = dl0.astype(dg_ref.dtype)
        dg_ref[:, d:] = dl1.astype(dg_ref.dtype)
        db_ref[:, :d] += jnp.sum(dl0, axis=0, keepdims=True)
        db_ref[:, d:] += jnp.sum(dl1, axis=0, keepdims=True)

    return pl.pallas_call(
        body, name="mix_bwd", grid=(s // ROWS,),
        in_specs=[_rows(2 * d), _vec(2 * d), _rows(d), _rows(d), _rows(d)],
        out_specs=[_rows(d), _rows(d), _rows(2 * d), _vec(2 * d)],
        out_shape=[jax.ShapeDtypeStruct((s, d), BF16), jax.ShapeDtypeStruct((s, d), BF16),
                   jax.ShapeDtypeStruct((s, 2 * d), BF16), jax.ShapeDtypeStruct((1, 2 * d), F32)],
        compiler_params=_cp("arbitrary"))(gl, bg, attn_d, ssm_d, dmix)


def _ln_stats(p):
    mu = jnp.mean(p, axis=1, keepdims=True)
    c = p - mu
    rstd = lax.rsqrt(jnp.mean(c * c, axis=1, keepdims=True) + LN_EPS)
    return c * rstd, rstd


def _ln_bwd(dy, xhat, rstd, g):
    dxh = dy * g
    return rstd * (dxh - jnp.mean(dxh, axis=1, keepdims=True) - xhat * jnp.mean(dxh * xhat, axis=1, keepdims=True))


def _ln1_fwd(x, mixed, g, b):
    s, d = x.shape

    def body(x_ref, m_ref, g_ref, b_ref, o_ref):
        xhat, _ = _ln_stats(DEEPNORM_ALPHA * x_ref[...] + m_ref[...])
        o_ref[...] = xhat * g_ref[...] + b_ref[...]

    return pl.pallas_call(body, name="ln1_fwd", grid=(s // ROWS,), in_specs=[_rows(d), _rows(d), _vec(d), _vec(d)],
                          out_specs=_rows(d), out_shape=jax.ShapeDtypeStruct((s, d), F32),
                          compiler_params=_cp("parallel"))(x, mixed, g, b)


def _ln2_loss(x1, h, target, g, b):
    s, d = x1.shape

    def body(x_ref, h_ref, t_ref, g_ref, b_ref, dp_ref, loss_ref, dg_ref, db_ref):
        @pl.when(pl.program_id(0) == 0)
        def _():
            loss_ref[...] = jnp.zeros_like(loss_ref)
            dg_ref[...] = jnp.zeros_like(dg_ref)
            db_ref[...] = jnp.zeros_like(db_ref)

        xhat, rstd = _ln_stats(DEEPNORM_ALPHA * x_ref[...] + h_ref[...])
        err = xhat * g_ref[...] + b_ref[...] - t_ref[...]
        part = 0.5 * jnp.sum(jnp.mean(err * err, axis=1, keepdims=True), axis=0, keepdims=True)
        loss_ref[...] += jnp.broadcast_to(part, loss_ref.shape)
        dy = err * (1.0 / d)
        dg_ref[...] += jnp.sum(dy * xhat, axis=0, keepdims=True)
        db_ref[...] += jnp.sum(dy, axis=0, keepdims=True)
        dp_ref[...] = _ln_bwd(dy, xhat, rstd, g_ref[...])

    return pl.pallas_call(
        body, name="ln2_loss", grid=(s // ROWS,), in_specs=[_rows(d), _rows(d), _rows(d), _vec(d), _vec(d)],
        out_specs=[_rows(d), _vec(LANES), _vec(d), _vec(d)],
        out_shape=[jax.ShapeDtypeStruct((s, d), F32), jax.ShapeDtypeStruct((1, LANES), F32),
                   jax.ShapeDtypeStruct((1, d), F32), jax.ShapeDtypeStruct((1, d), F32)],
        compiler_params=_cp("arbitrary"))(x1, h, target, g, b)


def _ln1_bwd(x, mixed, g, dpre2, dffn):
    s, d = x.shape

    def body(x_ref, m_ref, g_ref, d2_ref, df_ref, dp_ref, dr_ref, dg_ref, db_ref):
        @pl.when(pl.program_id(0) == 0)
        def _():
            dg_ref[...] = jnp.zeros_like(dg_ref)
            db_ref[...] = jnp.zeros_like(db_ref)

        xhat, rstd = _ln_stats(DEEPNORM_ALPHA * x_ref[...] + m_ref[...])
        dy = DEEPNORM_ALPHA * d2_ref[...] + df_ref[...]
        dg_ref[...] += jnp.sum(dy * xhat, axis=0, keepdims=True)
        db_ref[...] += jnp.sum(dy, axis=0, keepdims=True)
        dp = _ln_bwd(dy, xhat, rstd, g_ref[...])
        dp_ref[...] = dp
        dr_ref[...] = DEEPNORM_ALPHA * dp

    return pl.pallas_call(
        body, name="ln1_bwd", grid=(s // ROWS,), in_specs=[_rows(d), _rows(d), _vec(d), _rows(d), _rows(d)],
        out_specs=[_rows(d), _rows(d), _vec(d), _vec(d)],
        out_shape=[jax.ShapeDtypeStruct((s, d), F32), jax.ShapeDtypeStruct((s, d), F32),
                   jax.ShapeDtypeStruct((1, d), F32), jax.ShapeDtypeStruct((1, d), F32)],
        compiler_params=_cp("arbitrary"))(x, mixed, g, dpre2, dffn)


def _swiglu_fwd(gu):
    s = gu.shape[0]
    f = FFN_HIDDEN

    def body(g_ref, u_ref, o_ref):
        gg = g_ref[...].astype(F32)
        o_ref[...] = (gg * _sigmoid(gg) * u_ref[...].astype(F32)).astype(o_ref.dtype)

    return pl.pallas_call(
        body, name="swiglu_fwd", grid=(s // ROWS,),
        in_specs=[pl.BlockSpec((ROWS, f), lambda i: (i, 0)), pl.BlockSpec((ROWS, f), lambda i: (i, 1))],
        out_specs=_rows(f), out_shape=jax.ShapeDtypeStruct((s, f), BF16), compiler_params=_cp("parallel"))(gu, gu)


def _swiglu_bwd(gu, dact):
    s = gu.shape[0]
    f = FFN_HIDDEN

    def body(g_ref, u_ref, d_ref, o_ref):
        sg, dsg = _silu_and_grad(g_ref[...].astype(F32))
        dd = d_ref[...].astype(F32)
        o_ref[:, :f] = (dd * u_ref[...].astype(F32) * dsg).astype(o_ref.dtype)
        o_ref[:, f:] = (dd * sg).astype(o_ref.dtype)

    return pl.pallas_call(
        body, name="swiglu_bwd", grid=(s // ROWS,),
        in_specs=[pl.BlockSpec((ROWS, f), lambda i: (i, 0)), pl.BlockSpec((ROWS, f), lambda i: (i, 1)), _rows(f)],
        out_specs=_rows(2 * f), out_shape=jax.ShapeDtypeStruct((s, 2 * f), BF16),
        compiler_params=_cp("parallel"))(gu, gu, dact)


def _peer(k):
    x, y, c = lax.axis_index("x"), lax.axis_index("y"), lax.axis_index("c")
    kx, ky, kc = (k >> 2) & 1, (k >> 1) & 1, k & 1
    px = (1 - x) if kx else x
    py = (1 - y) if ky else y
    pc = (1 - c) if kc else c
    return (px, py, pc), 4 * px + 2 * py + pc


def _my_index():
    return 4 * lax.axis_index("x") + 2 * lax.axis_index("y") + lax.axis_index("c")


def _comm_copies(ins, outs, sems, scatter):
    send_sems, recv_sems, local_sems = sems
    me = _my_index()
    copies = [pltpu.make_async_copy(ins[t].at[me] if scatter else ins[t], outs[t].at[me], local_sems.at[t])
              for t in range(len(ins))]
    for k in range(1, N_DEV):
        peer, pidx = _peer(k)
        for t in range(len(ins)):
            copies.append(pltpu.make_async_remote_copy(
                src_ref=ins[t].at[pidx] if scatter else ins[t], dst_ref=outs[t].at[me],
                send_sem=send_sems.at[t, k - 1], recv_sem=recv_sems.at[t, k - 1], device_id=peer,
                device_id_type=pl.DeviceIdType.MESH))
    return copies


def _comm_sems(n):
    return [pltpu.SemaphoreType.DMA((n, N_DEV - 1)), pltpu.SemaphoreType.DMA((n, N_DEV - 1)),
            pltpu.SemaphoreType.DMA((n,))]


def _comm_out_shapes(parts, scatter):
    return [jax.ShapeDtypeStruct(p.shape if scatter else (N_DEV,) + p.shape, p.dtype) for p in parts]


def _comm_call(parts, scatter, name):
    n = len(parts)

    def body(*refs):
        copies = _comm_copies(refs[:n], refs[n:2 * n], refs[2 * n:], scatter)
        for cp in copies:
            cp.start()
        for cp in copies:
            cp.wait()

    anyspec = pl.BlockSpec(memory_space=pl.ANY)
    return pl.pallas_call(body, name=name, in_specs=[anyspec] * n, out_specs=[anyspec] * n,
                          out_shape=_comm_out_shapes(parts, scatter), scratch_shapes=_comm_sems(n))(*parts)


def _all_gather(parts):
    return _comm_call(parts, False, "all_gather")


def _exchange(parts):
    return _comm_call(parts, True, "grad_exchange")


def _adamw(recv, w, m, v, name):
    _, r, c = w.shape
    br = _tile(r, 128)
    c1 = 1.0 / (1.0 - ADAM_B1 ** ADAM_STEP)
    c2 = 1.0 / (1.0 - ADAM_B2 ** ADAM_STEP)

    def body(r_ref, w_ref, m_ref, v_ref, g_ref, d_ref, mo_ref, vo_ref):
        g = r_ref[0].astype(F32)
        for k in range(1, N_DEV):
            g = g + r_ref[k].astype(F32)
        mn = ADAM_B1 * m_ref[0] + (1.0 - ADAM_B1) * g
        vn = ADAM_B2 * v_ref[0] + (1.0 - ADAM_B2) * (g * g)
        g_ref[0] = g
        mo_ref[0] = mn
        vo_ref[0] = vn
        d_ref[0] = -ADAM_LR * ((mn * c1) / (jnp.sqrt(vn * c2) + ADAM_EPS) + ADAM_WD * w_ref[0])

    blk = pl.BlockSpec((1, br, c), lambda i: (0, i, 0))
    return pl.pallas_call(
        body, name=name, grid=(r // br,),
        in_specs=[pl.BlockSpec((N_DEV, br, c), lambda i: (0, i, 0)), blk, blk, blk],
        out_specs=[blk] * 4, out_shape=[jax.ShapeDtypeStruct((1, r, c), F32)] * 4,
        compiler_params=_cp("parallel"))(recv, w, m, v)


def _lane_row(pairs):
    row = jnp.zeros((LANES,), F32)
    for lane0, vec in pairs:
        row = lax.dynamic_update_slice(row, vec.astype(F32), (lane0,))
    return row.reshape(1, LANES)


def _stage_in(x, wts, small):
    s = x.shape[0]
    a = -jnp.exp(small["a_log"])
    bias_row = _lane_row([(DT_LANE0, small["dt_bias"]), (F_LANE0, small["b_forget"])])
    a_row = _lane_row([(DT_LANE0, a)])
    conv_b = small["conv_b"].reshape(1, -1)
    norm_w = small["ssm_norm_w"].reshape(1, -1)
    bg = small["b_gates"].reshape(1, -1)
    g1, b1 = small["ln1_g"].reshape(1, -1), small["ln1_b"].reshape(1, -1)
    g2, b2 = small["ln2_g"].reshape(1, -1), small["ln2_b"].reshape(1, -1)
    d_skip = small["d_skip"]
    xb = x.astype(BF16)

    qkv = _mm(xb, wts["qkv"], out_dtype=BF16, name="f_qkv")
    z = _mm(xb, wts["z"], out_dtype=BF16, name="f_z")
    xbc = _mm(xb, wts["xbc"], name="f_xbc")
    gl = _mm(xb, wts["gate"], out_dtype=BF16, name="f_gate")
    fd = _mm(xb, wts["fd"], name="f_fd")
    dt_c, ac_c, cf_c, dt_r, ac_r, cf_r = _stats_fwd(fd, bias_row, a_row)
    bk = _att_blocks(s)[1]
    ck4 = cf_r[F_LANE0:F_LANE0 + ATT_HEADS].reshape(N_HP, HP, s // bk, bk)
    return dict(locals())


def _stage_mid(c, attn, lse, wts, target):
    x, xb, qkv, z, xbc, gl, fd, ck4 = (c[k] for k in ("x", "xb", "qkv", "z", "xbc", "gl", "fd", "ck4"))
    dt_c, ac_c, dt_r, ac_r, a_row, bias_row = (c[k] for k in ("dt_c", "ac_c", "dt_r", "ac_r", "a_row", "bias_row"))
    conv_b, norm_w, bg, g1, b1, g2, b2, d_skip = (c[k] for k in ("conv_b", "norm_w", "bg", "g1", "b1", "g2", "b2",
                                                                "d_skip"))
    conv_w = c["wts"]["conv"]
    attn_d = _mm(attn, wts["pa"], out_dtype=BF16, name="f_pa")
    xact = _conv_fwd(xbc, conv_w, conv_b)
    dsk_pair = jnp.repeat(d_skip, SSM_HEAD_DIM).reshape(N_PAIR, LANES)
    y, hprev = _ssd_pair_fwd(xact, ac_c, dt_r, ac_r, dsk_pair)
    ssm = _gnorm_fwd(y, z, norm_w)
    ssm_d = _mm(ssm, wts["ps"], out_dtype=BF16, name="f_ps")
    mix = _mix_fwd(gl, bg, attn_d, ssm_d)
    mixed = _mm(mix, wts["out"], name="f_out")
    x1 = _ln1_fwd(x, mixed, g1, b1)
    gu = _mm(x1, wts["gu"], out_dtype=BF16, name="f_gu")
    act = _swiglu_fwd(gu)
    h = _mm(act, wts["down"], name="f_down")
    dpre2, loss_row, dg2, db2 = _ln2_loss(x1, h, target, g2, b2)

    d_act = _mm(dpre2, wts["down"], tb=True, out_dtype=BF16, name="b_down_x")
    dw_down = _mm(act, dpre2, ta=True, name="b_down_w")
    dgu = _swiglu_bwd(gu, d_act)
    dffn = _mm(dgu, wts["gu"], tb=True, name="b_gu_x")
    dw_gu = _mm(x1, dgu, ta=True, name="b_gu_w")
    dpre1, dxr, dg1, db1 = _ln1_bwd(x, mixed, g1, dpre2, dffn)
    dmix = _mm(dpre1, wts["out"], tb=True, out_dtype=BF16, name="b_out_x")
    dw_out = _mm(mix, dpre1, ta=True, name="b_out_w")
    dattn_d, dssm_d, dgl, dbg = _mix_bwd(gl, bg, attn_d, ssm_d, dmix)
    dssm = _mm(dssm_d, wts["ps"], tb=True, out_dtype=BF16, name="b_ps_x")
    dw_ps = _mm(ssm, dssm_d, ta=True, name="b_ps_w")
    dattn = _mm(dattn_d, wts["pa"], tb=True, name="b_pa_x")
    dw_pa = _mm(attn, dattn_d, ta=True, name="b_pa_w")
    dy, dz, dnw = _gnorm_bwd(y, z, norm_w, dssm)
    dxact, ddt, da_row, dds_pair = _ssd_pair_bwd(xact, dt_c, ac_c, dt_r, ac_r, hprev, dy, dsk_pair, a_row)
    dds = dds_pair.reshape(SSM_HEADS, SSM_HEAD_DIM).sum(axis=1)
    dpre_c, dconv_w, dconv_b = _conv_bwd_pre(xbc, conv_w, conv_b, dxact)
    dxbc = _conv_bwd_in(dpre_c, conv_w)
    st, do_b = _att_prep(dattn, attn, lse)
    late = dict(pa=dw_pa, ps=dw_ps, out=dw_out, gu=dw_gu, down=dw_down)
    keep = ("st", "do_b", "ddt", "dxr", "dz", "dxbc", "dgl", "dconv_w", "dconv_b", "da_row", "dds", "dnw", "dbg",
            "dg1", "db1", "dg2", "db2", "loss_row")
    loc = locals()
    return {**c, **{k: loc[k] for k in keep}}, late


def _stage_out(c, att_grads):
    dq, dk, dv, dck, dcq = att_grads
    wts, xb, fd, bias_row, ddt, dxr, dz, dxbc, dgl = (c[k] for k in ("wts", "xb", "fd", "bias_row", "ddt", "dxr", "dz",
                                                                  "dxbc", "dgl"))
    s, d, a = xb.shape[0], D_MODEL, c["a"]
    dck_rows = jnp.zeros((LANES, s), F32).at[F_LANE0:F_LANE0 + ATT_HEADS].set(dck.reshape(ATT_HEADS, s))
    dcq_cols = jnp.zeros((s, LANES), F32).at[:, F_LANE0:F_LANE0 + ATT_HEADS].set(dcq[:, ::ATT_HEAD_DIM])
    dfd, dbias = _stats_bwd(fd, bias_row, ddt, dck_rows, dcq_cols)

    wq, wk, wv = wts["qkv"][:, :d], wts["qkv"][:, d:2 * d], wts["qkv"][:, 2 * d:]
    dx = dxr
    for i, (g_, w_) in enumerate(((dq, wq), (dk, wk), (dv, wv), (dz, wts["z"]), (dxbc, wts["xbc"]),
                                  (dgl, wts["gate"]), (dfd, wts["fd"]))):
        dx = _mm(g_, w_, ta=(i == 0), tb=True, add=dx, name=f"b_in_x{i}")
    dw_in = [_mm(xb, g_, ta=True, tb=(i == 0), name=f"b_in_w{i}")
             for i, g_ in enumerate((dq, dk, dv, dz, dxbc, dgl, dfd))]

    grads = dict(q=dw_in[0], k=dw_in[1], v=dw_in[2], z=dw_in[3], xbc=dw_in[4], gate=dw_in[5], fd=dw_in[6],
                 conv=c["dconv_w"])
    small_g = dict(
        b_forget=dbias[0, F_LANE0:F_LANE0 + ATT_HEADS], conv_b=c["dconv_b"][0], dt_bias=dbias[0, :SSM_HEADS],
        a_log=c["da_row"][0, :SSM_HEADS] * a, d_skip=c["dds"], ssm_norm_w=c["dnw"][0], b_gates=c["dbg"][0],
        ln1_g=c["dg1"][0], ln1_b=c["db1"][0], ln2_g=c["dg2"][0], ln2_b=c["db2"][0])
    return c["loss_row"][0, 0], dx, grads, small_g


BIG = ("w_in", "w_proj_attn", "w_proj_ssm", "w_out", "w_ffn_gate", "w_ffn_up", "w_ffn_down", "conv_w")
EARLY = ("w_in", "conv_w")
LATE = ("w_proj_attn", "w_proj_ssm", "w_out", "w_ffn_gate", "w_ffn_up", "w_ffn_down")
SMALL = ("b_forget", "conv_b", "dt_bias", "a_log", "d_skip", "ssm_norm_w", "b_gates", "ln1_g", "ln1_b", "ln2_g",
         "ln2_b")
SMALL_ROWS = 96
IN_SHARD = IN_WIDTH // N_DEV
IN_SEGMENTS = (("q", 0, 1024), ("k", 1024, 1024), ("v", 2048, 1024), ("f", 3072, ATT_HEADS), ("z", 3088, SSM_INNER),
               ("xbc", 5136, SSM_CONV_DIM), ("dt", 8208, SSM_HEADS), ("gate", 8240, 2 * D_MODEL))


def _cols_from_shards(shards, lo, hi):
    w = shards[0].shape[1]
    pieces = []
    for j in range(len(shards)):
        a, b = max(lo, j * w), min(hi, (j + 1) * w)
        if a < b:
            pieces.append(shards[j][:, a - j * w:b - j * w])
    return pieces[0] if len(pieces) == 1 else jnp.concatenate(pieces, axis=1)


def _shards_from_parts(parts, width):
    shards = []
    for j in range(N_DEV):
        lo, hi = j * width, (j + 1) * width
        pieces = []
        for mat, c0 in parts:
            a, b = max(lo, c0), min(hi, c0 + mat.shape[1])
            if a < b:
                pieces.append(mat[:, a - c0:b - c0])
        shards.append(pieces[0] if len(pieces) == 1 else jnp.concatenate(pieces, axis=1))
    return shards


def _pack_small(vals):
    flat = jnp.concatenate([vals[n].reshape(-1) for n in SMALL])
    return jnp.pad(flat, (0, SMALL_ROWS * LANES - flat.shape[0])).reshape(SMALL_ROWS, LANES)


def _unpack_small(pack, shapes):
    flat = pack.reshape(-1)
    out, off = {}, 0
    for n in SMALL:
        sz = math.prod(shapes[n])
        out[n] = flat[off:off + sz].reshape(shapes[n])
        off += sz
    return out


def kernel(x, w_in, b_forget, conv_w, conv_b, dt_bias, a_log, d_skip, ssm_norm_w, w_proj_attn, w_proj_ssm, b_gates, w_out, ln1_g, ln1_b, w_ffn_gate, w_ffn_up, w_ffn_down, ln2_g, ln2_b, loss_target, m_w_in, m_b_forget, m_conv_w, m_conv_b, m_dt_bias, m_a_log, m_d_skip, m_ssm_norm_w, m_w_proj_attn, m_w_proj_ssm, m_b_gates, m_w_out, m_ln1_g, m_ln1_b, m_w_ffn_gate, m_w_ffn_up, m_w_ffn_down, m_ln2_g, m_ln2_b, v_w_in, v_b_forget, v_conv_w, v_conv_b, v_dt_bias, v_a_log, v_d_skip, v_ssm_norm_w, v_w_proj_attn, v_w_proj_ssm, v_b_gates, v_w_out, v_ln1_g, v_ln1_b, v_w_ffn_gate, v_w_ffn_up, v_w_ffn_down, v_ln2_g, v_ln2_b):
    args = dict(locals())
    d, f = D_MODEL, FFN_HIDDEN
    big_w = {n: args[n][0] for n in BIG}
    small_w = {n: args[n][0] for n in SMALL}
    big_shapes = {n: args[n].shape for n in BIG}
    small_shapes = {n: args[n].shape for n in SMALL}

    early = dict(zip(EARLY, _all_gather([big_w["w_in"].astype(BF16), big_w["conv_w"]])))
    in_shards = [early["w_in"][j] for j in range(N_DEV)]
    seg = {n: _cols_from_shards(in_shards, c0, c0 + w) for n, c0, w in IN_SEGMENTS}
    wfd = jnp.concatenate([seg["dt"], seg["f"], jnp.zeros((d, LANES - SSM_HEADS - ATT_HEADS), BF16)], axis=1)
    wts = dict(qkv=jnp.concatenate([seg["q"], seg["k"], seg["v"]], axis=1), z=seg["z"], xbc=seg["xbc"],
               gate=seg["gate"], fd=wfd, conv=jnp.concatenate([early["conv_w"][j] for j in range(N_DEV)], axis=1))

    ctx = _stage_in(x[0], wts, small_w)
    attn, lse, gathered = _attention_fwd(ctx["qkv"], ctx["ck4"], [big_w[n].astype(BF16) for n in LATE])
    full = dict(zip(LATE, gathered))
    late_w = dict(
        pa=full["w_proj_attn"].reshape(d, d), ps=full["w_proj_ssm"].reshape(SSM_INNER, d),
        out=full["w_out"].reshape(d, d),
        gu=jnp.concatenate([full["w_ffn_gate"][j] for j in range(N_DEV)]
                           + [full["w_ffn_up"][j] for j in range(N_DEV)], axis=1),
        down=full["w_ffn_down"].reshape(f, d))
    ctx, gl = _stage_mid(ctx, attn, lse, late_w, loss_target[0])
    late_dest = dict(
        w_ffn_gate=jnp.stack([s_.astype(BF16) for s_ in _shards_from_parts([(gl["gu"][:, :f], 0)], f // N_DEV)]),
        w_ffn_up=jnp.stack([s_.astype(BF16) for s_ in _shards_from_parts([(gl["gu"][:, f:], 0)], f // N_DEV)]))
    for n, key in (("w_proj_attn", "pa"), ("w_proj_ssm", "ps"), ("w_out", "out"), ("w_ffn_down", "down")):
        late_dest[n] = gl[key].astype(BF16).reshape((N_DEV,) + big_shapes[n][1:])
    att_grads, late_recv = _attention_bwd(ctx["qkv"], ctx["ck4"], ctx["st"], ctx["do_b"], [late_dest[n] for n in LATE])
    loss_part, grad_x, g, small_g = _stage_out(ctx, att_grads)
    loss = lax.psum(loss_part, ("x", "y", "c"))

    gfd = g["fd"]
    in_parts = dict(q=g["q"], k=g["k"], v=g["v"], f=gfd[:, F_LANE0:F_LANE0 + ATT_HEADS], z=g["z"], xbc=g["xbc"],
                    dt=gfd[:, DT_LANE0:DT_LANE0 + SSM_HEADS], gate=g["gate"])
    win_dest = jnp.stack([s_.astype(BF16) for s_ in
                          _shards_from_parts([(in_parts[n], c0) for n, c0, _ in IN_SEGMENTS], IN_SHARD)])
    conv_dest = jnp.stack(_shards_from_parts([(g["conv"], 0)], SSM_CONV_DIM // N_DEV))
    small_pack = _pack_small(small_g)
    early_recv = _exchange([win_dest, conv_dest, jnp.broadcast_to(small_pack, (N_DEV,) + small_pack.shape)])
    recv = dict(zip(LATE, late_recv))
    recv["w_in"], recv["conv_w"] = early_recv[0], early_recv[1]

    outs = {}
    for n in BIG:
        outs[n] = _adamw(recv[n], args[n], args["m_" + n], args["v_" + n], name="adamw_" + n)
    small4 = _adamw(early_recv[2], _pack_small(small_w)[None], _pack_small({n: args["m_" + n][0] for n in SMALL})[None],
                    _pack_small({n: args["v_" + n][0] for n in SMALL})[None], name="adamw_small")
    small_out = [_unpack_small(p, small_shapes) for p in small4]
    for n in SMALL:
        outs[n] = [so[n] for so in small_out]

    order = ("w_in", "b_forget", "conv_w", "conv_b", "dt_bias", "a_log", "d_skip", "ssm_norm_w", "w_proj_attn",
             "w_proj_ssm", "b_gates", "w_out", "ln1_g", "ln1_b", "w_ffn_gate", "w_ffn_up", "w_ffn_down", "ln2_g",
             "ln2_b")
    res = [loss, grad_x[None]]
    for i in range(4):
        res += [outs[n][i] for n in order]
    return tuple(res)
```

```python
import functools
import math

import jax
import jax.numpy as jnp
from jax import lax
from jax.experimental import pallas as pl
from jax.experimental.pallas import tpu as pltpu

F32 = jnp.float32
BF16 = jnp.bfloat16

N_DEV = 8
D_MODEL = 1024
ATT_HEADS = 16
ATT_HEAD_DIM = 64
SSM_INNER = 2048
SSM_HEADS = 32
SSM_HEAD_DIM = 64
SSM_GROUPS = 4
SSM_HEADS_PER_GROUP = 8
SSM_STATE = 128
SSM_CONV = 4
SSM_CHUNK = 128
SSM_CONV_DIM = 3072
FFN_HIDDEN = 2816
IN_WIDTH = 10288
DEEPNORM_ALPHA = 2.0 ** 0.25
LN_EPS = 1e-5
RMS_EPS = 1e-5
ADAM_LR, ADAM_B1, ADAM_B2, ADAM_EPS, ADAM_WD, ADAM_STEP = 0.001, 0.9, 0.999, 1e-08, 0.01, 10
ATT_SCALE = 1.0 / math.sqrt(ATT_HEAD_DIM)

LANES = 128
VMEM_LIMIT = 56 * 1024 * 1024
NEG = -1e30

DT_LANE0 = 0
F_LANE0 = 32
HI = lax.Precision.HIGHEST


def _cp(*sem):
    return pltpu.CompilerParams(dimension_semantics=sem, vmem_limit_bytes=VMEM_LIMIT)


def _tile(n, cap=1408):
    for t in (1408, 1024, 512, 384, 256, 128):
        if t <= cap and n % t == 0:
            return t
    return n


def _sigmoid(x):
    return 1.0 / (1.0 + jnp.exp(-x))


def _mm(a, b, *, ta=False, tb=False, out_dtype=F32, add=None, name):
    m, k = (a.shape[1], a.shape[0]) if ta else a.shape
    n = b.shape[0] if tb else b.shape[1]
    assert (b.shape[1] if tb else b.shape[0]) == k
    tm, tn, tk = _tile(m), _tile(n), _tile(k)
    nk = k // tk
    dims = (((0,) if ta else (1,), (1,) if tb else (0,)), ((), ()))

    def body_single(*refs):
        a_ref, b_ref = refs[:2]
        r = lax.dot_general(a_ref[...].astype(BF16), b_ref[...].astype(BF16), dims, preferred_element_type=F32)
        if add is not None:
            r = r + refs[2][...]
        refs[-1][...] = r.astype(refs[-1].dtype)

    def body(*refs):
        if add is None:
            a_ref, b_ref, o_ref, acc_ref = refs
        else:
            a_ref, b_ref, c_ref, o_ref, acc_ref = refs
        kk = pl.program_id(2)

        @pl.when(kk == 0)
        def _():
            acc_ref[...] = jnp.zeros_like(acc_ref)

        acc_ref[...] += lax.dot_general(a_ref[...].astype(BF16), b_ref[...].astype(BF16), dims,
                                        preferred_element_type=F32)

        @pl.when(kk == nk - 1)
        def _():
            r = acc_ref[...]
            if add is not None:
                r = r + c_ref[...]
            o_ref[...] = r.astype(o_ref.dtype)

    a_spec = pl.BlockSpec((tk, tm), lambda i, j, kk: (kk, i)) if ta else pl.BlockSpec((tm, tk), lambda i, j, kk: (i, kk))
    b_spec = pl.BlockSpec((tn, tk), lambda i, j, kk: (j, kk)) if tb else pl.BlockSpec((tk, tn), lambda i, j, kk: (kk, j))
    o_spec = pl.BlockSpec((tm, tn), lambda i, j, kk: (i, j))
    in_specs, args = [a_spec, b_spec], [a, b]
    if add is not None:
        in_specs.append(o_spec)
        args.append(add)
    return pl.pallas_call(
        body_single if nk == 1 else body, name=name, grid=(m // tm, n // tn, nk), in_specs=in_specs, out_specs=o_spec,
        out_shape=jax.ShapeDtypeStruct((m, n), out_dtype),
        scratch_shapes=[] if nk == 1 else [pltpu.VMEM((tm, tn), F32)],
        compiler_params=_cp("parallel", "parallel", "arbitrary"),
    )(*args)


def _tri(n, lower=True):
    r = lax.broadcasted_iota(jnp.int32, (n, n), 0)
    c = lax.broadcasted_iota(jnp.int32, (n, n), 1)
    return jnp.where((r >= c) if lower else (c >= r), 1.0, 0.0).astype(F32)


def _stats_fwd(fd, bias_row, a_row):
    s = fd.shape[0]
    blk = SSM_CHUNK

    def body(fd_ref, bias_ref, a_ref, dt_ref, ac_ref, cf_ref, dtr_ref, acr_ref, cfr_ref, carry_ref):
        @pl.when(pl.program_id(0) == 0)
        def _():
            carry_ref[...] = jnp.zeros_like(carry_ref)

        v = fd_ref[...] + bias_ref[...]
        dt = jnp.maximum(v, 0.0) + jnp.log(1.0 + jnp.exp(-jnp.abs(v)))
        lf = jnp.minimum(v, 0.0) - jnp.log(1.0 + jnp.exp(-jnp.abs(v)))
        tri = _tri(blk)
        ac = jnp.dot(tri, dt * a_ref[...], precision=HI, preferred_element_type=F32)
        cf = jnp.dot(tri, lf, precision=HI, preferred_element_type=F32) + carry_ref[0:1, :]
        carry_ref[...] = carry_ref[...] + jnp.sum(lf, axis=0, keepdims=True)
        dt_ref[...] = dt
        ac_ref[...] = ac
        cf_ref[...] = cf
        dtr_ref[...] = dt.T
        acr_ref[...] = ac.T
        cfr_ref[...] = cf.T

    col = pl.BlockSpec((blk, LANES), lambda i: (i, 0))
    row = pl.BlockSpec((LANES, blk), lambda i: (0, i))
    vec = pl.BlockSpec((1, LANES), lambda i: (0, 0))
    return pl.pallas_call(
        body, name="stats_fwd", grid=(s // blk,), in_specs=[col, vec, vec],
        out_specs=[col, col, col, row, row, row],
        out_shape=[jax.ShapeDtypeStruct((s, LANES), F32)] * 3 + [jax.ShapeDtypeStruct((LANES, s), F32)] * 3,
        scratch_shapes=[pltpu.VMEM((8, LANES), F32)],
        compiler_params=_cp("arbitrary"),
    )(fd, bias_row, a_row)


def _stats_bwd(fd, bias_row, ddt, dck_rows, dcq_cols):
    s = fd.shape[0]
    blk = SSM_CHUNK
    nb = s // blk

    def body(fd_ref, bias_ref, ddt_ref, dck_ref, dcq_ref, o_ref, db_ref, carry_ref):
        @pl.when(pl.program_id(0) == 0)
        def _():
            carry_ref[...] = jnp.zeros_like(carry_ref)
            db_ref[...] = jnp.zeros_like(db_ref)

        v = fd_ref[...] + bias_ref[...]
        dcum = dck_ref[...].T + dcq_ref[...]
        dlf = jnp.dot(_tri(blk, lower=False), dcum, precision=HI, preferred_element_type=F32) + carry_ref[0:1, :]
        carry_ref[...] = carry_ref[...] + jnp.sum(dcum, axis=0, keepdims=True)
        lane = lax.broadcasted_iota(jnp.int32, v.shape, 1)
        g = jnp.where(lane < F_LANE0, ddt_ref[...] * _sigmoid(v), dlf * _sigmoid(-v))
        g = jnp.where(lane < F_LANE0 + ATT_HEADS, g, 0.0)
        o_ref[...] = g.astype(o_ref.dtype)
        db_ref[...] += jnp.sum(g, axis=0, keepdims=True)

    col = pl.BlockSpec((blk, LANES), lambda i: (nb - 1 - i, 0))
    row = pl.BlockSpec((LANES, blk), lambda i: (0, nb - 1 - i))
    vec = pl.BlockSpec((1, LANES), lambda i: (0, 0))
    return pl.pallas_call(
        body, name="stats_bwd", grid=(nb,), in_specs=[col, vec, col, row, col], out_specs=[col, vec],
        out_shape=[jax.ShapeDtypeStruct((s, LANES), BF16), jax.ShapeDtypeStruct((1, LANES), F32)],
        scratch_shapes=[pltpu.VMEM((8, LANES), F32)],
        compiler_params=_cp("arbitrary"),
    )(fd, bias_row, ddt, dck_rows, dcq_cols)


HP = LANES // ATT_HEAD_DIM
N_HP = ATT_HEADS // HP


def _att_blocks(s):
    return (512, 1024) if s % 1024 == 0 and s >= 4096 else (64, 128)


_QK = (((1,), (1,)), ((), ()))
_HALF = ATT_HEAD_DIM // 2


def _head_cols(a):
    return slice(a * ATT_HEAD_DIM, (a + 1) * ATT_HEAD_DIM)


def _causal(shape, off):
    r = lax.broadcasted_iota(jnp.int32, shape, 0)
    c = lax.broadcasted_iota(jnp.int32, shape, 1)
    return c <= r + off


def _attention_fwd(qkv, ck4, gather_parts):
    s = qkv.shape[0]
    bq, bk = _att_blocks(s)
    nq, nk = s // bq, s // bk
    n = len(gather_parts)

    def body(q_ref, k_ref, v_ref, ck_ref, *rest):
        comm_in, (o_ref, lse_ref), comm_out, sems = rest[:n], rest[n:n + 2], rest[n + 2:2 * n + 2], rest[2 * n + 2:]
        i = pl.program_id(1)
        if n:
            @pl.when((pl.program_id(0) == 0) & (i == 0))
            def _():
                for cp in _comm_copies(comm_in, comm_out, sems, False):
                    cp.start()

        n_full = (i * bq) // bk
        qs = [(q_ref[:, _head_cols(a)].astype(F32) * ATT_SCALE).astype(BF16) for a in range(HP)]

        upper_v = lax.broadcasted_iota(jnp.int32, (bk, LANES), 1) >= ATT_HEAD_DIM
        upper_q = lax.broadcasted_iota(jnp.int32, (bq, LANES), 1) >= ATT_HEAD_DIM

        def step(j, carry, off=None):
            ks = pl.ds(pl.multiple_of(j * bk, bk), bk)
            v_both = v_ref[ks, :]
            out = []
            for a in range(HP):
                m, acc = carry[a]
                sc = lax.dot_general(qs[a], k_ref[ks, _head_cols(a)], _QK, preferred_element_type=F32)
                sc = sc - ck_ref[0, a, pl.ds(j, 1), :]
                if off is not None:
                    sc = jnp.where(_causal(sc.shape, off), sc, NEG)
                m_new = jnp.maximum(m, jnp.max(sc, axis=1, keepdims=True))
                p = jnp.exp((sc - m_new).astype(BF16))
                v_aug = jnp.where(upper_v == (a == 1), v_both, jnp.ones_like(v_both))
                acc = jnp.exp(m - m_new) * acc + jnp.dot(p, v_aug, preferred_element_type=F32)
                out.append((m_new, acc))
            return tuple(out)

        init = tuple((jnp.full((bq, 1), NEG, F32), jnp.zeros((bq, LANES), F32)) for _ in range(HP))
        carry = lax.fori_loop(0, n_full, step, init)
        carry = step(n_full, carry, off=i * bq - n_full * bk)
        outs, lses = [], []
        for a in range(HP):
            m, acc = carry[a]
            l = pltpu.roll(acc, ATT_HEAD_DIM, 1)
            outs.append(acc / l)
            lses.append(m + jnp.log(l))
        o_ref[...] = jnp.where(upper_q, outs[1], outs[0])
        lse_ref[...] = jnp.where(upper_q, lses[1], lses[0])
        if n:
            @pl.when((pl.program_id(0) == N_HP - 1) & (i == nq - 1))
            def _():
                for cp in _comm_copies(comm_in, comm_out, sems, False):
                    cp.wait()

    q_spec = pl.BlockSpec((bq, LANES), lambda h, i: (i, h))
    anyspec = pl.BlockSpec(memory_space=pl.ANY)
    res = pl.pallas_call(
        body, name="att_fwd", grid=(N_HP, nq),
        in_specs=[q_spec, pl.BlockSpec((s, LANES), lambda h, i: (0, N_HP + h)),
                  pl.BlockSpec((s, LANES), lambda h, i: (0, 2 * N_HP + h)),
                  pl.BlockSpec((1, HP, nk, bk), lambda h, i: (h, 0, 0, 0))] + [anyspec] * n,
        out_specs=[q_spec, q_spec] + [anyspec] * n,
        out_shape=[jax.ShapeDtypeStruct((s, D_MODEL), F32)] * 2 + _comm_out_shapes(gather_parts, False),
        scratch_shapes=_comm_sems(n) if n else [],
        compiler_params=_cp("arbitrary", "arbitrary"),
    )(qkv, qkv, qkv, ck4, *gather_parts)
    return res[0], res[1], list(res[2:])


def _att_prep(do, o, lse_rep):
    s = do.shape[0]
    bs = _tile(s, 512)

    def body(do_ref, o_ref, lse_ref, st_ref, dob_ref):
        r = lax.broadcasted_iota(jnp.int32, (LANES, LANES), 0) // ATT_HEAD_DIM
        c = lax.broadcasted_iota(jnp.int32, (LANES, LANES), 1) // ATT_HEAD_DIM
        e = jnp.where(r == c, 1.0, 0.0).astype(F32)
        lane = lax.broadcasted_iota(jnp.int32, (bs, LANES), 1)
        for p in range(D_MODEL // LANES):
            cs = slice(p * LANES, (p + 1) * LANES)
            dd = do_ref[:, cs]
            delta = jnp.dot(dd * o_ref[:, cs], e, precision=HI, preferred_element_type=F32)
            st_ref[:, cs] = jnp.where(lane % ATT_HEAD_DIM < _HALF, lse_ref[:, cs], delta)
            dob_ref[:, cs] = dd.astype(BF16)

    spec = pl.BlockSpec((bs, D_MODEL), lambda i: (i, 0))
    return pl.pallas_call(body, name="att_prep", grid=(s // bs,), in_specs=[spec, spec, spec], out_specs=[spec, spec],
                          out_shape=[jax.ShapeDtypeStruct((s, D_MODEL), F32), jax.ShapeDtypeStruct((s, D_MODEL), BF16)],
                          compiler_params=_cp("parallel"))(do, o, lse_rep)


def _attention_bwd(qkv, ck4, st, do_b, exchange_parts):
    s = qkv.shape[0]
    bq, bk = _att_blocks(s)
    nq, nk, per = s // bq, s // bk, bk // bq
    _T = (((0,), (0,)), ((), ()))
    n = len(exchange_parts)

    def body(q_ref, k_ref, v_ref, ck_ref, st_ref, do_ref, *rest):
        comm_in, (dq_ref, dk_ref, dv_ref, dck_ref, dcq_ref) = rest[:n], rest[n:n + 5]
        comm_out, sems, (dk_acc, dv_acc, dck_acc) = rest[n + 5:2 * n + 5], rest[2 * n + 5:-3], rest[-3:]
        j = pl.program_id(1)
        if n:
            @pl.when((pl.program_id(0) == 0) & (j == 0))
            def _():
                for cp in _comm_copies(comm_in, comm_out, sems, True):
                    cp.start()

        @pl.when(j == 0)
        def _():
            dq_ref[...] = jnp.zeros_like(dq_ref)
            dcq_ref[...] = jnp.zeros_like(dcq_ref)

        dk_acc[...] = jnp.zeros_like(dk_acc)
        dv_acc[...] = jnp.zeros_like(dv_acc)
        dck_acc[...] = jnp.zeros_like(dck_acc)

        k_t = [k_ref[:, _head_cols(a)].T for a in range(HP)]

        def step(i, off=None):
            rows = pl.ds(pl.multiple_of(i * bq, bq), bq)
            for a in range(HP):
                cs = _head_cols(a)
                q = (q_ref[rows, cs].astype(F32) * ATT_SCALE).astype(BF16)
                k = k_ref[:, cs]
                do_a = do_ref[rows, cs]
                sc = lax.dot_general(q, k, _QK, preferred_element_type=F32) - ck_ref[0, a, pl.ds(j, 1), :]
                if off is not None:
                    sc = jnp.where(_causal(sc.shape, off), sc, NEG)
                p = jnp.exp(sc - st_ref[rows, a * ATT_HEAD_DIM:a * ATT_HEAD_DIM + 1])
                dp = lax.dot_general(do_a, v_ref[:, cs], _QK, preferred_element_type=F32)
                ds = p * (dp - st_ref[rows, a * ATT_HEAD_DIM + _HALF:a * ATT_HEAD_DIM + _HALF + 1])
                ds_b = ds.astype(BF16)
                dv_acc[a] += jnp.dot(do_a.T, p.astype(BF16), preferred_element_type=F32)
                dk_acc[a] += jnp.dot(q.T, ds_b, preferred_element_type=F32)
                dq_ref[cs, rows] += lax.dot_general(k_t[a], ds_b, _QK, preferred_element_type=F32) * ATT_SCALE
                dck_acc[a] -= jnp.sum(ds, axis=0, keepdims=True)
                dcq_ref[rows, cs] += jnp.broadcast_to(jnp.sum(ds, axis=1, keepdims=True), (bq, ATT_HEAD_DIM))

        for t in range(per):
            step(j * per + t, off=t * bq)

        def full(i, c):
            step(i)
            return c

        lax.fori_loop((j + 1) * per, nq, full, 0)
        for a in range(HP):
            dk_ref[:, _head_cols(a)] = dk_acc[a].T.astype(dk_ref.dtype)
            dv_ref[:, _head_cols(a)] = dv_acc[a].T.astype(dv_ref.dtype)
            dck_ref[0, a, pl.ds(j, 1), :] = dck_acc[a]
        if n:
            @pl.when((pl.program_id(0) == N_HP - 1) & (j == nk - 1))
            def _():
                for cp in _comm_copies(comm_in, comm_out, sems, True):
                    cp.wait()

    res = pl.BlockSpec((s, LANES), lambda h, j: (0, h))
    ck_spec = pl.BlockSpec((1, HP, nk, bk), lambda h, j: (h, 0, 0, 0))
    kout = pl.BlockSpec((bk, LANES), lambda h, j: (j, h))
    anyspec = pl.BlockSpec(memory_space=pl.ANY)
    outs = pl.pallas_call(
        body, name="att_bwd", grid=(N_HP, nk),
        in_specs=[res, pl.BlockSpec((bk, LANES), lambda h, j: (j, N_HP + h)),
                  pl.BlockSpec((bk, LANES), lambda h, j: (j, 2 * N_HP + h)), ck_spec, res, res] + [anyspec] * n,
        out_specs=[pl.BlockSpec((LANES, s), lambda h, j: (h, 0)), kout, kout, ck_spec, res] + [anyspec] * n,
        out_shape=[jax.ShapeDtypeStruct((D_MODEL, s), F32), jax.ShapeDtypeStruct((s, D_MODEL), BF16),
                   jax.ShapeDtypeStruct((s, D_MODEL), BF16), jax.ShapeDtypeStruct((N_HP, HP, nk, bk), F32),
                   jax.ShapeDtypeStruct((s, D_MODEL), F32)] + _comm_out_shapes(exchange_parts, True),
        scratch_shapes=(_comm_sems(n) if n else [])
        + [pltpu.VMEM((HP, ATT_HEAD_DIM, bk), F32), pltpu.VMEM((HP, ATT_HEAD_DIM, bk), F32),
           pltpu.VMEM((HP, 1, bk), F32)],
        compiler_params=_cp("arbitrary", "arbitrary"),
    )(qkv, qkv, qkv, ck4, st, do_b, *exchange_parts)
    return outs[:5], list(outs[5:])


def _silu_and_grad(x):
    sg = _sigmoid(x)
    return x * sg, sg * (1.0 + x * (1.0 - sg))


def _conv_pre(cur, halo, w_ref, b_ref, first):
    halo = jnp.where(first, 0.0, halo)
    row = lax.broadcasted_iota(jnp.int32, cur.shape, 0)
    shifted = []
    for k in range(SSM_CONV):
        sh = SSM_CONV - 1 - k
        if sh == 0:
            shifted.append(cur)
            continue
        r = pltpu.roll(cur, sh, 0)
        hr = pltpu.roll(halo, sh, 0)
        top = jnp.where(row[0:8] < sh, hr, r[0:8])
        shifted.append(jnp.concatenate([top, r[8:]], axis=0))
    pre = b_ref[...] + sum(w_ref[k:k + 1, :] * shifted[k] for k in range(SSM_CONV))
    return pre, shifted


def _conv_specs(s, bs, bc):
    cur = pl.BlockSpec((bs, bc), lambda j, i: (i, j))
    halo = pl.BlockSpec((8, bc), lambda j, i: (jnp.maximum(i * (bs // 8) - 1, 0), j))
    w = pl.BlockSpec((SSM_CONV, bc), lambda j, i: (0, j))
    b = pl.BlockSpec((1, bc), lambda j, i: (0, j))
    return cur, halo, w, b


def _conv_fwd(xbc, w, b):
    s, c = xbc.shape
    bs, bc = _tile(s, 512), 1024

    def body(x_ref, h_ref, w_ref, b_ref, o_ref):
        pre, _ = _conv_pre(x_ref[...], h_ref[...], w_ref, b_ref, pl.program_id(1) == 0)
        o_ref[...] = pre * _sigmoid(pre)

    cur, halo, ws, bsp = _conv_specs(s, bs, bc)
    return pl.pallas_call(body, name="conv_fwd", grid=(c // bc, s // bs), in_specs=[cur, halo, ws, bsp],
                          out_specs=cur, out_shape=jax.ShapeDtypeStruct((s, c), F32),
                          compiler_params=_cp("parallel", "parallel"))(xbc, xbc, w, b)


def _conv_bwd_pre(xbc, w, b, dact):
    s, c = xbc.shape
    bs, bc = _tile(s, 512), 1024

    def body(x_ref, h_ref, w_ref, b_ref, g_ref, dp_ref, dw_ref, db_ref):
        @pl.when(pl.program_id(1) == 0)
        def _():
            dw_ref[...] = jnp.zeros_like(dw_ref)
            db_ref[...] = jnp.zeros_like(db_ref)

        pre, shifted = _conv_pre(x_ref[...], h_ref[...], w_ref, b_ref, pl.program_id(1) == 0)
        dpre = g_ref[...] * _silu_and_grad(pre)[1]
        dp_ref[...] = dpre
        db_ref[...] += jnp.sum(dpre, axis=0, keepdims=True)
        for k in range(SSM_CONV):
            dw_ref[k:k + 1, :] += jnp.sum(dpre * shifted[k], axis=0, keepdims=True)

    cur, halo, ws, bsp = _conv_specs(s, bs, bc)
    return pl.pallas_call(
        body, name="conv_bwd_pre", grid=(c // bc, s // bs), in_specs=[cur, halo, ws, bsp, cur],
        out_specs=[cur, ws, bsp],
        out_shape=[jax.ShapeDtypeStruct((s, c), F32), jax.ShapeDtypeStruct((SSM_CONV, c), F32),
                   jax.ShapeDtypeStruct((1, c), F32)],
        compiler_params=_cp("parallel", "arbitrary"))(xbc, xbc, w, b, dact)


def _conv_bwd_in(dpre, w):
    s, c = dpre.shape
    bs, bc = _tile(s, 512), 1024
    nb = s // bs

    def body(g_ref, n_ref, w_ref, o_ref):
        cur = g_ref[...]
        nxt = jnp.where(pl.program_id(1) == nb - 1, 0.0, n_ref[...])
        row = lax.broadcasted_iota(jnp.int32, cur.shape, 0)
        acc = w_ref[SSM_CONV - 1:SSM_CONV, :] * cur
        for sh in range(1, SSM_CONV):
            r = pltpu.roll(cur, bs - sh, 0)
            nr = pltpu.roll(nxt, 8 - sh, 0)
            bot = jnp.where(row[0:8] >= 8 - sh, nr, r[bs - 8:])
            acc = acc + w_ref[SSM_CONV - 1 - sh:SSM_CONV - sh, :] * jnp.concatenate([r[:bs - 8], bot], axis=0)
        o_ref[...] = acc.astype(o_ref.dtype)

    cur = pl.BlockSpec((bs, bc), lambda j, i: (i, j))
    nxt = pl.BlockSpec((8, bc), lambda j, i: (jnp.minimum((i + 1) * (bs // 8), s // 8 - 1), j))
    ws = pl.BlockSpec((SSM_CONV, bc), lambda j, i: (0, j))
    return pl.pallas_call(body, name="conv_bwd_in", grid=(c // bc, nb), in_specs=[cur, nxt, ws], out_specs=cur,
                          out_shape=jax.ShapeDtypeStruct((s, c), BF16),
                          compiler_params=_cp("parallel", "parallel"))(dpre, dpre, w)


def _dotT(a, b):
    return lax.dot_general(a.astype(BF16), b.astype(BF16), (((1,), (1,)), ((), ())), preferred_element_type=F32)


def _Tdot(a, b):
    return lax.dot_general(a.astype(BF16), b.astype(BF16), (((0,), (0,)), ((), ())), preferred_element_type=F32)


def _dot(a, b):
    return jnp.dot(a.astype(BF16), b.astype(BF16), preferred_element_type=F32)


def _ssd_head(xbc_ref, dt_ref, ac_ref, acr_ref, h):
    L = SSM_CHUNK
    xs = xbc_ref[:, h * SSM_HEAD_DIM:(h + 1) * SSM_HEAD_DIM]
    dt_col = dt_ref[:, h:h + 1]
    a_col = ac_ref[:, h:h + 1]
    a_row = acr_ref[h:h + 1, :]
    li = lax.broadcasted_iota(jnp.int32, (L, L), 0)
    si = lax.broadcasted_iota(jnp.int32, (L, L), 1)
    decay = jnp.exp(jnp.where(li >= si, a_col - a_row, NEG))
    a_last = ac_ref[L - 1:L, h:h + 1]
    return xs, dt_col, a_col, a_last, decay


def _ssd_fwd(xbc_act, dt_c, ac_c, ac_r, d_skip):
    s = xbc_act.shape[0]
    L, P, N, G, R = SSM_CHUNK, SSM_HEAD_DIM, SSM_STATE, SSM_GROUPS, SSM_HEADS_PER_GROUP
    nc = s // L

    def body(dsk_ref, xbc_ref, dt_ref, ac_ref, acr_ref, y_ref, hp_ref, st_ref):
        @pl.when(pl.program_id(0) == 0)
        def _():
            st_ref[...] = jnp.zeros_like(st_ref)

        for g in range(G):
            b_g = xbc_ref[:, SSM_INNER + g * N:SSM_INNER + (g + 1) * N]
            c_g = xbc_ref[:, SSM_INNER + G * N + g * N:SSM_INNER + G * N + (g + 1) * N]
            cb = _dotT(c_g, b_g)
            for r in range(R):
                h = g * R + r
                xs, dt_col, a_col, a_last, decay = _ssd_head(xbc_ref, dt_ref, ac_ref, acr_ref, h)
                xdt = xs * dt_col
                hprev = st_ref[h]
                y = _dot(cb * decay, xdt) + jnp.exp(a_col) * _dotT(c_g, hprev) + dsk_ref[h] * xs
                y_ref[:, h * P:(h + 1) * P] = y
                hp_ref[0, h] = hprev
                st_ref[h] = hprev * jnp.exp(a_last) + _Tdot(xdt * jnp.exp(a_last - a_col), b_g)

    col = pl.BlockSpec((L, LANES), lambda c: (c, 0))
    return pl.pallas_call(
        body, name="ssd_fwd", grid=(nc,),
        in_specs=[pl.BlockSpec(memory_space=pltpu.SMEM), pl.BlockSpec((L, SSM_CONV_DIM), lambda c: (c, 0)), col, col,
                  pl.BlockSpec((LANES, L), lambda c: (0, c))],
        out_specs=[pl.BlockSpec((L, SSM_INNER), lambda c: (c, 0)),
                   pl.BlockSpec((1, SSM_HEADS, P, N), lambda c: (c, 0, 0, 0))],
        out_shape=[jax.ShapeDtypeStruct((s, SSM_INNER), F32), jax.ShapeDtypeStruct((nc, SSM_HEADS, P, N), F32)],
        scratch_shapes=[pltpu.VMEM((SSM_HEADS, P, N), F32)],
        compiler_params=_cp("arbitrary"),
    )(d_skip, xbc_act, dt_c, ac_c, ac_r)


def _ssd_bwd(xbc_act, dt_c, ac_c, ac_r, hprev_all, dy, d_skip, a_row):
    s = xbc_act.shape[0]
    L, P, N, G, R = SSM_CHUNK, SSM_HEAD_DIM, SSM_STATE, SSM_GROUPS, SSM_HEADS_PER_GROUP
    nc = s // L

    def body(dsk_ref, xbc_ref, dt_ref, ac_ref, acr_ref, hp_ref, dy_ref, arow_ref,
             dx_ref, ddt_ref, da_ref, dds_ref, dh_ref):
        @pl.when(pl.program_id(0) == 0)
        def _():
            dh_ref[...] = jnp.zeros_like(dh_ref)
            da_ref[...] = jnp.zeros_like(da_ref)
            dds_ref[...] = jnp.zeros_like(dds_ref)

        lane = lax.broadcasted_iota(jnp.int32, (L, LANES), 1)
        sub = lax.broadcasted_iota(jnp.int32, (LANES, L), 0)
        rowi = lax.broadcasted_iota(jnp.int32, (L, 1), 0)
        lane1 = lax.broadcasted_iota(jnp.int32, (1, LANES), 1)
        da_c = jnp.zeros((L, LANES), F32)
        da_r = jnp.zeros((LANES, L), F32)
        ddt1 = jnp.zeros((L, LANES), F32)
        dds = jnp.zeros((1, LANES), F32)
        for g in range(G):
            b_g = xbc_ref[:, SSM_INNER + g * N:SSM_INNER + (g + 1) * N]
            c_g = xbc_ref[:, SSM_INNER + G * N + g * N:SSM_INNER + G * N + (g + 1) * N]
            cb = _dotT(c_g, b_g)
            dcb = jnp.zeros((L, L), F32)
            db_g = jnp.zeros((L, N), F32)
            dc_g = jnp.zeros((L, N), F32)
            for r in range(R):
                h = g * R + r
                xs, dt_col, a_col, a_last, decay = _ssd_head(xbc_ref, dt_ref, ac_ref, acr_ref, h)
                gy = dy_ref[:, h * P:(h + 1) * P]
                xdt = xs * dt_col
                hprev = hp_ref[0, h]
                dhn = dh_ref[h]
                e_a = jnp.exp(a_col)
                e_last = jnp.exp(a_last)
                e_col = jnp.exp(a_last - a_col)
                m = cb * decay
                yoff = e_a * _dotT(c_g, hprev)
                da_col = jnp.sum(gy * yoff, axis=1, keepdims=True)
                dc_g = dc_g + e_a * _dot(gy, hprev)
                dhp = _Tdot(gy * e_a, c_g) + dhn * e_last
                da_last = jnp.sum(jnp.sum(dhn * hprev, axis=1, keepdims=True), axis=0, keepdims=True) * e_last
                xds = _dot(xdt, dhn)
                db_g = db_g + e_col * xds
                de_e = jnp.sum(xds * b_g, axis=1, keepdims=True) * e_col
                da_col = da_col - de_e
                da_last = da_last + jnp.sum(de_e, axis=0, keepdims=True)
                dxdt = e_col * _dotT(b_g, dhn)
                dm = _dotT(gy, xdt)
                dxdt = dxdt + _Tdot(m, gy)
                dcb = dcb + dm * decay
                w = dm * m
                da_col = da_col + jnp.sum(w, axis=1, keepdims=True) + jnp.where(rowi == L - 1, da_last, 0.0)
                da_c = jnp.where(lane == h, da_col, da_c)
                da_r = jnp.where(sub == h, jnp.sum(w, axis=0, keepdims=True), da_r)
                ddt1 = jnp.where(lane == h, jnp.sum(dxdt * xs, axis=1, keepdims=True), ddt1)
                dds = jnp.where(lane1 == h, jnp.sum(jnp.sum(gy * xs, axis=1, keepdims=True), axis=0, keepdims=True),
                                dds)
                dx_ref[:, h * P:(h + 1) * P] = dxdt * dt_col + dsk_ref[h] * gy
                dh_ref[h] = dhp
            dx_ref[:, SSM_INNER + g * N:SSM_INNER + (g + 1) * N] = db_g + _Tdot(dcb, c_g)
            dx_ref[:, SSM_INNER + G * N + g * N:SSM_INNER + G * N + (g + 1) * N] = dc_g + _dot(dcb, b_g)
        dda = jnp.dot(_tri(L, lower=False), da_c - da_r.T, precision=HI, preferred_element_type=F32)
        ddt_ref[...] = dda * arow_ref[...] + ddt1
        da_ref[...] += jnp.sum(dda * dt_ref[...], axis=0, keepdims=True)
        dds_ref[...] += dds

    col = pl.BlockSpec((L, LANES), lambda c: (nc - 1 - c, 0))
    vec = pl.BlockSpec((1, LANES), lambda c: (0, 0))
    return pl.pallas_call(
        body, name="ssd_bwd", grid=(nc,),
        in_specs=[pl.BlockSpec(memory_space=pltpu.SMEM), pl.BlockSpec((L, SSM_CONV_DIM), lambda c: (nc - 1 - c, 0)),
                  col, col, pl.BlockSpec((LANES, L), lambda c: (0, nc - 1 - c)),
                  pl.BlockSpec((1, SSM_HEADS, P, N), lambda c: (nc - 1 - c, 0, 0, 0)),
                  pl.BlockSpec((L, SSM_INNER), lambda c: (nc - 1 - c, 0)), vec],
        out_specs=[pl.BlockSpec((L, SSM_CONV_DIM), lambda c: (nc - 1 - c, 0)), col, vec, vec],
        out_shape=[jax.ShapeDtypeStruct((s, SSM_CONV_DIM), F32), jax.ShapeDtypeStruct((s, LANES), F32),
                   jax.ShapeDtypeStruct((1, LANES), F32), jax.ShapeDtypeStruct((1, LANES), F32)],
        scratch_shapes=[pltpu.VMEM((SSM_HEADS, P, N), F32)],
        compiler_params=_cp("arbitrary"),
    )(d_skip, xbc_act, dt_c, ac_c, ac_r, hprev_all, dy, a_row)


N_PAIR = SSM_HEADS // HP
PAIRS_PER_GROUP = SSM_HEADS_PER_GROUP // HP


def _pair_consts():
    L = SSM_CHUNK
    lane = lax.broadcasted_iota(jnp.int32, (L, LANES), 1)
    lane1 = lax.broadcasted_iota(jnp.int32, (1, LANES), 1)
    li = lax.broadcasted_iota(jnp.int32, (L, L), 0)
    si = lax.broadcasted_iota(jnp.int32, (L, L), 1)
    return lane >= ATT_HEAD_DIM, lane1 >= ATT_HEAD_DIM, li, si


def _ssd_pair_fwd(xbc_act, ac_c, dt_r, ac_r, dsk_pair):
    s = xbc_act.shape[0]
    L, N, G = SSM_CHUNK, SSM_STATE, SSM_GROUPS
    nc = s // L

    def body(xbc_ref, ac_ref, dtr_ref, acr_ref, dsk_ref, y_ref, hp_ref, st_ref):
        @pl.when(pl.program_id(0) == 0)
        def _():
            st_ref[...] = jnp.zeros_like(st_ref)

        upper, up1, li, si = _pair_consts()
        for g in range(G):
            b_g = xbc_ref[:, SSM_INNER + g * N:SSM_INNER + (g + 1) * N]
            c_g = xbc_ref[:, SSM_INNER + G * N + g * N:SSM_INNER + G * N + (g + 1) * N]
            cb = _dotT(c_g, b_g)
            b_t = b_g.T
            for q in range(PAIRS_PER_GROUP):
                pp = g * PAIRS_PER_GROUP + q
                cols = slice(pp * LANES, (pp + 1) * LANES)
                xs = xbc_ref[:, cols]
                ht = st_ref[pp]
                hp_ref[0, pp] = ht
                y = dsk_ref[pp:pp + 1, :] * xs
                s_new = jnp.zeros((N, LANES), F32)
                ea, el = [], []
                for a in range(HP):
                    h = HP * pp + a
                    acol = jnp.broadcast_to(ac_ref[:, h:h + 1], (L, LANES))
                    arow, dtrow = acr_ref[h:h + 1, :], dtr_ref[h:h + 1, :]
                    alast = ac_ref[L - 1:L, h:h + 1]
                    decay = jnp.exp(jnp.where(li >= si, acol - arow, NEG))
                    xs_a = jnp.where(upper == (a == 1), xs, 0.0)
                    y = y + _dot(cb * decay * dtrow, xs_a)
                    s_new = s_new + _dot(b_t * (dtrow * jnp.exp(alast - arow)), xs_a)
                    ea.append(jnp.exp(acol))
                    el.append(jnp.exp(alast))
                y_ref[:, cols] = y + jnp.where(upper, ea[1], ea[0]) * _dot(c_g, ht)
                st_ref[pp] = ht * jnp.where(up1, el[1], el[0]) + s_new

    col = pl.BlockSpec((L, LANES), lambda c: (c, 0))
    row = pl.BlockSpec((LANES, L), lambda c: (0, c))
    return pl.pallas_call(
        body, name="ssd_fwd", grid=(nc,),
        in_specs=[pl.BlockSpec((L, SSM_CONV_DIM), lambda c: (c, 0)), col, row, row,
                  pl.BlockSpec((N_PAIR, LANES), lambda c: (0, 0))],
        out_specs=[pl.BlockSpec((L, SSM_INNER), lambda c: (c, 0)),
                   pl.BlockSpec((1, N_PAIR, N, LANES), lambda c: (c, 0, 0, 0))],
        out_shape=[jax.ShapeDtypeStruct((s, SSM_INNER), F32), jax.ShapeDtypeStruct((nc, N_PAIR, N, LANES), F32)],
        scratch_shapes=[pltpu.VMEM((N_PAIR, N, LANES), F32)],
        compiler_params=_cp("arbitrary"),
    )(xbc_act, ac_c, dt_r, ac_r, dsk_pair)


def _ssd_pair_bwd(xbc_act, dt_c, ac_c, dt_r, ac_r, hprev_all, dy, dsk_pair, a_row):
    s = xbc_act.shape[0]
    L, N, G = SSM_CHUNK, SSM_STATE, SSM_GROUPS
    nc = s // L
    rev = lambda c: nc - 1 - c

    def body(xbc_ref, dt_ref, ac_ref, dtr_ref, acr_ref, hp_ref, dy_ref, dsk_ref, arow_ref,
             dx_ref, ddt_ref, da_ref, dds_ref, dh_ref):
        @pl.when(pl.program_id(0) == 0)
        def _():
            dh_ref[...] = jnp.zeros_like(dh_ref)
            da_ref[...] = jnp.zeros_like(da_ref)
            dds_ref[...] = jnp.zeros_like(dds_ref)

        upper, up1, li, si = _pair_consts()
        lane = lax.broadcasted_iota(jnp.int32, (L, LANES), 1)
        sub = lax.broadcasted_iota(jnp.int32, (LANES, L), 0)
        lastrow = lax.broadcasted_iota(jnp.int32, (L, LANES), 0) == L - 1
        da_c = jnp.zeros((L, LANES), F32)
        da_r = jnp.zeros((LANES, L), F32)
        ddt_r = jnp.zeros((LANES, L), F32)
        for g in range(G):
            b_g = xbc_ref[:, SSM_INNER + g * N:SSM_INNER + (g + 1) * N]
            c_g = xbc_ref[:, SSM_INNER + G * N + g * N:SSM_INNER + G * N + (g + 1) * N]
            cb, cb_t = _dotT(c_g, b_g), _dotT(b_g, c_g)
            b_t, c_t = b_g.T, c_g.T
            dcb = jnp.zeros((L, L), F32)
            db_t = jnp.zeros((N, L), F32)
            dc = jnp.zeros((L, N), F32)
            for q in range(PAIRS_PER_GROUP):
                pp = g * PAIRS_PER_GROUP + q
                cols = slice(pp * LANES, (pp + 1) * LANES)
                xs, gy = xbc_ref[:, cols], dy_ref[:, cols]
                ht, dhn = hp_ref[0, pp], dh_ref[pp]
                acol = [jnp.broadcast_to(ac_ref[:, HP * pp + a:HP * pp + a + 1], (L, LANES)) for a in range(HP)]
                alast = [ac_ref[L - 1:L, HP * pp + a:HP * pp + a + 1] for a in range(HP)]
                ea = jnp.where(upper, jnp.exp(acol[1]), jnp.exp(acol[0]))
                el = jnp.where(up1, jnp.exp(alast[1]), jnp.exp(alast[0]))
                ge = gy * ea
                dc = dc + _dotT(ge, ht)
                dh_ref[pp] = _dot(c_t, ge) + dhn * el
                t_off = (ge * _dot(c_g, ht)).astype(BF16)
                hsum = jnp.sum(dhn * ht, axis=0, keepdims=True)
                dxs = dsk_ref[pp:pp + 1, :] * gy
                dds_ref[pp:pp + 1, :] += jnp.sum(gy * xs, axis=0, keepdims=True)
                for a in range(HP):
                    h = HP * pp + a
                    mine, mine1 = upper == (a == 1), up1 == (a == 1)
                    arow, dtrow = acr_ref[h:h + 1, :], dtr_ref[h:h + 1, :]
                    dtcol = jnp.broadcast_to(dt_ref[:, h:h + 1], (L, LANES))
                    xs_a, gy_a = jnp.where(mine, xs, 0.0), jnp.where(mine, gy, 0.0)
                    dhn_a = jnp.where(mine1, dhn, 0.0)
                    e_row = jnp.exp(alast[a] - arow)
                    w_row = dtrow * e_row
                    xd_t = _dotT(dhn_a, xs_a)
                    db_t = db_t + xd_t * w_row
                    dw = jnp.sum(b_t * xd_t, axis=0, keepdims=True)
                    de_e = dw * w_row
                    dal = (jnp.sum(jnp.where(mine1, hsum, 0.0), axis=1, keepdims=True) * jnp.exp(alast[a])
                           + jnp.sum(de_e, axis=1, keepdims=True))
                    dxs = dxs + _dot(b_g, dhn_a) * (dtcol * jnp.exp(alast[a] - acol[a]))
                    decay = jnp.exp(jnp.where(li >= si, acol[a] - arow, NEG))
                    decay_t = jnp.exp(jnp.where(si >= li, arow - acol[a], NEG))
                    m = cb * decay
                    dmdt = _dotT(gy_a, xs_a)
                    dxs = dxs + _dot(cb_t * decay_t * dtcol, gy_a)
                    dm = dmdt * dtrow
                    dcb = dcb + dm * decay
                    wb = (dm * m).astype(BF16)
                    onehot = jnp.where(lane == h, 1.0, 0.0).astype(BF16)
                    da_c = (da_c + jnp.dot(wb, onehot, preferred_element_type=F32)
                            + jnp.dot(jnp.where(mine, t_off, 0.0).astype(BF16), onehot, preferred_element_type=F32)
                            + jnp.where(lastrow & (lane == h), dal, 0.0))
                    da_r = jnp.where(sub == h, -(jnp.sum(wb.astype(F32), axis=0, keepdims=True) + de_e), da_r)
                    ddt_r = jnp.where(sub == h, dw * e_row + jnp.sum(dmdt * m, axis=0, keepdims=True), ddt_r)
                dx_ref[:, cols] = dxs
            dx_ref[:, SSM_INNER + g * N:SSM_INNER + (g + 1) * N] = (db_t + _dot(c_t, dcb)).T
            dx_ref[:, SSM_INNER + G * N + g * N:SSM_INNER + G * N + (g + 1) * N] = dc + _dot(dcb, b_g)
        dda = jnp.dot(_tri(L, lower=False), da_c + da_r.T, precision=HI, preferred_element_type=F32)
        ddt_ref[...] = dda * arow_ref[...] + ddt_r.T
        da_ref[...] += jnp.sum(dda * dt_ref[...], axis=0, keepdims=True)

    col = pl.BlockSpec((L, LANES), lambda c: (rev(c), 0))
    row = pl.BlockSpec((LANES, L), lambda c: (0, rev(c)))
    vec = pl.BlockSpec((1, LANES), lambda c: (0, 0))
    pairs = pl.BlockSpec((N_PAIR, LANES), lambda c: (0, 0))
    return pl.pallas_call(
        body, name="ssd_bwd", grid=(nc,),
        in_specs=[pl.BlockSpec((L, SSM_CONV_DIM), lambda c: (rev(c), 0)), col, col, row, row,
                  pl.BlockSpec((1, N_PAIR, N, LANES), lambda c: (rev(c), 0, 0, 0)),
                  pl.BlockSpec((L, SSM_INNER), lambda c: (rev(c), 0)), pairs, vec],
        out_specs=[pl.BlockSpec((L, SSM_CONV_DIM), lambda c: (rev(c), 0)), col, vec, pairs],
        out_shape=[jax.ShapeDtypeStruct((s, SSM_CONV_DIM), F32), jax.ShapeDtypeStruct((s, LANES), F32),
                   jax.ShapeDtypeStruct((1, LANES), F32), jax.ShapeDtypeStruct((N_PAIR, LANES), F32)],
        scratch_shapes=[pltpu.VMEM((N_PAIR, N, LANES), F32)],
        compiler_params=_cp("arbitrary"),
    )(xbc_act, dt_c, ac_c, dt_r, ac_r, hprev_all, dy, dsk_pair, a_row)


ROWS = 256
GW = SSM_INNER // SSM_GROUPS


def _rows(width, dtype=F32):
    return pl.BlockSpec((ROWS, width), lambda i: (i, 0))


def _vec(width):
    return pl.BlockSpec((1, width), lambda i: (0, 0))


def _gnorm_fwd(y, z, w):
    s = y.shape[0]

    def body(y_ref, z_ref, w_ref, o_ref):
        for g in range(SSM_GROUPS):
            cs = slice(g * GW, (g + 1) * GW)
            zz = z_ref[:, cs].astype(F32)
            u = y_ref[:, cs] * (zz * _sigmoid(zz))
            r = lax.rsqrt(jnp.mean(u * u, axis=1, keepdims=True) + RMS_EPS)
            o_ref[:, cs] = (u * r * w_ref[:, cs]).astype(o_ref.dtype)

    return pl.pallas_call(body, name="gnorm_fwd", grid=(s // ROWS,),
                          in_specs=[_rows(SSM_INNER), _rows(SSM_INNER), _vec(SSM_INNER)], out_specs=_rows(SSM_INNER),
                          out_shape=jax.ShapeDtypeStruct((s, SSM_INNER), BF16), compiler_params=_cp("parallel"))(y, z, w)


def _gnorm_bwd(y, z, w, do):
    s = y.shape[0]

    def body(y_ref, z_ref, w_ref, do_ref, dy_ref, dz_ref, dw_ref):
        @pl.when(pl.program_id(0) == 0)
        def _():
            dw_ref[...] = jnp.zeros_like(dw_ref)

        for g in range(SSM_GROUPS):
            cs = slice(g * GW, (g + 1) * GW)
            zz, yy, dd = z_ref[:, cs].astype(F32), y_ref[:, cs], do_ref[:, cs].astype(F32)
            sz, dsz = _silu_and_grad(zz)
            u = yy * sz
            r = lax.rsqrt(jnp.mean(u * u, axis=1, keepdims=True) + RMS_EPS)
            n = u * r
            dn = dd * w_ref[:, cs]
            dw_ref[:, cs] += jnp.sum(dd * n, axis=0, keepdims=True)
            du = r * (dn - n * jnp.mean(dn * n, axis=1, keepdims=True))
            dy_ref[:, cs] = du * sz
            dz_ref[:, cs] = (du * yy * dsz).astype(dz_ref.dtype)

    return pl.pallas_call(
        body, name="gnorm_bwd", grid=(s // ROWS,),
        in_specs=[_rows(SSM_INNER), _rows(SSM_INNER), _vec(SSM_INNER), _rows(SSM_INNER)],
        out_specs=[_rows(SSM_INNER), _rows(SSM_INNER), _vec(SSM_INNER)],
        out_shape=[jax.ShapeDtypeStruct((s, SSM_INNER), F32), jax.ShapeDtypeStruct((s, SSM_INNER), BF16),
                   jax.ShapeDtypeStruct((1, SSM_INNER), F32)],
        compiler_params=_cp("arbitrary"))(y, z, w, do)


def _mix_fwd(gl, bg, attn_d, ssm_d):
    s = gl.shape[0]
    d = D_MODEL

    def body(gl_ref, bg_ref, a_ref, m_ref, o_ref):
        g0 = _sigmoid(gl_ref[:, :d] + bg_ref[:, :d])
        g1 = _sigmoid(gl_ref[:, d:] + bg_ref[:, d:])
        o_ref[...] = (g0 * a_ref[...] + g1 * m_ref[...]).astype(o_ref.dtype)

    return pl.pallas_call(body, name="mix_fwd", grid=(s // ROWS,),
                          in_specs=[_rows(2 * d), _vec(2 * d), _rows(d), _rows(d)], out_specs=_rows(d),
                          out_shape=jax.ShapeDtypeStruct((s, d), BF16), compiler_params=_cp("parallel"))(
        gl, bg, attn_d, ssm_d)


def _mix_bwd(gl, bg, attn_d, ssm_d, dmix):
    s = gl.shape[0]
    d = D_MODEL

    def body(gl_ref, bg_ref, a_ref, m_ref, dm_ref, da_ref, ds_ref, dg_ref, db_ref):
        @pl.when(pl.program_id(0) == 0)
        def _():
            db_ref[...] = jnp.zeros_like(db_ref)

        g0 = _sigmoid(gl_ref[:, :d] + bg_ref[:, :d])
        g1 = _sigmoid(gl_ref[:, d:] + bg_ref[:, d:])
        dm = dm_ref[...].astype(F32)
        da_ref[...] = (dm * g0).astype(da_ref.dtype)
        ds_ref[...] = (dm * g1).astype(ds_ref.dtype)
        dl0 = dm * a_ref[...] * g0 * (1.0 - g0)
        dl1 = dm * m_ref[...] * g1 * (1.0 - g1)
        dg_ref[:, :d] = dl0.astype(dg_ref.dtype)
        dg_ref[:, d:] = dl1.astype(dg_ref.dtype)
        db_ref[:, :d] += jnp.sum(dl0, axis=0, keepdims=True)
        db_ref[:, d:] += jnp.sum(dl1, axis=0, keepdims=True)

    return pl.pallas_call(
        body, name="mix_bwd", grid=(s // ROWS,),
        in_specs=[_rows(2 * d), _vec(2 * d), _rows(d), _rows(d), _rows(d)],
        out_specs=[_rows(d), _rows(d), _rows(2 * d), _vec(2 * d)],
        out_shape=[jax.ShapeDtypeStruct((s, d), BF16), jax.ShapeDtypeStruct((s, d), BF16),
                   jax.ShapeDtypeStruct((s, 2 * d), BF16), jax.ShapeDtypeStruct((1, 2 * d), F32)],
        compiler_params=_cp("arbitrary"))(gl, bg, attn_d, ssm_d, dmix)


def _ln_stats(p):
    mu = jnp.mean(p, axis=1, keepdims=True)
    c = p - mu
    rstd = lax.rsqrt(jnp.mean(c * c, axis=1, keepdims=True) + LN_EPS)
    return c * rstd, rstd


def _ln_bwd(dy, xhat, rstd, g):
    dxh = dy * g
    return rstd * (dxh - jnp.mean(dxh, axis=1, keepdims=True) - xhat * jnp.mean(dxh * xhat, axis=1, keepdims=True))


def _ln1_fwd(x, mixed, g, b):
    s, d = x.shape

    def body(x_ref, m_ref, g_ref, b_ref, o_ref):
        xhat, _ = _ln_stats(DEEPNORM_ALPHA * x_ref[...] + m_ref[...])
        o_ref[...] = xhat * g_ref[...] + b_ref[...]

    return pl.pallas_call(body, name="ln1_fwd", grid=(s // ROWS,), in_specs=[_rows(d), _rows(d), _vec(d), _vec(d)],
                          out_specs=_rows(d), out_shape=jax.ShapeDtypeStruct((s, d), F32),
                          compiler_params=_cp("parallel"))(x, mixed, g, b)


def _ln2_loss(x1, h, target, g, b):
    s, d = x1.shape

    def body(x_ref, h_ref, t_ref, g_ref, b_ref, dp_ref, loss_ref, dg_ref, db_ref):
        @pl.when(pl.program_id(0) == 0)
        def _():
            loss_ref[...] = jnp.zeros_like(loss_ref)
            dg_ref[...] = jnp.zeros_like(dg_ref)
            db_ref[...] = jnp.zeros_like(db_ref)

        xhat, rstd = _ln_stats(DEEPNORM_ALPHA * x_ref[...] + h_ref[...])
        err = xhat * g_ref[...] + b_ref[...] - t_ref[...]
        part = 0.5 * jnp.sum(jnp.mean(err * err, axis=1, keepdims=True), axis=0, keepdims=True)
        loss_ref[...] += jnp.broadcast_to(part, loss_ref.shape)
        dy = err * (1.0 / d)
        dg_ref[...] += jnp.sum(dy * xhat, axis=0, keepdims=True)
        db_ref[...] += jnp.sum(dy, axis=0, keepdims=True)
        dp_ref[...] = _ln_bwd(dy, xhat, rstd, g_ref[...])

    return pl.pallas_call(
        body, name="ln2_loss", grid=(s // ROWS,), in_specs=[_rows(d), _rows(d), _rows(d), _vec(d), _vec(d)],
        out_specs=[_rows(d), _vec(LANES), _vec(d), _vec(d)],
        out_shape=[jax.ShapeDtypeStruct((s, d), F32), jax.ShapeDtypeStruct((1, LANES), F32),
                   jax.ShapeDtypeStruct((1, d), F32), jax.ShapeDtypeStruct((1, d), F32)],
        compiler_params=_cp("arbitrary"))(x1, h, target, g, b)


def _ln1_bwd(x, mixed, g, dpre2, dffn):
    s, d = x.shape

    def body(x_ref, m_ref, g_ref, d2_ref, df_ref, dp_ref, dr_ref, dg_ref, db_ref):
        @pl.when(pl.program_id(0) == 0)
        def _():
            dg_ref[...] = jnp.zeros_like(dg_ref)
            db_ref[...] = jnp.zeros_like(db_ref)

        xhat, rstd = _ln_stats(DEEPNORM_ALPHA * x_ref[...] + m_ref[...])
        dy = DEEPNORM_ALPHA * d2_ref[...] + df_ref[...]
        dg_ref[...] += jnp.sum(dy * xhat, axis=0, keepdims=True)
        db_ref[...] += jnp.sum(dy, axis=0, keepdims=True)
        dp = _ln_bwd(dy, xhat, rstd, g_ref[...])
        dp_ref[...] = dp
        dr_ref[...] = DEEPNORM_ALPHA * dp

    return pl.pallas_call(
        body, name="ln1_bwd", grid=(s // ROWS,), in_specs=[_rows(d), _rows(d), _vec(d), _rows(d), _rows(d)],
        out_specs=[_rows(d), _rows(d), _vec(d), _vec(d)],
        out_shape=[jax.ShapeDtypeStruct((s, d), F32), jax.ShapeDtypeStruct((s, d), F32),
                   jax.ShapeDtypeStruct((1, d), F32), jax.ShapeDtypeStruct((1, d), F32)],
        compiler_params=_cp("arbitrary"))(x, mixed, g, dpre2, dffn)


def _swiglu_fwd(gu):
    s = gu.shape[0]
    f = FFN_HIDDEN

    def body(g_ref, u_ref, o_ref):
        gg = g_ref[...].astype(F32)
        o_ref[...] = (gg * _sigmoid(gg) * u_ref[...].astype(F32)).astype(o_ref.dtype)

    return pl.pallas_call(
        body, name="swiglu_fwd", grid=(s // ROWS,),
        in_specs=[pl.BlockSpec((ROWS, f), lambda i: (i, 0)), pl.BlockSpec((ROWS, f), lambda i: (i, 1))],
        out_specs=_rows(f), out_shape=jax.ShapeDtypeStruct((s, f), BF16), compiler_params=_cp("parallel"))(gu, gu)


def _swiglu_bwd(gu, dact):
    s = gu.shape[0]
    f = FFN_HIDDEN

    def body(g_ref, u_ref, d_ref, o_ref):
        sg, dsg = _silu_and_grad(g_ref[...].astype(F32))
        dd = d_ref[...].astype(F32)
        o_ref[:, :f] = (dd * u_ref[...].astype(F32) * dsg).astype(o_ref.dtype)
        o_ref[:, f:] = (dd * sg).astype(o_ref.dtype)

    return pl.pallas_call(
        body, name="swiglu_bwd", grid=(s // ROWS,),
        in_specs=[pl.BlockSpec((ROWS, f), lambda i: (i, 0)), pl.BlockSpec((ROWS, f), lambda i: (i, 1)), _rows(f)],
        out_specs=_rows(2 * f), out_shape=jax.ShapeDtypeStruct((s, 2 * f), BF16),
        compiler_params=_cp("parallel"))(gu, gu, dact)


def _peer(k):
    x, y, c = lax.axis_index("x"), lax.axis_index("y"), lax.axis_index("c")
    kx, ky, kc = (k >> 2) & 1, (k >> 1) & 1, k & 1
    px = (1 - x) if kx else x
    py = (1 - y) if ky else y
    pc = (1 - c) if kc else c
    return (px, py, pc), 4 * px + 2 * py + pc


def _my_index():
    return 4 * lax.axis_index("x") + 2 * lax.axis_index("y") + lax.axis_index("c")


def _comm_copies(ins, outs, sems, scatter):
    send_sems, recv_sems, local_sems = sems
    me = _my_index()
    copies = [pltpu.make_async_copy(ins[t].at[me] if scatter else ins[t], outs[t].at[me], local_sems.at[t])
              for t in range(len(ins))]
    for k in range(1, N_DEV):
        peer, pidx = _peer(k)
        for t in range(len(ins)):
            copies.append(pltpu.make_async_remote_copy(
                src_ref=ins[t].at[pidx] if scatter else ins[t], dst_ref=outs[t].at[me],
                send_sem=send_sems.at[t, k - 1], recv_sem=recv_sems.at[t, k - 1], device_id=peer,
                device_id_type=pl.DeviceIdType.MESH))
    return copies


def _comm_sems(n):
    return [pltpu.SemaphoreType.DMA((n, N_DEV - 1)), pltpu.SemaphoreType.DMA((n, N_DEV - 1)),
            pltpu.SemaphoreType.DMA((n,))]


def _comm_out_shapes(parts, scatter):
    return [jax.ShapeDtypeStruct(p.shape if scatter else (N_DEV,) + p.shape, p.dtype) for p in parts]


def _comm_call(parts, scatter, name):
    n = len(parts)

    def body(*refs):
        copies = _comm_copies(refs[:n], refs[n:2 * n], refs[2 * n:], scatter)
        for cp in copies:
            cp.start()
        for cp in copies:
            cp.wait()

    anyspec = pl.BlockSpec(memory_space=pl.ANY)
    return pl.pallas_call(body, name=name, in_specs=[anyspec] * n, out_specs=[anyspec] * n,
                          out_shape=_comm_out_shapes(parts, scatter), scratch_shapes=_comm_sems(n))(*parts)


def _all_gather(parts):
    return _comm_call(parts, False, "all_gather")


def _exchange(parts):
    return _comm_call(parts, True, "grad_exchange")


def _remote_scatter_copies(ins, lands, send_sems, recv_sems):
    me = _my_index()
    copies = []
    for k in range(1, N_DEV):
        peer, pidx = _peer(k)
        for t in range(len(ins)):
            copies.append(pltpu.make_async_remote_copy(
                src_ref=ins[t].at[pidx], dst_ref=lands[t].at[me], send_sem=send_sems.at[t * (N_DEV - 1) + k - 1],
                recv_sem=recv_sems.at[t * (N_DEV - 1) + k - 1], device_id=peer, device_id_type=pl.DeviceIdType.MESH))
    return copies


def _place_own(parts):
    n = len(parts)

    def body(*refs):
        me = _my_index()
        copies = [pltpu.make_async_copy(refs[t].at[me], refs[n + t].at[me], refs[2 * n].at[t]) for t in range(n)]
        for cp in copies:
            cp.start()
        for cp in copies:
            cp.wait()

    anyspec = pl.BlockSpec(memory_space=pl.ANY)
    return pl.pallas_call(body, name="place_own", in_specs=[anyspec] * n, out_specs=[anyspec] * n,
                          out_shape=_comm_out_shapes(parts, True),
                          scratch_shapes=[pltpu.SemaphoreType.DMA((n,))])(*parts)


_HBM = pl.BlockSpec(memory_space=pltpu.HBM)
_SEM = pl.BlockSpec(memory_space=pltpu.SEMAPHORE)


def _exchange_start(parts, lands):
    n = len(parts)

    def body(*refs):
        ins, lnd, send_sems, recv_sems, token = refs[:n], refs[n:2 * n], refs[2 * n], refs[2 * n + 1], refs[-1]
        for cp in _remote_scatter_copies(ins, lnd, send_sems, recv_sems):
            cp.start()
        token[...] = jnp.zeros_like(token)

    hbm = [pltpu.HBM(p.shape, p.dtype) for p in parts]
    outs = pl.pallas_call(
        body, name="exchange_start",
        out_shape=[pltpu.SemaphoreType.DMA((n * (N_DEV - 1),)), pltpu.SemaphoreType.DMA((n * (N_DEV - 1),))] + hbm + hbm
        + [jax.ShapeDtypeStruct((8, LANES), F32)],
        in_specs=[_HBM] * (2 * n), out_specs=[_SEM, _SEM] + [_HBM] * (2 * n) + [pl.BlockSpec(memory_space=pltpu.VMEM)],
        input_output_aliases={t: 2 + t for t in range(2 * n)},
        compiler_params=pltpu.CompilerParams(has_side_effects=pltpu.SideEffectType.DATAFLOW_SIDE_EFFECTING),
    )(*[pltpu.with_memory_space_constraint(p, pltpu.HBM) for p in list(parts) + list(lands)])
    return outs[0], outs[1], list(outs[2:2 + n]), list(outs[2 + n:2 + 2 * n]), outs[-1]


def _exchange_wait(send_sems, recv_sems, parts, lands, after):
    n = len(parts)

    def body(*refs):
        ins, lnd, send_sems, recv_sems = refs[:n], refs[n:2 * n], refs[2 * n], refs[2 * n + 1]
        for cp in _remote_scatter_copies(ins, lnd, send_sems, recv_sems):
            cp.wait_send()
            cp.wait_recv()

    hbm = [pltpu.HBM(p.shape, p.dtype) for p in parts]
    outs = pl.pallas_call(
        body, name="exchange_wait", out_shape=hbm + hbm,
        in_specs=[_HBM] * (2 * n) + [_SEM, _SEM, pl.BlockSpec(memory_space=pl.ANY)], out_specs=[_HBM] * (2 * n),
        input_output_aliases={t: t for t in range(2 * n)},
        compiler_params=pltpu.CompilerParams(has_side_effects=pltpu.SideEffectType.DATAFLOW_SIDE_EFFECTING),
    )(*parts, *lands, send_sems, recv_sems, after)
    return list(outs[n:])


def _adamw(recv, w, m, v, name):
    _, r, c = w.shape
    br = _tile(r, 128)
    c1 = 1.0 / (1.0 - ADAM_B1 ** ADAM_STEP)
    c2 = 1.0 / (1.0 - ADAM_B2 ** ADAM_STEP)

    def body(r_ref, w_ref, m_ref, v_ref, g_ref, d_ref, mo_ref, vo_ref):
        g = r_ref[0].astype(F32)
        for k in range(1, N_DEV):
            g = g + r_ref[k].astype(F32)
        mn = ADAM_B1 * m_ref[0] + (1.0 - ADAM_B1) * g
        vn = ADAM_B2 * v_ref[0] + (1.0 - ADAM_B2) * (g * g)
        g_ref[0] = g
        mo_ref[0] = mn
        vo_ref[0] = vn
        d_ref[0] = -ADAM_LR * ((mn * c1) / (jnp.sqrt(vn * c2) + ADAM_EPS) + ADAM_WD * w_ref[0])

    blk = pl.BlockSpec((1, br, c), lambda i: (0, i, 0))
    return pl.pallas_call(
        body, name=name, grid=(r // br,),
        in_specs=[pl.BlockSpec((N_DEV, br, c), lambda i: (0, i, 0)), blk, blk, blk],
        out_specs=[blk] * 4, out_shape=[jax.ShapeDtypeStruct((1, r, c), F32)] * 4,
        compiler_params=_cp("parallel"))(recv, w, m, v)


def _lane_row(pairs):
    row = jnp.zeros((LANES,), F32)
    for lane0, vec in pairs:
        row = lax.dynamic_update_slice(row, vec.astype(F32), (lane0,))
    return row.reshape(1, LANES)


def _stage_in(x, wts, small):
    s = x.shape[0]
    a = -jnp.exp(small["a_log"])
    bias_row = _lane_row([(DT_LANE0, small["dt_bias"]), (F_LANE0, small["b_forget"])])
    a_row = _lane_row([(DT_LANE0, a)])
    conv_b = small["conv_b"].reshape(1, -1)
    norm_w = small["ssm_norm_w"].reshape(1, -1)
    bg = small["b_gates"].reshape(1, -1)
    g1, b1 = small["ln1_g"].reshape(1, -1), small["ln1_b"].reshape(1, -1)
    g2, b2 = small["ln2_g"].reshape(1, -1), small["ln2_b"].reshape(1, -1)
    d_skip = small["d_skip"]
    xb = x.astype(BF16)

    qkv = _mm(xb, wts["qkv"], out_dtype=BF16, name="f_qkv")
    z = _mm(xb, wts["z"], out_dtype=BF16, name="f_z")
    xbc = _mm(xb, wts["xbc"], name="f_xbc")
    gl = _mm(xb, wts["gate"], out_dtype=BF16, name="f_gate")
    fd = _mm(xb, wts["fd"], name="f_fd")
    dt_c, ac_c, cf_c, dt_r, ac_r, cf_r = _stats_fwd(fd, bias_row, a_row)
    bk = _att_blocks(s)[1]
    ck4 = cf_r[F_LANE0:F_LANE0 + ATT_HEADS].reshape(N_HP, HP, s // bk, bk)
    return dict(locals())


def _stage_mid(c, attn, lse, wts, target):
    x, xb, qkv, z, xbc, gl, fd, ck4 = (c[k] for k in ("x", "xb", "qkv", "z", "xbc", "gl", "fd", "ck4"))
    dt_c, ac_c, dt_r, ac_r, a_row, bias_row = (c[k] for k in ("dt_c", "ac_c", "dt_r", "ac_r", "a_row", "bias_row"))
    conv_b, norm_w, bg, g1, b1, g2, b2, d_skip = (c[k] for k in ("conv_b", "norm_w", "bg", "g1", "b1", "g2", "b2",
                                                                "d_skip"))
    conv_w = c["wts"]["conv"]
    attn_d = _mm(attn, wts["pa"], out_dtype=BF16, name="f_pa")
    xact = _conv_fwd(xbc, conv_w, conv_b)
    dsk_pair = jnp.repeat(d_skip, SSM_HEAD_DIM).reshape(N_PAIR, LANES)
    y, hprev = _ssd_pair_fwd(xact, ac_c, dt_r, ac_r, dsk_pair)
    ssm = _gnorm_fwd(y, z, norm_w)
    ssm_d = _mm(ssm, wts["ps"], out_dtype=BF16, name="f_ps")
    mix = _mix_fwd(gl, bg, attn_d, ssm_d)
    mixed = _mm(mix, wts["out"], name="f_out")
    x1 = _ln1_fwd(x, mixed, g1, b1)
    gu = _mm(x1, wts["gu"], out_dtype=BF16, name="f_gu")
    act = _swiglu_fwd(gu)
    h = _mm(act, wts["down"], name="f_down")
    dpre2, loss_row, dg2, db2 = _ln2_loss(x1, h, target, g2, b2)

    d_act = _mm(dpre2, wts["down"], tb=True, out_dtype=BF16, name="b_down_x")
    dw_down = _mm(act, dpre2, ta=True, name="b_down_w")
    dgu = _swiglu_bwd(gu, d_act)
    dffn = _mm(dgu, wts["gu"], tb=True, name="b_gu_x")
    dw_gu = _mm(x1, dgu, ta=True, name="b_gu_w")
    dpre1, dxr, dg1, db1 = _ln1_bwd(x, mixed, g1, dpre2, dffn)
    dmix = _mm(dpre1, wts["out"], tb=True, out_dtype=BF16, name="b_out_x")
    dw_out = _mm(mix, dpre1, ta=True, name="b_out_w")
    dattn_d, dssm_d, dgl, dbg = _mix_bwd(gl, bg, attn_d, ssm_d, dmix)
    dssm = _mm(dssm_d, wts["ps"], tb=True, out_dtype=BF16, name="b_ps_x")
    dw_ps = _mm(ssm, dssm_d, ta=True, name="b_ps_w")
    dattn = _mm(dattn_d, wts["pa"], tb=True, name="b_pa_x")
    dw_pa = _mm(attn, dattn_d, ta=True, name="b_pa_w")
    dy, dz, dnw = _gnorm_bwd(y, z, norm_w, dssm)
    dxact, ddt, da_row, dds_pair = _ssd_pair_bwd(xact, dt_c, ac_c, dt_r, ac_r, hprev, dy, dsk_pair, a_row)
    dds = dds_pair.reshape(SSM_HEADS, SSM_HEAD_DIM).sum(axis=1)
    dpre_c, dconv_w, dconv_b = _conv_bwd_pre(xbc, conv_w, conv_b, dxact)
    dxbc = _conv_bwd_in(dpre_c, conv_w)
    st, do_b = _att_prep(dattn, attn, lse)
    late = dict(pa=dw_pa, ps=dw_ps, out=dw_out, gu=dw_gu, down=dw_down)
    keep = ("st", "do_b", "ddt", "dxr", "dz", "dxbc", "dgl", "dconv_w", "dconv_b", "da_row", "dds", "dnw", "dbg",
            "dg1", "db1", "dg2", "db2", "loss_row")
    loc = locals()
    return {**c, **{k: loc[k] for k in keep}}, late


def _stage_out_w(c, att_grads):
    dq, dk, dv, dck, dcq = att_grads
    xb, fd, bias_row, ddt, dz, dxbc, dgl = (c[k] for k in ("xb", "fd", "bias_row", "ddt", "dz", "dxbc", "dgl"))
    s, a = xb.shape[0], c["a"]
    dck_rows = jnp.zeros((LANES, s), F32).at[F_LANE0:F_LANE0 + ATT_HEADS].set(dck.reshape(ATT_HEADS, s))
    dcq_cols = jnp.zeros((s, LANES), F32).at[:, F_LANE0:F_LANE0 + ATT_HEADS].set(dcq[:, ::ATT_HEAD_DIM])
    dfd, dbias = _stats_bwd(fd, bias_row, ddt, dck_rows, dcq_cols)
    dproj = (dq, dk, dv, dz, dxbc, dgl, dfd)
    dw_in = [_mm(xb, g_, ta=True, tb=(i == 0), name=f"b_in_w{i}") for i, g_ in enumerate(dproj)]
    grads = dict(q=dw_in[0], k=dw_in[1], v=dw_in[2], z=dw_in[3], xbc=dw_in[4], gate=dw_in[5], fd=dw_in[6],
                 conv=c["dconv_w"])
    small_g = dict(
        b_forget=dbias[0, F_LANE0:F_LANE0 + ATT_HEADS], conv_b=c["dconv_b"][0], dt_bias=dbias[0, :SSM_HEADS],
        a_log=c["da_row"][0, :SSM_HEADS] * a, d_skip=c["dds"], ssm_norm_w=c["dnw"][0], b_gates=c["dbg"][0],
        ln1_g=c["dg1"][0], ln1_b=c["db1"][0], ln2_g=c["dg2"][0], ln2_b=c["db2"][0])
    return c["loss_row"][0, 0], grads, small_g, dproj


def _stage_out_x(c, dproj, token):
    wts, d = c["wts"], D_MODEL
    wq = wts["qkv"][:, :d] + token.astype(BF16)
    wk, wv = wts["qkv"][:, d:2 * d], wts["qkv"][:, 2 * d:]
    dx = c["dxr"]
    for i, (g_, w_) in enumerate(zip(dproj, (wq, wk, wv, wts["z"], wts["xbc"], wts["gate"], wts["fd"]))):
        dx = _mm(g_, w_, ta=(i == 0), tb=True, add=dx, name=f"b_in_x{i}")
    return dx


BIG = ("w_in", "w_proj_attn", "w_proj_ssm", "w_out", "w_ffn_gate", "w_ffn_up", "w_ffn_down", "conv_w")
EARLY = ("w_in", "conv_w")
LATE = ("w_proj_attn", "w_proj_ssm", "w_out", "w_ffn_gate", "w_ffn_up", "w_ffn_down")
SMALL = ("b_forget", "conv_b", "dt_bias", "a_log", "d_skip", "ssm_norm_w", "b_gates", "ln1_g", "ln1_b", "ln2_g",
         "ln2_b")
SMALL_ROWS = 96
IN_SHARD = IN_WIDTH // N_DEV
IN_SEGMENTS = (("q", 0, 1024), ("k", 1024, 1024), ("v", 2048, 1024), ("f", 3072, ATT_HEADS), ("z", 3088, SSM_INNER),
               ("xbc", 5136, SSM_CONV_DIM), ("dt", 8208, SSM_HEADS), ("gate", 8240, 2 * D_MODEL))


def _cols_from_shards(shards, lo, hi):
    w = shards[0].shape[1]
    pieces = []
    for j in range(len(shards)):
        a, b = max(lo, j * w), min(hi, (j + 1) * w)
        if a < b:
            pieces.append(shards[j][:, a - j * w:b - j * w])
    return pieces[0] if len(pieces) == 1 else jnp.concatenate(pieces, axis=1)


def _shards_from_parts(parts, width):
    shards = []
    for j in range(N_DEV):
        lo, hi = j * width, (j + 1) * width
        pieces = []
        for mat, c0 in parts:
            a, b = max(lo, c0), min(hi, c0 + mat.shape[1])
            if a < b:
                pieces.append(mat[:, a - c0:b - c0])
        shards.append(pieces[0] if len(pieces) == 1 else jnp.concatenate(pieces, axis=1))
    return shards


def _pack_small(vals):
    flat = jnp.concatenate([vals[n].reshape(-1) for n in SMALL])
    return jnp.pad(flat, (0, SMALL_ROWS * LANES - flat.shape[0])).reshape(SMALL_ROWS, LANES)


def _unpack_small(pack, shapes):
    flat = pack.reshape(-1)
    out, off = {}, 0
    for n in SMALL:
        sz = math.prod(shapes[n])
        out[n] = flat[off:off + sz].reshape(shapes[n])
        off += sz
    return out


def kernel(x, w_in, b_forget, conv_w, conv_b, dt_bias, a_log, d_skip, ssm_norm_w, w_proj_attn, w_proj_ssm, b_gates, w_out, ln1_g, ln1_b, w_ffn_gate, w_ffn_up, w_ffn_down, ln2_g, ln2_b, loss_target, m_w_in, m_b_forget, m_conv_w, m_conv_b, m_dt_bias, m_a_log, m_d_skip, m_ssm_norm_w, m_w_proj_attn, m_w_proj_ssm, m_b_gates, m_w_out, m_ln1_g, m_ln1_b, m_w_ffn_gate, m_w_ffn_up, m_w_ffn_down, m_ln2_g, m_ln2_b, v_w_in, v_b_forget, v_conv_w, v_conv_b, v_dt_bias, v_a_log, v_d_skip, v_ssm_norm_w, v_w_proj_attn, v_w_proj_ssm, v_b_gates, v_w_out, v_ln1_g, v_ln1_b, v_w_ffn_gate, v_w_ffn_up, v_w_ffn_down, v_ln2_g, v_ln2_b):
    args = dict(locals())
    d, f = D_MODEL, FFN_HIDDEN
    big_w = {n: args[n][0] for n in BIG}
    small_w = {n: args[n][0] for n in SMALL}
    big_shapes = {n: args[n].shape for n in BIG}
    small_shapes = {n: args[n].shape for n in SMALL}

    early = dict(zip(EARLY, _all_gather([big_w["w_in"].astype(BF16), big_w["conv_w"]])))
    in_shards = [early["w_in"][j] for j in range(N_DEV)]
    seg = {n: _cols_from_shards(in_shards, c0, c0 + w) for n, c0, w in IN_SEGMENTS}
    wfd = jnp.concatenate([seg["dt"], seg["f"], jnp.zeros((d, LANES - SSM_HEADS - ATT_HEADS), BF16)], axis=1)
    wts = dict(qkv=jnp.concatenate([seg["q"], seg["k"], seg["v"]], axis=1), z=seg["z"], xbc=seg["xbc"],
               gate=seg["gate"], fd=wfd, conv=jnp.concatenate([early["conv_w"][j] for j in range(N_DEV)], axis=1))

    ctx = _stage_in(x[0], wts, small_w)
    attn, lse, gathered = _attention_fwd(ctx["qkv"], ctx["ck4"], [big_w[n].astype(BF16) for n in LATE])
    full = dict(zip(LATE, gathered))
    late_w = dict(
        pa=full["w_proj_attn"].reshape(d, d), ps=full["w_proj_ssm"].reshape(SSM_INNER, d),
        out=full["w_out"].reshape(d, d),
        gu=jnp.concatenate([full["w_ffn_gate"][j] for j in range(N_DEV)]
                           + [full["w_ffn_up"][j] for j in range(N_DEV)], axis=1),
        down=full["w_ffn_down"].reshape(f, d))
    ctx, gl = _stage_mid(ctx, attn, lse, late_w, loss_target[0])
    late_dest = dict(
        w_ffn_gate=jnp.stack([s_.astype(BF16) for s_ in _shards_from_parts([(gl["gu"][:, :f], 0)], f // N_DEV)]),
        w_ffn_up=jnp.stack([s_.astype(BF16) for s_ in _shards_from_parts([(gl["gu"][:, f:], 0)], f // N_DEV)]))
    for n, key in (("w_proj_attn", "pa"), ("w_proj_ssm", "ps"), ("w_out", "out"), ("w_ffn_down", "down")):
        late_dest[n] = gl[key].astype(BF16).reshape((N_DEV,) + big_shapes[n][1:])
    att_grads, late_recv = _attention_bwd(ctx["qkv"], ctx["ck4"], ctx["st"], ctx["do_b"], [late_dest[n] for n in LATE])
    loss_part, g, small_g, dproj = _stage_out_w(ctx, att_grads)
    loss = lax.psum(loss_part, ("x", "y", "c"))

    gfd = g["fd"]
    in_parts = dict(q=g["q"], k=g["k"], v=g["v"], f=gfd[:, F_LANE0:F_LANE0 + ATT_HEADS], z=g["z"], xbc=g["xbc"],
                    dt=gfd[:, DT_LANE0:DT_LANE0 + SSM_HEADS], gate=g["gate"])
    win_dest = jnp.stack([s_.astype(BF16) for s_ in
                          _shards_from_parts([(in_parts[n], c0) for n, c0, _ in IN_SEGMENTS], IN_SHARD)])
    conv_dest = jnp.stack(_shards_from_parts([(g["conv"], 0)], SSM_CONV_DIM // N_DEV))
    small_pack = _pack_small(small_g)
    last_parts = [win_dest, conv_dest, jnp.broadcast_to(small_pack, (N_DEV,) + small_pack.shape)]
    send_sems, recv_sems, parts_thru, lands_thru, token = _exchange_start(last_parts, _place_own(last_parts))
    grad_x = _stage_out_x(ctx, dproj, token[0, 0])
    early_recv = _exchange_wait(send_sems, recv_sems, parts_thru, lands_thru, grad_x)
    recv = dict(zip(LATE, late_recv))
    recv["w_in"], recv["conv_w"] = early_recv[0], early_recv[1]

    outs = {}
    for n in BIG:
        outs[n] = _adamw(recv[n], args[n], args["m_" + n], args["v_" + n], name="adamw_" + n)
    small4 = _adamw(early_recv[2], _pack_small(small_w)[None], _pack_small({n: args["m_" + n][0] for n in SMALL})[None],
                    _pack_small({n: args["v_" + n][0] for n in SMALL})[None], name="adamw_small")
    small_out = [_unpack_small(p, small_shapes) for p in small4]
    for n in SMALL:
        outs[n] = [so[n] for so in small_out]

    order = ("w_in", "b_forget", "conv_w", "conv_b", "dt_bias", "a_log", "d_skip", "ssm_norm_w", "w_proj_attn",
             "w_proj_ssm", "b_gates", "w_out", "ln1_g", "ln1_b", "w_ffn_gate", "w_ffn_up", "w_ffn_down", "ln2_g",
             "ln2_b")
    res = [loss, grad_x[None]]
    for i in range(4):
        res += [outs[n][i] for n in order]
    return tuple(res)
```

```python
import functools
import math

import jax
import jax.numpy as jnp
from jax import lax
from jax.experimental import pallas as pl
from jax.experimental.pallas import tpu as pltpu

F32 = jnp.float32
BF16 = jnp.bfloat16

N_DEV = 8
D_MODEL = 1024
ATT_HEADS = 16
ATT_HEAD_DIM = 64
SSM_INNER = 2048
SSM_HEADS = 32
SSM_HEAD_DIM = 64
SSM_GROUPS = 4
SSM_HEADS_PER_GROUP = 8
SSM_STATE = 128
SSM_CONV = 4
SSM_CHUNK = 128
SSM_CONV_DIM = 3072
FFN_HIDDEN = 2816
IN_WIDTH = 10288
DEEPNORM_ALPHA = 2.0 ** 0.25
LN_EPS = 1e-5
RMS_EPS = 1e-5
ADAM_LR, ADAM_B1, ADAM_B2, ADAM_EPS, ADAM_WD, ADAM_STEP = 0.001, 0.9, 0.999, 1e-08, 0.01, 10
ATT_SCALE = 1.0 / math.sqrt(ATT_HEAD_DIM)

LANES = 128
VMEM_LIMIT = 56 * 1024 * 1024
NEG = -1e30

DT_LANE0 = 0
F_LANE0 = 32
HI = lax.Precision.HIGHEST


def _cp(*sem):
    return pltpu.CompilerParams(dimension_semantics=sem, vmem_limit_bytes=VMEM_LIMIT)


def _tile(n, cap=1408):
    for t in (1408, 1024, 512, 384, 256, 128):
        if t <= cap and n % t == 0:
            return t
    return n


def _sigmoid(x):
    return 1.0 / (1.0 + jnp.exp(-x))


def _mm(a, b, *, ta=False, tb=False, out_dtype=F32, add=None, name):
    m, k = (a.shape[1], a.shape[0]) if ta else a.shape
    n = b.shape[0] if tb else b.shape[1]
    assert (b.shape[1] if tb else b.shape[0]) == k
    tm, tn, tk = _tile(m), _tile(n), _tile(k)
    nk = k // tk
    dims = (((0,) if ta else (1,), (1,) if tb else (0,)), ((), ()))

    def body_single(*refs):
        a_ref, b_ref = refs[:2]
        r = lax.dot_general(a_ref[...].astype(BF16), b_ref[...].astype(BF16), dims, preferred_element_type=F32)
        if add is not None:
            r = r + refs[2][...]
        refs[-1][...] = r.astype(refs[-1].dtype)

    def body(*refs):
        if add is None:
            a_ref, b_ref, o_ref, acc_ref = refs
        else:
            a_ref, b_ref, c_ref, o_ref, acc_ref = refs
        kk = pl.program_id(2)

        @pl.when(kk == 0)
        def _():
            acc_ref[...] = jnp.zeros_like(acc_ref)

        acc_ref[...] += lax.dot_general(a_ref[...].astype(BF16), b_ref[...].astype(BF16), dims,
                                        preferred_element_type=F32)

        @pl.when(kk == nk - 1)
        def _():
            r = acc_ref[...]
            if add is not None:
                r = r + c_ref[...]
            o_ref[...] = r.astype(o_ref.dtype)

    a_spec = pl.BlockSpec((tk, tm), lambda i, j, kk: (kk, i)) if ta else pl.BlockSpec((tm, tk), lambda i, j, kk: (i, kk))
    b_spec = pl.BlockSpec((tn, tk), lambda i, j, kk: (j, kk)) if tb else pl.BlockSpec((tk, tn), lambda i, j, kk: (kk, j))
    o_spec = pl.BlockSpec((tm, tn), lambda i, j, kk: (i, j))
    in_specs, args = [a_spec, b_spec], [a, b]
    if add is not None:
        in_specs.append(o_spec)
        args.append(add)
    return pl.pallas_call(
        body_single if nk == 1 else body, name=name, grid=(m // tm, n // tn, nk), in_specs=in_specs, out_specs=o_spec,
        out_shape=jax.ShapeDtypeStruct((m, n), out_dtype),
        scratch_shapes=[] if nk == 1 else [pltpu.VMEM((tm, tn), F32)],
        compiler_params=_cp("parallel", "parallel", "arbitrary"),
    )(*args)


def _tri(n, lower=True):
    r = lax.broadcasted_iota(jnp.int32, (n, n), 0)
    c = lax.broadcasted_iota(jnp.int32, (n, n), 1)
    return jnp.where((r >= c) if lower else (c >= r), 1.0, 0.0).astype(F32)


def _stats_fwd(fd, bias_row, a_row):
    s = fd.shape[0]
    blk = SSM_CHUNK

    def body(fd_ref, bias_ref, a_ref, dt_ref, ac_ref, cf_ref, dtr_ref, acr_ref, cfr_ref, carry_ref):
        @pl.when(pl.program_id(0) == 0)
        def _():
            carry_ref[...] = jnp.zeros_like(carry_ref)

        v = fd_ref[...] + bias_ref[...]
        dt = jnp.maximum(v, 0.0) + jnp.log(1.0 + jnp.exp(-jnp.abs(v)))
        lf = jnp.minimum(v, 0.0) - jnp.log(1.0 + jnp.exp(-jnp.abs(v)))
        tri = _tri(blk)
        ac = jnp.dot(tri, dt * a_ref[...], precision=HI, preferred_element_type=F32)
        cf = jnp.dot(tri, lf, precision=HI, preferred_element_type=F32) + carry_ref[0:1, :]
        carry_ref[...] = carry_ref[...] + jnp.sum(lf, axis=0, keepdims=True)
        dt_ref[...] = dt
        ac_ref[...] = ac
        cf_ref[...] = cf
        dtr_ref[...] = dt.T
        acr_ref[...] = ac.T
        cfr_ref[...] = cf.T

    col = pl.BlockSpec((blk, LANES), lambda i: (i, 0))
    row = pl.BlockSpec((LANES, blk), lambda i: (0, i))
    vec = pl.BlockSpec((1, LANES), lambda i: (0, 0))
    return pl.pallas_call(
        body, name="stats_fwd", grid=(s // blk,), in_specs=[col, vec, vec],
        out_specs=[col, col, col, row, row, row],
        out_shape=[jax.ShapeDtypeStruct((s, LANES), F32)] * 3 + [jax.ShapeDtypeStruct((LANES, s), F32)] * 3,
        scratch_shapes=[pltpu.VMEM((8, LANES), F32)],
        compiler_params=_cp("arbitrary"),
    )(fd, bias_row, a_row)


def _stats_bwd(fd, bias_row, ddt, dck_rows, dcq_cols):
    s = fd.shape[0]
    blk = SSM_CHUNK
    nb = s // blk

    def body(fd_ref, bias_ref, ddt_ref, dck_ref, dcq_ref, o_ref, db_ref, carry_ref):
        @pl.when(pl.program_id(0) == 0)
        def _():
            carry_ref[...] = jnp.zeros_like(carry_ref)
            db_ref[...] = jnp.zeros_like(db_ref)

        v = fd_ref[...] + bias_ref[...]
        dcum = dck_ref[...].T + dcq_ref[...]
        dlf = jnp.dot(_tri(blk, lower=False), dcum, precision=HI, preferred_element_type=F32) + carry_ref[0:1, :]
        carry_ref[...] = carry_ref[...] + jnp.sum(dcum, axis=0, keepdims=True)
        lane = lax.broadcasted_iota(jnp.int32, v.shape, 1)
        g = jnp.where(lane < F_LANE0, ddt_ref[...] * _sigmoid(v), dlf * _sigmoid(-v))
        g = jnp.where(lane < F_LANE0 + ATT_HEADS, g, 0.0)
        o_ref[...] = g.astype(o_ref.dtype)
        db_ref[...] += jnp.sum(g, axis=0, keepdims=True)

    col = pl.BlockSpec((blk, LANES), lambda i: (nb - 1 - i, 0))
    row = pl.BlockSpec((LANES, blk), lambda i: (0, nb - 1 - i))
    vec = pl.BlockSpec((1, LANES), lambda i: (0, 0))
    return pl.pallas_call(
        body, name="stats_bwd", grid=(nb,), in_specs=[col, vec, col, row, col], out_specs=[col, vec],
        out_shape=[jax.ShapeDtypeStruct((s, LANES), BF16), jax.ShapeDtypeStruct((1, LANES), F32)],
        scratch_shapes=[pltpu.VMEM((8, LANES), F32)],
        compiler_params=_cp("arbitrary"),
    )(fd, bias_row, ddt, dck_rows, dcq_cols)


HP = LANES // ATT_HEAD_DIM
N_HP = ATT_HEADS // HP


def _att_blocks(s):
    return (512, 1024) if s % 1024 == 0 and s >= 4096 else (64, 128)


_QK = (((1,), (1,)), ((), ()))
_HALF = ATT_HEAD_DIM // 2


def _head_cols(a):
    return slice(a * ATT_HEAD_DIM, (a + 1) * ATT_HEAD_DIM)


def _causal(shape, off):
    r = lax.broadcasted_iota(jnp.int32, shape, 0)
    c = lax.broadcasted_iota(jnp.int32, shape, 1)
    return c <= r + off


def _attention_fwd(qkv, ck4, gather_parts):
    s = qkv.shape[0]
    bq, bk = _att_blocks(s)
    nq, nk = s // bq, s // bk
    n = len(gather_parts)

    def body(q_ref, k_ref, v_ref, ck_ref, *rest):
        comm_in, (o_ref, lse_ref), comm_out, sems = rest[:n], rest[n:n + 2], rest[n + 2:2 * n + 2], rest[2 * n + 2:]
        i = pl.program_id(1)
        if n:
            @pl.when((pl.program_id(0) == 0) & (i == 0))
            def _():
                for cp in _comm_copies(comm_in, comm_out, sems, False):
                    cp.start()

        n_full = (i * bq) // bk
        qs = [(q_ref[:, _head_cols(a)].astype(F32) * ATT_SCALE).astype(BF16) for a in range(HP)]

        upper_v = lax.broadcasted_iota(jnp.int32, (bk, LANES), 1) >= ATT_HEAD_DIM
        upper_q = lax.broadcasted_iota(jnp.int32, (bq, LANES), 1) >= ATT_HEAD_DIM

        def step(j, carry, off=None):
            ks = pl.ds(pl.multiple_of(j * bk, bk), bk)
            v_both = v_ref[ks, :]
            out = []
            for a in range(HP):
                m, acc = carry[a]
                sc = lax.dot_general(qs[a], k_ref[ks, _head_cols(a)], _QK, preferred_element_type=F32)
                sc = sc - ck_ref[0, a, pl.ds(j, 1), :]
                if off is not None:
                    sc = jnp.where(_causal(sc.shape, off), sc, NEG)
                m_new = jnp.maximum(m, jnp.max(sc, axis=1, keepdims=True))
                p = jnp.exp((sc - m_new).astype(BF16))
                v_aug = jnp.where(upper_v == (a == 1), v_both, jnp.ones_like(v_both))
                acc = jnp.exp(m - m_new) * acc + jnp.dot(p, v_aug, preferred_element_type=F32)
                out.append((m_new, acc))
            return tuple(out)

        init = tuple((jnp.full((bq, 1), NEG, F32), jnp.zeros((bq, LANES), F32)) for _ in range(HP))
        carry = lax.fori_loop(0, n_full, step, init)
        carry = step(n_full, carry, off=i * bq - n_full * bk)
        outs, lses = [], []
        for a in range(HP):
            m, acc = carry[a]
            l = pltpu.roll(acc, ATT_HEAD_DIM, 1)
            outs.append(acc / l)
            lses.append(m + jnp.log(l))
        o_ref[...] = jnp.where(upper_q, outs[1], outs[0])
        lse_ref[...] = jnp.where(upper_q, lses[1], lses[0])
        if n:
            @pl.when((pl.program_id(0) == N_HP - 1) & (i == nq - 1))
            def _():
                for cp in _comm_copies(comm_in, comm_out, sems, False):
                    cp.wait()

    q_spec = pl.BlockSpec((bq, LANES), lambda h, i: (i, h))
    anyspec = pl.BlockSpec(memory_space=pl.ANY)
    res = pl.pallas_call(
        body, name="att_fwd", grid=(N_HP, nq),
        in_specs=[q_spec, pl.BlockSpec((s, LANES), lambda h, i: (0, N_HP + h)),
                  pl.BlockSpec((s, LANES), lambda h, i: (0, 2 * N_HP + h)),
                  pl.BlockSpec((1, HP, nk, bk), lambda h, i: (h, 0, 0, 0))] + [anyspec] * n,
        out_specs=[q_spec, q_spec] + [anyspec] * n,
        out_shape=[jax.ShapeDtypeStruct((s, D_MODEL), F32)] * 2 + _comm_out_shapes(gather_parts, False),
        scratch_shapes=_comm_sems(n) if n else [],
        compiler_params=_cp("arbitrary", "arbitrary"),
    )(qkv, qkv, qkv, ck4, *gather_parts)
    return res[0], res[1], list(res[2:])


def _att_prep(do, o, lse_rep):
    s = do.shape[0]
    bs = _tile(s, 512)

    def body(do_ref, o_ref, lse_ref, st_ref, dob_ref):
        r = lax.broadcasted_iota(jnp.int32, (LANES, LANES), 0) // ATT_HEAD_DIM
        c = lax.broadcasted_iota(jnp.int32, (LANES, LANES), 1) // ATT_HEAD_DIM
        e = jnp.where(r == c, 1.0, 0.0).astype(F32)
        lane = lax.broadcasted_iota(jnp.int32, (bs, LANES), 1)
        for p in range(D_MODEL // LANES):
            cs = slice(p * LANES, (p + 1) * LANES)
            dd = do_ref[:, cs]
            delta = jnp.dot(dd * o_ref[:, cs], e, precision=HI, preferred_element_type=F32)
            st_ref[:, cs] = jnp.where(lane % ATT_HEAD_DIM < _HALF, lse_ref[:, cs], delta)
            dob_ref[:, cs] = dd.astype(BF16)

    spec = pl.BlockSpec((bs, D_MODEL), lambda i: (i, 0))
    return pl.pallas_call(body, name="att_prep", grid=(s // bs,), in_specs=[spec, spec, spec], out_specs=[spec, spec],
                          out_shape=[jax.ShapeDtypeStruct((s, D_MODEL), F32), jax.ShapeDtypeStruct((s, D_MODEL), BF16)],
                          compiler_params=_cp("parallel"))(do, o, lse_rep)


def _attention_bwd(qkv, ck4, st, do_b, exchange_parts):
    s = qkv.shape[0]
    bq, bk = _att_blocks(s)
    nq, nk, per = s // bq, s // bk, bk // bq
    _T = (((0,), (0,)), ((), ()))
    n = len(exchange_parts)

    def body(q_ref, k_ref, v_ref, ck_ref, st_ref, do_ref, *rest):
        comm_in, (dq_ref, dk_ref, dv_ref, dck_ref, dcq_ref) = rest[:n], rest[n:n + 5]
        comm_out, sems, (dk_acc, dv_acc, dck_acc) = rest[n + 5:2 * n + 5], rest[2 * n + 5:-3], rest[-3:]
        j = pl.program_id(1)
        if n:
            @pl.when((pl.program_id(0) == 0) & (j == 0))
            def _():
                for cp in _comm_copies(comm_in, comm_out, sems, True):
                    cp.start()

        @pl.when(j == 0)
        def _():
            dq_ref[...] = jnp.zeros_like(dq_ref)
            dcq_ref[...] = jnp.zeros_like(dcq_ref)

        dk_acc[...] = jnp.zeros_like(dk_acc)
        dv_acc[...] = jnp.zeros_like(dv_acc)
        dck_acc[...] = jnp.zeros_like(dck_acc)

        k_t = [k_ref[:, _head_cols(a)].T for a in range(HP)]

        def step(i, off=None):
            rows = pl.ds(pl.multiple_of(i * bq, bq), bq)
            for a in range(HP):
                cs = _head_cols(a)
                q = (q_ref[rows, cs].astype(F32) * ATT_SCALE).astype(BF16)
                k = k_ref[:, cs]
                do_a = do_ref[rows, cs]
                sc = lax.dot_general(q, k, _QK, preferred_element_type=F32) - ck_ref[0, a, pl.ds(j, 1), :]
                if off is not None:
                    sc = jnp.where(_causal(sc.shape, off), sc, NEG)
                p = jnp.exp(sc - st_ref[rows, a * ATT_HEAD_DIM:a * ATT_HEAD_DIM + 1])
                dp = lax.dot_general(do_a, v_ref[:, cs], _QK, preferred_element_type=F32)
                ds = p * (dp - st_ref[rows, a * ATT_HEAD_DIM + _HALF:a * ATT_HEAD_DIM + _HALF + 1])
                ds_b = ds.astype(BF16)
                dv_acc[a] += jnp.dot(do_a.T, p.astype(BF16), preferred_element_type=F32)
                dk_acc[a] += jnp.dot(q.T, ds_b, preferred_element_type=F32)
                dq_ref[cs, rows] += lax.dot_general(k_t[a], ds_b, _QK, preferred_element_type=F32) * ATT_SCALE
                dck_acc[a] -= jnp.sum(ds, axis=0, keepdims=True)
                dcq_ref[rows, cs] += jnp.broadcast_to(jnp.sum(ds, axis=1, keepdims=True), (bq, ATT_HEAD_DIM))

        for t in range(per):
            step(j * per + t, off=t * bq)

        def full(i, c):
            step(i)
            return c

        lax.fori_loop((j + 1) * per, nq, full, 0)
        for a in range(HP):
            dk_ref[:, _head_cols(a)] = dk_acc[a].T.astype(dk_ref.dtype)
            dv_ref[:, _head_cols(a)] = dv_acc[a].T.astype(dv_ref.dtype)
            dck_ref[0, a, pl.ds(j, 1), :] = dck_acc[a]
        if n:
            @pl.when((pl.program_id(0) == N_HP - 1) & (j == nk - 1))
            def _():
                for cp in _comm_copies(comm_in, comm_out, sems, True):
                    cp.wait()

    res = pl.BlockSpec((s, LANES), lambda h, j: (0, h))
    ck_spec = pl.BlockSpec((1, HP, nk, bk), lambda h, j: (h, 0, 0, 0))
    kout = pl.BlockSpec((bk, LANES), lambda h, j: (j, h))
    anyspec = pl.BlockSpec(memory_space=pl.ANY)
    outs = pl.pallas_call(
        body, name="att_bwd", grid=(N_HP, nk),
        in_specs=[res, pl.BlockSpec((bk, LANES), lambda h, j: (j, N_HP + h)),
                  pl.BlockSpec((bk, LANES), lambda h, j: (j, 2 * N_HP + h)), ck_spec, res, res] + [anyspec] * n,
        out_specs=[pl.BlockSpec((LANES, s), lambda h, j: (h, 0)), kout, kout, ck_spec, res] + [anyspec] * n,
        out_shape=[jax.ShapeDtypeStruct((D_MODEL, s), F32), jax.ShapeDtypeStruct((s, D_MODEL), BF16),
                   jax.ShapeDtypeStruct((s, D_MODEL), BF16), jax.ShapeDtypeStruct((N_HP, HP, nk, bk), F32),
                   jax.ShapeDtypeStruct((s, D_MODEL), F32)] + _comm_out_shapes(exchange_parts, True),
        scratch_shapes=(_comm_sems(n) if n else [])
        + [pltpu.VMEM((HP, ATT_HEAD_DIM, bk), F32), pltpu.VMEM((HP, ATT_HEAD_DIM, bk), F32),
           pltpu.VMEM((HP, 1, bk), F32)],
        compiler_params=_cp("arbitrary", "arbitrary"),
    )(qkv, qkv, qkv, ck4, st, do_b, *exchange_parts)
    return outs[:5], list(outs[5:])


def _silu_and_grad(x):
    sg = _sigmoid(x)
    return x * sg, sg * (1.0 + x * (1.0 - sg))


def _conv_pre(cur, halo, w_ref, b_ref, first):
    halo = jnp.where(first, 0.0, halo)
    row = lax.broadcasted_iota(jnp.int32, cur.shape, 0)
    shifted = []
    for k in range(SSM_CONV):
        sh = SSM_CONV - 1 - k
        if sh == 0:
            shifted.append(cur)
            continue
        r = pltpu.roll(cur, sh, 0)
        hr = pltpu.roll(halo, sh, 0)
        top = jnp.where(row[0:8] < sh, hr, r[0:8])
        shifted.append(jnp.concatenate([top, r[8:]], axis=0))
    pre = b_ref[...] + sum(w_ref[k:k + 1, :] * shifted[k] for k in range(SSM_CONV))
    return pre, shifted


def _conv_specs(s, bs, bc):
    cur = pl.BlockSpec((bs, bc), lambda j, i: (i, j))
    halo = pl.BlockSpec((8, bc), lambda j, i: (jnp.maximum(i * (bs // 8) - 1, 0), j))
    w = pl.BlockSpec((SSM_CONV, bc), lambda j, i: (0, j))
    b = pl.BlockSpec((1, bc), lambda j, i: (0, j))
    return cur, halo, w, b


def _conv_fwd(xbc, w, b):
    s, c = xbc.shape
    bs, bc = _tile(s, 512), 1024

    def body(x_ref, h_ref, w_ref, b_ref, o_ref):
        pre, _ = _conv_pre(x_ref[...], h_ref[...], w_ref, b_ref, pl.program_id(1) == 0)
        o_ref[...] = pre * _sigmoid(pre)

    cur, halo, ws, bsp = _conv_specs(s, bs, bc)
    return pl.pallas_call(body, name="conv_fwd", grid=(c // bc, s // bs), in_specs=[cur, halo, ws, bsp],
                          out_specs=cur, out_shape=jax.ShapeDtypeStruct((s, c), F32),
                          compiler_params=_cp("parallel", "parallel"))(xbc, xbc, w, b)


def _conv_bwd_pre(xbc, w, b, dact):
    s, c = xbc.shape
    bs, bc = _tile(s, 512), 1024

    def body(x_ref, h_ref, w_ref, b_ref, g_ref, dp_ref, dw_ref, db_ref):
        @pl.when(pl.program_id(1) == 0)
        def _():
            dw_ref[...] = jnp.zeros_like(dw_ref)
            db_ref[...] = jnp.zeros_like(db_ref)

        pre, shifted = _conv_pre(x_ref[...], h_ref[...], w_ref, b_ref, pl.program_id(1) == 0)
        dpre = g_ref[...] * _silu_and_grad(pre)[1]
        dp_ref[...] = dpre
        db_ref[...] += jnp.sum(dpre, axis=0, keepdims=True)
        for k in range(SSM_CONV):
            dw_ref[k:k + 1, :] += jnp.sum(dpre * shifted[k], axis=0, keepdims=True)

    cur, halo, ws, bsp = _conv_specs(s, bs, bc)
    return pl.pallas_call(
        body, name="conv_bwd_pre", grid=(c // bc, s // bs), in_specs=[cur, halo, ws, bsp, cur],
        out_specs=[cur, ws, bsp],
        out_shape=[jax.ShapeDtypeStruct((s, c), F32), jax.ShapeDtypeStruct((SSM_CONV, c), F32),
                   jax.ShapeDtypeStruct((1, c), F32)],
        compiler_params=_cp("parallel", "arbitrary"))(xbc, xbc, w, b, dact)


def _conv_bwd_in(dpre, w):
    s, c = dpre.shape
    bs, bc = _tile(s, 512), 1024
    nb = s // bs

    def body(g_ref, n_ref, w_ref, o_ref):
        cur = g_ref[...]
        nxt = jnp.where(pl.program_id(1) == nb - 1, 0.0, n_ref[...])
        row = lax.broadcasted_iota(jnp.int32, cur.shape, 0)
        acc = w_ref[SSM_CONV - 1:SSM_CONV, :] * cur
        for sh in range(1, SSM_CONV):
            r = pltpu.roll(cur, bs - sh, 0)
            nr = pltpu.roll(nxt, 8 - sh, 0)
            bot = jnp.where(row[0:8] >= 8 - sh, nr, r[bs - 8:])
            acc = acc + w_ref[SSM_CONV - 1 - sh:SSM_CONV - sh, :] * jnp.concatenate([r[:bs - 8], bot], axis=0)
        o_ref[...] = acc.astype(o_ref.dtype)

    cur = pl.BlockSpec((bs, bc), lambda j, i: (i, j))
    nxt = pl.BlockSpec((8, bc), lambda j, i: (jnp.minimum((i + 1) * (bs // 8), s // 8 - 1), j))
    ws = pl.BlockSpec((SSM_CONV, bc), lambda j, i: (0, j))
    return pl.pallas_call(body, name="conv_bwd_in", grid=(c // bc, nb), in_specs=[cur, nxt, ws], out_specs=cur,
                          out_shape=jax.ShapeDtypeStruct((s, c), BF16),
                          compiler_params=_cp("parallel", "parallel"))(dpre, dpre, w)


def _dotT(a, b):
    return lax.dot_general(a.astype(BF16), b.astype(BF16), (((1,), (1,)), ((), ())), preferred_element_type=F32)


def _Tdot(a, b):
    return lax.dot_general(a.astype(BF16), b.astype(BF16), (((0,), (0,)), ((), ())), preferred_element_type=F32)


def _dot(a, b):
    return jnp.dot(a.astype(BF16), b.astype(BF16), preferred_element_type=F32)


def _ssd_head(xbc_ref, dt_ref, ac_ref, acr_ref, h):
    L = SSM_CHUNK
    xs = xbc_ref[:, h * SSM_HEAD_DIM:(h + 1) * SSM_HEAD_DIM]
    dt_col = dt_ref[:, h:h + 1]
    a_col = ac_ref[:, h:h + 1]
    a_row = acr_ref[h:h + 1, :]
    li = lax.broadcasted_iota(jnp.int32, (L, L), 0)
    si = lax.broadcasted_iota(jnp.int32, (L, L), 1)
    decay = jnp.exp(jnp.where(li >= si, a_col - a_row, NEG))
    a_last = ac_ref[L - 1:L, h:h + 1]
    return xs, dt_col, a_col, a_last, decay


def _ssd_fwd(xbc_act, dt_c, ac_c, ac_r, d_skip):
    s = xbc_act.shape[0]
    L, P, N, G, R = SSM_CHUNK, SSM_HEAD_DIM, SSM_STATE, SSM_GROUPS, SSM_HEADS_PER_GROUP
    nc = s // L

    def body(dsk_ref, xbc_ref, dt_ref, ac_ref, acr_ref, y_ref, hp_ref, st_ref):
        @pl.when(pl.program_id(0) == 0)
        def _():
            st_ref[...] = jnp.zeros_like(st_ref)

        for g in range(G):
            b_g = xbc_ref[:, SSM_INNER + g * N:SSM_INNER + (g + 1) * N]
            c_g = xbc_ref[:, SSM_INNER + G * N + g * N:SSM_INNER + G * N + (g + 1) * N]
            cb = _dotT(c_g, b_g)
            for r in range(R):
                h = g * R + r
                xs, dt_col, a_col, a_last, decay = _ssd_head(xbc_ref, dt_ref, ac_ref, acr_ref, h)
                xdt = xs * dt_col
                hprev = st_ref[h]
                y = _dot(cb * decay, xdt) + jnp.exp(a_col) * _dotT(c_g, hprev) + dsk_ref[h] * xs
                y_ref[:, h * P:(h + 1) * P] = y
                hp_ref[0, h] = hprev
                st_ref[h] = hprev * jnp.exp(a_last) + _Tdot(xdt * jnp.exp(a_last - a_col), b_g)

    col = pl.BlockSpec((L, LANES), lambda c: (c, 0))
    return pl.pallas_call(
        body, name="ssd_fwd", grid=(nc,),
        in_specs=[pl.BlockSpec(memory_space=pltpu.SMEM), pl.BlockSpec((L, SSM_CONV_DIM), lambda c: (c, 0)), col, col,
                  pl.BlockSpec((LANES, L), lambda c: (0, c))],
        out_specs=[pl.BlockSpec((L, SSM_INNER), lambda c: (c, 0)),
                   pl.BlockSpec((1, SSM_HEADS, P, N), lambda c: (c, 0, 0, 0))],
        out_shape=[jax.ShapeDtypeStruct((s, SSM_INNER), F32), jax.ShapeDtypeStruct((nc, SSM_HEADS, P, N), F32)],
        scratch_shapes=[pltpu.VMEM((SSM_HEADS, P, N), F32)],
        compiler_params=_cp("arbitrary"),
    )(d_skip, xbc_act, dt_c, ac_c, ac_r)


def _ssd_bwd(xbc_act, dt_c, ac_c, ac_r, hprev_all, dy, d_skip, a_row):
    s = xbc_act.shape[0]
    L, P, N, G, R = SSM_CHUNK, SSM_HEAD_DIM, SSM_STATE, SSM_GROUPS, SSM_HEADS_PER_GROUP
    nc = s // L

    def body(dsk_ref, xbc_ref, dt_ref, ac_ref, acr_ref, hp_ref, dy_ref, arow_ref,
             dx_ref, ddt_ref, da_ref, dds_ref, dh_ref):
        @pl.when(pl.program_id(0) == 0)
        def _():
            dh_ref[...] = jnp.zeros_like(dh_ref)
            da_ref[...] = jnp.zeros_like(da_ref)
            dds_ref[...] = jnp.zeros_like(dds_ref)

        lane = lax.broadcasted_iota(jnp.int32, (L, LANES), 1)
        sub = lax.broadcasted_iota(jnp.int32, (LANES, L), 0)
        rowi = lax.broadcasted_iota(jnp.int32, (L, 1), 0)
        lane1 = lax.broadcasted_iota(jnp.int32, (1, LANES), 1)
        da_c = jnp.zeros((L, LANES), F32)
        da_r = jnp.zeros((LANES, L), F32)
        ddt1 = jnp.zeros((L, LANES), F32)
        dds = jnp.zeros((1, LANES), F32)
        for g in range(G):
            b_g = xbc_ref[:, SSM_INNER + g * N:SSM_INNER + (g + 1) * N]
            c_g = xbc_ref[:, SSM_INNER + G * N + g * N:SSM_INNER + G * N + (g + 1) * N]
            cb = _dotT(c_g, b_g)
            dcb = jnp.zeros((L, L), F32)
            db_g = jnp.zeros((L, N), F32)
            dc_g = jnp.zeros((L, N), F32)
            for r in range(R):
                h = g * R + r
                xs, dt_col, a_col, a_last, decay = _ssd_head(xbc_ref, dt_ref, ac_ref, acr_ref, h)
                gy = dy_ref[:, h * P:(h + 1) * P]
                xdt = xs * dt_col
                hprev = hp_ref[0, h]
                dhn = dh_ref[h]
                e_a = jnp.exp(a_col)
                e_last = jnp.exp(a_last)
                e_col = jnp.exp(a_last - a_col)
                m = cb * decay
                yoff = e_a * _dotT(c_g, hprev)
                da_col = jnp.sum(gy * yoff, axis=1, keepdims=True)
                dc_g = dc_g + e_a * _dot(gy, hprev)
                dhp = _Tdot(gy * e_a, c_g) + dhn * e_last
                da_last = jnp.sum(jnp.sum(dhn * hprev, axis=1, keepdims=True), axis=0, keepdims=True) * e_last
                xds = _dot(xdt, dhn)
                db_g = db_g + e_col * xds
                de_e = jnp.sum(xds * b_g, axis=1, keepdims=True) * e_col
                da_col = da_col - de_e
                da_last = da_last + jnp.sum(de_e, axis=0, keepdims=True)
                dxdt = e_col * _dotT(b_g, dhn)
                dm = _dotT(gy, xdt)
                dxdt = dxdt + _Tdot(m, gy)
                dcb = dcb + dm * decay
                w = dm * m
                da_col = da_col + jnp.sum(w, axis=1, keepdims=True) + jnp.where(rowi == L - 1, da_last, 0.0)
                da_c = jnp.where(lane == h, da_col, da_c)
                da_r = jnp.where(sub == h, jnp.sum(w, axis=0, keepdims=True), da_r)
                ddt1 = jnp.where(lane == h, jnp.sum(dxdt * xs, axis=1, keepdims=True), ddt1)
                dds = jnp.where(lane1 == h, jnp.sum(jnp.sum(gy * xs, axis=1, keepdims=True), axis=0, keepdims=True),
                                dds)
                dx_ref[:, h * P:(h + 1) * P] = dxdt * dt_col + dsk_ref[h] * gy
                dh_ref[h] = dhp
            dx_ref[:, SSM_INNER + g * N:SSM_INNER + (g + 1) * N] = db_g + _Tdot(dcb, c_g)
            dx_ref[:, SSM_INNER + G * N + g * N:SSM_INNER + G * N + (g + 1) * N] = dc_g + _dot(dcb, b_g)
        dda = jnp.dot(_tri(L, lower=False), da_c - da_r.T, precision=HI, preferred_element_type=F32)
        ddt_ref[...] = dda * arow_ref[...] + ddt1
        da_ref[...] += jnp.sum(dda * dt_ref[...], axis=0, keepdims=True)
        dds_ref[...] += dds

    col = pl.BlockSpec((L, LANES), lambda c: (nc - 1 - c, 0))
    vec = pl.BlockSpec((1, LANES), lambda c: (0, 0))
    return pl.pallas_call(
        body, name="ssd_bwd", grid=(nc,),
        in_specs=[pl.BlockSpec(memory_space=pltpu.SMEM), pl.BlockSpec((L, SSM_CONV_DIM), lambda c: (nc - 1 - c, 0)),
                  col, col, pl.BlockSpec((LANES, L), lambda c: (0, nc - 1 - c)),
                  pl.BlockSpec((1, SSM_HEADS, P, N), lambda c: (nc - 1 - c, 0, 0, 0)),
                  pl.BlockSpec((L, SSM_INNER), lambda c: (nc - 1 - c, 0)), vec],
        out_specs=[pl.BlockSpec((L, SSM_CONV_DIM), lambda c: (nc - 1 - c, 0)), col, vec, vec],
        out_shape=[jax.ShapeDtypeStruct((s, SSM_CONV_DIM), F32), jax.ShapeDtypeStruct((s, LANES), F32),
                   jax.ShapeDtypeStruct((1, LANES), F32), jax.ShapeDtypeStruct((1, LANES), F32)],
        scratch_shapes=[pltpu.VMEM((SSM_HEADS, P, N), F32)],
        compiler_params=_cp("arbitrary"),
    )(d_skip, xbc_act, dt_c, ac_c, ac_r, hprev_all, dy, a_row)


N_PAIR = SSM_HEADS // HP
PAIRS_PER_GROUP = SSM_HEADS_PER_GROUP // HP


def _pair_consts():
    L = SSM_CHUNK
    lane = lax.broadcasted_iota(jnp.int32, (L, LANES), 1)
    lane1 = lax.broadcasted_iota(jnp.int32, (1, LANES), 1)
    li = lax.broadcasted_iota(jnp.int32, (L, L), 0)
    si = lax.broadcasted_iota(jnp.int32, (L, L), 1)
    return lane >= ATT_HEAD_DIM, lane1 >= ATT_HEAD_DIM, li, si


def _ssd_pair_fwd(xbc_act, ac_c, dt_r, ac_r, dsk_pair):
    s = xbc_act.shape[0]
    L, N, G = SSM_CHUNK, SSM_STATE, SSM_GROUPS
    nc = s // L

    def body(xbc_ref, ac_ref, dtr_ref, acr_ref, dsk_ref, y_ref, hp_ref, st_ref):
        @pl.when(pl.program_id(0) == 0)
        def _():
            st_ref[...] = jnp.zeros_like(st_ref)

        upper, up1, li, si = _pair_consts()
        for g in range(G):
            b_g = xbc_ref[:, SSM_INNER + g * N:SSM_INNER + (g + 1) * N]
            c_g = xbc_ref[:, SSM_INNER + G * N + g * N:SSM_INNER + G * N + (g + 1) * N]
            cb = _dotT(c_g, b_g)
            b_t = b_g.T
            for q in range(PAIRS_PER_GROUP):
                pp = g * PAIRS_PER_GROUP + q
                cols = slice(pp * LANES, (pp + 1) * LANES)
                xs = xbc_ref[:, cols]
                ht = st_ref[pp]
                hp_ref[0, pp] = ht
                y = dsk_ref[pp:pp + 1, :] * xs
                s_new = jnp.zeros((N, LANES), F32)
                ea, el = [], []
                for a in range(HP):
                    h = HP * pp + a
                    acol = jnp.broadcast_to(ac_ref[:, h:h + 1], (L, LANES))
                    arow, dtrow = acr_ref[h:h + 1, :], dtr_ref[h:h + 1, :]
                    alast = ac_ref[L - 1:L, h:h + 1]
                    decay = jnp.exp(jnp.where(li >= si, acol - arow, NEG))
                    xs_a = jnp.where(upper == (a == 1), xs, 0.0)
                    y = y + _dot(cb * decay * dtrow, xs_a)
                    s_new = s_new + _dot(b_t * (dtrow * jnp.exp(alast - arow)), xs_a)
                    ea.append(jnp.exp(acol))
                    el.append(jnp.exp(alast))
                y_ref[:, cols] = y + jnp.where(upper, ea[1], ea[0]) * _dot(c_g, ht)
                st_ref[pp] = ht * jnp.where(up1, el[1], el[0]) + s_new

    col = pl.BlockSpec((L, LANES), lambda c: (c, 0))
    row = pl.BlockSpec((LANES, L), lambda c: (0, c))
    return pl.pallas_call(
        body, name="ssd_fwd", grid=(nc,),
        in_specs=[pl.BlockSpec((L, SSM_CONV_DIM), lambda c: (c, 0)), col, row, row,
                  pl.BlockSpec((N_PAIR, LANES), lambda c: (0, 0))],
        out_specs=[pl.BlockSpec((L, SSM_INNER), lambda c: (c, 0)),
                   pl.BlockSpec((1, N_PAIR, N, LANES), lambda c: (c, 0, 0, 0))],
        out_shape=[jax.ShapeDtypeStruct((s, SSM_INNER), F32), jax.ShapeDtypeStruct((nc, N_PAIR, N, LANES), F32)],
        scratch_shapes=[pltpu.VMEM((N_PAIR, N, LANES), F32)],
        compiler_params=_cp("arbitrary"),
    )(xbc_act, ac_c, dt_r, ac_r, dsk_pair)


def _ssd_pair_bwd(xbc_act, dt_c, ac_c, dt_r, ac_r, hprev_all, dy, dsk_pair, a_row):
    s = xbc_act.shape[0]
    L, N, G = SSM_CHUNK, SSM_STATE, SSM_GROUPS
    nc = s // L
    rev = lambda c: nc - 1 - c

    def body(xbc_ref, dt_ref, ac_ref, dtr_ref, acr_ref, hp_ref, dy_ref, dsk_ref, arow_ref,
             dx_ref, ddt_ref, da_ref, dds_ref, dh_ref):
        @pl.when(pl.program_id(0) == 0)
        def _():
            dh_ref[...] = jnp.zeros_like(dh_ref)
            da_ref[...] = jnp.zeros_like(da_ref)
            dds_ref[...] = jnp.zeros_like(dds_ref)

        upper, up1, li, si = _pair_consts()
        lane = lax.broadcasted_iota(jnp.int32, (L, LANES), 1)
        sub = lax.broadcasted_iota(jnp.int32, (LANES, L), 0)
        lastrow = lax.broadcasted_iota(jnp.int32, (L, LANES), 0) == L - 1
        da_c = jnp.zeros((L, LANES), F32)
        da_r = jnp.zeros((LANES, L), F32)
        ddt_r = jnp.zeros((LANES, L), F32)
        for g in range(G):
            b_g = xbc_ref[:, SSM_INNER + g * N:SSM_INNER + (g + 1) * N]
            c_g = xbc_ref[:, SSM_INNER + G * N + g * N:SSM_INNER + G * N + (g + 1) * N]
            cb, cb_t = _dotT(c_g, b_g), _dotT(b_g, c_g)
            b_t, c_t = b_g.T, c_g.T
            dcb = jnp.zeros((L, L), F32)
            db_t = jnp.zeros((N, L), F32)
            dc = jnp.zeros((L, N), F32)
            for q in range(PAIRS_PER_GROUP):
                pp = g * PAIRS_PER_GROUP + q
                cols = slice(pp * LANES, (pp + 1) * LANES)
                xs, gy = xbc_ref[:, cols], dy_ref[:, cols]
                ht, dhn = hp_ref[0, pp], dh_ref[pp]
                acol = [jnp.broadcast_to(ac_ref[:, HP * pp + a:HP * pp + a + 1], (L, LANES)) for a in range(HP)]
                alast = [ac_ref[L - 1:L, HP * pp + a:HP * pp + a + 1] for a in range(HP)]
                ea = jnp.where(upper, jnp.exp(acol[1]), jnp.exp(acol[0]))
                el = jnp.where(up1, jnp.exp(alast[1]), jnp.exp(alast[0]))
                ge = gy * ea
                dc = dc + _dotT(ge, ht)
                dh_ref[pp] = _dot(c_t, ge) + dhn * el
                t_off = (ge * _dot(c_g, ht)).astype(BF16)
                hsum = jnp.sum(dhn * ht, axis=0, keepdims=True)
                dxs = dsk_ref[pp:pp + 1, :] * gy
                dds_ref[pp:pp + 1, :] += jnp.sum(gy * xs, axis=0, keepdims=True)
                for a in range(HP):
                    h = HP * pp + a
                    mine, mine1 = upper == (a == 1), up1 == (a == 1)
                    arow, dtrow = acr_ref[h:h + 1, :], dtr_ref[h:h + 1, :]
                    dtcol = jnp.broadcast_to(dt_ref[:, h:h + 1], (L, LANES))
                    xs_a, gy_a = jnp.where(mine, xs, 0.0), jnp.where(mine, gy, 0.0)
                    dhn_a = jnp.where(mine1, dhn, 0.0)
                    e_row = jnp.exp(alast[a] - arow)
                    w_row = dtrow * e_row
                    xd_t = _dotT(dhn_a, xs_a)
                    db_t = db_t + xd_t * w_row
                    dw = jnp.sum(b_t * xd_t, axis=0, keepdims=True)
                    de_e = dw * w_row
                    dal = (jnp.sum(jnp.where(mine1, hsum, 0.0), axis=1, keepdims=True) * jnp.exp(alast[a])
                           + jnp.sum(de_e, axis=1, keepdims=True))
                    dxs = dxs + _dot(b_g, dhn_a) * (dtcol * jnp.exp(alast[a] - acol[a]))
                    decay = jnp.exp(jnp.where(li >= si, acol[a] - arow, NEG))
                    decay_t = jnp.exp(jnp.where(si >= li, arow - acol[a], NEG))
                    m = cb * decay
                    dmdt = _dotT(gy_a, xs_a)
                    dxs = dxs + _dot(cb_t * decay_t * dtcol, gy_a)
                    dm = dmdt * dtrow
                    dcb = dcb + dm * decay
                    wb = (dm * m).astype(BF16)
                    onehot = jnp.where(lane == h, 1.0, 0.0).astype(BF16)
                    da_c = (da_c + jnp.dot(wb, onehot, preferred_element_type=F32)
                            + jnp.dot(jnp.where(mine, t_off, 0.0).astype(BF16), onehot, preferred_element_type=F32)
                            + jnp.where(lastrow & (lane == h), dal, 0.0))
                    da_r = jnp.where(sub == h, -(jnp.sum(wb.astype(F32), axis=0, keepdims=True) + de_e), da_r)
                    ddt_r = jnp.where(sub == h, dw * e_row + jnp.sum(dmdt * m, axis=0, keepdims=True), ddt_r)
                dx_ref[:, cols] = dxs
            dx_ref[:, SSM_INNER + g * N:SSM_INNER + (g + 1) * N] = (db_t + _dot(c_t, dcb)).T
            dx_ref[:, SSM_INNER + G * N + g * N:SSM_INNER + G * N + (g + 1) * N] = dc + _dot(dcb, b_g)
        dda = jnp.dot(_tri(L, lower=False), da_c + da_r.T, precision=HI, preferred_element_type=F32)
        ddt_ref[...] = dda * arow_ref[...] + ddt_r.T
        da_ref[...] += jnp.sum(dda * dt_ref[...], axis=0, keepdims=True)

    col = pl.BlockSpec((L, LANES), lambda c: (rev(c), 0))
    row = pl.BlockSpec((LANES, L), lambda c: (0, rev(c)))
    vec = pl.BlockSpec((1, LANES), lambda c: (0, 0))
    pairs = pl.BlockSpec((N_PAIR, LANES), lambda c: (0, 0))
    return pl.pallas_call(
        body, name="ssd_bwd", grid=(nc,),
        in_specs=[pl.BlockSpec((L, SSM_CONV_DIM), lambda c: (rev(c), 0)), col, col, row, row,
                  pl.BlockSpec((1, N_PAIR, N, LANES), lambda c: (rev(c), 0, 0, 0)),
                  pl.BlockSpec((L, SSM_INNER), lambda c: (rev(c), 0)), pairs, vec],
        out_specs=[pl.BlockSpec((L, SSM_CONV_DIM), lambda c: (rev(c), 0)), col, vec, pairs],
        out_shape=[jax.ShapeDtypeStruct((s, SSM_CONV_DIM), F32), jax.ShapeDtypeStruct((s, LANES), F32),
                   jax.ShapeDtypeStruct((1, LANES), F32), jax.ShapeDtypeStruct((N_PAIR, LANES), F32)],
        scratch_shapes=[pltpu.VMEM((N_PAIR, N, LANES), F32)],
        compiler_params=_cp("arbitrary"),
    )(xbc_act, dt_c, ac_c, dt_r, ac_r, hprev_all, dy, dsk_pair, a_row)


ROWS = 256
GW = SSM_INNER // SSM_GROUPS


def _rows(width, dtype=F32):
    return pl.BlockSpec((ROWS, width), lambda i: (i, 0))


def _vec(width):
    return pl.BlockSpec((1, width), lambda i: (0, 0))


def _gnorm_fwd(y, z, w):
    s = y.shape[0]

    def body(y_ref, z_ref, w_ref, o_ref):
        for g in range(SSM_GROUPS):
            cs = slice(g * GW, (g + 1) * GW)
            zz = z_ref[:, cs].astype(F32)
            u = y_ref[:, cs] * (zz * _sigmoid(zz))
            r = lax.rsqrt(jnp.mean(u * u, axis=1, keepdims=True) + RMS_EPS)
            o_ref[:, cs] = (u * r * w_ref[:, cs]).astype(o_ref.dtype)

    return pl.pallas_call(body, name="gnorm_fwd", grid=(s // ROWS,),
                          in_specs=[_rows(SSM_INNER), _rows(SSM_INNER), _vec(SSM_INNER)], out_specs=_rows(SSM_INNER),
                          out_shape=jax.ShapeDtypeStruct((s, SSM_INNER), BF16), compiler_params=_cp("parallel"))(y, z, w)


def _gnorm_bwd(y, z, w, do):
    s = y.shape[0]

    def body(y_ref, z_ref, w_ref, do_ref, dy_ref, dz_ref, dw_ref):
        @pl.when(pl.program_id(0) == 0)
        def _():
            dw_ref[...] = jnp.zeros_like(dw_ref)

        for g in range(SSM_GROUPS):
            cs = slice(g * GW, (g + 1) * GW)
            zz, yy, dd = z_ref[:, cs].astype(F32), y_ref[:, cs], do_ref[:, cs].astype(F32)
            sz, dsz = _silu_and_grad(zz)
            u = yy * sz
            r = lax.rsqrt(jnp.mean(u * u, axis=1, keepdims=True) + RMS_EPS)
            n = u * r
            dn = dd * w_ref[:, cs]
            dw_ref[:, cs] += jnp.sum(dd * n, axis=0, keepdims=True)
            du = r * (dn - n * jnp.mean(dn * n, axis=1, keepdims=True))
            dy_ref[:, cs] = du * sz
            dz_ref[:, cs] = (du * yy * dsz).astype(dz_ref.dtype)

    return pl.pallas_call(
        body, name="gnorm_bwd", grid=(s // ROWS,),
        in_specs=[_rows(SSM_INNER), _rows(SSM_INNER), _vec(SSM_INNER), _rows(SSM_INNER)],
        out_specs=[_rows(SSM_INNER), _rows(SSM_INNER), _vec(SSM_INNER)],
        out_shape=[jax.ShapeDtypeStruct((s, SSM_INNER), F32), jax.ShapeDtypeStruct((s, SSM_INNER), BF16),
                   jax.ShapeDtypeStruct((1, SSM_INNER), F32)],
        compiler_params=_cp("arbitrary"))(y, z, w, do)


def _mix_fwd(gl, bg, attn_d, ssm_d):
    s = gl.shape[0]
    d = D_MODEL

    def body(gl_ref, bg_ref, a_ref, m_ref, o_ref):
        g0 = _sigmoid(gl_ref[:, :d] + bg_ref[:, :d])
        g1 = _sigmoid(gl_ref[:, d:] + bg_ref[:, d:])
        o_ref[...] = (g0 * a_ref[...] + g1 * m_ref[...]).astype(o_ref.dtype)

    return pl.pallas_call(body, name="mix_fwd", grid=(s // ROWS,),
                          in_specs=[_rows(2 * d), _vec(2 * d), _rows(d), _rows(d)], out_specs=_rows(d),
                          out_shape=jax.ShapeDtypeStruct((s, d), BF16), compiler_params=_cp("parallel"))(
        gl, bg, attn_d, ssm_d)


def _mix_bwd(gl, bg, attn_d, ssm_d, dmix):
    s = gl.shape[0]
    d = D_MODEL

    def body(gl_ref, bg_ref, a_ref, m_ref, dm_ref, da_ref, ds_ref, dg_ref, db_ref):
        @pl.when(pl.program_id(0) == 0)
        def _():
            db_ref[...] = jnp.zeros_like(db_ref)

        g0 = _sigmoid(gl_ref[:, :d] + bg_ref[:, :d])
        g1 = _sigmoid(gl_ref[:, d:] + bg_ref[:, d:])
        dm = dm_ref[...].astype(F32)
        da_ref[...] = (dm * g0).astype(da_ref.dtype)
        ds_ref[...] = (dm * g1).astype(ds_ref.dtype)
        dl0 = dm * a_ref[...] * g0 * (1.0 - g0)
        dl1 = dm * m_ref[...] * g1 * (1.0 - g1)
        dg_ref[:, :d] = dl0.astype(dg_ref.dtype)
        dg_ref[:, d:] = dl1.astype(dg_ref.dtype)
        db_ref[:, :d] += jnp.sum(dl0, axis=0, keepdims=True)
        db_ref[:, d:] += jnp.sum(dl1, axis=0, keepdims=True)

    return pl.pallas_call(
        body, name="mix_bwd", grid=(s // ROWS,),
        in_specs=[_rows(2 * d), _vec(2 * d), _rows(d), _rows(d), _rows(d)],
        out_specs=[_rows(d), _rows(d), _rows(2 * d), _vec(2 * d)],
        out_shape=[jax.ShapeDtypeStruct((s, d), BF16), jax.ShapeDtypeStruct((s, d), BF16),
                   jax.ShapeDtypeStruct((s, 2 * d), BF16), jax.ShapeDtypeStruct((1, 2 * d), F32)],
        compiler_params=_cp("arbitrary"))(gl, bg, attn_d, ssm_d, dmix)


def _ln_stats(p):
    mu = jnp.mean(p, axis=1, keepdims=True)
    c = p - mu
    rstd = lax.rsqrt(jnp.mean(c * c, axis=1, keepdims=True) + LN_EPS)
    return c * rstd, rstd


def _ln_bwd(dy, xhat, rstd, g):
    dxh = dy * g
    return rstd * (dxh - jnp.mean(dxh, axis=1, keepdims=True) - xhat * jnp.mean(dxh * xhat, axis=1, keepdims=True))


def _ln1_fwd(x, mixed, g, b):
    s, d = x.shape

    def body(x_ref, m_ref, g_ref, b_ref, o_ref):
        xhat, _ = _ln_stats(DEEPNORM_ALPHA * x_ref[...] + m_ref[...])
        o_ref[...] = xhat * g_ref[...] + b_ref[...]

    return pl.pallas_call(body, name="ln1_fwd", grid=(s // ROWS,), in_specs=[_rows(d), _rows(d), _vec(d), _vec(d)],
                          out_specs=_rows(d), out_shape=jax.ShapeDtypeStruct((s, d), F32),
                          compiler_params=_cp("parallel"))(x, mixed, g, b)


def _ln2_loss(x1, h, target, g, b):
    s, d = x1.shape

    def body(x_ref, h_ref, t_ref, g_ref, b_ref, dp_ref, loss_ref, dg_ref, db_ref):
        @pl.when(pl.program_id(0) == 0)
        def _():
            loss_ref[...] = jnp.zeros_like(loss_ref)
            dg_ref[...] = jnp.zeros_like(dg_ref)
            db_ref[...] = jnp.zeros_like(db_ref)

        xhat, rstd = _ln_stats(DEEPNORM_ALPHA * x_ref[...] + h_ref[...])
        err = xhat * g_ref[...] + b_ref[...] - t_ref[...]
        part = 0.5 * jnp.sum(jnp.mean(err * err, axis=1, keepdims=True), axis=0, keepdims=True)
        loss_ref[...] += jnp.broadcast_to(part, loss_ref.shape)
        dy = err * (1.0 / d)
        dg_ref[...] += jnp.sum(dy * xhat, axis=0, keepdims=True)
        db_ref[...] += jnp.sum(dy, axis=0, keepdims=True)
        dp_ref[...] = _ln_bwd(dy, xhat, rstd, g_ref[...])

    return pl.pallas_call(
        body, name="ln2_loss", grid=(s // ROWS,), in_specs=[_rows(d), _rows(d), _rows(d), _vec(d), _vec(d)],
        out_specs=[_rows(d), _vec(LANES), _vec(d), _vec(d)],
        out_shape=[jax.ShapeDtypeStruct((s, d), F32), jax.ShapeDtypeStruct((1, LANES), F32),
                   jax.ShapeDtypeStruct((1, d), F32), jax.ShapeDtypeStruct((1, d), F32)],
        compiler_params=_cp("arbitrary"))(x1, h, target, g, b)


def _ln1_bwd(x, mixed, g, dpre2, dffn):
    s, d = x.shape

    def body(x_ref, m_ref, g_ref, d2_ref, df_ref, dp_ref, dr_ref, dg_ref, db_ref):
        @pl.when(pl.program_id(0) == 0)
        def _():
            dg_ref[...] = jnp.zeros_like(dg_ref)
            db_ref[...] = jnp.zeros_like(db_ref)

        xhat, rstd = _ln_stats(DEEPNORM_ALPHA * x_ref[...] + m_ref[...])
        dy = DEEPNORM_ALPHA * d2_ref[...] + df_ref[...]
        dg_ref[...] += jnp.sum(dy * xhat, axis=0, keepdims=True)
        db_ref[...] += jnp.sum(dy, axis=0, keepdims=True)
        dp = _ln_bwd(dy, xhat, rstd, g_ref[...])
        dp_ref[...] = dp
        dr_ref[...] = DEEPNORM_ALPHA * dp

    return pl.pallas_call(
        body, name="ln1_bwd", grid=(s // ROWS,), in_specs=[_rows(d), _rows(d), _vec(d), _rows(d), _rows(d)],
        out_specs=[_rows(d), _rows(d), _vec(d), _vec(d)],
        out_shape=[jax.ShapeDtypeStruct((s, d), F32), jax.ShapeDtypeStruct((s, d), F32),
                   jax.ShapeDtypeStruct((1, d), F32), jax.ShapeDtypeStruct((1, d), F32)],
        compiler_params=_cp("arbitrary"))(x, mixed, g, dpre2, dffn)


def _swiglu_fwd(gu):
    s = gu.shape[0]
    f = FFN_HIDDEN

    def body(g_ref, u_ref, o_ref):
        gg = g_ref[...].astype(F32)
        o_ref[...] = (gg * _sigmoid(gg) * u_ref[...].astype(F32)).astype(o_ref.dtype)

    return pl.pallas_call(
        body, name="swiglu_fwd", grid=(s // ROWS,),
        in_specs=[pl.BlockSpec((ROWS, f), lambda i: (i, 0)), pl.BlockSpec((ROWS, f), lambda i: (i, 1))],
        out_specs=_rows(f), out_shape=jax.ShapeDtypeStruct((s, f), BF16), compiler_params=_cp("parallel"))(gu, gu)


def _swiglu_bwd(gu, dact):
    s = gu.shape[0]
    f = FFN_HIDDEN

    def body(g_ref, u_ref, d_ref, o_ref):
        sg, dsg = _silu_and_grad(g_ref[...].astype(F32))
        dd = d_ref[...].astype(F32)
        o_ref[:, :f] = (dd * u_ref[...].astype(F32) * dsg).astype(o_ref.dtype)
        o_ref[:, f:] = (dd * sg).astype(o_ref.dtype)

    return pl.pallas_call(
        body, name="swiglu_bwd", grid=(s // ROWS,),
        in_specs=[pl.BlockSpec((ROWS, f), lambda i: (i, 0)), pl.BlockSpec((ROWS, f), lambda i: (i, 1)), _rows(f)],
        out_specs=_rows(2 * f), out_shape=jax.ShapeDtypeStruct((s, 2 * f), BF16),
        compiler_params=_cp("parallel"))(gu, gu, dact)


def _peer(k):
    x, y, c = lax.axis_index("x"), lax.axis_index("y"), lax.axis_index("c")
    kx, ky, kc = (k >> 2) & 1, (k >> 1) & 1, k & 1
    px = (1 - x) if kx else x
    py = (1 - y) if ky else y
    pc = (1 - c) if kc else c
    return (px, py, pc), 4 * px + 2 * py + pc


def _my_index():
    return 4 * lax.axis_index("x") + 2 * lax.axis_index("y") + lax.axis_index("c")


def _comm_copies(ins, outs, sems, scatter):
    send_sems, recv_sems, local_sems = sems
    me = _my_index()
    copies = [pltpu.make_async_copy(ins[t].at[me] if scatter else ins[t], outs[t].at[me], local_sems.at[t])
              for t in range(len(ins))]
    for k in range(1, N_DEV):
        peer, pidx = _peer(k)
        for t in range(len(ins)):
            copies.append(pltpu.make_async_remote_copy(
                src_ref=ins[t].at[pidx] if scatter else ins[t], dst_ref=outs[t].at[me],
                send_sem=send_sems.at[t, k - 1], recv_sem=recv_sems.at[t, k - 1], device_id=peer,
                device_id_type=pl.DeviceIdType.MESH))
    return copies


def _comm_sems(n):
    return [pltpu.SemaphoreType.DMA((n, N_DEV - 1)), pltpu.SemaphoreType.DMA((n, N_DEV - 1)),
            pltpu.SemaphoreType.DMA((n,))]


def _comm_out_shapes(parts, scatter):
    return [jax.ShapeDtypeStruct(p.shape if scatter else (N_DEV,) + p.shape, p.dtype) for p in parts]


def _comm_call(parts, scatter, name):
    n = len(parts)

    def body(*refs):
        copies = _comm_copies(refs[:n], refs[n:2 * n], refs[2 * n:], scatter)
        for cp in copies:
            cp.start()
        for cp in copies:
            cp.wait()

    anyspec = pl.BlockSpec(memory_space=pl.ANY)
    return pl.pallas_call(body, name=name, in_specs=[anyspec] * n, out_specs=[anyspec] * n,
                          out_shape=_comm_out_shapes(parts, scatter), scratch_shapes=_comm_sems(n))(*parts)


def _all_gather(parts):
    n = len(parts)

    def body(*refs):
        ins, outs = refs[:n], refs[n:2 * n]
        send_sems, recv_sems, local_sems = refs[2 * n:]
        x, y, c = lax.axis_index("x"), lax.axis_index("y"), lax.axis_index("c")
        me, sibling = (x, y, c), (x, y, 1 - c)
        chips = [(1 - x, y), (x, 1 - y), (1 - x, 1 - y)]

        def copy(t, k, block, to, src=None):
            dst = outs[t].at[4 * block[0] + 2 * block[1] + block[2]]
            return pltpu.make_async_remote_copy(
                src_ref=dst if src is None else src, dst_ref=dst, send_sem=send_sems.at[t, k],
                recv_sem=recv_sems.at[t, k], device_id=to, device_id_type=pl.DeviceIdType.MESH)

        mine = [pltpu.make_async_copy(ins[t], outs[t].at[_my_index()], local_sems.at[t]) for t in range(n)]
        for cp in mine:
            cp.start()
        first = [copy(t, 0, me, sibling, src=ins[t]) for t in range(n)]
        first += [copy(t, 1 + j, me, (*chip, c), src=ins[t]) for j, chip in enumerate(chips) for t in range(n)]
        for cp in first:
            cp.start()
        passed = []
        for j, chip in enumerate(chips):
            for t in range(n):
                copy(t, 1 + j, (*chip, c), me).wait_recv()
                passed.append(copy(t, 4 + j, (*chip, c), sibling))
                passed[-1].start()
        for t in range(n):
            copy(t, 0, sibling, me).wait_recv()
            for j, chip in enumerate(chips):
                copy(t, 4 + j, (*chip, 1 - c), me).wait_recv()
        for cp in first + passed:
            cp.wait_send()
        for cp in mine:
            cp.wait()

    anyspec = pl.BlockSpec(memory_space=pl.ANY)
    return pl.pallas_call(body, name="all_gather", in_specs=[anyspec] * n, out_specs=[anyspec] * n,
                          out_shape=_comm_out_shapes(parts, False), scratch_shapes=_comm_sems(n))(*parts)


def _exchange(parts):
    return _comm_call(parts, True, "grad_exchange")


def _remote_scatter_copies(ins, lands, send_sems, recv_sems):
    me = _my_index()
    copies = []
    for k in range(1, N_DEV):
        peer, pidx = _peer(k)
        for t in range(len(ins)):
            copies.append(pltpu.make_async_remote_copy(
                src_ref=ins[t].at[pidx], dst_ref=lands[t].at[me], send_sem=send_sems.at[t * (N_DEV - 1) + k - 1],
                recv_sem=recv_sems.at[t * (N_DEV - 1) + k - 1], device_id=peer, device_id_type=pl.DeviceIdType.MESH))
    return copies


def _landing_zones(parts):
    me = _my_index()
    return [jnp.where(lax.broadcasted_iota(jnp.int32, p.shape, 0) == me, p, jnp.zeros_like(p)) for p in parts]


_HBM = pl.BlockSpec(memory_space=pltpu.HBM)
_SEM = pl.BlockSpec(memory_space=pltpu.SEMAPHORE)


def _exchange_start(parts, lands):
    n = len(parts)

    def body(*refs):
        ins, lnd, send_sems, recv_sems, token = refs[:n], refs[n:2 * n], refs[2 * n], refs[2 * n + 1], refs[-1]
        for cp in _remote_scatter_copies(ins, lnd, send_sems, recv_sems):
            cp.start()
        token[...] = jnp.zeros_like(token)

    hbm = [pltpu.HBM(p.shape, p.dtype) for p in parts]
    outs = pl.pallas_call(
        body, name="exchange_start",
        out_shape=[pltpu.SemaphoreType.DMA((n * (N_DEV - 1),)), pltpu.SemaphoreType.DMA((n * (N_DEV - 1),))] + hbm + hbm
        + [jax.ShapeDtypeStruct((8, LANES), F32)],
        in_specs=[_HBM] * (2 * n), out_specs=[_SEM, _SEM] + [_HBM] * (2 * n) + [pl.BlockSpec(memory_space=pltpu.VMEM)],
        input_output_aliases={t: 2 + t for t in range(2 * n)},
        compiler_params=pltpu.CompilerParams(has_side_effects=pltpu.SideEffectType.DATAFLOW_SIDE_EFFECTING),
    )(*[pltpu.with_memory_space_constraint(p, pltpu.HBM) for p in list(parts) + list(lands)])
    return outs[0], outs[1], list(outs[2:2 + n]), list(outs[2 + n:2 + 2 * n]), outs[-1]


def _exchange_wait(send_sems, recv_sems, parts, lands, after):
    n = len(parts)

    def body(*refs):
        ins, lnd, send_sems, recv_sems = refs[:n], refs[n:2 * n], refs[2 * n], refs[2 * n + 1]
        for cp in _remote_scatter_copies(ins, lnd, send_sems, recv_sems):
            cp.wait_send()
            cp.wait_recv()

    hbm = [pltpu.HBM(p.shape, p.dtype) for p in parts]
    outs = pl.pallas_call(
        body, name="exchange_wait", out_shape=hbm + hbm,
        in_specs=[_HBM] * (2 * n) + [_SEM, _SEM, pl.BlockSpec(memory_space=pl.ANY)], out_specs=[_HBM] * (2 * n),
        input_output_aliases={t: t for t in range(2 * n)},
        compiler_params=pltpu.CompilerParams(has_side_effects=pltpu.SideEffectType.DATAFLOW_SIDE_EFFECTING),
    )(*parts, *lands, send_sems, recv_sems, after)
    return list(outs[n:])


def _adamw(recv, w, m, v, name):
    _, r, c = w.shape
    br = _tile(r, 128)
    c1 = 1.0 / (1.0 - ADAM_B1 ** ADAM_STEP)
    c2 = 1.0 / (1.0 - ADAM_B2 ** ADAM_STEP)

    def body(r_ref, w_ref, m_ref, v_ref, g_ref, d_ref, mo_ref, vo_ref):
        g = r_ref[0].astype(F32)
        for k in range(1, N_DEV):
            g = g + r_ref[k].astype(F32)
        mn = ADAM_B1 * m_ref[0] + (1.0 - ADAM_B1) * g
        vn = ADAM_B2 * v_ref[0] + (1.0 - ADAM_B2) * (g * g)
        g_ref[0] = g
        mo_ref[0] = mn
        vo_ref[0] = vn
        d_ref[0] = -ADAM_LR * ((mn * c1) / (jnp.sqrt(vn * c2) + ADAM_EPS) + ADAM_WD * w_ref[0])

    blk = pl.BlockSpec((1, br, c), lambda i: (0, i, 0))
    return pl.pallas_call(
        body, name=name, grid=(r // br,),
        in_specs=[pl.BlockSpec((N_DEV, br, c), lambda i: (0, i, 0)), blk, blk, blk],
        out_specs=[blk] * 4, out_shape=[jax.ShapeDtypeStruct((1, r, c), F32)] * 4,
        compiler_params=_cp("parallel"))(recv, w, m, v)


def _lane_row(pairs):
    row = jnp.zeros((LANES,), F32)
    for lane0, vec in pairs:
        row = lax.dynamic_update_slice(row, vec.astype(F32), (lane0,))
    return row.reshape(1, LANES)


def _stage_in(x, wts, small):
    s = x.shape[0]
    a = -jnp.exp(small["a_log"])
    bias_row = _lane_row([(DT_LANE0, small["dt_bias"]), (F_LANE0, small["b_forget"])])
    a_row = _lane_row([(DT_LANE0, a)])
    conv_b = small["conv_b"].reshape(1, -1)
    norm_w = small["ssm_norm_w"].reshape(1, -1)
    bg = small["b_gates"].reshape(1, -1)
    g1, b1 = small["ln1_g"].reshape(1, -1), small["ln1_b"].reshape(1, -1)
    g2, b2 = small["ln2_g"].reshape(1, -1), small["ln2_b"].reshape(1, -1)
    d_skip = small["d_skip"]
    xb = x.astype(BF16)

    qkv = _mm(xb, wts["qkv"], out_dtype=BF16, name="f_qkv")
    z = _mm(xb, wts["z"], out_dtype=BF16, name="f_z")
    xbc = _mm(xb, wts["xbc"], name="f_xbc")
    gl = _mm(xb, wts["gate"], out_dtype=BF16, name="f_gate")
    fd = _mm(xb, wts["fd"], name="f_fd")
    dt_c, ac_c, cf_c, dt_r, ac_r, cf_r = _stats_fwd(fd, bias_row, a_row)
    bk = _att_blocks(s)[1]
    ck4 = cf_r[F_LANE0:F_LANE0 + ATT_HEADS].reshape(N_HP, HP, s // bk, bk)
    return dict(locals())


def _stage_mid(c, attn, lse, wts, target):
    x, xb, qkv, z, xbc, gl, fd, ck4 = (c[k] for k in ("x", "xb", "qkv", "z", "xbc", "gl", "fd", "ck4"))
    dt_c, ac_c, dt_r, ac_r, a_row, bias_row = (c[k] for k in ("dt_c", "ac_c", "dt_r", "ac_r", "a_row", "bias_row"))
    conv_b, norm_w, bg, g1, b1, g2, b2, d_skip = (c[k] for k in ("conv_b", "norm_w", "bg", "g1", "b1", "g2", "b2",
                                                                "d_skip"))
    conv_w = c["wts"]["conv"]
    attn_d = _mm(attn, wts["pa"], out_dtype=BF16, name="f_pa")
    xact = _conv_fwd(xbc, conv_w, conv_b)
    dsk_pair = jnp.repeat(d_skip, SSM_HEAD_DIM).reshape(N_PAIR, LANES)
    y, hprev = _ssd_pair_fwd(xact, ac_c, dt_r, ac_r, dsk_pair)
    ssm = _gnorm_fwd(y, z, norm_w)
    ssm_d = _mm(ssm, wts["ps"], out_dtype=BF16, name="f_ps")
    mix = _mix_fwd(gl, bg, attn_d, ssm_d)
    mixed = _mm(mix, wts["out"], name="f_out")
    x1 = _ln1_fwd(x, mixed, g1, b1)
    gu = _mm(x1, wts["gu"], out_dtype=BF16, name="f_gu")
    act = _swiglu_fwd(gu)
    h = _mm(act, wts["down"], name="f_down")
    dpre2, loss_row, dg2, db2 = _ln2_loss(x1, h, target, g2, b2)

    d_act = _mm(dpre2, wts["down"], tb=True, out_dtype=BF16, name="b_down_x")
    dw_down = _mm(act, dpre2, ta=True, name="b_down_w")
    dgu = _swiglu_bwd(gu, d_act)
    dffn = _mm(dgu, wts["gu"], tb=True, name="b_gu_x")
    dw_gu = _mm(x1, dgu, ta=True, name="b_gu_w")
    dpre1, dxr, dg1, db1 = _ln1_bwd(x, mixed, g1, dpre2, dffn)
    dmix = _mm(dpre1, wts["out"], tb=True, out_dtype=BF16, name="b_out_x")
    dw_out = _mm(mix, dpre1, ta=True, name="b_out_w")
    dattn_d, dssm_d, dgl, dbg = _mix_bwd(gl, bg, attn_d, ssm_d, dmix)
    dssm = _mm(dssm_d, wts["ps"], tb=True, out_dtype=BF16, name="b_ps_x")
    dw_ps = _mm(ssm, dssm_d, ta=True, name="b_ps_w")
    dattn = _mm(dattn_d, wts["pa"], tb=True, name="b_pa_x")
    dw_pa = _mm(attn, dattn_d, ta=True, name="b_pa_w")
    dy, dz, dnw = _gnorm_bwd(y, z, norm_w, dssm)
    dxact, ddt, da_row, dds_pair = _ssd_pair_bwd(xact, dt_c, ac_c, dt_r, ac_r, hprev, dy, dsk_pair, a_row)
    dds = dds_pair.reshape(SSM_HEADS, SSM_HEAD_DIM).sum(axis=1)
    dpre_c, dconv_w, dconv_b = _conv_bwd_pre(xbc, conv_w, conv_b, dxact)
    dxbc = _conv_bwd_in(dpre_c, conv_w)
    st, do_b = _att_prep(dattn, attn, lse)
    late = dict(pa=dw_pa, ps=dw_ps, out=dw_out, gu=dw_gu, down=dw_down)
    keep = ("st", "do_b", "ddt", "dxr", "dz", "dxbc", "dgl", "dconv_w", "dconv_b", "da_row", "dds", "dnw", "dbg",
            "dg1", "db1", "dg2", "db2", "loss_row")
    loc = locals()
    return {**c, **{k: loc[k] for k in keep}}, late


def _stage_out_w(c, att_grads):
    dq, dk, dv, dck, dcq = att_grads
    xb, fd, bias_row, ddt, dz, dxbc, dgl = (c[k] for k in ("xb", "fd", "bias_row", "ddt", "dz", "dxbc", "dgl"))
    s, a = xb.shape[0], c["a"]
    dck_rows = jnp.zeros((LANES, s), F32).at[F_LANE0:F_LANE0 + ATT_HEADS].set(dck.reshape(ATT_HEADS, s))
    dcq_cols = jnp.zeros((s, LANES), F32).at[:, F_LANE0:F_LANE0 + ATT_HEADS].set(dcq[:, ::ATT_HEAD_DIM])
    dfd, dbias = _stats_bwd(fd, bias_row, ddt, dck_rows, dcq_cols)
    dproj = (dq, dk, dv, dz, dxbc, dgl, dfd)
    dw_in = [_mm(xb, g_, ta=True, tb=(i == 0), name=f"b_in_w{i}") for i, g_ in enumerate(dproj)]
    grads = dict(q=dw_in[0], k=dw_in[1], v=dw_in[2], z=dw_in[3], xbc=dw_in[4], gate=dw_in[5], fd=dw_in[6],
                 conv=c["dconv_w"])
    small_g = dict(
        b_forget=dbias[0, F_LANE0:F_LANE0 + ATT_HEADS], conv_b=c["dconv_b"][0], dt_bias=dbias[0, :SSM_HEADS],
        a_log=c["da_row"][0, :SSM_HEADS] * a, d_skip=c["dds"], ssm_norm_w=c["dnw"][0], b_gates=c["dbg"][0],
        ln1_g=c["dg1"][0], ln1_b=c["db1"][0], ln2_g=c["dg2"][0], ln2_b=c["db2"][0])
    return c["loss_row"][0, 0], grads, small_g, dproj


def _stage_out_x(c, dproj, token):
    wts, d = c["wts"], D_MODEL
    wq = wts["qkv"][:, :d] + token.astype(BF16)
    wk, wv = wts["qkv"][:, d:2 * d], wts["qkv"][:, 2 * d:]
    dx = c["dxr"]
    for i, (g_, w_) in enumerate(zip(dproj, (wq, wk, wv, wts["z"], wts["xbc"], wts["gate"], wts["fd"]))):
        dx = _mm(g_, w_, ta=(i == 0), tb=True, add=dx, name=f"b_in_x{i}")
    return dx


BIG = ("w_in", "w_proj_attn", "w_proj_ssm", "w_out", "w_ffn_gate", "w_ffn_up", "w_ffn_down", "conv_w")
EARLY = ("w_in", "conv_w")
LATE = ("w_proj_attn", "w_proj_ssm", "w_out", "w_ffn_gate", "w_ffn_up", "w_ffn_down")
SMALL = ("b_forget", "conv_b", "dt_bias", "a_log", "d_skip", "ssm_norm_w", "b_gates", "ln1_g", "ln1_b", "ln2_g",
         "ln2_b")
SMALL_ROWS = 96
IN_SHARD = IN_WIDTH // N_DEV
IN_SEGMENTS = (("q", 0, 1024), ("k", 1024, 1024), ("v", 2048, 1024), ("f", 3072, ATT_HEADS), ("z", 3088, SSM_INNER),
               ("xbc", 5136, SSM_CONV_DIM), ("dt", 8208, SSM_HEADS), ("gate", 8240, 2 * D_MODEL))


def _cols_from_shards(shards, lo, hi):
    w = shards[0].shape[1]
    pieces = []
    for j in range(len(shards)):
        a, b = max(lo, j * w), min(hi, (j + 1) * w)
        if a < b:
            pieces.append(shards[j][:, a - j * w:b - j * w])
    return pieces[0] if len(pieces) == 1 else jnp.concatenate(pieces, axis=1)


def _shards_from_parts(parts, width):
    shards = []
    for j in range(N_DEV):
        lo, hi = j * width, (j + 1) * width
        pieces = []
        for mat, c0 in parts:
            a, b = max(lo, c0), min(hi, c0 + mat.shape[1])
            if a < b:
                pieces.append(mat[:, a - c0:b - c0])
        shards.append(pieces[0] if len(pieces) == 1 else jnp.concatenate(pieces, axis=1))
    return shards


def _pack_small(vals):
    flat = jnp.concatenate([vals[n].reshape(-1) for n in SMALL])
    return jnp.pad(flat, (0, SMALL_ROWS * LANES - flat.shape[0])).reshape(SMALL_ROWS, LANES)


def _unpack_small(pack, shapes):
    flat = pack.reshape(-1)
    out, off = {}, 0
    for n in SMALL:
        sz = math.prod(shapes[n])
        out[n] = flat[off:off + sz].reshape(shapes[n])
        off += sz
    return out


def kernel(x, w_in, b_forget, conv_w, conv_b, dt_bias, a_log, d_skip, ssm_norm_w, w_proj_attn, w_proj_ssm, b_gates, w_out, ln1_g, ln1_b, w_ffn_gate, w_ffn_up, w_ffn_down, ln2_g, ln2_b, loss_target, m_w_in, m_b_forget, m_conv_w, m_conv_b, m_dt_bias, m_a_log, m_d_skip, m_ssm_norm_w, m_w_proj_attn, m_w_proj_ssm, m_b_gates, m_w_out, m_ln1_g, m_ln1_b, m_w_ffn_gate, m_w_ffn_up, m_w_ffn_down, m_ln2_g, m_ln2_b, v_w_in, v_b_forget, v_conv_w, v_conv_b, v_dt_bias, v_a_log, v_d_skip, v_ssm_norm_w, v_w_proj_attn, v_w_proj_ssm, v_b_gates, v_w_out, v_ln1_g, v_ln1_b, v_w_ffn_gate, v_w_ffn_up, v_w_ffn_down, v_ln2_g, v_ln2_b):
    args = dict(locals())
    d, f = D_MODEL, FFN_HIDDEN
    big_w = {n: args[n][0] for n in BIG}
    small_w = {n: args[n][0] for n in SMALL}
    big_shapes = {n: args[n].shape for n in BIG}
    small_shapes = {n: args[n].shape for n in SMALL}

    early = dict(zip(EARLY, _all_gather([big_w["w_in"].astype(BF16), big_w["conv_w"]])))
    in_shards = [early["w_in"][j] for j in range(N_DEV)]
    seg = {n: _cols_from_shards(in_shards, c0, c0 + w) for n, c0, w in IN_SEGMENTS}
    wfd = jnp.concatenate([seg["dt"], seg["f"], jnp.zeros((d, LANES - SSM_HEADS - ATT_HEADS), BF16)], axis=1)
    wts = dict(qkv=jnp.concatenate([seg["q"], seg["k"], seg["v"]], axis=1), z=seg["z"], xbc=seg["xbc"],
               gate=seg["gate"], fd=wfd, conv=jnp.concatenate([early["conv_w"][j] for j in range(N_DEV)], axis=1))

    ctx = _stage_in(x[0], wts, small_w)
    attn, lse, gathered = _attention_fwd(ctx["qkv"], ctx["ck4"], [big_w[n].astype(BF16) for n in LATE])
    full = dict(zip(LATE, gathered))
    late_w = dict(
        pa=full["w_proj_attn"].reshape(d, d), ps=full["w_proj_ssm"].reshape(SSM_INNER, d),
        out=full["w_out"].reshape(d, d),
        gu=jnp.concatenate([full["w_ffn_gate"][j] for j in range(N_DEV)]
                           + [full["w_ffn_up"][j] for j in range(N_DEV)], axis=1),
        down=full["w_ffn_down"].reshape(f, d))
    ctx, gl = _stage_mid(ctx, attn, lse, late_w, loss_target[0])
    late_dest = dict(
        w_ffn_gate=jnp.stack([s_.astype(BF16) for s_ in _shards_from_parts([(gl["gu"][:, :f], 0)], f // N_DEV)]),
        w_ffn_up=jnp.stack([s_.astype(BF16) for s_ in _shards_from_parts([(gl["gu"][:, f:], 0)], f // N_DEV)]))
    for n, key in (("w_proj_attn", "pa"), ("w_proj_ssm", "ps"), ("w_out", "out"), ("w_ffn_down", "down")):
        late_dest[n] = gl[key].astype(BF16).reshape((N_DEV,) + big_shapes[n][1:])
    att_grads, late_recv = _attention_bwd(ctx["qkv"], ctx["ck4"], ctx["st"], ctx["do_b"], [late_dest[n] for n in LATE])
    loss_part, g, small_g, dproj = _stage_out_w(ctx, att_grads)
    loss = lax.psum(loss_part, ("x", "y", "c"))

    gfd = g["fd"]
    in_parts = dict(q=g["q"], k=g["k"], v=g["v"], f=gfd[:, F_LANE0:F_LANE0 + ATT_HEADS], z=g["z"], xbc=g["xbc"],
                    dt=gfd[:, DT_LANE0:DT_LANE0 + SSM_HEADS], gate=g["gate"])
    win_dest = jnp.stack([s_.astype(BF16) for s_ in
                          _shards_from_parts([(in_parts[n], c0) for n, c0, _ in IN_SEGMENTS], IN_SHARD)])
    conv_dest = jnp.stack(_shards_from_parts([(g["conv"], 0)], SSM_CONV_DIM // N_DEV))
    small_pack = _pack_small(small_g)
    last_parts = [win_dest, conv_dest, jnp.broadcast_to(small_pack, (N_DEV,) + small_pack.shape)]
    send_sems, recv_sems, parts_thru, lands_thru, token = _exchange_start(last_parts, _landing_zones(last_parts))
    grad_x = _stage_out_x(ctx, dproj, token[0, 0])
    early_recv = _exchange_wait(send_sems, recv_sems, parts_thru, lands_thru, grad_x)
    recv = dict(zip(LATE, late_recv))
    recv["w_in"], recv["conv_w"] = early_recv[0], early_recv[1]

    outs = {}
    for n in BIG:
        outs[n] = _adamw(recv[n], args[n], args["m_" + n], args["v_" + n], name="adamw_" + n)
    small4 = _adamw(early_recv[2], _pack_small(small_w)[None], _pack_small({n: args["m_" + n][0] for n in SMALL})[None],
                    _pack_small({n: args["v_" + n][0] for n in SMALL})[None], name="adamw_small")
    small_out = [_unpack_small(p, small_shapes) for p in small4]
    for n in SMALL:
        outs[n] = [so[n] for so in small_out]

    order = ("w_in", "b_forget", "conv_w", "conv_b", "dt_bias", "a_log", "d_skip", "ssm_norm_w", "w_proj_attn",
             "w_proj_ssm", "b_gates", "w_out", "ln1_g", "ln1_b", "w_ffn_gate", "w_ffn_up", "w_ffn_down", "ln2_g",
             "ln2_b")
    res = [loss, grad_x[None]]
    for i in range(4):
        res += [outs[n][i] for n in order]
    return tuple(res)
```

```python
import functools
import math

import jax
import jax.numpy as jnp
from jax import lax
from jax.experimental import pallas as pl
from jax.experimental.pallas import tpu as pltpu

F32 = jnp.float32
BF16 = jnp.bfloat16

N_DEV = 8
D_MODEL = 1024
ATT_HEADS = 16
ATT_HEAD_DIM = 64
SSM_INNER = 2048
SSM_HEADS = 32
SSM_HEAD_DIM = 64
SSM_GROUPS = 4
SSM_HEADS_PER_GROUP = 8
SSM_STATE = 128
SSM_CONV = 4
SSM_CHUNK = 128
SSM_CONV_DIM = 3072
FFN_HIDDEN = 2816
IN_WIDTH = 10288
DEEPNORM_ALPHA = 2.0 ** 0.25
LN_EPS = 1e-5
RMS_EPS = 1e-5
ADAM_LR, ADAM_B1, ADAM_B2, ADAM_EPS, ADAM_WD, ADAM_STEP = 0.001, 0.9, 0.999, 1e-08, 0.01, 10
ATT_SCALE = 1.0 / math.sqrt(ATT_HEAD_DIM)

LANES = 128
VMEM_LIMIT = 56 * 1024 * 1024
NEG = -1e30

DT_LANE0 = 0
F_LANE0 = 32
HI = lax.Precision.HIGHEST


def _cp(*sem):
    return pltpu.CompilerParams(dimension_semantics=sem, vmem_limit_bytes=VMEM_LIMIT)


def _tile(n, cap=1408):
    for t in (1408, 1024, 512, 384, 256, 128):
        if t <= cap and n % t == 0:
            return t
    return n


def _sigmoid(x):
    return 1.0 / (1.0 + jnp.exp(-x))


def _mm(a, b, *, ta=False, tb=False, out_dtype=F32, add=None, name):
    m, k = (a.shape[1], a.shape[0]) if ta else a.shape
    n = b.shape[0] if tb else b.shape[1]
    assert (b.shape[1] if tb else b.shape[0]) == k
    tm, tn, tk = _tile(m), _tile(n), _tile(k)
    nk = k // tk
    dims = (((0,) if ta else (1,), (1,) if tb else (0,)), ((), ()))

    def body_single(*refs):
        a_ref, b_ref = refs[:2]
        r = lax.dot_general(a_ref[...].astype(BF16), b_ref[...].astype(BF16), dims, preferred_element_type=F32)
        if add is not None:
            r = r + refs[2][...]
        refs[-1][...] = r.astype(refs[-1].dtype)

    def body(*refs):
        if add is None:
            a_ref, b_ref, o_ref, acc_ref = refs
        else:
            a_ref, b_ref, c_ref, o_ref, acc_ref = refs
        kk = pl.program_id(2)

        @pl.when(kk == 0)
        def _():
            acc_ref[...] = jnp.zeros_like(acc_ref)

        acc_ref[...] += lax.dot_general(a_ref[...].astype(BF16), b_ref[...].astype(BF16), dims,
                                        preferred_element_type=F32)

        @pl.when(kk == nk - 1)
        def _():
            r = acc_ref[...]
            if add is not None:
                r = r + c_ref[...]
            o_ref[...] = r.astype(o_ref.dtype)

    a_spec = pl.BlockSpec((tk, tm), lambda i, j, kk: (kk, i)) if ta else pl.BlockSpec((tm, tk), lambda i, j, kk: (i, kk))
    b_spec = pl.BlockSpec((tn, tk), lambda i, j, kk: (j, kk)) if tb else pl.BlockSpec((tk, tn), lambda i, j, kk: (kk, j))
    o_spec = pl.BlockSpec((tm, tn), lambda i, j, kk: (i, j))
    in_specs, args = [a_spec, b_spec], [a, b]
    if add is not None:
        in_specs.append(o_spec)
        args.append(add)
    return pl.pallas_call(
        body_single if nk == 1 else body, name=name, grid=(m // tm, n // tn, nk), in_specs=in_specs, out_specs=o_spec,
        out_shape=jax.ShapeDtypeStruct((m, n), out_dtype),
        scratch_shapes=[] if nk == 1 else [pltpu.VMEM((tm, tn), F32)],
        compiler_params=_cp("parallel", "parallel", "arbitrary"),
    )(*args)


def _tri(n, lower=True):
    r = lax.broadcasted_iota(jnp.int32, (n, n), 0)
    c = lax.broadcasted_iota(jnp.int32, (n, n), 1)
    return jnp.where((r >= c) if lower else (c >= r), 1.0, 0.0).astype(F32)


def _stats_fwd(fd, bias_row, a_row):
    s = fd.shape[0]
    blk = SSM_CHUNK

    def body(fd_ref, bias_ref, a_ref, dt_ref, ac_ref, cf_ref, dtr_ref, acr_ref, cfr_ref, carry_ref):
        @pl.when(pl.program_id(0) == 0)
        def _():
            carry_ref[...] = jnp.zeros_like(carry_ref)

        v = fd_ref[...] + bias_ref[...]
        dt = jnp.maximum(v, 0.0) + jnp.log(1.0 + jnp.exp(-jnp.abs(v)))
        lf = jnp.minimum(v, 0.0) - jnp.log(1.0 + jnp.exp(-jnp.abs(v)))
        tri = _tri(blk)
        ac = jnp.dot(tri, dt * a_ref[...], precision=HI, preferred_element_type=F32)
        cf = jnp.dot(tri, lf, precision=HI, preferred_element_type=F32) + carry_ref[0:1, :]
        carry_ref[...] = carry_ref[...] + jnp.sum(lf, axis=0, keepdims=True)
        dt_ref[...] = dt
        ac_ref[...] = ac
        cf_ref[...] = cf
        dtr_ref[...] = dt.T
        acr_ref[...] = ac.T
        cfr_ref[...] = cf.T

    col = pl.BlockSpec((blk, LANES), lambda i: (i, 0))
    row = pl.BlockSpec((LANES, blk), lambda i: (0, i))
    vec = pl.BlockSpec((1, LANES), lambda i: (0, 0))
    return pl.pallas_call(
        body, name="stats_fwd", grid=(s // blk,), in_specs=[col, vec, vec],
        out_specs=[col, col, col, row, row, row],
        out_shape=[jax.ShapeDtypeStruct((s, LANES), F32)] * 3 + [jax.ShapeDtypeStruct((LANES, s), F32)] * 3,
        scratch_shapes=[pltpu.VMEM((8, LANES), F32)],
        compiler_params=_cp("arbitrary"),
    )(fd, bias_row, a_row)


def _stats_bwd(fd, bias_row, ddt, dck_rows, dcq_cols):
    s = fd.shape[0]
    blk = SSM_CHUNK
    nb = s // blk

    def body(fd_ref, bias_ref, ddt_ref, dck_ref, dcq_ref, o_ref, db_ref, carry_ref):
        @pl.when(pl.program_id(0) == 0)
        def _():
            carry_ref[...] = jnp.zeros_like(carry_ref)
            db_ref[...] = jnp.zeros_like(db_ref)

        v = fd_ref[...] + bias_ref[...]
        dcum = dck_ref[...].T + dcq_ref[...]
        dlf = jnp.dot(_tri(blk, lower=False), dcum, precision=HI, preferred_element_type=F32) + carry_ref[0:1, :]
        carry_ref[...] = carry_ref[...] + jnp.sum(dcum, axis=0, keepdims=True)
        lane = lax.broadcasted_iota(jnp.int32, v.shape, 1)
        g = jnp.where(lane < F_LANE0, ddt_ref[...] * _sigmoid(v), dlf * _sigmoid(-v))
        g = jnp.where(lane < F_LANE0 + ATT_HEADS, g, 0.0)
        o_ref[...] = g.astype(o_ref.dtype)
        db_ref[...] += jnp.sum(g, axis=0, keepdims=True)

    col = pl.BlockSpec((blk, LANES), lambda i: (nb - 1 - i, 0))
    row = pl.BlockSpec((LANES, blk), lambda i: (0, nb - 1 - i))
    vec = pl.BlockSpec((1, LANES), lambda i: (0, 0))
    return pl.pallas_call(
        body, name="stats_bwd", grid=(nb,), in_specs=[col, vec, col, row, col], out_specs=[col, vec],
        out_shape=[jax.ShapeDtypeStruct((s, LANES), BF16), jax.ShapeDtypeStruct((1, LANES), F32)],
        scratch_shapes=[pltpu.VMEM((8, LANES), F32)],
        compiler_params=_cp("arbitrary"),
    )(fd, bias_row, ddt, dck_rows, dcq_cols)


HP = LANES // ATT_HEAD_DIM
N_HP = ATT_HEADS // HP


def _att_blocks(s):
    return (512, 1024) if s % 1024 == 0 and s >= 4096 else (64, 128)


_QK = (((1,), (1,)), ((), ()))
_HALF = ATT_HEAD_DIM // 2


def _head_cols(a):
    return slice(a * ATT_HEAD_DIM, (a + 1) * ATT_HEAD_DIM)


def _causal(shape, off):
    r = lax.broadcasted_iota(jnp.int32, shape, 0)
    c = lax.broadcasted_iota(jnp.int32, shape, 1)
    return c <= r + off


def _attention_fwd(qkv, ck4, gather_parts):
    s = qkv.shape[0]
    bk = _att_blocks(s)[1]
    bq = bk
    nq, nk = s // bq, s // bk
    n = len(gather_parts)

    def body(q_ref, k_ref, v_ref, ck_ref, *rest):
        comm_in, (o_ref, lse_ref), comm_out, sems = rest[:n], rest[n:n + 2], rest[n + 2:2 * n + 2], rest[2 * n + 2:]
        i = pl.program_id(1)
        if n:
            @pl.when((pl.program_id(0) == 0) & (i == 0))
            def _():
                for cp in _comm_copies(comm_in, comm_out, sems, False):
                    cp.start()

        n_full = (i * bq) // bk
        qs = [(q_ref[:, _head_cols(a)].astype(F32) * ATT_SCALE).astype(BF16) for a in range(HP)]

        upper_v = lax.broadcasted_iota(jnp.int32, (bk, LANES), 1) >= ATT_HEAD_DIM
        upper_q = lax.broadcasted_iota(jnp.int32, (bq, LANES), 1) >= ATT_HEAD_DIM

        def step(j, carry, off=None):
            ks = pl.ds(pl.multiple_of(j * bk, bk), bk)
            v_both = v_ref[ks, :]
            out = []
            for a in range(HP):
                m, acc = carry[a]
                sc = lax.dot_general(qs[a], k_ref[ks, _head_cols(a)], _QK, preferred_element_type=F32)
                sc = sc - ck_ref[0, a, pl.ds(j, 1), :]
                if off is not None:
                    sc = jnp.where(_causal(sc.shape, off), sc, NEG)
                m_new = jnp.maximum(m, jnp.max(sc, axis=1, keepdims=True))
                p = jnp.exp((sc - m_new).astype(BF16))
                v_aug = jnp.where(upper_v == (a == 1), v_both, jnp.ones_like(v_both))
                acc = jnp.exp(m - m_new) * acc + jnp.dot(p, v_aug, preferred_element_type=F32)
                out.append((m_new, acc))
            return tuple(out)

        init = tuple((jnp.full((bq, 1), NEG, F32), jnp.zeros((bq, LANES), F32)) for _ in range(HP))
        carry = lax.fori_loop(0, n_full, step, init)
        carry = step(n_full, carry, off=i * bq - n_full * bk)
        outs, lses = [], []
        for a in range(HP):
            m, acc = carry[a]
            l = pltpu.roll(acc, ATT_HEAD_DIM, 1)
            outs.append(acc / l)
            lses.append(m + jnp.log(l))
        o_ref[...] = jnp.where(upper_q, outs[1], outs[0])
        lse_ref[...] = jnp.where(upper_q, lses[1], lses[0])
        if n:
            @pl.when((pl.program_id(0) == N_HP - 1) & (i == nq - 1))
            def _():
                for cp in _comm_copies(comm_in, comm_out, sems, False):
                    cp.wait()

    q_spec = pl.BlockSpec((bq, LANES), lambda h, i: (i, h))
    anyspec = pl.BlockSpec(memory_space=pl.ANY)
    res = pl.pallas_call(
        body, name="att_fwd", grid=(N_HP, nq),
        in_specs=[q_spec, pl.BlockSpec((s, LANES), lambda h, i: (0, N_HP + h)),
                  pl.BlockSpec((s, LANES), lambda h, i: (0, 2 * N_HP + h)),
                  pl.BlockSpec((1, HP, nk, bk), lambda h, i: (h, 0, 0, 0))] + [anyspec] * n,
        out_specs=[q_spec, q_spec] + [anyspec] * n,
        out_shape=[jax.ShapeDtypeStruct((s, D_MODEL), F32)] * 2 + _comm_out_shapes(gather_parts, False),
        scratch_shapes=_comm_sems(n) if n else [],
        compiler_params=_cp("arbitrary", "arbitrary"),
    )(qkv, qkv, qkv, ck4, *gather_parts)
    return res[0], res[1], list(res[2:])


def _att_prep(do, o, lse_rep):
    s = do.shape[0]
    bs = _tile(s, 512)

    def body(do_ref, o_ref, lse_ref, st_ref, dob_ref):
        r = lax.broadcasted_iota(jnp.int32, (LANES, LANES), 0) // ATT_HEAD_DIM
        c = lax.broadcasted_iota(jnp.int32, (LANES, LANES), 1) // ATT_HEAD_DIM
        e = jnp.where(r == c, 1.0, 0.0).astype(F32)
        lane = lax.broadcasted_iota(jnp.int32, (bs, LANES), 1)
        for p in range(D_MODEL // LANES):
            cs = slice(p * LANES, (p + 1) * LANES)
            dd = do_ref[:, cs]
            delta = jnp.dot(dd * o_ref[:, cs], e, precision=HI, preferred_element_type=F32)
            st_ref[:, cs] = jnp.where(lane % ATT_HEAD_DIM < _HALF, lse_ref[:, cs], delta)
            dob_ref[:, cs] = dd.astype(BF16)

    spec = pl.BlockSpec((bs, D_MODEL), lambda i: (i, 0))
    return pl.pallas_call(body, name="att_prep", grid=(s // bs,), in_specs=[spec, spec, spec], out_specs=[spec, spec],
                          out_shape=[jax.ShapeDtypeStruct((s, D_MODEL), F32), jax.ShapeDtypeStruct((s, D_MODEL), BF16)],
                          compiler_params=_cp("parallel"))(do, o, lse_rep)


def _attention_bwd(qkv, ck4, st, do_b, exchange_parts):
    s = qkv.shape[0]
    bq, bk = _att_blocks(s)
    nq, nk, per = s // bq, s // bk, bk // bq
    _T = (((0,), (0,)), ((), ()))
    n = len(exchange_parts)

    def body(q_ref, k_ref, v_ref, ck_ref, st_ref, do_ref, *rest):
        comm_in, (dq_ref, dk_ref, dv_ref, dck_ref, dcq_ref) = rest[:n], rest[n:n + 5]
        comm_out, sems, (dk_acc, dv_acc, dck_acc) = rest[n + 5:2 * n + 5], rest[2 * n + 5:-3], rest[-3:]
        j = pl.program_id(1)
        if n:
            @pl.when((pl.program_id(0) == 0) & (j == 0))
            def _():
                for cp in _comm_copies(comm_in, comm_out, sems, True):
                    cp.start()

        @pl.when(j == 0)
        def _():
            dq_ref[...] = jnp.zeros_like(dq_ref)
            dcq_ref[...] = jnp.zeros_like(dcq_ref)

        dk_acc[...] = jnp.zeros_like(dk_acc)
        dv_acc[...] = jnp.zeros_like(dv_acc)
        dck_acc[...] = jnp.zeros_like(dck_acc)

        k_t = [k_ref[:, _head_cols(a)].T for a in range(HP)]

        def step(i, off=None):
            rows = pl.ds(pl.multiple_of(i * bq, bq), bq)
            for a in range(HP):
                cs = _head_cols(a)
                q = (q_ref[rows, cs].astype(F32) * ATT_SCALE).astype(BF16)
                k = k_ref[:, cs]
                do_a = do_ref[rows, cs]
                sc = lax.dot_general(q, k, _QK, preferred_element_type=F32) - ck_ref[0, a, pl.ds(j, 1), :]
                if off is not None:
                    sc = jnp.where(_causal(sc.shape, off), sc, NEG)
                p = jnp.exp(sc - st_ref[rows, a * ATT_HEAD_DIM:a * ATT_HEAD_DIM + 1])
                dp = lax.dot_general(do_a, v_ref[:, cs], _QK, preferred_element_type=F32)
                ds = p * (dp - st_ref[rows, a * ATT_HEAD_DIM + _HALF:a * ATT_HEAD_DIM + _HALF + 1])
                ds_b = ds.astype(BF16)
                dv_acc[a] += jnp.dot(do_a.T, p.astype(BF16), preferred_element_type=F32)
                dk_acc[a] += jnp.dot(q.T, ds_b, preferred_element_type=F32)
                dq_ref[cs, rows] += lax.dot_general(k_t[a], ds_b, _QK, preferred_element_type=F32) * ATT_SCALE
                dck_acc[a] -= jnp.sum(ds, axis=0, keepdims=True)
                dcq_ref[rows, cs] += jnp.broadcast_to(jnp.sum(ds, axis=1, keepdims=True), (bq, ATT_HEAD_DIM))

        for t in range(per):
            step(j * per + t, off=t * bq)

        def full(i, c):
            step(i)
            return c

        lax.fori_loop((j + 1) * per, nq, full, 0)
        for a in range(HP):
            dk_ref[:, _head_cols(a)] = dk_acc[a].T.astype(dk_ref.dtype)
            dv_ref[:, _head_cols(a)] = dv_acc[a].T.astype(dv_ref.dtype)
            dck_ref[0, a, pl.ds(j, 1), :] = dck_acc[a]
        if n:
            @pl.when((pl.program_id(0) == N_HP - 1) & (j == nk - 1))
            def _():
                for cp in _comm_copies(comm_in, comm_out, sems, True):
                    cp.wait()

    res = pl.BlockSpec((s, LANES), lambda h, j: (0, h))
    ck_spec = pl.BlockSpec((1, HP, nk, bk), lambda h, j: (h, 0, 0, 0))
    kout = pl.BlockSpec((bk, LANES), lambda h, j: (j, h))
    anyspec = pl.BlockSpec(memory_space=pl.ANY)
    outs = pl.pallas_call(
        body, name="att_bwd", grid=(N_HP, nk),
        in_specs=[res, pl.BlockSpec((bk, LANES), lambda h, j: (j, N_HP + h)),
                  pl.BlockSpec((bk, LANES), lambda h, j: (j, 2 * N_HP + h)), ck_spec, res, res] + [anyspec] * n,
        out_specs=[pl.BlockSpec((LANES, s), lambda h, j: (h, 0)), kout, kout, ck_spec, res] + [anyspec] * n,
        out_shape=[jax.ShapeDtypeStruct((D_MODEL, s), F32), jax.ShapeDtypeStruct((s, D_MODEL), BF16),
                   jax.ShapeDtypeStruct((s, D_MODEL), BF16), jax.ShapeDtypeStruct((N_HP, HP, nk, bk), F32),
                   jax.ShapeDtypeStruct((s, D_MODEL), F32)] + _comm_out_shapes(exchange_parts, True),
        scratch_shapes=(_comm_sems(n) if n else [])
        + [pltpu.VMEM((HP, ATT_HEAD_DIM, bk), F32), pltpu.VMEM((HP, ATT_HEAD_DIM, bk), F32),
           pltpu.VMEM((HP, 1, bk), F32)],
        compiler_params=_cp("arbitrary", "arbitrary"),
    )(qkv, qkv, qkv, ck4, st, do_b, *exchange_parts)
    return outs[:5], list(outs[5:])


def _silu_and_grad(x):
    sg = _sigmoid(x)
    return x * sg, sg * (1.0 + x * (1.0 - sg))


def _conv_pre(cur, halo, w_ref, b_ref, first):
    halo = jnp.where(first, 0.0, halo)
    row = lax.broadcasted_iota(jnp.int32, cur.shape, 0)
    shifted = []
    for k in range(SSM_CONV):
        sh = SSM_CONV - 1 - k
        if sh == 0:
            shifted.append(cur)
            continue
        r = pltpu.roll(cur, sh, 0)
        hr = pltpu.roll(halo, sh, 0)
        top = jnp.where(row[0:8] < sh, hr, r[0:8])
        shifted.append(jnp.concatenate([top, r[8:]], axis=0))
    pre = b_ref[...] + sum(w_ref[k:k + 1, :] * shifted[k] for k in range(SSM_CONV))
    return pre, shifted


def _conv_specs(s, bs, bc):
    cur = pl.BlockSpec((bs, bc), lambda j, i: (i, j))
    halo = pl.BlockSpec((8, bc), lambda j, i: (jnp.maximum(i * (bs // 8) - 1, 0), j))
    w = pl.BlockSpec((SSM_CONV, bc), lambda j, i: (0, j))
    b = pl.BlockSpec((1, bc), lambda j, i: (0, j))
    return cur, halo, w, b


def _conv_fwd(xbc, w, b):
    s, c = xbc.shape
    bs, bc = _tile(s, 512), 1024

    def body(x_ref, h_ref, w_ref, b_ref, o_ref):
        pre, _ = _conv_pre(x_ref[...], h_ref[...], w_ref, b_ref, pl.program_id(1) == 0)
        o_ref[...] = pre * _sigmoid(pre)

    cur, halo, ws, bsp = _conv_specs(s, bs, bc)
    return pl.pallas_call(body, name="conv_fwd", grid=(c // bc, s // bs), in_specs=[cur, halo, ws, bsp],
                          out_specs=cur, out_shape=jax.ShapeDtypeStruct((s, c), F32),
                          compiler_params=_cp("parallel", "parallel"))(xbc, xbc, w, b)


def _conv_bwd_pre(xbc, w, b, dact):
    s, c = xbc.shape
    bs, bc = _tile(s, 512), 1024

    def body(x_ref, h_ref, w_ref, b_ref, g_ref, dp_ref, dw_ref, db_ref):
        @pl.when(pl.program_id(1) == 0)
        def _():
            dw_ref[...] = jnp.zeros_like(dw_ref)
            db_ref[...] = jnp.zeros_like(db_ref)

        pre, shifted = _conv_pre(x_ref[...], h_ref[...], w_ref, b_ref, pl.program_id(1) == 0)
        dpre = g_ref[...] * _silu_and_grad(pre)[1]
        dp_ref[...] = dpre
        db_ref[...] += jnp.sum(dpre, axis=0, keepdims=True)
        for k in range(SSM_CONV):
            dw_ref[k:k + 1, :] += jnp.sum(dpre * shifted[k], axis=0, keepdims=True)

    cur, halo, ws, bsp = _conv_specs(s, bs, bc)
    return pl.pallas_call(
        body, name="conv_bwd_pre", grid=(c // bc, s // bs), in_specs=[cur, halo, ws, bsp, cur],
        out_specs=[cur, ws, bsp],
        out_shape=[jax.ShapeDtypeStruct((s, c), F32), jax.ShapeDtypeStruct((SSM_CONV, c), F32),
                   jax.ShapeDtypeStruct((1, c), F32)],
        compiler_params=_cp("parallel", "arbitrary"))(xbc, xbc, w, b, dact)


def _conv_bwd_in(dpre, w):
    s, c = dpre.shape
    bs, bc = _tile(s, 512), 1024
    nb = s // bs

    def body(g_ref, n_ref, w_ref, o_ref):
        cur = g_ref[...]
        nxt = jnp.where(pl.program_id(1) == nb - 1, 0.0, n_ref[...])
        row = lax.broadcasted_iota(jnp.int32, cur.shape, 0)
        acc = w_ref[SSM_CONV - 1:SSM_CONV, :] * cur
        for sh in range(1, SSM_CONV):
            r = pltpu.roll(cur, bs - sh, 0)
            nr = pltpu.roll(nxt, 8 - sh, 0)
            bot = jnp.where(row[0:8] >= 8 - sh, nr, r[bs - 8:])
            acc = acc + w_ref[SSM_CONV - 1 - sh:SSM_CONV - sh, :] * jnp.concatenate([r[:bs - 8], bot], axis=0)
        o_ref[...] = acc.astype(o_ref.dtype)

    cur = pl.BlockSpec((bs, bc), lambda j, i: (i, j))
    nxt = pl.BlockSpec((8, bc), lambda j, i: (jnp.minimum((i + 1) * (bs // 8), s // 8 - 1), j))
    ws = pl.BlockSpec((SSM_CONV, bc), lambda j, i: (0, j))
    return pl.pallas_call(body, name="conv_bwd_in", grid=(c // bc, nb), in_specs=[cur, nxt, ws], out_specs=cur,
                          out_shape=jax.ShapeDtypeStruct((s, c), BF16),
                          compiler_params=_cp("parallel", "parallel"))(dpre, dpre, w)


def _dotT(a, b):
    return lax.dot_general(a.astype(BF16), b.astype(BF16), (((1,), (1,)), ((), ())), preferred_element_type=F32)


def _dot(a, b):
    return jnp.dot(a.astype(BF16), b.astype(BF16), preferred_element_type=F32)


N_PAIR = SSM_HEADS // HP
PAIRS_PER_GROUP = SSM_HEADS_PER_GROUP // HP


def _pair_consts():
    L = SSM_CHUNK
    lane = lax.broadcasted_iota(jnp.int32, (L, LANES), 1)
    lane1 = lax.broadcasted_iota(jnp.int32, (1, LANES), 1)
    li = lax.broadcasted_iota(jnp.int32, (L, L), 0)
    si = lax.broadcasted_iota(jnp.int32, (L, L), 1)
    return lane >= ATT_HEAD_DIM, lane1 >= ATT_HEAD_DIM, li, si


def _ssd_pair_fwd(xbc_act, ac_c, dt_r, ac_r, dsk_pair):
    s = xbc_act.shape[0]
    L, N, G = SSM_CHUNK, SSM_STATE, SSM_GROUPS
    nc = s // L

    def body(xbc_ref, ac_ref, dtr_ref, acr_ref, dsk_ref, y_ref, hp_ref, st_ref):
        @pl.when(pl.program_id(0) == 0)
        def _():
            st_ref[...] = jnp.zeros_like(st_ref)

        upper, up1, li, si = _pair_consts()
        for g in range(G):
            b_g = xbc_ref[:, SSM_INNER + g * N:SSM_INNER + (g + 1) * N]
            c_g = xbc_ref[:, SSM_INNER + G * N + g * N:SSM_INNER + G * N + (g + 1) * N]
            cb = _dotT(c_g, b_g)
            b_t = b_g.T
            for q in range(PAIRS_PER_GROUP):
                pp = g * PAIRS_PER_GROUP + q
                cols = slice(pp * LANES, (pp + 1) * LANES)
                xs = xbc_ref[:, cols]
                ht = st_ref[pp]
                hp_ref[0, pp] = ht
                y = dsk_ref[pp:pp + 1, :] * xs
                s_new = jnp.zeros((N, LANES), F32)
                ea, el = [], []
                for a in range(HP):
                    h = HP * pp + a
                    acol = jnp.broadcast_to(ac_ref[:, h:h + 1], (L, LANES))
                    arow, dtrow = acr_ref[h:h + 1, :], dtr_ref[h:h + 1, :]
                    alast = ac_ref[L - 1:L, h:h + 1]
                    decay = jnp.exp(jnp.where(li >= si, acol - arow, NEG))
                    xs_a = jnp.where(upper == (a == 1), xs, 0.0)
                    y = y + _dot(cb * decay * dtrow, xs_a)
                    s_new = s_new + _dot(b_t * (dtrow * jnp.exp(alast - arow)), xs_a)
                    ea.append(jnp.exp(acol))
                    el.append(jnp.exp(alast))
                y_ref[:, cols] = y + jnp.where(upper, ea[1], ea[0]) * _dot(c_g, ht)
                st_ref[pp] = ht * jnp.where(up1, el[1], el[0]) + s_new

    col = pl.BlockSpec((L, LANES), lambda c: (c, 0))
    row = pl.BlockSpec((LANES, L), lambda c: (0, c))
    return pl.pallas_call(
        body, name="ssd_fwd", grid=(nc,),
        in_specs=[pl.BlockSpec((L, SSM_CONV_DIM), lambda c: (c, 0)), col, row, row,
                  pl.BlockSpec((N_PAIR, LANES), lambda c: (0, 0))],
        out_specs=[pl.BlockSpec((L, SSM_INNER), lambda c: (c, 0)),
                   pl.BlockSpec((1, N_PAIR, N, LANES), lambda c: (c, 0, 0, 0))],
        out_shape=[jax.ShapeDtypeStruct((s, SSM_INNER), F32), jax.ShapeDtypeStruct((nc, N_PAIR, N, LANES), F32)],
        scratch_shapes=[pltpu.VMEM((N_PAIR, N, LANES), F32)],
        compiler_params=_cp("arbitrary"),
    )(xbc_act, ac_c, dt_r, ac_r, dsk_pair)


def _ssd_pair_bwd(xbc_act, dt_c, ac_c, dt_r, ac_r, hprev_all, dy, dsk_pair, a_row):
    s = xbc_act.shape[0]
    L, N, G = SSM_CHUNK, SSM_STATE, SSM_GROUPS
    nc = s // L
    rev = lambda c: nc - 1 - c

    def body(xbc_ref, dt_ref, ac_ref, dtr_ref, acr_ref, hp_ref, dy_ref, dsk_ref, arow_ref,
             dx_ref, ddt_ref, da_ref, dds_ref, dh_ref):
        @pl.when(pl.program_id(0) == 0)
        def _():
            dh_ref[...] = jnp.zeros_like(dh_ref)
            da_ref[...] = jnp.zeros_like(da_ref)
            dds_ref[...] = jnp.zeros_like(dds_ref)

        upper, up1, li, si = _pair_consts()
        lane = lax.broadcasted_iota(jnp.int32, (L, LANES), 1)
        sub = lax.broadcasted_iota(jnp.int32, (LANES, L), 0)
        lastrow = lax.broadcasted_iota(jnp.int32, (L, LANES), 0) == L - 1
        da_c = jnp.zeros((L, LANES), F32)
        da_r = jnp.zeros((LANES, L), F32)
        ddt_r = jnp.zeros((LANES, L), F32)
        for g in range(G):
            b_g = xbc_ref[:, SSM_INNER + g * N:SSM_INNER + (g + 1) * N]
            c_g = xbc_ref[:, SSM_INNER + G * N + g * N:SSM_INNER + G * N + (g + 1) * N]
            cb, cb_t = _dotT(c_g, b_g), _dotT(b_g, c_g)
            b_t, c_t = b_g.T, c_g.T
            dcb = jnp.zeros((L, L), F32)
            db_t = jnp.zeros((N, L), F32)
            dc = jnp.zeros((L, N), F32)
            for q in range(PAIRS_PER_GROUP):
                pp = g * PAIRS_PER_GROUP + q
                cols = slice(pp * LANES, (pp + 1) * LANES)
                xs, gy = xbc_ref[:, cols], dy_ref[:, cols]
                ht, dhn = hp_ref[0, pp], dh_ref[pp]
                acol = [jnp.broadcast_to(ac_ref[:, HP * pp + a:HP * pp + a + 1], (L, LANES)) for a in range(HP)]
                alast = [ac_ref[L - 1:L, HP * pp + a:HP * pp + a + 1] for a in range(HP)]
                ea = jnp.where(upper, jnp.exp(acol[1]), jnp.exp(acol[0]))
                el = jnp.where(up1, jnp.exp(alast[1]), jnp.exp(alast[0]))
                ge = gy * ea
                dc = dc + _dotT(ge, ht)
                dh_ref[pp] = _dot(c_t, ge) + dhn * el
                t_off = (ge * _dot(c_g, ht)).astype(BF16)
                hsum = jnp.sum(dhn * ht, axis=0, keepdims=True)
                dxs = dsk_ref[pp:pp + 1, :] * gy
                dds_ref[pp:pp + 1, :] += jnp.sum(gy * xs, axis=0, keepdims=True)
                for a in range(HP):
                    h = HP * pp + a
                    mine, mine1 = upper == (a == 1), up1 == (a == 1)
                    arow, dtrow = acr_ref[h:h + 1, :], dtr_ref[h:h + 1, :]
                    dtcol = jnp.broadcast_to(dt_ref[:, h:h + 1], (L, LANES))
                    xs_a, gy_a = jnp.where(mine, xs, 0.0), jnp.where(mine, gy, 0.0)
                    dhn_a = jnp.where(mine1, dhn, 0.0)
                    e_row = jnp.exp(alast[a] - arow)
                    w_row = dtrow * e_row
                    xd_t = _dotT(dhn_a, xs_a)
                    db_t = db_t + xd_t * w_row
                    dw = jnp.sum(b_t * xd_t, axis=0, keepdims=True)
                    de_e = dw * w_row
                    dal = (jnp.sum(jnp.where(mine1, hsum, 0.0), axis=1, keepdims=True) * jnp.exp(alast[a])
                           + jnp.sum(de_e, axis=1, keepdims=True))
                    dxs = dxs + _dot(b_g, dhn_a) * (dtcol * jnp.exp(alast[a] - acol[a]))
                    decay = jnp.exp(jnp.where(li >= si, acol[a] - arow, NEG))
                    decay_t = jnp.exp(jnp.where(si >= li, arow - acol[a], NEG))
                    m = cb * decay
                    dmdt = _dotT(gy_a, xs_a)
                    dxs = dxs + _dot(cb_t * decay_t * dtcol, gy_a)
                    dm = dmdt * dtrow
                    dcb = dcb + dm * decay
                    wb = (dm * m).astype(BF16)
                    onehot = jnp.where(lane == h, 1.0, 0.0).astype(BF16)
                    da_c = (da_c + jnp.dot(wb, onehot, preferred_element_type=F32)
                            + jnp.dot(jnp.where(mine, t_off, 0.0).astype(BF16), onehot, preferred_element_type=F32)
                            + jnp.where(lastrow & (lane == h), dal, 0.0))
                    da_r = jnp.where(sub == h, -(jnp.sum(wb.astype(F32), axis=0, keepdims=True) + de_e), da_r)
                    ddt_r = jnp.where(sub == h, dw * e_row + jnp.sum(dmdt * m, axis=0, keepdims=True), ddt_r)
                dx_ref[:, cols] = dxs
            dx_ref[:, SSM_INNER + g * N:SSM_INNER + (g + 1) * N] = (db_t + _dot(c_t, dcb)).T
            dx_ref[:, SSM_INNER + G * N + g * N:SSM_INNER + G * N + (g + 1) * N] = dc + _dot(dcb, b_g)
        dda = jnp.dot(_tri(L, lower=False), da_c + da_r.T, precision=HI, preferred_element_type=F32)
        ddt_ref[...] = dda * arow_ref[...] + ddt_r.T
        da_ref[...] += jnp.sum(dda * dt_ref[...], axis=0, keepdims=True)

    col = pl.BlockSpec((L, LANES), lambda c: (rev(c), 0))
    row = pl.BlockSpec((LANES, L), lambda c: (0, rev(c)))
    vec = pl.BlockSpec((1, LANES), lambda c: (0, 0))
    pairs = pl.BlockSpec((N_PAIR, LANES), lambda c: (0, 0))
    return pl.pallas_call(
        body, name="ssd_bwd", grid=(nc,),
        in_specs=[pl.BlockSpec((L, SSM_CONV_DIM), lambda c: (rev(c), 0)), col, col, row, row,
                  pl.BlockSpec((1, N_PAIR, N, LANES), lambda c: (rev(c), 0, 0, 0)),
                  pl.BlockSpec((L, SSM_INNER), lambda c: (rev(c), 0)), pairs, vec],
        out_specs=[pl.BlockSpec((L, SSM_CONV_DIM), lambda c: (rev(c), 0)), col, vec, pairs],
        out_shape=[jax.ShapeDtypeStruct((s, SSM_CONV_DIM), F32), jax.ShapeDtypeStruct((s, LANES), F32),
                   jax.ShapeDtypeStruct((1, LANES), F32), jax.ShapeDtypeStruct((N_PAIR, LANES), F32)],
        scratch_shapes=[pltpu.VMEM((N_PAIR, N, LANES), F32)],
        compiler_params=_cp("arbitrary"),
    )(xbc_act, dt_c, ac_c, dt_r, ac_r, hprev_all, dy, dsk_pair, a_row)


ROWS = 256
GW = SSM_INNER // SSM_GROUPS


def _rows(width, dtype=F32):
    return pl.BlockSpec((ROWS, width), lambda i: (i, 0))


def _vec(width):
    return pl.BlockSpec((1, width), lambda i: (0, 0))


def _gnorm_fwd(y, z, w):
    s = y.shape[0]

    def body(y_ref, z_ref, w_ref, o_ref):
        for g in range(SSM_GROUPS):
            cs = slice(g * GW, (g + 1) * GW)
            zz = z_ref[:, cs].astype(F32)
            u = y_ref[:, cs] * (zz * _sigmoid(zz))
            r = lax.rsqrt(jnp.mean(u * u, axis=1, keepdims=True) + RMS_EPS)
            o_ref[:, cs] = (u * r * w_ref[:, cs]).astype(o_ref.dtype)

    return pl.pallas_call(body, name="gnorm_fwd", grid=(s // ROWS,),
                          in_specs=[_rows(SSM_INNER), _rows(SSM_INNER), _vec(SSM_INNER)], out_specs=_rows(SSM_INNER),
                          out_shape=jax.ShapeDtypeStruct((s, SSM_INNER), BF16), compiler_params=_cp("parallel"))(y, z, w)


def _gnorm_bwd(y, z, w, do):
    s = y.shape[0]

    def body(y_ref, z_ref, w_ref, do_ref, dy_ref, dz_ref, dw_ref):
        @pl.when(pl.program_id(0) == 0)
        def _():
            dw_ref[...] = jnp.zeros_like(dw_ref)

        for g in range(SSM_GROUPS):
            cs = slice(g * GW, (g + 1) * GW)
            zz, yy, dd = z_ref[:, cs].astype(F32), y_ref[:, cs], do_ref[:, cs].astype(F32)
            sz, dsz = _silu_and_grad(zz)
            u = yy * sz
            r = lax.rsqrt(jnp.mean(u * u, axis=1, keepdims=True) + RMS_EPS)
            n = u * r
            dn = dd * w_ref[:, cs]
            dw_ref[:, cs] += jnp.sum(dd * n, axis=0, keepdims=True)
            du = r * (dn - n * jnp.mean(dn * n, axis=1, keepdims=True))
            dy_ref[:, cs] = du * sz
            dz_ref[:, cs] = (du * yy * dsz).astype(dz_ref.dtype)

    return pl.pallas_call(
        body, name="gnorm_bwd", grid=(s // ROWS,),
        in_specs=[_rows(SSM_INNER), _rows(SSM_INNER), _vec(SSM_INNER), _rows(SSM_INNER)],
        out_specs=[_rows(SSM_INNER), _rows(SSM_INNER), _vec(SSM_INNER)],
        out_shape=[jax.ShapeDtypeStruct((s, SSM_INNER), F32), jax.ShapeDtypeStruct((s, SSM_INNER), BF16),
                   jax.ShapeDtypeStruct((1, SSM_INNER), F32)],
        compiler_params=_cp("arbitrary"))(y, z, w, do)


def _mix_fwd(gl, bg, attn_d, ssm_d):
    s = gl.shape[0]
    d = D_MODEL

    def body(gl_ref, bg_ref, a_ref, m_ref, o_ref):
        g0 = _sigmoid(gl_ref[:, :d] + bg_ref[:, :d])
        g1 = _sigmoid(gl_ref[:, d:] + bg_ref[:, d:])
        o_ref[...] = (g0 * a_ref[...] + g1 * m_ref[...]).astype(o_ref.dtype)

    return pl.pallas_call(body, name="mix_fwd", grid=(s // ROWS,),
                          in_specs=[_rows(2 * d), _vec(2 * d), _rows(d), _rows(d)], out_specs=_rows(d),
                          out_shape=jax.ShapeDtypeStruct((s, d), BF16), compiler_params=_cp("parallel"))(
        gl, bg, attn_d, ssm_d)


def _mix_bwd(gl, bg, attn_d, ssm_d, dmix):
    s = gl.shape[0]
    d = D_MODEL

    def body(gl_ref, bg_ref, a_ref, m_ref, dm_ref, da_ref, ds_ref, dg_ref, db_ref):
        @pl.when(pl.program_id(0) == 0)
        def _():
            db_ref[...] = jnp.zeros_like(db_ref)

        g0 = _sigmoid(gl_ref[:, :d] + bg_ref[:, :d])
        g1 = _sigmoid(gl_ref[:, d:] + bg_ref[:, d:])
        dm = dm_ref[...].astype(F32)
        da_ref[...] = (dm * g0).astype(da_ref.dtype)
        ds_ref[...] = (dm * g1).astype(ds_ref.dtype)
        dl0 = dm * a_ref[...] * g0 * (1.0 - g0)
        dl1 = dm * m_ref[...] * g1 * (1.0 - g1)
        dg_ref[:, :d] = dl0.astype(dg_ref.dtype)
        dg_ref[:, d:] = dl1.astype(dg_ref.dtype)
        db_ref[:, :d] += jnp.sum(dl0, axis=0, keepdims=True)
        db_ref[:, d:] += jnp.sum(dl1, axis=0, keepdims=True)

    return pl.pallas_call(
        body, name="mix_bwd", grid=(s // ROWS,),
        in_specs=[_rows(2 * d), _vec(2 * d), _rows(d), _rows(d), _rows(d)],
        out_specs=[_rows(d), _rows(d), _rows(2 * d), _vec(2 * d)],
        out_shape=[jax.ShapeDtypeStruct((s, d), BF16), jax.ShapeDtypeStruct((s, d), BF16),
                   jax.ShapeDtypeStruct((s, 2 * d), BF16), jax.ShapeDtypeStruct((1, 2 * d), F32)],
        compiler_params=_cp("arbitrary"))(gl, bg, attn_d, ssm_d, dmix)


def _ln_stats(p):
    mu = jnp.mean(p, axis=1, keepdims=True)
    c = p - mu
    rstd = lax.rsqrt(jnp.mean(c * c, axis=1, keepdims=True) + LN_EPS)
    return c * rstd, rstd


def _ln_bwd(dy, xhat, rstd, g):
    dxh = dy * g
    return rstd * (dxh - jnp.mean(dxh, axis=1, keepdims=True) - xhat * jnp.mean(dxh * xhat, axis=1, keepdims=True))


def _ln1_fwd(x, mixed, g, b):
    s, d = x.shape

    def body(x_ref, m_ref, g_ref, b_ref, o_ref):
        xhat, _ = _ln_stats(DEEPNORM_ALPHA * x_ref[...] + m_ref[...])
        o_ref[...] = xhat * g_ref[...] + b_ref[...]

    return pl.pallas_call(body, name="ln1_fwd", grid=(s // ROWS,), in_specs=[_rows(d), _rows(d), _vec(d), _vec(d)],
                          out_specs=_rows(d), out_shape=jax.ShapeDtypeStruct((s, d), F32),
                          compiler_params=_cp("parallel"))(x, mixed, g, b)


def _ln2_loss(x1, h, target, g, b):
    s, d = x1.shape

    def body(x_ref, h_ref, t_ref, g_ref, b_ref, dp_ref, loss_ref, dg_ref, db_ref):
        @pl.when(pl.program_id(0) == 0)
        def _():
            loss_ref[...] = jnp.zeros_like(loss_ref)
            dg_ref[...] = jnp.zeros_like(dg_ref)
            db_ref[...] = jnp.zeros_like(db_ref)

        xhat, rstd = _ln_stats(DEEPNORM_ALPHA * x_ref[...] + h_ref[...])
        err = xhat * g_ref[...] + b_ref[...] - t_ref[...]
        part = 0.5 * jnp.sum(jnp.mean(err * err, axis=1, keepdims=True), axis=0, keepdims=True)
        loss_ref[...] += jnp.broadcast_to(part, loss_ref.shape)
        dy = err * (1.0 / d)
        dg_ref[...] += jnp.sum(dy * xhat, axis=0, keepdims=True)
        db_ref[...] += jnp.sum(dy, axis=0, keepdims=True)
        dp_ref[...] = _ln_bwd(dy, xhat, rstd, g_ref[...])

    return pl.pallas_call(
        body, name="ln2_loss", grid=(s // ROWS,), in_specs=[_rows(d), _rows(d), _rows(d), _vec(d), _vec(d)],
        out_specs=[_rows(d), _vec(LANES), _vec(d), _vec(d)],
        out_shape=[jax.ShapeDtypeStruct((s, d), F32), jax.ShapeDtypeStruct((1, LANES), F32),
                   jax.ShapeDtypeStruct((1, d), F32), jax.ShapeDtypeStruct((1, d), F32)],
        compiler_params=_cp("arbitrary"))(x1, h, target, g, b)


def _ln1_bwd(x, mixed, g, dpre2, dffn):
    s, d = x.shape

    def body(x_ref, m_ref, g_ref, d2_ref, df_ref, dp_ref, dr_ref, dg_ref, db_ref):
        @pl.when(pl.program_id(0) == 0)
        def _():
            dg_ref[...] = jnp.zeros_like(dg_ref)
            db_ref[...] = jnp.zeros_like(db_ref)

        xhat, rstd = _ln_stats(DEEPNORM_ALPHA * x_ref[...] + m_ref[...])
        dy = DEEPNORM_ALPHA * d2_ref[...] + df_ref[...]
        dg_ref[...] += jnp.sum(dy * xhat, axis=0, keepdims=True)
        db_ref[...] += jnp.sum(dy, axis=0, keepdims=True)
        dp = _ln_bwd(dy, xhat, rstd, g_ref[...])
        dp_ref[...] = dp
        dr_ref[...] = DEEPNORM_ALPHA * dp

    return pl.pallas_call(
        body, name="ln1_bwd", grid=(s // ROWS,), in_specs=[_rows(d), _rows(d), _vec(d), _rows(d), _rows(d)],
        out_specs=[_rows(d), _rows(d), _vec(d), _vec(d)],
        out_shape=[jax.ShapeDtypeStruct((s, d), F32), jax.ShapeDtypeStruct((s, d), F32),
                   jax.ShapeDtypeStruct((1, d), F32), jax.ShapeDtypeStruct((1, d), F32)],
        compiler_params=_cp("arbitrary"))(x, mixed, g, dpre2, dffn)


def _swiglu_fwd(gu):
    s = gu.shape[0]
    f = FFN_HIDDEN

    def body(g_ref, u_ref, o_ref):
        gg = g_ref[...].astype(F32)
        o_ref[...] = (gg * _sigmoid(gg) * u_ref[...].astype(F32)).astype(o_ref.dtype)

    return pl.pallas_call(
        body, name="swiglu_fwd", grid=(s // ROWS,),
        in_specs=[pl.BlockSpec((ROWS, f), lambda i: (i, 0)), pl.BlockSpec((ROWS, f), lambda i: (i, 1))],
        out_specs=_rows(f), out_shape=jax.ShapeDtypeStruct((s, f), BF16), compiler_params=_cp("parallel"))(gu, gu)


def _swiglu_bwd(gu, dact):
    s = gu.shape[0]
    f = FFN_HIDDEN

    def body(g_ref, u_ref, d_ref, o_ref):
        sg, dsg = _silu_and_grad(g_ref[...].astype(F32))
        dd = d_ref[...].astype(F32)
        o_ref[:, :f] = (dd * u_ref[...].astype(F32) * dsg).astype(o_ref.dtype)
        o_ref[:, f:] = (dd * sg).astype(o_ref.dtype)

    return pl.pallas_call(
        body, name="swiglu_bwd", grid=(s // ROWS,),
        in_specs=[pl.BlockSpec((ROWS, f), lambda i: (i, 0)), pl.BlockSpec((ROWS, f), lambda i: (i, 1)), _rows(f)],
        out_specs=_rows(2 * f), out_shape=jax.ShapeDtypeStruct((s, 2 * f), BF16),
        compiler_params=_cp("parallel"))(gu, gu, dact)


def _peer(k):
    x, y, c = lax.axis_index("x"), lax.axis_index("y"), lax.axis_index("c")
    kx, ky, kc = (k >> 2) & 1, (k >> 1) & 1, k & 1
    px = (1 - x) if kx else x
    py = (1 - y) if ky else y
    pc = (1 - c) if kc else c
    return (px, py, pc), 4 * px + 2 * py + pc


def _my_index():
    return 4 * lax.axis_index("x") + 2 * lax.axis_index("y") + lax.axis_index("c")


def _comm_copies(ins, outs, sems, scatter):
    send_sems, recv_sems, local_sems = sems
    me = _my_index()
    copies = [pltpu.make_async_copy(ins[t].at[me] if scatter else ins[t], outs[t].at[me], local_sems.at[t])
              for t in range(len(ins))]
    for k in range(1, N_DEV):
        peer, pidx = _peer(k)
        for t in range(len(ins)):
            copies.append(pltpu.make_async_remote_copy(
                src_ref=ins[t].at[pidx] if scatter else ins[t], dst_ref=outs[t].at[me],
                send_sem=send_sems.at[t, k - 1], recv_sem=recv_sems.at[t, k - 1], device_id=peer,
                device_id_type=pl.DeviceIdType.MESH))
    return copies


def _comm_sems(n):
    return [pltpu.SemaphoreType.DMA((n, N_DEV - 1)), pltpu.SemaphoreType.DMA((n, N_DEV - 1)),
            pltpu.SemaphoreType.DMA((n,))]


def _comm_out_shapes(parts, scatter):
    return [jax.ShapeDtypeStruct(p.shape if scatter else (N_DEV,) + p.shape, p.dtype) for p in parts]


def _all_gather(parts):
    n = len(parts)

    def body(*refs):
        ins, outs = refs[:n], refs[n:2 * n]
        send_sems, recv_sems, local_sems = refs[2 * n:]
        x, y, c = lax.axis_index("x"), lax.axis_index("y"), lax.axis_index("c")
        me, sibling = (x, y, c), (x, y, 1 - c)
        chips = [(1 - x, y), (x, 1 - y), (1 - x, 1 - y)]

        def copy(t, k, block, to, src=None):
            dst = outs[t].at[4 * block[0] + 2 * block[1] + block[2]]
            return pltpu.make_async_remote_copy(
                src_ref=dst if src is None else src, dst_ref=dst, send_sem=send_sems.at[t, k],
                recv_sem=recv_sems.at[t, k], device_id=to, device_id_type=pl.DeviceIdType.MESH)

        mine = [pltpu.make_async_copy(ins[t], outs[t].at[_my_index()], local_sems.at[t]) for t in range(n)]
        for cp in mine:
            cp.start()
        first = [copy(t, 0, me, sibling, src=ins[t]) for t in range(n)]
        first += [copy(t, 1 + j, me, (*chip, c), src=ins[t]) for j, chip in enumerate(chips) for t in range(n)]
        for cp in first:
            cp.start()
        passed = []
        for j, chip in enumerate(chips):
            for t in range(n):
                copy(t, 1 + j, (*chip, c), me).wait_recv()
                passed.append(copy(t, 4 + j, (*chip, c), sibling))
                passed[-1].start()
        for t in range(n):
            copy(t, 0, sibling, me).wait_recv()
            for j, chip in enumerate(chips):
                copy(t, 4 + j, (*chip, 1 - c), me).wait_recv()
        for cp in first + passed:
            cp.wait_send()
        for cp in mine:
            cp.wait()

    anyspec = pl.BlockSpec(memory_space=pl.ANY)
    return pl.pallas_call(body, name="all_gather", in_specs=[anyspec] * n, out_specs=[anyspec] * n,
                          out_shape=_comm_out_shapes(parts, False), scratch_shapes=_comm_sems(n))(*parts)


def _remote_scatter_copies(ins, lands, send_sems, recv_sems):
    me = _my_index()
    copies = []
    for k in range(1, N_DEV):
        peer, pidx = _peer(k)
        for t in range(len(ins)):
            copies.append(pltpu.make_async_remote_copy(
                src_ref=ins[t].at[pidx], dst_ref=lands[t].at[me], send_sem=send_sems.at[t * (N_DEV - 1) + k - 1],
                recv_sem=recv_sems.at[t * (N_DEV - 1) + k - 1], device_id=peer, device_id_type=pl.DeviceIdType.MESH))
    return copies


def _landing_zones(parts):
    me = _my_index()
    return [jnp.where(lax.broadcasted_iota(jnp.int32, p.shape, 0) == me, p, jnp.zeros_like(p)) for p in parts]


_HBM = pl.BlockSpec(memory_space=pltpu.HBM)
_SEM = pl.BlockSpec(memory_space=pltpu.SEMAPHORE)


def _exchange_start(parts, lands):
    n = len(parts)

    def body(*refs):
        ins, lnd, send_sems, recv_sems, token = refs[:n], refs[n:2 * n], refs[2 * n], refs[2 * n + 1], refs[-1]
        for cp in _remote_scatter_copies(ins, lnd, send_sems, recv_sems):
            cp.start()
        token[...] = jnp.zeros_like(token)

    hbm = [pltpu.HBM(p.shape, p.dtype) for p in parts]
    outs = pl.pallas_call(
        body, name="exchange_start",
        out_shape=[pltpu.SemaphoreType.DMA((n * (N_DEV - 1),)), pltpu.SemaphoreType.DMA((n * (N_DEV - 1),))] + hbm + hbm
        + [jax.ShapeDtypeStruct((8, LANES), F32)],
        in_specs=[_HBM] * (2 * n), out_specs=[_SEM, _SEM] + [_HBM] * (2 * n) + [pl.BlockSpec(memory_space=pltpu.VMEM)],
        input_output_aliases={t: 2 + t for t in range(2 * n)},
        compiler_params=pltpu.CompilerParams(has_side_effects=pltpu.SideEffectType.DATAFLOW_SIDE_EFFECTING),
    )(*[pltpu.with_memory_space_constraint(p, pltpu.HBM) for p in list(parts) + list(lands)])
    return outs[0], outs[1], list(outs[2:2 + n]), list(outs[2 + n:2 + 2 * n]), outs[-1]


def _exchange_wait(send_sems, recv_sems, parts, lands, after):
    n = len(parts)

    def body(*refs):
        ins, lnd, send_sems, recv_sems = refs[:n], refs[n:2 * n], refs[2 * n], refs[2 * n + 1]
        for cp in _remote_scatter_copies(ins, lnd, send_sems, recv_sems):
            cp.wait_send()
            cp.wait_recv()

    hbm = [pltpu.HBM(p.shape, p.dtype) for p in parts]
    outs = pl.pallas_call(
        body, name="exchange_wait", out_shape=hbm + hbm,
        in_specs=[_HBM] * (2 * n) + [_SEM, _SEM, pl.BlockSpec(memory_space=pl.ANY)], out_specs=[_HBM] * (2 * n),
        input_output_aliases={t: t for t in range(2 * n)},
        compiler_params=pltpu.CompilerParams(has_side_effects=pltpu.SideEffectType.DATAFLOW_SIDE_EFFECTING),
    )(*parts, *lands, send_sems, recv_sems, after)
    return list(outs[n:])


def _adamw(recv, w, m, v, name):
    _, r, c = w.shape
    br = _tile(r, 128)
    c1 = 1.0 / (1.0 - ADAM_B1 ** ADAM_STEP)
    c2 = 1.0 / (1.0 - ADAM_B2 ** ADAM_STEP)

    def body(r_ref, w_ref, m_ref, v_ref, g_ref, d_ref, mo_ref, vo_ref):
        g = r_ref[0].astype(F32)
        for k in range(1, N_DEV):
            g = g + r_ref[k].astype(F32)
        mn = ADAM_B1 * m_ref[0] + (1.0 - ADAM_B1) * g
        vn = ADAM_B2 * v_ref[0] + (1.0 - ADAM_B2) * (g * g)
        g_ref[0] = g
        mo_ref[0] = mn
        vo_ref[0] = vn
        d_ref[0] = -ADAM_LR * ((mn * c1) / (jnp.sqrt(vn * c2) + ADAM_EPS) + ADAM_WD * w_ref[0])

    blk = pl.BlockSpec((1, br, c), lambda i: (0, i, 0))
    return pl.pallas_call(
        body, name=name, grid=(r // br,),
        in_specs=[pl.BlockSpec((N_DEV, br, c), lambda i: (0, i, 0)), blk, blk, blk],
        out_specs=[blk] * 4, out_shape=[jax.ShapeDtypeStruct((1, r, c), F32)] * 4,
        compiler_params=_cp("parallel"))(recv, w, m, v)


def _lane_row(pairs):
    row = jnp.zeros((LANES,), F32)
    for lane0, vec in pairs:
        row = lax.dynamic_update_slice(row, vec.astype(F32), (lane0,))
    return row.reshape(1, LANES)


def _stage_in(x, wts, small):
    s = x.shape[0]
    a = -jnp.exp(small["a_log"])
    bias_row = _lane_row([(DT_LANE0, small["dt_bias"]), (F_LANE0, small["b_forget"])])
    a_row = _lane_row([(DT_LANE0, a)])
    conv_b = small["conv_b"].reshape(1, -1)
    norm_w = small["ssm_norm_w"].reshape(1, -1)
    bg = small["b_gates"].reshape(1, -1)
    g1, b1 = small["ln1_g"].reshape(1, -1), small["ln1_b"].reshape(1, -1)
    g2, b2 = small["ln2_g"].reshape(1, -1), small["ln2_b"].reshape(1, -1)
    d_skip = small["d_skip"]
    xb = x.astype(BF16)

    qkv = _mm(xb, wts["qkv"], out_dtype=BF16, name="f_qkv")
    z = _mm(xb, wts["z"], out_dtype=BF16, name="f_z")
    xbc = _mm(xb, wts["xbc"], name="f_xbc")
    gl = _mm(xb, wts["gate"], out_dtype=BF16, name="f_gate")
    fd = _mm(xb, wts["fd"], name="f_fd")
    dt_c, ac_c, cf_c, dt_r, ac_r, cf_r = _stats_fwd(fd, bias_row, a_row)
    bk = _att_blocks(s)[1]
    ck4 = cf_r[F_LANE0:F_LANE0 + ATT_HEADS].reshape(N_HP, HP, s // bk, bk)
    return dict(locals())


def _stage_mid(c, attn, lse, wts, target):
    x, xb, qkv, z, xbc, gl, fd, ck4 = (c[k] for k in ("x", "xb", "qkv", "z", "xbc", "gl", "fd", "ck4"))
    dt_c, ac_c, dt_r, ac_r, a_row, bias_row = (c[k] for k in ("dt_c", "ac_c", "dt_r", "ac_r", "a_row", "bias_row"))
    conv_b, norm_w, bg, g1, b1, g2, b2, d_skip = (c[k] for k in ("conv_b", "norm_w", "bg", "g1", "b1", "g2", "b2",
                                                                "d_skip"))
    conv_w = c["wts"]["conv"]
    attn_d = _mm(attn, wts["pa"], out_dtype=BF16, name="f_pa")
    xact = _conv_fwd(xbc, conv_w, conv_b)
    dsk_pair = jnp.repeat(d_skip, SSM_HEAD_DIM).reshape(N_PAIR, LANES)
    y, hprev = _ssd_pair_fwd(xact, ac_c, dt_r, ac_r, dsk_pair)
    ssm = _gnorm_fwd(y, z, norm_w)
    ssm_d = _mm(ssm, wts["ps"], out_dtype=BF16, name="f_ps")
    mix = _mix_fwd(gl, bg, attn_d, ssm_d)
    mixed = _mm(mix, wts["out"], name="f_out")
    x1 = _ln1_fwd(x, mixed, g1, b1)
    gu = _mm(x1, wts["gu"], out_dtype=BF16, name="f_gu")
    act = _swiglu_fwd(gu)
    h = _mm(act, wts["down"], name="f_down")
    dpre2, loss_row, dg2, db2 = _ln2_loss(x1, h, target, g2, b2)

    d_act = _mm(dpre2, wts["down"], tb=True, out_dtype=BF16, name="b_down_x")
    dw_down = _mm(act, dpre2, ta=True, name="b_down_w")
    dgu = _swiglu_bwd(gu, d_act)
    dffn = _mm(dgu, wts["gu"], tb=True, name="b_gu_x")
    dw_gu = _mm(x1, dgu, ta=True, name="b_gu_w")
    dpre1, dxr, dg1, db1 = _ln1_bwd(x, mixed, g1, dpre2, dffn)
    dmix = _mm(dpre1, wts["out"], tb=True, out_dtype=BF16, name="b_out_x")
    dw_out = _mm(mix, dpre1, ta=True, name="b_out_w")
    dattn_d, dssm_d, dgl, dbg = _mix_bwd(gl, bg, attn_d, ssm_d, dmix)
    dssm = _mm(dssm_d, wts["ps"], tb=True, out_dtype=BF16, name="b_ps_x")
    dw_ps = _mm(ssm, dssm_d, ta=True, name="b_ps_w")
    dattn = _mm(dattn_d, wts["pa"], tb=True, name="b_pa_x")
    dw_pa = _mm(attn, dattn_d, ta=True, name="b_pa_w")
    dy, dz, dnw = _gnorm_bwd(y, z, norm_w, dssm)
    dxact, ddt, da_row, dds_pair = _ssd_pair_bwd(xact, dt_c, ac_c, dt_r, ac_r, hprev, dy, dsk_pair, a_row)
    dds = dds_pair.reshape(SSM_HEADS, SSM_HEAD_DIM).sum(axis=1)
    dpre_c, dconv_w, dconv_b = _conv_bwd_pre(xbc, conv_w, conv_b, dxact)
    dxbc = _conv_bwd_in(dpre_c, conv_w)
    st, do_b = _att_prep(dattn, attn, lse)
    late = dict(pa=dw_pa, ps=dw_ps, out=dw_out, gu=dw_gu, down=dw_down)
    keep = ("st", "do_b", "ddt", "dxr", "dz", "dxbc", "dgl", "dconv_w", "dconv_b", "da_row", "dds", "dnw", "dbg",
            "dg1", "db1", "dg2", "db2", "loss_row")
    loc = locals()
    return {**c, **{k: loc[k] for k in keep}}, late


def _stage_out_w(c, att_grads):
    dq, dk, dv, dck, dcq = att_grads
    xb, fd, bias_row, ddt, dz, dxbc, dgl = (c[k] for k in ("xb", "fd", "bias_row", "ddt", "dz", "dxbc", "dgl"))
    s, a = xb.shape[0], c["a"]
    dck_rows = jnp.zeros((LANES, s), F32).at[F_LANE0:F_LANE0 + ATT_HEADS].set(dck.reshape(ATT_HEADS, s))
    dcq_cols = jnp.zeros((s, LANES), F32).at[:, F_LANE0:F_LANE0 + ATT_HEADS].set(dcq[:, ::ATT_HEAD_DIM])
    dfd, dbias = _stats_bwd(fd, bias_row, ddt, dck_rows, dcq_cols)
    dproj = (dq, dk, dv, dz, dxbc, dgl, dfd)
    dw_in = [_mm(xb, g_, ta=True, tb=(i == 0), name=f"b_in_w{i}") for i, g_ in enumerate(dproj)]
    grads = dict(q=dw_in[0], k=dw_in[1], v=dw_in[2], z=dw_in[3], xbc=dw_in[4], gate=dw_in[5], fd=dw_in[6],
                 conv=c["dconv_w"])
    small_g = dict(
        b_forget=dbias[0, F_LANE0:F_LANE0 + ATT_HEADS], conv_b=c["dconv_b"][0], dt_bias=dbias[0, :SSM_HEADS],
        a_log=c["da_row"][0, :SSM_HEADS] * a, d_skip=c["dds"], ssm_norm_w=c["dnw"][0], b_gates=c["dbg"][0],
        ln1_g=c["dg1"][0], ln1_b=c["db1"][0], ln2_g=c["dg2"][0], ln2_b=c["db2"][0])
    return c["loss_row"][0, 0], grads, small_g, dproj


def _stage_out_x(c, dproj, token):
    wts, d = c["wts"], D_MODEL
    wq = wts["qkv"][:, :d] + token.astype(BF16)
    wk, wv = wts["qkv"][:, d:2 * d], wts["qkv"][:, 2 * d:]
    dx = c["dxr"]
    for i, (g_, w_) in enumerate(zip(dproj, (wq, wk, wv, wts["z"], wts["xbc"], wts["gate"], wts["fd"]))):
        dx = _mm(g_, w_, ta=(i == 0), tb=True, add=dx, name=f"b_in_x{i}")
    return dx


BIG = ("w_in", "w_proj_attn", "w_proj_ssm", "w_out", "w_ffn_gate", "w_ffn_up", "w_ffn_down", "conv_w")
EARLY = ("w_in", "conv_w")
LATE = ("w_proj_attn", "w_proj_ssm", "w_out", "w_ffn_gate", "w_ffn_up", "w_ffn_down")
SMALL = ("b_forget", "conv_b", "dt_bias", "a_log", "d_skip", "ssm_norm_w", "b_gates", "ln1_g", "ln1_b", "ln2_g",
         "ln2_b")
SMALL_ROWS = 96
IN_SHARD = IN_WIDTH // N_DEV
IN_SEGMENTS = (("q", 0, 1024), ("k", 1024, 1024), ("v", 2048, 1024), ("f", 3072, ATT_HEADS), ("z", 3088, SSM_INNER),
               ("xbc", 5136, SSM_CONV_DIM), ("dt", 8208, SSM_HEADS), ("gate", 8240, 2 * D_MODEL))


def _cols_from_shards(shards, lo, hi):
    w = shards[0].shape[1]
    pieces = []
    for j in range(len(shards)):
        a, b = max(lo, j * w), min(hi, (j + 1) * w)
        if a < b:
            pieces.append(shards[j][:, a - j * w:b - j * w])
    return pieces[0] if len(pieces) == 1 else jnp.concatenate(pieces, axis=1)


def _shards_from_parts(parts, width):
    shards = []
    for j in range(N_DEV):
        lo, hi = j * width, (j + 1) * width
        pieces = []
        for mat, c0 in parts:
            a, b = max(lo, c0), min(hi, c0 + mat.shape[1])
            if a < b:
                pieces.append(mat[:, a - c0:b - c0])
        shards.append(pieces[0] if len(pieces) == 1 else jnp.concatenate(pieces, axis=1))
    return shards


def _pack_small(vals):
    flat = jnp.concatenate([vals[n].reshape(-1) for n in SMALL])
    return jnp.pad(flat, (0, SMALL_ROWS * LANES - flat.shape[0])).reshape(SMALL_ROWS, LANES)


def _unpack_small(pack, shapes):
    flat = pack.reshape(-1)
    out, off = {}, 0
    for n in SMALL:
        sz = math.prod(shapes[n])
        out[n] = flat[off:off + sz].reshape(shapes[n])
        off += sz
    return out


def kernel(x, w_in, b_forget, conv_w, conv_b, dt_bias, a_log, d_skip, ssm_norm_w, w_proj_attn, w_proj_ssm, b_gates, w_out, ln1_g, ln1_b, w_ffn_gate, w_ffn_up, w_ffn_down, ln2_g, ln2_b, loss_target, m_w_in, m_b_forget, m_conv_w, m_conv_b, m_dt_bias, m_a_log, m_d_skip, m_ssm_norm_w, m_w_proj_attn, m_w_proj_ssm, m_b_gates, m_w_out, m_ln1_g, m_ln1_b, m_w_ffn_gate, m_w_ffn_up, m_w_ffn_down, m_ln2_g, m_ln2_b, v_w_in, v_b_forget, v_conv_w, v_conv_b, v_dt_bias, v_a_log, v_d_skip, v_ssm_norm_w, v_w_proj_attn, v_w_proj_ssm, v_b_gates, v_w_out, v_ln1_g, v_ln1_b, v_w_ffn_gate, v_w_ffn_up, v_w_ffn_down, v_ln2_g, v_ln2_b):
    args = dict(locals())
    d, f = D_MODEL, FFN_HIDDEN
    big_w = {n: args[n][0] for n in BIG}
    small_w = {n: args[n][0] for n in SMALL}
    big_shapes = {n: args[n].shape for n in BIG}
    small_shapes = {n: args[n].shape for n in SMALL}

    early = dict(zip(EARLY, _all_gather([big_w["w_in"].astype(BF16), big_w["conv_w"]])))
    in_shards = [early["w_in"][j] for j in range(N_DEV)]
    seg = {n: _cols_from_shards(in_shards, c0, c0 + w) for n, c0, w in IN_SEGMENTS}
    wfd = jnp.concatenate([seg["dt"], seg["f"], jnp.zeros((d, LANES - SSM_HEADS - ATT_HEADS), BF16)], axis=1)
    wts = dict(qkv=jnp.concatenate([seg["q"], seg["k"], seg["v"]], axis=1), z=seg["z"], xbc=seg["xbc"],
               gate=seg["gate"], fd=wfd, conv=jnp.concatenate([early["conv_w"][j] for j in range(N_DEV)], axis=1))

    ctx = _stage_in(x[0], wts, small_w)
    attn, lse, gathered = _attention_fwd(ctx["qkv"], ctx["ck4"], [big_w[n].astype(BF16) for n in LATE])
    full = dict(zip(LATE, gathered))
    late_w = dict(
        pa=full["w_proj_attn"].reshape(d, d), ps=full["w_proj_ssm"].reshape(SSM_INNER, d),
        out=full["w_out"].reshape(d, d),
        gu=jnp.concatenate([full["w_ffn_gate"][j] for j in range(N_DEV)]
                           + [full["w_ffn_up"][j] for j in range(N_DEV)], axis=1),
        down=full["w_ffn_down"].reshape(f, d))
    ctx, gl = _stage_mid(ctx, attn, lse, late_w, loss_target[0])
    late_dest = dict(
        w_ffn_gate=jnp.stack([s_.astype(BF16) for s_ in _shards_from_parts([(gl["gu"][:, :f], 0)], f // N_DEV)]),
        w_ffn_up=jnp.stack([s_.astype(BF16) for s_ in _shards_from_parts([(gl["gu"][:, f:], 0)], f // N_DEV)]))
    for n, key in (("w_proj_attn", "pa"), ("w_proj_ssm", "ps"), ("w_out", "out"), ("w_ffn_down", "down")):
        late_dest[n] = gl[key].astype(BF16).reshape((N_DEV,) + big_shapes[n][1:])
    att_grads, late_recv = _attention_bwd(ctx["qkv"], ctx["ck4"], ctx["st"], ctx["do_b"], [late_dest[n] for n in LATE])
    loss_part, g, small_g, dproj = _stage_out_w(ctx, att_grads)
    loss = lax.psum(loss_part, ("x", "y", "c"))

    gfd = g["fd"]
    in_parts = dict(q=g["q"], k=g["k"], v=g["v"], f=gfd[:, F_LANE0:F_LANE0 + ATT_HEADS], z=g["z"], xbc=g["xbc"],
                    dt=gfd[:, DT_LANE0:DT_LANE0 + SSM_HEADS], gate=g["gate"])
    win_dest = jnp.stack([s_.astype(BF16) for s_ in
                          _shards_from_parts([(in_parts[n], c0) for n, c0, _ in IN_SEGMENTS], IN_SHARD)])
    conv_dest = jnp.stack(_shards_from_parts([(g["conv"], 0)], SSM_CONV_DIM // N_DEV))
    small_pack = _pack_small(small_g)
    last_parts = [win_dest, conv_dest, jnp.broadcast_to(small_pack, (N_DEV,) + small_pack.shape)]
    send_sems, recv_sems, parts_thru, lands_thru, token = _exchange_start(last_parts, _landing_zones(last_parts))
    grad_x = _stage_out_x(ctx, dproj, token[0, 0])
    early_recv = _exchange_wait(send_sems, recv_sems, parts_thru, lands_thru, grad_x)
    recv = dict(zip(LATE, late_recv))
    recv["w_in"], recv["conv_w"] = early_recv[0], early_recv[1]

    outs = {}
    for n in BIG:
        outs[n] = _adamw(recv[n], args[n], args["m_" + n], args["v_" + n], name="adamw_" + n)
    small4 = _adamw(early_recv[2], _pack_small(small_w)[None], _pack_small({n: args["m_" + n][0] for n in SMALL})[None],
                    _pack_small({n: args["v_" + n][0] for n in SMALL})[None], name="adamw_small")
    small_out = [_unpack_small(p, small_shapes) for p in small4]
    for n in SMALL:
        outs[n] = [so[n] for so in small_out]

    order = ("w_in", "b_forget", "conv_w", "conv_b", "dt_bias", "a_log", "d_skip", "ssm_norm_w", "w_proj_attn",
             "w_proj_ssm", "b_gates", "w_out", "ln1_g", "ln1_b", "w_ffn_gate", "w_ffn_up", "w_ffn_down", "ln2_g",
             "ln2_b")
    res = [loss, grad_x[None]]
    for i in range(4):
        res += [outs[n][i] for n in order]
    return tuple(res)
```

```python
import functools
import math

import jax
import jax.numpy as jnp
from jax import lax
from jax.experimental import pallas as pl
from jax.experimental.pallas import tpu as pltpu

F32 = jnp.float32
BF16 = jnp.bfloat16

N_DEV = 8
D_MODEL = 1024
ATT_HEADS = 16
ATT_HEAD_DIM = 64
SSM_INNER = 2048
SSM_HEADS = 32
SSM_HEAD_DIM = 64
SSM_GROUPS = 4
SSM_HEADS_PER_GROUP = 8
SSM_STATE = 128
SSM_CONV = 4
SSM_CHUNK = 128
SSM_CONV_DIM = 3072
FFN_HIDDEN = 2816
IN_WIDTH = 10288
DEEPNORM_ALPHA = 2.0 ** 0.25
LN_EPS = 1e-5
RMS_EPS = 1e-5
ADAM_LR, ADAM_B1, ADAM_B2, ADAM_EPS, ADAM_WD, ADAM_STEP = 0.001, 0.9, 0.999, 1e-08, 0.01, 10
ATT_SCALE = 1.0 / math.sqrt(ATT_HEAD_DIM)

LANES = 128
VMEM_LIMIT = 56 * 1024 * 1024
NEG = -1e30

DT_LANE0 = 0
F_LANE0 = 32
HI = lax.Precision.HIGHEST


def _cp(*sem):
    return pltpu.CompilerParams(dimension_semantics=sem, vmem_limit_bytes=VMEM_LIMIT)


def _tile(n, cap=1408):
    for t in (1408, 1024, 512, 384, 256, 128):
        if t <= cap and n % t == 0:
            return t
    return n


def _sigmoid(x):
    return 1.0 / (1.0 + jnp.exp(-x))


def _mm(a, b, *, ta=False, tb=False, out_dtype=F32, add=None, name):
    m, k = (a.shape[1], a.shape[0]) if ta else a.shape
    n = b.shape[0] if tb else b.shape[1]
    assert (b.shape[1] if tb else b.shape[0]) == k
    tm, tn, tk = _tile(m), _tile(n), _tile(k)
    nk = k // tk
    dims = (((0,) if ta else (1,), (1,) if tb else (0,)), ((), ()))

    def body_single(*refs):
        a_ref, b_ref = refs[:2]
        r = lax.dot_general(a_ref[...].astype(BF16), b_ref[...].astype(BF16), dims, preferred_element_type=F32)
        if add is not None:
            r = r + refs[2][...]
        refs[-1][...] = r.astype(refs[-1].dtype)

    def body(*refs):
        if add is None:
            a_ref, b_ref, o_ref, acc_ref = refs
        else:
            a_ref, b_ref, c_ref, o_ref, acc_ref = refs
        kk = pl.program_id(2)

        @pl.when(kk == 0)
        def _():
            acc_ref[...] = jnp.zeros_like(acc_ref)

        acc_ref[...] += lax.dot_general(a_ref[...].astype(BF16), b_ref[...].astype(BF16), dims,
                                        preferred_element_type=F32)

        @pl.when(kk == nk - 1)
        def _():
            r = acc_ref[...]
            if add is not None:
                r = r + c_ref[...]
            o_ref[...] = r.astype(o_ref.dtype)

    a_spec = pl.BlockSpec((tk, tm), lambda i, j, kk: (kk, i)) if ta else pl.BlockSpec((tm, tk), lambda i, j, kk: (i, kk))
    b_spec = pl.BlockSpec((tn, tk), lambda i, j, kk: (j, kk)) if tb else pl.BlockSpec((tk, tn), lambda i, j, kk: (kk, j))
    o_spec = pl.BlockSpec((tm, tn), lambda i, j, kk: (i, j))
    in_specs, args = [a_spec, b_spec], [a, b]
    if add is not None:
        in_specs.append(o_spec)
        args.append(add)
    return pl.pallas_call(
        body_single if nk == 1 else body, name=name, grid=(m // tm, n // tn, nk), in_specs=in_specs, out_specs=o_spec,
        out_shape=jax.ShapeDtypeStruct((m, n), out_dtype),
        scratch_shapes=[] if nk == 1 else [pltpu.VMEM((tm, tn), F32)],
        compiler_params=_cp("parallel", "parallel", "arbitrary"),
    )(*args)


def _tri(n, lower=True):
    r = lax.broadcasted_iota(jnp.int32, (n, n), 0)
    c = lax.broadcasted_iota(jnp.int32, (n, n), 1)
    return jnp.where((r >= c) if lower else (c >= r), 1.0, 0.0).astype(F32)


def _stats_fwd(fd, bias_row, a_row):
    s = fd.shape[0]
    blk = SSM_CHUNK

    def body(fd_ref, bias_ref, a_ref, dt_ref, ac_ref, cf_ref, dtr_ref, acr_ref, cfr_ref, carry_ref):
        @pl.when(pl.program_id(0) == 0)
        def _():
            carry_ref[...] = jnp.zeros_like(carry_ref)

        v = fd_ref[...] + bias_ref[...]
        dt = jnp.maximum(v, 0.0) + jnp.log(1.0 + jnp.exp(-jnp.abs(v)))
        lf = jnp.minimum(v, 0.0) - jnp.log(1.0 + jnp.exp(-jnp.abs(v)))
        tri = _tri(blk)
        ac = jnp.dot(tri, dt * a_ref[...], precision=HI, preferred_element_type=F32)
        cf = jnp.dot(tri, lf, precision=HI, preferred_element_type=F32) + carry_ref[0:1, :]
        carry_ref[...] = carry_ref[...] + jnp.sum(lf, axis=0, keepdims=True)
        dt_ref[...] = dt
        ac_ref[...] = ac
        cf_ref[...] = cf
        dtr_ref[...] = dt.T
        acr_ref[...] = ac.T
        cfr_ref[...] = cf.T

    col = pl.BlockSpec((blk, LANES), lambda i: (i, 0))
    row = pl.BlockSpec((LANES, blk), lambda i: (0, i))
    vec = pl.BlockSpec((1, LANES), lambda i: (0, 0))
    return pl.pallas_call(
        body, name="stats_fwd", grid=(s // blk,), in_specs=[col, vec, vec],
        out_specs=[col, col, col, row, row, row],
        out_shape=[jax.ShapeDtypeStruct((s, LANES), F32)] * 3 + [jax.ShapeDtypeStruct((LANES, s), F32)] * 3,
        scratch_shapes=[pltpu.VMEM((8, LANES), F32)],
        compiler_params=_cp("arbitrary"),
    )(fd, bias_row, a_row)


def _stats_bwd(fd, bias_row, ddt, dcum_rows):
    s = fd.shape[0]
    blk = SSM_CHUNK
    nb = s // blk

    def body(fd_ref, bias_ref, ddt_ref, dck_ref, o_ref, db_ref, carry_ref):
        @pl.when(pl.program_id(0) == 0)
        def _():
            carry_ref[...] = jnp.zeros_like(carry_ref)
            db_ref[...] = jnp.zeros_like(db_ref)

        v = fd_ref[...] + bias_ref[...]
        dcum = dck_ref[...].T
        dlf = jnp.dot(_tri(blk, lower=False), dcum, precision=HI, preferred_element_type=F32) + carry_ref[0:1, :]
        carry_ref[...] = carry_ref[...] + jnp.sum(dcum, axis=0, keepdims=True)
        lane = lax.broadcasted_iota(jnp.int32, v.shape, 1)
        g = jnp.where(lane < F_LANE0, ddt_ref[...] * _sigmoid(v), dlf * _sigmoid(-v))
        g = jnp.where(lane < F_LANE0 + ATT_HEADS, g, 0.0)
        o_ref[...] = g.astype(o_ref.dtype)
        db_ref[...] += jnp.sum(g, axis=0, keepdims=True)

    col = pl.BlockSpec((blk, LANES), lambda i: (nb - 1 - i, 0))
    row = pl.BlockSpec((LANES, blk), lambda i: (0, nb - 1 - i))
    vec = pl.BlockSpec((1, LANES), lambda i: (0, 0))
    return pl.pallas_call(
        body, name="stats_bwd", grid=(nb,), in_specs=[col, vec, col, row], out_specs=[col, vec],
        out_shape=[jax.ShapeDtypeStruct((s, LANES), BF16), jax.ShapeDtypeStruct((1, LANES), F32)],
        scratch_shapes=[pltpu.VMEM((8, LANES), F32)],
        compiler_params=_cp("arbitrary"),
    )(fd, bias_row, ddt, dcum_rows)


HP = LANES // ATT_HEAD_DIM
N_HP = ATT_HEADS // HP


def _att_blocks(s):
    return (512, 1024) if s % 1024 == 0 and s >= 4096 else (64, 128)


_QK = (((1,), (1,)), ((), ()))
_HALF = ATT_HEAD_DIM // 2
_PAD = 16


def _head_cols(a):
    return slice(a * ATT_HEAD_DIM, (a + 1) * ATT_HEAD_DIM)


def _causal(shape, off):
    r = lax.broadcasted_iota(jnp.int32, shape, 0)
    c = lax.broadcasted_iota(jnp.int32, shape, 1)
    return c <= r + off


def _attention_fwd(qkv, ck4, gather_parts):
    s = qkv.shape[0]
    bk = _att_blocks(s)[1]
    bq = bk
    nq, nk = s // bq, s // bk
    n = len(gather_parts)

    def body(q_ref, k_ref, v_ref, ck_ref, *rest):
        comm_in, (o_ref, lse_ref), comm_out, sems = rest[:n], rest[n:n + 2], rest[n + 2:2 * n + 2], rest[2 * n + 2:]
        i = pl.program_id(1)
        if n:
            @pl.when((pl.program_id(0) == 0) & (i == 0))
            def _():
                for cp in _comm_copies(comm_in, comm_out, sems, False):
                    cp.start()

        n_full = (i * bq) // bk
        qs = [(q_ref[:, _head_cols(a)].astype(F32) * ATT_SCALE).astype(BF16) for a in range(HP)]

        upper_v = lax.broadcasted_iota(jnp.int32, (bk, LANES), 1) >= ATT_HEAD_DIM
        upper_q = lax.broadcasted_iota(jnp.int32, (bq, LANES), 1) >= ATT_HEAD_DIM

        def step(j, carry, off=None):
            ks = pl.ds(pl.multiple_of(j * bk, bk), bk)
            v_both = v_ref[ks, :]
            out = []
            for a in range(HP):
                m, acc = carry[a]
                sc = lax.dot_general(qs[a], k_ref[ks, _head_cols(a)], _QK, preferred_element_type=F32)
                sc = sc - ck_ref[0, a, pl.ds(j, 1), :]
                if off is not None:
                    sc = jnp.where(_causal(sc.shape, off), sc, NEG)
                m_new = jnp.maximum(m, jnp.max(sc, axis=1, keepdims=True))
                p = jnp.exp((sc - m_new).astype(BF16))
                v_aug = jnp.where(upper_v == (a == 1), v_both, jnp.ones_like(v_both))
                acc = jnp.exp(m - m_new) * acc + jnp.dot(p, v_aug, preferred_element_type=F32)
                out.append((m_new, acc))
            return tuple(out)

        init = tuple((jnp.full((bq, 1), NEG, F32), jnp.zeros((bq, LANES), F32)) for _ in range(HP))
        carry = lax.fori_loop(0, n_full, step, init)
        carry = step(n_full, carry, off=i * bq - n_full * bk)
        outs, lses = [], []
        for a in range(HP):
            m, acc = carry[a]
            l = pltpu.roll(acc, ATT_HEAD_DIM, 1)
            outs.append(acc / l)
            lses.append(m + jnp.log(l))
        o_ref[...] = jnp.where(upper_q, outs[1], outs[0])
        lse_ref[...] = jnp.where(upper_q, lses[1], lses[0])
        if n:
            @pl.when((pl.program_id(0) == N_HP - 1) & (i == nq - 1))
            def _():
                for cp in _comm_copies(comm_in, comm_out, sems, False):
                    cp.wait()

    q_spec = pl.BlockSpec((bq, LANES), lambda h, i: (i, h))
    anyspec = pl.BlockSpec(memory_space=pl.ANY)
    res = pl.pallas_call(
        body, name="att_fwd", grid=(N_HP, nq),
        in_specs=[q_spec, pl.BlockSpec((s, LANES), lambda h, i: (0, N_HP + h)),
                  pl.BlockSpec((s, LANES), lambda h, i: (0, 2 * N_HP + h)),
                  pl.BlockSpec((1, HP, nk, bk), lambda h, i: (h, 0, 0, 0))] + [anyspec] * n,
        out_specs=[q_spec, q_spec] + [anyspec] * n,
        out_shape=[jax.ShapeDtypeStruct((s, D_MODEL), F32)] * 2 + _comm_out_shapes(gather_parts, False),
        scratch_shapes=_comm_sems(n) if n else [],
        compiler_params=_cp("arbitrary", "arbitrary"),
    )(qkv, qkv, qkv, ck4, *gather_parts)
    return res[0], res[1], list(res[2:])


def _att_prep(do, o, lse_rep):
    s = do.shape[0]
    bs = _tile(s, 512)

    def body(do_ref, o_ref, lse_ref, st_ref, dob_ref):
        r = lax.broadcasted_iota(jnp.int32, (LANES, LANES), 0) // ATT_HEAD_DIM
        c = lax.broadcasted_iota(jnp.int32, (LANES, LANES), 1) // ATT_HEAD_DIM
        e = jnp.where(r == c, 1.0, 0.0).astype(F32)
        lane = lax.broadcasted_iota(jnp.int32, (bs, LANES), 1)
        for p in range(D_MODEL // LANES):
            cs = slice(p * LANES, (p + 1) * LANES)
            dd = do_ref[:, cs]
            delta = jnp.dot(dd * o_ref[:, cs], e, precision=HI, preferred_element_type=F32)
            st_ref[:, cs] = jnp.where(lane % ATT_HEAD_DIM < _HALF, lse_ref[:, cs], delta)
            dob_ref[:, cs] = dd.astype(BF16)

    spec = pl.BlockSpec((bs, D_MODEL), lambda i: (i, 0))
    return pl.pallas_call(body, name="att_prep", grid=(s // bs,), in_specs=[spec, spec, spec], out_specs=[spec, spec],
                          out_shape=[jax.ShapeDtypeStruct((s, D_MODEL), F32), jax.ShapeDtypeStruct((s, D_MODEL), BF16)],
                          compiler_params=_cp("parallel"))(do, o, lse_rep)


def _attention_bwd(qkv, ck4, st, do_b, exchange_parts):
    s = qkv.shape[0]
    bq, bk = _att_blocks(s)
    nq, nk, per = s // bq, s // bk, bk // bq
    _T = (((0,), (0,)), ((), ()))
    n = len(exchange_parts)

    def body(q_ref, k_ref, v_ref, ck_ref, st_ref, do_ref, *rest):
        comm_in, (dq_ref, dk_ref, dv_ref, dck_ref, dcq_ref) = rest[:n], rest[n:n + 5]
        comm_out, sems, (dk_acc, dv_acc) = rest[n + 5:2 * n + 5], rest[2 * n + 5:-2], rest[-2:]
        j = pl.program_id(1)
        if n:
            @pl.when((pl.program_id(0) == 0) & (j == 0))
            def _():
                for cp in _comm_copies(comm_in, comm_out, sems, True):
                    cp.start()

        @pl.when(j == 0)
        def _():
            dq_ref[...] = jnp.zeros_like(dq_ref)
            dcq_ref[...] = jnp.zeros_like(dcq_ref)

        dk_acc[...] = jnp.zeros_like(dk_acc)
        dv_acc[...] = jnp.zeros_like(dv_acc)

        ones_q, ones_k = jnp.ones((_PAD, bq), BF16), jnp.ones((_PAD, bk), BF16)
        k_t = [jnp.concatenate([k_ref[:, _head_cols(a)].T, ones_k], axis=0) for a in range(HP)]

        def step(i, off=None):
            rows = pl.ds(pl.multiple_of(i * bq, bq), bq)
            for a in range(HP):
                cs = _head_cols(a)
                q = (q_ref[rows, cs].astype(F32) * ATT_SCALE).astype(BF16)
                k = k_ref[:, cs]
                do_a = do_ref[rows, cs]
                sc = lax.dot_general(q, k, _QK, preferred_element_type=F32) - ck_ref[0, a, pl.ds(j, 1), :]
                if off is not None:
                    sc = jnp.where(_causal(sc.shape, off), sc, NEG)
                p = jnp.exp(sc - st_ref[rows, a * ATT_HEAD_DIM:a * ATT_HEAD_DIM + 1])
                dp = lax.dot_general(do_a, v_ref[:, cs], _QK, preferred_element_type=F32)
                ds = p * (dp - st_ref[rows, a * ATT_HEAD_DIM + _HALF:a * ATT_HEAD_DIM + _HALF + 1])
                ds_b = ds.astype(BF16)
                dv_acc[a] += jnp.dot(do_a.T, p.astype(BF16), preferred_element_type=F32)
                dk_acc[a] += jnp.dot(jnp.concatenate([q.T, ones_q], axis=0), ds_b, preferred_element_type=F32)
                dqs = lax.dot_general(k_t[a], ds_b, _QK, preferred_element_type=F32)
                dq_ref[cs, rows] += dqs[:ATT_HEAD_DIM] * ATT_SCALE
                dcq_ref[0, a, pl.ds(i, 1), :] += jnp.sum(dqs[ATT_HEAD_DIM:ATT_HEAD_DIM + 8], axis=0,
                                                         keepdims=True) * 0.125

        for t in range(per):
            step(j * per + t, off=t * bq)

        def full(i, c):
            step(i)
            return c

        lax.fori_loop((j + 1) * per, nq, full, 0)
        for a in range(HP):
            dk_ref[:, _head_cols(a)] = dk_acc[a, :ATT_HEAD_DIM].T.astype(dk_ref.dtype)
            dv_ref[:, _head_cols(a)] = dv_acc[a].T.astype(dv_ref.dtype)
            dck_ref[0, a, pl.ds(j, 1), :] = -dk_acc[a, ATT_HEAD_DIM:ATT_HEAD_DIM + 1]
        if n:
            @pl.when((pl.program_id(0) == N_HP - 1) & (j == nk - 1))
            def _():
                for cp in _comm_copies(comm_in, comm_out, sems, True):
                    cp.wait()

    res = pl.BlockSpec((s, LANES), lambda h, j: (0, h))
    ck_spec = pl.BlockSpec((1, HP, nk, bk), lambda h, j: (h, 0, 0, 0))
    kout = pl.BlockSpec((bk, LANES), lambda h, j: (j, h))
    anyspec = pl.BlockSpec(memory_space=pl.ANY)
    outs = pl.pallas_call(
        body, name="att_bwd", grid=(N_HP, nk),
        in_specs=[res, pl.BlockSpec((bk, LANES), lambda h, j: (j, N_HP + h)),
                  pl.BlockSpec((bk, LANES), lambda h, j: (j, 2 * N_HP + h)), ck_spec, res, res] + [anyspec] * n,
        out_specs=[pl.BlockSpec((LANES, s), lambda h, j: (h, 0)), kout, kout, ck_spec,
                   pl.BlockSpec((1, HP, nq, bq), lambda h, j: (h, 0, 0, 0))] + [anyspec] * n,
        out_shape=[jax.ShapeDtypeStruct((D_MODEL, s), F32), jax.ShapeDtypeStruct((s, D_MODEL), BF16),
                   jax.ShapeDtypeStruct((s, D_MODEL), BF16), jax.ShapeDtypeStruct((N_HP, HP, nk, bk), F32),
                   jax.ShapeDtypeStruct((N_HP, HP, nq, bq), F32)] + _comm_out_shapes(exchange_parts, True),
        scratch_shapes=(_comm_sems(n) if n else [])
        + [pltpu.VMEM((HP, ATT_HEAD_DIM + _PAD, bk), F32), pltpu.VMEM((HP, ATT_HEAD_DIM, bk), F32)],
        compiler_params=_cp("arbitrary", "arbitrary"),
    )(qkv, qkv, qkv, ck4, st, do_b, *exchange_parts)
    return outs[:5], list(outs[5:])


def _silu_and_grad(x):
    sg = _sigmoid(x)
    return x * sg, sg * (1.0 + x * (1.0 - sg))


def _conv_pre(cur, halo, w_ref, b_ref, first):
    halo = jnp.where(first, 0.0, halo)
    row = lax.broadcasted_iota(jnp.int32, cur.shape, 0)
    shifted = []
    for k in range(SSM_CONV):
        sh = SSM_CONV - 1 - k
        if sh == 0:
            shifted.append(cur)
            continue
        r = pltpu.roll(cur, sh, 0)
        hr = pltpu.roll(halo, sh, 0)
        top = jnp.where(row[0:8] < sh, hr, r[0:8])
        shifted.append(jnp.concatenate([top, r[8:]], axis=0))
    pre = b_ref[...] + sum(w_ref[k:k + 1, :] * shifted[k] for k in range(SSM_CONV))
    return pre, shifted


def _conv_specs(s, bs, bc):
    cur = pl.BlockSpec((bs, bc), lambda j, i: (i, j))
    halo = pl.BlockSpec((8, bc), lambda j, i: (jnp.maximum(i * (bs // 8) - 1, 0), j))
    w = pl.BlockSpec((SSM_CONV, bc), lambda j, i: (0, j))
    b = pl.BlockSpec((1, bc), lambda j, i: (0, j))
    return cur, halo, w, b


def _conv_fwd(xbc, w, b):
    s, c = xbc.shape
    bs, bc = _tile(s, 512), 1024

    def body(x_ref, h_ref, w_ref, b_ref, o_ref):
        pre, _ = _conv_pre(x_ref[...], h_ref[...], w_ref, b_ref, pl.program_id(1) == 0)
        o_ref[...] = pre * _sigmoid(pre)

    cur, halo, ws, bsp = _conv_specs(s, bs, bc)
    return pl.pallas_call(body, name="conv_fwd", grid=(c // bc, s // bs), in_specs=[cur, halo, ws, bsp],
                          out_specs=cur, out_shape=jax.ShapeDtypeStruct((s, c), F32),
                          compiler_params=_cp("parallel", "parallel"))(xbc, xbc, w, b)


def _conv_bwd_pre(xbc, w, b, dact):
    s, c = xbc.shape
    bs, bc = _tile(s, 512), 1024

    def body(x_ref, h_ref, w_ref, b_ref, g_ref, dp_ref, dw_ref, db_ref):
        @pl.when(pl.program_id(1) == 0)
        def _():
            dw_ref[...] = jnp.zeros_like(dw_ref)
            db_ref[...] = jnp.zeros_like(db_ref)

        pre, shifted = _conv_pre(x_ref[...], h_ref[...], w_ref, b_ref, pl.program_id(1) == 0)
        dpre = g_ref[...] * _silu_and_grad(pre)[1]
        dp_ref[...] = dpre
        db_ref[...] += jnp.sum(dpre, axis=0, keepdims=True)
        for k in range(SSM_CONV):
            dw_ref[k:k + 1, :] += jnp.sum(dpre * shifted[k], axis=0, keepdims=True)

    cur, halo, ws, bsp = _conv_specs(s, bs, bc)
    return pl.pallas_call(
        body, name="conv_bwd_pre", grid=(c // bc, s // bs), in_specs=[cur, halo, ws, bsp, cur],
        out_specs=[cur, ws, bsp],
        out_shape=[jax.ShapeDtypeStruct((s, c), F32), jax.ShapeDtypeStruct((SSM_CONV, c), F32),
                   jax.ShapeDtypeStruct((1, c), F32)],
        compiler_params=_cp("parallel", "arbitrary"))(xbc, xbc, w, b, dact)


def _conv_bwd_in(dpre, w):
    s, c = dpre.shape
    bs, bc = _tile(s, 512), 1024
    nb = s // bs

    def body(g_ref, n_ref, w_ref, o_ref):
        cur = g_ref[...]
        nxt = jnp.where(pl.program_id(1) == nb - 1, 0.0, n_ref[...])
        row = lax.broadcasted_iota(jnp.int32, cur.shape, 0)
        acc = w_ref[SSM_CONV - 1:SSM_CONV, :] * cur
        for sh in range(1, SSM_CONV):
            r = pltpu.roll(cur, bs - sh, 0)
            nr = pltpu.roll(nxt, 8 - sh, 0)
            bot = jnp.where(row[0:8] >= 8 - sh, nr, r[bs - 8:])
            acc = acc + w_ref[SSM_CONV - 1 - sh:SSM_CONV - sh, :] * jnp.concatenate([r[:bs - 8], bot], axis=0)
        o_ref[...] = acc.astype(o_ref.dtype)

    cur = pl.BlockSpec((bs, bc), lambda j, i: (i, j))
    nxt = pl.BlockSpec((8, bc), lambda j, i: (jnp.minimum((i + 1) * (bs // 8), s // 8 - 1), j))
    ws = pl.BlockSpec((SSM_CONV, bc), lambda j, i: (0, j))
    return pl.pallas_call(body, name="conv_bwd_in", grid=(c // bc, nb), in_specs=[cur, nxt, ws], out_specs=cur,
                          out_shape=jax.ShapeDtypeStruct((s, c), BF16),
                          compiler_params=_cp("parallel", "parallel"))(dpre, dpre, w)


def _dotT(a, b):
    return lax.dot_general(a.astype(BF16), b.astype(BF16), (((1,), (1,)), ((), ())), preferred_element_type=F32)


def _dot(a, b):
    return jnp.dot(a.astype(BF16), b.astype(BF16), preferred_element_type=F32)


N_PAIR = SSM_HEADS // HP
PAIRS_PER_GROUP = SSM_HEADS_PER_GROUP // HP


def _pair_consts():
    L = SSM_CHUNK
    lane = lax.broadcasted_iota(jnp.int32, (L, LANES), 1)
    lane1 = lax.broadcasted_iota(jnp.int32, (1, LANES), 1)
    li = lax.broadcasted_iota(jnp.int32, (L, L), 0)
    si = lax.broadcasted_iota(jnp.int32, (L, L), 1)
    return lane >= ATT_HEAD_DIM, lane1 >= ATT_HEAD_DIM, li, si


def _ssd_pair_fwd(xbc_act, ac_c, dt_r, ac_r, dsk_pair):
    s = xbc_act.shape[0]
    L, N, G = SSM_CHUNK, SSM_STATE, SSM_GROUPS
    nc = s // L

    def body(xbc_ref, ac_ref, dtr_ref, acr_ref, dsk_ref, y_ref, hp_ref, st_ref):
        @pl.when(pl.program_id(0) == 0)
        def _():
            st_ref[...] = jnp.zeros_like(st_ref)

        upper, up1, li, si = _pair_consts()
        for g in range(G):
            b_g = xbc_ref[:, SSM_INNER + g * N:SSM_INNER + (g + 1) * N]
            c_g = xbc_ref[:, SSM_INNER + G * N + g * N:SSM_INNER + G * N + (g + 1) * N]
            cb = _dotT(c_g, b_g)
            b_t = b_g.T
            for q in range(PAIRS_PER_GROUP):
                pp = g * PAIRS_PER_GROUP + q
                cols = slice(pp * LANES, (pp + 1) * LANES)
                xs = xbc_ref[:, cols]
                ht = st_ref[pp]
                hp_ref[0, pp] = ht
                y = dsk_ref[pp:pp + 1, :] * xs
                s_new = jnp.zeros((N, LANES), F32)
                ea, el = [], []
                for a in range(HP):
                    h = HP * pp + a
                    acol = jnp.broadcast_to(ac_ref[:, h:h + 1], (L, LANES))
                    arow, dtrow = acr_ref[h:h + 1, :], dtr_ref[h:h + 1, :]
                    alast = ac_ref[L - 1:L, h:h + 1]
                    decay = jnp.exp(jnp.where(li >= si, acol - arow, NEG))
                    xs_a = jnp.where(upper == (a == 1), xs, 0.0)
                    y = y + _dot(cb * decay * dtrow, xs_a)
                    s_new = s_new + _dot(b_t * (dtrow * jnp.exp(alast - arow)), xs_a)
                    ea.append(jnp.exp(acol))
                    el.append(jnp.exp(alast))
                y_ref[:, cols] = y + jnp.where(upper, ea[1], ea[0]) * _dot(c_g, ht)
                st_ref[pp] = ht * jnp.where(up1, el[1], el[0]) + s_new

    col = pl.BlockSpec((L, LANES), lambda c: (c, 0))
    row = pl.BlockSpec((LANES, L), lambda c: (0, c))
    return pl.pallas_call(
        body, name="ssd_fwd", grid=(nc,),
        in_specs=[pl.BlockSpec((L, SSM_CONV_DIM), lambda c: (c, 0)), col, row, row,
                  pl.BlockSpec((N_PAIR, LANES), lambda c: (0, 0))],
        out_specs=[pl.BlockSpec((L, SSM_INNER), lambda c: (c, 0)),
                   pl.BlockSpec((1, N_PAIR, N, LANES), lambda c: (c, 0, 0, 0))],
        out_shape=[jax.ShapeDtypeStruct((s, SSM_INNER), F32), jax.ShapeDtypeStruct((nc, N_PAIR, N, LANES), F32)],
        scratch_shapes=[pltpu.VMEM((N_PAIR, N, LANES), F32)],
        compiler_params=_cp("arbitrary"),
    )(xbc_act, ac_c, dt_r, ac_r, dsk_pair)


def _ssd_pair_bwd(xbc_act, dt_c, ac_c, dt_r, ac_r, hprev_all, dy, dsk_pair, a_row):
    s = xbc_act.shape[0]
    L, N, G = SSM_CHUNK, SSM_STATE, SSM_GROUPS
    nc = s // L
    rev = lambda c: nc - 1 - c

    def body(xbc_ref, dt_ref, ac_ref, dtr_ref, acr_ref, hp_ref, dy_ref, dsk_ref, arow_ref,
             dx_ref, ddt_ref, da_ref, dds_ref, dh_ref):
        @pl.when(pl.program_id(0) == 0)
        def _():
            dh_ref[...] = jnp.zeros_like(dh_ref)
            da_ref[...] = jnp.zeros_like(da_ref)
            dds_ref[...] = jnp.zeros_like(dds_ref)

        upper, up1, li, si = _pair_consts()
        lane = lax.broadcasted_iota(jnp.int32, (L, LANES), 1)
        sub = lax.broadcasted_iota(jnp.int32, (LANES, L), 0)
        lastrow = lax.broadcasted_iota(jnp.int32, (L, LANES), 0) == L - 1
        da_c = jnp.zeros((L, LANES), F32)
        da_r = jnp.zeros((LANES, L), F32)
        ddt_r = jnp.zeros((LANES, L), F32)
        for g in range(G):
            b_g = xbc_ref[:, SSM_INNER + g * N:SSM_INNER + (g + 1) * N]
            c_g = xbc_ref[:, SSM_INNER + G * N + g * N:SSM_INNER + G * N + (g + 1) * N]
            cb, cb_t = _dotT(c_g, b_g), _dotT(b_g, c_g)
            b_t, c_t = b_g.T, c_g.T
            dcb = jnp.zeros((L, L), F32)
            db_t = jnp.zeros((N, L), F32)
            dc = jnp.zeros((L, N), F32)
            for q in range(PAIRS_PER_GROUP):
                pp = g * PAIRS_PER_GROUP + q
                cols = slice(pp * LANES, (pp + 1) * LANES)
                xs, gy = xbc_ref[:, cols], dy_ref[:, cols]
                ht, dhn = hp_ref[0, pp], dh_ref[pp]
                acol = [jnp.broadcast_to(ac_ref[:, HP * pp + a:HP * pp + a + 1], (L, LANES)) for a in range(HP)]
                alast = [ac_ref[L - 1:L, HP * pp + a:HP * pp + a + 1] for a in range(HP)]
                ea = jnp.where(upper, jnp.exp(acol[1]), jnp.exp(acol[0]))
                el = jnp.where(up1, jnp.exp(alast[1]), jnp.exp(alast[0]))
                ge = gy * ea
                dc = dc + _dotT(ge, ht)
                dh_ref[pp] = _dot(c_t, ge) + dhn * el
                t_off = (ge * _dot(c_g, ht)).astype(BF16)
                hsum = jnp.sum(dhn * ht, axis=0, keepdims=True)
                dxs = dsk_ref[pp:pp + 1, :] * gy
                dds_ref[pp:pp + 1, :] += jnp.sum(gy * xs, axis=0, keepdims=True)
                for a in range(HP):
                    h = HP * pp + a
                    mine, mine1 = upper == (a == 1), up1 == (a == 1)
                    arow, dtrow = acr_ref[h:h + 1, :], dtr_ref[h:h + 1, :]
                    dtcol = jnp.broadcast_to(dt_ref[:, h:h + 1], (L, LANES))
                    xs_a, gy_a = jnp.where(mine, xs, 0.0), jnp.where(mine, gy, 0.0)
                    dhn_a = jnp.where(mine1, dhn, 0.0)
                    e_row = jnp.exp(alast[a] - arow)
                    w_row = dtrow * e_row
                    xd_t = _dotT(dhn_a, xs_a)
                    db_t = db_t + xd_t * w_row
                    dw = jnp.sum(b_t * xd_t, axis=0, keepdims=True)
                    de_e = dw * w_row
                    dal = (jnp.sum(jnp.where(mine1, hsum, 0.0), axis=1, keepdims=True) * jnp.exp(alast[a])
                           + jnp.sum(de_e, axis=1, keepdims=True))
                    dxs = dxs + _dot(b_g, dhn_a) * (dtcol * jnp.exp(alast[a] - acol[a]))
                    decay = jnp.exp(jnp.where(li >= si, acol[a] - arow, NEG))
                    decay_t = jnp.exp(jnp.where(si >= li, arow - acol[a], NEG))
                    m = cb * decay
                    dmdt = _dotT(gy_a, xs_a)
                    dxs = dxs + _dot(cb_t * decay_t * dtcol, gy_a)
                    dm = dmdt * dtrow
                    dcb = dcb + dm * decay
                    wb = (dm * m).astype(BF16)
                    onehot = jnp.where(lane == h, 1.0, 0.0).astype(BF16)
                    da_c = (da_c + jnp.dot(wb, onehot, preferred_element_type=F32)
                            + jnp.dot(jnp.where(mine, t_off, 0.0).astype(BF16), onehot, preferred_element_type=F32)
                            + jnp.where(lastrow & (lane == h), dal, 0.0))
                    da_r = jnp.where(sub == h, -(jnp.sum(wb.astype(F32), axis=0, keepdims=True) + de_e), da_r)
                    ddt_r = jnp.where(sub == h, dw * e_row + jnp.sum(dmdt * m, axis=0, keepdims=True), ddt_r)
                dx_ref[:, cols] = dxs
            dx_ref[:, SSM_INNER + g * N:SSM_INNER + (g + 1) * N] = (db_t + _dot(c_t, dcb)).T
            dx_ref[:, SSM_INNER + G * N + g * N:SSM_INNER + G * N + (g + 1) * N] = dc + _dot(dcb, b_g)
        dda = jnp.dot(_tri(L, lower=False), da_c + da_r.T, precision=HI, preferred_element_type=F32)
        ddt_ref[...] = dda * arow_ref[...] + ddt_r.T
        da_ref[...] += jnp.sum(dda * dt_ref[...], axis=0, keepdims=True)

    col = pl.BlockSpec((L, LANES), lambda c: (rev(c), 0))
    row = pl.BlockSpec((LANES, L), lambda c: (0, rev(c)))
    vec = pl.BlockSpec((1, LANES), lambda c: (0, 0))
    pairs = pl.BlockSpec((N_PAIR, LANES), lambda c: (0, 0))
    return pl.pallas_call(
        body, name="ssd_bwd", grid=(nc,),
        in_specs=[pl.BlockSpec((L, SSM_CONV_DIM), lambda c: (rev(c), 0)), col, col, row, row,
                  pl.BlockSpec((1, N_PAIR, N, LANES), lambda c: (rev(c), 0, 0, 0)),
                  pl.BlockSpec((L, SSM_INNER), lambda c: (rev(c), 0)), pairs, vec],
        out_specs=[pl.BlockSpec((L, SSM_CONV_DIM), lambda c: (rev(c), 0)), col, vec, pairs],
        out_shape=[jax.ShapeDtypeStruct((s, SSM_CONV_DIM), F32), jax.ShapeDtypeStruct((s, LANES), F32),
                   jax.ShapeDtypeStruct((1, LANES), F32), jax.ShapeDtypeStruct((N_PAIR, LANES), F32)],
        scratch_shapes=[pltpu.VMEM((N_PAIR, N, LANES), F32)],
        compiler_params=_cp("arbitrary"),
    )(xbc_act, dt_c, ac_c, dt_r, ac_r, hprev_all, dy, dsk_pair, a_row)


ROWS = 256
GW = SSM_INNER // SSM_GROUPS


def _rows(width, dtype=F32):
    return pl.BlockSpec((ROWS, width), lambda i: (i, 0))


def _vec(width):
    return pl.BlockSpec((1, width), lambda i: (0, 0))


def _gnorm_fwd(y, z, w):
    s = y.shape[0]

    def body(y_ref, z_ref, w_ref, o_ref):
        for g in range(SSM_GROUPS):
            cs = slice(g * GW, (g + 1) * GW)
            zz = z_ref[:, cs].astype(F32)
            u = y_ref[:, cs] * (zz * _sigmoid(zz))
            r = lax.rsqrt(jnp.mean(u * u, axis=1, keepdims=True) + RMS_EPS)
            o_ref[:, cs] = (u * r * w_ref[:, cs]).astype(o_ref.dtype)

    return pl.pallas_call(body, name="gnorm_fwd", grid=(s // ROWS,),
                          in_specs=[_rows(SSM_INNER), _rows(SSM_INNER), _vec(SSM_INNER)], out_specs=_rows(SSM_INNER),
                          out_shape=jax.ShapeDtypeStruct((s, SSM_INNER), BF16), compiler_params=_cp("parallel"))(y, z, w)


def _gnorm_bwd(y, z, w, do):
    s = y.shape[0]

    def body(y_ref, z_ref, w_ref, do_ref, dy_ref, dz_ref, dw_ref):
        @pl.when(pl.program_id(0) == 0)
        def _():
            dw_ref[...] = jnp.zeros_like(dw_ref)

        for g in range(SSM_GROUPS):
            cs = slice(g * GW, (g + 1) * GW)
            zz, yy, dd = z_ref[:, cs].astype(F32), y_ref[:, cs], do_ref[:, cs].astype(F32)
            sz, dsz = _silu_and_grad(zz)
            u = yy * sz
            r = lax.rsqrt(jnp.mean(u * u, axis=1, keepdims=True) + RMS_EPS)
            n = u * r
            dn = dd * w_ref[:, cs]
            dw_ref[:, cs] += jnp.sum(dd * n, axis=0, keepdims=True)
            du = r * (dn - n * jnp.mean(dn * n, axis=1, keepdims=True))
            dy_ref[:, cs] = du * sz
            dz_ref[:, cs] = (du * yy * dsz).astype(dz_ref.dtype)

    return pl.pallas_call(
        body, name="gnorm_bwd", grid=(s // ROWS,),
        in_specs=[_rows(SSM_INNER), _rows(SSM_INNER), _vec(SSM_INNER), _rows(SSM_INNER)],
        out_specs=[_rows(SSM_INNER), _rows(SSM_INNER), _vec(SSM_INNER)],
        out_shape=[jax.ShapeDtypeStruct((s, SSM_INNER), F32), jax.ShapeDtypeStruct((s, SSM_INNER), BF16),
                   jax.ShapeDtypeStruct((1, SSM_INNER), F32)],
        compiler_params=_cp("arbitrary"))(y, z, w, do)


def _mix_fwd(gl, bg, attn_d, ssm_d):
    s = gl.shape[0]
    d = D_MODEL

    def body(gl_ref, bg_ref, a_ref, m_ref, o_ref):
        g0 = _sigmoid(gl_ref[:, :d] + bg_ref[:, :d])
        g1 = _sigmoid(gl_ref[:, d:] + bg_ref[:, d:])
        o_ref[...] = (g0 * a_ref[...] + g1 * m_ref[...]).astype(o_ref.dtype)

    return pl.pallas_call(body, name="mix_fwd", grid=(s // ROWS,),
                          in_specs=[_rows(2 * d), _vec(2 * d), _rows(d), _rows(d)], out_specs=_rows(d),
                          out_shape=jax.ShapeDtypeStruct((s, d), BF16), compiler_params=_cp("parallel"))(
        gl, bg, attn_d, ssm_d)


def _mix_bwd(gl, bg, attn_d, ssm_d, dmix):
    s = gl.shape[0]
    d = D_MODEL

    def body(gl_ref, bg_ref, a_ref, m_ref, dm_ref, da_ref, ds_ref, dg_ref, db_ref):
        @pl.when(pl.program_id(0) == 0)
        def _():
            db_ref[...] = jnp.zeros_like(db_ref)

        g0 = _sigmoid(gl_ref[:, :d] + bg_ref[:, :d])
        g1 = _sigmoid(gl_ref[:, d:] + bg_ref[:, d:])
        dm = dm_ref[...].astype(F32)
        da_ref[...] = (dm * g0).astype(da_ref.dtype)
        ds_ref[...] = (dm * g1).astype(ds_ref.dtype)
        dl0 = dm * a_ref[...] * g0 * (1.0 - g0)
        dl1 = dm * m_ref[...] * g1 * (1.0 - g1)
        dg_ref[:, :d] = dl0.astype(dg_ref.dtype)
        dg_ref[:, d:] = dl1.astype(dg_ref.dtype)
        db_ref[:, :d] += jnp.sum(dl0, axis=0, keepdims=True)
        db_ref[:, d:] += jnp.sum(dl1, axis=0, keepdims=True)

    return pl.pallas_call(
        body, name="mix_bwd", grid=(s // ROWS,),
        in_specs=[_rows(2 * d), _vec(2 * d), _rows(d), _rows(d), _rows(d)],
        out_specs=[_rows(d), _rows(d), _rows(2 * d), _vec(2 * d)],
        out_shape=[jax.ShapeDtypeStruct((s, d), BF16), jax.ShapeDtypeStruct((s, d), BF16),
                   jax.ShapeDtypeStruct((s, 2 * d), BF16), jax.ShapeDtypeStruct((1, 2 * d), F32)],
        compiler_params=_cp("arbitrary"))(gl, bg, attn_d, ssm_d, dmix)


def _ln_stats(p):
    mu = jnp.mean(p, axis=1, keepdims=True)
    c = p - mu
    rstd = lax.rsqrt(jnp.mean(c * c, axis=1, keepdims=True) + LN_EPS)
    return c * rstd, rstd


def _ln_bwd(dy, xhat, rstd, g):
    dxh = dy * g
    return rstd * (dxh - jnp.mean(dxh, axis=1, keepdims=True) - xhat * jnp.mean(dxh * xhat, axis=1, keepdims=True))


def _ln1_fwd(x, mixed, g, b):
    s, d = x.shape

    def body(x_ref, m_ref, g_ref, b_ref, o_ref):
        xhat, _ = _ln_stats(DEEPNORM_ALPHA * x_ref[...] + m_ref[...])
        o_ref[...] = xhat * g_ref[...] + b_ref[...]

    return pl.pallas_call(body, name="ln1_fwd", grid=(s // ROWS,), in_specs=[_rows(d), _rows(d), _vec(d), _vec(d)],
                          out_specs=_rows(d), out_shape=jax.ShapeDtypeStruct((s, d), F32),
                          compiler_params=_cp("parallel"))(x, mixed, g, b)


def _ln2_loss(x1, h, target, g, b):
    s, d = x1.shape

    def body(x_ref, h_ref, t_ref, g_ref, b_ref, dp_ref, loss_ref, dg_ref, db_ref):
        @pl.when(pl.program_id(0) == 0)
        def _():
            loss_ref[...] = jnp.zeros_like(loss_ref)
            dg_ref[...] = jnp.zeros_like(dg_ref)
            db_ref[...] = jnp.zeros_like(db_ref)

        xhat, rstd = _ln_stats(DEEPNORM_ALPHA * x_ref[...] + h_ref[...])
        err = xhat * g_ref[...] + b_ref[...] - t_ref[...]
        part = 0.5 * jnp.sum(jnp.mean(err * err, axis=1, keepdims=True), axis=0, keepdims=True)
        loss_ref[...] += jnp.broadcast_to(part, loss_ref.shape)
        dy = err * (1.0 / d)
        dg_ref[...] += jnp.sum(dy * xhat, axis=0, keepdims=True)
        db_ref[...] += jnp.sum(dy, axis=0, keepdims=True)
        dp_ref[...] = _ln_bwd(dy, xhat, rstd, g_ref[...])

    return pl.pallas_call(
        body, name="ln2_loss", grid=(s // ROWS,), in_specs=[_rows(d), _rows(d), _rows(d), _vec(d), _vec(d)],
        out_specs=[_rows(d), _vec(LANES), _vec(d), _vec(d)],
        out_shape=[jax.ShapeDtypeStruct((s, d), F32), jax.ShapeDtypeStruct((1, LANES), F32),
                   jax.ShapeDtypeStruct((1, d), F32), jax.ShapeDtypeStruct((1, d), F32)],
        compiler_params=_cp("arbitrary"))(x1, h, target, g, b)


def _ln1_bwd(x, mixed, g, dpre2, dffn):
    s, d = x.shape

    def body(x_ref, m_ref, g_ref, d2_ref, df_ref, dp_ref, dr_ref, dg_ref, db_ref):
        @pl.when(pl.program_id(0) == 0)
        def _():
            dg_ref[...] = jnp.zeros_like(dg_ref)
            db_ref[...] = jnp.zeros_like(db_ref)

        xhat, rstd = _ln_stats(DEEPNORM_ALPHA * x_ref[...] + m_ref[...])
        dy = DEEPNORM_ALPHA * d2_ref[...] + df_ref[...]
        dg_ref[...] += jnp.sum(dy * xhat, axis=0, keepdims=True)
        db_ref[...] += jnp.sum(dy, axis=0, keepdims=True)
        dp = _ln_bwd(dy, xhat, rstd, g_ref[...])
        dp_ref[...] = dp
        dr_ref[...] = DEEPNORM_ALPHA * dp

    return pl.pallas_call(
        body, name="ln1_bwd", grid=(s // ROWS,), in_specs=[_rows(d), _rows(d), _vec(d), _rows(d), _rows(d)],
        out_specs=[_rows(d), _rows(d), _vec(d), _vec(d)],
        out_shape=[jax.ShapeDtypeStruct((s, d), F32), jax.ShapeDtypeStruct((s, d), F32),
                   jax.ShapeDtypeStruct((1, d), F32), jax.ShapeDtypeStruct((1, d), F32)],
        compiler_params=_cp("arbitrary"))(x, mixed, g, dpre2, dffn)


def _swiglu_fwd(gu):
    s = gu.shape[0]
    f = FFN_HIDDEN

    def body(g_ref, u_ref, o_ref):
        gg = g_ref[...].astype(F32)
        o_ref[...] = (gg * _sigmoid(gg) * u_ref[...].astype(F32)).astype(o_ref.dtype)

    return pl.pallas_call(
        body, name="swiglu_fwd", grid=(s // ROWS,),
        in_specs=[pl.BlockSpec((ROWS, f), lambda i: (i, 0)), pl.BlockSpec((ROWS, f), lambda i: (i, 1))],
        out_specs=_rows(f), out_shape=jax.ShapeDtypeStruct((s, f), BF16), compiler_params=_cp("parallel"))(gu, gu)


def _swiglu_bwd(gu, dact):
    s = gu.shape[0]
    f = FFN_HIDDEN

    def body(g_ref, u_ref, d_ref, o_ref):
        sg, dsg = _silu_and_grad(g_ref[...].astype(F32))
        dd = d_ref[...].astype(F32)
        o_ref[:, :f] = (dd * u_ref[...].astype(F32) * dsg).astype(o_ref.dtype)
        o_ref[:, f:] = (dd * sg).astype(o_ref.dtype)

    return pl.pallas_call(
        body, name="swiglu_bwd", grid=(s // ROWS,),
        in_specs=[pl.BlockSpec((ROWS, f), lambda i: (i, 0)), pl.BlockSpec((ROWS, f), lambda i: (i, 1)), _rows(f)],
        out_specs=_rows(2 * f), out_shape=jax.ShapeDtypeStruct((s, 2 * f), BF16),
        compiler_params=_cp("parallel"))(gu, gu, dact)


def _peer(k):
    x, y, c = lax.axis_index("x"), lax.axis_index("y"), lax.axis_index("c")
    kx, ky, kc = (k >> 2) & 1, (k >> 1) & 1, k & 1
    px = (1 - x) if kx else x
    py = (1 - y) if ky else y
    pc = (1 - c) if kc else c
    return (px, py, pc), 4 * px + 2 * py + pc


def _my_index():
    return 4 * lax.axis_index("x") + 2 * lax.axis_index("y") + lax.axis_index("c")


def _comm_copies(ins, outs, sems, scatter):
    send_sems, recv_sems, local_sems = sems
    me = _my_index()
    copies = [pltpu.make_async_copy(ins[t].at[me] if scatter else ins[t], outs[t].at[me], local_sems.at[t])
              for t in range(len(ins))]
    for k in range(1, N_DEV):
        peer, pidx = _peer(k)
        for t in range(len(ins)):
            copies.append(pltpu.make_async_remote_copy(
                src_ref=ins[t].at[pidx] if scatter else ins[t], dst_ref=outs[t].at[me],
                send_sem=send_sems.at[t, k - 1], recv_sem=recv_sems.at[t, k - 1], device_id=peer,
                device_id_type=pl.DeviceIdType.MESH))
    return copies


def _comm_sems(n):
    return [pltpu.SemaphoreType.DMA((n, N_DEV - 1)), pltpu.SemaphoreType.DMA((n, N_DEV - 1)),
            pltpu.SemaphoreType.DMA((n,))]


def _comm_out_shapes(parts, scatter):
    return [jax.ShapeDtypeStruct(p.shape if scatter else (N_DEV,) + p.shape, p.dtype) for p in parts]


def _all_gather(parts):
    n = len(parts)

    def body(*refs):
        ins, outs = refs[:n], refs[n:2 * n]
        send_sems, recv_sems, local_sems = refs[2 * n:]
        x, y, c = lax.axis_index("x"), lax.axis_index("y"), lax.axis_index("c")
        me, sibling = (x, y, c), (x, y, 1 - c)
        chips = [(1 - x, y), (x, 1 - y), (1 - x, 1 - y)]

        def copy(t, k, block, to, src=None):
            dst = outs[t].at[4 * block[0] + 2 * block[1] + block[2]]
            return pltpu.make_async_remote_copy(
                src_ref=dst if src is None else src, dst_ref=dst, send_sem=send_sems.at[t, k],
                recv_sem=recv_sems.at[t, k], device_id=to, device_id_type=pl.DeviceIdType.MESH)

        mine = [pltpu.make_async_copy(ins[t], outs[t].at[_my_index()], local_sems.at[t]) for t in range(n)]
        for cp in mine:
            cp.start()
        first = [copy(t, 0, me, sibling, src=ins[t]) for t in range(n)]
        first += [copy(t, 1 + j, me, (*chip, c), src=ins[t]) for j, chip in enumerate(chips) for t in range(n)]
        for cp in first:
            cp.start()
        passed = []
        for j, chip in enumerate(chips):
            for t in range(n):
                copy(t, 1 + j, (*chip, c), me).wait_recv()
                passed.append(copy(t, 4 + j, (*chip, c), sibling))
                passed[-1].start()
        for t in range(n):
            copy(t, 0, sibling, me).wait_recv()
            for j, chip in enumerate(chips):
                copy(t, 4 + j, (*chip, 1 - c), me).wait_recv()
        for cp in first + passed:
            cp.wait_send()
        for cp in mine:
            cp.wait()

    anyspec = pl.BlockSpec(memory_space=pl.ANY)
    return pl.pallas_call(body, name="all_gather", in_specs=[anyspec] * n, out_specs=[anyspec] * n,
                          out_shape=_comm_out_shapes(parts, False), scratch_shapes=_comm_sems(n))(*parts)


def _remote_scatter_copies(ins, lands, send_sems, recv_sems):
    me = _my_index()
    copies = []
    for k in range(1, N_DEV):
        peer, pidx = _peer(k)
        for t in range(len(ins)):
            copies.append(pltpu.make_async_remote_copy(
                src_ref=ins[t].at[pidx], dst_ref=lands[t].at[me], send_sem=send_sems.at[t * (N_DEV - 1) + k - 1],
                recv_sem=recv_sems.at[t * (N_DEV - 1) + k - 1], device_id=peer, device_id_type=pl.DeviceIdType.MESH))
    return copies


def _landing_zones(parts):
    me = _my_index()
    return [jnp.where(lax.broadcasted_iota(jnp.int32, p.shape, 0) == me, p, jnp.zeros_like(p)) for p in parts]


_HBM = pl.BlockSpec(memory_space=pltpu.HBM)
_SEM = pl.BlockSpec(memory_space=pltpu.SEMAPHORE)


def _exchange_start(parts, lands):
    n = len(parts)

    def body(*refs):
        ins, lnd, send_sems, recv_sems, token = refs[:n], refs[n:2 * n], refs[2 * n], refs[2 * n + 1], refs[-1]
        for cp in _remote_scatter_copies(ins, lnd, send_sems, recv_sems):
            cp.start()
        token[...] = jnp.zeros_like(token)

    hbm = [pltpu.HBM(p.shape, p.dtype) for p in parts]
    outs = pl.pallas_call(
        body, name="exchange_start",
        out_shape=[pltpu.SemaphoreType.DMA((n * (N_DEV - 1),)), pltpu.SemaphoreType.DMA((n * (N_DEV - 1),))] + hbm + hbm
        + [jax.ShapeDtypeStruct((8, LANES), F32)],
        in_specs=[_HBM] * (2 * n), out_specs=[_SEM, _SEM] + [_HBM] * (2 * n) + [pl.BlockSpec(memory_space=pltpu.VMEM)],
        input_output_aliases={t: 2 + t for t in range(2 * n)},
        compiler_params=pltpu.CompilerParams(has_side_effects=pltpu.SideEffectType.DATAFLOW_SIDE_EFFECTING),
    )(*[pltpu.with_memory_space_constraint(p, pltpu.HBM) for p in list(parts) + list(lands)])
    return outs[0], outs[1], list(outs[2:2 + n]), list(outs[2 + n:2 + 2 * n]), outs[-1]


def _exchange_wait(send_sems, recv_sems, parts, lands, after):
    n = len(parts)

    def body(*refs):
        ins, lnd, send_sems, recv_sems = refs[:n], refs[n:2 * n], refs[2 * n], refs[2 * n + 1]
        for cp in _remote_scatter_copies(ins, lnd, send_sems, recv_sems):
            cp.wait_send()
            cp.wait_recv()

    hbm = [pltpu.HBM(p.shape, p.dtype) for p in parts]
    outs = pl.pallas_call(
        body, name="exchange_wait", out_shape=hbm + hbm,
        in_specs=[_HBM] * (2 * n) + [_SEM, _SEM, pl.BlockSpec(memory_space=pl.ANY)], out_specs=[_HBM] * (2 * n),
        input_output_aliases={t: t for t in range(2 * n)},
        compiler_params=pltpu.CompilerParams(has_side_effects=pltpu.SideEffectType.DATAFLOW_SIDE_EFFECTING),
    )(*parts, *lands, send_sems, recv_sems, after)
    return list(outs[n:])


def _adamw(recv, w, m, v, name):
    _, r, c = w.shape
    br = _tile(r, 128)
    c1 = 1.0 / (1.0 - ADAM_B1 ** ADAM_STEP)
    c2 = 1.0 / (1.0 - ADAM_B2 ** ADAM_STEP)

    def body(r_ref, w_ref, m_ref, v_ref, g_ref, d_ref, mo_ref, vo_ref):
        g = r_ref[0].astype(F32)
        for k in range(1, N_DEV):
            g = g + r_ref[k].astype(F32)
        mn = ADAM_B1 * m_ref[0] + (1.0 - ADAM_B1) * g
        vn = ADAM_B2 * v_ref[0] + (1.0 - ADAM_B2) * (g * g)
        g_ref[0] = g
        mo_ref[0] = mn
        vo_ref[0] = vn
        d_ref[0] = -ADAM_LR * ((mn * c1) / (jnp.sqrt(vn * c2) + ADAM_EPS) + ADAM_WD * w_ref[0])

    blk = pl.BlockSpec((1, br, c), lambda i: (0, i, 0))
    return pl.pallas_call(
        body, name=name, grid=(r // br,),
        in_specs=[pl.BlockSpec((N_DEV, br, c), lambda i: (0, i, 0)), blk, blk, blk],
        out_specs=[blk] * 4, out_shape=[jax.ShapeDtypeStruct((1, r, c), F32)] * 4,
        compiler_params=_cp("parallel"))(recv, w, m, v)


def _lane_row(pairs):
    row = jnp.zeros((LANES,), F32)
    for lane0, vec in pairs:
        row = lax.dynamic_update_slice(row, vec.astype(F32), (lane0,))
    return row.reshape(1, LANES)


def _stage_in(x, wts, small):
    s = x.shape[0]
    a = -jnp.exp(small["a_log"])
    bias_row = _lane_row([(DT_LANE0, small["dt_bias"]), (F_LANE0, small["b_forget"])])
    a_row = _lane_row([(DT_LANE0, a)])
    conv_b = small["conv_b"].reshape(1, -1)
    norm_w = small["ssm_norm_w"].reshape(1, -1)
    bg = small["b_gates"].reshape(1, -1)
    g1, b1 = small["ln1_g"].reshape(1, -1), small["ln1_b"].reshape(1, -1)
    g2, b2 = small["ln2_g"].reshape(1, -1), small["ln2_b"].reshape(1, -1)
    d_skip = small["d_skip"]
    xb = x.astype(BF16)

    qkv = _mm(xb, wts["qkv"], out_dtype=BF16, name="f_qkv")
    z = _mm(xb, wts["z"], out_dtype=BF16, name="f_z")
    xbc = _mm(xb, wts["xbc"], name="f_xbc")
    gl = _mm(xb, wts["gate"], out_dtype=BF16, name="f_gate")
    fd = _mm(xb, wts["fd"], name="f_fd")
    dt_c, ac_c, cf_c, dt_r, ac_r, cf_r = _stats_fwd(fd, bias_row, a_row)
    bk = _att_blocks(s)[1]
    ck4 = cf_r[F_LANE0:F_LANE0 + ATT_HEADS].reshape(N_HP, HP, s // bk, bk)
    return dict(locals())


def _stage_mid(c, attn, lse, wts, target):
    x, xb, qkv, z, xbc, gl, fd, ck4 = (c[k] for k in ("x", "xb", "qkv", "z", "xbc", "gl", "fd", "ck4"))
    dt_c, ac_c, dt_r, ac_r, a_row, bias_row = (c[k] for k in ("dt_c", "ac_c", "dt_r", "ac_r", "a_row", "bias_row"))
    conv_b, norm_w, bg, g1, b1, g2, b2, d_skip = (c[k] for k in ("conv_b", "norm_w", "bg", "g1", "b1", "g2", "b2",
                                                                "d_skip"))
    conv_w = c["wts"]["conv"]
    attn_d = _mm(attn, wts["pa"], out_dtype=BF16, name="f_pa")
    xact = _conv_fwd(xbc, conv_w, conv_b)
    dsk_pair = jnp.repeat(d_skip, SSM_HEAD_DIM).reshape(N_PAIR, LANES)
    y, hprev = _ssd_pair_fwd(xact, ac_c, dt_r, ac_r, dsk_pair)
    ssm = _gnorm_fwd(y, z, norm_w)
    ssm_d = _mm(ssm, wts["ps"], out_dtype=BF16, name="f_ps")
    mix = _mix_fwd(gl, bg, attn_d, ssm_d)
    mixed = _mm(mix, wts["out"], name="f_out")
    x1 = _ln1_fwd(x, mixed, g1, b1)
    gu = _mm(x1, wts["gu"], out_dtype=BF16, name="f_gu")
    act = _swiglu_fwd(gu)
    h = _mm(act, wts["down"], name="f_down")
    dpre2, loss_row, dg2, db2 = _ln2_loss(x1, h, target, g2, b2)

    d_act = _mm(dpre2, wts["down"], tb=True, out_dtype=BF16, name="b_down_x")
    dw_down = _mm(act, dpre2, ta=True, name="b_down_w")
    dgu = _swiglu_bwd(gu, d_act)
    dffn = _mm(dgu, wts["gu"], tb=True, name="b_gu_x")
    dw_gu = _mm(x1, dgu, ta=True, name="b_gu_w")
    dpre1, dxr, dg1, db1 = _ln1_bwd(x, mixed, g1, dpre2, dffn)
    dmix = _mm(dpre1, wts["out"], tb=True, out_dtype=BF16, name="b_out_x")
    dw_out = _mm(mix, dpre1, ta=True, name="b_out_w")
    dattn_d, dssm_d, dgl, dbg = _mix_bwd(gl, bg, attn_d, ssm_d, dmix)
    dssm = _mm(dssm_d, wts["ps"], tb=True, out_dtype=BF16, name="b_ps_x")
    dw_ps = _mm(ssm, dssm_d, ta=True, name="b_ps_w")
    dattn = _mm(dattn_d, wts["pa"], tb=True, name="b_pa_x")
    dw_pa = _mm(attn, dattn_d, ta=True, name="b_pa_w")
    dy, dz, dnw = _gnorm_bwd(y, z, norm_w, dssm)
    dxact, ddt, da_row, dds_pair = _ssd_pair_bwd(xact, dt_c, ac_c, dt_r, ac_r, hprev, dy, dsk_pair, a_row)
    dds = dds_pair.reshape(SSM_HEADS, SSM_HEAD_DIM).sum(axis=1)
    dpre_c, dconv_w, dconv_b = _conv_bwd_pre(xbc, conv_w, conv_b, dxact)
    dxbc = _conv_bwd_in(dpre_c, conv_w)
    st, do_b = _att_prep(dattn, attn, lse)
    late = dict(pa=dw_pa, ps=dw_ps, out=dw_out, gu=dw_gu, down=dw_down)
    keep = ("st", "do_b", "ddt", "dxr", "dz", "dxbc", "dgl", "dconv_w", "dconv_b", "da_row", "dds", "dnw", "dbg",
            "dg1", "db1", "dg2", "db2", "loss_row")
    loc = locals()
    return {**c, **{k: loc[k] for k in keep}}, late


def _stage_out_w(c, att_grads):
    dq, dk, dv, dck, dcq = att_grads
    xb, fd, bias_row, ddt, dz, dxbc, dgl = (c[k] for k in ("xb", "fd", "bias_row", "ddt", "dz", "dxbc", "dgl"))
    s, a = xb.shape[0], c["a"]
    dcum = dck.reshape(ATT_HEADS, s) + dcq.reshape(ATT_HEADS, s)
    dfd, dbias = _stats_bwd(fd, bias_row, ddt, jnp.zeros((LANES, s), F32).at[F_LANE0:F_LANE0 + ATT_HEADS].set(dcum))
    dproj = (dq, dk, dv, dz, dxbc, dgl, dfd)
    dw_in = [_mm(xb, g_, ta=True, tb=(i == 0), name=f"b_in_w{i}") for i, g_ in enumerate(dproj)]
    grads = dict(q=dw_in[0], k=dw_in[1], v=dw_in[2], z=dw_in[3], xbc=dw_in[4], gate=dw_in[5], fd=dw_in[6],
                 conv=c["dconv_w"])
    small_g = dict(
        b_forget=dbias[0, F_LANE0:F_LANE0 + ATT_HEADS], conv_b=c["dconv_b"][0], dt_bias=dbias[0, :SSM_HEADS],
        a_log=c["da_row"][0, :SSM_HEADS] * a, d_skip=c["dds"], ssm_norm_w=c["dnw"][0], b_gates=c["dbg"][0],
        ln1_g=c["dg1"][0], ln1_b=c["db1"][0], ln2_g=c["dg2"][0], ln2_b=c["db2"][0])
    return c["loss_row"][0, 0], grads, small_g, dproj


def _stage_out_x(c, dproj, token):
    wts, d = c["wts"], D_MODEL
    wq = wts["qkv"][:, :d] + token.astype(BF16)
    wk, wv = wts["qkv"][:, d:2 * d], wts["qkv"][:, 2 * d:]
    dx = c["dxr"]
    for i, (g_, w_) in enumerate(zip(dproj, (wq, wk, wv, wts["z"], wts["xbc"], wts["gate"], wts["fd"]))):
        dx = _mm(g_, w_, ta=(i == 0), tb=True, add=dx, name=f"b_in_x{i}")
    return dx


BIG = ("w_in", "w_proj_attn", "w_proj_ssm", "w_out", "w_ffn_gate", "w_ffn_up", "w_ffn_down", "conv_w")
EARLY = ("w_in", "conv_w")
LATE = ("w_proj_attn", "w_proj_ssm", "w_out", "w_ffn_gate", "w_ffn_up", "w_ffn_down")
SMALL = ("b_forget", "conv_b", "dt_bias", "a_log", "d_skip", "ssm_norm_w", "b_gates", "ln1_g", "ln1_b", "ln2_g",
         "ln2_b")
SMALL_ROWS = 96
IN_SHARD = IN_WIDTH // N_DEV
IN_SEGMENTS = (("q", 0, 1024), ("k", 1024, 1024), ("v", 2048, 1024), ("f", 3072, ATT_HEADS), ("z", 3088, SSM_INNER),
               ("xbc", 5136, SSM_CONV_DIM), ("dt", 8208, SSM_HEADS), ("gate", 8240, 2 * D_MODEL))


def _cols_from_shards(shards, lo, hi):
    w = shards[0].shape[1]
    pieces = []
    for j in range(len(shards)):
        a, b = max(lo, j * w), min(hi, (j + 1) * w)
        if a < b:
            pieces.append(shards[j][:, a - j * w:b - j * w])
    return pieces[0] if len(pieces) == 1 else jnp.concatenate(pieces, axis=1)


def _shards_from_parts(parts, width):
    shards = []
    for j in range(N_DEV):
        lo, hi = j * width, (j + 1) * width
        pieces = []
        for mat, c0 in parts:
            a, b = max(lo, c0), min(hi, c0 + mat.shape[1])
            if a < b:
                pieces.append(mat[:, a - c0:b - c0])
        shards.append(pieces[0] if len(pieces) == 1 else jnp.concatenate(pieces, axis=1))
    return shards


def _pack_small(vals):
    flat = jnp.concatenate([vals[n].reshape(-1) for n in SMALL])
    return jnp.pad(flat, (0, SMALL_ROWS * LANES - flat.shape[0])).reshape(SMALL_ROWS, LANES)


def _unpack_small(pack, shapes):
    flat = pack.reshape(-1)
    out, off = {}, 0
    for n in SMALL:
        sz = math.prod(shapes[n])
        out[n] = flat[off:off + sz].reshape(shapes[n])
        off += sz
    return out


def kernel(x, w_in, b_forget, conv_w, conv_b, dt_bias, a_log, d_skip, ssm_norm_w, w_proj_attn, w_proj_ssm, b_gates, w_out, ln1_g, ln1_b, w_ffn_gate, w_ffn_up, w_ffn_down, ln2_g, ln2_b, loss_target, m_w_in, m_b_forget, m_conv_w, m_conv_b, m_dt_bias, m_a_log, m_d_skip, m_ssm_norm_w, m_w_proj_attn, m_w_proj_ssm, m_b_gates, m_w_out, m_ln1_g, m_ln1_b, m_w_ffn_gate, m_w_ffn_up, m_w_ffn_down, m_ln2_g, m_ln2_b, v_w_in, v_b_forget, v_conv_w, v_conv_b, v_dt_bias, v_a_log, v_d_skip, v_ssm_norm_w, v_w_proj_attn, v_w_proj_ssm, v_b_gates, v_w_out, v_ln1_g, v_ln1_b, v_w_ffn_gate, v_w_ffn_up, v_w_ffn_down, v_ln2_g, v_ln2_b):
    args = dict(locals())
    d, f = D_MODEL, FFN_HIDDEN
    big_w = {n: args[n][0] for n in BIG}
    small_w = {n: args[n][0] for n in SMALL}
    big_shapes = {n: args[n].shape for n in BIG}
    small_shapes = {n: args[n].shape for n in SMALL}

    early = dict(zip(EARLY, _all_gather([big_w["w_in"].astype(BF16), big_w["conv_w"]])))
    in_shards = [early["w_in"][j] for j in range(N_DEV)]
    seg = {n: _cols_from_shards(in_shards, c0, c0 + w) for n, c0, w in IN_SEGMENTS}
    wfd = jnp.concatenate([seg["dt"], seg["f"], jnp.zeros((d, LANES - SSM_HEADS - ATT_HEADS), BF16)], axis=1)
    wts = dict(qkv=jnp.concatenate([seg["q"], seg["k"], seg["v"]], axis=1), z=seg["z"], xbc=seg["xbc"],
               gate=seg["gate"], fd=wfd, conv=jnp.concatenate([early["conv_w"][j] for j in range(N_DEV)], axis=1))

    ctx = _stage_in(x[0], wts, small_w)
    attn, lse, gathered = _attention_fwd(ctx["qkv"], ctx["ck4"], [big_w[n].astype(BF16) for n in LATE])
    full = dict(zip(LATE, gathered))
    late_w = dict(
        pa=full["w_proj_attn"].reshape(d, d), ps=full["w_proj_ssm"].reshape(SSM_INNER, d),
        out=full["w_out"].reshape(d, d),
        gu=jnp.concatenate([full["w_ffn_gate"][j] for j in range(N_DEV)]
                           + [full["w_ffn_up"][j] for j in range(N_DEV)], axis=1),
        down=full["w_ffn_down"].reshape(f, d))
    ctx, gl = _stage_mid(ctx, attn, lse, late_w, loss_target[0])
    late_dest = dict(
        w_ffn_gate=jnp.stack([s_.astype(BF16) for s_ in _shards_from_parts([(gl["gu"][:, :f], 0)], f // N_DEV)]),
        w_ffn_up=jnp.stack([s_.astype(BF16) for s_ in _shards_from_parts([(gl["gu"][:, f:], 0)], f // N_DEV)]))
    for n, key in (("w_proj_attn", "pa"), ("w_proj_ssm", "ps"), ("w_out", "out"), ("w_ffn_down", "down")):
        late_dest[n] = gl[key].astype(BF16).reshape((N_DEV,) + big_shapes[n][1:])
    att_grads, late_recv = _attention_bwd(ctx["qkv"], ctx["ck4"], ctx["st"], ctx["do_b"], [late_dest[n] for n in LATE])
    loss_part, g, small_g, dproj = _stage_out_w(ctx, att_grads)
    loss = lax.psum(loss_part, ("x", "y", "c"))

    gfd = g["fd"]
    in_parts = dict(q=g["q"], k=g["k"], v=g["v"], f=gfd[:, F_LANE0:F_LANE0 + ATT_HEADS], z=g["z"], xbc=g["xbc"],
                    dt=gfd[:, DT_LANE0:DT_LANE0 + SSM_HEADS], gate=g["gate"])
    win_dest = jnp.stack([s_.astype(BF16) for s_ in
                          _shards_from_parts([(in_parts[n], c0) for n, c0, _ in IN_SEGMENTS], IN_SHARD)])
    conv_dest = jnp.stack(_shards_from_parts([(g["conv"], 0)], SSM_CONV_DIM // N_DEV))
    small_pack = _pack_small(small_g)
    last_parts = [win_dest, conv_dest, jnp.broadcast_to(small_pack, (N_DEV,) + small_pack.shape)]
    send_sems, recv_sems, parts_thru, lands_thru, token = _exchange_start(last_parts, _landing_zones(last_parts))
    grad_x = _stage_out_x(ctx, dproj, token[0, 0])
    early_recv = _exchange_wait(send_sems, recv_sems, parts_thru, lands_thru, grad_x)
    recv = dict(zip(LATE, late_recv))
    recv["w_in"], recv["conv_w"] = early_recv[0], early_recv[1]

    outs = {}
    for n in BIG:
        outs[n] = _adamw(recv[n], args[n], args["m_" + n], args["v_" + n], name="adamw_" + n)
    small4 = _adamw(early_recv[2], _pack_small(small_w)[None], _pack_small({n: args["m_" + n][0] for n in SMALL})[None],
                    _pack_small({n: args["v_" + n][0] for n in SMALL})[None], name="adamw_small")
    small_out = [_unpack_small(p, small_shapes) for p in small4]
    for n in SMALL:
        outs[n] = [so[n] for so in small_out]

    order = ("w_in", "b_forget", "conv_w", "conv_b", "dt_bias", "a_log", "d_skip", "ssm_norm_w", "w_proj_attn",
             "w_proj_ssm", "b_gates", "w_out", "ln1_g", "ln1_b", "w_ffn_gate", "w_ffn_up", "w_ffn_down", "ln2_g",
             "ln2_b")
    res = [loss, grad_x[None]]
    for i in range(4):
        res += [outs[n][i] for n in order]
    return tuple(res)
```

```python
import functools
import math

import jax
import jax.numpy as jnp
from jax import lax
from jax.experimental import pallas as pl
from jax.experimental.pallas import tpu as pltpu

F32 = jnp.float32
BF16 = jnp.bfloat16

N_DEV = 8
D_MODEL = 1024
ATT_HEADS = 16
ATT_HEAD_DIM = 64
SSM_INNER = 2048
SSM_HEADS = 32
SSM_HEAD_DIM = 64
SSM_GROUPS = 4
SSM_HEADS_PER_GROUP = 8
SSM_STATE = 128
SSM_CONV = 4
SSM_CHUNK = 128
SSM_CONV_DIM = 3072
FFN_HIDDEN = 2816
IN_WIDTH = 10288
DEEPNORM_ALPHA = 2.0 ** 0.25
LN_EPS = 1e-5
RMS_EPS = 1e-5
ADAM_LR, ADAM_B1, ADAM_B2, ADAM_EPS, ADAM_WD, ADAM_STEP = 0.001, 0.9, 0.999, 1e-08, 0.01, 10
ATT_SCALE = 1.0 / math.sqrt(ATT_HEAD_DIM)

LANES = 128
VMEM_LIMIT = 56 * 1024 * 1024
NEG = -1e30

DT_LANE0 = 0
F_LANE0 = 32
HI = lax.Precision.HIGHEST


def _cp(*sem):
    return pltpu.CompilerParams(dimension_semantics=sem, vmem_limit_bytes=VMEM_LIMIT)


def _tile(n, cap=1408):
    for t in (1408, 1024, 512, 384, 256, 128):
        if t <= cap and n % t == 0:
            return t
    return n


def _sigmoid(x):
    return 1.0 / (1.0 + jnp.exp(-x))


def _mm(a, b, *, ta=False, tb=False, out_dtype=F32, add=None, name):
    m, k = (a.shape[1], a.shape[0]) if ta else a.shape
    n = b.shape[0] if tb else b.shape[1]
    assert (b.shape[1] if tb else b.shape[0]) == k
    tm, tn, tk = _tile(m), _tile(n), _tile(k)
    nk = k // tk
    dims = (((0,) if ta else (1,), (1,) if tb else (0,)), ((), ()))

    def body_single(*refs):
        a_ref, b_ref = refs[:2]
        r = lax.dot_general(a_ref[...].astype(BF16), b_ref[...].astype(BF16), dims, preferred_element_type=F32)
        if add is not None:
            r = r + refs[2][...]
        refs[-1][...] = r.astype(refs[-1].dtype)

    def body(*refs):
        if add is None:
            a_ref, b_ref, o_ref, acc_ref = refs
        else:
            a_ref, b_ref, c_ref, o_ref, acc_ref = refs
        kk = pl.program_id(2)

        @pl.when(kk == 0)
        def _():
            acc_ref[...] = jnp.zeros_like(acc_ref)

        acc_ref[...] += lax.dot_general(a_ref[...].astype(BF16), b_ref[...].astype(BF16), dims,
                                        preferred_element_type=F32)

        @pl.when(kk == nk - 1)
        def _():
            r = acc_ref[...]
            if add is not None:
                r = r + c_ref[...]
            o_ref[...] = r.astype(o_ref.dtype)

    a_spec = pl.BlockSpec((tk, tm), lambda i, j, kk: (kk, i)) if ta else pl.BlockSpec((tm, tk), lambda i, j, kk: (i, kk))
    b_spec = pl.BlockSpec((tn, tk), lambda i, j, kk: (j, kk)) if tb else pl.BlockSpec((tk, tn), lambda i, j, kk: (kk, j))
    o_spec = pl.BlockSpec((tm, tn), lambda i, j, kk: (i, j))
    in_specs, args = [a_spec, b_spec], [a, b]
    if add is not None:
        in_specs.append(o_spec)
        args.append(add)
    return pl.pallas_call(
        body_single if nk == 1 else body, name=name, grid=(m // tm, n // tn, nk), in_specs=in_specs, out_specs=o_spec,
        out_shape=jax.ShapeDtypeStruct((m, n), out_dtype),
        scratch_shapes=[] if nk == 1 else [pltpu.VMEM((tm, tn), F32)],
        compiler_params=_cp("parallel", "parallel", "arbitrary"),
    )(*args)


def _tri(n, lower=True):
    r = lax.broadcasted_iota(jnp.int32, (n, n), 0)
    c = lax.broadcasted_iota(jnp.int32, (n, n), 1)
    return jnp.where((r >= c) if lower else (c >= r), 1.0, 0.0).astype(F32)


def _stats_fwd(fd, bias_row, a_row):
    s = fd.shape[0]
    blk = SSM_CHUNK

    def body(fd_ref, bias_ref, a_ref, dt_ref, ac_ref, cf_ref, dtr_ref, acr_ref, cfr_ref, carry_ref):
        @pl.when(pl.program_id(0) == 0)
        def _():
            carry_ref[...] = jnp.zeros_like(carry_ref)

        v = fd_ref[...] + bias_ref[...]
        dt = jnp.maximum(v, 0.0) + jnp.log(1.0 + jnp.exp(-jnp.abs(v)))
        lf = jnp.minimum(v, 0.0) - jnp.log(1.0 + jnp.exp(-jnp.abs(v)))
        tri = _tri(blk)
        ac = jnp.dot(tri, dt * a_ref[...], precision=HI, preferred_element_type=F32)
        cf = jnp.dot(tri, lf, precision=HI, preferred_element_type=F32) + carry_ref[0:1, :]
        carry_ref[...] = carry_ref[...] + jnp.sum(lf, axis=0, keepdims=True)
        dt_ref[...] = dt
        ac_ref[...] = ac
        cf_ref[...] = cf
        dtr_ref[...] = dt.T
        acr_ref[...] = ac.T
        cfr_ref[...] = cf.T

    col = pl.BlockSpec((blk, LANES), lambda i: (i, 0))
    row = pl.BlockSpec((LANES, blk), lambda i: (0, i))
    vec = pl.BlockSpec((1, LANES), lambda i: (0, 0))
    return pl.pallas_call(
        body, name="stats_fwd", grid=(s // blk,), in_specs=[col, vec, vec],
        out_specs=[col, col, col, row, row, row],
        out_shape=[jax.ShapeDtypeStruct((s, LANES), F32)] * 3 + [jax.ShapeDtypeStruct((LANES, s), F32)] * 3,
        scratch_shapes=[pltpu.VMEM((8, LANES), F32)],
        compiler_params=_cp("arbitrary"),
    )(fd, bias_row, a_row)


def _stats_bwd(fd, bias_row, ddt, dcum_rows):
    s = fd.shape[0]
    blk = SSM_CHUNK
    nb = s // blk

    def body(fd_ref, bias_ref, ddt_ref, dck_ref, o_ref, db_ref, carry_ref):
        @pl.when(pl.program_id(0) == 0)
        def _():
            carry_ref[...] = jnp.zeros_like(carry_ref)
            db_ref[...] = jnp.zeros_like(db_ref)

        v = fd_ref[...] + bias_ref[...]
        dcum = dck_ref[...].T
        dlf = jnp.dot(_tri(blk, lower=False), dcum, precision=HI, preferred_element_type=F32) + carry_ref[0:1, :]
        carry_ref[...] = carry_ref[...] + jnp.sum(dcum, axis=0, keepdims=True)
        lane = lax.broadcasted_iota(jnp.int32, v.shape, 1)
        g = jnp.where(lane < F_LANE0, ddt_ref[...] * _sigmoid(v), dlf * _sigmoid(-v))
        g = jnp.where(lane < F_LANE0 + ATT_HEADS, g, 0.0)
        o_ref[...] = g.astype(o_ref.dtype)
        db_ref[...] += jnp.sum(g, axis=0, keepdims=True)

    col = pl.BlockSpec((blk, LANES), lambda i: (nb - 1 - i, 0))
    row = pl.BlockSpec((LANES, blk), lambda i: (0, nb - 1 - i))
    vec = pl.BlockSpec((1, LANES), lambda i: (0, 0))
    return pl.pallas_call(
        body, name="stats_bwd", grid=(nb,), in_specs=[col, vec, col, row], out_specs=[col, vec],
        out_shape=[jax.ShapeDtypeStruct((s, LANES), BF16), jax.ShapeDtypeStruct((1, LANES), F32)],
        scratch_shapes=[pltpu.VMEM((8, LANES), F32)],
        compiler_params=_cp("arbitrary"),
    )(fd, bias_row, ddt, dcum_rows)


HP = LANES // ATT_HEAD_DIM
N_HP = ATT_HEADS // HP


def _att_blocks(s):
    return (512, 1024) if s % 1024 == 0 and s >= 4096 else (64, 128)


_QK = (((1,), (1,)), ((), ()))
_HALF = ATT_HEAD_DIM // 2
_PAD = 16


def _head_cols(a):
    return slice(a * ATT_HEAD_DIM, (a + 1) * ATT_HEAD_DIM)


def _causal(shape, off):
    r = lax.broadcasted_iota(jnp.int32, shape, 0)
    c = lax.broadcasted_iota(jnp.int32, shape, 1)
    return c <= r + off


def _attention_fwd(qkv, ck4, gather_parts):
    s = qkv.shape[0]
    bk = _att_blocks(s)[1]
    bq = bk
    nq, nk = s // bq, s // bk
    n = len(gather_parts)

    def body(q_ref, k_ref, v_ref, ck_ref, *rest):
        comm_in, (o_ref, lse_ref), comm_out, sems = rest[:n], rest[n:n + 2], rest[n + 2:2 * n + 2], rest[2 * n + 2:]
        i = pl.program_id(1)
        if n:
            @pl.when((pl.program_id(0) == 0) & (i == 0))
            def _():
                for cp in _comm_copies(comm_in, comm_out, sems, False):
                    cp.start()

        n_full = (i * bq) // bk
        qs = [(q_ref[:, _head_cols(a)].astype(F32) * ATT_SCALE).astype(BF16) for a in range(HP)]

        upper_v = lax.broadcasted_iota(jnp.int32, (bk, LANES), 1) >= ATT_HEAD_DIM
        upper_q = lax.broadcasted_iota(jnp.int32, (bq, LANES), 1) >= ATT_HEAD_DIM

        def step(j, carry, off=None):
            ks = pl.ds(pl.multiple_of(j * bk, bk), bk)
            v_both = v_ref[ks, :]
            out = []
            for a in range(HP):
                m, acc = carry[a]
                sc = lax.dot_general(qs[a], k_ref[ks, _head_cols(a)], _QK, preferred_element_type=F32)
                sc = sc - ck_ref[0, a, pl.ds(j, 1), :]
                if off is not None:
                    sc = jnp.where(_causal(sc.shape, off), sc, NEG)
                m_new = jnp.maximum(m, jnp.max(sc, axis=1, keepdims=True))
                p = jnp.exp((sc - m_new).astype(BF16))
                v_aug = jnp.where(upper_v == (a == 1), v_both, jnp.ones_like(v_both))
                acc = jnp.exp(m - m_new) * acc + jnp.dot(p, v_aug, preferred_element_type=F32)
                out.append((m_new, acc))
            return tuple(out)

        init = tuple((jnp.full((bq, 1), NEG, F32), jnp.zeros((bq, LANES), F32)) for _ in range(HP))
        carry = lax.fori_loop(0, n_full, step, init)
        carry = step(n_full, carry, off=i * bq - n_full * bk)
        outs, lses = [], []
        for a in range(HP):
            m, acc = carry[a]
            l = pltpu.roll(acc, ATT_HEAD_DIM, 1)
            outs.append(acc / l)
            lses.append(m + jnp.log(l))
        o_ref[...] = jnp.where(upper_q, outs[1], outs[0])
        lse_ref[...] = jnp.where(upper_q, lses[1], lses[0])
        if n:
            @pl.when((pl.program_id(0) == N_HP - 1) & (i == nq - 1))
            def _():
                for cp in _comm_copies(comm_in, comm_out, sems, False):
                    cp.wait()

    q_spec = pl.BlockSpec((bq, LANES), lambda h, i: (i, h))
    anyspec = pl.BlockSpec(memory_space=pl.ANY)
    res = pl.pallas_call(
        body, name="att_fwd", grid=(N_HP, nq),
        in_specs=[q_spec, pl.BlockSpec((s, LANES), lambda h, i: (0, N_HP + h)),
                  pl.BlockSpec((s, LANES), lambda h, i: (0, 2 * N_HP + h)),
                  pl.BlockSpec((1, HP, nk, bk), lambda h, i: (h, 0, 0, 0))] + [anyspec] * n,
        out_specs=[q_spec, q_spec] + [anyspec] * n,
        out_shape=[jax.ShapeDtypeStruct((s, D_MODEL), F32)] * 2 + _comm_out_shapes(gather_parts, False),
        scratch_shapes=_comm_sems(n) if n else [],
        compiler_params=_cp("arbitrary", "arbitrary"),
    )(qkv, qkv, qkv, ck4, *gather_parts)
    return res[0], res[1], list(res[2:])


def _att_prep(do, o, lse_rep):
    s = do.shape[0]
    bs = _tile(s, 512)

    def body(do_ref, o_ref, lse_ref, st_ref, dob_ref):
        r = lax.broadcasted_iota(jnp.int32, (LANES, LANES), 0) // ATT_HEAD_DIM
        c = lax.broadcasted_iota(jnp.int32, (LANES, LANES), 1) // ATT_HEAD_DIM
        e = jnp.where(r == c, 1.0, 0.0).astype(F32)
        lane = lax.broadcasted_iota(jnp.int32, (bs, LANES), 1)
        for p in range(D_MODEL // LANES):
            cs = slice(p * LANES, (p + 1) * LANES)
            dd = do_ref[:, cs]
            delta = jnp.dot(dd * o_ref[:, cs], e, precision=HI, preferred_element_type=F32)
            st_ref[:, cs] = jnp.where(lane % ATT_HEAD_DIM < _HALF, lse_ref[:, cs], delta)
            dob_ref[:, cs] = dd.astype(BF16)

    spec = pl.BlockSpec((bs, D_MODEL), lambda i: (i, 0))
    return pl.pallas_call(body, name="att_prep", grid=(s // bs,), in_specs=[spec, spec, spec], out_specs=[spec, spec],
                          out_shape=[jax.ShapeDtypeStruct((s, D_MODEL), F32), jax.ShapeDtypeStruct((s, D_MODEL), BF16)],
                          compiler_params=_cp("parallel"))(do, o, lse_rep)


def _attention_bwd(qkv, ck4, st, do_b, exchange_parts):
    s = qkv.shape[0]
    bq, bk = _att_blocks(s)
    nq, nk, per = s // bq, s // bk, bk // bq
    _T = (((0,), (0,)), ((), ()))
    n = len(exchange_parts)

    def body(q_ref, k_ref, v_ref, ck_ref, st_ref, do_ref, *rest):
        comm_in, (dq_ref, dk_ref, dv_ref, dck_ref, dcq_ref) = rest[:n], rest[n:n + 5]
        comm_out, sems, (dk_acc, dv_acc) = rest[n + 5:2 * n + 5], rest[2 * n + 5:-2], rest[-2:]
        j = pl.program_id(1)
        if n:
            @pl.when((pl.program_id(0) == 0) & (j == 0))
            def _():
                for cp in _comm_copies(comm_in, comm_out, sems, True):
                    cp.start()

        @pl.when(j == 0)
        def _():
            dq_ref[...] = jnp.zeros_like(dq_ref)
            dcq_ref[...] = jnp.zeros_like(dcq_ref)

        dk_acc[...] = jnp.zeros_like(dk_acc)
        dv_acc[...] = jnp.zeros_like(dv_acc)

        ones_q, ones_k = jnp.ones((_PAD, bq), BF16), jnp.ones((_PAD, bk), BF16)
        k_t = [jnp.concatenate([k_ref[:, _head_cols(a)].T, ones_k], axis=0) for a in range(HP)]

        def step(i, off=None):
            rows = pl.ds(pl.multiple_of(i * bq, bq), bq)
            for a in range(HP):
                cs = _head_cols(a)
                q = (q_ref[rows, cs].astype(F32) * ATT_SCALE).astype(BF16)
                k = k_ref[:, cs]
                do_a = do_ref[rows, cs]
                sc = lax.dot_general(q, k, _QK, preferred_element_type=F32) - ck_ref[0, a, pl.ds(j, 1), :]
                if off is not None:
                    sc = jnp.where(_causal(sc.shape, off), sc, NEG)
                p = jnp.exp(sc - st_ref[rows, a * ATT_HEAD_DIM:a * ATT_HEAD_DIM + 1])
                dp = lax.dot_general(do_a, v_ref[:, cs], _QK, preferred_element_type=F32)
                ds = p * (dp - st_ref[rows, a * ATT_HEAD_DIM + _HALF:a * ATT_HEAD_DIM + _HALF + 1])
                ds_b = ds.astype(BF16)
                dv_acc[a] += jnp.dot(do_a.T, p.astype(BF16), preferred_element_type=F32)
                dk_acc[a] += jnp.dot(jnp.concatenate([q.T, ones_q], axis=0), ds_b, preferred_element_type=F32)
                dqs = lax.dot_general(k_t[a], ds_b, _QK, preferred_element_type=F32)
                dq_ref[cs, rows] += dqs[:ATT_HEAD_DIM] * ATT_SCALE
                dcq_ref[0, a, pl.ds(i, 1), :] += jnp.sum(dqs[ATT_HEAD_DIM:ATT_HEAD_DIM + 8], axis=0,
                                                         keepdims=True) * 0.125

        for t in range(per):
            step(j * per + t, off=t * bq)

        def full(i, c):
            step(i)
            return c

        lax.fori_loop((j + 1) * per, nq, full, 0)
        for a in range(HP):
            dk_ref[:, _head_cols(a)] = dk_acc[a, :ATT_HEAD_DIM].T.astype(dk_ref.dtype)
            dv_ref[:, _head_cols(a)] = dv_acc[a].T.astype(dv_ref.dtype)
            dck_ref[0, a, pl.ds(j, 1), :] = -dk_acc[a, ATT_HEAD_DIM:ATT_HEAD_DIM + 1]
        if n:
            @pl.when((pl.program_id(0) == N_HP - 1) & (j == nk - 1))
            def _():
                for cp in _comm_copies(comm_in, comm_out, sems, True):
                    cp.wait()

    res = pl.BlockSpec((s, LANES), lambda h, j: (0, h))
    ck_spec = pl.BlockSpec((1, HP, nk, bk), lambda h, j: (h, 0, 0, 0))
    kout = pl.BlockSpec((bk, LANES), lambda h, j: (j, h))
    anyspec = pl.BlockSpec(memory_space=pl.ANY)
    outs = pl.pallas_call(
        body, name="att_bwd", grid=(N_HP, nk),
        in_specs=[res, pl.BlockSpec((bk, LANES), lambda h, j: (j, N_HP + h)),
                  pl.BlockSpec((bk, LANES), lambda h, j: (j, 2 * N_HP + h)), ck_spec, res, res] + [anyspec] * n,
        out_specs=[pl.BlockSpec((LANES, s), lambda h, j: (h, 0)), kout, kout, ck_spec,
                   pl.BlockSpec((1, HP, nq, bq), lambda h, j: (h, 0, 0, 0))] + [anyspec] * n,
        out_shape=[jax.ShapeDtypeStruct((D_MODEL, s), F32), jax.ShapeDtypeStruct((s, D_MODEL), BF16),
                   jax.ShapeDtypeStruct((s, D_MODEL), BF16), jax.ShapeDtypeStruct((N_HP, HP, nk, bk), F32),
                   jax.ShapeDtypeStruct((N_HP, HP, nq, bq), F32)] + _comm_out_shapes(exchange_parts, True),
        scratch_shapes=(_comm_sems(n) if n else [])
        + [pltpu.VMEM((HP, ATT_HEAD_DIM + _PAD, bk), F32), pltpu.VMEM((HP, ATT_HEAD_DIM, bk), F32)],
        compiler_params=_cp("arbitrary", "arbitrary"),
    )(qkv, qkv, qkv, ck4, st, do_b, *exchange_parts)
    return outs[:5], list(outs[5:])


def _silu_and_grad(x):
    sg = _sigmoid(x)
    return x * sg, sg * (1.0 + x * (1.0 - sg))


def _conv_pre(cur, halo, w_ref, b_ref, first):
    halo = jnp.where(first, 0.0, halo)
    row = lax.broadcasted_iota(jnp.int32, cur.shape, 0)
    shifted = []
    for k in range(SSM_CONV):
        sh = SSM_CONV - 1 - k
        if sh == 0:
            shifted.append(cur)
            continue
        r = pltpu.roll(cur, sh, 0)
        hr = pltpu.roll(halo, sh, 0)
        top = jnp.where(row[0:8] < sh, hr, r[0:8])
        shifted.append(jnp.concatenate([top, r[8:]], axis=0))
    pre = b_ref[...] + sum(w_ref[k:k + 1, :] * shifted[k] for k in range(SSM_CONV))
    return pre, shifted


def _conv_specs(s, bs, bc):
    cur = pl.BlockSpec((bs, bc), lambda j, i: (i, j))
    halo = pl.BlockSpec((8, bc), lambda j, i: (jnp.maximum(i * (bs // 8) - 1, 0), j))
    w = pl.BlockSpec((SSM_CONV, bc), lambda j, i: (0, j))
    b = pl.BlockSpec((1, bc), lambda j, i: (0, j))
    return cur, halo, w, b


def _conv_fwd(xbc, w, b):
    s, c = xbc.shape
    bs, bc = _tile(s, 512), 1024

    def body(x_ref, h_ref, w_ref, b_ref, o_ref):
        pre, _ = _conv_pre(x_ref[...], h_ref[...], w_ref, b_ref, pl.program_id(1) == 0)
        o_ref[...] = pre * _sigmoid(pre)

    cur, halo, ws, bsp = _conv_specs(s, bs, bc)
    return pl.pallas_call(body, name="conv_fwd", grid=(c // bc, s // bs), in_specs=[cur, halo, ws, bsp],
                          out_specs=cur, out_shape=jax.ShapeDtypeStruct((s, c), F32),
                          compiler_params=_cp("parallel", "parallel"))(xbc, xbc, w, b)


def _conv_bwd_pre(xbc, w, b, dact):
    s, c = xbc.shape
    bs, bc = _tile(s, 512), 1024

    def body(x_ref, h_ref, w_ref, b_ref, g_ref, dp_ref, dw_ref, db_ref):
        @pl.when(pl.program_id(1) == 0)
        def _():
            dw_ref[...] = jnp.zeros_like(dw_ref)
            db_ref[...] = jnp.zeros_like(db_ref)

        pre, shifted = _conv_pre(x_ref[...], h_ref[...], w_ref, b_ref, pl.program_id(1) == 0)
        dpre = g_ref[...] * _silu_and_grad(pre)[1]
        dp_ref[...] = dpre
        db_ref[...] += jnp.sum(dpre, axis=0, keepdims=True)
        for k in range(SSM_CONV):
            dw_ref[k:k + 1, :] += jnp.sum(dpre * shifted[k], axis=0, keepdims=True)

    cur, halo, ws, bsp = _conv_specs(s, bs, bc)
    return pl.pallas_call(
        body, name="conv_bwd_pre", grid=(c // bc, s // bs), in_specs=[cur, halo, ws, bsp, cur],
        out_specs=[cur, ws, bsp],
        out_shape=[jax.ShapeDtypeStruct((s, c), F32), jax.ShapeDtypeStruct((SSM_CONV, c), F32),
                   jax.ShapeDtypeStruct((1, c), F32)],
        compiler_params=_cp("parallel", "arbitrary"))(xbc, xbc, w, b, dact)


def _conv_bwd_in(dpre, w):
    s, c = dpre.shape
    bs, bc = _tile(s, 512), 1024
    nb = s // bs

    def body(g_ref, n_ref, w_ref, o_ref):
        cur = g_ref[...]
        nxt = jnp.where(pl.program_id(1) == nb - 1, 0.0, n_ref[...])
        row = lax.broadcasted_iota(jnp.int32, cur.shape, 0)
        acc = w_ref[SSM_CONV - 1:SSM_CONV, :] * cur
        for sh in range(1, SSM_CONV):
            r = pltpu.roll(cur, bs - sh, 0)
            nr = pltpu.roll(nxt, 8 - sh, 0)
            bot = jnp.where(row[0:8] >= 8 - sh, nr, r[bs - 8:])
            acc = acc + w_ref[SSM_CONV - 1 - sh:SSM_CONV - sh, :] * jnp.concatenate([r[:bs - 8], bot], axis=0)
        o_ref[...] = acc.astype(o_ref.dtype)

    cur = pl.BlockSpec((bs, bc), lambda j, i: (i, j))
    nxt = pl.BlockSpec((8, bc), lambda j, i: (jnp.minimum((i + 1) * (bs // 8), s // 8 - 1), j))
    ws = pl.BlockSpec((SSM_CONV, bc), lambda j, i: (0, j))
    return pl.pallas_call(body, name="conv_bwd_in", grid=(c // bc, nb), in_specs=[cur, nxt, ws], out_specs=cur,
                          out_shape=jax.ShapeDtypeStruct((s, c), BF16),
                          compiler_params=_cp("parallel", "parallel"))(dpre, dpre, w)


def _dotT(a, b):
    return lax.dot_general(a.astype(BF16), b.astype(BF16), (((1,), (1,)), ((), ())), preferred_element_type=F32)


def _dot(a, b):
    return jnp.dot(a.astype(BF16), b.astype(BF16), preferred_element_type=F32)


N_PAIR = SSM_HEADS // HP
PAIRS_PER_GROUP = SSM_HEADS_PER_GROUP // HP


def _pair_consts():
    L = SSM_CHUNK
    lane = lax.broadcasted_iota(jnp.int32, (L, LANES), 1)
    lane1 = lax.broadcasted_iota(jnp.int32, (1, LANES), 1)
    li = lax.broadcasted_iota(jnp.int32, (L, L), 0)
    si = lax.broadcasted_iota(jnp.int32, (L, L), 1)
    return lane >= ATT_HEAD_DIM, lane1 >= ATT_HEAD_DIM, li, si


def _ssd_pair_fwd(xbc_act, ac_c, dt_r, ac_r, dsk_pair):
    s = xbc_act.shape[0]
    L, N, G = SSM_CHUNK, SSM_STATE, SSM_GROUPS
    nc = s // L

    def body(xbc_ref, ac_ref, dtr_ref, acr_ref, dsk_ref, y_ref, hp_ref, st_ref):
        @pl.when(pl.program_id(0) == 0)
        def _():
            st_ref[...] = jnp.zeros_like(st_ref)

        upper, up1, li, si = _pair_consts()
        for g in range(G):
            b_g = xbc_ref[:, SSM_INNER + g * N:SSM_INNER + (g + 1) * N]
            c_g = xbc_ref[:, SSM_INNER + G * N + g * N:SSM_INNER + G * N + (g + 1) * N]
            cb = _dotT(c_g, b_g)
            b_t = b_g.T
            for q in range(PAIRS_PER_GROUP):
                pp = g * PAIRS_PER_GROUP + q
                cols = slice(pp * LANES, (pp + 1) * LANES)
                xs = xbc_ref[:, cols]
                ht = st_ref[pp]
                hp_ref[0, pp] = ht
                y = dsk_ref[pp:pp + 1, :] * xs
                s_new = jnp.zeros((N, LANES), F32)
                ea, el = [], []
                for a in range(HP):
                    h = HP * pp + a
                    acol = jnp.broadcast_to(ac_ref[:, h:h + 1], (L, LANES))
                    arow, dtrow = acr_ref[h:h + 1, :], dtr_ref[h:h + 1, :]
                    alast = ac_ref[L - 1:L, h:h + 1]
                    decay = jnp.exp(jnp.where(li >= si, acol - arow, NEG))
                    xs_a = jnp.where(upper == (a == 1), xs, 0.0)
                    y = y + _dot(cb * decay * dtrow, xs_a)
                    s_new = s_new + _dot(b_t * (dtrow * jnp.exp(alast - arow)), xs_a)
                    ea.append(jnp.exp(acol))
                    el.append(jnp.exp(alast))
                y_ref[:, cols] = y + jnp.where(upper, ea[1], ea[0]) * _dot(c_g, ht)
                st_ref[pp] = ht * jnp.where(up1, el[1], el[0]) + s_new

    col = pl.BlockSpec((L, LANES), lambda c: (c, 0))
    row = pl.BlockSpec((LANES, L), lambda c: (0, c))
    return pl.pallas_call(
        body, name="ssd_fwd", grid=(nc,),
        in_specs=[pl.BlockSpec((L, SSM_CONV_DIM), lambda c: (c, 0)), col, row, row,
                  pl.BlockSpec((N_PAIR, LANES), lambda c: (0, 0))],
        out_specs=[pl.BlockSpec((L, SSM_INNER), lambda c: (c, 0)),
                   pl.BlockSpec((1, N_PAIR, N, LANES), lambda c: (c, 0, 0, 0))],
        out_shape=[jax.ShapeDtypeStruct((s, SSM_INNER), F32), jax.ShapeDtypeStruct((nc, N_PAIR, N, LANES), F32)],
        scratch_shapes=[pltpu.VMEM((N_PAIR, N, LANES), F32)],
        compiler_params=_cp("arbitrary"),
    )(xbc_act, ac_c, dt_r, ac_r, dsk_pair)


def _ssd_pair_bwd(xbc_act, dt_c, ac_c, dt_r, ac_r, hprev_all, dy, dsk_pair, a_row):
    s = xbc_act.shape[0]
    L, N, G = SSM_CHUNK, SSM_STATE, SSM_GROUPS
    nc = s // L
    rev = lambda c: nc - 1 - c

    def body(xbc_ref, dt_ref, ac_ref, dtr_ref, acr_ref, hp_ref, dy_ref, dsk_ref, arow_ref,
             dx_ref, ddt_ref, da_ref, dds_ref, dh_ref):
        @pl.when(pl.program_id(0) == 0)
        def _():
            dh_ref[...] = jnp.zeros_like(dh_ref)
            da_ref[...] = jnp.zeros_like(da_ref)
            dds_ref[...] = jnp.zeros_like(dds_ref)

        upper, up1, li, si = _pair_consts()
        lane = lax.broadcasted_iota(jnp.int32, (L, LANES), 1)
        sub = lax.broadcasted_iota(jnp.int32, (LANES, L), 0)
        lastrow = lax.broadcasted_iota(jnp.int32, (L, LANES), 0) == L - 1
        da_c = jnp.zeros((L, LANES), F32)
        da_r = jnp.zeros((LANES, L), F32)
        ddt_r = jnp.zeros((LANES, L), F32)
        for g in range(G):
            b_g = xbc_ref[:, SSM_INNER + g * N:SSM_INNER + (g + 1) * N]
            c_g = xbc_ref[:, SSM_INNER + G * N + g * N:SSM_INNER + G * N + (g + 1) * N]
            cb, cb_t = _dotT(c_g, b_g), _dotT(b_g, c_g)
            b_t, c_t = b_g.T, c_g.T
            dcb = jnp.zeros((L, L), F32)
            db_t = jnp.zeros((N, L), F32)
            dc = jnp.zeros((L, N), F32)
            for q in range(PAIRS_PER_GROUP):
                pp = g * PAIRS_PER_GROUP + q
                cols = slice(pp * LANES, (pp + 1) * LANES)
                xs, gy = xbc_ref[:, cols], dy_ref[:, cols]
                ht, dhn = hp_ref[0, pp], dh_ref[pp]
                acol = [jnp.broadcast_to(ac_ref[:, HP * pp + a:HP * pp + a + 1], (L, LANES)) for a in range(HP)]
                alast = [ac_ref[L - 1:L, HP * pp + a:HP * pp + a + 1] for a in range(HP)]
                ea = jnp.where(upper, jnp.exp(acol[1]), jnp.exp(acol[0]))
                el = jnp.where(up1, jnp.exp(alast[1]), jnp.exp(alast[0]))
                ge = gy * ea
                dc = dc + _dotT(ge, ht)
                dh_ref[pp] = _dot(c_t, ge) + dhn * el
                t_off = (ge * _dot(c_g, ht)).astype(BF16)
                hsum = jnp.sum(dhn * ht, axis=0, keepdims=True)
                dxs = dsk_ref[pp:pp + 1, :] * gy
                dds_ref[pp:pp + 1, :] += jnp.sum(gy * xs, axis=0, keepdims=True)
                for a in range(HP):
                    h = HP * pp + a
                    mine, mine1 = upper == (a == 1), up1 == (a == 1)
                    arow, dtrow = acr_ref[h:h + 1, :], dtr_ref[h:h + 1, :]
                    dtcol = jnp.broadcast_to(dt_ref[:, h:h + 1], (L, LANES))
                    xs_a, gy_a = jnp.where(mine, xs, 0.0), jnp.where(mine, gy, 0.0)
                    dhn_a = jnp.where(mine1, dhn, 0.0)
                    e_row = jnp.exp(alast[a] - arow)
                    w_row = dtrow * e_row
                    xd_t = _dotT(dhn_a, xs_a)
                    db_t = db_t + xd_t * w_row
                    dw = jnp.sum(b_t * xd_t, axis=0, keepdims=True)
                    de_e = dw * w_row
                    dal = (jnp.sum(jnp.where(mine1, hsum, 0.0), axis=1, keepdims=True) * jnp.exp(alast[a])
                           + jnp.sum(de_e, axis=1, keepdims=True))
                    dxs = dxs + _dot(b_g, dhn_a) * (dtcol * jnp.exp(alast[a] - acol[a]))
                    decay = jnp.exp(jnp.where(li >= si, acol[a] - arow, NEG))
                    decay_t = jnp.exp(jnp.where(si >= li, arow - acol[a], NEG))
                    m = cb * decay
                    dmdt = _dotT(gy_a, xs_a)
                    dxs = dxs + _dot(cb_t * decay_t * dtcol, gy_a)
                    dm = dmdt * dtrow
                    dcb = dcb + dm * decay
                    wb = (dm * m).astype(BF16)
                    onehot = jnp.where(lane == h, 1.0, 0.0).astype(BF16)
                    da_c = (da_c + jnp.dot(wb, onehot, preferred_element_type=F32)
                            + jnp.dot(jnp.where(mine, t_off, 0.0).astype(BF16), onehot, preferred_element_type=F32)
                            + jnp.where(lastrow & (lane == h), dal, 0.0))
                    da_r = jnp.where(sub == h, -(jnp.sum(wb.astype(F32), axis=0, keepdims=True) + de_e), da_r)
                    ddt_r = jnp.where(sub == h, dw * e_row + jnp.sum(dmdt * m, axis=0, keepdims=True), ddt_r)
                dx_ref[:, cols] = dxs
            dx_ref[:, SSM_INNER + g * N:SSM_INNER + (g + 1) * N] = (db_t + _dot(c_t, dcb)).T
            dx_ref[:, SSM_INNER + G * N + g * N:SSM_INNER + G * N + (g + 1) * N] = dc + _dot(dcb, b_g)
        dda = jnp.dot(_tri(L, lower=False), da_c + da_r.T, precision=HI, preferred_element_type=F32)
        ddt_ref[...] = dda * arow_ref[...] + ddt_r.T
        da_ref[...] += jnp.sum(dda * dt_ref[...], axis=0, keepdims=True)

    col = pl.BlockSpec((L, LANES), lambda c: (rev(c), 0))
    row = pl.BlockSpec((LANES, L), lambda c: (0, rev(c)))
    vec = pl.BlockSpec((1, LANES), lambda c: (0, 0))
    pairs = pl.BlockSpec((N_PAIR, LANES), lambda c: (0, 0))
    return pl.pallas_call(
        body, name="ssd_bwd", grid=(nc,),
        in_specs=[pl.BlockSpec((L, SSM_CONV_DIM), lambda c: (rev(c), 0)), col, col, row, row,
                  pl.BlockSpec((1, N_PAIR, N, LANES), lambda c: (rev(c), 0, 0, 0)),
                  pl.BlockSpec((L, SSM_INNER), lambda c: (rev(c), 0)), pairs, vec],
        out_specs=[pl.BlockSpec((L, SSM_CONV_DIM), lambda c: (rev(c), 0)), col, vec, pairs],
        out_shape=[jax.ShapeDtypeStruct((s, SSM_CONV_DIM), F32), jax.ShapeDtypeStruct((s, LANES), F32),
                   jax.ShapeDtypeStruct((1, LANES), F32), jax.ShapeDtypeStruct((N_PAIR, LANES), F32)],
        scratch_shapes=[pltpu.VMEM((N_PAIR, N, LANES), F32)],
        compiler_params=_cp("arbitrary"),
    )(xbc_act, dt_c, ac_c, dt_r, ac_r, hprev_all, dy, dsk_pair, a_row)


ROWS = 512
GW = SSM_INNER // SSM_GROUPS


def _rows(width, dtype=F32):
    return pl.BlockSpec((ROWS, width), lambda i: (i, 0))


def _vec(width):
    return pl.BlockSpec((1, width), lambda i: (0, 0))


def _gnorm_fwd(y, z, w):
    s = y.shape[0]

    def body(y_ref, z_ref, w_ref, o_ref):
        for g in range(SSM_GROUPS):
            cs = slice(g * GW, (g + 1) * GW)
            zz = z_ref[:, cs].astype(F32)
            u = y_ref[:, cs] * (zz * _sigmoid(zz))
            r = lax.rsqrt(jnp.mean(u * u, axis=1, keepdims=True) + RMS_EPS)
            o_ref[:, cs] = (u * r * w_ref[:, cs]).astype(o_ref.dtype)

    return pl.pallas_call(body, name="gnorm_fwd", grid=(s // ROWS,),
                          in_specs=[_rows(SSM_INNER), _rows(SSM_INNER), _vec(SSM_INNER)], out_specs=_rows(SSM_INNER),
                          out_shape=jax.ShapeDtypeStruct((s, SSM_INNER), BF16), compiler_params=_cp("parallel"))(y, z, w)


def _gnorm_bwd(y, z, w, do):
    s = y.shape[0]

    def body(y_ref, z_ref, w_ref, do_ref, dy_ref, dz_ref, dw_ref):
        @pl.when(pl.program_id(0) == 0)
        def _():
            dw_ref[...] = jnp.zeros_like(dw_ref)

        for g in range(SSM_GROUPS):
            cs = slice(g * GW, (g + 1) * GW)
            zz, yy, dd = z_ref[:, cs].astype(F32), y_ref[:, cs], do_ref[:, cs].astype(F32)
            sz, dsz = _silu_and_grad(zz)
            u = yy * sz
            r = lax.rsqrt(jnp.mean(u * u, axis=1, keepdims=True) + RMS_EPS)
            n = u * r
            dn = dd * w_ref[:, cs]
            dw_ref[:, cs] += jnp.sum(dd * n, axis=0, keepdims=True)
            du = r * (dn - n * jnp.mean(dn * n, axis=1, keepdims=True))
            dy_ref[:, cs] = du * sz
            dz_ref[:, cs] = (du * yy * dsz).astype(dz_ref.dtype)

    return pl.pallas_call(
        body, name="gnorm_bwd", grid=(s // ROWS,),
        in_specs=[_rows(SSM_INNER), _rows(SSM_INNER), _vec(SSM_INNER), _rows(SSM_INNER)],
        out_specs=[_rows(SSM_INNER), _rows(SSM_INNER), _vec(SSM_INNER)],
        out_shape=[jax.ShapeDtypeStruct((s, SSM_INNER), F32), jax.ShapeDtypeStruct((s, SSM_INNER), BF16),
                   jax.ShapeDtypeStruct((1, SSM_INNER), F32)],
        compiler_params=_cp("arbitrary"))(y, z, w, do)


def _mix_fwd(gl, bg, attn_d, ssm_d):
    s = gl.shape[0]
    d = D_MODEL

    def body(gl_ref, bg_ref, a_ref, m_ref, o_ref):
        g0 = _sigmoid(gl_ref[:, :d] + bg_ref[:, :d])
        g1 = _sigmoid(gl_ref[:, d:] + bg_ref[:, d:])
        o_ref[...] = (g0 * a_ref[...] + g1 * m_ref[...]).astype(o_ref.dtype)

    return pl.pallas_call(body, name="mix_fwd", grid=(s // ROWS,),
                          in_specs=[_rows(2 * d), _vec(2 * d), _rows(d), _rows(d)], out_specs=_rows(d),
                          out_shape=jax.ShapeDtypeStruct((s, d), BF16), compiler_params=_cp("parallel"))(
        gl, bg, attn_d, ssm_d)


def _mix_bwd(gl, bg, attn_d, ssm_d, dmix):
    s = gl.shape[0]
    d = D_MODEL

    def body(gl_ref, bg_ref, a_ref, m_ref, dm_ref, da_ref, ds_ref, dg_ref, db_ref):
        @pl.when(pl.program_id(0) == 0)
        def _():
            db_ref[...] = jnp.zeros_like(db_ref)

        g0 = _sigmoid(gl_ref[:, :d] + bg_ref[:, :d])
        g1 = _sigmoid(gl_ref[:, d:] + bg_ref[:, d:])
        dm = dm_ref[...].astype(F32)
        da_ref[...] = (dm * g0).astype(da_ref.dtype)
        ds_ref[...] = (dm * g1).astype(ds_ref.dtype)
        dl0 = dm * a_ref[...] * g0 * (1.0 - g0)
        dl1 = dm * m_ref[...] * g1 * (1.0 - g1)
        dg_ref[:, :d] = dl0.astype(dg_ref.dtype)
        dg_ref[:, d:] = dl1.astype(dg_ref.dtype)
        db_ref[:, :d] += jnp.sum(dl0, axis=0, keepdims=True)
        db_ref[:, d:] += jnp.sum(dl1, axis=0, keepdims=True)

    return pl.pallas_call(
        body, name="mix_bwd", grid=(s // ROWS,),
        in_specs=[_rows(2 * d), _vec(2 * d), _rows(d), _rows(d), _rows(d)],
        out_specs=[_rows(d), _rows(d), _rows(2 * d), _vec(2 * d)],
        out_shape=[jax.ShapeDtypeStruct((s, d), BF16), jax.ShapeDtypeStruct((s, d), BF16),
                   jax.ShapeDtypeStruct((s, 2 * d), BF16), jax.ShapeDtypeStruct((1, 2 * d), F32)],
        compiler_params=_cp("arbitrary"))(gl, bg, attn_d, ssm_d, dmix)


def _ln_stats(p):
    mu = jnp.mean(p, axis=1, keepdims=True)
    c = p - mu
    rstd = lax.rsqrt(jnp.mean(c * c, axis=1, keepdims=True) + LN_EPS)
    return c * rstd, rstd


def _ln_bwd(dy, xhat, rstd, g):
    dxh = dy * g
    return rstd * (dxh - jnp.mean(dxh, axis=1, keepdims=True) - xhat * jnp.mean(dxh * xhat, axis=1, keepdims=True))


def _ln1_fwd(x, mixed, g, b):
    s, d = x.shape

    def body(x_ref, m_ref, g_ref, b_ref, o_ref, ob_ref):
        xhat, _ = _ln_stats(DEEPNORM_ALPHA * x_ref[...] + m_ref[...])
        y = xhat * g_ref[...] + b_ref[...]
        o_ref[...] = y
        ob_ref[...] = y.astype(BF16)

    return pl.pallas_call(body, name="ln1_fwd", grid=(s // ROWS,), in_specs=[_rows(d), _rows(d), _vec(d), _vec(d)],
                          out_specs=[_rows(d), _rows(d)],
                          out_shape=[jax.ShapeDtypeStruct((s, d), F32), jax.ShapeDtypeStruct((s, d), BF16)],
                          compiler_params=_cp("parallel"))(x, mixed, g, b)


def _ln2_loss(x1, h, target, g, b):
    s, d = x1.shape

    def body(x_ref, h_ref, t_ref, g_ref, b_ref, dp_ref, dpb_ref, loss_ref, dg_ref, db_ref):
        @pl.when(pl.program_id(0) == 0)
        def _():
            loss_ref[...] = jnp.zeros_like(loss_ref)
            dg_ref[...] = jnp.zeros_like(dg_ref)
            db_ref[...] = jnp.zeros_like(db_ref)

        xhat, rstd = _ln_stats(DEEPNORM_ALPHA * x_ref[...] + h_ref[...])
        err = xhat * g_ref[...] + b_ref[...] - t_ref[...]
        part = 0.5 * jnp.sum(jnp.mean(err * err, axis=1, keepdims=True), axis=0, keepdims=True)
        loss_ref[...] += jnp.broadcast_to(part, loss_ref.shape)
        dy = err * (1.0 / d)
        dg_ref[...] += jnp.sum(dy * xhat, axis=0, keepdims=True)
        db_ref[...] += jnp.sum(dy, axis=0, keepdims=True)
        dp = _ln_bwd(dy, xhat, rstd, g_ref[...])
        dp_ref[...] = dp
        dpb_ref[...] = dp.astype(BF16)

    return pl.pallas_call(
        body, name="ln2_loss", grid=(s // ROWS,), in_specs=[_rows(d), _rows(d), _rows(d), _vec(d), _vec(d)],
        out_specs=[_rows(d), _rows(d), _vec(LANES), _vec(d), _vec(d)],
        out_shape=[jax.ShapeDtypeStruct((s, d), F32), jax.ShapeDtypeStruct((s, d), BF16),
                   jax.ShapeDtypeStruct((1, LANES), F32),
                   jax.ShapeDtypeStruct((1, d), F32), jax.ShapeDtypeStruct((1, d), F32)],
        compiler_params=_cp("arbitrary"))(x1, h, target, g, b)


def _ln1_bwd(x, mixed, g, dpre2, dffn):
    s, d = x.shape

    def body(x_ref, m_ref, g_ref, d2_ref, df_ref, dp_ref, dr_ref, dg_ref, db_ref):
        @pl.when(pl.program_id(0) == 0)
        def _():
            dg_ref[...] = jnp.zeros_like(dg_ref)
            db_ref[...] = jnp.zeros_like(db_ref)

        xhat, rstd = _ln_stats(DEEPNORM_ALPHA * x_ref[...] + m_ref[...])
        dy = DEEPNORM_ALPHA * d2_ref[...] + df_ref[...]
        dg_ref[...] += jnp.sum(dy * xhat, axis=0, keepdims=True)
        db_ref[...] += jnp.sum(dy, axis=0, keepdims=True)
        dp = _ln_bwd(dy, xhat, rstd, g_ref[...])
        dp_ref[...] = dp.astype(BF16)
        dr_ref[...] = DEEPNORM_ALPHA * dp

    return pl.pallas_call(
        body, name="ln1_bwd", grid=(s // ROWS,), in_specs=[_rows(d), _rows(d), _vec(d), _rows(d), _rows(d)],
        out_specs=[_rows(d), _rows(d), _vec(d), _vec(d)],
        out_shape=[jax.ShapeDtypeStruct((s, d), BF16), jax.ShapeDtypeStruct((s, d), F32),
                   jax.ShapeDtypeStruct((1, d), F32), jax.ShapeDtypeStruct((1, d), F32)],
        compiler_params=_cp("arbitrary"))(x, mixed, g, dpre2, dffn)


def _swiglu_fwd(gu):
    s = gu.shape[0]
    f = FFN_HIDDEN

    def body(g_ref, u_ref, o_ref):
        gg = g_ref[...].astype(F32)
        o_ref[...] = (gg * _sigmoid(gg) * u_ref[...].astype(F32)).astype(o_ref.dtype)

    return pl.pallas_call(
        body, name="swiglu_fwd", grid=(s // ROWS,),
        in_specs=[pl.BlockSpec((ROWS, f), lambda i: (i, 0)), pl.BlockSpec((ROWS, f), lambda i: (i, 1))],
        out_specs=_rows(f), out_shape=jax.ShapeDtypeStruct((s, f), BF16), compiler_params=_cp("parallel"))(gu, gu)


def _swiglu_bwd(gu, dact):
    s = gu.shape[0]
    f = FFN_HIDDEN

    def body(g_ref, u_ref, d_ref, o_ref):
        sg, dsg = _silu_and_grad(g_ref[...].astype(F32))
        dd = d_ref[...].astype(F32)
        o_ref[:, :f] = (dd * u_ref[...].astype(F32) * dsg).astype(o_ref.dtype)
        o_ref[:, f:] = (dd * sg).astype(o_ref.dtype)

    return pl.pallas_call(
        body, name="swiglu_bwd", grid=(s // ROWS,),
        in_specs=[pl.BlockSpec((ROWS, f), lambda i: (i, 0)), pl.BlockSpec((ROWS, f), lambda i: (i, 1)), _rows(f)],
        out_specs=_rows(2 * f), out_shape=jax.ShapeDtypeStruct((s, 2 * f), BF16),
        compiler_params=_cp("parallel"))(gu, gu, dact)


def _peer(k):
    x, y, c = lax.axis_index("x"), lax.axis_index("y"), lax.axis_index("c")
    kx, ky, kc = (k >> 2) & 1, (k >> 1) & 1, k & 1
    px = (1 - x) if kx else x
    py = (1 - y) if ky else y
    pc = (1 - c) if kc else c
    return (px, py, pc), 4 * px + 2 * py + pc


def _my_index():
    return 4 * lax.axis_index("x") + 2 * lax.axis_index("y") + lax.axis_index("c")


def _comm_copies(ins, outs, sems, scatter):
    send_sems, recv_sems, local_sems = sems
    me = _my_index()
    copies = [pltpu.make_async_copy(ins[t].at[me] if scatter else ins[t], outs[t].at[me], local_sems.at[t])
              for t in range(len(ins))]
    for k in range(1, N_DEV):
        peer, pidx = _peer(k)
        for t in range(len(ins)):
            copies.append(pltpu.make_async_remote_copy(
                src_ref=ins[t].at[pidx] if scatter else ins[t], dst_ref=outs[t].at[me],
                send_sem=send_sems.at[t, k - 1], recv_sem=recv_sems.at[t, k - 1], device_id=peer,
                device_id_type=pl.DeviceIdType.MESH))
    return copies


def _comm_sems(n):
    return [pltpu.SemaphoreType.DMA((n, N_DEV - 1)), pltpu.SemaphoreType.DMA((n, N_DEV - 1)),
            pltpu.SemaphoreType.DMA((n,))]


def _comm_out_shapes(parts, scatter):
    return [jax.ShapeDtypeStruct(p.shape if scatter else (N_DEV,) + p.shape, p.dtype) for p in parts]


def _all_gather(parts):
    n = len(parts)

    def body(*refs):
        ins, outs = refs[:n], refs[n:2 * n]
        send_sems, recv_sems, local_sems = refs[2 * n:]
        x, y, c = lax.axis_index("x"), lax.axis_index("y"), lax.axis_index("c")
        me, sibling = (x, y, c), (x, y, 1 - c)
        chips = [(1 - x, y), (x, 1 - y), (1 - x, 1 - y)]

        def copy(t, k, block, to, src=None):
            dst = outs[t].at[4 * block[0] + 2 * block[1] + block[2]]
            return pltpu.make_async_remote_copy(
                src_ref=dst if src is None else src, dst_ref=dst, send_sem=send_sems.at[t, k],
                recv_sem=recv_sems.at[t, k], device_id=to, device_id_type=pl.DeviceIdType.MESH)

        mine = [pltpu.make_async_copy(ins[t], outs[t].at[_my_index()], local_sems.at[t]) for t in range(n)]
        for cp in mine:
            cp.start()
        first = [copy(t, 0, me, sibling, src=ins[t]) for t in range(n)]
        first += [copy(t, 1 + j, me, (*chip, c), src=ins[t]) for j, chip in enumerate(chips) for t in range(n)]
        for cp in first:
            cp.start()
        passed = []
        for j, chip in enumerate(chips):
            for t in range(n):
                copy(t, 1 + j, (*chip, c), me).wait_recv()
                passed.append(copy(t, 4 + j, (*chip, c), sibling))
                passed[-1].start()
        for t in range(n):
            copy(t, 0, sibling, me).wait_recv()
            for j, chip in enumerate(chips):
                copy(t, 4 + j, (*chip, 1 - c), me).wait_recv()
        for cp in first + passed:
            cp.wait_send()
        for cp in mine:
            cp.wait()

    anyspec = pl.BlockSpec(memory_space=pl.ANY)
    return pl.pallas_call(body, name="all_gather", in_specs=[anyspec] * n, out_specs=[anyspec] * n,
                          out_shape=_comm_out_shapes(parts, False), scratch_shapes=_comm_sems(n))(*parts)


def _remote_scatter_copies(ins, lands, send_sems, recv_sems):
    me = _my_index()
    copies = []
    for k in range(1, N_DEV):
        peer, pidx = _peer(k)
        for t in range(len(ins)):
            copies.append(pltpu.make_async_remote_copy(
                src_ref=ins[t].at[pidx], dst_ref=lands[t].at[me], send_sem=send_sems.at[t * (N_DEV - 1) + k - 1],
                recv_sem=recv_sems.at[t * (N_DEV - 1) + k - 1], device_id=peer, device_id_type=pl.DeviceIdType.MESH))
    return copies


def _landing_zones(parts):
    me = _my_index()
    return [jnp.where(lax.broadcasted_iota(jnp.int32, p.shape, 0) == me, p, jnp.zeros_like(p)) for p in parts]


_HBM = pl.BlockSpec(memory_space=pltpu.HBM)
_SEM = pl.BlockSpec(memory_space=pltpu.SEMAPHORE)


def _exchange_start(parts, lands):
    n = len(parts)

    def body(*refs):
        ins, lnd, send_sems, recv_sems, token = refs[:n], refs[n:2 * n], refs[2 * n], refs[2 * n + 1], refs[-1]
        for cp in _remote_scatter_copies(ins, lnd, send_sems, recv_sems):
            cp.start()
        token[...] = jnp.zeros_like(token)

    hbm = [pltpu.HBM(p.shape, p.dtype) for p in parts]
    outs = pl.pallas_call(
        body, name="exchange_start",
        out_shape=[pltpu.SemaphoreType.DMA((n * (N_DEV - 1),)), pltpu.SemaphoreType.DMA((n * (N_DEV - 1),))] + hbm + hbm
        + [jax.ShapeDtypeStruct((8, LANES), F32)],
        in_specs=[_HBM] * (2 * n), out_specs=[_SEM, _SEM] + [_HBM] * (2 * n) + [pl.BlockSpec(memory_space=pltpu.VMEM)],
        input_output_aliases={t: 2 + t for t in range(2 * n)},
        compiler_params=pltpu.CompilerParams(has_side_effects=pltpu.SideEffectType.DATAFLOW_SIDE_EFFECTING),
    )(*[pltpu.with_memory_space_constraint(p, pltpu.HBM) for p in list(parts) + list(lands)])
    return outs[0], outs[1], list(outs[2:2 + n]), list(outs[2 + n:2 + 2 * n]), outs[-1]


def _exchange_wait(send_sems, recv_sems, parts, lands, after):
    n = len(parts)

    def body(*refs):
        ins, lnd, send_sems, recv_sems = refs[:n], refs[n:2 * n], refs[2 * n], refs[2 * n + 1]
        for cp in _remote_scatter_copies(ins, lnd, send_sems, recv_sems):
            cp.wait_send()
            cp.wait_recv()

    hbm = [pltpu.HBM(p.shape, p.dtype) for p in parts]
    outs = pl.pallas_call(
        body, name="exchange_wait", out_shape=hbm + hbm,
        in_specs=[_HBM] * (2 * n) + [_SEM, _SEM, pl.BlockSpec(memory_space=pl.ANY)], out_specs=[_HBM] * (2 * n),
        input_output_aliases={t: t for t in range(2 * n)},
        compiler_params=pltpu.CompilerParams(has_side_effects=pltpu.SideEffectType.DATAFLOW_SIDE_EFFECTING),
    )(*parts, *lands, send_sems, recv_sems, after)
    return list(outs[n:])


def _adamw(recv, w, m, v, name):
    _, r, c = w.shape
    br = _tile(r, 128)
    c1 = 1.0 / (1.0 - ADAM_B1 ** ADAM_STEP)
    c2 = 1.0 / (1.0 - ADAM_B2 ** ADAM_STEP)

    def body(r_ref, w_ref, m_ref, v_ref, g_ref, d_ref, mo_ref, vo_ref):
        g = r_ref[0].astype(F32)
        for k in range(1, N_DEV):
            g = g + r_ref[k].astype(F32)
        mn = ADAM_B1 * m_ref[0] + (1.0 - ADAM_B1) * g
        vn = ADAM_B2 * v_ref[0] + (1.0 - ADAM_B2) * (g * g)
        g_ref[0] = g
        mo_ref[0] = mn
        vo_ref[0] = vn
        d_ref[0] = -ADAM_LR * ((mn * c1) / (jnp.sqrt(vn * c2) + ADAM_EPS) + ADAM_WD * w_ref[0])

    blk = pl.BlockSpec((1, br, c), lambda i: (0, i, 0))
    return pl.pallas_call(
        body, name=name, grid=(r // br,),
        in_specs=[pl.BlockSpec((N_DEV, br, c), lambda i: (0, i, 0)), blk, blk, blk],
        out_specs=[blk] * 4, out_shape=[jax.ShapeDtypeStruct((1, r, c), F32)] * 4,
        compiler_params=_cp("parallel"))(recv, w, m, v)


def _lane_row(pairs):
    row = jnp.zeros((LANES,), F32)
    for lane0, vec in pairs:
        row = lax.dynamic_update_slice(row, vec.astype(F32), (lane0,))
    return row.reshape(1, LANES)


def _stage_in(x, wts, small):
    s = x.shape[0]
    a = -jnp.exp(small["a_log"])
    bias_row = _lane_row([(DT_LANE0, small["dt_bias"]), (F_LANE0, small["b_forget"])])
    a_row = _lane_row([(DT_LANE0, a)])
    conv_b = small["conv_b"].reshape(1, -1)
    norm_w = small["ssm_norm_w"].reshape(1, -1)
    bg = small["b_gates"].reshape(1, -1)
    g1, b1 = small["ln1_g"].reshape(1, -1), small["ln1_b"].reshape(1, -1)
    g2, b2 = small["ln2_g"].reshape(1, -1), small["ln2_b"].reshape(1, -1)
    d_skip = small["d_skip"]
    xb = x.astype(BF16)

    qkv = _mm(xb, wts["qkv"], out_dtype=BF16, name="f_qkv")
    z = _mm(xb, wts["z"], out_dtype=BF16, name="f_z")
    xbc = _mm(xb, wts["xbc"], name="f_xbc")
    gl = _mm(xb, wts["gate"], out_dtype=BF16, name="f_gate")
    fd = _mm(xb, wts["fd"], name="f_fd")
    dt_c, ac_c, cf_c, dt_r, ac_r, cf_r = _stats_fwd(fd, bias_row, a_row)
    bk = _att_blocks(s)[1]
    ck4 = cf_r[F_LANE0:F_LANE0 + ATT_HEADS].reshape(N_HP, HP, s // bk, bk)
    return dict(locals())


def _stage_mid(c, attn, lse, wts, target):
    x, xb, qkv, z, xbc, gl, fd, ck4 = (c[k] for k in ("x", "xb", "qkv", "z", "xbc", "gl", "fd", "ck4"))
    dt_c, ac_c, dt_r, ac_r, a_row, bias_row = (c[k] for k in ("dt_c", "ac_c", "dt_r", "ac_r", "a_row", "bias_row"))
    conv_b, norm_w, bg, g1, b1, g2, b2, d_skip = (c[k] for k in ("conv_b", "norm_w", "bg", "g1", "b1", "g2", "b2",
                                                                "d_skip"))
    conv_w = c["wts"]["conv"]
    attn_d = _mm(attn, wts["pa"], out_dtype=BF16, name="f_pa")
    xact = _conv_fwd(xbc, conv_w, conv_b)
    dsk_pair = jnp.repeat(d_skip, SSM_HEAD_DIM).reshape(N_PAIR, LANES)
    y, hprev = _ssd_pair_fwd(xact, ac_c, dt_r, ac_r, dsk_pair)
    ssm = _gnorm_fwd(y, z, norm_w)
    ssm_d = _mm(ssm, wts["ps"], out_dtype=BF16, name="f_ps")
    mix = _mix_fwd(gl, bg, attn_d, ssm_d)
    mixed = _mm(mix, wts["out"], name="f_out")
    x1, x1_b = _ln1_fwd(x, mixed, g1, b1)
    gu = _mm(x1_b, wts["gu"], out_dtype=BF16, name="f_gu")
    act = _swiglu_fwd(gu)
    h = _mm(act, wts["down"], name="f_down")
    dpre2, dpre2_b, loss_row, dg2, db2 = _ln2_loss(x1, h, target, g2, b2)

    d_act = _mm(dpre2_b, wts["down"], tb=True, out_dtype=BF16, name="b_down_x")
    dw_down = _mm(act, dpre2_b, ta=True, name="b_down_w")
    dgu = _swiglu_bwd(gu, d_act)
    dffn = _mm(dgu, wts["gu"], tb=True, name="b_gu_x")
    dw_gu = _mm(x1_b, dgu, ta=True, name="b_gu_w")
    dpre1, dxr, dg1, db1 = _ln1_bwd(x, mixed, g1, dpre2, dffn)
    dmix = _mm(dpre1, wts["out"], tb=True, out_dtype=BF16, name="b_out_x")
    dw_out = _mm(mix, dpre1, ta=True, name="b_out_w")
    dattn_d, dssm_d, dgl, dbg = _mix_bwd(gl, bg, attn_d, ssm_d, dmix)
    dssm = _mm(dssm_d, wts["ps"], tb=True, out_dtype=BF16, name="b_ps_x")
    dw_ps = _mm(ssm, dssm_d, ta=True, name="b_ps_w")
    dattn = _mm(dattn_d, wts["pa"], tb=True, name="b_pa_x")
    dw_pa = _mm(attn, dattn_d, ta=True, name="b_pa_w")
    dy, dz, dnw = _gnorm_bwd(y, z, norm_w, dssm)
    dxact, ddt, da_row, dds_pair = _ssd_pair_bwd(xact, dt_c, ac_c, dt_r, ac_r, hprev, dy, dsk_pair, a_row)
    dds = dds_pair.reshape(SSM_HEADS, SSM_HEAD_DIM).sum(axis=1)
    dpre_c, dconv_w, dconv_b = _conv_bwd_pre(xbc, conv_w, conv_b, dxact)
    dxbc = _conv_bwd_in(dpre_c, conv_w)
    st, do_b = _att_prep(dattn, attn, lse)
    late = dict(pa=dw_pa, ps=dw_ps, out=dw_out, gu=dw_gu, down=dw_down)
    keep = ("st", "do_b", "ddt", "dxr", "dz", "dxbc", "dgl", "dconv_w", "dconv_b", "da_row", "dds", "dnw", "dbg",
            "dg1", "db1", "dg2", "db2", "loss_row")
    loc = locals()
    return {**c, **{k: loc[k] for k in keep}}, late


def _stage_out_w(c, att_grads):
    dq, dk, dv, dck, dcq = att_grads
    xb, fd, bias_row, ddt, dz, dxbc, dgl = (c[k] for k in ("xb", "fd", "bias_row", "ddt", "dz", "dxbc", "dgl"))
    s, a = xb.shape[0], c["a"]
    dcum = dck.reshape(ATT_HEADS, s) + dcq.reshape(ATT_HEADS, s)
    dfd, dbias = _stats_bwd(fd, bias_row, ddt, jnp.zeros((LANES, s), F32).at[F_LANE0:F_LANE0 + ATT_HEADS].set(dcum))
    dproj = (dq, dk, dv, dz, dxbc, dgl, dfd)
    dw_in = [_mm(xb, g_, ta=True, tb=(i == 0), name=f"b_in_w{i}") for i, g_ in enumerate(dproj)]
    grads = dict(q=dw_in[0], k=dw_in[1], v=dw_in[2], z=dw_in[3], xbc=dw_in[4], gate=dw_in[5], fd=dw_in[6],
                 conv=c["dconv_w"])
    small_g = dict(
        b_forget=dbias[0, F_LANE0:F_LANE0 + ATT_HEADS], conv_b=c["dconv_b"][0], dt_bias=dbias[0, :SSM_HEADS],
        a_log=c["da_row"][0, :SSM_HEADS] * a, d_skip=c["dds"], ssm_norm_w=c["dnw"][0], b_gates=c["dbg"][0],
        ln1_g=c["dg1"][0], ln1_b=c["db1"][0], ln2_g=c["dg2"][0], ln2_b=c["db2"][0])
    return c["loss_row"][0, 0], grads, small_g, dproj


def _stage_out_x(c, dproj, token):
    wts, d = c["wts"], D_MODEL
    wq = wts["qkv"][:, :d] + token.astype(BF16)
    wk, wv = wts["qkv"][:, d:2 * d], wts["qkv"][:, 2 * d:]
    dx = c["dxr"]
    for i, (g_, w_) in enumerate(zip(dproj, (wq, wk, wv, wts["z"], wts["xbc"], wts["gate"], wts["fd"]))):
        dx = _mm(g_, w_, ta=(i == 0), tb=True, add=dx, name=f"b_in_x{i}")
    return dx


BIG = ("w_in", "w_proj_attn", "w_proj_ssm", "w_out", "w_ffn_gate", "w_ffn_up", "w_ffn_down", "conv_w")
EARLY = ("w_in", "conv_w")
LATE = ("w_proj_attn", "w_proj_ssm", "w_out", "w_ffn_gate", "w_ffn_up", "w_ffn_down")
SMALL = ("b_forget", "conv_b", "dt_bias", "a_log", "d_skip", "ssm_norm_w", "b_gates", "ln1_g", "ln1_b", "ln2_g",
         "ln2_b")
SMALL_ROWS = 96
IN_SHARD = IN_WIDTH // N_DEV
IN_SEGMENTS = (("q", 0, 1024), ("k", 1024, 1024), ("v", 2048, 1024), ("f", 3072, ATT_HEADS), ("z", 3088, SSM_INNER),
               ("xbc", 5136, SSM_CONV_DIM), ("dt", 8208, SSM_HEADS), ("gate", 8240, 2 * D_MODEL))


def _cols_from_shards(shards, lo, hi):
    w = shards[0].shape[1]
    pieces = []
    for j in range(len(shards)):
        a, b = max(lo, j * w), min(hi, (j + 1) * w)
        if a < b:
            pieces.append(shards[j][:, a - j * w:b - j * w])
    return pieces[0] if len(pieces) == 1 else jnp.concatenate(pieces, axis=1)


def _shards_from_parts(parts, width):
    shards = []
    for j in range(N_DEV):
        lo, hi = j * width, (j + 1) * width
        pieces = []
        for mat, c0 in parts:
            a, b = max(lo, c0), min(hi, c0 + mat.shape[1])
            if a < b:
                pieces.append(mat[:, a - c0:b - c0])
        shards.append(pieces[0] if len(pieces) == 1 else jnp.concatenate(pieces, axis=1))
    return shards


def _pack_small(vals):
    flat = jnp.concatenate([vals[n].reshape(-1) for n in SMALL])
    return jnp.pad(flat, (0, SMALL_ROWS * LANES - flat.shape[0])).reshape(SMALL_ROWS, LANES)


def _unpack_small(pack, shapes):
    flat = pack.reshape(-1)
    out, off = {}, 0
    for n in SMALL:
        sz = math.prod(shapes[n])
        out[n] = flat[off:off + sz].reshape(shapes[n])
        off += sz
    return out


def kernel(x, w_in, b_forget, conv_w, conv_b, dt_bias, a_log, d_skip, ssm_norm_w, w_proj_attn, w_proj_ssm, b_gates, w_out, ln1_g, ln1_b, w_ffn_gate, w_ffn_up, w_ffn_down, ln2_g, ln2_b, loss_target, m_w_in, m_b_forget, m_conv_w, m_conv_b, m_dt_bias, m_a_log, m_d_skip, m_ssm_norm_w, m_w_proj_attn, m_w_proj_ssm, m_b_gates, m_w_out, m_ln1_g, m_ln1_b, m_w_ffn_gate, m_w_ffn_up, m_w_ffn_down, m_ln2_g, m_ln2_b, v_w_in, v_b_forget, v_conv_w, v_conv_b, v_dt_bias, v_a_log, v_d_skip, v_ssm_norm_w, v_w_proj_attn, v_w_proj_ssm, v_b_gates, v_w_out, v_ln1_g, v_ln1_b, v_w_ffn_gate, v_w_ffn_up, v_w_ffn_down, v_ln2_g, v_ln2_b):
    args = dict(locals())
    d, f = D_MODEL, FFN_HIDDEN
    big_w = {n: args[n][0] for n in BIG}
    small_w = {n: args[n][0] for n in SMALL}
    big_shapes = {n: args[n].shape for n in BIG}
    small_shapes = {n: args[n].shape for n in SMALL}

    early = dict(zip(EARLY, _all_gather([big_w["w_in"].astype(BF16), big_w["conv_w"]])))
    in_shards = [early["w_in"][j] for j in range(N_DEV)]
    seg = {n: _cols_from_shards(in_shards, c0, c0 + w) for n, c0, w in IN_SEGMENTS}
    wfd = jnp.concatenate([seg["dt"], seg["f"], jnp.zeros((d, LANES - SSM_HEADS - ATT_HEADS), BF16)], axis=1)
    wts = dict(qkv=jnp.concatenate([seg["q"], seg["k"], seg["v"]], axis=1), z=seg["z"], xbc=seg["xbc"],
               gate=seg["gate"], fd=wfd, conv=jnp.concatenate([early["conv_w"][j] for j in range(N_DEV)], axis=1))

    ctx = _stage_in(x[0], wts, small_w)
    attn, lse, gathered = _attention_fwd(ctx["qkv"], ctx["ck4"], [big_w[n].astype(BF16) for n in LATE])
    full = dict(zip(LATE, gathered))
    late_w = dict(
        pa=full["w_proj_attn"].reshape(d, d), ps=full["w_proj_ssm"].reshape(SSM_INNER, d),
        out=full["w_out"].reshape(d, d),
        gu=jnp.concatenate([full["w_ffn_gate"][j] for j in range(N_DEV)]
                           + [full["w_ffn_up"][j] for j in range(N_DEV)], axis=1),
        down=full["w_ffn_down"].reshape(f, d))
    ctx, gl = _stage_mid(ctx, attn, lse, late_w, loss_target[0])
    late_dest = dict(
        w_ffn_gate=jnp.stack([s_.astype(BF16) for s_ in _shards_from_parts([(gl["gu"][:, :f], 0)], f // N_DEV)]),
        w_ffn_up=jnp.stack([s_.astype(BF16) for s_ in _shards_from_parts([(gl["gu"][:, f:], 0)], f // N_DEV)]))
    for n, key in (("w_proj_attn", "pa"), ("w_proj_ssm", "ps"), ("w_out", "out"), ("w_ffn_down", "down")):
        late_dest[n] = gl[key].astype(BF16).reshape((N_DEV,) + big_shapes[n][1:])
    att_grads, late_recv = _attention_bwd(ctx["qkv"], ctx["ck4"], ctx["st"], ctx["do_b"], [late_dest[n] for n in LATE])
    loss_part, g, small_g, dproj = _stage_out_w(ctx, att_grads)
    loss = lax.psum(loss_part, ("x", "y", "c"))

    gfd = g["fd"]
    in_parts = dict(q=g["q"], k=g["k"], v=g["v"], f=gfd[:, F_LANE0:F_LANE0 + ATT_HEADS], z=g["z"], xbc=g["xbc"],
                    dt=gfd[:, DT_LANE0:DT_LANE0 + SSM_HEADS], gate=g["gate"])
    win_dest = jnp.stack([s_.astype(BF16) for s_ in
                          _shards_from_parts([(in_parts[n], c0) for n, c0, _ in IN_SEGMENTS], IN_SHARD)])
    conv_dest = jnp.stack(_shards_from_parts([(g["conv"], 0)], SSM_CONV_DIM // N_DEV))
    small_pack = _pack_small(small_g)
    last_parts = [win_dest, conv_dest, jnp.broadcast_to(small_pack, (N_DEV,) + small_pack.shape)]
    send_sems, recv_sems, parts_thru, lands_thru, token = _exchange_start(last_parts, _landing_zones(last_parts))
    grad_x = _stage_out_x(ctx, dproj, token[0, 0])
    early_recv = _exchange_wait(send_sems, recv_sems, parts_thru, lands_thru, grad_x)
    recv = dict(zip(LATE, late_recv))
    recv["w_in"], recv["conv_w"] = early_recv[0], early_recv[1]

    outs = {}
    for n in BIG:
        outs[n] = _adamw(recv[n], args[n], args["m_" + n], args["v_" + n], name="adamw_" + n)
    small4 = _adamw(early_recv[2], _pack_small(small_w)[None], _pack_small({n: args["m_" + n][0] for n in SMALL})[None],
                    _pack_small({n: args["v_" + n][0] for n in SMALL})[None], name="adamw_small")
    small_out = [_unpack_small(p, small_shapes) for p in small4]
    for n in SMALL:
        outs[n] = [so[n] for so in small_out]

    order = ("w_in", "b_forget", "conv_w", "conv_b", "dt_bias", "a_log", "d_skip", "ssm_norm_w", "w_proj_attn",
             "w_proj_ssm", "b_gates", "w_out", "ln1_g", "ln1_b", "w_ffn_gate", "w_ffn_up", "w_ffn_down", "ln2_g",
             "ln2_b")
    res = [loss, grad_x[None]]
    for i in range(4):
        res += [outs[n][i] for n in order]
    return tuple(res)
```

```python
import functools
import math

import jax
import jax.numpy as jnp
from jax import lax
from jax.experimental import pallas as pl
from jax.experimental.pallas import tpu as pltpu

F32 = jnp.float32
BF16 = jnp.bfloat16

N_DEV = 8
D_MODEL = 1024
ATT_HEADS = 16
ATT_HEAD_DIM = 64
SSM_INNER = 2048
SSM_HEADS = 32
SSM_HEAD_DIM = 64
SSM_GROUPS = 4
SSM_HEADS_PER_GROUP = 8
SSM_STATE = 128
SSM_CONV = 4
SSM_CHUNK = 128
SSM_CONV_DIM = 3072
FFN_HIDDEN = 2816
IN_WIDTH = 10288
DEEPNORM_ALPHA = 2.0 ** 0.25
LN_EPS = 1e-5
RMS_EPS = 1e-5
ADAM_LR, ADAM_B1, ADAM_B2, ADAM_EPS, ADAM_WD, ADAM_STEP = 0.001, 0.9, 0.999, 1e-08, 0.01, 10
ATT_SCALE = 1.0 / math.sqrt(ATT_HEAD_DIM)

LANES = 128
VMEM_LIMIT = 56 * 1024 * 1024
NEG = -1e30

DT_LANE0 = 0
F_LANE0 = 32
HI = lax.Precision.HIGHEST


def _cp(*sem):
    return pltpu.CompilerParams(dimension_semantics=sem, vmem_limit_bytes=VMEM_LIMIT)


def _tile(n, cap=1408):
    for t in (1408, 1024, 512, 384, 256, 128):
        if t <= cap and n % t == 0:
            return t
    return n


def _sigmoid(x):
    return 1.0 / (1.0 + jnp.exp(-x))


def _mm(a, b, *, ta=False, tb=False, out_dtype=F32, add=None, name):
    m, k = (a.shape[1], a.shape[0]) if ta else a.shape
    n = b.shape[0] if tb else b.shape[1]
    assert (b.shape[1] if tb else b.shape[0]) == k
    tm, tn, tk = _tile(m), _tile(n), _tile(k)
    nk = k // tk
    dims = (((0,) if ta else (1,), (1,) if tb else (0,)), ((), ()))

    def body_single(*refs):
        a_ref, b_ref = refs[:2]
        r = lax.dot_general(a_ref[...].astype(BF16), b_ref[...].astype(BF16), dims, preferred_element_type=F32)
        if add is not None:
            r = r + refs[2][...]
        refs[-1][...] = r.astype(refs[-1].dtype)

    def body(*refs):
        if add is None:
            a_ref, b_ref, o_ref, acc_ref = refs
        else:
            a_ref, b_ref, c_ref, o_ref, acc_ref = refs
        kk = pl.program_id(2)

        @pl.when(kk == 0)
        def _():
            acc_ref[...] = jnp.zeros_like(acc_ref)

        acc_ref[...] += lax.dot_general(a_ref[...].astype(BF16), b_ref[...].astype(BF16), dims,
                                        preferred_element_type=F32)

        @pl.when(kk == nk - 1)
        def _():
            r = acc_ref[...]
            if add is not None:
                r = r + c_ref[...]
            o_ref[...] = r.astype(o_ref.dtype)

    a_spec = pl.BlockSpec((tk, tm), lambda i, j, kk: (kk, i)) if ta else pl.BlockSpec((tm, tk), lambda i, j, kk: (i, kk))
    b_spec = pl.BlockSpec((tn, tk), lambda i, j, kk: (j, kk)) if tb else pl.BlockSpec((tk, tn), lambda i, j, kk: (kk, j))
    o_spec = pl.BlockSpec((tm, tn), lambda i, j, kk: (i, j))
    in_specs, args = [a_spec, b_spec], [a, b]
    if add is not None:
        in_specs.append(o_spec)
        args.append(add)
    return pl.pallas_call(
        body_single if nk == 1 else body, name=name, grid=(m // tm, n // tn, nk), in_specs=in_specs, out_specs=o_spec,
        out_shape=jax.ShapeDtypeStruct((m, n), out_dtype),
        scratch_shapes=[] if nk == 1 else [pltpu.VMEM((tm, tn), F32)],
        compiler_params=_cp("parallel", "parallel", "arbitrary"),
    )(*args)


def _tri(n, lower=True):
    r = lax.broadcasted_iota(jnp.int32, (n, n), 0)
    c = lax.broadcasted_iota(jnp.int32, (n, n), 1)
    return jnp.where((r >= c) if lower else (c >= r), 1.0, 0.0).astype(F32)


def _stats_fwd(fd, bias_row, a_row):
    s = fd.shape[0]
    blk = SSM_CHUNK

    def body(fd_ref, bias_ref, a_ref, dt_ref, ac_ref, cf_ref, dtr_ref, acr_ref, cfr_ref, carry_ref):
        @pl.when(pl.program_id(0) == 0)
        def _():
            carry_ref[...] = jnp.zeros_like(carry_ref)

        v = fd_ref[...] + bias_ref[...]
        dt = jnp.maximum(v, 0.0) + jnp.log(1.0 + jnp.exp(-jnp.abs(v)))
        lf = jnp.minimum(v, 0.0) - jnp.log(1.0 + jnp.exp(-jnp.abs(v)))
        tri = _tri(blk)
        ac = jnp.dot(tri, dt * a_ref[...], precision=HI, preferred_element_type=F32)
        cf = jnp.dot(tri, lf, precision=HI, preferred_element_type=F32) + carry_ref[0:1, :]
        carry_ref[...] = carry_ref[...] + jnp.sum(lf, axis=0, keepdims=True)
        dt_ref[...] = dt
        ac_ref[...] = ac
        cf_ref[...] = cf
        dtr_ref[...] = dt.T
        acr_ref[...] = ac.T
        cfr_ref[...] = cf.T

    col = pl.BlockSpec((blk, LANES), lambda i: (i, 0))
    row = pl.BlockSpec((LANES, blk), lambda i: (0, i))
    vec = pl.BlockSpec((1, LANES), lambda i: (0, 0))
    return pl.pallas_call(
        body, name="stats_fwd", grid=(s // blk,), in_specs=[col, vec, vec],
        out_specs=[col, col, col, row, row, row],
        out_shape=[jax.ShapeDtypeStruct((s, LANES), F32)] * 3 + [jax.ShapeDtypeStruct((LANES, s), F32)] * 3,
        scratch_shapes=[pltpu.VMEM((8, LANES), F32)],
        compiler_params=_cp("arbitrary"),
    )(fd, bias_row, a_row)


def _stats_bwd(fd, bias_row, ddt, dcum_rows):
    s = fd.shape[0]
    blk = SSM_CHUNK
    nb = s // blk

    def body(fd_ref, bias_ref, ddt_ref, dck_ref, o_ref, db_ref, carry_ref):
        @pl.when(pl.program_id(0) == 0)
        def _():
            carry_ref[...] = jnp.zeros_like(carry_ref)
            db_ref[...] = jnp.zeros_like(db_ref)

        v = fd_ref[...] + bias_ref[...]
        dcum = dck_ref[...].T
        dlf = jnp.dot(_tri(blk, lower=False), dcum, precision=HI, preferred_element_type=F32) + carry_ref[0:1, :]
        carry_ref[...] = carry_ref[...] + jnp.sum(dcum, axis=0, keepdims=True)
        lane = lax.broadcasted_iota(jnp.int32, v.shape, 1)
        g = jnp.where(lane < F_LANE0, ddt_ref[...] * _sigmoid(v), dlf * _sigmoid(-v))
        g = jnp.where(lane < F_LANE0 + ATT_HEADS, g, 0.0)
        o_ref[...] = g.astype(o_ref.dtype)
        db_ref[...] += jnp.sum(g, axis=0, keepdims=True)

    col = pl.BlockSpec((blk, LANES), lambda i: (nb - 1 - i, 0))
    row = pl.BlockSpec((LANES, blk), lambda i: (0, nb - 1 - i))
    vec = pl.BlockSpec((1, LANES), lambda i: (0, 0))
    return pl.pallas_call(
        body, name="stats_bwd", grid=(nb,), in_specs=[col, vec, col, row], out_specs=[col, vec],
        out_shape=[jax.ShapeDtypeStruct((s, LANES), BF16), jax.ShapeDtypeStruct((1, LANES), F32)],
        scratch_shapes=[pltpu.VMEM((8, LANES), F32)],
        compiler_params=_cp("arbitrary"),
    )(fd, bias_row, ddt, dcum_rows)


HP = LANES // ATT_HEAD_DIM
N_HP = ATT_HEADS // HP


def _att_blocks(s):
    return (512, 1024) if s % 1024 == 0 and s >= 4096 else (64, 128)


_QK = (((1,), (1,)), ((), ()))
_HALF = ATT_HEAD_DIM // 2
_PAD = 16


def _head_cols(a):
    return slice(a * ATT_HEAD_DIM, (a + 1) * ATT_HEAD_DIM)


def _causal(shape, off):
    r = lax.broadcasted_iota(jnp.int32, shape, 0)
    c = lax.broadcasted_iota(jnp.int32, shape, 1)
    return c <= r + off


def _attention_fwd(qkv, ck4, gather_parts):
    s = qkv.shape[0]
    bk = _att_blocks(s)[1]
    bq = bk
    nq, nk = s // bq, s // bk
    n = len(gather_parts)

    def body(q_ref, k_ref, v_ref, ck_ref, *rest):
        comm_in, (o_ref, lse_ref), comm_out, sems = rest[:n], rest[n:n + 2], rest[n + 2:2 * n + 2], rest[2 * n + 2:]
        i = pl.program_id(1)
        if n:
            @pl.when((pl.program_id(0) == 0) & (i == 0))
            def _():
                for cp in _comm_copies(comm_in, comm_out, sems, False):
                    cp.start()

        n_full = (i * bq) // bk
        qs = [(q_ref[:, _head_cols(a)].astype(F32) * ATT_SCALE).astype(BF16) for a in range(HP)]

        upper_q = lax.broadcasted_iota(jnp.int32, (bq, LANES), 1) >= ATT_HEAD_DIM

        def absorb(j, carry, keys=slice(0, bk), rows=slice(0, bq), masked=False):
            nk_ = keys.stop - keys.start
            ks = pl.ds(pl.multiple_of(j * bk, bk) + keys.start, nk_)
            v_both = v_ref[ks, :]
            out = []
            for a in range(HP):
                m, acc = carry[a]
                sc = lax.dot_general(qs[a][rows], k_ref[ks, _head_cols(a)], _QK, preferred_element_type=F32)
                sc = sc - ck_ref[0, a, pl.ds(j, 1), keys]
                if masked:
                    sc = jnp.where(_causal(sc.shape, 0), sc, NEG)
                m_new = jnp.maximum(m, jnp.max(sc, axis=1, keepdims=True))
                p = jnp.exp((sc - m_new).astype(BF16))
                upper_v = lax.broadcasted_iota(jnp.int32, (nk_, LANES), 1) >= ATT_HEAD_DIM
                v_aug = jnp.where(upper_v == (a == 1), v_both, jnp.ones_like(v_both))
                acc = jnp.exp(m - m_new) * acc + jnp.dot(p, v_aug, preferred_element_type=F32)
                out.append((m_new, acc))
            return tuple(out)

        init = tuple((jnp.full((bq, 1), NEG, F32), jnp.zeros((bq, LANES), F32)) for _ in range(HP))
        carry = lax.fori_loop(0, n_full, absorb, init)
        hq = bq // 2
        carry = absorb(n_full, carry, keys=slice(0, hq), masked=True)
        low = absorb(n_full, tuple((m[hq:], acc[hq:]) for m, acc in carry), keys=slice(hq, bk), rows=slice(hq, bq),
                     masked=True)
        carry = tuple((jnp.concatenate([m[:hq], ml], axis=0), jnp.concatenate([acc[:hq], al], axis=0))
                      for (m, acc), (ml, al) in zip(carry, low))
        outs, lses = [], []
        for a in range(HP):
            m, acc = carry[a]
            l = pltpu.roll(acc, ATT_HEAD_DIM, 1)
            outs.append(acc / l)
            lses.append(m + jnp.log(l))
        o_ref[...] = jnp.where(upper_q, outs[1], outs[0])
        lse_ref[...] = jnp.where(upper_q, lses[1], lses[0])
        if n:
            @pl.when((pl.program_id(0) == N_HP - 1) & (i == nq - 1))
            def _():
                for cp in _comm_copies(comm_in, comm_out, sems, False):
                    cp.wait()

    q_spec = pl.BlockSpec((bq, LANES), lambda h, i: (i, h))
    anyspec = pl.BlockSpec(memory_space=pl.ANY)
    res = pl.pallas_call(
        body, name="att_fwd", grid=(N_HP, nq),
        in_specs=[q_spec, pl.BlockSpec((s, LANES), lambda h, i: (0, N_HP + h)),
                  pl.BlockSpec((s, LANES), lambda h, i: (0, 2 * N_HP + h)),
                  pl.BlockSpec((1, HP, nk, bk), lambda h, i: (h, 0, 0, 0))] + [anyspec] * n,
        out_specs=[q_spec, q_spec] + [anyspec] * n,
        out_shape=[jax.ShapeDtypeStruct((s, D_MODEL), F32)] * 2 + _comm_out_shapes(gather_parts, False),
        scratch_shapes=_comm_sems(n) if n else [],
        compiler_params=_cp("arbitrary", "arbitrary"),
    )(qkv, qkv, qkv, ck4, *gather_parts)
    return res[0], res[1], list(res[2:])


def _att_prep(do, o, lse_rep):
    s = do.shape[0]
    bs = _tile(s, 512)

    def body(do_ref, o_ref, lse_ref, st_ref, dob_ref):
        r = lax.broadcasted_iota(jnp.int32, (LANES, LANES), 0) // ATT_HEAD_DIM
        c = lax.broadcasted_iota(jnp.int32, (LANES, LANES), 1) // ATT_HEAD_DIM
        e = jnp.where(r == c, 1.0, 0.0).astype(F32)
        lane = lax.broadcasted_iota(jnp.int32, (bs, LANES), 1)
        for p in range(D_MODEL // LANES):
            cs = slice(p * LANES, (p + 1) * LANES)
            dd = do_ref[:, cs]
            delta = jnp.dot(dd * o_ref[:, cs], e, precision=HI, preferred_element_type=F32)
            st_ref[:, cs] = jnp.where(lane % ATT_HEAD_DIM < _HALF, lse_ref[:, cs], delta)
            dob_ref[:, cs] = dd.astype(BF16)

    spec = pl.BlockSpec((bs, D_MODEL), lambda i: (i, 0))
    return pl.pallas_call(body, name="att_prep", grid=(s // bs,), in_specs=[spec, spec, spec], out_specs=[spec, spec],
                          out_shape=[jax.ShapeDtypeStruct((s, D_MODEL), F32), jax.ShapeDtypeStruct((s, D_MODEL), BF16)],
                          compiler_params=_cp("parallel"))(do, o, lse_rep)


def _attention_bwd(qkv, ck4, st, do_b, exchange_parts):
    s = qkv.shape[0]
    bq, bk = _att_blocks(s)
    nq, nk, per = s // bq, s // bk, bk // bq
    _T = (((0,), (0,)), ((), ()))
    n = len(exchange_parts)

    def body(q_ref, k_ref, v_ref, ck_ref, st_ref, do_ref, *rest):
        comm_in, (dq_ref, dk_ref, dv_ref, dck_ref, dcq_ref) = rest[:n], rest[n:n + 5]
        comm_out, sems, (dk_acc, dv_acc) = rest[n + 5:2 * n + 5], rest[2 * n + 5:-2], rest[-2:]
        j = pl.program_id(1)
        if n:
            @pl.when((pl.program_id(0) == 0) & (j == 0))
            def _():
                for cp in _comm_copies(comm_in, comm_out, sems, True):
                    cp.start()

        @pl.when(j == 0)
        def _():
            dq_ref[...] = jnp.zeros_like(dq_ref)
            dcq_ref[...] = jnp.zeros_like(dcq_ref)

        dk_acc[...] = jnp.zeros_like(dk_acc)
        dv_acc[...] = jnp.zeros_like(dv_acc)

        ones_q, ones_k = jnp.ones((_PAD, bq), BF16), jnp.ones((_PAD, bk), BF16)
        k_t = [jnp.concatenate([k_ref[:, _head_cols(a)].T, ones_k], axis=0) for a in range(HP)]

        def step(i, off=None, kl=slice(0, bk)):
            rows = pl.ds(pl.multiple_of(i * bq, bq), bq)
            for a in range(HP):
                cs = _head_cols(a)
                q = (q_ref[rows, cs].astype(F32) * ATT_SCALE).astype(BF16)
                do_a = do_ref[rows, cs]
                sc = lax.dot_general(q, k_ref[kl, cs], _QK, preferred_element_type=F32) - ck_ref[0, a, pl.ds(j, 1), kl]
                if off is not None:
                    sc = jnp.where(_causal(sc.shape, off), sc, NEG)
                p = jnp.exp(sc - st_ref[rows, a * ATT_HEAD_DIM:a * ATT_HEAD_DIM + 1])
                dp = lax.dot_general(do_a, v_ref[kl, cs], _QK, preferred_element_type=F32)
                ds = p * (dp - st_ref[rows, a * ATT_HEAD_DIM + _HALF:a * ATT_HEAD_DIM + _HALF + 1])
                ds_b = ds.astype(BF16)
                dv_acc[a, :, kl] += jnp.dot(do_a.T, p.astype(BF16), preferred_element_type=F32)
                dk_acc[a, :, kl] += jnp.dot(jnp.concatenate([q.T, ones_q], axis=0), ds_b, preferred_element_type=F32)
                dqs = lax.dot_general(k_t[a][:, kl], ds_b, _QK, preferred_element_type=F32)
                dq_ref[cs, rows] += dqs[:ATT_HEAD_DIM] * ATT_SCALE
                dcq_ref[0, a, pl.ds(i, 1), :] += jnp.sum(dqs[ATT_HEAD_DIM:ATT_HEAD_DIM + 8], axis=0,
                                                         keepdims=True) * 0.125

        for t in range(per):
            step(j * per + t, off=t * bq, kl=slice(0, (t + 1) * bq))

        def full(i, c):
            step(i)
            return c

        lax.fori_loop((j + 1) * per, nq, full, 0)
        for a in range(HP):
            dk_ref[:, _head_cols(a)] = dk_acc[a, :ATT_HEAD_DIM].T.astype(dk_ref.dtype)
            dv_ref[:, _head_cols(a)] = dv_acc[a].T.astype(dv_ref.dtype)
            dck_ref[0, a, pl.ds(j, 1), :] = -dk_acc[a, ATT_HEAD_DIM:ATT_HEAD_DIM + 1]
        if n:
            @pl.when((pl.program_id(0) == N_HP - 1) & (j == nk - 1))
            def _():
                for cp in _comm_copies(comm_in, comm_out, sems, True):
                    cp.wait()

    res = pl.BlockSpec((s, LANES), lambda h, j: (0, h))
    ck_spec = pl.BlockSpec((1, HP, nk, bk), lambda h, j: (h, 0, 0, 0))
    kout = pl.BlockSpec((bk, LANES), lambda h, j: (j, h))
    anyspec = pl.BlockSpec(memory_space=pl.ANY)
    outs = pl.pallas_call(
        body, name="att_bwd", grid=(N_HP, nk),
        in_specs=[res, pl.BlockSpec((bk, LANES), lambda h, j: (j, N_HP + h)),
                  pl.BlockSpec((bk, LANES), lambda h, j: (j, 2 * N_HP + h)), ck_spec, res, res] + [anyspec] * n,
        out_specs=[pl.BlockSpec((LANES, s), lambda h, j: (h, 0)), kout, kout, ck_spec,
                   pl.BlockSpec((1, HP, nq, bq), lambda h, j: (h, 0, 0, 0))] + [anyspec] * n,
        out_shape=[jax.ShapeDtypeStruct((D_MODEL, s), F32), jax.ShapeDtypeStruct((s, D_MODEL), BF16),
                   jax.ShapeDtypeStruct((s, D_MODEL), BF16), jax.ShapeDtypeStruct((N_HP, HP, nk, bk), F32),
                   jax.ShapeDtypeStruct((N_HP, HP, nq, bq), F32)] + _comm_out_shapes(exchange_parts, True),
        scratch_shapes=(_comm_sems(n) if n else [])
        + [pltpu.VMEM((HP, ATT_HEAD_DIM + _PAD, bk), F32), pltpu.VMEM((HP, ATT_HEAD_DIM, bk), F32)],
        compiler_params=_cp("arbitrary", "arbitrary"),
    )(qkv, qkv, qkv, ck4, st, do_b, *exchange_parts)
    return outs[:5], list(outs[5:])


def _silu_and_grad(x):
    sg = _sigmoid(x)
    return x * sg, sg * (1.0 + x * (1.0 - sg))


def _conv_pre(cur, halo, w_ref, b_ref, first):
    halo = jnp.where(first, 0.0, halo)
    row = lax.broadcasted_iota(jnp.int32, cur.shape, 0)
    shifted = []
    for k in range(SSM_CONV):
        sh = SSM_CONV - 1 - k
        if sh == 0:
            shifted.append(cur)
            continue
        r = pltpu.roll(cur, sh, 0)
        hr = pltpu.roll(halo, sh, 0)
        top = jnp.where(row[0:8] < sh, hr, r[0:8])
        shifted.append(jnp.concatenate([top, r[8:]], axis=0))
    pre = b_ref[...] + sum(w_ref[k:k + 1, :] * shifted[k] for k in range(SSM_CONV))
    return pre, shifted


def _conv_specs(s, bs, bc):
    cur = pl.BlockSpec((bs, bc), lambda j, i: (i, j))
    halo = pl.BlockSpec((8, bc), lambda j, i: (jnp.maximum(i * (bs // 8) - 1, 0), j))
    w = pl.BlockSpec((SSM_CONV, bc), lambda j, i: (0, j))
    b = pl.BlockSpec((1, bc), lambda j, i: (0, j))
    return cur, halo, w, b


def _conv_fwd(xbc, w, b):
    s, c = xbc.shape
    bs, bc = _tile(s, 512), 1024

    def body(x_ref, h_ref, w_ref, b_ref, o_ref):
        pre, _ = _conv_pre(x_ref[...], h_ref[...], w_ref, b_ref, pl.program_id(1) == 0)
        o_ref[...] = pre * _sigmoid(pre)

    cur, halo, ws, bsp = _conv_specs(s, bs, bc)
    return pl.pallas_call(body, name="conv_fwd", grid=(c // bc, s // bs), in_specs=[cur, halo, ws, bsp],
                          out_specs=cur, out_shape=jax.ShapeDtypeStruct((s, c), F32),
                          compiler_params=_cp("parallel", "parallel"))(xbc, xbc, w, b)


def _conv_bwd_pre(xbc, w, b, dact):
    s, c = xbc.shape
    bs, bc = _tile(s, 512), 1024

    def body(x_ref, h_ref, w_ref, b_ref, g_ref, dp_ref, dw_ref, db_ref):
        @pl.when(pl.program_id(1) == 0)
        def _():
            dw_ref[...] = jnp.zeros_like(dw_ref)
            db_ref[...] = jnp.zeros_like(db_ref)

        pre, shifted = _conv_pre(x_ref[...], h_ref[...], w_ref, b_ref, pl.program_id(1) == 0)
        dpre = g_ref[...] * _silu_and_grad(pre)[1]
        dp_ref[...] = dpre
        db_ref[...] += jnp.sum(dpre, axis=0, keepdims=True)
        for k in range(SSM_CONV):
            dw_ref[k:k + 1, :] += jnp.sum(dpre * shifted[k], axis=0, keepdims=True)

    cur, halo, ws, bsp = _conv_specs(s, bs, bc)
    return pl.pallas_call(
        body, name="conv_bwd_pre", grid=(c // bc, s // bs), in_specs=[cur, halo, ws, bsp, cur],
        out_specs=[cur, ws, bsp],
        out_shape=[jax.ShapeDtypeStruct((s, c), F32), jax.ShapeDtypeStruct((SSM_CONV, c), F32),
                   jax.ShapeDtypeStruct((1, c), F32)],
        compiler_params=_cp("parallel", "arbitrary"))(xbc, xbc, w, b, dact)


def _conv_bwd_in(dpre, w):
    s, c = dpre.shape
    bs, bc = _tile(s, 512), 1024
    nb = s // bs

    def body(g_ref, n_ref, w_ref, o_ref):
        cur = g_ref[...]
        nxt = jnp.where(pl.program_id(1) == nb - 1, 0.0, n_ref[...])
        row = lax.broadcasted_iota(jnp.int32, cur.shape, 0)
        acc = w_ref[SSM_CONV - 1:SSM_CONV, :] * cur
        for sh in range(1, SSM_CONV):
            r = pltpu.roll(cur, bs - sh, 0)
            nr = pltpu.roll(nxt, 8 - sh, 0)
            bot = jnp.where(row[0:8] >= 8 - sh, nr, r[bs - 8:])
            acc = acc + w_ref[SSM_CONV - 1 - sh:SSM_CONV - sh, :] * jnp.concatenate([r[:bs - 8], bot], axis=0)
        o_ref[...] = acc.astype(o_ref.dtype)

    cur = pl.BlockSpec((bs, bc), lambda j, i: (i, j))
    nxt = pl.BlockSpec((8, bc), lambda j, i: (jnp.minimum((i + 1) * (bs // 8), s // 8 - 1), j))
    ws = pl.BlockSpec((SSM_CONV, bc), lambda j, i: (0, j))
    return pl.pallas_call(body, name="conv_bwd_in", grid=(c // bc, nb), in_specs=[cur, nxt, ws], out_specs=cur,
                          out_shape=jax.ShapeDtypeStruct((s, c), BF16),
                          compiler_params=_cp("parallel", "parallel"))(dpre, dpre, w)


def _dotT(a, b):
    return lax.dot_general(a.astype(BF16), b.astype(BF16), (((1,), (1,)), ((), ())), preferred_element_type=F32)


def _dot(a, b):
    return jnp.dot(a.astype(BF16), b.astype(BF16), preferred_element_type=F32)


N_PAIR = SSM_HEADS // HP
PAIRS_PER_GROUP = SSM_HEADS_PER_GROUP // HP


def _pair_consts():
    L = SSM_CHUNK
    lane = lax.broadcasted_iota(jnp.int32, (L, LANES), 1)
    lane1 = lax.broadcasted_iota(jnp.int32, (1, LANES), 1)
    li = lax.broadcasted_iota(jnp.int32, (L, L), 0)
    si = lax.broadcasted_iota(jnp.int32, (L, L), 1)
    return lane >= ATT_HEAD_DIM, lane1 >= ATT_HEAD_DIM, li, si


def _ssd_pair_fwd(xbc_act, ac_c, dt_r, ac_r, dsk_pair):
    s = xbc_act.shape[0]
    L, N, G = SSM_CHUNK, SSM_STATE, SSM_GROUPS
    nc = s // L

    def body(xbc_ref, ac_ref, dtr_ref, acr_ref, dsk_ref, y_ref, hp_ref, st_ref):
        @pl.when(pl.program_id(0) == 0)
        def _():
            st_ref[...] = jnp.zeros_like(st_ref)

        upper, up1, li, si = _pair_consts()
        for g in range(G):
            b_g = xbc_ref[:, SSM_INNER + g * N:SSM_INNER + (g + 1) * N]
            c_g = xbc_ref[:, SSM_INNER + G * N + g * N:SSM_INNER + G * N + (g + 1) * N]
            cb = _dotT(c_g, b_g)
            b_t = b_g.T
            for q in range(PAIRS_PER_GROUP):
                pp = g * PAIRS_PER_GROUP + q
                cols = slice(pp * LANES, (pp + 1) * LANES)
                xs = xbc_ref[:, cols]
                ht = st_ref[pp]
                hp_ref[0, pp] = ht
                y = dsk_ref[pp:pp + 1, :] * xs
                s_new = jnp.zeros((N, LANES), F32)
                ea, el = [], []
                for a in range(HP):
                    h = HP * pp + a
                    acol = jnp.broadcast_to(ac_ref[:, h:h + 1], (L, LANES))
                    arow, dtrow = acr_ref[h:h + 1, :], dtr_ref[h:h + 1, :]
                    alast = ac_ref[L - 1:L, h:h + 1]
                    decay = jnp.exp(jnp.where(li >= si, acol - arow, NEG))
                    xs_a = jnp.where(upper == (a == 1), xs, 0.0)
                    y = y + _dot(cb * decay * dtrow, xs_a)
                    s_new = s_new + _dot(b_t * (dtrow * jnp.exp(alast - arow)), xs_a)
                    ea.append(jnp.exp(acol))
                    el.append(jnp.exp(alast))
                y_ref[:, cols] = y + jnp.where(upper, ea[1], ea[0]) * _dot(c_g, ht)
                st_ref[pp] = ht * jnp.where(up1, el[1], el[0]) + s_new

    col = pl.BlockSpec((L, LANES), lambda c: (c, 0))
    row = pl.BlockSpec((LANES, L), lambda c: (0, c))
    return pl.pallas_call(
        body, name="ssd_fwd", grid=(nc,),
        in_specs=[pl.BlockSpec((L, SSM_CONV_DIM), lambda c: (c, 0)), col, row, row,
                  pl.BlockSpec((N_PAIR, LANES), lambda c: (0, 0))],
        out_specs=[pl.BlockSpec((L, SSM_INNER), lambda c: (c, 0)),
                   pl.BlockSpec((1, N_PAIR, N, LANES), lambda c: (c, 0, 0, 0))],
        out_shape=[jax.ShapeDtypeStruct((s, SSM_INNER), F32), jax.ShapeDtypeStruct((nc, N_PAIR, N, LANES), F32)],
        scratch_shapes=[pltpu.VMEM((N_PAIR, N, LANES), F32)],
        compiler_params=_cp("arbitrary"),
    )(xbc_act, ac_c, dt_r, ac_r, dsk_pair)


def _ssd_pair_bwd(xbc_act, dt_c, ac_c, dt_r, ac_r, hprev_all, dy, dsk_pair, a_row):
    s = xbc_act.shape[0]
    L, N, G = SSM_CHUNK, SSM_STATE, SSM_GROUPS
    nc = s // L
    rev = lambda c: nc - 1 - c

    def body(xbc_ref, dt_ref, ac_ref, dtr_ref, acr_ref, hp_ref, dy_ref, dsk_ref, arow_ref,
             dx_ref, ddt_ref, da_ref, dds_ref, dh_ref):
        @pl.when(pl.program_id(0) == 0)
        def _():
            dh_ref[...] = jnp.zeros_like(dh_ref)
            da_ref[...] = jnp.zeros_like(da_ref)
            dds_ref[...] = jnp.zeros_like(dds_ref)

        upper, up1, li, si = _pair_consts()
        lane = lax.broadcasted_iota(jnp.int32, (L, LANES), 1)
        sub = lax.broadcasted_iota(jnp.int32, (LANES, L), 0)
        lastrow = lax.broadcasted_iota(jnp.int32, (L, LANES), 0) == L - 1
        da_c = jnp.zeros((L, LANES), F32)
        da_r = jnp.zeros((LANES, L), F32)
        ddt_r = jnp.zeros((LANES, L), F32)
        for g in range(G):
            b_g = xbc_ref[:, SSM_INNER + g * N:SSM_INNER + (g + 1) * N]
            c_g = xbc_ref[:, SSM_INNER + G * N + g * N:SSM_INNER + G * N + (g + 1) * N]
            cb, cb_t = _dotT(c_g, b_g), _dotT(b_g, c_g)
            b_t, c_t = b_g.T, c_g.T
            dcb = jnp.zeros((L, L), F32)
            db_t = jnp.zeros((N, L), F32)
            dc = jnp.zeros((L, N), F32)
            for q in range(PAIRS_PER_GROUP):
                pp = g * PAIRS_PER_GROUP + q
                cols = slice(pp * LANES, (pp + 1) * LANES)
                xs, gy = xbc_ref[:, cols], dy_ref[:, cols]
                ht, dhn = hp_ref[0, pp], dh_ref[pp]
                acol = [jnp.broadcast_to(ac_ref[:, HP * pp + a:HP * pp + a + 1], (L, LANES)) for a in range(HP)]
                alast = [ac_ref[L - 1:L, HP * pp + a:HP * pp + a + 1] for a in range(HP)]
                ea = jnp.where(upper, jnp.exp(acol[1]), jnp.exp(acol[0]))
                el = jnp.where(up1, jnp.exp(alast[1]), jnp.exp(alast[0]))
                ge = gy * ea
                dc = dc + _dotT(ge, ht)
                dh_ref[pp] = _dot(c_t, ge) + dhn * el
                t_off = (ge * _dot(c_g, ht)).astype(BF16)
                hsum = jnp.sum(dhn * ht, axis=0, keepdims=True)
                dxs = dsk_ref[pp:pp + 1, :] * gy
                dds_ref[pp:pp + 1, :] += jnp.sum(gy * xs, axis=0, keepdims=True)
                for a in range(HP):
                    h = HP * pp + a
                    mine, mine1 = upper == (a == 1), up1 == (a == 1)
                    arow, dtrow = acr_ref[h:h + 1, :], dtr_ref[h:h + 1, :]
                    dtcol = jnp.broadcast_to(dt_ref[:, h:h + 1], (L, LANES))
                    xs_a, gy_a = jnp.where(mine, xs, 0.0), jnp.where(mine, gy, 0.0)
                    dhn_a = jnp.where(mine1, dhn, 0.0)
                    e_row = jnp.exp(alast[a] - arow)
                    w_row = dtrow * e_row
                    xd_t = _dotT(dhn_a, xs_a)
                    db_t = db_t + xd_t * w_row
                    dw = jnp.sum(b_t * xd_t, axis=0, keepdims=True)
                    de_e = dw * w_row
                    dal = (jnp.sum(jnp.where(mine1, hsum, 0.0), axis=1, keepdims=True) * jnp.exp(alast[a])
                           + jnp.sum(de_e, axis=1, keepdims=True))
                    dxs = dxs + _dot(b_g, dhn_a) * (dtcol * jnp.exp(alast[a] - acol[a]))
                    decay = jnp.exp(jnp.where(li >= si, acol[a] - arow, NEG))
                    decay_t = jnp.exp(jnp.where(si >= li, arow - acol[a], NEG))
                    m = cb * decay
                    dmdt = _dotT(gy_a, xs_a)
                    dxs = dxs + _dot(cb_t * decay_t * dtcol, gy_a)
                    dm = dmdt * dtrow
                    dcb = dcb + dm * decay
                    wb = (dm * m).astype(BF16)
                    onehot = jnp.where(lane == h, 1.0, 0.0).astype(BF16)
                    da_c = (da_c + jnp.dot(wb, onehot, preferred_element_type=F32)
                            + jnp.dot(jnp.where(mine, t_off, 0.0).astype(BF16), onehot, preferred_element_type=F32)
                            + jnp.where(lastrow & (lane == h), dal, 0.0))
                    da_r = jnp.where(sub == h, -(jnp.sum(wb.astype(F32), axis=0, keepdims=True) + de_e), da_r)
                    ddt_r = jnp.where(sub == h, dw * e_row + jnp.sum(dmdt * m, axis=0, keepdims=True), ddt_r)
                dx_ref[:, cols] = dxs
            dx_ref[:, SSM_INNER + g * N:SSM_INNER + (g + 1) * N] = (db_t + _dot(c_t, dcb)).T
            dx_ref[:, SSM_INNER + G * N + g * N:SSM_INNER + G * N + (g + 1) * N] = dc + _dot(dcb, b_g)
        dda = jnp.dot(_tri(L, lower=False), da_c + da_r.T, precision=HI, preferred_element_type=F32)
        ddt_ref[...] = dda * arow_ref[...] + ddt_r.T
        da_ref[...] += jnp.sum(dda * dt_ref[...], axis=0, keepdims=True)

    col = pl.BlockSpec((L, LANES), lambda c: (rev(c), 0))
    row = pl.BlockSpec((LANES, L), lambda c: (0, rev(c)))
    vec = pl.BlockSpec((1, LANES), lambda c: (0, 0))
    pairs = pl.BlockSpec((N_PAIR, LANES), lambda c: (0, 0))
    return pl.pallas_call(
        body, name="ssd_bwd", grid=(nc,),
        in_specs=[pl.BlockSpec((L, SSM_CONV_DIM), lambda c: (rev(c), 0)), col, col, row, row,
                  pl.BlockSpec((1, N_PAIR, N, LANES), lambda c: (rev(c), 0, 0, 0)),
                  pl.BlockSpec((L, SSM_INNER), lambda c: (rev(c), 0)), pairs, vec],
        out_specs=[pl.BlockSpec((L, SSM_CONV_DIM), lambda c: (rev(c), 0)), col, vec, pairs],
        out_shape=[jax.ShapeDtypeStruct((s, SSM_CONV_DIM), F32), jax.ShapeDtypeStruct((s, LANES), F32),
                   jax.ShapeDtypeStruct((1, LANES), F32), jax.ShapeDtypeStruct((N_PAIR, LANES), F32)],
        scratch_shapes=[pltpu.VMEM((N_PAIR, N, LANES), F32)],
        compiler_params=_cp("arbitrary"),
    )(xbc_act, dt_c, ac_c, dt_r, ac_r, hprev_all, dy, dsk_pair, a_row)


ROWS = 512
GW = SSM_INNER // SSM_GROUPS


def _rows(width, dtype=F32):
    return pl.BlockSpec((ROWS, width), lambda i: (i, 0))


def _vec(width):
    return pl.BlockSpec((1, width), lambda i: (0, 0))


def _gnorm_fwd(y, z, w):
    s = y.shape[0]

    def body(y_ref, z_ref, w_ref, o_ref):
        for g in range(SSM_GROUPS):
            cs = slice(g * GW, (g + 1) * GW)
            zz = z_ref[:, cs].astype(F32)
            u = y_ref[:, cs] * (zz * _sigmoid(zz))
            r = lax.rsqrt(jnp.mean(u * u, axis=1, keepdims=True) + RMS_EPS)
            o_ref[:, cs] = (u * r * w_ref[:, cs]).astype(o_ref.dtype)

    return pl.pallas_call(body, name="gnorm_fwd", grid=(s // ROWS,),
                          in_specs=[_rows(SSM_INNER), _rows(SSM_INNER), _vec(SSM_INNER)], out_specs=_rows(SSM_INNER),
                          out_shape=jax.ShapeDtypeStruct((s, SSM_INNER), BF16), compiler_params=_cp("parallel"))(y, z, w)


def _gnorm_bwd(y, z, w, do):
    s = y.shape[0]

    def body(y_ref, z_ref, w_ref, do_ref, dy_ref, dz_ref, dw_ref):
        @pl.when(pl.program_id(0) == 0)
        def _():
            dw_ref[...] = jnp.zeros_like(dw_ref)

        for g in range(SSM_GROUPS):
            cs = slice(g * GW, (g + 1) * GW)
            zz, yy, dd = z_ref[:, cs].astype(F32), y_ref[:, cs], do_ref[:, cs].astype(F32)
            sz, dsz = _silu_and_grad(zz)
            u = yy * sz
            r = lax.rsqrt(jnp.mean(u * u, axis=1, keepdims=True) + RMS_EPS)
            n = u * r
            dn = dd * w_ref[:, cs]
            dw_ref[:, cs] += jnp.sum(dd * n, axis=0, keepdims=True)
            du = r * (dn - n * jnp.mean(dn * n, axis=1, keepdims=True))
            dy_ref[:, cs] = du * sz
            dz_ref[:, cs] = (du * yy * dsz).astype(dz_ref.dtype)

    return pl.pallas_call(
        body, name="gnorm_bwd", grid=(s // ROWS,),
        in_specs=[_rows(SSM_INNER), _rows(SSM_INNER), _vec(SSM_INNER), _rows(SSM_INNER)],
        out_specs=[_rows(SSM_INNER), _rows(SSM_INNER), _vec(SSM_INNER)],
        out_shape=[jax.ShapeDtypeStruct((s, SSM_INNER), F32), jax.ShapeDtypeStruct((s, SSM_INNER), BF16),
                   jax.ShapeDtypeStruct((1, SSM_INNER), F32)],
        compiler_params=_cp("arbitrary"))(y, z, w, do)


def _mix_fwd(gl, bg, attn_d, ssm_d):
    s = gl.shape[0]
    d = D_MODEL

    def body(gl_ref, bg_ref, a_ref, m_ref, o_ref):
        g0 = _sigmoid(gl_ref[:, :d] + bg_ref[:, :d])
        g1 = _sigmoid(gl_ref[:, d:] + bg_ref[:, d:])
        o_ref[...] = (g0 * a_ref[...] + g1 * m_ref[...]).astype(o_ref.dtype)

    return pl.pallas_call(body, name="mix_fwd", grid=(s // ROWS,),
                          in_specs=[_rows(2 * d), _vec(2 * d), _rows(d), _rows(d)], out_specs=_rows(d),
                          out_shape=jax.ShapeDtypeStruct((s, d), BF16), compiler_params=_cp("parallel"))(
        gl, bg, attn_d, ssm_d)


def _mix_bwd(gl, bg, attn_d, ssm_d, dmix):
    s = gl.shape[0]
    d = D_MODEL

    def body(gl_ref, bg_ref, a_ref, m_ref, dm_ref, da_ref, ds_ref, dg_ref, db_ref):
        @pl.when(pl.program_id(0) == 0)
        def _():
            db_ref[...] = jnp.zeros_like(db_ref)

        g0 = _sigmoid(gl_ref[:, :d] + bg_ref[:, :d])
        g1 = _sigmoid(gl_ref[:, d:] + bg_ref[:, d:])
        dm = dm_ref[...].astype(F32)
        da_ref[...] = (dm * g0).astype(da_ref.dtype)
        ds_ref[...] = (dm * g1).astype(ds_ref.dtype)
        dl0 = dm * a_ref[...] * g0 * (1.0 - g0)
        dl1 = dm * m_ref[...] * g1 * (1.0 - g1)
        dg_ref[:, :d] = dl0.astype(dg_ref.dtype)
        dg_ref[:, d:] = dl1.astype(dg_ref.dtype)
        db_ref[:, :d] += jnp.sum(dl0, axis=0, keepdims=True)
        db_ref[:, d:] += jnp.sum(dl1, axis=0, keepdims=True)

    return pl.pallas_call(
        body, name="mix_bwd", grid=(s // ROWS,),
        in_specs=[_rows(2 * d), _vec(2 * d), _rows(d), _rows(d), _rows(d)],
        out_specs=[_rows(d), _rows(d), _rows(2 * d), _vec(2 * d)],
        out_shape=[jax.ShapeDtypeStruct((s, d), BF16), jax.ShapeDtypeStruct((s, d), BF16),
                   jax.ShapeDtypeStruct((s, 2 * d), BF16), jax.ShapeDtypeStruct((1, 2 * d), F32)],
        compiler_params=_cp("arbitrary"))(gl, bg, attn_d, ssm_d, dmix)


def _ln_stats(p):
    mu = jnp.mean(p, axis=1, keepdims=True)
    c = p - mu
    rstd = lax.rsqrt(jnp.mean(c * c, axis=1, keepdims=True) + LN_EPS)
    return c * rstd, rstd


def _ln_bwd(dy, xhat, rstd, g):
    dxh = dy * g
    return rstd * (dxh - jnp.mean(dxh, axis=1, keepdims=True) - xhat * jnp.mean(dxh * xhat, axis=1, keepdims=True))


def _ln1_fwd(x, mixed, g, b):
    s, d = x.shape

    def body(x_ref, m_ref, g_ref, b_ref, o_ref, ob_ref):
        xhat, _ = _ln_stats(DEEPNORM_ALPHA * x_ref[...] + m_ref[...])
        y = xhat * g_ref[...] + b_ref[...]
        o_ref[...] = y
        ob_ref[...] = y.astype(BF16)

    return pl.pallas_call(body, name="ln1_fwd", grid=(s // ROWS,), in_specs=[_rows(d), _rows(d), _vec(d), _vec(d)],
                          out_specs=[_rows(d), _rows(d)],
                          out_shape=[jax.ShapeDtypeStruct((s, d), F32), jax.ShapeDtypeStruct((s, d), BF16)],
                          compiler_params=_cp("parallel"))(x, mixed, g, b)


def _ln2_loss(x1, h, target, g, b):
    s, d = x1.shape

    def body(x_ref, h_ref, t_ref, g_ref, b_ref, dp_ref, dpb_ref, loss_ref, dg_ref, db_ref):
        @pl.when(pl.program_id(0) == 0)
        def _():
            loss_ref[...] = jnp.zeros_like(loss_ref)
            dg_ref[...] = jnp.zeros_like(dg_ref)
            db_ref[...] = jnp.zeros_like(db_ref)

        xhat, rstd = _ln_stats(DEEPNORM_ALPHA * x_ref[...] + h_ref[...])
        err = xhat * g_ref[...] + b_ref[...] - t_ref[...]
        part = 0.5 * jnp.sum(jnp.mean(err * err, axis=1, keepdims=True), axis=0, keepdims=True)
        loss_ref[...] += jnp.broadcast_to(part, loss_ref.shape)
        dy = err * (1.0 / d)
        dg_ref[...] += jnp.sum(dy * xhat, axis=0, keepdims=True)
        db_ref[...] += jnp.sum(dy, axis=0, keepdims=True)
        dp = _ln_bwd(dy, xhat, rstd, g_ref[...])
        dp_ref[...] = dp
        dpb_ref[...] = dp.astype(BF16)

    return pl.pallas_call(
        body, name="ln2_loss", grid=(s // ROWS,), in_specs=[_rows(d), _rows(d), _rows(d), _vec(d), _vec(d)],
        out_specs=[_rows(d), _rows(d), _vec(LANES), _vec(d), _vec(d)],
        out_shape=[jax.ShapeDtypeStruct((s, d), F32), jax.ShapeDtypeStruct((s, d), BF16),
                   jax.ShapeDtypeStruct((1, LANES), F32),
                   jax.ShapeDtypeStruct((1, d), F32), jax.ShapeDtypeStruct((1, d), F32)],
        compiler_params=_cp("arbitrary"))(x1, h, target, g, b)


def _ln1_bwd(x, mixed, g, dpre2, dffn):
    s, d = x.shape

    def body(x_ref, m_ref, g_ref, d2_ref, df_ref, dp_ref, dr_ref, dg_ref, db_ref):
        @pl.when(pl.program_id(0) == 0)
        def _():
            dg_ref[...] = jnp.zeros_like(dg_ref)
            db_ref[...] = jnp.zeros_like(db_ref)

        xhat, rstd = _ln_stats(DEEPNORM_ALPHA * x_ref[...] + m_ref[...])
        dy = DEEPNORM_ALPHA * d2_ref[...] + df_ref[...]
        dg_ref[...] += jnp.sum(dy * xhat, axis=0, keepdims=True)
        db_ref[...] += jnp.sum(dy, axis=0, keepdims=True)
        dp = _ln_bwd(dy, xhat, rstd, g_ref[...])
        dp_ref[...] = dp.astype(BF16)
        dr_ref[...] = DEEPNORM_ALPHA * dp

    return pl.pallas_call(
        body, name="ln1_bwd", grid=(s // ROWS,), in_specs=[_rows(d), _rows(d), _vec(d), _rows(d), _rows(d)],
        out_specs=[_rows(d), _rows(d), _vec(d), _vec(d)],
        out_shape=[jax.ShapeDtypeStruct((s, d), BF16), jax.ShapeDtypeStruct((s, d), F32),
                   jax.ShapeDtypeStruct((1, d), F32), jax.ShapeDtypeStruct((1, d), F32)],
        compiler_params=_cp("arbitrary"))(x, mixed, g, dpre2, dffn)


def _swiglu_fwd(gu):
    s = gu.shape[0]
    f = FFN_HIDDEN

    def body(g_ref, u_ref, o_ref):
        gg = g_ref[...].astype(F32)
        o_ref[...] = (gg * _sigmoid(gg) * u_ref[...].astype(F32)).astype(o_ref.dtype)

    return pl.pallas_call(
        body, name="swiglu_fwd", grid=(s // ROWS,),
        in_specs=[pl.BlockSpec((ROWS, f), lambda i: (i, 0)), pl.BlockSpec((ROWS, f), lambda i: (i, 1))],
        out_specs=_rows(f), out_shape=jax.ShapeDtypeStruct((s, f), BF16), compiler_params=_cp("parallel"))(gu, gu)


def _swiglu_bwd(gu, dact):
    s = gu.shape[0]
    f = FFN_HIDDEN

    def body(g_ref, u_ref, d_ref, o_ref):
        sg, dsg = _silu_and_grad(g_ref[...].astype(F32))
        dd = d_ref[...].astype(F32)
        o_ref[:, :f] = (dd * u_ref[...].astype(F32) * dsg).astype(o_ref.dtype)
        o_ref[:, f:] = (dd * sg).astype(o_ref.dtype)

    return pl.pallas_call(
        body, name="swiglu_bwd", grid=(s // ROWS,),
        in_specs=[pl.BlockSpec((ROWS, f), lambda i: (i, 0)), pl.BlockSpec((ROWS, f), lambda i: (i, 1)), _rows(f)],
        out_specs=_rows(2 * f), out_shape=jax.ShapeDtypeStruct((s, 2 * f), BF16),
        compiler_params=_cp("parallel"))(gu, gu, dact)


def _peer(k):
    x, y, c = lax.axis_index("x"), lax.axis_index("y"), lax.axis_index("c")
    kx, ky, kc = (k >> 2) & 1, (k >> 1) & 1, k & 1
    px = (1 - x) if kx else x
    py = (1 - y) if ky else y
    pc = (1 - c) if kc else c
    return (px, py, pc), 4 * px + 2 * py + pc


def _my_index():
    return 4 * lax.axis_index("x") + 2 * lax.axis_index("y") + lax.axis_index("c")


def _comm_copies(ins, outs, sems, scatter):
    send_sems, recv_sems, local_sems = sems
    me = _my_index()
    copies = [pltpu.make_async_copy(ins[t].at[me] if scatter else ins[t], outs[t].at[me], local_sems.at[t])
              for t in range(len(ins))]
    for k in range(1, N_DEV):
        peer, pidx = _peer(k)
        for t in range(len(ins)):
            copies.append(pltpu.make_async_remote_copy(
                src_ref=ins[t].at[pidx] if scatter else ins[t], dst_ref=outs[t].at[me],
                send_sem=send_sems.at[t, k - 1], recv_sem=recv_sems.at[t, k - 1], device_id=peer,
                device_id_type=pl.DeviceIdType.MESH))
    return copies


def _comm_sems(n):
    return [pltpu.SemaphoreType.DMA((n, N_DEV - 1)), pltpu.SemaphoreType.DMA((n, N_DEV - 1)),
            pltpu.SemaphoreType.DMA((n,))]


def _comm_out_shapes(parts, scatter):
    return [jax.ShapeDtypeStruct(p.shape if scatter else (N_DEV,) + p.shape, p.dtype) for p in parts]


def _all_gather(parts):
    n = len(parts)

    def body(*refs):
        ins, outs = refs[:n], refs[n:2 * n]
        send_sems, recv_sems, local_sems = refs[2 * n:]
        x, y, c = lax.axis_index("x"), lax.axis_index("y"), lax.axis_index("c")
        me, sibling = (x, y, c), (x, y, 1 - c)
        chips = [(1 - x, y), (x, 1 - y), (1 - x, 1 - y)]

        def copy(t, k, block, to, src=None):
            dst = outs[t].at[4 * block[0] + 2 * block[1] + block[2]]
            return pltpu.make_async_remote_copy(
                src_ref=dst if src is None else src, dst_ref=dst, send_sem=send_sems.at[t, k],
                recv_sem=recv_sems.at[t, k], device_id=to, device_id_type=pl.DeviceIdType.MESH)

        mine = [pltpu.make_async_copy(ins[t], outs[t].at[_my_index()], local_sems.at[t]) for t in range(n)]
        for cp in mine:
            cp.start()
        first = [copy(t, 0, me, sibling, src=ins[t]) for t in range(n)]
        first += [copy(t, 1 + j, me, (*chip, c), src=ins[t]) for j, chip in enumerate(chips) for t in range(n)]
        for cp in first:
            cp.start()
        passed = []
        for j, chip in enumerate(chips):
            for t in range(n):
                copy(t, 1 + j, (*chip, c), me).wait_recv()
                passed.append(copy(t, 4 + j, (*chip, c), sibling))
                passed[-1].start()
        for t in range(n):
            copy(t, 0, sibling, me).wait_recv()
            for j, chip in enumerate(chips):
                copy(t, 4 + j, (*chip, 1 - c), me).wait_recv()
        for cp in first + passed:
            cp.wait_send()
        for cp in mine:
            cp.wait()

    anyspec = pl.BlockSpec(memory_space=pl.ANY)
    return pl.pallas_call(body, name="all_gather", in_specs=[anyspec] * n, out_specs=[anyspec] * n,
                          out_shape=_comm_out_shapes(parts, False), scratch_shapes=_comm_sems(n))(*parts)


def _remote_scatter_copies(ins, lands, send_sems, recv_sems):
    me = _my_index()
    copies = []
    for k in range(1, N_DEV):
        peer, pidx = _peer(k)
        for t in range(len(ins)):
            copies.append(pltpu.make_async_remote_copy(
                src_ref=ins[t].at[pidx], dst_ref=lands[t].at[me], send_sem=send_sems.at[t * (N_DEV - 1) + k - 1],
                recv_sem=recv_sems.at[t * (N_DEV - 1) + k - 1], device_id=peer, device_id_type=pl.DeviceIdType.MESH))
    return copies


def _landing_zones(parts):
    me = _my_index()
    return [jnp.where(lax.broadcasted_iota(jnp.int32, p.shape, 0) == me, p, jnp.zeros_like(p)) for p in parts]


_HBM = pl.BlockSpec(memory_space=pltpu.HBM)
_SEM = pl.BlockSpec(memory_space=pltpu.SEMAPHORE)


def _exchange_start(parts, lands):
    n = len(parts)

    def body(*refs):
        ins, lnd, send_sems, recv_sems, token = refs[:n], refs[n:2 * n], refs[2 * n], refs[2 * n + 1], refs[-1]
        for cp in _remote_scatter_copies(ins, lnd, send_sems, recv_sems):
            cp.start()
        token[...] = jnp.zeros_like(token)

    hbm = [pltpu.HBM(p.shape, p.dtype) for p in parts]
    outs = pl.pallas_call(
        body, name="exchange_start",
        out_shape=[pltpu.SemaphoreType.DMA((n * (N_DEV - 1),)), pltpu.SemaphoreType.DMA((n * (N_DEV - 1),))] + hbm + hbm
        + [jax.ShapeDtypeStruct((8, LANES), F32)],
        in_specs=[_HBM] * (2 * n), out_specs=[_SEM, _SEM] + [_HBM] * (2 * n) + [pl.BlockSpec(memory_space=pltpu.VMEM)],
        input_output_aliases={t: 2 + t for t in range(2 * n)},
        compiler_params=pltpu.CompilerParams(has_side_effects=pltpu.SideEffectType.DATAFLOW_SIDE_EFFECTING),
    )(*[pltpu.with_memory_space_constraint(p, pltpu.HBM) for p in list(parts) + list(lands)])
    return outs[0], outs[1], list(outs[2:2 + n]), list(outs[2 + n:2 + 2 * n]), outs[-1]


def _exchange_wait(send_sems, recv_sems, parts, lands, after):
    n = len(parts)

    def body(*refs):
        ins, lnd, send_sems, recv_sems = refs[:n], refs[n:2 * n], refs[2 * n], refs[2 * n + 1]
        for cp in _remote_scatter_copies(ins, lnd, send_sems, recv_sems):
            cp.wait_send()
            cp.wait_recv()

    hbm = [pltpu.HBM(p.shape, p.dtype) for p in parts]
    outs = pl.pallas_call(
        body, name="exchange_wait", out_shape=hbm + hbm,
        in_specs=[_HBM] * (2 * n) + [_SEM, _SEM, pl.BlockSpec(memory_space=pl.ANY)], out_specs=[_HBM] * (2 * n),
        input_output_aliases={t: t for t in range(2 * n)},
        compiler_params=pltpu.CompilerParams(has_side_effects=pltpu.SideEffectType.DATAFLOW_SIDE_EFFECTING),
    )(*parts, *lands, send_sems, recv_sems, after)
    return list(outs[n:])


def _adamw(recv, w, m, v, name):
    _, r, c = w.shape
    br = _tile(r, 128)
    c1 = 1.0 / (1.0 - ADAM_B1 ** ADAM_STEP)
    c2 = 1.0 / (1.0 - ADAM_B2 ** ADAM_STEP)

    def body(r_ref, w_ref, m_ref, v_ref, g_ref, d_ref, mo_ref, vo_ref):
        g = r_ref[0].astype(F32)
        for k in range(1, N_DEV):
            g = g + r_ref[k].astype(F32)
        mn = ADAM_B1 * m_ref[0] + (1.0 - ADAM_B1) * g
        vn = ADAM_B2 * v_ref[0] + (1.0 - ADAM_B2) * (g * g)
        g_ref[0] = g
        mo_ref[0] = mn
        vo_ref[0] = vn
        d_ref[0] = -ADAM_LR * ((mn * c1) / (jnp.sqrt(vn * c2) + ADAM_EPS) + ADAM_WD * w_ref[0])

    blk = pl.BlockSpec((1, br, c), lambda i: (0, i, 0))
    return pl.pallas_call(
        body, name=name, grid=(r // br,),
        in_specs=[pl.BlockSpec((N_DEV, br, c), lambda i: (0, i, 0)), blk, blk, blk],
        out_specs=[blk] * 4, out_shape=[jax.ShapeDtypeStruct((1, r, c), F32)] * 4,
        compiler_params=_cp("parallel"))(recv, w, m, v)


def _lane_row(pairs):
    row = jnp.zeros((LANES,), F32)
    for lane0, vec in pairs:
        row = lax.dynamic_update_slice(row, vec.astype(F32), (lane0,))
    return row.reshape(1, LANES)


def _stage_in(x, wts, small):
    s = x.shape[0]
    a = -jnp.exp(small["a_log"])
    bias_row = _lane_row([(DT_LANE0, small["dt_bias"]), (F_LANE0, small["b_forget"])])
    a_row = _lane_row([(DT_LANE0, a)])
    conv_b = small["conv_b"].reshape(1, -1)
    norm_w = small["ssm_norm_w"].reshape(1, -1)
    bg = small["b_gates"].reshape(1, -1)
    g1, b1 = small["ln1_g"].reshape(1, -1), small["ln1_b"].reshape(1, -1)
    g2, b2 = small["ln2_g"].reshape(1, -1), small["ln2_b"].reshape(1, -1)
    d_skip = small["d_skip"]
    xb = x.astype(BF16)

    qkv = _mm(xb, wts["qkv"], out_dtype=BF16, name="f_qkv")
    z = _mm(xb, wts["z"], out_dtype=BF16, name="f_z")
    xbc = _mm(xb, wts["xbc"], name="f_xbc")
    gl = _mm(xb, wts["gate"], out_dtype=BF16, name="f_gate")
    fd = _mm(xb, wts["fd"], name="f_fd")
    dt_c, ac_c, cf_c, dt_r, ac_r, cf_r = _stats_fwd(fd, bias_row, a_row)
    bk = _att_blocks(s)[1]
    ck4 = cf_r[F_LANE0:F_LANE0 + ATT_HEADS].reshape(N_HP, HP, s // bk, bk)
    return dict(locals())


def _stage_mid(c, attn, lse, wts, target):
    x, xb, qkv, z, xbc, gl, fd, ck4 = (c[k] for k in ("x", "xb", "qkv", "z", "xbc", "gl", "fd", "ck4"))
    dt_c, ac_c, dt_r, ac_r, a_row, bias_row = (c[k] for k in ("dt_c", "ac_c", "dt_r", "ac_r", "a_row", "bias_row"))
    conv_b, norm_w, bg, g1, b1, g2, b2, d_skip = (c[k] for k in ("conv_b", "norm_w", "bg", "g1", "b1", "g2", "b2",
                                                                "d_skip"))
    conv_w = c["wts"]["conv"]
    attn_d = _mm(attn, wts["pa"], out_dtype=BF16, name="f_pa")
    xact = _conv_fwd(xbc, conv_w, conv_b)
    dsk_pair = jnp.repeat(d_skip, SSM_HEAD_DIM).reshape(N_PAIR, LANES)
    y, hprev = _ssd_pair_fwd(xact, ac_c, dt_r, ac_r, dsk_pair)
    ssm = _gnorm_fwd(y, z, norm_w)
    ssm_d = _mm(ssm, wts["ps"], out_dtype=BF16, name="f_ps")
    mix = _mix_fwd(gl, bg, attn_d, ssm_d)
    mixed = _mm(mix, wts["out"], name="f_out")
    x1, x1_b = _ln1_fwd(x, mixed, g1, b1)
    gu = _mm(x1_b, wts["gu"], out_dtype=BF16, name="f_gu")
    act = _swiglu_fwd(gu)
    h = _mm(act, wts["down"], name="f_down")
    dpre2, dpre2_b, loss_row, dg2, db2 = _ln2_loss(x1, h, target, g2, b2)

    d_act = _mm(dpre2_b, wts["down"], tb=True, out_dtype=BF16, name="b_down_x")
    dw_down = _mm(act, dpre2_b, ta=True, name="b_down_w")
    dgu = _swiglu_bwd(gu, d_act)
    dffn = _mm(dgu, wts["gu"], tb=True, name="b_gu_x")
    dw_gu = _mm(x1_b, dgu, ta=True, name="b_gu_w")
    dpre1, dxr, dg1, db1 = _ln1_bwd(x, mixed, g1, dpre2, dffn)
    dmix = _mm(dpre1, wts["out"], tb=True, out_dtype=BF16, name="b_out_x")
    dw_out = _mm(mix, dpre1, ta=True, name="b_out_w")
    dattn_d, dssm_d, dgl, dbg = _mix_bwd(gl, bg, attn_d, ssm_d, dmix)
    dssm = _mm(dssm_d, wts["ps"], tb=True, out_dtype=BF16, name="b_ps_x")
    dw_ps = _mm(ssm, dssm_d, ta=True, name="b_ps_w")
    dattn = _mm(dattn_d, wts["pa"], tb=True, name="b_pa_x")
    dw_pa = _mm(attn, dattn_d, ta=True, name="b_pa_w")
    dy, dz, dnw = _gnorm_bwd(y, z, norm_w, dssm)
    dxact, ddt, da_row, dds_pair = _ssd_pair_bwd(xact, dt_c, ac_c, dt_r, ac_r, hprev, dy, dsk_pair, a_row)
    dds = dds_pair.reshape(SSM_HEADS, SSM_HEAD_DIM).sum(axis=1)
    dpre_c, dconv_w, dconv_b = _conv_bwd_pre(xbc, conv_w, conv_b, dxact)
    dxbc = _conv_bwd_in(dpre_c, conv_w)
    st, do_b = _att_prep(dattn, attn, lse)
    late = dict(pa=dw_pa, ps=dw_ps, out=dw_out, gu=dw_gu, down=dw_down)
    keep = ("st", "do_b", "ddt", "dxr", "dz", "dxbc", "dgl", "dconv_w", "dconv_b", "da_row", "dds", "dnw", "dbg",
            "dg1", "db1", "dg2", "db2", "loss_row")
    loc = locals()
    return {**c, **{k: loc[k] for k in keep}}, late


def _stage_out_w(c, att_grads):
    dq, dk, dv, dck, dcq = att_grads
    xb, fd, bias_row, ddt, dz, dxbc, dgl = (c[k] for k in ("xb", "fd", "bias_row", "ddt", "dz", "dxbc", "dgl"))
    s, a = xb.shape[0], c["a"]
    dcum = dck.reshape(ATT_HEADS, s) + dcq.reshape(ATT_HEADS, s)
    dfd, dbias = _stats_bwd(fd, bias_row, ddt, jnp.zeros((LANES, s), F32).at[F_LANE0:F_LANE0 + ATT_HEADS].set(dcum))
    dproj = (dq, dk, dv, dz, dxbc, dgl, dfd)
    dw_in = [_mm(xb, g_, ta=True, tb=(i == 0), name=f"b_in_w{i}") for i, g_ in enumerate(dproj)]
    grads = dict(q=dw_in[0], k=dw_in[1], v=dw_in[2], z=dw_in[3], xbc=dw_in[4], gate=dw_in[5], fd=dw_in[6],
                 conv=c["dconv_w"])
    small_g = dict(
        b_forget=dbias[0, F_LANE0:F_LANE0 + ATT_HEADS], conv_b=c["dconv_b"][0], dt_bias=dbias[0, :SSM_HEADS],
        a_log=c["da_row"][0, :SSM_HEADS] * a, d_skip=c["dds"], ssm_norm_w=c["dnw"][0], b_gates=c["dbg"][0],
        ln1_g=c["dg1"][0], ln1_b=c["db1"][0], ln2_g=c["dg2"][0], ln2_b=c["db2"][0])
    return c["loss_row"][0, 0], grads, small_g, dproj


def _stage_out_x(c, dproj, token):
    wts, d = c["wts"], D_MODEL
    wq = wts["qkv"][:, :d] + token.astype(BF16)
    wk, wv = wts["qkv"][:, d:2 * d], wts["qkv"][:, 2 * d:]
    dx = c["dxr"]
    for i, (g_, w_) in enumerate(zip(dproj, (wq, wk, wv, wts["z"], wts["xbc"], wts["gate"], wts["fd"]))):
        dx = _mm(g_, w_, ta=(i == 0), tb=True, add=dx, name=f"b_in_x{i}")
    return dx


BIG = ("w_in", "w_proj_attn", "w_proj_ssm", "w_out", "w_ffn_gate", "w_ffn_up", "w_ffn_down", "conv_w")
EARLY = ("w_in", "conv_w")
LATE = ("w_proj_attn", "w_proj_ssm", "w_out", "w_ffn_gate", "w_ffn_up", "w_ffn_down")
SMALL = ("b_forget", "conv_b", "dt_bias", "a_log", "d_skip", "ssm_norm_w", "b_gates", "ln1_g", "ln1_b", "ln2_g",
         "ln2_b")
SMALL_ROWS = 96
IN_SHARD = IN_WIDTH // N_DEV
IN_SEGMENTS = (("q", 0, 1024), ("k", 1024, 1024), ("v", 2048, 1024), ("f", 3072, ATT_HEADS), ("z", 3088, SSM_INNER),
               ("xbc", 5136, SSM_CONV_DIM), ("dt", 8208, SSM_HEADS), ("gate", 8240, 2 * D_MODEL))


def _cols_from_shards(shards, lo, hi):
    w = shards[0].shape[1]
    pieces = []
    for j in range(len(shards)):
        a, b = max(lo, j * w), min(hi, (j + 1) * w)
        if a < b:
            pieces.append(shards[j][:, a - j * w:b - j * w])
    return pieces[0] if len(pieces) == 1 else jnp.concatenate(pieces, axis=1)


def _shards_from_parts(parts, width):
    shards = []
    for j in range(N_DEV):
        lo, hi = j * width, (j + 1) * width
        pieces = []
        for mat, c0 in parts:
            a, b = max(lo, c0), min(hi, c0 + mat.shape[1])
            if a < b:
                pieces.append(mat[:, a - c0:b - c0])
        shards.append(pieces[0] if len(pieces) == 1 else jnp.concatenate(pieces, axis=1))
    return shards


def _pack_small(vals):
    flat = jnp.concatenate([vals[n].reshape(-1) for n in SMALL])
    return jnp.pad(flat, (0, SMALL_ROWS * LANES - flat.shape[0])).reshape(SMALL_ROWS, LANES)


def _unpack_small(pack, shapes):
    flat = pack.reshape(-1)
    out, off = {}, 0
    for n in SMALL:
        sz = math.prod(shapes[n])
        out[n] = flat[off:off + sz].reshape(shapes[n])
        off += sz
    return out


def kernel(x, w_in, b_forget, conv_w, conv_b, dt_bias, a_log, d_skip, ssm_norm_w, w_proj_attn, w_proj_ssm, b_gates, w_out, ln1_g, ln1_b, w_ffn_gate, w_ffn_up, w_ffn_down, ln2_g, ln2_b, loss_target, m_w_in, m_b_forget, m_conv_w, m_conv_b, m_dt_bias, m_a_log, m_d_skip, m_ssm_norm_w, m_w_proj_attn, m_w_proj_ssm, m_b_gates, m_w_out, m_ln1_g, m_ln1_b, m_w_ffn_gate, m_w_ffn_up, m_w_ffn_down, m_ln2_g, m_ln2_b, v_w_in, v_b_forget, v_conv_w, v_conv_b, v_dt_bias, v_a_log, v_d_skip, v_ssm_norm_w, v_w_proj_attn, v_w_proj_ssm, v_b_gates, v_w_out, v_ln1_g, v_ln1_b, v_w_ffn_gate, v_w_ffn_up, v_w_ffn_down, v_ln2_g, v_ln2_b):
    args = dict(locals())
    d, f = D_MODEL, FFN_HIDDEN
    big_w = {n: args[n][0] for n in BIG}
    small_w = {n: args[n][0] for n in SMALL}
    big_shapes = {n: args[n].shape for n in BIG}
    small_shapes = {n: args[n].shape for n in SMALL}

    early = dict(zip(EARLY, _all_gather([big_w["w_in"].astype(BF16), big_w["conv_w"]])))
    in_shards = [early["w_in"][j] for j in range(N_DEV)]
    seg = {n: _cols_from_shards(in_shards, c0, c0 + w) for n, c0, w in IN_SEGMENTS}
    wfd = jnp.concatenate([seg["dt"], seg["f"], jnp.zeros((d, LANES - SSM_HEADS - ATT_HEADS), BF16)], axis=1)
    wts = dict(qkv=jnp.concatenate([seg["q"], seg["k"], seg["v"]], axis=1), z=seg["z"], xbc=seg["xbc"],
               gate=seg["gate"], fd=wfd, conv=jnp.concatenate([early["conv_w"][j] for j in range(N_DEV)], axis=1))

    ctx = _stage_in(x[0], wts, small_w)
    attn, lse, gathered = _attention_fwd(ctx["qkv"], ctx["ck4"], [big_w[n].astype(BF16) for n in LATE])
    full = dict(zip(LATE, gathered))
    late_w = dict(
        pa=full["w_proj_attn"].reshape(d, d), ps=full["w_proj_ssm"].reshape(SSM_INNER, d),
        out=full["w_out"].reshape(d, d),
        gu=jnp.concatenate([full["w_ffn_gate"][j] for j in range(N_DEV)]
                           + [full["w_ffn_up"][j] for j in range(N_DEV)], axis=1),
        down=full["w_ffn_down"].reshape(f, d))
    ctx, gl = _stage_mid(ctx, attn, lse, late_w, loss_target[0])
    late_dest = dict(
        w_ffn_gate=jnp.stack([s_.astype(BF16) for s_ in _shards_from_parts([(gl["gu"][:, :f], 0)], f // N_DEV)]),
        w_ffn_up=jnp.stack([s_.astype(BF16) for s_ in _shards_from_parts([(gl["gu"][:, f:], 0)], f // N_DEV)]))
    for n, key in (("w_proj_attn", "pa"), ("w_proj_ssm", "ps"), ("w_out", "out"), ("w_ffn_down", "down")):
        late_dest[n] = gl[key].astype(BF16).reshape((N_DEV,) + big_shapes[n][1:])
    att_grads, late_recv = _attention_bwd(ctx["qkv"], ctx["ck4"], ctx["st"], ctx["do_b"], [late_dest[n] for n in LATE])
    loss_part, g, small_g, dproj = _stage_out_w(ctx, att_grads)
    loss = lax.psum(loss_part, ("x", "y", "c"))

    gfd = g["fd"]
    in_parts = dict(q=g["q"], k=g["k"], v=g["v"], f=gfd[:, F_LANE0:F_LANE0 + ATT_HEADS], z=g["z"], xbc=g["xbc"],
                    dt=gfd[:, DT_LANE0:DT_LANE0 + SSM_HEADS], gate=g["gate"])
    win_dest = jnp.stack([s_.astype(BF16) for s_ in
                          _shards_from_parts([(in_parts[n], c0) for n, c0, _ in IN_SEGMENTS], IN_SHARD)])
    conv_dest = jnp.stack(_shards_from_parts([(g["conv"], 0)], SSM_CONV_DIM // N_DEV))
    small_pack = _pack_small(small_g)
    last_parts = [win_dest, conv_dest, jnp.broadcast_to(small_pack, (N_DEV,) + small_pack.shape)]
    send_sems, recv_sems, parts_thru, lands_thru, token = _exchange_start(last_parts, _landing_zones(last_parts))
    grad_x = _stage_out_x(ctx, dproj, token[0, 0])
    early_recv = _exchange_wait(send_sems, recv_sems, parts_thru, lands_thru, grad_x)
    recv = dict(zip(LATE, late_recv))
    recv["w_in"], recv["conv_w"] = early_recv[0], early_recv[1]

    outs = {}
    for n in BIG:
        outs[n] = _adamw(recv[n], args[n], args["m_" + n], args["v_" + n], name="adamw_" + n)
    small4 = _adamw(early_recv[2], _pack_small(small_w)[None], _pack_small({n: args["m_" + n][0] for n in SMALL})[None],
                    _pack_small({n: args["v_" + n][0] for n in SMALL})[None], name="adamw_small")
    small_out = [_unpack_small(p, small_shapes) for p in small4]
    for n in SMALL:
        outs[n] = [so[n] for so in small_out]

    order = ("w_in", "b_forget", "conv_w", "conv_b", "dt_bias", "a_log", "d_skip", "ssm_norm_w", "w_proj_attn",
             "w_proj_ssm", "b_gates", "w_out", "ln1_g", "ln1_b", "w_ffn_gate", "w_ffn_up", "w_ffn_down", "ln2_g",
             "ln2_b")
    res = [loss, grad_x[None]]
    for i in range(4):
        res += [outs[n][i] for n in order]
    return tuple(res)
```

```python
import functools
import math

import jax
import jax.numpy as jnp
from jax import lax
from jax.experimental import pallas as pl
from jax.experimental.pallas import tpu as pltpu

F32 = jnp.float32
BF16 = jnp.bfloat16

N_DEV = 8
D_MODEL = 1024
ATT_HEADS = 16
ATT_HEAD_DIM = 64
SSM_INNER = 2048
SSM_HEADS = 32
SSM_HEAD_DIM = 64
SSM_GROUPS = 4
SSM_HEADS_PER_GROUP = 8
SSM_STATE = 128
SSM_CONV = 4
SSM_CHUNK = 128
SSM_CONV_DIM = 3072
FFN_HIDDEN = 2816
IN_WIDTH = 10288
DEEPNORM_ALPHA = 2.0 ** 0.25
LN_EPS = 1e-5
RMS_EPS = 1e-5
ADAM_LR, ADAM_B1, ADAM_B2, ADAM_EPS, ADAM_WD, ADAM_STEP = 0.001, 0.9, 0.999, 1e-08, 0.01, 10
ATT_SCALE = 1.0 / math.sqrt(ATT_HEAD_DIM)

LANES = 128
VMEM_LIMIT = 56 * 1024 * 1024
NEG = -1e30

DT_LANE0 = 0
F_LANE0 = 32
HI = lax.Precision.HIGHEST


def _cp(*sem):
    return pltpu.CompilerParams(dimension_semantics=sem, vmem_limit_bytes=VMEM_LIMIT)


def _tile(n, cap=1408):
    for t in (1408, 1024, 512, 384, 256, 128):
        if t <= cap and n % t == 0:
            return t
    return n


def _sigmoid(x):
    return 1.0 / (1.0 + jnp.exp(-x))


def _mm(a, b, *, ta=False, tb=False, out_dtype=F32, add=None, name):
    m, k = (a.shape[1], a.shape[0]) if ta else a.shape
    n = b.shape[0] if tb else b.shape[1]
    assert (b.shape[1] if tb else b.shape[0]) == k
    tm, tn, tk = _tile(m), _tile(n), _tile(k)
    nk = k // tk
    dims = (((0,) if ta else (1,), (1,) if tb else (0,)), ((), ()))

    def body_single(*refs):
        a_ref, b_ref = refs[:2]
        r = lax.dot_general(a_ref[...].astype(BF16), b_ref[...].astype(BF16), dims, preferred_element_type=F32)
        if add is not None:
            r = r + refs[2][...]
        refs[-1][...] = r.astype(refs[-1].dtype)

    def body(*refs):
        if add is None:
            a_ref, b_ref, o_ref, acc_ref = refs
        else:
            a_ref, b_ref, c_ref, o_ref, acc_ref = refs
        kk = pl.program_id(2)

        @pl.when(kk == 0)
        def _():
            acc_ref[...] = jnp.zeros_like(acc_ref)

        acc_ref[...] += lax.dot_general(a_ref[...].astype(BF16), b_ref[...].astype(BF16), dims,
                                        preferred_element_type=F32)

        @pl.when(kk == nk - 1)
        def _():
            r = acc_ref[...]
            if add is not None:
                r = r + c_ref[...]
            o_ref[...] = r.astype(o_ref.dtype)

    a_spec = pl.BlockSpec((tk, tm), lambda i, j, kk: (kk, i)) if ta else pl.BlockSpec((tm, tk), lambda i, j, kk: (i, kk))
    b_spec = pl.BlockSpec((tn, tk), lambda i, j, kk: (j, kk)) if tb else pl.BlockSpec((tk, tn), lambda i, j, kk: (kk, j))
    o_spec = pl.BlockSpec((tm, tn), lambda i, j, kk: (i, j))
    in_specs, args = [a_spec, b_spec], [a, b]
    if add is not None:
        in_specs.append(o_spec)
        args.append(add)
    return pl.pallas_call(
        body_single if nk == 1 else body, name=name, grid=(m // tm, n // tn, nk), in_specs=in_specs, out_specs=o_spec,
        out_shape=jax.ShapeDtypeStruct((m, n), out_dtype),
        scratch_shapes=[] if nk == 1 else [pltpu.VMEM((tm, tn), F32)],
        compiler_params=_cp("parallel", "parallel", "arbitrary"),
    )(*args)


def _mm_sum(pairs, add, name):
    m, n = add.shape
    tm, tk = _tile(m, 512), 1024
    steps = [b.shape[1] // tk for _, b, _ in pairs]
    starts = [sum(steps[:p]) for p in range(len(pairs))]
    total = sum(steps)

    def body(*refs):
        a_refs, b_refs = refs[:len(pairs)], refs[len(pairs):2 * len(pairs)]
        c_ref, o_ref, acc_ref = refs[2 * len(pairs):]
        kk = pl.program_id(1)

        @pl.when(kk == 0)
        def _():
            acc_ref[...] = c_ref[...]

        for p, (_, _, at) in enumerate(pairs):
            @pl.when((kk >= starts[p]) & (kk < starts[p] + steps[p]))
            def _(p=p, at=at):
                dims = (((0,) if at else (1,), (1,)), ((), ()))
                acc_ref[...] += lax.dot_general(a_refs[p][...].astype(BF16), b_refs[p][...].astype(BF16), dims,
                                                preferred_element_type=F32)

        @pl.when(kk == total - 1)
        def _():
            o_ref[...] = acc_ref[...]

    def kidx(p):
        return lambda kk: jnp.clip(kk - starts[p], 0, steps[p] - 1)

    a_specs = [pl.BlockSpec((tk, tm), lambda i, kk, f=kidx(p): (f(kk), i)) if at else
               pl.BlockSpec((tm, tk), lambda i, kk, f=kidx(p): (i, f(kk))) for p, (_, _, at) in enumerate(pairs)]
    b_specs = [pl.BlockSpec((n, tk), lambda i, kk, f=kidx(p): (0, f(kk))) for p in range(len(pairs))]
    o_spec = pl.BlockSpec((tm, n), lambda i, kk: (i, 0))
    return pl.pallas_call(
        body, name=name, grid=(m // tm, total), in_specs=a_specs + b_specs + [o_spec], out_specs=o_spec,
        out_shape=jax.ShapeDtypeStruct((m, n), F32), scratch_shapes=[pltpu.VMEM((tm, n), F32)],
        compiler_params=_cp("parallel", "arbitrary"),
    )(*[a for a, _, _ in pairs], *[b for _, b, _ in pairs], add)


def _tri(n, lower=True):
    r = lax.broadcasted_iota(jnp.int32, (n, n), 0)
    c = lax.broadcasted_iota(jnp.int32, (n, n), 1)
    return jnp.where((r >= c) if lower else (c >= r), 1.0, 0.0).astype(F32)


def _stats_fwd(fd, bias_row, a_row):
    s = fd.shape[0]
    blk = SSM_CHUNK

    def body(fd_ref, bias_ref, a_ref, dt_ref, ac_ref, cf_ref, dtr_ref, acr_ref, cfr_ref, carry_ref):
        @pl.when(pl.program_id(0) == 0)
        def _():
            carry_ref[...] = jnp.zeros_like(carry_ref)

        v = fd_ref[...] + bias_ref[...]
        dt = jnp.maximum(v, 0.0) + jnp.log(1.0 + jnp.exp(-jnp.abs(v)))
        lf = jnp.minimum(v, 0.0) - jnp.log(1.0 + jnp.exp(-jnp.abs(v)))
        tri = _tri(blk)
        ac = jnp.dot(tri, dt * a_ref[...], precision=HI, preferred_element_type=F32)
        cf = jnp.dot(tri, lf, precision=HI, preferred_element_type=F32) + carry_ref[0:1, :]
        carry_ref[...] = carry_ref[...] + jnp.sum(lf, axis=0, keepdims=True)
        dt_ref[...] = dt
        ac_ref[...] = ac
        cf_ref[...] = cf
        dtr_ref[...] = dt.T
        acr_ref[...] = ac.T
        cfr_ref[...] = cf.T

    col = pl.BlockSpec((blk, LANES), lambda i: (i, 0))
    row = pl.BlockSpec((LANES, blk), lambda i: (0, i))
    vec = pl.BlockSpec((1, LANES), lambda i: (0, 0))
    return pl.pallas_call(
        body, name="stats_fwd", grid=(s // blk,), in_specs=[col, vec, vec],
        out_specs=[col, col, col, row, row, row],
        out_shape=[jax.ShapeDtypeStruct((s, LANES), F32)] * 3 + [jax.ShapeDtypeStruct((LANES, s), F32)] * 3,
        scratch_shapes=[pltpu.VMEM((8, LANES), F32)],
        compiler_params=_cp("arbitrary"),
    )(fd, bias_row, a_row)


def _stats_bwd(fd, bias_row, ddt, dcum_rows):
    s = fd.shape[0]
    blk = SSM_CHUNK
    nb = s // blk

    def body(fd_ref, bias_ref, ddt_ref, dck_ref, o_ref, db_ref, carry_ref):
        @pl.when(pl.program_id(0) == 0)
        def _():
            carry_ref[...] = jnp.zeros_like(carry_ref)
            db_ref[...] = jnp.zeros_like(db_ref)

        v = fd_ref[...] + bias_ref[...]
        dcum = dck_ref[...].T
        dlf = jnp.dot(_tri(blk, lower=False), dcum, precision=HI, preferred_element_type=F32) + carry_ref[0:1, :]
        carry_ref[...] = carry_ref[...] + jnp.sum(dcum, axis=0, keepdims=True)
        lane = lax.broadcasted_iota(jnp.int32, v.shape, 1)
        g = jnp.where(lane < F_LANE0, ddt_ref[...] * _sigmoid(v), dlf * _sigmoid(-v))
        g = jnp.where(lane < F_LANE0 + ATT_HEADS, g, 0.0)
        o_ref[...] = g.astype(o_ref.dtype)
        db_ref[...] += jnp.sum(g, axis=0, keepdims=True)

    col = pl.BlockSpec((blk, LANES), lambda i: (nb - 1 - i, 0))
    row = pl.BlockSpec((LANES, blk), lambda i: (0, nb - 1 - i))
    vec = pl.BlockSpec((1, LANES), lambda i: (0, 0))
    return pl.pallas_call(
        body, name="stats_bwd", grid=(nb,), in_specs=[col, vec, col, row], out_specs=[col, vec],
        out_shape=[jax.ShapeDtypeStruct((s, LANES), BF16), jax.ShapeDtypeStruct((1, LANES), F32)],
        scratch_shapes=[pltpu.VMEM((8, LANES), F32)],
        compiler_params=_cp("arbitrary"),
    )(fd, bias_row, ddt, dcum_rows)


HP = LANES // ATT_HEAD_DIM
N_HP = ATT_HEADS // HP


def _att_blocks(s):
    return (512, 1024) if s % 1024 == 0 and s >= 4096 else (64, 128)


_QK = (((1,), (1,)), ((), ()))
_HALF = ATT_HEAD_DIM // 2
_PAD = 16


def _head_cols(a):
    return slice(a * ATT_HEAD_DIM, (a + 1) * ATT_HEAD_DIM)


def _causal(shape, off):
    r = lax.broadcasted_iota(jnp.int32, shape, 0)
    c = lax.broadcasted_iota(jnp.int32, shape, 1)
    return c <= r + off


def _attention_fwd(qkv, ck4, gather_parts):
    s = qkv.shape[0]
    bk = _att_blocks(s)[1]
    bq = bk
    nq, nk = s // bq, s // bk
    n = len(gather_parts)

    def body(q_ref, k_ref, v_ref, ck_ref, *rest):
        comm_in, (o_ref, lse_ref), comm_out, sems = rest[:n], rest[n:n + 2], rest[n + 2:2 * n + 2], rest[2 * n + 2:]
        i = pl.program_id(1)
        if n:
            @pl.when((pl.program_id(0) == 0) & (i == 0))
            def _():
                for cp in _comm_copies(comm_in, comm_out, sems, False):
                    cp.start()

        n_full = (i * bq) // bk
        qs = [(q_ref[:, _head_cols(a)].astype(F32) * ATT_SCALE).astype(BF16) for a in range(HP)]

        upper_q = lax.broadcasted_iota(jnp.int32, (bq, LANES), 1) >= ATT_HEAD_DIM

        def absorb(j, carry, keys=slice(0, bk), rows=slice(0, bq), masked=False):
            nk_ = keys.stop - keys.start
            ks = pl.ds(pl.multiple_of(j * bk, bk) + keys.start, nk_)
            v_both = v_ref[ks, :]
            out = []
            for a in range(HP):
                m, acc = carry[a]
                sc = lax.dot_general(qs[a][rows], k_ref[ks, _head_cols(a)], _QK, preferred_element_type=F32)
                sc = sc - ck_ref[0, a, pl.ds(j, 1), keys]
                if masked:
                    sc = jnp.where(_causal(sc.shape, 0), sc, NEG)
                m_new = jnp.maximum(m, jnp.max(sc, axis=1, keepdims=True))
                p = jnp.exp((sc - m_new).astype(BF16))
                upper_v = lax.broadcasted_iota(jnp.int32, (nk_, LANES), 1) >= ATT_HEAD_DIM
                v_aug = jnp.where(upper_v == (a == 1), v_both, jnp.ones_like(v_both))
                acc = jnp.exp(m - m_new) * acc + jnp.dot(p, v_aug, preferred_element_type=F32)
                out.append((m_new, acc))
            return tuple(out)

        init = tuple((jnp.full((bq, 1), NEG, F32), jnp.zeros((bq, LANES), F32)) for _ in range(HP))
        carry = lax.fori_loop(0, n_full, absorb, init)
        hq = bq // 2
        carry = absorb(n_full, carry, keys=slice(0, hq), masked=True)
        low = absorb(n_full, tuple((m[hq:], acc[hq:]) for m, acc in carry), keys=slice(hq, bk), rows=slice(hq, bq),
                     masked=True)
        carry = tuple((jnp.concatenate([m[:hq], ml], axis=0), jnp.concatenate([acc[:hq], al], axis=0))
                      for (m, acc), (ml, al) in zip(carry, low))
        outs, lses = [], []
        for a in range(HP):
            m, acc = carry[a]
            l = pltpu.roll(acc, ATT_HEAD_DIM, 1)
            outs.append(acc / l)
            lses.append(m + jnp.log(l))
        o_ref[...] = jnp.where(upper_q, outs[1], outs[0])
        lse_ref[...] = jnp.where(upper_q, lses[1], lses[0])
        if n:
            @pl.when((pl.program_id(0) == N_HP - 1) & (i == nq - 1))
            def _():
                for cp in _comm_copies(comm_in, comm_out, sems, False):
                    cp.wait()

    q_spec = pl.BlockSpec((bq, LANES), lambda h, i: (i, h))
    anyspec = pl.BlockSpec(memory_space=pl.ANY)
    res = pl.pallas_call(
        body, name="att_fwd", grid=(N_HP, nq),
        in_specs=[q_spec, pl.BlockSpec((s, LANES), lambda h, i: (0, N_HP + h)),
                  pl.BlockSpec((s, LANES), lambda h, i: (0, 2 * N_HP + h)),
                  pl.BlockSpec((1, HP, nk, bk), lambda h, i: (h, 0, 0, 0))] + [anyspec] * n,
        out_specs=[q_spec, q_spec] + [anyspec] * n,
        out_shape=[jax.ShapeDtypeStruct((s, D_MODEL), F32)] * 2 + _comm_out_shapes(gather_parts, False),
        scratch_shapes=_comm_sems(n) if n else [],
        compiler_params=_cp("arbitrary", "arbitrary"),
    )(qkv, qkv, qkv, ck4, *gather_parts)
    return res[0], res[1], list(res[2:])


def _att_prep(do, o, lse_rep):
    s = do.shape[0]
    bs = _tile(s, 512)

    def body(do_ref, o_ref, lse_ref, st_ref, dob_ref):
        r = lax.broadcasted_iota(jnp.int32, (LANES, LANES), 0) // ATT_HEAD_DIM
        c = lax.broadcasted_iota(jnp.int32, (LANES, LANES), 1) // ATT_HEAD_DIM
        e = jnp.where(r == c, 1.0, 0.0).astype(F32)
        lane = lax.broadcasted_iota(jnp.int32, (bs, LANES), 1)
        for p in range(D_MODEL // LANES):
            cs = slice(p * LANES, (p + 1) * LANES)
            dd = do_ref[:, cs]
            delta = jnp.dot(dd * o_ref[:, cs], e, precision=HI, preferred_element_type=F32)
            st_ref[:, cs] = jnp.where(lane % ATT_HEAD_DIM < _HALF, lse_ref[:, cs], delta)
            dob_ref[:, cs] = dd.astype(BF16)

    spec = pl.BlockSpec((bs, D_MODEL), lambda i: (i, 0))
    return pl.pallas_call(body, name="att_prep", grid=(s // bs,), in_specs=[spec, spec, spec], out_specs=[spec, spec],
                          out_shape=[jax.ShapeDtypeStruct((s, D_MODEL), F32), jax.ShapeDtypeStruct((s, D_MODEL), BF16)],
                          compiler_params=_cp("parallel"))(do, o, lse_rep)


def _attention_bwd(qkv, ck4, st, do_b, exchange_parts):
    s = qkv.shape[0]
    bq, bk = _att_blocks(s)
    nq, nk, per = s // bq, s // bk, bk // bq
    _T = (((0,), (0,)), ((), ()))
    n = len(exchange_parts)

    def body(q_ref, k_ref, v_ref, ck_ref, st_ref, do_ref, *rest):
        comm_in, (dq_ref, dk_ref, dv_ref, dck_ref, dcq_ref) = rest[:n], rest[n:n + 5]
        comm_out, sems, (dk_acc, dv_acc) = rest[n + 5:2 * n + 5], rest[2 * n + 5:-2], rest[-2:]
        j = pl.program_id(1)
        if n:
            @pl.when((pl.program_id(0) == 0) & (j == 0))
            def _():
                for cp in _comm_copies(comm_in, comm_out, sems, True):
                    cp.start()

        @pl.when(j == 0)
        def _():
            dq_ref[...] = jnp.zeros_like(dq_ref)
            dcq_ref[...] = jnp.zeros_like(dcq_ref)

        dk_acc[...] = jnp.zeros_like(dk_acc)
        dv_acc[...] = jnp.zeros_like(dv_acc)

        ones_q, ones_k = jnp.ones((_PAD, bq), BF16), jnp.ones((_PAD, bk), BF16)
        k_t = [jnp.concatenate([k_ref[:, _head_cols(a)].T, ones_k], axis=0) for a in range(HP)]

        def step(i, off=None, kl=slice(0, bk)):
            rows = pl.ds(pl.multiple_of(i * bq, bq), bq)
            for a in range(HP):
                cs = _head_cols(a)
                q = (q_ref[rows, cs].astype(F32) * ATT_SCALE).astype(BF16)
                do_a = do_ref[rows, cs]
                sc = lax.dot_general(q, k_ref[kl, cs], _QK, preferred_element_type=F32) - ck_ref[0, a, pl.ds(j, 1), kl]
                if off is not None:
                    sc = jnp.where(_causal(sc.shape, off), sc, NEG)
                p = jnp.exp(sc - st_ref[rows, a * ATT_HEAD_DIM:a * ATT_HEAD_DIM + 1])
                dp = lax.dot_general(do_a, v_ref[kl, cs], _QK, preferred_element_type=F32)
                ds = p * (dp - st_ref[rows, a * ATT_HEAD_DIM + _HALF:a * ATT_HEAD_DIM + _HALF + 1])
                ds_b = ds.astype(BF16)
                dv_acc[a, :, kl] += jnp.dot(do_a.T, p.astype(BF16), preferred_element_type=F32)
                dk_acc[a, :, kl] += jnp.dot(jnp.concatenate([q.T, ones_q], axis=0), ds_b, preferred_element_type=F32)
                dqs = lax.dot_general(k_t[a][:, kl], ds_b, _QK, preferred_element_type=F32)
                dq_ref[cs, rows] += dqs[:ATT_HEAD_DIM] * ATT_SCALE
                dcq_ref[0, a, pl.ds(i, 1), :] += jnp.sum(dqs[ATT_HEAD_DIM:ATT_HEAD_DIM + 8], axis=0,
                                                         keepdims=True) * 0.125

        for t in range(per):
            step(j * per + t, off=t * bq, kl=slice(0, (t + 1) * bq))

        def full(i, c):
            step(i)
            return c

        lax.fori_loop((j + 1) * per, nq, full, 0)
        for a in range(HP):
            dk_ref[:, _head_cols(a)] = dk_acc[a, :ATT_HEAD_DIM].T.astype(dk_ref.dtype)
            dv_ref[:, _head_cols(a)] = dv_acc[a].T.astype(dv_ref.dtype)
            dck_ref[0, a, pl.ds(j, 1), :] = -dk_acc[a, ATT_HEAD_DIM:ATT_HEAD_DIM + 1]
        if n:
            @pl.when((pl.program_id(0) == N_HP - 1) & (j == nk - 1))
            def _():
                for cp in _comm_copies(comm_in, comm_out, sems, True):
                    cp.wait()

    res = pl.BlockSpec((s, LANES), lambda h, j: (0, h))
    ck_spec = pl.BlockSpec((1, HP, nk, bk), lambda h, j: (h, 0, 0, 0))
    kout = pl.BlockSpec((bk, LANES), lambda h, j: (j, h))
    anyspec = pl.BlockSpec(memory_space=pl.ANY)
    outs = pl.pallas_call(
        body, name="att_bwd", grid=(N_HP, nk),
        in_specs=[res, pl.BlockSpec((bk, LANES), lambda h, j: (j, N_HP + h)),
                  pl.BlockSpec((bk, LANES), lambda h, j: (j, 2 * N_HP + h)), ck_spec, res, res] + [anyspec] * n,
        out_specs=[pl.BlockSpec((LANES, s), lambda h, j: (h, 0)), kout, kout, ck_spec,
                   pl.BlockSpec((1, HP, nq, bq), lambda h, j: (h, 0, 0, 0))] + [anyspec] * n,
        out_shape=[jax.ShapeDtypeStruct((D_MODEL, s), F32), jax.ShapeDtypeStruct((s, D_MODEL), BF16),
                   jax.ShapeDtypeStruct((s, D_MODEL), BF16), jax.ShapeDtypeStruct((N_HP, HP, nk, bk), F32),
                   jax.ShapeDtypeStruct((N_HP, HP, nq, bq), F32)] + _comm_out_shapes(exchange_parts, True),
        scratch_shapes=(_comm_sems(n) if n else [])
        + [pltpu.VMEM((HP, ATT_HEAD_DIM + _PAD, bk), F32), pltpu.VMEM((HP, ATT_HEAD_DIM, bk), F32)],
        compiler_params=_cp("arbitrary", "arbitrary"),
    )(qkv, qkv, qkv, ck4, st, do_b, *exchange_parts)
    return outs[:5], list(outs[5:])


def _silu_and_grad(x):
    sg = _sigmoid(x)
    return x * sg, sg * (1.0 + x * (1.0 - sg))


def _conv_pre(cur, halo, w_ref, b_ref, first):
    halo = jnp.where(first, 0.0, halo)
    row = lax.broadcasted_iota(jnp.int32, cur.shape, 0)
    shifted = []
    for k in range(SSM_CONV):
        sh = SSM_CONV - 1 - k
        if sh == 0:
            shifted.append(cur)
            continue
        r = pltpu.roll(cur, sh, 0)
        hr = pltpu.roll(halo, sh, 0)
        top = jnp.where(row[0:8] < sh, hr, r[0:8])
        shifted.append(jnp.concatenate([top, r[8:]], axis=0))
    pre = b_ref[...] + sum(w_ref[k:k + 1, :] * shifted[k] for k in range(SSM_CONV))
    return pre, shifted


def _conv_specs(s, bs, bc):
    cur = pl.BlockSpec((bs, bc), lambda j, i: (i, j))
    halo = pl.BlockSpec((8, bc), lambda j, i: (jnp.maximum(i * (bs // 8) - 1, 0), j))
    w = pl.BlockSpec((SSM_CONV, bc), lambda j, i: (0, j))
    b = pl.BlockSpec((1, bc), lambda j, i: (0, j))
    return cur, halo, w, b


def _conv_fwd(xbc, w, b):
    s, c = xbc.shape
    bs, bc = _tile(s, 512), 1024

    def body(x_ref, h_ref, w_ref, b_ref, o_ref):
        pre, _ = _conv_pre(x_ref[...], h_ref[...], w_ref, b_ref, pl.program_id(1) == 0)
        o_ref[...] = pre * _sigmoid(pre)

    cur, halo, ws, bsp = _conv_specs(s, bs, bc)
    return pl.pallas_call(body, name="conv_fwd", grid=(c // bc, s // bs), in_specs=[cur, halo, ws, bsp],
                          out_specs=cur, out_shape=jax.ShapeDtypeStruct((s, c), F32),
                          compiler_params=_cp("parallel", "parallel"))(xbc, xbc, w, b)


def _conv_bwd_pre(xbc, w, b, dact):
    s, c = xbc.shape
    bs, bc = _tile(s, 512), 1024

    def body(x_ref, h_ref, w_ref, b_ref, g_ref, dp_ref, dw_ref, db_ref):
        @pl.when(pl.program_id(1) == 0)
        def _():
            dw_ref[...] = jnp.zeros_like(dw_ref)
            db_ref[...] = jnp.zeros_like(db_ref)

        pre, shifted = _conv_pre(x_ref[...], h_ref[...], w_ref, b_ref, pl.program_id(1) == 0)
        dpre = g_ref[...] * _silu_and_grad(pre)[1]
        dp_ref[...] = dpre
        db_ref[...] += jnp.sum(dpre, axis=0, keepdims=True)
        for k in range(SSM_CONV):
            dw_ref[k:k + 1, :] += jnp.sum(dpre * shifted[k], axis=0, keepdims=True)

    cur, halo, ws, bsp = _conv_specs(s, bs, bc)
    return pl.pallas_call(
        body, name="conv_bwd_pre", grid=(c // bc, s // bs), in_specs=[cur, halo, ws, bsp, cur],
        out_specs=[cur, ws, bsp],
        out_shape=[jax.ShapeDtypeStruct((s, c), F32), jax.ShapeDtypeStruct((SSM_CONV, c), F32),
                   jax.ShapeDtypeStruct((1, c), F32)],
        compiler_params=_cp("parallel", "arbitrary"))(xbc, xbc, w, b, dact)


def _conv_bwd_in(dpre, w):
    s, c = dpre.shape
    bs, bc = _tile(s, 512), 1024
    nb = s // bs

    def body(g_ref, n_ref, w_ref, o_ref):
        cur = g_ref[...]
        nxt = jnp.where(pl.program_id(1) == nb - 1, 0.0, n_ref[...])
        row = lax.broadcasted_iota(jnp.int32, cur.shape, 0)
        acc = w_ref[SSM_CONV - 1:SSM_CONV, :] * cur
        for sh in range(1, SSM_CONV):
            r = pltpu.roll(cur, bs - sh, 0)
            nr = pltpu.roll(nxt, 8 - sh, 0)
            bot = jnp.where(row[0:8] >= 8 - sh, nr, r[bs - 8:])
            acc = acc + w_ref[SSM_CONV - 1 - sh:SSM_CONV - sh, :] * jnp.concatenate([r[:bs - 8], bot], axis=0)
        o_ref[...] = acc.astype(o_ref.dtype)

    cur = pl.BlockSpec((bs, bc), lambda j, i: (i, j))
    nxt = pl.BlockSpec((8, bc), lambda j, i: (jnp.minimum((i + 1) * (bs // 8), s // 8 - 1), j))
    ws = pl.BlockSpec((SSM_CONV, bc), lambda j, i: (0, j))
    return pl.pallas_call(body, name="conv_bwd_in", grid=(c // bc, nb), in_specs=[cur, nxt, ws], out_specs=cur,
                          out_shape=jax.ShapeDtypeStruct((s, c), BF16),
                          compiler_params=_cp("parallel", "parallel"))(dpre, dpre, w)


def _dotT(a, b):
    return lax.dot_general(a.astype(BF16), b.astype(BF16), (((1,), (1,)), ((), ())), preferred_element_type=F32)


def _dot(a, b):
    return jnp.dot(a.astype(BF16), b.astype(BF16), preferred_element_type=F32)


N_PAIR = SSM_HEADS // HP
PAIRS_PER_GROUP = SSM_HEADS_PER_GROUP // HP


def _pair_consts():
    L = SSM_CHUNK
    lane = lax.broadcasted_iota(jnp.int32, (L, LANES), 1)
    lane1 = lax.broadcasted_iota(jnp.int32, (1, LANES), 1)
    li = lax.broadcasted_iota(jnp.int32, (L, L), 0)
    si = lax.broadcasted_iota(jnp.int32, (L, L), 1)
    return lane >= ATT_HEAD_DIM, lane1 >= ATT_HEAD_DIM, li, si


def _ssd_pair_fwd(xbc_act, ac_c, dt_r, ac_r, dsk_pair):
    s = xbc_act.shape[0]
    L, N, G = SSM_CHUNK, SSM_STATE, SSM_GROUPS
    nc = s // L

    def body(xbc_ref, ac_ref, dtr_ref, acr_ref, dsk_ref, y_ref, hp_ref, st_ref):
        @pl.when(pl.program_id(0) == 0)
        def _():
            st_ref[...] = jnp.zeros_like(st_ref)

        upper, up1, li, si = _pair_consts()
        for g in range(G):
            b_g = xbc_ref[:, SSM_INNER + g * N:SSM_INNER + (g + 1) * N]
            c_g = xbc_ref[:, SSM_INNER + G * N + g * N:SSM_INNER + G * N + (g + 1) * N]
            cb = _dotT(c_g, b_g)
            b_t = b_g.T
            for q in range(PAIRS_PER_GROUP):
                pp = g * PAIRS_PER_GROUP + q
                cols = slice(pp * LANES, (pp + 1) * LANES)
                xs = xbc_ref[:, cols]
                ht = st_ref[pp]
                hp_ref[0, pp] = ht
                y = dsk_ref[pp:pp + 1, :] * xs
                s_new = jnp.zeros((N, LANES), F32)
                ea, el = [], []
                for a in range(HP):
                    h = HP * pp + a
                    acol = jnp.broadcast_to(ac_ref[:, h:h + 1], (L, LANES))
                    arow, dtrow = acr_ref[h:h + 1, :], dtr_ref[h:h + 1, :]
                    alast = ac_ref[L - 1:L, h:h + 1]
                    decay = jnp.exp(jnp.where(li >= si, acol - arow, NEG))
                    xs_a = jnp.where(upper == (a == 1), xs, 0.0)
                    y = y + _dot(cb * decay * dtrow, xs_a)
                    s_new = s_new + _dot(b_t * (dtrow * jnp.exp(alast - arow)), xs_a)
                    ea.append(jnp.exp(acol))
                    el.append(jnp.exp(alast))
                y_ref[:, cols] = y + jnp.where(upper, ea[1], ea[0]) * _dot(c_g, ht)
                st_ref[pp] = ht * jnp.where(up1, el[1], el[0]) + s_new

    col = pl.BlockSpec((L, LANES), lambda c: (c, 0))
    row = pl.BlockSpec((LANES, L), lambda c: (0, c))
    return pl.pallas_call(
        body, name="ssd_fwd", grid=(nc,),
        in_specs=[pl.BlockSpec((L, SSM_CONV_DIM), lambda c: (c, 0)), col, row, row,
                  pl.BlockSpec((N_PAIR, LANES), lambda c: (0, 0))],
        out_specs=[pl.BlockSpec((L, SSM_INNER), lambda c: (c, 0)),
                   pl.BlockSpec((1, N_PAIR, N, LANES), lambda c: (c, 0, 0, 0))],
        out_shape=[jax.ShapeDtypeStruct((s, SSM_INNER), F32), jax.ShapeDtypeStruct((nc, N_PAIR, N, LANES), F32)],
        scratch_shapes=[pltpu.VMEM((N_PAIR, N, LANES), F32)],
        compiler_params=_cp("arbitrary"),
    )(xbc_act, ac_c, dt_r, ac_r, dsk_pair)


def _ssd_pair_bwd(xbc_act, dt_c, ac_c, dt_r, ac_r, hprev_all, dy, dsk_pair, a_row):
    s = xbc_act.shape[0]
    L, N, G = SSM_CHUNK, SSM_STATE, SSM_GROUPS
    nc = s // L
    rev = lambda c: nc - 1 - c

    def body(xbc_ref, dt_ref, ac_ref, dtr_ref, acr_ref, hp_ref, dy_ref, dsk_ref, arow_ref,
             dx_ref, ddt_ref, da_ref, dds_ref, dh_ref):
        @pl.when(pl.program_id(0) == 0)
        def _():
            dh_ref[...] = jnp.zeros_like(dh_ref)
            da_ref[...] = jnp.zeros_like(da_ref)
            dds_ref[...] = jnp.zeros_like(dds_ref)

        upper, up1, li, si = _pair_consts()
        lane = lax.broadcasted_iota(jnp.int32, (L, LANES), 1)
        sub = lax.broadcasted_iota(jnp.int32, (LANES, L), 0)
        lastrow = lax.broadcasted_iota(jnp.int32, (L, LANES), 0) == L - 1
        da_c = jnp.zeros((L, LANES), F32)
        da_r = jnp.zeros((LANES, L), F32)
        ddt_r = jnp.zeros((LANES, L), F32)
        for g in range(G):
            b_g = xbc_ref[:, SSM_INNER + g * N:SSM_INNER + (g + 1) * N]
            c_g = xbc_ref[:, SSM_INNER + G * N + g * N:SSM_INNER + G * N + (g + 1) * N]
            cb, cb_t = _dotT(c_g, b_g), _dotT(b_g, c_g)
            b_t, c_t = b_g.T, c_g.T
            dcb = jnp.zeros((L, L), F32)
            db_t = jnp.zeros((N, L), F32)
            dc = jnp.zeros((L, N), F32)
            for q in range(PAIRS_PER_GROUP):
                pp = g * PAIRS_PER_GROUP + q
                cols = slice(pp * LANES, (pp + 1) * LANES)
                xs, gy = xbc_ref[:, cols], dy_ref[:, cols]
                ht, dhn = hp_ref[0, pp], dh_ref[pp]
                acol = [jnp.broadcast_to(ac_ref[:, HP * pp + a:HP * pp + a + 1], (L, LANES)) for a in range(HP)]
                alast = [ac_ref[L - 1:L, HP * pp + a:HP * pp + a + 1] for a in range(HP)]
                ea = jnp.where(upper, jnp.exp(acol[1]), jnp.exp(acol[0]))
                el = jnp.where(up1, jnp.exp(alast[1]), jnp.exp(alast[0]))
                ge = gy * ea
                dc = dc + _dotT(ge, ht)
                dh_ref[pp] = _dot(c_t, ge) + dhn * el
                t_off = (ge * _dot(c_g, ht)).astype(BF16)
                hsum = jnp.sum(dhn * ht, axis=0, keepdims=True)
                dxs = dsk_ref[pp:pp + 1, :] * gy
                dds_ref[pp:pp + 1, :] += jnp.sum(gy * xs, axis=0, keepdims=True)
                for a in range(HP):
                    h = HP * pp + a
                    mine, mine1 = upper == (a == 1), up1 == (a == 1)
                    arow, dtrow = acr_ref[h:h + 1, :], dtr_ref[h:h + 1, :]
                    dtcol = jnp.broadcast_to(dt_ref[:, h:h + 1], (L, LANES))
                    xs_a, gy_a = jnp.where(mine, xs, 0.0), jnp.where(mine, gy, 0.0)
                    dhn_a = jnp.where(mine1, dhn, 0.0)
                    e_row = jnp.exp(alast[a] - arow)
                    w_row = dtrow * e_row
                    xd_t = _dotT(dhn_a, xs_a)
                    db_t = db_t + xd_t * w_row
                    dw = jnp.sum(b_t * xd_t, axis=0, keepdims=True)
                    de_e = dw * w_row
                    dal = (jnp.sum(jnp.where(mine1, hsum, 0.0), axis=1, keepdims=True) * jnp.exp(alast[a])
                           + jnp.sum(de_e, axis=1, keepdims=True))
                    dxs = dxs + _dot(b_g, dhn_a) * (dtcol * jnp.exp(alast[a] - acol[a]))
                    decay = jnp.exp(jnp.where(li >= si, acol[a] - arow, NEG))
                    decay_t = jnp.exp(jnp.where(si >= li, arow - acol[a], NEG))
                    m = cb * decay
                    dmdt = _dotT(gy_a, xs_a)
                    dxs = dxs + _dot(cb_t * decay_t * dtcol, gy_a)
                    dm = dmdt * dtrow
                    dcb = dcb + dm * decay
                    wb = (dm * m).astype(BF16)
                    onehot = jnp.where(lane == h, 1.0, 0.0).astype(BF16)
                    da_c = (da_c + jnp.dot(wb, onehot, preferred_element_type=F32)
                            + jnp.dot(jnp.where(mine, t_off, 0.0).astype(BF16), onehot, preferred_element_type=F32)
                            + jnp.where(lastrow & (lane == h), dal, 0.0))
                    da_r = jnp.where(sub == h, -(jnp.sum(wb.astype(F32), axis=0, keepdims=True) + de_e), da_r)
                    ddt_r = jnp.where(sub == h, dw * e_row + jnp.sum(dmdt * m, axis=0, keepdims=True), ddt_r)
                dx_ref[:, cols] = dxs
            dx_ref[:, SSM_INNER + g * N:SSM_INNER + (g + 1) * N] = (db_t + _dot(c_t, dcb)).T
            dx_ref[:, SSM_INNER + G * N + g * N:SSM_INNER + G * N + (g + 1) * N] = dc + _dot(dcb, b_g)
        dda = jnp.dot(_tri(L, lower=False), da_c + da_r.T, precision=HI, preferred_element_type=F32)
        ddt_ref[...] = dda * arow_ref[...] + ddt_r.T
        da_ref[...] += jnp.sum(dda * dt_ref[...], axis=0, keepdims=True)

    col = pl.BlockSpec((L, LANES), lambda c: (rev(c), 0))
    row = pl.BlockSpec((LANES, L), lambda c: (0, rev(c)))
    vec = pl.BlockSpec((1, LANES), lambda c: (0, 0))
    pairs = pl.BlockSpec((N_PAIR, LANES), lambda c: (0, 0))
    return pl.pallas_call(
        body, name="ssd_bwd", grid=(nc,),
        in_specs=[pl.BlockSpec((L, SSM_CONV_DIM), lambda c: (rev(c), 0)), col, col, row, row,
                  pl.BlockSpec((1, N_PAIR, N, LANES), lambda c: (rev(c), 0, 0, 0)),
                  pl.BlockSpec((L, SSM_INNER), lambda c: (rev(c), 0)), pairs, vec],
        out_specs=[pl.BlockSpec((L, SSM_CONV_DIM), lambda c: (rev(c), 0)), col, vec, pairs],
        out_shape=[jax.ShapeDtypeStruct((s, SSM_CONV_DIM), F32), jax.ShapeDtypeStruct((s, LANES), F32),
                   jax.ShapeDtypeStruct((1, LANES), F32), jax.ShapeDtypeStruct((N_PAIR, LANES), F32)],
        scratch_shapes=[pltpu.VMEM((N_PAIR, N, LANES), F32)],
        compiler_params=_cp("arbitrary"),
    )(xbc_act, dt_c, ac_c, dt_r, ac_r, hprev_all, dy, dsk_pair, a_row)


ROWS = 512
GW = SSM_INNER // SSM_GROUPS


def _rows(width, dtype=F32):
    return pl.BlockSpec((ROWS, width), lambda i: (i, 0))


def _vec(width):
    return pl.BlockSpec((1, width), lambda i: (0, 0))


def _gnorm_fwd(y, z, w):
    s = y.shape[0]

    def body(y_ref, z_ref, w_ref, o_ref):
        for g in range(SSM_GROUPS):
            cs = slice(g * GW, (g + 1) * GW)
            zz = z_ref[:, cs].astype(F32)
            u = y_ref[:, cs] * (zz * _sigmoid(zz))
            r = lax.rsqrt(jnp.mean(u * u, axis=1, keepdims=True) + RMS_EPS)
            o_ref[:, cs] = (u * r * w_ref[:, cs]).astype(o_ref.dtype)

    return pl.pallas_call(body, name="gnorm_fwd", grid=(s // ROWS,),
                          in_specs=[_rows(SSM_INNER), _rows(SSM_INNER), _vec(SSM_INNER)], out_specs=_rows(SSM_INNER),
                          out_shape=jax.ShapeDtypeStruct((s, SSM_INNER), BF16), compiler_params=_cp("parallel"))(y, z, w)


def _gnorm_bwd(y, z, w, do):
    s = y.shape[0]

    def body(y_ref, z_ref, w_ref, do_ref, dy_ref, dz_ref, dw_ref):
        @pl.when(pl.program_id(0) == 0)
        def _():
            dw_ref[...] = jnp.zeros_like(dw_ref)

        for g in range(SSM_GROUPS):
            cs = slice(g * GW, (g + 1) * GW)
            zz, yy, dd = z_ref[:, cs].astype(F32), y_ref[:, cs], do_ref[:, cs].astype(F32)
            sz, dsz = _silu_and_grad(zz)
            u = yy * sz
            r = lax.rsqrt(jnp.mean(u * u, axis=1, keepdims=True) + RMS_EPS)
            n = u * r
            dn = dd * w_ref[:, cs]
            dw_ref[:, cs] += jnp.sum(dd * n, axis=0, keepdims=True)
            du = r * (dn - n * jnp.mean(dn * n, axis=1, keepdims=True))
            dy_ref[:, cs] = du * sz
            dz_ref[:, cs] = (du * yy * dsz).astype(dz_ref.dtype)

    return pl.pallas_call(
        body, name="gnorm_bwd", grid=(s // ROWS,),
        in_specs=[_rows(SSM_INNER), _rows(SSM_INNER), _vec(SSM_INNER), _rows(SSM_INNER)],
        out_specs=[_rows(SSM_INNER), _rows(SSM_INNER), _vec(SSM_INNER)],
        out_shape=[jax.ShapeDtypeStruct((s, SSM_INNER), F32), jax.ShapeDtypeStruct((s, SSM_INNER), BF16),
                   jax.ShapeDtypeStruct((1, SSM_INNER), F32)],
        compiler_params=_cp("arbitrary"))(y, z, w, do)


def _mix_fwd(gl, bg, attn_d, ssm_d):
    s = gl.shape[0]
    d = D_MODEL

    def body(gl_ref, bg_ref, a_ref, m_ref, o_ref):
        g0 = _sigmoid(gl_ref[:, :d] + bg_ref[:, :d])
        g1 = _sigmoid(gl_ref[:, d:] + bg_ref[:, d:])
        o_ref[...] = (g0 * a_ref[...] + g1 * m_ref[...]).astype(o_ref.dtype)

    return pl.pallas_call(body, name="mix_fwd", grid=(s // ROWS,),
                          in_specs=[_rows(2 * d), _vec(2 * d), _rows(d), _rows(d)], out_specs=_rows(d),
                          out_shape=jax.ShapeDtypeStruct((s, d), BF16), compiler_params=_cp("parallel"))(
        gl, bg, attn_d, ssm_d)


def _mix_bwd(gl, bg, attn_d, ssm_d, dmix):
    s = gl.shape[0]
    d = D_MODEL

    def body(gl_ref, bg_ref, a_ref, m_ref, dm_ref, da_ref, ds_ref, dg_ref, db_ref):
        @pl.when(pl.program_id(0) == 0)
        def _():
            db_ref[...] = jnp.zeros_like(db_ref)

        g0 = _sigmoid(gl_ref[:, :d] + bg_ref[:, :d])
        g1 = _sigmoid(gl_ref[:, d:] + bg_ref[:, d:])
        dm = dm_ref[...].astype(F32)
        da_ref[...] = (dm * g0).astype(da_ref.dtype)
        ds_ref[...] = (dm * g1).astype(ds_ref.dtype)
        dl0 = dm * a_ref[...] * g0 * (1.0 - g0)
        dl1 = dm * m_ref[...] * g1 * (1.0 - g1)
        dg_ref[:, :d] = dl0.astype(dg_ref.dtype)
        dg_ref[:, d:] = dl1.astype(dg_ref.dtype)
        db_ref[:, :d] += jnp.sum(dl0, axis=0, keepdims=True)
        db_ref[:, d:] += jnp.sum(dl1, axis=0, keepdims=True)

    return pl.pallas_call(
        body, name="mix_bwd", grid=(s // ROWS,),
        in_specs=[_rows(2 * d), _vec(2 * d), _rows(d), _rows(d), _rows(d)],
        out_specs=[_rows(d), _rows(d), _rows(2 * d), _vec(2 * d)],
        out_shape=[jax.ShapeDtypeStruct((s, d), BF16), jax.ShapeDtypeStruct((s, d), BF16),
                   jax.ShapeDtypeStruct((s, 2 * d), BF16), jax.ShapeDtypeStruct((1, 2 * d), F32)],
        compiler_params=_cp("arbitrary"))(gl, bg, attn_d, ssm_d, dmix)


def _ln_stats(p):
    mu = jnp.mean(p, axis=1, keepdims=True)
    c = p - mu
    rstd = lax.rsqrt(jnp.mean(c * c, axis=1, keepdims=True) + LN_EPS)
    return c * rstd, rstd


def _ln_bwd(dy, xhat, rstd, g):
    dxh = dy * g
    return rstd * (dxh - jnp.mean(dxh, axis=1, keepdims=True) - xhat * jnp.mean(dxh * xhat, axis=1, keepdims=True))


def _ln1_fwd(x, mixed, g, b):
    s, d = x.shape

    def body(x_ref, m_ref, g_ref, b_ref, o_ref, ob_ref):
        xhat, _ = _ln_stats(DEEPNORM_ALPHA * x_ref[...] + m_ref[...])
        y = xhat * g_ref[...] + b_ref[...]
        o_ref[...] = y
        ob_ref[...] = y.astype(BF16)

    return pl.pallas_call(body, name="ln1_fwd", grid=(s // ROWS,), in_specs=[_rows(d), _rows(d), _vec(d), _vec(d)],
                          out_specs=[_rows(d), _rows(d)],
                          out_shape=[jax.ShapeDtypeStruct((s, d), F32), jax.ShapeDtypeStruct((s, d), BF16)],
                          compiler_params=_cp("parallel"))(x, mixed, g, b)


def _ln2_loss(x1, h, target, g, b):
    s, d = x1.shape

    def body(x_ref, h_ref, t_ref, g_ref, b_ref, dp_ref, dpb_ref, loss_ref, dg_ref, db_ref):
        @pl.when(pl.program_id(0) == 0)
        def _():
            loss_ref[...] = jnp.zeros_like(loss_ref)
            dg_ref[...] = jnp.zeros_like(dg_ref)
            db_ref[...] = jnp.zeros_like(db_ref)

        xhat, rstd = _ln_stats(DEEPNORM_ALPHA * x_ref[...] + h_ref[...])
        err = xhat * g_ref[...] + b_ref[...] - t_ref[...]
        part = 0.5 * jnp.sum(jnp.mean(err * err, axis=1, keepdims=True), axis=0, keepdims=True)
        loss_ref[...] += jnp.broadcast_to(part, loss_ref.shape)
        dy = err * (1.0 / d)
        dg_ref[...] += jnp.sum(dy * xhat, axis=0, keepdims=True)
        db_ref[...] += jnp.sum(dy, axis=0, keepdims=True)
        dp = _ln_bwd(dy, xhat, rstd, g_ref[...])
        dp_ref[...] = dp
        dpb_ref[...] = dp.astype(BF16)

    return pl.pallas_call(
        body, name="ln2_loss", grid=(s // ROWS,), in_specs=[_rows(d), _rows(d), _rows(d), _vec(d), _vec(d)],
        out_specs=[_rows(d), _rows(d), _vec(LANES), _vec(d), _vec(d)],
        out_shape=[jax.ShapeDtypeStruct((s, d), F32), jax.ShapeDtypeStruct((s, d), BF16),
                   jax.ShapeDtypeStruct((1, LANES), F32),
                   jax.ShapeDtypeStruct((1, d), F32), jax.ShapeDtypeStruct((1, d), F32)],
        compiler_params=_cp("arbitrary"))(x1, h, target, g, b)


def _ln1_bwd(x, mixed, g, dpre2, dffn):
    s, d = x.shape

    def body(x_ref, m_ref, g_ref, d2_ref, df_ref, dp_ref, dr_ref, dg_ref, db_ref):
        @pl.when(pl.program_id(0) == 0)
        def _():
            dg_ref[...] = jnp.zeros_like(dg_ref)
            db_ref[...] = jnp.zeros_like(db_ref)

        xhat, rstd = _ln_stats(DEEPNORM_ALPHA * x_ref[...] + m_ref[...])
        dy = DEEPNORM_ALPHA * d2_ref[...] + df_ref[...]
        dg_ref[...] += jnp.sum(dy * xhat, axis=0, keepdims=True)
        db_ref[...] += jnp.sum(dy, axis=0, keepdims=True)
        dp = _ln_bwd(dy, xhat, rstd, g_ref[...])
        dp_ref[...] = dp.astype(BF16)
        dr_ref[...] = DEEPNORM_ALPHA * dp

    return pl.pallas_call(
        body, name="ln1_bwd", grid=(s // ROWS,), in_specs=[_rows(d), _rows(d), _vec(d), _rows(d), _rows(d)],
        out_specs=[_rows(d), _rows(d), _vec(d), _vec(d)],
        out_shape=[jax.ShapeDtypeStruct((s, d), BF16), jax.ShapeDtypeStruct((s, d), F32),
                   jax.ShapeDtypeStruct((1, d), F32), jax.ShapeDtypeStruct((1, d), F32)],
        compiler_params=_cp("arbitrary"))(x, mixed, g, dpre2, dffn)


def _swiglu_fwd(gu):
    s = gu.shape[0]
    f = FFN_HIDDEN

    def body(g_ref, u_ref, o_ref):
        gg = g_ref[...].astype(F32)
        o_ref[...] = (gg * _sigmoid(gg) * u_ref[...].astype(F32)).astype(o_ref.dtype)

    return pl.pallas_call(
        body, name="swiglu_fwd", grid=(s // ROWS,),
        in_specs=[pl.BlockSpec((ROWS, f), lambda i: (i, 0)), pl.BlockSpec((ROWS, f), lambda i: (i, 1))],
        out_specs=_rows(f), out_shape=jax.ShapeDtypeStruct((s, f), BF16), compiler_params=_cp("parallel"))(gu, gu)


def _swiglu_bwd(gu, dact):
    s = gu.shape[0]
    f = FFN_HIDDEN

    def body(g_ref, u_ref, d_ref, o_ref):
        sg, dsg = _silu_and_grad(g_ref[...].astype(F32))
        dd = d_ref[...].astype(F32)
        o_ref[:, :f] = (dd * u_ref[...].astype(F32) * dsg).astype(o_ref.dtype)
        o_ref[:, f:] = (dd * sg).astype(o_ref.dtype)

    return pl.pallas_call(
        body, name="swiglu_bwd", grid=(s // ROWS,),
        in_specs=[pl.BlockSpec((ROWS, f), lambda i: (i, 0)), pl.BlockSpec((ROWS, f), lambda i: (i, 1)), _rows(f)],
        out_specs=_rows(2 * f), out_shape=jax.ShapeDtypeStruct((s, 2 * f), BF16),
        compiler_params=_cp("parallel"))(gu, gu, dact)


def _peer(k):
    x, y, c = lax.axis_index("x"), lax.axis_index("y"), lax.axis_index("c")
    kx, ky, kc = (k >> 2) & 1, (k >> 1) & 1, k & 1
    px = (1 - x) if kx else x
    py = (1 - y) if ky else y
    pc = (1 - c) if kc else c
    return (px, py, pc), 4 * px + 2 * py + pc


def _my_index():
    return 4 * lax.axis_index("x") + 2 * lax.axis_index("y") + lax.axis_index("c")


def _comm_copies(ins, outs, sems, scatter):
    send_sems, recv_sems, local_sems = sems
    me = _my_index()
    copies = [pltpu.make_async_copy(ins[t].at[me] if scatter else ins[t], outs[t].at[me], local_sems.at[t])
              for t in range(len(ins))]
    for k in range(1, N_DEV):
        peer, pidx = _peer(k)
        for t in range(len(ins)):
            copies.append(pltpu.make_async_remote_copy(
                src_ref=ins[t].at[pidx] if scatter else ins[t], dst_ref=outs[t].at[me],
                send_sem=send_sems.at[t, k - 1], recv_sem=recv_sems.at[t, k - 1], device_id=peer,
                device_id_type=pl.DeviceIdType.MESH))
    return copies


def _comm_sems(n):
    return [pltpu.SemaphoreType.DMA((n, N_DEV - 1)), pltpu.SemaphoreType.DMA((n, N_DEV - 1)),
            pltpu.SemaphoreType.DMA((n,))]


def _comm_out_shapes(parts, scatter):
    return [jax.ShapeDtypeStruct(p.shape if scatter else (N_DEV,) + p.shape, p.dtype) for p in parts]


def _all_gather(parts):
    n = len(parts)

    def body(*refs):
        ins, outs = refs[:n], refs[n:2 * n]
        send_sems, recv_sems, local_sems = refs[2 * n:]
        x, y, c = lax.axis_index("x"), lax.axis_index("y"), lax.axis_index("c")
        me, sibling = (x, y, c), (x, y, 1 - c)
        chips = [(1 - x, y), (x, 1 - y), (1 - x, 1 - y)]

        def copy(t, k, block, to, src=None):
            dst = outs[t].at[4 * block[0] + 2 * block[1] + block[2]]
            return pltpu.make_async_remote_copy(
                src_ref=dst if src is None else src, dst_ref=dst, send_sem=send_sems.at[t, k],
                recv_sem=recv_sems.at[t, k], device_id=to, device_id_type=pl.DeviceIdType.MESH)

        mine = [pltpu.make_async_copy(ins[t], outs[t].at[_my_index()], local_sems.at[t]) for t in range(n)]
        for cp in mine:
            cp.start()
        first = [copy(t, 0, me, sibling, src=ins[t]) for t in range(n)]
        first += [copy(t, 1 + j, me, (*chip, c), src=ins[t]) for j, chip in enumerate(chips) for t in range(n)]
        for cp in first:
            cp.start()
        passed = []
        for j, chip in enumerate(chips):
            for t in range(n):
                copy(t, 1 + j, (*chip, c), me).wait_recv()
                passed.append(copy(t, 4 + j, (*chip, c), sibling))
                passed[-1].start()
        for t in range(n):
            copy(t, 0, sibling, me).wait_recv()
            for j, chip in enumerate(chips):
                copy(t, 4 + j, (*chip, 1 - c), me).wait_recv()
        for cp in first + passed:
            cp.wait_send()
        for cp in mine:
            cp.wait()

    anyspec = pl.BlockSpec(memory_space=pl.ANY)
    return pl.pallas_call(body, name="all_gather", in_specs=[anyspec] * n, out_specs=[anyspec] * n,
                          out_shape=_comm_out_shapes(parts, False), scratch_shapes=_comm_sems(n))(*parts)


def _remote_scatter_copies(ins, lands, send_sems, recv_sems):
    me = _my_index()
    copies = []
    for k in range(1, N_DEV):
        peer, pidx = _peer(k)
        for t in range(len(ins)):
            copies.append(pltpu.make_async_remote_copy(
                src_ref=ins[t].at[pidx], dst_ref=lands[t].at[me], send_sem=send_sems.at[t * (N_DEV - 1) + k - 1],
                recv_sem=recv_sems.at[t * (N_DEV - 1) + k - 1], device_id=peer, device_id_type=pl.DeviceIdType.MESH))
    return copies


def _landing_zones(parts):
    me = _my_index()
    return [jnp.where(lax.broadcasted_iota(jnp.int32, p.shape, 0) == me, p, jnp.zeros_like(p)) for p in parts]


_HBM = pl.BlockSpec(memory_space=pltpu.HBM)
_SEM = pl.BlockSpec(memory_space=pltpu.SEMAPHORE)


def _exchange_start(parts, lands):
    n = len(parts)

    def body(*refs):
        ins, lnd, send_sems, recv_sems, token = refs[:n], refs[n:2 * n], refs[2 * n], refs[2 * n + 1], refs[-1]
        for cp in _remote_scatter_copies(ins, lnd, send_sems, recv_sems):
            cp.start()
        token[...] = jnp.zeros_like(token)

    hbm = [pltpu.HBM(p.shape, p.dtype) for p in parts]
    outs = pl.pallas_call(
        body, name="exchange_start",
        out_shape=[pltpu.SemaphoreType.DMA((n * (N_DEV - 1),)), pltpu.SemaphoreType.DMA((n * (N_DEV - 1),))] + hbm + hbm
        + [jax.ShapeDtypeStruct((8, LANES), F32)],
        in_specs=[_HBM] * (2 * n), out_specs=[_SEM, _SEM] + [_HBM] * (2 * n) + [pl.BlockSpec(memory_space=pltpu.VMEM)],
        input_output_aliases={t: 2 + t for t in range(2 * n)},
        compiler_params=pltpu.CompilerParams(has_side_effects=pltpu.SideEffectType.DATAFLOW_SIDE_EFFECTING),
    )(*[pltpu.with_memory_space_constraint(p, pltpu.HBM) for p in list(parts) + list(lands)])
    return outs[0], outs[1], list(outs[2:2 + n]), list(outs[2 + n:2 + 2 * n]), outs[-1]


def _exchange_wait(send_sems, recv_sems, parts, lands, after):
    n = len(parts)

    def body(*refs):
        ins, lnd, send_sems, recv_sems = refs[:n], refs[n:2 * n], refs[2 * n], refs[2 * n + 1]
        for cp in _remote_scatter_copies(ins, lnd, send_sems, recv_sems):
            cp.wait_send()
            cp.wait_recv()

    hbm = [pltpu.HBM(p.shape, p.dtype) for p in parts]
    outs = pl.pallas_call(
        body, name="exchange_wait", out_shape=hbm + hbm,
        in_specs=[_HBM] * (2 * n) + [_SEM, _SEM, pl.BlockSpec(memory_space=pl.ANY)], out_specs=[_HBM] * (2 * n),
        input_output_aliases={t: t for t in range(2 * n)},
        compiler_params=pltpu.CompilerParams(has_side_effects=pltpu.SideEffectType.DATAFLOW_SIDE_EFFECTING),
    )(*parts, *lands, send_sems, recv_sems, after)
    return list(outs[n:])


def _adamw(recv, w, m, v, name):
    _, r, c = w.shape
    br = _tile(r, 128)
    c1 = 1.0 / (1.0 - ADAM_B1 ** ADAM_STEP)
    c2 = 1.0 / (1.0 - ADAM_B2 ** ADAM_STEP)

    def body(r_ref, w_ref, m_ref, v_ref, g_ref, d_ref, mo_ref, vo_ref):
        g = r_ref[0].astype(F32)
        for k in range(1, N_DEV):
            g = g + r_ref[k].astype(F32)
        mn = ADAM_B1 * m_ref[0] + (1.0 - ADAM_B1) * g
        vn = ADAM_B2 * v_ref[0] + (1.0 - ADAM_B2) * (g * g)
        g_ref[0] = g
        mo_ref[0] = mn
        vo_ref[0] = vn
        d_ref[0] = -ADAM_LR * ((mn * c1) / (jnp.sqrt(vn * c2) + ADAM_EPS) + ADAM_WD * w_ref[0])

    blk = pl.BlockSpec((1, br, c), lambda i: (0, i, 0))
    return pl.pallas_call(
        body, name=name, grid=(r // br,),
        in_specs=[pl.BlockSpec((N_DEV, br, c), lambda i: (0, i, 0)), blk, blk, blk],
        out_specs=[blk] * 4, out_shape=[jax.ShapeDtypeStruct((1, r, c), F32)] * 4,
        compiler_params=_cp("parallel"))(recv, w, m, v)


def _lane_row(pairs):
    row = jnp.zeros((LANES,), F32)
    for lane0, vec in pairs:
        row = lax.dynamic_update_slice(row, vec.astype(F32), (lane0,))
    return row.reshape(1, LANES)


def _stage_in(x, wts, small):
    s = x.shape[0]
    a = -jnp.exp(small["a_log"])
    bias_row = _lane_row([(DT_LANE0, small["dt_bias"]), (F_LANE0, small["b_forget"])])
    a_row = _lane_row([(DT_LANE0, a)])
    conv_b = small["conv_b"].reshape(1, -1)
    norm_w = small["ssm_norm_w"].reshape(1, -1)
    bg = small["b_gates"].reshape(1, -1)
    g1, b1 = small["ln1_g"].reshape(1, -1), small["ln1_b"].reshape(1, -1)
    g2, b2 = small["ln2_g"].reshape(1, -1), small["ln2_b"].reshape(1, -1)
    d_skip = small["d_skip"]
    xb = x.astype(BF16)

    qkv = _mm(xb, wts["qkv"], out_dtype=BF16, name="f_qkv")
    z = _mm(xb, wts["z"], out_dtype=BF16, name="f_z")
    xbc = _mm(xb, wts["xbc"], name="f_xbc")
    gl = _mm(xb, wts["gate"], out_dtype=BF16, name="f_gate")
    fd = _mm(xb, wts["fd"], name="f_fd")
    dt_c, ac_c, cf_c, dt_r, ac_r, cf_r = _stats_fwd(fd, bias_row, a_row)
    bk = _att_blocks(s)[1]
    ck4 = cf_r[F_LANE0:F_LANE0 + ATT_HEADS].reshape(N_HP, HP, s // bk, bk)
    return dict(locals())


def _stage_mid(c, attn, lse, wts, target):
    x, xb, qkv, z, xbc, gl, fd, ck4 = (c[k] for k in ("x", "xb", "qkv", "z", "xbc", "gl", "fd", "ck4"))
    dt_c, ac_c, dt_r, ac_r, a_row, bias_row = (c[k] for k in ("dt_c", "ac_c", "dt_r", "ac_r", "a_row", "bias_row"))
    conv_b, norm_w, bg, g1, b1, g2, b2, d_skip = (c[k] for k in ("conv_b", "norm_w", "bg", "g1", "b1", "g2", "b2",
                                                                "d_skip"))
    conv_w = c["wts"]["conv"]
    attn_d = _mm(attn, wts["pa"], out_dtype=BF16, name="f_pa")
    xact = _conv_fwd(xbc, conv_w, conv_b)
    dsk_pair = jnp.repeat(d_skip, SSM_HEAD_DIM).reshape(N_PAIR, LANES)
    y, hprev = _ssd_pair_fwd(xact, ac_c, dt_r, ac_r, dsk_pair)
    ssm = _gnorm_fwd(y, z, norm_w)
    ssm_d = _mm(ssm, wts["ps"], out_dtype=BF16, name="f_ps")
    mix = _mix_fwd(gl, bg, attn_d, ssm_d)
    mixed = _mm(mix, wts["out"], name="f_out")
    x1, x1_b = _ln1_fwd(x, mixed, g1, b1)
    gu = _mm(x1_b, wts["gu"], out_dtype=BF16, name="f_gu")
    act = _swiglu_fwd(gu)
    h = _mm(act, wts["down"], name="f_down")
    dpre2, dpre2_b, loss_row, dg2, db2 = _ln2_loss(x1, h, target, g2, b2)

    d_act = _mm(dpre2_b, wts["down"], tb=True, out_dtype=BF16, name="b_down_x")
    dw_down = _mm(act, dpre2_b, ta=True, name="b_down_w")
    dgu = _swiglu_bwd(gu, d_act)
    dffn = _mm(dgu, wts["gu"], tb=True, name="b_gu_x")
    dw_gu = _mm(x1_b, dgu, ta=True, name="b_gu_w")
    dpre1, dxr, dg1, db1 = _ln1_bwd(x, mixed, g1, dpre2, dffn)
    dmix = _mm(dpre1, wts["out"], tb=True, out_dtype=BF16, name="b_out_x")
    dw_out = _mm(mix, dpre1, ta=True, name="b_out_w")
    dattn_d, dssm_d, dgl, dbg = _mix_bwd(gl, bg, attn_d, ssm_d, dmix)
    dssm = _mm(dssm_d, wts["ps"], tb=True, out_dtype=BF16, name="b_ps_x")
    dw_ps = _mm(ssm, dssm_d, ta=True, name="b_ps_w")
    dattn = _mm(dattn_d, wts["pa"], tb=True, name="b_pa_x")
    dw_pa = _mm(attn, dattn_d, ta=True, name="b_pa_w")
    dy, dz, dnw = _gnorm_bwd(y, z, norm_w, dssm)
    dxact, ddt, da_row, dds_pair = _ssd_pair_bwd(xact, dt_c, ac_c, dt_r, ac_r, hprev, dy, dsk_pair, a_row)
    dds = dds_pair.reshape(SSM_HEADS, SSM_HEAD_DIM).sum(axis=1)
    dpre_c, dconv_w, dconv_b = _conv_bwd_pre(xbc, conv_w, conv_b, dxact)
    dxbc = _conv_bwd_in(dpre_c, conv_w)
    st, do_b = _att_prep(dattn, attn, lse)
    late = dict(pa=dw_pa, ps=dw_ps, out=dw_out, gu=dw_gu, down=dw_down)
    keep = ("st", "do_b", "ddt", "dxr", "dz", "dxbc", "dgl", "dconv_w", "dconv_b", "da_row", "dds", "dnw", "dbg",
            "dg1", "db1", "dg2", "db2", "loss_row")
    loc = locals()
    return {**c, **{k: loc[k] for k in keep}}, late


def _stage_out_w(c, att_grads):
    dq, dk, dv, dck, dcq = att_grads
    xb, fd, bias_row, ddt, dz, dxbc, dgl = (c[k] for k in ("xb", "fd", "bias_row", "ddt", "dz", "dxbc", "dgl"))
    s, a = xb.shape[0], c["a"]
    dcum = dck.reshape(ATT_HEADS, s) + dcq.reshape(ATT_HEADS, s)
    dfd, dbias = _stats_bwd(fd, bias_row, ddt, jnp.zeros((LANES, s), F32).at[F_LANE0:F_LANE0 + ATT_HEADS].set(dcum))
    dproj = (dq, dk, dv, dz, dxbc, dgl, dfd)
    dw_in = [_mm(xb, g_, ta=True, tb=(i == 0), name=f"b_in_w{i}") for i, g_ in enumerate(dproj)]
    grads = dict(q=dw_in[0], k=dw_in[1], v=dw_in[2], z=dw_in[3], xbc=dw_in[4], gate=dw_in[5], fd=dw_in[6],
                 conv=c["dconv_w"])
    small_g = dict(
        b_forget=dbias[0, F_LANE0:F_LANE0 + ATT_HEADS], conv_b=c["dconv_b"][0], dt_bias=dbias[0, :SSM_HEADS],
        a_log=c["da_row"][0, :SSM_HEADS] * a, d_skip=c["dds"], ssm_norm_w=c["dnw"][0], b_gates=c["dbg"][0],
        ln1_g=c["dg1"][0], ln1_b=c["db1"][0], ln2_g=c["dg2"][0], ln2_b=c["db2"][0])
    return c["loss_row"][0, 0], grads, small_g, dproj


def _stage_out_x(c, dproj, token):
    wts, d = c["wts"], D_MODEL
    wq = wts["qkv"][:, :d] + token.astype(BF16)
    wk, wv = wts["qkv"][:, d:2 * d], wts["qkv"][:, 2 * d:]
    pairs = [(g_, w_, i == 0) for i, (g_, w_) in enumerate(zip(dproj[:6], (wq, wk, wv, wts["z"], wts["xbc"],
                                                                           wts["gate"])))]
    dx = _mm_sum(pairs, c["dxr"], name="b_in_x")
    return _mm(dproj[6], wts["fd"], tb=True, add=dx, name="b_in_x_fd")


BIG = ("w_in", "w_proj_attn", "w_proj_ssm", "w_out", "w_ffn_gate", "w_ffn_up", "w_ffn_down", "conv_w")
EARLY = ("w_in", "conv_w")
LATE = ("w_proj_attn", "w_proj_ssm", "w_out", "w_ffn_gate", "w_ffn_up", "w_ffn_down")
SMALL = ("b_forget", "conv_b", "dt_bias", "a_log", "d_skip", "ssm_norm_w", "b_gates", "ln1_g", "ln1_b", "ln2_g",
         "ln2_b")
SMALL_ROWS = 96
IN_SHARD = IN_WIDTH // N_DEV
IN_SEGMENTS = (("q", 0, 1024), ("k", 1024, 1024), ("v", 2048, 1024), ("f", 3072, ATT_HEADS), ("z", 3088, SSM_INNER),
               ("xbc", 5136, SSM_CONV_DIM), ("dt", 8208, SSM_HEADS), ("gate", 8240, 2 * D_MODEL))


def _cols_from_shards(shards, lo, hi):
    w = shards[0].shape[1]
    pieces = []
    for j in range(len(shards)):
        a, b = max(lo, j * w), min(hi, (j + 1) * w)
        if a < b:
            pieces.append(shards[j][:, a - j * w:b - j * w])
    return pieces[0] if len(pieces) == 1 else jnp.concatenate(pieces, axis=1)


def _shards_from_parts(parts, width):
    shards = []
    for j in range(N_DEV):
        lo, hi = j * width, (j + 1) * width
        pieces = []
        for mat, c0 in parts:
            a, b = max(lo, c0), min(hi, c0 + mat.shape[1])
            if a < b:
                pieces.append(mat[:, a - c0:b - c0])
        shards.append(pieces[0] if len(pieces) == 1 else jnp.concatenate(pieces, axis=1))
    return shards


def _pack_small(vals):
    flat = jnp.concatenate([vals[n].reshape(-1) for n in SMALL])
    return jnp.pad(flat, (0, SMALL_ROWS * LANES - flat.shape[0])).reshape(SMALL_ROWS, LANES)


def _unpack_small(pack, shapes):
    flat = pack.reshape(-1)
    out, off = {}, 0
    for n in SMALL:
        sz = math.prod(shapes[n])
        out[n] = flat[off:off + sz].reshape(shapes[n])
        off += sz
    return out


def kernel(x, w_in, b_forget, conv_w, conv_b, dt_bias, a_log, d_skip, ssm_norm_w, w_proj_attn, w_proj_ssm, b_gates, w_out, ln1_g, ln1_b, w_ffn_gate, w_ffn_up, w_ffn_down, ln2_g, ln2_b, loss_target, m_w_in, m_b_forget, m_conv_w, m_conv_b, m_dt_bias, m_a_log, m_d_skip, m_ssm_norm_w, m_w_proj_attn, m_w_proj_ssm, m_b_gates, m_w_out, m_ln1_g, m_ln1_b, m_w_ffn_gate, m_w_ffn_up, m_w_ffn_down, m_ln2_g, m_ln2_b, v_w_in, v_b_forget, v_conv_w, v_conv_b, v_dt_bias, v_a_log, v_d_skip, v_ssm_norm_w, v_w_proj_attn, v_w_proj_ssm, v_b_gates, v_w_out, v_ln1_g, v_ln1_b, v_w_ffn_gate, v_w_ffn_up, v_w_ffn_down, v_ln2_g, v_ln2_b):
    args = dict(locals())
    d, f = D_MODEL, FFN_HIDDEN
    big_w = {n: args[n][0] for n in BIG}
    small_w = {n: args[n][0] for n in SMALL}
    big_shapes = {n: args[n].shape for n in BIG}
    small_shapes = {n: args[n].shape for n in SMALL}

    early = dict(zip(EARLY, _all_gather([big_w["w_in"].astype(BF16), big_w["conv_w"]])))
    in_shards = [early["w_in"][j] for j in range(N_DEV)]
    seg = {n: _cols_from_shards(in_shards, c0, c0 + w) for n, c0, w in IN_SEGMENTS}
    wfd = jnp.concatenate([seg["dt"], seg["f"], jnp.zeros((d, LANES - SSM_HEADS - ATT_HEADS), BF16)], axis=1)
    wts = dict(qkv=jnp.concatenate([seg["q"], seg["k"], seg["v"]], axis=1), z=seg["z"], xbc=seg["xbc"],
               gate=seg["gate"], fd=wfd, conv=jnp.concatenate([early["conv_w"][j] for j in range(N_DEV)], axis=1))

    ctx = _stage_in(x[0], wts, small_w)
    attn, lse, gathered = _attention_fwd(ctx["qkv"], ctx["ck4"], [big_w[n].astype(BF16) for n in LATE])
    full = dict(zip(LATE, gathered))
    late_w = dict(
        pa=full["w_proj_attn"].reshape(d, d), ps=full["w_proj_ssm"].reshape(SSM_INNER, d),
        out=full["w_out"].reshape(d, d),
        gu=jnp.concatenate([full["w_ffn_gate"][j] for j in range(N_DEV)]
                           + [full["w_ffn_up"][j] for j in range(N_DEV)], axis=1),
        down=full["w_ffn_down"].reshape(f, d))
    ctx, gl = _stage_mid(ctx, attn, lse, late_w, loss_target[0])
    late_dest = dict(
        w_ffn_gate=jnp.stack([s_.astype(BF16) for s_ in _shards_from_parts([(gl["gu"][:, :f], 0)], f // N_DEV)]),
        w_ffn_up=jnp.stack([s_.astype(BF16) for s_ in _shards_from_parts([(gl["gu"][:, f:], 0)], f // N_DEV)]))
    for n, key in (("w_proj_attn", "pa"), ("w_proj_ssm", "ps"), ("w_out", "out"), ("w_ffn_down", "down")):
        late_dest[n] = gl[key].astype(BF16).reshape((N_DEV,) + big_shapes[n][1:])
    att_grads, late_recv = _attention_bwd(ctx["qkv"], ctx["ck4"], ctx["st"], ctx["do_b"], [late_dest[n] for n in LATE])
    loss_part, g, small_g, dproj = _stage_out_w(ctx, att_grads)
    loss = lax.psum(loss_part, ("x", "y", "c"))

    gfd = g["fd"]
    in_parts = dict(q=g["q"], k=g["k"], v=g["v"], f=gfd[:, F_LANE0:F_LANE0 + ATT_HEADS], z=g["z"], xbc=g["xbc"],
                    dt=gfd[:, DT_LANE0:DT_LANE0 + SSM_HEADS], gate=g["gate"])
    win_dest = jnp.stack([s_.astype(BF16) for s_ in
                          _shards_from_parts([(in_parts[n], c0) for n, c0, _ in IN_SEGMENTS], IN_SHARD)])
    conv_dest = jnp.stack(_shards_from_parts([(g["conv"], 0)], SSM_CONV_DIM // N_DEV))
    small_pack = _pack_small(small_g)
    last_parts = [win_dest, conv_dest, jnp.broadcast_to(small_pack, (N_DEV,) + small_pack.shape)]
    send_sems, recv_sems, parts_thru, lands_thru, token = _exchange_start(last_parts, _landing_zones(last_parts))
    grad_x = _stage_out_x(ctx, dproj, token[0, 0])
    early_recv = _exchange_wait(send_sems, recv_sems, parts_thru, lands_thru, grad_x)
    recv = dict(zip(LATE, late_recv))
    recv["w_in"], recv["conv_w"] = early_recv[0], early_recv[1]

    outs = {}
    for n in BIG:
        outs[n] = _adamw(recv[n], args[n], args["m_" + n], args["v_" + n], name="adamw_" + n)
    small4 = _adamw(early_recv[2], _pack_small(small_w)[None], _pack_small({n: args["m_" + n][0] for n in SMALL})[None],
                    _pack_small({n: args["v_" + n][0] for n in SMALL})[None], name="adamw_small")
    small_out = [_unpack_small(p, small_shapes) for p in small4]
    for n in SMALL:
        outs[n] = [so[n] for so in small_out]

    order = ("w_in", "b_forget", "conv_w", "conv_b", "dt_bias", "a_log", "d_skip", "ssm_norm_w", "w_proj_attn",
             "w_proj_ssm", "b_gates", "w_out", "ln1_g", "ln1_b", "w_ffn_gate", "w_ffn_up", "w_ffn_down", "ln2_g",
             "ln2_b")
    res = [loss, grad_x[None]]
    for i in range(4):
        res += [outs[n][i] for n in order]
    return tuple(res)
```

```python
import functools
import math

import jax
import jax.numpy as jnp
from jax import lax
from jax.experimental import pallas as pl
from jax.experimental.pallas import tpu as pltpu

F32 = jnp.float32
BF16 = jnp.bfloat16

N_DEV = 8
D_MODEL = 1024
ATT_HEADS = 16
ATT_HEAD_DIM = 64
SSM_INNER = 2048
SSM_HEADS = 32
SSM_HEAD_DIM = 64
SSM_GROUPS = 4
SSM_HEADS_PER_GROUP = 8
SSM_STATE = 128
SSM_CONV = 4
SSM_CHUNK = 128
SSM_CONV_DIM = 3072
FFN_HIDDEN = 2816
IN_WIDTH = 10288
DEEPNORM_ALPHA = 2.0 ** 0.25
LN_EPS = 1e-5
RMS_EPS = 1e-5
ADAM_LR, ADAM_B1, ADAM_B2, ADAM_EPS, ADAM_WD, ADAM_STEP = 0.001, 0.9, 0.999, 1e-08, 0.01, 10
ATT_SCALE = 1.0 / math.sqrt(ATT_HEAD_DIM)

LANES = 128
VMEM_LIMIT = 56 * 1024 * 1024
NEG = -1e30

DT_LANE0 = 0
F_LANE0 = 32
HI = lax.Precision.HIGHEST


def _cp(*sem):
    return pltpu.CompilerParams(dimension_semantics=sem, vmem_limit_bytes=VMEM_LIMIT)


def _tile(n, cap=1408):
    for t in (1408, 1024, 512, 384, 256, 128):
        if t <= cap and n % t == 0:
            return t
    return n


def _sigmoid(x):
    return 1.0 / (1.0 + jnp.exp(-x))


def _mm(a, b, *, ta=False, tb=False, out_dtype=F32, add=None, name):
    m, k = (a.shape[1], a.shape[0]) if ta else a.shape
    n = b.shape[0] if tb else b.shape[1]
    assert (b.shape[1] if tb else b.shape[0]) == k
    tm, tn, tk = _tile(m), _tile(n), _tile(k)
    nk = k // tk
    dims = (((0,) if ta else (1,), (1,) if tb else (0,)), ((), ()))

    def body_single(*refs):
        a_ref, b_ref = refs[:2]
        r = lax.dot_general(a_ref[...].astype(BF16), b_ref[...].astype(BF16), dims, preferred_element_type=F32)
        if add is not None:
            r = r + refs[2][...]
        refs[-1][...] = r.astype(refs[-1].dtype)

    def body(*refs):
        if add is None:
            a_ref, b_ref, o_ref, acc_ref = refs
        else:
            a_ref, b_ref, c_ref, o_ref, acc_ref = refs
        kk = pl.program_id(2)

        @pl.when(kk == 0)
        def _():
            acc_ref[...] = jnp.zeros_like(acc_ref)

        acc_ref[...] += lax.dot_general(a_ref[...].astype(BF16), b_ref[...].astype(BF16), dims,
                                        preferred_element_type=F32)

        @pl.when(kk == nk - 1)
        def _():
            r = acc_ref[...]
            if add is not None:
                r = r + c_ref[...]
            o_ref[...] = r.astype(o_ref.dtype)

    a_spec = pl.BlockSpec((tk, tm), lambda i, j, kk: (kk, i)) if ta else pl.BlockSpec((tm, tk), lambda i, j, kk: (i, kk))
    b_spec = pl.BlockSpec((tn, tk), lambda i, j, kk: (j, kk)) if tb else pl.BlockSpec((tk, tn), lambda i, j, kk: (kk, j))
    o_spec = pl.BlockSpec((tm, tn), lambda i, j, kk: (i, j))
    in_specs, args = [a_spec, b_spec], [a, b]
    if add is not None:
        in_specs.append(o_spec)
        args.append(add)
    return pl.pallas_call(
        body_single if nk == 1 else body, name=name, grid=(m // tm, n // tn, nk), in_specs=in_specs, out_specs=o_spec,
        out_shape=jax.ShapeDtypeStruct((m, n), out_dtype),
        scratch_shapes=[] if nk == 1 else [pltpu.VMEM((tm, tn), F32)],
        compiler_params=_cp("parallel", "parallel", "arbitrary"),
    )(*args)


def _tri(n, lower=True):
    r = lax.broadcasted_iota(jnp.int32, (n, n), 0)
    c = lax.broadcasted_iota(jnp.int32, (n, n), 1)
    return jnp.where((r >= c) if lower else (c >= r), 1.0, 0.0).astype(F32)


def _stats_fwd(fd, bias_row, a_row):
    s = fd.shape[0]
    blk = SSM_CHUNK

    def body(fd_ref, bias_ref, a_ref, dt_ref, ac_ref, cf_ref, dtr_ref, acr_ref, cfr_ref, carry_ref):
        @pl.when(pl.program_id(0) == 0)
        def _():
            carry_ref[...] = jnp.zeros_like(carry_ref)

        v = fd_ref[...] + bias_ref[...]
        dt = jnp.maximum(v, 0.0) + jnp.log(1.0 + jnp.exp(-jnp.abs(v)))
        lf = jnp.minimum(v, 0.0) - jnp.log(1.0 + jnp.exp(-jnp.abs(v)))
        tri = _tri(blk)
        ac = jnp.dot(tri, dt * a_ref[...], precision=HI, preferred_element_type=F32)
        cf = jnp.dot(tri, lf, precision=HI, preferred_element_type=F32) + carry_ref[0:1, :]
        carry_ref[...] = carry_ref[...] + jnp.sum(lf, axis=0, keepdims=True)
        dt_ref[...] = dt
        ac_ref[...] = ac
        cf_ref[...] = cf
        dtr_ref[...] = dt.T
        acr_ref[...] = ac.T
        cfr_ref[...] = cf.T

    col = pl.BlockSpec((blk, LANES), lambda i: (i, 0))
    row = pl.BlockSpec((LANES, blk), lambda i: (0, i))
    vec = pl.BlockSpec((1, LANES), lambda i: (0, 0))
    return pl.pallas_call(
        body, name="stats_fwd", grid=(s // blk,), in_specs=[col, vec, vec],
        out_specs=[col, col, col, row, row, row],
        out_shape=[jax.ShapeDtypeStruct((s, LANES), F32)] * 3 + [jax.ShapeDtypeStruct((LANES, s), F32)] * 3,
        scratch_shapes=[pltpu.VMEM((8, LANES), F32)],
        compiler_params=_cp("arbitrary"),
    )(fd, bias_row, a_row)


def _stats_bwd(fd, bias_row, ddt, dcum_rows):
    s = fd.shape[0]
    blk = SSM_CHUNK
    nb = s // blk

    def body(fd_ref, bias_ref, ddt_ref, dck_ref, o_ref, db_ref, carry_ref):
        @pl.when(pl.program_id(0) == 0)
        def _():
            carry_ref[...] = jnp.zeros_like(carry_ref)
            db_ref[...] = jnp.zeros_like(db_ref)

        v = fd_ref[...] + bias_ref[...]
        dcum = dck_ref[...].T
        dlf = jnp.dot(_tri(blk, lower=False), dcum, precision=HI, preferred_element_type=F32) + carry_ref[0:1, :]
        carry_ref[...] = carry_ref[...] + jnp.sum(dcum, axis=0, keepdims=True)
        lane = lax.broadcasted_iota(jnp.int32, v.shape, 1)
        g = jnp.where(lane < F_LANE0, ddt_ref[...] * _sigmoid(v), dlf * _sigmoid(-v))
        g = jnp.where(lane < F_LANE0 + ATT_HEADS, g, 0.0)
        o_ref[...] = g.astype(o_ref.dtype)
        db_ref[...] += jnp.sum(g, axis=0, keepdims=True)

    col = pl.BlockSpec((blk, LANES), lambda i: (nb - 1 - i, 0))
    row = pl.BlockSpec((LANES, blk), lambda i: (0, nb - 1 - i))
    vec = pl.BlockSpec((1, LANES), lambda i: (0, 0))
    return pl.pallas_call(
        body, name="stats_bwd", grid=(nb,), in_specs=[col, vec, col, row], out_specs=[col, vec],
        out_shape=[jax.ShapeDtypeStruct((s, LANES), BF16), jax.ShapeDtypeStruct((1, LANES), F32)],
        scratch_shapes=[pltpu.VMEM((8, LANES), F32)],
        compiler_params=_cp("arbitrary"),
    )(fd, bias_row, ddt, dcum_rows)


HP = LANES // ATT_HEAD_DIM
N_HP = ATT_HEADS // HP


def _att_blocks(s):
    return (512, 1024) if s % 1024 == 0 and s >= 4096 else (64, 128)


_QK = (((1,), (1,)), ((), ()))
_HALF = ATT_HEAD_DIM // 2
_PAD = 16


def _head_cols(a):
    return slice(a * ATT_HEAD_DIM, (a + 1) * ATT_HEAD_DIM)


def _causal(shape, off):
    r = lax.broadcasted_iota(jnp.int32, shape, 0)
    c = lax.broadcasted_iota(jnp.int32, shape, 1)
    return c <= r + off


def _attention_fwd(qkv, ck4, gather_parts):
    s = qkv.shape[0]
    bk = _att_blocks(s)[1]
    bq = bk
    nq, nk = s // bq, s // bk
    n = len(gather_parts)

    def body(q_ref, k_ref, v_ref, ck_ref, *rest):
        comm_in, (o_ref, lse_ref), comm_out, sems = rest[:n], rest[n:n + 2], rest[n + 2:2 * n + 2], rest[2 * n + 2:]
        i = pl.program_id(1)
        if n:
            @pl.when((pl.program_id(0) == 0) & (i == 0))
            def _():
                for cp in _comm_copies(comm_in, comm_out, sems, False):
                    cp.start()

        n_full = (i * bq) // bk
        qs = [(q_ref[:, _head_cols(a)].astype(F32) * ATT_SCALE).astype(BF16) for a in range(HP)]

        upper_q = lax.broadcasted_iota(jnp.int32, (bq, LANES), 1) >= ATT_HEAD_DIM

        def absorb(j, carry, keys=slice(0, bk), rows=slice(0, bq), masked=False):
            nk_ = keys.stop - keys.start
            ks = pl.ds(pl.multiple_of(j * bk, bk) + keys.start, nk_)
            v_both = v_ref[ks, :]
            out = []
            for a in range(HP):
                m, acc = carry[a]
                sc = lax.dot_general(qs[a][rows], k_ref[ks, _head_cols(a)], _QK, preferred_element_type=F32)
                sc = sc - ck_ref[0, a, pl.ds(j, 1), keys]
                if masked:
                    sc = jnp.where(_causal(sc.shape, 0), sc, NEG)
                m_new = jnp.maximum(m, jnp.max(sc, axis=1, keepdims=True))
                p = jnp.exp((sc - m_new).astype(BF16))
                upper_v = lax.broadcasted_iota(jnp.int32, (nk_, LANES), 1) >= ATT_HEAD_DIM
                v_aug = jnp.where(upper_v == (a == 1), v_both, jnp.ones_like(v_both))
                acc = jnp.exp(m - m_new) * acc + jnp.dot(p, v_aug, preferred_element_type=F32)
                out.append((m_new, acc))
            return tuple(out)

        init = tuple((jnp.full((bq, 1), NEG, F32), jnp.zeros((bq, LANES), F32)) for _ in range(HP))
        carry = lax.fori_loop(0, n_full, absorb, init)
        hq = bq // 2
        carry = absorb(n_full, carry, keys=slice(0, hq), masked=True)
        low = absorb(n_full, tuple((m[hq:], acc[hq:]) for m, acc in carry), keys=slice(hq, bk), rows=slice(hq, bq),
                     masked=True)
        carry = tuple((jnp.concatenate([m[:hq], ml], axis=0), jnp.concatenate([acc[:hq], al], axis=0))
                      for (m, acc), (ml, al) in zip(carry, low))
        outs, lses = [], []
        for a in range(HP):
            m, acc = carry[a]
            l = pltpu.roll(acc, ATT_HEAD_DIM, 1)
            outs.append(acc / l)
            lses.append(m + jnp.log(l))
        o_ref[...] = jnp.where(upper_q, outs[1], outs[0])
        lse_ref[...] = jnp.where(upper_q, lses[1], lses[0])
        if n:
            @pl.when((pl.program_id(0) == N_HP - 1) & (i == nq - 1))
            def _():
                for cp in _comm_copies(comm_in, comm_out, sems, False):
                    cp.wait()

    q_spec = pl.BlockSpec((bq, LANES), lambda h, i: (i, h))
    anyspec = pl.BlockSpec(memory_space=pl.ANY)
    res = pl.pallas_call(
        body, name="att_fwd", grid=(N_HP, nq),
        in_specs=[q_spec, pl.BlockSpec((s, LANES), lambda h, i: (0, N_HP + h)),
                  pl.BlockSpec((s, LANES), lambda h, i: (0, 2 * N_HP + h)),
                  pl.BlockSpec((1, HP, nk, bk), lambda h, i: (h, 0, 0, 0))] + [anyspec] * n,
        out_specs=[q_spec, q_spec] + [anyspec] * n,
        out_shape=[jax.ShapeDtypeStruct((s, D_MODEL), F32)] * 2 + _comm_out_shapes(gather_parts, False),
        scratch_shapes=_comm_sems(n) if n else [],
        compiler_params=_cp("arbitrary", "arbitrary"),
    )(qkv, qkv, qkv, ck4, *gather_parts)
    return res[0], res[1], list(res[2:])


def _att_prep(do, o, lse_rep):
    s = do.shape[0]
    bs = _tile(s, 512)

    def body(do_ref, o_ref, lse_ref, st_ref, dob_ref):
        r = lax.broadcasted_iota(jnp.int32, (LANES, LANES), 0) // ATT_HEAD_DIM
        c = lax.broadcasted_iota(jnp.int32, (LANES, LANES), 1) // ATT_HEAD_DIM
        e = jnp.where(r == c, 1.0, 0.0).astype(F32)
        lane = lax.broadcasted_iota(jnp.int32, (bs, LANES), 1)
        for p in range(D_MODEL // LANES):
            cs = slice(p * LANES, (p + 1) * LANES)
            dd = do_ref[:, cs]
            delta = jnp.dot(dd * o_ref[:, cs], e, precision=HI, preferred_element_type=F32)
            st_ref[:, cs] = jnp.where(lane % ATT_HEAD_DIM < _HALF, lse_ref[:, cs], delta)
            dob_ref[:, cs] = dd.astype(BF16)

    spec = pl.BlockSpec((bs, D_MODEL), lambda i: (i, 0))
    return pl.pallas_call(body, name="att_prep", grid=(s // bs,), in_specs=[spec, spec, spec], out_specs=[spec, spec],
                          out_shape=[jax.ShapeDtypeStruct((s, D_MODEL), F32), jax.ShapeDtypeStruct((s, D_MODEL), BF16)],
                          compiler_params=_cp("parallel"))(do, o, lse_rep)


def _attention_bwd(qkv, ck4, st, do_b, exchange_parts):
    s = qkv.shape[0]
    bq, bk = _att_blocks(s)
    nq, nk, per = s // bq, s // bk, bk // bq
    _T = (((0,), (0,)), ((), ()))
    n = len(exchange_parts)

    def body(q_ref, k_ref, v_ref, ck_ref, st_ref, do_ref, *rest):
        comm_in, (dq_ref, dk_ref, dv_ref, dck_ref, dcq_ref) = rest[:n], rest[n:n + 5]
        comm_out, sems, (dk_acc, dv_acc) = rest[n + 5:2 * n + 5], rest[2 * n + 5:-2], rest[-2:]
        j = pl.program_id(1)
        if n:
            @pl.when((pl.program_id(0) == 0) & (j == 0))
            def _():
                for cp in _comm_copies(comm_in, comm_out, sems, True):
                    cp.start()

        @pl.when(j == 0)
        def _():
            dq_ref[...] = jnp.zeros_like(dq_ref)
            dcq_ref[...] = jnp.zeros_like(dcq_ref)

        dk_acc[...] = jnp.zeros_like(dk_acc)
        dv_acc[...] = jnp.zeros_like(dv_acc)

        ones_q, ones_k = jnp.ones((_PAD, bq), BF16), jnp.ones((_PAD, bk), BF16)
        k_t = [jnp.concatenate([k_ref[:, _head_cols(a)].T, ones_k], axis=0) for a in range(HP)]

        def step(i, off=None, kl=slice(0, bk)):
            rows = pl.ds(pl.multiple_of(i * bq, bq), bq)
            for a in range(HP):
                cs = _head_cols(a)
                q = (q_ref[rows, cs].astype(F32) * ATT_SCALE).astype(BF16)
                do_a = do_ref[rows, cs]
                sc = lax.dot_general(q, k_ref[kl, cs], _QK, preferred_element_type=F32) - ck_ref[0, a, pl.ds(j, 1), kl]
                if off is not None:
                    sc = jnp.where(_causal(sc.shape, off), sc, NEG)
                p = jnp.exp(sc - st_ref[rows, a * ATT_HEAD_DIM:a * ATT_HEAD_DIM + 1])
                dp = lax.dot_general(do_a, v_ref[kl, cs], _QK, preferred_element_type=F32)
                ds = p * (dp - st_ref[rows, a * ATT_HEAD_DIM + _HALF:a * ATT_HEAD_DIM + _HALF + 1])
                ds_b = ds.astype(BF16)
                dv_acc[a, :, kl] += jnp.dot(do_a.T, p.astype(BF16), preferred_element_type=F32)
                dk_acc[a, :, kl] += jnp.dot(jnp.concatenate([q.T, ones_q], axis=0), ds_b, preferred_element_type=F32)
                dqs = lax.dot_general(k_t[a][:, kl], ds_b, _QK, preferred_element_type=F32)
                dq_ref[cs, rows] += dqs[:ATT_HEAD_DIM] * ATT_SCALE
                dcq_ref[0, a, pl.ds(i, 1), :] += jnp.sum(dqs[ATT_HEAD_DIM:ATT_HEAD_DIM + 8], axis=0,
                                                         keepdims=True) * 0.125

        for t in range(per):
            step(j * per + t, off=t * bq, kl=slice(0, (t + 1) * bq))

        def full(i, c):
            step(i)
            return c

        lax.fori_loop((j + 1) * per, nq, full, 0)
        for a in range(HP):
            dk_ref[:, _head_cols(a)] = dk_acc[a, :ATT_HEAD_DIM].T.astype(dk_ref.dtype)
            dv_ref[:, _head_cols(a)] = dv_acc[a].T.astype(dv_ref.dtype)
            dck_ref[0, a, pl.ds(j, 1), :] = -dk_acc[a, ATT_HEAD_DIM:ATT_HEAD_DIM + 1]
        if n:
            @pl.when((pl.program_id(0) == N_HP - 1) & (j == nk - 1))
            def _():
                for cp in _comm_copies(comm_in, comm_out, sems, True):
                    cp.wait()

    res = pl.BlockSpec((s, LANES), lambda h, j: (0, h))
    ck_spec = pl.BlockSpec((1, HP, nk, bk), lambda h, j: (h, 0, 0, 0))
    kout = pl.BlockSpec((bk, LANES), lambda h, j: (j, h))
    anyspec = pl.BlockSpec(memory_space=pl.ANY)
    outs = pl.pallas_call(
        body, name="att_bwd", grid=(N_HP, nk),
        in_specs=[res, pl.BlockSpec((bk, LANES), lambda h, j: (j, N_HP + h)),
                  pl.BlockSpec((bk, LANES), lambda h, j: (j, 2 * N_HP + h)), ck_spec, res, res] + [anyspec] * n,
        out_specs=[pl.BlockSpec((LANES, s), lambda h, j: (h, 0)), kout, kout, ck_spec,
                   pl.BlockSpec((1, HP, nq, bq), lambda h, j: (h, 0, 0, 0))] + [anyspec] * n,
        out_shape=[jax.ShapeDtypeStruct((D_MODEL, s), F32), jax.ShapeDtypeStruct((s, D_MODEL), BF16),
                   jax.ShapeDtypeStruct((s, D_MODEL), BF16), jax.ShapeDtypeStruct((N_HP, HP, nk, bk), F32),
                   jax.ShapeDtypeStruct((N_HP, HP, nq, bq), F32)] + _comm_out_shapes(exchange_parts, True),
        scratch_shapes=(_comm_sems(n) if n else [])
        + [pltpu.VMEM((HP, ATT_HEAD_DIM + _PAD, bk), F32), pltpu.VMEM((HP, ATT_HEAD_DIM, bk), F32)],
        compiler_params=_cp("arbitrary", "arbitrary"),
    )(qkv, qkv, qkv, ck4, st, do_b, *exchange_parts)
    return outs[:5], list(outs[5:])


def _silu_and_grad(x):
    sg = _sigmoid(x)
    return x * sg, sg * (1.0 + x * (1.0 - sg))


def _conv_pre(cur, halo, w_ref, b_ref, first):
    halo = jnp.where(first, 0.0, halo)
    row = lax.broadcasted_iota(jnp.int32, cur.shape, 0)
    shifted = []
    for k in range(SSM_CONV):
        sh = SSM_CONV - 1 - k
        if sh == 0:
            shifted.append(cur)
            continue
        r = pltpu.roll(cur, sh, 0)
        hr = pltpu.roll(halo, sh, 0)
        top = jnp.where(row[0:8] < sh, hr, r[0:8])
        shifted.append(jnp.concatenate([top, r[8:]], axis=0))
    pre = b_ref[...] + sum(w_ref[k:k + 1, :] * shifted[k] for k in range(SSM_CONV))
    return pre, shifted


def _conv_specs(s, bs, bc):
    cur = pl.BlockSpec((bs, bc), lambda j, i: (i, j))
    halo = pl.BlockSpec((8, bc), lambda j, i: (jnp.maximum(i * (bs // 8) - 1, 0), j))
    w = pl.BlockSpec((SSM_CONV, bc), lambda j, i: (0, j))
    b = pl.BlockSpec((1, bc), lambda j, i: (0, j))
    return cur, halo, w, b


def _conv_fwd(xbc, w, b):
    s, c = xbc.shape
    bs, bc = _tile(s, 512), 1024

    def body(x_ref, h_ref, w_ref, b_ref, o_ref):
        pre, _ = _conv_pre(x_ref[...], h_ref[...], w_ref, b_ref, pl.program_id(1) == 0)
        o_ref[...] = pre * _sigmoid(pre)

    cur, halo, ws, bsp = _conv_specs(s, bs, bc)
    return pl.pallas_call(body, name="conv_fwd", grid=(c // bc, s // bs), in_specs=[cur, halo, ws, bsp],
                          out_specs=cur, out_shape=jax.ShapeDtypeStruct((s, c), F32),
                          compiler_params=_cp("parallel", "parallel"))(xbc, xbc, w, b)


def _conv_bwd_pre(xbc, w, b, dact):
    s, c = xbc.shape
    bs, bc = _tile(s, 512), 1024

    def body(x_ref, h_ref, w_ref, b_ref, g_ref, dp_ref, dw_ref, db_ref):
        @pl.when(pl.program_id(1) == 0)
        def _():
            dw_ref[...] = jnp.zeros_like(dw_ref)
            db_ref[...] = jnp.zeros_like(db_ref)

        pre, shifted = _conv_pre(x_ref[...], h_ref[...], w_ref, b_ref, pl.program_id(1) == 0)
        dpre = g_ref[...].astype(F32) * _silu_and_grad(pre)[1]
        dp_ref[...] = dpre.astype(dp_ref.dtype)
        db_ref[...] += jnp.sum(dpre, axis=0, keepdims=True)
        for k in range(SSM_CONV):
            dw_ref[k:k + 1, :] += jnp.sum(dpre * shifted[k], axis=0, keepdims=True)

    cur, halo, ws, bsp = _conv_specs(s, bs, bc)
    return pl.pallas_call(
        body, name="conv_bwd_pre", grid=(c // bc, s // bs), in_specs=[cur, halo, ws, bsp, cur],
        out_specs=[cur, ws, bsp],
        out_shape=[jax.ShapeDtypeStruct((s, c), BF16), jax.ShapeDtypeStruct((SSM_CONV, c), F32),
                   jax.ShapeDtypeStruct((1, c), F32)],
        compiler_params=_cp("parallel", "arbitrary"))(xbc, xbc, w, b, dact)


def _conv_bwd_in(dpre, w):
    s, c = dpre.shape
    bs, bc = _tile(s, 512), 1024
    nb = s // bs

    def body(g_ref, n_ref, w_ref, o_ref):
        cur = g_ref[...].astype(F32)
        nxt = jnp.where(pl.program_id(1) == nb - 1, 0.0, n_ref[...].astype(F32)[0:8])
        row = lax.broadcasted_iota(jnp.int32, cur.shape, 0)
        acc = w_ref[SSM_CONV - 1:SSM_CONV, :] * cur
        for sh in range(1, SSM_CONV):
            r = pltpu.roll(cur, bs - sh, 0)
            nr = pltpu.roll(nxt, 8 - sh, 0)
            bot = jnp.where(row[0:8] >= 8 - sh, nr, r[bs - 8:])
            acc = acc + w_ref[SSM_CONV - 1 - sh:SSM_CONV - sh, :] * jnp.concatenate([r[:bs - 8], bot], axis=0)
        o_ref[...] = acc.astype(o_ref.dtype)

    cur = pl.BlockSpec((bs, bc), lambda j, i: (i, j))
    nxt = pl.BlockSpec((16, bc), lambda j, i: (jnp.minimum((i + 1) * (bs // 16), s // 16 - 1), j))
    ws = pl.BlockSpec((SSM_CONV, bc), lambda j, i: (0, j))
    return pl.pallas_call(body, name="conv_bwd_in", grid=(c // bc, nb), in_specs=[cur, nxt, ws], out_specs=cur,
                          out_shape=jax.ShapeDtypeStruct((s, c), BF16),
                          compiler_params=_cp("parallel", "parallel"))(dpre, dpre, w)


def _dotT(a, b):
    return lax.dot_general(a.astype(BF16), b.astype(BF16), (((1,), (1,)), ((), ())), preferred_element_type=F32)


def _dot(a, b):
    return jnp.dot(a.astype(BF16), b.astype(BF16), preferred_element_type=F32)


N_PAIR = SSM_HEADS // HP
PAIRS_PER_GROUP = SSM_HEADS_PER_GROUP // HP


def _pair_consts():
    L = SSM_CHUNK
    lane = lax.broadcasted_iota(jnp.int32, (L, LANES), 1)
    lane1 = lax.broadcasted_iota(jnp.int32, (1, LANES), 1)
    li = lax.broadcasted_iota(jnp.int32, (L, L), 0)
    si = lax.broadcasted_iota(jnp.int32, (L, L), 1)
    return lane >= ATT_HEAD_DIM, lane1 >= ATT_HEAD_DIM, li, si


def _ssd_pair_fwd(xbc_act, ac_c, dt_r, ac_r, dsk_pair):
    s = xbc_act.shape[0]
    L, N, G = SSM_CHUNK, SSM_STATE, SSM_GROUPS
    nc = s // L

    def body(xbc_ref, ac_ref, dtr_ref, acr_ref, dsk_ref, y_ref, hp_ref, st_ref):
        @pl.when(pl.program_id(0) == 0)
        def _():
            st_ref[...] = jnp.zeros_like(st_ref)

        upper, up1, li, si = _pair_consts()
        for g in range(G):
            b_g = xbc_ref[:, SSM_INNER + g * N:SSM_INNER + (g + 1) * N]
            c_g = xbc_ref[:, SSM_INNER + G * N + g * N:SSM_INNER + G * N + (g + 1) * N]
            cb = _dotT(c_g, b_g)
            b_t = b_g.T
            for q in range(PAIRS_PER_GROUP):
                pp = g * PAIRS_PER_GROUP + q
                cols = slice(pp * LANES, (pp + 1) * LANES)
                xs = xbc_ref[:, cols]
                ht = st_ref[pp]
                hp_ref[0, pp] = ht
                y = dsk_ref[pp:pp + 1, :] * xs
                s_new = jnp.zeros((N, LANES), F32)
                ea, el = [], []
                for a in range(HP):
                    h = HP * pp + a
                    acol = jnp.broadcast_to(ac_ref[:, h:h + 1], (L, LANES))
                    arow, dtrow = acr_ref[h:h + 1, :], dtr_ref[h:h + 1, :]
                    alast = ac_ref[L - 1:L, h:h + 1]
                    decay = jnp.exp(jnp.where(li >= si, acol - arow, NEG))
                    xs_a = jnp.where(upper == (a == 1), xs, 0.0)
                    y = y + _dot(cb * decay * dtrow, xs_a)
                    s_new = s_new + _dot(b_t * (dtrow * jnp.exp(alast - arow)), xs_a)
                    ea.append(jnp.exp(acol))
                    el.append(jnp.exp(alast))
                y_ref[:, cols] = y + jnp.where(upper, ea[1], ea[0]) * _dot(c_g, ht)
                st_ref[pp] = ht * jnp.where(up1, el[1], el[0]) + s_new

    col = pl.BlockSpec((L, LANES), lambda c: (c, 0))
    row = pl.BlockSpec((LANES, L), lambda c: (0, c))
    return pl.pallas_call(
        body, name="ssd_fwd", grid=(nc,),
        in_specs=[pl.BlockSpec((L, SSM_CONV_DIM), lambda c: (c, 0)), col, row, row,
                  pl.BlockSpec((N_PAIR, LANES), lambda c: (0, 0))],
        out_specs=[pl.BlockSpec((L, SSM_INNER), lambda c: (c, 0)),
                   pl.BlockSpec((1, N_PAIR, N, LANES), lambda c: (c, 0, 0, 0))],
        out_shape=[jax.ShapeDtypeStruct((s, SSM_INNER), F32), jax.ShapeDtypeStruct((nc, N_PAIR, N, LANES), F32)],
        scratch_shapes=[pltpu.VMEM((N_PAIR, N, LANES), F32)],
        compiler_params=_cp("arbitrary"),
    )(xbc_act, ac_c, dt_r, ac_r, dsk_pair)


def _ssd_pair_bwd(xbc_act, dt_c, ac_c, dt_r, ac_r, hprev_all, dy, dsk_pair, a_row):
    s = xbc_act.shape[0]
    L, N, G = SSM_CHUNK, SSM_STATE, SSM_GROUPS
    nc = s // L
    rev = lambda c: nc - 1 - c

    def body(xbc_ref, dt_ref, ac_ref, dtr_ref, acr_ref, hp_ref, dy_ref, dsk_ref, arow_ref,
             dx_ref, ddt_ref, da_ref, dds_ref, dh_ref):
        @pl.when(pl.program_id(0) == 0)
        def _():
            dh_ref[...] = jnp.zeros_like(dh_ref)
            da_ref[...] = jnp.zeros_like(da_ref)
            dds_ref[...] = jnp.zeros_like(dds_ref)

        upper, up1, li, si = _pair_consts()
        lane = lax.broadcasted_iota(jnp.int32, (L, LANES), 1)
        sub = lax.broadcasted_iota(jnp.int32, (LANES, L), 0)
        lastrow = lax.broadcasted_iota(jnp.int32, (L, LANES), 0) == L - 1
        da_c = jnp.zeros((L, LANES), F32)
        da_r = jnp.zeros((LANES, L), F32)
        ddt_r = jnp.zeros((LANES, L), F32)
        for g in range(G):
            b_g = xbc_ref[:, SSM_INNER + g * N:SSM_INNER + (g + 1) * N]
            c_g = xbc_ref[:, SSM_INNER + G * N + g * N:SSM_INNER + G * N + (g + 1) * N]
            cb, cb_t = _dotT(c_g, b_g), _dotT(b_g, c_g)
            b_t, c_t = b_g.T, c_g.T
            dcb = jnp.zeros((L, L), F32)
            db_t = jnp.zeros((N, L), F32)
            dc = jnp.zeros((L, N), F32)
            for q in range(PAIRS_PER_GROUP):
                pp = g * PAIRS_PER_GROUP + q
                cols = slice(pp * LANES, (pp + 1) * LANES)
                xs, gy = xbc_ref[:, cols], dy_ref[:, cols].astype(F32)
                ht, dhn = hp_ref[0, pp], dh_ref[pp]
                acol = [jnp.broadcast_to(ac_ref[:, HP * pp + a:HP * pp + a + 1], (L, LANES)) for a in range(HP)]
                alast = [ac_ref[L - 1:L, HP * pp + a:HP * pp + a + 1] for a in range(HP)]
                ea = jnp.where(upper, jnp.exp(acol[1]), jnp.exp(acol[0]))
                el = jnp.where(up1, jnp.exp(alast[1]), jnp.exp(alast[0]))
                ge = gy * ea
                dc = dc + _dotT(ge, ht)
                dh_ref[pp] = _dot(c_t, ge) + dhn * el
                t_off = (ge * _dot(c_g, ht)).astype(BF16)
                hsum = jnp.sum(dhn * ht, axis=0, keepdims=True)
                dxs = dsk_ref[pp:pp + 1, :] * gy
                dds_ref[pp:pp + 1, :] += jnp.sum(gy * xs, axis=0, keepdims=True)
                for a in range(HP):
                    h = HP * pp + a
                    mine, mine1 = upper == (a == 1), up1 == (a == 1)
                    arow, dtrow = acr_ref[h:h + 1, :], dtr_ref[h:h + 1, :]
                    dtcol = jnp.broadcast_to(dt_ref[:, h:h + 1], (L, LANES))
                    xs_a, gy_a = jnp.where(mine, xs, 0.0), jnp.where(mine, gy, 0.0)
                    dhn_a = jnp.where(mine1, dhn, 0.0)
                    e_row = jnp.exp(alast[a] - arow)
                    w_row = dtrow * e_row
                    xd_t = _dotT(dhn_a, xs_a)
                    db_t = db_t + xd_t * w_row
                    dw = jnp.sum(b_t * xd_t, axis=0, keepdims=True)
                    de_e = dw * w_row
                    dal = (jnp.sum(jnp.where(mine1, hsum, 0.0), axis=1, keepdims=True) * jnp.exp(alast[a])
                           + jnp.sum(de_e, axis=1, keepdims=True))
                    dxs = dxs + _dot(b_g, dhn_a) * (dtcol * jnp.exp(alast[a] - acol[a]))
                    decay = jnp.exp(jnp.where(li >= si, acol[a] - arow, NEG))
                    decay_t = jnp.exp(jnp.where(si >= li, arow - acol[a], NEG))
                    m = cb * decay
                    dmdt = _dotT(gy_a, xs_a)
                    dxs = dxs + _dot(cb_t * decay_t * dtcol, gy_a)
                    dm = dmdt * dtrow
                    dcb = dcb + dm * decay
                    wb = (dm * m).astype(BF16)
                    onehot = jnp.where(lane == h, 1.0, 0.0).astype(BF16)
                    da_c = (da_c + jnp.dot(wb, onehot, preferred_element_type=F32)
                            + jnp.dot(jnp.where(mine, t_off, 0.0).astype(BF16), onehot, preferred_element_type=F32)
                            + jnp.where(lastrow & (lane == h), dal, 0.0))
                    da_r = jnp.where(sub == h, -(jnp.sum(wb.astype(F32), axis=0, keepdims=True) + de_e), da_r)
                    ddt_r = jnp.where(sub == h, dw * e_row + jnp.sum(dmdt * m, axis=0, keepdims=True), ddt_r)
                dx_ref[:, cols] = dxs.astype(dx_ref.dtype)
            dx_ref[:, SSM_INNER + g * N:SSM_INNER + (g + 1) * N] = (db_t + _dot(c_t, dcb)).T.astype(dx_ref.dtype)
            dx_ref[:, SSM_INNER + G * N + g * N:SSM_INNER + G * N + (g + 1) * N] = (
                dc + _dot(dcb, b_g)).astype(dx_ref.dtype)
        dda = jnp.dot(_tri(L, lower=False), da_c + da_r.T, precision=HI, preferred_element_type=F32)
        ddt_ref[...] = dda * arow_ref[...] + ddt_r.T
        da_ref[...] += jnp.sum(dda * dt_ref[...], axis=0, keepdims=True)

    col = pl.BlockSpec((L, LANES), lambda c: (rev(c), 0))
    row = pl.BlockSpec((LANES, L), lambda c: (0, rev(c)))
    vec = pl.BlockSpec((1, LANES), lambda c: (0, 0))
    pairs = pl.BlockSpec((N_PAIR, LANES), lambda c: (0, 0))
    return pl.pallas_call(
        body, name="ssd_bwd", grid=(nc,),
        in_specs=[pl.BlockSpec((L, SSM_CONV_DIM), lambda c: (rev(c), 0)), col, col, row, row,
                  pl.BlockSpec((1, N_PAIR, N, LANES), lambda c: (rev(c), 0, 0, 0)),
                  pl.BlockSpec((L, SSM_INNER), lambda c: (rev(c), 0)), pairs, vec],
        out_specs=[pl.BlockSpec((L, SSM_CONV_DIM), lambda c: (rev(c), 0)), col, vec, pairs],
        out_shape=[jax.ShapeDtypeStruct((s, SSM_CONV_DIM), BF16), jax.ShapeDtypeStruct((s, LANES), F32),
                   jax.ShapeDtypeStruct((1, LANES), F32), jax.ShapeDtypeStruct((N_PAIR, LANES), F32)],
        scratch_shapes=[pltpu.VMEM((N_PAIR, N, LANES), F32)],
        compiler_params=_cp("arbitrary"),
    )(xbc_act, dt_c, ac_c, dt_r, ac_r, hprev_all, dy, dsk_pair, a_row)


ROWS = 512
GW = SSM_INNER // SSM_GROUPS


def _rows(width, dtype=F32):
    return pl.BlockSpec((ROWS, width), lambda i: (i, 0))


def _vec(width):
    return pl.BlockSpec((1, width), lambda i: (0, 0))


def _gnorm_fwd(y, z, w):
    s = y.shape[0]

    def body(y_ref, z_ref, w_ref, o_ref):
        for g in range(SSM_GROUPS):
            cs = slice(g * GW, (g + 1) * GW)
            zz = z_ref[:, cs].astype(F32)
            u = y_ref[:, cs] * (zz * _sigmoid(zz))
            r = lax.rsqrt(jnp.mean(u * u, axis=1, keepdims=True) + RMS_EPS)
            o_ref[:, cs] = (u * r * w_ref[:, cs]).astype(o_ref.dtype)

    return pl.pallas_call(body, name="gnorm_fwd", grid=(s // ROWS,),
                          in_specs=[_rows(SSM_INNER), _rows(SSM_INNER), _vec(SSM_INNER)], out_specs=_rows(SSM_INNER),
                          out_shape=jax.ShapeDtypeStruct((s, SSM_INNER), BF16), compiler_params=_cp("parallel"))(y, z, w)


def _gnorm_bwd(y, z, w, do):
    s = y.shape[0]

    def body(y_ref, z_ref, w_ref, do_ref, dy_ref, dz_ref, dw_ref):
        @pl.when(pl.program_id(0) == 0)
        def _():
            dw_ref[...] = jnp.zeros_like(dw_ref)

        for g in range(SSM_GROUPS):
            cs = slice(g * GW, (g + 1) * GW)
            zz, yy, dd = z_ref[:, cs].astype(F32), y_ref[:, cs], do_ref[:, cs].astype(F32)
            sz, dsz = _silu_and_grad(zz)
            u = yy * sz
            r = lax.rsqrt(jnp.mean(u * u, axis=1, keepdims=True) + RMS_EPS)
            n = u * r
            dn = dd * w_ref[:, cs]
            dw_ref[:, cs] += jnp.sum(dd * n, axis=0, keepdims=True)
            du = r * (dn - n * jnp.mean(dn * n, axis=1, keepdims=True))
            dy_ref[:, cs] = (du * sz).astype(dy_ref.dtype)
            dz_ref[:, cs] = (du * yy * dsz).astype(dz_ref.dtype)

    return pl.pallas_call(
        body, name="gnorm_bwd", grid=(s // ROWS,),
        in_specs=[_rows(SSM_INNER), _rows(SSM_INNER), _vec(SSM_INNER), _rows(SSM_INNER)],
        out_specs=[_rows(SSM_INNER), _rows(SSM_INNER), _vec(SSM_INNER)],
        out_shape=[jax.ShapeDtypeStruct((s, SSM_INNER), BF16), jax.ShapeDtypeStruct((s, SSM_INNER), BF16),
                   jax.ShapeDtypeStruct((1, SSM_INNER), F32)],
        compiler_params=_cp("arbitrary"))(y, z, w, do)


def _mix_fwd(gl, bg, attn_d, ssm_d):
    s = gl.shape[0]
    d = D_MODEL

    def body(gl_ref, bg_ref, a_ref, m_ref, o_ref):
        g0 = _sigmoid(gl_ref[:, :d] + bg_ref[:, :d])
        g1 = _sigmoid(gl_ref[:, d:] + bg_ref[:, d:])
        o_ref[...] = (g0 * a_ref[...] + g1 * m_ref[...]).astype(o_ref.dtype)

    return pl.pallas_call(body, name="mix_fwd", grid=(s // ROWS,),
                          in_specs=[_rows(2 * d), _vec(2 * d), _rows(d), _rows(d)], out_specs=_rows(d),
                          out_shape=jax.ShapeDtypeStruct((s, d), BF16), compiler_params=_cp("parallel"))(
        gl, bg, attn_d, ssm_d)


def _mix_bwd(gl, bg, attn_d, ssm_d, dmix):
    s = gl.shape[0]
    d = D_MODEL

    def body(gl_ref, bg_ref, a_ref, m_ref, dm_ref, da_ref, ds_ref, dg_ref, db_ref):
        @pl.when(pl.program_id(0) == 0)
        def _():
            db_ref[...] = jnp.zeros_like(db_ref)

        g0 = _sigmoid(gl_ref[:, :d] + bg_ref[:, :d])
        g1 = _sigmoid(gl_ref[:, d:] + bg_ref[:, d:])
        dm = dm_ref[...].astype(F32)
        da_ref[...] = (dm * g0).astype(da_ref.dtype)
        ds_ref[...] = (dm * g1).astype(ds_ref.dtype)
        dl0 = dm * a_ref[...] * g0 * (1.0 - g0)
        dl1 = dm * m_ref[...] * g1 * (1.0 - g1)
        dg_ref[:, :d] = dl0.astype(dg_ref.dtype)
        dg_ref[:, d:] = dl1.astype(dg_ref.dtype)
        db_ref[:, :d] += jnp.sum(dl0, axis=0, keepdims=True)
        db_ref[:, d:] += jnp.sum(dl1, axis=0, keepdims=True)

    return pl.pallas_call(
        body, name="mix_bwd", grid=(s // ROWS,),
        in_specs=[_rows(2 * d), _vec(2 * d), _rows(d), _rows(d), _rows(d)],
        out_specs=[_rows(d), _rows(d), _rows(2 * d), _vec(2 * d)],
        out_shape=[jax.ShapeDtypeStruct((s, d), BF16), jax.ShapeDtypeStruct((s, d), BF16),
                   jax.ShapeDtypeStruct((s, 2 * d), BF16), jax.ShapeDtypeStruct((1, 2 * d), F32)],
        compiler_params=_cp("arbitrary"))(gl, bg, attn_d, ssm_d, dmix)


def _ln_stats(p):
    mu = jnp.mean(p, axis=1, keepdims=True)
    c = p - mu
    rstd = lax.rsqrt(jnp.mean(c * c, axis=1, keepdims=True) + LN_EPS)
    return c * rstd, rstd


def _ln_bwd(dy, xhat, rstd, g):
    dxh = dy * g
    return rstd * (dxh - jnp.mean(dxh, axis=1, keepdims=True) - xhat * jnp.mean(dxh * xhat, axis=1, keepdims=True))


def _ln1_fwd(x, mixed, g, b):
    s, d = x.shape

    def body(x_ref, m_ref, g_ref, b_ref, o_ref, ob_ref):
        xhat, _ = _ln_stats(DEEPNORM_ALPHA * x_ref[...] + m_ref[...])
        y = xhat * g_ref[...] + b_ref[...]
        o_ref[...] = y
        ob_ref[...] = y.astype(BF16)

    return pl.pallas_call(body, name="ln1_fwd", grid=(s // ROWS,), in_specs=[_rows(d), _rows(d), _vec(d), _vec(d)],
                          out_specs=[_rows(d), _rows(d)],
                          out_shape=[jax.ShapeDtypeStruct((s, d), F32), jax.ShapeDtypeStruct((s, d), BF16)],
                          compiler_params=_cp("parallel"))(x, mixed, g, b)


def _ln2_loss(x1, h, target, g, b):
    s, d = x1.shape

    def body(x_ref, h_ref, t_ref, g_ref, b_ref, dp_ref, dpb_ref, loss_ref, dg_ref, db_ref):
        @pl.when(pl.program_id(0) == 0)
        def _():
            loss_ref[...] = jnp.zeros_like(loss_ref)
            dg_ref[...] = jnp.zeros_like(dg_ref)
            db_ref[...] = jnp.zeros_like(db_ref)

        xhat, rstd = _ln_stats(DEEPNORM_ALPHA * x_ref[...] + h_ref[...])
        err = xhat * g_ref[...] + b_ref[...] - t_ref[...]
        part = 0.5 * jnp.sum(jnp.mean(err * err, axis=1, keepdims=True), axis=0, keepdims=True)
        loss_ref[...] += jnp.broadcast_to(part, loss_ref.shape)
        dy = err * (1.0 / d)
        dg_ref[...] += jnp.sum(dy * xhat, axis=0, keepdims=True)
        db_ref[...] += jnp.sum(dy, axis=0, keepdims=True)
        dp = _ln_bwd(dy, xhat, rstd, g_ref[...])
        dp_ref[...] = dp
        dpb_ref[...] = dp.astype(BF16)

    return pl.pallas_call(
        body, name="ln2_loss", grid=(s // ROWS,), in_specs=[_rows(d), _rows(d), _rows(d), _vec(d), _vec(d)],
        out_specs=[_rows(d), _rows(d), _vec(LANES), _vec(d), _vec(d)],
        out_shape=[jax.ShapeDtypeStruct((s, d), F32), jax.ShapeDtypeStruct((s, d), BF16),
                   jax.ShapeDtypeStruct((1, LANES), F32),
                   jax.ShapeDtypeStruct((1, d), F32), jax.ShapeDtypeStruct((1, d), F32)],
        compiler_params=_cp("arbitrary"))(x1, h, target, g, b)


def _ln1_bwd(x, mixed, g, dpre2, dffn):
    s, d = x.shape

    def body(x_ref, m_ref, g_ref, d2_ref, df_ref, dp_ref, dr_ref, dg_ref, db_ref):
        @pl.when(pl.program_id(0) == 0)
        def _():
            dg_ref[...] = jnp.zeros_like(dg_ref)
            db_ref[...] = jnp.zeros_like(db_ref)

        xhat, rstd = _ln_stats(DEEPNORM_ALPHA * x_ref[...] + m_ref[...])
        dy = DEEPNORM_ALPHA * d2_ref[...] + df_ref[...]
        dg_ref[...] += jnp.sum(dy * xhat, axis=0, keepdims=True)
        db_ref[...] += jnp.sum(dy, axis=0, keepdims=True)
        dp = _ln_bwd(dy, xhat, rstd, g_ref[...])
        dp_ref[...] = dp.astype(BF16)
        dr_ref[...] = DEEPNORM_ALPHA * dp

    return pl.pallas_call(
        body, name="ln1_bwd", grid=(s // ROWS,), in_specs=[_rows(d), _rows(d), _vec(d), _rows(d), _rows(d)],
        out_specs=[_rows(d), _rows(d), _vec(d), _vec(d)],
        out_shape=[jax.ShapeDtypeStruct((s, d), BF16), jax.ShapeDtypeStruct((s, d), F32),
                   jax.ShapeDtypeStruct((1, d), F32), jax.ShapeDtypeStruct((1, d), F32)],
        compiler_params=_cp("arbitrary"))(x, mixed, g, dpre2, dffn)


def _swiglu_fwd(gu):
    s = gu.shape[0]
    f = FFN_HIDDEN

    def body(g_ref, u_ref, o_ref):
        gg = g_ref[...].astype(F32)
        o_ref[...] = (gg * _sigmoid(gg) * u_ref[...].astype(F32)).astype(o_ref.dtype)

    return pl.pallas_call(
        body, name="swiglu_fwd", grid=(s // ROWS,),
        in_specs=[pl.BlockSpec((ROWS, f), lambda i: (i, 0)), pl.BlockSpec((ROWS, f), lambda i: (i, 1))],
        out_specs=_rows(f), out_shape=jax.ShapeDtypeStruct((s, f), BF16), compiler_params=_cp("parallel"))(gu, gu)


def _swiglu_bwd(gu, dact):
    s = gu.shape[0]
    f = FFN_HIDDEN

    def body(g_ref, u_ref, d_ref, o_ref):
        sg, dsg = _silu_and_grad(g_ref[...].astype(F32))
        dd = d_ref[...].astype(F32)
        o_ref[:, :f] = (dd * u_ref[...].astype(F32) * dsg).astype(o_ref.dtype)
        o_ref[:, f:] = (dd * sg).astype(o_ref.dtype)

    return pl.pallas_call(
        body, name="swiglu_bwd", grid=(s // ROWS,),
        in_specs=[pl.BlockSpec((ROWS, f), lambda i: (i, 0)), pl.BlockSpec((ROWS, f), lambda i: (i, 1)), _rows(f)],
        out_specs=_rows(2 * f), out_shape=jax.ShapeDtypeStruct((s, 2 * f), BF16),
        compiler_params=_cp("parallel"))(gu, gu, dact)


def _peer(k):
    x, y, c = lax.axis_index("x"), lax.axis_index("y"), lax.axis_index("c")
    kx, ky, kc = (k >> 2) & 1, (k >> 1) & 1, k & 1
    px = (1 - x) if kx else x
    py = (1 - y) if ky else y
    pc = (1 - c) if kc else c
    return (px, py, pc), 4 * px + 2 * py + pc


def _my_index():
    return 4 * lax.axis_index("x") + 2 * lax.axis_index("y") + lax.axis_index("c")


def _comm_copies(ins, outs, sems, scatter):
    send_sems, recv_sems, local_sems = sems
    me = _my_index()
    copies = [pltpu.make_async_copy(ins[t].at[me] if scatter else ins[t], outs[t].at[me], local_sems.at[t])
              for t in range(len(ins))]
    for k in range(1, N_DEV):
        peer, pidx = _peer(k)
        for t in range(len(ins)):
            copies.append(pltpu.make_async_remote_copy(
                src_ref=ins[t].at[pidx] if scatter else ins[t], dst_ref=outs[t].at[me],
                send_sem=send_sems.at[t, k - 1], recv_sem=recv_sems.at[t, k - 1], device_id=peer,
                device_id_type=pl.DeviceIdType.MESH))
    return copies


def _comm_sems(n):
    return [pltpu.SemaphoreType.DMA((n, N_DEV - 1)), pltpu.SemaphoreType.DMA((n, N_DEV - 1)),
            pltpu.SemaphoreType.DMA((n,))]


def _comm_out_shapes(parts, scatter):
    return [jax.ShapeDtypeStruct(p.shape if scatter else (N_DEV,) + p.shape, p.dtype) for p in parts]


def _all_gather(parts):
    n = len(parts)

    def body(*refs):
        ins, outs = refs[:n], refs[n:2 * n]
        send_sems, recv_sems, local_sems = refs[2 * n:]
        x, y, c = lax.axis_index("x"), lax.axis_index("y"), lax.axis_index("c")
        me, sibling = (x, y, c), (x, y, 1 - c)
        chips = [(1 - x, y), (x, 1 - y), (1 - x, 1 - y)]

        def copy(t, k, block, to, src=None):
            dst = outs[t].at[4 * block[0] + 2 * block[1] + block[2]]
            return pltpu.make_async_remote_copy(
                src_ref=dst if src is None else src, dst_ref=dst, send_sem=send_sems.at[t, k],
                recv_sem=recv_sems.at[t, k], device_id=to, device_id_type=pl.DeviceIdType.MESH)

        mine = [pltpu.make_async_copy(ins[t], outs[t].at[_my_index()], local_sems.at[t]) for t in range(n)]
        for cp in mine:
            cp.start()
        first = [copy(t, 0, me, sibling, src=ins[t]) for t in range(n)]
        first += [copy(t, 1 + j, me, (*chip, c), src=ins[t]) for j, chip in enumerate(chips) for t in range(n)]
        for cp in first:
            cp.start()
        passed = []
        for j, chip in enumerate(chips):
            for t in range(n):
                copy(t, 1 + j, (*chip, c), me).wait_recv()
                passed.append(copy(t, 4 + j, (*chip, c), sibling))
                passed[-1].start()
        for t in range(n):
            copy(t, 0, sibling, me).wait_recv()
            for j, chip in enumerate(chips):
                copy(t, 4 + j, (*chip, 1 - c), me).wait_recv()
        for cp in first + passed:
            cp.wait_send()
        for cp in mine:
            cp.wait()

    anyspec = pl.BlockSpec(memory_space=pl.ANY)
    return pl.pallas_call(body, name="all_gather", in_specs=[anyspec] * n, out_specs=[anyspec] * n,
                          out_shape=_comm_out_shapes(parts, False), scratch_shapes=_comm_sems(n))(*parts)


def _remote_scatter_copies(ins, lands, send_sems, recv_sems):
    me = _my_index()
    copies = []
    for k in range(1, N_DEV):
        peer, pidx = _peer(k)
        for t in range(len(ins)):
            copies.append(pltpu.make_async_remote_copy(
                src_ref=ins[t].at[pidx], dst_ref=lands[t].at[me], send_sem=send_sems.at[t * (N_DEV - 1) + k - 1],
                recv_sem=recv_sems.at[t * (N_DEV - 1) + k - 1], device_id=peer, device_id_type=pl.DeviceIdType.MESH))
    return copies


def _landing_zones(parts):
    me = _my_index()
    return [jnp.where(lax.broadcasted_iota(jnp.int32, p.shape, 0) == me, p, jnp.zeros_like(p)) for p in parts]


_HBM = pl.BlockSpec(memory_space=pltpu.HBM)
_SEM = pl.BlockSpec(memory_space=pltpu.SEMAPHORE)


def _exchange_start(parts, lands):
    n = len(parts)

    def body(*refs):
        ins, lnd, send_sems, recv_sems, token = refs[:n], refs[n:2 * n], refs[2 * n], refs[2 * n + 1], refs[-1]
        for cp in _remote_scatter_copies(ins, lnd, send_sems, recv_sems):
            cp.start()
        token[...] = jnp.zeros_like(token)

    hbm = [pltpu.HBM(p.shape, p.dtype) for p in parts]
    outs = pl.pallas_call(
        body, name="exchange_start",
        out_shape=[pltpu.SemaphoreType.DMA((n * (N_DEV - 1),)), pltpu.SemaphoreType.DMA((n * (N_DEV - 1),))] + hbm + hbm
        + [jax.ShapeDtypeStruct((8, LANES), F32)],
        in_specs=[_HBM] * (2 * n), out_specs=[_SEM, _SEM] + [_HBM] * (2 * n) + [pl.BlockSpec(memory_space=pltpu.VMEM)],
        input_output_aliases={t: 2 + t for t in range(2 * n)},
        compiler_params=pltpu.CompilerParams(has_side_effects=pltpu.SideEffectType.DATAFLOW_SIDE_EFFECTING),
    )(*[pltpu.with_memory_space_constraint(p, pltpu.HBM) for p in list(parts) + list(lands)])
    return outs[0], outs[1], list(outs[2:2 + n]), list(outs[2 + n:2 + 2 * n]), outs[-1]


def _exchange_wait(send_sems, recv_sems, parts, lands, after):
    n = len(parts)

    def body(*refs):
        ins, lnd, send_sems, recv_sems = refs[:n], refs[n:2 * n], refs[2 * n], refs[2 * n + 1]
        for cp in _remote_scatter_copies(ins, lnd, send_sems, recv_sems):
            cp.wait_send()
            cp.wait_recv()

    hbm = [pltpu.HBM(p.shape, p.dtype) for p in parts]
    outs = pl.pallas_call(
        body, name="exchange_wait", out_shape=hbm + hbm,
        in_specs=[_HBM] * (2 * n) + [_SEM, _SEM, pl.BlockSpec(memory_space=pl.ANY)], out_specs=[_HBM] * (2 * n),
        input_output_aliases={t: t for t in range(2 * n)},
        compiler_params=pltpu.CompilerParams(has_side_effects=pltpu.SideEffectType.DATAFLOW_SIDE_EFFECTING),
    )(*parts, *lands, send_sems, recv_sems, after)
    return list(outs[n:])


def _adamw(recv, w, m, v, name):
    _, r, c = w.shape
    br = _tile(r, 128)
    c1 = 1.0 / (1.0 - ADAM_B1 ** ADAM_STEP)
    c2 = 1.0 / (1.0 - ADAM_B2 ** ADAM_STEP)

    def body(r_ref, w_ref, m_ref, v_ref, g_ref, d_ref, mo_ref, vo_ref):
        g = r_ref[0].astype(F32)
        for k in range(1, N_DEV):
            g = g + r_ref[k].astype(F32)
        mn = ADAM_B1 * m_ref[0] + (1.0 - ADAM_B1) * g
        vn = ADAM_B2 * v_ref[0] + (1.0 - ADAM_B2) * (g * g)
        g_ref[0] = g
        mo_ref[0] = mn
        vo_ref[0] = vn
        d_ref[0] = -ADAM_LR * ((mn * c1) / (jnp.sqrt(vn * c2) + ADAM_EPS) + ADAM_WD * w_ref[0])

    blk = pl.BlockSpec((1, br, c), lambda i: (0, i, 0))
    return pl.pallas_call(
        body, name=name, grid=(r // br,),
        in_specs=[pl.BlockSpec((N_DEV, br, c), lambda i: (0, i, 0)), blk, blk, blk],
        out_specs=[blk] * 4, out_shape=[jax.ShapeDtypeStruct((1, r, c), F32)] * 4,
        compiler_params=_cp("parallel"))(recv, w, m, v)


def _lane_row(pairs):
    row = jnp.zeros((LANES,), F32)
    for lane0, vec in pairs:
        row = lax.dynamic_update_slice(row, vec.astype(F32), (lane0,))
    return row.reshape(1, LANES)


def _stage_in(x, wts, small):
    s = x.shape[0]
    a = -jnp.exp(small["a_log"])
    bias_row = _lane_row([(DT_LANE0, small["dt_bias"]), (F_LANE0, small["b_forget"])])
    a_row = _lane_row([(DT_LANE0, a)])
    conv_b = small["conv_b"].reshape(1, -1)
    norm_w = small["ssm_norm_w"].reshape(1, -1)
    bg = small["b_gates"].reshape(1, -1)
    g1, b1 = small["ln1_g"].reshape(1, -1), small["ln1_b"].reshape(1, -1)
    g2, b2 = small["ln2_g"].reshape(1, -1), small["ln2_b"].reshape(1, -1)
    d_skip = small["d_skip"]
    xb = x.astype(BF16)

    qkv = _mm(xb, wts["qkv"], out_dtype=BF16, name="f_qkv")
    z = _mm(xb, wts["z"], out_dtype=BF16, name="f_z")
    xbc = _mm(xb, wts["xbc"], name="f_xbc")
    gl = _mm(xb, wts["gate"], out_dtype=BF16, name="f_gate")
    fd = _mm(xb, wts["fd"], name="f_fd")
    dt_c, ac_c, cf_c, dt_r, ac_r, cf_r = _stats_fwd(fd, bias_row, a_row)
    bk = _att_blocks(s)[1]
    ck4 = cf_r[F_LANE0:F_LANE0 + ATT_HEADS].reshape(N_HP, HP, s // bk, bk)
    return dict(locals())


def _stage_mid(c, attn, lse, wts, target):
    x, xb, qkv, z, xbc, gl, fd, ck4 = (c[k] for k in ("x", "xb", "qkv", "z", "xbc", "gl", "fd", "ck4"))
    dt_c, ac_c, dt_r, ac_r, a_row, bias_row = (c[k] for k in ("dt_c", "ac_c", "dt_r", "ac_r", "a_row", "bias_row"))
    conv_b, norm_w, bg, g1, b1, g2, b2, d_skip = (c[k] for k in ("conv_b", "norm_w", "bg", "g1", "b1", "g2", "b2",
                                                                "d_skip"))
    conv_w = c["wts"]["conv"]
    attn_d = _mm(attn, wts["pa"], out_dtype=BF16, name="f_pa")
    xact = _conv_fwd(xbc, conv_w, conv_b)
    dsk_pair = jnp.repeat(d_skip, SSM_HEAD_DIM).reshape(N_PAIR, LANES)
    y, hprev = _ssd_pair_fwd(xact, ac_c, dt_r, ac_r, dsk_pair)
    ssm = _gnorm_fwd(y, z, norm_w)
    ssm_d = _mm(ssm, wts["ps"], out_dtype=BF16, name="f_ps")
    mix = _mix_fwd(gl, bg, attn_d, ssm_d)
    mixed = _mm(mix, wts["out"], name="f_out")
    x1, x1_b = _ln1_fwd(x, mixed, g1, b1)
    gu = _mm(x1_b, wts["gu"], out_dtype=BF16, name="f_gu")
    act = _swiglu_fwd(gu)
    h = _mm(act, wts["down"], name="f_down")
    dpre2, dpre2_b, loss_row, dg2, db2 = _ln2_loss(x1, h, target, g2, b2)

    d_act = _mm(dpre2_b, wts["down"], tb=True, out_dtype=BF16, name="b_down_x")
    dw_down = _mm(act, dpre2_b, ta=True, name="b_down_w")
    dgu = _swiglu_bwd(gu, d_act)
    dffn = _mm(dgu, wts["gu"], tb=True, name="b_gu_x")
    dw_gu = _mm(x1_b, dgu, ta=True, name="b_gu_w")
    dpre1, dxr, dg1, db1 = _ln1_bwd(x, mixed, g1, dpre2, dffn)
    dmix = _mm(dpre1, wts["out"], tb=True, out_dtype=BF16, name="b_out_x")
    dw_out = _mm(mix, dpre1, ta=True, name="b_out_w")
    dattn_d, dssm_d, dgl, dbg = _mix_bwd(gl, bg, attn_d, ssm_d, dmix)
    dssm = _mm(dssm_d, wts["ps"], tb=True, out_dtype=BF16, name="b_ps_x")
    dw_ps = _mm(ssm, dssm_d, ta=True, name="b_ps_w")
    dattn = _mm(dattn_d, wts["pa"], tb=True, name="b_pa_x")
    dw_pa = _mm(attn, dattn_d, ta=True, name="b_pa_w")
    dy, dz, dnw = _gnorm_bwd(y, z, norm_w, dssm)
    dxact, ddt, da_row, dds_pair = _ssd_pair_bwd(xact, dt_c, ac_c, dt_r, ac_r, hprev, dy, dsk_pair, a_row)
    dds = dds_pair.reshape(SSM_HEADS, SSM_HEAD_DIM).sum(axis=1)
    dpre_c, dconv_w, dconv_b = _conv_bwd_pre(xbc, conv_w, conv_b, dxact)
    dxbc = _conv_bwd_in(dpre_c, conv_w)
    st, do_b = _att_prep(dattn, attn, lse)
    late = dict(pa=dw_pa, ps=dw_ps, out=dw_out, gu=dw_gu, down=dw_down)
    keep = ("st", "do_b", "ddt", "dxr", "dz", "dxbc", "dgl", "dconv_w", "dconv_b", "da_row", "dds", "dnw", "dbg",
            "dg1", "db1", "dg2", "db2", "loss_row")
    loc = locals()
    return {**c, **{k: loc[k] for k in keep}}, late


def _stage_out_w(c, att_grads):
    dq, dk, dv, dck, dcq = att_grads
    xb, fd, bias_row, ddt, dz, dxbc, dgl = (c[k] for k in ("xb", "fd", "bias_row", "ddt", "dz", "dxbc", "dgl"))
    s, a = xb.shape[0], c["a"]
    dcum = dck.reshape(ATT_HEADS, s) + dcq.reshape(ATT_HEADS, s)
    dfd, dbias = _stats_bwd(fd, bias_row, ddt, jnp.zeros((LANES, s), F32).at[F_LANE0:F_LANE0 + ATT_HEADS].set(dcum))
    dproj = (dq, dk, dv, dz, dxbc, dgl, dfd)
    dw_in = [_mm(xb, g_, ta=True, tb=(i == 0), name=f"b_in_w{i}") for i, g_ in enumerate(dproj)]
    grads = dict(q=dw_in[0], k=dw_in[1], v=dw_in[2], z=dw_in[3], xbc=dw_in[4], gate=dw_in[5], fd=dw_in[6],
                 conv=c["dconv_w"])
    small_g = dict(
        b_forget=dbias[0, F_LANE0:F_LANE0 + ATT_HEADS], conv_b=c["dconv_b"][0], dt_bias=dbias[0, :SSM_HEADS],
        a_log=c["da_row"][0, :SSM_HEADS] * a, d_skip=c["dds"], ssm_norm_w=c["dnw"][0], b_gates=c["dbg"][0],
        ln1_g=c["dg1"][0], ln1_b=c["db1"][0], ln2_g=c["dg2"][0], ln2_b=c["db2"][0])
    return c["loss_row"][0, 0], grads, small_g, dproj


def _stage_out_x(c, dproj, token):
    wts, d = c["wts"], D_MODEL
    wq = wts["qkv"][:, :d] + token.astype(BF16)
    wk, wv = wts["qkv"][:, d:2 * d], wts["qkv"][:, 2 * d:]
    dx = c["dxr"]
    for i, (g_, w_) in enumerate(zip(dproj, (wq, wk, wv, wts["z"], wts["xbc"], wts["gate"], wts["fd"]))):
        dx = _mm(g_, w_, ta=(i == 0), tb=True, add=dx, name=f"b_in_x{i}")
    return dx


BIG = ("w_in", "w_proj_attn", "w_proj_ssm", "w_out", "w_ffn_gate", "w_ffn_up", "w_ffn_down", "conv_w")
EARLY = ("w_in", "conv_w")
LATE = ("w_proj_attn", "w_proj_ssm", "w_out", "w_ffn_gate", "w_ffn_up", "w_ffn_down")
SMALL = ("b_forget", "conv_b", "dt_bias", "a_log", "d_skip", "ssm_norm_w", "b_gates", "ln1_g", "ln1_b", "ln2_g",
         "ln2_b")
SMALL_ROWS = 96
IN_SHARD = IN_WIDTH // N_DEV
IN_SEGMENTS = (("q", 0, 1024), ("k", 1024, 1024), ("v", 2048, 1024), ("f", 3072, ATT_HEADS), ("z", 3088, SSM_INNER),
               ("xbc", 5136, SSM_CONV_DIM), ("dt", 8208, SSM_HEADS), ("gate", 8240, 2 * D_MODEL))


def _cols_from_shards(shards, lo, hi):
    w = shards[0].shape[1]
    pieces = []
    for j in range(len(shards)):
        a, b = max(lo, j * w), min(hi, (j + 1) * w)
        if a < b:
            pieces.append(shards[j][:, a - j * w:b - j * w])
    return pieces[0] if len(pieces) == 1 else jnp.concatenate(pieces, axis=1)


def _shards_from_parts(parts, width):
    shards = []
    for j in range(N_DEV):
        lo, hi = j * width, (j + 1) * width
        pieces = []
        for mat, c0 in parts:
            a, b = max(lo, c0), min(hi, c0 + mat.shape[1])
            if a < b:
                pieces.append(mat[:, a - c0:b - c0])
        shards.append(pieces[0] if len(pieces) == 1 else jnp.concatenate(pieces, axis=1))
    return shards


def _pack_small(vals):
    flat = jnp.concatenate([vals[n].reshape(-1) for n in SMALL])
    return jnp.pad(flat, (0, SMALL_ROWS * LANES - flat.shape[0])).reshape(SMALL_ROWS, LANES)


def _unpack_small(pack, shapes):
    flat = pack.reshape(-1)
    out, off = {}, 0
    for n in SMALL:
        sz = math.prod(shapes[n])
        out[n] = flat[off:off + sz].reshape(shapes[n])
        off += sz
    return out


def kernel(x, w_in, b_forget, conv_w, conv_b, dt_bias, a_log, d_skip, ssm_norm_w, w_proj_attn, w_proj_ssm, b_gates, w_out, ln1_g, ln1_b, w_ffn_gate, w_ffn_up, w_ffn_down, ln2_g, ln2_b, loss_target, m_w_in, m_b_forget, m_conv_w, m_conv_b, m_dt_bias, m_a_log, m_d_skip, m_ssm_norm_w, m_w_proj_attn, m_w_proj_ssm, m_b_gates, m_w_out, m_ln1_g, m_ln1_b, m_w_ffn_gate, m_w_ffn_up, m_w_ffn_down, m_ln2_g, m_ln2_b, v_w_in, v_b_forget, v_conv_w, v_conv_b, v_dt_bias, v_a_log, v_d_skip, v_ssm_norm_w, v_w_proj_attn, v_w_proj_ssm, v_b_gates, v_w_out, v_ln1_g, v_ln1_b, v_w_ffn_gate, v_w_ffn_up, v_w_ffn_down, v_ln2_g, v_ln2_b):
    args = dict(locals())
    d, f = D_MODEL, FFN_HIDDEN
    big_w = {n: args[n][0] for n in BIG}
    small_w = {n: args[n][0] for n in SMALL}
    big_shapes = {n: args[n].shape for n in BIG}
    small_shapes = {n: args[n].shape for n in SMALL}

    early = dict(zip(EARLY, _all_gather([big_w["w_in"].astype(BF16), big_w["conv_w"]])))
    in_shards = [early["w_in"][j] for j in range(N_DEV)]
    seg = {n: _cols_from_shards(in_shards, c0, c0 + w) for n, c0, w in IN_SEGMENTS}
    wfd = jnp.concatenate([seg["dt"], seg["f"], jnp.zeros((d, LANES - SSM_HEADS - ATT_HEADS), BF16)], axis=1)
    wts = dict(qkv=jnp.concatenate([seg["q"], seg["k"], seg["v"]], axis=1), z=seg["z"], xbc=seg["xbc"],
               gate=seg["gate"], fd=wfd, conv=jnp.concatenate([early["conv_w"][j] for j in range(N_DEV)], axis=1))

    ctx = _stage_in(x[0], wts, small_w)
    attn, lse, gathered = _attention_fwd(ctx["qkv"], ctx["ck4"], [big_w[n].astype(BF16) for n in LATE])
    full = dict(zip(LATE, gathered))
    late_w = dict(
        pa=full["w_proj_attn"].reshape(d, d), ps=full["w_proj_ssm"].reshape(SSM_INNER, d),
        out=full["w_out"].reshape(d, d),
        gu=jnp.concatenate([full["w_ffn_gate"][j] for j in range(N_DEV)]
                           + [full["w_ffn_up"][j] for j in range(N_DEV)], axis=1),
        down=full["w_ffn_down"].reshape(f, d))
    ctx, gl = _stage_mid(ctx, attn, lse, late_w, loss_target[0])
    late_dest = dict(
        w_ffn_gate=jnp.stack([s_.astype(BF16) for s_ in _shards_from_parts([(gl["gu"][:, :f], 0)], f // N_DEV)]),
        w_ffn_up=jnp.stack([s_.astype(BF16) for s_ in _shards_from_parts([(gl["gu"][:, f:], 0)], f // N_DEV)]))
    for n, key in (("w_proj_attn", "pa"), ("w_proj_ssm", "ps"), ("w_out", "out"), ("w_ffn_down", "down")):
        late_dest[n] = gl[key].astype(BF16).reshape((N_DEV,) + big_shapes[n][1:])
    att_grads, late_recv = _attention_bwd(ctx["qkv"], ctx["ck4"], ctx["st"], ctx["do_b"], [late_dest[n] for n in LATE])
    loss_part, g, small_g, dproj = _stage_out_w(ctx, att_grads)
    loss = lax.psum(loss_part, ("x", "y", "c"))

    gfd = g["fd"]
    in_parts = dict(q=g["q"], k=g["k"], v=g["v"], f=gfd[:, F_LANE0:F_LANE0 + ATT_HEADS], z=g["z"], xbc=g["xbc"],
                    dt=gfd[:, DT_LANE0:DT_LANE0 + SSM_HEADS], gate=g["gate"])
    win_dest = jnp.stack([s_.astype(BF16) for s_ in
                          _shards_from_parts([(in_parts[n], c0) for n, c0, _ in IN_SEGMENTS], IN_SHARD)])
    conv_dest = jnp.stack(_shards_from_parts([(g["conv"], 0)], SSM_CONV_DIM // N_DEV))
    small_pack = _pack_small(small_g)
    last_parts = [win_dest, conv_dest, jnp.broadcast_to(small_pack, (N_DEV,) + small_pack.shape)]
    send_sems, recv_sems, parts_thru, lands_thru, token = _exchange_start(last_parts, _landing_zones(last_parts))
    grad_x = _stage_out_x(ctx, dproj, token[0, 0])
    early_recv = _exchange_wait(send_sems, recv_sems, parts_thru, lands_thru, grad_x)
    recv = dict(zip(LATE, late_recv))
    recv["w_in"], recv["conv_w"] = early_recv[0], early_recv[1]

    outs = {}
    for n in BIG:
        outs[n] = _adamw(recv[n], args[n], args["m_" + n], args["v_" + n], name="adamw_" + n)
    small4 = _adamw(early_recv[2], _pack_small(small_w)[None], _pack_small({n: args["m_" + n][0] for n in SMALL})[None],
                    _pack_small({n: args["v_" + n][0] for n in SMALL})[None], name="adamw_small")
    small_out = [_unpack_small(p, small_shapes) for p in small4]
    for n in SMALL:
        outs[n] = [so[n] for so in small_out]

    order = ("w_in", "b_forget", "conv_w", "conv_b", "dt_bias", "a_log", "d_skip", "ssm_norm_w", "w_proj_attn",
             "w_proj_ssm", "b_gates", "w_out", "ln1_g", "ln1_b", "w_ffn_gate", "w_ffn_up", "w_ffn_down", "ln2_g",
             "ln2_b")
    res = [loss, grad_x[None]]
    for i in range(4):
        res += [outs[n][i] for n in order]
    return tuple(res)
```

```python
import functools
import math

import jax
import jax.numpy as jnp
from jax import lax
from jax.experimental import pallas as pl
from jax.experimental.pallas import tpu as pltpu

F32 = jnp.float32
BF16 = jnp.bfloat16

N_DEV = 8
D_MODEL = 1024
ATT_HEADS = 16
ATT_HEAD_DIM = 64
SSM_INNER = 2048
SSM_HEADS = 32
SSM_HEAD_DIM = 64
SSM_GROUPS = 4
SSM_HEADS_PER_GROUP = 8
SSM_STATE = 128
SSM_CONV = 4
SSM_CHUNK = 128
SSM_CONV_DIM = 3072
FFN_HIDDEN = 2816
IN_WIDTH = 10288
DEEPNORM_ALPHA = 2.0 ** 0.25
LN_EPS = 1e-5
RMS_EPS = 1e-5
ADAM_LR, ADAM_B1, ADAM_B2, ADAM_EPS, ADAM_WD, ADAM_STEP = 0.001, 0.9, 0.999, 1e-08, 0.01, 10
ATT_SCALE = 1.0 / math.sqrt(ATT_HEAD_DIM)

LANES = 128
VMEM_LIMIT = 56 * 1024 * 1024
NEG = -1e30

DT_LANE0 = 0
F_LANE0 = 32
HI = lax.Precision.HIGHEST


def _cp(*sem):
    return pltpu.CompilerParams(dimension_semantics=sem, vmem_limit_bytes=VMEM_LIMIT)


def _tile(n, cap=1408):
    for t in (1408, 1024, 512, 384, 256, 128):
        if t <= cap and n % t == 0:
            return t
    return n


def _sigmoid(x):
    return 1.0 / (1.0 + jnp.exp(-x))


def _mm(a, b, *, ta=False, tb=False, out_dtype=F32, add=None, name):
    m, k = (a.shape[1], a.shape[0]) if ta else a.shape
    n = b.shape[0] if tb else b.shape[1]
    assert (b.shape[1] if tb else b.shape[0]) == k
    tm, tn, tk = _tile(m), _tile(n), _tile(k)
    nk = k // tk
    dims = (((0,) if ta else (1,), (1,) if tb else (0,)), ((), ()))

    def body_single(*refs):
        a_ref, b_ref = refs[:2]
        r = lax.dot_general(a_ref[...].astype(BF16), b_ref[...].astype(BF16), dims, preferred_element_type=F32)
        if add is not None:
            r = r + refs[2][...]
        refs[-1][...] = r.astype(refs[-1].dtype)

    def body(*refs):
        if add is None:
            a_ref, b_ref, o_ref, acc_ref = refs
        else:
            a_ref, b_ref, c_ref, o_ref, acc_ref = refs
        kk = pl.program_id(2)

        @pl.when(kk == 0)
        def _():
            acc_ref[...] = jnp.zeros_like(acc_ref)

        acc_ref[...] += lax.dot_general(a_ref[...].astype(BF16), b_ref[...].astype(BF16), dims,
                                        preferred_element_type=F32)

        @pl.when(kk == nk - 1)
        def _():
            r = acc_ref[...]
            if add is not None:
                r = r + c_ref[...]
            o_ref[...] = r.astype(o_ref.dtype)

    a_spec = pl.BlockSpec((tk, tm), lambda i, j, kk: (kk, i)) if ta else pl.BlockSpec((tm, tk), lambda i, j, kk: (i, kk))
    b_spec = pl.BlockSpec((tn, tk), lambda i, j, kk: (j, kk)) if tb else pl.BlockSpec((tk, tn), lambda i, j, kk: (kk, j))
    o_spec = pl.BlockSpec((tm, tn), lambda i, j, kk: (i, j))
    in_specs, args = [a_spec, b_spec], [a, b]
    if add is not None:
        in_specs.append(o_spec)
        args.append(add)
    return pl.pallas_call(
        body_single if nk == 1 else body, name=name, grid=(m // tm, n // tn, nk), in_specs=in_specs, out_specs=o_spec,
        out_shape=jax.ShapeDtypeStruct((m, n), out_dtype),
        scratch_shapes=[] if nk == 1 else [pltpu.VMEM((tm, tn), F32)],
        compiler_params=_cp("parallel", "parallel", "arbitrary"),
    )(*args)


def _tri(n, lower=True):
    r = lax.broadcasted_iota(jnp.int32, (n, n), 0)
    c = lax.broadcasted_iota(jnp.int32, (n, n), 1)
    return jnp.where((r >= c) if lower else (c >= r), 1.0, 0.0).astype(F32)


def _stats_fwd(fd, bias_row, a_row):
    s = fd.shape[0]
    blk = SSM_CHUNK

    def body(fd_ref, bias_ref, a_ref, dt_ref, ac_ref, cf_ref, dtr_ref, acr_ref, cfr_ref, carry_ref):
        @pl.when(pl.program_id(0) == 0)
        def _():
            carry_ref[...] = jnp.zeros_like(carry_ref)

        v = fd_ref[...] + bias_ref[...]
        dt = jnp.maximum(v, 0.0) + jnp.log(1.0 + jnp.exp(-jnp.abs(v)))
        lf = jnp.minimum(v, 0.0) - jnp.log(1.0 + jnp.exp(-jnp.abs(v)))
        tri = _tri(blk)
        ac = jnp.dot(tri, dt * a_ref[...], precision=HI, preferred_element_type=F32)
        cf = jnp.dot(tri, lf, precision=HI, preferred_element_type=F32) + carry_ref[0:1, :]
        carry_ref[...] = carry_ref[...] + jnp.sum(lf, axis=0, keepdims=True)
        dt_ref[...] = dt
        ac_ref[...] = ac
        cf_ref[...] = cf
        dtr_ref[...] = dt.T
        acr_ref[...] = ac.T
        cfr_ref[...] = cf.T

    col = pl.BlockSpec((blk, LANES), lambda i: (i, 0))
    row = pl.BlockSpec((LANES, blk), lambda i: (0, i))
    vec = pl.BlockSpec((1, LANES), lambda i: (0, 0))
    return pl.pallas_call(
        body, name="stats_fwd", grid=(s // blk,), in_specs=[col, vec, vec],
        out_specs=[col, col, col, row, row, row],
        out_shape=[jax.ShapeDtypeStruct((s, LANES), F32)] * 3 + [jax.ShapeDtypeStruct((LANES, s), F32)] * 3,
        scratch_shapes=[pltpu.VMEM((8, LANES), F32)],
        compiler_params=_cp("arbitrary"),
    )(fd, bias_row, a_row)


def _stats_bwd(fd, bias_row, ddt, dcum_rows):
    s = fd.shape[0]
    blk = SSM_CHUNK
    nb = s // blk

    def body(fd_ref, bias_ref, ddt_ref, dck_ref, o_ref, db_ref, carry_ref):
        @pl.when(pl.program_id(0) == 0)
        def _():
            carry_ref[...] = jnp.zeros_like(carry_ref)
            db_ref[...] = jnp.zeros_like(db_ref)

        v = fd_ref[...] + bias_ref[...]
        dcum = dck_ref[...].T
        dlf = jnp.dot(_tri(blk, lower=False), dcum, precision=HI, preferred_element_type=F32) + carry_ref[0:1, :]
        carry_ref[...] = carry_ref[...] + jnp.sum(dcum, axis=0, keepdims=True)
        lane = lax.broadcasted_iota(jnp.int32, v.shape, 1)
        g = jnp.where(lane < F_LANE0, ddt_ref[...] * _sigmoid(v), dlf * _sigmoid(-v))
        g = jnp.where(lane < F_LANE0 + ATT_HEADS, g, 0.0)
        o_ref[...] = g.astype(o_ref.dtype)
        db_ref[...] += jnp.sum(g, axis=0, keepdims=True)

    col = pl.BlockSpec((blk, LANES), lambda i: (nb - 1 - i, 0))
    row = pl.BlockSpec((LANES, blk), lambda i: (0, nb - 1 - i))
    vec = pl.BlockSpec((1, LANES), lambda i: (0, 0))
    return pl.pallas_call(
        body, name="stats_bwd", grid=(nb,), in_specs=[col, vec, col, row], out_specs=[col, vec],
        out_shape=[jax.ShapeDtypeStruct((s, LANES), BF16), jax.ShapeDtypeStruct((1, LANES), F32)],
        scratch_shapes=[pltpu.VMEM((8, LANES), F32)],
        compiler_params=_cp("arbitrary"),
    )(fd, bias_row, ddt, dcum_rows)


HP = LANES // ATT_HEAD_DIM
N_HP = ATT_HEADS // HP


def _att_blocks(s):
    return (512, 1024) if s % 1024 == 0 and s >= 4096 else (64, 128)


_QK = (((1,), (1,)), ((), ()))
_HALF = ATT_HEAD_DIM // 2
_PAD = 16


def _head_cols(a):
    return slice(a * ATT_HEAD_DIM, (a + 1) * ATT_HEAD_DIM)


def _causal(shape, off):
    r = lax.broadcasted_iota(jnp.int32, shape, 0)
    c = lax.broadcasted_iota(jnp.int32, shape, 1)
    return c <= r + off


def _attention_fwd(qkv, ck4, gather_parts):
    s = qkv.shape[0]
    bk = _att_blocks(s)[1]
    bq = bk
    nq, nk = s // bq, s // bk
    n = len(gather_parts)

    def body(q_ref, k_ref, v_ref, ck_ref, *rest):
        comm_in, (o_ref, lse_ref), comm_out, sems = rest[:n], rest[n:n + 2], rest[n + 2:2 * n + 2], rest[2 * n + 2:]
        i = pl.program_id(1)
        if n:
            @pl.when((pl.program_id(0) == 0) & (i == 0))
            def _():
                for cp in _comm_copies(comm_in, comm_out, sems, False):
                    cp.start()

        n_full = (i * bq) // bk
        qs = [(q_ref[:, _head_cols(a)].astype(F32) * ATT_SCALE).astype(BF16) for a in range(HP)]

        upper_q = lax.broadcasted_iota(jnp.int32, (bq, LANES), 1) >= ATT_HEAD_DIM

        def absorb(j, carry, keys=slice(0, bk), rows=slice(0, bq), masked=False):
            nk_ = keys.stop - keys.start
            ks = pl.ds(pl.multiple_of(j * bk, bk) + keys.start, nk_)
            v_both = v_ref[ks, :]
            out = []
            for a in range(HP):
                m, acc = carry[a]
                sc = lax.dot_general(qs[a][rows], k_ref[ks, _head_cols(a)], _QK, preferred_element_type=F32)
                sc = sc - ck_ref[0, a, pl.ds(j, 1), keys]
                if masked:
                    sc = jnp.where(_causal(sc.shape, 0), sc, NEG)
                m_new = jnp.maximum(m, jnp.max(sc, axis=1, keepdims=True))
                p = jnp.exp((sc - m_new).astype(BF16))
                upper_v = lax.broadcasted_iota(jnp.int32, (nk_, LANES), 1) >= ATT_HEAD_DIM
                v_aug = jnp.where(upper_v == (a == 1), v_both, jnp.ones_like(v_both))
                acc = jnp.exp(m - m_new) * acc + jnp.dot(p, v_aug, preferred_element_type=F32)
                out.append((m_new, acc))
            return tuple(out)

        init = tuple((jnp.full((bq, 1), NEG, F32), jnp.zeros((bq, LANES), F32)) for _ in range(HP))
        carry = lax.fori_loop(0, n_full, absorb, init)
        hq = bq // 2
        carry = absorb(n_full, carry, keys=slice(0, hq), masked=True)
        low = absorb(n_full, tuple((m[hq:], acc[hq:]) for m, acc in carry), keys=slice(hq, bk), rows=slice(hq, bq),
                     masked=True)
        carry = tuple((jnp.concatenate([m[:hq], ml], axis=0), jnp.concatenate([acc[:hq], al], axis=0))
                      for (m, acc), (ml, al) in zip(carry, low))
        outs, lses = [], []
        for a in range(HP):
            m, acc = carry[a]
            l = pltpu.roll(acc, ATT_HEAD_DIM, 1)
            outs.append(acc / l)
            lses.append(m + jnp.log(l))
        o_ref[...] = jnp.where(upper_q, outs[1], outs[0])
        lse_ref[...] = jnp.where(upper_q, lses[1], lses[0])
        if n:
            @pl.when((pl.program_id(0) == N_HP - 1) & (i == nq - 1))
            def _():
                for cp in _comm_copies(comm_in, comm_out, sems, False):
                    cp.wait()

    q_spec = pl.BlockSpec((bq, LANES), lambda h, i: (i, h))
    anyspec = pl.BlockSpec(memory_space=pl.ANY)
    res = pl.pallas_call(
        body, name="att_fwd", grid=(N_HP, nq),
        in_specs=[q_spec, pl.BlockSpec((s, LANES), lambda h, i: (0, N_HP + h)),
                  pl.BlockSpec((s, LANES), lambda h, i: (0, 2 * N_HP + h)),
                  pl.BlockSpec((1, HP, nk, bk), lambda h, i: (h, 0, 0, 0))] + [anyspec] * n,
        out_specs=[q_spec, q_spec] + [anyspec] * n,
        out_shape=[jax.ShapeDtypeStruct((s, D_MODEL), F32)] * 2 + _comm_out_shapes(gather_parts, False),
        scratch_shapes=_comm_sems(n) if n else [],
        compiler_params=_cp("arbitrary", "arbitrary"),
    )(qkv, qkv, qkv, ck4, *gather_parts)
    return res[0], res[1], list(res[2:])


def _att_prep(do, o, lse_rep):
    s = do.shape[0]
    bs = _tile(s, 512)

    def body(do_ref, o_ref, lse_ref, st_ref, dob_ref):
        r = lax.broadcasted_iota(jnp.int32, (LANES, LANES), 0) // ATT_HEAD_DIM
        c = lax.broadcasted_iota(jnp.int32, (LANES, LANES), 1) // ATT_HEAD_DIM
        e = jnp.where(r == c, 1.0, 0.0).astype(F32)
        lane = lax.broadcasted_iota(jnp.int32, (bs, LANES), 1)
        for p in range(D_MODEL // LANES):
            cs = slice(p * LANES, (p + 1) * LANES)
            dd = do_ref[:, cs]
            delta = jnp.dot(dd * o_ref[:, cs], e, precision=HI, preferred_element_type=F32)
            st_ref[:, cs] = jnp.where(lane % ATT_HEAD_DIM < _HALF, lse_ref[:, cs], delta)
            dob_ref[:, cs] = dd.astype(BF16)

    spec = pl.BlockSpec((bs, D_MODEL), lambda i: (i, 0))
    return pl.pallas_call(body, name="att_prep", grid=(s // bs,), in_specs=[spec, spec, spec], out_specs=[spec, spec],
                          out_shape=[jax.ShapeDtypeStruct((s, D_MODEL), F32), jax.ShapeDtypeStruct((s, D_MODEL), BF16)],
                          compiler_params=_cp("parallel"))(do, o, lse_rep)


def _attention_bwd(qkv, ck4, st, do_b, exchange_parts):
    s = qkv.shape[0]
    bq, bk = _att_blocks(s)
    nq, nk, per = s // bq, s // bk, bk // bq
    _T = (((0,), (0,)), ((), ()))
    n = len(exchange_parts)

    def body(q_ref, k_ref, v_ref, ck_ref, st_ref, do_ref, *rest):
        comm_in, (dq_ref, dk_ref, dv_ref, dck_ref, dcq_ref) = rest[:n], rest[n:n + 5]
        comm_out, sems, (dk_acc, dv_acc) = rest[n + 5:2 * n + 5], rest[2 * n + 5:-2], rest[-2:]
        j = pl.program_id(1)
        if n:
            @pl.when((pl.program_id(0) == 0) & (j == 0))
            def _():
                for cp in _comm_copies(comm_in, comm_out, sems, True):
                    cp.start()

        @pl.when(j == 0)
        def _():
            dq_ref[...] = jnp.zeros_like(dq_ref)
            dcq_ref[...] = jnp.zeros_like(dcq_ref)

        dk_acc[...] = jnp.zeros_like(dk_acc)
        dv_acc[...] = jnp.zeros_like(dv_acc)

        ones_q, ones_k = jnp.ones((_PAD, bq), BF16), jnp.ones((_PAD, bk), BF16)
        k_t = [jnp.concatenate([k_ref[:, _head_cols(a)].T, ones_k], axis=0) for a in range(HP)]

        def step(i, off=None, kl=slice(0, bk)):
            rows = pl.ds(pl.multiple_of(i * bq, bq), bq)
            for a in range(HP):
                cs = _head_cols(a)
                q = (q_ref[rows, cs].astype(F32) * ATT_SCALE).astype(BF16)
                do_a = do_ref[rows, cs]
                sc = lax.dot_general(q, k_ref[kl, cs], _QK, preferred_element_type=F32) - ck_ref[0, a, pl.ds(j, 1), kl]
                if off is not None:
                    sc = jnp.where(_causal(sc.shape, off), sc, NEG)
                p = jnp.exp(sc - st_ref[rows, a * ATT_HEAD_DIM:a * ATT_HEAD_DIM + 1])
                dp = lax.dot_general(do_a, v_ref[kl, cs], _QK, preferred_element_type=F32)
                ds = p * (dp - st_ref[rows, a * ATT_HEAD_DIM + _HALF:a * ATT_HEAD_DIM + _HALF + 1])
                ds_b = ds.astype(BF16)
                dv_acc[a, :, kl] += jnp.dot(do_a.T, p.astype(BF16), preferred_element_type=F32)
                dk_acc[a, :, kl] += jnp.dot(jnp.concatenate([q.T, ones_q], axis=0), ds_b, preferred_element_type=F32)
                dqs = lax.dot_general(k_t[a][:, kl], ds_b, _QK, preferred_element_type=F32)
                dq_ref[cs, rows] += dqs[:ATT_HEAD_DIM] * ATT_SCALE
                dcq_ref[0, a, pl.ds(i, 1), :] += jnp.sum(dqs[ATT_HEAD_DIM:ATT_HEAD_DIM + 8], axis=0,
                                                         keepdims=True) * 0.125

        for t in range(per):
            step(j * per + t, off=t * bq, kl=slice(0, (t + 1) * bq))

        def full(i, c):
            step(i)
            return c

        lax.fori_loop((j + 1) * per, nq, full, 0)
        for a in range(HP):
            dk_ref[:, _head_cols(a)] = dk_acc[a, :ATT_HEAD_DIM].T.astype(dk_ref.dtype)
            dv_ref[:, _head_cols(a)] = dv_acc[a].T.astype(dv_ref.dtype)
            dck_ref[0, a, pl.ds(j, 1), :] = -dk_acc[a, ATT_HEAD_DIM:ATT_HEAD_DIM + 1]
        if n:
            @pl.when((pl.program_id(0) == N_HP - 1) & (j == nk - 1))
            def _():
                for cp in _comm_copies(comm_in, comm_out, sems, True):
                    cp.wait()

    res = pl.BlockSpec((s, LANES), lambda h, j: (0, h))
    ck_spec = pl.BlockSpec((1, HP, nk, bk), lambda h, j: (h, 0, 0, 0))
    kout = pl.BlockSpec((bk, LANES), lambda h, j: (j, h))
    anyspec = pl.BlockSpec(memory_space=pl.ANY)
    outs = pl.pallas_call(
        body, name="att_bwd", grid=(N_HP, nk),
        in_specs=[res, pl.BlockSpec((bk, LANES), lambda h, j: (j, N_HP + h)),
                  pl.BlockSpec((bk, LANES), lambda h, j: (j, 2 * N_HP + h)), ck_spec, res, res] + [anyspec] * n,
        out_specs=[pl.BlockSpec((LANES, s), lambda h, j: (h, 0)), kout, kout, ck_spec,
                   pl.BlockSpec((1, HP, nq, bq), lambda h, j: (h, 0, 0, 0))] + [anyspec] * n,
        out_shape=[jax.ShapeDtypeStruct((D_MODEL, s), F32), jax.ShapeDtypeStruct((s, D_MODEL), BF16),
                   jax.ShapeDtypeStruct((s, D_MODEL), BF16), jax.ShapeDtypeStruct((N_HP, HP, nk, bk), F32),
                   jax.ShapeDtypeStruct((N_HP, HP, nq, bq), F32)] + _comm_out_shapes(exchange_parts, True),
        scratch_shapes=(_comm_sems(n) if n else [])
        + [pltpu.VMEM((HP, ATT_HEAD_DIM + _PAD, bk), F32), pltpu.VMEM((HP, ATT_HEAD_DIM, bk), F32)],
        compiler_params=_cp("arbitrary", "arbitrary"),
    )(qkv, qkv, qkv, ck4, st, do_b, *exchange_parts)
    return outs[:5], list(outs[5:])


def _silu_and_grad(x):
    sg = _sigmoid(x)
    return x * sg, sg * (1.0 + x * (1.0 - sg))


SUBLANES = 8


def _conv_taps(cur, before, w_rows, bias):
    n, c = cur.shape
    cur3 = cur.reshape(n // SUBLANES, SUBLANES, c)
    sub = lax.broadcasted_iota(jnp.int32, (1, SUBLANES, c), 1)
    taps = []
    for k in range(SSM_CONV):
        sh = SSM_CONV - 1 - k
        if sh == 0:
            taps.append(cur3)
            continue
        rot = pltpu.roll(cur3, sh, 1)
        prev = jnp.concatenate([pltpu.roll(before, sh, 0)[None], rot[:-1]], axis=0)
        taps.append(jnp.where(sub < sh, prev, rot))
    pre = bias[None] + sum(w_rows[k][None] * taps[k] for k in range(SSM_CONV))
    return pre.reshape(n, c), [t.reshape(n, c) for t in taps]


def _conv_col_chunks(w_ref, b_ref, bc, cc):
    for c0 in range(0, bc, cc):
        cols = slice(c0, c0 + cc)
        yield cols, [w_ref[k:k + 1, cols] for k in range(SSM_CONV)], None if b_ref is None else b_ref[:, cols]


def _conv_specs(s, bs, bc):
    cur = pl.BlockSpec((bs, bc), lambda j, i: (i, j))
    halo = pl.BlockSpec((8, bc), lambda j, i: (jnp.maximum(i * (bs // 8) - 1, 0), j))
    w = pl.BlockSpec((SSM_CONV, bc), lambda j, i: (0, j))
    b = pl.BlockSpec((1, bc), lambda j, i: (0, j))
    return cur, halo, w, b


def _conv_fwd(xbc, w, b):
    s, c = xbc.shape
    bs, bc = _tile(s, 512), 1024
    rc, cc = bs, 128

    def body(x_ref, h_ref, w_ref, b_ref, o_ref):
        first = pl.program_id(1) == 0
        for cols, w_rows, bias in _conv_col_chunks(w_ref, b_ref, bc, cc):
            def step(r, before):
                rows = pl.ds(pl.multiple_of(r * rc, rc), rc)
                cur = x_ref[rows, cols]
                pre, _ = _conv_taps(cur, before, w_rows, bias)
                o_ref[rows, cols] = pre * _sigmoid(pre)
                return cur[rc - SUBLANES:]

            lax.fori_loop(0, bs // rc, step, jnp.where(first, 0.0, h_ref[:, cols]))

    cur, halo, ws, bsp = _conv_specs(s, bs, bc)
    return pl.pallas_call(body, name="conv_fwd", grid=(c // bc, s // bs), in_specs=[cur, halo, ws, bsp],
                          out_specs=cur, out_shape=jax.ShapeDtypeStruct((s, c), F32),
                          compiler_params=_cp("parallel", "parallel"))(xbc, xbc, w, b)


def _conv_bwd_pre(xbc, w, b, dact):
    s, c = xbc.shape
    bs, bc = _tile(s, 512), 1024
    rc, cc = _tile(bs, 256), 128

    def body(x_ref, h_ref, w_ref, b_ref, g_ref, dp_ref, dw_ref, db_ref):
        first = pl.program_id(1) == 0

        @pl.when(first)
        def _():
            dw_ref[...] = jnp.zeros_like(dw_ref)
            db_ref[...] = jnp.zeros_like(db_ref)

        for cols, w_rows, bias in _conv_col_chunks(w_ref, b_ref, bc, cc):
            def step(r, carry):
                before, sums = carry
                rows = pl.ds(pl.multiple_of(r * rc, rc), rc)
                cur = x_ref[rows, cols]
                pre, taps = _conv_taps(cur, before, w_rows, bias)
                dpre = g_ref[rows, cols].astype(F32) * _silu_and_grad(pre)[1]
                dp_ref[rows, cols] = dpre.astype(dp_ref.dtype)
                terms = [dpre * t for t in taps] + [dpre]
                sums = tuple(a + jnp.sum(t.reshape(rc // SUBLANES, SUBLANES, cc), axis=0) for a, t in zip(sums, terms))
                return cur[rc - SUBLANES:], sums

            zero = jnp.zeros((SUBLANES, cc), F32)
            _, sums = lax.fori_loop(0, bs // rc, step,
                                    (jnp.where(first, 0.0, h_ref[:, cols]), (zero,) * (SSM_CONV + 1)))
            for k in range(SSM_CONV):
                dw_ref[k:k + 1, cols] += jnp.sum(sums[k], axis=0, keepdims=True)
            db_ref[:, cols] += jnp.sum(sums[SSM_CONV], axis=0, keepdims=True)

    cur, halo, ws, bsp = _conv_specs(s, bs, bc)
    return pl.pallas_call(
        body, name="conv_bwd_pre", grid=(c // bc, s // bs), in_specs=[cur, halo, ws, bsp, cur],
        out_specs=[cur, ws, bsp],
        out_shape=[jax.ShapeDtypeStruct((s, c), BF16), jax.ShapeDtypeStruct((SSM_CONV, c), F32),
                   jax.ShapeDtypeStruct((1, c), F32)],
        compiler_params=_cp("parallel", "arbitrary"))(xbc, xbc, w, b, dact)


def _conv_bwd_in(dpre, w):
    s, c = dpre.shape
    bs, bc = _tile(s, 512), 1024
    nb = s // bs
    rc, cc = bs, 256
    nr = bs // rc

    def body(g_ref, n_ref, w_ref, o_ref):
        last = pl.program_id(1) == nb - 1
        sub = lax.broadcasted_iota(jnp.int32, (1, SUBLANES, cc), 1)
        for cols, w_rows, _ in _conv_col_chunks(w_ref, None, bc, cc):
            def step(i, after):
                rows = pl.ds(pl.multiple_of((nr - 1 - i) * rc, rc), rc)
                cur = g_ref[rows, cols].astype(F32)
                cur3 = cur.reshape(rc // SUBLANES, SUBLANES, cc)
                acc = w_rows[SSM_CONV - 1][None] * cur3
                for sh in range(1, SSM_CONV):
                    rot = pltpu.roll(cur3, SUBLANES - sh, 1)
                    nxt = jnp.concatenate([rot[1:], pltpu.roll(after, SUBLANES - sh, 0)[None]], axis=0)
                    acc = acc + w_rows[SSM_CONV - 1 - sh][None] * jnp.where(sub >= SUBLANES - sh, nxt, rot)
                o_ref[rows, cols] = acc.reshape(rc, cc).astype(o_ref.dtype)
                return cur[0:SUBLANES]

            lax.fori_loop(0, nr, step, jnp.where(last, 0.0, n_ref[0:SUBLANES, cols].astype(F32)))

    cur = pl.BlockSpec((bs, bc), lambda j, i: (i, j))
    nxt = pl.BlockSpec((16, bc), lambda j, i: (jnp.minimum((i + 1) * (bs // 16), s // 16 - 1), j))
    ws = pl.BlockSpec((SSM_CONV, bc), lambda j, i: (0, j))
    return pl.pallas_call(body, name="conv_bwd_in", grid=(c // bc, nb), in_specs=[cur, nxt, ws], out_specs=cur,
                          out_shape=jax.ShapeDtypeStruct((s, c), BF16),
                          compiler_params=_cp("parallel", "parallel"))(dpre, dpre, w)


def _dotT(a, b):
    return lax.dot_general(a.astype(BF16), b.astype(BF16), (((1,), (1,)), ((), ())), preferred_element_type=F32)


def _dot(a, b):
    return jnp.dot(a.astype(BF16), b.astype(BF16), preferred_element_type=F32)


N_PAIR = SSM_HEADS // HP
PAIRS_PER_GROUP = SSM_HEADS_PER_GROUP // HP


def _pair_consts():
    L = SSM_CHUNK
    lane = lax.broadcasted_iota(jnp.int32, (L, LANES), 1)
    lane1 = lax.broadcasted_iota(jnp.int32, (1, LANES), 1)
    li = lax.broadcasted_iota(jnp.int32, (L, L), 0)
    si = lax.broadcasted_iota(jnp.int32, (L, L), 1)
    return lane >= ATT_HEAD_DIM, lane1 >= ATT_HEAD_DIM, li, si


def _ssd_pair_fwd(xbc_act, ac_c, dt_r, ac_r, dsk_pair):
    s = xbc_act.shape[0]
    L, N, G = SSM_CHUNK, SSM_STATE, SSM_GROUPS
    nc = s // L

    def body(xbc_ref, ac_ref, dtr_ref, acr_ref, dsk_ref, y_ref, hp_ref, st_ref):
        @pl.when(pl.program_id(0) == 0)
        def _():
            st_ref[...] = jnp.zeros_like(st_ref)

        upper, up1, li, si = _pair_consts()
        for g in range(G):
            b_g = xbc_ref[:, SSM_INNER + g * N:SSM_INNER + (g + 1) * N]
            c_g = xbc_ref[:, SSM_INNER + G * N + g * N:SSM_INNER + G * N + (g + 1) * N]
            cb = _dotT(c_g, b_g)
            b_t = b_g.T
            for q in range(PAIRS_PER_GROUP):
                pp = g * PAIRS_PER_GROUP + q
                cols = slice(pp * LANES, (pp + 1) * LANES)
                xs = xbc_ref[:, cols]
                ht = st_ref[pp]
                hp_ref[0, pp] = ht
                y = dsk_ref[pp:pp + 1, :] * xs
                s_new = jnp.zeros((N, LANES), F32)
                ea, el = [], []
                for a in range(HP):
                    h = HP * pp + a
                    acol = jnp.broadcast_to(ac_ref[:, h:h + 1], (L, LANES))
                    arow, dtrow = acr_ref[h:h + 1, :], dtr_ref[h:h + 1, :]
                    alast = ac_ref[L - 1:L, h:h + 1]
                    decay = jnp.exp(jnp.where(li >= si, acol - arow, NEG))
                    xs_a = jnp.where(upper == (a == 1), xs, 0.0)
                    y = y + _dot(cb * decay * dtrow, xs_a)
                    s_new = s_new + _dot(b_t * (dtrow * jnp.exp(alast - arow)), xs_a)
                    ea.append(jnp.exp(acol))
                    el.append(jnp.exp(alast))
                y_ref[:, cols] = y + jnp.where(upper, ea[1], ea[0]) * _dot(c_g, ht)
                st_ref[pp] = ht * jnp.where(up1, el[1], el[0]) + s_new

    col = pl.BlockSpec((L, LANES), lambda c: (c, 0))
    row = pl.BlockSpec((LANES, L), lambda c: (0, c))
    return pl.pallas_call(
        body, name="ssd_fwd", grid=(nc,),
        in_specs=[pl.BlockSpec((L, SSM_CONV_DIM), lambda c: (c, 0)), col, row, row,
                  pl.BlockSpec((N_PAIR, LANES), lambda c: (0, 0))],
        out_specs=[pl.BlockSpec((L, SSM_INNER), lambda c: (c, 0)),
                   pl.BlockSpec((1, N_PAIR, N, LANES), lambda c: (c, 0, 0, 0))],
        out_shape=[jax.ShapeDtypeStruct((s, SSM_INNER), F32), jax.ShapeDtypeStruct((nc, N_PAIR, N, LANES), F32)],
        scratch_shapes=[pltpu.VMEM((N_PAIR, N, LANES), F32)],
        compiler_params=_cp("arbitrary"),
    )(xbc_act, ac_c, dt_r, ac_r, dsk_pair)


def _ssd_pair_bwd(xbc_act, dt_c, ac_c, dt_r, ac_r, hprev_all, dy, dsk_pair, a_row):
    s = xbc_act.shape[0]
    L, N, G = SSM_CHUNK, SSM_STATE, SSM_GROUPS
    nc = s // L
    rev = lambda c: nc - 1 - c

    def body(xbc_ref, dt_ref, ac_ref, dtr_ref, acr_ref, hp_ref, dy_ref, dsk_ref, arow_ref,
             dx_ref, ddt_ref, da_ref, dds_ref, dh_ref):
        @pl.when(pl.program_id(0) == 0)
        def _():
            dh_ref[...] = jnp.zeros_like(dh_ref)
            da_ref[...] = jnp.zeros_like(da_ref)
            dds_ref[...] = jnp.zeros_like(dds_ref)

        upper, up1, li, si = _pair_consts()
        lane = lax.broadcasted_iota(jnp.int32, (L, LANES), 1)
        sub = lax.broadcasted_iota(jnp.int32, (LANES, L), 0)
        lastrow = lax.broadcasted_iota(jnp.int32, (L, LANES), 0) == L - 1
        da_c = jnp.zeros((L, LANES), F32)
        da_r = jnp.zeros((LANES, L), F32)
        ddt_r = jnp.zeros((LANES, L), F32)
        for g in range(G):
            b_g = xbc_ref[:, SSM_INNER + g * N:SSM_INNER + (g + 1) * N]
            c_g = xbc_ref[:, SSM_INNER + G * N + g * N:SSM_INNER + G * N + (g + 1) * N]
            cb, cb_t = _dotT(c_g, b_g), _dotT(b_g, c_g)
            b_t, c_t = b_g.T, c_g.T
            dcb = jnp.zeros((L, L), F32)
            db_t = jnp.zeros((N, L), F32)
            dc = jnp.zeros((L, N), F32)
            for q in range(PAIRS_PER_GROUP):
                pp = g * PAIRS_PER_GROUP + q
                cols = slice(pp * LANES, (pp + 1) * LANES)
                xs, gy = xbc_ref[:, cols], dy_ref[:, cols].astype(F32)
                ht, dhn = hp_ref[0, pp], dh_ref[pp]
                acol = [jnp.broadcast_to(ac_ref[:, HP * pp + a:HP * pp + a + 1], (L, LANES)) for a in range(HP)]
                alast = [ac_ref[L - 1:L, HP * pp + a:HP * pp + a + 1] for a in range(HP)]
                ea = jnp.where(upper, jnp.exp(acol[1]), jnp.exp(acol[0]))
                el = jnp.where(up1, jnp.exp(alast[1]), jnp.exp(alast[0]))
                ge = gy * ea
                dc = dc + _dotT(ge, ht)
                dh_ref[pp] = _dot(c_t, ge) + dhn * el
                t_off = (ge * _dot(c_g, ht)).astype(BF16)
                hsum = jnp.sum(dhn * ht, axis=0, keepdims=True)
                dxs = dsk_ref[pp:pp + 1, :] * gy
                dds_ref[pp:pp + 1, :] += jnp.sum(gy * xs, axis=0, keepdims=True)
                for a in range(HP):
                    h = HP * pp + a
                    mine, mine1 = upper == (a == 1), up1 == (a == 1)
                    arow, dtrow = acr_ref[h:h + 1, :], dtr_ref[h:h + 1, :]
                    dtcol = jnp.broadcast_to(dt_ref[:, h:h + 1], (L, LANES))
                    xs_a, gy_a = jnp.where(mine, xs, 0.0), jnp.where(mine, gy, 0.0)
                    dhn_a = jnp.where(mine1, dhn, 0.0)
                    e_row = jnp.exp(alast[a] - arow)
                    w_row = dtrow * e_row
                    xd_t = _dotT(dhn_a, xs_a)
                    db_t = db_t + xd_t * w_row
                    dw = jnp.sum(b_t * xd_t, axis=0, keepdims=True)
                    de_e = dw * w_row
                    dal = (jnp.sum(jnp.where(mine1, hsum, 0.0), axis=1, keepdims=True) * jnp.exp(alast[a])
                           + jnp.sum(de_e, axis=1, keepdims=True))
                    dxs = dxs + _dot(b_g, dhn_a) * (dtcol * jnp.exp(alast[a] - acol[a]))
                    decay = jnp.exp(jnp.where(li >= si, acol[a] - arow, NEG))
                    decay_t = jnp.exp(jnp.where(si >= li, arow - acol[a], NEG))
                    m = cb * decay
                    dmdt = _dotT(gy_a, xs_a)
                    dxs = dxs + _dot(cb_t * decay_t * dtcol, gy_a)
                    dm = dmdt * dtrow
                    dcb = dcb + dm * decay
                    wb = (dm * m).astype(BF16)
                    onehot = jnp.where(lane == h, 1.0, 0.0).astype(BF16)
                    da_c = (da_c + jnp.dot(wb, onehot, preferred_element_type=F32)
                            + jnp.dot(jnp.where(mine, t_off, 0.0).astype(BF16), onehot, preferred_element_type=F32)
                            + jnp.where(lastrow & (lane == h), dal, 0.0))
                    da_r = jnp.where(sub == h, -(jnp.sum(wb.astype(F32), axis=0, keepdims=True) + de_e), da_r)
                    ddt_r = jnp.where(sub == h, dw * e_row + jnp.sum(dmdt * m, axis=0, keepdims=True), ddt_r)
                dx_ref[:, cols] = dxs.astype(dx_ref.dtype)
            dx_ref[:, SSM_INNER + g * N:SSM_INNER + (g + 1) * N] = (db_t + _dot(c_t, dcb)).T.astype(dx_ref.dtype)
            dx_ref[:, SSM_INNER + G * N + g * N:SSM_INNER + G * N + (g + 1) * N] = (
                dc + _dot(dcb, b_g)).astype(dx_ref.dtype)
        dda = jnp.dot(_tri(L, lower=False), da_c + da_r.T, precision=HI, preferred_element_type=F32)
        ddt_ref[...] = dda * arow_ref[...] + ddt_r.T
        da_ref[...] += jnp.sum(dda * dt_ref[...], axis=0, keepdims=True)

    col = pl.BlockSpec((L, LANES), lambda c: (rev(c), 0))
    row = pl.BlockSpec((LANES, L), lambda c: (0, rev(c)))
    vec = pl.BlockSpec((1, LANES), lambda c: (0, 0))
    pairs = pl.BlockSpec((N_PAIR, LANES), lambda c: (0, 0))
    return pl.pallas_call(
        body, name="ssd_bwd", grid=(nc,),
        in_specs=[pl.BlockSpec((L, SSM_CONV_DIM), lambda c: (rev(c), 0)), col, col, row, row,
                  pl.BlockSpec((1, N_PAIR, N, LANES), lambda c: (rev(c), 0, 0, 0)),
                  pl.BlockSpec((L, SSM_INNER), lambda c: (rev(c), 0)), pairs, vec],
        out_specs=[pl.BlockSpec((L, SSM_CONV_DIM), lambda c: (rev(c), 0)), col, vec, pairs],
        out_shape=[jax.ShapeDtypeStruct((s, SSM_CONV_DIM), BF16), jax.ShapeDtypeStruct((s, LANES), F32),
                   jax.ShapeDtypeStruct((1, LANES), F32), jax.ShapeDtypeStruct((N_PAIR, LANES), F32)],
        scratch_shapes=[pltpu.VMEM((N_PAIR, N, LANES), F32)],
        compiler_params=_cp("arbitrary"),
    )(xbc_act, dt_c, ac_c, dt_r, ac_r, hprev_all, dy, dsk_pair, a_row)


ROWS = 512
GW = SSM_INNER // SSM_GROUPS


def _rows(width, dtype=F32):
    return pl.BlockSpec((ROWS, width), lambda i: (i, 0))


def _vec(width):
    return pl.BlockSpec((1, width), lambda i: (0, 0))


def _gnorm_fwd(y, z, w):
    s = y.shape[0]

    def body(y_ref, z_ref, w_ref, o_ref):
        for g in range(SSM_GROUPS):
            cs = slice(g * GW, (g + 1) * GW)
            zz = z_ref[:, cs].astype(F32)
            u = y_ref[:, cs] * (zz * _sigmoid(zz))
            r = lax.rsqrt(jnp.mean(u * u, axis=1, keepdims=True) + RMS_EPS)
            o_ref[:, cs] = (u * r * w_ref[:, cs]).astype(o_ref.dtype)

    return pl.pallas_call(body, name="gnorm_fwd", grid=(s // ROWS,),
                          in_specs=[_rows(SSM_INNER), _rows(SSM_INNER), _vec(SSM_INNER)], out_specs=_rows(SSM_INNER),
                          out_shape=jax.ShapeDtypeStruct((s, SSM_INNER), BF16), compiler_params=_cp("parallel"))(y, z, w)


def _gnorm_bwd(y, z, w, do):
    s = y.shape[0]

    def body(y_ref, z_ref, w_ref, do_ref, dy_ref, dz_ref, dw_ref):
        @pl.when(pl.program_id(0) == 0)
        def _():
            dw_ref[...] = jnp.zeros_like(dw_ref)

        for g in range(SSM_GROUPS):
            cs = slice(g * GW, (g + 1) * GW)
            zz, yy, dd = z_ref[:, cs].astype(F32), y_ref[:, cs], do_ref[:, cs].astype(F32)
            sz, dsz = _silu_and_grad(zz)
            u = yy * sz
            r = lax.rsqrt(jnp.mean(u * u, axis=1, keepdims=True) + RMS_EPS)
            n = u * r
            dn = dd * w_ref[:, cs]
            dw_ref[:, cs] += jnp.sum(dd * n, axis=0, keepdims=True)
            du = r * (dn - n * jnp.mean(dn * n, axis=1, keepdims=True))
            dy_ref[:, cs] = (du * sz).astype(dy_ref.dtype)
            dz_ref[:, cs] = (du * yy * dsz).astype(dz_ref.dtype)

    return pl.pallas_call(
        body, name="gnorm_bwd", grid=(s // ROWS,),
        in_specs=[_rows(SSM_INNER), _rows(SSM_INNER), _vec(SSM_INNER), _rows(SSM_INNER)],
        out_specs=[_rows(SSM_INNER), _rows(SSM_INNER), _vec(SSM_INNER)],
        out_shape=[jax.ShapeDtypeStruct((s, SSM_INNER), BF16), jax.ShapeDtypeStruct((s, SSM_INNER), BF16),
                   jax.ShapeDtypeStruct((1, SSM_INNER), F32)],
        compiler_params=_cp("arbitrary"))(y, z, w, do)


def _mix_fwd(gl, bg, attn_d, ssm_d):
    s = gl.shape[0]
    d = D_MODEL

    def body(gl_ref, bg_ref, a_ref, m_ref, o_ref):
        g0 = _sigmoid(gl_ref[:, :d] + bg_ref[:, :d])
        g1 = _sigmoid(gl_ref[:, d:] + bg_ref[:, d:])
        o_ref[...] = (g0 * a_ref[...] + g1 * m_ref[...]).astype(o_ref.dtype)

    return pl.pallas_call(body, name="mix_fwd", grid=(s // ROWS,),
                          in_specs=[_rows(2 * d), _vec(2 * d), _rows(d), _rows(d)], out_specs=_rows(d),
                          out_shape=jax.ShapeDtypeStruct((s, d), BF16), compiler_params=_cp("parallel"))(
        gl, bg, attn_d, ssm_d)


def _mix_bwd(gl, bg, attn_d, ssm_d, dmix):
    s = gl.shape[0]
    d = D_MODEL

    def body(gl_ref, bg_ref, a_ref, m_ref, dm_ref, da_ref, ds_ref, dg_ref, db_ref):
        @pl.when(pl.program_id(0) == 0)
        def _():
            db_ref[...] = jnp.zeros_like(db_ref)

        g0 = _sigmoid(gl_ref[:, :d] + bg_ref[:, :d])
        g1 = _sigmoid(gl_ref[:, d:] + bg_ref[:, d:])
        dm = dm_ref[...].astype(F32)
        da_ref[...] = (dm * g0).astype(da_ref.dtype)
        ds_ref[...] = (dm * g1).astype(ds_ref.dtype)
        dl0 = dm * a_ref[...] * g0 * (1.0 - g0)
        dl1 = dm * m_ref[...] * g1 * (1.0 - g1)
        dg_ref[:, :d] = dl0.astype(dg_ref.dtype)
        dg_ref[:, d:] = dl1.astype(dg_ref.dtype)
        db_ref[:, :d] += jnp.sum(dl0, axis=0, keepdims=True)
        db_ref[:, d:] += jnp.sum(dl1, axis=0, keepdims=True)

    return pl.pallas_call(
        body, name="mix_bwd", grid=(s // ROWS,),
        in_specs=[_rows(2 * d), _vec(2 * d), _rows(d), _rows(d), _rows(d)],
        out_specs=[_rows(d), _rows(d), _rows(2 * d), _vec(2 * d)],
        out_shape=[jax.ShapeDtypeStruct((s, d), BF16), jax.ShapeDtypeStruct((s, d), BF16),
                   jax.ShapeDtypeStruct((s, 2 * d), BF16), jax.ShapeDtypeStruct((1, 2 * d), F32)],
        compiler_params=_cp("arbitrary"))(gl, bg, attn_d, ssm_d, dmix)


def _ln_stats(p):
    mu = jnp.mean(p, axis=1, keepdims=True)
    c = p - mu
    rstd = lax.rsqrt(jnp.mean(c * c, axis=1, keepdims=True) + LN_EPS)
    return c * rstd, rstd


def _ln_bwd(dy, xhat, rstd, g):
    dxh = dy * g
    return rstd * (dxh - jnp.mean(dxh, axis=1, keepdims=True) - xhat * jnp.mean(dxh * xhat, axis=1, keepdims=True))


def _ln1_fwd(x, mixed, g, b):
    s, d = x.shape

    def body(x_ref, m_ref, g_ref, b_ref, o_ref, ob_ref):
        xhat, _ = _ln_stats(DEEPNORM_ALPHA * x_ref[...] + m_ref[...])
        y = xhat * g_ref[...] + b_ref[...]
        o_ref[...] = y
        ob_ref[...] = y.astype(BF16)

    return pl.pallas_call(body, name="ln1_fwd", grid=(s // ROWS,), in_specs=[_rows(d), _rows(d), _vec(d), _vec(d)],
                          out_specs=[_rows(d), _rows(d)],
                          out_shape=[jax.ShapeDtypeStruct((s, d), F32), jax.ShapeDtypeStruct((s, d), BF16)],
                          compiler_params=_cp("parallel"))(x, mixed, g, b)


def _ln2_loss(x1, h, target, g, b):
    s, d = x1.shape

    def body(x_ref, h_ref, t_ref, g_ref, b_ref, dp_ref, dpb_ref, loss_ref, dg_ref, db_ref):
        @pl.when(pl.program_id(0) == 0)
        def _():
            loss_ref[...] = jnp.zeros_like(loss_ref)
            dg_ref[...] = jnp.zeros_like(dg_ref)
            db_ref[...] = jnp.zeros_like(db_ref)

        xhat, rstd = _ln_stats(DEEPNORM_ALPHA * x_ref[...] + h_ref[...])
        err = xhat * g_ref[...] + b_ref[...] - t_ref[...]
        part = 0.5 * jnp.sum(jnp.mean(err * err, axis=1, keepdims=True), axis=0, keepdims=True)
        loss_ref[...] += jnp.broadcast_to(part, loss_ref.shape)
        dy = err * (1.0 / d)
        dg_ref[...] += jnp.sum(dy * xhat, axis=0, keepdims=True)
        db_ref[...] += jnp.sum(dy, axis=0, keepdims=True)
        dp = _ln_bwd(dy, xhat, rstd, g_ref[...])
        dp_ref[...] = dp
        dpb_ref[...] = dp.astype(BF16)

    return pl.pallas_call(
        body, name="ln2_loss", grid=(s // ROWS,), in_specs=[_rows(d), _rows(d), _rows(d), _vec(d), _vec(d)],
        out_specs=[_rows(d), _rows(d), _vec(LANES), _vec(d), _vec(d)],
        out_shape=[jax.ShapeDtypeStruct((s, d), F32), jax.ShapeDtypeStruct((s, d), BF16),
                   jax.ShapeDtypeStruct((1, LANES), F32),
                   jax.ShapeDtypeStruct((1, d), F32), jax.ShapeDtypeStruct((1, d), F32)],
        compiler_params=_cp("arbitrary"))(x1, h, target, g, b)


def _ln1_bwd(x, mixed, g, dpre2, dffn):
    s, d = x.shape

    def body(x_ref, m_ref, g_ref, d2_ref, df_ref, dp_ref, dr_ref, dg_ref, db_ref):
        @pl.when(pl.program_id(0) == 0)
        def _():
            dg_ref[...] = jnp.zeros_like(dg_ref)
            db_ref[...] = jnp.zeros_like(db_ref)

        xhat, rstd = _ln_stats(DEEPNORM_ALPHA * x_ref[...] + m_ref[...])
        dy = DEEPNORM_ALPHA * d2_ref[...] + df_ref[...]
        dg_ref[...] += jnp.sum(dy * xhat, axis=0, keepdims=True)
        db_ref[...] += jnp.sum(dy, axis=0, keepdims=True)
        dp = _ln_bwd(dy, xhat, rstd, g_ref[...])
        dp_ref[...] = dp.astype(BF16)
        dr_ref[...] = DEEPNORM_ALPHA * dp

    return pl.pallas_call(
        body, name="ln1_bwd", grid=(s // ROWS,), in_specs=[_rows(d), _rows(d), _vec(d), _rows(d), _rows(d)],
        out_specs=[_rows(d), _rows(d), _vec(d), _vec(d)],
        out_shape=[jax.ShapeDtypeStruct((s, d), BF16), jax.ShapeDtypeStruct((s, d), F32),
                   jax.ShapeDtypeStruct((1, d), F32), jax.ShapeDtypeStruct((1, d), F32)],
        compiler_params=_cp("arbitrary"))(x, mixed, g, dpre2, dffn)


def _swiglu_fwd(gu):
    s = gu.shape[0]
    f = FFN_HIDDEN

    def body(g_ref, u_ref, o_ref):
        gg = g_ref[...].astype(F32)
        o_ref[...] = (gg * _sigmoid(gg) * u_ref[...].astype(F32)).astype(o_ref.dtype)

    return pl.pallas_call(
        body, name="swiglu_fwd", grid=(s // ROWS,),
        in_specs=[pl.BlockSpec((ROWS, f), lambda i: (i, 0)), pl.BlockSpec((ROWS, f), lambda i: (i, 1))],
        out_specs=_rows(f), out_shape=jax.ShapeDtypeStruct((s, f), BF16), compiler_params=_cp("parallel"))(gu, gu)


def _swiglu_bwd(gu, dact):
    s = gu.shape[0]
    f = FFN_HIDDEN

    def body(g_ref, u_ref, d_ref, o_ref):
        sg, dsg = _silu_and_grad(g_ref[...].astype(F32))
        dd = d_ref[...].astype(F32)
        o_ref[:, :f] = (dd * u_ref[...].astype(F32) * dsg).astype(o_ref.dtype)
        o_ref[:, f:] = (dd * sg).astype(o_ref.dtype)

    return pl.pallas_call(
        body, name="swiglu_bwd", grid=(s // ROWS,),
        in_specs=[pl.BlockSpec((ROWS, f), lambda i: (i, 0)), pl.BlockSpec((ROWS, f), lambda i: (i, 1)), _rows(f)],
        out_specs=_rows(2 * f), out_shape=jax.ShapeDtypeStruct((s, 2 * f), BF16),
        compiler_params=_cp("parallel"))(gu, gu, dact)


def _peer(k):
    x, y, c = lax.axis_index("x"), lax.axis_index("y"), lax.axis_index("c")
    kx, ky, kc = (k >> 2) & 1, (k >> 1) & 1, k & 1
    px = (1 - x) if kx else x
    py = (1 - y) if ky else y
    pc = (1 - c) if kc else c
    return (px, py, pc), 4 * px + 2 * py + pc


def _my_index():
    return 4 * lax.axis_index("x") + 2 * lax.axis_index("y") + lax.axis_index("c")


def _comm_copies(ins, outs, sems, scatter):
    send_sems, recv_sems, local_sems = sems
    me = _my_index()
    copies = [pltpu.make_async_copy(ins[t].at[me] if scatter else ins[t], outs[t].at[me], local_sems.at[t])
              for t in range(len(ins))]
    for k in range(1, N_DEV):
        peer, pidx = _peer(k)
        for t in range(len(ins)):
            copies.append(pltpu.make_async_remote_copy(
                src_ref=ins[t].at[pidx] if scatter else ins[t], dst_ref=outs[t].at[me],
                send_sem=send_sems.at[t, k - 1], recv_sem=recv_sems.at[t, k - 1], device_id=peer,
                device_id_type=pl.DeviceIdType.MESH))
    return copies


def _comm_sems(n):
    return [pltpu.SemaphoreType.DMA((n, N_DEV - 1)), pltpu.SemaphoreType.DMA((n, N_DEV - 1)),
            pltpu.SemaphoreType.DMA((n,))]


def _comm_out_shapes(parts, scatter):
    return [jax.ShapeDtypeStruct(p.shape if scatter else (N_DEV,) + p.shape, p.dtype) for p in parts]


def _all_gather(parts):
    n = len(parts)

    def body(*refs):
        ins, outs = refs[:n], refs[n:2 * n]
        send_sems, recv_sems, local_sems = refs[2 * n:]
        x, y, c = lax.axis_index("x"), lax.axis_index("y"), lax.axis_index("c")
        me, sibling = (x, y, c), (x, y, 1 - c)
        chips = [(1 - x, y), (x, 1 - y), (1 - x, 1 - y)]

        def copy(t, k, block, to, src=None):
            dst = outs[t].at[4 * block[0] + 2 * block[1] + block[2]]
            return pltpu.make_async_remote_copy(
                src_ref=dst if src is None else src, dst_ref=dst, send_sem=send_sems.at[t, k],
                recv_sem=recv_sems.at[t, k], device_id=to, device_id_type=pl.DeviceIdType.MESH)

        mine = [pltpu.make_async_copy(ins[t], outs[t].at[_my_index()], local_sems.at[t]) for t in range(n)]
        for cp in mine:
            cp.start()
        first = [copy(t, 0, me, sibling, src=ins[t]) for t in range(n)]
        first += [copy(t, 1 + j, me, (*chip, c), src=ins[t]) for j, chip in enumerate(chips) for t in range(n)]
        for cp in first:
            cp.start()
        passed = []
        for j, chip in enumerate(chips):
            for t in range(n):
                copy(t, 1 + j, (*chip, c), me).wait_recv()
                passed.append(copy(t, 4 + j, (*chip, c), sibling))
                passed[-1].start()
        for t in range(n):
            copy(t, 0, sibling, me).wait_recv()
            for j, chip in enumerate(chips):
                copy(t, 4 + j, (*chip, 1 - c), me).wait_recv()
        for cp in first + passed:
            cp.wait_send()
        for cp in mine:
            cp.wait()

    anyspec = pl.BlockSpec(memory_space=pl.ANY)
    return pl.pallas_call(body, name="all_gather", in_specs=[anyspec] * n, out_specs=[anyspec] * n,
                          out_shape=_comm_out_shapes(parts, False), scratch_shapes=_comm_sems(n))(*parts)


def _remote_scatter_copies(ins, lands, send_sems, recv_sems):
    me = _my_index()
    copies = []
    for k in range(1, N_DEV):
        peer, pidx = _peer(k)
        for t in range(len(ins)):
            copies.append(pltpu.make_async_remote_copy(
                src_ref=ins[t].at[pidx], dst_ref=lands[t].at[me], send_sem=send_sems.at[t * (N_DEV - 1) + k - 1],
                recv_sem=recv_sems.at[t * (N_DEV - 1) + k - 1], device_id=peer, device_id_type=pl.DeviceIdType.MESH))
    return copies


def _landing_zones(parts):
    me = _my_index()
    return [jnp.where(lax.broadcasted_iota(jnp.int32, p.shape, 0) == me, p, jnp.zeros_like(p)) for p in parts]


_HBM = pl.BlockSpec(memory_space=pltpu.HBM)
_SEM = pl.BlockSpec(memory_space=pltpu.SEMAPHORE)


def _exchange_start(parts, lands):
    n = len(parts)

    def body(*refs):
        ins, lnd, send_sems, recv_sems, token = refs[:n], refs[n:2 * n], refs[2 * n], refs[2 * n + 1], refs[-1]
        for cp in _remote_scatter_copies(ins, lnd, send_sems, recv_sems):
            cp.start()
        token[...] = jnp.zeros_like(token)

    hbm = [pltpu.HBM(p.shape, p.dtype) for p in parts]
    outs = pl.pallas_call(
        body, name="exchange_start",
        out_shape=[pltpu.SemaphoreType.DMA((n * (N_DEV - 1),)), pltpu.SemaphoreType.DMA((n * (N_DEV - 1),))] + hbm + hbm
        + [jax.ShapeDtypeStruct((8, LANES), F32)],
        in_specs=[_HBM] * (2 * n), out_specs=[_SEM, _SEM] + [_HBM] * (2 * n) + [pl.BlockSpec(memory_space=pltpu.VMEM)],
        input_output_aliases={t: 2 + t for t in range(2 * n)},
        compiler_params=pltpu.CompilerParams(has_side_effects=pltpu.SideEffectType.DATAFLOW_SIDE_EFFECTING),
    )(*[pltpu.with_memory_space_constraint(p, pltpu.HBM) for p in list(parts) + list(lands)])
    return outs[0], outs[1], list(outs[2:2 + n]), list(outs[2 + n:2 + 2 * n]), outs[-1]


def _exchange_wait(send_sems, recv_sems, parts, lands, after):
    n = len(parts)

    def body(*refs):
        ins, lnd, send_sems, recv_sems = refs[:n], refs[n:2 * n], refs[2 * n], refs[2 * n + 1]
        for cp in _remote_scatter_copies(ins, lnd, send_sems, recv_sems):
            cp.wait_send()
            cp.wait_recv()

    hbm = [pltpu.HBM(p.shape, p.dtype) for p in parts]
    outs = pl.pallas_call(
        body, name="exchange_wait", out_shape=hbm + hbm,
        in_specs=[_HBM] * (2 * n) + [_SEM, _SEM, pl.BlockSpec(memory_space=pl.ANY)], out_specs=[_HBM] * (2 * n),
        input_output_aliases={t: t for t in range(2 * n)},
        compiler_params=pltpu.CompilerParams(has_side_effects=pltpu.SideEffectType.DATAFLOW_SIDE_EFFECTING),
    )(*parts, *lands, send_sems, recv_sems, after)
    return list(outs[n:])


def _adamw(recv, w, m, v, name):
    _, r, c = w.shape
    br = _tile(r, 128)
    c1 = 1.0 / (1.0 - ADAM_B1 ** ADAM_STEP)
    c2 = 1.0 / (1.0 - ADAM_B2 ** ADAM_STEP)

    def body(r_ref, w_ref, m_ref, v_ref, g_ref, d_ref, mo_ref, vo_ref):
        g = r_ref[0].astype(F32)
        for k in range(1, N_DEV):
            g = g + r_ref[k].astype(F32)
        mn = ADAM_B1 * m_ref[0] + (1.0 - ADAM_B1) * g
        vn = ADAM_B2 * v_ref[0] + (1.0 - ADAM_B2) * (g * g)
        g_ref[0] = g
        mo_ref[0] = mn
        vo_ref[0] = vn
        d_ref[0] = -ADAM_LR * ((mn * c1) / (jnp.sqrt(vn * c2) + ADAM_EPS) + ADAM_WD * w_ref[0])

    blk = pl.BlockSpec((1, br, c), lambda i: (0, i, 0))
    return pl.pallas_call(
        body, name=name, grid=(r // br,),
        in_specs=[pl.BlockSpec((N_DEV, br, c), lambda i: (0, i, 0)), blk, blk, blk],
        out_specs=[blk] * 4, out_shape=[jax.ShapeDtypeStruct((1, r, c), F32)] * 4,
        compiler_params=_cp("parallel"))(recv, w, m, v)


def _lane_row(pairs):
    row = jnp.zeros((LANES,), F32)
    for lane0, vec in pairs:
        row = lax.dynamic_update_slice(row, vec.astype(F32), (lane0,))
    return row.reshape(1, LANES)


def _stage_in(x, wts, small):
    s = x.shape[0]
    a = -jnp.exp(small["a_log"])
    bias_row = _lane_row([(DT_LANE0, small["dt_bias"]), (F_LANE0, small["b_forget"])])
    a_row = _lane_row([(DT_LANE0, a)])
    conv_b = small["conv_b"].reshape(1, -1)
    norm_w = small["ssm_norm_w"].reshape(1, -1)
    bg = small["b_gates"].reshape(1, -1)
    g1, b1 = small["ln1_g"].reshape(1, -1), small["ln1_b"].reshape(1, -1)
    g2, b2 = small["ln2_g"].reshape(1, -1), small["ln2_b"].reshape(1, -1)
    d_skip = small["d_skip"]
    xb = x.astype(BF16)

    qkv = _mm(xb, wts["qkv"], out_dtype=BF16, name="f_qkv")
    z = _mm(xb, wts["z"], out_dtype=BF16, name="f_z")
    xbc = _mm(xb, wts["xbc"], name="f_xbc")
    gl = _mm(xb, wts["gate"], out_dtype=BF16, name="f_gate")
    fd = _mm(xb, wts["fd"], name="f_fd")
    dt_c, ac_c, cf_c, dt_r, ac_r, cf_r = _stats_fwd(fd, bias_row, a_row)
    bk = _att_blocks(s)[1]
    ck4 = cf_r[F_LANE0:F_LANE0 + ATT_HEADS].reshape(N_HP, HP, s // bk, bk)
    return dict(locals())


def _stage_mid(c, attn, lse, wts, target):
    x, xb, qkv, z, xbc, gl, fd, ck4 = (c[k] for k in ("x", "xb", "qkv", "z", "xbc", "gl", "fd", "ck4"))
    dt_c, ac_c, dt_r, ac_r, a_row, bias_row = (c[k] for k in ("dt_c", "ac_c", "dt_r", "ac_r", "a_row", "bias_row"))
    conv_b, norm_w, bg, g1, b1, g2, b2, d_skip = (c[k] for k in ("conv_b", "norm_w", "bg", "g1", "b1", "g2", "b2",
                                                                "d_skip"))
    conv_w = c["wts"]["conv"]
    attn_d = _mm(attn, wts["pa"], out_dtype=BF16, name="f_pa")
    xact = _conv_fwd(xbc, conv_w, conv_b)
    dsk_pair = jnp.repeat(d_skip, SSM_HEAD_DIM).reshape(N_PAIR, LANES)
    y, hprev = _ssd_pair_fwd(xact, ac_c, dt_r, ac_r, dsk_pair)
    ssm = _gnorm_fwd(y, z, norm_w)
    ssm_d = _mm(ssm, wts["ps"], out_dtype=BF16, name="f_ps")
    mix = _mix_fwd(gl, bg, attn_d, ssm_d)
    mixed = _mm(mix, wts["out"], name="f_out")
    x1, x1_b = _ln1_fwd(x, mixed, g1, b1)
    gu = _mm(x1_b, wts["gu"], out_dtype=BF16, name="f_gu")
    act = _swiglu_fwd(gu)
    h = _mm(act, wts["down"], name="f_down")
    dpre2, dpre2_b, loss_row, dg2, db2 = _ln2_loss(x1, h, target, g2, b2)

    d_act = _mm(dpre2_b, wts["down"], tb=True, out_dtype=BF16, name="b_down_x")
    dw_down = _mm(act, dpre2_b, ta=True, name="b_down_w")
    dgu = _swiglu_bwd(gu, d_act)
    dffn = _mm(dgu, wts["gu"], tb=True, name="b_gu_x")
    dw_gu = _mm(x1_b, dgu, ta=True, name="b_gu_w")
    dpre1, dxr, dg1, db1 = _ln1_bwd(x, mixed, g1, dpre2, dffn)
    dmix = _mm(dpre1, wts["out"], tb=True, out_dtype=BF16, name="b_out_x")
    dw_out = _mm(mix, dpre1, ta=True, name="b_out_w")
    dattn_d, dssm_d, dgl, dbg = _mix_bwd(gl, bg, attn_d, ssm_d, dmix)
    dssm = _mm(dssm_d, wts["ps"], tb=True, out_dtype=BF16, name="b_ps_x")
    dw_ps = _mm(ssm, dssm_d, ta=True, name="b_ps_w")
    dattn = _mm(dattn_d, wts["pa"], tb=True, name="b_pa_x")
    dw_pa = _mm(attn, dattn_d, ta=True, name="b_pa_w")
    dy, dz, dnw = _gnorm_bwd(y, z, norm_w, dssm)
    dxact, ddt, da_row, dds_pair = _ssd_pair_bwd(xact, dt_c, ac_c, dt_r, ac_r, hprev, dy, dsk_pair, a_row)
    dds = dds_pair.reshape(SSM_HEADS, SSM_HEAD_DIM).sum(axis=1)
    dpre_c, dconv_w, dconv_b = _conv_bwd_pre(xbc, conv_w, conv_b, dxact)
    dxbc = _conv_bwd_in(dpre_c, conv_w)
    st, do_b = _att_prep(dattn, attn, lse)
    late = dict(pa=dw_pa, ps=dw_ps, out=dw_out, gu=dw_gu, down=dw_down)
    keep = ("st", "do_b", "ddt", "dxr", "dz", "dxbc", "dgl", "dconv_w", "dconv_b", "da_row", "dds", "dnw", "dbg",
            "dg1", "db1", "dg2", "db2", "loss_row")
    loc = locals()
    return {**c, **{k: loc[k] for k in keep}}, late


def _stage_out_w(c, att_grads):
    dq, dk, dv, dck, dcq = att_grads
    xb, fd, bias_row, ddt, dz, dxbc, dgl = (c[k] for k in ("xb", "fd", "bias_row", "ddt", "dz", "dxbc", "dgl"))
    s, a = xb.shape[0], c["a"]
    dcum = dck.reshape(ATT_HEADS, s) + dcq.reshape(ATT_HEADS, s)
    dfd, dbias = _stats_bwd(fd, bias_row, ddt, jnp.zeros((LANES, s), F32).at[F_LANE0:F_LANE0 + ATT_HEADS].set(dcum))
    dproj = (dq, dk, dv, dz, dxbc, dgl, dfd)
    dw_in = [_mm(xb, g_, ta=True, tb=(i == 0), name=f"b_in_w{i}") for i, g_ in enumerate(dproj)]
    grads = dict(q=dw_in[0], k=dw_in[1], v=dw_in[2], z=dw_in[3], xbc=dw_in[4], gate=dw_in[5], fd=dw_in[6],
                 conv=c["dconv_w"])
    small_g = dict(
        b_forget=dbias[0, F_LANE0:F_LANE0 + ATT_HEADS], conv_b=c["dconv_b"][0], dt_bias=dbias[0, :SSM_HEADS],
        a_log=c["da_row"][0, :SSM_HEADS] * a, d_skip=c["dds"], ssm_norm_w=c["dnw"][0], b_gates=c["dbg"][0],
        ln1_g=c["dg1"][0], ln1_b=c["db1"][0], ln2_g=c["dg2"][0], ln2_b=c["db2"][0])
    return c["loss_row"][0, 0], grads, small_g, dproj


def _stage_out_x(c, dproj, token):
    wts, d = c["wts"], D_MODEL
    wq = wts["qkv"][:, :d] + token.astype(BF16)
    wk, wv = wts["qkv"][:, d:2 * d], wts["qkv"][:, 2 * d:]
    dx = c["dxr"]
    for i, (g_, w_) in enumerate(zip(dproj, (wq, wk, wv, wts["z"], wts["xbc"], wts["gate"], wts["fd"]))):
        dx = _mm(g_, w_, ta=(i == 0), tb=True, add=dx, name=f"b_in_x{i}")
    return dx


BIG = ("w_in", "w_proj_attn", "w_proj_ssm", "w_out", "w_ffn_gate", "w_ffn_up", "w_ffn_down", "conv_w")
EARLY = ("w_in", "conv_w")
LATE = ("w_proj_attn", "w_proj_ssm", "w_out", "w_ffn_gate", "w_ffn_up", "w_ffn_down")
SMALL = ("b_forget", "conv_b", "dt_bias", "a_log", "d_skip", "ssm_norm_w", "b_gates", "ln1_g", "ln1_b", "ln2_g",
         "ln2_b")
SMALL_ROWS = 96
IN_SHARD = IN_WIDTH // N_DEV
IN_SEGMENTS = (("q", 0, 1024), ("k", 1024, 1024), ("v", 2048, 1024), ("f", 3072, ATT_HEADS), ("z", 3088, SSM_INNER),
               ("xbc", 5136, SSM_CONV_DIM), ("dt", 8208, SSM_HEADS), ("gate", 8240, 2 * D_MODEL))


def _cols_from_shards(shards, lo, hi):
    w = shards[0].shape[1]
    pieces = []
    for j in range(len(shards)):
        a, b = max(lo, j * w), min(hi, (j + 1) * w)
        if a < b:
            pieces.append(shards[j][:, a - j * w:b - j * w])
    return pieces[0] if len(pieces) == 1 else jnp.concatenate(pieces, axis=1)


def _shards_from_parts(parts, width):
    shards = []
    for j in range(N_DEV):
        lo, hi = j * width, (j + 1) * width
        pieces = []
        for mat, c0 in parts:
            a, b = max(lo, c0), min(hi, c0 + mat.shape[1])
            if a < b:
                pieces.append(mat[:, a - c0:b - c0])
        shards.append(pieces[0] if len(pieces) == 1 else jnp.concatenate(pieces, axis=1))
    return shards


def _pack_small(vals):
    flat = jnp.concatenate([vals[n].reshape(-1) for n in SMALL])
    return jnp.pad(flat, (0, SMALL_ROWS * LANES - flat.shape[0])).reshape(SMALL_ROWS, LANES)


def _unpack_small(pack, shapes):
    flat = pack.reshape(-1)
    out, off = {}, 0
    for n in SMALL:
        sz = math.prod(shapes[n])
        out[n] = flat[off:off + sz].reshape(shapes[n])
        off += sz
    return out


def kernel(x, w_in, b_forget, conv_w, conv_b, dt_bias, a_log, d_skip, ssm_norm_w, w_proj_attn, w_proj_ssm, b_gates, w_out, ln1_g, ln1_b, w_ffn_gate, w_ffn_up, w_ffn_down, ln2_g, ln2_b, loss_target, m_w_in, m_b_forget, m_conv_w, m_conv_b, m_dt_bias, m_a_log, m_d_skip, m_ssm_norm_w, m_w_proj_attn, m_w_proj_ssm, m_b_gates, m_w_out, m_ln1_g, m_ln1_b, m_w_ffn_gate, m_w_ffn_up, m_w_ffn_down, m_ln2_g, m_ln2_b, v_w_in, v_b_forget, v_conv_w, v_conv_b, v_dt_bias, v_a_log, v_d_skip, v_ssm_norm_w, v_w_proj_attn, v_w_proj_ssm, v_b_gates, v_w_out, v_ln1_g, v_ln1_b, v_w_ffn_gate, v_w_ffn_up, v_w_ffn_down, v_ln2_g, v_ln2_b):
    args = dict(locals())
    d, f = D_MODEL, FFN_HIDDEN
    big_w = {n: args[n][0] for n in BIG}
    small_w = {n: args[n][0] for n in SMALL}
    big_shapes = {n: args[n].shape for n in BIG}
    small_shapes = {n: args[n].shape for n in SMALL}

    early = dict(zip(EARLY, _all_gather([big_w["w_in"].astype(BF16), big_w["conv_w"]])))
    in_shards = [early["w_in"][j] for j in range(N_DEV)]
    seg = {n: _cols_from_shards(in_shards, c0, c0 + w) for n, c0, w in IN_SEGMENTS}
    wfd = jnp.concatenate([seg["dt"], seg["f"], jnp.zeros((d, LANES - SSM_HEADS - ATT_HEADS), BF16)], axis=1)
    wts = dict(qkv=jnp.concatenate([seg["q"], seg["k"], seg["v"]], axis=1), z=seg["z"], xbc=seg["xbc"],
               gate=seg["gate"], fd=wfd, conv=jnp.concatenate([early["conv_w"][j] for j in range(N_DEV)], axis=1))

    ctx = _stage_in(x[0], wts, small_w)
    attn, lse, gathered = _attention_fwd(ctx["qkv"], ctx["ck4"], [big_w[n].astype(BF16) for n in LATE])
    full = dict(zip(LATE, gathered))
    late_w = dict(
        pa=full["w_proj_attn"].reshape(d, d), ps=full["w_proj_ssm"].reshape(SSM_INNER, d),
        out=full["w_out"].reshape(d, d),
        gu=jnp.concatenate([full["w_ffn_gate"][j] for j in range(N_DEV)]
                           + [full["w_ffn_up"][j] for j in range(N_DEV)], axis=1),
        down=full["w_ffn_down"].reshape(f, d))
    ctx, gl = _stage_mid(ctx, attn, lse, late_w, loss_target[0])
    late_dest = dict(
        w_ffn_gate=jnp.stack([s_.astype(BF16) for s_ in _shards_from_parts([(gl["gu"][:, :f], 0)], f // N_DEV)]),
        w_ffn_up=jnp.stack([s_.astype(BF16) for s_ in _shards_from_parts([(gl["gu"][:, f:], 0)], f // N_DEV)]))
    for n, key in (("w_proj_attn", "pa"), ("w_proj_ssm", "ps"), ("w_out", "out"), ("w_ffn_down", "down")):
        late_dest[n] = gl[key].astype(BF16).reshape((N_DEV,) + big_shapes[n][1:])
    att_grads, late_recv = _attention_bwd(ctx["qkv"], ctx["ck4"], ctx["st"], ctx["do_b"], [late_dest[n] for n in LATE])
    loss_part, g, small_g, dproj = _stage_out_w(ctx, att_grads)
    loss = lax.psum(loss_part, ("x", "y", "c"))

    gfd = g["fd"]
    in_parts = dict(q=g["q"], k=g["k"], v=g["v"], f=gfd[:, F_LANE0:F_LANE0 + ATT_HEADS], z=g["z"], xbc=g["xbc"],
                    dt=gfd[:, DT_LANE0:DT_LANE0 + SSM_HEADS], gate=g["gate"])
    win_dest = jnp.stack([s_.astype(BF16) for s_ in
                          _shards_from_parts([(in_parts[n], c0) for n, c0, _ in IN_SEGMENTS], IN_SHARD)])
    conv_dest = jnp.stack(_shards_from_parts([(g["conv"], 0)], SSM_CONV_DIM // N_DEV))
    small_pack = _pack_small(small_g)
    last_parts = [win_dest, conv_dest, jnp.broadcast_to(small_pack, (N_DEV,) + small_pack.shape)]
    send_sems, recv_sems, parts_thru, lands_thru, token = _exchange_start(last_parts, _landing_zones(last_parts))
    grad_x = _stage_out_x(ctx, dproj, token[0, 0])
    early_recv = _exchange_wait(send_sems, recv_sems, parts_thru, lands_thru, grad_x)
    recv = dict(zip(LATE, late_recv))
    recv["w_in"], recv["conv_w"] = early_recv[0], early_recv[1]

    outs = {}
    for n in BIG:
        outs[n] = _adamw(recv[n], args[n], args["m_" + n], args["v_" + n], name="adamw_" + n)
    small4 = _adamw(early_recv[2], _pack_small(small_w)[None], _pack_small({n: args["m_" + n][0] for n in SMALL})[None],
                    _pack_small({n: args["v_" + n][0] for n in SMALL})[None], name="adamw_small")
    small_out = [_unpack_small(p, small_shapes) for p in small4]
    for n in SMALL:
        outs[n] = [so[n] for so in small_out]

    order = ("w_in", "b_forget", "conv_w", "conv_b", "dt_bias", "a_log", "d_skip", "ssm_norm_w", "w_proj_attn",
             "w_proj_ssm", "b_gates", "w_out", "ln1_g", "ln1_b", "w_ffn_gate", "w_ffn_up", "w_ffn_down", "ln2_g",
             "ln2_b")
    res = [loss, grad_x[None]]
    for i in range(4):
        res += [outs[n][i] for n in order]
    return tuple(res)
```

```python
import functools
import math

import jax
import jax.numpy as jnp
from jax import lax
from jax.experimental import pallas as pl
from jax.experimental.pallas import tpu as pltpu

F32 = jnp.float32
BF16 = jnp.bfloat16

N_DEV = 8
D_MODEL = 1024
ATT_HEADS = 16
ATT_HEAD_DIM = 64
SSM_INNER = 2048
SSM_HEADS = 32
SSM_HEAD_DIM = 64
SSM_GROUPS = 4
SSM_HEADS_PER_GROUP = 8
SSM_STATE = 128
SSM_CONV = 4
SSM_CHUNK = 128
SSM_CONV_DIM = 3072
FFN_HIDDEN = 2816
IN_WIDTH = 10288
DEEPNORM_ALPHA = 2.0 ** 0.25
LN_EPS = 1e-5
RMS_EPS = 1e-5
ADAM_LR, ADAM_B1, ADAM_B2, ADAM_EPS, ADAM_WD, ADAM_STEP = 0.001, 0.9, 0.999, 1e-08, 0.01, 10
ATT_SCALE = 1.0 / math.sqrt(ATT_HEAD_DIM)

LANES = 128
VMEM_LIMIT = 56 * 1024 * 1024
NEG = -1e30

DT_LANE0 = 0
F_LANE0 = 32
HI = lax.Precision.HIGHEST


def _cp(*sem):
    return pltpu.CompilerParams(dimension_semantics=sem, vmem_limit_bytes=VMEM_LIMIT)


def _tile(n, cap=1408):
    for t in (3072, 2816, 2048, 1536, 1408, 1024, 512, 384, 256, 128):
        if t <= cap and n % t == 0:
            return t
    return n


MM_VMEM_BUDGET = VMEM_LIMIT - 4 * 2 ** 20


def _mm_tiles(m, n, k, a_bytes, b_bytes, out_bytes, has_add):
    tm = _tile(m)
    tk = next((t for t in (2816, 1024, 512, 256, 128) if k % t == 0), k)

    def need(tn):
        blocks = tm * tk * a_bytes + tk * tn * b_bytes + tm * tn * (out_bytes + (4 if has_add else 0))
        casts = (tm * tk * 2 if a_bytes == 4 else 0) + (tk * tn * 2 if b_bytes == 4 else 0)
        return 2 * blocks + casts + tm * tn * 4
    tn = next((t for t in (3072, 2816, 2048, 1536, 1408, 1024, 512, 384, 256) if n % t == 0 and need(t) <= MM_VMEM_BUDGET),
              _tile(n, 128))
    return tm, tn, tk


def _sigmoid(x):
    return 1.0 / (1.0 + jnp.exp(-x))


def _mm(a, b, *, ta=False, tb=False, out_dtype=F32, add=None, name):
    m, k = (a.shape[1], a.shape[0]) if ta else a.shape
    n = b.shape[0] if tb else b.shape[1]
    assert (b.shape[1] if tb else b.shape[0]) == k
    tm, tn, tk = _mm_tiles(m, n, k, a.dtype.itemsize, b.dtype.itemsize, jnp.dtype(out_dtype).itemsize, add is not None)
    nk = k // tk
    dims = (((0,) if ta else (1,), (1,) if tb else (0,)), ((), ()))

    def body_single(*refs):
        a_ref, b_ref = refs[:2]
        r = lax.dot_general(a_ref[...].astype(BF16), b_ref[...].astype(BF16), dims, preferred_element_type=F32)
        if add is not None:
            r = r + refs[2][...]
        refs[-1][...] = r.astype(refs[-1].dtype)

    def body(*refs):
        if add is None:
            a_ref, b_ref, o_ref, acc_ref = refs
        else:
            a_ref, b_ref, c_ref, o_ref, acc_ref = refs
        kk = pl.program_id(2)

        @pl.when(kk == 0)
        def _():
            acc_ref[...] = jnp.zeros_like(acc_ref)

        acc_ref[...] += lax.dot_general(a_ref[...].astype(BF16), b_ref[...].astype(BF16), dims,
                                        preferred_element_type=F32)

        @pl.when(kk == nk - 1)
        def _():
            r = acc_ref[...]
            if add is not None:
                r = r + c_ref[...]
            o_ref[...] = r.astype(o_ref.dtype)

    a_spec = pl.BlockSpec((tk, tm), lambda i, j, kk: (kk, i)) if ta else pl.BlockSpec((tm, tk), lambda i, j, kk: (i, kk))
    b_spec = pl.BlockSpec((tn, tk), lambda i, j, kk: (j, kk)) if tb else pl.BlockSpec((tk, tn), lambda i, j, kk: (kk, j))
    o_spec = pl.BlockSpec((tm, tn), lambda i, j, kk: (i, j))
    in_specs, args = [a_spec, b_spec], [a, b]
    if add is not None:
        in_specs.append(o_spec)
        args.append(add)
    return pl.pallas_call(
        body_single if nk == 1 else body, name=name, grid=(m // tm, n // tn, nk), in_specs=in_specs, out_specs=o_spec,
        out_shape=jax.ShapeDtypeStruct((m, n), out_dtype),
        scratch_shapes=[] if nk == 1 else [pltpu.VMEM((tm, tn), F32)],
        compiler_params=_cp("parallel", "parallel", "arbitrary"),
    )(*args)


def _tri(n, lower=True):
    r = lax.broadcasted_iota(jnp.int32, (n, n), 0)
    c = lax.broadcasted_iota(jnp.int32, (n, n), 1)
    return jnp.where((r >= c) if lower else (c >= r), 1.0, 0.0).astype(F32)


def _stats_fwd(fd, bias_row, a_row):
    s = fd.shape[0]
    blk = SSM_CHUNK

    def body(fd_ref, bias_ref, a_ref, dt_ref, ac_ref, cf_ref, dtr_ref, acr_ref, cfr_ref, carry_ref):
        @pl.when(pl.program_id(0) == 0)
        def _():
            carry_ref[...] = jnp.zeros_like(carry_ref)

        v = fd_ref[...] + bias_ref[...]
        dt = jnp.maximum(v, 0.0) + jnp.log(1.0 + jnp.exp(-jnp.abs(v)))
        lf = jnp.minimum(v, 0.0) - jnp.log(1.0 + jnp.exp(-jnp.abs(v)))
        tri = _tri(blk)
        ac = jnp.dot(tri, dt * a_ref[...], precision=HI, preferred_element_type=F32)
        cf = jnp.dot(tri, lf, precision=HI, preferred_element_type=F32) + carry_ref[0:1, :]
        carry_ref[...] = carry_ref[...] + jnp.sum(lf, axis=0, keepdims=True)
        dt_ref[...] = dt
        ac_ref[...] = ac
        cf_ref[...] = cf
        dtr_ref[...] = dt.T
        acr_ref[...] = ac.T
        cfr_ref[...] = cf.T

    col = pl.BlockSpec((blk, LANES), lambda i: (i, 0))
    row = pl.BlockSpec((LANES, blk), lambda i: (0, i))
    vec = pl.BlockSpec((1, LANES), lambda i: (0, 0))
    return pl.pallas_call(
        body, name="stats_fwd", grid=(s // blk,), in_specs=[col, vec, vec],
        out_specs=[col, col, col, row, row, row],
        out_shape=[jax.ShapeDtypeStruct((s, LANES), F32)] * 3 + [jax.ShapeDtypeStruct((LANES, s), F32)] * 3,
        scratch_shapes=[pltpu.VMEM((8, LANES), F32)],
        compiler_params=_cp("arbitrary"),
    )(fd, bias_row, a_row)


def _stats_bwd(fd, bias_row, ddt, dcum_rows):
    s = fd.shape[0]
    blk = SSM_CHUNK
    nb = s // blk

    def body(fd_ref, bias_ref, ddt_ref, dck_ref, o_ref, db_ref, carry_ref):
        @pl.when(pl.program_id(0) == 0)
        def _():
            carry_ref[...] = jnp.zeros_like(carry_ref)
            db_ref[...] = jnp.zeros_like(db_ref)

        v = fd_ref[...] + bias_ref[...]
        dcum = dck_ref[...].T
        dlf = jnp.dot(_tri(blk, lower=False), dcum, precision=HI, preferred_element_type=F32) + carry_ref[0:1, :]
        carry_ref[...] = carry_ref[...] + jnp.sum(dcum, axis=0, keepdims=True)
        lane = lax.broadcasted_iota(jnp.int32, v.shape, 1)
        g = jnp.where(lane < F_LANE0, ddt_ref[...] * _sigmoid(v), dlf * _sigmoid(-v))
        g = jnp.where(lane < F_LANE0 + ATT_HEADS, g, 0.0)
        o_ref[...] = g.astype(o_ref.dtype)
        db_ref[...] += jnp.sum(g, axis=0, keepdims=True)

    col = pl.BlockSpec((blk, LANES), lambda i: (nb - 1 - i, 0))
    row = pl.BlockSpec((LANES, blk), lambda i: (0, nb - 1 - i))
    vec = pl.BlockSpec((1, LANES), lambda i: (0, 0))
    return pl.pallas_call(
        body, name="stats_bwd", grid=(nb,), in_specs=[col, vec, col, row], out_specs=[col, vec],
        out_shape=[jax.ShapeDtypeStruct((s, LANES), BF16), jax.ShapeDtypeStruct((1, LANES), F32)],
        scratch_shapes=[pltpu.VMEM((8, LANES), F32)],
        compiler_params=_cp("arbitrary"),
    )(fd, bias_row, ddt, dcum_rows)


HP = LANES // ATT_HEAD_DIM
N_HP = ATT_HEADS // HP


def _att_blocks(s):
    return (512, 1024) if s % 1024 == 0 and s >= 4096 else (64, 128)


_QK = (((1,), (1,)), ((), ()))
_HALF = ATT_HEAD_DIM // 2
_PAD = 16


def _head_cols(a):
    return slice(a * ATT_HEAD_DIM, (a + 1) * ATT_HEAD_DIM)


def _causal(shape, off):
    r = lax.broadcasted_iota(jnp.int32, shape, 0)
    c = lax.broadcasted_iota(jnp.int32, shape, 1)
    return c <= r + off


def _attention_fwd(qkv, ck4, gather_parts):
    s = qkv.shape[0]
    bk = _att_blocks(s)[1]
    bq = bk
    nq, nk = s // bq, s // bk
    n = len(gather_parts)

    def body(q_ref, k_ref, v_ref, ck_ref, *rest):
        comm_in, (o_ref, lse_ref), comm_out, sems = rest[:n], rest[n:n + 2], rest[n + 2:2 * n + 2], rest[2 * n + 2:]
        i = pl.program_id(1)
        if n:
            @pl.when((pl.program_id(0) == 0) & (i == 0))
            def _():
                for cp in _comm_copies(comm_in, comm_out, sems, False):
                    cp.start()

        n_full = (i * bq) // bk
        qs = [(q_ref[:, _head_cols(a)].astype(F32) * ATT_SCALE).astype(BF16) for a in range(HP)]

        upper_q = lax.broadcasted_iota(jnp.int32, (bq, LANES), 1) >= ATT_HEAD_DIM

        def absorb(j, carry, keys=slice(0, bk), rows=slice(0, bq), masked=False):
            nk_ = keys.stop - keys.start
            ks = pl.ds(pl.multiple_of(j * bk, bk) + keys.start, nk_)
            v_both = v_ref[ks, :]
            out = []
            for a in range(HP):
                m, acc = carry[a]
                sc = lax.dot_general(qs[a][rows], k_ref[ks, _head_cols(a)], _QK, preferred_element_type=F32)
                sc = sc - ck_ref[0, a, pl.ds(j, 1), keys]
                if masked:
                    sc = jnp.where(_causal(sc.shape, 0), sc, NEG)
                m_new = jnp.maximum(m, jnp.max(sc, axis=1, keepdims=True))
                p = jnp.exp((sc - m_new).astype(BF16))
                upper_v = lax.broadcasted_iota(jnp.int32, (nk_, LANES), 1) >= ATT_HEAD_DIM
                v_aug = jnp.where(upper_v == (a == 1), v_both, jnp.ones_like(v_both))
                acc = jnp.exp(m - m_new) * acc + jnp.dot(p, v_aug, preferred_element_type=F32)
                out.append((m_new, acc))
            return tuple(out)

        init = tuple((jnp.full((bq, 1), NEG, F32), jnp.zeros((bq, LANES), F32)) for _ in range(HP))
        carry = lax.fori_loop(0, n_full, absorb, init)
        hq = bq // 2
        carry = absorb(n_full, carry, keys=slice(0, hq), masked=True)
        low = absorb(n_full, tuple((m[hq:], acc[hq:]) for m, acc in carry), keys=slice(hq, bk), rows=slice(hq, bq),
                     masked=True)
        carry = tuple((jnp.concatenate([m[:hq], ml], axis=0), jnp.concatenate([acc[:hq], al], axis=0))
                      for (m, acc), (ml, al) in zip(carry, low))
        outs, lses = [], []
        for a in range(HP):
            m, acc = carry[a]
            l = pltpu.roll(acc, ATT_HEAD_DIM, 1)
            outs.append(acc / l)
            lses.append(m + jnp.log(l))
        o_ref[...] = jnp.where(upper_q, outs[1], outs[0])
        lse_ref[...] = jnp.where(upper_q, lses[1], lses[0])
        if n:
            @pl.when((pl.program_id(0) == N_HP - 1) & (i == nq - 1))
            def _():
                for cp in _comm_copies(comm_in, comm_out, sems, False):
                    cp.wait()

    q_spec = pl.BlockSpec((bq, LANES), lambda h, i: (i, h))
    anyspec = pl.BlockSpec(memory_space=pl.ANY)
    res = pl.pallas_call(
        body, name="att_fwd", grid=(N_HP, nq),
        in_specs=[q_spec, pl.BlockSpec((s, LANES), lambda h, i: (0, N_HP + h)),
                  pl.BlockSpec((s, LANES), lambda h, i: (0, 2 * N_HP + h)),
                  pl.BlockSpec((1, HP, nk, bk), lambda h, i: (h, 0, 0, 0))] + [anyspec] * n,
        out_specs=[q_spec, q_spec] + [anyspec] * n,
        out_shape=[jax.ShapeDtypeStruct((s, D_MODEL), F32)] * 2 + _comm_out_shapes(gather_parts, False),
        scratch_shapes=_comm_sems(n) if n else [],
        compiler_params=_cp("arbitrary", "arbitrary"),
    )(qkv, qkv, qkv, ck4, *gather_parts)
    return res[0], res[1], list(res[2:])


def _att_prep(do, o, lse_rep):
    s = do.shape[0]
    bs = _tile(s, 512)

    def body(do_ref, o_ref, lse_ref, st_ref, dob_ref):
        r = lax.broadcasted_iota(jnp.int32, (LANES, LANES), 0) // ATT_HEAD_DIM
        c = lax.broadcasted_iota(jnp.int32, (LANES, LANES), 1) // ATT_HEAD_DIM
        e = jnp.where(r == c, 1.0, 0.0).astype(F32)
        lane = lax.broadcasted_iota(jnp.int32, (bs, LANES), 1)
        for p in range(D_MODEL // LANES):
            cs = slice(p * LANES, (p + 1) * LANES)
            dd = do_ref[:, cs]
            delta = jnp.dot(dd * o_ref[:, cs], e, precision=HI, preferred_element_type=F32)
            st_ref[:, cs] = jnp.where(lane % ATT_HEAD_DIM < _HALF, lse_ref[:, cs], delta)
            dob_ref[:, cs] = dd.astype(BF16)

    spec = pl.BlockSpec((bs, D_MODEL), lambda i: (i, 0))
    return pl.pallas_call(body, name="att_prep", grid=(s // bs,), in_specs=[spec, spec, spec], out_specs=[spec, spec],
                          out_shape=[jax.ShapeDtypeStruct((s, D_MODEL), F32), jax.ShapeDtypeStruct((s, D_MODEL), BF16)],
                          compiler_params=_cp("parallel"))(do, o, lse_rep)


def _attention_bwd(qkv, ck4, st, do_b, exchange_parts):
    s = qkv.shape[0]
    bq, bk = _att_blocks(s)
    nq, nk, per = s // bq, s // bk, bk // bq
    _T = (((0,), (0,)), ((), ()))
    n = len(exchange_parts)

    def body(q_ref, k_ref, v_ref, ck_ref, st_ref, do_ref, *rest):
        comm_in, (dq_ref, dk_ref, dv_ref, dck_ref, dcq_ref) = rest[:n], rest[n:n + 5]
        comm_out, sems, (dk_acc, dv_acc) = rest[n + 5:2 * n + 5], rest[2 * n + 5:-2], rest[-2:]
        j = pl.program_id(1)
        if n:
            @pl.when((pl.program_id(0) == 0) & (j == 0))
            def _():
                for cp in _comm_copies(comm_in, comm_out, sems, True):
                    cp.start()

        @pl.when(j == 0)
        def _():
            dq_ref[...] = jnp.zeros_like(dq_ref)
            dcq_ref[...] = jnp.zeros_like(dcq_ref)

        dk_acc[...] = jnp.zeros_like(dk_acc)
        dv_acc[...] = jnp.zeros_like(dv_acc)

        ones_q, ones_k = jnp.ones((_PAD, bq), BF16), jnp.ones((_PAD, bk), BF16)
        k_t = [jnp.concatenate([k_ref[:, _head_cols(a)].T, ones_k], axis=0) for a in range(HP)]

        def step(i, off=None, kl=slice(0, bk)):
            rows = pl.ds(pl.multiple_of(i * bq, bq), bq)
            for a in range(HP):
                cs = _head_cols(a)
                q = (q_ref[rows, cs].astype(F32) * ATT_SCALE).astype(BF16)
                do_a = do_ref[rows, cs]
                sc = lax.dot_general(q, k_ref[kl, cs], _QK, preferred_element_type=F32) - ck_ref[0, a, pl.ds(j, 1), kl]
                if off is not None:
                    sc = jnp.where(_causal(sc.shape, off), sc, NEG)
                p = jnp.exp(sc - st_ref[rows, a * ATT_HEAD_DIM:a * ATT_HEAD_DIM + 1])
                dp = lax.dot_general(do_a, v_ref[kl, cs], _QK, preferred_element_type=F32)
                ds = p * (dp - st_ref[rows, a * ATT_HEAD_DIM + _HALF:a * ATT_HEAD_DIM + _HALF + 1])
                ds_b = ds.astype(BF16)
                dv_acc[a, :, kl] += jnp.dot(do_a.T, p.astype(BF16), preferred_element_type=F32)
                dk_acc[a, :, kl] += jnp.dot(jnp.concatenate([q.T, ones_q], axis=0), ds_b, preferred_element_type=F32)
                dqs = lax.dot_general(k_t[a][:, kl], ds_b, _QK, preferred_element_type=F32)
                dq_ref[cs, rows] += dqs[:ATT_HEAD_DIM] * ATT_SCALE
                dcq_ref[0, a, pl.ds(i, 1), :] += jnp.sum(dqs[ATT_HEAD_DIM:ATT_HEAD_DIM + 8], axis=0,
                                                         keepdims=True) * 0.125

        for t in range(per):
            step(j * per + t, off=t * bq, kl=slice(0, (t + 1) * bq))

        def full(i, c):
            step(i)
            return c

        lax.fori_loop((j + 1) * per, nq, full, 0)
        for a in range(HP):
            dk_ref[:, _head_cols(a)] = dk_acc[a, :ATT_HEAD_DIM].T.astype(dk_ref.dtype)
            dv_ref[:, _head_cols(a)] = dv_acc[a].T.astype(dv_ref.dtype)
            dck_ref[0, a, pl.ds(j, 1), :] = -dk_acc[a, ATT_HEAD_DIM:ATT_HEAD_DIM + 1]
        if n:
            @pl.when((pl.program_id(0) == N_HP - 1) & (j == nk - 1))
            def _():
                for cp in _comm_copies(comm_in, comm_out, sems, True):
                    cp.wait()

    res = pl.BlockSpec((s, LANES), lambda h, j: (0, h))
    ck_spec = pl.BlockSpec((1, HP, nk, bk), lambda h, j: (h, 0, 0, 0))
    kout = pl.BlockSpec((bk, LANES), lambda h, j: (j, h))
    anyspec = pl.BlockSpec(memory_space=pl.ANY)
    outs = pl.pallas_call(
        body, name="att_bwd", grid=(N_HP, nk),
        in_specs=[res, pl.BlockSpec((bk, LANES), lambda h, j: (j, N_HP + h)),
                  pl.BlockSpec((bk, LANES), lambda h, j: (j, 2 * N_HP + h)), ck_spec, res, res] + [anyspec] * n,
        out_specs=[pl.BlockSpec((LANES, s), lambda h, j: (h, 0)), kout, kout, ck_spec,
                   pl.BlockSpec((1, HP, nq, bq), lambda h, j: (h, 0, 0, 0))] + [anyspec] * n,
        out_shape=[jax.ShapeDtypeStruct((D_MODEL, s), F32), jax.ShapeDtypeStruct((s, D_MODEL), BF16),
                   jax.ShapeDtypeStruct((s, D_MODEL), BF16), jax.ShapeDtypeStruct((N_HP, HP, nk, bk), F32),
                   jax.ShapeDtypeStruct((N_HP, HP, nq, bq), F32)] + _comm_out_shapes(exchange_parts, True),
        scratch_shapes=(_comm_sems(n) if n else [])
        + [pltpu.VMEM((HP, ATT_HEAD_DIM + _PAD, bk), F32), pltpu.VMEM((HP, ATT_HEAD_DIM, bk), F32)],
        compiler_params=_cp("arbitrary", "arbitrary"),
    )(qkv, qkv, qkv, ck4, st, do_b, *exchange_parts)
    return outs[:5], list(outs[5:])


def _silu_and_grad(x):
    sg = _sigmoid(x)
    return x * sg, sg * (1.0 + x * (1.0 - sg))


SUBLANES = 8


def _conv_taps(cur, before, w_rows, bias):
    n, c = cur.shape
    cur3 = cur.reshape(n // SUBLANES, SUBLANES, c)
    sub = lax.broadcasted_iota(jnp.int32, (1, SUBLANES, c), 1)
    taps = []
    for k in range(SSM_CONV):
        sh = SSM_CONV - 1 - k
        if sh == 0:
            taps.append(cur3)
            continue
        rot = pltpu.roll(cur3, sh, 1)
        prev = jnp.concatenate([pltpu.roll(before, sh, 0)[None], rot[:-1]], axis=0)
        taps.append(jnp.where(sub < sh, prev, rot))
    pre = bias[None] + sum(w_rows[k][None] * taps[k] for k in range(SSM_CONV))
    return pre.reshape(n, c), [t.reshape(n, c) for t in taps]


def _conv_col_chunks(w_ref, b_ref, bc, cc):
    for c0 in range(0, bc, cc):
        cols = slice(c0, c0 + cc)
        yield cols, [w_ref[k:k + 1, cols] for k in range(SSM_CONV)], None if b_ref is None else b_ref[:, cols]


def _conv_specs(s, bs, bc):
    cur = pl.BlockSpec((bs, bc), lambda j, i: (i, j))
    halo = pl.BlockSpec((8, bc), lambda j, i: (jnp.maximum(i * (bs // 8) - 1, 0), j))
    w = pl.BlockSpec((SSM_CONV, bc), lambda j, i: (0, j))
    b = pl.BlockSpec((1, bc), lambda j, i: (0, j))
    return cur, halo, w, b


def _conv_fwd(xbc, w, b):
    s, c = xbc.shape
    bs, bc = _tile(s, 512), 1024
    rc, cc = bs, 128

    def body(x_ref, h_ref, w_ref, b_ref, o_ref):
        first = pl.program_id(1) == 0
        for cols, w_rows, bias in _conv_col_chunks(w_ref, b_ref, bc, cc):
            def step(r, before):
                rows = pl.ds(pl.multiple_of(r * rc, rc), rc)
                cur = x_ref[rows, cols]
                pre, _ = _conv_taps(cur, before, w_rows, bias)
                o_ref[rows, cols] = pre * _sigmoid(pre)
                return cur[rc - SUBLANES:]

            lax.fori_loop(0, bs // rc, step, jnp.where(first, 0.0, h_ref[:, cols]))

    cur, halo, ws, bsp = _conv_specs(s, bs, bc)
    return pl.pallas_call(body, name="conv_fwd", grid=(c // bc, s // bs), in_specs=[cur, halo, ws, bsp],
                          out_specs=cur, out_shape=jax.ShapeDtypeStruct((s, c), F32),
                          compiler_params=_cp("parallel", "parallel"))(xbc, xbc, w, b)


def _conv_bwd_pre(xbc, w, b, dact):
    s, c = xbc.shape
    bs, bc = _tile(s, 512), 1024
    rc, cc = _tile(bs, 256), 128

    def body(x_ref, h_ref, w_ref, b_ref, g_ref, dp_ref, dw_ref, db_ref):
        first = pl.program_id(1) == 0

        @pl.when(first)
        def _():
            dw_ref[...] = jnp.zeros_like(dw_ref)
            db_ref[...] = jnp.zeros_like(db_ref)

        for cols, w_rows, bias in _conv_col_chunks(w_ref, b_ref, bc, cc):
            def step(r, carry):
                before, sums = carry
                rows = pl.ds(pl.multiple_of(r * rc, rc), rc)
                cur = x_ref[rows, cols]
                pre, taps = _conv_taps(cur, before, w_rows, bias)
                dpre = g_ref[rows, cols].astype(F32) * _silu_and_grad(pre)[1]
                dp_ref[rows, cols] = dpre.astype(dp_ref.dtype)
                terms = [dpre * t for t in taps] + [dpre]
                sums = tuple(a + jnp.sum(t.reshape(rc // SUBLANES, SUBLANES, cc), axis=0) for a, t in zip(sums, terms))
                return cur[rc - SUBLANES:], sums

            zero = jnp.zeros((SUBLANES, cc), F32)
            _, sums = lax.fori_loop(0, bs // rc, step,
                                    (jnp.where(first, 0.0, h_ref[:, cols]), (zero,) * (SSM_CONV + 1)))
            for k in range(SSM_CONV):
                dw_ref[k:k + 1, cols] += jnp.sum(sums[k], axis=0, keepdims=True)
            db_ref[:, cols] += jnp.sum(sums[SSM_CONV], axis=0, keepdims=True)

    cur, halo, ws, bsp = _conv_specs(s, bs, bc)
    return pl.pallas_call(
        body, name="conv_bwd_pre", grid=(c // bc, s // bs), in_specs=[cur, halo, ws, bsp, cur],
        out_specs=[cur, ws, bsp],
        out_shape=[jax.ShapeDtypeStruct((s, c), BF16), jax.ShapeDtypeStruct((SSM_CONV, c), F32),
                   jax.ShapeDtypeStruct((1, c), F32)],
        compiler_params=_cp("parallel", "arbitrary"))(xbc, xbc, w, b, dact)


def _conv_bwd_in(dpre, w):
    s, c = dpre.shape
    bs, bc = _tile(s, 512), 1024
    nb = s // bs
    rc, cc = bs, 256
    nr = bs // rc

    def body(g_ref, n_ref, w_ref, o_ref):
        last = pl.program_id(1) == nb - 1
        sub = lax.broadcasted_iota(jnp.int32, (1, SUBLANES, cc), 1)
        for cols, w_rows, _ in _conv_col_chunks(w_ref, None, bc, cc):
            def step(i, after):
                rows = pl.ds(pl.multiple_of((nr - 1 - i) * rc, rc), rc)
                cur = g_ref[rows, cols].astype(F32)
                cur3 = cur.reshape(rc // SUBLANES, SUBLANES, cc)
                acc = w_rows[SSM_CONV - 1][None] * cur3
                for sh in range(1, SSM_CONV):
                    rot = pltpu.roll(cur3, SUBLANES - sh, 1)
                    nxt = jnp.concatenate([rot[1:], pltpu.roll(after, SUBLANES - sh, 0)[None]], axis=0)
                    acc = acc + w_rows[SSM_CONV - 1 - sh][None] * jnp.where(sub >= SUBLANES - sh, nxt, rot)
                o_ref[rows, cols] = acc.reshape(rc, cc).astype(o_ref.dtype)
                return cur[0:SUBLANES]

            lax.fori_loop(0, nr, step, jnp.where(last, 0.0, n_ref[0:SUBLANES, cols].astype(F32)))

    cur = pl.BlockSpec((bs, bc), lambda j, i: (i, j))
    nxt = pl.BlockSpec((16, bc), lambda j, i: (jnp.minimum((i + 1) * (bs // 16), s // 16 - 1), j))
    ws = pl.BlockSpec((SSM_CONV, bc), lambda j, i: (0, j))
    return pl.pallas_call(body, name="conv_bwd_in", grid=(c // bc, nb), in_specs=[cur, nxt, ws], out_specs=cur,
                          out_shape=jax.ShapeDtypeStruct((s, c), BF16),
                          compiler_params=_cp("parallel", "parallel"))(dpre, dpre, w)


def _dotT(a, b):
    return lax.dot_general(a.astype(BF16), b.astype(BF16), (((1,), (1,)), ((), ())), preferred_element_type=F32)


def _dot(a, b):
    return jnp.dot(a.astype(BF16), b.astype(BF16), preferred_element_type=F32)


N_PAIR = SSM_HEADS // HP
PAIRS_PER_GROUP = SSM_HEADS_PER_GROUP // HP


def _pair_consts():
    L = SSM_CHUNK
    lane = lax.broadcasted_iota(jnp.int32, (L, LANES), 1)
    lane1 = lax.broadcasted_iota(jnp.int32, (1, LANES), 1)
    li = lax.broadcasted_iota(jnp.int32, (L, L), 0)
    si = lax.broadcasted_iota(jnp.int32, (L, L), 1)
    return lane >= ATT_HEAD_DIM, lane1 >= ATT_HEAD_DIM, li, si


def _ssd_pair_fwd(xbc_act, ac_c, dt_r, ac_r, dsk_pair):
    s = xbc_act.shape[0]
    L, N, G = SSM_CHUNK, SSM_STATE, SSM_GROUPS
    nc = s // L

    def body(xbc_ref, ac_ref, dtr_ref, acr_ref, dsk_ref, y_ref, hp_ref, st_ref):
        @pl.when(pl.program_id(0) == 0)
        def _():
            st_ref[...] = jnp.zeros_like(st_ref)

        upper, up1, li, si = _pair_consts()
        for g in range(G):
            b_g = xbc_ref[:, SSM_INNER + g * N:SSM_INNER + (g + 1) * N]
            c_g = xbc_ref[:, SSM_INNER + G * N + g * N:SSM_INNER + G * N + (g + 1) * N]
            cb = _dotT(c_g, b_g)
            b_t = b_g.T
            for q in range(PAIRS_PER_GROUP):
                pp = g * PAIRS_PER_GROUP + q
                cols = slice(pp * LANES, (pp + 1) * LANES)
                xs = xbc_ref[:, cols]
                ht = st_ref[pp]
                hp_ref[0, pp] = ht
                y = dsk_ref[pp:pp + 1, :] * xs
                s_new = jnp.zeros((N, LANES), F32)
                ea, el = [], []
                for a in range(HP):
                    h = HP * pp + a
                    acol = jnp.broadcast_to(ac_ref[:, h:h + 1], (L, LANES))
                    arow, dtrow = acr_ref[h:h + 1, :], dtr_ref[h:h + 1, :]
                    alast = ac_ref[L - 1:L, h:h + 1]
                    decay = jnp.exp(jnp.where(li >= si, acol - arow, NEG))
                    xs_a = jnp.where(upper == (a == 1), xs, 0.0)
                    y = y + _dot(cb * decay * dtrow, xs_a)
                    s_new = s_new + _dot(b_t * (dtrow * jnp.exp(alast - arow)), xs_a)
                    ea.append(jnp.exp(acol))
                    el.append(jnp.exp(alast))
                y_ref[:, cols] = y + jnp.where(upper, ea[1], ea[0]) * _dot(c_g, ht)
                st_ref[pp] = ht * jnp.where(up1, el[1], el[0]) + s_new

    col = pl.BlockSpec((L, LANES), lambda c: (c, 0))
    row = pl.BlockSpec((LANES, L), lambda c: (0, c))
    return pl.pallas_call(
        body, name="ssd_fwd", grid=(nc,),
        in_specs=[pl.BlockSpec((L, SSM_CONV_DIM), lambda c: (c, 0)), col, row, row,
                  pl.BlockSpec((N_PAIR, LANES), lambda c: (0, 0))],
        out_specs=[pl.BlockSpec((L, SSM_INNER), lambda c: (c, 0)),
                   pl.BlockSpec((1, N_PAIR, N, LANES), lambda c: (c, 0, 0, 0))],
        out_shape=[jax.ShapeDtypeStruct((s, SSM_INNER), F32), jax.ShapeDtypeStruct((nc, N_PAIR, N, LANES), F32)],
        scratch_shapes=[pltpu.VMEM((N_PAIR, N, LANES), F32)],
        compiler_params=_cp("arbitrary"),
    )(xbc_act, ac_c, dt_r, ac_r, dsk_pair)


def _ssd_pair_bwd(xbc_act, dt_c, ac_c, dt_r, ac_r, hprev_all, dy, dsk_pair, a_row):
    s = xbc_act.shape[0]
    L, N, G = SSM_CHUNK, SSM_STATE, SSM_GROUPS
    nc = s // L
    rev = lambda c: nc - 1 - c

    def body(xbc_ref, dt_ref, ac_ref, dtr_ref, acr_ref, hp_ref, dy_ref, dsk_ref, arow_ref,
             dx_ref, ddt_ref, da_ref, dds_ref, dh_ref):
        @pl.when(pl.program_id(0) == 0)
        def _():
            dh_ref[...] = jnp.zeros_like(dh_ref)
            da_ref[...] = jnp.zeros_like(da_ref)
            dds_ref[...] = jnp.zeros_like(dds_ref)

        upper, up1, li, si = _pair_consts()
        lane = lax.broadcasted_iota(jnp.int32, (L, LANES), 1)
        sub = lax.broadcasted_iota(jnp.int32, (LANES, L), 0)
        lastrow = lax.broadcasted_iota(jnp.int32, (L, LANES), 0) == L - 1
        da_c = jnp.zeros((L, LANES), F32)
        da_r = jnp.zeros((LANES, L), F32)
        ddt_r = jnp.zeros((LANES, L), F32)
        for g in range(G):
            b_g = xbc_ref[:, SSM_INNER + g * N:SSM_INNER + (g + 1) * N]
            c_g = xbc_ref[:, SSM_INNER + G * N + g * N:SSM_INNER + G * N + (g + 1) * N]
            cb, cb_t = _dotT(c_g, b_g), _dotT(b_g, c_g)
            b_t, c_t = b_g.T, c_g.T
            dcb = jnp.zeros((L, L), F32)
            db_t = jnp.zeros((N, L), F32)
            dc = jnp.zeros((L, N), F32)
            for q in range(PAIRS_PER_GROUP):
                pp = g * PAIRS_PER_GROUP + q
                cols = slice(pp * LANES, (pp + 1) * LANES)
                xs, gy = xbc_ref[:, cols], dy_ref[:, cols].astype(F32)
                ht, dhn = hp_ref[0, pp], dh_ref[pp]
                acol = [jnp.broadcast_to(ac_ref[:, HP * pp + a:HP * pp + a + 1], (L, LANES)) for a in range(HP)]
                alast = [ac_ref[L - 1:L, HP * pp + a:HP * pp + a + 1] for a in range(HP)]
                ea = jnp.where(upper, jnp.exp(acol[1]), jnp.exp(acol[0]))
                el = jnp.where(up1, jnp.exp(alast[1]), jnp.exp(alast[0]))
                ge = gy * ea
                dc = dc + _dotT(ge, ht)
                dh_ref[pp] = _dot(c_t, ge) + dhn * el
                t_off = (ge * _dot(c_g, ht)).astype(BF16)
                hsum = jnp.sum(dhn * ht, axis=0, keepdims=True)
                dxs = dsk_ref[pp:pp + 1, :] * gy
                dds_ref[pp:pp + 1, :] += jnp.sum(gy * xs, axis=0, keepdims=True)
                for a in range(HP):
                    h = HP * pp + a
                    mine, mine1 = upper == (a == 1), up1 == (a == 1)
                    arow, dtrow = acr_ref[h:h + 1, :], dtr_ref[h:h + 1, :]
                    dtcol = jnp.broadcast_to(dt_ref[:, h:h + 1], (L, LANES))
                    xs_a, gy_a = jnp.where(mine, xs, 0.0), jnp.where(mine, gy, 0.0)
                    dhn_a = jnp.where(mine1, dhn, 0.0)
                    e_row = jnp.exp(alast[a] - arow)
                    w_row = dtrow * e_row
                    xd_t = _dotT(dhn_a, xs_a)
                    db_t = db_t + xd_t * w_row
                    dw = jnp.sum(b_t * xd_t, axis=0, keepdims=True)
                    de_e = dw * w_row
                    dal = (jnp.sum(jnp.where(mine1, hsum, 0.0), axis=1, keepdims=True) * jnp.exp(alast[a])
                           + jnp.sum(de_e, axis=1, keepdims=True))
                    dxs = dxs + _dot(b_g, dhn_a) * (dtcol * jnp.exp(alast[a] - acol[a]))
                    decay = jnp.exp(jnp.where(li >= si, acol[a] - arow, NEG))
                    decay_t = jnp.exp(jnp.where(si >= li, arow - acol[a], NEG))
                    m = cb * decay
                    dmdt = _dotT(gy_a, xs_a)
                    dxs = dxs + _dot(cb_t * decay_t * dtcol, gy_a)
                    dm = dmdt * dtrow
                    dcb = dcb + dm * decay
                    wb = (dm * m).astype(BF16)
                    onehot = jnp.where(lane == h, 1.0, 0.0).astype(BF16)
                    da_c = (da_c + jnp.dot(wb, onehot, preferred_element_type=F32)
                            + jnp.dot(jnp.where(mine, t_off, 0.0).astype(BF16), onehot, preferred_element_type=F32)
                            + jnp.where(lastrow & (lane == h), dal, 0.0))
                    da_r = jnp.where(sub == h, -(jnp.sum(wb.astype(F32), axis=0, keepdims=True) + de_e), da_r)
                    ddt_r = jnp.where(sub == h, dw * e_row + jnp.sum(dmdt * m, axis=0, keepdims=True), ddt_r)
                dx_ref[:, cols] = dxs.astype(dx_ref.dtype)
            dx_ref[:, SSM_INNER + g * N:SSM_INNER + (g + 1) * N] = (db_t + _dot(c_t, dcb)).T.astype(dx_ref.dtype)
            dx_ref[:, SSM_INNER + G * N + g * N:SSM_INNER + G * N + (g + 1) * N] = (
                dc + _dot(dcb, b_g)).astype(dx_ref.dtype)
        dda = jnp.dot(_tri(L, lower=False), da_c + da_r.T, precision=HI, preferred_element_type=F32)
        ddt_ref[...] = dda * arow_ref[...] + ddt_r.T
        da_ref[...] += jnp.sum(dda * dt_ref[...], axis=0, keepdims=True)

    col = pl.BlockSpec((L, LANES), lambda c: (rev(c), 0))
    row = pl.BlockSpec((LANES, L), lambda c: (0, rev(c)))
    vec = pl.BlockSpec((1, LANES), lambda c: (0, 0))
    pairs = pl.BlockSpec((N_PAIR, LANES), lambda c: (0, 0))
    return pl.pallas_call(
        body, name="ssd_bwd", grid=(nc,),
        in_specs=[pl.BlockSpec((L, SSM_CONV_DIM), lambda c: (rev(c), 0)), col, col, row, row,
                  pl.BlockSpec((1, N_PAIR, N, LANES), lambda c: (rev(c), 0, 0, 0)),
                  pl.BlockSpec((L, SSM_INNER), lambda c: (rev(c), 0)), pairs, vec],
        out_specs=[pl.BlockSpec((L, SSM_CONV_DIM), lambda c: (rev(c), 0)), col, vec, pairs],
        out_shape=[jax.ShapeDtypeStruct((s, SSM_CONV_DIM), BF16), jax.ShapeDtypeStruct((s, LANES), F32),
                   jax.ShapeDtypeStruct((1, LANES), F32), jax.ShapeDtypeStruct((N_PAIR, LANES), F32)],
        scratch_shapes=[pltpu.VMEM((N_PAIR, N, LANES), F32)],
        compiler_params=_cp("arbitrary"),
    )(xbc_act, dt_c, ac_c, dt_r, ac_r, hprev_all, dy, dsk_pair, a_row)


ROWS = 512
GW = SSM_INNER // SSM_GROUPS


def _rows(width, dtype=F32):
    return pl.BlockSpec((ROWS, width), lambda i: (i, 0))


def _vec(width):
    return pl.BlockSpec((1, width), lambda i: (0, 0))


def _gnorm_fwd(y, z, w):
    s = y.shape[0]

    def body(y_ref, z_ref, w_ref, o_ref):
        for g in range(SSM_GROUPS):
            cs = slice(g * GW, (g + 1) * GW)
            zz = z_ref[:, cs].astype(F32)
            u = y_ref[:, cs] * (zz * _sigmoid(zz))
            r = lax.rsqrt(jnp.mean(u * u, axis=1, keepdims=True) + RMS_EPS)
            o_ref[:, cs] = (u * r * w_ref[:, cs]).astype(o_ref.dtype)

    return pl.pallas_call(body, name="gnorm_fwd", grid=(s // ROWS,),
                          in_specs=[_rows(SSM_INNER), _rows(SSM_INNER), _vec(SSM_INNER)], out_specs=_rows(SSM_INNER),
                          out_shape=jax.ShapeDtypeStruct((s, SSM_INNER), BF16), compiler_params=_cp("parallel"))(y, z, w)


def _gnorm_bwd(y, z, w, do):
    s = y.shape[0]

    def body(y_ref, z_ref, w_ref, do_ref, dy_ref, dz_ref, dw_ref):
        @pl.when(pl.program_id(0) == 0)
        def _():
            dw_ref[...] = jnp.zeros_like(dw_ref)

        for g in range(SSM_GROUPS):
            cs = slice(g * GW, (g + 1) * GW)
            zz, yy, dd = z_ref[:, cs].astype(F32), y_ref[:, cs], do_ref[:, cs].astype(F32)
            sz, dsz = _silu_and_grad(zz)
            u = yy * sz
            r = lax.rsqrt(jnp.mean(u * u, axis=1, keepdims=True) + RMS_EPS)
            n = u * r
            dn = dd * w_ref[:, cs]
            dw_ref[:, cs] += jnp.sum(dd * n, axis=0, keepdims=True)
            du = r * (dn - n * jnp.mean(dn * n, axis=1, keepdims=True))
            dy_ref[:, cs] = (du * sz).astype(dy_ref.dtype)
            dz_ref[:, cs] = (du * yy * dsz).astype(dz_ref.dtype)

    return pl.pallas_call(
        body, name="gnorm_bwd", grid=(s // ROWS,),
        in_specs=[_rows(SSM_INNER), _rows(SSM_INNER), _vec(SSM_INNER), _rows(SSM_INNER)],
        out_specs=[_rows(SSM_INNER), _rows(SSM_INNER), _vec(SSM_INNER)],
        out_shape=[jax.ShapeDtypeStruct((s, SSM_INNER), BF16), jax.ShapeDtypeStruct((s, SSM_INNER), BF16),
                   jax.ShapeDtypeStruct((1, SSM_INNER), F32)],
        compiler_params=_cp("arbitrary"))(y, z, w, do)


def _mix_fwd(gl, bg, attn_d, ssm_d):
    s = gl.shape[0]
    d = D_MODEL

    def body(gl_ref, bg_ref, a_ref, m_ref, o_ref):
        g0 = _sigmoid(gl_ref[:, :d] + bg_ref[:, :d])
        g1 = _sigmoid(gl_ref[:, d:] + bg_ref[:, d:])
        o_ref[...] = (g0 * a_ref[...] + g1 * m_ref[...]).astype(o_ref.dtype)

    return pl.pallas_call(body, name="mix_fwd", grid=(s // ROWS,),
                          in_specs=[_rows(2 * d), _vec(2 * d), _rows(d), _rows(d)], out_specs=_rows(d),
                          out_shape=jax.ShapeDtypeStruct((s, d), BF16), compiler_params=_cp("parallel"))(
        gl, bg, attn_d, ssm_d)


def _mix_bwd(gl, bg, attn_d, ssm_d, dmix):
    s = gl.shape[0]
    d = D_MODEL

    def body(gl_ref, bg_ref, a_ref, m_ref, dm_ref, da_ref, ds_ref, dg_ref, db_ref):
        @pl.when(pl.program_id(0) == 0)
        def _():
            db_ref[...] = jnp.zeros_like(db_ref)

        g0 = _sigmoid(gl_ref[:, :d] + bg_ref[:, :d])
        g1 = _sigmoid(gl_ref[:, d:] + bg_ref[:, d:])
        dm = dm_ref[...].astype(F32)
        da_ref[...] = (dm * g0).astype(da_ref.dtype)
        ds_ref[...] = (dm * g1).astype(ds_ref.dtype)
        dl0 = dm * a_ref[...] * g0 * (1.0 - g0)
        dl1 = dm * m_ref[...] * g1 * (1.0 - g1)
        dg_ref[:, :d] = dl0.astype(dg_ref.dtype)
        dg_ref[:, d:] = dl1.astype(dg_ref.dtype)
        db_ref[:, :d] += jnp.sum(dl0, axis=0, keepdims=True)
        db_ref[:, d:] += jnp.sum(dl1, axis=0, keepdims=True)

    return pl.pallas_call(
        body, name="mix_bwd", grid=(s // ROWS,),
        in_specs=[_rows(2 * d), _vec(2 * d), _rows(d), _rows(d), _rows(d)],
        out_specs=[_rows(d), _rows(d), _rows(2 * d), _vec(2 * d)],
        out_shape=[jax.ShapeDtypeStruct((s, d), BF16), jax.ShapeDtypeStruct((s, d), BF16),
                   jax.ShapeDtypeStruct((s, 2 * d), BF16), jax.ShapeDtypeStruct((1, 2 * d), F32)],
        compiler_params=_cp("arbitrary"))(gl, bg, attn_d, ssm_d, dmix)


def _ln_stats(p):
    mu = jnp.mean(p, axis=1, keepdims=True)
    c = p - mu
    rstd = lax.rsqrt(jnp.mean(c * c, axis=1, keepdims=True) + LN_EPS)
    return c * rstd, rstd


def _ln_bwd(dy, xhat, rstd, g):
    dxh = dy * g
    return rstd * (dxh - jnp.mean(dxh, axis=1, keepdims=True) - xhat * jnp.mean(dxh * xhat, axis=1, keepdims=True))


def _ln1_fwd(x, mixed, g, b):
    s, d = x.shape

    def body(x_ref, m_ref, g_ref, b_ref, o_ref, ob_ref):
        xhat, _ = _ln_stats(DEEPNORM_ALPHA * x_ref[...] + m_ref[...])
        y = xhat * g_ref[...] + b_ref[...]
        o_ref[...] = y
        ob_ref[...] = y.astype(BF16)

    return pl.pallas_call(body, name="ln1_fwd", grid=(s // ROWS,), in_specs=[_rows(d), _rows(d), _vec(d), _vec(d)],
                          out_specs=[_rows(d), _rows(d)],
                          out_shape=[jax.ShapeDtypeStruct((s, d), F32), jax.ShapeDtypeStruct((s, d), BF16)],
                          compiler_params=_cp("parallel"))(x, mixed, g, b)


def _ln2_loss(x1, h, target, g, b):
    s, d = x1.shape

    def body(x_ref, h_ref, t_ref, g_ref, b_ref, dp_ref, dpb_ref, loss_ref, dg_ref, db_ref):
        @pl.when(pl.program_id(0) == 0)
        def _():
            loss_ref[...] = jnp.zeros_like(loss_ref)
            dg_ref[...] = jnp.zeros_like(dg_ref)
            db_ref[...] = jnp.zeros_like(db_ref)

        xhat, rstd = _ln_stats(DEEPNORM_ALPHA * x_ref[...] + h_ref[...])
        err = xhat * g_ref[...] + b_ref[...] - t_ref[...]
        part = 0.5 * jnp.sum(jnp.mean(err * err, axis=1, keepdims=True), axis=0, keepdims=True)
        loss_ref[...] += jnp.broadcast_to(part, loss_ref.shape)
        dy = err * (1.0 / d)
        dg_ref[...] += jnp.sum(dy * xhat, axis=0, keepdims=True)
        db_ref[...] += jnp.sum(dy, axis=0, keepdims=True)
        dp = _ln_bwd(dy, xhat, rstd, g_ref[...])
        dp_ref[...] = dp
        dpb_ref[...] = dp.astype(BF16)

    return pl.pallas_call(
        body, name="ln2_loss", grid=(s // ROWS,), in_specs=[_rows(d), _rows(d), _rows(d), _vec(d), _vec(d)],
        out_specs=[_rows(d), _rows(d), _vec(LANES), _vec(d), _vec(d)],
        out_shape=[jax.ShapeDtypeStruct((s, d), F32), jax.ShapeDtypeStruct((s, d), BF16),
                   jax.ShapeDtypeStruct((1, LANES), F32),
                   jax.ShapeDtypeStruct((1, d), F32), jax.ShapeDtypeStruct((1, d), F32)],
        compiler_params=_cp("arbitrary"))(x1, h, target, g, b)


def _ln1_bwd(x, mixed, g, dpre2, dffn):
    s, d = x.shape

    def body(x_ref, m_ref, g_ref, d2_ref, df_ref, dp_ref, dr_ref, dg_ref, db_ref):
        @pl.when(pl.program_id(0) == 0)
        def _():
            dg_ref[...] = jnp.zeros_like(dg_ref)
            db_ref[...] = jnp.zeros_like(db_ref)

        xhat, rstd = _ln_stats(DEEPNORM_ALPHA * x_ref[...] + m_ref[...])
        dy = DEEPNORM_ALPHA * d2_ref[...] + df_ref[...]
        dg_ref[...] += jnp.sum(dy * xhat, axis=0, keepdims=True)
        db_ref[...] += jnp.sum(dy, axis=0, keepdims=True)
        dp = _ln_bwd(dy, xhat, rstd, g_ref[...])
        dp_ref[...] = dp.astype(BF16)
        dr_ref[...] = DEEPNORM_ALPHA * dp

    return pl.pallas_call(
        body, name="ln1_bwd", grid=(s // ROWS,), in_specs=[_rows(d), _rows(d), _vec(d), _rows(d), _rows(d)],
        out_specs=[_rows(d), _rows(d), _vec(d), _vec(d)],
        out_shape=[jax.ShapeDtypeStruct((s, d), BF16), jax.ShapeDtypeStruct((s, d), F32),
                   jax.ShapeDtypeStruct((1, d), F32), jax.ShapeDtypeStruct((1, d), F32)],
        compiler_params=_cp("arbitrary"))(x, mixed, g, dpre2, dffn)


def _swiglu_fwd(gu):
    s = gu.shape[0]
    f = FFN_HIDDEN

    def body(g_ref, u_ref, o_ref):
        gg = g_ref[...].astype(F32)
        o_ref[...] = (gg * _sigmoid(gg) * u_ref[...].astype(F32)).astype(o_ref.dtype)

    return pl.pallas_call(
        body, name="swiglu_fwd", grid=(s // ROWS,),
        in_specs=[pl.BlockSpec((ROWS, f), lambda i: (i, 0)), pl.BlockSpec((ROWS, f), lambda i: (i, 1))],
        out_specs=_rows(f), out_shape=jax.ShapeDtypeStruct((s, f), BF16), compiler_params=_cp("parallel"))(gu, gu)


def _swiglu_bwd(gu, dact):
    s = gu.shape[0]
    f = FFN_HIDDEN

    def body(g_ref, u_ref, d_ref, o_ref):
        sg, dsg = _silu_and_grad(g_ref[...].astype(F32))
        dd = d_ref[...].astype(F32)
        o_ref[:, :f] = (dd * u_ref[...].astype(F32) * dsg).astype(o_ref.dtype)
        o_ref[:, f:] = (dd * sg).astype(o_ref.dtype)

    return pl.pallas_call(
        body, name="swiglu_bwd", grid=(s // ROWS,),
        in_specs=[pl.BlockSpec((ROWS, f), lambda i: (i, 0)), pl.BlockSpec((ROWS, f), lambda i: (i, 1)), _rows(f)],
        out_specs=_rows(2 * f), out_shape=jax.ShapeDtypeStruct((s, 2 * f), BF16),
        compiler_params=_cp("parallel"))(gu, gu, dact)


def _peer(k):
    x, y, c = lax.axis_index("x"), lax.axis_index("y"), lax.axis_index("c")
    kx, ky, kc = (k >> 2) & 1, (k >> 1) & 1, k & 1
    px = (1 - x) if kx else x
    py = (1 - y) if ky else y
    pc = (1 - c) if kc else c
    return (px, py, pc), 4 * px + 2 * py + pc


def _my_index():
    return 4 * lax.axis_index("x") + 2 * lax.axis_index("y") + lax.axis_index("c")


def _comm_copies(ins, outs, sems, scatter):
    send_sems, recv_sems, local_sems = sems
    me = _my_index()
    copies = [pltpu.make_async_copy(ins[t].at[me] if scatter else ins[t], outs[t].at[me], local_sems.at[t])
              for t in range(len(ins))]
    for k in range(1, N_DEV):
        peer, pidx = _peer(k)
        for t in range(len(ins)):
            copies.append(pltpu.make_async_remote_copy(
                src_ref=ins[t].at[pidx] if scatter else ins[t], dst_ref=outs[t].at[me],
                send_sem=send_sems.at[t, k - 1], recv_sem=recv_sems.at[t, k - 1], device_id=peer,
                device_id_type=pl.DeviceIdType.MESH))
    return copies


def _comm_sems(n):
    return [pltpu.SemaphoreType.DMA((n, N_DEV - 1)), pltpu.SemaphoreType.DMA((n, N_DEV - 1)),
            pltpu.SemaphoreType.DMA((n,))]


def _comm_out_shapes(parts, scatter):
    return [jax.ShapeDtypeStruct(p.shape if scatter else (N_DEV,) + p.shape, p.dtype) for p in parts]


def _all_gather(parts):
    n = len(parts)

    def body(*refs):
        ins, outs = refs[:n], refs[n:2 * n]
        send_sems, recv_sems, local_sems = refs[2 * n:]
        x, y, c = lax.axis_index("x"), lax.axis_index("y"), lax.axis_index("c")
        me, sibling = (x, y, c), (x, y, 1 - c)
        chips = [(1 - x, y), (x, 1 - y), (1 - x, 1 - y)]

        def copy(t, k, block, to, src=None):
            dst = outs[t].at[4 * block[0] + 2 * block[1] + block[2]]
            return pltpu.make_async_remote_copy(
                src_ref=dst if src is None else src, dst_ref=dst, send_sem=send_sems.at[t, k],
                recv_sem=recv_sems.at[t, k], device_id=to, device_id_type=pl.DeviceIdType.MESH)

        mine = [pltpu.make_async_copy(ins[t], outs[t].at[_my_index()], local_sems.at[t]) for t in range(n)]
        for cp in mine:
            cp.start()
        first = [copy(t, 0, me, sibling, src=ins[t]) for t in range(n)]
        first += [copy(t, 1 + j, me, (*chip, c), src=ins[t]) for j, chip in enumerate(chips) for t in range(n)]
        for cp in first:
            cp.start()
        passed = []
        for j, chip in enumerate(chips):
            for t in range(n):
                copy(t, 1 + j, (*chip, c), me).wait_recv()
                passed.append(copy(t, 4 + j, (*chip, c), sibling))
                passed[-1].start()
        for t in range(n):
            copy(t, 0, sibling, me).wait_recv()
            for j, chip in enumerate(chips):
                copy(t, 4 + j, (*chip, 1 - c), me).wait_recv()
        for cp in first + passed:
            cp.wait_send()
        for cp in mine:
            cp.wait()

    anyspec = pl.BlockSpec(memory_space=pl.ANY)
    return pl.pallas_call(body, name="all_gather", in_specs=[anyspec] * n, out_specs=[anyspec] * n,
                          out_shape=_comm_out_shapes(parts, False), scratch_shapes=_comm_sems(n))(*parts)


def _remote_scatter_copies(ins, lands, send_sems, recv_sems):
    me = _my_index()
    copies = []
    for k in range(1, N_DEV):
        peer, pidx = _peer(k)
        for t in range(len(ins)):
            copies.append(pltpu.make_async_remote_copy(
                src_ref=ins[t].at[pidx], dst_ref=lands[t].at[me], send_sem=send_sems.at[t * (N_DEV - 1) + k - 1],
                recv_sem=recv_sems.at[t * (N_DEV - 1) + k - 1], device_id=peer, device_id_type=pl.DeviceIdType.MESH))
    return copies


def _landing_zones(parts):
    me = _my_index()
    return [jnp.where(lax.broadcasted_iota(jnp.int32, p.shape, 0) == me, p, jnp.zeros_like(p)) for p in parts]


_HBM = pl.BlockSpec(memory_space=pltpu.HBM)
_SEM = pl.BlockSpec(memory_space=pltpu.SEMAPHORE)


def _exchange_start(parts, lands):
    n = len(parts)

    def body(*refs):
        ins, lnd, send_sems, recv_sems, token = refs[:n], refs[n:2 * n], refs[2 * n], refs[2 * n + 1], refs[-1]
        for cp in _remote_scatter_copies(ins, lnd, send_sems, recv_sems):
            cp.start()
        token[...] = jnp.zeros_like(token)

    hbm = [pltpu.HBM(p.shape, p.dtype) for p in parts]
    outs = pl.pallas_call(
        body, name="exchange_start",
        out_shape=[pltpu.SemaphoreType.DMA((n * (N_DEV - 1),)), pltpu.SemaphoreType.DMA((n * (N_DEV - 1),))] + hbm + hbm
        + [jax.ShapeDtypeStruct((8, LANES), F32)],
        in_specs=[_HBM] * (2 * n), out_specs=[_SEM, _SEM] + [_HBM] * (2 * n) + [pl.BlockSpec(memory_space=pltpu.VMEM)],
        input_output_aliases={t: 2 + t for t in range(2 * n)},
        compiler_params=pltpu.CompilerParams(has_side_effects=pltpu.SideEffectType.DATAFLOW_SIDE_EFFECTING),
    )(*[pltpu.with_memory_space_constraint(p, pltpu.HBM) for p in list(parts) + list(lands)])
    return outs[0], outs[1], list(outs[2:2 + n]), list(outs[2 + n:2 + 2 * n]), outs[-1]


def _exchange_wait(send_sems, recv_sems, parts, lands, after):
    n = len(parts)

    def body(*refs):
        ins, lnd, send_sems, recv_sems = refs[:n], refs[n:2 * n], refs[2 * n], refs[2 * n + 1]
        for cp in _remote_scatter_copies(ins, lnd, send_sems, recv_sems):
            cp.wait_send()
            cp.wait_recv()

    hbm = [pltpu.HBM(p.shape, p.dtype) for p in parts]
    outs = pl.pallas_call(
        body, name="exchange_wait", out_shape=hbm + hbm,
        in_specs=[_HBM] * (2 * n) + [_SEM, _SEM, pl.BlockSpec(memory_space=pl.ANY)], out_specs=[_HBM] * (2 * n),
        input_output_aliases={t: t for t in range(2 * n)},
        compiler_params=pltpu.CompilerParams(has_side_effects=pltpu.SideEffectType.DATAFLOW_SIDE_EFFECTING),
    )(*parts, *lands, send_sems, recv_sems, after)
    return list(outs[n:])


def _adamw(recv, w, m, v, name):
    _, r, c = w.shape
    br = _tile(r, 128)
    c1 = 1.0 / (1.0 - ADAM_B1 ** ADAM_STEP)
    c2 = 1.0 / (1.0 - ADAM_B2 ** ADAM_STEP)

    def body(r_ref, w_ref, m_ref, v_ref, g_ref, d_ref, mo_ref, vo_ref):
        g = r_ref[0].astype(F32)
        for k in range(1, N_DEV):
            g = g + r_ref[k].astype(F32)
        mn = ADAM_B1 * m_ref[0] + (1.0 - ADAM_B1) * g
        vn = ADAM_B2 * v_ref[0] + (1.0 - ADAM_B2) * (g * g)
        g_ref[0] = g
        mo_ref[0] = mn
        vo_ref[0] = vn
        d_ref[0] = -ADAM_LR * ((mn * c1) / (jnp.sqrt(vn * c2) + ADAM_EPS) + ADAM_WD * w_ref[0])

    blk = pl.BlockSpec((1, br, c), lambda i: (0, i, 0))
    return pl.pallas_call(
        body, name=name, grid=(r // br,),
        in_specs=[pl.BlockSpec((N_DEV, br, c), lambda i: (0, i, 0)), blk, blk, blk],
        out_specs=[blk] * 4, out_shape=[jax.ShapeDtypeStruct((1, r, c), F32)] * 4,
        compiler_params=_cp("parallel"))(recv, w, m, v)


def _lane_row(pairs):
    row = jnp.zeros((LANES,), F32)
    for lane0, vec in pairs:
        row = lax.dynamic_update_slice(row, vec.astype(F32), (lane0,))
    return row.reshape(1, LANES)


def _stage_in(x, wts, small):
    s = x.shape[0]
    a = -jnp.exp(small["a_log"])
    bias_row = _lane_row([(DT_LANE0, small["dt_bias"]), (F_LANE0, small["b_forget"])])
    a_row = _lane_row([(DT_LANE0, a)])
    conv_b = small["conv_b"].reshape(1, -1)
    norm_w = small["ssm_norm_w"].reshape(1, -1)
    bg = small["b_gates"].reshape(1, -1)
    g1, b1 = small["ln1_g"].reshape(1, -1), small["ln1_b"].reshape(1, -1)
    g2, b2 = small["ln2_g"].reshape(1, -1), small["ln2_b"].reshape(1, -1)
    d_skip = small["d_skip"]
    xb = x.astype(BF16)

    qkv = _mm(xb, wts["qkv"], out_dtype=BF16, name="f_qkv")
    z = _mm(xb, wts["z"], out_dtype=BF16, name="f_z")
    xbc = _mm(xb, wts["xbc"], name="f_xbc")
    gl = _mm(xb, wts["gate"], out_dtype=BF16, name="f_gate")
    fd = _mm(xb, wts["fd"], name="f_fd")
    dt_c, ac_c, cf_c, dt_r, ac_r, cf_r = _stats_fwd(fd, bias_row, a_row)
    bk = _att_blocks(s)[1]
    ck4 = cf_r[F_LANE0:F_LANE0 + ATT_HEADS].reshape(N_HP, HP, s // bk, bk)
    return dict(locals())


def _stage_mid(c, attn, lse, wts, target):
    x, xb, qkv, z, xbc, gl, fd, ck4 = (c[k] for k in ("x", "xb", "qkv", "z", "xbc", "gl", "fd", "ck4"))
    dt_c, ac_c, dt_r, ac_r, a_row, bias_row = (c[k] for k in ("dt_c", "ac_c", "dt_r", "ac_r", "a_row", "bias_row"))
    conv_b, norm_w, bg, g1, b1, g2, b2, d_skip = (c[k] for k in ("conv_b", "norm_w", "bg", "g1", "b1", "g2", "b2",
                                                                "d_skip"))
    conv_w = c["wts"]["conv"]
    attn_d = _mm(attn, wts["pa"], out_dtype=BF16, name="f_pa")
    xact = _conv_fwd(xbc, conv_w, conv_b)
    dsk_pair = jnp.repeat(d_skip, SSM_HEAD_DIM).reshape(N_PAIR, LANES)
    y, hprev = _ssd_pair_fwd(xact, ac_c, dt_r, ac_r, dsk_pair)
    ssm = _gnorm_fwd(y, z, norm_w)
    ssm_d = _mm(ssm, wts["ps"], out_dtype=BF16, name="f_ps")
    mix = _mix_fwd(gl, bg, attn_d, ssm_d)
    mixed = _mm(mix, wts["out"], name="f_out")
    x1, x1_b = _ln1_fwd(x, mixed, g1, b1)
    gu = _mm(x1_b, wts["gu"], out_dtype=BF16, name="f_gu")
    act = _swiglu_fwd(gu)
    h = _mm(act, wts["down"], name="f_down")
    dpre2, dpre2_b, loss_row, dg2, db2 = _ln2_loss(x1, h, target, g2, b2)

    d_act = _mm(dpre2_b, wts["down"], tb=True, out_dtype=BF16, name="b_down_x")
    dw_down = _mm(act, dpre2_b, ta=True, name="b_down_w")
    dgu = _swiglu_bwd(gu, d_act)
    dffn = _mm(dgu, wts["gu"], tb=True, name="b_gu_x")
    dw_gu = _mm(x1_b, dgu, ta=True, name="b_gu_w")
    dpre1, dxr, dg1, db1 = _ln1_bwd(x, mixed, g1, dpre2, dffn)
    dmix = _mm(dpre1, wts["out"], tb=True, out_dtype=BF16, name="b_out_x")
    dw_out = _mm(mix, dpre1, ta=True, name="b_out_w")
    dattn_d, dssm_d, dgl, dbg = _mix_bwd(gl, bg, attn_d, ssm_d, dmix)
    dssm = _mm(dssm_d, wts["ps"], tb=True, out_dtype=BF16, name="b_ps_x")
    dw_ps = _mm(ssm, dssm_d, ta=True, name="b_ps_w")
    dattn = _mm(dattn_d, wts["pa"], tb=True, name="b_pa_x")
    dw_pa = _mm(attn, dattn_d, ta=True, name="b_pa_w")
    dy, dz, dnw = _gnorm_bwd(y, z, norm_w, dssm)
    dxact, ddt, da_row, dds_pair = _ssd_pair_bwd(xact, dt_c, ac_c, dt_r, ac_r, hprev, dy, dsk_pair, a_row)
    dds = dds_pair.reshape(SSM_HEADS, SSM_HEAD_DIM).sum(axis=1)
    dpre_c, dconv_w, dconv_b = _conv_bwd_pre(xbc, conv_w, conv_b, dxact)
    dxbc = _conv_bwd_in(dpre_c, conv_w)
    st, do_b = _att_prep(dattn, attn, lse)
    late = dict(pa=dw_pa, ps=dw_ps, out=dw_out, gu=dw_gu, down=dw_down)
    keep = ("st", "do_b", "ddt", "dxr", "dz", "dxbc", "dgl", "dconv_w", "dconv_b", "da_row", "dds", "dnw", "dbg",
            "dg1", "db1", "dg2", "db2", "loss_row")
    loc = locals()
    return {**c, **{k: loc[k] for k in keep}}, late


def _stage_out_w(c, att_grads):
    dq, dk, dv, dck, dcq = att_grads
    xb, fd, bias_row, ddt, dz, dxbc, dgl = (c[k] for k in ("xb", "fd", "bias_row", "ddt", "dz", "dxbc", "dgl"))
    s, a = xb.shape[0], c["a"]
    dcum = dck.reshape(ATT_HEADS, s) + dcq.reshape(ATT_HEADS, s)
    dfd, dbias = _stats_bwd(fd, bias_row, ddt, jnp.zeros((LANES, s), F32).at[F_LANE0:F_LANE0 + ATT_HEADS].set(dcum))
    dproj = (dq, dk, dv, dz, dxbc, dgl, dfd)
    dw_in = [_mm(xb, g_, ta=True, tb=(i == 0), name=f"b_in_w{i}") for i, g_ in enumerate(dproj)]
    grads = dict(q=dw_in[0], k=dw_in[1], v=dw_in[2], z=dw_in[3], xbc=dw_in[4], gate=dw_in[5], fd=dw_in[6],
                 conv=c["dconv_w"])
    small_g = dict(
        b_forget=dbias[0, F_LANE0:F_LANE0 + ATT_HEADS], conv_b=c["dconv_b"][0], dt_bias=dbias[0, :SSM_HEADS],
        a_log=c["da_row"][0, :SSM_HEADS] * a, d_skip=c["dds"], ssm_norm_w=c["dnw"][0], b_gates=c["dbg"][0],
        ln1_g=c["dg1"][0], ln1_b=c["db1"][0], ln2_g=c["dg2"][0], ln2_b=c["db2"][0])
    return c["loss_row"][0, 0], grads, small_g, dproj


def _stage_out_x(c, dproj, token):
    wts, d = c["wts"], D_MODEL
    wq = wts["qkv"][:, :d] + token.astype(BF16)
    wk, wv = wts["qkv"][:, d:2 * d], wts["qkv"][:, 2 * d:]
    dx = c["dxr"]
    for i, (g_, w_) in enumerate(zip(dproj, (wq, wk, wv, wts["z"], wts["xbc"], wts["gate"], wts["fd"]))):
        dx = _mm(g_, w_, ta=(i == 0), tb=True, add=dx, name=f"b_in_x{i}")
    return dx


BIG = ("w_in", "w_proj_attn", "w_proj_ssm", "w_out", "w_ffn_gate", "w_ffn_up", "w_ffn_down", "conv_w")
EARLY = ("w_in", "conv_w")
LATE = ("w_proj_attn", "w_proj_ssm", "w_out", "w_ffn_gate", "w_ffn_up", "w_ffn_down")
SMALL = ("b_forget", "conv_b", "dt_bias", "a_log", "d_skip", "ssm_norm_w", "b_gates", "ln1_g", "ln1_b", "ln2_g",
         "ln2_b")
SMALL_ROWS = 96
IN_SHARD = IN_WIDTH // N_DEV
IN_SEGMENTS = (("q", 0, 1024), ("k", 1024, 1024), ("v", 2048, 1024), ("f", 3072, ATT_HEADS), ("z", 3088, SSM_INNER),
               ("xbc", 5136, SSM_CONV_DIM), ("dt", 8208, SSM_HEADS), ("gate", 8240, 2 * D_MODEL))


def _cols_from_shards(shards, lo, hi):
    w = shards[0].shape[1]
    pieces = []
    for j in range(len(shards)):
        a, b = max(lo, j * w), min(hi, (j + 1) * w)
        if a < b:
            pieces.append(shards[j][:, a - j * w:b - j * w])
    return pieces[0] if len(pieces) == 1 else jnp.concatenate(pieces, axis=1)


def _shards_from_parts(parts, width):
    shards = []
    for j in range(N_DEV):
        lo, hi = j * width, (j + 1) * width
        pieces = []
        for mat, c0 in parts:
            a, b = max(lo, c0), min(hi, c0 + mat.shape[1])
            if a < b:
                pieces.append(mat[:, a - c0:b - c0])
        shards.append(pieces[0] if len(pieces) == 1 else jnp.concatenate(pieces, axis=1))
    return shards


def _pack_small(vals):
    flat = jnp.concatenate([vals[n].reshape(-1) for n in SMALL])
    return jnp.pad(flat, (0, SMALL_ROWS * LANES - flat.shape[0])).reshape(SMALL_ROWS, LANES)


def _unpack_small(pack, shapes):
    flat = pack.reshape(-1)
    out, off = {}, 0
    for n in SMALL:
        sz = math.prod(shapes[n])
        out[n] = flat[off:off + sz].reshape(shapes[n])
        off += sz
    return out


def kernel(x, w_in, b_forget, conv_w, conv_b, dt_bias, a_log, d_skip, ssm_norm_w, w_proj_attn, w_proj_ssm, b_gates, w_out, ln1_g, ln1_b, w_ffn_gate, w_ffn_up, w_ffn_down, ln2_g, ln2_b, loss_target, m_w_in, m_b_forget, m_conv_w, m_conv_b, m_dt_bias, m_a_log, m_d_skip, m_ssm_norm_w, m_w_proj_attn, m_w_proj_ssm, m_b_gates, m_w_out, m_ln1_g, m_ln1_b, m_w_ffn_gate, m_w_ffn_up, m_w_ffn_down, m_ln2_g, m_ln2_b, v_w_in, v_b_forget, v_conv_w, v_conv_b, v_dt_bias, v_a_log, v_d_skip, v_ssm_norm_w, v_w_proj_attn, v_w_proj_ssm, v_b_gates, v_w_out, v_ln1_g, v_ln1_b, v_w_ffn_gate, v_w_ffn_up, v_w_ffn_down, v_ln2_g, v_ln2_b):
    args = dict(locals())
    d, f = D_MODEL, FFN_HIDDEN
    big_w = {n: args[n][0] for n in BIG}
    small_w = {n: args[n][0] for n in SMALL}
    big_shapes = {n: args[n].shape for n in BIG}
    small_shapes = {n: args[n].shape for n in SMALL}

    early = dict(zip(EARLY, _all_gather([big_w["w_in"].astype(BF16), big_w["conv_w"]])))
    in_shards = [early["w_in"][j] for j in range(N_DEV)]
    seg = {n: _cols_from_shards(in_shards, c0, c0 + w) for n, c0, w in IN_SEGMENTS}
    wfd = jnp.concatenate([seg["dt"], seg["f"], jnp.zeros((d, LANES - SSM_HEADS - ATT_HEADS), BF16)], axis=1)
    wts = dict(qkv=jnp.concatenate([seg["q"], seg["k"], seg["v"]], axis=1), z=seg["z"], xbc=seg["xbc"],
               gate=seg["gate"], fd=wfd, conv=jnp.concatenate([early["conv_w"][j] for j in range(N_DEV)], axis=1))

    ctx = _stage_in(x[0], wts, small_w)
    attn, lse, gathered = _attention_fwd(ctx["qkv"], ctx["ck4"], [big_w[n].astype(BF16) for n in LATE])
    full = dict(zip(LATE, gathered))
    late_w = dict(
        pa=full["w_proj_attn"].reshape(d, d), ps=full["w_proj_ssm"].reshape(SSM_INNER, d),
        out=full["w_out"].reshape(d, d),
        gu=jnp.concatenate([full["w_ffn_gate"][j] for j in range(N_DEV)]
                           + [full["w_ffn_up"][j] for j in range(N_DEV)], axis=1),
        down=full["w_ffn_down"].reshape(f, d))
    ctx, gl = _stage_mid(ctx, attn, lse, late_w, loss_target[0])
    late_dest = dict(
        w_ffn_gate=jnp.stack([s_.astype(BF16) for s_ in _shards_from_parts([(gl["gu"][:, :f], 0)], f // N_DEV)]),
        w_ffn_up=jnp.stack([s_.astype(BF16) for s_ in _shards_from_parts([(gl["gu"][:, f:], 0)], f // N_DEV)]))
    for n, key in (("w_proj_attn", "pa"), ("w_proj_ssm", "ps"), ("w_out", "out"), ("w_ffn_down", "down")):
        late_dest[n] = gl[key].astype(BF16).reshape((N_DEV,) + big_shapes[n][1:])
    att_grads, late_recv = _attention_bwd(ctx["qkv"], ctx["ck4"], ctx["st"], ctx["do_b"], [late_dest[n] for n in LATE])
    loss_part, g, small_g, dproj = _stage_out_w(ctx, att_grads)
    loss = lax.psum(loss_part, ("x", "y", "c"))

    gfd = g["fd"]
    in_parts = dict(q=g["q"], k=g["k"], v=g["v"], f=gfd[:, F_LANE0:F_LANE0 + ATT_HEADS], z=g["z"], xbc=g["xbc"],
                    dt=gfd[:, DT_LANE0:DT_LANE0 + SSM_HEADS], gate=g["gate"])
    win_dest = jnp.stack([s_.astype(BF16) for s_ in
                          _shards_from_parts([(in_parts[n], c0) for n, c0, _ in IN_SEGMENTS], IN_SHARD)])
    conv_dest = jnp.stack(_shards_from_parts([(g["conv"], 0)], SSM_CONV_DIM // N_DEV))
    small_pack = _pack_small(small_g)
    last_parts = [win_dest, conv_dest, jnp.broadcast_to(small_pack, (N_DEV,) + small_pack.shape)]
    send_sems, recv_sems, parts_thru, lands_thru, token = _exchange_start(last_parts, _landing_zones(last_parts))
    grad_x = _stage_out_x(ctx, dproj, token[0, 0])
    early_recv = _exchange_wait(send_sems, recv_sems, parts_thru, lands_thru, grad_x)
    recv = dict(zip(LATE, late_recv))
    recv["w_in"], recv["conv_w"] = early_recv[0], early_recv[1]

    outs = {}
    for n in BIG:
        outs[n] = _adamw(recv[n], args[n], args["m_" + n], args["v_" + n], name="adamw_" + n)
    small4 = _adamw(early_recv[2], _pack_small(small_w)[None], _pack_small({n: args["m_" + n][0] for n in SMALL})[None],
                    _pack_small({n: args["v_" + n][0] for n in SMALL})[None], name="adamw_small")
    small_out = [_unpack_small(p, small_shapes) for p in small4]
    for n in SMALL:
        outs[n] = [so[n] for so in small_out]

    order = ("w_in", "b_forget", "conv_w", "conv_b", "dt_bias", "a_log", "d_skip", "ssm_norm_w", "w_proj_attn",
             "w_proj_ssm", "b_gates", "w_out", "ln1_g", "ln1_b", "w_ffn_gate", "w_ffn_up", "w_ffn_down", "ln2_g",
             "ln2_b")
    res = [loss, grad_x[None]]
    for i in range(4):
        res += [outs[n][i] for n in order]
    return tuple(res)
```

```python
import functools
import math

import jax
import jax.numpy as jnp
from jax import lax
from jax.experimental import pallas as pl
from jax.experimental.pallas import tpu as pltpu

F32 = jnp.float32
BF16 = jnp.bfloat16

N_DEV = 8
D_MODEL = 1024
ATT_HEADS = 16
ATT_HEAD_DIM = 64
SSM_INNER = 2048
SSM_HEADS = 32
SSM_HEAD_DIM = 64
SSM_GROUPS = 4
SSM_HEADS_PER_GROUP = 8
SSM_STATE = 128
SSM_CONV = 4
SSM_CHUNK = 128
SSM_CONV_DIM = 3072
FFN_HIDDEN = 2816
IN_WIDTH = 10288
DEEPNORM_ALPHA = 2.0 ** 0.25
LN_EPS = 1e-5
RMS_EPS = 1e-5
ADAM_LR, ADAM_B1, ADAM_B2, ADAM_EPS, ADAM_WD, ADAM_STEP = 0.001, 0.9, 0.999, 1e-08, 0.01, 10
ATT_SCALE = 1.0 / math.sqrt(ATT_HEAD_DIM)

LANES = 128
VMEM_LIMIT = 56 * 1024 * 1024
NEG = -1e30

DT_LANE0 = 0
F_LANE0 = 32
HI = lax.Precision.HIGHEST


def _cp(*sem):
    return pltpu.CompilerParams(dimension_semantics=sem, vmem_limit_bytes=VMEM_LIMIT)


def _tile(n, cap=1408):
    for t in (3072, 2816, 2048, 1536, 1408, 1024, 512, 384, 256, 128):
        if t <= cap and n % t == 0:
            return t
    return n


MM_VMEM_BUDGET = VMEM_LIMIT - 4 * 2 ** 20


def _mm_tiles(m, n, k, a_bytes, b_bytes, out_bytes, has_add):
    tm = _tile(m)

    def need(tn, tk):
        blocks = tm * tk * a_bytes + tk * tn * b_bytes + tm * tn * (out_bytes + (4 if has_add else 0))
        casts = (tm * tk * 2 if a_bytes == 4 else 0) + (tk * tn * 2 if b_bytes == 4 else 0)
        return 2 * blocks + casts + tm * tn * 4

    tns = [t for t in (3072, 2816, 2048, 1536, 1408, 1024, 512, 384, 256, 128) if n % t == 0] or [n]
    for tk in [t for t in (4096, 3072, 2816, 2048, 1024, 512, 256, 128) if k % t == 0] or [k]:
        fits = [tn for tn in tns if need(tn, tk) <= MM_VMEM_BUDGET]
        if fits and fits[0] >= min(1024, tns[0]):
            return tm, fits[0], tk
    return tm, tns[-1], tk


def _sigmoid(x):
    return 1.0 / (1.0 + jnp.exp(-x))


def _mm(a, b, *, ta=False, tb=False, out_dtype=F32, add=None, name):
    m, k = (a.shape[1], a.shape[0]) if ta else a.shape
    n = b.shape[0] if tb else b.shape[1]
    assert (b.shape[1] if tb else b.shape[0]) == k
    tm, tn, tk = _mm_tiles(m, n, k, a.dtype.itemsize, b.dtype.itemsize, jnp.dtype(out_dtype).itemsize, add is not None)
    nk = k // tk
    dims = (((0,) if ta else (1,), (1,) if tb else (0,)), ((), ()))

    def body_single(*refs):
        a_ref, b_ref = refs[:2]
        r = lax.dot_general(a_ref[...].astype(BF16), b_ref[...].astype(BF16), dims, preferred_element_type=F32)
        if add is not None:
            r = r + refs[2][...]
        refs[-1][...] = r.astype(refs[-1].dtype)

    def body(*refs):
        if add is None:
            a_ref, b_ref, o_ref, acc_ref = refs
        else:
            a_ref, b_ref, c_ref, o_ref, acc_ref = refs
        kk = pl.program_id(2)

        @pl.when(kk == 0)
        def _():
            acc_ref[...] = jnp.zeros_like(acc_ref)

        acc_ref[...] += lax.dot_general(a_ref[...].astype(BF16), b_ref[...].astype(BF16), dims,
                                        preferred_element_type=F32)

        @pl.when(kk == nk - 1)
        def _():
            r = acc_ref[...]
            if add is not None:
                r = r + c_ref[...]
            o_ref[...] = r.astype(o_ref.dtype)

    a_spec = pl.BlockSpec((tk, tm), lambda i, j, kk: (kk, i)) if ta else pl.BlockSpec((tm, tk), lambda i, j, kk: (i, kk))
    b_spec = pl.BlockSpec((tn, tk), lambda i, j, kk: (j, kk)) if tb else pl.BlockSpec((tk, tn), lambda i, j, kk: (kk, j))
    o_spec = pl.BlockSpec((tm, tn), lambda i, j, kk: (i, j))
    in_specs, args = [a_spec, b_spec], [a, b]
    if add is not None:
        in_specs.append(o_spec)
        args.append(add)
    return pl.pallas_call(
        body_single if nk == 1 else body, name=name, grid=(m // tm, n // tn, nk), in_specs=in_specs, out_specs=o_spec,
        out_shape=jax.ShapeDtypeStruct((m, n), out_dtype),
        scratch_shapes=[] if nk == 1 else [pltpu.VMEM((tm, tn), F32)],
        compiler_params=_cp("parallel", "parallel", "arbitrary"),
    )(*args)


def _tri(n, lower=True):
    r = lax.broadcasted_iota(jnp.int32, (n, n), 0)
    c = lax.broadcasted_iota(jnp.int32, (n, n), 1)
    return jnp.where((r >= c) if lower else (c >= r), 1.0, 0.0).astype(F32)


def _stats_fwd(fd, bias_row, a_row):
    s = fd.shape[0]
    blk = SSM_CHUNK

    def body(fd_ref, bias_ref, a_ref, dt_ref, ac_ref, cf_ref, dtr_ref, acr_ref, cfr_ref, carry_ref):
        @pl.when(pl.program_id(0) == 0)
        def _():
            carry_ref[...] = jnp.zeros_like(carry_ref)

        v = fd_ref[...] + bias_ref[...]
        dt = jnp.maximum(v, 0.0) + jnp.log(1.0 + jnp.exp(-jnp.abs(v)))
        lf = jnp.minimum(v, 0.0) - jnp.log(1.0 + jnp.exp(-jnp.abs(v)))
        tri = _tri(blk)
        ac = jnp.dot(tri, dt * a_ref[...], precision=HI, preferred_element_type=F32)
        cf = jnp.dot(tri, lf, precision=HI, preferred_element_type=F32) + carry_ref[0:1, :]
        carry_ref[...] = carry_ref[...] + jnp.sum(lf, axis=0, keepdims=True)
        dt_ref[...] = dt
        ac_ref[...] = ac
        cf_ref[...] = cf
        dtr_ref[...] = dt.T
        acr_ref[...] = ac.T
        cfr_ref[...] = cf.T

    col = pl.BlockSpec((blk, LANES), lambda i: (i, 0))
    row = pl.BlockSpec((LANES, blk), lambda i: (0, i))
    vec = pl.BlockSpec((1, LANES), lambda i: (0, 0))
    return pl.pallas_call(
        body, name="stats_fwd", grid=(s // blk,), in_specs=[col, vec, vec],
        out_specs=[col, col, col, row, row, row],
        out_shape=[jax.ShapeDtypeStruct((s, LANES), F32)] * 3 + [jax.ShapeDtypeStruct((LANES, s), F32)] * 3,
        scratch_shapes=[pltpu.VMEM((8, LANES), F32)],
        compiler_params=_cp("arbitrary"),
    )(fd, bias_row, a_row)


def _stats_bwd(fd, bias_row, ddt, dcum_rows):
    s = fd.shape[0]
    blk = SSM_CHUNK
    nb = s // blk

    def body(fd_ref, bias_ref, ddt_ref, dck_ref, o_ref, db_ref, carry_ref):
        @pl.when(pl.program_id(0) == 0)
        def _():
            carry_ref[...] = jnp.zeros_like(carry_ref)
            db_ref[...] = jnp.zeros_like(db_ref)

        v = fd_ref[...] + bias_ref[...]
        dcum = dck_ref[...].T
        dlf = jnp.dot(_tri(blk, lower=False), dcum, precision=HI, preferred_element_type=F32) + carry_ref[0:1, :]
        carry_ref[...] = carry_ref[...] + jnp.sum(dcum, axis=0, keepdims=True)
        lane = lax.broadcasted_iota(jnp.int32, v.shape, 1)
        g = jnp.where(lane < F_LANE0, ddt_ref[...] * _sigmoid(v), dlf * _sigmoid(-v))
        g = jnp.where(lane < F_LANE0 + ATT_HEADS, g, 0.0)
        o_ref[...] = g.astype(o_ref.dtype)
        db_ref[...] += jnp.sum(g, axis=0, keepdims=True)

    col = pl.BlockSpec((blk, LANES), lambda i: (nb - 1 - i, 0))
    row = pl.BlockSpec((LANES, blk), lambda i: (0, nb - 1 - i))
    vec = pl.BlockSpec((1, LANES), lambda i: (0, 0))
    return pl.pallas_call(
        body, name="stats_bwd", grid=(nb,), in_specs=[col, vec, col, row], out_specs=[col, vec],
        out_shape=[jax.ShapeDtypeStruct((s, LANES), BF16), jax.ShapeDtypeStruct((1, LANES), F32)],
        scratch_shapes=[pltpu.VMEM((8, LANES), F32)],
        compiler_params=_cp("arbitrary"),
    )(fd, bias_row, ddt, dcum_rows)


HP = LANES // ATT_HEAD_DIM
N_HP = ATT_HEADS // HP


def _att_blocks(s):
    return (512, 1024) if s % 1024 == 0 and s >= 4096 else (64, 128)


_QK = (((1,), (1,)), ((), ()))
_HALF = ATT_HEAD_DIM // 2
_PAD = 16


def _head_cols(a):
    return slice(a * ATT_HEAD_DIM, (a + 1) * ATT_HEAD_DIM)


def _causal(shape, off):
    r = lax.broadcasted_iota(jnp.int32, shape, 0)
    c = lax.broadcasted_iota(jnp.int32, shape, 1)
    return c <= r + off


def _attention_fwd(qkv, ck4, gather_parts):
    s = qkv.shape[0]
    bk = _att_blocks(s)[1]
    bq = bk
    nq, nk = s // bq, s // bk
    n = len(gather_parts)

    def body(q_ref, k_ref, v_ref, ck_ref, *rest):
        comm_in, (o_ref, lse_ref), comm_out, sems = rest[:n], rest[n:n + 2], rest[n + 2:2 * n + 2], rest[2 * n + 2:]
        i = pl.program_id(1)
        if n:
            @pl.when((pl.program_id(0) == 0) & (i == 0))
            def _():
                for cp in _comm_copies(comm_in, comm_out, sems, False):
                    cp.start()

        n_full = (i * bq) // bk
        qs = [(q_ref[:, _head_cols(a)].astype(F32) * ATT_SCALE).astype(BF16) for a in range(HP)]

        upper_q = lax.broadcasted_iota(jnp.int32, (bq, LANES), 1) >= ATT_HEAD_DIM

        def absorb(j, carry, keys=slice(0, bk), rows=slice(0, bq), masked=False):
            nk_ = keys.stop - keys.start
            ks = pl.ds(pl.multiple_of(j * bk, bk) + keys.start, nk_)
            v_both = v_ref[ks, :]
            out = []
            for a in range(HP):
                m, acc = carry[a]
                sc = lax.dot_general(qs[a][rows], k_ref[ks, _head_cols(a)], _QK, preferred_element_type=F32)
                sc = sc - ck_ref[0, a, pl.ds(j, 1), keys]
                if masked:
                    sc = jnp.where(_causal(sc.shape, 0), sc, NEG)
                m_new = jnp.maximum(m, jnp.max(sc, axis=1, keepdims=True))
                p = jnp.exp((sc - m_new).astype(BF16))
                upper_v = lax.broadcasted_iota(jnp.int32, (nk_, LANES), 1) >= ATT_HEAD_DIM
                v_aug = jnp.where(upper_v == (a == 1), v_both, jnp.ones_like(v_both))
                acc = jnp.exp(m - m_new) * acc + jnp.dot(p, v_aug, preferred_element_type=F32)
                out.append((m_new, acc))
            return tuple(out)

        init = tuple((jnp.full((bq, 1), NEG, F32), jnp.zeros((bq, LANES), F32)) for _ in range(HP))
        carry = lax.fori_loop(0, n_full, absorb, init)
        hq = bq // 2
        carry = absorb(n_full, carry, keys=slice(0, hq), masked=True)
        low = absorb(n_full, tuple((m[hq:], acc[hq:]) for m, acc in carry), keys=slice(hq, bk), rows=slice(hq, bq),
                     masked=True)
        carry = tuple((jnp.concatenate([m[:hq], ml], axis=0), jnp.concatenate([acc[:hq], al], axis=0))
                      for (m, acc), (ml, al) in zip(carry, low))
        outs, lses = [], []
        for a in range(HP):
            m, acc = carry[a]
            l = pltpu.roll(acc, ATT_HEAD_DIM, 1)
            outs.append(acc / l)
            lses.append(m + jnp.log(l))
        o_ref[...] = jnp.where(upper_q, outs[1], outs[0])
        lse_ref[...] = jnp.where(upper_q, lses[1], lses[0])
        if n:
            @pl.when((pl.program_id(0) == N_HP - 1) & (i == nq - 1))
            def _():
                for cp in _comm_copies(comm_in, comm_out, sems, False):
                    cp.wait()

    q_spec = pl.BlockSpec((bq, LANES), lambda h, i: (i, h))
    anyspec = pl.BlockSpec(memory_space=pl.ANY)
    res = pl.pallas_call(
        body, name="att_fwd", grid=(N_HP, nq),
        in_specs=[q_spec, pl.BlockSpec((s, LANES), lambda h, i: (0, N_HP + h)),
                  pl.BlockSpec((s, LANES), lambda h, i: (0, 2 * N_HP + h)),
                  pl.BlockSpec((1, HP, nk, bk), lambda h, i: (h, 0, 0, 0))] + [anyspec] * n,
        out_specs=[q_spec, q_spec] + [anyspec] * n,
        out_shape=[jax.ShapeDtypeStruct((s, D_MODEL), F32)] * 2 + _comm_out_shapes(gather_parts, False),
        scratch_shapes=_comm_sems(n) if n else [],
        compiler_params=_cp("arbitrary", "arbitrary"),
    )(qkv, qkv, qkv, ck4, *gather_parts)
    return res[0], res[1], list(res[2:])


def _att_prep(do, o, lse_rep):
    s = do.shape[0]
    bs = _tile(s, 512)

    def body(do_ref, o_ref, lse_ref, st_ref, dob_ref):
        r = lax.broadcasted_iota(jnp.int32, (LANES, LANES), 0) // ATT_HEAD_DIM
        c = lax.broadcasted_iota(jnp.int32, (LANES, LANES), 1) // ATT_HEAD_DIM
        e = jnp.where(r == c, 1.0, 0.0).astype(F32)
        lane = lax.broadcasted_iota(jnp.int32, (bs, LANES), 1)
        for p in range(D_MODEL // LANES):
            cs = slice(p * LANES, (p + 1) * LANES)
            dd = do_ref[:, cs]
            delta = jnp.dot(dd * o_ref[:, cs], e, precision=HI, preferred_element_type=F32)
            st_ref[:, cs] = jnp.where(lane % ATT_HEAD_DIM < _HALF, lse_ref[:, cs], delta)
            dob_ref[:, cs] = dd.astype(BF16)

    spec = pl.BlockSpec((bs, D_MODEL), lambda i: (i, 0))
    return pl.pallas_call(body, name="att_prep", grid=(s // bs,), in_specs=[spec, spec, spec], out_specs=[spec, spec],
                          out_shape=[jax.ShapeDtypeStruct((s, D_MODEL), F32), jax.ShapeDtypeStruct((s, D_MODEL), BF16)],
                          compiler_params=_cp("parallel"))(do, o, lse_rep)


def _attention_bwd(qkv, ck4, st, do_b, exchange_parts):
    s = qkv.shape[0]
    bq, bk = _att_blocks(s)
    nq, nk, per = s // bq, s // bk, bk // bq
    _T = (((0,), (0,)), ((), ()))
    n = len(exchange_parts)

    def body(q_ref, k_ref, v_ref, ck_ref, st_ref, do_ref, *rest):
        comm_in, (dq_ref, dk_ref, dv_ref, dck_ref, dcq_ref) = rest[:n], rest[n:n + 5]
        comm_out, sems, (dk_acc, dv_acc) = rest[n + 5:2 * n + 5], rest[2 * n + 5:-2], rest[-2:]
        j = pl.program_id(1)
        if n:
            @pl.when((pl.program_id(0) == 0) & (j == 0))
            def _():
                for cp in _comm_copies(comm_in, comm_out, sems, True):
                    cp.start()

        @pl.when(j == 0)
        def _():
            dq_ref[...] = jnp.zeros_like(dq_ref)
            dcq_ref[...] = jnp.zeros_like(dcq_ref)

        dk_acc[...] = jnp.zeros_like(dk_acc)
        dv_acc[...] = jnp.zeros_like(dv_acc)

        ones_q, ones_k = jnp.ones((_PAD, bq), BF16), jnp.ones((_PAD, bk), BF16)
        k_t = [jnp.concatenate([k_ref[:, _head_cols(a)].T, ones_k], axis=0) for a in range(HP)]

        def step(i, off=None, kl=slice(0, bk)):
            rows = pl.ds(pl.multiple_of(i * bq, bq), bq)
            for a in range(HP):
                cs = _head_cols(a)
                q = (q_ref[rows, cs].astype(F32) * ATT_SCALE).astype(BF16)
                do_a = do_ref[rows, cs]
                sc = lax.dot_general(q, k_ref[kl, cs], _QK, preferred_element_type=F32) - ck_ref[0, a, pl.ds(j, 1), kl]
                if off is not None:
                    sc = jnp.where(_causal(sc.shape, off), sc, NEG)
                p = jnp.exp(sc - st_ref[rows, a * ATT_HEAD_DIM:a * ATT_HEAD_DIM + 1])
                dp = lax.dot_general(do_a, v_ref[kl, cs], _QK, preferred_element_type=F32)
                ds = p * (dp - st_ref[rows, a * ATT_HEAD_DIM + _HALF:a * ATT_HEAD_DIM + _HALF + 1])
                ds_b = ds.astype(BF16)
                dv_acc[a, :, kl] += jnp.dot(do_a.T, p.astype(BF16), preferred_element_type=F32)
                dk_acc[a, :, kl] += jnp.dot(jnp.concatenate([q.T, ones_q], axis=0), ds_b, preferred_element_type=F32)
                dqs = lax.dot_general(k_t[a][:, kl], ds_b, _QK, preferred_element_type=F32)
                dq_ref[cs, rows] += dqs[:ATT_HEAD_DIM] * ATT_SCALE
                dcq_ref[0, a, pl.ds(i, 1), :] += jnp.sum(dqs[ATT_HEAD_DIM:ATT_HEAD_DIM + 8], axis=0,
                                                         keepdims=True) * 0.125

        for t in range(per):
            step(j * per + t, off=t * bq, kl=slice(0, (t + 1) * bq))

        def full(i, c):
            step(i)
            return c

        lax.fori_loop((j + 1) * per, nq, full, 0)
        for a in range(HP):
            dk_ref[:, _head_cols(a)] = dk_acc[a, :ATT_HEAD_DIM].T.astype(dk_ref.dtype)
            dv_ref[:, _head_cols(a)] = dv_acc[a].T.astype(dv_ref.dtype)
            dck_ref[0, a, pl.ds(j, 1), :] = -dk_acc[a, ATT_HEAD_DIM:ATT_HEAD_DIM + 1]
        if n:
            @pl.when((pl.program_id(0) == N_HP - 1) & (j == nk - 1))
            def _():
                for cp in _comm_copies(comm_in, comm_out, sems, True):
                    cp.wait()

    res = pl.BlockSpec((s, LANES), lambda h, j: (0, h))
    ck_spec = pl.BlockSpec((1, HP, nk, bk), lambda h, j: (h, 0, 0, 0))
    kout = pl.BlockSpec((bk, LANES), lambda h, j: (j, h))
    anyspec = pl.BlockSpec(memory_space=pl.ANY)
    outs = pl.pallas_call(
        body, name="att_bwd", grid=(N_HP, nk),
        in_specs=[res, pl.BlockSpec((bk, LANES), lambda h, j: (j, N_HP + h)),
                  pl.BlockSpec((bk, LANES), lambda h, j: (j, 2 * N_HP + h)), ck_spec, res, res] + [anyspec] * n,
        out_specs=[pl.BlockSpec((LANES, s), lambda h, j: (h, 0)), kout, kout, ck_spec,
                   pl.BlockSpec((1, HP, nq, bq), lambda h, j: (h, 0, 0, 0))] + [anyspec] * n,
        out_shape=[jax.ShapeDtypeStruct((D_MODEL, s), F32), jax.ShapeDtypeStruct((s, D_MODEL), BF16),
                   jax.ShapeDtypeStruct((s, D_MODEL), BF16), jax.ShapeDtypeStruct((N_HP, HP, nk, bk), F32),
                   jax.ShapeDtypeStruct((N_HP, HP, nq, bq), F32)] + _comm_out_shapes(exchange_parts, True),
        scratch_shapes=(_comm_sems(n) if n else [])
        + [pltpu.VMEM((HP, ATT_HEAD_DIM + _PAD, bk), F32), pltpu.VMEM((HP, ATT_HEAD_DIM, bk), F32)],
        compiler_params=_cp("arbitrary", "arbitrary"),
    )(qkv, qkv, qkv, ck4, st, do_b, *exchange_parts)
    return outs[:5], list(outs[5:])


def _silu_and_grad(x):
    sg = _sigmoid(x)
    return x * sg, sg * (1.0 + x * (1.0 - sg))


SUBLANES = 8


def _conv_taps(cur, before, w_rows, bias):
    n, c = cur.shape
    cur3 = cur.reshape(n // SUBLANES, SUBLANES, c)
    sub = lax.broadcasted_iota(jnp.int32, (1, SUBLANES, c), 1)
    taps = []
    for k in range(SSM_CONV):
        sh = SSM_CONV - 1 - k
        if sh == 0:
            taps.append(cur3)
            continue
        rot = pltpu.roll(cur3, sh, 1)
        prev = jnp.concatenate([pltpu.roll(before, sh, 0)[None], rot[:-1]], axis=0)
        taps.append(jnp.where(sub < sh, prev, rot))
    pre = bias[None] + sum(w_rows[k][None] * taps[k] for k in range(SSM_CONV))
    return pre.reshape(n, c), [t.reshape(n, c) for t in taps]


def _conv_col_chunks(w_ref, b_ref, bc, cc):
    for c0 in range(0, bc, cc):
        cols = slice(c0, c0 + cc)
        yield cols, [w_ref[k:k + 1, cols] for k in range(SSM_CONV)], None if b_ref is None else b_ref[:, cols]


def _conv_specs(s, bs, bc):
    cur = pl.BlockSpec((bs, bc), lambda j, i: (i, j))
    halo = pl.BlockSpec((8, bc), lambda j, i: (jnp.maximum(i * (bs // 8) - 1, 0), j))
    w = pl.BlockSpec((SSM_CONV, bc), lambda j, i: (0, j))
    b = pl.BlockSpec((1, bc), lambda j, i: (0, j))
    return cur, halo, w, b


def _conv_fwd(xbc, w, b):
    s, c = xbc.shape
    bs, bc = _tile(s, 512), 1024
    rc, cc = bs, 128

    def body(x_ref, h_ref, w_ref, b_ref, o_ref):
        first = pl.program_id(1) == 0
        for cols, w_rows, bias in _conv_col_chunks(w_ref, b_ref, bc, cc):
            def step(r, before):
                rows = pl.ds(pl.multiple_of(r * rc, rc), rc)
                cur = x_ref[rows, cols]
                pre, _ = _conv_taps(cur, before, w_rows, bias)
                o_ref[rows, cols] = pre * _sigmoid(pre)
                return cur[rc - SUBLANES:]

            lax.fori_loop(0, bs // rc, step, jnp.where(first, 0.0, h_ref[:, cols]))

    cur, halo, ws, bsp = _conv_specs(s, bs, bc)
    return pl.pallas_call(body, name="conv_fwd", grid=(c // bc, s // bs), in_specs=[cur, halo, ws, bsp],
                          out_specs=cur, out_shape=jax.ShapeDtypeStruct((s, c), F32),
                          compiler_params=_cp("parallel", "parallel"))(xbc, xbc, w, b)


def _conv_bwd_pre(xbc, w, b, dact):
    s, c = xbc.shape
    bs, bc = _tile(s, 512), 1024
    rc, cc = _tile(bs, 256), 128

    def body(x_ref, h_ref, w_ref, b_ref, g_ref, dp_ref, dw_ref, db_ref):
        first = pl.program_id(1) == 0

        @pl.when(first)
        def _():
            dw_ref[...] = jnp.zeros_like(dw_ref)
            db_ref[...] = jnp.zeros_like(db_ref)

        for cols, w_rows, bias in _conv_col_chunks(w_ref, b_ref, bc, cc):
            def step(r, carry):
                before, sums = carry
                rows = pl.ds(pl.multiple_of(r * rc, rc), rc)
                cur = x_ref[rows, cols]
                pre, taps = _conv_taps(cur, before, w_rows, bias)
                dpre = g_ref[rows, cols].astype(F32) * _silu_and_grad(pre)[1]
                dp_ref[rows, cols] = dpre.astype(dp_ref.dtype)
                terms = [dpre * t for t in taps] + [dpre]
                sums = tuple(a + jnp.sum(t.reshape(rc // SUBLANES, SUBLANES, cc), axis=0) for a, t in zip(sums, terms))
                return cur[rc - SUBLANES:], sums

            zero = jnp.zeros((SUBLANES, cc), F32)
            _, sums = lax.fori_loop(0, bs // rc, step,
                                    (jnp.where(first, 0.0, h_ref[:, cols]), (zero,) * (SSM_CONV + 1)))
            for k in range(SSM_CONV):
                dw_ref[k:k + 1, cols] += jnp.sum(sums[k], axis=0, keepdims=True)
            db_ref[:, cols] += jnp.sum(sums[SSM_CONV], axis=0, keepdims=True)

    cur, halo, ws, bsp = _conv_specs(s, bs, bc)
    return pl.pallas_call(
        body, name="conv_bwd_pre", grid=(c // bc, s // bs), in_specs=[cur, halo, ws, bsp, cur],
        out_specs=[cur, ws, bsp],
        out_shape=[jax.ShapeDtypeStruct((s, c), BF16), jax.ShapeDtypeStruct((SSM_CONV, c), F32),
                   jax.ShapeDtypeStruct((1, c), F32)],
        compiler_params=_cp("parallel", "arbitrary"))(xbc, xbc, w, b, dact)


def _conv_bwd_in(dpre, w):
    s, c = dpre.shape
    bs, bc = _tile(s, 512), 1024
    nb = s // bs
    rc, cc = bs, 256
    nr = bs // rc

    def body(g_ref, n_ref, w_ref, o_ref):
        last = pl.program_id(1) == nb - 1
        sub = lax.broadcasted_iota(jnp.int32, (1, SUBLANES, cc), 1)
        for cols, w_rows, _ in _conv_col_chunks(w_ref, None, bc, cc):
            def step(i, after):
                rows = pl.ds(pl.multiple_of((nr - 1 - i) * rc, rc), rc)
                cur = g_ref[rows, cols].astype(F32)
                cur3 = cur.reshape(rc // SUBLANES, SUBLANES, cc)
                acc = w_rows[SSM_CONV - 1][None] * cur3
                for sh in range(1, SSM_CONV):
                    rot = pltpu.roll(cur3, SUBLANES - sh, 1)
                    nxt = jnp.concatenate([rot[1:], pltpu.roll(after, SUBLANES - sh, 0)[None]], axis=0)
                    acc = acc + w_rows[SSM_CONV - 1 - sh][None] * jnp.where(sub >= SUBLANES - sh, nxt, rot)
                o_ref[rows, cols] = acc.reshape(rc, cc).astype(o_ref.dtype)
                return cur[0:SUBLANES]

            lax.fori_loop(0, nr, step, jnp.where(last, 0.0, n_ref[0:SUBLANES, cols].astype(F32)))

    cur = pl.BlockSpec((bs, bc), lambda j, i: (i, j))
    nxt = pl.BlockSpec((16, bc), lambda j, i: (jnp.minimum((i + 1) * (bs // 16), s // 16 - 1), j))
    ws = pl.BlockSpec((SSM_CONV, bc), lambda j, i: (0, j))
    return pl.pallas_call(body, name="conv_bwd_in", grid=(c // bc, nb), in_specs=[cur, nxt, ws], out_specs=cur,
                          out_shape=jax.ShapeDtypeStruct((s, c), BF16),
                          compiler_params=_cp("parallel", "parallel"))(dpre, dpre, w)


def _dotT(a, b):
    return lax.dot_general(a.astype(BF16), b.astype(BF16), (((1,), (1,)), ((), ())), preferred_element_type=F32)


def _dot(a, b):
    return jnp.dot(a.astype(BF16), b.astype(BF16), preferred_element_type=F32)


N_PAIR = SSM_HEADS // HP
PAIRS_PER_GROUP = SSM_HEADS_PER_GROUP // HP


def _pair_consts():
    L = SSM_CHUNK
    lane = lax.broadcasted_iota(jnp.int32, (L, LANES), 1)
    lane1 = lax.broadcasted_iota(jnp.int32, (1, LANES), 1)
    li = lax.broadcasted_iota(jnp.int32, (L, L), 0)
    si = lax.broadcasted_iota(jnp.int32, (L, L), 1)
    return lane >= ATT_HEAD_DIM, lane1 >= ATT_HEAD_DIM, li, si


def _ssd_pair_fwd(xbc_act, ac_c, dt_r, ac_r, dsk_pair):
    s = xbc_act.shape[0]
    L, N, G = SSM_CHUNK, SSM_STATE, SSM_GROUPS
    nc = s // L

    def body(xbc_ref, ac_ref, dtr_ref, acr_ref, dsk_ref, y_ref, hp_ref, st_ref):
        @pl.when(pl.program_id(0) == 0)
        def _():
            st_ref[...] = jnp.zeros_like(st_ref)

        upper, up1, li, si = _pair_consts()
        for g in range(G):
            b_g = xbc_ref[:, SSM_INNER + g * N:SSM_INNER + (g + 1) * N]
            c_g = xbc_ref[:, SSM_INNER + G * N + g * N:SSM_INNER + G * N + (g + 1) * N]
            cb = _dotT(c_g, b_g)
            b_t = b_g.T
            for q in range(PAIRS_PER_GROUP):
                pp = g * PAIRS_PER_GROUP + q
                cols = slice(pp * LANES, (pp + 1) * LANES)
                xs = xbc_ref[:, cols]
                ht = st_ref[pp]
                hp_ref[0, pp] = ht
                y = dsk_ref[pp:pp + 1, :] * xs
                s_new = jnp.zeros((N, LANES), F32)
                ea, el = [], []
                for a in range(HP):
                    h = HP * pp + a
                    acol = jnp.broadcast_to(ac_ref[:, h:h + 1], (L, LANES))
                    arow, dtrow = acr_ref[h:h + 1, :], dtr_ref[h:h + 1, :]
                    alast = ac_ref[L - 1:L, h:h + 1]
                    decay = jnp.exp(jnp.where(li >= si, acol - arow, NEG))
                    xs_a = jnp.where(upper == (a == 1), xs, 0.0)
                    y = y + _dot(cb * decay * dtrow, xs_a)
                    s_new = s_new + _dot(b_t * (dtrow * jnp.exp(alast - arow)), xs_a)
                    ea.append(jnp.exp(acol))
                    el.append(jnp.exp(alast))
                y_ref[:, cols] = y + jnp.where(upper, ea[1], ea[0]) * _dot(c_g, ht)
                st_ref[pp] = ht * jnp.where(up1, el[1], el[0]) + s_new

    col = pl.BlockSpec((L, LANES), lambda c: (c, 0))
    row = pl.BlockSpec((LANES, L), lambda c: (0, c))
    return pl.pallas_call(
        body, name="ssd_fwd", grid=(nc,),
        in_specs=[pl.BlockSpec((L, SSM_CONV_DIM), lambda c: (c, 0)), col, row, row,
                  pl.BlockSpec((N_PAIR, LANES), lambda c: (0, 0))],
        out_specs=[pl.BlockSpec((L, SSM_INNER), lambda c: (c, 0)),
                   pl.BlockSpec((1, N_PAIR, N, LANES), lambda c: (c, 0, 0, 0))],
        out_shape=[jax.ShapeDtypeStruct((s, SSM_INNER), F32), jax.ShapeDtypeStruct((nc, N_PAIR, N, LANES), F32)],
        scratch_shapes=[pltpu.VMEM((N_PAIR, N, LANES), F32)],
        compiler_params=_cp("arbitrary"),
    )(xbc_act, ac_c, dt_r, ac_r, dsk_pair)


def _ssd_pair_bwd(xbc_act, dt_c, ac_c, dt_r, ac_r, hprev_all, dy, dsk_pair, a_row):
    s = xbc_act.shape[0]
    L, N, G = SSM_CHUNK, SSM_STATE, SSM_GROUPS
    nc = s // L
    rev = lambda c: nc - 1 - c

    def body(xbc_ref, dt_ref, ac_ref, dtr_ref, acr_ref, hp_ref, dy_ref, dsk_ref, arow_ref,
             dx_ref, ddt_ref, da_ref, dds_ref, dh_ref):
        @pl.when(pl.program_id(0) == 0)
        def _():
            dh_ref[...] = jnp.zeros_like(dh_ref)
            da_ref[...] = jnp.zeros_like(da_ref)
            dds_ref[...] = jnp.zeros_like(dds_ref)

        upper, up1, li, si = _pair_consts()
        lane = lax.broadcasted_iota(jnp.int32, (L, LANES), 1)
        sub = lax.broadcasted_iota(jnp.int32, (LANES, L), 0)
        lastrow = lax.broadcasted_iota(jnp.int32, (L, LANES), 0) == L - 1
        da_c = jnp.zeros((L, LANES), F32)
        da_r = jnp.zeros((LANES, L), F32)
        ddt_r = jnp.zeros((LANES, L), F32)
        for g in range(G):
            b_g = xbc_ref[:, SSM_INNER + g * N:SSM_INNER + (g + 1) * N]
            c_g = xbc_ref[:, SSM_INNER + G * N + g * N:SSM_INNER + G * N + (g + 1) * N]
            cb, cb_t = _dotT(c_g, b_g), _dotT(b_g, c_g)
            b_t, c_t = b_g.T, c_g.T
            dcb = jnp.zeros((L, L), F32)
            db_t = jnp.zeros((N, L), F32)
            dc = jnp.zeros((L, N), F32)
            for q in range(PAIRS_PER_GROUP):
                pp = g * PAIRS_PER_GROUP + q
                cols = slice(pp * LANES, (pp + 1) * LANES)
                xs, gy = xbc_ref[:, cols], dy_ref[:, cols].astype(F32)
                ht, dhn = hp_ref[0, pp], dh_ref[pp]
                acol = [jnp.broadcast_to(ac_ref[:, HP * pp + a:HP * pp + a + 1], (L, LANES)) for a in range(HP)]
                alast = [ac_ref[L - 1:L, HP * pp + a:HP * pp + a + 1] for a in range(HP)]
                ea = jnp.where(upper, jnp.exp(acol[1]), jnp.exp(acol[0]))
                el = jnp.where(up1, jnp.exp(alast[1]), jnp.exp(alast[0]))
                ge = gy * ea
                dc = dc + _dotT(ge, ht)
                dh_ref[pp] = _dot(c_t, ge) + dhn * el
                t_off = (ge * _dot(c_g, ht)).astype(BF16)
                hsum = jnp.sum(dhn * ht, axis=0, keepdims=True)
                dxs = dsk_ref[pp:pp + 1, :] * gy
                dds_ref[pp:pp + 1, :] += jnp.sum(gy * xs, axis=0, keepdims=True)
                for a in range(HP):
                    h = HP * pp + a
                    mine, mine1 = upper == (a == 1), up1 == (a == 1)
                    arow, dtrow = acr_ref[h:h + 1, :], dtr_ref[h:h + 1, :]
                    dtcol = jnp.broadcast_to(dt_ref[:, h:h + 1], (L, LANES))
                    xs_a, gy_a = jnp.where(mine, xs, 0.0), jnp.where(mine, gy, 0.0)
                    dhn_a = jnp.where(mine1, dhn, 0.0)
                    e_row = jnp.exp(alast[a] - arow)
                    w_row = dtrow * e_row
                    xd_t = _dotT(dhn_a, xs_a)
                    db_t = db_t + xd_t * w_row
                    dw = jnp.sum(b_t * xd_t, axis=0, keepdims=True)
                    de_e = dw * w_row
                    dal = (jnp.sum(jnp.where(mine1, hsum, 0.0), axis=1, keepdims=True) * jnp.exp(alast[a])
                           + jnp.sum(de_e, axis=1, keepdims=True))
                    dxs = dxs + _dot(b_g, dhn_a) * (dtcol * jnp.exp(alast[a] - acol[a]))
                    decay = jnp.exp(jnp.where(li >= si, acol[a] - arow, NEG))
                    decay_t = jnp.exp(jnp.where(si >= li, arow - acol[a], NEG))
                    m = cb * decay
                    dmdt = _dotT(gy_a, xs_a)
                    dxs = dxs + _dot(cb_t * decay_t * dtcol, gy_a)
                    dm = dmdt * dtrow
                    dcb = dcb + dm * decay
                    wb = (dm * m).astype(BF16)
                    onehot = jnp.where(lane == h, 1.0, 0.0).astype(BF16)
                    da_c = (da_c + jnp.dot(wb, onehot, preferred_element_type=F32)
                            + jnp.dot(jnp.where(mine, t_off, 0.0).astype(BF16), onehot, preferred_element_type=F32)
                            + jnp.where(lastrow & (lane == h), dal, 0.0))
                    da_r = jnp.where(sub == h, -(jnp.sum(wb.astype(F32), axis=0, keepdims=True) + de_e), da_r)
                    ddt_r = jnp.where(sub == h, dw * e_row + jnp.sum(dmdt * m, axis=0, keepdims=True), ddt_r)
                dx_ref[:, cols] = dxs.astype(dx_ref.dtype)
            dx_ref[:, SSM_INNER + g * N:SSM_INNER + (g + 1) * N] = (db_t + _dot(c_t, dcb)).T.astype(dx_ref.dtype)
            dx_ref[:, SSM_INNER + G * N + g * N:SSM_INNER + G * N + (g + 1) * N] = (
                dc + _dot(dcb, b_g)).astype(dx_ref.dtype)
        dda = jnp.dot(_tri(L, lower=False), da_c + da_r.T, precision=HI, preferred_element_type=F32)
        ddt_ref[...] = dda * arow_ref[...] + ddt_r.T
        da_ref[...] += jnp.sum(dda * dt_ref[...], axis=0, keepdims=True)

    col = pl.BlockSpec((L, LANES), lambda c: (rev(c), 0))
    row = pl.BlockSpec((LANES, L), lambda c: (0, rev(c)))
    vec = pl.BlockSpec((1, LANES), lambda c: (0, 0))
    pairs = pl.BlockSpec((N_PAIR, LANES), lambda c: (0, 0))
    return pl.pallas_call(
        body, name="ssd_bwd", grid=(nc,),
        in_specs=[pl.BlockSpec((L, SSM_CONV_DIM), lambda c: (rev(c), 0)), col, col, row, row,
                  pl.BlockSpec((1, N_PAIR, N, LANES), lambda c: (rev(c), 0, 0, 0)),
                  pl.BlockSpec((L, SSM_INNER), lambda c: (rev(c), 0)), pairs, vec],
        out_specs=[pl.BlockSpec((L, SSM_CONV_DIM), lambda c: (rev(c), 0)), col, vec, pairs],
        out_shape=[jax.ShapeDtypeStruct((s, SSM_CONV_DIM), BF16), jax.ShapeDtypeStruct((s, LANES), F32),
                   jax.ShapeDtypeStruct((1, LANES), F32), jax.ShapeDtypeStruct((N_PAIR, LANES), F32)],
        scratch_shapes=[pltpu.VMEM((N_PAIR, N, LANES), F32)],
        compiler_params=_cp("arbitrary"),
    )(xbc_act, dt_c, ac_c, dt_r, ac_r, hprev_all, dy, dsk_pair, a_row)


ROWS = 512
GW = SSM_INNER // SSM_GROUPS


def _rows(width, dtype=F32):
    return pl.BlockSpec((ROWS, width), lambda i: (i, 0))


def _vec(width):
    return pl.BlockSpec((1, width), lambda i: (0, 0))


def _gnorm_fwd(y, z, w):
    s = y.shape[0]

    def body(y_ref, z_ref, w_ref, o_ref):
        for g in range(SSM_GROUPS):
            cs = slice(g * GW, (g + 1) * GW)
            zz = z_ref[:, cs].astype(F32)
            u = y_ref[:, cs] * (zz * _sigmoid(zz))
            r = lax.rsqrt(jnp.mean(u * u, axis=1, keepdims=True) + RMS_EPS)
            o_ref[:, cs] = (u * r * w_ref[:, cs]).astype(o_ref.dtype)

    return pl.pallas_call(body, name="gnorm_fwd", grid=(s // ROWS,),
                          in_specs=[_rows(SSM_INNER), _rows(SSM_INNER), _vec(SSM_INNER)], out_specs=_rows(SSM_INNER),
                          out_shape=jax.ShapeDtypeStruct((s, SSM_INNER), BF16), compiler_params=_cp("parallel"))(y, z, w)


def _gnorm_bwd(y, z, w, do):
    s = y.shape[0]

    def body(y_ref, z_ref, w_ref, do_ref, dy_ref, dz_ref, dw_ref):
        @pl.when(pl.program_id(0) == 0)
        def _():
            dw_ref[...] = jnp.zeros_like(dw_ref)

        for g in range(SSM_GROUPS):
            cs = slice(g * GW, (g + 1) * GW)
            zz, yy, dd = z_ref[:, cs].astype(F32), y_ref[:, cs], do_ref[:, cs].astype(F32)
            sz, dsz = _silu_and_grad(zz)
            u = yy * sz
            r = lax.rsqrt(jnp.mean(u * u, axis=1, keepdims=True) + RMS_EPS)
            n = u * r
            dn = dd * w_ref[:, cs]
            dw_ref[:, cs] += jnp.sum(dd * n, axis=0, keepdims=True)
            du = r * (dn - n * jnp.mean(dn * n, axis=1, keepdims=True))
            dy_ref[:, cs] = (du * sz).astype(dy_ref.dtype)
            dz_ref[:, cs] = (du * yy * dsz).astype(dz_ref.dtype)

    return pl.pallas_call(
        body, name="gnorm_bwd", grid=(s // ROWS,),
        in_specs=[_rows(SSM_INNER), _rows(SSM_INNER), _vec(SSM_INNER), _rows(SSM_INNER)],
        out_specs=[_rows(SSM_INNER), _rows(SSM_INNER), _vec(SSM_INNER)],
        out_shape=[jax.ShapeDtypeStruct((s, SSM_INNER), BF16), jax.ShapeDtypeStruct((s, SSM_INNER), BF16),
                   jax.ShapeDtypeStruct((1, SSM_INNER), F32)],
        compiler_params=_cp("arbitrary"))(y, z, w, do)


def _mix_fwd(gl, bg, attn_d, ssm_d):
    s = gl.shape[0]
    d = D_MODEL

    def body(gl_ref, bg_ref, a_ref, m_ref, o_ref):
        g0 = _sigmoid(gl_ref[:, :d] + bg_ref[:, :d])
        g1 = _sigmoid(gl_ref[:, d:] + bg_ref[:, d:])
        o_ref[...] = (g0 * a_ref[...] + g1 * m_ref[...]).astype(o_ref.dtype)

    return pl.pallas_call(body, name="mix_fwd", grid=(s // ROWS,),
                          in_specs=[_rows(2 * d), _vec(2 * d), _rows(d), _rows(d)], out_specs=_rows(d),
                          out_shape=jax.ShapeDtypeStruct((s, d), BF16), compiler_params=_cp("parallel"))(
        gl, bg, attn_d, ssm_d)


def _mix_bwd(gl, bg, attn_d, ssm_d, dmix):
    s = gl.shape[0]
    d = D_MODEL

    def body(gl_ref, bg_ref, a_ref, m_ref, dm_ref, da_ref, ds_ref, dg_ref, db_ref):
        @pl.when(pl.program_id(0) == 0)
        def _():
            db_ref[...] = jnp.zeros_like(db_ref)

        g0 = _sigmoid(gl_ref[:, :d] + bg_ref[:, :d])
        g1 = _sigmoid(gl_ref[:, d:] + bg_ref[:, d:])
        dm = dm_ref[...].astype(F32)
        da_ref[...] = (dm * g0).astype(da_ref.dtype)
        ds_ref[...] = (dm * g1).astype(ds_ref.dtype)
        dl0 = dm * a_ref[...] * g0 * (1.0 - g0)
        dl1 = dm * m_ref[...] * g1 * (1.0 - g1)
        dg_ref[:, :d] = dl0.astype(dg_ref.dtype)
        dg_ref[:, d:] = dl1.astype(dg_ref.dtype)
        db_ref[:, :d] += jnp.sum(dl0, axis=0, keepdims=True)
        db_ref[:, d:] += jnp.sum(dl1, axis=0, keepdims=True)

    return pl.pallas_call(
        body, name="mix_bwd", grid=(s // ROWS,),
        in_specs=[_rows(2 * d), _vec(2 * d), _rows(d), _rows(d), _rows(d)],
        out_specs=[_rows(d), _rows(d), _rows(2 * d), _vec(2 * d)],
        out_shape=[jax.ShapeDtypeStruct((s, d), BF16), jax.ShapeDtypeStruct((s, d), BF16),
                   jax.ShapeDtypeStruct((s, 2 * d), BF16), jax.ShapeDtypeStruct((1, 2 * d), F32)],
        compiler_params=_cp("arbitrary"))(gl, bg, attn_d, ssm_d, dmix)


def _ln_stats(p):
    mu = jnp.mean(p, axis=1, keepdims=True)
    c = p - mu
    rstd = lax.rsqrt(jnp.mean(c * c, axis=1, keepdims=True) + LN_EPS)
    return c * rstd, rstd


def _ln_bwd(dy, xhat, rstd, g):
    dxh = dy * g
    return rstd * (dxh - jnp.mean(dxh, axis=1, keepdims=True) - xhat * jnp.mean(dxh * xhat, axis=1, keepdims=True))


def _ln1_fwd(x, mixed, g, b):
    s, d = x.shape

    def body(x_ref, m_ref, g_ref, b_ref, o_ref, ob_ref):
        xhat, _ = _ln_stats(DEEPNORM_ALPHA * x_ref[...] + m_ref[...])
        y = xhat * g_ref[...] + b_ref[...]
        o_ref[...] = y
        ob_ref[...] = y.astype(BF16)

    return pl.pallas_call(body, name="ln1_fwd", grid=(s // ROWS,), in_specs=[_rows(d), _rows(d), _vec(d), _vec(d)],
                          out_specs=[_rows(d), _rows(d)],
                          out_shape=[jax.ShapeDtypeStruct((s, d), F32), jax.ShapeDtypeStruct((s, d), BF16)],
                          compiler_params=_cp("parallel"))(x, mixed, g, b)


def _ln2_loss(x1, h, target, g, b):
    s, d = x1.shape

    def body(x_ref, h_ref, t_ref, g_ref, b_ref, dp_ref, dpb_ref, loss_ref, dg_ref, db_ref):
        @pl.when(pl.program_id(0) == 0)
        def _():
            loss_ref[...] = jnp.zeros_like(loss_ref)
            dg_ref[...] = jnp.zeros_like(dg_ref)
            db_ref[...] = jnp.zeros_like(db_ref)

        xhat, rstd = _ln_stats(DEEPNORM_ALPHA * x_ref[...] + h_ref[...])
        err = xhat * g_ref[...] + b_ref[...] - t_ref[...]
        part = 0.5 * jnp.sum(jnp.mean(err * err, axis=1, keepdims=True), axis=0, keepdims=True)
        loss_ref[...] += jnp.broadcast_to(part, loss_ref.shape)
        dy = err * (1.0 / d)
        dg_ref[...] += jnp.sum(dy * xhat, axis=0, keepdims=True)
        db_ref[...] += jnp.sum(dy, axis=0, keepdims=True)
        dp = _ln_bwd(dy, xhat, rstd, g_ref[...])
        dp_ref[...] = dp
        dpb_ref[...] = dp.astype(BF16)

    return pl.pallas_call(
        body, name="ln2_loss", grid=(s // ROWS,), in_specs=[_rows(d), _rows(d), _rows(d), _vec(d), _vec(d)],
        out_specs=[_rows(d), _rows(d), _vec(LANES), _vec(d), _vec(d)],
        out_shape=[jax.ShapeDtypeStruct((s, d), F32), jax.ShapeDtypeStruct((s, d), BF16),
                   jax.ShapeDtypeStruct((1, LANES), F32),
                   jax.ShapeDtypeStruct((1, d), F32), jax.ShapeDtypeStruct((1, d), F32)],
        compiler_params=_cp("arbitrary"))(x1, h, target, g, b)


def _ln1_bwd(x, mixed, g, dpre2, dffn):
    s, d = x.shape

    def body(x_ref, m_ref, g_ref, d2_ref, df_ref, dp_ref, dr_ref, dg_ref, db_ref):
        @pl.when(pl.program_id(0) == 0)
        def _():
            dg_ref[...] = jnp.zeros_like(dg_ref)
            db_ref[...] = jnp.zeros_like(db_ref)

        xhat, rstd = _ln_stats(DEEPNORM_ALPHA * x_ref[...] + m_ref[...])
        dy = DEEPNORM_ALPHA * d2_ref[...] + df_ref[...]
        dg_ref[...] += jnp.sum(dy * xhat, axis=0, keepdims=True)
        db_ref[...] += jnp.sum(dy, axis=0, keepdims=True)
        dp = _ln_bwd(dy, xhat, rstd, g_ref[...])
        dp_ref[...] = dp.astype(BF16)
        dr_ref[...] = DEEPNORM_ALPHA * dp

    return pl.pallas_call(
        body, name="ln1_bwd", grid=(s // ROWS,), in_specs=[_rows(d), _rows(d), _vec(d), _rows(d), _rows(d)],
        out_specs=[_rows(d), _rows(d), _vec(d), _vec(d)],
        out_shape=[jax.ShapeDtypeStruct((s, d), BF16), jax.ShapeDtypeStruct((s, d), F32),
                   jax.ShapeDtypeStruct((1, d), F32), jax.ShapeDtypeStruct((1, d), F32)],
        compiler_params=_cp("arbitrary"))(x, mixed, g, dpre2, dffn)


def _swiglu_fwd(gu):
    s = gu.shape[0]
    f = FFN_HIDDEN

    def body(g_ref, u_ref, o_ref):
        gg = g_ref[...].astype(F32)
        o_ref[...] = (gg * _sigmoid(gg) * u_ref[...].astype(F32)).astype(o_ref.dtype)

    return pl.pallas_call(
        body, name="swiglu_fwd", grid=(s // ROWS,),
        in_specs=[pl.BlockSpec((ROWS, f), lambda i: (i, 0)), pl.BlockSpec((ROWS, f), lambda i: (i, 1))],
        out_specs=_rows(f), out_shape=jax.ShapeDtypeStruct((s, f), BF16), compiler_params=_cp("parallel"))(gu, gu)


def _swiglu_bwd(gu, dact):
    s = gu.shape[0]
    f = FFN_HIDDEN

    def body(g_ref, u_ref, d_ref, o_ref):
        sg, dsg = _silu_and_grad(g_ref[...].astype(F32))
        dd = d_ref[...].astype(F32)
        o_ref[:, :f] = (dd * u_ref[...].astype(F32) * dsg).astype(o_ref.dtype)
        o_ref[:, f:] = (dd * sg).astype(o_ref.dtype)

    return pl.pallas_call(
        body, name="swiglu_bwd", grid=(s // ROWS,),
        in_specs=[pl.BlockSpec((ROWS, f), lambda i: (i, 0)), pl.BlockSpec((ROWS, f), lambda i: (i, 1)), _rows(f)],
        out_specs=_rows(2 * f), out_shape=jax.ShapeDtypeStruct((s, 2 * f), BF16),
        compiler_params=_cp("parallel"))(gu, gu, dact)


def _peer(k):
    x, y, c = lax.axis_index("x"), lax.axis_index("y"), lax.axis_index("c")
    kx, ky, kc = (k >> 2) & 1, (k >> 1) & 1, k & 1
    px = (1 - x) if kx else x
    py = (1 - y) if ky else y
    pc = (1 - c) if kc else c
    return (px, py, pc), 4 * px + 2 * py + pc


def _my_index():
    return 4 * lax.axis_index("x") + 2 * lax.axis_index("y") + lax.axis_index("c")


def _comm_copies(ins, outs, sems, scatter):
    send_sems, recv_sems, local_sems = sems
    me = _my_index()
    copies = [pltpu.make_async_copy(ins[t].at[me] if scatter else ins[t], outs[t].at[me], local_sems.at[t])
              for t in range(len(ins))]
    for k in range(1, N_DEV):
        peer, pidx = _peer(k)
        for t in range(len(ins)):
            copies.append(pltpu.make_async_remote_copy(
                src_ref=ins[t].at[pidx] if scatter else ins[t], dst_ref=outs[t].at[me],
                send_sem=send_sems.at[t, k - 1], recv_sem=recv_sems.at[t, k - 1], device_id=peer,
                device_id_type=pl.DeviceIdType.MESH))
    return copies


def _comm_sems(n):
    return [pltpu.SemaphoreType.DMA((n, N_DEV - 1)), pltpu.SemaphoreType.DMA((n, N_DEV - 1)),
            pltpu.SemaphoreType.DMA((n,))]


def _comm_out_shapes(parts, scatter):
    return [jax.ShapeDtypeStruct(p.shape if scatter else (N_DEV,) + p.shape, p.dtype) for p in parts]


def _all_gather(parts):
    n = len(parts)

    def body(*refs):
        ins, outs = refs[:n], refs[n:2 * n]
        send_sems, recv_sems, local_sems = refs[2 * n:]
        x, y, c = lax.axis_index("x"), lax.axis_index("y"), lax.axis_index("c")
        me, sibling = (x, y, c), (x, y, 1 - c)
        chips = [(1 - x, y), (x, 1 - y), (1 - x, 1 - y)]

        def copy(t, k, block, to, src=None):
            dst = outs[t].at[4 * block[0] + 2 * block[1] + block[2]]
            return pltpu.make_async_remote_copy(
                src_ref=dst if src is None else src, dst_ref=dst, send_sem=send_sems.at[t, k],
                recv_sem=recv_sems.at[t, k], device_id=to, device_id_type=pl.DeviceIdType.MESH)

        mine = [pltpu.make_async_copy(ins[t], outs[t].at[_my_index()], local_sems.at[t]) for t in range(n)]
        for cp in mine:
            cp.start()
        first = [copy(t, 0, me, sibling, src=ins[t]) for t in range(n)]
        first += [copy(t, 1 + j, me, (*chip, c), src=ins[t]) for j, chip in enumerate(chips) for t in range(n)]
        for cp in first:
            cp.start()
        passed = []
        for j, chip in enumerate(chips):
            for t in range(n):
                copy(t, 1 + j, (*chip, c), me).wait_recv()
                passed.append(copy(t, 4 + j, (*chip, c), sibling))
                passed[-1].start()
        for t in range(n):
            copy(t, 0, sibling, me).wait_recv()
            for j, chip in enumerate(chips):
                copy(t, 4 + j, (*chip, 1 - c), me).wait_recv()
        for cp in first + passed:
            cp.wait_send()
        for cp in mine:
            cp.wait()

    anyspec = pl.BlockSpec(memory_space=pl.ANY)
    return pl.pallas_call(body, name="all_gather", in_specs=[anyspec] * n, out_specs=[anyspec] * n,
                          out_shape=_comm_out_shapes(parts, False), scratch_shapes=_comm_sems(n))(*parts)


def _remote_scatter_copies(ins, lands, send_sems, recv_sems):
    me = _my_index()
    copies = []
    for k in range(1, N_DEV):
        peer, pidx = _peer(k)
        for t in range(len(ins)):
            copies.append(pltpu.make_async_remote_copy(
                src_ref=ins[t].at[pidx], dst_ref=lands[t].at[me], send_sem=send_sems.at[t * (N_DEV - 1) + k - 1],
                recv_sem=recv_sems.at[t * (N_DEV - 1) + k - 1], device_id=peer, device_id_type=pl.DeviceIdType.MESH))
    return copies


def _landing_zones(parts):
    me = _my_index()
    return [jnp.where(lax.broadcasted_iota(jnp.int32, p.shape, 0) == me, p, jnp.zeros_like(p)) for p in parts]


_HBM = pl.BlockSpec(memory_space=pltpu.HBM)
_SEM = pl.BlockSpec(memory_space=pltpu.SEMAPHORE)


def _exchange_start(parts, lands):
    n = len(parts)

    def body(*refs):
        ins, lnd, send_sems, recv_sems, token = refs[:n], refs[n:2 * n], refs[2 * n], refs[2 * n + 1], refs[-1]
        for cp in _remote_scatter_copies(ins, lnd, send_sems, recv_sems):
            cp.start()
        token[...] = jnp.zeros_like(token)

    hbm = [pltpu.HBM(p.shape, p.dtype) for p in parts]
    outs = pl.pallas_call(
        body, name="exchange_start",
        out_shape=[pltpu.SemaphoreType.DMA((n * (N_DEV - 1),)), pltpu.SemaphoreType.DMA((n * (N_DEV - 1),))] + hbm + hbm
        + [jax.ShapeDtypeStruct((8, LANES), F32)],
        in_specs=[_HBM] * (2 * n), out_specs=[_SEM, _SEM] + [_HBM] * (2 * n) + [pl.BlockSpec(memory_space=pltpu.VMEM)],
        input_output_aliases={t: 2 + t for t in range(2 * n)},
        compiler_params=pltpu.CompilerParams(has_side_effects=pltpu.SideEffectType.DATAFLOW_SIDE_EFFECTING),
    )(*[pltpu.with_memory_space_constraint(p, pltpu.HBM) for p in list(parts) + list(lands)])
    return outs[0], outs[1], list(outs[2:2 + n]), list(outs[2 + n:2 + 2 * n]), outs[-1]


def _exchange_wait(send_sems, recv_sems, parts, lands, after):
    n = len(parts)

    def body(*refs):
        ins, lnd, send_sems, recv_sems = refs[:n], refs[n:2 * n], refs[2 * n], refs[2 * n + 1]
        for cp in _remote_scatter_copies(ins, lnd, send_sems, recv_sems):
            cp.wait_send()
            cp.wait_recv()

    hbm = [pltpu.HBM(p.shape, p.dtype) for p in parts]
    outs = pl.pallas_call(
        body, name="exchange_wait", out_shape=hbm + hbm,
        in_specs=[_HBM] * (2 * n) + [_SEM, _SEM, pl.BlockSpec(memory_space=pl.ANY)], out_specs=[_HBM] * (2 * n),
        input_output_aliases={t: t for t in range(2 * n)},
        compiler_params=pltpu.CompilerParams(has_side_effects=pltpu.SideEffectType.DATAFLOW_SIDE_EFFECTING),
    )(*parts, *lands, send_sems, recv_sems, after)
    return list(outs[n:])


def _adamw(recv, w, m, v, name):
    _, r, c = w.shape
    br = _tile(r, 128)
    c1 = 1.0 / (1.0 - ADAM_B1 ** ADAM_STEP)
    c2 = 1.0 / (1.0 - ADAM_B2 ** ADAM_STEP)

    def body(r_ref, w_ref, m_ref, v_ref, g_ref, d_ref, mo_ref, vo_ref):
        g = r_ref[0].astype(F32)
        for k in range(1, N_DEV):
            g = g + r_ref[k].astype(F32)
        mn = ADAM_B1 * m_ref[0] + (1.0 - ADAM_B1) * g
        vn = ADAM_B2 * v_ref[0] + (1.0 - ADAM_B2) * (g * g)
        g_ref[0] = g
        mo_ref[0] = mn
        vo_ref[0] = vn
        d_ref[0] = -ADAM_LR * ((mn * c1) / (jnp.sqrt(vn * c2) + ADAM_EPS) + ADAM_WD * w_ref[0])

    blk = pl.BlockSpec((1, br, c), lambda i: (0, i, 0))
    return pl.pallas_call(
        body, name=name, grid=(r // br,),
        in_specs=[pl.BlockSpec((N_DEV, br, c), lambda i: (0, i, 0)), blk, blk, blk],
        out_specs=[blk] * 4, out_shape=[jax.ShapeDtypeStruct((1, r, c), F32)] * 4,
        compiler_params=_cp("parallel"))(recv, w, m, v)


def _lane_row(pairs):
    row = jnp.zeros((LANES,), F32)
    for lane0, vec in pairs:
        row = lax.dynamic_update_slice(row, vec.astype(F32), (lane0,))
    return row.reshape(1, LANES)


def _stage_in(x, wts, small):
    s = x.shape[0]
    a = -jnp.exp(small["a_log"])
    bias_row = _lane_row([(DT_LANE0, small["dt_bias"]), (F_LANE0, small["b_forget"])])
    a_row = _lane_row([(DT_LANE0, a)])
    conv_b = small["conv_b"].reshape(1, -1)
    norm_w = small["ssm_norm_w"].reshape(1, -1)
    bg = small["b_gates"].reshape(1, -1)
    g1, b1 = small["ln1_g"].reshape(1, -1), small["ln1_b"].reshape(1, -1)
    g2, b2 = small["ln2_g"].reshape(1, -1), small["ln2_b"].reshape(1, -1)
    d_skip = small["d_skip"]
    xb = x.astype(BF16)

    qkv = _mm(xb, wts["qkv"], out_dtype=BF16, name="f_qkv")
    z = _mm(xb, wts["z"], out_dtype=BF16, name="f_z")
    xbc = _mm(xb, wts["xbc"], name="f_xbc")
    gl = _mm(xb, wts["gate"], out_dtype=BF16, name="f_gate")
    fd = _mm(xb, wts["fd"], name="f_fd")
    dt_c, ac_c, cf_c, dt_r, ac_r, cf_r = _stats_fwd(fd, bias_row, a_row)
    bk = _att_blocks(s)[1]
    ck4 = cf_r[F_LANE0:F_LANE0 + ATT_HEADS].reshape(N_HP, HP, s // bk, bk)
    return dict(locals())


def _stage_mid(c, attn, lse, wts, target):
    x, xb, qkv, z, xbc, gl, fd, ck4 = (c[k] for k in ("x", "xb", "qkv", "z", "xbc", "gl", "fd", "ck4"))
    dt_c, ac_c, dt_r, ac_r, a_row, bias_row = (c[k] for k in ("dt_c", "ac_c", "dt_r", "ac_r", "a_row", "bias_row"))
    conv_b, norm_w, bg, g1, b1, g2, b2, d_skip = (c[k] for k in ("conv_b", "norm_w", "bg", "g1", "b1", "g2", "b2",
                                                                "d_skip"))
    conv_w = c["wts"]["conv"]
    attn_d = _mm(attn, wts["pa"], out_dtype=BF16, name="f_pa")
    xact = _conv_fwd(xbc, conv_w, conv_b)
    dsk_pair = jnp.repeat(d_skip, SSM_HEAD_DIM).reshape(N_PAIR, LANES)
    y, hprev = _ssd_pair_fwd(xact, ac_c, dt_r, ac_r, dsk_pair)
    ssm = _gnorm_fwd(y, z, norm_w)
    ssm_d = _mm(ssm, wts["ps"], out_dtype=BF16, name="f_ps")
    mix = _mix_fwd(gl, bg, attn_d, ssm_d)
    mixed = _mm(mix, wts["out"], name="f_out")
    x1, x1_b = _ln1_fwd(x, mixed, g1, b1)
    gu = _mm(x1_b, wts["gu"], out_dtype=BF16, name="f_gu")
    act = _swiglu_fwd(gu)
    h = _mm(act, wts["down"], name="f_down")
    dpre2, dpre2_b, loss_row, dg2, db2 = _ln2_loss(x1, h, target, g2, b2)

    d_act = _mm(dpre2_b, wts["down"], tb=True, out_dtype=BF16, name="b_down_x")
    dw_down = _mm(act, dpre2_b, ta=True, name="b_down_w")
    dgu = _swiglu_bwd(gu, d_act)
    dffn = _mm(dgu, wts["gu"], tb=True, name="b_gu_x")
    dw_gu = _mm(x1_b, dgu, ta=True, name="b_gu_w")
    dpre1, dxr, dg1, db1 = _ln1_bwd(x, mixed, g1, dpre2, dffn)
    dmix = _mm(dpre1, wts["out"], tb=True, out_dtype=BF16, name="b_out_x")
    dw_out = _mm(mix, dpre1, ta=True, name="b_out_w")
    dattn_d, dssm_d, dgl, dbg = _mix_bwd(gl, bg, attn_d, ssm_d, dmix)
    dssm = _mm(dssm_d, wts["ps"], tb=True, out_dtype=BF16, name="b_ps_x")
    dw_ps = _mm(ssm, dssm_d, ta=True, name="b_ps_w")
    dattn = _mm(dattn_d, wts["pa"], tb=True, name="b_pa_x")
    dw_pa = _mm(attn, dattn_d, ta=True, name="b_pa_w")
    dy, dz, dnw = _gnorm_bwd(y, z, norm_w, dssm)
    dxact, ddt, da_row, dds_pair = _ssd_pair_bwd(xact, dt_c, ac_c, dt_r, ac_r, hprev, dy, dsk_pair, a_row)
    dds = dds_pair.reshape(SSM_HEADS, SSM_HEAD_DIM).sum(axis=1)
    dpre_c, dconv_w, dconv_b = _conv_bwd_pre(xbc, conv_w, conv_b, dxact)
    dxbc = _conv_bwd_in(dpre_c, conv_w)
    st, do_b = _att_prep(dattn, attn, lse)
    late = dict(pa=dw_pa, ps=dw_ps, out=dw_out, gu=dw_gu, down=dw_down)
    keep = ("st", "do_b", "ddt", "dxr", "dz", "dxbc", "dgl", "dconv_w", "dconv_b", "da_row", "dds", "dnw", "dbg",
            "dg1", "db1", "dg2", "db2", "loss_row")
    loc = locals()
    return {**c, **{k: loc[k] for k in keep}}, late


def _stage_out_w(c, att_grads):
    dq, dk, dv, dck, dcq = att_grads
    xb, fd, bias_row, ddt, dz, dxbc, dgl = (c[k] for k in ("xb", "fd", "bias_row", "ddt", "dz", "dxbc", "dgl"))
    s, a = xb.shape[0], c["a"]
    dcum = dck.reshape(ATT_HEADS, s) + dcq.reshape(ATT_HEADS, s)
    dfd, dbias = _stats_bwd(fd, bias_row, ddt, jnp.zeros((LANES, s), F32).at[F_LANE0:F_LANE0 + ATT_HEADS].set(dcum))
    dproj = (dq, dk, dv, dz, dxbc, dgl, dfd)
    dw_in = [_mm(xb, g_, ta=True, tb=(i == 0), name=f"b_in_w{i}") for i, g_ in enumerate(dproj)]
    grads = dict(q=dw_in[0], k=dw_in[1], v=dw_in[2], z=dw_in[3], xbc=dw_in[4], gate=dw_in[5], fd=dw_in[6],
                 conv=c["dconv_w"])
    small_g = dict(
        b_forget=dbias[0, F_LANE0:F_LANE0 + ATT_HEADS], conv_b=c["dconv_b"][0], dt_bias=dbias[0, :SSM_HEADS],
        a_log=c["da_row"][0, :SSM_HEADS] * a, d_skip=c["dds"], ssm_norm_w=c["dnw"][0], b_gates=c["dbg"][0],
        ln1_g=c["dg1"][0], ln1_b=c["db1"][0], ln2_g=c["dg2"][0], ln2_b=c["db2"][0])
    return c["loss_row"][0, 0], grads, small_g, dproj


def _stage_out_x(c, dproj, token):
    wts, d = c["wts"], D_MODEL
    wq = wts["qkv"][:, :d] + token.astype(BF16)
    wk, wv = wts["qkv"][:, d:2 * d], wts["qkv"][:, 2 * d:]
    dx = c["dxr"]
    for i, (g_, w_) in enumerate(zip(dproj, (wq, wk, wv, wts["z"], wts["xbc"], wts["gate"], wts["fd"]))):
        dx = _mm(g_, w_, ta=(i == 0), tb=True, add=dx, name=f"b_in_x{i}")
    return dx


BIG = ("w_in", "w_proj_attn", "w_proj_ssm", "w_out", "w_ffn_gate", "w_ffn_up", "w_ffn_down", "conv_w")
EARLY = ("w_in", "conv_w")
LATE = ("w_proj_attn", "w_proj_ssm", "w_out", "w_ffn_gate", "w_ffn_up", "w_ffn_down")
SMALL = ("b_forget", "conv_b", "dt_bias", "a_log", "d_skip", "ssm_norm_w", "b_gates", "ln1_g", "ln1_b", "ln2_g",
         "ln2_b")
SMALL_ROWS = 96
IN_SHARD = IN_WIDTH // N_DEV
IN_SEGMENTS = (("q", 0, 1024), ("k", 1024, 1024), ("v", 2048, 1024), ("f", 3072, ATT_HEADS), ("z", 3088, SSM_INNER),
               ("xbc", 5136, SSM_CONV_DIM), ("dt", 8208, SSM_HEADS), ("gate", 8240, 2 * D_MODEL))


def _cols_from_shards(shards, lo, hi):
    w = shards[0].shape[1]
    pieces = []
    for j in range(len(shards)):
        a, b = max(lo, j * w), min(hi, (j + 1) * w)
        if a < b:
            pieces.append(shards[j][:, a - j * w:b - j * w])
    return pieces[0] if len(pieces) == 1 else jnp.concatenate(pieces, axis=1)


def _shards_from_parts(parts, width):
    shards = []
    for j in range(N_DEV):
        lo, hi = j * width, (j + 1) * width
        pieces = []
        for mat, c0 in parts:
            a, b = max(lo, c0), min(hi, c0 + mat.shape[1])
            if a < b:
                pieces.append(mat[:, a - c0:b - c0])
        shards.append(pieces[0] if len(pieces) == 1 else jnp.concatenate(pieces, axis=1))
    return shards


def _pack_small(vals):
    flat = jnp.concatenate([vals[n].reshape(-1) for n in SMALL])
    return jnp.pad(flat, (0, SMALL_ROWS * LANES - flat.shape[0])).reshape(SMALL_ROWS, LANES)


def _unpack_small(pack, shapes):
    flat = pack.reshape(-1)
    out, off = {}, 0
    for n in SMALL:
        sz = math.prod(shapes[n])
        out[n] = flat[off:off + sz].reshape(shapes[n])
        off += sz
    return out


def kernel(x, w_in, b_forget, conv_w, conv_b, dt_bias, a_log, d_skip, ssm_norm_w, w_proj_attn, w_proj_ssm, b_gates, w_out, ln1_g, ln1_b, w_ffn_gate, w_ffn_up, w_ffn_down, ln2_g, ln2_b, loss_target, m_w_in, m_b_forget, m_conv_w, m_conv_b, m_dt_bias, m_a_log, m_d_skip, m_ssm_norm_w, m_w_proj_attn, m_w_proj_ssm, m_b_gates, m_w_out, m_ln1_g, m_ln1_b, m_w_ffn_gate, m_w_ffn_up, m_w_ffn_down, m_ln2_g, m_ln2_b, v_w_in, v_b_forget, v_conv_w, v_conv_b, v_dt_bias, v_a_log, v_d_skip, v_ssm_norm_w, v_w_proj_attn, v_w_proj_ssm, v_b_gates, v_w_out, v_ln1_g, v_ln1_b, v_w_ffn_gate, v_w_ffn_up, v_w_ffn_down, v_ln2_g, v_ln2_b):
    args = dict(locals())
    d, f = D_MODEL, FFN_HIDDEN
    big_w = {n: args[n][0] for n in BIG}
    small_w = {n: args[n][0] for n in SMALL}
    big_shapes = {n: args[n].shape for n in BIG}
    small_shapes = {n: args[n].shape for n in SMALL}

    early = dict(zip(EARLY, _all_gather([big_w["w_in"].astype(BF16), big_w["conv_w"]])))
    in_shards = [early["w_in"][j] for j in range(N_DEV)]
    seg = {n: _cols_from_shards(in_shards, c0, c0 + w) for n, c0, w in IN_SEGMENTS}
    wfd = jnp.concatenate([seg["dt"], seg["f"], jnp.zeros((d, LANES - SSM_HEADS - ATT_HEADS), BF16)], axis=1)
    wts = dict(qkv=jnp.concatenate([seg["q"], seg["k"], seg["v"]], axis=1), z=seg["z"], xbc=seg["xbc"],
               gate=seg["gate"], fd=wfd, conv=jnp.concatenate([early["conv_w"][j] for j in range(N_DEV)], axis=1))

    ctx = _stage_in(x[0], wts, small_w)
    attn, lse, gathered = _attention_fwd(ctx["qkv"], ctx["ck4"], [big_w[n].astype(BF16) for n in LATE])
    full = dict(zip(LATE, gathered))
    late_w = dict(
        pa=full["w_proj_attn"].reshape(d, d), ps=full["w_proj_ssm"].reshape(SSM_INNER, d),
        out=full["w_out"].reshape(d, d),
        gu=jnp.concatenate([full["w_ffn_gate"][j] for j in range(N_DEV)]
                           + [full["w_ffn_up"][j] for j in range(N_DEV)], axis=1),
        down=full["w_ffn_down"].reshape(f, d))
    ctx, gl = _stage_mid(ctx, attn, lse, late_w, loss_target[0])
    late_dest = dict(
        w_ffn_gate=jnp.stack([s_.astype(BF16) for s_ in _shards_from_parts([(gl["gu"][:, :f], 0)], f // N_DEV)]),
        w_ffn_up=jnp.stack([s_.astype(BF16) for s_ in _shards_from_parts([(gl["gu"][:, f:], 0)], f // N_DEV)]))
    for n, key in (("w_proj_attn", "pa"), ("w_proj_ssm", "ps"), ("w_out", "out"), ("w_ffn_down", "down")):
        late_dest[n] = gl[key].astype(BF16).reshape((N_DEV,) + big_shapes[n][1:])
    att_grads, late_recv = _attention_bwd(ctx["qkv"], ctx["ck4"], ctx["st"], ctx["do_b"], [late_dest[n] for n in LATE])
    loss_part, g, small_g, dproj = _stage_out_w(ctx, att_grads)
    loss = lax.psum(loss_part, ("x", "y", "c"))

    gfd = g["fd"]
    in_parts = dict(q=g["q"], k=g["k"], v=g["v"], f=gfd[:, F_LANE0:F_LANE0 + ATT_HEADS], z=g["z"], xbc=g["xbc"],
                    dt=gfd[:, DT_LANE0:DT_LANE0 + SSM_HEADS], gate=g["gate"])
    win_dest = jnp.stack([s_.astype(BF16) for s_ in
                          _shards_from_parts([(in_parts[n], c0) for n, c0, _ in IN_SEGMENTS], IN_SHARD)])
    conv_dest = jnp.stack(_shards_from_parts([(g["conv"], 0)], SSM_CONV_DIM // N_DEV))
    small_pack = _pack_small(small_g)
    last_parts = [win_dest, conv_dest, jnp.broadcast_to(small_pack, (N_DEV,) + small_pack.shape)]
    send_sems, recv_sems, parts_thru, lands_thru, token = _exchange_start(last_parts, _landing_zones(last_parts))
    grad_x = _stage_out_x(ctx, dproj, token[0, 0])
    early_recv = _exchange_wait(send_sems, recv_sems, parts_thru, lands_thru, grad_x)
    recv = dict(zip(LATE, late_recv))
    recv["w_in"], recv["conv_w"] = early_recv[0], early_recv[1]

    outs = {}
    for n in BIG:
        outs[n] = _adamw(recv[n], args[n], args["m_" + n], args["v_" + n], name="adamw_" + n)
    small4 = _adamw(early_recv[2], _pack_small(small_w)[None], _pack_small({n: args["m_" + n][0] for n in SMALL})[None],
                    _pack_small({n: args["v_" + n][0] for n in SMALL})[None], name="adamw_small")
    small_out = [_unpack_small(p, small_shapes) for p in small4]
    for n in SMALL:
        outs[n] = [so[n] for so in small_out]

    order = ("w_in", "b_forget", "conv_w", "conv_b", "dt_bias", "a_log", "d_skip", "ssm_norm_w", "w_proj_attn",
             "w_proj_ssm", "b_gates", "w_out", "ln1_g", "ln1_b", "w_ffn_gate", "w_ffn_up", "w_ffn_down", "ln2_g",
             "ln2_b")
    res = [loss, grad_x[None]]
    for i in range(4):
        res += [outs[n][i] for n in order]
    return tuple(res)
```

```python
import functools
import math

import jax
import jax.numpy as jnp
from jax import lax
from jax.experimental import pallas as pl
from jax.experimental.pallas import tpu as pltpu

F32 = jnp.float32
BF16 = jnp.bfloat16

N_DEV = 8
D_MODEL = 1024
ATT_HEADS = 16
ATT_HEAD_DIM = 64
SSM_INNER = 2048
SSM_HEADS = 32
SSM_HEAD_DIM = 64
SSM_GROUPS = 4
SSM_HEADS_PER_GROUP = 8
SSM_STATE = 128
SSM_CONV = 4
SSM_CHUNK = 128
SSM_CONV_DIM = 3072
FFN_HIDDEN = 2816
IN_WIDTH = 10288
DEEPNORM_ALPHA = 2.0 ** 0.25
LN_EPS = 1e-5
RMS_EPS = 1e-5
ADAM_LR, ADAM_B1, ADAM_B2, ADAM_EPS, ADAM_WD, ADAM_STEP = 0.001, 0.9, 0.999, 1e-08, 0.01, 10
ATT_SCALE = 1.0 / math.sqrt(ATT_HEAD_DIM)

LANES = 128
VMEM_LIMIT = 56 * 1024 * 1024
NEG = -1e30

DT_LANE0 = 0
F_LANE0 = 32
HI = lax.Precision.HIGHEST


def _cp(*sem):
    return pltpu.CompilerParams(dimension_semantics=sem, vmem_limit_bytes=VMEM_LIMIT)


def _tile(n, cap=1408):
    for t in (3072, 2816, 2048, 1536, 1408, 1024, 512, 384, 256, 128):
        if t <= cap and n % t == 0:
            return t
    return n


MM_VMEM_BUDGET = VMEM_LIMIT - 4 * 2 ** 20


def _mm_tiles(m, n, k, a_bytes, b_bytes, out_bytes, has_add):
    tm = _tile(m)

    def need(tn, tk):
        blocks = tm * tk * a_bytes + tk * tn * b_bytes + tm * tn * (out_bytes + (4 if has_add else 0))
        casts = (tm * tk * 2 if a_bytes == 4 else 0) + (tk * tn * 2 if b_bytes == 4 else 0)
        return 2 * blocks + casts + tm * tn * 4

    tns = [t for t in (3072, 2816, 2048, 1536, 1408, 1024, 512, 384, 256, 128) if n % t == 0] or [n]
    for tk in ([k] if k <= 3072 else []) + [t for t in (2816, 1024, 512, 256, 128) if k % t == 0]:
        fits = [tn for tn in tns if need(tn, tk) <= MM_VMEM_BUDGET]
        if fits and fits[0] >= min(1024, tns[0]):
            return tm, fits[0], tk
    return tm, tns[-1], tk


def _sigmoid(x):
    return 1.0 / (1.0 + jnp.exp(-x))


def _mm(a, b, *, ta=False, tb=False, out_dtype=F32, add=None, name):
    m, k = (a.shape[1], a.shape[0]) if ta else a.shape
    n = b.shape[0] if tb else b.shape[1]
    assert (b.shape[1] if tb else b.shape[0]) == k
    tm, tn, tk = _mm_tiles(m, n, k, a.dtype.itemsize, b.dtype.itemsize, jnp.dtype(out_dtype).itemsize, add is not None)
    nk = k // tk
    dims = (((0,) if ta else (1,), (1,) if tb else (0,)), ((), ()))

    def body_single(*refs):
        a_ref, b_ref = refs[:2]
        r = lax.dot_general(a_ref[...].astype(BF16), b_ref[...].astype(BF16), dims, preferred_element_type=F32)
        if add is not None:
            r = r + refs[2][...]
        refs[-1][...] = r.astype(refs[-1].dtype)

    def body(*refs):
        if add is None:
            a_ref, b_ref, o_ref, acc_ref = refs
        else:
            a_ref, b_ref, c_ref, o_ref, acc_ref = refs
        kk = pl.program_id(2)

        @pl.when(kk == 0)
        def _():
            acc_ref[...] = jnp.zeros_like(acc_ref)

        acc_ref[...] += lax.dot_general(a_ref[...].astype(BF16), b_ref[...].astype(BF16), dims,
                                        preferred_element_type=F32)

        @pl.when(kk == nk - 1)
        def _():
            r = acc_ref[...]
            if add is not None:
                r = r + c_ref[...]
            o_ref[...] = r.astype(o_ref.dtype)

    a_spec = pl.BlockSpec((tk, tm), lambda i, j, kk: (kk, i)) if ta else pl.BlockSpec((tm, tk), lambda i, j, kk: (i, kk))
    b_spec = pl.BlockSpec((tn, tk), lambda i, j, kk: (j, kk)) if tb else pl.BlockSpec((tk, tn), lambda i, j, kk: (kk, j))
    o_spec = pl.BlockSpec((tm, tn), lambda i, j, kk: (i, j))
    in_specs, args = [a_spec, b_spec], [a, b]
    if add is not None:
        in_specs.append(o_spec)
        args.append(add)
    return pl.pallas_call(
        body_single if nk == 1 else body, name=name, grid=(m // tm, n // tn, nk), in_specs=in_specs, out_specs=o_spec,
        out_shape=jax.ShapeDtypeStruct((m, n), out_dtype),
        scratch_shapes=[] if nk == 1 else [pltpu.VMEM((tm, tn), F32)],
        compiler_params=_cp("parallel", "parallel", "arbitrary"),
    )(*args)


def _mm_sum(terms, add, name):
    m, n = add.shape
    ks = [a.shape[0] if ta else a.shape[1] for a, ta, _, _ in terms]

    def need(tm):
        blocks = sum(tm * k * a.dtype.itemsize + n * k * b.dtype.itemsize for (a, _, b, _), k in zip(terms, ks))
        casts = sum(tm * k * 2 for (a, _, _, _), k in zip(terms, ks) if a.dtype.itemsize == 4)
        return 2 * (blocks + 2 * tm * n * 4) + casts + tm * n * 4

    tm = next(t for t in (1024, 512, 256, 128) if m % t == 0 and need(t) <= MM_VMEM_BUDGET)

    def body(*refs):
        r = refs[-2][...]
        for i, (_, ta, _, _) in enumerate(terms):
            dims = (((0,) if ta else (1,), (1,)), ((), ()))
            r = r + lax.dot_general(refs[2 * i][...].astype(BF16), refs[2 * i + 1][...].astype(BF16), dims,
                                    preferred_element_type=F32)
        refs[-1][...] = r

    in_specs, args = [], []
    for (a, ta, b, jb), k in zip(terms, ks):
        in_specs += [pl.BlockSpec((k, tm), lambda i: (0, i)) if ta else pl.BlockSpec((tm, k), lambda i: (i, 0)),
                     pl.BlockSpec((n, k), lambda i, jb=jb: (0, jb))]
        args += [a, b]
    o_spec = pl.BlockSpec((tm, n), lambda i: (i, 0))
    return pl.pallas_call(body, name=name, grid=(m // tm,), in_specs=in_specs + [o_spec], out_specs=o_spec,
                          out_shape=jax.ShapeDtypeStruct((m, n), F32), compiler_params=_cp("parallel"))(*args, add)


def _tri(n, lower=True):
    r = lax.broadcasted_iota(jnp.int32, (n, n), 0)
    c = lax.broadcasted_iota(jnp.int32, (n, n), 1)
    return jnp.where((r >= c) if lower else (c >= r), 1.0, 0.0).astype(F32)


def _stats_fwd(fd, bias_row, a_row):
    s = fd.shape[0]
    blk = SSM_CHUNK

    def body(fd_ref, bias_ref, a_ref, dt_ref, ac_ref, cf_ref, dtr_ref, acr_ref, cfr_ref, carry_ref):
        @pl.when(pl.program_id(0) == 0)
        def _():
            carry_ref[...] = jnp.zeros_like(carry_ref)

        v = fd_ref[...] + bias_ref[...]
        dt = jnp.maximum(v, 0.0) + jnp.log(1.0 + jnp.exp(-jnp.abs(v)))
        lf = jnp.minimum(v, 0.0) - jnp.log(1.0 + jnp.exp(-jnp.abs(v)))
        tri = _tri(blk)
        ac = jnp.dot(tri, dt * a_ref[...], precision=HI, preferred_element_type=F32)
        cf = jnp.dot(tri, lf, precision=HI, preferred_element_type=F32) + carry_ref[0:1, :]
        carry_ref[...] = carry_ref[...] + jnp.sum(lf, axis=0, keepdims=True)
        dt_ref[...] = dt
        ac_ref[...] = ac
        cf_ref[...] = cf
        dtr_ref[...] = dt.T
        acr_ref[...] = ac.T
        cfr_ref[...] = cf.T

    col = pl.BlockSpec((blk, LANES), lambda i: (i, 0))
    row = pl.BlockSpec((LANES, blk), lambda i: (0, i))
    vec = pl.BlockSpec((1, LANES), lambda i: (0, 0))
    return pl.pallas_call(
        body, name="stats_fwd", grid=(s // blk,), in_specs=[col, vec, vec],
        out_specs=[col, col, col, row, row, row],
        out_shape=[jax.ShapeDtypeStruct((s, LANES), F32)] * 3 + [jax.ShapeDtypeStruct((LANES, s), F32)] * 3,
        scratch_shapes=[pltpu.VMEM((8, LANES), F32)],
        compiler_params=_cp("arbitrary"),
    )(fd, bias_row, a_row)


def _stats_bwd(fd, bias_row, ddt, dcum_rows):
    s = fd.shape[0]
    blk = SSM_CHUNK
    nb = s // blk

    def body(fd_ref, bias_ref, ddt_ref, dck_ref, o_ref, db_ref, carry_ref):
        @pl.when(pl.program_id(0) == 0)
        def _():
            carry_ref[...] = jnp.zeros_like(carry_ref)
            db_ref[...] = jnp.zeros_like(db_ref)

        v = fd_ref[...] + bias_ref[...]
        dcum = dck_ref[...].T
        dlf = jnp.dot(_tri(blk, lower=False), dcum, precision=HI, preferred_element_type=F32) + carry_ref[0:1, :]
        carry_ref[...] = carry_ref[...] + jnp.sum(dcum, axis=0, keepdims=True)
        lane = lax.broadcasted_iota(jnp.int32, v.shape, 1)
        g = jnp.where(lane < F_LANE0, ddt_ref[...] * _sigmoid(v), dlf * _sigmoid(-v))
        g = jnp.where(lane < F_LANE0 + ATT_HEADS, g, 0.0)
        o_ref[...] = g.astype(o_ref.dtype)
        db_ref[...] += jnp.sum(g, axis=0, keepdims=True)

    col = pl.BlockSpec((blk, LANES), lambda i: (nb - 1 - i, 0))
    row = pl.BlockSpec((LANES, blk), lambda i: (0, nb - 1 - i))
    vec = pl.BlockSpec((1, LANES), lambda i: (0, 0))
    return pl.pallas_call(
        body, name="stats_bwd", grid=(nb,), in_specs=[col, vec, col, row], out_specs=[col, vec],
        out_shape=[jax.ShapeDtypeStruct((s, LANES), BF16), jax.ShapeDtypeStruct((1, LANES), F32)],
        scratch_shapes=[pltpu.VMEM((8, LANES), F32)],
        compiler_params=_cp("arbitrary"),
    )(fd, bias_row, ddt, dcum_rows)


HP = LANES // ATT_HEAD_DIM
N_HP = ATT_HEADS // HP


def _att_blocks(s):
    return (512, 1024) if s % 1024 == 0 and s >= 4096 else (64, 128)


_QK = (((1,), (1,)), ((), ()))
_HALF = ATT_HEAD_DIM // 2
_PAD = 16


def _head_cols(a):
    return slice(a * ATT_HEAD_DIM, (a + 1) * ATT_HEAD_DIM)


def _causal(shape, off):
    r = lax.broadcasted_iota(jnp.int32, shape, 0)
    c = lax.broadcasted_iota(jnp.int32, shape, 1)
    return c <= r + off


def _attention_fwd(qkv, ck4, gather_parts):
    s = qkv.shape[0]
    bk = _att_blocks(s)[1]
    bq = bk
    nq, nk = s // bq, s // bk
    n = len(gather_parts)

    def body(q_ref, k_ref, v_ref, ck_ref, *rest):
        comm_in, (o_ref, lse_ref), comm_out, sems = rest[:n], rest[n:n + 2], rest[n + 2:2 * n + 2], rest[2 * n + 2:]
        i = pl.program_id(1)
        if n:
            @pl.when((pl.program_id(0) == 0) & (i == 0))
            def _():
                for cp in _comm_copies(comm_in, comm_out, sems, False):
                    cp.start()

        n_full = (i * bq) // bk
        qs = [(q_ref[:, _head_cols(a)].astype(F32) * ATT_SCALE).astype(BF16) for a in range(HP)]

        upper_q = lax.broadcasted_iota(jnp.int32, (bq, LANES), 1) >= ATT_HEAD_DIM

        def absorb(j, carry, keys=slice(0, bk), rows=slice(0, bq), masked=False):
            nk_ = keys.stop - keys.start
            ks = pl.ds(pl.multiple_of(j * bk, bk) + keys.start, nk_)
            v_both = v_ref[ks, :]
            out = []
            for a in range(HP):
                m, acc = carry[a]
                sc = lax.dot_general(qs[a][rows], k_ref[ks, _head_cols(a)], _QK, preferred_element_type=F32)
                sc = sc - ck_ref[0, a, pl.ds(j, 1), keys]
                if masked:
                    sc = jnp.where(_causal(sc.shape, 0), sc, NEG)
                m_new = jnp.maximum(m, jnp.max(sc, axis=1, keepdims=True))
                p = jnp.exp((sc - m_new).astype(BF16))
                upper_v = lax.broadcasted_iota(jnp.int32, (nk_, LANES), 1) >= ATT_HEAD_DIM
                v_aug = jnp.where(upper_v == (a == 1), v_both, jnp.ones_like(v_both))
                acc = jnp.exp(m - m_new) * acc + jnp.dot(p, v_aug, preferred_element_type=F32)
                out.append((m_new, acc))
            return tuple(out)

        init = tuple((jnp.full((bq, 1), NEG, F32), jnp.zeros((bq, LANES), F32)) for _ in range(HP))
        carry = lax.fori_loop(0, n_full, absorb, init)
        hq = bq // 2
        carry = absorb(n_full, carry, keys=slice(0, hq), masked=True)
        low = absorb(n_full, tuple((m[hq:], acc[hq:]) for m, acc in carry), keys=slice(hq, bk), rows=slice(hq, bq),
                     masked=True)
        carry = tuple((jnp.concatenate([m[:hq], ml], axis=0), jnp.concatenate([acc[:hq], al], axis=0))
                      for (m, acc), (ml, al) in zip(carry, low))
        outs, lses = [], []
        for a in range(HP):
            m, acc = carry[a]
            l = pltpu.roll(acc, ATT_HEAD_DIM, 1)
            outs.append(acc / l)
            lses.append(m + jnp.log(l))
        o_ref[...] = jnp.where(upper_q, outs[1], outs[0])
        lse_ref[...] = jnp.where(upper_q, lses[1], lses[0])
        if n:
            @pl.when((pl.program_id(0) == N_HP - 1) & (i == nq - 1))
            def _():
                for cp in _comm_copies(comm_in, comm_out, sems, False):
                    cp.wait()

    q_spec = pl.BlockSpec((bq, LANES), lambda h, i: (i, h))
    anyspec = pl.BlockSpec(memory_space=pl.ANY)
    res = pl.pallas_call(
        body, name="att_fwd", grid=(N_HP, nq),
        in_specs=[q_spec, pl.BlockSpec((s, LANES), lambda h, i: (0, N_HP + h)),
                  pl.BlockSpec((s, LANES), lambda h, i: (0, 2 * N_HP + h)),
                  pl.BlockSpec((1, HP, nk, bk), lambda h, i: (h, 0, 0, 0))] + [anyspec] * n,
        out_specs=[q_spec, q_spec] + [anyspec] * n,
        out_shape=[jax.ShapeDtypeStruct((s, D_MODEL), F32)] * 2 + _comm_out_shapes(gather_parts, False),
        scratch_shapes=_comm_sems(n) if n else [],
        compiler_params=_cp("arbitrary", "arbitrary"),
    )(qkv, qkv, qkv, ck4, *gather_parts)
    return res[0], res[1], list(res[2:])


def _att_prep(do, o, lse_rep):
    s = do.shape[0]
    bs = _tile(s, 512)

    def body(do_ref, o_ref, lse_ref, st_ref, dob_ref):
        r = lax.broadcasted_iota(jnp.int32, (LANES, LANES), 0) // ATT_HEAD_DIM
        c = lax.broadcasted_iota(jnp.int32, (LANES, LANES), 1) // ATT_HEAD_DIM
        e = jnp.where(r == c, 1.0, 0.0).astype(F32)
        lane = lax.broadcasted_iota(jnp.int32, (bs, LANES), 1)
        for p in range(D_MODEL // LANES):
            cs = slice(p * LANES, (p + 1) * LANES)
            dd = do_ref[:, cs]
            delta = jnp.dot(dd * o_ref[:, cs], e, precision=HI, preferred_element_type=F32)
            st_ref[:, cs] = jnp.where(lane % ATT_HEAD_DIM < _HALF, lse_ref[:, cs], delta)
            dob_ref[:, cs] = dd.astype(BF16)

    spec = pl.BlockSpec((bs, D_MODEL), lambda i: (i, 0))
    return pl.pallas_call(body, name="att_prep", grid=(s // bs,), in_specs=[spec, spec, spec], out_specs=[spec, spec],
                          out_shape=[jax.ShapeDtypeStruct((s, D_MODEL), F32), jax.ShapeDtypeStruct((s, D_MODEL), BF16)],
                          compiler_params=_cp("parallel"))(do, o, lse_rep)


def _attention_bwd(qkv, ck4, st, do_b, exchange_parts):
    s = qkv.shape[0]
    bq, bk = _att_blocks(s)
    nq, nk, per = s // bq, s // bk, bk // bq
    _T = (((0,), (0,)), ((), ()))
    n = len(exchange_parts)

    def body(q_ref, k_ref, v_ref, ck_ref, st_ref, do_ref, *rest):
        comm_in, (dq_ref, dk_ref, dv_ref, dck_ref, dcq_ref) = rest[:n], rest[n:n + 5]
        comm_out, sems, (dk_acc, dv_acc) = rest[n + 5:2 * n + 5], rest[2 * n + 5:-2], rest[-2:]
        j = pl.program_id(1)
        if n:
            @pl.when((pl.program_id(0) == 0) & (j == 0))
            def _():
                for cp in _comm_copies(comm_in, comm_out, sems, True):
                    cp.start()

        @pl.when(j == 0)
        def _():
            dq_ref[...] = jnp.zeros_like(dq_ref)
            dcq_ref[...] = jnp.zeros_like(dcq_ref)

        dk_acc[...] = jnp.zeros_like(dk_acc)
        dv_acc[...] = jnp.zeros_like(dv_acc)

        ones_q, ones_k = jnp.ones((_PAD, bq), BF16), jnp.ones((_PAD, bk), BF16)
        k_t = [jnp.concatenate([k_ref[:, _head_cols(a)].T, ones_k], axis=0) for a in range(HP)]

        def step(i, off=None, kl=slice(0, bk)):
            rows = pl.ds(pl.multiple_of(i * bq, bq), bq)
            for a in range(HP):
                cs = _head_cols(a)
                q = (q_ref[rows, cs].astype(F32) * ATT_SCALE).astype(BF16)
                do_a = do_ref[rows, cs]
                sc = lax.dot_general(q, k_ref[kl, cs], _QK, preferred_element_type=F32) - ck_ref[0, a, pl.ds(j, 1), kl]
                if off is not None:
                    sc = jnp.where(_causal(sc.shape, off), sc, NEG)
                p = jnp.exp(sc - st_ref[rows, a * ATT_HEAD_DIM:a * ATT_HEAD_DIM + 1])
                dp = lax.dot_general(do_a, v_ref[kl, cs], _QK, preferred_element_type=F32)
                ds = p * (dp - st_ref[rows, a * ATT_HEAD_DIM + _HALF:a * ATT_HEAD_DIM + _HALF + 1])
                ds_b = ds.astype(BF16)
                dv_acc[a, :, kl] += jnp.dot(do_a.T, p.astype(BF16), preferred_element_type=F32)
                dk_acc[a, :, kl] += jnp.dot(jnp.concatenate([q.T, ones_q], axis=0), ds_b, preferred_element_type=F32)
                dqs = lax.dot_general(k_t[a][:, kl], ds_b, _QK, preferred_element_type=F32)
                dq_ref[cs, rows] += dqs[:ATT_HEAD_DIM] * ATT_SCALE
                dcq_ref[0, a, pl.ds(i, 1), :] += jnp.sum(dqs[ATT_HEAD_DIM:ATT_HEAD_DIM + 8], axis=0,
                                                         keepdims=True) * 0.125

        for t in range(per):
            step(j * per + t, off=t * bq, kl=slice(0, (t + 1) * bq))

        def full(i, c):
            step(i)
            return c

        lax.fori_loop((j + 1) * per, nq, full, 0)
        for a in range(HP):
            dk_ref[:, _head_cols(a)] = dk_acc[a, :ATT_HEAD_DIM].T.astype(dk_ref.dtype)
            dv_ref[:, _head_cols(a)] = dv_acc[a].T.astype(dv_ref.dtype)
            dck_ref[0, a, pl.ds(j, 1), :] = -dk_acc[a, ATT_HEAD_DIM:ATT_HEAD_DIM + 1]
        if n:
            @pl.when((pl.program_id(0) == N_HP - 1) & (j == nk - 1))
            def _():
                for cp in _comm_copies(comm_in, comm_out, sems, True):
                    cp.wait()

    res = pl.BlockSpec((s, LANES), lambda h, j: (0, h))
    ck_spec = pl.BlockSpec((1, HP, nk, bk), lambda h, j: (h, 0, 0, 0))
    kout = pl.BlockSpec((bk, LANES), lambda h, j: (j, h))
    anyspec = pl.BlockSpec(memory_space=pl.ANY)
    outs = pl.pallas_call(
        body, name="att_bwd", grid=(N_HP, nk),
        in_specs=[res, pl.BlockSpec((bk, LANES), lambda h, j: (j, N_HP + h)),
                  pl.BlockSpec((bk, LANES), lambda h, j: (j, 2 * N_HP + h)), ck_spec, res, res] + [anyspec] * n,
        out_specs=[pl.BlockSpec((LANES, s), lambda h, j: (h, 0)), kout, kout, ck_spec,
                   pl.BlockSpec((1, HP, nq, bq), lambda h, j: (h, 0, 0, 0))] + [anyspec] * n,
        out_shape=[jax.ShapeDtypeStruct((D_MODEL, s), F32), jax.ShapeDtypeStruct((s, D_MODEL), BF16),
                   jax.ShapeDtypeStruct((s, D_MODEL), BF16), jax.ShapeDtypeStruct((N_HP, HP, nk, bk), F32),
                   jax.ShapeDtypeStruct((N_HP, HP, nq, bq), F32)] + _comm_out_shapes(exchange_parts, True),
        scratch_shapes=(_comm_sems(n) if n else [])
        + [pltpu.VMEM((HP, ATT_HEAD_DIM + _PAD, bk), F32), pltpu.VMEM((HP, ATT_HEAD_DIM, bk), F32)],
        compiler_params=_cp("arbitrary", "arbitrary"),
    )(qkv, qkv, qkv, ck4, st, do_b, *exchange_parts)
    return outs[:5], list(outs[5:])


def _silu_and_grad(x):
    sg = _sigmoid(x)
    return x * sg, sg * (1.0 + x * (1.0 - sg))


SUBLANES = 8


def _conv_taps(cur, before, w_rows, bias):
    n, c = cur.shape
    cur3 = cur.reshape(n // SUBLANES, SUBLANES, c)
    sub = lax.broadcasted_iota(jnp.int32, (1, SUBLANES, c), 1)
    taps = []
    for k in range(SSM_CONV):
        sh = SSM_CONV - 1 - k
        if sh == 0:
            taps.append(cur3)
            continue
        rot = pltpu.roll(cur3, sh, 1)
        prev = jnp.concatenate([pltpu.roll(before, sh, 0)[None], rot[:-1]], axis=0)
        taps.append(jnp.where(sub < sh, prev, rot))
    pre = bias[None] + sum(w_rows[k][None] * taps[k] for k in range(SSM_CONV))
    return pre.reshape(n, c), [t.reshape(n, c) for t in taps]


def _conv_col_chunks(w_ref, b_ref, bc, cc):
    for c0 in range(0, bc, cc):
        cols = slice(c0, c0 + cc)
        yield cols, [w_ref[k:k + 1, cols] for k in range(SSM_CONV)], None if b_ref is None else b_ref[:, cols]


def _conv_specs(s, bs, bc):
    cur = pl.BlockSpec((bs, bc), lambda j, i: (i, j))
    halo = pl.BlockSpec((8, bc), lambda j, i: (jnp.maximum(i * (bs // 8) - 1, 0), j))
    w = pl.BlockSpec((SSM_CONV, bc), lambda j, i: (0, j))
    b = pl.BlockSpec((1, bc), lambda j, i: (0, j))
    return cur, halo, w, b


def _conv_fwd(xbc, w, b):
    s, c = xbc.shape
    bs, bc = _tile(s, 512), 1024
    rc, cc = bs, 128

    def body(x_ref, h_ref, w_ref, b_ref, o_ref):
        first = pl.program_id(1) == 0
        for cols, w_rows, bias in _conv_col_chunks(w_ref, b_ref, bc, cc):
            def step(r, before):
                rows = pl.ds(pl.multiple_of(r * rc, rc), rc)
                cur = x_ref[rows, cols]
                pre, _ = _conv_taps(cur, before, w_rows, bias)
                o_ref[rows, cols] = pre * _sigmoid(pre)
                return cur[rc - SUBLANES:]

            lax.fori_loop(0, bs // rc, step, jnp.where(first, 0.0, h_ref[:, cols]))

    cur, halo, ws, bsp = _conv_specs(s, bs, bc)
    return pl.pallas_call(body, name="conv_fwd", grid=(c // bc, s // bs), in_specs=[cur, halo, ws, bsp],
                          out_specs=cur, out_shape=jax.ShapeDtypeStruct((s, c), F32),
                          compiler_params=_cp("parallel", "parallel"))(xbc, xbc, w, b)


def _conv_bwd_pre(xbc, w, b, dact):
    s, c = xbc.shape
    bs, bc = _tile(s, 512), 1024
    rc, cc = _tile(bs, 256), 128

    def body(x_ref, h_ref, w_ref, b_ref, g_ref, dp_ref, dw_ref, db_ref):
        first = pl.program_id(1) == 0

        @pl.when(first)
        def _():
            dw_ref[...] = jnp.zeros_like(dw_ref)
            db_ref[...] = jnp.zeros_like(db_ref)

        for cols, w_rows, bias in _conv_col_chunks(w_ref, b_ref, bc, cc):
            def step(r, carry):
                before, sums = carry
                rows = pl.ds(pl.multiple_of(r * rc, rc), rc)
                cur = x_ref[rows, cols]
                pre, taps = _conv_taps(cur, before, w_rows, bias)
                dpre = g_ref[rows, cols].astype(F32) * _silu_and_grad(pre)[1]
                dp_ref[rows, cols] = dpre.astype(dp_ref.dtype)
                terms = [dpre * t for t in taps] + [dpre]
                sums = tuple(a + jnp.sum(t.reshape(rc // SUBLANES, SUBLANES, cc), axis=0) for a, t in zip(sums, terms))
                return cur[rc - SUBLANES:], sums

            zero = jnp.zeros((SUBLANES, cc), F32)
            _, sums = lax.fori_loop(0, bs // rc, step,
                                    (jnp.where(first, 0.0, h_ref[:, cols]), (zero,) * (SSM_CONV + 1)))
            for k in range(SSM_CONV):
                dw_ref[k:k + 1, cols] += jnp.sum(sums[k], axis=0, keepdims=True)
            db_ref[:, cols] += jnp.sum(sums[SSM_CONV], axis=0, keepdims=True)

    cur, halo, ws, bsp = _conv_specs(s, bs, bc)
    return pl.pallas_call(
        body, name="conv_bwd_pre", grid=(c // bc, s // bs), in_specs=[cur, halo, ws, bsp, cur],
        out_specs=[cur, ws, bsp],
        out_shape=[jax.ShapeDtypeStruct((s, c), BF16), jax.ShapeDtypeStruct((SSM_CONV, c), F32),
                   jax.ShapeDtypeStruct((1, c), F32)],
        compiler_params=_cp("parallel", "arbitrary"))(xbc, xbc, w, b, dact)


def _conv_bwd_in(dpre, w):
    s, c = dpre.shape
    bs, bc = _tile(s, 512), 1024
    nb = s // bs
    rc, cc = bs, 256
    nr = bs // rc

    def body(g_ref, n_ref, w_ref, o_ref):
        last = pl.program_id(1) == nb - 1
        sub = lax.broadcasted_iota(jnp.int32, (1, SUBLANES, cc), 1)
        for cols, w_rows, _ in _conv_col_chunks(w_ref, None, bc, cc):
            def step(i, after):
                rows = pl.ds(pl.multiple_of((nr - 1 - i) * rc, rc), rc)
                cur = g_ref[rows, cols].astype(F32)
                cur3 = cur.reshape(rc // SUBLANES, SUBLANES, cc)
                acc = w_rows[SSM_CONV - 1][None] * cur3
                for sh in range(1, SSM_CONV):
                    rot = pltpu.roll(cur3, SUBLANES - sh, 1)
                    nxt = jnp.concatenate([rot[1:], pltpu.roll(after, SUBLANES - sh, 0)[None]], axis=0)
                    acc = acc + w_rows[SSM_CONV - 1 - sh][None] * jnp.where(sub >= SUBLANES - sh, nxt, rot)
                o_ref[rows, cols] = acc.reshape(rc, cc).astype(o_ref.dtype)
                return cur[0:SUBLANES]

            lax.fori_loop(0, nr, step, jnp.where(last, 0.0, n_ref[0:SUBLANES, cols].astype(F32)))

    cur = pl.BlockSpec((bs, bc), lambda j, i: (i, j))
    nxt = pl.BlockSpec((16, bc), lambda j, i: (jnp.minimum((i + 1) * (bs // 16), s // 16 - 1), j))
    ws = pl.BlockSpec((SSM_CONV, bc), lambda j, i: (0, j))
    return pl.pallas_call(body, name="conv_bwd_in", grid=(c // bc, nb), in_specs=[cur, nxt, ws], out_specs=cur,
                          out_shape=jax.ShapeDtypeStruct((s, c), BF16),
                          compiler_params=_cp("parallel", "parallel"))(dpre, dpre, w)


def _dotT(a, b):
    return lax.dot_general(a.astype(BF16), b.astype(BF16), (((1,), (1,)), ((), ())), preferred_element_type=F32)


def _dot(a, b):
    return jnp.dot(a.astype(BF16), b.astype(BF16), preferred_element_type=F32)


N_PAIR = SSM_HEADS // HP
PAIRS_PER_GROUP = SSM_HEADS_PER_GROUP // HP


def _pair_consts():
    L = SSM_CHUNK
    lane = lax.broadcasted_iota(jnp.int32, (L, LANES), 1)
    lane1 = lax.broadcasted_iota(jnp.int32, (1, LANES), 1)
    li = lax.broadcasted_iota(jnp.int32, (L, L), 0)
    si = lax.broadcasted_iota(jnp.int32, (L, L), 1)
    return lane >= ATT_HEAD_DIM, lane1 >= ATT_HEAD_DIM, li, si


def _ssd_pair_fwd(xbc_act, ac_c, dt_r, ac_r, dsk_pair):
    s = xbc_act.shape[0]
    L, N, G = SSM_CHUNK, SSM_STATE, SSM_GROUPS
    nc = s // L

    def body(xbc_ref, ac_ref, dtr_ref, acr_ref, dsk_ref, y_ref, hp_ref, st_ref):
        @pl.when(pl.program_id(0) == 0)
        def _():
            st_ref[...] = jnp.zeros_like(st_ref)

        upper, up1, li, si = _pair_consts()
        for g in range(G):
            b_g = xbc_ref[:, SSM_INNER + g * N:SSM_INNER + (g + 1) * N]
            c_g = xbc_ref[:, SSM_INNER + G * N + g * N:SSM_INNER + G * N + (g + 1) * N]
            cb = _dotT(c_g, b_g)
            b_t = b_g.T
            for q in range(PAIRS_PER_GROUP):
                pp = g * PAIRS_PER_GROUP + q
                cols = slice(pp * LANES, (pp + 1) * LANES)
                xs = xbc_ref[:, cols]
                ht = st_ref[pp]
                hp_ref[0, pp] = ht
                y = dsk_ref[pp:pp + 1, :] * xs
                s_new = jnp.zeros((N, LANES), F32)
                ea, el = [], []
                for a in range(HP):
                    h = HP * pp + a
                    acol = jnp.broadcast_to(ac_ref[:, h:h + 1], (L, LANES))
                    arow, dtrow = acr_ref[h:h + 1, :], dtr_ref[h:h + 1, :]
                    alast = ac_ref[L - 1:L, h:h + 1]
                    decay = jnp.exp(jnp.where(li >= si, acol - arow, NEG))
                    xs_a = jnp.where(upper == (a == 1), xs, 0.0)
                    y = y + _dot(cb * decay * dtrow, xs_a)
                    s_new = s_new + _dot(b_t * (dtrow * jnp.exp(alast - arow)), xs_a)
                    ea.append(jnp.exp(acol))
                    el.append(jnp.exp(alast))
                y_ref[:, cols] = y + jnp.where(upper, ea[1], ea[0]) * _dot(c_g, ht)
                st_ref[pp] = ht * jnp.where(up1, el[1], el[0]) + s_new

    col = pl.BlockSpec((L, LANES), lambda c: (c, 0))
    row = pl.BlockSpec((LANES, L), lambda c: (0, c))
    return pl.pallas_call(
        body, name="ssd_fwd", grid=(nc,),
        in_specs=[pl.BlockSpec((L, SSM_CONV_DIM), lambda c: (c, 0)), col, row, row,
                  pl.BlockSpec((N_PAIR, LANES), lambda c: (0, 0))],
        out_specs=[pl.BlockSpec((L, SSM_INNER), lambda c: (c, 0)),
                   pl.BlockSpec((1, N_PAIR, N, LANES), lambda c: (c, 0, 0, 0))],
        out_shape=[jax.ShapeDtypeStruct((s, SSM_INNER), F32), jax.ShapeDtypeStruct((nc, N_PAIR, N, LANES), F32)],
        scratch_shapes=[pltpu.VMEM((N_PAIR, N, LANES), F32)],
        compiler_params=_cp("arbitrary"),
    )(xbc_act, ac_c, dt_r, ac_r, dsk_pair)


def _ssd_pair_bwd(xbc_act, dt_c, ac_c, dt_r, ac_r, hprev_all, dy, dsk_pair, a_row):
    s = xbc_act.shape[0]
    L, N, G = SSM_CHUNK, SSM_STATE, SSM_GROUPS
    nc = s // L
    rev = lambda c: nc - 1 - c

    def body(xbc_ref, dt_ref, ac_ref, dtr_ref, acr_ref, hp_ref, dy_ref, dsk_ref, arow_ref,
             dx_ref, ddt_ref, da_ref, dds_ref, dh_ref):
        @pl.when(pl.program_id(0) == 0)
        def _():
            dh_ref[...] = jnp.zeros_like(dh_ref)
            da_ref[...] = jnp.zeros_like(da_ref)
            dds_ref[...] = jnp.zeros_like(dds_ref)

        upper, up1, li, si = _pair_consts()
        lane = lax.broadcasted_iota(jnp.int32, (L, LANES), 1)
        sub = lax.broadcasted_iota(jnp.int32, (LANES, L), 0)
        lastrow = lax.broadcasted_iota(jnp.int32, (L, LANES), 0) == L - 1
        da_c = jnp.zeros((L, LANES), F32)
        da_r = jnp.zeros((LANES, L), F32)
        ddt_r = jnp.zeros((LANES, L), F32)
        for g in range(G):
            b_g = xbc_ref[:, SSM_INNER + g * N:SSM_INNER + (g + 1) * N]
            c_g = xbc_ref[:, SSM_INNER + G * N + g * N:SSM_INNER + G * N + (g + 1) * N]
            cb, cb_t = _dotT(c_g, b_g), _dotT(b_g, c_g)
            b_t, c_t = b_g.T, c_g.T
            dcb = jnp.zeros((L, L), F32)
            db_t = jnp.zeros((N, L), F32)
            dc = jnp.zeros((L, N), F32)
            for q in range(PAIRS_PER_GROUP):
                pp = g * PAIRS_PER_GROUP + q
                cols = slice(pp * LANES, (pp + 1) * LANES)
                xs, gy = xbc_ref[:, cols], dy_ref[:, cols].astype(F32)
                ht, dhn = hp_ref[0, pp], dh_ref[pp]
                acol = [jnp.broadcast_to(ac_ref[:, HP * pp + a:HP * pp + a + 1], (L, LANES)) for a in range(HP)]
                alast = [ac_ref[L - 1:L, HP * pp + a:HP * pp + a + 1] for a in range(HP)]
                ea = jnp.where(upper, jnp.exp(acol[1]), jnp.exp(acol[0]))
                el = jnp.where(up1, jnp.exp(alast[1]), jnp.exp(alast[0]))
                ge = gy * ea
                dc = dc + _dotT(ge, ht)
                dh_ref[pp] = _dot(c_t, ge) + dhn * el
                t_off = (ge * _dot(c_g, ht)).astype(BF16)
                hsum = jnp.sum(dhn * ht, axis=0, keepdims=True)
                dxs = dsk_ref[pp:pp + 1, :] * gy
                dds_ref[pp:pp + 1, :] += jnp.sum(gy * xs, axis=0, keepdims=True)
                for a in range(HP):
                    h = HP * pp + a
                    mine, mine1 = upper == (a == 1), up1 == (a == 1)
                    arow, dtrow = acr_ref[h:h + 1, :], dtr_ref[h:h + 1, :]
                    dtcol = jnp.broadcast_to(dt_ref[:, h:h + 1], (L, LANES))
                    xs_a, gy_a = jnp.where(mine, xs, 0.0), jnp.where(mine, gy, 0.0)
                    dhn_a = jnp.where(mine1, dhn, 0.0)
                    e_row = jnp.exp(alast[a] - arow)
                    w_row = dtrow * e_row
                    xd_t = _dotT(dhn_a, xs_a)
                    db_t = db_t + xd_t * w_row
                    dw = jnp.sum(b_t * xd_t, axis=0, keepdims=True)
                    de_e = dw * w_row
                    dal = (jnp.sum(jnp.where(mine1, hsum, 0.0), axis=1, keepdims=True) * jnp.exp(alast[a])
                           + jnp.sum(de_e, axis=1, keepdims=True))
                    dxs = dxs + _dot(b_g, dhn_a) * (dtcol * jnp.exp(alast[a] - acol[a]))
                    decay = jnp.exp(jnp.where(li >= si, acol[a] - arow, NEG))
                    decay_t = jnp.exp(jnp.where(si >= li, arow - acol[a], NEG))
                    m = cb * decay
                    dmdt = _dotT(gy_a, xs_a)
                    dxs = dxs + _dot(cb_t * decay_t * dtcol, gy_a)
                    dm = dmdt * dtrow
                    dcb = dcb + dm * decay
                    wb = (dm * m).astype(BF16)
                    onehot = jnp.where(lane == h, 1.0, 0.0).astype(BF16)
                    da_c = (da_c + jnp.dot(wb, onehot, preferred_element_type=F32)
                            + jnp.dot(jnp.where(mine, t_off, 0.0).astype(BF16), onehot, preferred_element_type=F32)
                            + jnp.where(lastrow & (lane == h), dal, 0.0))
                    da_r = jnp.where(sub == h, -(jnp.sum(wb.astype(F32), axis=0, keepdims=True) + de_e), da_r)
                    ddt_r = jnp.where(sub == h, dw * e_row + jnp.sum(dmdt * m, axis=0, keepdims=True), ddt_r)
                dx_ref[:, cols] = dxs.astype(dx_ref.dtype)
            dx_ref[:, SSM_INNER + g * N:SSM_INNER + (g + 1) * N] = (db_t + _dot(c_t, dcb)).T.astype(dx_ref.dtype)
            dx_ref[:, SSM_INNER + G * N + g * N:SSM_INNER + G * N + (g + 1) * N] = (
                dc + _dot(dcb, b_g)).astype(dx_ref.dtype)
        dda = jnp.dot(_tri(L, lower=False), da_c + da_r.T, precision=HI, preferred_element_type=F32)
        ddt_ref[...] = dda * arow_ref[...] + ddt_r.T
        da_ref[...] += jnp.sum(dda * dt_ref[...], axis=0, keepdims=True)

    col = pl.BlockSpec((L, LANES), lambda c: (rev(c), 0))
    row = pl.BlockSpec((LANES, L), lambda c: (0, rev(c)))
    vec = pl.BlockSpec((1, LANES), lambda c: (0, 0))
    pairs = pl.BlockSpec((N_PAIR, LANES), lambda c: (0, 0))
    return pl.pallas_call(
        body, name="ssd_bwd", grid=(nc,),
        in_specs=[pl.BlockSpec((L, SSM_CONV_DIM), lambda c: (rev(c), 0)), col, col, row, row,
                  pl.BlockSpec((1, N_PAIR, N, LANES), lambda c: (rev(c), 0, 0, 0)),
                  pl.BlockSpec((L, SSM_INNER), lambda c: (rev(c), 0)), pairs, vec],
        out_specs=[pl.BlockSpec((L, SSM_CONV_DIM), lambda c: (rev(c), 0)), col, vec, pairs],
        out_shape=[jax.ShapeDtypeStruct((s, SSM_CONV_DIM), BF16), jax.ShapeDtypeStruct((s, LANES), F32),
                   jax.ShapeDtypeStruct((1, LANES), F32), jax.ShapeDtypeStruct((N_PAIR, LANES), F32)],
        scratch_shapes=[pltpu.VMEM((N_PAIR, N, LANES), F32)],
        compiler_params=_cp("arbitrary"),
    )(xbc_act, dt_c, ac_c, dt_r, ac_r, hprev_all, dy, dsk_pair, a_row)


ROWS = 512
GW = SSM_INNER // SSM_GROUPS


def _rows(width, dtype=F32):
    return pl.BlockSpec((ROWS, width), lambda i: (i, 0))


def _vec(width):
    return pl.BlockSpec((1, width), lambda i: (0, 0))


def _gnorm_fwd(y, z, w):
    s = y.shape[0]

    def body(y_ref, z_ref, w_ref, o_ref):
        for g in range(SSM_GROUPS):
            cs = slice(g * GW, (g + 1) * GW)
            zz = z_ref[:, cs].astype(F32)
            u = y_ref[:, cs] * (zz * _sigmoid(zz))
            r = lax.rsqrt(jnp.mean(u * u, axis=1, keepdims=True) + RMS_EPS)
            o_ref[:, cs] = (u * r * w_ref[:, cs]).astype(o_ref.dtype)

    return pl.pallas_call(body, name="gnorm_fwd", grid=(s // ROWS,),
                          in_specs=[_rows(SSM_INNER), _rows(SSM_INNER), _vec(SSM_INNER)], out_specs=_rows(SSM_INNER),
                          out_shape=jax.ShapeDtypeStruct((s, SSM_INNER), BF16), compiler_params=_cp("parallel"))(y, z, w)


def _gnorm_bwd(y, z, w, do):
    s = y.shape[0]

    def body(y_ref, z_ref, w_ref, do_ref, dy_ref, dz_ref, dw_ref):
        @pl.when(pl.program_id(0) == 0)
        def _():
            dw_ref[...] = jnp.zeros_like(dw_ref)

        for g in range(SSM_GROUPS):
            cs = slice(g * GW, (g + 1) * GW)
            zz, yy, dd = z_ref[:, cs].astype(F32), y_ref[:, cs], do_ref[:, cs].astype(F32)
            sz, dsz = _silu_and_grad(zz)
            u = yy * sz
            r = lax.rsqrt(jnp.mean(u * u, axis=1, keepdims=True) + RMS_EPS)
            n = u * r
            dn = dd * w_ref[:, cs]
            dw_ref[:, cs] += jnp.sum(dd * n, axis=0, keepdims=True)
            du = r * (dn - n * jnp.mean(dn * n, axis=1, keepdims=True))
            dy_ref[:, cs] = (du * sz).astype(dy_ref.dtype)
            dz_ref[:, cs] = (du * yy * dsz).astype(dz_ref.dtype)

    return pl.pallas_call(
        body, name="gnorm_bwd", grid=(s // ROWS,),
        in_specs=[_rows(SSM_INNER), _rows(SSM_INNER), _vec(SSM_INNER), _rows(SSM_INNER)],
        out_specs=[_rows(SSM_INNER), _rows(SSM_INNER), _vec(SSM_INNER)],
        out_shape=[jax.ShapeDtypeStruct((s, SSM_INNER), BF16), jax.ShapeDtypeStruct((s, SSM_INNER), BF16),
                   jax.ShapeDtypeStruct((1, SSM_INNER), F32)],
        compiler_params=_cp("arbitrary"))(y, z, w, do)


def _mix_fwd(gl, bg, attn_d, ssm_d):
    s = gl.shape[0]
    d = D_MODEL

    def body(gl_ref, bg_ref, a_ref, m_ref, o_ref):
        g0 = _sigmoid(gl_ref[:, :d] + bg_ref[:, :d])
        g1 = _sigmoid(gl_ref[:, d:] + bg_ref[:, d:])
        o_ref[...] = (g0 * a_ref[...] + g1 * m_ref[...]).astype(o_ref.dtype)

    return pl.pallas_call(body, name="mix_fwd", grid=(s // ROWS,),
                          in_specs=[_rows(2 * d), _vec(2 * d), _rows(d), _rows(d)], out_specs=_rows(d),
                          out_shape=jax.ShapeDtypeStruct((s, d), BF16), compiler_params=_cp("parallel"))(
        gl, bg, attn_d, ssm_d)


def _mix_bwd(gl, bg, attn_d, ssm_d, dmix):
    s = gl.shape[0]
    d = D_MODEL

    def body(gl_ref, bg_ref, a_ref, m_ref, dm_ref, da_ref, ds_ref, dg_ref, db_ref):
        @pl.when(pl.program_id(0) == 0)
        def _():
            db_ref[...] = jnp.zeros_like(db_ref)

        g0 = _sigmoid(gl_ref[:, :d] + bg_ref[:, :d])
        g1 = _sigmoid(gl_ref[:, d:] + bg_ref[:, d:])
        dm = dm_ref[...].astype(F32)
        da_ref[...] = (dm * g0).astype(da_ref.dtype)
        ds_ref[...] = (dm * g1).astype(ds_ref.dtype)
        dl0 = dm * a_ref[...] * g0 * (1.0 - g0)
        dl1 = dm * m_ref[...] * g1 * (1.0 - g1)
        dg_ref[:, :d] = dl0.astype(dg_ref.dtype)
        dg_ref[:, d:] = dl1.astype(dg_ref.dtype)
        db_ref[:, :d] += jnp.sum(dl0, axis=0, keepdims=True)
        db_ref[:, d:] += jnp.sum(dl1, axis=0, keepdims=True)

    return pl.pallas_call(
        body, name="mix_bwd", grid=(s // ROWS,),
        in_specs=[_rows(2 * d), _vec(2 * d), _rows(d), _rows(d), _rows(d)],
        out_specs=[_rows(d), _rows(d), _rows(2 * d), _vec(2 * d)],
        out_shape=[jax.ShapeDtypeStruct((s, d), BF16), jax.ShapeDtypeStruct((s, d), BF16),
                   jax.ShapeDtypeStruct((s, 2 * d), BF16), jax.ShapeDtypeStruct((1, 2 * d), F32)],
        compiler_params=_cp("arbitrary"))(gl, bg, attn_d, ssm_d, dmix)


def _ln_stats(p):
    mu = jnp.mean(p, axis=1, keepdims=True)
    c = p - mu
    rstd = lax.rsqrt(jnp.mean(c * c, axis=1, keepdims=True) + LN_EPS)
    return c * rstd, rstd


def _ln_bwd(dy, xhat, rstd, g):
    dxh = dy * g
    return rstd * (dxh - jnp.mean(dxh, axis=1, keepdims=True) - xhat * jnp.mean(dxh * xhat, axis=1, keepdims=True))


def _ln1_fwd(x, mixed, g, b):
    s, d = x.shape

    def body(x_ref, m_ref, g_ref, b_ref, o_ref, ob_ref):
        xhat, _ = _ln_stats(DEEPNORM_ALPHA * x_ref[...] + m_ref[...])
        y = xhat * g_ref[...] + b_ref[...]
        o_ref[...] = y
        ob_ref[...] = y.astype(BF16)

    return pl.pallas_call(body, name="ln1_fwd", grid=(s // ROWS,), in_specs=[_rows(d), _rows(d), _vec(d), _vec(d)],
                          out_specs=[_rows(d), _rows(d)],
                          out_shape=[jax.ShapeDtypeStruct((s, d), F32), jax.ShapeDtypeStruct((s, d), BF16)],
                          compiler_params=_cp("parallel"))(x, mixed, g, b)


def _ln2_loss(x1, h, target, g, b):
    s, d = x1.shape

    def body(x_ref, h_ref, t_ref, g_ref, b_ref, dp_ref, dpb_ref, loss_ref, dg_ref, db_ref):
        @pl.when(pl.program_id(0) == 0)
        def _():
            loss_ref[...] = jnp.zeros_like(loss_ref)
            dg_ref[...] = jnp.zeros_like(dg_ref)
            db_ref[...] = jnp.zeros_like(db_ref)

        xhat, rstd = _ln_stats(DEEPNORM_ALPHA * x_ref[...] + h_ref[...])
        err = xhat * g_ref[...] + b_ref[...] - t_ref[...]
        part = 0.5 * jnp.sum(jnp.mean(err * err, axis=1, keepdims=True), axis=0, keepdims=True)
        loss_ref[...] += jnp.broadcast_to(part, loss_ref.shape)
        dy = err * (1.0 / d)
        dg_ref[...] += jnp.sum(dy * xhat, axis=0, keepdims=True)
        db_ref[...] += jnp.sum(dy, axis=0, keepdims=True)
        dp = _ln_bwd(dy, xhat, rstd, g_ref[...])
        dp_ref[...] = dp
        dpb_ref[...] = dp.astype(BF16)

    return pl.pallas_call(
        body, name="ln2_loss", grid=(s // ROWS,), in_specs=[_rows(d), _rows(d), _rows(d), _vec(d), _vec(d)],
        out_specs=[_rows(d), _rows(d), _vec(LANES), _vec(d), _vec(d)],
        out_shape=[jax.ShapeDtypeStruct((s, d), F32), jax.ShapeDtypeStruct((s, d), BF16),
                   jax.ShapeDtypeStruct((1, LANES), F32),
                   jax.ShapeDtypeStruct((1, d), F32), jax.ShapeDtypeStruct((1, d), F32)],
        compiler_params=_cp("arbitrary"))(x1, h, target, g, b)


def _ln1_bwd(x, mixed, g, dpre2, dffn):
    s, d = x.shape

    def body(x_ref, m_ref, g_ref, d2_ref, df_ref, dp_ref, dr_ref, dg_ref, db_ref):
        @pl.when(pl.program_id(0) == 0)
        def _():
            dg_ref[...] = jnp.zeros_like(dg_ref)
            db_ref[...] = jnp.zeros_like(db_ref)

        xhat, rstd = _ln_stats(DEEPNORM_ALPHA * x_ref[...] + m_ref[...])
        dy = DEEPNORM_ALPHA * d2_ref[...] + df_ref[...]
        dg_ref[...] += jnp.sum(dy * xhat, axis=0, keepdims=True)
        db_ref[...] += jnp.sum(dy, axis=0, keepdims=True)
        dp = _ln_bwd(dy, xhat, rstd, g_ref[...])
        dp_ref[...] = dp.astype(BF16)
        dr_ref[...] = DEEPNORM_ALPHA * dp

    return pl.pallas_call(
        body, name="ln1_bwd", grid=(s // ROWS,), in_specs=[_rows(d), _rows(d), _vec(d), _rows(d), _rows(d)],
        out_specs=[_rows(d), _rows(d), _vec(d), _vec(d)],
        out_shape=[jax.ShapeDtypeStruct((s, d), BF16), jax.ShapeDtypeStruct((s, d), F32),
                   jax.ShapeDtypeStruct((1, d), F32), jax.ShapeDtypeStruct((1, d), F32)],
        compiler_params=_cp("arbitrary"))(x, mixed, g, dpre2, dffn)


def _swiglu_fwd(gu):
    s = gu.shape[0]
    f = FFN_HIDDEN

    def body(g_ref, u_ref, o_ref):
        gg = g_ref[...].astype(F32)
        o_ref[...] = (gg * _sigmoid(gg) * u_ref[...].astype(F32)).astype(o_ref.dtype)

    return pl.pallas_call(
        body, name="swiglu_fwd", grid=(s // ROWS,),
        in_specs=[pl.BlockSpec((ROWS, f), lambda i: (i, 0)), pl.BlockSpec((ROWS, f), lambda i: (i, 1))],
        out_specs=_rows(f), out_shape=jax.ShapeDtypeStruct((s, f), BF16), compiler_params=_cp("parallel"))(gu, gu)


def _swiglu_bwd(gu, dact):
    s = gu.shape[0]
    f = FFN_HIDDEN

    def body(g_ref, u_ref, d_ref, o_ref):
        sg, dsg = _silu_and_grad(g_ref[...].astype(F32))
        dd = d_ref[...].astype(F32)
        o_ref[:, :f] = (dd * u_ref[...].astype(F32) * dsg).astype(o_ref.dtype)
        o_ref[:, f:] = (dd * sg).astype(o_ref.dtype)

    return pl.pallas_call(
        body, name="swiglu_bwd", grid=(s // ROWS,),
        in_specs=[pl.BlockSpec((ROWS, f), lambda i: (i, 0)), pl.BlockSpec((ROWS, f), lambda i: (i, 1)), _rows(f)],
        out_specs=_rows(2 * f), out_shape=jax.ShapeDtypeStruct((s, 2 * f), BF16),
        compiler_params=_cp("parallel"))(gu, gu, dact)


def _peer(k):
    x, y, c = lax.axis_index("x"), lax.axis_index("y"), lax.axis_index("c")
    kx, ky, kc = (k >> 2) & 1, (k >> 1) & 1, k & 1
    px = (1 - x) if kx else x
    py = (1 - y) if ky else y
    pc = (1 - c) if kc else c
    return (px, py, pc), 4 * px + 2 * py + pc


def _my_index():
    return 4 * lax.axis_index("x") + 2 * lax.axis_index("y") + lax.axis_index("c")


def _comm_copies(ins, outs, sems, scatter):
    send_sems, recv_sems, local_sems = sems
    me = _my_index()
    copies = [pltpu.make_async_copy(ins[t].at[me] if scatter else ins[t], outs[t].at[me], local_sems.at[t])
              for t in range(len(ins))]
    for k in range(1, N_DEV):
        peer, pidx = _peer(k)
        for t in range(len(ins)):
            copies.append(pltpu.make_async_remote_copy(
                src_ref=ins[t].at[pidx] if scatter else ins[t], dst_ref=outs[t].at[me],
                send_sem=send_sems.at[t, k - 1], recv_sem=recv_sems.at[t, k - 1], device_id=peer,
                device_id_type=pl.DeviceIdType.MESH))
    return copies


def _comm_sems(n):
    return [pltpu.SemaphoreType.DMA((n, N_DEV - 1)), pltpu.SemaphoreType.DMA((n, N_DEV - 1)),
            pltpu.SemaphoreType.DMA((n,))]


def _comm_out_shapes(parts, scatter):
    return [jax.ShapeDtypeStruct(p.shape if scatter else (N_DEV,) + p.shape, p.dtype) for p in parts]


def _all_gather(parts):
    n = len(parts)

    def body(*refs):
        ins, outs = refs[:n], refs[n:2 * n]
        send_sems, recv_sems, local_sems = refs[2 * n:]
        x, y, c = lax.axis_index("x"), lax.axis_index("y"), lax.axis_index("c")
        me, sibling = (x, y, c), (x, y, 1 - c)
        chips = [(1 - x, y), (x, 1 - y), (1 - x, 1 - y)]

        def copy(t, k, block, to, src=None):
            dst = outs[t].at[4 * block[0] + 2 * block[1] + block[2]]
            return pltpu.make_async_remote_copy(
                src_ref=dst if src is None else src, dst_ref=dst, send_sem=send_sems.at[t, k],
                recv_sem=recv_sems.at[t, k], device_id=to, device_id_type=pl.DeviceIdType.MESH)

        mine = [pltpu.make_async_copy(ins[t], outs[t].at[_my_index()], local_sems.at[t]) for t in range(n)]
        for cp in mine:
            cp.start()
        first = [copy(t, 0, me, sibling, src=ins[t]) for t in range(n)]
        first += [copy(t, 1 + j, me, (*chip, c), src=ins[t]) for j, chip in enumerate(chips) for t in range(n)]
        for cp in first:
            cp.start()
        passed = []
        for j, chip in enumerate(chips):
            for t in range(n):
                copy(t, 1 + j, (*chip, c), me).wait_recv()
                passed.append(copy(t, 4 + j, (*chip, c), sibling))
                passed[-1].start()
        for t in range(n):
            copy(t, 0, sibling, me).wait_recv()
            for j, chip in enumerate(chips):
                copy(t, 4 + j, (*chip, 1 - c), me).wait_recv()
        for cp in first + passed:
            cp.wait_send()
        for cp in mine:
            cp.wait()

    anyspec = pl.BlockSpec(memory_space=pl.ANY)
    return pl.pallas_call(body, name="all_gather", in_specs=[anyspec] * n, out_specs=[anyspec] * n,
                          out_shape=_comm_out_shapes(parts, False), scratch_shapes=_comm_sems(n))(*parts)


def _remote_scatter_copies(ins, lands, send_sems, recv_sems):
    me = _my_index()
    copies = []
    for k in range(1, N_DEV):
        peer, pidx = _peer(k)
        for t in range(len(ins)):
            copies.append(pltpu.make_async_remote_copy(
                src_ref=ins[t].at[pidx], dst_ref=lands[t].at[me], send_sem=send_sems.at[t * (N_DEV - 1) + k - 1],
                recv_sem=recv_sems.at[t * (N_DEV - 1) + k - 1], device_id=peer, device_id_type=pl.DeviceIdType.MESH))
    return copies


def _landing_zones(parts):
    me = _my_index()
    return [jnp.where(lax.broadcasted_iota(jnp.int32, p.shape, 0) == me, p, jnp.zeros_like(p)) for p in parts]


_HBM = pl.BlockSpec(memory_space=pltpu.HBM)
_SEM = pl.BlockSpec(memory_space=pltpu.SEMAPHORE)


def _exchange_start(parts, lands):
    n = len(parts)

    def body(*refs):
        ins, lnd, send_sems, recv_sems, token = refs[:n], refs[n:2 * n], refs[2 * n], refs[2 * n + 1], refs[-1]
        for cp in _remote_scatter_copies(ins, lnd, send_sems, recv_sems):
            cp.start()
        token[...] = jnp.zeros_like(token)

    hbm = [pltpu.HBM(p.shape, p.dtype) for p in parts]
    outs = pl.pallas_call(
        body, name="exchange_start",
        out_shape=[pltpu.SemaphoreType.DMA((n * (N_DEV - 1),)), pltpu.SemaphoreType.DMA((n * (N_DEV - 1),))] + hbm + hbm
        + [jax.ShapeDtypeStruct((8, LANES), F32)],
        in_specs=[_HBM] * (2 * n), out_specs=[_SEM, _SEM] + [_HBM] * (2 * n) + [pl.BlockSpec(memory_space=pltpu.VMEM)],
        input_output_aliases={t: 2 + t for t in range(2 * n)},
        compiler_params=pltpu.CompilerParams(has_side_effects=pltpu.SideEffectType.DATAFLOW_SIDE_EFFECTING),
    )(*[pltpu.with_memory_space_constraint(p, pltpu.HBM) for p in list(parts) + list(lands)])
    return outs[0], outs[1], list(outs[2:2 + n]), list(outs[2 + n:2 + 2 * n]), outs[-1]


def _exchange_wait(send_sems, recv_sems, parts, lands, after):
    n = len(parts)

    def body(*refs):
        ins, lnd, send_sems, recv_sems = refs[:n], refs[n:2 * n], refs[2 * n], refs[2 * n + 1]
        for cp in _remote_scatter_copies(ins, lnd, send_sems, recv_sems):
            cp.wait_send()
            cp.wait_recv()

    hbm = [pltpu.HBM(p.shape, p.dtype) for p in parts]
    outs = pl.pallas_call(
        body, name="exchange_wait", out_shape=hbm + hbm,
        in_specs=[_HBM] * (2 * n) + [_SEM, _SEM, pl.BlockSpec(memory_space=pl.ANY)], out_specs=[_HBM] * (2 * n),
        input_output_aliases={t: t for t in range(2 * n)},
        compiler_params=pltpu.CompilerParams(has_side_effects=pltpu.SideEffectType.DATAFLOW_SIDE_EFFECTING),
    )(*parts, *lands, send_sems, recv_sems, after)
    return list(outs[n:])


def _adamw(recv, w, m, v, name):
    _, r, c = w.shape
    br = _tile(r, 128)
    c1 = 1.0 / (1.0 - ADAM_B1 ** ADAM_STEP)
    c2 = 1.0 / (1.0 - ADAM_B2 ** ADAM_STEP)

    def body(r_ref, w_ref, m_ref, v_ref, g_ref, d_ref, mo_ref, vo_ref):
        g = r_ref[0].astype(F32)
        for k in range(1, N_DEV):
            g = g + r_ref[k].astype(F32)
        mn = ADAM_B1 * m_ref[0] + (1.0 - ADAM_B1) * g
        vn = ADAM_B2 * v_ref[0] + (1.0 - ADAM_B2) * (g * g)
        g_ref[0] = g
        mo_ref[0] = mn
        vo_ref[0] = vn
        d_ref[0] = -ADAM_LR * ((mn * c1) / (jnp.sqrt(vn * c2) + ADAM_EPS) + ADAM_WD * w_ref[0])

    blk = pl.BlockSpec((1, br, c), lambda i: (0, i, 0))
    return pl.pallas_call(
        body, name=name, grid=(r // br,),
        in_specs=[pl.BlockSpec((N_DEV, br, c), lambda i: (0, i, 0)), blk, blk, blk],
        out_specs=[blk] * 4, out_shape=[jax.ShapeDtypeStruct((1, r, c), F32)] * 4,
        compiler_params=_cp("parallel"))(recv, w, m, v)


def _lane_row(pairs):
    row = jnp.zeros((LANES,), F32)
    for lane0, vec in pairs:
        row = lax.dynamic_update_slice(row, vec.astype(F32), (lane0,))
    return row.reshape(1, LANES)


def _stage_in(x, wts, small):
    s = x.shape[0]
    a = -jnp.exp(small["a_log"])
    bias_row = _lane_row([(DT_LANE0, small["dt_bias"]), (F_LANE0, small["b_forget"])])
    a_row = _lane_row([(DT_LANE0, a)])
    conv_b = small["conv_b"].reshape(1, -1)
    norm_w = small["ssm_norm_w"].reshape(1, -1)
    bg = small["b_gates"].reshape(1, -1)
    g1, b1 = small["ln1_g"].reshape(1, -1), small["ln1_b"].reshape(1, -1)
    g2, b2 = small["ln2_g"].reshape(1, -1), small["ln2_b"].reshape(1, -1)
    d_skip = small["d_skip"]
    xb = x.astype(BF16)

    qkv = _mm(xb, wts["qkv"], out_dtype=BF16, name="f_qkv")
    z = _mm(xb, wts["z"], out_dtype=BF16, name="f_z")
    xbc = _mm(xb, wts["xbc"], name="f_xbc")
    gl = _mm(xb, wts["gate"], out_dtype=BF16, name="f_gate")
    fd = _mm(xb, wts["fd"], name="f_fd")
    dt_c, ac_c, cf_c, dt_r, ac_r, cf_r = _stats_fwd(fd, bias_row, a_row)
    bk = _att_blocks(s)[1]
    ck4 = cf_r[F_LANE0:F_LANE0 + ATT_HEADS].reshape(N_HP, HP, s // bk, bk)
    return dict(locals())


def _stage_mid(c, attn, lse, wts, target):
    x, xb, qkv, z, xbc, gl, fd, ck4 = (c[k] for k in ("x", "xb", "qkv", "z", "xbc", "gl", "fd", "ck4"))
    dt_c, ac_c, dt_r, ac_r, a_row, bias_row = (c[k] for k in ("dt_c", "ac_c", "dt_r", "ac_r", "a_row", "bias_row"))
    conv_b, norm_w, bg, g1, b1, g2, b2, d_skip = (c[k] for k in ("conv_b", "norm_w", "bg", "g1", "b1", "g2", "b2",
                                                                "d_skip"))
    conv_w = c["wts"]["conv"]
    attn_d = _mm(attn, wts["pa"], out_dtype=BF16, name="f_pa")
    xact = _conv_fwd(xbc, conv_w, conv_b)
    dsk_pair = jnp.repeat(d_skip, SSM_HEAD_DIM).reshape(N_PAIR, LANES)
    y, hprev = _ssd_pair_fwd(xact, ac_c, dt_r, ac_r, dsk_pair)
    ssm = _gnorm_fwd(y, z, norm_w)
    ssm_d = _mm(ssm, wts["ps"], out_dtype=BF16, name="f_ps")
    mix = _mix_fwd(gl, bg, attn_d, ssm_d)
    mixed = _mm(mix, wts["out"], name="f_out")
    x1, x1_b = _ln1_fwd(x, mixed, g1, b1)
    gu = _mm(x1_b, wts["gu"], out_dtype=BF16, name="f_gu")
    act = _swiglu_fwd(gu)
    h = _mm(act, wts["down"], name="f_down")
    dpre2, dpre2_b, loss_row, dg2, db2 = _ln2_loss(x1, h, target, g2, b2)

    d_act = _mm(dpre2_b, wts["down"], tb=True, out_dtype=BF16, name="b_down_x")
    dw_down = _mm(act, dpre2_b, ta=True, name="b_down_w")
    dgu = _swiglu_bwd(gu, d_act)
    dffn = _mm(dgu, wts["gu"], tb=True, name="b_gu_x")
    dw_gu = _mm(x1_b, dgu, ta=True, name="b_gu_w")
    dpre1, dxr, dg1, db1 = _ln1_bwd(x, mixed, g1, dpre2, dffn)
    dmix = _mm(dpre1, wts["out"], tb=True, out_dtype=BF16, name="b_out_x")
    dw_out = _mm(mix, dpre1, ta=True, name="b_out_w")
    dattn_d, dssm_d, dgl, dbg = _mix_bwd(gl, bg, attn_d, ssm_d, dmix)
    dssm = _mm(dssm_d, wts["ps"], tb=True, out_dtype=BF16, name="b_ps_x")
    dw_ps = _mm(ssm, dssm_d, ta=True, name="b_ps_w")
    dattn = _mm(dattn_d, wts["pa"], tb=True, name="b_pa_x")
    dw_pa = _mm(attn, dattn_d, ta=True, name="b_pa_w")
    dy, dz, dnw = _gnorm_bwd(y, z, norm_w, dssm)
    dxact, ddt, da_row, dds_pair = _ssd_pair_bwd(xact, dt_c, ac_c, dt_r, ac_r, hprev, dy, dsk_pair, a_row)
    dds = dds_pair.reshape(SSM_HEADS, SSM_HEAD_DIM).sum(axis=1)
    dpre_c, dconv_w, dconv_b = _conv_bwd_pre(xbc, conv_w, conv_b, dxact)
    dxbc = _conv_bwd_in(dpre_c, conv_w)
    st, do_b = _att_prep(dattn, attn, lse)
    late = dict(pa=dw_pa, ps=dw_ps, out=dw_out, gu=dw_gu, down=dw_down)
    keep = ("st", "do_b", "ddt", "dxr", "dz", "dxbc", "dgl", "dconv_w", "dconv_b", "da_row", "dds", "dnw", "dbg",
            "dg1", "db1", "dg2", "db2", "loss_row")
    loc = locals()
    return {**c, **{k: loc[k] for k in keep}}, late


def _stage_out_w(c, att_grads):
    dq, dk, dv, dck, dcq = att_grads
    xb, fd, bias_row, ddt, dz, dxbc, dgl = (c[k] for k in ("xb", "fd", "bias_row", "ddt", "dz", "dxbc", "dgl"))
    s, a = xb.shape[0], c["a"]
    dcum = dck.reshape(ATT_HEADS, s) + dcq.reshape(ATT_HEADS, s)
    dfd, dbias = _stats_bwd(fd, bias_row, ddt, jnp.zeros((LANES, s), F32).at[F_LANE0:F_LANE0 + ATT_HEADS].set(dcum))
    dproj = (dq, dk, dv, dz, dxbc, dgl, dfd)
    dw_in = [_mm(xb, g_, ta=True, tb=(i == 0), name=f"b_in_w{i}") for i, g_ in enumerate(dproj)]
    grads = dict(q=dw_in[0], k=dw_in[1], v=dw_in[2], z=dw_in[3], xbc=dw_in[4], gate=dw_in[5], fd=dw_in[6],
                 conv=c["dconv_w"])
    small_g = dict(
        b_forget=dbias[0, F_LANE0:F_LANE0 + ATT_HEADS], conv_b=c["dconv_b"][0], dt_bias=dbias[0, :SSM_HEADS],
        a_log=c["da_row"][0, :SSM_HEADS] * a, d_skip=c["dds"], ssm_norm_w=c["dnw"][0], b_gates=c["dbg"][0],
        ln1_g=c["dg1"][0], ln1_b=c["db1"][0], ln2_g=c["dg2"][0], ln2_b=c["db2"][0])
    return c["loss_row"][0, 0], grads, small_g, dproj


def _stage_out_x(c, dproj, token):
    wts = c["wts"]
    dq_t, dk, dv, dz, dxbc, dgate, dfd = dproj
    w_fd = wts["fd"] + token.astype(BF16)
    dx = _mm_sum([(dq_t, True, wts["qkv"], 0), (dk, False, wts["qkv"], 1), (dv, False, wts["qkv"], 2),
                  (dfd, False, w_fd, 0)], c["dxr"], "b_in_x_qkv_fd")
    dx = _mm_sum([(dz, False, wts["z"], 0), (dgate, False, wts["gate"], 0)], dx, "b_in_x_z_gate")
    return _mm_sum([(dxbc, False, wts["xbc"], 0)], dx, "b_in_x_xbc")


BIG = ("w_in", "w_proj_attn", "w_proj_ssm", "w_out", "w_ffn_gate", "w_ffn_up", "w_ffn_down", "conv_w")
EARLY = ("w_in", "conv_w")
LATE = ("w_proj_attn", "w_proj_ssm", "w_out", "w_ffn_gate", "w_ffn_up", "w_ffn_down")
SMALL = ("b_forget", "conv_b", "dt_bias", "a_log", "d_skip", "ssm_norm_w", "b_gates", "ln1_g", "ln1_b", "ln2_g",
         "ln2_b")
SMALL_ROWS = 96
IN_SHARD = IN_WIDTH // N_DEV
IN_SEGMENTS = (("q", 0, 1024), ("k", 1024, 1024), ("v", 2048, 1024), ("f", 3072, ATT_HEADS), ("z", 3088, SSM_INNER),
               ("xbc", 5136, SSM_CONV_DIM), ("dt", 8208, SSM_HEADS), ("gate", 8240, 2 * D_MODEL))


def _cols_from_shards(shards, lo, hi):
    w = shards[0].shape[1]
    pieces = []
    for j in range(len(shards)):
        a, b = max(lo, j * w), min(hi, (j + 1) * w)
        if a < b:
            pieces.append(shards[j][:, a - j * w:b - j * w])
    return pieces[0] if len(pieces) == 1 else jnp.concatenate(pieces, axis=1)


def _shards_from_parts(parts, width):
    shards = []
    for j in range(N_DEV):
        lo, hi = j * width, (j + 1) * width
        pieces = []
        for mat, c0 in parts:
            a, b = max(lo, c0), min(hi, c0 + mat.shape[1])
            if a < b:
                pieces.append(mat[:, a - c0:b - c0])
        shards.append(pieces[0] if len(pieces) == 1 else jnp.concatenate(pieces, axis=1))
    return shards


def _pack_small(vals):
    flat = jnp.concatenate([vals[n].reshape(-1) for n in SMALL])
    return jnp.pad(flat, (0, SMALL_ROWS * LANES - flat.shape[0])).reshape(SMALL_ROWS, LANES)


def _unpack_small(pack, shapes):
    flat = pack.reshape(-1)
    out, off = {}, 0
    for n in SMALL:
        sz = math.prod(shapes[n])
        out[n] = flat[off:off + sz].reshape(shapes[n])
        off += sz
    return out


def kernel(x, w_in, b_forget, conv_w, conv_b, dt_bias, a_log, d_skip, ssm_norm_w, w_proj_attn, w_proj_ssm, b_gates, w_out, ln1_g, ln1_b, w_ffn_gate, w_ffn_up, w_ffn_down, ln2_g, ln2_b, loss_target, m_w_in, m_b_forget, m_conv_w, m_conv_b, m_dt_bias, m_a_log, m_d_skip, m_ssm_norm_w, m_w_proj_attn, m_w_proj_ssm, m_b_gates, m_w_out, m_ln1_g, m_ln1_b, m_w_ffn_gate, m_w_ffn_up, m_w_ffn_down, m_ln2_g, m_ln2_b, v_w_in, v_b_forget, v_conv_w, v_conv_b, v_dt_bias, v_a_log, v_d_skip, v_ssm_norm_w, v_w_proj_attn, v_w_proj_ssm, v_b_gates, v_w_out, v_ln1_g, v_ln1_b, v_w_ffn_gate, v_w_ffn_up, v_w_ffn_down, v_ln2_g, v_ln2_b):
    args = dict(locals())
    d, f = D_MODEL, FFN_HIDDEN
    big_w = {n: args[n][0] for n in BIG}
    small_w = {n: args[n][0] for n in SMALL}
    big_shapes = {n: args[n].shape for n in BIG}
    small_shapes = {n: args[n].shape for n in SMALL}

    early = dict(zip(EARLY, _all_gather([big_w["w_in"].astype(BF16), big_w["conv_w"]])))
    in_shards = [early["w_in"][j] for j in range(N_DEV)]
    seg = {n: _cols_from_shards(in_shards, c0, c0 + w) for n, c0, w in IN_SEGMENTS}
    wfd = jnp.concatenate([seg["dt"], seg["f"], jnp.zeros((d, LANES - SSM_HEADS - ATT_HEADS), BF16)], axis=1)
    wts = dict(qkv=jnp.concatenate([seg["q"], seg["k"], seg["v"]], axis=1), z=seg["z"], xbc=seg["xbc"],
               gate=seg["gate"], fd=wfd, conv=jnp.concatenate([early["conv_w"][j] for j in range(N_DEV)], axis=1))

    ctx = _stage_in(x[0], wts, small_w)
    attn, lse, gathered = _attention_fwd(ctx["qkv"], ctx["ck4"], [big_w[n].astype(BF16) for n in LATE])
    full = dict(zip(LATE, gathered))
    late_w = dict(
        pa=full["w_proj_attn"].reshape(d, d), ps=full["w_proj_ssm"].reshape(SSM_INNER, d),
        out=full["w_out"].reshape(d, d),
        gu=jnp.concatenate([full["w_ffn_gate"][j] for j in range(N_DEV)]
                           + [full["w_ffn_up"][j] for j in range(N_DEV)], axis=1),
        down=full["w_ffn_down"].reshape(f, d))
    ctx, gl = _stage_mid(ctx, attn, lse, late_w, loss_target[0])
    late_dest = dict(
        w_ffn_gate=jnp.stack([s_.astype(BF16) for s_ in _shards_from_parts([(gl["gu"][:, :f], 0)], f // N_DEV)]),
        w_ffn_up=jnp.stack([s_.astype(BF16) for s_ in _shards_from_parts([(gl["gu"][:, f:], 0)], f // N_DEV)]))
    for n, key in (("w_proj_attn", "pa"), ("w_proj_ssm", "ps"), ("w_out", "out"), ("w_ffn_down", "down")):
        late_dest[n] = gl[key].astype(BF16).reshape((N_DEV,) + big_shapes[n][1:])
    att_grads, late_recv = _attention_bwd(ctx["qkv"], ctx["ck4"], ctx["st"], ctx["do_b"], [late_dest[n] for n in LATE])
    loss_part, g, small_g, dproj = _stage_out_w(ctx, att_grads)
    loss = lax.psum(loss_part, ("x", "y", "c"))

    gfd = g["fd"]
    in_parts = dict(q=g["q"], k=g["k"], v=g["v"], f=gfd[:, F_LANE0:F_LANE0 + ATT_HEADS], z=g["z"], xbc=g["xbc"],
                    dt=gfd[:, DT_LANE0:DT_LANE0 + SSM_HEADS], gate=g["gate"])
    win_dest = jnp.stack([s_.astype(BF16) for s_ in
                          _shards_from_parts([(in_parts[n], c0) for n, c0, _ in IN_SEGMENTS], IN_SHARD)])
    conv_dest = jnp.stack(_shards_from_parts([(g["conv"], 0)], SSM_CONV_DIM // N_DEV))
    small_pack = _pack_small(small_g)
    last_parts = [win_dest, conv_dest, jnp.broadcast_to(small_pack, (N_DEV,) + small_pack.shape)]
    send_sems, recv_sems, parts_thru, lands_thru, token = _exchange_start(last_parts, _landing_zones(last_parts))
    grad_x = _stage_out_x(ctx, dproj, token[0, 0])
    early_recv = _exchange_wait(send_sems, recv_sems, parts_thru, lands_thru, grad_x)
    recv = dict(zip(LATE, late_recv))
    recv["w_in"], recv["conv_w"] = early_recv[0], early_recv[1]

    outs = {}
    for n in BIG:
        outs[n] = _adamw(recv[n], args[n], args["m_" + n], args["v_" + n], name="adamw_" + n)
    small4 = _adamw(early_recv[2], _pack_small(small_w)[None], _pack_small({n: args["m_" + n][0] for n in SMALL})[None],
                    _pack_small({n: args["v_" + n][0] for n in SMALL})[None], name="adamw_small")
    small_out = [_unpack_small(p, small_shapes) for p in small4]
    for n in SMALL:
        outs[n] = [so[n] for so in small_out]

    order = ("w_in", "b_forget", "conv_w", "conv_b", "dt_bias", "a_log", "d_skip", "ssm_norm_w", "w_proj_attn",
             "w_proj_ssm", "b_gates", "w_out", "ln1_g", "ln1_b", "w_ffn_gate", "w_ffn_up", "w_ffn_down", "ln2_g",
             "ln2_b")
    res = [loss, grad_x[None]]
    for i in range(4):
        res += [outs[n][i] for n in order]
    return tuple(res)
```

```python
import functools
import math

import jax
import jax.numpy as jnp
from jax import lax
from jax.experimental import pallas as pl
from jax.experimental.pallas import tpu as pltpu

F32 = jnp.float32
BF16 = jnp.bfloat16

N_DEV = 8
D_MODEL = 1024
ATT_HEADS = 16
ATT_HEAD_DIM = 64
SSM_INNER = 2048
SSM_HEADS = 32
SSM_HEAD_DIM = 64
SSM_GROUPS = 4
SSM_HEADS_PER_GROUP = 8
SSM_STATE = 128
SSM_CONV = 4
SSM_CHUNK = 128
SSM_CONV_DIM = 3072
FFN_HIDDEN = 2816
IN_WIDTH = 10288
DEEPNORM_ALPHA = 2.0 ** 0.25
LN_EPS = 1e-5
RMS_EPS = 1e-5
ADAM_LR, ADAM_B1, ADAM_B2, ADAM_EPS, ADAM_WD, ADAM_STEP = 0.001, 0.9, 0.999, 1e-08, 0.01, 10
ATT_SCALE = 1.0 / math.sqrt(ATT_HEAD_DIM)

LANES = 128
VMEM_LIMIT = 56 * 1024 * 1024
NEG = -1e30

DT_LANE0 = 0
F_LANE0 = 32
HI = lax.Precision.HIGHEST


def _cp(*sem):
    return pltpu.CompilerParams(dimension_semantics=sem, vmem_limit_bytes=VMEM_LIMIT)


def _tile(n, cap=1408):
    for t in (3072, 2816, 2048, 1536, 1408, 1024, 512, 384, 256, 128):
        if t <= cap and n % t == 0:
            return t
    return n


MM_VMEM_BUDGET = VMEM_LIMIT - 4 * 2 ** 20


def _mm_tiles(m, n, k, a_bytes, b_bytes, out_bytes, has_add):
    tm = _tile(m)

    def need(tn, tk):
        blocks = tm * tk * a_bytes + tk * tn * b_bytes + tm * tn * (out_bytes + (4 if has_add else 0))
        casts = (tm * tk * 2 if a_bytes == 4 else 0) + (tk * tn * 2 if b_bytes == 4 else 0)
        return 2 * blocks + casts + tm * tn * 4

    tns = [t for t in (3072, 2816, 2048, 1536, 1408, 1024, 512, 384, 256, 128) if n % t == 0] or [n]
    for tk in ([k] if k <= 3072 else []) + [t for t in (2816, 1024, 512, 256, 128) if k % t == 0]:
        fits = [tn for tn in tns if need(tn, tk) <= MM_VMEM_BUDGET]
        if fits and fits[0] >= min(1024, tns[0]):
            return tm, fits[0], tk
    return tm, tns[-1], tk


def _sigmoid(x):
    return 1.0 / (1.0 + jnp.exp(-x))


def _mm(a, b, *, ta=False, tb=False, out_dtype=F32, add=None, name):
    m, k = (a.shape[1], a.shape[0]) if ta else a.shape
    n = b.shape[0] if tb else b.shape[1]
    assert (b.shape[1] if tb else b.shape[0]) == k
    tm, tn, tk = _mm_tiles(m, n, k, a.dtype.itemsize, b.dtype.itemsize, jnp.dtype(out_dtype).itemsize, add is not None)
    nk = k // tk
    dims = (((0,) if ta else (1,), (1,) if tb else (0,)), ((), ()))

    def body_single(*refs):
        a_ref, b_ref = refs[:2]
        r = lax.dot_general(a_ref[...].astype(BF16), b_ref[...].astype(BF16), dims, preferred_element_type=F32)
        if add is not None:
            r = r + refs[2][...]
        refs[-1][...] = r.astype(refs[-1].dtype)

    def body(*refs):
        if add is None:
            a_ref, b_ref, o_ref, acc_ref = refs
        else:
            a_ref, b_ref, c_ref, o_ref, acc_ref = refs
        kk = pl.program_id(2)

        @pl.when(kk == 0)
        def _():
            acc_ref[...] = jnp.zeros_like(acc_ref)

        acc_ref[...] += lax.dot_general(a_ref[...].astype(BF16), b_ref[...].astype(BF16), dims,
                                        preferred_element_type=F32)

        @pl.when(kk == nk - 1)
        def _():
            r = acc_ref[...]
            if add is not None:
                r = r + c_ref[...]
            o_ref[...] = r.astype(o_ref.dtype)

    a_spec = pl.BlockSpec((tk, tm), lambda i, j, kk: (kk, i)) if ta else pl.BlockSpec((tm, tk), lambda i, j, kk: (i, kk))
    b_spec = pl.BlockSpec((tn, tk), lambda i, j, kk: (j, kk)) if tb else pl.BlockSpec((tk, tn), lambda i, j, kk: (kk, j))
    o_spec = pl.BlockSpec((tm, tn), lambda i, j, kk: (i, j))
    in_specs, args = [a_spec, b_spec], [a, b]
    if add is not None:
        in_specs.append(o_spec)
        args.append(add)
    return pl.pallas_call(
        body_single if nk == 1 else body, name=name, grid=(m // tm, n // tn, nk), in_specs=in_specs, out_specs=o_spec,
        out_shape=jax.ShapeDtypeStruct((m, n), out_dtype),
        scratch_shapes=[] if nk == 1 else [pltpu.VMEM((tm, tn), F32)],
        compiler_params=_cp("parallel", "parallel", "arbitrary"),
    )(*args)


def _mm_sum(terms, add, name):
    m, n = add.shape
    ks = [a.shape[0] if ta else a.shape[1] for a, ta, _, _ in terms]

    def need(tm):
        blocks = sum(tm * k * a.dtype.itemsize + n * k * b.dtype.itemsize for (a, _, b, _), k in zip(terms, ks))
        casts = sum(tm * k * 2 for (a, _, _, _), k in zip(terms, ks) if a.dtype.itemsize == 4)
        return 2 * (blocks + 2 * tm * n * 4) + casts + tm * n * 4

    tm = next(t for t in (1024, 512, 256, 128) if m % t == 0 and need(t) <= MM_VMEM_BUDGET)

    def body(*refs):
        r = refs[-2][...]
        for i, (_, ta, _, _) in enumerate(terms):
            dims = (((0,) if ta else (1,), (1,)), ((), ()))
            r = r + lax.dot_general(refs[2 * i][...].astype(BF16), refs[2 * i + 1][...].astype(BF16), dims,
                                    preferred_element_type=F32)
        refs[-1][...] = r

    in_specs, args = [], []
    for (a, ta, b, jb), k in zip(terms, ks):
        in_specs += [pl.BlockSpec((k, tm), lambda i: (0, i)) if ta else pl.BlockSpec((tm, k), lambda i: (i, 0)),
                     pl.BlockSpec((n, k), lambda i, jb=jb: (0, jb))]
        args += [a, b]
    o_spec = pl.BlockSpec((tm, n), lambda i: (i, 0))
    return pl.pallas_call(body, name=name, grid=(m // tm,), in_specs=in_specs + [o_spec], out_specs=o_spec,
                          out_shape=jax.ShapeDtypeStruct((m, n), F32), compiler_params=_cp("parallel"))(*args, add)


def _tri(n, lower=True):
    r = lax.broadcasted_iota(jnp.int32, (n, n), 0)
    c = lax.broadcasted_iota(jnp.int32, (n, n), 1)
    return jnp.where((r >= c) if lower else (c >= r), 1.0, 0.0).astype(F32)


def _stats_fwd(fd, bias_row, a_row):
    s = fd.shape[0]
    blk = SSM_CHUNK

    def body(fd_ref, bias_ref, a_ref, dt_ref, ac_ref, cf_ref, dtr_ref, acr_ref, cfr_ref, carry_ref):
        @pl.when(pl.program_id(0) == 0)
        def _():
            carry_ref[...] = jnp.zeros_like(carry_ref)

        v = fd_ref[...] + bias_ref[...]
        dt = jnp.maximum(v, 0.0) + jnp.log(1.0 + jnp.exp(-jnp.abs(v)))
        lf = jnp.minimum(v, 0.0) - jnp.log(1.0 + jnp.exp(-jnp.abs(v)))
        tri = _tri(blk)
        ac = jnp.dot(tri, dt * a_ref[...], precision=HI, preferred_element_type=F32)
        cf = jnp.dot(tri, lf, precision=HI, preferred_element_type=F32) + carry_ref[0:1, :]
        carry_ref[...] = carry_ref[...] + jnp.sum(lf, axis=0, keepdims=True)
        dt_ref[...] = dt
        ac_ref[...] = ac
        cf_ref[...] = cf
        dtr_ref[...] = dt.T
        acr_ref[...] = ac.T
        cfr_ref[...] = cf.T

    col = pl.BlockSpec((blk, LANES), lambda i: (i, 0))
    row = pl.BlockSpec((LANES, blk), lambda i: (0, i))
    vec = pl.BlockSpec((1, LANES), lambda i: (0, 0))
    return pl.pallas_call(
        body, name="stats_fwd", grid=(s // blk,), in_specs=[col, vec, vec],
        out_specs=[col, col, col, row, row, row],
        out_shape=[jax.ShapeDtypeStruct((s, LANES), F32)] * 3 + [jax.ShapeDtypeStruct((LANES, s), F32)] * 3,
        scratch_shapes=[pltpu.VMEM((8, LANES), F32)],
        compiler_params=_cp("arbitrary"),
    )(fd, bias_row, a_row)


def _stats_bwd(fd, bias_row, ddt, dcum_rows):
    s = fd.shape[0]
    blk = SSM_CHUNK
    nb = s // blk

    def body(fd_ref, bias_ref, ddt_ref, dck_ref, o_ref, db_ref, carry_ref):
        @pl.when(pl.program_id(0) == 0)
        def _():
            carry_ref[...] = jnp.zeros_like(carry_ref)
            db_ref[...] = jnp.zeros_like(db_ref)

        v = fd_ref[...] + bias_ref[...]
        dcum = dck_ref[...].T
        dlf = jnp.dot(_tri(blk, lower=False), dcum, precision=HI, preferred_element_type=F32) + carry_ref[0:1, :]
        carry_ref[...] = carry_ref[...] + jnp.sum(dcum, axis=0, keepdims=True)
        lane = lax.broadcasted_iota(jnp.int32, v.shape, 1)
        g = jnp.where(lane < F_LANE0, ddt_ref[...] * _sigmoid(v), dlf * _sigmoid(-v))
        g = jnp.where(lane < F_LANE0 + ATT_HEADS, g, 0.0)
        o_ref[...] = g.astype(o_ref.dtype)
        db_ref[...] += jnp.sum(g, axis=0, keepdims=True)

    col = pl.BlockSpec((blk, LANES), lambda i: (nb - 1 - i, 0))
    row = pl.BlockSpec((LANES, blk), lambda i: (0, nb - 1 - i))
    vec = pl.BlockSpec((1, LANES), lambda i: (0, 0))
    return pl.pallas_call(
        body, name="stats_bwd", grid=(nb,), in_specs=[col, vec, col, row], out_specs=[col, vec],
        out_shape=[jax.ShapeDtypeStruct((s, LANES), BF16), jax.ShapeDtypeStruct((1, LANES), F32)],
        scratch_shapes=[pltpu.VMEM((8, LANES), F32)],
        compiler_params=_cp("arbitrary"),
    )(fd, bias_row, ddt, dcum_rows)


HP = LANES // ATT_HEAD_DIM
N_HP = ATT_HEADS // HP


def _att_blocks(s):
    return (512, 1024) if s % 1024 == 0 and s >= 4096 else (64, 128)


_QK = (((1,), (1,)), ((), ()))
_HALF = ATT_HEAD_DIM // 2
_PAD = 16


def _head_cols(a):
    return slice(a * ATT_HEAD_DIM, (a + 1) * ATT_HEAD_DIM)


def _causal(shape, off):
    r = lax.broadcasted_iota(jnp.int32, shape, 0)
    c = lax.broadcasted_iota(jnp.int32, shape, 1)
    return c <= r + off


def _attention_fwd(qkv, ck4, gather_parts):
    s = qkv.shape[0]
    bk = _att_blocks(s)[1]
    bq = bk
    nq, nk = s // bq, s // bk
    n = len(gather_parts)

    def body(q_ref, k_ref, v_ref, ck_ref, *rest):
        comm_in, (o_ref, lse_ref), comm_out, sems = rest[:n], rest[n:n + 2], rest[n + 2:2 * n + 2], rest[2 * n + 2:]
        i = pl.program_id(1)
        if n:
            @pl.when((pl.program_id(0) == 0) & (i == 0))
            def _():
                for cp in _comm_copies(comm_in, comm_out, sems, False):
                    cp.start()

        n_full = (i * bq) // bk
        qs = [(q_ref[:, _head_cols(a)].astype(F32) * ATT_SCALE).astype(BF16) for a in range(HP)]

        upper_q = lax.broadcasted_iota(jnp.int32, (bq, LANES), 1) >= ATT_HEAD_DIM

        def absorb(j, carry, keys=slice(0, bk), rows=slice(0, bq), masked=False):
            nk_ = keys.stop - keys.start
            ks = pl.ds(pl.multiple_of(j * bk, bk) + keys.start, nk_)
            v_both = v_ref[ks, :]
            out = []
            for a in range(HP):
                m, acc = carry[a]
                sc = lax.dot_general(qs[a][rows], k_ref[ks, _head_cols(a)], _QK, preferred_element_type=F32)
                sc = sc - ck_ref[0, a, pl.ds(j, 1), keys]
                if masked:
                    sc = jnp.where(_causal(sc.shape, 0), sc, NEG)
                m_new = jnp.maximum(m, jnp.max(sc, axis=1, keepdims=True))
                p = jnp.exp((sc - m_new).astype(BF16))
                upper_v = lax.broadcasted_iota(jnp.int32, (nk_, LANES), 1) >= ATT_HEAD_DIM
                v_aug = jnp.where(upper_v == (a == 1), v_both, jnp.ones_like(v_both))
                acc = jnp.exp(m - m_new) * acc + jnp.dot(p, v_aug, preferred_element_type=F32)
                out.append((m_new, acc))
            return tuple(out)

        init = tuple((jnp.full((bq, 1), NEG, F32), jnp.zeros((bq, LANES), F32)) for _ in range(HP))
        carry = lax.fori_loop(0, n_full, absorb, init)
        hq = bq // 2
        carry = absorb(n_full, carry, keys=slice(0, hq), masked=True)
        low = absorb(n_full, tuple((m[hq:], acc[hq:]) for m, acc in carry), keys=slice(hq, bk), rows=slice(hq, bq),
                     masked=True)
        carry = tuple((jnp.concatenate([m[:hq], ml], axis=0), jnp.concatenate([acc[:hq], al], axis=0))
                      for (m, acc), (ml, al) in zip(carry, low))
        outs, lses = [], []
        for a in range(HP):
            m, acc = carry[a]
            l = pltpu.roll(acc, ATT_HEAD_DIM, 1)
            outs.append(acc / l)
            lses.append(m + jnp.log(l))
        o_ref[...] = jnp.where(upper_q, outs[1], outs[0])
        lse_ref[...] = jnp.where(upper_q, lses[1], lses[0])
        if n:
            @pl.when((pl.program_id(0) == N_HP - 1) & (i == nq - 1))
            def _():
                for cp in _comm_copies(comm_in, comm_out, sems, False):
                    cp.wait()

    q_spec = pl.BlockSpec((bq, LANES), lambda h, i: (i, h))
    anyspec = pl.BlockSpec(memory_space=pl.ANY)
    res = pl.pallas_call(
        body, name="att_fwd", grid=(N_HP, nq),
        in_specs=[q_spec, pl.BlockSpec((s, LANES), lambda h, i: (0, N_HP + h)),
                  pl.BlockSpec((s, LANES), lambda h, i: (0, 2 * N_HP + h)),
                  pl.BlockSpec((1, HP, nk, bk), lambda h, i: (h, 0, 0, 0))] + [anyspec] * n,
        out_specs=[q_spec, q_spec] + [anyspec] * n,
        out_shape=[jax.ShapeDtypeStruct((s, D_MODEL), F32)] * 2 + _comm_out_shapes(gather_parts, False),
        scratch_shapes=_comm_sems(n) if n else [],
        compiler_params=_cp("arbitrary", "arbitrary"),
    )(qkv, qkv, qkv, ck4, *gather_parts)
    return res[0], res[1], list(res[2:])


def _att_prep(dattn_d, w_pa, o, lse_rep):
    s = o.shape[0]
    bs = _tile(s, 512)

    def body(g_ref, w_ref, o_ref, lse_ref, st_ref, dob_ref):
        do = lax.dot_general(g_ref[...], w_ref[...], (((1,), (1,)), ((), ())), preferred_element_type=F32)
        r = lax.broadcasted_iota(jnp.int32, (LANES, LANES), 0) // ATT_HEAD_DIM
        c = lax.broadcasted_iota(jnp.int32, (LANES, LANES), 1) // ATT_HEAD_DIM
        e = jnp.where(r == c, 1.0, 0.0).astype(F32)
        lane = lax.broadcasted_iota(jnp.int32, (bs, LANES), 1)
        for p in range(D_MODEL // LANES):
            cs = slice(p * LANES, (p + 1) * LANES)
            dd = do[:, cs]
            delta = jnp.dot(dd * o_ref[:, cs], e, precision=HI, preferred_element_type=F32)
            st_ref[:, cs] = jnp.where(lane % ATT_HEAD_DIM < _HALF, lse_ref[:, cs], delta)
            dob_ref[:, cs] = dd.astype(BF16)

    assert dattn_d.dtype == BF16 and w_pa.dtype == BF16
    spec = pl.BlockSpec((bs, D_MODEL), lambda i: (i, 0))
    w_spec = pl.BlockSpec(w_pa.shape, lambda i: (0, 0))
    return pl.pallas_call(body, name="att_prep", grid=(s // bs,), in_specs=[spec, w_spec, spec, spec],
                          out_specs=[spec, spec],
                          out_shape=[jax.ShapeDtypeStruct((s, D_MODEL), F32), jax.ShapeDtypeStruct((s, D_MODEL), BF16)],
                          compiler_params=_cp("parallel"))(dattn_d, w_pa, o, lse_rep)


def _attention_bwd(qkv, ck4, st, do_b, exchange_parts):
    s = qkv.shape[0]
    bq, bk = _att_blocks(s)
    nq, nk, per = s // bq, s // bk, bk // bq
    _T = (((0,), (0,)), ((), ()))
    n = len(exchange_parts)

    def body(q_ref, k_ref, v_ref, ck_ref, st_ref, do_ref, *rest):
        comm_in, (dq_ref, dk_ref, dv_ref, dck_ref, dcq_ref) = rest[:n], rest[n:n + 5]
        comm_out, sems, (dk_acc, dv_acc) = rest[n + 5:2 * n + 5], rest[2 * n + 5:-2], rest[-2:]
        j = pl.program_id(1)
        if n:
            @pl.when((pl.program_id(0) == 0) & (j == 0))
            def _():
                for cp in _comm_copies(comm_in, comm_out, sems, True):
                    cp.start()

        @pl.when(j == 0)
        def _():
            dq_ref[...] = jnp.zeros_like(dq_ref)
            dcq_ref[...] = jnp.zeros_like(dcq_ref)

        dk_acc[...] = jnp.zeros_like(dk_acc)
        dv_acc[...] = jnp.zeros_like(dv_acc)

        ones_q, ones_k = jnp.ones((_PAD, bq), BF16), jnp.ones((_PAD, bk), BF16)
        k_t = [jnp.concatenate([k_ref[:, _head_cols(a)].T, ones_k], axis=0) for a in range(HP)]

        def step(i, off=None, kl=slice(0, bk)):
            rows = pl.ds(pl.multiple_of(i * bq, bq), bq)
            for a in range(HP):
                cs = _head_cols(a)
                q = (q_ref[rows, cs].astype(F32) * ATT_SCALE).astype(BF16)
                do_a = do_ref[rows, cs]
                sc = lax.dot_general(q, k_ref[kl, cs], _QK, preferred_element_type=F32) - ck_ref[0, a, pl.ds(j, 1), kl]
                if off is not None:
                    sc = jnp.where(_causal(sc.shape, off), sc, NEG)
                p = jnp.exp(sc - st_ref[rows, a * ATT_HEAD_DIM:a * ATT_HEAD_DIM + 1])
                dp = lax.dot_general(do_a, v_ref[kl, cs], _QK, preferred_element_type=F32)
                ds = p * (dp - st_ref[rows, a * ATT_HEAD_DIM + _HALF:a * ATT_HEAD_DIM + _HALF + 1])
                ds_b = ds.astype(BF16)
                dv_acc[a, :, kl] += jnp.dot(do_a.T, p.astype(BF16), preferred_element_type=F32)
                dk_acc[a, :, kl] += jnp.dot(jnp.concatenate([q.T, ones_q], axis=0), ds_b, preferred_element_type=F32)
                dqs = lax.dot_general(k_t[a][:, kl], ds_b, _QK, preferred_element_type=F32)
                dq_ref[cs, rows] += dqs[:ATT_HEAD_DIM] * ATT_SCALE
                dcq_ref[0, a, pl.ds(i, 1), :] += jnp.sum(dqs[ATT_HEAD_DIM:ATT_HEAD_DIM + 8], axis=0,
                                                         keepdims=True) * 0.125

        for t in range(per):
            step(j * per + t, off=t * bq, kl=slice(0, (t + 1) * bq))

        def full(i, c):
            step(i)
            return c

        lax.fori_loop((j + 1) * per, nq, full, 0)
        for a in range(HP):
            dk_ref[:, _head_cols(a)] = dk_acc[a, :ATT_HEAD_DIM].T.astype(dk_ref.dtype)
            dv_ref[:, _head_cols(a)] = dv_acc[a].T.astype(dv_ref.dtype)
            dck_ref[0, a, pl.ds(j, 1), :] = -dk_acc[a, ATT_HEAD_DIM:ATT_HEAD_DIM + 1]
        if n:
            @pl.when((pl.program_id(0) == N_HP - 1) & (j == nk - 1))
            def _():
                for cp in _comm_copies(comm_in, comm_out, sems, True):
                    cp.wait()

    res = pl.BlockSpec((s, LANES), lambda h, j: (0, h))
    ck_spec = pl.BlockSpec((1, HP, nk, bk), lambda h, j: (h, 0, 0, 0))
    kout = pl.BlockSpec((bk, LANES), lambda h, j: (j, h))
    anyspec = pl.BlockSpec(memory_space=pl.ANY)
    outs = pl.pallas_call(
        body, name="att_bwd", grid=(N_HP, nk),
        in_specs=[res, pl.BlockSpec((bk, LANES), lambda h, j: (j, N_HP + h)),
                  pl.BlockSpec((bk, LANES), lambda h, j: (j, 2 * N_HP + h)), ck_spec, res, res] + [anyspec] * n,
        out_specs=[pl.BlockSpec((LANES, s), lambda h, j: (h, 0)), kout, kout, ck_spec,
                   pl.BlockSpec((1, HP, nq, bq), lambda h, j: (h, 0, 0, 0))] + [anyspec] * n,
        out_shape=[jax.ShapeDtypeStruct((D_MODEL, s), F32), jax.ShapeDtypeStruct((s, D_MODEL), BF16),
                   jax.ShapeDtypeStruct((s, D_MODEL), BF16), jax.ShapeDtypeStruct((N_HP, HP, nk, bk), F32),
                   jax.ShapeDtypeStruct((N_HP, HP, nq, bq), F32)] + _comm_out_shapes(exchange_parts, True),
        scratch_shapes=(_comm_sems(n) if n else [])
        + [pltpu.VMEM((HP, ATT_HEAD_DIM + _PAD, bk), F32), pltpu.VMEM((HP, ATT_HEAD_DIM, bk), F32)],
        compiler_params=_cp("arbitrary", "arbitrary"),
    )(qkv, qkv, qkv, ck4, st, do_b, *exchange_parts)
    return outs[:5], list(outs[5:])


def _silu_and_grad(x):
    sg = _sigmoid(x)
    return x * sg, sg * (1.0 + x * (1.0 - sg))


SUBLANES = 8


def _conv_taps(cur, before, w_rows, bias):
    n, c = cur.shape
    cur3 = cur.reshape(n // SUBLANES, SUBLANES, c)
    sub = lax.broadcasted_iota(jnp.int32, (1, SUBLANES, c), 1)
    taps = []
    for k in range(SSM_CONV):
        sh = SSM_CONV - 1 - k
        if sh == 0:
            taps.append(cur3)
            continue
        rot = pltpu.roll(cur3, sh, 1)
        prev = jnp.concatenate([pltpu.roll(before, sh, 0)[None], rot[:-1]], axis=0)
        taps.append(jnp.where(sub < sh, prev, rot))
    pre = bias[None] + sum(w_rows[k][None] * taps[k] for k in range(SSM_CONV))
    return pre.reshape(n, c), [t.reshape(n, c) for t in taps]


def _conv_col_chunks(w_ref, b_ref, bc, cc):
    for c0 in range(0, bc, cc):
        cols = slice(c0, c0 + cc)
        yield cols, [w_ref[k:k + 1, cols] for k in range(SSM_CONV)], None if b_ref is None else b_ref[:, cols]


def _conv_specs(s, bs, bc):
    cur = pl.BlockSpec((bs, bc), lambda j, i: (i, j))
    halo = pl.BlockSpec((8, bc), lambda j, i: (jnp.maximum(i * (bs // 8) - 1, 0), j))
    w = pl.BlockSpec((SSM_CONV, bc), lambda j, i: (0, j))
    b = pl.BlockSpec((1, bc), lambda j, i: (0, j))
    return cur, halo, w, b


def _conv_fwd(xbc, w, b):
    s, c = xbc.shape
    bs, bc = _tile(s, 512), 1024
    rc, cc = bs, 128

    def body(x_ref, h_ref, w_ref, b_ref, o_ref):
        first = pl.program_id(1) == 0
        for cols, w_rows, bias in _conv_col_chunks(w_ref, b_ref, bc, cc):
            def step(r, before):
                rows = pl.ds(pl.multiple_of(r * rc, rc), rc)
                cur = x_ref[rows, cols]
                pre, _ = _conv_taps(cur, before, w_rows, bias)
                o_ref[rows, cols] = pre * _sigmoid(pre)
                return cur[rc - SUBLANES:]

            lax.fori_loop(0, bs // rc, step, jnp.where(first, 0.0, h_ref[:, cols]))

    cur, halo, ws, bsp = _conv_specs(s, bs, bc)
    return pl.pallas_call(body, name="conv_fwd", grid=(c // bc, s // bs), in_specs=[cur, halo, ws, bsp],
                          out_specs=cur, out_shape=jax.ShapeDtypeStruct((s, c), F32),
                          compiler_params=_cp("parallel", "parallel"))(xbc, xbc, w, b)


def _conv_bwd_pre(xbc, w, b, dact):
    s, c = xbc.shape
    bs, bc = _tile(s, 512), 1024
    rc, cc = _tile(bs, 256), 128

    def body(x_ref, h_ref, w_ref, b_ref, g_ref, dp_ref, dw_ref, db_ref):
        first = pl.program_id(1) == 0

        @pl.when(first)
        def _():
            dw_ref[...] = jnp.zeros_like(dw_ref)
            db_ref[...] = jnp.zeros_like(db_ref)

        for cols, w_rows, bias in _conv_col_chunks(w_ref, b_ref, bc, cc):
            def step(r, carry):
                before, sums = carry
                rows = pl.ds(pl.multiple_of(r * rc, rc), rc)
                cur = x_ref[rows, cols]
                pre, taps = _conv_taps(cur, before, w_rows, bias)
                dpre = g_ref[rows, cols].astype(F32) * _silu_and_grad(pre)[1]
                dp_ref[rows, cols] = dpre.astype(dp_ref.dtype)
                terms = [dpre * t for t in taps] + [dpre]
                sums = tuple(a + jnp.sum(t.reshape(rc // SUBLANES, SUBLANES, cc), axis=0) for a, t in zip(sums, terms))
                return cur[rc - SUBLANES:], sums

            zero = jnp.zeros((SUBLANES, cc), F32)
            _, sums = lax.fori_loop(0, bs // rc, step,
                                    (jnp.where(first, 0.0, h_ref[:, cols]), (zero,) * (SSM_CONV + 1)))
            for k in range(SSM_CONV):
                dw_ref[k:k + 1, cols] += jnp.sum(sums[k], axis=0, keepdims=True)
            db_ref[:, cols] += jnp.sum(sums[SSM_CONV], axis=0, keepdims=True)

    cur, halo, ws, bsp = _conv_specs(s, bs, bc)
    return pl.pallas_call(
        body, name="conv_bwd_pre", grid=(c // bc, s // bs), in_specs=[cur, halo, ws, bsp, cur],
        out_specs=[cur, ws, bsp],
        out_shape=[jax.ShapeDtypeStruct((s, c), BF16), jax.ShapeDtypeStruct((SSM_CONV, c), F32),
                   jax.ShapeDtypeStruct((1, c), F32)],
        compiler_params=_cp("parallel", "arbitrary"))(xbc, xbc, w, b, dact)


def _conv_bwd_in(dpre, w):
    s, c = dpre.shape
    bs, bc = _tile(s, 512), 1024
    nb = s // bs
    rc, cc = bs, 256
    nr = bs // rc

    def body(g_ref, n_ref, w_ref, o_ref):
        last = pl.program_id(1) == nb - 1
        sub = lax.broadcasted_iota(jnp.int32, (1, SUBLANES, cc), 1)
        for cols, w_rows, _ in _conv_col_chunks(w_ref, None, bc, cc):
            def step(i, after):
                rows = pl.ds(pl.multiple_of((nr - 1 - i) * rc, rc), rc)
                cur = g_ref[rows, cols].astype(F32)
                cur3 = cur.reshape(rc // SUBLANES, SUBLANES, cc)
                acc = w_rows[SSM_CONV - 1][None] * cur3
                for sh in range(1, SSM_CONV):
                    rot = pltpu.roll(cur3, SUBLANES - sh, 1)
                    nxt = jnp.concatenate([rot[1:], pltpu.roll(after, SUBLANES - sh, 0)[None]], axis=0)
                    acc = acc + w_rows[SSM_CONV - 1 - sh][None] * jnp.where(sub >= SUBLANES - sh, nxt, rot)
                o_ref[rows, cols] = acc.reshape(rc, cc).astype(o_ref.dtype)
                return cur[0:SUBLANES]

            lax.fori_loop(0, nr, step, jnp.where(last, 0.0, n_ref[0:SUBLANES, cols].astype(F32)))

    cur = pl.BlockSpec((bs, bc), lambda j, i: (i, j))
    nxt = pl.BlockSpec((16, bc), lambda j, i: (jnp.minimum((i + 1) * (bs // 16), s // 16 - 1), j))
    ws = pl.BlockSpec((SSM_CONV, bc), lambda j, i: (0, j))
    return pl.pallas_call(body, name="conv_bwd_in", grid=(c // bc, nb), in_specs=[cur, nxt, ws], out_specs=cur,
                          out_shape=jax.ShapeDtypeStruct((s, c), BF16),
                          compiler_params=_cp("parallel", "parallel"))(dpre, dpre, w)


def _dotT(a, b):
    return lax.dot_general(a.astype(BF16), b.astype(BF16), (((1,), (1,)), ((), ())), preferred_element_type=F32)


def _dot(a, b):
    return jnp.dot(a.astype(BF16), b.astype(BF16), preferred_element_type=F32)


N_PAIR = SSM_HEADS // HP
PAIRS_PER_GROUP = SSM_HEADS_PER_GROUP // HP


def _pair_consts():
    L = SSM_CHUNK
    lane = lax.broadcasted_iota(jnp.int32, (L, LANES), 1)
    lane1 = lax.broadcasted_iota(jnp.int32, (1, LANES), 1)
    li = lax.broadcasted_iota(jnp.int32, (L, L), 0)
    si = lax.broadcasted_iota(jnp.int32, (L, L), 1)
    return lane >= ATT_HEAD_DIM, lane1 >= ATT_HEAD_DIM, li, si


def _ssd_pair_fwd(xbc_act, ac_c, dt_r, ac_r, dsk_pair):
    s = xbc_act.shape[0]
    L, N, G = SSM_CHUNK, SSM_STATE, SSM_GROUPS
    nc = s // L

    def body(xbc_ref, ac_ref, dtr_ref, acr_ref, dsk_ref, y_ref, hp_ref, st_ref):
        @pl.when(pl.program_id(0) == 0)
        def _():
            st_ref[...] = jnp.zeros_like(st_ref)

        upper, up1, li, si = _pair_consts()
        for g in range(G):
            b_g = xbc_ref[:, SSM_INNER + g * N:SSM_INNER + (g + 1) * N]
            c_g = xbc_ref[:, SSM_INNER + G * N + g * N:SSM_INNER + G * N + (g + 1) * N]
            cb = _dotT(c_g, b_g)
            b_t = b_g.T
            for q in range(PAIRS_PER_GROUP):
                pp = g * PAIRS_PER_GROUP + q
                cols = slice(pp * LANES, (pp + 1) * LANES)
                xs = xbc_ref[:, cols]
                ht = st_ref[pp]
                hp_ref[0, pp] = ht
                y = dsk_ref[pp:pp + 1, :] * xs
                s_new = jnp.zeros((N, LANES), F32)
                ea, el = [], []
                for a in range(HP):
                    h = HP * pp + a
                    acol = jnp.broadcast_to(ac_ref[:, h:h + 1], (L, LANES))
                    arow, dtrow = acr_ref[h:h + 1, :], dtr_ref[h:h + 1, :]
                    alast = ac_ref[L - 1:L, h:h + 1]
                    decay = jnp.exp(jnp.where(li >= si, acol - arow, NEG))
                    xs_a = jnp.where(upper == (a == 1), xs, 0.0)
                    y = y + _dot(cb * decay * dtrow, xs_a)
                    s_new = s_new + _dot(b_t * (dtrow * jnp.exp(alast - arow)), xs_a)
                    ea.append(jnp.exp(acol))
                    el.append(jnp.exp(alast))
                y_ref[:, cols] = y + jnp.where(upper, ea[1], ea[0]) * _dot(c_g, ht)
                st_ref[pp] = ht * jnp.where(up1, el[1], el[0]) + s_new

    col = pl.BlockSpec((L, LANES), lambda c: (c, 0))
    row = pl.BlockSpec((LANES, L), lambda c: (0, c))
    return pl.pallas_call(
        body, name="ssd_fwd", grid=(nc,),
        in_specs=[pl.BlockSpec((L, SSM_CONV_DIM), lambda c: (c, 0)), col, row, row,
                  pl.BlockSpec((N_PAIR, LANES), lambda c: (0, 0))],
        out_specs=[pl.BlockSpec((L, SSM_INNER), lambda c: (c, 0)),
                   pl.BlockSpec((1, N_PAIR, N, LANES), lambda c: (c, 0, 0, 0))],
        out_shape=[jax.ShapeDtypeStruct((s, SSM_INNER), F32), jax.ShapeDtypeStruct((nc, N_PAIR, N, LANES), F32)],
        scratch_shapes=[pltpu.VMEM((N_PAIR, N, LANES), F32)],
        compiler_params=_cp("arbitrary"),
    )(xbc_act, ac_c, dt_r, ac_r, dsk_pair)


def _ssd_pair_bwd(xbc_act, dt_c, ac_c, dt_r, ac_r, hprev_all, dy, dsk_pair, a_row):
    s = xbc_act.shape[0]
    L, N, G = SSM_CHUNK, SSM_STATE, SSM_GROUPS
    nc = s // L
    rev = lambda c: nc - 1 - c

    def body(xbc_ref, dt_ref, ac_ref, dtr_ref, acr_ref, hp_ref, dy_ref, dsk_ref, arow_ref,
             dx_ref, ddt_ref, da_ref, dds_ref, dh_ref):
        @pl.when(pl.program_id(0) == 0)
        def _():
            dh_ref[...] = jnp.zeros_like(dh_ref)
            da_ref[...] = jnp.zeros_like(da_ref)
            dds_ref[...] = jnp.zeros_like(dds_ref)

        upper, up1, li, si = _pair_consts()
        lane = lax.broadcasted_iota(jnp.int32, (L, LANES), 1)
        sub = lax.broadcasted_iota(jnp.int32, (LANES, L), 0)
        lastrow = lax.broadcasted_iota(jnp.int32, (L, LANES), 0) == L - 1
        da_c = jnp.zeros((L, LANES), F32)
        da_r = jnp.zeros((LANES, L), F32)
        ddt_r = jnp.zeros((LANES, L), F32)
        for g in range(G):
            b_g = xbc_ref[:, SSM_INNER + g * N:SSM_INNER + (g + 1) * N]
            c_g = xbc_ref[:, SSM_INNER + G * N + g * N:SSM_INNER + G * N + (g + 1) * N]
            cb, cb_t = _dotT(c_g, b_g), _dotT(b_g, c_g)
            b_t, c_t = b_g.T, c_g.T
            dcb = jnp.zeros((L, L), F32)
            db_t = jnp.zeros((N, L), F32)
            dc = jnp.zeros((L, N), F32)
            for q in range(PAIRS_PER_GROUP):
                pp = g * PAIRS_PER_GROUP + q
                cols = slice(pp * LANES, (pp + 1) * LANES)
                xs, gy = xbc_ref[:, cols], dy_ref[:, cols].astype(F32)
                ht, dhn = hp_ref[0, pp], dh_ref[pp]
                acol = [jnp.broadcast_to(ac_ref[:, HP * pp + a:HP * pp + a + 1], (L, LANES)) for a in range(HP)]
                alast = [ac_ref[L - 1:L, HP * pp + a:HP * pp + a + 1] for a in range(HP)]
                ea = jnp.where(upper, jnp.exp(acol[1]), jnp.exp(acol[0]))
                el = jnp.where(up1, jnp.exp(alast[1]), jnp.exp(alast[0]))
                ge = gy * ea
                dc = dc + _dotT(ge, ht)
                dh_ref[pp] = _dot(c_t, ge) + dhn * el
                t_off = (ge * _dot(c_g, ht)).astype(BF16)
                hsum = jnp.sum(dhn * ht, axis=0, keepdims=True)
                dxs = dsk_ref[pp:pp + 1, :] * gy
                dds_ref[pp:pp + 1, :] += jnp.sum(gy * xs, axis=0, keepdims=True)
                for a in range(HP):
                    h = HP * pp + a
                    mine, mine1 = upper == (a == 1), up1 == (a == 1)
                    arow, dtrow = acr_ref[h:h + 1, :], dtr_ref[h:h + 1, :]
                    dtcol = jnp.broadcast_to(dt_ref[:, h:h + 1], (L, LANES))
                    xs_a, gy_a = jnp.where(mine, xs, 0.0), jnp.where(mine, gy, 0.0)
                    dhn_a = jnp.where(mine1, dhn, 0.0)
                    e_row = jnp.exp(alast[a] - arow)
                    w_row = dtrow * e_row
                    xd_t = _dotT(dhn_a, xs_a)
                    db_t = db_t + xd_t * w_row
                    dw = jnp.sum(b_t * xd_t, axis=0, keepdims=True)
                    de_e = dw * w_row
                    dal = (jnp.sum(jnp.where(mine1, hsum, 0.0), axis=1, keepdims=True) * jnp.exp(alast[a])
                           + jnp.sum(de_e, axis=1, keepdims=True))
                    dxs = dxs + _dot(b_g, dhn_a) * (dtcol * jnp.exp(alast[a] - acol[a]))
                    decay = jnp.exp(jnp.where(li >= si, acol[a] - arow, NEG))
                    decay_t = jnp.exp(jnp.where(si >= li, arow - acol[a], NEG))
                    m = cb * decay
                    dmdt = _dotT(gy_a, xs_a)
                    dxs = dxs + _dot(cb_t * decay_t * dtcol, gy_a)
                    dm = dmdt * dtrow
                    dcb = dcb + dm * decay
                    wb = (dm * m).astype(BF16)
                    onehot = jnp.where(lane == h, 1.0, 0.0).astype(BF16)
                    da_c = (da_c + jnp.dot(wb, onehot, preferred_element_type=F32)
                            + jnp.dot(jnp.where(mine, t_off, 0.0).astype(BF16), onehot, preferred_element_type=F32)
                            + jnp.where(lastrow & (lane == h), dal, 0.0))
                    da_r = jnp.where(sub == h, -(jnp.sum(wb.astype(F32), axis=0, keepdims=True) + de_e), da_r)
                    ddt_r = jnp.where(sub == h, dw * e_row + jnp.sum(dmdt * m, axis=0, keepdims=True), ddt_r)
                dx_ref[:, cols] = dxs.astype(dx_ref.dtype)
            dx_ref[:, SSM_INNER + g * N:SSM_INNER + (g + 1) * N] = (db_t + _dot(c_t, dcb)).T.astype(dx_ref.dtype)
            dx_ref[:, SSM_INNER + G * N + g * N:SSM_INNER + G * N + (g + 1) * N] = (
                dc + _dot(dcb, b_g)).astype(dx_ref.dtype)
        dda = jnp.dot(_tri(L, lower=False), da_c + da_r.T, precision=HI, preferred_element_type=F32)
        ddt_ref[...] = dda * arow_ref[...] + ddt_r.T
        da_ref[...] += jnp.sum(dda * dt_ref[...], axis=0, keepdims=True)

    col = pl.BlockSpec((L, LANES), lambda c: (rev(c), 0))
    row = pl.BlockSpec((LANES, L), lambda c: (0, rev(c)))
    vec = pl.BlockSpec((1, LANES), lambda c: (0, 0))
    pairs = pl.BlockSpec((N_PAIR, LANES), lambda c: (0, 0))
    return pl.pallas_call(
        body, name="ssd_bwd", grid=(nc,),
        in_specs=[pl.BlockSpec((L, SSM_CONV_DIM), lambda c: (rev(c), 0)), col, col, row, row,
                  pl.BlockSpec((1, N_PAIR, N, LANES), lambda c: (rev(c), 0, 0, 0)),
                  pl.BlockSpec((L, SSM_INNER), lambda c: (rev(c), 0)), pairs, vec],
        out_specs=[pl.BlockSpec((L, SSM_CONV_DIM), lambda c: (rev(c), 0)), col, vec, pairs],
        out_shape=[jax.ShapeDtypeStruct((s, SSM_CONV_DIM), BF16), jax.ShapeDtypeStruct((s, LANES), F32),
                   jax.ShapeDtypeStruct((1, LANES), F32), jax.ShapeDtypeStruct((N_PAIR, LANES), F32)],
        scratch_shapes=[pltpu.VMEM((N_PAIR, N, LANES), F32)],
        compiler_params=_cp("arbitrary"),
    )(xbc_act, dt_c, ac_c, dt_r, ac_r, hprev_all, dy, dsk_pair, a_row)


ROWS = 512
GW = SSM_INNER // SSM_GROUPS


def _rows(width, dtype=F32):
    return pl.BlockSpec((ROWS, width), lambda i: (i, 0))


def _vec(width):
    return pl.BlockSpec((1, width), lambda i: (0, 0))


def _gnorm_fwd(y, z, w):
    s = y.shape[0]

    def body(y_ref, z_ref, w_ref, o_ref):
        for g in range(SSM_GROUPS):
            cs = slice(g * GW, (g + 1) * GW)
            zz = z_ref[:, cs].astype(F32)
            u = y_ref[:, cs] * (zz * _sigmoid(zz))
            r = lax.rsqrt(jnp.mean(u * u, axis=1, keepdims=True) + RMS_EPS)
            o_ref[:, cs] = (u * r * w_ref[:, cs]).astype(o_ref.dtype)

    return pl.pallas_call(body, name="gnorm_fwd", grid=(s // ROWS,),
                          in_specs=[_rows(SSM_INNER), _rows(SSM_INNER), _vec(SSM_INNER)], out_specs=_rows(SSM_INNER),
                          out_shape=jax.ShapeDtypeStruct((s, SSM_INNER), BF16), compiler_params=_cp("parallel"))(y, z, w)


def _gnorm_bwd(y, z, w, do):
    s = y.shape[0]

    def body(y_ref, z_ref, w_ref, do_ref, dy_ref, dz_ref, dw_ref):
        @pl.when(pl.program_id(0) == 0)
        def _():
            dw_ref[...] = jnp.zeros_like(dw_ref)

        for g in range(SSM_GROUPS):
            cs = slice(g * GW, (g + 1) * GW)
            zz, yy, dd = z_ref[:, cs].astype(F32), y_ref[:, cs], do_ref[:, cs].astype(F32)
            sz, dsz = _silu_and_grad(zz)
            u = yy * sz
            r = lax.rsqrt(jnp.mean(u * u, axis=1, keepdims=True) + RMS_EPS)
            n = u * r
            dn = dd * w_ref[:, cs]
            dw_ref[:, cs] += jnp.sum(dd * n, axis=0, keepdims=True)
            du = r * (dn - n * jnp.mean(dn * n, axis=1, keepdims=True))
            dy_ref[:, cs] = (du * sz).astype(dy_ref.dtype)
            dz_ref[:, cs] = (du * yy * dsz).astype(dz_ref.dtype)

    return pl.pallas_call(
        body, name="gnorm_bwd", grid=(s // ROWS,),
        in_specs=[_rows(SSM_INNER), _rows(SSM_INNER), _vec(SSM_INNER), _rows(SSM_INNER)],
        out_specs=[_rows(SSM_INNER), _rows(SSM_INNER), _vec(SSM_INNER)],
        out_shape=[jax.ShapeDtypeStruct((s, SSM_INNER), BF16), jax.ShapeDtypeStruct((s, SSM_INNER), BF16),
                   jax.ShapeDtypeStruct((1, SSM_INNER), F32)],
        compiler_params=_cp("arbitrary"))(y, z, w, do)


def _mix_fwd(gl, bg, attn_d, ssm_d):
    s = gl.shape[0]
    d = D_MODEL

    def body(gl_ref, bg_ref, a_ref, m_ref, o_ref):
        g0 = _sigmoid(gl_ref[:, :d] + bg_ref[:, :d])
        g1 = _sigmoid(gl_ref[:, d:] + bg_ref[:, d:])
        o_ref[...] = (g0 * a_ref[...] + g1 * m_ref[...]).astype(o_ref.dtype)

    return pl.pallas_call(body, name="mix_fwd", grid=(s // ROWS,),
                          in_specs=[_rows(2 * d), _vec(2 * d), _rows(d), _rows(d)], out_specs=_rows(d),
                          out_shape=jax.ShapeDtypeStruct((s, d), BF16), compiler_params=_cp("parallel"))(
        gl, bg, attn_d, ssm_d)


def _mix_bwd(gl, bg, attn_d, ssm_d, dmix):
    s = gl.shape[0]
    d = D_MODEL

    def body(gl_ref, bg_ref, a_ref, m_ref, dm_ref, da_ref, ds_ref, dg_ref, db_ref):
        @pl.when(pl.program_id(0) == 0)
        def _():
            db_ref[...] = jnp.zeros_like(db_ref)

        g0 = _sigmoid(gl_ref[:, :d] + bg_ref[:, :d])
        g1 = _sigmoid(gl_ref[:, d:] + bg_ref[:, d:])
        dm = dm_ref[...].astype(F32)
        da_ref[...] = (dm * g0).astype(da_ref.dtype)
        ds_ref[...] = (dm * g1).astype(ds_ref.dtype)
        dl0 = dm * a_ref[...] * g0 * (1.0 - g0)
        dl1 = dm * m_ref[...] * g1 * (1.0 - g1)
        dg_ref[:, :d] = dl0.astype(dg_ref.dtype)
        dg_ref[:, d:] = dl1.astype(dg_ref.dtype)
        db_ref[:, :d] += jnp.sum(dl0, axis=0, keepdims=True)
        db_ref[:, d:] += jnp.sum(dl1, axis=0, keepdims=True)

    return pl.pallas_call(
        body, name="mix_bwd", grid=(s // ROWS,),
        in_specs=[_rows(2 * d), _vec(2 * d), _rows(d), _rows(d), _rows(d)],
        out_specs=[_rows(d), _rows(d), _rows(2 * d), _vec(2 * d)],
        out_shape=[jax.ShapeDtypeStruct((s, d), BF16), jax.ShapeDtypeStruct((s, d), BF16),
                   jax.ShapeDtypeStruct((s, 2 * d), BF16), jax.ShapeDtypeStruct((1, 2 * d), F32)],
        compiler_params=_cp("arbitrary"))(gl, bg, attn_d, ssm_d, dmix)


def _ln_stats(p):
    mu = jnp.mean(p, axis=1, keepdims=True)
    c = p - mu
    rstd = lax.rsqrt(jnp.mean(c * c, axis=1, keepdims=True) + LN_EPS)
    return c * rstd, rstd


def _ln_bwd(dy, xhat, rstd, g):
    dxh = dy * g
    return rstd * (dxh - jnp.mean(dxh, axis=1, keepdims=True) - xhat * jnp.mean(dxh * xhat, axis=1, keepdims=True))


def _ln1_fwd(x, mixed, g, b):
    s, d = x.shape

    def body(x_ref, m_ref, g_ref, b_ref, o_ref, ob_ref):
        xhat, _ = _ln_stats(DEEPNORM_ALPHA * x_ref[...] + m_ref[...])
        y = xhat * g_ref[...] + b_ref[...]
        o_ref[...] = y
        ob_ref[...] = y.astype(BF16)

    return pl.pallas_call(body, name="ln1_fwd", grid=(s // ROWS,), in_specs=[_rows(d), _rows(d), _vec(d), _vec(d)],
                          out_specs=[_rows(d), _rows(d)],
                          out_shape=[jax.ShapeDtypeStruct((s, d), F32), jax.ShapeDtypeStruct((s, d), BF16)],
                          compiler_params=_cp("parallel"))(x, mixed, g, b)


def _ln2_loss(x1, h, target, g, b):
    s, d = x1.shape

    def body(x_ref, h_ref, t_ref, g_ref, b_ref, dp_ref, dpb_ref, loss_ref, dg_ref, db_ref):
        @pl.when(pl.program_id(0) == 0)
        def _():
            loss_ref[...] = jnp.zeros_like(loss_ref)
            dg_ref[...] = jnp.zeros_like(dg_ref)
            db_ref[...] = jnp.zeros_like(db_ref)

        xhat, rstd = _ln_stats(DEEPNORM_ALPHA * x_ref[...] + h_ref[...])
        err = xhat * g_ref[...] + b_ref[...] - t_ref[...]
        part = 0.5 * jnp.sum(jnp.mean(err * err, axis=1, keepdims=True), axis=0, keepdims=True)
        loss_ref[...] += jnp.broadcast_to(part, loss_ref.shape)
        dy = err * (1.0 / d)
        dg_ref[...] += jnp.sum(dy * xhat, axis=0, keepdims=True)
        db_ref[...] += jnp.sum(dy, axis=0, keepdims=True)
        dp = _ln_bwd(dy, xhat, rstd, g_ref[...])
        dp_ref[...] = dp
        dpb_ref[...] = dp.astype(BF16)

    return pl.pallas_call(
        body, name="ln2_loss", grid=(s // ROWS,), in_specs=[_rows(d), _rows(d), _rows(d), _vec(d), _vec(d)],
        out_specs=[_rows(d), _rows(d), _vec(LANES), _vec(d), _vec(d)],
        out_shape=[jax.ShapeDtypeStruct((s, d), F32), jax.ShapeDtypeStruct((s, d), BF16),
                   jax.ShapeDtypeStruct((1, LANES), F32),
                   jax.ShapeDtypeStruct((1, d), F32), jax.ShapeDtypeStruct((1, d), F32)],
        compiler_params=_cp("arbitrary"))(x1, h, target, g, b)


def _ln1_bwd(x, mixed, g, dpre2, dffn):
    s, d = x.shape

    def body(x_ref, m_ref, g_ref, d2_ref, df_ref, dp_ref, dr_ref, dg_ref, db_ref):
        @pl.when(pl.program_id(0) == 0)
        def _():
            dg_ref[...] = jnp.zeros_like(dg_ref)
            db_ref[...] = jnp.zeros_like(db_ref)

        xhat, rstd = _ln_stats(DEEPNORM_ALPHA * x_ref[...] + m_ref[...])
        dy = DEEPNORM_ALPHA * d2_ref[...] + df_ref[...]
        dg_ref[...] += jnp.sum(dy * xhat, axis=0, keepdims=True)
        db_ref[...] += jnp.sum(dy, axis=0, keepdims=True)
        dp = _ln_bwd(dy, xhat, rstd, g_ref[...])
        dp_ref[...] = dp.astype(BF16)
        dr_ref[...] = DEEPNORM_ALPHA * dp

    return pl.pallas_call(
        body, name="ln1_bwd", grid=(s // ROWS,), in_specs=[_rows(d), _rows(d), _vec(d), _rows(d), _rows(d)],
        out_specs=[_rows(d), _rows(d), _vec(d), _vec(d)],
        out_shape=[jax.ShapeDtypeStruct((s, d), BF16), jax.ShapeDtypeStruct((s, d), F32),
                   jax.ShapeDtypeStruct((1, d), F32), jax.ShapeDtypeStruct((1, d), F32)],
        compiler_params=_cp("arbitrary"))(x, mixed, g, dpre2, dffn)


def _swiglu_fwd(gu):
    s = gu.shape[0]
    f = FFN_HIDDEN

    def body(g_ref, u_ref, o_ref):
        gg = g_ref[...].astype(F32)
        o_ref[...] = (gg * _sigmoid(gg) * u_ref[...].astype(F32)).astype(o_ref.dtype)

    return pl.pallas_call(
        body, name="swiglu_fwd", grid=(s // ROWS,),
        in_specs=[pl.BlockSpec((ROWS, f), lambda i: (i, 0)), pl.BlockSpec((ROWS, f), lambda i: (i, 1))],
        out_specs=_rows(f), out_shape=jax.ShapeDtypeStruct((s, f), BF16), compiler_params=_cp("parallel"))(gu, gu)


def _swiglu_bwd(gu, dact):
    s = gu.shape[0]
    f = FFN_HIDDEN

    def body(g_ref, u_ref, d_ref, o_ref):
        sg, dsg = _silu_and_grad(g_ref[...].astype(F32))
        dd = d_ref[...].astype(F32)
        o_ref[:, :f] = (dd * u_ref[...].astype(F32) * dsg).astype(o_ref.dtype)
        o_ref[:, f:] = (dd * sg).astype(o_ref.dtype)

    return pl.pallas_call(
        body, name="swiglu_bwd", grid=(s // ROWS,),
        in_specs=[pl.BlockSpec((ROWS, f), lambda i: (i, 0)), pl.BlockSpec((ROWS, f), lambda i: (i, 1)), _rows(f)],
        out_specs=_rows(2 * f), out_shape=jax.ShapeDtypeStruct((s, 2 * f), BF16),
        compiler_params=_cp("parallel"))(gu, gu, dact)


def _peer(k):
    x, y, c = lax.axis_index("x"), lax.axis_index("y"), lax.axis_index("c")
    kx, ky, kc = (k >> 2) & 1, (k >> 1) & 1, k & 1
    px = (1 - x) if kx else x
    py = (1 - y) if ky else y
    pc = (1 - c) if kc else c
    return (px, py, pc), 4 * px + 2 * py + pc


def _my_index():
    return 4 * lax.axis_index("x") + 2 * lax.axis_index("y") + lax.axis_index("c")


def _comm_copies(ins, outs, sems, scatter):
    send_sems, recv_sems, local_sems = sems
    me = _my_index()
    copies = [pltpu.make_async_copy(ins[t].at[me] if scatter else ins[t], outs[t].at[me], local_sems.at[t])
              for t in range(len(ins))]
    for k in range(1, N_DEV):
        peer, pidx = _peer(k)
        for t in range(len(ins)):
            copies.append(pltpu.make_async_remote_copy(
                src_ref=ins[t].at[pidx] if scatter else ins[t], dst_ref=outs[t].at[me],
                send_sem=send_sems.at[t, k - 1], recv_sem=recv_sems.at[t, k - 1], device_id=peer,
                device_id_type=pl.DeviceIdType.MESH))
    return copies


def _comm_sems(n):
    return [pltpu.SemaphoreType.DMA((n, N_DEV - 1)), pltpu.SemaphoreType.DMA((n, N_DEV - 1)),
            pltpu.SemaphoreType.DMA((n,))]


def _comm_out_shapes(parts, scatter):
    return [jax.ShapeDtypeStruct(p.shape if scatter else (N_DEV,) + p.shape, p.dtype) for p in parts]


def _all_gather(parts):
    n = len(parts)

    def body(*refs):
        ins, outs = refs[:n], refs[n:2 * n]
        send_sems, recv_sems, local_sems = refs[2 * n:]
        x, y, c = lax.axis_index("x"), lax.axis_index("y"), lax.axis_index("c")
        me, sibling = (x, y, c), (x, y, 1 - c)
        chips = [(1 - x, y), (x, 1 - y), (1 - x, 1 - y)]

        def copy(t, k, block, to, src=None):
            dst = outs[t].at[4 * block[0] + 2 * block[1] + block[2]]
            return pltpu.make_async_remote_copy(
                src_ref=dst if src is None else src, dst_ref=dst, send_sem=send_sems.at[t, k],
                recv_sem=recv_sems.at[t, k], device_id=to, device_id_type=pl.DeviceIdType.MESH)

        mine = [pltpu.make_async_copy(ins[t], outs[t].at[_my_index()], local_sems.at[t]) for t in range(n)]
        for cp in mine:
            cp.start()
        first = [copy(t, 0, me, sibling, src=ins[t]) for t in range(n)]
        first += [copy(t, 1 + j, me, (*chip, c), src=ins[t]) for j, chip in enumerate(chips) for t in range(n)]
        for cp in first:
            cp.start()
        passed = []
        for j, chip in enumerate(chips):
            for t in range(n):
                copy(t, 1 + j, (*chip, c), me).wait_recv()
                passed.append(copy(t, 4 + j, (*chip, c), sibling))
                passed[-1].start()
        for t in range(n):
            copy(t, 0, sibling, me).wait_recv()
            for j, chip in enumerate(chips):
                copy(t, 4 + j, (*chip, 1 - c), me).wait_recv()
        for cp in first + passed:
            cp.wait_send()
        for cp in mine:
            cp.wait()

    anyspec = pl.BlockSpec(memory_space=pl.ANY)
    return pl.pallas_call(body, name="all_gather", in_specs=[anyspec] * n, out_specs=[anyspec] * n,
                          out_shape=_comm_out_shapes(parts, False), scratch_shapes=_comm_sems(n))(*parts)


def _remote_scatter_copies(ins, lands, send_sems, recv_sems):
    me = _my_index()
    copies = []
    for k in range(1, N_DEV):
        peer, pidx = _peer(k)
        for t in range(len(ins)):
            copies.append(pltpu.make_async_remote_copy(
                src_ref=ins[t].at[pidx], dst_ref=lands[t].at[me], send_sem=send_sems.at[t * (N_DEV - 1) + k - 1],
                recv_sem=recv_sems.at[t * (N_DEV - 1) + k - 1], device_id=peer, device_id_type=pl.DeviceIdType.MESH))
    return copies


def _landing_zones(parts):
    me = _my_index()
    return [jnp.where(lax.broadcasted_iota(jnp.int32, p.shape, 0) == me, p, jnp.zeros_like(p)) for p in parts]


_HBM = pl.BlockSpec(memory_space=pltpu.HBM)
_SEM = pl.BlockSpec(memory_space=pltpu.SEMAPHORE)


def _exchange_start(parts, lands):
    n = len(parts)

    def body(*refs):
        ins, lnd, send_sems, recv_sems, token = refs[:n], refs[n:2 * n], refs[2 * n], refs[2 * n + 1], refs[-1]
        for cp in _remote_scatter_copies(ins, lnd, send_sems, recv_sems):
            cp.start()
        token[...] = jnp.zeros_like(token)

    hbm = [pltpu.HBM(p.shape, p.dtype) for p in parts]
    outs = pl.pallas_call(
        body, name="exchange_start",
        out_shape=[pltpu.SemaphoreType.DMA((n * (N_DEV - 1),)), pltpu.SemaphoreType.DMA((n * (N_DEV - 1),))] + hbm + hbm
        + [jax.ShapeDtypeStruct((8, LANES), F32)],
        in_specs=[_HBM] * (2 * n), out_specs=[_SEM, _SEM] + [_HBM] * (2 * n) + [pl.BlockSpec(memory_space=pltpu.VMEM)],
        input_output_aliases={t: 2 + t for t in range(2 * n)},
        compiler_params=pltpu.CompilerParams(has_side_effects=pltpu.SideEffectType.DATAFLOW_SIDE_EFFECTING),
    )(*[pltpu.with_memory_space_constraint(p, pltpu.HBM) for p in list(parts) + list(lands)])
    return outs[0], outs[1], list(outs[2:2 + n]), list(outs[2 + n:2 + 2 * n]), outs[-1]


def _exchange_wait(send_sems, recv_sems, parts, lands, after):
    n = len(parts)

    def body(*refs):
        ins, lnd, send_sems, recv_sems = refs[:n], refs[n:2 * n], refs[2 * n], refs[2 * n + 1]
        for cp in _remote_scatter_copies(ins, lnd, send_sems, recv_sems):
            cp.wait_send()
            cp.wait_recv()

    hbm = [pltpu.HBM(p.shape, p.dtype) for p in parts]
    outs = pl.pallas_call(
        body, name="exchange_wait", out_shape=hbm + hbm,
        in_specs=[_HBM] * (2 * n) + [_SEM, _SEM, pl.BlockSpec(memory_space=pl.ANY)], out_specs=[_HBM] * (2 * n),
        input_output_aliases={t: t for t in range(2 * n)},
        compiler_params=pltpu.CompilerParams(has_side_effects=pltpu.SideEffectType.DATAFLOW_SIDE_EFFECTING),
    )(*parts, *lands, send_sems, recv_sems, after)
    return list(outs[n:])


def _adamw(recv, w, m, v, name):
    _, r, c = w.shape
    br = _tile(r, 128)
    c1 = 1.0 / (1.0 - ADAM_B1 ** ADAM_STEP)
    c2 = 1.0 / (1.0 - ADAM_B2 ** ADAM_STEP)

    def body(r_ref, w_ref, m_ref, v_ref, g_ref, d_ref, mo_ref, vo_ref):
        g = r_ref[0].astype(F32)
        for k in range(1, N_DEV):
            g = g + r_ref[k].astype(F32)
        mn = ADAM_B1 * m_ref[0] + (1.0 - ADAM_B1) * g
        vn = ADAM_B2 * v_ref[0] + (1.0 - ADAM_B2) * (g * g)
        g_ref[0] = g
        mo_ref[0] = mn
        vo_ref[0] = vn
        d_ref[0] = -ADAM_LR * ((mn * c1) / (jnp.sqrt(vn * c2) + ADAM_EPS) + ADAM_WD * w_ref[0])

    blk = pl.BlockSpec((1, br, c), lambda i: (0, i, 0))
    return pl.pallas_call(
        body, name=name, grid=(r // br,),
        in_specs=[pl.BlockSpec((N_DEV, br, c), lambda i: (0, i, 0)), blk, blk, blk],
        out_specs=[blk] * 4, out_shape=[jax.ShapeDtypeStruct((1, r, c), F32)] * 4,
        compiler_params=_cp("parallel"))(recv, w, m, v)


def _lane_row(pairs):
    row = jnp.zeros((LANES,), F32)
    for lane0, vec in pairs:
        row = lax.dynamic_update_slice(row, vec.astype(F32), (lane0,))
    return row.reshape(1, LANES)


def _stage_in(x, wts, small):
    s = x.shape[0]
    a = -jnp.exp(small["a_log"])
    bias_row = _lane_row([(DT_LANE0, small["dt_bias"]), (F_LANE0, small["b_forget"])])
    a_row = _lane_row([(DT_LANE0, a)])
    conv_b = small["conv_b"].reshape(1, -1)
    norm_w = small["ssm_norm_w"].reshape(1, -1)
    bg = small["b_gates"].reshape(1, -1)
    g1, b1 = small["ln1_g"].reshape(1, -1), small["ln1_b"].reshape(1, -1)
    g2, b2 = small["ln2_g"].reshape(1, -1), small["ln2_b"].reshape(1, -1)
    d_skip = small["d_skip"]
    xb = x.astype(BF16)

    qkv = _mm(xb, wts["qkv"], out_dtype=BF16, name="f_qkv")
    z = _mm(xb, wts["z"], out_dtype=BF16, name="f_z")
    xbc = _mm(xb, wts["xbc"], name="f_xbc")
    gl = _mm(xb, wts["gate"], out_dtype=BF16, name="f_gate")
    fd = _mm(xb, wts["fd"], name="f_fd")
    dt_c, ac_c, cf_c, dt_r, ac_r, cf_r = _stats_fwd(fd, bias_row, a_row)
    bk = _att_blocks(s)[1]
    ck4 = cf_r[F_LANE0:F_LANE0 + ATT_HEADS].reshape(N_HP, HP, s // bk, bk)
    return dict(locals())


def _stage_mid(c, attn, lse, wts, target):
    x, xb, qkv, z, xbc, gl, fd, ck4 = (c[k] for k in ("x", "xb", "qkv", "z", "xbc", "gl", "fd", "ck4"))
    dt_c, ac_c, dt_r, ac_r, a_row, bias_row = (c[k] for k in ("dt_c", "ac_c", "dt_r", "ac_r", "a_row", "bias_row"))
    conv_b, norm_w, bg, g1, b1, g2, b2, d_skip = (c[k] for k in ("conv_b", "norm_w", "bg", "g1", "b1", "g2", "b2",
                                                                "d_skip"))
    conv_w = c["wts"]["conv"]
    attn_d = _mm(attn, wts["pa"], out_dtype=BF16, name="f_pa")
    xact = _conv_fwd(xbc, conv_w, conv_b)
    dsk_pair = jnp.repeat(d_skip, SSM_HEAD_DIM).reshape(N_PAIR, LANES)
    y, hprev = _ssd_pair_fwd(xact, ac_c, dt_r, ac_r, dsk_pair)
    ssm = _gnorm_fwd(y, z, norm_w)
    ssm_d = _mm(ssm, wts["ps"], out_dtype=BF16, name="f_ps")
    mix = _mix_fwd(gl, bg, attn_d, ssm_d)
    mixed = _mm(mix, wts["out"], name="f_out")
    x1, x1_b = _ln1_fwd(x, mixed, g1, b1)
    gu = _mm(x1_b, wts["gu"], out_dtype=BF16, name="f_gu")
    act = _swiglu_fwd(gu)
    h = _mm(act, wts["down"], name="f_down")
    dpre2, dpre2_b, loss_row, dg2, db2 = _ln2_loss(x1, h, target, g2, b2)

    d_act = _mm(dpre2_b, wts["down"], tb=True, out_dtype=BF16, name="b_down_x")
    dw_down = _mm(act, dpre2_b, ta=True, name="b_down_w")
    dgu = _swiglu_bwd(gu, d_act)
    dffn = _mm(dgu, wts["gu"], tb=True, name="b_gu_x")
    dw_gu = _mm(x1_b, dgu, ta=True, name="b_gu_w")
    dpre1, dxr, dg1, db1 = _ln1_bwd(x, mixed, g1, dpre2, dffn)
    dmix = _mm(dpre1, wts["out"], tb=True, out_dtype=BF16, name="b_out_x")
    dw_out = _mm(mix, dpre1, ta=True, name="b_out_w")
    dattn_d, dssm_d, dgl, dbg = _mix_bwd(gl, bg, attn_d, ssm_d, dmix)
    dssm = _mm(dssm_d, wts["ps"], tb=True, out_dtype=BF16, name="b_ps_x")
    dw_ps = _mm(ssm, dssm_d, ta=True, name="b_ps_w")
    dw_pa = _mm(attn, dattn_d, ta=True, name="b_pa_w")
    dy, dz, dnw = _gnorm_bwd(y, z, norm_w, dssm)
    dxact, ddt, da_row, dds_pair = _ssd_pair_bwd(xact, dt_c, ac_c, dt_r, ac_r, hprev, dy, dsk_pair, a_row)
    dds = dds_pair.reshape(SSM_HEADS, SSM_HEAD_DIM).sum(axis=1)
    dpre_c, dconv_w, dconv_b = _conv_bwd_pre(xbc, conv_w, conv_b, dxact)
    dxbc = _conv_bwd_in(dpre_c, conv_w)
    st, do_b = _att_prep(dattn_d, wts["pa"], attn, lse)
    late = dict(pa=dw_pa, ps=dw_ps, out=dw_out, gu=dw_gu, down=dw_down)
    keep = ("st", "do_b", "ddt", "dxr", "dz", "dxbc", "dgl", "dconv_w", "dconv_b", "da_row", "dds", "dnw", "dbg",
            "dg1", "db1", "dg2", "db2", "loss_row")
    loc = locals()
    return {**c, **{k: loc[k] for k in keep}}, late


def _stage_out_w(c, att_grads):
    dq, dk, dv, dck, dcq = att_grads
    xb, fd, bias_row, ddt, dz, dxbc, dgl = (c[k] for k in ("xb", "fd", "bias_row", "ddt", "dz", "dxbc", "dgl"))
    s, a = xb.shape[0], c["a"]
    dcum = dck.reshape(ATT_HEADS, s) + dcq.reshape(ATT_HEADS, s)
    dfd, dbias = _stats_bwd(fd, bias_row, ddt, jnp.zeros((LANES, s), F32).at[F_LANE0:F_LANE0 + ATT_HEADS].set(dcum))
    dproj = (dq, dk, dv, dz, dxbc, dgl, dfd)
    dw_in = [_mm(xb, g_, ta=True, tb=(i == 0), name=f"b_in_w{i}") for i, g_ in enumerate(dproj)]
    grads = dict(q=dw_in[0], k=dw_in[1], v=dw_in[2], z=dw_in[3], xbc=dw_in[4], gate=dw_in[5], fd=dw_in[6],
                 conv=c["dconv_w"])
    small_g = dict(
        b_forget=dbias[0, F_LANE0:F_LANE0 + ATT_HEADS], conv_b=c["dconv_b"][0], dt_bias=dbias[0, :SSM_HEADS],
        a_log=c["da_row"][0, :SSM_HEADS] * a, d_skip=c["dds"], ssm_norm_w=c["dnw"][0], b_gates=c["dbg"][0],
        ln1_g=c["dg1"][0], ln1_b=c["db1"][0], ln2_g=c["dg2"][0], ln2_b=c["db2"][0])
    return c["loss_row"][0, 0], grads, small_g, dproj


def _stage_out_x(c, dproj, token):
    wts = c["wts"]
    dq_t, dk, dv, dz, dxbc, dgate, dfd = dproj
    w_fd = wts["fd"] + token.astype(BF16)
    dx = _mm_sum([(dq_t, True, wts["qkv"], 0), (dk, False, wts["qkv"], 1), (dv, False, wts["qkv"], 2),
                  (dfd, False, w_fd, 0)], c["dxr"], "b_in_x_qkv_fd")
    dx = _mm_sum([(dz, False, wts["z"], 0), (dgate, False, wts["gate"], 0)], dx, "b_in_x_z_gate")
    return _mm_sum([(dxbc, False, wts["xbc"], 0)], dx, "b_in_x_xbc")


BIG = ("w_in", "w_proj_attn", "w_proj_ssm", "w_out", "w_ffn_gate", "w_ffn_up", "w_ffn_down", "conv_w")
EARLY = ("w_in", "conv_w")
LATE = ("w_proj_attn", "w_proj_ssm", "w_out", "w_ffn_gate", "w_ffn_up", "w_ffn_down")
SMALL = ("b_forget", "conv_b", "dt_bias", "a_log", "d_skip", "ssm_norm_w", "b_gates", "ln1_g", "ln1_b", "ln2_g",
         "ln2_b")
SMALL_ROWS = 96
IN_SHARD = IN_WIDTH // N_DEV
IN_SEGMENTS = (("q", 0, 1024), ("k", 1024, 1024), ("v", 2048, 1024), ("f", 3072, ATT_HEADS), ("z", 3088, SSM_INNER),
               ("xbc", 5136, SSM_CONV_DIM), ("dt", 8208, SSM_HEADS), ("gate", 8240, 2 * D_MODEL))


def _cols_from_shards(shards, lo, hi):
    w = shards[0].shape[1]
    pieces = []
    for j in range(len(shards)):
        a, b = max(lo, j * w), min(hi, (j + 1) * w)
        if a < b:
            pieces.append(shards[j][:, a - j * w:b - j * w])
    return pieces[0] if len(pieces) == 1 else jnp.concatenate(pieces, axis=1)


def _shards_from_parts(parts, width):
    shards = []
    for j in range(N_DEV):
        lo, hi = j * width, (j + 1) * width
        pieces = []
        for mat, c0 in parts:
            a, b = max(lo, c0), min(hi, c0 + mat.shape[1])
            if a < b:
                pieces.append(mat[:, a - c0:b - c0])
        shards.append(pieces[0] if len(pieces) == 1 else jnp.concatenate(pieces, axis=1))
    return shards


def _pack_small(vals):
    flat = jnp.concatenate([vals[n].reshape(-1) for n in SMALL])
    return jnp.pad(flat, (0, SMALL_ROWS * LANES - flat.shape[0])).reshape(SMALL_ROWS, LANES)


def _unpack_small(pack, shapes):
    flat = pack.reshape(-1)
    out, off = {}, 0
    for n in SMALL:
        sz = math.prod(shapes[n])
        out[n] = flat[off:off + sz].reshape(shapes[n])
        off += sz
    return out


def kernel(x, w_in, b_forget, conv_w, conv_b, dt_bias, a_log, d_skip, ssm_norm_w, w_proj_attn, w_proj_ssm, b_gates, w_out, ln1_g, ln1_b, w_ffn_gate, w_ffn_up, w_ffn_down, ln2_g, ln2_b, loss_target, m_w_in, m_b_forget, m_conv_w, m_conv_b, m_dt_bias, m_a_log, m_d_skip, m_ssm_norm_w, m_w_proj_attn, m_w_proj_ssm, m_b_gates, m_w_out, m_ln1_g, m_ln1_b, m_w_ffn_gate, m_w_ffn_up, m_w_ffn_down, m_ln2_g, m_ln2_b, v_w_in, v_b_forget, v_conv_w, v_conv_b, v_dt_bias, v_a_log, v_d_skip, v_ssm_norm_w, v_w_proj_attn, v_w_proj_ssm, v_b_gates, v_w_out, v_ln1_g, v_ln1_b, v_w_ffn_gate, v_w_ffn_up, v_w_ffn_down, v_ln2_g, v_ln2_b):
    args = dict(locals())
    d, f = D_MODEL, FFN_HIDDEN
    big_w = {n: args[n][0] for n in BIG}
    small_w = {n: args[n][0] for n in SMALL}
    big_shapes = {n: args[n].shape for n in BIG}
    small_shapes = {n: args[n].shape for n in SMALL}

    early = dict(zip(EARLY, _all_gather([big_w["w_in"].astype(BF16), big_w["conv_w"]])))
    in_shards = [early["w_in"][j] for j in range(N_DEV)]
    seg = {n: _cols_from_shards(in_shards, c0, c0 + w) for n, c0, w in IN_SEGMENTS}
    wfd = jnp.concatenate([seg["dt"], seg["f"], jnp.zeros((d, LANES - SSM_HEADS - ATT_HEADS), BF16)], axis=1)
    wts = dict(qkv=jnp.concatenate([seg["q"], seg["k"], seg["v"]], axis=1), z=seg["z"], xbc=seg["xbc"],
               gate=seg["gate"], fd=wfd, conv=jnp.concatenate([early["conv_w"][j] for j in range(N_DEV)], axis=1))

    ctx = _stage_in(x[0], wts, small_w)
    attn, lse, gathered = _attention_fwd(ctx["qkv"], ctx["ck4"], [big_w[n].astype(BF16) for n in LATE])
    full = dict(zip(LATE, gathered))
    late_w = dict(
        pa=full["w_proj_attn"].reshape(d, d), ps=full["w_proj_ssm"].reshape(SSM_INNER, d),
        out=full["w_out"].reshape(d, d),
        gu=jnp.concatenate([full["w_ffn_gate"][j] for j in range(N_DEV)]
                           + [full["w_ffn_up"][j] for j in range(N_DEV)], axis=1),
        down=full["w_ffn_down"].reshape(f, d))
    ctx, gl = _stage_mid(ctx, attn, lse, late_w, loss_target[0])
    late_dest = dict(
        w_ffn_gate=jnp.stack([s_.astype(BF16) for s_ in _shards_from_parts([(gl["gu"][:, :f], 0)], f // N_DEV)]),
        w_ffn_up=jnp.stack([s_.astype(BF16) for s_ in _shards_from_parts([(gl["gu"][:, f:], 0)], f // N_DEV)]))
    for n, key in (("w_proj_attn", "pa"), ("w_proj_ssm", "ps"), ("w_out", "out"), ("w_ffn_down", "down")):
        late_dest[n] = gl[key].astype(BF16).reshape((N_DEV,) + big_shapes[n][1:])
    att_grads, late_recv = _attention_bwd(ctx["qkv"], ctx["ck4"], ctx["st"], ctx["do_b"], [late_dest[n] for n in LATE])
    loss_part, g, small_g, dproj = _stage_out_w(ctx, att_grads)
    loss = lax.psum(loss_part, ("x", "y", "c"))

    gfd = g["fd"]
    in_parts = dict(q=g["q"], k=g["k"], v=g["v"], f=gfd[:, F_LANE0:F_LANE0 + ATT_HEADS], z=g["z"], xbc=g["xbc"],
                    dt=gfd[:, DT_LANE0:DT_LANE0 + SSM_HEADS], gate=g["gate"])
    win_dest = jnp.stack([s_.astype(BF16) for s_ in
                          _shards_from_parts([(in_parts[n], c0) for n, c0, _ in IN_SEGMENTS], IN_SHARD)])
    conv_dest = jnp.stack(_shards_from_parts([(g["conv"], 0)], SSM_CONV_DIM // N_DEV))
    small_pack = _pack_small(small_g)
    last_parts = [win_dest, conv_dest, jnp.broadcast_to(small_pack, (N_DEV,) + small_pack.shape)]
    send_sems, recv_sems, parts_thru, lands_thru, token = _exchange_start(last_parts, _landing_zones(last_parts))
    grad_x = _stage_out_x(ctx, dproj, token[0, 0])
    early_recv = _exchange_wait(send_sems, recv_sems, parts_thru, lands_thru, grad_x)
    recv = dict(zip(LATE, late_recv))
    recv["w_in"], recv["conv_w"] = early_recv[0], early_recv[1]

    outs = {}
    for n in BIG:
        outs[n] = _adamw(recv[n], args[n], args["m_" + n], args["v_" + n], name="adamw_" + n)
    small4 = _adamw(early_recv[2], _pack_small(small_w)[None], _pack_small({n: args["m_" + n][0] for n in SMALL})[None],
                    _pack_small({n: args["v_" + n][0] for n in SMALL})[None], name="adamw_small")
    small_out = [_unpack_small(p, small_shapes) for p in small4]
    for n in SMALL:
        outs[n] = [so[n] for so in small_out]

    order = ("w_in", "b_forget", "conv_w", "conv_b", "dt_bias", "a_log", "d_skip", "ssm_norm_w", "w_proj_attn",
             "w_proj_ssm", "b_gates", "w_out", "ln1_g", "ln1_b", "w_ffn_gate", "w_ffn_up", "w_ffn_down", "ln2_g",
             "ln2_b")
    res = [loss, grad_x[None]]
    for i in range(4):
        res += [outs[n][i] for n in order]
    return tuple(res)
```

```python
import functools
import math

import jax
import jax.numpy as jnp
from jax import lax
from jax.experimental import pallas as pl
from jax.experimental.pallas import tpu as pltpu

F32 = jnp.float32
BF16 = jnp.bfloat16

N_DEV = 8
D_MODEL = 1024
ATT_HEADS = 16
ATT_HEAD_DIM = 64
SSM_INNER = 2048
SSM_HEADS = 32
SSM_HEAD_DIM = 64
SSM_GROUPS = 4
SSM_HEADS_PER_GROUP = 8
SSM_STATE = 128
SSM_CONV = 4
SSM_CHUNK = 128
SSM_CONV_DIM = 3072
FFN_HIDDEN = 2816
IN_WIDTH = 10288
DEEPNORM_ALPHA = 2.0 ** 0.25
LN_EPS = 1e-5
RMS_EPS = 1e-5
ADAM_LR, ADAM_B1, ADAM_B2, ADAM_EPS, ADAM_WD, ADAM_STEP = 0.001, 0.9, 0.999, 1e-08, 0.01, 10
ATT_SCALE = 1.0 / math.sqrt(ATT_HEAD_DIM)

LANES = 128
VMEM_LIMIT = 56 * 1024 * 1024
NEG = -1e30

DT_LANE0 = 0
F_LANE0 = 32
HI = lax.Precision.HIGHEST


def _cp(*sem):
    return pltpu.CompilerParams(dimension_semantics=sem, vmem_limit_bytes=VMEM_LIMIT)


def _tile(n, cap=1408):
    for t in (3072, 2816, 2048, 1536, 1408, 1024, 512, 384, 256, 128):
        if t <= cap and n % t == 0:
            return t
    return n


MM_VMEM_BUDGET = VMEM_LIMIT - 4 * 2 ** 20


def _mm_tiles(m, n, k, a_bytes, b_bytes, out_bytes, has_add):
    tm = _tile(m)

    def need(tn, tk):
        blocks = tm * tk * a_bytes + tk * tn * b_bytes + tm * tn * (out_bytes + (4 if has_add else 0))
        casts = (tm * tk * 2 if a_bytes == 4 else 0) + (tk * tn * 2 if b_bytes == 4 else 0)
        return 2 * blocks + casts + tm * tn * 4

    tns = [t for t in (3072, 2816, 2048, 1536, 1408, 1024, 512, 384, 256, 128) if n % t == 0] or [n]
    for tk in ([k] if k <= 3072 else []) + [t for t in (2816, 1024, 512, 256, 128) if k % t == 0]:
        fits = [tn for tn in tns if need(tn, tk) <= MM_VMEM_BUDGET]
        if fits and fits[0] >= min(1024, tns[0]):
            return tm, fits[0], tk
    return tm, tns[-1], tk


def _sigmoid(x):
    return 1.0 / (1.0 + jnp.exp(-x))


def _mm(a, b, *, ta=False, tb=False, out_dtype=F32, add=None, name):
    m, k = (a.shape[1], a.shape[0]) if ta else a.shape
    n = b.shape[0] if tb else b.shape[1]
    assert (b.shape[1] if tb else b.shape[0]) == k
    tm, tn, tk = _mm_tiles(m, n, k, a.dtype.itemsize, b.dtype.itemsize, jnp.dtype(out_dtype).itemsize, add is not None)
    nk = k // tk
    dims = (((0,) if ta else (1,), (1,) if tb else (0,)), ((), ()))

    def body_single(*refs):
        a_ref, b_ref = refs[:2]
        r = lax.dot_general(a_ref[...].astype(BF16), b_ref[...].astype(BF16), dims, preferred_element_type=F32)
        if add is not None:
            r = r + refs[2][...]
        refs[-1][...] = r.astype(refs[-1].dtype)

    def body(*refs):
        if add is None:
            a_ref, b_ref, o_ref, acc_ref = refs
        else:
            a_ref, b_ref, c_ref, o_ref, acc_ref = refs
        kk = pl.program_id(2)

        @pl.when(kk == 0)
        def _():
            acc_ref[...] = jnp.zeros_like(acc_ref)

        acc_ref[...] += lax.dot_general(a_ref[...].astype(BF16), b_ref[...].astype(BF16), dims,
                                        preferred_element_type=F32)

        @pl.when(kk == nk - 1)
        def _():
            r = acc_ref[...]
            if add is not None:
                r = r + c_ref[...]
            o_ref[...] = r.astype(o_ref.dtype)

    a_spec = pl.BlockSpec((tk, tm), lambda i, j, kk: (kk, i)) if ta else pl.BlockSpec((tm, tk), lambda i, j, kk: (i, kk))
    b_spec = pl.BlockSpec((tn, tk), lambda i, j, kk: (j, kk)) if tb else pl.BlockSpec((tk, tn), lambda i, j, kk: (kk, j))
    o_spec = pl.BlockSpec((tm, tn), lambda i, j, kk: (i, j))
    in_specs, args = [a_spec, b_spec], [a, b]
    if add is not None:
        in_specs.append(o_spec)
        args.append(add)
    return pl.pallas_call(
        body_single if nk == 1 else body, name=name, grid=(m // tm, n // tn, nk), in_specs=in_specs, out_specs=o_spec,
        out_shape=jax.ShapeDtypeStruct((m, n), out_dtype),
        scratch_shapes=[] if nk == 1 else [pltpu.VMEM((tm, tn), F32)],
        compiler_params=_cp("parallel", "parallel", "arbitrary"),
    )(*args)


def _mm_sum(terms, add, name):
    m, n = add.shape
    ks = [a.shape[0] if ta else a.shape[1] for a, ta, _, _ in terms]

    def need(tm):
        blocks = sum(tm * k * a.dtype.itemsize + n * k * b.dtype.itemsize for (a, _, b, _), k in zip(terms, ks))
        casts = sum(tm * k * 2 for (a, _, _, _), k in zip(terms, ks) if a.dtype.itemsize == 4)
        return 2 * (blocks + 2 * tm * n * 4) + casts + tm * n * 4

    tm = next(t for t in (1024, 512, 256, 128) if m % t == 0 and need(t) <= MM_VMEM_BUDGET)

    def body(*refs):
        r = refs[-2][...]
        for i, (_, ta, _, _) in enumerate(terms):
            dims = (((0,) if ta else (1,), (1,)), ((), ()))
            r = r + lax.dot_general(refs[2 * i][...].astype(BF16), refs[2 * i + 1][...].astype(BF16), dims,
                                    preferred_element_type=F32)
        refs[-1][...] = r

    in_specs, args = [], []
    for (a, ta, b, jb), k in zip(terms, ks):
        in_specs += [pl.BlockSpec((k, tm), lambda i: (0, i)) if ta else pl.BlockSpec((tm, k), lambda i: (i, 0)),
                     pl.BlockSpec((n, k), lambda i, jb=jb: (0, jb))]
        args += [a, b]
    o_spec = pl.BlockSpec((tm, n), lambda i: (i, 0))
    return pl.pallas_call(body, name=name, grid=(m // tm,), in_specs=in_specs + [o_spec], out_specs=o_spec,
                          out_shape=jax.ShapeDtypeStruct((m, n), F32), compiler_params=_cp("parallel"))(*args, add)


def _tri(n, lower=True):
    r = lax.broadcasted_iota(jnp.int32, (n, n), 0)
    c = lax.broadcasted_iota(jnp.int32, (n, n), 1)
    return jnp.where((r >= c) if lower else (c >= r), 1.0, 0.0).astype(F32)


def _stats_fwd(fd, bias_row, a_row):
    s = fd.shape[0]
    blk = SSM_CHUNK

    def body(fd_ref, bias_ref, a_ref, dt_ref, ac_ref, cf_ref, dtr_ref, acr_ref, cfr_ref, carry_ref):
        @pl.when(pl.program_id(0) == 0)
        def _():
            carry_ref[...] = jnp.zeros_like(carry_ref)

        v = fd_ref[...] + bias_ref[...]
        dt = jnp.maximum(v, 0.0) + jnp.log(1.0 + jnp.exp(-jnp.abs(v)))
        lf = jnp.minimum(v, 0.0) - jnp.log(1.0 + jnp.exp(-jnp.abs(v)))
        tri = _tri(blk)
        ac = jnp.dot(tri, dt * a_ref[...], precision=HI, preferred_element_type=F32)
        cf = jnp.dot(tri, lf, precision=HI, preferred_element_type=F32) + carry_ref[0:1, :]
        carry_ref[...] = carry_ref[...] + jnp.sum(lf, axis=0, keepdims=True)
        dt_ref[...] = dt
        ac_ref[...] = ac
        cf_ref[...] = cf
        dtr_ref[...] = dt.T
        acr_ref[...] = ac.T
        cfr_ref[...] = cf.T

    col = pl.BlockSpec((blk, LANES), lambda i: (i, 0))
    row = pl.BlockSpec((LANES, blk), lambda i: (0, i))
    vec = pl.BlockSpec((1, LANES), lambda i: (0, 0))
    return pl.pallas_call(
        body, name="stats_fwd", grid=(s // blk,), in_specs=[col, vec, vec],
        out_specs=[col, col, col, row, row, row],
        out_shape=[jax.ShapeDtypeStruct((s, LANES), F32)] * 3 + [jax.ShapeDtypeStruct((LANES, s), F32)] * 3,
        scratch_shapes=[pltpu.VMEM((8, LANES), F32)],
        compiler_params=_cp("arbitrary"),
    )(fd, bias_row, a_row)


def _stats_bwd(fd, bias_row, ddt, dcum_rows):
    s = fd.shape[0]
    blk = SSM_CHUNK
    nb = s // blk

    def body(fd_ref, bias_ref, ddt_ref, dck_ref, o_ref, db_ref, carry_ref):
        @pl.when(pl.program_id(0) == 0)
        def _():
            carry_ref[...] = jnp.zeros_like(carry_ref)
            db_ref[...] = jnp.zeros_like(db_ref)

        v = fd_ref[...] + bias_ref[...]
        dcum = dck_ref[...].T
        dlf = jnp.dot(_tri(blk, lower=False), dcum, precision=HI, preferred_element_type=F32) + carry_ref[0:1, :]
        carry_ref[...] = carry_ref[...] + jnp.sum(dcum, axis=0, keepdims=True)
        lane = lax.broadcasted_iota(jnp.int32, v.shape, 1)
        g = jnp.where(lane < F_LANE0, ddt_ref[...] * _sigmoid(v), dlf * _sigmoid(-v))
        g = jnp.where(lane < F_LANE0 + ATT_HEADS, g, 0.0)
        o_ref[...] = g.astype(o_ref.dtype)
        db_ref[...] += jnp.sum(g, axis=0, keepdims=True)

    col = pl.BlockSpec((blk, LANES), lambda i: (nb - 1 - i, 0))
    row = pl.BlockSpec((LANES, blk), lambda i: (0, nb - 1 - i))
    vec = pl.BlockSpec((1, LANES), lambda i: (0, 0))
    return pl.pallas_call(
        body, name="stats_bwd", grid=(nb,), in_specs=[col, vec, col, row], out_specs=[col, vec],
        out_shape=[jax.ShapeDtypeStruct((s, LANES), BF16), jax.ShapeDtypeStruct((1, LANES), F32)],
        scratch_shapes=[pltpu.VMEM((8, LANES), F32)],
        compiler_params=_cp("arbitrary"),
    )(fd, bias_row, ddt, dcum_rows)


HP = LANES // ATT_HEAD_DIM
N_HP = ATT_HEADS // HP


def _att_blocks(s):
    return (512, 1024) if s % 1024 == 0 and s >= 4096 else (64, 128)


_QK = (((1,), (1,)), ((), ()))
_HALF = ATT_HEAD_DIM // 2
_PAD = 16


def _head_cols(a):
    return slice(a * ATT_HEAD_DIM, (a + 1) * ATT_HEAD_DIM)


def _causal(shape, off):
    r = lax.broadcasted_iota(jnp.int32, shape, 0)
    c = lax.broadcasted_iota(jnp.int32, shape, 1)
    return c <= r + off


def _attention_fwd(qkv, ck4, gather_parts):
    s = qkv.shape[0]
    bk = _att_blocks(s)[1]
    bq = bk
    nq, nk = s // bq, s // bk
    n = len(gather_parts)

    def body(q_ref, k_ref, v_ref, ck_ref, *rest):
        comm_in, (o_ref, lse_ref), comm_out, sems = rest[:n], rest[n:n + 2], rest[n + 2:2 * n + 2], rest[2 * n + 2:]
        i = pl.program_id(1)
        if n:
            @pl.when((pl.program_id(0) == 0) & (i == 0))
            def _():
                for cp in _comm_copies(comm_in, comm_out, sems, False):
                    cp.start()

        n_full = (i * bq) // bk
        qs = [(q_ref[:, _head_cols(a)].astype(F32) * ATT_SCALE).astype(BF16) for a in range(HP)]

        upper_q = lax.broadcasted_iota(jnp.int32, (bq, LANES), 1) >= ATT_HEAD_DIM

        def absorb(j, carry, keys=slice(0, bk), rows=slice(0, bq), masked=False):
            nk_ = keys.stop - keys.start
            ks = pl.ds(pl.multiple_of(j * bk, bk) + keys.start, nk_)
            v_both = v_ref[ks, :]
            out = []
            for a in range(HP):
                m, acc = carry[a]
                sc = lax.dot_general(qs[a][rows], k_ref[ks, _head_cols(a)], _QK, preferred_element_type=F32)
                sc = sc - ck_ref[0, a, pl.ds(j, 1), keys]
                if masked:
                    sc = jnp.where(_causal(sc.shape, 0), sc, NEG)
                m_new = jnp.maximum(m, jnp.max(sc, axis=1, keepdims=True))
                p = jnp.exp((sc - m_new).astype(BF16))
                upper_v = lax.broadcasted_iota(jnp.int32, (nk_, LANES), 1) >= ATT_HEAD_DIM
                v_aug = jnp.where(upper_v == (a == 1), v_both, jnp.ones_like(v_both))
                acc = jnp.exp(m - m_new) * acc + jnp.dot(p, v_aug, preferred_element_type=F32)
                out.append((m_new, acc))
            return tuple(out)

        init = tuple((jnp.full((bq, 1), NEG, F32), jnp.zeros((bq, LANES), F32)) for _ in range(HP))
        carry = lax.fori_loop(0, n_full, absorb, init)
        hq = bq // 2
        carry = absorb(n_full, carry, keys=slice(0, hq), masked=True)
        low = absorb(n_full, tuple((m[hq:], acc[hq:]) for m, acc in carry), keys=slice(hq, bk), rows=slice(hq, bq),
                     masked=True)
        carry = tuple((jnp.concatenate([m[:hq], ml], axis=0), jnp.concatenate([acc[:hq], al], axis=0))
                      for (m, acc), (ml, al) in zip(carry, low))
        outs, lses = [], []
        for a in range(HP):
            m, acc = carry[a]
            l = pltpu.roll(acc, ATT_HEAD_DIM, 1)
            outs.append(acc / l)
            lses.append(m + jnp.log(l))
        o_ref[...] = jnp.where(upper_q, outs[1], outs[0])
        lse_ref[...] = jnp.where(upper_q, lses[1], lses[0])
        if n:
            @pl.when((pl.program_id(0) == N_HP - 1) & (i == nq - 1))
            def _():
                for cp in _comm_copies(comm_in, comm_out, sems, False):
                    cp.wait()

    q_spec = pl.BlockSpec((bq, LANES), lambda h, i: (i, h))
    anyspec = pl.BlockSpec(memory_space=pl.ANY)
    res = pl.pallas_call(
        body, name="att_fwd", grid=(N_HP, nq),
        in_specs=[q_spec, pl.BlockSpec((s, LANES), lambda h, i: (0, N_HP + h)),
                  pl.BlockSpec((s, LANES), lambda h, i: (0, 2 * N_HP + h)),
                  pl.BlockSpec((1, HP, nk, bk), lambda h, i: (h, 0, 0, 0))] + [anyspec] * n,
        out_specs=[q_spec, q_spec] + [anyspec] * n,
        out_shape=[jax.ShapeDtypeStruct((s, D_MODEL), F32)] * 2 + _comm_out_shapes(gather_parts, False),
        scratch_shapes=_comm_sems(n) if n else [],
        compiler_params=_cp("arbitrary", "arbitrary"),
    )(qkv, qkv, qkv, ck4, *gather_parts)
    return res[0], res[1], list(res[2:])


def _att_prep(dattn_d, w_pa, o, lse_rep):
    s = o.shape[0]
    bs = _tile(s, 512)

    def body(g_ref, w_ref, o_ref, lse_ref, st_ref, dob_ref):
        do = lax.dot_general(g_ref[...], w_ref[...], (((1,), (1,)), ((), ())), preferred_element_type=F32)
        r = lax.broadcasted_iota(jnp.int32, (LANES, LANES), 0) // ATT_HEAD_DIM
        c = lax.broadcasted_iota(jnp.int32, (LANES, LANES), 1) // ATT_HEAD_DIM
        e = jnp.where(r == c, 1.0, 0.0).astype(F32)
        lane = lax.broadcasted_iota(jnp.int32, (bs, LANES), 1)
        for p in range(D_MODEL // LANES):
            cs = slice(p * LANES, (p + 1) * LANES)
            dd = do[:, cs]
            delta = jnp.dot(dd * o_ref[:, cs], e, precision=HI, preferred_element_type=F32)
            st_ref[:, cs] = jnp.where(lane % ATT_HEAD_DIM < _HALF, lse_ref[:, cs], delta)
            dob_ref[:, cs] = dd.astype(BF16)

    assert dattn_d.dtype == BF16 and w_pa.dtype == BF16
    spec = pl.BlockSpec((bs, D_MODEL), lambda i: (i, 0))
    w_spec = pl.BlockSpec(w_pa.shape, lambda i: (0, 0))
    return pl.pallas_call(body, name="att_prep", grid=(s // bs,), in_specs=[spec, w_spec, spec, spec],
                          out_specs=[spec, spec],
                          out_shape=[jax.ShapeDtypeStruct((s, D_MODEL), F32), jax.ShapeDtypeStruct((s, D_MODEL), BF16)],
                          compiler_params=_cp("parallel"))(dattn_d, w_pa, o, lse_rep)


def _attention_bwd(qkv, ck4, st, do_b, exchange_parts):
    s = qkv.shape[0]
    bq, bk = _att_blocks(s)
    nq, nk, per = s // bq, s // bk, bk // bq
    _T = (((0,), (0,)), ((), ()))
    n = len(exchange_parts)

    def body(q_ref, k_ref, v_ref, ck_ref, st_ref, do_ref, *rest):
        comm_in, (dq_ref, dk_ref, dv_ref, dck_ref, dcq_ref) = rest[:n], rest[n:n + 5]
        comm_out, sems, (dk_acc, dv_acc) = rest[n + 5:2 * n + 5], rest[2 * n + 5:-2], rest[-2:]
        j = pl.program_id(1)
        if n:
            @pl.when((pl.program_id(0) == 0) & (j == 0))
            def _():
                for cp in _comm_copies(comm_in, comm_out, sems, True):
                    cp.start()

        @pl.when(j == 0)
        def _():
            dq_ref[...] = jnp.zeros_like(dq_ref)
            dcq_ref[...] = jnp.zeros_like(dcq_ref)

        dk_acc[...] = jnp.zeros_like(dk_acc)
        dv_acc[...] = jnp.zeros_like(dv_acc)

        ones_q, ones_k = jnp.ones((_PAD, bq), BF16), jnp.ones((_PAD, bk), BF16)
        k_t = [jnp.concatenate([k_ref[:, _head_cols(a)].T, ones_k], axis=0) for a in range(HP)]

        def step(i, off=None, kl=slice(0, bk)):
            rows = pl.ds(pl.multiple_of(i * bq, bq), bq)
            for a in range(HP):
                cs = _head_cols(a)
                q = (q_ref[rows, cs].astype(F32) * ATT_SCALE).astype(BF16)
                do_a = do_ref[rows, cs]
                sc = lax.dot_general(q, k_ref[kl, cs], _QK, preferred_element_type=F32) - ck_ref[0, a, pl.ds(j, 1), kl]
                if off is not None:
                    sc = jnp.where(_causal(sc.shape, off), sc, NEG)
                p = jnp.exp(sc - st_ref[rows, a * ATT_HEAD_DIM:a * ATT_HEAD_DIM + 1])
                dp = lax.dot_general(do_a, v_ref[kl, cs], _QK, preferred_element_type=F32)
                ds = p * (dp - st_ref[rows, a * ATT_HEAD_DIM + _HALF:a * ATT_HEAD_DIM + _HALF + 1])
                ds_b = ds.astype(BF16)
                dv_acc[a, :, kl] += jnp.dot(do_a.T, p.astype(BF16), preferred_element_type=F32)
                dk_acc[a, :, kl] += jnp.dot(jnp.concatenate([q.T, ones_q], axis=0), ds_b, preferred_element_type=F32)
                dqs = lax.dot_general(k_t[a][:, kl], ds_b, _QK, preferred_element_type=F32)
                dq_ref[cs, rows] += dqs[:ATT_HEAD_DIM] * ATT_SCALE
                dcq_ref[0, a, pl.ds(i, 1), :] += jnp.sum(dqs[ATT_HEAD_DIM:ATT_HEAD_DIM + 8], axis=0,
                                                         keepdims=True) * 0.125

        for t in range(per):
            step(j * per + t, off=t * bq, kl=slice(0, (t + 1) * bq))

        def full(i, c):
            step(i)
            return c

        lax.fori_loop((j + 1) * per, nq, full, 0)
        for a in range(HP):
            dk_ref[:, _head_cols(a)] = dk_acc[a, :ATT_HEAD_DIM].T.astype(dk_ref.dtype)
            dv_ref[:, _head_cols(a)] = dv_acc[a].T.astype(dv_ref.dtype)
            dck_ref[0, a, pl.ds(j, 1), :] = -dk_acc[a, ATT_HEAD_DIM:ATT_HEAD_DIM + 1]
        if n:
            @pl.when((pl.program_id(0) == N_HP - 1) & (j == nk - 1))
            def _():
                for cp in _comm_copies(comm_in, comm_out, sems, True):
                    cp.wait()

    res = pl.BlockSpec((s, LANES), lambda h, j: (0, h))
    ck_spec = pl.BlockSpec((1, HP, nk, bk), lambda h, j: (h, 0, 0, 0))
    kout = pl.BlockSpec((bk, LANES), lambda h, j: (j, h))
    anyspec = pl.BlockSpec(memory_space=pl.ANY)
    outs = pl.pallas_call(
        body, name="att_bwd", grid=(N_HP, nk),
        in_specs=[res, pl.BlockSpec((bk, LANES), lambda h, j: (j, N_HP + h)),
                  pl.BlockSpec((bk, LANES), lambda h, j: (j, 2 * N_HP + h)), ck_spec, res, res] + [anyspec] * n,
        out_specs=[pl.BlockSpec((LANES, s), lambda h, j: (h, 0)), kout, kout, ck_spec,
                   pl.BlockSpec((1, HP, nq, bq), lambda h, j: (h, 0, 0, 0))] + [anyspec] * n,
        out_shape=[jax.ShapeDtypeStruct((D_MODEL, s), F32), jax.ShapeDtypeStruct((s, D_MODEL), BF16),
                   jax.ShapeDtypeStruct((s, D_MODEL), BF16), jax.ShapeDtypeStruct((N_HP, HP, nk, bk), F32),
                   jax.ShapeDtypeStruct((N_HP, HP, nq, bq), F32)] + _comm_out_shapes(exchange_parts, True),
        scratch_shapes=(_comm_sems(n) if n else [])
        + [pltpu.VMEM((HP, ATT_HEAD_DIM + _PAD, bk), F32), pltpu.VMEM((HP, ATT_HEAD_DIM, bk), F32)],
        compiler_params=_cp("arbitrary", "arbitrary"),
    )(qkv, qkv, qkv, ck4, st, do_b, *exchange_parts)
    return outs[:5], list(outs[5:])


def _silu_and_grad(x):
    sg = _sigmoid(x)
    return x * sg, sg * (1.0 + x * (1.0 - sg))


SUBLANES = 8


def _conv_taps(cur, before, w_rows, bias):
    n, c = cur.shape
    cur3 = cur.reshape(n // SUBLANES, SUBLANES, c)
    sub = lax.broadcasted_iota(jnp.int32, (1, SUBLANES, c), 1)
    taps = []
    for k in range(SSM_CONV):
        sh = SSM_CONV - 1 - k
        if sh == 0:
            taps.append(cur3)
            continue
        rot = pltpu.roll(cur3, sh, 1)
        prev = jnp.concatenate([pltpu.roll(before, sh, 0)[None], rot[:-1]], axis=0)
        taps.append(jnp.where(sub < sh, prev, rot))
    pre = bias[None] + sum(w_rows[k][None] * taps[k] for k in range(SSM_CONV))
    return pre.reshape(n, c), [t.reshape(n, c) for t in taps]


def _conv_col_chunks(w_ref, b_ref, bc, cc):
    for c0 in range(0, bc, cc):
        cols = slice(c0, c0 + cc)
        yield cols, [w_ref[k:k + 1, cols] for k in range(SSM_CONV)], None if b_ref is None else b_ref[:, cols]


def _conv_specs(s, bs, bc):
    cur = pl.BlockSpec((bs, bc), lambda j, i: (i, j))
    halo = pl.BlockSpec((8, bc), lambda j, i: (jnp.maximum(i * (bs // 8) - 1, 0), j))
    w = pl.BlockSpec((SSM_CONV, bc), lambda j, i: (0, j))
    b = pl.BlockSpec((1, bc), lambda j, i: (0, j))
    return cur, halo, w, b


def _conv_fwd(xbc, w, b):
    s, c = xbc.shape
    bs, bc = _tile(s, 512), 1024
    rc, cc = bs, 128

    def body(x_ref, h_ref, w_ref, b_ref, o_ref):
        first = pl.program_id(1) == 0
        for cols, w_rows, bias in _conv_col_chunks(w_ref, b_ref, bc, cc):
            def step(r, before):
                rows = pl.ds(pl.multiple_of(r * rc, rc), rc)
                cur = x_ref[rows, cols]
                pre, _ = _conv_taps(cur, before, w_rows, bias)
                o_ref[rows, cols] = pre * _sigmoid(pre)
                return cur[rc - SUBLANES:]

            lax.fori_loop(0, bs // rc, step, jnp.where(first, 0.0, h_ref[:, cols]))

    cur, halo, ws, bsp = _conv_specs(s, bs, bc)
    return pl.pallas_call(body, name="conv_fwd", grid=(c // bc, s // bs), in_specs=[cur, halo, ws, bsp],
                          out_specs=cur, out_shape=jax.ShapeDtypeStruct((s, c), F32),
                          compiler_params=_cp("parallel", "parallel"))(xbc, xbc, w, b)


def _conv_bwd_pre(xbc, w, b, dact):
    s, c = xbc.shape
    bs, bc = _tile(s, 512), 1024
    rc, cc = _tile(bs, 256), 128

    def body(x_ref, h_ref, w_ref, b_ref, g_ref, dp_ref, dw_ref, db_ref):
        first = pl.program_id(1) == 0

        @pl.when(first)
        def _():
            dw_ref[...] = jnp.zeros_like(dw_ref)
            db_ref[...] = jnp.zeros_like(db_ref)

        for cols, w_rows, bias in _conv_col_chunks(w_ref, b_ref, bc, cc):
            def step(r, carry):
                before, sums = carry
                rows = pl.ds(pl.multiple_of(r * rc, rc), rc)
                cur = x_ref[rows, cols]
                pre, taps = _conv_taps(cur, before, w_rows, bias)
                dpre = g_ref[rows, cols].astype(F32) * _silu_and_grad(pre)[1]
                dp_ref[rows, cols] = dpre.astype(dp_ref.dtype)
                terms = [dpre * t for t in taps] + [dpre]
                sums = tuple(a + jnp.sum(t.reshape(rc // SUBLANES, SUBLANES, cc), axis=0) for a, t in zip(sums, terms))
                return cur[rc - SUBLANES:], sums

            zero = jnp.zeros((SUBLANES, cc), F32)
            _, sums = lax.fori_loop(0, bs // rc, step,
                                    (jnp.where(first, 0.0, h_ref[:, cols]), (zero,) * (SSM_CONV + 1)))
            for k in range(SSM_CONV):
                dw_ref[k:k + 1, cols] += jnp.sum(sums[k], axis=0, keepdims=True)
            db_ref[:, cols] += jnp.sum(sums[SSM_CONV], axis=0, keepdims=True)

    cur, halo, ws, bsp = _conv_specs(s, bs, bc)
    return pl.pallas_call(
        body, name="conv_bwd_pre", grid=(c // bc, s // bs), in_specs=[cur, halo, ws, bsp, cur],
        out_specs=[cur, ws, bsp],
        out_shape=[jax.ShapeDtypeStruct((s, c), BF16), jax.ShapeDtypeStruct((SSM_CONV, c), F32),
                   jax.ShapeDtypeStruct((1, c), F32)],
        compiler_params=_cp("parallel", "arbitrary"))(xbc, xbc, w, b, dact)


def _conv_bwd_in(dpre, w):
    s, c = dpre.shape
    bs, bc = _tile(s, 512), 1024
    nb = s // bs
    rc, cc = bs, 256
    nr = bs // rc

    def body(g_ref, n_ref, w_ref, o_ref):
        last = pl.program_id(1) == nb - 1
        sub = lax.broadcasted_iota(jnp.int32, (1, SUBLANES, cc), 1)
        for cols, w_rows, _ in _conv_col_chunks(w_ref, None, bc, cc):
            def step(i, after):
                rows = pl.ds(pl.multiple_of((nr - 1 - i) * rc, rc), rc)
                cur = g_ref[rows, cols].astype(F32)
                cur3 = cur.reshape(rc // SUBLANES, SUBLANES, cc)
                acc = w_rows[SSM_CONV - 1][None] * cur3
                for sh in range(1, SSM_CONV):
                    rot = pltpu.roll(cur3, SUBLANES - sh, 1)
                    nxt = jnp.concatenate([rot[1:], pltpu.roll(after, SUBLANES - sh, 0)[None]], axis=0)
                    acc = acc + w_rows[SSM_CONV - 1 - sh][None] * jnp.where(sub >= SUBLANES - sh, nxt, rot)
                o_ref[rows, cols] = acc.reshape(rc, cc).astype(o_ref.dtype)
                return cur[0:SUBLANES]

            lax.fori_loop(0, nr, step, jnp.where(last, 0.0, n_ref[0:SUBLANES, cols].astype(F32)))

    cur = pl.BlockSpec((bs, bc), lambda j, i: (i, j))
    nxt = pl.BlockSpec((16, bc), lambda j, i: (jnp.minimum((i + 1) * (bs // 16), s // 16 - 1), j))
    ws = pl.BlockSpec((SSM_CONV, bc), lambda j, i: (0, j))
    return pl.pallas_call(body, name="conv_bwd_in", grid=(c // bc, nb), in_specs=[cur, nxt, ws], out_specs=cur,
                          out_shape=jax.ShapeDtypeStruct((s, c), BF16),
                          compiler_params=_cp("parallel", "parallel"))(dpre, dpre, w)


def _dotT(a, b):
    return lax.dot_general(a.astype(BF16), b.astype(BF16), (((1,), (1,)), ((), ())), preferred_element_type=F32)


def _dot(a, b):
    return jnp.dot(a.astype(BF16), b.astype(BF16), preferred_element_type=F32)


N_PAIR = SSM_HEADS // HP
PAIRS_PER_GROUP = SSM_HEADS_PER_GROUP // HP


def _pair_consts():
    L = SSM_CHUNK
    lane = lax.broadcasted_iota(jnp.int32, (L, LANES), 1)
    lane1 = lax.broadcasted_iota(jnp.int32, (1, LANES), 1)
    li = lax.broadcasted_iota(jnp.int32, (L, L), 0)
    si = lax.broadcasted_iota(jnp.int32, (L, L), 1)
    return lane >= ATT_HEAD_DIM, lane1 >= ATT_HEAD_DIM, li, si


def _ssd_pair_fwd(xbc_act, ac_c, dt_r, ac_r, dsk_pair):
    s = xbc_act.shape[0]
    L, N, G = SSM_CHUNK, SSM_STATE, SSM_GROUPS
    nc = s // L

    def body(xbc_ref, ac_ref, dtr_ref, acr_ref, dsk_ref, y_ref, hp_ref, st_ref):
        @pl.when(pl.program_id(0) == 0)
        def _():
            st_ref[...] = jnp.zeros_like(st_ref)

        upper, up1, li, si = _pair_consts()
        for g in range(G):
            b_g = xbc_ref[:, SSM_INNER + g * N:SSM_INNER + (g + 1) * N]
            c_g = xbc_ref[:, SSM_INNER + G * N + g * N:SSM_INNER + G * N + (g + 1) * N]
            cb = _dotT(c_g, b_g)
            b_t = b_g.T
            for q in range(PAIRS_PER_GROUP):
                pp = g * PAIRS_PER_GROUP + q
                cols = slice(pp * LANES, (pp + 1) * LANES)
                xs = xbc_ref[:, cols]
                ht = st_ref[pp]
                hp_ref[0, pp] = ht
                y = dsk_ref[pp:pp + 1, :] * xs
                s_new = jnp.zeros((N, LANES), F32)
                ea, el = [], []
                for a in range(HP):
                    h = HP * pp + a
                    acol = jnp.broadcast_to(ac_ref[:, h:h + 1], (L, LANES))
                    arow, dtrow = acr_ref[h:h + 1, :], dtr_ref[h:h + 1, :]
                    alast = ac_ref[L - 1:L, h:h + 1]
                    decay = jnp.exp(jnp.where(li >= si, acol - arow, NEG))
                    xs_a = jnp.where(upper == (a == 1), xs, 0.0)
                    y = y + _dot(cb * decay * dtrow, xs_a)
                    s_new = s_new + _dot(b_t * (dtrow * jnp.exp(alast - arow)), xs_a)
                    ea.append(jnp.exp(acol))
                    el.append(jnp.exp(alast))
                y_ref[:, cols] = y + jnp.where(upper, ea[1], ea[0]) * _dot(c_g, ht)
                st_ref[pp] = ht * jnp.where(up1, el[1], el[0]) + s_new

    col = pl.BlockSpec((L, LANES), lambda c: (c, 0))
    row = pl.BlockSpec((LANES, L), lambda c: (0, c))
    return pl.pallas_call(
        body, name="ssd_fwd", grid=(nc,),
        in_specs=[pl.BlockSpec((L, SSM_CONV_DIM), lambda c: (c, 0)), col, row, row,
                  pl.BlockSpec((N_PAIR, LANES), lambda c: (0, 0))],
        out_specs=[pl.BlockSpec((L, SSM_INNER), lambda c: (c, 0)),
                   pl.BlockSpec((1, N_PAIR, N, LANES), lambda c: (c, 0, 0, 0))],
        out_shape=[jax.ShapeDtypeStruct((s, SSM_INNER), F32), jax.ShapeDtypeStruct((nc, N_PAIR, N, LANES), F32)],
        scratch_shapes=[pltpu.VMEM((N_PAIR, N, LANES), F32)],
        compiler_params=_cp("arbitrary"),
    )(xbc_act, ac_c, dt_r, ac_r, dsk_pair)


def _ssd_pair_bwd(xbc_act, dt_c, ac_c, dt_r, ac_r, hprev_all, dy, dsk_pair, a_row):
    s = xbc_act.shape[0]
    L, N, G = SSM_CHUNK, SSM_STATE, SSM_GROUPS
    nc = s // L
    rev = lambda c: nc - 1 - c

    def body(xbc_ref, dt_ref, ac_ref, dtr_ref, acr_ref, hp_ref, dy_ref, dsk_ref, arow_ref,
             dx_ref, ddt_ref, da_ref, dds_ref, dh_ref):
        @pl.when(pl.program_id(0) == 0)
        def _():
            dh_ref[...] = jnp.zeros_like(dh_ref)
            da_ref[...] = jnp.zeros_like(da_ref)
            dds_ref[...] = jnp.zeros_like(dds_ref)

        upper, up1, li, si = _pair_consts()
        lane = lax.broadcasted_iota(jnp.int32, (L, LANES), 1)
        sub = lax.broadcasted_iota(jnp.int32, (LANES, L), 0)
        lastrow = lax.broadcasted_iota(jnp.int32, (L, LANES), 0) == L - 1
        da_c = jnp.zeros((L, LANES), F32)
        da_r = jnp.zeros((LANES, L), F32)
        ddt_r = jnp.zeros((LANES, L), F32)
        for g in range(G):
            b_g = xbc_ref[:, SSM_INNER + g * N:SSM_INNER + (g + 1) * N]
            c_g = xbc_ref[:, SSM_INNER + G * N + g * N:SSM_INNER + G * N + (g + 1) * N]
            cb, cb_t = _dotT(c_g, b_g), _dotT(b_g, c_g)
            b_t, c_t = b_g.T, c_g.T
            dcb = jnp.zeros((L, L), F32)
            db_t = jnp.zeros((N, L), F32)
            dc = jnp.zeros((L, N), F32)
            for q in range(PAIRS_PER_GROUP):
                pp = g * PAIRS_PER_GROUP + q
                cols = slice(pp * LANES, (pp + 1) * LANES)
                xs, gy = xbc_ref[:, cols], dy_ref[:, cols].astype(F32)
                ht, dhn = hp_ref[0, pp], dh_ref[pp]
                acol = [jnp.broadcast_to(ac_ref[:, HP * pp + a:HP * pp + a + 1], (L, LANES)) for a in range(HP)]
                alast = [ac_ref[L - 1:L, HP * pp + a:HP * pp + a + 1] for a in range(HP)]
                ea = jnp.where(upper, jnp.exp(acol[1]), jnp.exp(acol[0]))
                el = jnp.where(up1, jnp.exp(alast[1]), jnp.exp(alast[0]))
                ge = gy * ea
                dc = dc + _dotT(ge, ht)
                dh_ref[pp] = _dot(c_t, ge) + dhn * el
                t_off = (ge * _dot(c_g, ht)).astype(BF16)
                hsum = jnp.sum(dhn * ht, axis=0, keepdims=True)
                dxs = dsk_ref[pp:pp + 1, :] * gy
                dds_ref[pp:pp + 1, :] += jnp.sum(gy * xs, axis=0, keepdims=True)
                for a in range(HP):
                    h = HP * pp + a
                    mine, mine1 = upper == (a == 1), up1 == (a == 1)
                    arow, dtrow = acr_ref[h:h + 1, :], dtr_ref[h:h + 1, :]
                    dtcol = jnp.broadcast_to(dt_ref[:, h:h + 1], (L, LANES))
                    xs_a, gy_a = jnp.where(mine, xs, 0.0), jnp.where(mine, gy, 0.0)
                    dhn_a = jnp.where(mine1, dhn, 0.0)
                    e_row = jnp.exp(alast[a] - arow)
                    w_row = dtrow * e_row
                    xd_t = _dotT(dhn_a, xs_a)
                    db_t = db_t + xd_t * w_row
                    dw = jnp.sum(b_t * xd_t, axis=0, keepdims=True)
                    de_e = dw * w_row
                    dal = (jnp.sum(jnp.where(mine1, hsum, 0.0), axis=1, keepdims=True) * jnp.exp(alast[a])
                           + jnp.sum(de_e, axis=1, keepdims=True))
                    dxs = dxs + _dot(b_g, dhn_a) * (dtcol * jnp.exp(alast[a] - acol[a]))
                    decay = jnp.exp(jnp.where(li >= si, acol[a] - arow, NEG))
                    decay_t = jnp.exp(jnp.where(si >= li, arow - acol[a], NEG))
                    m = cb * decay
                    dmdt = _dotT(gy_a, xs_a)
                    dxs = dxs + _dot(cb_t * decay_t * dtcol, gy_a)
                    dm = dmdt * dtrow
                    dcb = dcb + dm * decay
                    wb = (dm * m).astype(BF16)
                    onehot = jnp.where(lane == h, 1.0, 0.0).astype(BF16)
                    da_c = (da_c + jnp.dot(wb, onehot, preferred_element_type=F32)
                            + jnp.dot(jnp.where(mine, t_off, 0.0).astype(BF16), onehot, preferred_element_type=F32)
                            + jnp.where(lastrow & (lane == h), dal, 0.0))
                    da_r = jnp.where(sub == h, -(jnp.sum(wb.astype(F32), axis=0, keepdims=True) + de_e), da_r)
                    ddt_r = jnp.where(sub == h, dw * e_row + jnp.sum(dmdt * m, axis=0, keepdims=True), ddt_r)
                dx_ref[:, cols] = dxs.astype(dx_ref.dtype)
            dx_ref[:, SSM_INNER + g * N:SSM_INNER + (g + 1) * N] = (db_t + _dot(c_t, dcb)).T.astype(dx_ref.dtype)
            dx_ref[:, SSM_INNER + G * N + g * N:SSM_INNER + G * N + (g + 1) * N] = (
                dc + _dot(dcb, b_g)).astype(dx_ref.dtype)
        dda = jnp.dot(_tri(L, lower=False), da_c + da_r.T, precision=HI, preferred_element_type=F32)
        ddt_ref[...] = dda * arow_ref[...] + ddt_r.T
        da_ref[...] += jnp.sum(dda * dt_ref[...], axis=0, keepdims=True)

    col = pl.BlockSpec((L, LANES), lambda c: (rev(c), 0))
    row = pl.BlockSpec((LANES, L), lambda c: (0, rev(c)))
    vec = pl.BlockSpec((1, LANES), lambda c: (0, 0))
    pairs = pl.BlockSpec((N_PAIR, LANES), lambda c: (0, 0))
    return pl.pallas_call(
        body, name="ssd_bwd", grid=(nc,),
        in_specs=[pl.BlockSpec((L, SSM_CONV_DIM), lambda c: (rev(c), 0)), col, col, row, row,
                  pl.BlockSpec((1, N_PAIR, N, LANES), lambda c: (rev(c), 0, 0, 0)),
                  pl.BlockSpec((L, SSM_INNER), lambda c: (rev(c), 0)), pairs, vec],
        out_specs=[pl.BlockSpec((L, SSM_CONV_DIM), lambda c: (rev(c), 0)), col, vec, pairs],
        out_shape=[jax.ShapeDtypeStruct((s, SSM_CONV_DIM), BF16), jax.ShapeDtypeStruct((s, LANES), F32),
                   jax.ShapeDtypeStruct((1, LANES), F32), jax.ShapeDtypeStruct((N_PAIR, LANES), F32)],
        scratch_shapes=[pltpu.VMEM((N_PAIR, N, LANES), F32)],
        compiler_params=_cp("arbitrary"),
    )(xbc_act, dt_c, ac_c, dt_r, ac_r, hprev_all, dy, dsk_pair, a_row)


ROWS = 512
GW = SSM_INNER // SSM_GROUPS


def _rows(width, dtype=F32):
    return pl.BlockSpec((ROWS, width), lambda i: (i, 0))


def _vec(width):
    return pl.BlockSpec((1, width), lambda i: (0, 0))


def _gnorm_fwd(y, z, w):
    s = y.shape[0]

    def body(y_ref, z_ref, w_ref, o_ref):
        for g in range(SSM_GROUPS):
            cs = slice(g * GW, (g + 1) * GW)
            zz = z_ref[:, cs].astype(F32)
            u = y_ref[:, cs] * (zz * _sigmoid(zz))
            r = lax.rsqrt(jnp.mean(u * u, axis=1, keepdims=True) + RMS_EPS)
            o_ref[:, cs] = (u * r * w_ref[:, cs]).astype(o_ref.dtype)

    return pl.pallas_call(body, name="gnorm_fwd", grid=(s // ROWS,),
                          in_specs=[_rows(SSM_INNER), _rows(SSM_INNER), _vec(SSM_INNER)], out_specs=_rows(SSM_INNER),
                          out_shape=jax.ShapeDtypeStruct((s, SSM_INNER), BF16), compiler_params=_cp("parallel"))(y, z, w)


def _gnorm_bwd(y, z, w, do):
    s = y.shape[0]

    def body(y_ref, z_ref, w_ref, do_ref, dy_ref, dz_ref, dw_ref):
        @pl.when(pl.program_id(0) == 0)
        def _():
            dw_ref[...] = jnp.zeros_like(dw_ref)

        for g in range(SSM_GROUPS):
            cs = slice(g * GW, (g + 1) * GW)
            zz, yy, dd = z_ref[:, cs].astype(F32), y_ref[:, cs], do_ref[:, cs].astype(F32)
            sz, dsz = _silu_and_grad(zz)
            u = yy * sz
            r = lax.rsqrt(jnp.mean(u * u, axis=1, keepdims=True) + RMS_EPS)
            n = u * r
            dn = dd * w_ref[:, cs]
            dw_ref[:, cs] += jnp.sum(dd * n, axis=0, keepdims=True)
            du = r * (dn - n * jnp.mean(dn * n, axis=1, keepdims=True))
            dy_ref[:, cs] = (du * sz).astype(dy_ref.dtype)
            dz_ref[:, cs] = (du * yy * dsz).astype(dz_ref.dtype)

    return pl.pallas_call(
        body, name="gnorm_bwd", grid=(s // ROWS,),
        in_specs=[_rows(SSM_INNER), _rows(SSM_INNER), _vec(SSM_INNER), _rows(SSM_INNER)],
        out_specs=[_rows(SSM_INNER), _rows(SSM_INNER), _vec(SSM_INNER)],
        out_shape=[jax.ShapeDtypeStruct((s, SSM_INNER), BF16), jax.ShapeDtypeStruct((s, SSM_INNER), BF16),
                   jax.ShapeDtypeStruct((1, SSM_INNER), F32)],
        compiler_params=_cp("arbitrary"))(y, z, w, do)


def _mix_fwd(gl, bg, attn_d, ssm_d):
    s = gl.shape[0]
    d = D_MODEL

    def body(gl_ref, bg_ref, a_ref, m_ref, o_ref):
        g0 = _sigmoid(gl_ref[:, :d] + bg_ref[:, :d])
        g1 = _sigmoid(gl_ref[:, d:] + bg_ref[:, d:])
        o_ref[...] = (g0 * a_ref[...] + g1 * m_ref[...]).astype(o_ref.dtype)

    return pl.pallas_call(body, name="mix_fwd", grid=(s // ROWS,),
                          in_specs=[_rows(2 * d), _vec(2 * d), _rows(d), _rows(d)], out_specs=_rows(d),
                          out_shape=jax.ShapeDtypeStruct((s, d), BF16), compiler_params=_cp("parallel"))(
        gl, bg, attn_d, ssm_d)


def _mix_bwd(gl, bg, attn_d, ssm_d, dmix):
    s = gl.shape[0]
    d = D_MODEL

    def body(gl_ref, bg_ref, a_ref, m_ref, dm_ref, da_ref, ds_ref, dg_ref, db_ref):
        @pl.when(pl.program_id(0) == 0)
        def _():
            db_ref[...] = jnp.zeros_like(db_ref)

        g0 = _sigmoid(gl_ref[:, :d] + bg_ref[:, :d])
        g1 = _sigmoid(gl_ref[:, d:] + bg_ref[:, d:])
        dm = dm_ref[...].astype(F32)
        da_ref[...] = (dm * g0).astype(da_ref.dtype)
        ds_ref[...] = (dm * g1).astype(ds_ref.dtype)
        dl0 = dm * a_ref[...] * g0 * (1.0 - g0)
        dl1 = dm * m_ref[...] * g1 * (1.0 - g1)
        dg_ref[:, :d] = dl0.astype(dg_ref.dtype)
        dg_ref[:, d:] = dl1.astype(dg_ref.dtype)
        db_ref[:, :d] += jnp.sum(dl0, axis=0, keepdims=True)
        db_ref[:, d:] += jnp.sum(dl1, axis=0, keepdims=True)

    return pl.pallas_call(
        body, name="mix_bwd", grid=(s // ROWS,),
        in_specs=[_rows(2 * d), _vec(2 * d), _rows(d), _rows(d), _rows(d)],
        out_specs=[_rows(d), _rows(d), _rows(2 * d), _vec(2 * d)],
        out_shape=[jax.ShapeDtypeStruct((s, d), BF16), jax.ShapeDtypeStruct((s, d), BF16),
                   jax.ShapeDtypeStruct((s, 2 * d), BF16), jax.ShapeDtypeStruct((1, 2 * d), F32)],
        compiler_params=_cp("arbitrary"))(gl, bg, attn_d, ssm_d, dmix)


def _ln_stats(p):
    mu = jnp.mean(p, axis=1, keepdims=True)
    c = p - mu
    rstd = lax.rsqrt(jnp.mean(c * c, axis=1, keepdims=True) + LN_EPS)
    return c * rstd, rstd


def _ln_bwd(dy, xhat, rstd, g):
    dxh = dy * g
    return rstd * (dxh - jnp.mean(dxh, axis=1, keepdims=True) - xhat * jnp.mean(dxh * xhat, axis=1, keepdims=True))


def _ln1_fwd(x, mixed, g, b):
    s, d = x.shape

    def body(x_ref, m_ref, g_ref, b_ref, o_ref, ob_ref):
        xhat, _ = _ln_stats(DEEPNORM_ALPHA * x_ref[...] + m_ref[...])
        y = xhat * g_ref[...] + b_ref[...]
        o_ref[...] = y
        ob_ref[...] = y.astype(BF16)

    return pl.pallas_call(body, name="ln1_fwd", grid=(s // ROWS,), in_specs=[_rows(d), _rows(d), _vec(d), _vec(d)],
                          out_specs=[_rows(d), _rows(d)],
                          out_shape=[jax.ShapeDtypeStruct((s, d), F32), jax.ShapeDtypeStruct((s, d), BF16)],
                          compiler_params=_cp("parallel"))(x, mixed, g, b)


def _ln2_loss(x1, act, w_down, target, g, b):
    s, d = x1.shape
    assert act.dtype == BF16 and w_down.dtype == BF16

    def body(x_ref, a_ref, w_ref, t_ref, g_ref, b_ref, dp_ref, dpb_ref, loss_ref, dg_ref, db_ref):
        @pl.when(pl.program_id(0) == 0)
        def _():
            loss_ref[...] = jnp.zeros_like(loss_ref)
            dg_ref[...] = jnp.zeros_like(dg_ref)
            db_ref[...] = jnp.zeros_like(db_ref)

        h = jnp.dot(a_ref[...], w_ref[...], preferred_element_type=F32)
        xhat, rstd = _ln_stats(DEEPNORM_ALPHA * x_ref[...] + h)
        err = xhat * g_ref[...] + b_ref[...] - t_ref[...]
        part = 0.5 * jnp.sum(jnp.mean(err * err, axis=1, keepdims=True), axis=0, keepdims=True)
        loss_ref[...] += jnp.broadcast_to(part, loss_ref.shape)
        dy = err * (1.0 / d)
        dg_ref[...] += jnp.sum(dy * xhat, axis=0, keepdims=True)
        db_ref[...] += jnp.sum(dy, axis=0, keepdims=True)
        dp = _ln_bwd(dy, xhat, rstd, g_ref[...])
        dp_ref[...] = dp
        dpb_ref[...] = dp.astype(BF16)

    return pl.pallas_call(
        body, name="ln2_loss", grid=(s // ROWS,),
        in_specs=[_rows(d), _rows(act.shape[1]), pl.BlockSpec(w_down.shape, lambda i: (0, 0)), _rows(d), _vec(d), _vec(d)],
        out_specs=[_rows(d), _rows(d), _vec(LANES), _vec(d), _vec(d)],
        out_shape=[jax.ShapeDtypeStruct((s, d), F32), jax.ShapeDtypeStruct((s, d), BF16),
                   jax.ShapeDtypeStruct((1, LANES), F32),
                   jax.ShapeDtypeStruct((1, d), F32), jax.ShapeDtypeStruct((1, d), F32)],
        compiler_params=_cp("arbitrary"))(x1, act, w_down, target, g, b)


def _ln1_bwd(x, mixed, g, dpre2, dffn):
    s, d = x.shape

    def body(x_ref, m_ref, g_ref, d2_ref, df_ref, dp_ref, dr_ref, dg_ref, db_ref):
        @pl.when(pl.program_id(0) == 0)
        def _():
            dg_ref[...] = jnp.zeros_like(dg_ref)
            db_ref[...] = jnp.zeros_like(db_ref)

        xhat, rstd = _ln_stats(DEEPNORM_ALPHA * x_ref[...] + m_ref[...])
        dy = DEEPNORM_ALPHA * d2_ref[...] + df_ref[...]
        dg_ref[...] += jnp.sum(dy * xhat, axis=0, keepdims=True)
        db_ref[...] += jnp.sum(dy, axis=0, keepdims=True)
        dp = _ln_bwd(dy, xhat, rstd, g_ref[...])
        dp_ref[...] = dp.astype(BF16)
        dr_ref[...] = DEEPNORM_ALPHA * dp

    return pl.pallas_call(
        body, name="ln1_bwd", grid=(s // ROWS,), in_specs=[_rows(d), _rows(d), _vec(d), _rows(d), _rows(d)],
        out_specs=[_rows(d), _rows(d), _vec(d), _vec(d)],
        out_shape=[jax.ShapeDtypeStruct((s, d), BF16), jax.ShapeDtypeStruct((s, d), F32),
                   jax.ShapeDtypeStruct((1, d), F32), jax.ShapeDtypeStruct((1, d), F32)],
        compiler_params=_cp("arbitrary"))(x, mixed, g, dpre2, dffn)


def _swiglu_fwd(gu):
    s = gu.shape[0]
    f = FFN_HIDDEN

    def body(g_ref, u_ref, o_ref):
        gg = g_ref[...].astype(F32)
        o_ref[...] = (gg * _sigmoid(gg) * u_ref[...].astype(F32)).astype(o_ref.dtype)

    return pl.pallas_call(
        body, name="swiglu_fwd", grid=(s // ROWS,),
        in_specs=[pl.BlockSpec((ROWS, f), lambda i: (i, 0)), pl.BlockSpec((ROWS, f), lambda i: (i, 1))],
        out_specs=_rows(f), out_shape=jax.ShapeDtypeStruct((s, f), BF16), compiler_params=_cp("parallel"))(gu, gu)


def _swiglu_bwd(gu, dact):
    s = gu.shape[0]
    f = FFN_HIDDEN

    def body(g_ref, u_ref, d_ref, o_ref):
        sg, dsg = _silu_and_grad(g_ref[...].astype(F32))
        dd = d_ref[...].astype(F32)
        o_ref[:, :f] = (dd * u_ref[...].astype(F32) * dsg).astype(o_ref.dtype)
        o_ref[:, f:] = (dd * sg).astype(o_ref.dtype)

    return pl.pallas_call(
        body, name="swiglu_bwd", grid=(s // ROWS,),
        in_specs=[pl.BlockSpec((ROWS, f), lambda i: (i, 0)), pl.BlockSpec((ROWS, f), lambda i: (i, 1)), _rows(f)],
        out_specs=_rows(2 * f), out_shape=jax.ShapeDtypeStruct((s, 2 * f), BF16),
        compiler_params=_cp("parallel"))(gu, gu, dact)


def _peer(k):
    x, y, c = lax.axis_index("x"), lax.axis_index("y"), lax.axis_index("c")
    kx, ky, kc = (k >> 2) & 1, (k >> 1) & 1, k & 1
    px = (1 - x) if kx else x
    py = (1 - y) if ky else y
    pc = (1 - c) if kc else c
    return (px, py, pc), 4 * px + 2 * py + pc


def _my_index():
    return 4 * lax.axis_index("x") + 2 * lax.axis_index("y") + lax.axis_index("c")


def _comm_copies(ins, outs, sems, scatter):
    send_sems, recv_sems, local_sems = sems
    me = _my_index()
    copies = [pltpu.make_async_copy(ins[t].at[me] if scatter else ins[t], outs[t].at[me], local_sems.at[t])
              for t in range(len(ins))]
    for k in range(1, N_DEV):
        peer, pidx = _peer(k)
        for t in range(len(ins)):
            copies.append(pltpu.make_async_remote_copy(
                src_ref=ins[t].at[pidx] if scatter else ins[t], dst_ref=outs[t].at[me],
                send_sem=send_sems.at[t, k - 1], recv_sem=recv_sems.at[t, k - 1], device_id=peer,
                device_id_type=pl.DeviceIdType.MESH))
    return copies


def _comm_sems(n):
    return [pltpu.SemaphoreType.DMA((n, N_DEV - 1)), pltpu.SemaphoreType.DMA((n, N_DEV - 1)),
            pltpu.SemaphoreType.DMA((n,))]


def _comm_out_shapes(parts, scatter):
    return [jax.ShapeDtypeStruct(p.shape if scatter else (N_DEV,) + p.shape, p.dtype) for p in parts]


def _all_gather(parts):
    n = len(parts)

    def body(*refs):
        ins, outs = refs[:n], refs[n:2 * n]
        send_sems, recv_sems, local_sems = refs[2 * n:]
        x, y, c = lax.axis_index("x"), lax.axis_index("y"), lax.axis_index("c")
        me, sibling = (x, y, c), (x, y, 1 - c)
        chips = [(1 - x, y), (x, 1 - y), (1 - x, 1 - y)]

        def copy(t, k, block, to, src=None):
            dst = outs[t].at[4 * block[0] + 2 * block[1] + block[2]]
            return pltpu.make_async_remote_copy(
                src_ref=dst if src is None else src, dst_ref=dst, send_sem=send_sems.at[t, k],
                recv_sem=recv_sems.at[t, k], device_id=to, device_id_type=pl.DeviceIdType.MESH)

        mine = [pltpu.make_async_copy(ins[t], outs[t].at[_my_index()], local_sems.at[t]) for t in range(n)]
        for cp in mine:
            cp.start()
        first = [copy(t, 0, me, sibling, src=ins[t]) for t in range(n)]
        first += [copy(t, 1 + j, me, (*chip, c), src=ins[t]) for j, chip in enumerate(chips) for t in range(n)]
        for cp in first:
            cp.start()
        passed = []
        for j, chip in enumerate(chips):
            for t in range(n):
                copy(t, 1 + j, (*chip, c), me).wait_recv()
                passed.append(copy(t, 4 + j, (*chip, c), sibling))
                passed[-1].start()
        for t in range(n):
            copy(t, 0, sibling, me).wait_recv()
            for j, chip in enumerate(chips):
                copy(t, 4 + j, (*chip, 1 - c), me).wait_recv()
        for cp in first + passed:
            cp.wait_send()
        for cp in mine:
            cp.wait()

    anyspec = pl.BlockSpec(memory_space=pl.ANY)
    return pl.pallas_call(body, name="all_gather", in_specs=[anyspec] * n, out_specs=[anyspec] * n,
                          out_shape=_comm_out_shapes(parts, False), scratch_shapes=_comm_sems(n))(*parts)


def _remote_scatter_copies(ins, lands, send_sems, recv_sems):
    me = _my_index()
    copies = []
    for k in range(1, N_DEV):
        peer, pidx = _peer(k)
        for t in range(len(ins)):
            copies.append(pltpu.make_async_remote_copy(
                src_ref=ins[t].at[pidx], dst_ref=lands[t].at[me], send_sem=send_sems.at[t * (N_DEV - 1) + k - 1],
                recv_sem=recv_sems.at[t * (N_DEV - 1) + k - 1], device_id=peer, device_id_type=pl.DeviceIdType.MESH))
    return copies


def _landing_zones(parts):
    me = _my_index()
    return [jnp.where(lax.broadcasted_iota(jnp.int32, p.shape, 0) == me, p, jnp.zeros_like(p)) for p in parts]


_HBM = pl.BlockSpec(memory_space=pltpu.HBM)
_SEM = pl.BlockSpec(memory_space=pltpu.SEMAPHORE)


def _exchange_start(parts, lands):
    n = len(parts)

    def body(*refs):
        ins, lnd, send_sems, recv_sems, token = refs[:n], refs[n:2 * n], refs[2 * n], refs[2 * n + 1], refs[-1]
        for cp in _remote_scatter_copies(ins, lnd, send_sems, recv_sems):
            cp.start()
        token[...] = jnp.zeros_like(token)

    hbm = [pltpu.HBM(p.shape, p.dtype) for p in parts]
    outs = pl.pallas_call(
        body, name="exchange_start",
        out_shape=[pltpu.SemaphoreType.DMA((n * (N_DEV - 1),)), pltpu.SemaphoreType.DMA((n * (N_DEV - 1),))] + hbm + hbm
        + [jax.ShapeDtypeStruct((8, LANES), F32)],
        in_specs=[_HBM] * (2 * n), out_specs=[_SEM, _SEM] + [_HBM] * (2 * n) + [pl.BlockSpec(memory_space=pltpu.VMEM)],
        input_output_aliases={t: 2 + t for t in range(2 * n)},
        compiler_params=pltpu.CompilerParams(has_side_effects=pltpu.SideEffectType.DATAFLOW_SIDE_EFFECTING),
    )(*[pltpu.with_memory_space_constraint(p, pltpu.HBM) for p in list(parts) + list(lands)])
    return outs[0], outs[1], list(outs[2:2 + n]), list(outs[2 + n:2 + 2 * n]), outs[-1]


def _exchange_wait(send_sems, recv_sems, parts, lands, after):
    n = len(parts)

    def body(*refs):
        ins, lnd, send_sems, recv_sems = refs[:n], refs[n:2 * n], refs[2 * n], refs[2 * n + 1]
        for cp in _remote_scatter_copies(ins, lnd, send_sems, recv_sems):
            cp.wait_send()
            cp.wait_recv()

    hbm = [pltpu.HBM(p.shape, p.dtype) for p in parts]
    outs = pl.pallas_call(
        body, name="exchange_wait", out_shape=hbm + hbm,
        in_specs=[_HBM] * (2 * n) + [_SEM, _SEM, pl.BlockSpec(memory_space=pl.ANY)], out_specs=[_HBM] * (2 * n),
        input_output_aliases={t: t for t in range(2 * n)},
        compiler_params=pltpu.CompilerParams(has_side_effects=pltpu.SideEffectType.DATAFLOW_SIDE_EFFECTING),
    )(*parts, *lands, send_sems, recv_sems, after)
    return list(outs[n:])


def _adamw(recv, w, m, v, name):
    _, r, c = w.shape
    br = _tile(r, 128)
    c1 = 1.0 / (1.0 - ADAM_B1 ** ADAM_STEP)
    c2 = 1.0 / (1.0 - ADAM_B2 ** ADAM_STEP)

    def body(r_ref, w_ref, m_ref, v_ref, g_ref, d_ref, mo_ref, vo_ref):
        g = r_ref[0].astype(F32)
        for k in range(1, N_DEV):
            g = g + r_ref[k].astype(F32)
        mn = ADAM_B1 * m_ref[0] + (1.0 - ADAM_B1) * g
        vn = ADAM_B2 * v_ref[0] + (1.0 - ADAM_B2) * (g * g)
        g_ref[0] = g
        mo_ref[0] = mn
        vo_ref[0] = vn
        d_ref[0] = -ADAM_LR * ((mn * c1) / (jnp.sqrt(vn * c2) + ADAM_EPS) + ADAM_WD * w_ref[0])

    blk = pl.BlockSpec((1, br, c), lambda i: (0, i, 0))
    return pl.pallas_call(
        body, name=name, grid=(r // br,),
        in_specs=[pl.BlockSpec((N_DEV, br, c), lambda i: (0, i, 0)), blk, blk, blk],
        out_specs=[blk] * 4, out_shape=[jax.ShapeDtypeStruct((1, r, c), F32)] * 4,
        compiler_params=_cp("parallel"))(recv, w, m, v)


def _lane_row(pairs):
    row = jnp.zeros((LANES,), F32)
    for lane0, vec in pairs:
        row = lax.dynamic_update_slice(row, vec.astype(F32), (lane0,))
    return row.reshape(1, LANES)


def _stage_in(x, wts, small):
    s = x.shape[0]
    a = -jnp.exp(small["a_log"])
    bias_row = _lane_row([(DT_LANE0, small["dt_bias"]), (F_LANE0, small["b_forget"])])
    a_row = _lane_row([(DT_LANE0, a)])
    conv_b = small["conv_b"].reshape(1, -1)
    norm_w = small["ssm_norm_w"].reshape(1, -1)
    bg = small["b_gates"].reshape(1, -1)
    g1, b1 = small["ln1_g"].reshape(1, -1), small["ln1_b"].reshape(1, -1)
    g2, b2 = small["ln2_g"].reshape(1, -1), small["ln2_b"].reshape(1, -1)
    d_skip = small["d_skip"]
    xb = x.astype(BF16)

    qkv = _mm(xb, wts["qkv"], out_dtype=BF16, name="f_qkv")
    z = _mm(xb, wts["z"], out_dtype=BF16, name="f_z")
    xbc = _mm(xb, wts["xbc"], name="f_xbc")
    gl = _mm(xb, wts["gate"], out_dtype=BF16, name="f_gate")
    fd = _mm(xb, wts["fd"], name="f_fd")
    dt_c, ac_c, cf_c, dt_r, ac_r, cf_r = _stats_fwd(fd, bias_row, a_row)
    bk = _att_blocks(s)[1]
    ck4 = cf_r[F_LANE0:F_LANE0 + ATT_HEADS].reshape(N_HP, HP, s // bk, bk)
    return dict(locals())


def _stage_mid(c, attn, lse, wts, target):
    x, xb, qkv, z, xbc, gl, fd, ck4 = (c[k] for k in ("x", "xb", "qkv", "z", "xbc", "gl", "fd", "ck4"))
    dt_c, ac_c, dt_r, ac_r, a_row, bias_row = (c[k] for k in ("dt_c", "ac_c", "dt_r", "ac_r", "a_row", "bias_row"))
    conv_b, norm_w, bg, g1, b1, g2, b2, d_skip = (c[k] for k in ("conv_b", "norm_w", "bg", "g1", "b1", "g2", "b2",
                                                                "d_skip"))
    conv_w = c["wts"]["conv"]
    attn_d = _mm(attn, wts["pa"], out_dtype=BF16, name="f_pa")
    xact = _conv_fwd(xbc, conv_w, conv_b)
    dsk_pair = jnp.repeat(d_skip, SSM_HEAD_DIM).reshape(N_PAIR, LANES)
    y, hprev = _ssd_pair_fwd(xact, ac_c, dt_r, ac_r, dsk_pair)
    ssm = _gnorm_fwd(y, z, norm_w)
    ssm_d = _mm(ssm, wts["ps"], out_dtype=BF16, name="f_ps")
    mix = _mix_fwd(gl, bg, attn_d, ssm_d)
    mixed = _mm(mix, wts["out"], name="f_out")
    x1, x1_b = _ln1_fwd(x, mixed, g1, b1)
    gu = _mm(x1_b, wts["gu"], out_dtype=BF16, name="f_gu")
    act = _swiglu_fwd(gu)
    dpre2, dpre2_b, loss_row, dg2, db2 = _ln2_loss(x1, act, wts["down"], target, g2, b2)

    d_act = _mm(dpre2_b, wts["down"], tb=True, out_dtype=BF16, name="b_down_x")
    dw_down = _mm(act, dpre2_b, ta=True, name="b_down_w")
    dgu = _swiglu_bwd(gu, d_act)
    dffn = _mm(dgu, wts["gu"], tb=True, name="b_gu_x")
    dw_gu = _mm(x1_b, dgu, ta=True, name="b_gu_w")
    dpre1, dxr, dg1, db1 = _ln1_bwd(x, mixed, g1, dpre2, dffn)
    dmix = _mm(dpre1, wts["out"], tb=True, out_dtype=BF16, name="b_out_x")
    dw_out = _mm(mix, dpre1, ta=True, name="b_out_w")
    dattn_d, dssm_d, dgl, dbg = _mix_bwd(gl, bg, attn_d, ssm_d, dmix)
    dssm = _mm(dssm_d, wts["ps"], tb=True, out_dtype=BF16, name="b_ps_x")
    dw_ps = _mm(ssm, dssm_d, ta=True, name="b_ps_w")
    dw_pa = _mm(attn, dattn_d, ta=True, name="b_pa_w")
    dy, dz, dnw = _gnorm_bwd(y, z, norm_w, dssm)
    dxact, ddt, da_row, dds_pair = _ssd_pair_bwd(xact, dt_c, ac_c, dt_r, ac_r, hprev, dy, dsk_pair, a_row)
    dds = dds_pair.reshape(SSM_HEADS, SSM_HEAD_DIM).sum(axis=1)
    dpre_c, dconv_w, dconv_b = _conv_bwd_pre(xbc, conv_w, conv_b, dxact)
    dxbc = _conv_bwd_in(dpre_c, conv_w)
    st, do_b = _att_prep(dattn_d, wts["pa"], attn, lse)
    late = dict(pa=dw_pa, ps=dw_ps, out=dw_out, gu=dw_gu, down=dw_down)
    keep = ("st", "do_b", "ddt", "dxr", "dz", "dxbc", "dgl", "dconv_w", "dconv_b", "da_row", "dds", "dnw", "dbg",
            "dg1", "db1", "dg2", "db2", "loss_row")
    loc = locals()
    return {**c, **{k: loc[k] for k in keep}}, late


def _stage_out_w(c, att_grads):
    dq, dk, dv, dck, dcq = att_grads
    xb, fd, bias_row, ddt, dz, dxbc, dgl = (c[k] for k in ("xb", "fd", "bias_row", "ddt", "dz", "dxbc", "dgl"))
    s, a = xb.shape[0], c["a"]
    dcum = dck.reshape(ATT_HEADS, s) + dcq.reshape(ATT_HEADS, s)
    dfd, dbias = _stats_bwd(fd, bias_row, ddt, jnp.zeros((LANES, s), F32).at[F_LANE0:F_LANE0 + ATT_HEADS].set(dcum))
    dproj = (dq, dk, dv, dz, dxbc, dgl, dfd)
    dw_in = [_mm(xb, g_, ta=True, tb=(i == 0), name=f"b_in_w{i}") for i, g_ in enumerate(dproj)]
    grads = dict(q=dw_in[0], k=dw_in[1], v=dw_in[2], z=dw_in[3], xbc=dw_in[4], gate=dw_in[5], fd=dw_in[6],
                 conv=c["dconv_w"])
    small_g = dict(
        b_forget=dbias[0, F_LANE0:F_LANE0 + ATT_HEADS], conv_b=c["dconv_b"][0], dt_bias=dbias[0, :SSM_HEADS],
        a_log=c["da_row"][0, :SSM_HEADS] * a, d_skip=c["dds"], ssm_norm_w=c["dnw"][0], b_gates=c["dbg"][0],
        ln1_g=c["dg1"][0], ln1_b=c["db1"][0], ln2_g=c["dg2"][0], ln2_b=c["db2"][0])
    return c["loss_row"][0, 0], grads, small_g, dproj


def _stage_out_x(c, dproj, token):
    wts = c["wts"]
    dq_t, dk, dv, dz, dxbc, dgate, dfd = dproj
    w_fd = wts["fd"] + token.astype(BF16)
    dx = _mm_sum([(dq_t, True, wts["qkv"], 0), (dk, False, wts["qkv"], 1), (dv, False, wts["qkv"], 2),
                  (dfd, False, w_fd, 0)], c["dxr"], "b_in_x_qkv_fd")
    dx = _mm_sum([(dz, False, wts["z"], 0), (dgate, False, wts["gate"], 0)], dx, "b_in_x_z_gate")
    return _mm_sum([(dxbc, False, wts["xbc"], 0)], dx, "b_in_x_xbc")


BIG = ("w_in", "w_proj_attn", "w_proj_ssm", "w_out", "w_ffn_gate", "w_ffn_up", "w_ffn_down", "conv_w")
EARLY = ("w_in", "conv_w")
LATE = ("w_proj_attn", "w_proj_ssm", "w_out", "w_ffn_gate", "w_ffn_up", "w_ffn_down")
SMALL = ("b_forget", "conv_b", "dt_bias", "a_log", "d_skip", "ssm_norm_w", "b_gates", "ln1_g", "ln1_b", "ln2_g",
         "ln2_b")
SMALL_ROWS = 96
IN_SHARD = IN_WIDTH // N_DEV
IN_SEGMENTS = (("q", 0, 1024), ("k", 1024, 1024), ("v", 2048, 1024), ("f", 3072, ATT_HEADS), ("z", 3088, SSM_INNER),
               ("xbc", 5136, SSM_CONV_DIM), ("dt", 8208, SSM_HEADS), ("gate", 8240, 2 * D_MODEL))


def _cols_from_shards(shards, lo, hi):
    w = shards[0].shape[1]
    pieces = []
    for j in range(len(shards)):
        a, b = max(lo, j * w), min(hi, (j + 1) * w)
        if a < b:
            pieces.append(shards[j][:, a - j * w:b - j * w])
    return pieces[0] if len(pieces) == 1 else jnp.concatenate(pieces, axis=1)


def _shards_from_parts(parts, width):
    shards = []
    for j in range(N_DEV):
        lo, hi = j * width, (j + 1) * width
        pieces = []
        for mat, c0 in parts:
            a, b = max(lo, c0), min(hi, c0 + mat.shape[1])
            if a < b:
                pieces.append(mat[:, a - c0:b - c0])
        shards.append(pieces[0] if len(pieces) == 1 else jnp.concatenate(pieces, axis=1))
    return shards


def _pack_small(vals):
    flat = jnp.concatenate([vals[n].reshape(-1) for n in SMALL])
    return jnp.pad(flat, (0, SMALL_ROWS * LANES - flat.shape[0])).reshape(SMALL_ROWS, LANES)


def _unpack_small(pack, shapes):
    flat = pack.reshape(-1)
    out, off = {}, 0
    for n in SMALL:
        sz = math.prod(shapes[n])
        out[n] = flat[off:off + sz].reshape(shapes[n])
        off += sz
    return out


def kernel(x, w_in, b_forget, conv_w, conv_b, dt_bias, a_log, d_skip, ssm_norm_w, w_proj_attn, w_proj_ssm, b_gates, w_out, ln1_g, ln1_b, w_ffn_gate, w_ffn_up, w_ffn_down, ln2_g, ln2_b, loss_target, m_w_in, m_b_forget, m_conv_w, m_conv_b, m_dt_bias, m_a_log, m_d_skip, m_ssm_norm_w, m_w_proj_attn, m_w_proj_ssm, m_b_gates, m_w_out, m_ln1_g, m_ln1_b, m_w_ffn_gate, m_w_ffn_up, m_w_ffn_down, m_ln2_g, m_ln2_b, v_w_in, v_b_forget, v_conv_w, v_conv_b, v_dt_bias, v_a_log, v_d_skip, v_ssm_norm_w, v_w_proj_attn, v_w_proj_ssm, v_b_gates, v_w_out, v_ln1_g, v_ln1_b, v_w_ffn_gate, v_w_ffn_up, v_w_ffn_down, v_ln2_g, v_ln2_b):
    args = dict(locals())
    d, f = D_MODEL, FFN_HIDDEN
    big_w = {n: args[n][0] for n in BIG}
    small_w = {n: args[n][0] for n in SMALL}
    big_shapes = {n: args[n].shape for n in BIG}
    small_shapes = {n: args[n].shape for n in SMALL}

    early = dict(zip(EARLY, _all_gather([big_w["w_in"].astype(BF16), big_w["conv_w"]])))
    in_shards = [early["w_in"][j] for j in range(N_DEV)]
    seg = {n: _cols_from_shards(in_shards, c0, c0 + w) for n, c0, w in IN_SEGMENTS}
    wfd = jnp.concatenate([seg["dt"], seg["f"], jnp.zeros((d, LANES - SSM_HEADS - ATT_HEADS), BF16)], axis=1)
    wts = dict(qkv=jnp.concatenate([seg["q"], seg["k"], seg["v"]], axis=1), z=seg["z"], xbc=seg["xbc"],
               gate=seg["gate"], fd=wfd, conv=jnp.concatenate([early["conv_w"][j] for j in range(N_DEV)], axis=1))

    ctx = _stage_in(x[0], wts, small_w)
    attn, lse, gathered = _attention_fwd(ctx["qkv"], ctx["ck4"], [big_w[n].astype(BF16) for n in LATE])
    full = dict(zip(LATE, gathered))
    late_w = dict(
        pa=full["w_proj_attn"].reshape(d, d), ps=full["w_proj_ssm"].reshape(SSM_INNER, d),
        out=full["w_out"].reshape(d, d),
        gu=jnp.concatenate([full["w_ffn_gate"][j] for j in range(N_DEV)]
                           + [full["w_ffn_up"][j] for j in range(N_DEV)], axis=1),
        down=full["w_ffn_down"].reshape(f, d))
    ctx, gl = _stage_mid(ctx, attn, lse, late_w, loss_target[0])
    late_dest = dict(
        w_ffn_gate=jnp.stack([s_.astype(BF16) for s_ in _shards_from_parts([(gl["gu"][:, :f], 0)], f // N_DEV)]),
        w_ffn_up=jnp.stack([s_.astype(BF16) for s_ in _shards_from_parts([(gl["gu"][:, f:], 0)], f // N_DEV)]))
    for n, key in (("w_proj_attn", "pa"), ("w_proj_ssm", "ps"), ("w_out", "out"), ("w_ffn_down", "down")):
        late_dest[n] = gl[key].astype(BF16).reshape((N_DEV,) + big_shapes[n][1:])
    att_grads, late_recv = _attention_bwd(ctx["qkv"], ctx["ck4"], ctx["st"], ctx["do_b"], [late_dest[n] for n in LATE])
    loss_part, g, small_g, dproj = _stage_out_w(ctx, att_grads)
    loss = lax.psum(loss_part, ("x", "y", "c"))

    gfd = g["fd"]
    in_parts = dict(q=g["q"], k=g["k"], v=g["v"], f=gfd[:, F_LANE0:F_LANE0 + ATT_HEADS], z=g["z"], xbc=g["xbc"],
                    dt=gfd[:, DT_LANE0:DT_LANE0 + SSM_HEADS], gate=g["gate"])
    win_dest = jnp.stack([s_.astype(BF16) for s_ in
                          _shards_from_parts([(in_parts[n], c0) for n, c0, _ in IN_SEGMENTS], IN_SHARD)])
    conv_dest = jnp.stack(_shards_from_parts([(g["conv"], 0)], SSM_CONV_DIM // N_DEV))
    small_pack = _pack_small(small_g)
    last_parts = [win_dest, conv_dest, jnp.broadcast_to(small_pack, (N_DEV,) + small_pack.shape)]
    send_sems, recv_sems, parts_thru, lands_thru, token = _exchange_start(last_parts, _landing_zones(last_parts))
    grad_x = _stage_out_x(ctx, dproj, token[0, 0])
    early_recv = _exchange_wait(send_sems, recv_sems, parts_thru, lands_thru, grad_x)
    recv = dict(zip(LATE, late_recv))
    recv["w_in"], recv["conv_w"] = early_recv[0], early_recv[1]

    outs = {}
    for n in BIG:
        outs[n] = _adamw(recv[n], args[n], args["m_" + n], args["v_" + n], name="adamw_" + n)
    small4 = _adamw(early_recv[2], _pack_small(small_w)[None], _pack_small({n: args["m_" + n][0] for n in SMALL})[None],
                    _pack_small({n: args["v_" + n][0] for n in SMALL})[None], name="adamw_small")
    small_out = [_unpack_small(p, small_shapes) for p in small4]
    for n in SMALL:
        outs[n] = [so[n] for so in small_out]

    order = ("w_in", "b_forget", "conv_w", "conv_b", "dt_bias", "a_log", "d_skip", "ssm_norm_w", "w_proj_attn",
             "w_proj_ssm", "b_gates", "w_out", "ln1_g", "ln1_b", "w_ffn_gate", "w_ffn_up", "w_ffn_down", "ln2_g",
             "ln2_b")
    res = [loss, grad_x[None]]
    for i in range(4):
        res += [outs[n][i] for n in order]
    return tuple(res)
```

```python
import functools
import math

import jax
import jax.numpy as jnp
from jax import lax
from jax.experimental import pallas as pl
from jax.experimental.pallas import tpu as pltpu

F32 = jnp.float32
BF16 = jnp.bfloat16

N_DEV = 8
D_MODEL = 1024
ATT_HEADS = 16
ATT_HEAD_DIM = 64
SSM_INNER = 2048
SSM_HEADS = 32
SSM_HEAD_DIM = 64
SSM_GROUPS = 4
SSM_HEADS_PER_GROUP = 8
SSM_STATE = 128
SSM_CONV = 4
SSM_CHUNK = 128
SSM_CONV_DIM = 3072
FFN_HIDDEN = 2816
IN_WIDTH = 10288
DEEPNORM_ALPHA = 2.0 ** 0.25
LN_EPS = 1e-5
RMS_EPS = 1e-5
ADAM_LR, ADAM_B1, ADAM_B2, ADAM_EPS, ADAM_WD, ADAM_STEP = 0.001, 0.9, 0.999, 1e-08, 0.01, 10
ATT_SCALE = 1.0 / math.sqrt(ATT_HEAD_DIM)

LANES = 128
VMEM_LIMIT = 56 * 1024 * 1024
NEG = -1e30

DT_LANE0 = 0
F_LANE0 = 32
HI = lax.Precision.HIGHEST


def _cp(*sem):
    return pltpu.CompilerParams(dimension_semantics=sem, vmem_limit_bytes=VMEM_LIMIT)


def _tile(n, cap=1408):
    for t in (3072, 2816, 2048, 1536, 1408, 1024, 512, 384, 256, 128):
        if t <= cap and n % t == 0:
            return t
    return n


MM_VMEM_BUDGET = VMEM_LIMIT - 4 * 2 ** 20


def _mm_tiles(m, n, k, a_bytes, b_bytes, out_bytes, has_add):
    tm = _tile(m)

    def need(tn, tk):
        blocks = tm * tk * a_bytes + tk * tn * b_bytes + tm * tn * (out_bytes + (4 if has_add else 0))
        casts = (tm * tk * 2 if a_bytes == 4 else 0) + (tk * tn * 2 if b_bytes == 4 else 0)
        return 2 * blocks + casts + tm * tn * 4

    tns = [t for t in (3072, 2816, 2048, 1536, 1408, 1024, 512, 384, 256, 128) if n % t == 0] or [n]
    for tk in ([k] if k <= 3072 else []) + [t for t in (2816, 1024, 512, 256, 128) if k % t == 0]:
        fits = [tn for tn in tns if need(tn, tk) <= MM_VMEM_BUDGET]
        if fits and fits[0] >= min(1024, tns[0]):
            return tm, fits[0], tk
    return tm, tns[-1], tk


def _sigmoid(x):
    return 1.0 / (1.0 + jnp.exp(-x))


def _mm(a, b, *, ta=False, tb=False, out_dtype=F32, add=None, name):
    m, k = (a.shape[1], a.shape[0]) if ta else a.shape
    n = b.shape[0] if tb else b.shape[1]
    assert (b.shape[1] if tb else b.shape[0]) == k
    tm, tn, tk = _mm_tiles(m, n, k, a.dtype.itemsize, b.dtype.itemsize, jnp.dtype(out_dtype).itemsize, add is not None)
    nk = k // tk
    dims = (((0,) if ta else (1,), (1,) if tb else (0,)), ((), ()))

    def body_single(*refs):
        a_ref, b_ref = refs[:2]
        r = lax.dot_general(a_ref[...].astype(BF16), b_ref[...].astype(BF16), dims, preferred_element_type=F32)
        if add is not None:
            r = r + refs[2][...]
        refs[-1][...] = r.astype(refs[-1].dtype)

    def body(*refs):
        if add is None:
            a_ref, b_ref, o_ref, acc_ref = refs
        else:
            a_ref, b_ref, c_ref, o_ref, acc_ref = refs
        kk = pl.program_id(2)

        @pl.when(kk == 0)
        def _():
            acc_ref[...] = jnp.zeros_like(acc_ref)

        acc_ref[...] += lax.dot_general(a_ref[...].astype(BF16), b_ref[...].astype(BF16), dims,
                                        preferred_element_type=F32)

        @pl.when(kk == nk - 1)
        def _():
            r = acc_ref[...]
            if add is not None:
                r = r + c_ref[...]
            o_ref[...] = r.astype(o_ref.dtype)

    a_spec = pl.BlockSpec((tk, tm), lambda i, j, kk: (kk, i)) if ta else pl.BlockSpec((tm, tk), lambda i, j, kk: (i, kk))
    b_spec = pl.BlockSpec((tn, tk), lambda i, j, kk: (j, kk)) if tb else pl.BlockSpec((tk, tn), lambda i, j, kk: (kk, j))
    o_spec = pl.BlockSpec((tm, tn), lambda i, j, kk: (i, j))
    in_specs, args = [a_spec, b_spec], [a, b]
    if add is not None:
        in_specs.append(o_spec)
        args.append(add)
    return pl.pallas_call(
        body_single if nk == 1 else body, name=name, grid=(m // tm, n // tn, nk), in_specs=in_specs, out_specs=o_spec,
        out_shape=jax.ShapeDtypeStruct((m, n), out_dtype),
        scratch_shapes=[] if nk == 1 else [pltpu.VMEM((tm, tn), F32)],
        compiler_params=_cp("parallel", "parallel", "arbitrary"),
    )(*args)


def _mm_sum(terms, add, name):
    m, n = add.shape
    ks = [a.shape[0] if ta else a.shape[1] for a, ta, _, _ in terms]

    def need(tm):
        blocks = sum(tm * k * a.dtype.itemsize + n * k * b.dtype.itemsize for (a, _, b, _), k in zip(terms, ks))
        casts = sum(tm * k * 2 for (a, _, _, _), k in zip(terms, ks) if a.dtype.itemsize == 4)
        return 2 * (blocks + 2 * tm * n * 4) + casts + tm * n * 4

    tm = next(t for t in (1024, 512, 256, 128) if m % t == 0 and need(t) <= MM_VMEM_BUDGET)

    def body(*refs):
        r = refs[-2][...]
        for i, (_, ta, _, _) in enumerate(terms):
            dims = (((0,) if ta else (1,), (1,)), ((), ()))
            r = r + lax.dot_general(refs[2 * i][...].astype(BF16), refs[2 * i + 1][...].astype(BF16), dims,
                                    preferred_element_type=F32)
        refs[-1][...] = r

    in_specs, args = [], []
    for (a, ta, b, jb), k in zip(terms, ks):
        in_specs += [pl.BlockSpec((k, tm), lambda i: (0, i)) if ta else pl.BlockSpec((tm, k), lambda i: (i, 0)),
                     pl.BlockSpec((n, k), lambda i, jb=jb: (0, jb))]
        args += [a, b]
    o_spec = pl.BlockSpec((tm, n), lambda i: (i, 0))
    return pl.pallas_call(body, name=name, grid=(m // tm,), in_specs=in_specs + [o_spec], out_specs=o_spec,
                          out_shape=jax.ShapeDtypeStruct((m, n), F32), compiler_params=_cp("parallel"))(*args, add)


def _tri(n, lower=True):
    r = lax.broadcasted_iota(jnp.int32, (n, n), 0)
    c = lax.broadcasted_iota(jnp.int32, (n, n), 1)
    return jnp.where((r >= c) if lower else (c >= r), 1.0, 0.0).astype(F32)


def _stats_fwd(fd, bias_row, a_row):
    s = fd.shape[0]
    blk = SSM_CHUNK

    def body(fd_ref, bias_ref, a_ref, dt_ref, ac_ref, cf_ref, dtr_ref, acr_ref, cfr_ref, carry_ref):
        @pl.when(pl.program_id(0) == 0)
        def _():
            carry_ref[...] = jnp.zeros_like(carry_ref)

        v = fd_ref[...] + bias_ref[...]
        dt = jnp.maximum(v, 0.0) + jnp.log(1.0 + jnp.exp(-jnp.abs(v)))
        lf = jnp.minimum(v, 0.0) - jnp.log(1.0 + jnp.exp(-jnp.abs(v)))
        tri = _tri(blk)
        ac = jnp.dot(tri, dt * a_ref[...], precision=HI, preferred_element_type=F32)
        cf = jnp.dot(tri, lf, precision=HI, preferred_element_type=F32) + carry_ref[0:1, :]
        carry_ref[...] = carry_ref[...] + jnp.sum(lf, axis=0, keepdims=True)
        dt_ref[...] = dt
        ac_ref[...] = ac
        cf_ref[...] = cf
        dtr_ref[...] = dt.T
        acr_ref[...] = ac.T
        cfr_ref[...] = cf.T

    col = pl.BlockSpec((blk, LANES), lambda i: (i, 0))
    row = pl.BlockSpec((LANES, blk), lambda i: (0, i))
    vec = pl.BlockSpec((1, LANES), lambda i: (0, 0))
    return pl.pallas_call(
        body, name="stats_fwd", grid=(s // blk,), in_specs=[col, vec, vec],
        out_specs=[col, col, col, row, row, row],
        out_shape=[jax.ShapeDtypeStruct((s, LANES), F32)] * 3 + [jax.ShapeDtypeStruct((LANES, s), F32)] * 3,
        scratch_shapes=[pltpu.VMEM((8, LANES), F32)],
        compiler_params=_cp("arbitrary"),
    )(fd, bias_row, a_row)


def _stats_bwd(fd, bias_row, ddt, dcum_rows):
    s = fd.shape[0]
    blk = SSM_CHUNK
    nb = s // blk

    def body(fd_ref, bias_ref, ddt_ref, dck_ref, o_ref, db_ref, carry_ref):
        @pl.when(pl.program_id(0) == 0)
        def _():
            carry_ref[...] = jnp.zeros_like(carry_ref)
            db_ref[...] = jnp.zeros_like(db_ref)

        v = fd_ref[...] + bias_ref[...]
        dcum = dck_ref[...].T
        dlf = jnp.dot(_tri(blk, lower=False), dcum, precision=HI, preferred_element_type=F32) + carry_ref[0:1, :]
        carry_ref[...] = carry_ref[...] + jnp.sum(dcum, axis=0, keepdims=True)
        lane = lax.broadcasted_iota(jnp.int32, v.shape, 1)
        g = jnp.where(lane < F_LANE0, ddt_ref[...] * _sigmoid(v), dlf * _sigmoid(-v))
        g = jnp.where(lane < F_LANE0 + ATT_HEADS, g, 0.0)
        o_ref[...] = g.astype(o_ref.dtype)
        db_ref[...] += jnp.sum(g, axis=0, keepdims=True)

    col = pl.BlockSpec((blk, LANES), lambda i: (nb - 1 - i, 0))
    row = pl.BlockSpec((LANES, blk), lambda i: (0, nb - 1 - i))
    vec = pl.BlockSpec((1, LANES), lambda i: (0, 0))
    return pl.pallas_call(
        body, name="stats_bwd", grid=(nb,), in_specs=[col, vec, col, row], out_specs=[col, vec],
        out_shape=[jax.ShapeDtypeStruct((s, LANES), BF16), jax.ShapeDtypeStruct((1, LANES), F32)],
        scratch_shapes=[pltpu.VMEM((8, LANES), F32)],
        compiler_params=_cp("arbitrary"),
    )(fd, bias_row, ddt, dcum_rows)


HP = LANES // ATT_HEAD_DIM
N_HP = ATT_HEADS // HP


def _att_blocks(s):
    return (512, 1024) if s % 1024 == 0 and s >= 4096 else (64, 128)


_QK = (((1,), (1,)), ((), ()))
_HALF = ATT_HEAD_DIM // 2
_PAD = 16


def _head_cols(a):
    return slice(a * ATT_HEAD_DIM, (a + 1) * ATT_HEAD_DIM)


def _causal(shape, off):
    r = lax.broadcasted_iota(jnp.int32, shape, 0)
    c = lax.broadcasted_iota(jnp.int32, shape, 1)
    return c <= r + off


def _attention_fwd(qkv, ck4, gather_parts):
    s = qkv.shape[0]
    bk = _att_blocks(s)[1]
    bq = bk
    nq, nk = s // bq, s // bk
    n = len(gather_parts)

    def body(q_ref, k_ref, v_ref, ck_ref, *rest):
        comm_in, (o_ref, lse_ref), comm_out, sems = rest[:n], rest[n:n + 2], rest[n + 2:2 * n + 2], rest[2 * n + 2:]
        i = pl.program_id(1)
        if n:
            @pl.when((pl.program_id(0) == 0) & (i == 0))
            def _():
                for cp in _comm_copies(comm_in, comm_out, sems, False):
                    cp.start()

        n_full = (i * bq) // bk
        qs = [(q_ref[:, _head_cols(a)].astype(F32) * ATT_SCALE).astype(BF16) for a in range(HP)]

        upper_q = lax.broadcasted_iota(jnp.int32, (bq, LANES), 1) >= ATT_HEAD_DIM

        def absorb(j, carry, keys=slice(0, bk), rows=slice(0, bq), masked=False):
            nk_ = keys.stop - keys.start
            ks = pl.ds(pl.multiple_of(j * bk, bk) + keys.start, nk_)
            v_both = v_ref[ks, :]
            out = []
            for a in range(HP):
                m, acc = carry[a]
                sc = lax.dot_general(qs[a][rows], k_ref[ks, _head_cols(a)], _QK, preferred_element_type=F32)
                sc = sc - ck_ref[0, a, pl.ds(j, 1), keys]
                if masked:
                    sc = jnp.where(_causal(sc.shape, 0), sc, NEG)
                m_new = jnp.maximum(m, jnp.max(sc, axis=1, keepdims=True))
                p = jnp.exp((sc - m_new).astype(BF16))
                upper_v = lax.broadcasted_iota(jnp.int32, (nk_, LANES), 1) >= ATT_HEAD_DIM
                v_aug = jnp.where(upper_v == (a == 1), v_both, jnp.ones_like(v_both))
                acc = jnp.exp(m - m_new) * acc + jnp.dot(p, v_aug, preferred_element_type=F32)
                out.append((m_new, acc))
            return tuple(out)

        init = tuple((jnp.full((bq, 1), NEG, F32), jnp.zeros((bq, LANES), F32)) for _ in range(HP))
        carry = lax.fori_loop(0, n_full, absorb, init)
        hq = bq // 2
        carry = absorb(n_full, carry, keys=slice(0, hq), masked=True)
        low = absorb(n_full, tuple((m[hq:], acc[hq:]) for m, acc in carry), keys=slice(hq, bk), rows=slice(hq, bq),
                     masked=True)
        carry = tuple((jnp.concatenate([m[:hq], ml], axis=0), jnp.concatenate([acc[:hq], al], axis=0))
                      for (m, acc), (ml, al) in zip(carry, low))
        outs, lses = [], []
        for a in range(HP):
            m, acc = carry[a]
            l = pltpu.roll(acc, ATT_HEAD_DIM, 1)
            outs.append(acc / l)
            lses.append(m + jnp.log(l))
        o_ref[...] = jnp.where(upper_q, outs[1], outs[0])
        lse_ref[...] = jnp.where(upper_q, lses[1], lses[0])
        if n:
            @pl.when((pl.program_id(0) == N_HP - 1) & (i == nq - 1))
            def _():
                for cp in _comm_copies(comm_in, comm_out, sems, False):
                    cp.wait()

    q_spec = pl.BlockSpec((bq, LANES), lambda h, i: (i, h))
    anyspec = pl.BlockSpec(memory_space=pl.ANY)
    res = pl.pallas_call(
        body, name="att_fwd", grid=(N_HP, nq),
        in_specs=[q_spec, pl.BlockSpec((s, LANES), lambda h, i: (0, N_HP + h)),
                  pl.BlockSpec((s, LANES), lambda h, i: (0, 2 * N_HP + h)),
                  pl.BlockSpec((1, HP, nk, bk), lambda h, i: (h, 0, 0, 0))] + [anyspec] * n,
        out_specs=[q_spec, q_spec] + [anyspec] * n,
        out_shape=[jax.ShapeDtypeStruct((s, D_MODEL), F32)] * 2 + _comm_out_shapes(gather_parts, False),
        scratch_shapes=_comm_sems(n) if n else [],
        compiler_params=_cp("arbitrary", "arbitrary"),
    )(qkv, qkv, qkv, ck4, *gather_parts)
    return res[0], res[1], list(res[2:])


def _att_prep(dattn_d, w_pa, o, lse_rep):
    s = o.shape[0]
    bs = _tile(s, 512)

    def body(g_ref, w_ref, o_ref, lse_ref, st_ref, dob_ref):
        do = lax.dot_general(g_ref[...], w_ref[...], (((1,), (1,)), ((), ())), preferred_element_type=F32)
        r = lax.broadcasted_iota(jnp.int32, (LANES, LANES), 0) // ATT_HEAD_DIM
        c = lax.broadcasted_iota(jnp.int32, (LANES, LANES), 1) // ATT_HEAD_DIM
        e = jnp.where(r == c, 1.0, 0.0).astype(F32)
        lane = lax.broadcasted_iota(jnp.int32, (bs, LANES), 1)
        for p in range(D_MODEL // LANES):
            cs = slice(p * LANES, (p + 1) * LANES)
            dd = do[:, cs]
            delta = jnp.dot(dd * o_ref[:, cs], e, precision=HI, preferred_element_type=F32)
            st_ref[:, cs] = jnp.where(lane % ATT_HEAD_DIM < _HALF, lse_ref[:, cs], delta)
            dob_ref[:, cs] = dd.astype(BF16)

    assert dattn_d.dtype == BF16 and w_pa.dtype == BF16
    spec = pl.BlockSpec((bs, D_MODEL), lambda i: (i, 0))
    w_spec = pl.BlockSpec(w_pa.shape, lambda i: (0, 0))
    return pl.pallas_call(body, name="att_prep", grid=(s // bs,), in_specs=[spec, w_spec, spec, spec],
                          out_specs=[spec, spec],
                          out_shape=[jax.ShapeDtypeStruct((s, D_MODEL), F32), jax.ShapeDtypeStruct((s, D_MODEL), BF16)],
                          compiler_params=_cp("parallel"))(dattn_d, w_pa, o, lse_rep)


def _attention_bwd(qkv, ck4, st, do_b, exchange_parts):
    s = qkv.shape[0]
    bq, bk = _att_blocks(s)
    nq, nk, per = s // bq, s // bk, bk // bq
    _T = (((0,), (0,)), ((), ()))
    n = len(exchange_parts)

    def body(q_ref, k_ref, v_ref, ck_ref, st_ref, do_ref, *rest):
        comm_in, (dq_ref, dk_ref, dv_ref, dck_ref, dcq_ref) = rest[:n], rest[n:n + 5]
        comm_out, sems, (dk_acc, dv_acc) = rest[n + 5:2 * n + 5], rest[2 * n + 5:-2], rest[-2:]
        j = pl.program_id(1)
        if n:
            @pl.when((pl.program_id(0) == 0) & (j == 0))
            def _():
                for cp in _comm_copies(comm_in, comm_out, sems, True):
                    cp.start()

        @pl.when(j == 0)
        def _():
            dq_ref[...] = jnp.zeros_like(dq_ref)
            dcq_ref[...] = jnp.zeros_like(dcq_ref)

        dk_acc[...] = jnp.zeros_like(dk_acc)
        dv_acc[...] = jnp.zeros_like(dv_acc)

        ones_q, ones_k = jnp.ones((_PAD, bq), BF16), jnp.ones((_PAD, bk), BF16)
        k_t = [jnp.concatenate([k_ref[:, _head_cols(a)].T, ones_k], axis=0) for a in range(HP)]

        def step(i, off=None, kl=slice(0, bk)):
            rows = pl.ds(pl.multiple_of(i * bq, bq), bq)
            for a in range(HP):
                cs = _head_cols(a)
                q = (q_ref[rows, cs].astype(F32) * ATT_SCALE).astype(BF16)
                do_a = do_ref[rows, cs]
                sc = lax.dot_general(q, k_ref[kl, cs], _QK, preferred_element_type=F32) - ck_ref[0, a, pl.ds(j, 1), kl]
                if off is not None:
                    sc = jnp.where(_causal(sc.shape, off), sc, NEG)
                p = jnp.exp(sc - st_ref[rows, a * ATT_HEAD_DIM:a * ATT_HEAD_DIM + 1])
                dp = lax.dot_general(do_a, v_ref[kl, cs], _QK, preferred_element_type=F32)
                ds = p * (dp - st_ref[rows, a * ATT_HEAD_DIM + _HALF:a * ATT_HEAD_DIM + _HALF + 1])
                ds_b = ds.astype(BF16)
                dv_acc[a, :, kl] += jnp.dot(do_a.T, p.astype(BF16), preferred_element_type=F32)
                dk_acc[a, :, kl] += jnp.dot(jnp.concatenate([q.T, ones_q], axis=0), ds_b, preferred_element_type=F32)
                dqs = lax.dot_general(k_t[a][:, kl], ds_b, _QK, preferred_element_type=F32)
                dq_ref[cs, rows] += dqs[:ATT_HEAD_DIM] * ATT_SCALE
                dcq_ref[0, a, pl.ds(i, 1), :] += jnp.sum(dqs[ATT_HEAD_DIM:ATT_HEAD_DIM + 8], axis=0,
                                                         keepdims=True) * 0.125

        for t in range(per):
            step(j * per + t, off=t * bq, kl=slice(0, (t + 1) * bq))

        def full(i, c):
            step(i)
            return c

        lax.fori_loop((j + 1) * per, nq, full, 0)
        for a in range(HP):
            dk_ref[:, _head_cols(a)] = dk_acc[a, :ATT_HEAD_DIM].T.astype(dk_ref.dtype)
            dv_ref[:, _head_cols(a)] = dv_acc[a].T.astype(dv_ref.dtype)
            dck_ref[0, a, pl.ds(j, 1), :] = -dk_acc[a, ATT_HEAD_DIM:ATT_HEAD_DIM + 1]
        if n:
            @pl.when((pl.program_id(0) == N_HP - 1) & (j == nk - 1))
            def _():
                for cp in _comm_copies(comm_in, comm_out, sems, True):
                    cp.wait()

    res = pl.BlockSpec((s, LANES), lambda h, j: (0, h))
    ck_spec = pl.BlockSpec((1, HP, nk, bk), lambda h, j: (h, 0, 0, 0))
    kout = pl.BlockSpec((bk, LANES), lambda h, j: (j, h))
    anyspec = pl.BlockSpec(memory_space=pl.ANY)
    outs = pl.pallas_call(
        body, name="att_bwd", grid=(N_HP, nk),
        in_specs=[res, pl.BlockSpec((bk, LANES), lambda h, j: (j, N_HP + h)),
                  pl.BlockSpec((bk, LANES), lambda h, j: (j, 2 * N_HP + h)), ck_spec, res, res] + [anyspec] * n,
        out_specs=[pl.BlockSpec((LANES, s), lambda h, j: (h, 0)), kout, kout, ck_spec,
                   pl.BlockSpec((1, HP, nq, bq), lambda h, j: (h, 0, 0, 0))] + [anyspec] * n,
        out_shape=[jax.ShapeDtypeStruct((D_MODEL, s), F32), jax.ShapeDtypeStruct((s, D_MODEL), BF16),
                   jax.ShapeDtypeStruct((s, D_MODEL), BF16), jax.ShapeDtypeStruct((N_HP, HP, nk, bk), F32),
                   jax.ShapeDtypeStruct((N_HP, HP, nq, bq), F32)] + _comm_out_shapes(exchange_parts, True),
        scratch_shapes=(_comm_sems(n) if n else [])
        + [pltpu.VMEM((HP, ATT_HEAD_DIM + _PAD, bk), F32), pltpu.VMEM((HP, ATT_HEAD_DIM, bk), F32)],
        compiler_params=_cp("arbitrary", "arbitrary"),
    )(qkv, qkv, qkv, ck4, st, do_b, *exchange_parts)
    return outs[:5], list(outs[5:])


def _silu_and_grad(x):
    sg = _sigmoid(x)
    return x * sg, sg * (1.0 + x * (1.0 - sg))


SUBLANES = 8


def _conv_taps(cur, before, w_rows, bias):
    n, c = cur.shape
    cur3 = cur.reshape(n // SUBLANES, SUBLANES, c)
    sub = lax.broadcasted_iota(jnp.int32, (1, SUBLANES, c), 1)
    taps = []
    for k in range(SSM_CONV):
        sh = SSM_CONV - 1 - k
        if sh == 0:
            taps.append(cur3)
            continue
        rot = pltpu.roll(cur3, sh, 1)
        prev = jnp.concatenate([pltpu.roll(before, sh, 0)[None], rot[:-1]], axis=0)
        taps.append(jnp.where(sub < sh, prev, rot))
    pre = bias[None] + sum(w_rows[k][None] * taps[k] for k in range(SSM_CONV))
    return pre.reshape(n, c), [t.reshape(n, c) for t in taps]


def _conv_col_chunks(w_ref, b_ref, bc, cc):
    for c0 in range(0, bc, cc):
        cols = slice(c0, c0 + cc)
        yield cols, [w_ref[k:k + 1, cols] for k in range(SSM_CONV)], None if b_ref is None else b_ref[:, cols]


def _conv_specs(s, bs, bc):
    cur = pl.BlockSpec((bs, bc), lambda j, i: (i, j))
    halo = pl.BlockSpec((8, bc), lambda j, i: (jnp.maximum(i * (bs // 8) - 1, 0), j))
    w = pl.BlockSpec((SSM_CONV, bc), lambda j, i: (0, j))
    b = pl.BlockSpec((1, bc), lambda j, i: (0, j))
    return cur, halo, w, b


def _conv_fwd(xbc, w, b):
    s, c = xbc.shape
    bs, bc = _tile(s, 512), 1024
    rc, cc = bs, 128

    def body(x_ref, h_ref, w_ref, b_ref, o_ref):
        first = pl.program_id(1) == 0
        for cols, w_rows, bias in _conv_col_chunks(w_ref, b_ref, bc, cc):
            def step(r, before):
                rows = pl.ds(pl.multiple_of(r * rc, rc), rc)
                cur = x_ref[rows, cols]
                pre, _ = _conv_taps(cur, before, w_rows, bias)
                o_ref[rows, cols] = pre * _sigmoid(pre)
                return cur[rc - SUBLANES:]

            lax.fori_loop(0, bs // rc, step, jnp.where(first, 0.0, h_ref[:, cols]))

    cur, halo, ws, bsp = _conv_specs(s, bs, bc)
    return pl.pallas_call(body, name="conv_fwd", grid=(c // bc, s // bs), in_specs=[cur, halo, ws, bsp],
                          out_specs=cur, out_shape=jax.ShapeDtypeStruct((s, c), F32),
                          compiler_params=_cp("parallel", "parallel"))(xbc, xbc, w, b)


def _conv_bwd_pre(xbc, w, b, dact):
    s, c = xbc.shape
    bs, bc = _tile(s, 512), 1024
    rc, cc = _tile(bs, 256), 128

    def body(x_ref, h_ref, w_ref, b_ref, g_ref, dp_ref, dw_ref, db_ref):
        first = pl.program_id(1) == 0

        @pl.when(first)
        def _():
            dw_ref[...] = jnp.zeros_like(dw_ref)
            db_ref[...] = jnp.zeros_like(db_ref)

        for cols, w_rows, bias in _conv_col_chunks(w_ref, b_ref, bc, cc):
            def step(r, carry):
                before, sums = carry
                rows = pl.ds(pl.multiple_of(r * rc, rc), rc)
                cur = x_ref[rows, cols]
                pre, taps = _conv_taps(cur, before, w_rows, bias)
                dpre = g_ref[rows, cols].astype(F32) * _silu_and_grad(pre)[1]
                dp_ref[rows, cols] = dpre.astype(dp_ref.dtype)
                terms = [dpre * t for t in taps] + [dpre]
                sums = tuple(a + jnp.sum(t.reshape(rc // SUBLANES, SUBLANES, cc), axis=0) for a, t in zip(sums, terms))
                return cur[rc - SUBLANES:], sums

            zero = jnp.zeros((SUBLANES, cc), F32)
            _, sums = lax.fori_loop(0, bs // rc, step,
                                    (jnp.where(first, 0.0, h_ref[:, cols]), (zero,) * (SSM_CONV + 1)))
            for k in range(SSM_CONV):
                dw_ref[k:k + 1, cols] += jnp.sum(sums[k], axis=0, keepdims=True)
            db_ref[:, cols] += jnp.sum(sums[SSM_CONV], axis=0, keepdims=True)

    cur, halo, ws, bsp = _conv_specs(s, bs, bc)
    return pl.pallas_call(
        body, name="conv_bwd_pre", grid=(c // bc, s // bs), in_specs=[cur, halo, ws, bsp, cur],
        out_specs=[cur, ws, bsp],
        out_shape=[jax.ShapeDtypeStruct((s, c), BF16), jax.ShapeDtypeStruct((SSM_CONV, c), F32),
                   jax.ShapeDtypeStruct((1, c), F32)],
        compiler_params=_cp("parallel", "arbitrary"))(xbc, xbc, w, b, dact)


def _conv_bwd_in(dpre, w):
    s, c = dpre.shape
    bs, bc = _tile(s, 512), 1024
    nb = s // bs
    rc, cc = bs, 256
    nr = bs // rc

    def body(g_ref, n_ref, w_ref, o_ref):
        last = pl.program_id(1) == nb - 1
        sub = lax.broadcasted_iota(jnp.int32, (1, SUBLANES, cc), 1)
        for cols, w_rows, _ in _conv_col_chunks(w_ref, None, bc, cc):
            def step(i, after):
                rows = pl.ds(pl.multiple_of((nr - 1 - i) * rc, rc), rc)
                cur = g_ref[rows, cols].astype(F32)
                cur3 = cur.reshape(rc // SUBLANES, SUBLANES, cc)
                acc = w_rows[SSM_CONV - 1][None] * cur3
                for sh in range(1, SSM_CONV):
                    rot = pltpu.roll(cur3, SUBLANES - sh, 1)
                    nxt = jnp.concatenate([rot[1:], pltpu.roll(after, SUBLANES - sh, 0)[None]], axis=0)
                    acc = acc + w_rows[SSM_CONV - 1 - sh][None] * jnp.where(sub >= SUBLANES - sh, nxt, rot)
                o_ref[rows, cols] = acc.reshape(rc, cc).astype(o_ref.dtype)
                return cur[0:SUBLANES]

            lax.fori_loop(0, nr, step, jnp.where(last, 0.0, n_ref[0:SUBLANES, cols].astype(F32)))

    cur = pl.BlockSpec((bs, bc), lambda j, i: (i, j))
    nxt = pl.BlockSpec((16, bc), lambda j, i: (jnp.minimum((i + 1) * (bs // 16), s // 16 - 1), j))
    ws = pl.BlockSpec((SSM_CONV, bc), lambda j, i: (0, j))
    return pl.pallas_call(body, name="conv_bwd_in", grid=(c // bc, nb), in_specs=[cur, nxt, ws], out_specs=cur,
                          out_shape=jax.ShapeDtypeStruct((s, c), BF16),
                          compiler_params=_cp("parallel", "parallel"))(dpre, dpre, w)


def _dotT(a, b):
    return lax.dot_general(a.astype(BF16), b.astype(BF16), (((1,), (1,)), ((), ())), preferred_element_type=F32)


def _dot(a, b):
    return jnp.dot(a.astype(BF16), b.astype(BF16), preferred_element_type=F32)


N_PAIR = SSM_HEADS // HP
PAIRS_PER_GROUP = SSM_HEADS_PER_GROUP // HP


def _pair_consts():
    L = SSM_CHUNK
    lane = lax.broadcasted_iota(jnp.int32, (L, LANES), 1)
    lane1 = lax.broadcasted_iota(jnp.int32, (1, LANES), 1)
    li = lax.broadcasted_iota(jnp.int32, (L, L), 0)
    si = lax.broadcasted_iota(jnp.int32, (L, L), 1)
    return lane >= ATT_HEAD_DIM, lane1 >= ATT_HEAD_DIM, li, si


def _ssd_pair_fwd(xbc_act, ac_c, dt_r, ac_r, dsk_pair):
    s = xbc_act.shape[0]
    L, N, G = SSM_CHUNK, SSM_STATE, SSM_GROUPS
    nc = s // L

    def body(xbc_ref, ac_ref, dtr_ref, acr_ref, dsk_ref, y_ref, hp_ref, st_ref):
        @pl.when(pl.program_id(0) == 0)
        def _():
            st_ref[...] = jnp.zeros_like(st_ref)

        upper, up1, li, si = _pair_consts()
        for g in range(G):
            b_g = xbc_ref[:, SSM_INNER + g * N:SSM_INNER + (g + 1) * N]
            c_g = xbc_ref[:, SSM_INNER + G * N + g * N:SSM_INNER + G * N + (g + 1) * N]
            cb = _dotT(c_g, b_g)
            b_t = b_g.T
            for q in range(PAIRS_PER_GROUP):
                pp = g * PAIRS_PER_GROUP + q
                cols = slice(pp * LANES, (pp + 1) * LANES)
                xs = xbc_ref[:, cols]
                ht = st_ref[pp]
                hp_ref[0, pp] = ht
                y = dsk_ref[pp:pp + 1, :] * xs
                s_new = jnp.zeros((N, LANES), F32)
                ea, el = [], []
                for a in range(HP):
                    h = HP * pp + a
                    acol = jnp.broadcast_to(ac_ref[:, h:h + 1], (L, LANES))
                    arow, dtrow = acr_ref[h:h + 1, :], dtr_ref[h:h + 1, :]
                    alast = ac_ref[L - 1:L, h:h + 1]
                    decay = jnp.exp(jnp.where(li >= si, acol - arow, NEG))
                    xs_a = jnp.where(upper == (a == 1), xs, 0.0)
                    y = y + _dot(cb * decay * dtrow, xs_a)
                    s_new = s_new + _dot(b_t * (dtrow * jnp.exp(alast - arow)), xs_a)
                    ea.append(jnp.exp(acol))
                    el.append(jnp.exp(alast))
                y_ref[:, cols] = y + jnp.where(upper, ea[1], ea[0]) * _dot(c_g, ht)
                st_ref[pp] = ht * jnp.where(up1, el[1], el[0]) + s_new

    col = pl.BlockSpec((L, LANES), lambda c: (c, 0))
    row = pl.BlockSpec((LANES, L), lambda c: (0, c))
    return pl.pallas_call(
        body, name="ssd_fwd", grid=(nc,),
        in_specs=[pl.BlockSpec((L, SSM_CONV_DIM), lambda c: (c, 0)), col, row, row,
                  pl.BlockSpec((N_PAIR, LANES), lambda c: (0, 0))],
        out_specs=[pl.BlockSpec((L, SSM_INNER), lambda c: (c, 0)),
                   pl.BlockSpec((1, N_PAIR, N, LANES), lambda c: (c, 0, 0, 0))],
        out_shape=[jax.ShapeDtypeStruct((s, SSM_INNER), F32), jax.ShapeDtypeStruct((nc, N_PAIR, N, LANES), F32)],
        scratch_shapes=[pltpu.VMEM((N_PAIR, N, LANES), F32)],
        compiler_params=_cp("arbitrary"),
    )(xbc_act, ac_c, dt_r, ac_r, dsk_pair)


def _ssd_pair_bwd(xbc_act, dt_c, ac_c, dt_r, ac_r, hprev_all, dy, dsk_pair, a_row):
    s = xbc_act.shape[0]
    L, N, G = SSM_CHUNK, SSM_STATE, SSM_GROUPS
    nc = s // L
    rev = lambda c: nc - 1 - c

    def body(xbc_ref, dt_ref, ac_ref, dtr_ref, acr_ref, hp_ref, dy_ref, dsk_ref, arow_ref,
             dx_ref, ddt_ref, da_ref, dds_ref, dh_ref):
        @pl.when(pl.program_id(0) == 0)
        def _():
            dh_ref[...] = jnp.zeros_like(dh_ref)
            da_ref[...] = jnp.zeros_like(da_ref)
            dds_ref[...] = jnp.zeros_like(dds_ref)

        upper, up1, li, si = _pair_consts()
        lane = lax.broadcasted_iota(jnp.int32, (L, LANES), 1)
        sub = lax.broadcasted_iota(jnp.int32, (LANES, L), 0)
        lastrow = lax.broadcasted_iota(jnp.int32, (L, LANES), 0) == L - 1
        da_c = jnp.zeros((L, LANES), F32)
        da_r = jnp.zeros((LANES, L), F32)
        ddt_r = jnp.zeros((LANES, L), F32)
        for g in range(G):
            b_g = xbc_ref[:, SSM_INNER + g * N:SSM_INNER + (g + 1) * N]
            c_g = xbc_ref[:, SSM_INNER + G * N + g * N:SSM_INNER + G * N + (g + 1) * N]
            cb, cb_t = _dotT(c_g, b_g), _dotT(b_g, c_g)
            b_t, c_t = b_g.T, c_g.T
            dcb = jnp.zeros((L, L), F32)
            db_t = jnp.zeros((N, L), F32)
            dc = jnp.zeros((L, N), F32)
            for q in range(PAIRS_PER_GROUP):
                pp = g * PAIRS_PER_GROUP + q
                cols = slice(pp * LANES, (pp + 1) * LANES)
                xs, gy = xbc_ref[:, cols], dy_ref[:, cols].astype(F32)
                ht, dhn = hp_ref[0, pp], dh_ref[pp]
                acol = [jnp.broadcast_to(ac_ref[:, HP * pp + a:HP * pp + a + 1], (L, LANES)) for a in range(HP)]
                alast = [ac_ref[L - 1:L, HP * pp + a:HP * pp + a + 1] for a in range(HP)]
                ea = jnp.where(upper, jnp.exp(acol[1]), jnp.exp(acol[0]))
                el = jnp.where(up1, jnp.exp(alast[1]), jnp.exp(alast[0]))
                ge = gy * ea
                dc = dc + _dotT(ge, ht)
                dh_ref[pp] = _dot(c_t, ge) + dhn * el
                t_off = (ge * _dot(c_g, ht)).astype(BF16)
                hsum = jnp.sum(dhn * ht, axis=0, keepdims=True)
                dxs = dsk_ref[pp:pp + 1, :] * gy
                dds_ref[pp:pp + 1, :] += jnp.sum(gy * xs, axis=0, keepdims=True)
                for a in range(HP):
                    h = HP * pp + a
                    mine, mine1 = upper == (a == 1), up1 == (a == 1)
                    arow, dtrow = acr_ref[h:h + 1, :], dtr_ref[h:h + 1, :]
                    dtcol = jnp.broadcast_to(dt_ref[:, h:h + 1], (L, LANES))
                    xs_a, gy_a = jnp.where(mine, xs, 0.0), jnp.where(mine, gy, 0.0)
                    dhn_a = jnp.where(mine1, dhn, 0.0)
                    e_row = jnp.exp(alast[a] - arow)
                    w_row = dtrow * e_row
                    xd_t = _dotT(dhn_a, xs_a)
                    db_t = db_t + xd_t * w_row
                    dw = jnp.sum(b_t * xd_t, axis=0, keepdims=True)
                    de_e = dw * w_row
                    dal = (jnp.sum(jnp.where(mine1, hsum, 0.0), axis=1, keepdims=True) * jnp.exp(alast[a])
                           + jnp.sum(de_e, axis=1, keepdims=True))
                    dxs = dxs + _dot(b_g, dhn_a) * (dtcol * jnp.exp(alast[a] - acol[a]))
                    decay = jnp.exp(jnp.where(li >= si, acol[a] - arow, NEG))
                    decay_t = jnp.exp(jnp.where(si >= li, arow - acol[a], NEG))
                    m = cb * decay
                    dmdt = _dotT(gy_a, xs_a)
                    dxs = dxs + _dot(cb_t * decay_t * dtcol, gy_a)
                    dm = dmdt * dtrow
                    dcb = dcb + dm * decay
                    wb = (dm * m).astype(BF16)
                    onehot = jnp.where(lane == h, 1.0, 0.0).astype(BF16)
                    da_c = (da_c + jnp.dot(wb, onehot, preferred_element_type=F32)
                            + jnp.dot(jnp.where(mine, t_off, 0.0).astype(BF16), onehot, preferred_element_type=F32)
                            + jnp.where(lastrow & (lane == h), dal, 0.0))
                    da_r = jnp.where(sub == h, -(jnp.sum(wb.astype(F32), axis=0, keepdims=True) + de_e), da_r)
                    ddt_r = jnp.where(sub == h, dw * e_row + jnp.sum(dmdt * m, axis=0, keepdims=True), ddt_r)
                dx_ref[:, cols] = dxs.astype(dx_ref.dtype)
            dx_ref[:, SSM_INNER + g * N:SSM_INNER + (g + 1) * N] = (db_t + _dot(c_t, dcb)).T.astype(dx_ref.dtype)
            dx_ref[:, SSM_INNER + G * N + g * N:SSM_INNER + G * N + (g + 1) * N] = (
                dc + _dot(dcb, b_g)).astype(dx_ref.dtype)
        dda = jnp.dot(_tri(L, lower=False), da_c + da_r.T, precision=HI, preferred_element_type=F32)
        ddt_ref[...] = dda * arow_ref[...] + ddt_r.T
        da_ref[...] += jnp.sum(dda * dt_ref[...], axis=0, keepdims=True)

    col = pl.BlockSpec((L, LANES), lambda c: (rev(c), 0))
    row = pl.BlockSpec((LANES, L), lambda c: (0, rev(c)))
    vec = pl.BlockSpec((1, LANES), lambda c: (0, 0))
    pairs = pl.BlockSpec((N_PAIR, LANES), lambda c: (0, 0))
    return pl.pallas_call(
        body, name="ssd_bwd", grid=(nc,),
        in_specs=[pl.BlockSpec((L, SSM_CONV_DIM), lambda c: (rev(c), 0)), col, col, row, row,
                  pl.BlockSpec((1, N_PAIR, N, LANES), lambda c: (rev(c), 0, 0, 0)),
                  pl.BlockSpec((L, SSM_INNER), lambda c: (rev(c), 0)), pairs, vec],
        out_specs=[pl.BlockSpec((L, SSM_CONV_DIM), lambda c: (rev(c), 0)), col, vec, pairs],
        out_shape=[jax.ShapeDtypeStruct((s, SSM_CONV_DIM), BF16), jax.ShapeDtypeStruct((s, LANES), F32),
                   jax.ShapeDtypeStruct((1, LANES), F32), jax.ShapeDtypeStruct((N_PAIR, LANES), F32)],
        scratch_shapes=[pltpu.VMEM((N_PAIR, N, LANES), F32)],
        compiler_params=_cp("arbitrary"),
    )(xbc_act, dt_c, ac_c, dt_r, ac_r, hprev_all, dy, dsk_pair, a_row)


ROWS = 512
GW = SSM_INNER // SSM_GROUPS


def _rows(width, dtype=F32):
    return pl.BlockSpec((ROWS, width), lambda i: (i, 0))


def _vec(width):
    return pl.BlockSpec((1, width), lambda i: (0, 0))


def _gnorm_fwd(y, z, w):
    s = y.shape[0]

    def body(y_ref, z_ref, w_ref, o_ref):
        for g in range(SSM_GROUPS):
            cs = slice(g * GW, (g + 1) * GW)
            zz = z_ref[:, cs].astype(F32)
            u = y_ref[:, cs] * (zz * _sigmoid(zz))
            r = lax.rsqrt(jnp.mean(u * u, axis=1, keepdims=True) + RMS_EPS)
            o_ref[:, cs] = (u * r * w_ref[:, cs]).astype(o_ref.dtype)

    return pl.pallas_call(body, name="gnorm_fwd", grid=(s // ROWS,),
                          in_specs=[_rows(SSM_INNER), _rows(SSM_INNER), _vec(SSM_INNER)], out_specs=_rows(SSM_INNER),
                          out_shape=jax.ShapeDtypeStruct((s, SSM_INNER), BF16), compiler_params=_cp("parallel"))(y, z, w)


def _gnorm_bwd(y, z, w, do):
    s = y.shape[0]

    def body(y_ref, z_ref, w_ref, do_ref, dy_ref, dz_ref, dw_ref):
        @pl.when(pl.program_id(0) == 0)
        def _():
            dw_ref[...] = jnp.zeros_like(dw_ref)

        for g in range(SSM_GROUPS):
            cs = slice(g * GW, (g + 1) * GW)
            zz, yy, dd = z_ref[:, cs].astype(F32), y_ref[:, cs], do_ref[:, cs].astype(F32)
            sz, dsz = _silu_and_grad(zz)
            u = yy * sz
            r = lax.rsqrt(jnp.mean(u * u, axis=1, keepdims=True) + RMS_EPS)
            n = u * r
            dn = dd * w_ref[:, cs]
            dw_ref[:, cs] += jnp.sum(dd * n, axis=0, keepdims=True)
            du = r * (dn - n * jnp.mean(dn * n, axis=1, keepdims=True))
            dy_ref[:, cs] = (du * sz).astype(dy_ref.dtype)
            dz_ref[:, cs] = (du * yy * dsz).astype(dz_ref.dtype)

    return pl.pallas_call(
        body, name="gnorm_bwd", grid=(s // ROWS,),
        in_specs=[_rows(SSM_INNER), _rows(SSM_INNER), _vec(SSM_INNER), _rows(SSM_INNER)],
        out_specs=[_rows(SSM_INNER), _rows(SSM_INNER), _vec(SSM_INNER)],
        out_shape=[jax.ShapeDtypeStruct((s, SSM_INNER), BF16), jax.ShapeDtypeStruct((s, SSM_INNER), BF16),
                   jax.ShapeDtypeStruct((1, SSM_INNER), F32)],
        compiler_params=_cp("arbitrary"))(y, z, w, do)


def _mix_fwd(gl, bg, attn_d, ssm_d):
    s = gl.shape[0]
    d = D_MODEL

    def body(gl_ref, bg_ref, a_ref, m_ref, o_ref):
        g0 = _sigmoid(gl_ref[:, :d] + bg_ref[:, :d])
        g1 = _sigmoid(gl_ref[:, d:] + bg_ref[:, d:])
        o_ref[...] = (g0 * a_ref[...] + g1 * m_ref[...]).astype(o_ref.dtype)

    return pl.pallas_call(body, name="mix_fwd", grid=(s // ROWS,),
                          in_specs=[_rows(2 * d), _vec(2 * d), _rows(d), _rows(d)], out_specs=_rows(d),
                          out_shape=jax.ShapeDtypeStruct((s, d), BF16), compiler_params=_cp("parallel"))(
        gl, bg, attn_d, ssm_d)


def _mix_bwd(gl, bg, attn_d, ssm_d, dmix):
    s = gl.shape[0]
    d = D_MODEL

    def body(gl_ref, bg_ref, a_ref, m_ref, dm_ref, da_ref, ds_ref, dg_ref, db_ref):
        @pl.when(pl.program_id(0) == 0)
        def _():
            db_ref[...] = jnp.zeros_like(db_ref)

        g0 = _sigmoid(gl_ref[:, :d] + bg_ref[:, :d])
        g1 = _sigmoid(gl_ref[:, d:] + bg_ref[:, d:])
        dm = dm_ref[...].astype(F32)
        da_ref[...] = (dm * g0).astype(da_ref.dtype)
        ds_ref[...] = (dm * g1).astype(ds_ref.dtype)
        dl0 = dm * a_ref[...] * g0 * (1.0 - g0)
        dl1 = dm * m_ref[...] * g1 * (1.0 - g1)
        dg_ref[:, :d] = dl0.astype(dg_ref.dtype)
        dg_ref[:, d:] = dl1.astype(dg_ref.dtype)
        db_ref[:, :d] += jnp.sum(dl0, axis=0, keepdims=True)
        db_ref[:, d:] += jnp.sum(dl1, axis=0, keepdims=True)

    return pl.pallas_call(
        body, name="mix_bwd", grid=(s // ROWS,),
        in_specs=[_rows(2 * d), _vec(2 * d), _rows(d), _rows(d), _rows(d)],
        out_specs=[_rows(d), _rows(d), _rows(2 * d), _vec(2 * d)],
        out_shape=[jax.ShapeDtypeStruct((s, d), BF16), jax.ShapeDtypeStruct((s, d), BF16),
                   jax.ShapeDtypeStruct((s, 2 * d), BF16), jax.ShapeDtypeStruct((1, 2 * d), F32)],
        compiler_params=_cp("arbitrary"))(gl, bg, attn_d, ssm_d, dmix)


def _ln_stats(p):
    mu = jnp.mean(p, axis=1, keepdims=True)
    c = p - mu
    rstd = lax.rsqrt(jnp.mean(c * c, axis=1, keepdims=True) + LN_EPS)
    return c * rstd, rstd


def _ln_bwd(dy, xhat, rstd, g):
    dxh = dy * g
    return rstd * (dxh - jnp.mean(dxh, axis=1, keepdims=True) - xhat * jnp.mean(dxh * xhat, axis=1, keepdims=True))


def _ln1_fwd(x, mixed, g, b):
    s, d = x.shape

    def body(x_ref, m_ref, g_ref, b_ref, o_ref, ob_ref):
        xhat, _ = _ln_stats(DEEPNORM_ALPHA * x_ref[...] + m_ref[...])
        y = xhat * g_ref[...] + b_ref[...]
        o_ref[...] = y
        ob_ref[...] = y.astype(BF16)

    return pl.pallas_call(body, name="ln1_fwd", grid=(s // ROWS,), in_specs=[_rows(d), _rows(d), _vec(d), _vec(d)],
                          out_specs=[_rows(d), _rows(d)],
                          out_shape=[jax.ShapeDtypeStruct((s, d), F32), jax.ShapeDtypeStruct((s, d), BF16)],
                          compiler_params=_cp("parallel"))(x, mixed, g, b)


def _ln2_loss(x1, act, w_down, target, g, b):
    s, d = x1.shape
    assert act.dtype == BF16 and w_down.dtype == BF16

    def body(x_ref, a_ref, w_ref, t_ref, g_ref, b_ref, dp_ref, dpb_ref, loss_ref, dg_ref, db_ref):
        @pl.when(pl.program_id(0) == 0)
        def _():
            loss_ref[...] = jnp.zeros_like(loss_ref)
            dg_ref[...] = jnp.zeros_like(dg_ref)
            db_ref[...] = jnp.zeros_like(db_ref)

        h = jnp.dot(a_ref[...], w_ref[...], preferred_element_type=F32)
        xhat, rstd = _ln_stats(DEEPNORM_ALPHA * x_ref[...] + h)
        err = xhat * g_ref[...] + b_ref[...] - t_ref[...]
        part = 0.5 * jnp.sum(jnp.mean(err * err, axis=1, keepdims=True), axis=0, keepdims=True)
        loss_ref[...] += jnp.broadcast_to(part, loss_ref.shape)
        dy = err * (1.0 / d)
        dg_ref[...] += jnp.sum(dy * xhat, axis=0, keepdims=True)
        db_ref[...] += jnp.sum(dy, axis=0, keepdims=True)
        dp = _ln_bwd(dy, xhat, rstd, g_ref[...])
        dp_ref[...] = dp
        dpb_ref[...] = dp.astype(BF16)

    return pl.pallas_call(
        body, name="ln2_loss", grid=(s // ROWS,),
        in_specs=[_rows(d), _rows(act.shape[1]), pl.BlockSpec(w_down.shape, lambda i: (0, 0)), _rows(d), _vec(d), _vec(d)],
        out_specs=[_rows(d), _rows(d), _vec(LANES), _vec(d), _vec(d)],
        out_shape=[jax.ShapeDtypeStruct((s, d), F32), jax.ShapeDtypeStruct((s, d), BF16),
                   jax.ShapeDtypeStruct((1, LANES), F32),
                   jax.ShapeDtypeStruct((1, d), F32), jax.ShapeDtypeStruct((1, d), F32)],
        compiler_params=_cp("arbitrary"))(x1, act, w_down, target, g, b)


def _ln1_bwd(x, mixed, g, dpre2, dgu, w_gu):
    s, d = x.shape
    assert dgu.dtype == BF16 and w_gu.dtype == BF16
    rb = ROWS // 2
    _rows = lambda width: pl.BlockSpec((rb, width), lambda i: (i, 0))

    def body(x_ref, m_ref, g_ref, d2_ref, du_ref, w_ref, dp_ref, dr_ref, dg_ref, db_ref):
        @pl.when(pl.program_id(0) == 0)
        def _():
            dg_ref[...] = jnp.zeros_like(dg_ref)
            db_ref[...] = jnp.zeros_like(db_ref)

        xhat, rstd = _ln_stats(DEEPNORM_ALPHA * x_ref[...] + m_ref[...])
        dffn = lax.dot_general(du_ref[...], w_ref[...], (((1,), (1,)), ((), ())), preferred_element_type=F32)
        dy = DEEPNORM_ALPHA * d2_ref[...] + dffn
        dg_ref[...] += jnp.sum(dy * xhat, axis=0, keepdims=True)
        db_ref[...] += jnp.sum(dy, axis=0, keepdims=True)
        dp = _ln_bwd(dy, xhat, rstd, g_ref[...])
        dp_ref[...] = dp.astype(BF16)
        dr_ref[...] = DEEPNORM_ALPHA * dp

    return pl.pallas_call(
        body, name="ln1_bwd", grid=(s // rb,),
        in_specs=[_rows(d), _rows(d), _vec(d), _rows(d), _rows(dgu.shape[1]), pl.BlockSpec(w_gu.shape, lambda i: (0, 0))],
        out_specs=[_rows(d), _rows(d), _vec(d), _vec(d)],
        out_shape=[jax.ShapeDtypeStruct((s, d), BF16), jax.ShapeDtypeStruct((s, d), F32),
                   jax.ShapeDtypeStruct((1, d), F32), jax.ShapeDtypeStruct((1, d), F32)],
        compiler_params=_cp("arbitrary"))(x, mixed, g, dpre2, dgu, w_gu)


def _swiglu_fwd(gu):
    s = gu.shape[0]
    f = FFN_HIDDEN

    def body(g_ref, u_ref, o_ref):
        gg = g_ref[...].astype(F32)
        o_ref[...] = (gg * _sigmoid(gg) * u_ref[...].astype(F32)).astype(o_ref.dtype)

    return pl.pallas_call(
        body, name="swiglu_fwd", grid=(s // ROWS,),
        in_specs=[pl.BlockSpec((ROWS, f), lambda i: (i, 0)), pl.BlockSpec((ROWS, f), lambda i: (i, 1))],
        out_specs=_rows(f), out_shape=jax.ShapeDtypeStruct((s, f), BF16), compiler_params=_cp("parallel"))(gu, gu)


def _swiglu_bwd(gu, dact):
    s = gu.shape[0]
    f = FFN_HIDDEN

    def body(g_ref, u_ref, d_ref, o_ref):
        sg, dsg = _silu_and_grad(g_ref[...].astype(F32))
        dd = d_ref[...].astype(F32)
        o_ref[:, :f] = (dd * u_ref[...].astype(F32) * dsg).astype(o_ref.dtype)
        o_ref[:, f:] = (dd * sg).astype(o_ref.dtype)

    return pl.pallas_call(
        body, name="swiglu_bwd", grid=(s // ROWS,),
        in_specs=[pl.BlockSpec((ROWS, f), lambda i: (i, 0)), pl.BlockSpec((ROWS, f), lambda i: (i, 1)), _rows(f)],
        out_specs=_rows(2 * f), out_shape=jax.ShapeDtypeStruct((s, 2 * f), BF16),
        compiler_params=_cp("parallel"))(gu, gu, dact)


def _peer(k):
    x, y, c = lax.axis_index("x"), lax.axis_index("y"), lax.axis_index("c")
    kx, ky, kc = (k >> 2) & 1, (k >> 1) & 1, k & 1
    px = (1 - x) if kx else x
    py = (1 - y) if ky else y
    pc = (1 - c) if kc else c
    return (px, py, pc), 4 * px + 2 * py + pc


def _my_index():
    return 4 * lax.axis_index("x") + 2 * lax.axis_index("y") + lax.axis_index("c")


def _comm_copies(ins, outs, sems, scatter):
    send_sems, recv_sems, local_sems = sems
    me = _my_index()
    copies = [pltpu.make_async_copy(ins[t].at[me] if scatter else ins[t], outs[t].at[me], local_sems.at[t])
              for t in range(len(ins))]
    for k in range(1, N_DEV):
        peer, pidx = _peer(k)
        for t in range(len(ins)):
            copies.append(pltpu.make_async_remote_copy(
                src_ref=ins[t].at[pidx] if scatter else ins[t], dst_ref=outs[t].at[me],
                send_sem=send_sems.at[t, k - 1], recv_sem=recv_sems.at[t, k - 1], device_id=peer,
                device_id_type=pl.DeviceIdType.MESH))
    return copies


def _comm_sems(n):
    return [pltpu.SemaphoreType.DMA((n, N_DEV - 1)), pltpu.SemaphoreType.DMA((n, N_DEV - 1)),
            pltpu.SemaphoreType.DMA((n,))]


def _comm_out_shapes(parts, scatter):
    return [jax.ShapeDtypeStruct(p.shape if scatter else (N_DEV,) + p.shape, p.dtype) for p in parts]


def _all_gather(parts):
    n = len(parts)

    def body(*refs):
        ins, outs = refs[:n], refs[n:2 * n]
        send_sems, recv_sems, local_sems = refs[2 * n:]
        x, y, c = lax.axis_index("x"), lax.axis_index("y"), lax.axis_index("c")
        me, sibling = (x, y, c), (x, y, 1 - c)
        chips = [(1 - x, y), (x, 1 - y), (1 - x, 1 - y)]

        def copy(t, k, block, to, src=None):
            dst = outs[t].at[4 * block[0] + 2 * block[1] + block[2]]
            return pltpu.make_async_remote_copy(
                src_ref=dst if src is None else src, dst_ref=dst, send_sem=send_sems.at[t, k],
                recv_sem=recv_sems.at[t, k], device_id=to, device_id_type=pl.DeviceIdType.MESH)

        mine = [pltpu.make_async_copy(ins[t], outs[t].at[_my_index()], local_sems.at[t]) for t in range(n)]
        for cp in mine:
            cp.start()
        first = [copy(t, 0, me, sibling, src=ins[t]) for t in range(n)]
        first += [copy(t, 1 + j, me, (*chip, c), src=ins[t]) for j, chip in enumerate(chips) for t in range(n)]
        for cp in first:
            cp.start()
        passed = []
        for j, chip in enumerate(chips):
            for t in range(n):
                copy(t, 1 + j, (*chip, c), me).wait_recv()
                passed.append(copy(t, 4 + j, (*chip, c), sibling))
                passed[-1].start()
        for t in range(n):
            copy(t, 0, sibling, me).wait_recv()
            for j, chip in enumerate(chips):
                copy(t, 4 + j, (*chip, 1 - c), me).wait_recv()
        for cp in first + passed:
            cp.wait_send()
        for cp in mine:
            cp.wait()

    anyspec = pl.BlockSpec(memory_space=pl.ANY)
    return pl.pallas_call(body, name="all_gather", in_specs=[anyspec] * n, out_specs=[anyspec] * n,
                          out_shape=_comm_out_shapes(parts, False), scratch_shapes=_comm_sems(n))(*parts)


def _remote_scatter_copies(ins, lands, send_sems, recv_sems):
    me = _my_index()
    copies = []
    for k in range(1, N_DEV):
        peer, pidx = _peer(k)
        for t in range(len(ins)):
            copies.append(pltpu.make_async_remote_copy(
                src_ref=ins[t].at[pidx], dst_ref=lands[t].at[me], send_sem=send_sems.at[t * (N_DEV - 1) + k - 1],
                recv_sem=recv_sems.at[t * (N_DEV - 1) + k - 1], device_id=peer, device_id_type=pl.DeviceIdType.MESH))
    return copies


def _landing_zones(parts):
    me = _my_index()
    return [jnp.where(lax.broadcasted_iota(jnp.int32, p.shape, 0) == me, p, jnp.zeros_like(p)) for p in parts]


_HBM = pl.BlockSpec(memory_space=pltpu.HBM)
_SEM = pl.BlockSpec(memory_space=pltpu.SEMAPHORE)


def _exchange_start(parts, lands):
    n = len(parts)

    def body(*refs):
        ins, lnd, send_sems, recv_sems, token = refs[:n], refs[n:2 * n], refs[2 * n], refs[2 * n + 1], refs[-1]
        for cp in _remote_scatter_copies(ins, lnd, send_sems, recv_sems):
            cp.start()
        token[...] = jnp.zeros_like(token)

    hbm = [pltpu.HBM(p.shape, p.dtype) for p in parts]
    outs = pl.pallas_call(
        body, name="exchange_start",
        out_shape=[pltpu.SemaphoreType.DMA((n * (N_DEV - 1),)), pltpu.SemaphoreType.DMA((n * (N_DEV - 1),))] + hbm + hbm
        + [jax.ShapeDtypeStruct((8, LANES), F32)],
        in_specs=[_HBM] * (2 * n), out_specs=[_SEM, _SEM] + [_HBM] * (2 * n) + [pl.BlockSpec(memory_space=pltpu.VMEM)],
        input_output_aliases={t: 2 + t for t in range(2 * n)},
        compiler_params=pltpu.CompilerParams(has_side_effects=pltpu.SideEffectType.DATAFLOW_SIDE_EFFECTING),
    )(*[pltpu.with_memory_space_constraint(p, pltpu.HBM) for p in list(parts) + list(lands)])
    return outs[0], outs[1], list(outs[2:2 + n]), list(outs[2 + n:2 + 2 * n]), outs[-1]


def _exchange_wait(send_sems, recv_sems, parts, lands, after):
    n = len(parts)

    def body(*refs):
        ins, lnd, send_sems, recv_sems = refs[:n], refs[n:2 * n], refs[2 * n], refs[2 * n + 1]
        for cp in _remote_scatter_copies(ins, lnd, send_sems, recv_sems):
            cp.wait_send()
            cp.wait_recv()

    hbm = [pltpu.HBM(p.shape, p.dtype) for p in parts]
    outs = pl.pallas_call(
        body, name="exchange_wait", out_shape=hbm + hbm,
        in_specs=[_HBM] * (2 * n) + [_SEM, _SEM, pl.BlockSpec(memory_space=pl.ANY)], out_specs=[_HBM] * (2 * n),
        input_output_aliases={t: t for t in range(2 * n)},
        compiler_params=pltpu.CompilerParams(has_side_effects=pltpu.SideEffectType.DATAFLOW_SIDE_EFFECTING),
    )(*parts, *lands, send_sems, recv_sems, after)
    return list(outs[n:])


def _adamw(recv, w, m, v, name):
    _, r, c = w.shape
    br = _tile(r, 128)
    c1 = 1.0 / (1.0 - ADAM_B1 ** ADAM_STEP)
    c2 = 1.0 / (1.0 - ADAM_B2 ** ADAM_STEP)

    def body(r_ref, w_ref, m_ref, v_ref, g_ref, d_ref, mo_ref, vo_ref):
        g = r_ref[0].astype(F32)
        for k in range(1, N_DEV):
            g = g + r_ref[k].astype(F32)
        mn = ADAM_B1 * m_ref[0] + (1.0 - ADAM_B1) * g
        vn = ADAM_B2 * v_ref[0] + (1.0 - ADAM_B2) * (g * g)
        g_ref[0] = g
        mo_ref[0] = mn
        vo_ref[0] = vn
        d_ref[0] = -ADAM_LR * ((mn * c1) / (jnp.sqrt(vn * c2) + ADAM_EPS) + ADAM_WD * w_ref[0])

    blk = pl.BlockSpec((1, br, c), lambda i: (0, i, 0))
    return pl.pallas_call(
        body, name=name, grid=(r // br,),
        in_specs=[pl.BlockSpec((N_DEV, br, c), lambda i: (0, i, 0)), blk, blk, blk],
        out_specs=[blk] * 4, out_shape=[jax.ShapeDtypeStruct((1, r, c), F32)] * 4,
        compiler_params=_cp("parallel"))(recv, w, m, v)


def _lane_row(pairs):
    row = jnp.zeros((LANES,), F32)
    for lane0, vec in pairs:
        row = lax.dynamic_update_slice(row, vec.astype(F32), (lane0,))
    return row.reshape(1, LANES)


def _stage_in(x, wts, small):
    s = x.shape[0]
    a = -jnp.exp(small["a_log"])
    bias_row = _lane_row([(DT_LANE0, small["dt_bias"]), (F_LANE0, small["b_forget"])])
    a_row = _lane_row([(DT_LANE0, a)])
    conv_b = small["conv_b"].reshape(1, -1)
    norm_w = small["ssm_norm_w"].reshape(1, -1)
    bg = small["b_gates"].reshape(1, -1)
    g1, b1 = small["ln1_g"].reshape(1, -1), small["ln1_b"].reshape(1, -1)
    g2, b2 = small["ln2_g"].reshape(1, -1), small["ln2_b"].reshape(1, -1)
    d_skip = small["d_skip"]
    xb = x.astype(BF16)

    qkv = _mm(xb, wts["qkv"], out_dtype=BF16, name="f_qkv")
    z = _mm(xb, wts["z"], out_dtype=BF16, name="f_z")
    xbc = _mm(xb, wts["xbc"], name="f_xbc")
    gl = _mm(xb, wts["gate"], out_dtype=BF16, name="f_gate")
    fd = _mm(xb, wts["fd"], name="f_fd")
    dt_c, ac_c, cf_c, dt_r, ac_r, cf_r = _stats_fwd(fd, bias_row, a_row)
    bk = _att_blocks(s)[1]
    ck4 = cf_r[F_LANE0:F_LANE0 + ATT_HEADS].reshape(N_HP, HP, s // bk, bk)
    return dict(locals())


def _stage_mid(c, attn, lse, wts, target):
    x, xb, qkv, z, xbc, gl, fd, ck4 = (c[k] for k in ("x", "xb", "qkv", "z", "xbc", "gl", "fd", "ck4"))
    dt_c, ac_c, dt_r, ac_r, a_row, bias_row = (c[k] for k in ("dt_c", "ac_c", "dt_r", "ac_r", "a_row", "bias_row"))
    conv_b, norm_w, bg, g1, b1, g2, b2, d_skip = (c[k] for k in ("conv_b", "norm_w", "bg", "g1", "b1", "g2", "b2",
                                                                "d_skip"))
    conv_w = c["wts"]["conv"]
    attn_d = _mm(attn, wts["pa"], out_dtype=BF16, name="f_pa")
    xact = _conv_fwd(xbc, conv_w, conv_b)
    dsk_pair = jnp.repeat(d_skip, SSM_HEAD_DIM).reshape(N_PAIR, LANES)
    y, hprev = _ssd_pair_fwd(xact, ac_c, dt_r, ac_r, dsk_pair)
    ssm = _gnorm_fwd(y, z, norm_w)
    ssm_d = _mm(ssm, wts["ps"], out_dtype=BF16, name="f_ps")
    mix = _mix_fwd(gl, bg, attn_d, ssm_d)
    mixed = _mm(mix, wts["out"], name="f_out")
    x1, x1_b = _ln1_fwd(x, mixed, g1, b1)
    gu = _mm(x1_b, wts["gu"], out_dtype=BF16, name="f_gu")
    act = _swiglu_fwd(gu)
    dpre2, dpre2_b, loss_row, dg2, db2 = _ln2_loss(x1, act, wts["down"], target, g2, b2)

    d_act = _mm(dpre2_b, wts["down"], tb=True, out_dtype=BF16, name="b_down_x")
    dw_down = _mm(act, dpre2_b, ta=True, name="b_down_w")
    dgu = _swiglu_bwd(gu, d_act)
    dw_gu = _mm(x1_b, dgu, ta=True, name="b_gu_w")
    dpre1, dxr, dg1, db1 = _ln1_bwd(x, mixed, g1, dpre2, dgu, wts["gu"])
    dmix = _mm(dpre1, wts["out"], tb=True, out_dtype=BF16, name="b_out_x")
    dw_out = _mm(mix, dpre1, ta=True, name="b_out_w")
    dattn_d, dssm_d, dgl, dbg = _mix_bwd(gl, bg, attn_d, ssm_d, dmix)
    dssm = _mm(dssm_d, wts["ps"], tb=True, out_dtype=BF16, name="b_ps_x")
    dw_ps = _mm(ssm, dssm_d, ta=True, name="b_ps_w")
    dw_pa = _mm(attn, dattn_d, ta=True, name="b_pa_w")
    dy, dz, dnw = _gnorm_bwd(y, z, norm_w, dssm)
    dxact, ddt, da_row, dds_pair = _ssd_pair_bwd(xact, dt_c, ac_c, dt_r, ac_r, hprev, dy, dsk_pair, a_row)
    dds = dds_pair.reshape(SSM_HEADS, SSM_HEAD_DIM).sum(axis=1)
    dpre_c, dconv_w, dconv_b = _conv_bwd_pre(xbc, conv_w, conv_b, dxact)
    dxbc = _conv_bwd_in(dpre_c, conv_w)
    st, do_b = _att_prep(dattn_d, wts["pa"], attn, lse)
    late = dict(pa=dw_pa, ps=dw_ps, out=dw_out, gu=dw_gu, down=dw_down)
    keep = ("st", "do_b", "ddt", "dxr", "dz", "dxbc", "dgl", "dconv_w", "dconv_b", "da_row", "dds", "dnw", "dbg",
            "dg1", "db1", "dg2", "db2", "loss_row")
    loc = locals()
    return {**c, **{k: loc[k] for k in keep}}, late


def _stage_out_w(c, att_grads):
    dq, dk, dv, dck, dcq = att_grads
    xb, fd, bias_row, ddt, dz, dxbc, dgl = (c[k] for k in ("xb", "fd", "bias_row", "ddt", "dz", "dxbc", "dgl"))
    s, a = xb.shape[0], c["a"]
    dcum = dck.reshape(ATT_HEADS, s) + dcq.reshape(ATT_HEADS, s)
    dfd, dbias = _stats_bwd(fd, bias_row, ddt, jnp.zeros((LANES, s), F32).at[F_LANE0:F_LANE0 + ATT_HEADS].set(dcum))
    dproj = (dq, dk, dv, dz, dxbc, dgl, dfd)
    dw_in = [_mm(xb, g_, ta=True, tb=(i == 0), name=f"b_in_w{i}") for i, g_ in enumerate(dproj)]
    grads = dict(q=dw_in[0], k=dw_in[1], v=dw_in[2], z=dw_in[3], xbc=dw_in[4], gate=dw_in[5], fd=dw_in[6],
                 conv=c["dconv_w"])
    small_g = dict(
        b_forget=dbias[0, F_LANE0:F_LANE0 + ATT_HEADS], conv_b=c["dconv_b"][0], dt_bias=dbias[0, :SSM_HEADS],
        a_log=c["da_row"][0, :SSM_HEADS] * a, d_skip=c["dds"], ssm_norm_w=c["dnw"][0], b_gates=c["dbg"][0],
        ln1_g=c["dg1"][0], ln1_b=c["db1"][0], ln2_g=c["dg2"][0], ln2_b=c["db2"][0])
    return c["loss_row"][0, 0], grads, small_g, dproj


def _stage_out_x(c, dproj, token):
    wts = c["wts"]
    dq_t, dk, dv, dz, dxbc, dgate, dfd = dproj
    w_fd = wts["fd"] + token.astype(BF16)
    dx = _mm_sum([(dq_t, True, wts["qkv"], 0), (dk, False, wts["qkv"], 1), (dv, False, wts["qkv"], 2),
                  (dfd, False, w_fd, 0)], c["dxr"], "b_in_x_qkv_fd")
    dx = _mm_sum([(dz, False, wts["z"], 0), (dgate, False, wts["gate"], 0)], dx, "b_in_x_z_gate")
    return _mm_sum([(dxbc, False, wts["xbc"], 0)], dx, "b_in_x_xbc")


BIG = ("w_in", "w_proj_attn", "w_proj_ssm", "w_out", "w_ffn_gate", "w_ffn_up", "w_ffn_down", "conv_w")
EARLY = ("w_in", "conv_w")
LATE = ("w_proj_attn", "w_proj_ssm", "w_out", "w_ffn_gate", "w_ffn_up", "w_ffn_down")
SMALL = ("b_forget", "conv_b", "dt_bias", "a_log", "d_skip", "ssm_norm_w", "b_gates", "ln1_g", "ln1_b", "ln2_g",
         "ln2_b")
SMALL_ROWS = 96
IN_SHARD = IN_WIDTH // N_DEV
IN_SEGMENTS = (("q", 0, 1024), ("k", 1024, 1024), ("v", 2048, 1024), ("f", 3072, ATT_HEADS), ("z", 3088, SSM_INNER),
               ("xbc", 5136, SSM_CONV_DIM), ("dt", 8208, SSM_HEADS), ("gate", 8240, 2 * D_MODEL))


def _cols_from_shards(shards, lo, hi):
    w = shards[0].shape[1]
    pieces = []
    for j in range(len(shards)):
        a, b = max(lo, j * w), min(hi, (j + 1) * w)
        if a < b:
            pieces.append(shards[j][:, a - j * w:b - j * w])
    return pieces[0] if len(pieces) == 1 else jnp.concatenate(pieces, axis=1)


def _shards_from_parts(parts, width):
    shards = []
    for j in range(N_DEV):
        lo, hi = j * width, (j + 1) * width
        pieces = []
        for mat, c0 in parts:
            a, b = max(lo, c0), min(hi, c0 + mat.shape[1])
            if a < b:
                pieces.append(mat[:, a - c0:b - c0])
        shards.append(pieces[0] if len(pieces) == 1 else jnp.concatenate(pieces, axis=1))
    return shards


def _pack_small(vals):
    flat = jnp.concatenate([vals[n].reshape(-1) for n in SMALL])
    return jnp.pad(flat, (0, SMALL_ROWS * LANES - flat.shape[0])).reshape(SMALL_ROWS, LANES)


def _unpack_small(pack, shapes):
    flat = pack.reshape(-1)
    out, off = {}, 0
    for n in SMALL:
        sz = math.prod(shapes[n])
        out[n] = flat[off:off + sz].reshape(shapes[n])
        off += sz
    return out


def kernel(x, w_in, b_forget, conv_w, conv_b, dt_bias, a_log, d_skip, ssm_norm_w, w_proj_attn, w_proj_ssm, b_gates, w_out, ln1_g, ln1_b, w_ffn_gate, w_ffn_up, w_ffn_down, ln2_g, ln2_b, loss_target, m_w_in, m_b_forget, m_conv_w, m_conv_b, m_dt_bias, m_a_log, m_d_skip, m_ssm_norm_w, m_w_proj_attn, m_w_proj_ssm, m_b_gates, m_w_out, m_ln1_g, m_ln1_b, m_w_ffn_gate, m_w_ffn_up, m_w_ffn_down, m_ln2_g, m_ln2_b, v_w_in, v_b_forget, v_conv_w, v_conv_b, v_dt_bias, v_a_log, v_d_skip, v_ssm_norm_w, v_w_proj_attn, v_w_proj_ssm, v_b_gates, v_w_out, v_ln1_g, v_ln1_b, v_w_ffn_gate, v_w_ffn_up, v_w_ffn_down, v_ln2_g, v_ln2_b):
    args = dict(locals())
    d, f = D_MODEL, FFN_HIDDEN
    big_w = {n: args[n][0] for n in BIG}
    small_w = {n: args[n][0] for n in SMALL}
    big_shapes = {n: args[n].shape for n in BIG}
    small_shapes = {n: args[n].shape for n in SMALL}

    early = dict(zip(EARLY, _all_gather([big_w["w_in"].astype(BF16), big_w["conv_w"]])))
    in_shards = [early["w_in"][j] for j in range(N_DEV)]
    seg = {n: _cols_from_shards(in_shards, c0, c0 + w) for n, c0, w in IN_SEGMENTS}
    wfd = jnp.concatenate([seg["dt"], seg["f"], jnp.zeros((d, LANES - SSM_HEADS - ATT_HEADS), BF16)], axis=1)
    wts = dict(qkv=jnp.concatenate([seg["q"], seg["k"], seg["v"]], axis=1), z=seg["z"], xbc=seg["xbc"],
               gate=seg["gate"], fd=wfd, conv=jnp.concatenate([early["conv_w"][j] for j in range(N_DEV)], axis=1))

    ctx = _stage_in(x[0], wts, small_w)
    attn, lse, gathered = _attention_fwd(ctx["qkv"], ctx["ck4"], [big_w[n].astype(BF16) for n in LATE])
    full = dict(zip(LATE, gathered))
    late_w = dict(
        pa=full["w_proj_attn"].reshape(d, d), ps=full["w_proj_ssm"].reshape(SSM_INNER, d),
        out=full["w_out"].reshape(d, d),
        gu=jnp.concatenate([full["w_ffn_gate"][j] for j in range(N_DEV)]
                           + [full["w_ffn_up"][j] for j in range(N_DEV)], axis=1),
        down=full["w_ffn_down"].reshape(f, d))
    ctx, gl = _stage_mid(ctx, attn, lse, late_w, loss_target[0])
    late_dest = dict(
        w_ffn_gate=jnp.stack([s_.astype(BF16) for s_ in _shards_from_parts([(gl["gu"][:, :f], 0)], f // N_DEV)]),
        w_ffn_up=jnp.stack([s_.astype(BF16) for s_ in _shards_from_parts([(gl["gu"][:, f:], 0)], f // N_DEV)]))
    for n, key in (("w_proj_attn", "pa"), ("w_proj_ssm", "ps"), ("w_out", "out"), ("w_ffn_down", "down")):
        late_dest[n] = gl[key].astype(BF16).reshape((N_DEV,) + big_shapes[n][1:])
    att_grads, late_recv = _attention_bwd(ctx["qkv"], ctx["ck4"], ctx["st"], ctx["do_b"], [late_dest[n] for n in LATE])
    loss_part, g, small_g, dproj = _stage_out_w(ctx, att_grads)
    loss = lax.psum(loss_part, ("x", "y", "c"))

    gfd = g["fd"]
    in_parts = dict(q=g["q"], k=g["k"], v=g["v"], f=gfd[:, F_LANE0:F_LANE0 + ATT_HEADS], z=g["z"], xbc=g["xbc"],
                    dt=gfd[:, DT_LANE0:DT_LANE0 + SSM_HEADS], gate=g["gate"])
    win_dest = jnp.stack([s_.astype(BF16) for s_ in
                          _shards_from_parts([(in_parts[n], c0) for n, c0, _ in IN_SEGMENTS], IN_SHARD)])
    conv_dest = jnp.stack(_shards_from_parts([(g["conv"], 0)], SSM_CONV_DIM // N_DEV))
    small_pack = _pack_small(small_g)
    last_parts = [win_dest, conv_dest, jnp.broadcast_to(small_pack, (N_DEV,) + small_pack.shape)]
    send_sems, recv_sems, parts_thru, lands_thru, token = _exchange_start(last_parts, _landing_zones(last_parts))
    grad_x = _stage_out_x(ctx, dproj, token[0, 0])
    early_recv = _exchange_wait(send_sems, recv_sems, parts_thru, lands_thru, grad_x)
    recv = dict(zip(LATE, late_recv))
    recv["w_in"], recv["conv_w"] = early_recv[0], early_recv[1]

    outs = {}
    for n in BIG:
        outs[n] = _adamw(recv[n], args[n], args["m_" + n], args["v_" + n], name="adamw_" + n)
    small4 = _adamw(early_recv[2], _pack_small(small_w)[None], _pack_small({n: args["m_" + n][0] for n in SMALL})[None],
                    _pack_small({n: args["v_" + n][0] for n in SMALL})[None], name="adamw_small")
    small_out = [_unpack_small(p, small_shapes) for p in small4]
    for n in SMALL:
        outs[n] = [so[n] for so in small_out]

    order = ("w_in", "b_forget", "conv_w", "conv_b", "dt_bias", "a_log", "d_skip", "ssm_norm_w", "w_proj_attn",
             "w_proj_ssm", "b_gates", "w_out", "ln1_g", "ln1_b", "w_ffn_gate", "w_ffn_up", "w_ffn_down", "ln2_g",
             "ln2_b")
    res = [loss, grad_x[None]]
    for i in range(4):
        res += [outs[n][i] for n in order]
    return tuple(res)
```

```python
import functools
import math

import jax
import jax.numpy as jnp
from jax import lax
from jax.experimental import pallas as pl
from jax.experimental.pallas import tpu as pltpu

F32 = jnp.float32
BF16 = jnp.bfloat16

N_DEV = 8
D_MODEL = 1024
ATT_HEADS = 16
ATT_HEAD_DIM = 64
SSM_INNER = 2048
SSM_HEADS = 32
SSM_HEAD_DIM = 64
SSM_GROUPS = 4
SSM_HEADS_PER_GROUP = 8
SSM_STATE = 128
SSM_CONV = 4
SSM_CHUNK = 128
SSM_CONV_DIM = 3072
FFN_HIDDEN = 2816
IN_WIDTH = 10288
DEEPNORM_ALPHA = 2.0 ** 0.25
LN_EPS = 1e-5
RMS_EPS = 1e-5
ADAM_LR, ADAM_B1, ADAM_B2, ADAM_EPS, ADAM_WD, ADAM_STEP = 0.001, 0.9, 0.999, 1e-08, 0.01, 10
ATT_SCALE = 1.0 / math.sqrt(ATT_HEAD_DIM)

LANES = 128
VMEM_LIMIT = 56 * 1024 * 1024
NEG = -1e30

DT_LANE0 = 0
F_LANE0 = 32
HI = lax.Precision.HIGHEST


def _cp(*sem):
    return pltpu.CompilerParams(dimension_semantics=sem, vmem_limit_bytes=VMEM_LIMIT)


def _tile(n, cap=1408):
    for t in (3072, 2816, 2048, 1536, 1408, 1024, 512, 384, 256, 128):
        if t <= cap and n % t == 0:
            return t
    return n


MM_VMEM_BUDGET = VMEM_LIMIT - 4 * 2 ** 20


def _mm_tiles(m, n, k, a_bytes, b_bytes, out_bytes, has_add):
    tm = _tile(m)

    def need(tn, tk):
        blocks = tm * tk * a_bytes + tk * tn * b_bytes + tm * tn * (out_bytes + (4 if has_add else 0))
        casts = (tm * tk * 2 if a_bytes == 4 else 0) + (tk * tn * 2 if b_bytes == 4 else 0)
        return 2 * blocks + casts + tm * tn * 4

    tns = [t for t in (3072, 2816, 2048, 1536, 1408, 1024, 512, 384, 256, 128) if n % t == 0] or [n]
    for tk in ([k] if k <= 3072 else []) + [t for t in (2816, 1024, 512, 256, 128) if k % t == 0]:
        fits = [tn for tn in tns if need(tn, tk) <= MM_VMEM_BUDGET]
        if fits and fits[0] >= min(1024, tns[0]):
            return tm, fits[0], tk
    return tm, tns[-1], tk


def _sigmoid(x):
    return 1.0 / (1.0 + jnp.exp(-x))


def _mm(a, b, *, ta=False, tb=False, out_dtype=F32, add=None, name):
    m, k = (a.shape[1], a.shape[0]) if ta else a.shape
    n = b.shape[0] if tb else b.shape[1]
    assert (b.shape[1] if tb else b.shape[0]) == k
    tm, tn, tk = _mm_tiles(m, n, k, a.dtype.itemsize, b.dtype.itemsize, jnp.dtype(out_dtype).itemsize, add is not None)
    nk = k // tk
    dims = (((0,) if ta else (1,), (1,) if tb else (0,)), ((), ()))

    def body_single(*refs):
        a_ref, b_ref = refs[:2]
        r = lax.dot_general(a_ref[...].astype(BF16), b_ref[...].astype(BF16), dims, preferred_element_type=F32)
        if add is not None:
            r = r + refs[2][...]
        refs[-1][...] = r.astype(refs[-1].dtype)

    def body(*refs):
        if add is None:
            a_ref, b_ref, o_ref, acc_ref = refs
        else:
            a_ref, b_ref, c_ref, o_ref, acc_ref = refs
        kk = pl.program_id(2)

        @pl.when(kk == 0)
        def _():
            acc_ref[...] = jnp.zeros_like(acc_ref)

        acc_ref[...] += lax.dot_general(a_ref[...].astype(BF16), b_ref[...].astype(BF16), dims,
                                        preferred_element_type=F32)

        @pl.when(kk == nk - 1)
        def _():
            r = acc_ref[...]
            if add is not None:
                r = r + c_ref[...]
            o_ref[...] = r.astype(o_ref.dtype)

    a_spec = pl.BlockSpec((tk, tm), lambda i, j, kk: (kk, i)) if ta else pl.BlockSpec((tm, tk), lambda i, j, kk: (i, kk))
    b_spec = pl.BlockSpec((tn, tk), lambda i, j, kk: (j, kk)) if tb else pl.BlockSpec((tk, tn), lambda i, j, kk: (kk, j))
    o_spec = pl.BlockSpec((tm, tn), lambda i, j, kk: (i, j))
    in_specs, args = [a_spec, b_spec], [a, b]
    if add is not None:
        in_specs.append(o_spec)
        args.append(add)
    return pl.pallas_call(
        body_single if nk == 1 else body, name=name, grid=(m // tm, n // tn, nk), in_specs=in_specs, out_specs=o_spec,
        out_shape=jax.ShapeDtypeStruct((m, n), out_dtype),
        scratch_shapes=[] if nk == 1 else [pltpu.VMEM((tm, tn), F32)],
        compiler_params=_cp("parallel", "parallel", "arbitrary"),
    )(*args)


def _mm_sum(terms, add, name):
    m, n = add.shape
    ks = [a.shape[0] if ta else a.shape[1] for a, ta, _, _ in terms]

    def need(tm):
        blocks = sum(tm * k * a.dtype.itemsize + n * k * b.dtype.itemsize for (a, _, b, _), k in zip(terms, ks))
        casts = sum(tm * k * 2 for (a, _, _, _), k in zip(terms, ks) if a.dtype.itemsize == 4)
        return 2 * (blocks + 2 * tm * n * 4) + casts + tm * n * 4

    tm = next(t for t in (1024, 512, 256, 128) if m % t == 0 and need(t) <= MM_VMEM_BUDGET)

    def body(*refs):
        r = refs[-2][...]
        for i, (_, ta, _, _) in enumerate(terms):
            dims = (((0,) if ta else (1,), (1,)), ((), ()))
            r = r + lax.dot_general(refs[2 * i][...].astype(BF16), refs[2 * i + 1][...].astype(BF16), dims,
                                    preferred_element_type=F32)
        refs[-1][...] = r

    in_specs, args = [], []
    for (a, ta, b, jb), k in zip(terms, ks):
        in_specs += [pl.BlockSpec((k, tm), lambda i: (0, i)) if ta else pl.BlockSpec((tm, k), lambda i: (i, 0)),
                     pl.BlockSpec((n, k), lambda i, jb=jb: (0, jb))]
        args += [a, b]
    o_spec = pl.BlockSpec((tm, n), lambda i: (i, 0))
    return pl.pallas_call(body, name=name, grid=(m // tm,), in_specs=in_specs + [o_spec], out_specs=o_spec,
                          out_shape=jax.ShapeDtypeStruct((m, n), F32), compiler_params=_cp("parallel"))(*args, add)


def _tri(n, lower=True):
    r = lax.broadcasted_iota(jnp.int32, (n, n), 0)
    c = lax.broadcasted_iota(jnp.int32, (n, n), 1)
    return jnp.where((r >= c) if lower else (c >= r), 1.0, 0.0).astype(F32)


def _stats_fwd(fd, bias_row, a_row):
    s = fd.shape[0]
    blk = SSM_CHUNK

    def body(fd_ref, bias_ref, a_ref, dt_ref, ac_ref, cf_ref, dtr_ref, acr_ref, cfr_ref, carry_ref):
        @pl.when(pl.program_id(0) == 0)
        def _():
            carry_ref[...] = jnp.zeros_like(carry_ref)

        v = fd_ref[...] + bias_ref[...]
        dt = jnp.maximum(v, 0.0) + jnp.log(1.0 + jnp.exp(-jnp.abs(v)))
        lf = jnp.minimum(v, 0.0) - jnp.log(1.0 + jnp.exp(-jnp.abs(v)))
        tri = _tri(blk)
        ac = jnp.dot(tri, dt * a_ref[...], precision=HI, preferred_element_type=F32)
        cf = jnp.dot(tri, lf, precision=HI, preferred_element_type=F32) + carry_ref[0:1, :]
        carry_ref[...] = carry_ref[...] + jnp.sum(lf, axis=0, keepdims=True)
        dt_ref[...] = dt
        ac_ref[...] = ac
        cf_ref[...] = cf
        dtr_ref[...] = dt.T
        acr_ref[...] = ac.T
        cfr_ref[...] = cf.T

    col = pl.BlockSpec((blk, LANES), lambda i: (i, 0))
    row = pl.BlockSpec((LANES, blk), lambda i: (0, i))
    vec = pl.BlockSpec((1, LANES), lambda i: (0, 0))
    return pl.pallas_call(
        body, name="stats_fwd", grid=(s // blk,), in_specs=[col, vec, vec],
        out_specs=[col, col, col, row, row, row],
        out_shape=[jax.ShapeDtypeStruct((s, LANES), F32)] * 3 + [jax.ShapeDtypeStruct((LANES, s), F32)] * 3,
        scratch_shapes=[pltpu.VMEM((8, LANES), F32)],
        compiler_params=_cp("arbitrary"),
    )(fd, bias_row, a_row)


def _stats_bwd(fd, bias_row, ddt, dcum_rows):
    s = fd.shape[0]
    blk = SSM_CHUNK
    nb = s // blk

    def body(fd_ref, bias_ref, ddt_ref, dck_ref, o_ref, db_ref, carry_ref):
        @pl.when(pl.program_id(0) == 0)
        def _():
            carry_ref[...] = jnp.zeros_like(carry_ref)
            db_ref[...] = jnp.zeros_like(db_ref)

        v = fd_ref[...] + bias_ref[...]
        dcum = dck_ref[...].T
        dlf = jnp.dot(_tri(blk, lower=False), dcum, precision=HI, preferred_element_type=F32) + carry_ref[0:1, :]
        carry_ref[...] = carry_ref[...] + jnp.sum(dcum, axis=0, keepdims=True)
        lane = lax.broadcasted_iota(jnp.int32, v.shape, 1)
        g = jnp.where(lane < F_LANE0, ddt_ref[...] * _sigmoid(v), dlf * _sigmoid(-v))
        g = jnp.where(lane < F_LANE0 + ATT_HEADS, g, 0.0)
        o_ref[...] = g.astype(o_ref.dtype)
        db_ref[...] += jnp.sum(g, axis=0, keepdims=True)

    col = pl.BlockSpec((blk, LANES), lambda i: (nb - 1 - i, 0))
    row = pl.BlockSpec((LANES, blk), lambda i: (0, nb - 1 - i))
    vec = pl.BlockSpec((1, LANES), lambda i: (0, 0))
    return pl.pallas_call(
        body, name="stats_bwd", grid=(nb,), in_specs=[col, vec, col, row], out_specs=[col, vec],
        out_shape=[jax.ShapeDtypeStruct((s, LANES), BF16), jax.ShapeDtypeStruct((1, LANES), F32)],
        scratch_shapes=[pltpu.VMEM((8, LANES), F32)],
        compiler_params=_cp("arbitrary"),
    )(fd, bias_row, ddt, dcum_rows)


HP = LANES // ATT_HEAD_DIM
N_HP = ATT_HEADS // HP


def _att_blocks(s):
    return (512, 1024) if s % 1024 == 0 and s >= 4096 else (64, 128)


_QK = (((1,), (1,)), ((), ()))
_HALF = ATT_HEAD_DIM // 2
_PAD = 16


def _head_cols(a):
    return slice(a * ATT_HEAD_DIM, (a + 1) * ATT_HEAD_DIM)


def _causal(shape, off):
    r = lax.broadcasted_iota(jnp.int32, shape, 0)
    c = lax.broadcasted_iota(jnp.int32, shape, 1)
    return c <= r + off


def _attention_fwd(qkv, ck4, gather_parts):
    s = qkv.shape[0]
    bk = _att_blocks(s)[1]
    bq = bk
    nq, nk = s // bq, s // bk
    n = len(gather_parts)

    def body(q_ref, k_ref, v_ref, ck_ref, *rest):
        comm_in, (o_ref, lse_ref), comm_out, sems = rest[:n], rest[n:n + 2], rest[n + 2:2 * n + 2], rest[2 * n + 2:]
        i = pl.program_id(1)
        if n:
            @pl.when((pl.program_id(0) == 0) & (i == 0))
            def _():
                for cp in _comm_copies(comm_in, comm_out, sems, False):
                    cp.start()

        n_full = (i * bq) // bk
        qs = [(q_ref[:, _head_cols(a)].astype(F32) * ATT_SCALE).astype(BF16) for a in range(HP)]

        upper_q = lax.broadcasted_iota(jnp.int32, (bq, LANES), 1) >= ATT_HEAD_DIM

        def absorb(j, carry, keys=slice(0, bk), rows=slice(0, bq), masked=False):
            nk_ = keys.stop - keys.start
            ks = pl.ds(pl.multiple_of(j * bk, bk) + keys.start, nk_)
            v_both = v_ref[ks, :]
            out = []
            for a in range(HP):
                m, acc = carry[a]
                sc = lax.dot_general(qs[a][rows], k_ref[ks, _head_cols(a)], _QK, preferred_element_type=F32)
                sc = sc - ck_ref[0, a, pl.ds(j, 1), keys]
                if masked:
                    sc = jnp.where(_causal(sc.shape, 0), sc, NEG)
                m_new = jnp.maximum(m, jnp.max(sc, axis=1, keepdims=True))
                p = jnp.exp((sc - m_new).astype(BF16))
                upper_v = lax.broadcasted_iota(jnp.int32, (nk_, LANES), 1) >= ATT_HEAD_DIM
                v_aug = jnp.where(upper_v == (a == 1), v_both, jnp.ones_like(v_both))
                acc = jnp.exp(m - m_new) * acc + jnp.dot(p, v_aug, preferred_element_type=F32)
                out.append((m_new, acc))
            return tuple(out)

        init = tuple((jnp.full((bq, 1), NEG, F32), jnp.zeros((bq, LANES), F32)) for _ in range(HP))
        carry = lax.fori_loop(0, n_full, absorb, init)
        hq = bq // 2
        carry = absorb(n_full, carry, keys=slice(0, hq), masked=True)
        low = absorb(n_full, tuple((m[hq:], acc[hq:]) for m, acc in carry), keys=slice(hq, bk), rows=slice(hq, bq),
                     masked=True)
        carry = tuple((jnp.concatenate([m[:hq], ml], axis=0), jnp.concatenate([acc[:hq], al], axis=0))
                      for (m, acc), (ml, al) in zip(carry, low))
        outs, lses = [], []
        for a in range(HP):
            m, acc = carry[a]
            l = pltpu.roll(acc, ATT_HEAD_DIM, 1)
            outs.append(acc / l)
            lses.append(m + jnp.log(l))
        o_ref[...] = jnp.where(upper_q, outs[1], outs[0])
        lse_ref[...] = jnp.where(upper_q, lses[1], lses[0])
        if n:
            @pl.when((pl.program_id(0) == N_HP - 1) & (i == nq - 1))
            def _():
                for cp in _comm_copies(comm_in, comm_out, sems, False):
                    cp.wait()

    q_spec = pl.BlockSpec((bq, LANES), lambda h, i: (i, h))
    anyspec = pl.BlockSpec(memory_space=pl.ANY)
    res = pl.pallas_call(
        body, name="att_fwd", grid=(N_HP, nq),
        in_specs=[q_spec, pl.BlockSpec((s, LANES), lambda h, i: (0, N_HP + h)),
                  pl.BlockSpec((s, LANES), lambda h, i: (0, 2 * N_HP + h)),
                  pl.BlockSpec((1, HP, nk, bk), lambda h, i: (h, 0, 0, 0))] + [anyspec] * n,
        out_specs=[q_spec, q_spec] + [anyspec] * n,
        out_shape=[jax.ShapeDtypeStruct((s, D_MODEL), F32)] * 2 + _comm_out_shapes(gather_parts, False),
        scratch_shapes=_comm_sems(n) if n else [],
        compiler_params=_cp("arbitrary", "arbitrary"),
    )(qkv, qkv, qkv, ck4, *gather_parts)
    return res[0], res[1], list(res[2:])


def _att_prep(dattn_d, w_pa, o, lse_rep):
    s = o.shape[0]
    bs = _tile(s, 512)

    def body(g_ref, w_ref, o_ref, lse_ref, st_ref, dob_ref):
        do = lax.dot_general(g_ref[...], w_ref[...], (((1,), (1,)), ((), ())), preferred_element_type=F32)
        r = lax.broadcasted_iota(jnp.int32, (LANES, LANES), 0) // ATT_HEAD_DIM
        c = lax.broadcasted_iota(jnp.int32, (LANES, LANES), 1) // ATT_HEAD_DIM
        e = jnp.where(r == c, 1.0, 0.0).astype(F32)
        lane = lax.broadcasted_iota(jnp.int32, (bs, LANES), 1)
        for p in range(D_MODEL // LANES):
            cs = slice(p * LANES, (p + 1) * LANES)
            dd = do[:, cs]
            delta = jnp.dot(dd * o_ref[:, cs], e, precision=HI, preferred_element_type=F32)
            st_ref[:, cs] = jnp.where(lane % ATT_HEAD_DIM < _HALF, lse_ref[:, cs], delta)
            dob_ref[:, cs] = dd.astype(BF16)

    assert dattn_d.dtype == BF16 and w_pa.dtype == BF16
    spec = pl.BlockSpec((bs, D_MODEL), lambda i: (i, 0))
    w_spec = pl.BlockSpec(w_pa.shape, lambda i: (0, 0))
    return pl.pallas_call(body, name="att_prep", grid=(s // bs,), in_specs=[spec, w_spec, spec, spec],
                          out_specs=[spec, spec],
                          out_shape=[jax.ShapeDtypeStruct((s, D_MODEL), F32), jax.ShapeDtypeStruct((s, D_MODEL), BF16)],
                          compiler_params=_cp("parallel"))(dattn_d, w_pa, o, lse_rep)


def _attention_bwd(qkv, ck4, st, do_b, exchange_parts):
    s = qkv.shape[0]
    bq, bk = _att_blocks(s)
    nq, nk, per = s // bq, s // bk, bk // bq
    _T = (((0,), (0,)), ((), ()))
    n = len(exchange_parts)

    def body(q_ref, k_ref, v_ref, ck_ref, st_ref, do_ref, *rest):
        comm_in, (dq_ref, dk_ref, dv_ref, dck_ref, dcq_ref) = rest[:n], rest[n:n + 5]
        comm_out, sems, (dk_acc, dv_acc) = rest[n + 5:2 * n + 5], rest[2 * n + 5:-2], rest[-2:]
        j = pl.program_id(1)
        if n:
            @pl.when((pl.program_id(0) == 0) & (j == 0))
            def _():
                for cp in _comm_copies(comm_in, comm_out, sems, True):
                    cp.start()

        @pl.when(j == 0)
        def _():
            dq_ref[...] = jnp.zeros_like(dq_ref)
            dcq_ref[...] = jnp.zeros_like(dcq_ref)

        dk_acc[...] = jnp.zeros_like(dk_acc)
        dv_acc[...] = jnp.zeros_like(dv_acc)

        ones_q, ones_k = jnp.ones((_PAD, bq), BF16), jnp.ones((_PAD, bk), BF16)
        k_t = [jnp.concatenate([k_ref[:, _head_cols(a)].T, ones_k], axis=0) for a in range(HP)]

        def step(i, off=None, kl=slice(0, bk)):
            rows = pl.ds(pl.multiple_of(i * bq, bq), bq)
            for a in range(HP):
                cs = _head_cols(a)
                q = (q_ref[rows, cs].astype(F32) * ATT_SCALE).astype(BF16)
                do_a = do_ref[rows, cs]
                sc = lax.dot_general(q, k_ref[kl, cs], _QK, preferred_element_type=F32) - ck_ref[0, a, pl.ds(j, 1), kl]
                if off is not None:
                    sc = jnp.where(_causal(sc.shape, off), sc, NEG)
                p = jnp.exp(sc - st_ref[rows, a * ATT_HEAD_DIM:a * ATT_HEAD_DIM + 1])
                dp = lax.dot_general(do_a, v_ref[kl, cs], _QK, preferred_element_type=F32)
                ds = p * (dp - st_ref[rows, a * ATT_HEAD_DIM + _HALF:a * ATT_HEAD_DIM + _HALF + 1])
                ds_b = ds.astype(BF16)
                dv_acc[a, :, kl] += jnp.dot(do_a.T, p.astype(BF16), preferred_element_type=F32)
                dk_acc[a, :, kl] += jnp.dot(jnp.concatenate([q.T, ones_q], axis=0), ds_b, preferred_element_type=F32)
                dqs = lax.dot_general(k_t[a][:, kl], ds_b, _QK, preferred_element_type=F32)
                dq_ref[cs, rows] += dqs[:ATT_HEAD_DIM] * ATT_SCALE
                dcq_ref[0, a, pl.ds(i, 1), :] += jnp.sum(dqs[ATT_HEAD_DIM:ATT_HEAD_DIM + 8], axis=0,
                                                         keepdims=True) * 0.125

        for t in range(per):
            step(j * per + t, off=t * bq, kl=slice(0, (t + 1) * bq))

        def full(i, c):
            step(i)
            return c

        lax.fori_loop((j + 1) * per, nq, full, 0)
        for a in range(HP):
            dk_ref[:, _head_cols(a)] = dk_acc[a, :ATT_HEAD_DIM].T.astype(dk_ref.dtype)
            dv_ref[:, _head_cols(a)] = dv_acc[a].T.astype(dv_ref.dtype)
            dck_ref[0, a, pl.ds(j, 1), :] = -dk_acc[a, ATT_HEAD_DIM:ATT_HEAD_DIM + 1]
        if n:
            @pl.when((pl.program_id(0) == N_HP - 1) & (j == nk - 1))
            def _():
                for cp in _comm_copies(comm_in, comm_out, sems, True):
                    cp.wait()

    res = pl.BlockSpec((s, LANES), lambda h, j: (0, h))
    ck_spec = pl.BlockSpec((1, HP, nk, bk), lambda h, j: (h, 0, 0, 0))
    kout = pl.BlockSpec((bk, LANES), lambda h, j: (j, h))
    anyspec = pl.BlockSpec(memory_space=pl.ANY)
    outs = pl.pallas_call(
        body, name="att_bwd", grid=(N_HP, nk),
        in_specs=[res, pl.BlockSpec((bk, LANES), lambda h, j: (j, N_HP + h)),
                  pl.BlockSpec((bk, LANES), lambda h, j: (j, 2 * N_HP + h)), ck_spec, res, res] + [anyspec] * n,
        out_specs=[pl.BlockSpec((LANES, s), lambda h, j: (h, 0)), kout, kout, ck_spec,
                   pl.BlockSpec((1, HP, nq, bq), lambda h, j: (h, 0, 0, 0))] + [anyspec] * n,
        out_shape=[jax.ShapeDtypeStruct((D_MODEL, s), F32), jax.ShapeDtypeStruct((s, D_MODEL), BF16),
                   jax.ShapeDtypeStruct((s, D_MODEL), BF16), jax.ShapeDtypeStruct((N_HP, HP, nk, bk), F32),
                   jax.ShapeDtypeStruct((N_HP, HP, nq, bq), F32)] + _comm_out_shapes(exchange_parts, True),
        scratch_shapes=(_comm_sems(n) if n else [])
        + [pltpu.VMEM((HP, ATT_HEAD_DIM + _PAD, bk), F32), pltpu.VMEM((HP, ATT_HEAD_DIM, bk), F32)],
        compiler_params=_cp("arbitrary", "arbitrary"),
    )(qkv, qkv, qkv, ck4, st, do_b, *exchange_parts)
    return outs[:5], list(outs[5:])


def _silu_and_grad(x):
    sg = _sigmoid(x)
    return x * sg, sg * (1.0 + x * (1.0 - sg))


SUBLANES = 8


def _conv_taps(cur, before, w_rows, bias):
    n, c = cur.shape
    cur3 = cur.reshape(n // SUBLANES, SUBLANES, c)
    sub = lax.broadcasted_iota(jnp.int32, (1, SUBLANES, c), 1)
    taps = []
    for k in range(SSM_CONV):
        sh = SSM_CONV - 1 - k
        if sh == 0:
            taps.append(cur3)
            continue
        rot = pltpu.roll(cur3, sh, 1)
        prev = jnp.concatenate([pltpu.roll(before, sh, 0)[None], rot[:-1]], axis=0)
        taps.append(jnp.where(sub < sh, prev, rot))
    pre = bias[None] + sum(w_rows[k][None] * taps[k] for k in range(SSM_CONV))
    return pre.reshape(n, c), [t.reshape(n, c) for t in taps]


def _conv_col_chunks(w_ref, b_ref, bc, cc):
    for c0 in range(0, bc, cc):
        cols = slice(c0, c0 + cc)
        yield cols, [w_ref[k:k + 1, cols] for k in range(SSM_CONV)], None if b_ref is None else b_ref[:, cols]


def _conv_specs(s, bs, bc):
    cur = pl.BlockSpec((bs, bc), lambda j, i: (i, j))
    halo = pl.BlockSpec((8, bc), lambda j, i: (jnp.maximum(i * (bs // 8) - 1, 0), j))
    w = pl.BlockSpec((SSM_CONV, bc), lambda j, i: (0, j))
    b = pl.BlockSpec((1, bc), lambda j, i: (0, j))
    return cur, halo, w, b


def _conv_fwd(xbc, w, b):
    s, c = xbc.shape
    bs, bc = _tile(s, 512), 1024
    rc, cc = bs, 128

    def body(x_ref, h_ref, w_ref, b_ref, o_ref):
        first = pl.program_id(1) == 0
        for cols, w_rows, bias in _conv_col_chunks(w_ref, b_ref, bc, cc):
            def step(r, before):
                rows = pl.ds(pl.multiple_of(r * rc, rc), rc)
                cur = x_ref[rows, cols]
                pre, _ = _conv_taps(cur, before, w_rows, bias)
                o_ref[rows, cols] = pre * _sigmoid(pre)
                return cur[rc - SUBLANES:]

            lax.fori_loop(0, bs // rc, step, jnp.where(first, 0.0, h_ref[:, cols]))

    cur, halo, ws, bsp = _conv_specs(s, bs, bc)
    return pl.pallas_call(body, name="conv_fwd", grid=(c // bc, s // bs), in_specs=[cur, halo, ws, bsp],
                          out_specs=cur, out_shape=jax.ShapeDtypeStruct((s, c), F32),
                          compiler_params=_cp("parallel", "parallel"))(xbc, xbc, w, b)


def _conv_bwd_pre(xbc, w, b, dact):
    s, c = xbc.shape
    bs, bc = _tile(s, 512), 1024
    rc, cc = _tile(bs, 256), 128

    def body(x_ref, h_ref, w_ref, b_ref, g_ref, dp_ref, dw_ref, db_ref):
        first = pl.program_id(1) == 0

        @pl.when(first)
        def _():
            dw_ref[...] = jnp.zeros_like(dw_ref)
            db_ref[...] = jnp.zeros_like(db_ref)

        for cols, w_rows, bias in _conv_col_chunks(w_ref, b_ref, bc, cc):
            def step(r, carry):
                before, sums = carry
                rows = pl.ds(pl.multiple_of(r * rc, rc), rc)
                cur = x_ref[rows, cols]
                pre, taps = _conv_taps(cur, before, w_rows, bias)
                dpre = g_ref[rows, cols].astype(F32) * _silu_and_grad(pre)[1]
                dp_ref[rows, cols] = dpre.astype(dp_ref.dtype)
                terms = [dpre * t for t in taps] + [dpre]
                sums = tuple(a + jnp.sum(t.reshape(rc // SUBLANES, SUBLANES, cc), axis=0) for a, t in zip(sums, terms))
                return cur[rc - SUBLANES:], sums

            zero = jnp.zeros((SUBLANES, cc), F32)
            _, sums = lax.fori_loop(0, bs // rc, step,
                                    (jnp.where(first, 0.0, h_ref[:, cols]), (zero,) * (SSM_CONV + 1)))
            for k in range(SSM_CONV):
                dw_ref[k:k + 1, cols] += jnp.sum(sums[k], axis=0, keepdims=True)
            db_ref[:, cols] += jnp.sum(sums[SSM_CONV], axis=0, keepdims=True)

    cur, halo, ws, bsp = _conv_specs(s, bs, bc)
    return pl.pallas_call(
        body, name="conv_bwd_pre", grid=(c // bc, s // bs), in_specs=[cur, halo, ws, bsp, cur],
        out_specs=[cur, ws, bsp],
        out_shape=[jax.ShapeDtypeStruct((s, c), BF16), jax.ShapeDtypeStruct((SSM_CONV, c), F32),
                   jax.ShapeDtypeStruct((1, c), F32)],
        compiler_params=_cp("parallel", "arbitrary"))(xbc, xbc, w, b, dact)


def _conv_bwd_in(dpre, w):
    s, c = dpre.shape
    bs, bc = _tile(s, 512), 1024
    nb = s // bs
    rc, cc = bs, 256
    nr = bs // rc

    def body(g_ref, n_ref, w_ref, o_ref):
        last = pl.program_id(1) == nb - 1
        sub = lax.broadcasted_iota(jnp.int32, (1, SUBLANES, cc), 1)
        for cols, w_rows, _ in _conv_col_chunks(w_ref, None, bc, cc):
            def step(i, after):
                rows = pl.ds(pl.multiple_of((nr - 1 - i) * rc, rc), rc)
                cur = g_ref[rows, cols].astype(F32)
                cur3 = cur.reshape(rc // SUBLANES, SUBLANES, cc)
                acc = w_rows[SSM_CONV - 1][None] * cur3
                for sh in range(1, SSM_CONV):
                    rot = pltpu.roll(cur3, SUBLANES - sh, 1)
                    nxt = jnp.concatenate([rot[1:], pltpu.roll(after, SUBLANES - sh, 0)[None]], axis=0)
                    acc = acc + w_rows[SSM_CONV - 1 - sh][None] * jnp.where(sub >= SUBLANES - sh, nxt, rot)
                o_ref[rows, cols] = acc.reshape(rc, cc).astype(o_ref.dtype)
                return cur[0:SUBLANES]

            lax.fori_loop(0, nr, step, jnp.where(last, 0.0, n_ref[0:SUBLANES, cols].astype(F32)))

    cur = pl.BlockSpec((bs, bc), lambda j, i: (i, j))
    nxt = pl.BlockSpec((16, bc), lambda j, i: (jnp.minimum((i + 1) * (bs // 16), s // 16 - 1), j))
    ws = pl.BlockSpec((SSM_CONV, bc), lambda j, i: (0, j))
    return pl.pallas_call(body, name="conv_bwd_in", grid=(c // bc, nb), in_specs=[cur, nxt, ws], out_specs=cur,
                          out_shape=jax.ShapeDtypeStruct((s, c), BF16),
                          compiler_params=_cp("parallel", "parallel"))(dpre, dpre, w)


def _dotT(a, b):
    return lax.dot_general(a.astype(BF16), b.astype(BF16), (((1,), (1,)), ((), ())), preferred_element_type=F32)


def _dot(a, b):
    return jnp.dot(a.astype(BF16), b.astype(BF16), preferred_element_type=F32)


N_PAIR = SSM_HEADS // HP
PAIRS_PER_GROUP = SSM_HEADS_PER_GROUP // HP


def _pair_consts():
    L = SSM_CHUNK
    lane = lax.broadcasted_iota(jnp.int32, (L, LANES), 1)
    lane1 = lax.broadcasted_iota(jnp.int32, (1, LANES), 1)
    li = lax.broadcasted_iota(jnp.int32, (L, L), 0)
    si = lax.broadcasted_iota(jnp.int32, (L, L), 1)
    return lane >= ATT_HEAD_DIM, lane1 >= ATT_HEAD_DIM, li, si


def _ssd_pair_fwd(xbc_act, ac_c, dt_r, ac_r, dsk_pair):
    s = xbc_act.shape[0]
    L, N, G = SSM_CHUNK, SSM_STATE, SSM_GROUPS
    nc = s // L

    def body(xbc_ref, ac_ref, dtr_ref, acr_ref, dsk_ref, y_ref, hp_ref, st_ref):
        @pl.when(pl.program_id(0) == 0)
        def _():
            st_ref[...] = jnp.zeros_like(st_ref)

        upper, up1, li, si = _pair_consts()
        for g in range(G):
            b_g = xbc_ref[:, SSM_INNER + g * N:SSM_INNER + (g + 1) * N]
            c_g = xbc_ref[:, SSM_INNER + G * N + g * N:SSM_INNER + G * N + (g + 1) * N]
            cb = _dotT(c_g, b_g)
            b_t = b_g.T
            for q in range(PAIRS_PER_GROUP):
                pp = g * PAIRS_PER_GROUP + q
                cols = slice(pp * LANES, (pp + 1) * LANES)
                xs = xbc_ref[:, cols]
                ht = st_ref[pp]
                hp_ref[0, pp] = ht
                y = dsk_ref[pp:pp + 1, :] * xs
                s_new = jnp.zeros((N, LANES), F32)
                ea, el = [], []
                for a in range(HP):
                    h = HP * pp + a
                    acol = jnp.broadcast_to(ac_ref[:, h:h + 1], (L, LANES))
                    arow, dtrow = acr_ref[h:h + 1, :], dtr_ref[h:h + 1, :]
                    alast = ac_ref[L - 1:L, h:h + 1]
                    decay = jnp.exp(jnp.where(li >= si, acol - arow, NEG))
                    xs_a = jnp.where(upper == (a == 1), xs, 0.0)
                    y = y + _dot(cb * decay * dtrow, xs_a)
                    s_new = s_new + _dot(b_t * (dtrow * jnp.exp(alast - arow)), xs_a)
                    ea.append(jnp.exp(acol))
                    el.append(jnp.exp(alast))
                y_ref[:, cols] = y + jnp.where(upper, ea[1], ea[0]) * _dot(c_g, ht)
                st_ref[pp] = ht * jnp.where(up1, el[1], el[0]) + s_new

    col = pl.BlockSpec((L, LANES), lambda c: (c, 0))
    row = pl.BlockSpec((LANES, L), lambda c: (0, c))
    return pl.pallas_call(
        body, name="ssd_fwd", grid=(nc,),
        in_specs=[pl.BlockSpec((L, SSM_CONV_DIM), lambda c: (c, 0)), col, row, row,
                  pl.BlockSpec((N_PAIR, LANES), lambda c: (0, 0))],
        out_specs=[pl.BlockSpec((L, SSM_INNER), lambda c: (c, 0)),
                   pl.BlockSpec((1, N_PAIR, N, LANES), lambda c: (c, 0, 0, 0))],
        out_shape=[jax.ShapeDtypeStruct((s, SSM_INNER), F32), jax.ShapeDtypeStruct((nc, N_PAIR, N, LANES), F32)],
        scratch_shapes=[pltpu.VMEM((N_PAIR, N, LANES), F32)],
        compiler_params=_cp("arbitrary"),
    )(xbc_act, ac_c, dt_r, ac_r, dsk_pair)


def _ssd_pair_bwd(xbc_act, dt_c, ac_c, dt_r, ac_r, hprev_all, dy, dsk_pair, a_row):
    s = xbc_act.shape[0]
    L, N, G = SSM_CHUNK, SSM_STATE, SSM_GROUPS
    nc = s // L
    rev = lambda c: nc - 1 - c

    def body(xbc_ref, dt_ref, ac_ref, dtr_ref, acr_ref, hp_ref, dy_ref, dsk_ref, arow_ref,
             dx_ref, ddt_ref, da_ref, dds_ref, dh_ref):
        @pl.when(pl.program_id(0) == 0)
        def _():
            dh_ref[...] = jnp.zeros_like(dh_ref)
            da_ref[...] = jnp.zeros_like(da_ref)
            dds_ref[...] = jnp.zeros_like(dds_ref)

        upper, up1, li, si = _pair_consts()
        lane = lax.broadcasted_iota(jnp.int32, (L, LANES), 1)
        sub = lax.broadcasted_iota(jnp.int32, (LANES, L), 0)
        lastrow = lax.broadcasted_iota(jnp.int32, (L, LANES), 0) == L - 1
        da_c = jnp.zeros((L, LANES), F32)
        da_r = jnp.zeros((LANES, L), F32)
        ddt_r = jnp.zeros((LANES, L), F32)
        for g in range(G):
            b_g = xbc_ref[:, SSM_INNER + g * N:SSM_INNER + (g + 1) * N]
            c_g = xbc_ref[:, SSM_INNER + G * N + g * N:SSM_INNER + G * N + (g + 1) * N]
            cb, cb_t = _dotT(c_g, b_g), _dotT(b_g, c_g)
            b_t, c_t = b_g.T, c_g.T
            dcb = jnp.zeros((L, L), F32)
            db_t = jnp.zeros((N, L), F32)
            dc = jnp.zeros((L, N), F32)
            for q in range(PAIRS_PER_GROUP):
                pp = g * PAIRS_PER_GROUP + q
                cols = slice(pp * LANES, (pp + 1) * LANES)
                xs, gy = xbc_ref[:, cols], dy_ref[:, cols].astype(F32)
                ht, dhn = hp_ref[0, pp], dh_ref[pp]
                acol = [jnp.broadcast_to(ac_ref[:, HP * pp + a:HP * pp + a + 1], (L, LANES)) for a in range(HP)]
                alast = [ac_ref[L - 1:L, HP * pp + a:HP * pp + a + 1] for a in range(HP)]
                ea = jnp.where(upper, jnp.exp(acol[1]), jnp.exp(acol[0]))
                el = jnp.where(up1, jnp.exp(alast[1]), jnp.exp(alast[0]))
                ge = gy * ea
                dc = dc + _dotT(ge, ht)
                dh_ref[pp] = _dot(c_t, ge) + dhn * el
                t_off = (ge * _dot(c_g, ht)).astype(BF16)
                hsum = jnp.sum(dhn * ht, axis=0, keepdims=True)
                dxs = dsk_ref[pp:pp + 1, :] * gy
                dds_ref[pp:pp + 1, :] += jnp.sum(gy * xs, axis=0, keepdims=True)
                for a in range(HP):
                    h = HP * pp + a
                    mine, mine1 = upper == (a == 1), up1 == (a == 1)
                    arow, dtrow = acr_ref[h:h + 1, :], dtr_ref[h:h + 1, :]
                    dtcol = jnp.broadcast_to(dt_ref[:, h:h + 1], (L, LANES))
                    xs_a, gy_a = jnp.where(mine, xs, 0.0), jnp.where(mine, gy, 0.0)
                    dhn_a = jnp.where(mine1, dhn, 0.0)
                    e_row = jnp.exp(alast[a] - arow)
                    w_row = dtrow * e_row
                    xd_t = _dotT(dhn_a, xs_a)
                    db_t = db_t + xd_t * w_row
                    dw = jnp.sum(b_t * xd_t, axis=0, keepdims=True)
                    de_e = dw * w_row
                    dal = (jnp.sum(jnp.where(mine1, hsum, 0.0), axis=1, keepdims=True) * jnp.exp(alast[a])
                           + jnp.sum(de_e, axis=1, keepdims=True))
                    dxs = dxs + _dot(b_g, dhn_a) * (dtcol * jnp.exp(alast[a] - acol[a]))
                    decay = jnp.exp(jnp.where(li >= si, acol[a] - arow, NEG))
                    decay_t = jnp.exp(jnp.where(si >= li, arow - acol[a], NEG))
                    m = cb * decay
                    dmdt = _dotT(gy_a, xs_a)
                    dxs = dxs + _dot(cb_t * decay_t * dtcol, gy_a)
                    dm = dmdt * dtrow
                    dcb = dcb + dm * decay
                    wb = (dm * m).astype(BF16)
                    onehot = jnp.where(lane == h, 1.0, 0.0).astype(BF16)
                    da_c = (da_c + jnp.dot(wb, onehot, preferred_element_type=F32)
                            + jnp.dot(jnp.where(mine, t_off, 0.0).astype(BF16), onehot, preferred_element_type=F32)
                            + jnp.where(lastrow & (lane == h), dal, 0.0))
                    da_r = jnp.where(sub == h, -(jnp.sum(wb.astype(F32), axis=0, keepdims=True) + de_e), da_r)
                    ddt_r = jnp.where(sub == h, dw * e_row + jnp.sum(dmdt * m, axis=0, keepdims=True), ddt_r)
                dx_ref[:, cols] = dxs.astype(dx_ref.dtype)
            dx_ref[:, SSM_INNER + g * N:SSM_INNER + (g + 1) * N] = (db_t + _dot(c_t, dcb)).T.astype(dx_ref.dtype)
            dx_ref[:, SSM_INNER + G * N + g * N:SSM_INNER + G * N + (g + 1) * N] = (
                dc + _dot(dcb, b_g)).astype(dx_ref.dtype)
        dda = jnp.dot(_tri(L, lower=False), da_c + da_r.T, precision=HI, preferred_element_type=F32)
        ddt_ref[...] = dda * arow_ref[...] + ddt_r.T
        da_ref[...] += jnp.sum(dda * dt_ref[...], axis=0, keepdims=True)

    col = pl.BlockSpec((L, LANES), lambda c: (rev(c), 0))
    row = pl.BlockSpec((LANES, L), lambda c: (0, rev(c)))
    vec = pl.BlockSpec((1, LANES), lambda c: (0, 0))
    pairs = pl.BlockSpec((N_PAIR, LANES), lambda c: (0, 0))
    return pl.pallas_call(
        body, name="ssd_bwd", grid=(nc,),
        in_specs=[pl.BlockSpec((L, SSM_CONV_DIM), lambda c: (rev(c), 0)), col, col, row, row,
                  pl.BlockSpec((1, N_PAIR, N, LANES), lambda c: (rev(c), 0, 0, 0)),
                  pl.BlockSpec((L, SSM_INNER), lambda c: (rev(c), 0)), pairs, vec],
        out_specs=[pl.BlockSpec((L, SSM_CONV_DIM), lambda c: (rev(c), 0)), col, vec, pairs],
        out_shape=[jax.ShapeDtypeStruct((s, SSM_CONV_DIM), BF16), jax.ShapeDtypeStruct((s, LANES), F32),
                   jax.ShapeDtypeStruct((1, LANES), F32), jax.ShapeDtypeStruct((N_PAIR, LANES), F32)],
        scratch_shapes=[pltpu.VMEM((N_PAIR, N, LANES), F32)],
        compiler_params=_cp("arbitrary"),
    )(xbc_act, dt_c, ac_c, dt_r, ac_r, hprev_all, dy, dsk_pair, a_row)


ROWS = 512
GW = SSM_INNER // SSM_GROUPS


def _rows(width, dtype=F32):
    return pl.BlockSpec((ROWS, width), lambda i: (i, 0))


def _vec(width):
    return pl.BlockSpec((1, width), lambda i: (0, 0))


def _gnorm_fwd(y, z, w):
    s = y.shape[0]

    def body(y_ref, z_ref, w_ref, o_ref):
        for g in range(SSM_GROUPS):
            cs = slice(g * GW, (g + 1) * GW)
            zz = z_ref[:, cs].astype(F32)
            u = y_ref[:, cs] * (zz * _sigmoid(zz))
            r = lax.rsqrt(jnp.mean(u * u, axis=1, keepdims=True) + RMS_EPS)
            o_ref[:, cs] = (u * r * w_ref[:, cs]).astype(o_ref.dtype)

    return pl.pallas_call(body, name="gnorm_fwd", grid=(s // ROWS,),
                          in_specs=[_rows(SSM_INNER), _rows(SSM_INNER), _vec(SSM_INNER)], out_specs=_rows(SSM_INNER),
                          out_shape=jax.ShapeDtypeStruct((s, SSM_INNER), BF16), compiler_params=_cp("parallel"))(y, z, w)


def _gnorm_bwd(y, z, w, dssm_d, w_ps):
    s = y.shape[0]
    assert dssm_d.dtype == BF16 and w_ps.dtype == BF16

    def body(y_ref, z_ref, w_ref, g_ref, p_ref, dy_ref, dz_ref, dw_ref):
        @pl.when(pl.program_id(0) == 0)
        def _():
            dw_ref[...] = jnp.zeros_like(dw_ref)

        do = lax.dot_general(g_ref[...], p_ref[...], (((1,), (1,)), ((), ())), preferred_element_type=F32)
        for g in range(SSM_GROUPS):
            cs = slice(g * GW, (g + 1) * GW)
            zz, yy, dd = z_ref[:, cs].astype(F32), y_ref[:, cs], do[:, cs]
            sz, dsz = _silu_and_grad(zz)
            u = yy * sz
            r = lax.rsqrt(jnp.mean(u * u, axis=1, keepdims=True) + RMS_EPS)
            n = u * r
            dn = dd * w_ref[:, cs]
            dw_ref[:, cs] += jnp.sum(dd * n, axis=0, keepdims=True)
            du = r * (dn - n * jnp.mean(dn * n, axis=1, keepdims=True))
            dy_ref[:, cs] = (du * sz).astype(dy_ref.dtype)
            dz_ref[:, cs] = (du * yy * dsz).astype(dz_ref.dtype)

    return pl.pallas_call(
        body, name="gnorm_bwd", grid=(s // ROWS,),
        in_specs=[_rows(SSM_INNER), _rows(SSM_INNER), _vec(SSM_INNER), _rows(dssm_d.shape[1]),
                  pl.BlockSpec(w_ps.shape, lambda i: (0, 0))],
        out_specs=[_rows(SSM_INNER), _rows(SSM_INNER), _vec(SSM_INNER)],
        out_shape=[jax.ShapeDtypeStruct((s, SSM_INNER), BF16), jax.ShapeDtypeStruct((s, SSM_INNER), BF16),
                   jax.ShapeDtypeStruct((1, SSM_INNER), F32)],
        compiler_params=_cp("arbitrary"))(y, z, w, dssm_d, w_ps)


def _mix_fwd(gl, bg, attn_d, ssm_d):
    s = gl.shape[0]
    d = D_MODEL

    def body(gl_ref, bg_ref, a_ref, m_ref, o_ref):
        g0 = _sigmoid(gl_ref[:, :d] + bg_ref[:, :d])
        g1 = _sigmoid(gl_ref[:, d:] + bg_ref[:, d:])
        o_ref[...] = (g0 * a_ref[...] + g1 * m_ref[...]).astype(o_ref.dtype)

    return pl.pallas_call(body, name="mix_fwd", grid=(s // ROWS,),
                          in_specs=[_rows(2 * d), _vec(2 * d), _rows(d), _rows(d)], out_specs=_rows(d),
                          out_shape=jax.ShapeDtypeStruct((s, d), BF16), compiler_params=_cp("parallel"))(
        gl, bg, attn_d, ssm_d)


def _mix_bwd(gl, bg, attn_d, ssm_d, dmix):
    s = gl.shape[0]
    d = D_MODEL

    def body(gl_ref, bg_ref, a_ref, m_ref, dm_ref, da_ref, ds_ref, dg_ref, db_ref):
        @pl.when(pl.program_id(0) == 0)
        def _():
            db_ref[...] = jnp.zeros_like(db_ref)

        g0 = _sigmoid(gl_ref[:, :d] + bg_ref[:, :d])
        g1 = _sigmoid(gl_ref[:, d:] + bg_ref[:, d:])
        dm = dm_ref[...].astype(F32)
        da_ref[...] = (dm * g0).astype(da_ref.dtype)
        ds_ref[...] = (dm * g1).astype(ds_ref.dtype)
        dl0 = dm * a_ref[...] * g0 * (1.0 - g0)
        dl1 = dm * m_ref[...] * g1 * (1.0 - g1)
        dg_ref[:, :d] = dl0.astype(dg_ref.dtype)
        dg_ref[:, d:] = dl1.astype(dg_ref.dtype)
        db_ref[:, :d] += jnp.sum(dl0, axis=0, keepdims=True)
        db_ref[:, d:] += jnp.sum(dl1, axis=0, keepdims=True)

    return pl.pallas_call(
        body, name="mix_bwd", grid=(s // ROWS,),
        in_specs=[_rows(2 * d), _vec(2 * d), _rows(d), _rows(d), _rows(d)],
        out_specs=[_rows(d), _rows(d), _rows(2 * d), _vec(2 * d)],
        out_shape=[jax.ShapeDtypeStruct((s, d), BF16), jax.ShapeDtypeStruct((s, d), BF16),
                   jax.ShapeDtypeStruct((s, 2 * d), BF16), jax.ShapeDtypeStruct((1, 2 * d), F32)],
        compiler_params=_cp("arbitrary"))(gl, bg, attn_d, ssm_d, dmix)


def _ln_stats(p):
    mu = jnp.mean(p, axis=1, keepdims=True)
    c = p - mu
    rstd = lax.rsqrt(jnp.mean(c * c, axis=1, keepdims=True) + LN_EPS)
    return c * rstd, rstd


def _ln_bwd(dy, xhat, rstd, g):
    dxh = dy * g
    return rstd * (dxh - jnp.mean(dxh, axis=1, keepdims=True) - xhat * jnp.mean(dxh * xhat, axis=1, keepdims=True))


def _ln1_fwd(x, mixed, g, b):
    s, d = x.shape

    def body(x_ref, m_ref, g_ref, b_ref, o_ref, ob_ref):
        xhat, _ = _ln_stats(DEEPNORM_ALPHA * x_ref[...] + m_ref[...])
        y = xhat * g_ref[...] + b_ref[...]
        o_ref[...] = y
        ob_ref[...] = y.astype(BF16)

    return pl.pallas_call(body, name="ln1_fwd", grid=(s // ROWS,), in_specs=[_rows(d), _rows(d), _vec(d), _vec(d)],
                          out_specs=[_rows(d), _rows(d)],
                          out_shape=[jax.ShapeDtypeStruct((s, d), F32), jax.ShapeDtypeStruct((s, d), BF16)],
                          compiler_params=_cp("parallel"))(x, mixed, g, b)


def _ln2_loss(x1, act, w_down, target, g, b):
    s, d = x1.shape
    assert act.dtype == BF16 and w_down.dtype == BF16

    def body(x_ref, a_ref, w_ref, t_ref, g_ref, b_ref, dp_ref, dpb_ref, loss_ref, dg_ref, db_ref):
        @pl.when(pl.program_id(0) == 0)
        def _():
            loss_ref[...] = jnp.zeros_like(loss_ref)
            dg_ref[...] = jnp.zeros_like(dg_ref)
            db_ref[...] = jnp.zeros_like(db_ref)

        h = jnp.dot(a_ref[...], w_ref[...], preferred_element_type=F32)
        xhat, rstd = _ln_stats(DEEPNORM_ALPHA * x_ref[...] + h)
        err = xhat * g_ref[...] + b_ref[...] - t_ref[...]
        part = 0.5 * jnp.sum(jnp.mean(err * err, axis=1, keepdims=True), axis=0, keepdims=True)
        loss_ref[...] += jnp.broadcast_to(part, loss_ref.shape)
        dy = err * (1.0 / d)
        dg_ref[...] += jnp.sum(dy * xhat, axis=0, keepdims=True)
        db_ref[...] += jnp.sum(dy, axis=0, keepdims=True)
        dp = _ln_bwd(dy, xhat, rstd, g_ref[...])
        dp_ref[...] = dp
        dpb_ref[...] = dp.astype(BF16)

    return pl.pallas_call(
        body, name="ln2_loss", grid=(s // ROWS,),
        in_specs=[_rows(d), _rows(act.shape[1]), pl.BlockSpec(w_down.shape, lambda i: (0, 0)), _rows(d), _vec(d), _vec(d)],
        out_specs=[_rows(d), _rows(d), _vec(LANES), _vec(d), _vec(d)],
        out_shape=[jax.ShapeDtypeStruct((s, d), F32), jax.ShapeDtypeStruct((s, d), BF16),
                   jax.ShapeDtypeStruct((1, LANES), F32),
                   jax.ShapeDtypeStruct((1, d), F32), jax.ShapeDtypeStruct((1, d), F32)],
        compiler_params=_cp("arbitrary"))(x1, act, w_down, target, g, b)


def _ln1_bwd(x, mixed, g, dpre2, dgu, w_gu):
    s, d = x.shape
    assert dgu.dtype == BF16 and w_gu.dtype == BF16
    rb = ROWS // 2
    _rows = lambda width: pl.BlockSpec((rb, width), lambda i: (i, 0))

    def body(x_ref, m_ref, g_ref, d2_ref, du_ref, w_ref, dp_ref, dr_ref, dg_ref, db_ref):
        @pl.when(pl.program_id(0) == 0)
        def _():
            dg_ref[...] = jnp.zeros_like(dg_ref)
            db_ref[...] = jnp.zeros_like(db_ref)

        xhat, rstd = _ln_stats(DEEPNORM_ALPHA * x_ref[...] + m_ref[...])
        dffn = lax.dot_general(du_ref[...], w_ref[...], (((1,), (1,)), ((), ())), preferred_element_type=F32)
        dy = DEEPNORM_ALPHA * d2_ref[...] + dffn
        dg_ref[...] += jnp.sum(dy * xhat, axis=0, keepdims=True)
        db_ref[...] += jnp.sum(dy, axis=0, keepdims=True)
        dp = _ln_bwd(dy, xhat, rstd, g_ref[...])
        dp_ref[...] = dp.astype(BF16)
        dr_ref[...] = DEEPNORM_ALPHA * dp

    return pl.pallas_call(
        body, name="ln1_bwd", grid=(s // rb,),
        in_specs=[_rows(d), _rows(d), _vec(d), _rows(d), _rows(dgu.shape[1]), pl.BlockSpec(w_gu.shape, lambda i: (0, 0))],
        out_specs=[_rows(d), _rows(d), _vec(d), _vec(d)],
        out_shape=[jax.ShapeDtypeStruct((s, d), BF16), jax.ShapeDtypeStruct((s, d), F32),
                   jax.ShapeDtypeStruct((1, d), F32), jax.ShapeDtypeStruct((1, d), F32)],
        compiler_params=_cp("arbitrary"))(x, mixed, g, dpre2, dgu, w_gu)


def _swiglu_fwd(gu):
    s = gu.shape[0]
    f = FFN_HIDDEN

    def body(g_ref, u_ref, o_ref):
        gg = g_ref[...].astype(F32)
        o_ref[...] = (gg * _sigmoid(gg) * u_ref[...].astype(F32)).astype(o_ref.dtype)

    return pl.pallas_call(
        body, name="swiglu_fwd", grid=(s // ROWS,),
        in_specs=[pl.BlockSpec((ROWS, f), lambda i: (i, 0)), pl.BlockSpec((ROWS, f), lambda i: (i, 1))],
        out_specs=_rows(f), out_shape=jax.ShapeDtypeStruct((s, f), BF16), compiler_params=_cp("parallel"))(gu, gu)


def _swiglu_bwd(gu, dact):
    s = gu.shape[0]
    f = FFN_HIDDEN

    def body(g_ref, u_ref, d_ref, o_ref):
        sg, dsg = _silu_and_grad(g_ref[...].astype(F32))
        dd = d_ref[...].astype(F32)
        o_ref[:, :f] = (dd * u_ref[...].astype(F32) * dsg).astype(o_ref.dtype)
        o_ref[:, f:] = (dd * sg).astype(o_ref.dtype)

    return pl.pallas_call(
        body, name="swiglu_bwd", grid=(s // ROWS,),
        in_specs=[pl.BlockSpec((ROWS, f), lambda i: (i, 0)), pl.BlockSpec((ROWS, f), lambda i: (i, 1)), _rows(f)],
        out_specs=_rows(2 * f), out_shape=jax.ShapeDtypeStruct((s, 2 * f), BF16),
        compiler_params=_cp("parallel"))(gu, gu, dact)


def _peer(k):
    x, y, c = lax.axis_index("x"), lax.axis_index("y"), lax.axis_index("c")
    kx, ky, kc = (k >> 2) & 1, (k >> 1) & 1, k & 1
    px = (1 - x) if kx else x
    py = (1 - y) if ky else y
    pc = (1 - c) if kc else c
    return (px, py, pc), 4 * px + 2 * py + pc


def _my_index():
    return 4 * lax.axis_index("x") + 2 * lax.axis_index("y") + lax.axis_index("c")


def _comm_copies(ins, outs, sems, scatter):
    send_sems, recv_sems, local_sems = sems
    me = _my_index()
    copies = [pltpu.make_async_copy(ins[t].at[me] if scatter else ins[t], outs[t].at[me], local_sems.at[t])
              for t in range(len(ins))]
    for k in range(1, N_DEV):
        peer, pidx = _peer(k)
        for t in range(len(ins)):
            copies.append(pltpu.make_async_remote_copy(
                src_ref=ins[t].at[pidx] if scatter else ins[t], dst_ref=outs[t].at[me],
                send_sem=send_sems.at[t, k - 1], recv_sem=recv_sems.at[t, k - 1], device_id=peer,
                device_id_type=pl.DeviceIdType.MESH))
    return copies


def _comm_sems(n):
    return [pltpu.SemaphoreType.DMA((n, N_DEV - 1)), pltpu.SemaphoreType.DMA((n, N_DEV - 1)),
            pltpu.SemaphoreType.DMA((n,))]


def _comm_out_shapes(parts, scatter):
    return [jax.ShapeDtypeStruct(p.shape if scatter else (N_DEV,) + p.shape, p.dtype) for p in parts]


def _all_gather(parts):
    n = len(parts)

    def body(*refs):
        ins, outs = refs[:n], refs[n:2 * n]
        send_sems, recv_sems, local_sems = refs[2 * n:]
        x, y, c = lax.axis_index("x"), lax.axis_index("y"), lax.axis_index("c")
        me, sibling = (x, y, c), (x, y, 1 - c)
        chips = [(1 - x, y), (x, 1 - y), (1 - x, 1 - y)]

        def copy(t, k, block, to, src=None):
            dst = outs[t].at[4 * block[0] + 2 * block[1] + block[2]]
            return pltpu.make_async_remote_copy(
                src_ref=dst if src is None else src, dst_ref=dst, send_sem=send_sems.at[t, k],
                recv_sem=recv_sems.at[t, k], device_id=to, device_id_type=pl.DeviceIdType.MESH)

        mine = [pltpu.make_async_copy(ins[t], outs[t].at[_my_index()], local_sems.at[t]) for t in range(n)]
        for cp in mine:
            cp.start()
        first = [copy(t, 0, me, sibling, src=ins[t]) for t in range(n)]
        first += [copy(t, 1 + j, me, (*chip, c), src=ins[t]) for j, chip in enumerate(chips) for t in range(n)]
        for cp in first:
            cp.start()
        passed = []
        for j, chip in enumerate(chips):
            for t in range(n):
                copy(t, 1 + j, (*chip, c), me).wait_recv()
                passed.append(copy(t, 4 + j, (*chip, c), sibling))
                passed[-1].start()
        for t in range(n):
            copy(t, 0, sibling, me).wait_recv()
            for j, chip in enumerate(chips):
                copy(t, 4 + j, (*chip, 1 - c), me).wait_recv()
        for cp in first + passed:
            cp.wait_send()
        for cp in mine:
            cp.wait()

    anyspec = pl.BlockSpec(memory_space=pl.ANY)
    return pl.pallas_call(body, name="all_gather", in_specs=[anyspec] * n, out_specs=[anyspec] * n,
                          out_shape=_comm_out_shapes(parts, False), scratch_shapes=_comm_sems(n))(*parts)


def _remote_scatter_copies(ins, lands, send_sems, recv_sems):
    me = _my_index()
    copies = []
    for k in range(1, N_DEV):
        peer, pidx = _peer(k)
        for t in range(len(ins)):
            copies.append(pltpu.make_async_remote_copy(
                src_ref=ins[t].at[pidx], dst_ref=lands[t].at[me], send_sem=send_sems.at[t * (N_DEV - 1) + k - 1],
                recv_sem=recv_sems.at[t * (N_DEV - 1) + k - 1], device_id=peer, device_id_type=pl.DeviceIdType.MESH))
    return copies


def _landing_zones(parts):
    me = _my_index()
    return [jnp.where(lax.broadcasted_iota(jnp.int32, p.shape, 0) == me, p, jnp.zeros_like(p)) for p in parts]


_HBM = pl.BlockSpec(memory_space=pltpu.HBM)
_SEM = pl.BlockSpec(memory_space=pltpu.SEMAPHORE)


def _exchange_start(parts, lands):
    n = len(parts)

    def body(*refs):
        ins, lnd, send_sems, recv_sems, token = refs[:n], refs[n:2 * n], refs[2 * n], refs[2 * n + 1], refs[-1]
        for cp in _remote_scatter_copies(ins, lnd, send_sems, recv_sems):
            cp.start()
        token[...] = jnp.zeros_like(token)

    hbm = [pltpu.HBM(p.shape, p.dtype) for p in parts]
    outs = pl.pallas_call(
        body, name="exchange_start",
        out_shape=[pltpu.SemaphoreType.DMA((n * (N_DEV - 1),)), pltpu.SemaphoreType.DMA((n * (N_DEV - 1),))] + hbm + hbm
        + [jax.ShapeDtypeStruct((8, LANES), F32)],
        in_specs=[_HBM] * (2 * n), out_specs=[_SEM, _SEM] + [_HBM] * (2 * n) + [pl.BlockSpec(memory_space=pltpu.VMEM)],
        input_output_aliases={t: 2 + t for t in range(2 * n)},
        compiler_params=pltpu.CompilerParams(has_side_effects=pltpu.SideEffectType.DATAFLOW_SIDE_EFFECTING),
    )(*[pltpu.with_memory_space_constraint(p, pltpu.HBM) for p in list(parts) + list(lands)])
    return outs[0], outs[1], list(outs[2:2 + n]), list(outs[2 + n:2 + 2 * n]), outs[-1]


def _exchange_wait(send_sems, recv_sems, parts, lands, after):
    n = len(parts)

    def body(*refs):
        ins, lnd, send_sems, recv_sems = refs[:n], refs[n:2 * n], refs[2 * n], refs[2 * n + 1]
        for cp in _remote_scatter_copies(ins, lnd, send_sems, recv_sems):
            cp.wait_send()
            cp.wait_recv()

    hbm = [pltpu.HBM(p.shape, p.dtype) for p in parts]
    outs = pl.pallas_call(
        body, name="exchange_wait", out_shape=hbm + hbm,
        in_specs=[_HBM] * (2 * n) + [_SEM, _SEM, pl.BlockSpec(memory_space=pl.ANY)], out_specs=[_HBM] * (2 * n),
        input_output_aliases={t: t for t in range(2 * n)},
        compiler_params=pltpu.CompilerParams(has_side_effects=pltpu.SideEffectType.DATAFLOW_SIDE_EFFECTING),
    )(*parts, *lands, send_sems, recv_sems, after)
    return list(outs[n:])


def _adamw(recv, w, m, v, name):
    _, r, c = w.shape
    br = _tile(r, 128)
    c1 = 1.0 / (1.0 - ADAM_B1 ** ADAM_STEP)
    c2 = 1.0 / (1.0 - ADAM_B2 ** ADAM_STEP)

    def body(r_ref, w_ref, m_ref, v_ref, g_ref, d_ref, mo_ref, vo_ref):
        g = r_ref[0].astype(F32)
        for k in range(1, N_DEV):
            g = g + r_ref[k].astype(F32)
        mn = ADAM_B1 * m_ref[0] + (1.0 - ADAM_B1) * g
        vn = ADAM_B2 * v_ref[0] + (1.0 - ADAM_B2) * (g * g)
        g_ref[0] = g
        mo_ref[0] = mn
        vo_ref[0] = vn
        d_ref[0] = -ADAM_LR * ((mn * c1) / (jnp.sqrt(vn * c2) + ADAM_EPS) + ADAM_WD * w_ref[0])

    blk = pl.BlockSpec((1, br, c), lambda i: (0, i, 0))
    return pl.pallas_call(
        body, name=name, grid=(r // br,),
        in_specs=[pl.BlockSpec((N_DEV, br, c), lambda i: (0, i, 0)), blk, blk, blk],
        out_specs=[blk] * 4, out_shape=[jax.ShapeDtypeStruct((1, r, c), F32)] * 4,
        compiler_params=_cp("parallel"))(recv, w, m, v)


def _lane_row(pairs):
    row = jnp.zeros((LANES,), F32)
    for lane0, vec in pairs:
        row = lax.dynamic_update_slice(row, vec.astype(F32), (lane0,))
    return row.reshape(1, LANES)


def _stage_in(x, wts, small):
    s = x.shape[0]
    a = -jnp.exp(small["a_log"])
    bias_row = _lane_row([(DT_LANE0, small["dt_bias"]), (F_LANE0, small["b_forget"])])
    a_row = _lane_row([(DT_LANE0, a)])
    conv_b = small["conv_b"].reshape(1, -1)
    norm_w = small["ssm_norm_w"].reshape(1, -1)
    bg = small["b_gates"].reshape(1, -1)
    g1, b1 = small["ln1_g"].reshape(1, -1), small["ln1_b"].reshape(1, -1)
    g2, b2 = small["ln2_g"].reshape(1, -1), small["ln2_b"].reshape(1, -1)
    d_skip = small["d_skip"]
    xb = x.astype(BF16)

    qkv = _mm(xb, wts["qkv"], out_dtype=BF16, name="f_qkv")
    z = _mm(xb, wts["z"], out_dtype=BF16, name="f_z")
    xbc = _mm(xb, wts["xbc"], name="f_xbc")
    gl = _mm(xb, wts["gate"], out_dtype=BF16, name="f_gate")
    fd = _mm(xb, wts["fd"], name="f_fd")
    dt_c, ac_c, cf_c, dt_r, ac_r, cf_r = _stats_fwd(fd, bias_row, a_row)
    bk = _att_blocks(s)[1]
    ck4 = cf_r[F_LANE0:F_LANE0 + ATT_HEADS].reshape(N_HP, HP, s // bk, bk)
    return dict(locals())


def _stage_mid(c, attn, lse, wts, target):
    x, xb, qkv, z, xbc, gl, fd, ck4 = (c[k] for k in ("x", "xb", "qkv", "z", "xbc", "gl", "fd", "ck4"))
    dt_c, ac_c, dt_r, ac_r, a_row, bias_row = (c[k] for k in ("dt_c", "ac_c", "dt_r", "ac_r", "a_row", "bias_row"))
    conv_b, norm_w, bg, g1, b1, g2, b2, d_skip = (c[k] for k in ("conv_b", "norm_w", "bg", "g1", "b1", "g2", "b2",
                                                                "d_skip"))
    conv_w = c["wts"]["conv"]
    attn_d = _mm(attn, wts["pa"], out_dtype=BF16, name="f_pa")
    xact = _conv_fwd(xbc, conv_w, conv_b)
    dsk_pair = jnp.repeat(d_skip, SSM_HEAD_DIM).reshape(N_PAIR, LANES)
    y, hprev = _ssd_pair_fwd(xact, ac_c, dt_r, ac_r, dsk_pair)
    ssm = _gnorm_fwd(y, z, norm_w)
    ssm_d = _mm(ssm, wts["ps"], out_dtype=BF16, name="f_ps")
    mix = _mix_fwd(gl, bg, attn_d, ssm_d)
    mixed = _mm(mix, wts["out"], name="f_out")
    x1, x1_b = _ln1_fwd(x, mixed, g1, b1)
    gu = _mm(x1_b, wts["gu"], out_dtype=BF16, name="f_gu")
    act = _swiglu_fwd(gu)
    dpre2, dpre2_b, loss_row, dg2, db2 = _ln2_loss(x1, act, wts["down"], target, g2, b2)

    d_act = _mm(dpre2_b, wts["down"], tb=True, out_dtype=BF16, name="b_down_x")
    dw_down = _mm(act, dpre2_b, ta=True, name="b_down_w")
    dgu = _swiglu_bwd(gu, d_act)
    dw_gu = _mm(x1_b, dgu, ta=True, name="b_gu_w")
    dpre1, dxr, dg1, db1 = _ln1_bwd(x, mixed, g1, dpre2, dgu, wts["gu"])
    dmix = _mm(dpre1, wts["out"], tb=True, out_dtype=BF16, name="b_out_x")
    dw_out = _mm(mix, dpre1, ta=True, name="b_out_w")
    dattn_d, dssm_d, dgl, dbg = _mix_bwd(gl, bg, attn_d, ssm_d, dmix)
    dw_ps = _mm(ssm, dssm_d, ta=True, name="b_ps_w")
    dw_pa = _mm(attn, dattn_d, ta=True, name="b_pa_w")
    dy, dz, dnw = _gnorm_bwd(y, z, norm_w, dssm_d, wts["ps"])
    dxact, ddt, da_row, dds_pair = _ssd_pair_bwd(xact, dt_c, ac_c, dt_r, ac_r, hprev, dy, dsk_pair, a_row)
    dds = dds_pair.reshape(SSM_HEADS, SSM_HEAD_DIM).sum(axis=1)
    dpre_c, dconv_w, dconv_b = _conv_bwd_pre(xbc, conv_w, conv_b, dxact)
    dxbc = _conv_bwd_in(dpre_c, conv_w)
    st, do_b = _att_prep(dattn_d, wts["pa"], attn, lse)
    late = dict(pa=dw_pa, ps=dw_ps, out=dw_out, gu=dw_gu, down=dw_down)
    keep = ("st", "do_b", "ddt", "dxr", "dz", "dxbc", "dgl", "dconv_w", "dconv_b", "da_row", "dds", "dnw", "dbg",
            "dg1", "db1", "dg2", "db2", "loss_row")
    loc = locals()
    return {**c, **{k: loc[k] for k in keep}}, late


def _stage_out_w(c, att_grads):
    dq, dk, dv, dck, dcq = att_grads
    xb, fd, bias_row, ddt, dz, dxbc, dgl = (c[k] for k in ("xb", "fd", "bias_row", "ddt", "dz", "dxbc", "dgl"))
    s, a = xb.shape[0], c["a"]
    dcum = dck.reshape(ATT_HEADS, s) + dcq.reshape(ATT_HEADS, s)
    dfd, dbias = _stats_bwd(fd, bias_row, ddt, jnp.zeros((LANES, s), F32).at[F_LANE0:F_LANE0 + ATT_HEADS].set(dcum))
    dproj = (dq, dk, dv, dz, dxbc, dgl, dfd)
    dw_in = [_mm(xb, g_, ta=True, tb=(i == 0), name=f"b_in_w{i}") for i, g_ in enumerate(dproj)]
    grads = dict(q=dw_in[0], k=dw_in[1], v=dw_in[2], z=dw_in[3], xbc=dw_in[4], gate=dw_in[5], fd=dw_in[6],
                 conv=c["dconv_w"])
    small_g = dict(
        b_forget=dbias[0, F_LANE0:F_LANE0 + ATT_HEADS], conv_b=c["dconv_b"][0], dt_bias=dbias[0, :SSM_HEADS],
        a_log=c["da_row"][0, :SSM_HEADS] * a, d_skip=c["dds"], ssm_norm_w=c["dnw"][0], b_gates=c["dbg"][0],
        ln1_g=c["dg1"][0], ln1_b=c["db1"][0], ln2_g=c["dg2"][0], ln2_b=c["db2"][0])
    return c["loss_row"][0, 0], grads, small_g, dproj


def _stage_out_x(c, dproj, token):
    wts = c["wts"]
    dq_t, dk, dv, dz, dxbc, dgate, dfd = dproj
    w_fd = wts["fd"] + token.astype(BF16)
    dx = _mm_sum([(dq_t, True, wts["qkv"], 0), (dk, False, wts["qkv"], 1), (dv, False, wts["qkv"], 2),
                  (dfd, False, w_fd, 0)], c["dxr"], "b_in_x_qkv_fd")
    dx = _mm_sum([(dz, False, wts["z"], 0), (dgate, False, wts["gate"], 0)], dx, "b_in_x_z_gate")
    return _mm_sum([(dxbc, False, wts["xbc"], 0)], dx, "b_in_x_xbc")


BIG = ("w_in", "w_proj_attn", "w_proj_ssm", "w_out", "w_ffn_gate", "w_ffn_up", "w_ffn_down", "conv_w")
EARLY = ("w_in", "conv_w")
LATE = ("w_proj_attn", "w_proj_ssm", "w_out", "w_ffn_gate", "w_ffn_up", "w_ffn_down")
SMALL = ("b_forget", "conv_b", "dt_bias", "a_log", "d_skip", "ssm_norm_w", "b_gates", "ln1_g", "ln1_b", "ln2_g",
         "ln2_b")
SMALL_ROWS = 96
IN_SHARD = IN_WIDTH // N_DEV
IN_SEGMENTS = (("q", 0, 1024), ("k", 1024, 1024), ("v", 2048, 1024), ("f", 3072, ATT_HEADS), ("z", 3088, SSM_INNER),
               ("xbc", 5136, SSM_CONV_DIM), ("dt", 8208, SSM_HEADS), ("gate", 8240, 2 * D_MODEL))


def _cols_from_shards(shards, lo, hi):
    w = shards[0].shape[1]
    pieces = []
    for j in range(len(shards)):
        a, b = max(lo, j * w), min(hi, (j + 1) * w)
        if a < b:
            pieces.append(shards[j][:, a - j * w:b - j * w])
    return pieces[0] if len(pieces) == 1 else jnp.concatenate(pieces, axis=1)


def _shards_from_parts(parts, width):
    shards = []
    for j in range(N_DEV):
        lo, hi = j * width, (j + 1) * width
        pieces = []
        for mat, c0 in parts:
            a, b = max(lo, c0), min(hi, c0 + mat.shape[1])
            if a < b:
                pieces.append(mat[:, a - c0:b - c0])
        shards.append(pieces[0] if len(pieces) == 1 else jnp.concatenate(pieces, axis=1))
    return shards


def _pack_small(vals):
    flat = jnp.concatenate([vals[n].reshape(-1) for n in SMALL])
    return jnp.pad(flat, (0, SMALL_ROWS * LANES - flat.shape[0])).reshape(SMALL_ROWS, LANES)


def _unpack_small(pack, shapes):
    flat = pack.reshape(-1)
    out, off = {}, 0
    for n in SMALL:
        sz = math.prod(shapes[n])
        out[n] = flat[off:off + sz].reshape(shapes[n])
        off += sz
    return out


def kernel(x, w_in, b_forget, conv_w, conv_b, dt_bias, a_log, d_skip, ssm_norm_w, w_proj_attn, w_proj_ssm, b_gates, w_out, ln1_g, ln1_b, w_ffn_gate, w_ffn_up, w_ffn_down, ln2_g, ln2_b, loss_target, m_w_in, m_b_forget, m_conv_w, m_conv_b, m_dt_bias, m_a_log, m_d_skip, m_ssm_norm_w, m_w_proj_attn, m_w_proj_ssm, m_b_gates, m_w_out, m_ln1_g, m_ln1_b, m_w_ffn_gate, m_w_ffn_up, m_w_ffn_down, m_ln2_g, m_ln2_b, v_w_in, v_b_forget, v_conv_w, v_conv_b, v_dt_bias, v_a_log, v_d_skip, v_ssm_norm_w, v_w_proj_attn, v_w_proj_ssm, v_b_gates, v_w_out, v_ln1_g, v_ln1_b, v_w_ffn_gate, v_w_ffn_up, v_w_ffn_down, v_ln2_g, v_ln2_b):
    args = dict(locals())
    d, f = D_MODEL, FFN_HIDDEN
    big_w = {n: args[n][0] for n in BIG}
    small_w = {n: args[n][0] for n in SMALL}
    big_shapes = {n: args[n].shape for n in BIG}
    small_shapes = {n: args[n].shape for n in SMALL}

    early = dict(zip(EARLY, _all_gather([big_w["w_in"].astype(BF16), big_w["conv_w"]])))
    in_shards = [early["w_in"][j] for j in range(N_DEV)]
    seg = {n: _cols_from_shards(in_shards, c0, c0 + w) for n, c0, w in IN_SEGMENTS}
    wfd = jnp.concatenate([seg["dt"], seg["f"], jnp.zeros((d, LANES - SSM_HEADS - ATT_HEADS), BF16)], axis=1)
    wts = dict(qkv=jnp.concatenate([seg["q"], seg["k"], seg["v"]], axis=1), z=seg["z"], xbc=seg["xbc"],
               gate=seg["gate"], fd=wfd, conv=jnp.concatenate([early["conv_w"][j] for j in range(N_DEV)], axis=1))

    ctx = _stage_in(x[0], wts, small_w)
    attn, lse, gathered = _attention_fwd(ctx["qkv"], ctx["ck4"], [big_w[n].astype(BF16) for n in LATE])
    full = dict(zip(LATE, gathered))
    late_w = dict(
        pa=full["w_proj_attn"].reshape(d, d), ps=full["w_proj_ssm"].reshape(SSM_INNER, d),
        out=full["w_out"].reshape(d, d),
        gu=jnp.concatenate([full["w_ffn_gate"][j] for j in range(N_DEV)]
                           + [full["w_ffn_up"][j] for j in range(N_DEV)], axis=1),
        down=full["w_ffn_down"].reshape(f, d))
    ctx, gl = _stage_mid(ctx, attn, lse, late_w, loss_target[0])
    late_dest = dict(
        w_ffn_gate=jnp.stack([s_.astype(BF16) for s_ in _shards_from_parts([(gl["gu"][:, :f], 0)], f // N_DEV)]),
        w_ffn_up=jnp.stack([s_.astype(BF16) for s_ in _shards_from_parts([(gl["gu"][:, f:], 0)], f // N_DEV)]))
    for n, key in (("w_proj_attn", "pa"), ("w_proj_ssm", "ps"), ("w_out", "out"), ("w_ffn_down", "down")):
        late_dest[n] = gl[key].astype(BF16).reshape((N_DEV,) + big_shapes[n][1:])
    att_grads, late_recv = _attention_bwd(ctx["qkv"], ctx["ck4"], ctx["st"], ctx["do_b"], [late_dest[n] for n in LATE])
    loss_part, g, small_g, dproj = _stage_out_w(ctx, att_grads)
    loss = lax.psum(loss_part, ("x", "y", "c"))

    gfd = g["fd"]
    in_parts = dict(q=g["q"], k=g["k"], v=g["v"], f=gfd[:, F_LANE0:F_LANE0 + ATT_HEADS], z=g["z"], xbc=g["xbc"],
                    dt=gfd[:, DT_LANE0:DT_LANE0 + SSM_HEADS], gate=g["gate"])
    win_dest = jnp.stack([s_.astype(BF16) for s_ in
                          _shards_from_parts([(in_parts[n], c0) for n, c0, _ in IN_SEGMENTS], IN_SHARD)])
    conv_dest = jnp.stack(_shards_from_parts([(g["conv"], 0)], SSM_CONV_DIM // N_DEV))
    small_pack = _pack_small(small_g)
    last_parts = [win_dest, conv_dest, jnp.broadcast_to(small_pack, (N_DEV,) + small_pack.shape)]
    send_sems, recv_sems, parts_thru, lands_thru, token = _exchange_start(last_parts, _landing_zones(last_parts))
    grad_x = _stage_out_x(ctx, dproj, token[0, 0])
    early_recv = _exchange_wait(send_sems, recv_sems, parts_thru, lands_thru, grad_x)
    recv = dict(zip(LATE, late_recv))
    recv["w_in"], recv["conv_w"] = early_recv[0], early_recv[1]

    outs = {}
    for n in BIG:
        outs[n] = _adamw(recv[n], args[n], args["m_" + n], args["v_" + n], name="adamw_" + n)
    small4 = _adamw(early_recv[2], _pack_small(small_w)[None], _pack_small({n: args["m_" + n][0] for n in SMALL})[None],
                    _pack_small({n: args["v_" + n][0] for n in SMALL})[None], name="adamw_small")
    small_out = [_unpack_small(p, small_shapes) for p in small4]
    for n in SMALL:
        outs[n] = [so[n] for so in small_out]

    order = ("w_in", "b_forget", "conv_w", "conv_b", "dt_bias", "a_log", "d_skip", "ssm_norm_w", "w_proj_attn",
             "w_proj_ssm", "b_gates", "w_out", "ln1_g", "ln1_b", "w_ffn_gate", "w_ffn_up", "w_ffn_down", "ln2_g",
             "ln2_b")
    res = [loss, grad_x[None]]
    for i in range(4):
        res += [outs[n][i] for n in order]
    return tuple(res)
```
